```python
import jax, jax.numpy as jnp
from jax import lax
import numpy as np

D_MODEL = 1024
BATCH = 8
SEQ = 4096
DEPTH = 1

HEAD_DIM = 64
N_HEADS = 8
N_KV_HEADS = 2
GROUP = N_HEADS // N_KV_HEADS
WINDOW = 128
BLOCK = 128
ATTN_SCALE = HEAD_DIM ** -0.5
ATTN_WIDTH = N_HEADS * HEAD_DIM
KV_WIDTH = N_KV_HEADS * HEAD_DIM
CONV_GROUPS = 8
CONV_WIDTH = CONV_GROUPS * 64
CONV_K = 3
IN_WIDTH = ATTN_WIDTH + 2 * KV_WIDTH + 3 * CONV_WIDTH + 2 * D_MODEL
D_FF = 2816
FFN_CONV_K = 3
NORM_EPS = 1e-5

kernel_name = "hybrid_swa_sink_shortconv_gated_convffn"


def rms_norm(x, g):
    xf = x.astype(jnp.float32)
    y = xf * lax.rsqrt(jnp.mean(xf * xf, axis=-1, keepdims=True) + NORM_EPS)
    return (y * g.astype(jnp.float32)).astype(x.dtype)


def causal_depthwise_conv(x, w):
    k_width, ch = w.shape
    return lax.conv_general_dilated(
        x, w[:, None, :].astype(x.dtype), window_strides=(1,), padding=((k_width - 1, 0),),
        dimension_numbers=("NWC", "WIO", "NWC"), feature_group_count=ch)


def sliding_window_attention(q, k, v, sinks):
    b, s, _ = q.shape
    nb = s // BLOCK
    q = q.reshape(b, nb, BLOCK, N_KV_HEADS, GROUP, HEAD_DIM)
    k = k.reshape(b, nb, BLOCK, N_KV_HEADS, HEAD_DIM)
    v = v.reshape(b, nb, BLOCK, N_KV_HEADS, HEAD_DIM)

    def with_prev(t):
        prev = jnp.pad(t, ((0, 0), (1, 0), (0, 0), (0, 0), (0, 0)))[:, :-1]
        return jnp.concatenate([prev, t], axis=2)

    kw, vw = with_prev(k), with_prev(v)
    scores = jnp.einsum("bnqhgd,bnkhd->bnhgqk", q, kw).astype(jnp.float32) * ATTN_SCALE
    qi = jnp.arange(BLOCK)[:, None]
    kj = jnp.arange(2 * BLOCK)[None, :]
    dist = qi + BLOCK - kj
    band = (dist >= 0) & (dist < WINDOW)
    real = (jnp.arange(nb)[:, None, None] > 0) | (kj[None] >= BLOCK)
    mask = band[None] & real
    scores = jnp.where(mask[None, :, None, None], scores, -jnp.inf)
    sink = jnp.broadcast_to(sinks.astype(jnp.float32).reshape(1, 1, N_KV_HEADS, GROUP, 1, 1),
                            scores.shape[:-1] + (1,))
    probs = jax.nn.softmax(jnp.concatenate([scores, sink], axis=-1), axis=-1)[..., :-1]
    out = jnp.einsum("bnhgqk,bnkhd->bnqhgd", probs.astype(v.dtype), vw)
    return out.reshape(b, s, ATTN_WIDTH)


def _fwd_setup_inputs(seed: int = 0) -> dict:
    key = jax.random.key(seed)
    ks = jax.random.split(key, 16)
    f32 = jnp.float32

    def nrm(k, shape, scale):
        return jax.random.normal(k, shape, f32) * scale

    return {
        "x": nrm(ks[0], (BATCH, SEQ, D_MODEL), 1.0),
        "mix_norm": 1.0 + nrm(ks[1], (DEPTH, D_MODEL), 0.02),
        "w_in": nrm(ks[2], (DEPTH, D_MODEL, IN_WIDTH), D_MODEL ** -0.5),
        "b_in": nrm(ks[3], (DEPTH, IN_WIDTH), 0.02),
        "sinks": nrm(ks[4], (DEPTH, N_HEADS), 0.5),
        "conv_w": nrm(ks[5], (DEPTH, CONV_K, CONV_WIDTH), CONV_K ** -0.5),
        "w_attn_branch": nrm(ks[6], (DEPTH, ATTN_WIDTH, D_MODEL), ATTN_WIDTH ** -0.5),
        "w_conv_branch": nrm(ks[7], (DEPTH, CONV_WIDTH, D_MODEL), CONV_WIDTH ** -0.5),
        "w_out": nrm(ks[8], (DEPTH, D_MODEL, D_MODEL), D_MODEL ** -0.5),
        "ffn_norm": 1.0 + nrm(ks[9], (DEPTH, D_MODEL), 0.02),
        "w_up": nrm(ks[10], (DEPTH, D_MODEL, 2 * D_FF), D_MODEL ** -0.5),
        "ffn_conv_w": nrm(ks[11], (DEPTH, FFN_CONV_K, 2 * D_FF), FFN_CONV_K ** -0.5),
        "w_down": nrm(ks[12], (DEPTH, D_FF, D_MODEL), D_FF ** -0.5),
        "final_norm": 1.0 + nrm(ks[13], (D_MODEL,), 0.02),
    }


def _fwd_reference(x, mix_norm, w_in, b_in, sinks, conv_w, w_attn_branch, w_conv_branch, w_out,
              ffn_norm, w_up, ffn_conv_w, w_down, final_norm):
    h = x
    splits = np.cumsum([ATTN_WIDTH, KV_WIDTH, KV_WIDTH, CONV_WIDTH, CONV_WIDTH, CONV_WIDTH, D_MODEL])
    for l in range(DEPTH):
        xn = rms_norm(h, mix_norm[l])
        proj = jnp.einsum("bsd,dp->bsp", xn, w_in[l]) + b_in[l]
        q, k, v, cb, cc, cx, ga, gc = jnp.split(proj, splits, axis=-1)
        attn = sliding_window_attention(q, k, v, sinks[l])
        conv = cb * causal_depthwise_conv(cc * cx, conv_w[l])
        merged = (jax.nn.sigmoid(ga) * jnp.einsum("bsc,cd->bsd", attn, w_attn_branch[l])
                  + jax.nn.sigmoid(gc) * jnp.einsum("bsc,cd->bsd", conv, w_conv_branch[l]))
        h = h + jnp.einsum("bsd,de->bse", merged, w_out[l])
        hn = rms_norm(h, ffn_norm[l])
        up = causal_depthwise_conv(jnp.einsum("bsd,df->bsf", hn, w_up[l]), ffn_conv_w[l])
        gate, val = jnp.split(up, 2, axis=-1)
        h = h + jnp.einsum("bsf,fd->bsd", jax.nn.silu(gate) * val, w_down[l])
    return rms_norm(h, final_norm)


import jax as _jax
import jax.numpy as _jnp

TWIN_FORMAT = 'train_step'
FWD_PARAMS = ['x', 'mix_norm', 'w_in', 'b_in', 'sinks', 'conv_w', 'w_attn_branch', 'w_conv_branch', 'w_out', 'ffn_norm', 'w_up', 'ffn_conv_w', 'w_down', 'final_norm']
TWIN_WEIGHTS = ['mix_norm', 'w_in', 'b_in', 'sinks', 'conv_w', 'w_attn_branch', 'w_conv_branch', 'w_out', 'ffn_norm', 'w_up', 'ffn_conv_w', 'w_down', 'final_norm']
TWIN_DIFF_INPUT = 'x'
TWIN_INPUTS = ['x', 'mix_norm', 'w_in', 'b_in', 'sinks', 'conv_w', 'w_attn_branch', 'w_conv_branch', 'w_out', 'ffn_norm', 'w_up', 'ffn_conv_w', 'w_down', 'final_norm', 'loss_target', 'm_mix_norm', 'm_w_in', 'm_b_in', 'm_sinks', 'm_conv_w', 'm_w_attn_branch', 'm_w_conv_branch', 'm_w_out', 'm_ffn_norm', 'm_w_up', 'm_ffn_conv_w', 'm_w_down', 'm_final_norm', 'v_mix_norm', 'v_w_in', 'v_b_in', 'v_sinks', 'v_conv_w', 'v_w_attn_branch', 'v_w_conv_branch', 'v_w_out', 'v_ffn_norm', 'v_w_up', 'v_ffn_conv_w', 'v_w_down', 'v_final_norm']
TWIN_OUTPUTS = ['loss', 'grad_x', 'grad_mix_norm', 'grad_w_in', 'grad_b_in', 'grad_sinks', 'grad_conv_w', 'grad_w_attn_branch', 'grad_w_conv_branch', 'grad_w_out', 'grad_ffn_norm', 'grad_w_up', 'grad_ffn_conv_w', 'grad_w_down', 'grad_final_norm', 'delta_mix_norm', 'delta_w_in', 'delta_b_in', 'delta_sinks', 'delta_conv_w', 'delta_w_attn_branch', 'delta_w_conv_branch', 'delta_w_out', 'delta_ffn_norm', 'delta_w_up', 'delta_ffn_conv_w', 'delta_w_down', 'delta_final_norm', 'new_m_mix_norm', 'new_m_w_in', 'new_m_b_in', 'new_m_sinks', 'new_m_conv_w', 'new_m_w_attn_branch', 'new_m_w_conv_branch', 'new_m_w_out', 'new_m_ffn_norm', 'new_m_w_up', 'new_m_ffn_conv_w', 'new_m_w_down', 'new_m_final_norm', 'new_v_mix_norm', 'new_v_w_in', 'new_v_b_in', 'new_v_sinks', 'new_v_conv_w', 'new_v_w_attn_branch', 'new_v_w_conv_branch', 'new_v_w_out', 'new_v_ffn_norm', 'new_v_w_up', 'new_v_ffn_conv_w', 'new_v_w_down', 'new_v_final_norm']
TWIN_LEAF_KINDS = {'loss': 'loss', 'grad_x': 'grad_x', 'grad_mix_norm': 'grad_w', 'grad_w_in': 'grad_w', 'grad_b_in': 'grad_w', 'grad_sinks': 'grad_w', 'grad_conv_w': 'grad_w', 'grad_w_attn_branch': 'grad_w', 'grad_w_conv_branch': 'grad_w', 'grad_w_out': 'grad_w', 'grad_ffn_norm': 'grad_w', 'grad_w_up': 'grad_w', 'grad_ffn_conv_w': 'grad_w', 'grad_w_down': 'grad_w', 'grad_final_norm': 'grad_w', 'delta_mix_norm': 'delta_w', 'delta_w_in': 'delta_w', 'delta_b_in': 'delta_w', 'delta_sinks': 'delta_w', 'delta_conv_w': 'delta_w', 'delta_w_attn_branch': 'delta_w', 'delta_w_conv_branch': 'delta_w', 'delta_w_out': 'delta_w', 'delta_ffn_norm': 'delta_w', 'delta_w_up': 'delta_w', 'delta_ffn_conv_w': 'delta_w', 'delta_w_down': 'delta_w', 'delta_final_norm': 'delta_w', 'new_m_mix_norm': 'new_m', 'new_m_w_in': 'new_m', 'new_m_b_in': 'new_m', 'new_m_sinks': 'new_m', 'new_m_conv_w': 'new_m', 'new_m_w_attn_branch': 'new_m', 'new_m_w_conv_branch': 'new_m', 'new_m_w_out': 'new_m', 'new_m_ffn_norm': 'new_m', 'new_m_w_up': 'new_m', 'new_m_ffn_conv_w': 'new_m', 'new_m_w_down': 'new_m', 'new_m_final_norm': 'new_m', 'new_v_mix_norm': 'new_v', 'new_v_w_in': 'new_v', 'new_v_b_in': 'new_v', 'new_v_sinks': 'new_v', 'new_v_conv_w': 'new_v', 'new_v_w_attn_branch': 'new_v', 'new_v_w_conv_branch': 'new_v', 'new_v_w_out': 'new_v', 'new_v_ffn_norm': 'new_v', 'new_v_w_up': 'new_v', 'new_v_ffn_conv_w': 'new_v', 'new_v_w_down': 'new_v', 'new_v_final_norm': 'new_v'}


def _forward(args):
    return _fwd_reference(*[args[k] for k in FWD_PARAMS])


def _output_shape():
    out = _jax.eval_shape(lambda: _forward(_fwd_setup_inputs(0)))
    return out.shape, out.dtype

N_MICROBATCH = 1
ADAM_LR = 0.001
ADAM_B1 = 0.9
ADAM_B2 = 0.999
ADAM_EPS = 1e-08
ADAM_WD = 0.01
ADAM_STEP = 10
PER_EXAMPLE_BATCH_AXIS = {'x': 0, 'loss_target': 0}
SHARED_INPUTS = []
_WEIGHT_DTYPES = {'mix_norm': _jnp.float32, 'w_in': _jnp.float32, 'b_in': _jnp.float32, 'sinks': _jnp.float32, 'conv_w': _jnp.float32, 'w_attn_branch': _jnp.float32, 'w_conv_branch': _jnp.float32, 'w_out': _jnp.float32, 'ffn_norm': _jnp.float32, 'w_up': _jnp.float32, 'ffn_conv_w': _jnp.float32, 'w_down': _jnp.float32, 'final_norm': _jnp.float32}
MOMENT_SCALE = {'mix_norm': 1.686218e-01, 'w_in': 8.266083e-02, 'b_in': 9.553989e-02, 'sinks': 2.191988e-02, 'conv_w': 1.473576e-01, 'w_attn_branch': 1.930702e-02, 'w_conv_branch': 9.473703e-02, 'w_out': 9.656401e-02, 'ffn_norm': 1.231490e-01, 'w_up': 5.226678e-02, 'ffn_conv_w': 5.168728e-02, 'w_down': 8.542675e-02, 'final_norm': 3.198621e+01}


def _to_microbatches(a, axis):
    t = _jnp.moveaxis(a, axis, 0)
    t = t.reshape((N_MICROBATCH, t.shape[0] // N_MICROBATCH) + t.shape[1:])
    return _jnp.moveaxis(t, 1, axis + 1)


def setup_inputs(seed: int = 0) -> dict:
    inp = _fwd_setup_inputs(seed)
    key = _jax.random.fold_in(_jax.random.key(seed), 7919)
    shape, _ = _output_shape()
    out = dict(inp)
    out["loss_target"] = _jax.random.normal(_jax.random.fold_in(key, 0), shape, _jnp.float32)
    for i, name in enumerate(TWIN_WEIGHTS):
        w = inp[name].astype(_jnp.float32)
        if MOMENT_SCALE is None:
            s = _jnp.sqrt(_jnp.mean(_jnp.square(w)) + 1e-30)
        else:
            s = MOMENT_SCALE[name]
        km, kv = _jax.random.split(_jax.random.fold_in(key, i + 1))
        out[name] = w
        out["m_" + name] = s * _jax.random.normal(km, w.shape, _jnp.float32)
        out["v_" + name] = (s * s) * _jax.random.uniform(kv, w.shape, _jnp.float32, 0.5, 1.5)
    if N_MICROBATCH > 1:
        for name, axis in PER_EXAMPLE_BATCH_AXIS.items():
            out[name] = _to_microbatches(out[name], axis)
    return {'x': out['x'], 'mix_norm': out['mix_norm'], 'w_in': out['w_in'], 'b_in': out['b_in'], 'sinks': out['sinks'], 'conv_w': out['conv_w'], 'w_attn_branch': out['w_attn_branch'], 'w_conv_branch': out['w_conv_branch'], 'w_out': out['w_out'], 'ffn_norm': out['ffn_norm'], 'w_up': out['w_up'], 'ffn_conv_w': out['ffn_conv_w'], 'w_down': out['w_down'], 'final_norm': out['final_norm'], 'loss_target': out['loss_target'], 'm_mix_norm': out['m_mix_norm'], 'm_w_in': out['m_w_in'], 'm_b_in': out['m_b_in'], 'm_sinks': out['m_sinks'], 'm_conv_w': out['m_conv_w'], 'm_w_attn_branch': out['m_w_attn_branch'], 'm_w_conv_branch': out['m_w_conv_branch'], 'm_w_out': out['m_w_out'], 'm_ffn_norm': out['m_ffn_norm'], 'm_w_up': out['m_w_up'], 'm_ffn_conv_w': out['m_ffn_conv_w'], 'm_w_down': out['m_w_down'], 'm_final_norm': out['m_final_norm'], 'v_mix_norm': out['v_mix_norm'], 'v_w_in': out['v_w_in'], 'v_b_in': out['v_b_in'], 'v_sinks': out['v_sinks'], 'v_conv_w': out['v_conv_w'], 'v_w_attn_branch': out['v_w_attn_branch'], 'v_w_conv_branch': out['v_w_conv_branch'], 'v_w_out': out['v_w_out'], 'v_ffn_norm': out['v_ffn_norm'], 'v_w_up': out['v_w_up'], 'v_ffn_conv_w': out['v_ffn_conv_w'], 'v_w_down': out['v_w_down'], 'v_final_norm': out['v_final_norm']}


def _loss(weights, diff, rest, loss_target):
    with _jax.named_scope("forward"):
        args = {**rest, TWIN_DIFF_INPUT: diff, **{k: w.astype(_WEIGHT_DTYPES[k]) for k, w in weights.items()}}
        y = _forward(args)
    with _jax.named_scope("loss_head"):
        err = _jnp.square(y.astype(_jnp.float32) - loss_target)
        return 0.5 * _jnp.sum(_jnp.mean(err, axis=-1)) if err.ndim else 0.5 * err


def _adamw(w, g, m, v):
    m = ADAM_B1 * m + (1.0 - ADAM_B1) * g
    v = ADAM_B2 * v + (1.0 - ADAM_B2) * _jnp.square(g)
    m_hat = m / (1.0 - ADAM_B1 ** ADAM_STEP)
    v_hat = v / (1.0 - ADAM_B2 ** ADAM_STEP)
    delta = -ADAM_LR * (m_hat / (_jnp.sqrt(v_hat) + ADAM_EPS) + ADAM_WD * w)
    return delta, m, v


def reference(x, mix_norm, w_in, b_in, sinks, conv_w, w_attn_branch, w_conv_branch, w_out, ffn_norm, w_up, ffn_conv_w, w_down, final_norm, loss_target, m_mix_norm, m_w_in, m_b_in, m_sinks, m_conv_w, m_w_attn_branch, m_w_conv_branch, m_w_out, m_ffn_norm, m_w_up, m_ffn_conv_w, m_w_down, m_final_norm, v_mix_norm, v_w_in, v_b_in, v_sinks, v_conv_w, v_w_attn_branch, v_w_conv_branch, v_w_out, v_ffn_norm, v_w_up, v_ffn_conv_w, v_w_down, v_final_norm):
    given = dict(x=x, mix_norm=mix_norm, w_in=w_in, b_in=b_in, sinks=sinks, conv_w=conv_w, w_attn_branch=w_attn_branch, w_conv_branch=w_conv_branch, w_out=w_out, ffn_norm=ffn_norm, w_up=w_up, ffn_conv_w=ffn_conv_w, w_down=w_down, final_norm=final_norm, loss_target=loss_target, m_mix_norm=m_mix_norm, m_w_in=m_w_in, m_b_in=m_b_in, m_sinks=m_sinks, m_conv_w=m_conv_w, m_w_attn_branch=m_w_attn_branch, m_w_conv_branch=m_w_conv_branch, m_w_out=m_w_out, m_ffn_norm=m_ffn_norm, m_w_up=m_w_up, m_ffn_conv_w=m_ffn_conv_w, m_w_down=m_w_down, m_final_norm=m_final_norm, v_mix_norm=v_mix_norm, v_w_in=v_w_in, v_b_in=v_b_in, v_sinks=v_sinks, v_conv_w=v_conv_w, v_w_attn_branch=v_w_attn_branch, v_w_conv_branch=v_w_conv_branch, v_w_out=v_w_out, v_ffn_norm=v_ffn_norm, v_w_up=v_w_up, v_ffn_conv_w=v_ffn_conv_w, v_w_down=v_w_down, v_final_norm=v_final_norm)
    weights = {n: given[n] for n in TWIN_WEIGHTS}
    shared = {n: given[n] for n in SHARED_INPUTS}
    per_example = {n: given[n] for n in ['x']}
    grad_fn = _jax.value_and_grad(_loss, argnums=(0, 1))

    def one_microbatch(ex, loss_target):
        ex = dict(ex)
        diff = ex.pop(TWIN_DIFF_INPUT)
        return grad_fn(weights, diff, {**shared, **ex}, loss_target)

    if N_MICROBATCH == 1:
        loss, (grad_w, grad_x) = one_microbatch(per_example, given["loss_target"])
    else:
        def body(carry, xs):
            loss_sum, grad_sum = carry
            l_k, (gw_k, gx_k) = one_microbatch(xs[0], xs[1])
            with _jax.named_scope("update"):
                return (loss_sum + l_k, _jax.tree.map(_jnp.add, grad_sum, gw_k)), gx_k

        init = (_jnp.zeros((), _jnp.float32), _jax.tree.map(_jnp.zeros_like, weights))
        (loss, grad_w), grad_x = _jax.lax.scan(body, init, (per_example, given["loss_target"]))
    with _jax.named_scope("update"):
        delta_w, new_m, new_v = {}, {}, {}
        for n in TWIN_WEIGHTS:
            delta_w[n], new_m[n], new_v[n] = _adamw(weights[n], grad_w[n], given["m_" + n], given["v_" + n])
    return (loss, grad_x, *[grad_w[n] for n in TWIN_WEIGHTS], *[delta_w[n] for n in TWIN_WEIGHTS],
            *[new_m[n] for n in TWIN_WEIGHTS], *[new_v[n] for n in TWIN_WEIGHTS])
```

```python
import functools

import jax
import jax.numpy as jnp
from jax import lax
from jax.experimental import pallas as pl
from jax.experimental.pallas import tpu as pltpu

F32 = jnp.float32
BF16 = jnp.bfloat16
MESH = pl.DeviceIdType.MESH
N_DEV = 8

D_MODEL = 1024
HEAD_DIM = 64
N_HEADS = 8
BLOCK = 128
ATTN_W = 512
KV_W = 128
CONV_W = 512
QKV_W = ATTN_W + 2 * KV_W
CBX_W = 3 * CONV_W
GATE_W = 2 * D_MODEL
IN_W = QKV_W + CBX_W + GATE_W
D_FF = 2816
FF_CHUNK = 1408
NORM_EPS = 1e-5
ATTN_SCALE = HEAD_DIM ** -0.5
NEG = -1e30
HALO = 16

ADAM_LR = 0.001
ADAM_B1 = 0.9
ADAM_B2 = 0.999
ADAM_EPS = 1e-08
ADAM_WD = 0.01
ADAM_STEP = 10

VMEM_LIMIT = 56 * 1024 * 1024
SMALL_ROWS = 32

NT = (((1,), (1,)), ((), ()))
TN = (((0,), (0,)), ((), ()))


def _cparams(sem=None):
    return pltpu.CompilerParams(dimension_semantics=sem, vmem_limit_bytes=VMEM_LIMIT)


def _sig(v):
    return 1.0 / (1.0 + jnp.exp(-v))


def _row_tile(s):
    return 256 if s % 256 == 0 else s


def _shift_down(u, halo, k):
    r = pltpu.roll(u, k, axis=0)
    row = lax.broadcasted_iota(jnp.int32, u.shape, 0)
    for t in range(k):
        r = jnp.where(row == t, halo[HALO - k + t:HALO - k + t + 1, :], r)
    return r


def _shift_up(u, halo, k):
    n = u.shape[0]
    r = pltpu.roll(u, n - k, axis=0)
    row = lax.broadcasted_iota(jnp.int32, u.shape, 0)
    for t in range(k):
        r = jnp.where(row == n - k + t, halo[t:t + 1, :], r)
    return r


def _prev_halo_map(tm):
    return lambda i: (jnp.maximum(i * (tm // HALO) - 1, 0), 0)


def _next_halo_map(tm, s):
    return lambda i: (jnp.minimum((i + 1) * (tm // HALO), s // HALO - 1), 0)


def _full(shape):
    return pl.BlockSpec(shape, lambda *_: (0,) * len(shape))


def _rows(tm, c):
    return pl.BlockSpec((tm, c), lambda i: (i, 0))


def _all_gather(shards):
    n = len(shards)

    def body(*refs):
        ins, outs = refs[:n], refs[n:2 * n]
        send_sems, recv_sems, local_sems = refs[2 * n:]
        x, y, c = lax.axis_index("x"), lax.axis_index("y"), lax.axis_index("c")
        me, sibling = (x, y, c), (x, y, 1 - c)
        chips = [(1 - x, y), (x, 1 - y), (1 - x, 1 - y)]

        def rows(k, dev):
            r = ins[k].shape[0]
            start = pl.multiple_of((4 * dev[0] + 2 * dev[1] + dev[2]) * r, 8)
            return outs[k].at[pl.ds(start, r), :]

        def copy(k, j, block, to, src=None):
            return pltpu.make_async_remote_copy(
                src_ref=rows(k, block) if src is None else src, dst_ref=rows(k, block),
                send_sem=send_sems.at[7 * k + j], recv_sem=recv_sems.at[7 * k + j],
                device_id=to, device_id_type=MESH)

        mine = [pltpu.make_async_copy(ins[k], rows(k, me), local_sems.at[k]) for k in range(n)]
        for cp in mine:
            cp.start()
        first = []
        for k in range(n):
            first.append(copy(k, 0, me, sibling, src=ins[k]))
            first += [copy(k, 1 + j, me, (*chip, c), src=ins[k]) for j, chip in enumerate(chips)]
        for cp in first:
            cp.start()
        passed = []
        for j, chip in enumerate(chips):
            for k in range(n):
                copy(k, 1 + j, (*chip, c), me).wait_recv()
                fwd = copy(k, 4 + j, (*chip, c), sibling)
                fwd.start()
                passed.append(fwd)
        for k in range(n):
            copy(k, 0, sibling, me).wait_recv()
            for j, chip in enumerate(chips):
                copy(k, 4 + j, (*chip, 1 - c), me).wait_recv()
        for cp in first + passed:
            cp.wait_send()
        for cp in mine:
            cp.wait()

    any_spec = pl.BlockSpec(memory_space=pl.ANY)
    return pl.pallas_call(
        body, name="all_gather_weights",
        out_shape=[jax.ShapeDtypeStruct((N_DEV * s.shape[0], s.shape[1]), s.dtype) for s in shards],
        in_specs=[any_spec] * n, out_specs=[any_spec] * n,
        scratch_shapes=[pltpu.SemaphoreType.DMA((7 * n,)), pltpu.SemaphoreType.DMA((7 * n,)),
                        pltpu.SemaphoreType.DMA((n,))],
    )(*shards)


def _reduce_scatter(partials, small):
    n = len(partials)

    def body(*refs):
        ins, small_ref = refs[:n], refs[n]
        outs, small_out = refs[n + 1:2 * n + 1], refs[2 * n + 1]
        send_sems, recv_sems, local_sems = refs[2 * n + 2:]
        x, y, c = lax.axis_index("x"), lax.axis_index("y"), lax.axis_index("c")
        me_idx = 4 * x + 2 * y + c

        def slab(ref, idx, r):
            return ref.at[pl.ds(pl.multiple_of(idx * r, 8), r), :]

        copies, local = [], []
        for k in range(n + 1):
            src_ref = ins[k] if k < n else small_ref
            dst_ref = outs[k] if k < n else small_out
            r = dst_ref.shape[0] // N_DEV
            own_src = slab(src_ref, me_idx, r) if k < n else src_ref
            cp = pltpu.make_async_copy(own_src, slab(dst_ref, me_idx, r), local_sems.at[k])
            cp.start()
            local.append(cp)
            for j in range(1, N_DEV):
                peer = (x ^ (j >> 2), y ^ ((j >> 1) & 1), c ^ (j & 1))
                peer_idx = 4 * peer[0] + 2 * peer[1] + peer[2]
                src = slab(src_ref, peer_idx, r) if k < n else src_ref
                cp = pltpu.make_async_remote_copy(
                    src_ref=src, dst_ref=slab(dst_ref, me_idx, r),
                    send_sem=send_sems.at[7 * k + j - 1], recv_sem=recv_sems.at[7 * k + j - 1],
                    device_id=peer, device_id_type=MESH)
                cp.start()
                copies.append(cp)
        for cp in copies:
            cp.wait_recv()
        for cp in copies:
            cp.wait_send()
        for cp in local:
            cp.wait()

    any_spec = pl.BlockSpec(memory_space=pl.ANY)
    outs = [jax.ShapeDtypeStruct(p.shape, p.dtype) for p in partials]
    outs.append(jax.ShapeDtypeStruct((N_DEV * small.shape[0], small.shape[1]), small.dtype))
    return pl.pallas_call(
        body, name="reduce_scatter_grads", out_shape=outs,
        in_specs=[any_spec] * (n + 1), out_specs=[any_spec] * (n + 1),
        scratch_shapes=[pltpu.SemaphoreType.DMA((7 * (n + 1),)), pltpu.SemaphoreType.DMA((7 * (n + 1),)),
                        pltpu.SemaphoreType.DMA((n + 1,))],
    )(*partials, small)


def _norm_inproj(x, g, win_t, b_in):
    s = x.shape[0]
    tm = _row_tile(s)
    widths = (QKV_W, CBX_W, GATE_W)

    def body(x_ref, g_ref, w_ref, b_ref, xn_ref, qkv_ref, cbx_ref, gate_ref):
        xv = x_ref[...]
        r = lax.rsqrt(jnp.mean(xv * xv, axis=-1, keepdims=True) + NORM_EPS)
        xn = (xv * r * g_ref[...]).astype(BF16)
        xn_ref[...] = xn
        off = 0
        for o_ref, w in zip((qkv_ref, cbx_ref, gate_ref), widths):
            acc = lax.dot_general(xn, w_ref[off:off + w, :], NT, preferred_element_type=F32)
            o_ref[...] = (acc + b_ref[:, off:off + w]).astype(BF16)
            off += w

    return pl.pallas_call(
        body, name="norm_inproj", grid=(s // tm,),
        in_specs=[_rows(tm, D_MODEL), _full((1, D_MODEL)), _full((IN_W, D_MODEL)), _full((1, IN_W))],
        out_specs=[_rows(tm, D_MODEL)] + [_rows(tm, w) for w in widths],
        out_shape=[jax.ShapeDtypeStruct((s, D_MODEL), BF16)] + [jax.ShapeDtypeStruct((s, w), BF16) for w in widths],
        compiler_params=_cparams(("arbitrary",)),
    )(x, g, win_t, b_in)


def _attn_specs():
    prev = lambda n: jnp.maximum(n - 1, 0)
    return [pl.BlockSpec((BLOCK, ATTN_W), lambda n: (n, 0)),
            pl.BlockSpec((BLOCK, KV_W), lambda n: (prev(n), ATTN_W // KV_W)),
            pl.BlockSpec((BLOCK, KV_W), lambda n: (n, ATTN_W // KV_W)),
            pl.BlockSpec((BLOCK, KV_W), lambda n: (prev(n), ATTN_W // KV_W + 1)),
            pl.BlockSpec((BLOCK, KV_W), lambda n: (n, ATTN_W // KV_W + 1))]


def _lower_lanes():
    return lax.broadcasted_iota(jnp.int32, (BLOCK, 128), 1) < HEAD_DIM


def _stack_heads(val, kh):
    lower = _lower_lanes()
    parts = []
    for g in range(4):
        h = kh * 4 + g
        blk = val[:, (h // 2) * 128:(h // 2 + 1) * 128]
        keep = lower if h % 2 == 0 else jnp.logical_not(lower)
        parts.append(jnp.where(keep, blk, jnp.zeros_like(blk)))
    return jnp.concatenate(parts, axis=0)


def _dup_kv(prev_ref, cur_ref, kh):
    t = jnp.concatenate([prev_ref[...], cur_ref[...]], axis=0).astype(F32)
    rolled = pltpu.roll(t, HEAD_DIM, axis=1)
    lower = lax.broadcasted_iota(jnp.int32, t.shape, 1) < HEAD_DIM
    dup = jnp.where(lower, t, rolled) if kh == 0 else jnp.where(lower, rolled, t)
    return dup.astype(BF16)


def _attn_mask(n):
    row = lax.broadcasted_iota(jnp.int32, (4 * BLOCK, 2 * BLOCK), 0)
    kj = lax.broadcasted_iota(jnp.int32, (4 * BLOCK, 2 * BLOCK), 1)
    dist = (row & (BLOCK - 1)) + BLOCK - kj
    band = jnp.logical_and(dist >= 0, dist < BLOCK)
    return jnp.logical_and(band, jnp.logical_or(kj >= BLOCK, n > 0))


def _sink_col(sinks_ref, kh):
    gi = lax.broadcasted_iota(jnp.int32, (4 * BLOCK, 1), 0) // BLOCK
    col = jnp.zeros((4 * BLOCK, 1), F32)
    for g in range(4):
        col = jnp.where(gi == g, sinks_ref[0, kh * 4 + g], col)
    return col


def _attn_fwd(qkv, sinks):
    s = qkv.shape[0]

    def body(sinks_ref, q_ref, kp_ref, kc_ref, vp_ref, vc_ref, o_ref, lse_ref):
        n = pl.program_id(0)
        mask = _attn_mask(n)
        lower = _lower_lanes()
        lane = lax.broadcasted_iota(jnp.int32, (BLOCK, 128), 1)
        qv = q_ref[...]
        lse_out = jnp.zeros((BLOCK, 128), F32)
        for kh in range(2):
            qs = _stack_heads(qv, kh)
            kd, vd = _dup_kv(kp_ref, kc_ref, kh), _dup_kv(vp_ref, vc_ref, kh)
            sc = lax.dot_general(qs, kd, NT, preferred_element_type=F32) * ATTN_SCALE
            sc = jnp.where(mask, sc, NEG)
            sink = _sink_col(sinks_ref, kh)
            m = jnp.maximum(jnp.max(sc, axis=1, keepdims=True), sink)
            p = jnp.exp(sc - m)
            l = jnp.sum(p, axis=1, keepdims=True) + jnp.exp(sink - m)
            o = jnp.dot(p.astype(BF16), vd, preferred_element_type=F32) / l
            lse = m + jnp.log(l)
            for pair in range(2):
                lo = o[(2 * pair) * BLOCK:(2 * pair + 1) * BLOCK]
                hi = o[(2 * pair + 1) * BLOCK:(2 * pair + 2) * BLOCK]
                col = (kh * 2 + pair) * 128
                o_ref[:, col:col + 128] = jnp.where(lower, lo, hi).astype(BF16)
            for g in range(4):
                lse_out = jnp.where(lane == kh * 4 + g, lse[g * BLOCK:(g + 1) * BLOCK], lse_out)
        lse_ref[...] = lse_out

    return pl.pallas_call(
        body, name="attn_fwd", grid=(s // BLOCK,),
        in_specs=[pl.BlockSpec(memory_space=pltpu.SMEM)] + _attn_specs(),
        out_specs=[pl.BlockSpec((BLOCK, ATTN_W), lambda n: (n, 0)), pl.BlockSpec((BLOCK, 128), lambda n: (n, 0))],
        out_shape=[jax.ShapeDtypeStruct((s, ATTN_W), BF16), jax.ShapeDtypeStruct((s, 128), F32)],
        compiler_params=_cparams(("arbitrary",)),
    )(sinks, qkv, qkv, qkv, qkv, qkv)


def _conv_u(cbx_ref, halo_ref, w_ref, first):
    cb = cbx_ref[:, 0:CONV_W].astype(F32)
    cc = cbx_ref[:, CONV_W:2 * CONV_W].astype(F32)
    cx = cbx_ref[:, 2 * CONV_W:3 * CONV_W].astype(F32)
    u = cc * cx
    uh = halo_ref[:, CONV_W:2 * CONV_W].astype(F32) * halo_ref[:, 2 * CONV_W:3 * CONV_W].astype(F32)
    uh = jnp.where(first, 0.0, uh)
    u1, u2 = _shift_down(u, uh, 1), _shift_down(u, uh, 2)
    cv = w_ref[0:1, :] * u2 + w_ref[1:2, :] * u1 + w_ref[2:3, :] * u
    return cb, cc, cx, u, u1, u2, cv


def _mix_fwd(x, cbx, gates, attn, conv_w, wa, wc, wout):
    s = x.shape[0]
    tm = _row_tile(s)

    def body(x_ref, cbx_ref, halo_ref, gate_ref, attn_ref, cw_ref, wa_ref, wc_ref, wo_ref,
             conv_ref, ap_ref, cp_ref, mg_ref, h1_ref):
        first = pl.program_id(0) == 0
        cb, _, _, _, _, _, cv = _conv_u(cbx_ref, halo_ref, cw_ref, first)
        conv = (cb * cv).astype(BF16)
        conv_ref[...] = conv
        ap = jnp.dot(attn_ref[...], wa_ref[...], preferred_element_type=F32)
        cp = jnp.dot(conv, wc_ref[...], preferred_element_type=F32)
        ap_ref[...] = ap.astype(BF16)
        cp_ref[...] = cp.astype(BF16)
        ga = gate_ref[:, 0:D_MODEL].astype(F32)
        gc = gate_ref[:, D_MODEL:2 * D_MODEL].astype(F32)
        merged = (_sig(ga) * ap + _sig(gc) * cp).astype(BF16)
        mg_ref[...] = merged
        h1_ref[...] = x_ref[...] + jnp.dot(merged, wo_ref[...], preferred_element_type=F32)

    return pl.pallas_call(
        body, name="mix_fwd", grid=(s // tm,),
        in_specs=[_rows(tm, D_MODEL), _rows(tm, CBX_W), pl.BlockSpec((HALO, CBX_W), _prev_halo_map(tm)),
                  _rows(tm, GATE_W), _rows(tm, ATTN_W), _full((3, CONV_W)), _full((ATTN_W, D_MODEL)),
                  _full((CONV_W, D_MODEL)), _full((D_MODEL, D_MODEL))],
        out_specs=[_rows(tm, CONV_W), _rows(tm, D_MODEL), _rows(tm, D_MODEL), _rows(tm, D_MODEL), _rows(tm, D_MODEL)],
        out_shape=[jax.ShapeDtypeStruct((s, CONV_W), BF16), jax.ShapeDtypeStruct((s, D_MODEL), BF16),
                   jax.ShapeDtypeStruct((s, D_MODEL), BF16), jax.ShapeDtypeStruct((s, D_MODEL), BF16),
                   jax.ShapeDtypeStruct((s, D_MODEL), F32)],
        compiler_params=_cparams(("arbitrary",)),
    )(x, cbx, cbx, gates, attn, conv_w, wa, wc, wout)


def _ffn_up(h1, g, wup_t):
    s = h1.shape[0]
    tm = _row_tile(s)

    def body(h_ref, g_ref, w_ref, hn_ref, up_ref):
        hv = h_ref[...]
        r = lax.rsqrt(jnp.mean(hv * hv, axis=-1, keepdims=True) + NORM_EPS)
        hn = (hv * r * g_ref[...]).astype(BF16)
        hn_ref[...] = hn
        for c in range(2 * D_FF // FF_CHUNK):
            sl = slice(c * FF_CHUNK, (c + 1) * FF_CHUNK)
            up_ref[:, sl] = lax.dot_general(hn, w_ref[sl, :], NT, preferred_element_type=F32).astype(BF16)

    return pl.pallas_call(
        body, name="ffn_up", grid=(s // tm,),
        in_specs=[_rows(tm, D_MODEL), _full((1, D_MODEL)), _full((2 * D_FF, D_MODEL))],
        out_specs=[_rows(tm, D_MODEL), _rows(tm, 2 * D_FF)],
        out_shape=[jax.ShapeDtypeStruct((s, D_MODEL), BF16), jax.ShapeDtypeStruct((s, 2 * D_FF), BF16)],
        compiler_params=_cparams(("arbitrary",)),
    )(h1, g, wup_t)


def _ffn_conv_cols(up_ref, halo_ref, fcw_ref, first, off):
    u = up_ref[:, off:off + FF_CHUNK].astype(F32)
    uh = jnp.where(first, 0.0, halo_ref[:, off:off + FF_CHUNK].astype(F32))
    w = fcw_ref[:, off:off + FF_CHUNK]
    return w[0:1] * _shift_down(u, uh, 2) + w[1:2] * _shift_down(u, uh, 1) + w[2:3] * u


def _ffn_down_loss(up_pre, fcw, wdown, h1, fnorm, target):
    s = h1.shape[0]
    tm = _row_tile(s)

    def body(up_ref, halo_ref, fcw_ref, wd_ref, h1_ref, fn_ref, t_ref, act_ref, dh2_ref, loss_ref, dfn_ref):
        i = pl.program_id(0)

        @pl.when(i == 0)
        def _():
            loss_ref[...] = jnp.zeros_like(loss_ref)
            dfn_ref[...] = jnp.zeros_like(dfn_ref)

        h2 = h1_ref[...]
        for c in range(D_FF // FF_CHUNK):
            gate = _ffn_conv_cols(up_ref, halo_ref, fcw_ref, i == 0, c * FF_CHUNK)
            val = _ffn_conv_cols(up_ref, halo_ref, fcw_ref, i == 0, D_FF + c * FF_CHUNK)
            act = (gate * _sig(gate) * val).astype(BF16)
            act_ref[:, c * FF_CHUNK:(c + 1) * FF_CHUNK] = act
            h2 = h2 + jnp.dot(act, wd_ref[c * FF_CHUNK:(c + 1) * FF_CHUNK, :], preferred_element_type=F32)
        r = lax.rsqrt(jnp.mean(h2 * h2, axis=-1, keepdims=True) + NORM_EPS)
        yhat = h2 * r
        fn = fn_ref[...]
        diff = yhat * fn - t_ref[...]
        loss_ref[...] += 0.5 * jnp.sum(jnp.sum(diff * diff, axis=1, keepdims=True), axis=0, keepdims=True) / D_MODEL
        dy = diff * (1.0 / D_MODEL)
        dfn_ref[...] += jnp.sum(dy * yhat, axis=0, keepdims=True)
        dyh = dy * fn
        dh2_ref[...] = r * (dyh - yhat * jnp.mean(dyh * yhat, axis=-1, keepdims=True))

    return pl.pallas_call(
        body, name="ffn_down_loss", grid=(s // tm,),
        in_specs=[_rows(tm, 2 * D_FF), pl.BlockSpec((HALO, 2 * D_FF), _prev_halo_map(tm)), _full((3, 2 * D_FF)),
                  _full((D_FF, D_MODEL)), _rows(tm, D_MODEL), _full((1, D_MODEL)), _rows(tm, D_MODEL)],
        out_specs=[_rows(tm, D_FF), _rows(tm, D_MODEL), _full((1, 128)), _full((1, D_MODEL))],
        out_shape=[jax.ShapeDtypeStruct((s, D_FF), BF16), jax.ShapeDtypeStruct((s, D_MODEL), F32),
                   jax.ShapeDtypeStruct((1, 128), F32), jax.ShapeDtypeStruct((1, D_MODEL), F32)],
        compiler_params=_cparams(("arbitrary",)),
    )(up_pre, up_pre, fcw, wdown, h1, fnorm, target)


def _ffn_act_bwd(dh2, wdown, up_pre, fcw):
    s = dh2.shape[0]
    tm = _row_tile(s)

    def body(dh_ref, wd_ref, up_ref, halo_ref, fcw_ref, dup_ref):
        first = pl.program_id(0) == 0
        dh = dh_ref[...].astype(BF16)
        for c in range(D_FF // FF_CHUNK):
            sl = slice(c * FF_CHUNK, (c + 1) * FF_CHUNK)
            dact = lax.dot_general(dh, wd_ref[sl, :], NT, preferred_element_type=F32)
            gate = _ffn_conv_cols(up_ref, halo_ref, fcw_ref, first, c * FF_CHUNK)
            val = _ffn_conv_cols(up_ref, halo_ref, fcw_ref, first, D_FF + c * FF_CHUNK)
            sg = _sig(gate)
            dup_ref[:, sl] = (dact * val * (sg * (1.0 + gate * (1.0 - sg)))).astype(BF16)
            dup_ref[:, D_FF + c * FF_CHUNK:D_FF + (c + 1) * FF_CHUNK] = (dact * gate * sg).astype(BF16)

    return pl.pallas_call(
        body, name="ffn_act_bwd", grid=(s // tm,),
        in_specs=[_rows(tm, D_MODEL), _full((D_FF, D_MODEL)), _rows(tm, 2 * D_FF),
                  pl.BlockSpec((HALO, 2 * D_FF), _prev_halo_map(tm)), _full((3, 2 * D_FF))],
        out_specs=_rows(tm, 2 * D_FF),
        out_shape=jax.ShapeDtypeStruct((s, 2 * D_FF), BF16),
        compiler_params=_cparams(("arbitrary",)),
    )(dh2, wdown, up_pre, up_pre, fcw)


def _conv_bwd(dy, x, w, width, chunk, name):
    s = dy.shape[0]
    tm = _row_tile(s)

    def body(dy_ref, dyn_ref, x_ref, xh_ref, w_ref, dx_ref, dw_ref):
        i = pl.program_id(0)

        @pl.when(i == 0)
        def _():
            dw_ref[...] = jnp.zeros_like(dw_ref)

        last = i == s // tm - 1
        for c in range(width // chunk):
            sl = slice(c * chunk, (c + 1) * chunk)
            d = dy_ref[:, sl].astype(F32)
            dn = jnp.where(last, 0.0, dyn_ref[:, sl].astype(F32))
            xv = x_ref[:, sl].astype(F32)
            xh = jnp.where(i == 0, 0.0, xh_ref[:, sl].astype(F32))
            wv = w_ref[:, sl]
            dx = wv[2:3] * d + wv[1:2] * _shift_up(d, dn, 1) + wv[0:1] * _shift_up(d, dn, 2)
            dx_ref[:, sl] = dx.astype(BF16)
            dw_ref[0:1, sl] += jnp.sum(d * _shift_down(xv, xh, 2), axis=0, keepdims=True)
            dw_ref[1:2, sl] += jnp.sum(d * _shift_down(xv, xh, 1), axis=0, keepdims=True)
            dw_ref[2:3, sl] += jnp.sum(d * xv, axis=0, keepdims=True)

    return pl.pallas_call(
        body, name=name, grid=(s // tm,),
        in_specs=[_rows(tm, width), pl.BlockSpec((HALO, width), _next_halo_map(tm, s)), _rows(tm, width),
                  pl.BlockSpec((HALO, width), _prev_halo_map(tm)), _full((3, width))],
        out_specs=[_rows(tm, width), _full((3, width))],
        out_shape=[jax.ShapeDtypeStruct((s, width), BF16), jax.ShapeDtypeStruct((3, width), F32)],
        compiler_params=_cparams(("arbitrary",)),
    )(dy, dy, x, x, w)


def _matmul_tn(a, b, tk, name, ts=512):
    s, ka = a.shape
    n = b.shape[1]
    ts = min(ts, s)
    steps = s // ts

    def body(a_ref, b_ref, o_ref, acc_ref):
        j = pl.program_id(1)

        @pl.when(j == 0)
        def _():
            acc_ref[...] = jnp.zeros_like(acc_ref)

        acc_ref[...] += lax.dot_general(a_ref[...].astype(BF16), b_ref[...].astype(BF16), TN,
                                        preferred_element_type=F32)

        @pl.when(j == steps - 1)
        def _():
            o_ref[...] = acc_ref[...].astype(BF16)

    return pl.pallas_call(
        body, name=name, grid=(ka // tk, steps),
        in_specs=[pl.BlockSpec((ts, tk), lambda i, j: (j, i)), pl.BlockSpec((ts, n), lambda i, j: (j, 0))],
        out_specs=pl.BlockSpec((tk, n), lambda i, j: (i, 0)),
        out_shape=jax.ShapeDtypeStruct((ka, n), BF16),
        scratch_shapes=[pltpu.VMEM((tk, n), F32)],
        compiler_params=_cparams(("arbitrary", "arbitrary")),
    )(a, b)


def _matmul_tn_colslabs(a, b, name, ts=512):
    s, ka = a.shape
    n = b.shape[1]
    ts = min(ts, s)
    steps = s // ts

    def body(a_ref, b_ref, o_ref, acc_ref):
        j = pl.program_id(1)

        @pl.when(j == 0)
        def _():
            acc_ref[...] = jnp.zeros_like(acc_ref)

        acc_ref[...] += lax.dot_general(a_ref[...], b_ref[...], TN, preferred_element_type=F32)

        @pl.when(j == steps - 1)
        def _():
            o_ref[...] = acc_ref[...].astype(BF16)

    return pl.pallas_call(
        body, name=name, grid=(n // 128, steps),
        in_specs=[pl.BlockSpec((ts, ka), lambda i, j: (j, 0)), pl.BlockSpec((ts, 128), lambda i, j: (j, i))],
        out_specs=pl.BlockSpec((ka, 128), lambda i, j: (i, 0)),
        out_shape=jax.ShapeDtypeStruct((n // 128 * ka, 128), BF16),
        scratch_shapes=[pltpu.VMEM((ka, 128), F32)],
        compiler_params=_cparams(("arbitrary", "arbitrary")),
    )(a, b)


def _norm_bwd_tile(xv, g, dy):
    r = lax.rsqrt(jnp.mean(xv * xv, axis=-1, keepdims=True) + NORM_EPS)
    xhat = xv * r
    dg = jnp.sum(dy * xhat, axis=0, keepdims=True)
    dyh = dy * g
    return r * (dyh - xhat * jnp.mean(dyh * xhat, axis=-1, keepdims=True)), dg


def _ffn_up_bwd(dup_pre, wup_t, h1, g, dh2):
    s = h1.shape[0]
    tm = _row_tile(s)

    def body(du_ref, w_ref, h_ref, g_ref, dh2_ref, dh1_ref, dg_ref):
        @pl.when(pl.program_id(0) == 0)
        def _():
            dg_ref[...] = jnp.zeros_like(dg_ref)

        dhn = jnp.dot(du_ref[...], w_ref[...], preferred_element_type=F32)
        dx, dg = _norm_bwd_tile(h_ref[...], g_ref[...], dhn)
        dg_ref[...] += dg
        dh1_ref[...] = dh2_ref[...] + dx

    return pl.pallas_call(
        body, name="ffn_up_bwd", grid=(s // tm,),
        in_specs=[_rows(tm, 2 * D_FF), _full((2 * D_FF, D_MODEL)), _rows(tm, D_MODEL), _full((1, D_MODEL)),
                  _rows(tm, D_MODEL)],
        out_specs=[_rows(tm, D_MODEL), _full((1, D_MODEL))],
        out_shape=[jax.ShapeDtypeStruct((s, D_MODEL), F32), jax.ShapeDtypeStruct((1, D_MODEL), F32)],
        compiler_params=_cparams(("arbitrary",)),
    )(dup_pre, wup_t, h1, g, dh2)


def _mix_bwd(dh1, wout, gates, ap, cp, wa, wc, cbx, conv_w):
    s = dh1.shape[0]
    tm = _row_tile(s)

    def body(dh_ref, wo_ref, gate_ref, ap_ref, cp_ref, wa_ref, wc_ref, cbx_ref, halo_ref, cw_ref,
             dg_ref, da_ref, dc_ref, dattn_ref, dcb_ref, dcv_ref):
        first = pl.program_id(0) == 0
        dm = lax.dot_general(dh_ref[...].astype(BF16), wo_ref[...], NT, preferred_element_type=F32)
        sa = _sig(gate_ref[:, 0:D_MODEL].astype(F32))
        sc = _sig(gate_ref[:, D_MODEL:2 * D_MODEL].astype(F32))
        da = (dm * sa).astype(BF16)
        dc = (dm * sc).astype(BF16)
        da_ref[...] = da
        dc_ref[...] = dc
        dg_ref[:, 0:D_MODEL] = (dm * ap_ref[...].astype(F32) * sa * (1.0 - sa)).astype(BF16)
        dg_ref[:, D_MODEL:2 * D_MODEL] = (dm * cp_ref[...].astype(F32) * sc * (1.0 - sc)).astype(BF16)
        dattn_ref[...] = lax.dot_general(da, wa_ref[...], NT, preferred_element_type=F32).astype(BF16)
        dconv = lax.dot_general(dc, wc_ref[...], NT, preferred_element_type=F32)
        cb, _, _, _, _, _, cv = _conv_u(cbx_ref, halo_ref, cw_ref, first)
        dcb_ref[...] = (dconv * cv).astype(BF16)
        dcv_ref[...] = (dconv * cb).astype(BF16)

    return pl.pallas_call(
        body, name="mix_bwd", grid=(s // tm,),
        in_specs=[_rows(tm, D_MODEL), _full((D_MODEL, D_MODEL)), _rows(tm, GATE_W), _rows(tm, D_MODEL),
                  _rows(tm, D_MODEL), _full((ATTN_W, D_MODEL)), _full((CONV_W, D_MODEL)), _rows(tm, CBX_W),
                  pl.BlockSpec((HALO, CBX_W), _prev_halo_map(tm)), _full((3, CONV_W))],
        out_specs=[_rows(tm, GATE_W), _rows(tm, D_MODEL), _rows(tm, D_MODEL), _rows(tm, ATTN_W),
                   _rows(tm, CONV_W), _rows(tm, CONV_W)],
        out_shape=[jax.ShapeDtypeStruct((s, GATE_W), BF16), jax.ShapeDtypeStruct((s, D_MODEL), BF16),
                   jax.ShapeDtypeStruct((s, D_MODEL), BF16), jax.ShapeDtypeStruct((s, ATTN_W), BF16),
                   jax.ShapeDtypeStruct((s, CONV_W), BF16), jax.ShapeDtypeStruct((s, CONV_W), BF16)],
        compiler_params=_cparams(("arbitrary",)),
    )(dh1, wout, gates, ap, cp, wa, wc, cbx, cbx, conv_w)


def _conv_branch_bwd(dcv, cbx, conv_w):
    s = dcv.shape[0]
    tm = _row_tile(s)

    def body(d_ref, dn_ref, cbx_ref, halo_ref, w_ref, dcc_ref, dcx_ref, dw_ref):
        i = pl.program_id(0)

        @pl.when(i == 0)
        def _():
            dw_ref[...] = jnp.zeros_like(dw_ref)

        last = i == s // tm - 1
        _, cc, cx, u, u1, u2, _ = _conv_u(cbx_ref, halo_ref, w_ref, i == 0)
        d = d_ref[...].astype(F32)
        dn = jnp.where(last, 0.0, dn_ref[...].astype(F32))
        du = w_ref[2:3, :] * d + w_ref[1:2, :] * _shift_up(d, dn, 1) + w_ref[0:1, :] * _shift_up(d, dn, 2)
        dcc_ref[...] = (du * cx).astype(BF16)
        dcx_ref[...] = (du * cc).astype(BF16)
        dw_ref[0:1, :] += jnp.sum(d * u2, axis=0, keepdims=True)
        dw_ref[1:2, :] += jnp.sum(d * u1, axis=0, keepdims=True)
        dw_ref[2:3, :] += jnp.sum(d * u, axis=0, keepdims=True)

    return pl.pallas_call(
        body, name="conv_branch_bwd", grid=(s // tm,),
        in_specs=[_rows(tm, CONV_W), pl.BlockSpec((HALO, CONV_W), _next_halo_map(tm, s)), _rows(tm, CBX_W),
                  pl.BlockSpec((HALO, CBX_W), _prev_halo_map(tm)), _full((3, CONV_W))],
        out_specs=[_rows(tm, CONV_W), _rows(tm, CONV_W), _full((3, CONV_W))],
        out_shape=[jax.ShapeDtypeStruct((s, CONV_W), BF16), jax.ShapeDtypeStruct((s, CONV_W), BF16),
                   jax.ShapeDtypeStruct((3, CONV_W), F32)],
        compiler_params=_cparams(("arbitrary",)),
    )(dcv, dcv, cbx, cbx, conv_w)


def _attn_bwd(qkv, sinks, attn, lse, dattn):
    s = qkv.shape[0]

    def body(sinks_ref, q_ref, kp_ref, kc_ref, vp_ref, vc_ref, o_ref, lse_ref, do_ref,
             dq_ref, dk_ref, dv_ref, ds_ref):
        n = pl.program_id(0)

        @pl.when(n == 0)
        def _():
            dk_ref[...] = jnp.zeros_like(dk_ref)
            dv_ref[...] = jnp.zeros_like(dv_ref)
            ds_ref[...] = jnp.zeros_like(ds_ref)

        mask = _attn_mask(n)
        lower = _lower_lanes()
        lane = lax.broadcasted_iota(jnp.int32, (BLOCK, 128), 1)
        lower2 = lax.broadcasted_iota(jnp.int32, (2 * BLOCK, 128), 1) < HEAD_DIM
        lane1 = lax.broadcasted_iota(jnp.int32, (1, 128), 1)
        qv, ov, dov, lsev = q_ref[...], o_ref[...], do_ref[...], lse_ref[...]
        dk_fold, dv_fold = [], []
        dsink = jnp.zeros((1, 128), F32)
        for kh in range(2):
            qs = _stack_heads(qv, kh)
            dos = _stack_heads(dov, kh)
            os_ = _stack_heads(ov, kh)
            kd, vd = _dup_kv(kp_ref, kc_ref, kh), _dup_kv(vp_ref, vc_ref, kh)
            lse = jnp.concatenate(
                [jnp.sum(jnp.where(lane == kh * 4 + g, lsev, 0.0), axis=1, keepdims=True) for g in range(4)], axis=0)
            sc = lax.dot_general(qs, kd, NT, preferred_element_type=F32) * ATTN_SCALE
            p = jnp.exp(jnp.where(mask, sc, NEG) - lse)
            dp = lax.dot_general(dos, vd, NT, preferred_element_type=F32)
            delta = jnp.sum(dos.astype(F32) * os_.astype(F32), axis=1, keepdims=True)
            dsc = (p * (dp - delta) * ATTN_SCALE).astype(BF16)
            dqs = jnp.dot(dsc, kd, preferred_element_type=F32)
            for pair in range(2):
                lo = dqs[(2 * pair) * BLOCK:(2 * pair + 1) * BLOCK]
                hi = dqs[(2 * pair + 1) * BLOCK:(2 * pair + 2) * BLOCK]
                col = (kh * 2 + pair) * 128
                dq_ref[:, col:col + 128] = jnp.where(lower, lo, hi).astype(BF16)
            dkd = lax.dot_general(dsc, qs, TN, preferred_element_type=F32)
            dvd = lax.dot_general(p.astype(BF16), dos, TN, preferred_element_type=F32)
            dk_fold.append(dkd + pltpu.roll(dkd, HEAD_DIM, axis=1))
            dv_fold.append(dvd + pltpu.roll(dvd, HEAD_DIM, axis=1))
            psink = jnp.exp(_sink_col(sinks_ref, kh) - lse) * delta
            for g in range(4):
                tot = jnp.sum(psink[g * BLOCK:(g + 1) * BLOCK], axis=0, keepdims=True)
                dsink = dsink - jnp.where(lane1 == kh * 4 + g, tot, 0.0)
        dk2 = jnp.where(lower2, dk_fold[0], dk_fold[1])
        dv2 = jnp.where(lower2, dv_fold[0], dv_fold[1])
        ds_ref[...] += dsink
        cur = pl.ds(pl.multiple_of(n * BLOCK, BLOCK), BLOCK)
        dk_ref[cur, :] += dk2[BLOCK:]
        dv_ref[cur, :] += dv2[BLOCK:]

        @pl.when(n > 0)
        def _():
            prev = pl.ds(pl.multiple_of((n - 1) * BLOCK, BLOCK), BLOCK)
            dk_ref[prev, :] += dk2[:BLOCK]
            dv_ref[prev, :] += dv2[:BLOCK]

    blk = lambda w: pl.BlockSpec((BLOCK, w), lambda n: (n, 0))
    return pl.pallas_call(
        body, name="attn_bwd", grid=(s // BLOCK,),
        in_specs=[pl.BlockSpec(memory_space=pltpu.SMEM)] + _attn_specs() + [blk(ATTN_W), blk(128), blk(ATTN_W)],
        out_specs=[blk(ATTN_W), _full((s, KV_W)), _full((s, KV_W)), _full((1, 128))],
        out_shape=[jax.ShapeDtypeStruct((s, ATTN_W), BF16), jax.ShapeDtypeStruct((s, KV_W), F32),
                   jax.ShapeDtypeStruct((s, KV_W), F32), jax.ShapeDtypeStruct((1, 128), F32)],
        compiler_params=_cparams(("arbitrary",)),
    )(sinks, qkv, qkv, qkv, qkv, qkv, attn, lse, dattn)


def _inproj_bwd(dq, dk, dv, dcb, dcc, dcx, dgates, win_t, x, g, dh1):
    s = x.shape[0]
    tm = _row_tile(s)
    pieces = (ATTN_W, KV_W, KV_W, CONV_W, CONV_W, CONV_W, GATE_W)

    def body(dq_ref, dk_ref, dv_ref, dcb_ref, dcc_ref, dcx_ref, dgt_ref, w_ref, x_ref, g_ref, dh_ref,
             dp_ref, dx_ref, db_ref, dg_ref):
        @pl.when(pl.program_id(0) == 0)
        def _():
            db_ref[...] = jnp.zeros_like(db_ref)
            dg_ref[...] = jnp.zeros_like(dg_ref)

        off = 0
        for ref, w in zip((dq_ref, dk_ref, dv_ref, dcb_ref, dcc_ref, dcx_ref, dgt_ref), pieces):
            v = ref[...].astype(BF16)
            dp_ref[:, off:off + w] = v
            db_ref[:, off:off + w] += jnp.sum(v.astype(F32), axis=0, keepdims=True)
            off += w
        dxn = jnp.dot(dp_ref[...], w_ref[...], preferred_element_type=F32)
        dx, dg = _norm_bwd_tile(x_ref[...], g_ref[...], dxn)
        dg_ref[...] += dg
        dx_ref[...] = dh_ref[...] + dx

    return pl.pallas_call(
        body, name="inproj_bwd", grid=(s // tm,),
        in_specs=[_rows(tm, w) for w in pieces] + [_full((IN_W, D_MODEL)), _rows(tm, D_MODEL), _full((1, D_MODEL)),
                                                   _rows(tm, D_MODEL)],
        out_specs=[_rows(tm, IN_W), _rows(tm, D_MODEL), _full((1, IN_W)), _full((1, D_MODEL))],
        out_shape=[jax.ShapeDtypeStruct((s, IN_W), BF16), jax.ShapeDtypeStruct((s, D_MODEL), F32),
                   jax.ShapeDtypeStruct((1, IN_W), F32), jax.ShapeDtypeStruct((1, D_MODEL), F32)],
        compiler_params=_cparams(("arbitrary",)),
    )(dq, dk, dv, dcb, dcc, dcx, dgates, win_t, x, g, dh1)


def _adam_math(w, g, m, v):
    m2 = ADAM_B1 * m + (1.0 - ADAM_B1) * g
    v2 = ADAM_B2 * v + (1.0 - ADAM_B2) * (g * g)
    m_hat = m2 / (1.0 - ADAM_B1 ** ADAM_STEP)
    v_hat = v2 / (1.0 - ADAM_B2 ** ADAM_STEP)
    delta = -ADAM_LR * (m_hat / (jnp.sqrt(v_hat) + ADAM_EPS) + ADAM_WD * w)
    return delta, m2, v2


def _sum_slots(ref):
    tot = ref[0].astype(F32)
    for i in range(1, N_DEV):
        tot = tot + ref[i].astype(F32)
    return tot


def _sum_adamw(parts, w, m, v, tr, name):
    r, c = w.shape

    def body(p_ref, w_ref, m_ref, v_ref, g_ref, d_ref, m2_ref, v2_ref):
        g = _sum_slots(p_ref)
        g_ref[...] = g
        d_ref[...], m2_ref[...], v2_ref[...] = _adam_math(w_ref[...], g, m_ref[...], v_ref[...])

    spec = pl.BlockSpec((tr, c), lambda i: (i, 0))
    return pl.pallas_call(
        body, name=name, grid=(r // tr,),
        in_specs=[pl.BlockSpec((N_DEV, tr, c), lambda i: (0, i, 0)), spec, spec, spec],
        out_specs=[spec] * 4, out_shape=[jax.ShapeDtypeStruct((r, c), F32)] * 4,
        compiler_params=_cparams(("arbitrary",)),
    )(parts, w, m, v)


def _sum_only(parts, tr, name):
    _, r, c = parts.shape

    def body(p_ref, g_ref):
        g_ref[...] = _sum_slots(p_ref)

    return pl.pallas_call(
        body, name=name, grid=(r // tr,),
        in_specs=[pl.BlockSpec((N_DEV, tr, c), lambda i: (0, i, 0))],
        out_specs=pl.BlockSpec((tr, c), lambda i: (i, 0)), out_shape=jax.ShapeDtypeStruct((r, c), F32),
        compiler_params=_cparams(("arbitrary",)),
    )(parts)


def _adamw(w, g, m, v, tr, name):
    r, c = w.shape

    def body(w_ref, g_ref, m_ref, v_ref, d_ref, m2_ref, v2_ref):
        d_ref[...], m2_ref[...], v2_ref[...] = _adam_math(w_ref[...], g_ref[...], m_ref[...], v_ref[...])

    spec = pl.BlockSpec((tr, c), lambda i: (i, 0))
    return pl.pallas_call(
        body, name=name, grid=(r // tr,), in_specs=[spec] * 4, out_specs=[spec] * 3,
        out_shape=[jax.ShapeDtypeStruct((r, c), F32)] * 3, compiler_params=_cparams(("arbitrary",)),
    )(w, g, m, v)


def _pad_cols(a, c):
    return jnp.pad(a, ((0, 0), (0, c - a.shape[1])))


def kernel(x, mix_norm, w_in, b_in, sinks, conv_w, w_attn_branch, w_conv_branch, w_out, ffn_norm, w_up, ffn_conv_w, w_down, final_norm, loss_target, m_mix_norm, m_w_in, m_b_in, m_sinks, m_conv_w, m_w_attn_branch, m_w_conv_branch, m_w_out, m_ffn_norm, m_w_up, m_ffn_conv_w, m_w_down, m_final_norm, v_mix_norm, v_w_in, v_b_in, v_sinks, v_conv_w, v_w_attn_branch, v_w_conv_branch, v_w_out, v_ffn_norm, v_w_up, v_ffn_conv_w, v_w_down, v_final_norm):
    s = x.shape[1]
    xs, tgt = x[0], loss_target[0]
    me = 4 * lax.axis_index("x") + 2 * lax.axis_index("y") + lax.axis_index("c")

    conv_sh = jnp.concatenate([_pad_cols(ffn_conv_w[0], 768), _pad_cols(conv_w[0], 768),
                               jnp.zeros((2, 768), F32)], axis=0)
    shards = [w_in[0].T.astype(BF16), w_up[0].T.astype(BF16), w_out[0].astype(BF16), w_down[0].astype(BF16),
              w_attn_branch[0].astype(BF16), w_conv_branch[0].astype(BF16), conv_sh]
    win_t, wup_t, wout, wdown, wa_s, wc_s, conv_g = _all_gather(shards)
    unslab = lambda t: jnp.transpose(t.reshape(N_DEV, ATTN_W, 128), (1, 0, 2)).reshape(ATTN_W, D_MODEL)
    wa, wc = unslab(wa_s), unslab(wc_s)
    conv_g = conv_g.reshape(N_DEV, 8, 768)
    fcw = jnp.transpose(conv_g[:, 0:3, :2 * D_FF // N_DEV], (1, 0, 2)).reshape(3, 2 * D_FF)
    cw = jnp.transpose(conv_g[:, 3:6, :CONV_W // N_DEV], (1, 0, 2)).reshape(3, CONV_W)

    xn, qkv, cbx, gates = _norm_inproj(xs, mix_norm, win_t, b_in)
    attn, lse = _attn_fwd(qkv, sinks)
    conv, ap, cp, merged, h1 = _mix_fwd(xs, cbx, gates, attn, cw, wa, wc, wout)
    hn, up_pre = _ffn_up(h1, ffn_norm, wup_t)
    act, dh2, loss_p, dfn_p = _ffn_down_loss(up_pre, fcw, wdown, h1, final_norm.reshape(1, D_MODEL), tgt)

    dup = _ffn_act_bwd(dh2, wdown, up_pre, fcw)
    g_wdown = _matmul_tn(act, dh2, FF_CHUNK, "grad_w_down")
    dup_pre, dfcw_p = _conv_bwd(dup, up_pre, fcw, 2 * D_FF, FF_CHUNK, "ffn_conv_bwd")
    g_wup_t = _matmul_tn(dup_pre, hn, FF_CHUNK, "grad_w_up")
    dh1, dffn_p = _ffn_up_bwd(dup_pre, wup_t, h1, ffn_norm, dh2)
    g_wout = _matmul_tn(merged, dh1, D_MODEL, "grad_w_out")
    dgates, da, dc, dattn, dcb, dcv = _mix_bwd(dh1, wout, gates, ap, cp, wa, wc, cbx, cw)
    g_wa = _matmul_tn_colslabs(attn, da, "grad_w_attn_branch")
    g_wc = _matmul_tn_colslabs(conv, dc, "grad_w_conv_branch")
    dcc, dcx, dcw_p = _conv_branch_bwd(dcv, cbx, cw)
    dq, dk, dv, dsink_p = _attn_bwd(qkv, sinks, attn, lse, dattn)
    dproj, dx, dbin_p, dmix_p = _inproj_bwd(dq, dk, dv, dcb, dcc, dcx, dgates, win_t, xs, mix_norm, dh1)
    g_win_t = _matmul_tn(dproj, xn, IN_W // 2, "grad_w_in")

    row = lambda a: _pad_cols(a.reshape(1, -1), D_MODEL)
    small = jnp.concatenate(
        [dmix_p, dffn_p, dfn_p, row(dsink_p[:, :N_HEADS]), row(loss_p[:, :1]),
         _pad_cols(dbin_p, 5 * D_MODEL).reshape(5, D_MODEL), _pad_cols(dcw_p, D_MODEL),
         _pad_cols(dfcw_p, 6 * D_MODEL).reshape(18, D_MODEL), jnp.zeros((1, D_MODEL), F32)], axis=0)
    r_win, r_wup, r_wout, r_wdown, r_wa, r_wc, r_small = _reduce_scatter(
        [g_win_t, g_wup_t, g_wout, g_wdown, g_wa, g_wc], small)

    def slots(t):
        return t.reshape(N_DEV, t.shape[0] // N_DEV, t.shape[1])

    small_t = _sum_only(slots(r_small), SMALL_ROWS, "sum_small")
    g_mix, g_ffn, g_fn = small_t[0:1], small_t[1:2], small_t[2:3]
    g_sinks, loss = small_t[3:4, :N_HEADS], small_t[4, 0]
    g_bin = small_t[5:10].reshape(1, 5 * D_MODEL)[:, :IN_W]
    g_cw_full = small_t[10:13, :CONV_W]
    g_fcw_full = small_t[13:31].reshape(3, 6 * D_MODEL)[:, :2 * D_FF]
    g_cw = lax.dynamic_slice_in_dim(g_cw_full, me * (CONV_W // N_DEV), CONV_W // N_DEV, axis=1)
    g_fcw = lax.dynamic_slice_in_dim(g_fcw_full, me * (2 * D_FF // N_DEV), 2 * D_FF // N_DEV, axis=1)

    g_win = _sum_only(slots(r_win), IN_W // N_DEV // 2, "sum_w_in").T
    g_wup = _sum_only(slots(r_wup), 2 * D_FF // N_DEV // 2, "sum_w_up").T
    big = {}
    big["w_in"] = (g_win,) + tuple(_adamw(w_in[0], g_win, m_w_in[0], v_w_in[0], 256, "adamw_w_in"))
    big["w_up"] = (g_wup,) + tuple(_adamw(w_up[0], g_wup, m_w_up[0], v_w_up[0], 256, "adamw_w_up"))
    big["w_out"] = _sum_adamw(slots(r_wout), w_out[0], m_w_out[0], v_w_out[0], 128, "adamw_w_out")
    big["w_down"] = _sum_adamw(slots(r_wdown), w_down[0], m_w_down[0], v_w_down[0], 176, "adamw_w_down")
    big["w_attn_branch"] = _sum_adamw(slots(r_wa), w_attn_branch[0], m_w_attn_branch[0], v_w_attn_branch[0], 256,
                                      "adamw_w_attn_branch")
    big["w_conv_branch"] = _sum_adamw(slots(r_wc), w_conv_branch[0], m_w_conv_branch[0], v_w_conv_branch[0], 256,
                                      "adamw_w_conv_branch")

    def small_adam(w, g, m, v, name):
        shp = w.shape
        w2, m2, v2 = (t.reshape(-1, shp[-1]) for t in (w, m, v))
        d, mm, vv = _adamw(w2, g.reshape(w2.shape), m2, v2, w2.shape[0], name)
        return g.reshape(shp), d.reshape(shp), mm.reshape(shp), vv.reshape(shp)

    res = {
        "mix_norm": small_adam(mix_norm, g_mix, m_mix_norm, v_mix_norm, "adamw_mix_norm"),
        "b_in": small_adam(b_in, g_bin, m_b_in, v_b_in, "adamw_b_in"),
        "sinks": small_adam(sinks, g_sinks, m_sinks, v_sinks, "adamw_sinks"),
        "conv_w": small_adam(conv_w, g_cw, m_conv_w, v_conv_w, "adamw_conv_w"),
        "ffn_norm": small_adam(ffn_norm, g_ffn, m_ffn_norm, v_ffn_norm, "adamw_ffn_norm"),
        "ffn_conv_w": small_adam(ffn_conv_w, g_fcw, m_ffn_conv_w, v_ffn_conv_w, "adamw_ffn_conv_w"),
        "final_norm": small_adam(final_norm, g_fn, m_final_norm, v_final_norm, "adamw_final_norm"),
    }
    for name, ref_w in (("w_in", w_in), ("w_up", w_up), ("w_out", w_out), ("w_down", w_down),
                        ("w_attn_branch", w_attn_branch), ("w_conv_branch", w_conv_branch)):
        res[name] = tuple(t.reshape(ref_w.shape) for t in big[name])

    order = ["mix_norm", "w_in", "b_in", "sinks", "conv_w", "w_attn_branch", "w_conv_branch", "w_out",
             "ffn_norm", "w_up", "ffn_conv_w", "w_down", "final_norm"]
    out = [loss, dx.reshape(x.shape)]
    for k in range(4):
        out += [res[name][k] for name in order]
    return tuple(out)
```

```python
import math

import jax
import jax.numpy as jnp
from jax import lax
from jax.experimental import pallas as pl
from jax.experimental.pallas import tpu as pltpu

F32 = jnp.float32
BF16 = jnp.bfloat16
MESH = pl.DeviceIdType.MESH
N_DEV = 8

D_MODEL = 1024
HEAD_DIM = 64
N_HEADS = 8
BLOCK = 128
ATTN_W = 512
KV_W = 128
CONV_W = 512
QKV_W = ATTN_W + 2 * KV_W
CBX_W = 3 * CONV_W
GATE_W = 2 * D_MODEL
IN_W = QKV_W + CBX_W + GATE_W
D_FF = 2816
FF_CHUNK = 1408
NORM_EPS = 1e-5
ATTN_SCALE = HEAD_DIM ** -0.5
NEG = -1e30
HALO = 16

ADAM_LR = 0.001
ADAM_B1 = 0.9
ADAM_B2 = 0.999
ADAM_EPS = 1e-08
ADAM_WD = 0.01
ADAM_STEP = 10

VMEM_LIMIT = 56 * 1024 * 1024
SMALL_ROWS = 32

NT = (((1,), (1,)), ((), ()))
TN = (((0,), (0,)), ((), ()))
ANY = pl.BlockSpec(memory_space=pl.ANY)


def _sig(v):
    return 1.0 / (1.0 + jnp.exp(-v))


def _row_tile(s):
    return 256 if s % 256 == 0 else s


def _shift_down(u, halo, k):
    r = pltpu.roll(u, k, axis=0)
    row = lax.broadcasted_iota(jnp.int32, u.shape, 0)
    for t in range(k):
        r = jnp.where(row == t, halo[HALO - k + t:HALO - k + t + 1, :], r)
    return r


def _shift_up(u, halo, k):
    n = u.shape[0]
    r = pltpu.roll(u, n - k, axis=0)
    row = lax.broadcasted_iota(jnp.int32, u.shape, 0)
    for t in range(k):
        r = jnp.where(row == n - k + t, halo[t:t + 1, :], r)
    return r


def _prev_halo_map(tm):
    return lambda i: (jnp.maximum(i * (tm // HALO) - 1, 0), 0)


def _next_halo_map(tm, s):
    return lambda i: (jnp.minimum((i + 1) * (tm // HALO), s // HALO - 1), 0)


def _full(shape):
    return pl.BlockSpec(shape, lambda *_: (0,) * len(shape))


def _rows(tm, c):
    return pl.BlockSpec((tm, c), lambda i: (i, 0))


def _sds(shape, dtype):
    return jax.ShapeDtypeStruct(shape, dtype)


def _my_place():
    x, y, c = lax.axis_index("x"), lax.axis_index("y"), lax.axis_index("c")
    return x, y, c


class _AllGather:
    def __init__(self, shards):
        self.ins = list(shards)
        n = len(shards)
        self.out_shape = [_sds((N_DEV * s.shape[0], s.shape[1]), s.dtype) for s in shards]
        self.sems = [pltpu.SemaphoreType.DMA((7 * n,)), pltpu.SemaphoreType.DMA((7 * n,)),
                     pltpu.SemaphoreType.DMA((n,))]

    def _parts(self, ins, outs, sems):
        send_sems, recv_sems, local_sems = sems
        x, y, c = _my_place()
        me, sibling = (x, y, c), (x, y, 1 - c)
        chips = [(1 - x, y), (x, 1 - y), (1 - x, 1 - y)]

        def rows(k, dev):
            r = ins[k].shape[0]
            start = pl.multiple_of((4 * dev[0] + 2 * dev[1] + dev[2]) * r, 8)
            return outs[k].at[pl.ds(start, r), :]

        def copy(k, j, block, to, src=None):
            return pltpu.make_async_remote_copy(
                src_ref=rows(k, block) if src is None else src, dst_ref=rows(k, block),
                send_sem=send_sems.at[7 * k + j], recv_sem=recv_sems.at[7 * k + j],
                device_id=to, device_id_type=MESH)

        n = len(ins)
        mine = [pltpu.make_async_copy(ins[k], rows(k, me), local_sems.at[k]) for k in range(n)]
        first = []
        for k in range(n):
            first.append(copy(k, 0, me, sibling, src=ins[k]))
            first += [copy(k, 1 + j, me, (*chip, c), src=ins[k]) for j, chip in enumerate(chips)]
        return me, sibling, chips, copy, mine, first

    def start(self, ins, outs, sems):
        _, _, _, _, mine, first = self._parts(ins, outs, sems)
        for cp in mine + first:
            cp.start()

    def finish(self, ins, outs, sems):
        me, sibling, chips, copy, mine, first = self._parts(ins, outs, sems)
        c = me[2]
        n = len(ins)
        passed = []
        for j, chip in enumerate(chips):
            for k in range(n):
                copy(k, 1 + j, (*chip, c), me).wait_recv()
                fwd = copy(k, 4 + j, (*chip, c), sibling)
                fwd.start()
                passed.append(fwd)
        for k in range(n):
            copy(k, 0, sibling, me).wait_recv()
            for j, chip in enumerate(chips):
                copy(k, 4 + j, (*chip, 1 - c), me).wait_recv()
        for cp in first + passed:
            cp.wait_send()
        for cp in mine:
            cp.wait()


class _ReduceScatter:
    def __init__(self, parts, bcast=()):
        self.parts = [(lo, cnt) for _, lo, cnt in parts]
        self.n_parts = len(parts)
        self.ins = [a for a, _, _ in parts] + list(bcast)
        self.out_shape = [_sds((N_DEV * cnt, a.shape[1]), a.dtype) for a, _, cnt in parts]
        self.out_shape += [_sds((N_DEV * b.shape[0], b.shape[1]), b.dtype) for b in bcast]
        n = len(self.ins)
        self.sems = [pltpu.SemaphoreType.DMA((7 * n,)), pltpu.SemaphoreType.DMA((7 * n,)),
                     pltpu.SemaphoreType.DMA((n,))]

    def _copies(self, ins, outs, sems):
        send_sems, recv_sems, local_sems = sems
        x, y, c = _my_place()
        me_idx = 4 * x + 2 * y + c
        remote, local = [], []
        for k in range(len(ins)):
            cnt = outs[k].shape[0] // N_DEV
            dst = outs[k].at[pl.ds(pl.multiple_of(me_idx * cnt, 8), cnt), :]
            if k < self.n_parts:
                lo, _ = self.parts[k]
                r = ins[k].shape[0] // N_DEV
                src_of = lambda idx: ins[k].at[pl.ds(pl.multiple_of(idx * r + lo, 8), cnt), :]
            else:
                src_of = lambda idx: ins[k]
            local.append(pltpu.make_async_copy(src_of(me_idx), dst, local_sems.at[k]))
            for j in range(1, N_DEV):
                peer = (x ^ (j >> 2), y ^ ((j >> 1) & 1), c ^ (j & 1))
                peer_idx = 4 * peer[0] + 2 * peer[1] + peer[2]
                remote.append(pltpu.make_async_remote_copy(
                    src_ref=src_of(peer_idx), dst_ref=dst,
                    send_sem=send_sems.at[7 * k + j - 1], recv_sem=recv_sems.at[7 * k + j - 1],
                    device_id=peer, device_id_type=MESH))
        return remote, local

    def start(self, ins, outs, sems):
        remote, local = self._copies(ins, outs, sems)
        for cp in local + remote:
            cp.start()

    def finish(self, ins, outs, sems):
        remote, local = self._copies(ins, outs, sems)
        for cp in remote:
            cp.wait_recv()
        for cp in remote:
            cp.wait_send()
        for cp in local:
            cp.wait()


def _pcall(body, name, grid, in_specs, out_specs, out_shape, args, scratch=(), comm=None):
    params = pltpu.CompilerParams(dimension_semantics=("arbitrary",) * len(grid), vmem_limit_bytes=VMEM_LIMIT)
    in_specs, out_specs, out_shape, scratch = list(in_specs), list(out_specs), list(out_shape), list(scratch)
    if comm is None:
        res = pl.pallas_call(body, name=name, grid=grid, in_specs=in_specs, out_specs=out_specs, out_shape=out_shape,
                             scratch_shapes=scratch, compiler_params=params)(*args)
        return list(res), []
    n_in, n_out, n_scr = len(in_specs), len(out_specs), len(scratch)
    ci, co = len(comm.ins), len(comm.out_shape)
    total = math.prod(grid)

    def carried(*refs):
        bounds = [0, n_in, n_in + ci, n_in + ci + n_out, n_in + ci + n_out + co, n_in + ci + n_out + co + n_scr]
        ins, cins, outs, couts, scr = (refs[a:b] for a, b in zip(bounds[:-1], bounds[1:]))
        sems = refs[bounds[-1]:]
        step = pl.program_id(0)
        for d in range(1, len(grid)):
            step = step * grid[d] + pl.program_id(d)

        @pl.when(step == 0)
        def _():
            comm.start(cins, couts, sems)

        body(*ins, *outs, *scr)

        @pl.when(step == total - 1)
        def _():
            comm.finish(cins, couts, sems)

    res = pl.pallas_call(
        carried, name=name, grid=grid, in_specs=in_specs + [ANY] * ci, out_specs=out_specs + [ANY] * co,
        out_shape=out_shape + comm.out_shape, scratch_shapes=scratch + comm.sems, compiler_params=params,
    )(*args, *comm.ins)
    return list(res[:n_out]), list(res[n_out:])


def _exchange_only(comm, name):
    def body(*refs):
        ci, co = len(comm.ins), len(comm.out_shape)
        comm.start(refs[:ci], refs[ci:ci + co], refs[ci + co:])
        comm.finish(refs[:ci], refs[ci:ci + co], refs[ci + co:])

    return pl.pallas_call(body, name=name, out_shape=comm.out_shape, in_specs=[ANY] * len(comm.ins),
                          out_specs=[ANY] * len(comm.out_shape), scratch_shapes=comm.sems)(*comm.ins)


def _norm_inproj(x, g, win_t, b_in, comm):
    s = x.shape[0]
    tm = _row_tile(s)
    widths = (QKV_W, CBX_W, GATE_W)

    def body(x_ref, g_ref, w_ref, b_ref, xn_ref, qkv_ref, cbx_ref, gate_ref):
        xv = x_ref[...]
        r = lax.rsqrt(jnp.mean(xv * xv, axis=-1, keepdims=True) + NORM_EPS)
        xn = (xv * r * g_ref[...]).astype(BF16)
        xn_ref[...] = xn
        off = 0
        for o_ref, w in zip((qkv_ref, cbx_ref, gate_ref), widths):
            acc = lax.dot_general(xn, w_ref[off:off + w, :], NT, preferred_element_type=F32)
            o_ref[...] = (acc + b_ref[:, off:off + w]).astype(BF16)
            off += w

    return _pcall(
        body, "norm_inproj", (s // tm,),
        [_rows(tm, D_MODEL), _full((1, D_MODEL)), _full((IN_W, D_MODEL)), _full((1, IN_W))],
        [_rows(tm, D_MODEL)] + [_rows(tm, w) for w in widths],
        [_sds((s, D_MODEL), BF16)] + [_sds((s, w), BF16) for w in widths],
        (x, g, win_t, b_in), comm=comm)


def _attn_specs():
    prev = lambda n: jnp.maximum(n - 1, 0)
    return [pl.BlockSpec((BLOCK, ATTN_W), lambda n: (n, 0)),
            pl.BlockSpec((BLOCK, KV_W), lambda n: (prev(n), ATTN_W // KV_W)),
            pl.BlockSpec((BLOCK, KV_W), lambda n: (n, ATTN_W // KV_W)),
            pl.BlockSpec((BLOCK, KV_W), lambda n: (prev(n), ATTN_W // KV_W + 1)),
            pl.BlockSpec((BLOCK, KV_W), lambda n: (n, ATTN_W // KV_W + 1))]


def _lower_lanes():
    return lax.broadcasted_iota(jnp.int32, (BLOCK, 128), 1) < HEAD_DIM


def _stack_heads(val, kh):
    lower = _lower_lanes()
    parts = []
    for g in range(4):
        h = kh * 4 + g
        blk = val[:, (h // 2) * 128:(h // 2 + 1) * 128]
        keep = lower if h % 2 == 0 else jnp.logical_not(lower)
        parts.append(jnp.where(keep, blk, jnp.zeros_like(blk)))
    return jnp.concatenate(parts, axis=0)


def _dup_kv(prev_ref, cur_ref, kh):
    t = jnp.concatenate([prev_ref[...], cur_ref[...]], axis=0).astype(F32)
    rolled = pltpu.roll(t, HEAD_DIM, axis=1)
    lower = lax.broadcasted_iota(jnp.int32, t.shape, 1) < HEAD_DIM
    dup = jnp.where(lower, t, rolled) if kh == 0 else jnp.where(lower, rolled, t)
    return dup.astype(BF16)


def _attn_mask(n):
    row = lax.broadcasted_iota(jnp.int32, (4 * BLOCK, 2 * BLOCK), 0)
    kj = lax.broadcasted_iota(jnp.int32, (4 * BLOCK, 2 * BLOCK), 1)
    dist = (row & (BLOCK - 1)) + BLOCK - kj
    band = jnp.logical_and(dist >= 0, dist < BLOCK)
    return jnp.logical_and(band, jnp.logical_or(kj >= BLOCK, n > 0))


def _sink_col(sinks_ref, kh):
    gi = lax.broadcasted_iota(jnp.int32, (4 * BLOCK, 1), 0) // BLOCK
    col = jnp.zeros((4 * BLOCK, 1), F32)
    for g in range(4):
        col = jnp.where(gi == g, sinks_ref[0, kh * 4 + g], col)
    return col


def _attn_fwd(qkv, sinks, comm):
    s = qkv.shape[0]

    def body(sinks_ref, q_ref, kp_ref, kc_ref, vp_ref, vc_ref, o_ref, lse_ref):
        n = pl.program_id(0)
        mask = _attn_mask(n)
        lower = _lower_lanes()
        lane = lax.broadcasted_iota(jnp.int32, (BLOCK, 128), 1)
        qv = q_ref[...]
        lse_out = jnp.zeros((BLOCK, 128), F32)
        for kh in range(2):
            qs = _stack_heads(qv, kh)
            kd, vd = _dup_kv(kp_ref, kc_ref, kh), _dup_kv(vp_ref, vc_ref, kh)
            sc = lax.dot_general(qs, kd, NT, preferred_element_type=F32) * ATTN_SCALE
            sc = jnp.where(mask, sc, NEG)
            sink = _sink_col(sinks_ref, kh)
            m = jnp.maximum(jnp.max(sc, axis=1, keepdims=True), sink)
            p = jnp.exp(sc - m)
            l = jnp.sum(p, axis=1, keepdims=True) + jnp.exp(sink - m)
            o = jnp.dot(p.astype(BF16), vd, preferred_element_type=F32) / l
            lse = m + jnp.log(l)
            for pair in range(2):
                lo = o[(2 * pair) * BLOCK:(2 * pair + 1) * BLOCK]
                hi = o[(2 * pair + 1) * BLOCK:(2 * pair + 2) * BLOCK]
                col = (kh * 2 + pair) * 128
                o_ref[:, col:col + 128] = jnp.where(lower, lo, hi).astype(BF16)
            for g in range(4):
                lse_out = jnp.where(lane == kh * 4 + g, lse[g * BLOCK:(g + 1) * BLOCK], lse_out)
        lse_ref[...] = lse_out

    return _pcall(
        body, "attn_fwd", (s // BLOCK,),
        [pl.BlockSpec(memory_space=pltpu.SMEM)] + _attn_specs(),
        [pl.BlockSpec((BLOCK, ATTN_W), lambda n: (n, 0)), pl.BlockSpec((BLOCK, 128), lambda n: (n, 0))],
        [_sds((s, ATTN_W), BF16), _sds((s, 128), F32)],
        (sinks, qkv, qkv, qkv, qkv, qkv), comm=comm)


def _conv_u(cbx_ref, halo_ref, w_ref, first):
    cb = cbx_ref[:, 0:CONV_W].astype(F32)
    cc = cbx_ref[:, CONV_W:2 * CONV_W].astype(F32)
    cx = cbx_ref[:, 2 * CONV_W:3 * CONV_W].astype(F32)
    u = cc * cx
    uh = halo_ref[:, CONV_W:2 * CONV_W].astype(F32) * halo_ref[:, 2 * CONV_W:3 * CONV_W].astype(F32)
    uh = jnp.where(first, 0.0, uh)
    u1, u2 = _shift_down(u, uh, 1), _shift_down(u, uh, 2)
    cv = w_ref[0:1, :] * u2 + w_ref[1:2, :] * u1 + w_ref[2:3, :] * u
    return cb, cc, cx, u, u1, u2, cv


def _mix_fwd(x, cbx, gates, attn, conv_w, wa, wc, wout, comm):
    s = x.shape[0]
    tm = _row_tile(s)

    def body(x_ref, cbx_ref, halo_ref, gate_ref, attn_ref, cw_ref, wa_ref, wc_ref, wo_ref,
             conv_ref, ap_ref, cp_ref, mg_ref, h1_ref):
        first = pl.program_id(0) == 0
        cb, _, _, _, _, _, cv = _conv_u(cbx_ref, halo_ref, cw_ref, first)
        conv = (cb * cv).astype(BF16)
        conv_ref[...] = conv
        ap = jnp.dot(attn_ref[...], wa_ref[...], preferred_element_type=F32)
        cp = jnp.dot(conv, wc_ref[...], preferred_element_type=F32)
        ap_ref[...] = ap.astype(BF16)
        cp_ref[...] = cp.astype(BF16)
        ga = gate_ref[:, 0:D_MODEL].astype(F32)
        gc = gate_ref[:, D_MODEL:2 * D_MODEL].astype(F32)
        merged = (_sig(ga) * ap + _sig(gc) * cp).astype(BF16)
        mg_ref[...] = merged
        h1_ref[...] = x_ref[...] + jnp.dot(merged, wo_ref[...], preferred_element_type=F32)

    return _pcall(
        body, "mix_fwd", (s // tm,),
        [_rows(tm, D_MODEL), _rows(tm, CBX_W), pl.BlockSpec((HALO, CBX_W), _prev_halo_map(tm)),
         _rows(tm, GATE_W), _rows(tm, ATTN_W), _full((3, CONV_W)), _full((ATTN_W, D_MODEL)),
         _full((CONV_W, D_MODEL)), _full((D_MODEL, D_MODEL))],
        [_rows(tm, CONV_W), _rows(tm, D_MODEL), _rows(tm, D_MODEL), _rows(tm, D_MODEL), _rows(tm, D_MODEL)],
        [_sds((s, CONV_W), BF16), _sds((s, D_MODEL), BF16), _sds((s, D_MODEL), BF16), _sds((s, D_MODEL), BF16),
         _sds((s, D_MODEL), F32)],
        (x, cbx, cbx, gates, attn, conv_w, wa, wc, wout), comm=comm)


def _ffn_up(h1, g, wup_t):
    s = h1.shape[0]
    tm = _row_tile(s)

    def body(h_ref, g_ref, w_ref, hn_ref, up_ref):
        hv = h_ref[...]
        r = lax.rsqrt(jnp.mean(hv * hv, axis=-1, keepdims=True) + NORM_EPS)
        hn = (hv * r * g_ref[...]).astype(BF16)
        hn_ref[...] = hn
        for c in range(2 * D_FF // FF_CHUNK):
            sl = slice(c * FF_CHUNK, (c + 1) * FF_CHUNK)
            up_ref[:, sl] = lax.dot_general(hn, w_ref[sl, :], NT, preferred_element_type=F32).astype(BF16)

    return _pcall(
        body, "ffn_up", (s // tm,),
        [_rows(tm, D_MODEL), _full((1, D_MODEL)), _full((2 * D_FF, D_MODEL))],
        [_rows(tm, D_MODEL), _rows(tm, 2 * D_FF)],
        [_sds((s, D_MODEL), BF16), _sds((s, 2 * D_FF), BF16)],
        (h1, g, wup_t))[0]


def _ffn_conv_cols(up_ref, halo_ref, fcw_ref, first, off):
    u = up_ref[:, off:off + FF_CHUNK].astype(F32)
    uh = jnp.where(first, 0.0, halo_ref[:, off:off + FF_CHUNK].astype(F32))
    w = fcw_ref[:, off:off + FF_CHUNK]
    return w[0:1] * _shift_down(u, uh, 2) + w[1:2] * _shift_down(u, uh, 1) + w[2:3] * u


def _ffn_down_loss(up_pre, fcw, wdown, h1, fnorm, target):
    s = h1.shape[0]
    tm = _row_tile(s)

    def body(up_ref, halo_ref, fcw_ref, wd_ref, h1_ref, fn_ref, t_ref, act_ref, dh2_ref, loss_ref, dfn_ref):
        i = pl.program_id(0)

        @pl.when(i == 0)
        def _():
            loss_ref[...] = jnp.zeros_like(loss_ref)
            dfn_ref[...] = jnp.zeros_like(dfn_ref)

        h2 = h1_ref[...]
        for c in range(D_FF // FF_CHUNK):
            gate = _ffn_conv_cols(up_ref, halo_ref, fcw_ref, i == 0, c * FF_CHUNK)
            val = _ffn_conv_cols(up_ref, halo_ref, fcw_ref, i == 0, D_FF + c * FF_CHUNK)
            act = (gate * _sig(gate) * val).astype(BF16)
            act_ref[:, c * FF_CHUNK:(c + 1) * FF_CHUNK] = act
            h2 = h2 + jnp.dot(act, wd_ref[c * FF_CHUNK:(c + 1) * FF_CHUNK, :], preferred_element_type=F32)
        r = lax.rsqrt(jnp.mean(h2 * h2, axis=-1, keepdims=True) + NORM_EPS)
        yhat = h2 * r
        fn = fn_ref[...]
        diff = yhat * fn - t_ref[...]
        loss_ref[...] += 0.5 * jnp.sum(jnp.sum(diff * diff, axis=1, keepdims=True), axis=0, keepdims=True) / D_MODEL
        dy = diff * (1.0 / D_MODEL)
        dfn_ref[...] += jnp.sum(dy * yhat, axis=0, keepdims=True)
        dyh = dy * fn
        dh2_ref[...] = r * (dyh - yhat * jnp.mean(dyh * yhat, axis=-1, keepdims=True))

    return _pcall(
        body, "ffn_down_loss", (s // tm,),
        [_rows(tm, 2 * D_FF), pl.BlockSpec((HALO, 2 * D_FF), _prev_halo_map(tm)), _full((3, 2 * D_FF)),
         _full((D_FF, D_MODEL)), _rows(tm, D_MODEL), _full((1, D_MODEL)), _rows(tm, D_MODEL)],
        [_rows(tm, D_FF), _rows(tm, D_MODEL), _full((1, 128)), _full((1, D_MODEL))],
        [_sds((s, D_FF), BF16), _sds((s, D_MODEL), F32), _sds((1, 128), F32), _sds((1, D_MODEL), F32)],
        (up_pre, up_pre, fcw, wdown, h1, fnorm, target))[0]


def _ffn_act_bwd(dh2, wdown, up_pre, fcw, comm):
    s = dh2.shape[0]
    tm = _row_tile(s)

    def body(dh_ref, wd_ref, up_ref, halo_ref, fcw_ref, dup_ref):
        first = pl.program_id(0) == 0
        dh = dh_ref[...].astype(BF16)
        for c in range(D_FF // FF_CHUNK):
            sl = slice(c * FF_CHUNK, (c + 1) * FF_CHUNK)
            dact = lax.dot_general(dh, wd_ref[sl, :], NT, preferred_element_type=F32)
            gate = _ffn_conv_cols(up_ref, halo_ref, fcw_ref, first, c * FF_CHUNK)
            val = _ffn_conv_cols(up_ref, halo_ref, fcw_ref, first, D_FF + c * FF_CHUNK)
            sg = _sig(gate)
            dup_ref[:, sl] = (dact * val * (sg * (1.0 + gate * (1.0 - sg)))).astype(BF16)
            dup_ref[:, D_FF + c * FF_CHUNK:D_FF + (c + 1) * FF_CHUNK] = (dact * gate * sg).astype(BF16)

    return _pcall(
        body, "ffn_act_bwd", (s // tm,),
        [_rows(tm, D_MODEL), _full((D_FF, D_MODEL)), _rows(tm, 2 * D_FF),
         pl.BlockSpec((HALO, 2 * D_FF), _prev_halo_map(tm)), _full((3, 2 * D_FF))],
        [_rows(tm, 2 * D_FF)], [_sds((s, 2 * D_FF), BF16)],
        (dh2, wdown, up_pre, up_pre, fcw), comm=comm)


def _conv_bwd(dy, x, w, width, chunk, name):
    s = dy.shape[0]
    tm = _row_tile(s)

    def body(dy_ref, dyn_ref, x_ref, xh_ref, w_ref, dx_ref, dw_ref):
        i = pl.program_id(0)

        @pl.when(i == 0)
        def _():
            dw_ref[...] = jnp.zeros_like(dw_ref)

        last = i == s // tm - 1
        for c in range(width // chunk):
            sl = slice(c * chunk, (c + 1) * chunk)
            d = dy_ref[:, sl].astype(F32)
            dn = jnp.where(last, 0.0, dyn_ref[:, sl].astype(F32))
            xv = x_ref[:, sl].astype(F32)
            xh = jnp.where(i == 0, 0.0, xh_ref[:, sl].astype(F32))
            wv = w_ref[:, sl]
            dx = wv[2:3] * d + wv[1:2] * _shift_up(d, dn, 1) + wv[0:1] * _shift_up(d, dn, 2)
            dx_ref[:, sl] = dx.astype(BF16)
            dw_ref[0:1, sl] += jnp.sum(d * _shift_down(xv, xh, 2), axis=0, keepdims=True)
            dw_ref[1:2, sl] += jnp.sum(d * _shift_down(xv, xh, 1), axis=0, keepdims=True)
            dw_ref[2:3, sl] += jnp.sum(d * xv, axis=0, keepdims=True)

    return _pcall(
        body, name, (s // tm,),
        [_rows(tm, width), pl.BlockSpec((HALO, width), _next_halo_map(tm, s)), _rows(tm, width),
         pl.BlockSpec((HALO, width), _prev_halo_map(tm)), _full((3, width))],
        [_rows(tm, width), _full((3, width))],
        [_sds((s, width), BF16), _sds((3, width), F32)],
        (dy, dy, x, x, w))[0]


def _matmul_tn(a, b, tk, name, ts=512):
    s, ka = a.shape
    n = b.shape[1]
    ts = min(ts, s)
    steps = s // ts

    def body(a_ref, b_ref, o_ref, acc_ref):
        j = pl.program_id(1)

        @pl.when(j == 0)
        def _():
            acc_ref[...] = jnp.zeros_like(acc_ref)

        acc_ref[...] += lax.dot_general(a_ref[...].astype(BF16), b_ref[...].astype(BF16), TN,
                                        preferred_element_type=F32)

        @pl.when(j == steps - 1)
        def _():
            o_ref[...] = acc_ref[...].astype(BF16)

    return _pcall(
        body, name, (ka // tk, steps),
        [pl.BlockSpec((ts, tk), lambda i, j: (j, i)), pl.BlockSpec((ts, n), lambda i, j: (j, 0))],
        [pl.BlockSpec((tk, n), lambda i, j: (i, 0))], [_sds((ka, n), BF16)],
        (a, b), scratch=[pltpu.VMEM((tk, n), F32)])[0][0]


def _norm_bwd_tile(xv, g, dy):
    r = lax.rsqrt(jnp.mean(xv * xv, axis=-1, keepdims=True) + NORM_EPS)
    xhat = xv * r
    dg = jnp.sum(dy * xhat, axis=0, keepdims=True)
    dyh = dy * g
    return r * (dyh - xhat * jnp.mean(dyh * xhat, axis=-1, keepdims=True)), dg


def _ffn_up_bwd(dup_pre, wup_t, h1, g, dh2, comm):
    s = h1.shape[0]
    tm = _row_tile(s)

    def body(du_ref, w_ref, h_ref, g_ref, dh2_ref, dh1_ref, dg_ref):
        @pl.when(pl.program_id(0) == 0)
        def _():
            dg_ref[...] = jnp.zeros_like(dg_ref)

        dhn = jnp.dot(du_ref[...], w_ref[...], preferred_element_type=F32)
        dx, dg = _norm_bwd_tile(h_ref[...], g_ref[...], dhn)
        dg_ref[...] += dg
        dh1_ref[...] = dh2_ref[...] + dx

    return _pcall(
        body, "ffn_up_bwd", (s // tm,),
        [_rows(tm, 2 * D_FF), _full((2 * D_FF, D_MODEL)), _rows(tm, D_MODEL), _full((1, D_MODEL)),
         _rows(tm, D_MODEL)],
        [_rows(tm, D_MODEL), _full((1, D_MODEL))],
        [_sds((s, D_MODEL), F32), _sds((1, D_MODEL), F32)],
        (dup_pre, wup_t, h1, g, dh2), comm=comm)


def _mix_bwd(dh1, wout, gates, ap, cp, wa, wc, cbx, conv_w, comm):
    s = dh1.shape[0]
    tm = _row_tile(s)

    def body(dh_ref, wo_ref, gate_ref, ap_ref, cp_ref, wa_ref, wc_ref, cbx_ref, halo_ref, cw_ref,
             dg_ref, da_ref, dc_ref, dattn_ref, dcb_ref, dcv_ref):
        first = pl.program_id(0) == 0
        dm = lax.dot_general(dh_ref[...].astype(BF16), wo_ref[...], NT, preferred_element_type=F32)
        sa = _sig(gate_ref[:, 0:D_MODEL].astype(F32))
        sc = _sig(gate_ref[:, D_MODEL:2 * D_MODEL].astype(F32))
        da = (dm * sa).astype(BF16)
        dc = (dm * sc).astype(BF16)
        da_ref[...] = da
        dc_ref[...] = dc
        dg_ref[:, 0:D_MODEL] = (dm * ap_ref[...].astype(F32) * sa * (1.0 - sa)).astype(BF16)
        dg_ref[:, D_MODEL:2 * D_MODEL] = (dm * cp_ref[...].astype(F32) * sc * (1.0 - sc)).astype(BF16)
        dattn_ref[...] = lax.dot_general(da, wa_ref[...], NT, preferred_element_type=F32).astype(BF16)
        dconv = lax.dot_general(dc, wc_ref[...], NT, preferred_element_type=F32)
        cb, _, _, _, _, _, cv = _conv_u(cbx_ref, halo_ref, cw_ref, first)
        dcb_ref[...] = (dconv * cv).astype(BF16)
        dcv_ref[...] = (dconv * cb).astype(BF16)

    return _pcall(
        body, "mix_bwd", (s // tm,),
        [_rows(tm, D_MODEL), _full((D_MODEL, D_MODEL)), _rows(tm, GATE_W), _rows(tm, D_MODEL),
         _rows(tm, D_MODEL), _full((ATTN_W, D_MODEL)), _full((CONV_W, D_MODEL)), _rows(tm, CBX_W),
         pl.BlockSpec((HALO, CBX_W), _prev_halo_map(tm)), _full((3, CONV_W))],
        [_rows(tm, GATE_W), _rows(tm, D_MODEL), _rows(tm, D_MODEL), _rows(tm, ATTN_W),
         _rows(tm, CONV_W), _rows(tm, CONV_W)],
        [_sds((s, GATE_W), BF16), _sds((s, D_MODEL), BF16), _sds((s, D_MODEL), BF16), _sds((s, ATTN_W), BF16),
         _sds((s, CONV_W), BF16), _sds((s, CONV_W), BF16)],
        (dh1, wout, gates, ap, cp, wa, wc, cbx, cbx, conv_w), comm=comm)


def _conv_branch_bwd(dcv, cbx, conv_w):
    s = dcv.shape[0]
    tm = _row_tile(s)

    def body(d_ref, dn_ref, cbx_ref, halo_ref, w_ref, dcc_ref, dcx_ref, dw_ref):
        i = pl.program_id(0)

        @pl.when(i == 0)
        def _():
            dw_ref[...] = jnp.zeros_like(dw_ref)

        last = i == s // tm - 1
        _, cc, cx, u, u1, u2, _ = _conv_u(cbx_ref, halo_ref, w_ref, i == 0)
        d = d_ref[...].astype(F32)
        dn = jnp.where(last, 0.0, dn_ref[...].astype(F32))
        du = w_ref[2:3, :] * d + w_ref[1:2, :] * _shift_up(d, dn, 1) + w_ref[0:1, :] * _shift_up(d, dn, 2)
        dcc_ref[...] = (du * cx).astype(BF16)
        dcx_ref[...] = (du * cc).astype(BF16)
        dw_ref[0:1, :] += jnp.sum(d * u2, axis=0, keepdims=True)
        dw_ref[1:2, :] += jnp.sum(d * u1, axis=0, keepdims=True)
        dw_ref[2:3, :] += jnp.sum(d * u, axis=0, keepdims=True)

    return _pcall(
        body, "conv_branch_bwd", (s // tm,),
        [_rows(tm, CONV_W), pl.BlockSpec((HALO, CONV_W), _next_halo_map(tm, s)), _rows(tm, CBX_W),
         pl.BlockSpec((HALO, CBX_W), _prev_halo_map(tm)), _full((3, CONV_W))],
        [_rows(tm, CONV_W), _rows(tm, CONV_W), _full((3, CONV_W))],
        [_sds((s, CONV_W), BF16), _sds((s, CONV_W), BF16), _sds((3, CONV_W), F32)],
        (dcv, dcv, cbx, cbx, conv_w))[0]


def _attn_bwd(qkv, sinks, attn, lse, dattn, comm):
    s = qkv.shape[0]

    def body(sinks_ref, q_ref, kp_ref, kc_ref, vp_ref, vc_ref, o_ref, lse_ref, do_ref,
             dq_ref, dk_ref, dv_ref, ds_ref):
        n = pl.program_id(0)

        @pl.when(n == 0)
        def _():
            dk_ref[...] = jnp.zeros_like(dk_ref)
            dv_ref[...] = jnp.zeros_like(dv_ref)
            ds_ref[...] = jnp.zeros_like(ds_ref)

        mask = _attn_mask(n)
        lower = _lower_lanes()
        lane = lax.broadcasted_iota(jnp.int32, (BLOCK, 128), 1)
        lower2 = lax.broadcasted_iota(jnp.int32, (2 * BLOCK, 128), 1) < HEAD_DIM
        lane1 = lax.broadcasted_iota(jnp.int32, (1, 128), 1)
        qv, ov, dov, lsev = q_ref[...], o_ref[...], do_ref[...], lse_ref[...]
        dk_fold, dv_fold = [], []
        dsink = jnp.zeros((1, 128), F32)
        for kh in range(2):
            qs = _stack_heads(qv, kh)
            dos = _stack_heads(dov, kh)
            os_ = _stack_heads(ov, kh)
            kd, vd = _dup_kv(kp_ref, kc_ref, kh), _dup_kv(vp_ref, vc_ref, kh)
            lse = jnp.concatenate(
                [jnp.sum(jnp.where(lane == kh * 4 + g, lsev, 0.0), axis=1, keepdims=True) for g in range(4)], axis=0)
            sc = lax.dot_general(qs, kd, NT, preferred_element_type=F32) * ATTN_SCALE
            p = jnp.exp(jnp.where(mask, sc, NEG) - lse)
            dp = lax.dot_general(dos, vd, NT, preferred_element_type=F32)
            delta = jnp.sum(dos.astype(F32) * os_.astype(F32), axis=1, keepdims=True)
            dsc = (p * (dp - delta) * ATTN_SCALE).astype(BF16)
            dqs = jnp.dot(dsc, kd, preferred_element_type=F32)
            for pair in range(2):
                lo = dqs[(2 * pair) * BLOCK:(2 * pair + 1) * BLOCK]
                hi = dqs[(2 * pair + 1) * BLOCK:(2 * pair + 2) * BLOCK]
                col = (kh * 2 + pair) * 128
                dq_ref[:, col:col + 128] = jnp.where(lower, lo, hi).astype(BF16)
            dkd = lax.dot_general(dsc, qs, TN, preferred_element_type=F32)
            dvd = lax.dot_general(p.astype(BF16), dos, TN, preferred_element_type=F32)
            dk_fold.append(dkd + pltpu.roll(dkd, HEAD_DIM, axis=1))
            dv_fold.append(dvd + pltpu.roll(dvd, HEAD_DIM, axis=1))
            psink = jnp.exp(_sink_col(sinks_ref, kh) - lse) * delta
            for g in range(4):
                tot = jnp.sum(psink[g * BLOCK:(g + 1) * BLOCK], axis=0, keepdims=True)
                dsink = dsink - jnp.where(lane1 == kh * 4 + g, tot, 0.0)
        dk2 = jnp.where(lower2, dk_fold[0], dk_fold[1])
        dv2 = jnp.where(lower2, dv_fold[0], dv_fold[1])
        ds_ref[...] += dsink
        cur = pl.ds(pl.multiple_of(n * BLOCK, BLOCK), BLOCK)
        dk_ref[cur, :] += dk2[BLOCK:]
        dv_ref[cur, :] += dv2[BLOCK:]

        @pl.when(n > 0)
        def _():
            prev = pl.ds(pl.multiple_of((n - 1) * BLOCK, BLOCK), BLOCK)
            dk_ref[prev, :] += dk2[:BLOCK]
            dv_ref[prev, :] += dv2[:BLOCK]

    blk = lambda w: pl.BlockSpec((BLOCK, w), lambda n: (n, 0))
    return _pcall(
        body, "attn_bwd", (s // BLOCK,),
        [pl.BlockSpec(memory_space=pltpu.SMEM)] + _attn_specs() + [blk(ATTN_W), blk(128), blk(ATTN_W)],
        [blk(ATTN_W), _full((s, KV_W)), _full((s, KV_W)), _full((1, 128))],
        [_sds((s, ATTN_W), BF16), _sds((s, KV_W), F32), _sds((s, KV_W), F32), _sds((1, 128), F32)],
        (sinks, qkv, qkv, qkv, qkv, qkv, attn, lse, dattn), comm=comm)


def _assemble_dproj(dq, dk, dv, dcb, dcc, dcx, dgates):
    s = dq.shape[0]
    tm = _row_tile(s)
    pieces = (ATTN_W, KV_W, KV_W, CONV_W, CONV_W, CONV_W, GATE_W)

    def body(*refs):
        srcs, dp_ref, db_ref = refs[:len(pieces)], refs[-2], refs[-1]

        @pl.when(pl.program_id(0) == 0)
        def _():
            db_ref[...] = jnp.zeros_like(db_ref)

        off = 0
        for ref, w in zip(srcs, pieces):
            v = ref[...].astype(BF16)
            dp_ref[:, off:off + w] = v
            db_ref[:, off:off + w] += jnp.sum(v.astype(F32), axis=0, keepdims=True)
            off += w

    return _pcall(
        body, "assemble_dproj", (s // tm,), [_rows(tm, w) for w in pieces],
        [_rows(tm, IN_W), _full((1, IN_W))], [_sds((s, IN_W), BF16), _sds((1, IN_W), F32)],
        (dq, dk, dv, dcb, dcc, dcx, dgates))[0]


def _inproj_bwd(dproj, win_t, x, g, dh1, comm):
    s = x.shape[0]
    tm = _row_tile(s)

    def body(dp_ref, w_ref, x_ref, g_ref, dh_ref, dx_ref, dg_ref):
        @pl.when(pl.program_id(0) == 0)
        def _():
            dg_ref[...] = jnp.zeros_like(dg_ref)

        dxn = jnp.dot(dp_ref[...], w_ref[...], preferred_element_type=F32)
        dx, dg = _norm_bwd_tile(x_ref[...], g_ref[...], dxn)
        dg_ref[...] += dg
        dx_ref[...] = dh_ref[...] + dx

    return _pcall(
        body, "inproj_bwd", (s // tm,),
        [_rows(tm, IN_W), _full((IN_W, D_MODEL)), _rows(tm, D_MODEL), _full((1, D_MODEL)), _rows(tm, D_MODEL)],
        [_rows(tm, D_MODEL), _full((1, D_MODEL))],
        [_sds((s, D_MODEL), F32), _sds((1, D_MODEL), F32)],
        (dproj, win_t, x, g, dh1), comm=comm)


def _adam_math(w, g, m, v):
    m2 = ADAM_B1 * m + (1.0 - ADAM_B1) * g
    v2 = ADAM_B2 * v + (1.0 - ADAM_B2) * (g * g)
    m_hat = m2 / (1.0 - ADAM_B1 ** ADAM_STEP)
    v_hat = v2 / (1.0 - ADAM_B2 ** ADAM_STEP)
    delta = -ADAM_LR * (m_hat / (jnp.sqrt(v_hat) + ADAM_EPS) + ADAM_WD * w)
    return delta, m2, v2


def _sum_slots(ref):
    tot = ref[0].astype(F32)
    for i in range(1, N_DEV):
        tot = tot + ref[i].astype(F32)
    return tot


def _sum_adamw(parts, w, m, v, tr, name):
    r, c = w.shape

    def body(p_ref, w_ref, m_ref, v_ref, g_ref, d_ref, m2_ref, v2_ref):
        g = _sum_slots(p_ref)
        g_ref[...] = g
        d_ref[...], m2_ref[...], v2_ref[...] = _adam_math(w_ref[...], g, m_ref[...], v_ref[...])

    spec = pl.BlockSpec((tr, c), lambda i: (i, 0))
    return _pcall(body, name, (r // tr,), [pl.BlockSpec((N_DEV, tr, c), lambda i: (0, i, 0)), spec, spec, spec],
                  [spec] * 4, [_sds((r, c), F32)] * 4, (parts, w, m, v))[0]


def _sum_halves(parts_a, parts_b, tr, name):
    _, r, c = parts_a.shape
    nt = r // tr

    def body(a_ref, b_ref, g_ref):
        i = pl.program_id(0)

        @pl.when(i < nt)
        def _():
            g_ref[...] = _sum_slots(a_ref)

        @pl.when(i >= nt)
        def _():
            g_ref[...] = _sum_slots(b_ref)

    return _pcall(
        body, name, (2 * nt,),
        [pl.BlockSpec((N_DEV, tr, c), lambda i: (0, jnp.minimum(i, nt - 1), 0)),
         pl.BlockSpec((N_DEV, tr, c), lambda i: (0, jnp.maximum(i - nt, 0), 0))],
        [pl.BlockSpec((tr, c), lambda i: (i, 0))], [_sds((2 * r, c), F32)], (parts_a, parts_b))[0][0]


def _sum_only(parts, tr, name):
    _, r, c = parts.shape

    def body(p_ref, g_ref):
        g_ref[...] = _sum_slots(p_ref)

    return _pcall(body, name, (r // tr,), [pl.BlockSpec((N_DEV, tr, c), lambda i: (0, i, 0))],
                  [pl.BlockSpec((tr, c), lambda i: (i, 0))], [_sds((r, c), F32)], (parts,))[0][0]


def _adamw(w, g, m, v, tr, name):
    r, c = w.shape

    def body(w_ref, g_ref, m_ref, v_ref, d_ref, m2_ref, v2_ref):
        d_ref[...], m2_ref[...], v2_ref[...] = _adam_math(w_ref[...], g_ref[...], m_ref[...], v_ref[...])

    spec = pl.BlockSpec((tr, c), lambda i: (i, 0))
    return _pcall(body, name, (r // tr,), [spec] * 4, [spec] * 3, [_sds((r, c), F32)] * 3, (w, g, m, v))[0]


def _pad_cols(a, c):
    return jnp.pad(a, ((0, 0), (0, c - a.shape[1])))


def _to_col_slabs(g):
    r = g.shape[0]
    return jnp.transpose(g.reshape(r, N_DEV, 128), (1, 0, 2)).reshape(N_DEV * r, 128)


def _from_col_slabs(t):
    r = t.shape[0] // N_DEV
    return jnp.transpose(t.reshape(N_DEV, r, 128), (1, 0, 2)).reshape(r, N_DEV * 128)


def _slots(t):
    return t.reshape(N_DEV, t.shape[0] // N_DEV, t.shape[1])


def kernel(x, mix_norm, w_in, b_in, sinks, conv_w, w_attn_branch, w_conv_branch, w_out, ffn_norm, w_up, ffn_conv_w, w_down, final_norm, loss_target, m_mix_norm, m_w_in, m_b_in, m_sinks, m_conv_w, m_w_attn_branch, m_w_conv_branch, m_w_out, m_ffn_norm, m_w_up, m_ffn_conv_w, m_w_down, m_final_norm, v_mix_norm, v_w_in, v_b_in, v_sinks, v_conv_w, v_w_attn_branch, v_w_conv_branch, v_w_out, v_ffn_norm, v_w_up, v_ffn_conv_w, v_w_down, v_final_norm):
    xs, tgt = x[0], loss_target[0]
    me = 4 * lax.axis_index("x") + 2 * lax.axis_index("y") + lax.axis_index("c")
    in_rows, up_rows = IN_W // N_DEV, 2 * D_FF // N_DEV

    conv_sh = jnp.concatenate([_pad_cols(ffn_conv_w[0], 768), _pad_cols(conv_w[0], 768),
                               jnp.zeros((2, 768), F32)], axis=0)
    win_sh, wup_sh = w_in[0].T.astype(BF16), w_up[0].T.astype(BF16)
    wout_sh, wdown_sh = w_out[0].astype(BF16), w_down[0].astype(BF16)
    wa_sh, wc_sh = w_attn_branch[0].astype(BF16), w_conv_branch[0].astype(BF16)

    (win_t,) = _exchange_only(_AllGather([win_sh]), "gather_w_in")
    (xn, qkv, cbx, gates), (wup_t,) = _norm_inproj(xs, mix_norm, win_t, b_in, _AllGather([wup_sh]))
    (attn, lse), (wa_s, wc_s, wout, conv_g) = _attn_fwd(qkv, sinks, _AllGather([wa_sh, wc_sh, wout_sh, conv_sh]))
    wa, wc = _from_col_slabs(wa_s), _from_col_slabs(wc_s)
    conv_g = conv_g.reshape(N_DEV, 8, 768)
    fcw = jnp.transpose(conv_g[:, 0:3, :up_rows], (1, 0, 2)).reshape(3, 2 * D_FF)
    cw = jnp.transpose(conv_g[:, 3:6, :CONV_W // N_DEV], (1, 0, 2)).reshape(3, CONV_W)
    (conv, ap, cp, merged, h1), (wdown,) = _mix_fwd(xs, cbx, gates, attn, cw, wa, wc, wout, _AllGather([wdown_sh]))
    hn, up_pre = _ffn_up(h1, ffn_norm, wup_t)
    act, dh2, loss_p, dfn_p = _ffn_down_loss(up_pre, fcw, wdown, h1, final_norm.reshape(1, D_MODEL), tgt)

    g_wdown = _matmul_tn(act, dh2, FF_CHUNK, "grad_w_down")
    (dup,), (r_wdown,) = _ffn_act_bwd(dh2, wdown, up_pre, fcw, _ReduceScatter([(g_wdown, 0, D_FF // N_DEV)]))
    dup_pre, dfcw_p = _conv_bwd(dup, up_pre, fcw, 2 * D_FF, FF_CHUNK, "ffn_conv_bwd")
    g_wup_t = _matmul_tn(dup_pre, hn, FF_CHUNK, "grad_w_up")
    (dh1, dffn_p), (r_wup_a,) = _ffn_up_bwd(dup_pre, wup_t, h1, ffn_norm, dh2,
                                            _ReduceScatter([(g_wup_t, 0, up_rows // 2)]))
    g_wout = _matmul_tn(merged, dh1, D_MODEL, "grad_w_out")
    (dgates, da, dc, dattn, dcb, dcv), (r_wup_b,) = _mix_bwd(
        dh1, wout, gates, ap, cp, wa, wc, cbx, cw, _ReduceScatter([(g_wup_t, up_rows // 2, up_rows // 2)]))
    g_wa = _to_col_slabs(_matmul_tn(attn, da, ATTN_W, "grad_w_attn_branch"))
    g_wc = _to_col_slabs(_matmul_tn(conv, dc, CONV_W, "grad_w_conv_branch"))
    dcc, dcx, dcw_p = _conv_branch_bwd(dcv, cbx, cw)
    (dq, dk, dv, dsink_p), (r_wout, r_wa, r_wc) = _attn_bwd(
        qkv, sinks, attn, lse, dattn,
        _ReduceScatter([(g_wout, 0, D_MODEL // N_DEV), (g_wa, 0, ATTN_W), (g_wc, 0, CONV_W)]))
    dproj, dbin_p = _assemble_dproj(dq, dk, dv, dcb, dcc, dcx, dgates)
    g_win_t = _matmul_tn(dproj, xn, IN_W // 2, "grad_w_in")
    (dx, dmix_p), (r_win_a,) = _inproj_bwd(dproj, win_t, xs, mix_norm, dh1,
                                           _ReduceScatter([(g_win_t, 0, in_rows // 2)]))

    row = lambda a: _pad_cols(a.reshape(1, -1), D_MODEL)
    small = jnp.concatenate(
        [dmix_p, dffn_p, dfn_p, row(dsink_p[:, :N_HEADS]), row(loss_p[:, :1]),
         _pad_cols(dbin_p, 5 * D_MODEL).reshape(5, D_MODEL), _pad_cols(dcw_p, D_MODEL),
         _pad_cols(dfcw_p, 6 * D_MODEL).reshape(18, D_MODEL), jnp.zeros((1, D_MODEL), F32)], axis=0)
    r_win_b, r_small = _exchange_only(_ReduceScatter([(g_win_t, in_rows // 2, in_rows // 2)], [small]),
                                      "scatter_w_in_small")

    small_t = _sum_only(_slots(r_small), SMALL_ROWS, "sum_small")
    g_mix, g_ffn, g_fn = small_t[0:1], small_t[1:2], small_t[2:3]
    g_sinks, loss = small_t[3:4, :N_HEADS], small_t[4, 0]
    g_bin = small_t[5:10].reshape(1, 5 * D_MODEL)[:, :IN_W]
    g_cw_full = small_t[10:13, :CONV_W]
    g_fcw_full = small_t[13:31].reshape(3, 6 * D_MODEL)[:, :2 * D_FF]
    g_cw = lax.dynamic_slice_in_dim(g_cw_full, me * (CONV_W // N_DEV), CONV_W // N_DEV, axis=1)
    g_fcw = lax.dynamic_slice_in_dim(g_fcw_full, me * up_rows, up_rows, axis=1)

    g_win = _sum_halves(_slots(r_win_a), _slots(r_win_b), in_rows // 2, "sum_w_in").T
    g_wup = _sum_halves(_slots(r_wup_a), _slots(r_wup_b), up_rows // 2, "sum_w_up").T
    big = {}
    big["w_in"] = (g_win,) + tuple(_adamw(w_in[0], g_win, m_w_in[0], v_w_in[0], 256, "adamw_w_in"))
    big["w_up"] = (g_wup,) + tuple(_adamw(w_up[0], g_wup, m_w_up[0], v_w_up[0], 256, "adamw_w_up"))
    big["w_out"] = _sum_adamw(_slots(r_wout), w_out[0], m_w_out[0], v_w_out[0], 128, "adamw_w_out")
    big["w_down"] = _sum_adamw(_slots(r_wdown), w_down[0], m_w_down[0], v_w_down[0], 176, "adamw_w_down")
    big["w_attn_branch"] = _sum_adamw(_slots(r_wa), w_attn_branch[0], m_w_attn_branch[0], v_w_attn_branch[0], 256,
                                      "adamw_w_attn_branch")
    big["w_conv_branch"] = _sum_adamw(_slots(r_wc), w_conv_branch[0], m_w_conv_branch[0], v_w_conv_branch[0], 256,
                                      "adamw_w_conv_branch")

    def small_adam(w, g, m, v, name):
        shp = w.shape
        w2, m2, v2 = (t.reshape(-1, shp[-1]) for t in (w, m, v))
        d, mm, vv = _adamw(w2, g.reshape(w2.shape), m2, v2, w2.shape[0], name)
        return g.reshape(shp), d.reshape(shp), mm.reshape(shp), vv.reshape(shp)

    res = {
        "mix_norm": small_adam(mix_norm, g_mix, m_mix_norm, v_mix_norm, "adamw_mix_norm"),
        "b_in": small_adam(b_in, g_bin, m_b_in, v_b_in, "adamw_b_in"),
        "sinks": small_adam(sinks, g_sinks, m_sinks, v_sinks, "adamw_sinks"),
        "conv_w": small_adam(conv_w, g_cw, m_conv_w, v_conv_w, "adamw_conv_w"),
        "ffn_norm": small_adam(ffn_norm, g_ffn, m_ffn_norm, v_ffn_norm, "adamw_ffn_norm"),
        "ffn_conv_w": small_adam(ffn_conv_w, g_fcw, m_ffn_conv_w, v_ffn_conv_w, "adamw_ffn_conv_w"),
        "final_norm": small_adam(final_norm, g_fn, m_final_norm, v_final_norm, "adamw_final_norm"),
    }
    for name, ref_w in (("w_in", w_in), ("w_up", w_up), ("w_out", w_out), ("w_down", w_down),
                        ("w_attn_branch", w_attn_branch), ("w_conv_branch", w_conv_branch)):
        res[name] = tuple(t.reshape(ref_w.shape) for t in big[name])

    order = ["mix_norm", "w_in", "b_in", "sinks", "conv_w", "w_attn_branch", "w_conv_branch", "w_out",
             "ffn_norm", "w_up", "ffn_conv_w", "w_down", "final_norm"]
    out = [loss, dx.reshape(x.shape)]
    for k in range(4):
        out += [res[name][k] for name in order]
    return tuple(out)
```

```python
import math

import jax
import jax.numpy as jnp
from jax import lax
from jax.experimental import pallas as pl
from jax.experimental.pallas import tpu as pltpu

F32 = jnp.float32
BF16 = jnp.bfloat16
MESH = pl.DeviceIdType.MESH
N_DEV = 8

D_MODEL = 1024
HEAD_DIM = 64
N_HEADS = 8
BLOCK = 128
ATTN_W = 512
KV_W = 128
CONV_W = 512
QKV_W = ATTN_W + 2 * KV_W
CBX_W = 3 * CONV_W
GATE_W = 2 * D_MODEL
IN_W = QKV_W + CBX_W + GATE_W
D_FF = 2816
FF_CHUNK = 1408
NORM_EPS = 1e-5
ATTN_SCALE = HEAD_DIM ** -0.5
NEG = -1e30
HALO = 16

ADAM_LR = 0.001
ADAM_B1 = 0.9
ADAM_B2 = 0.999
ADAM_EPS = 1e-08
ADAM_WD = 0.01
ADAM_STEP = 10

VMEM_LIMIT = 56 * 1024 * 1024
SMALL_ROWS = 32

NT = (((1,), (1,)), ((), ()))
TN = (((0,), (0,)), ((), ()))
ANY = pl.BlockSpec(memory_space=pl.ANY)


def _sig(v):
    return 1.0 / (1.0 + jnp.exp(-v))


def _row_tile(s, pref=256):
    return pref if s % pref == 0 else s


def _shifts_down(u, halo, ks):
    ext = jnp.concatenate([halo, u], axis=0)
    return [pltpu.roll(ext, k, axis=0)[HALO:, :] for k in ks]


def _shifts_up(u, halo, ks):
    n = u.shape[0]
    ext = jnp.concatenate([u, halo], axis=0)
    return [pltpu.roll(ext, n + HALO - k, axis=0)[:n, :] for k in ks]


def _prev_halo_map(tm):
    return lambda i: (jnp.maximum(i * (tm // HALO) - 1, 0), 0)


def _next_halo_map(tm, s):
    return lambda i: (jnp.minimum((i + 1) * (tm // HALO), s // HALO - 1), 0)


def _full(shape):
    return pl.BlockSpec(shape, lambda *_: (0,) * len(shape))


def _resident(shape):
    return pl.BlockSpec(shape, lambda *_: (0,) * len(shape), pipeline_mode=pl.Buffered(1))


def _rows(tm, c):
    return pl.BlockSpec((tm, c), lambda i: (i, 0))


def _sds(shape, dtype):
    return jax.ShapeDtypeStruct(shape, dtype)


def _my_place():
    x, y, c = lax.axis_index("x"), lax.axis_index("y"), lax.axis_index("c")
    return x, y, c


class _AllGather:
    def __init__(self, shards):
        self.ins = list(shards)
        n = len(shards)
        self.out_shape = [_sds((N_DEV * s.shape[0], s.shape[1]), s.dtype) for s in shards]
        self.sems = [pltpu.SemaphoreType.DMA((7 * n,)), pltpu.SemaphoreType.DMA((7 * n,)),
                     pltpu.SemaphoreType.DMA((n,))]

    def _parts(self, ins, outs, sems):
        send_sems, recv_sems, local_sems = sems
        x, y, c = _my_place()
        me, sibling = (x, y, c), (x, y, 1 - c)
        chips = [(1 - x, y), (x, 1 - y), (1 - x, 1 - y)]

        def rows(k, dev):
            r = ins[k].shape[0]
            start = pl.multiple_of((4 * dev[0] + 2 * dev[1] + dev[2]) * r, 8)
            return outs[k].at[pl.ds(start, r), :]

        def copy(k, j, block, to, src=None):
            return pltpu.make_async_remote_copy(
                src_ref=rows(k, block) if src is None else src, dst_ref=rows(k, block),
                send_sem=send_sems.at[7 * k + j], recv_sem=recv_sems.at[7 * k + j],
                device_id=to, device_id_type=MESH)

        n = len(ins)
        mine = [pltpu.make_async_copy(ins[k], rows(k, me), local_sems.at[k]) for k in range(n)]
        first = []
        for k in range(n):
            first.append(copy(k, 0, me, sibling, src=ins[k]))
            first += [copy(k, 1 + j, me, (*chip, c), src=ins[k]) for j, chip in enumerate(chips)]
        return me, sibling, chips, copy, mine, first

    def start(self, ins, outs, sems):
        _, _, _, _, mine, first = self._parts(ins, outs, sems)
        for cp in mine + first:
            cp.start()

    def finish(self, ins, outs, sems):
        me, sibling, chips, copy, mine, first = self._parts(ins, outs, sems)
        c = me[2]
        n = len(ins)
        passed = []
        for j, chip in enumerate(chips):
            for k in range(n):
                copy(k, 1 + j, (*chip, c), me).wait_recv()
                fwd = copy(k, 4 + j, (*chip, c), sibling)
                fwd.start()
                passed.append(fwd)
        for k in range(n):
            copy(k, 0, sibling, me).wait_recv()
            for j, chip in enumerate(chips):
                copy(k, 4 + j, (*chip, 1 - c), me).wait_recv()
        for cp in first + passed:
            cp.wait_send()
        for cp in mine:
            cp.wait()


class _ReduceScatter:
    def __init__(self, parts, bcast=()):
        self.parts = [(lo, cnt) for _, lo, cnt in parts]
        self.n_parts = len(parts)
        self.ins = [a for a, _, _ in parts] + list(bcast)
        self.out_shape = [_sds((N_DEV * cnt, a.shape[1]), a.dtype) for a, _, cnt in parts]
        self.out_shape += [_sds((N_DEV * b.shape[0], b.shape[1]), b.dtype) for b in bcast]
        n = len(self.ins)
        self.sems = [pltpu.SemaphoreType.DMA((7 * n,)), pltpu.SemaphoreType.DMA((7 * n,)),
                     pltpu.SemaphoreType.DMA((n,))]

    def _copies(self, ins, outs, sems):
        send_sems, recv_sems, local_sems = sems
        x, y, c = _my_place()
        me_idx = 4 * x + 2 * y + c
        remote, local = [], []
        for k in range(len(ins)):
            cnt = outs[k].shape[0] // N_DEV
            dst = outs[k].at[pl.ds(pl.multiple_of(me_idx * cnt, 8), cnt), :]
            if k < self.n_parts:
                lo, _ = self.parts[k]
                r = ins[k].shape[0] // N_DEV
                src_of = lambda idx: ins[k].at[pl.ds(pl.multiple_of(idx * r + lo, 8), cnt), :]
            else:
                src_of = lambda idx: ins[k]
            local.append(pltpu.make_async_copy(src_of(me_idx), dst, local_sems.at[k]))
            for j in range(1, N_DEV):
                peer = (x ^ (j >> 2), y ^ ((j >> 1) & 1), c ^ (j & 1))
                peer_idx = 4 * peer[0] + 2 * peer[1] + peer[2]
                remote.append(pltpu.make_async_remote_copy(
                    src_ref=src_of(peer_idx), dst_ref=dst,
                    send_sem=send_sems.at[7 * k + j - 1], recv_sem=recv_sems.at[7 * k + j - 1],
                    device_id=peer, device_id_type=MESH))
        return remote, local

    def start(self, ins, outs, sems):
        remote, local = self._copies(ins, outs, sems)
        for cp in local + remote:
            cp.start()

    def finish(self, ins, outs, sems):
        remote, local = self._copies(ins, outs, sems)
        for cp in remote:
            cp.wait_recv()
        for cp in remote:
            cp.wait_send()
        for cp in local:
            cp.wait()


def _pcall(body, name, grid, in_specs, out_specs, out_shape, args, scratch=(), comm=None):
    params = pltpu.CompilerParams(dimension_semantics=("arbitrary",) * len(grid), vmem_limit_bytes=VMEM_LIMIT)
    in_specs, out_specs, out_shape, scratch = list(in_specs), list(out_specs), list(out_shape), list(scratch)
    if comm is None:
        res = pl.pallas_call(body, name=name, grid=grid, in_specs=in_specs, out_specs=out_specs, out_shape=out_shape,
                             scratch_shapes=scratch, compiler_params=params)(*args)
        return list(res), []
    n_in, n_out, n_scr = len(in_specs), len(out_specs), len(scratch)
    ci, co = len(comm.ins), len(comm.out_shape)
    total = math.prod(grid)

    def carried(*refs):
        bounds = [0, n_in, n_in + ci, n_in + ci + n_out, n_in + ci + n_out + co, n_in + ci + n_out + co + n_scr]
        ins, cins, outs, couts, scr = (refs[a:b] for a, b in zip(bounds[:-1], bounds[1:]))
        sems = refs[bounds[-1]:]
        step = pl.program_id(0)
        for d in range(1, len(grid)):
            step = step * grid[d] + pl.program_id(d)

        @pl.when(step == 0)
        def _():
            comm.start(cins, couts, sems)

        body(*ins, *outs, *scr)

        @pl.when(step == total - 1)
        def _():
            comm.finish(cins, couts, sems)

    res = pl.pallas_call(
        carried, name=name, grid=grid, in_specs=in_specs + [ANY] * ci, out_specs=out_specs + [ANY] * co,
        out_shape=out_shape + comm.out_shape, scratch_shapes=scratch + comm.sems, compiler_params=params,
    )(*args, *comm.ins)
    return list(res[:n_out]), list(res[n_out:])


def _exchange_only(comm, name):
    def body(*refs):
        ci, co = len(comm.ins), len(comm.out_shape)
        comm.start(refs[:ci], refs[ci:ci + co], refs[ci + co:])
        comm.finish(refs[:ci], refs[ci:ci + co], refs[ci + co:])

    return pl.pallas_call(body, name=name, out_shape=comm.out_shape, in_specs=[ANY] * len(comm.ins),
                          out_specs=[ANY] * len(comm.out_shape), scratch_shapes=comm.sems)(*comm.ins)


def _norm_inproj(x, g, win_t, b_in, comm):
    s = x.shape[0]
    tm = _row_tile(s, 512)
    widths = (QKV_W, CBX_W, GATE_W)

    def body(x_ref, g_ref, w_ref, b_ref, xn_ref, qkv_ref, cbx_ref, gate_ref):
        xv = x_ref[...]
        r = lax.rsqrt(jnp.mean(xv * xv, axis=-1, keepdims=True) + NORM_EPS)
        xn = (xv * r * g_ref[...]).astype(BF16)
        xn_ref[...] = xn
        off = 0
        for o_ref, w in zip((qkv_ref, cbx_ref, gate_ref), widths):
            acc = lax.dot_general(xn, w_ref[off:off + w, :], NT, preferred_element_type=F32)
            o_ref[...] = (acc + b_ref[:, off:off + w]).astype(BF16)
            off += w

    return _pcall(
        body, "norm_inproj", (s // tm,),
        [_rows(tm, D_MODEL), _full((1, D_MODEL)), _resident((IN_W, D_MODEL)), _full((1, IN_W))],
        [_rows(tm, D_MODEL)] + [_rows(tm, w) for w in widths],
        [_sds((s, D_MODEL), BF16)] + [_sds((s, w), BF16) for w in widths],
        (x, g, win_t, b_in), comm=comm)


def _attn_specs():
    prev = lambda n: jnp.maximum(n - 1, 0)
    return [pl.BlockSpec((BLOCK, ATTN_W), lambda n: (n, 0)),
            pl.BlockSpec((BLOCK, KV_W), lambda n: (prev(n), ATTN_W // KV_W)),
            pl.BlockSpec((BLOCK, KV_W), lambda n: (n, ATTN_W // KV_W)),
            pl.BlockSpec((BLOCK, KV_W), lambda n: (prev(n), ATTN_W // KV_W + 1)),
            pl.BlockSpec((BLOCK, KV_W), lambda n: (n, ATTN_W // KV_W + 1))]


def _lower_lanes():
    return lax.broadcasted_iota(jnp.int32, (BLOCK, 128), 1) < HEAD_DIM


def _stack_heads(val, kh):
    lower = _lower_lanes()
    parts = []
    for g in range(4):
        h = kh * 4 + g
        blk = val[:, (h // 2) * 128:(h // 2 + 1) * 128]
        keep = lower if h % 2 == 0 else jnp.logical_not(lower)
        parts.append(jnp.where(keep, blk, jnp.zeros_like(blk)))
    return jnp.concatenate(parts, axis=0)


def _dup_kv(prev_ref, cur_ref, kh):
    t = jnp.concatenate([prev_ref[...], cur_ref[...]], axis=0).astype(F32)
    rolled = pltpu.roll(t, HEAD_DIM, axis=1)
    lower = lax.broadcasted_iota(jnp.int32, t.shape, 1) < HEAD_DIM
    dup = jnp.where(lower, t, rolled) if kh == 0 else jnp.where(lower, rolled, t)
    return dup.astype(BF16)


def _attn_mask(n):
    row = lax.broadcasted_iota(jnp.int32, (4 * BLOCK, 2 * BLOCK), 0)
    kj = lax.broadcasted_iota(jnp.int32, (4 * BLOCK, 2 * BLOCK), 1)
    dist = (row & (BLOCK - 1)) + BLOCK - kj
    band = jnp.logical_and(dist >= 0, dist < BLOCK)
    return jnp.logical_and(band, jnp.logical_or(kj >= BLOCK, n > 0))


def _sink_col(sinks_ref, kh):
    gi = lax.broadcasted_iota(jnp.int32, (4 * BLOCK, 1), 0) // BLOCK
    col = jnp.zeros((4 * BLOCK, 1), F32)
    for g in range(4):
        col = jnp.where(gi == g, sinks_ref[0, kh * 4 + g], col)
    return col


def _attn_fwd(qkv, sinks, comm):
    s = qkv.shape[0]

    def body(sinks_ref, q_ref, kp_ref, kc_ref, vp_ref, vc_ref, o_ref, lse_ref):
        n = pl.program_id(0)
        mask = _attn_mask(n)
        lower = _lower_lanes()
        lane = lax.broadcasted_iota(jnp.int32, (BLOCK, 128), 1)
        qv = q_ref[...]
        lse_out = jnp.zeros((BLOCK, 128), F32)
        for kh in range(2):
            qs = _stack_heads(qv, kh)
            kd, vd = _dup_kv(kp_ref, kc_ref, kh), _dup_kv(vp_ref, vc_ref, kh)
            sc = lax.dot_general(qs, kd, NT, preferred_element_type=F32) * ATTN_SCALE
            sc = jnp.where(mask, sc, NEG)
            sink = _sink_col(sinks_ref, kh)
            m = jnp.maximum(jnp.max(sc, axis=1, keepdims=True), sink)
            p = jnp.exp(sc - m)
            l = jnp.sum(p, axis=1, keepdims=True) + jnp.exp(sink - m)
            o = jnp.dot(p.astype(BF16), vd, preferred_element_type=F32) / l
            lse = m + jnp.log(l)
            for pair in range(2):
                lo = o[(2 * pair) * BLOCK:(2 * pair + 1) * BLOCK]
                hi = o[(2 * pair + 1) * BLOCK:(2 * pair + 2) * BLOCK]
                col = (kh * 2 + pair) * 128
                o_ref[:, col:col + 128] = jnp.where(lower, lo, hi).astype(BF16)
            for g in range(4):
                lse_out = jnp.where(lane == kh * 4 + g, lse[g * BLOCK:(g + 1) * BLOCK], lse_out)
        lse_ref[...] = lse_out

    return _pcall(
        body, "attn_fwd", (s // BLOCK,),
        [pl.BlockSpec(memory_space=pltpu.SMEM)] + _attn_specs(),
        [pl.BlockSpec((BLOCK, ATTN_W), lambda n: (n, 0)), pl.BlockSpec((BLOCK, 128), lambda n: (n, 0))],
        [_sds((s, ATTN_W), BF16), _sds((s, 128), F32)],
        (sinks, qkv, qkv, qkv, qkv, qkv), comm=comm)


def _conv_u(cbx_ref, halo_ref, w_ref, first):
    cb = cbx_ref[:, 0:CONV_W].astype(F32)
    cc = cbx_ref[:, CONV_W:2 * CONV_W].astype(F32)
    cx = cbx_ref[:, 2 * CONV_W:3 * CONV_W].astype(F32)
    u = cc * cx
    uh = halo_ref[:, CONV_W:2 * CONV_W].astype(F32) * halo_ref[:, 2 * CONV_W:3 * CONV_W].astype(F32)
    uh = jnp.where(first, 0.0, uh)
    u1, u2 = _shifts_down(u, uh, (1, 2))
    cv = w_ref[0:1, :] * u2 + w_ref[1:2, :] * u1 + w_ref[2:3, :] * u
    return cb, cc, cx, u, cv


def _mix_fwd(x, cbx, gates, attn, conv_w, wa, wc, wout, comm):
    s = x.shape[0]
    tm = _row_tile(s)

    def body(x_ref, cbx_ref, halo_ref, gate_ref, attn_ref, cw_ref, wa_ref, wc_ref, wo_ref,
             conv_ref, ap_ref, cp_ref, mg_ref, h1_ref):
        first = pl.program_id(0) == 0
        cb, _, _, _, cv = _conv_u(cbx_ref, halo_ref, cw_ref, first)
        conv = (cb * cv).astype(BF16)
        conv_ref[...] = conv
        ap = jnp.dot(attn_ref[...], wa_ref[...], preferred_element_type=F32)
        cp = jnp.dot(conv, wc_ref[...], preferred_element_type=F32)
        ap_ref[...] = ap.astype(BF16)
        cp_ref[...] = cp.astype(BF16)
        ga = gate_ref[:, 0:D_MODEL].astype(F32)
        gc = gate_ref[:, D_MODEL:2 * D_MODEL].astype(F32)
        merged = (_sig(ga) * ap + _sig(gc) * cp).astype(BF16)
        mg_ref[...] = merged
        h1_ref[...] = x_ref[...] + jnp.dot(merged, wo_ref[...], preferred_element_type=F32)

    return _pcall(
        body, "mix_fwd", (s // tm,),
        [_rows(tm, D_MODEL), _rows(tm, CBX_W), pl.BlockSpec((HALO, CBX_W), _prev_halo_map(tm)),
         _rows(tm, GATE_W), _rows(tm, ATTN_W), _full((3, CONV_W)), _full((ATTN_W, D_MODEL)),
         _full((CONV_W, D_MODEL)), _full((D_MODEL, D_MODEL))],
        [_rows(tm, CONV_W), _rows(tm, D_MODEL), _rows(tm, D_MODEL), _rows(tm, D_MODEL), _rows(tm, D_MODEL)],
        [_sds((s, CONV_W), BF16), _sds((s, D_MODEL), BF16), _sds((s, D_MODEL), BF16), _sds((s, D_MODEL), BF16),
         _sds((s, D_MODEL), F32)],
        (x, cbx, cbx, gates, attn, conv_w, wa, wc, wout), comm=comm)


def _ffn_up(h1, g, wup_t):
    s = h1.shape[0]
    tm = _row_tile(s, 512)

    def body(h_ref, g_ref, w_ref, hn_ref, up_ref):
        hv = h_ref[...]
        r = lax.rsqrt(jnp.mean(hv * hv, axis=-1, keepdims=True) + NORM_EPS)
        hn = (hv * r * g_ref[...]).astype(BF16)
        hn_ref[...] = hn
        for c in range(2 * D_FF // FF_CHUNK):
            sl = slice(c * FF_CHUNK, (c + 1) * FF_CHUNK)
            up_ref[:, sl] = lax.dot_general(hn, w_ref[sl, :], NT, preferred_element_type=F32).astype(BF16)

    return _pcall(
        body, "ffn_up", (s // tm,),
        [_rows(tm, D_MODEL), _full((1, D_MODEL)), _resident((2 * D_FF, D_MODEL))],
        [_rows(tm, D_MODEL), _rows(tm, 2 * D_FF)],
        [_sds((s, D_MODEL), BF16), _sds((s, 2 * D_FF), BF16)],
        (h1, g, wup_t))[0]


def _ffn_conv_cols(up_ref, halo_ref, fcw_ref, first, off):
    u = up_ref[:, off:off + FF_CHUNK].astype(F32)
    uh = jnp.where(first, 0.0, halo_ref[:, off:off + FF_CHUNK].astype(F32))
    w = fcw_ref[:, off:off + FF_CHUNK]
    u1, u2 = _shifts_down(u, uh, (1, 2))
    return w[0:1] * u2 + w[1:2] * u1 + w[2:3] * u


def _ffn_down_loss(up_pre, fcw, wdown, h1, fnorm, target):
    s = h1.shape[0]
    tm = _row_tile(s)

    def body(up_ref, halo_ref, fcw_ref, wd_ref, h1_ref, fn_ref, t_ref, cu_ref, act_ref, dh2_ref, loss_ref, dfn_ref):
        i = pl.program_id(0)

        @pl.when(i == 0)
        def _():
            loss_ref[...] = jnp.zeros_like(loss_ref)
            dfn_ref[...] = jnp.zeros_like(dfn_ref)

        h2 = h1_ref[...]
        for c in range(D_FF // FF_CHUNK):
            gsl = slice(c * FF_CHUNK, (c + 1) * FF_CHUNK)
            vsl = slice(D_FF + c * FF_CHUNK, D_FF + (c + 1) * FF_CHUNK)
            gate = _ffn_conv_cols(up_ref, halo_ref, fcw_ref, i == 0, c * FF_CHUNK)
            cu_ref[:, gsl] = gate.astype(BF16)
            val = _ffn_conv_cols(up_ref, halo_ref, fcw_ref, i == 0, D_FF + c * FF_CHUNK)
            cu_ref[:, vsl] = val.astype(BF16)
            act = (gate * _sig(gate) * val).astype(BF16)
            act_ref[:, gsl] = act
            h2 = h2 + jnp.dot(act, wd_ref[gsl, :], preferred_element_type=F32)
        r = lax.rsqrt(jnp.mean(h2 * h2, axis=-1, keepdims=True) + NORM_EPS)
        yhat = h2 * r
        fn = fn_ref[...]
        diff = yhat * fn - t_ref[...]
        loss_ref[...] += 0.5 * jnp.sum(jnp.sum(diff * diff, axis=1, keepdims=True), axis=0, keepdims=True) / D_MODEL
        dy = diff * (1.0 / D_MODEL)
        dfn_ref[...] += jnp.sum(dy * yhat, axis=0, keepdims=True)
        dyh = dy * fn
        dh2_ref[...] = r * (dyh - yhat * jnp.mean(dyh * yhat, axis=-1, keepdims=True))

    return _pcall(
        body, "ffn_down_loss", (s // tm,),
        [_rows(tm, 2 * D_FF), pl.BlockSpec((HALO, 2 * D_FF), _prev_halo_map(tm)), _full((3, 2 * D_FF)),
         _resident((D_FF, D_MODEL)), _rows(tm, D_MODEL), _full((1, D_MODEL)), _rows(tm, D_MODEL)],
        [_rows(tm, 2 * D_FF), _rows(tm, D_FF), _rows(tm, D_MODEL), _full((1, 128)), _full((1, D_MODEL))],
        [_sds((s, 2 * D_FF), BF16), _sds((s, D_FF), BF16), _sds((s, D_MODEL), F32), _sds((1, 128), F32),
         _sds((1, D_MODEL), F32)],
        (up_pre, up_pre, fcw, wdown, h1, fnorm, target))[0]


def _ffn_act_bwd(dh2, wdown, up, comm):
    s = dh2.shape[0]
    tm = _row_tile(s)

    def body(dh_ref, wd_ref, up_ref, dup_ref):
        dh = dh_ref[...].astype(BF16)
        for c in range(D_FF // FF_CHUNK):
            gsl = slice(c * FF_CHUNK, (c + 1) * FF_CHUNK)
            vsl = slice(D_FF + c * FF_CHUNK, D_FF + (c + 1) * FF_CHUNK)
            dact = lax.dot_general(dh, wd_ref[gsl, :], NT, preferred_element_type=F32)
            gate = up_ref[:, gsl].astype(F32)
            val = up_ref[:, vsl].astype(F32)
            sg = _sig(gate)
            dup_ref[:, gsl] = (dact * val * (sg * (1.0 + gate * (1.0 - sg)))).astype(BF16)
            dup_ref[:, vsl] = (dact * gate * sg).astype(BF16)

    return _pcall(
        body, "ffn_act_bwd", (s // tm,),
        [_rows(tm, D_MODEL), _resident((D_FF, D_MODEL)), _rows(tm, 2 * D_FF)],
        [_rows(tm, 2 * D_FF)], [_sds((s, 2 * D_FF), BF16)],
        (dh2, wdown, up), comm=comm)


def _conv_bwd(dy, x, w, width, chunk, name):
    s = dy.shape[0]
    tm = _row_tile(s)

    def body(dy_ref, dyn_ref, x_ref, w_ref, dx_ref, dw_ref):
        i = pl.program_id(0)

        @pl.when(i == 0)
        def _():
            dw_ref[...] = jnp.zeros_like(dw_ref)

        last = i == s // tm - 1
        for c in range(width // chunk):
            sl = slice(c * chunk, (c + 1) * chunk)
            d = dy_ref[:, sl].astype(F32)
            dn = jnp.where(last, 0.0, dyn_ref[:, sl].astype(F32))
            xv = x_ref[:, sl].astype(F32)
            wv = w_ref[:, sl]
            d1, d2 = _shifts_up(d, dn, (1, 2))
            dx = wv[2:3] * d + wv[1:2] * d1 + wv[0:1] * d2
            dx_ref[:, sl] = dx.astype(BF16)
            dw_ref[0:1, sl] += jnp.sum(d2 * xv, axis=0, keepdims=True)
            dw_ref[1:2, sl] += jnp.sum(d1 * xv, axis=0, keepdims=True)
            dw_ref[2:3, sl] += jnp.sum(d * xv, axis=0, keepdims=True)

    return _pcall(
        body, name, (s // tm,),
        [_rows(tm, width), pl.BlockSpec((HALO, width), _next_halo_map(tm, s)), _rows(tm, width),
         _full((3, width))],
        [_rows(tm, width), _full((3, width))],
        [_sds((s, width), BF16), _sds((3, width), F32)],
        (dy, dy, x, w))[0]


def _matmul_tn(a, b, tk, name, ts=1024):
    s, ka = a.shape
    n = b.shape[1]
    ts = min(ts, s)
    steps = s // ts

    def body(a_ref, b_ref, o_ref, acc_ref):
        j = pl.program_id(1)

        @pl.when(j == 0)
        def _():
            acc_ref[...] = jnp.zeros_like(acc_ref)

        acc_ref[...] += lax.dot_general(a_ref[...].astype(BF16), b_ref[...].astype(BF16), TN,
                                        preferred_element_type=F32)

        @pl.when(j == steps - 1)
        def _():
            o_ref[...] = acc_ref[...].astype(BF16)

    return _pcall(
        body, name, (ka // tk, steps),
        [pl.BlockSpec((ts, tk), lambda i, j: (j, i)), pl.BlockSpec((ts, n), lambda i, j: (j, 0))],
        [pl.BlockSpec((tk, n), lambda i, j: (i, 0))], [_sds((ka, n), BF16)],
        (a, b), scratch=[pltpu.VMEM((tk, n), F32)])[0][0]


def _norm_bwd_tile(xv, g, dy):
    r = lax.rsqrt(jnp.mean(xv * xv, axis=-1, keepdims=True) + NORM_EPS)
    xhat = xv * r
    dg = jnp.sum(dy * xhat, axis=0, keepdims=True)
    dyh = dy * g
    return r * (dyh - xhat * jnp.mean(dyh * xhat, axis=-1, keepdims=True)), dg


def _ffn_up_bwd(dup_pre, wup_t, h1, g, dh2, comm):
    s = h1.shape[0]
    tm = _row_tile(s, 512)

    def body(du_ref, w_ref, h_ref, g_ref, dh2_ref, dh1_ref, dg_ref):
        @pl.when(pl.program_id(0) == 0)
        def _():
            dg_ref[...] = jnp.zeros_like(dg_ref)

        dhn = jnp.dot(du_ref[...], w_ref[...], preferred_element_type=F32)
        dx, dg = _norm_bwd_tile(h_ref[...], g_ref[...], dhn)
        dg_ref[...] += dg
        dh1_ref[...] = dh2_ref[...] + dx

    return _pcall(
        body, "ffn_up_bwd", (s // tm,),
        [_rows(tm, 2 * D_FF), _resident((2 * D_FF, D_MODEL)), _rows(tm, D_MODEL), _full((1, D_MODEL)),
         _rows(tm, D_MODEL)],
        [_rows(tm, D_MODEL), _full((1, D_MODEL))],
        [_sds((s, D_MODEL), F32), _sds((1, D_MODEL), F32)],
        (dup_pre, wup_t, h1, g, dh2), comm=comm)


def _mix_bwd(dh1, wout, gates, ap, cp, wa, wc, cbx, conv_w, comm):
    s = dh1.shape[0]
    tm = _row_tile(s)

    def body(dh_ref, wo_ref, gate_ref, ap_ref, cp_ref, wa_ref, wc_ref, cbx_ref, halo_ref, cw_ref,
             dg_ref, da_ref, dc_ref, dattn_ref, dcb_ref, dcv_ref):
        first = pl.program_id(0) == 0
        dm = lax.dot_general(dh_ref[...].astype(BF16), wo_ref[...], NT, preferred_element_type=F32)
        sa = _sig(gate_ref[:, 0:D_MODEL].astype(F32))
        sc = _sig(gate_ref[:, D_MODEL:2 * D_MODEL].astype(F32))
        da = (dm * sa).astype(BF16)
        dc = (dm * sc).astype(BF16)
        da_ref[...] = da
        dc_ref[...] = dc
        dg_ref[:, 0:D_MODEL] = (dm * ap_ref[...].astype(F32) * sa * (1.0 - sa)).astype(BF16)
        dg_ref[:, D_MODEL:2 * D_MODEL] = (dm * cp_ref[...].astype(F32) * sc * (1.0 - sc)).astype(BF16)
        dattn_ref[...] = lax.dot_general(da, wa_ref[...], NT, preferred_element_type=F32).astype(BF16)
        dconv = lax.dot_general(dc, wc_ref[...], NT, preferred_element_type=F32)
        cb, _, _, _, cv = _conv_u(cbx_ref, halo_ref, cw_ref, first)
        dcb_ref[...] = (dconv * cv).astype(BF16)
        dcv_ref[...] = (dconv * cb).astype(BF16)

    return _pcall(
        body, "mix_bwd", (s // tm,),
        [_rows(tm, D_MODEL), _full((D_MODEL, D_MODEL)), _rows(tm, GATE_W), _rows(tm, D_MODEL),
         _rows(tm, D_MODEL), _full((ATTN_W, D_MODEL)), _full((CONV_W, D_MODEL)), _rows(tm, CBX_W),
         pl.BlockSpec((HALO, CBX_W), _prev_halo_map(tm)), _full((3, CONV_W))],
        [_rows(tm, GATE_W), _rows(tm, D_MODEL), _rows(tm, D_MODEL), _rows(tm, ATTN_W),
         _rows(tm, CONV_W), _rows(tm, CONV_W)],
        [_sds((s, GATE_W), BF16), _sds((s, D_MODEL), BF16), _sds((s, D_MODEL), BF16), _sds((s, ATTN_W), BF16),
         _sds((s, CONV_W), BF16), _sds((s, CONV_W), BF16)],
        (dh1, wout, gates, ap, cp, wa, wc, cbx, cbx, conv_w), comm=comm)


def _conv_branch_bwd(dcv, cbx, conv_w):
    s = dcv.shape[0]
    tm = _row_tile(s)

    def body(d_ref, dn_ref, cbx_ref, w_ref, dcc_ref, dcx_ref, dw_ref):
        i = pl.program_id(0)

        @pl.when(i == 0)
        def _():
            dw_ref[...] = jnp.zeros_like(dw_ref)

        last = i == s // tm - 1
        cc = cbx_ref[:, CONV_W:2 * CONV_W].astype(F32)
        cx = cbx_ref[:, 2 * CONV_W:3 * CONV_W].astype(F32)
        u = cc * cx
        d = d_ref[...].astype(F32)
        dn = jnp.where(last, 0.0, dn_ref[...].astype(F32))
        d1, d2 = _shifts_up(d, dn, (1, 2))
        du = w_ref[2:3, :] * d + w_ref[1:2, :] * d1 + w_ref[0:1, :] * d2
        dcc_ref[...] = (du * cx).astype(BF16)
        dcx_ref[...] = (du * cc).astype(BF16)
        dw_ref[0:1, :] += jnp.sum(d2 * u, axis=0, keepdims=True)
        dw_ref[1:2, :] += jnp.sum(d1 * u, axis=0, keepdims=True)
        dw_ref[2:3, :] += jnp.sum(d * u, axis=0, keepdims=True)

    return _pcall(
        body, "conv_branch_bwd", (s // tm,),
        [_rows(tm, CONV_W), pl.BlockSpec((HALO, CONV_W), _next_halo_map(tm, s)), _rows(tm, CBX_W),
         _full((3, CONV_W))],
        [_rows(tm, CONV_W), _rows(tm, CONV_W), _full((3, CONV_W))],
        [_sds((s, CONV_W), BF16), _sds((s, CONV_W), BF16), _sds((3, CONV_W), F32)],
        (dcv, dcv, cbx, conv_w))[0]


def _attn_bwd(qkv, sinks, attn, lse, dattn, comm):
    s = qkv.shape[0]

    def body(sinks_ref, q_ref, kp_ref, kc_ref, vp_ref, vc_ref, o_ref, lse_ref, do_ref,
             dq_ref, dk_ref, dv_ref, ds_ref):
        n = pl.program_id(0)

        @pl.when(n == 0)
        def _():
            dk_ref[...] = jnp.zeros_like(dk_ref)
            dv_ref[...] = jnp.zeros_like(dv_ref)
            ds_ref[...] = jnp.zeros_like(ds_ref)

        mask = _attn_mask(n)
        lower = _lower_lanes()
        lane = lax.broadcasted_iota(jnp.int32, (BLOCK, 128), 1)
        lower2 = lax.broadcasted_iota(jnp.int32, (2 * BLOCK, 128), 1) < HEAD_DIM
        lane1 = lax.broadcasted_iota(jnp.int32, (1, 128), 1)
        qv, ov, dov, lsev = q_ref[...], o_ref[...], do_ref[...], lse_ref[...]
        dk_fold, dv_fold = [], []
        dsink = jnp.zeros((1, 128), F32)
        for kh in range(2):
            qs = _stack_heads(qv, kh)
            dos = _stack_heads(dov, kh)
            os_ = _stack_heads(ov, kh)
            kd, vd = _dup_kv(kp_ref, kc_ref, kh), _dup_kv(vp_ref, vc_ref, kh)
            lse = jnp.concatenate(
                [jnp.sum(jnp.where(lane == kh * 4 + g, lsev, 0.0), axis=1, keepdims=True) for g in range(4)], axis=0)
            sc = lax.dot_general(qs, kd, NT, preferred_element_type=F32) * ATTN_SCALE
            p = jnp.exp(jnp.where(mask, sc, NEG) - lse)
            dp = lax.dot_general(dos, vd, NT, preferred_element_type=F32)
            delta = jnp.sum(dos.astype(F32) * os_.astype(F32), axis=1, keepdims=True)
            dsc = (p * (dp - delta) * ATTN_SCALE).astype(BF16)
            dqs = jnp.dot(dsc, kd, preferred_element_type=F32)
            for pair in range(2):
                lo = dqs[(2 * pair) * BLOCK:(2 * pair + 1) * BLOCK]
                hi = dqs[(2 * pair + 1) * BLOCK:(2 * pair + 2) * BLOCK]
                col = (kh * 2 + pair) * 128
                dq_ref[:, col:col + 128] = jnp.where(lower, lo, hi).astype(BF16)
            dkd = lax.dot_general(dsc, qs, TN, preferred_element_type=F32)
            dvd = lax.dot_general(p.astype(BF16), dos, TN, preferred_element_type=F32)
            dk_fold.append(dkd + pltpu.roll(dkd, HEAD_DIM, axis=1))
            dv_fold.append(dvd + pltpu.roll(dvd, HEAD_DIM, axis=1))
            psink = jnp.exp(_sink_col(sinks_ref, kh) - lse) * delta
            for g in range(4):
                tot = jnp.sum(psink[g * BLOCK:(g + 1) * BLOCK], axis=0, keepdims=True)
                dsink = dsink - jnp.where(lane1 == kh * 4 + g, tot, 0.0)
        dk2 = jnp.where(lower2, dk_fold[0], dk_fold[1])
        dv2 = jnp.where(lower2, dv_fold[0], dv_fold[1])
        ds_ref[...] += dsink
        cur = pl.ds(pl.multiple_of(n * BLOCK, BLOCK), BLOCK)
        dk_ref[cur, :] += dk2[BLOCK:]
        dv_ref[cur, :] += dv2[BLOCK:]

        @pl.when(n > 0)
        def _():
            prev = pl.ds(pl.multiple_of((n - 1) * BLOCK, BLOCK), BLOCK)
            dk_ref[prev, :] += dk2[:BLOCK]
            dv_ref[prev, :] += dv2[:BLOCK]

    blk = lambda w: pl.BlockSpec((BLOCK, w), lambda n: (n, 0))
    return _pcall(
        body, "attn_bwd", (s // BLOCK,),
        [pl.BlockSpec(memory_space=pltpu.SMEM)] + _attn_specs() + [blk(ATTN_W), blk(128), blk(ATTN_W)],
        [blk(ATTN_W), _full((s, KV_W)), _full((s, KV_W)), _full((1, 128))],
        [_sds((s, ATTN_W), BF16), _sds((s, KV_W), F32), _sds((s, KV_W), F32), _sds((1, 128), F32)],
        (sinks, qkv, qkv, qkv, qkv, qkv, attn, lse, dattn), comm=comm)


def _assemble_dproj(dq, dk, dv, dcb, dcc, dcx, dgates):
    s = dq.shape[0]
    tm = _row_tile(s)
    pieces = (ATTN_W, KV_W, KV_W, CONV_W, CONV_W, CONV_W, GATE_W)

    def body(*refs):
        srcs, dp_ref, db_ref = refs[:len(pieces)], refs[-2], refs[-1]

        @pl.when(pl.program_id(0) == 0)
        def _():
            db_ref[...] = jnp.zeros_like(db_ref)

        off = 0
        for ref, w in zip(srcs, pieces):
            v = ref[...].astype(BF16)
            dp_ref[:, off:off + w] = v
            db_ref[:, off:off + w] += jnp.sum(v.astype(F32), axis=0, keepdims=True)
            off += w

    return _pcall(
        body, "assemble_dproj", (s // tm,), [_rows(tm, w) for w in pieces],
        [_rows(tm, IN_W), _full((1, IN_W))], [_sds((s, IN_W), BF16), _sds((1, IN_W), F32)],
        (dq, dk, dv, dcb, dcc, dcx, dgates))[0]


def _inproj_bwd(dproj, win_t, x, g, dh1, comm):
    s = x.shape[0]
    tm = _row_tile(s, 512)

    def body(dp_ref, w_ref, x_ref, g_ref, dh_ref, dx_ref, dg_ref):
        @pl.when(pl.program_id(0) == 0)
        def _():
            dg_ref[...] = jnp.zeros_like(dg_ref)

        dxn = jnp.dot(dp_ref[...], w_ref[...], preferred_element_type=F32)
        dx, dg = _norm_bwd_tile(x_ref[...], g_ref[...], dxn)
        dg_ref[...] += dg
        dx_ref[...] = dh_ref[...] + dx

    return _pcall(
        body, "inproj_bwd", (s // tm,),
        [_rows(tm, IN_W), _resident((IN_W, D_MODEL)), _rows(tm, D_MODEL), _full((1, D_MODEL)), _rows(tm, D_MODEL)],
        [_rows(tm, D_MODEL), _full((1, D_MODEL))],
        [_sds((s, D_MODEL), F32), _sds((1, D_MODEL), F32)],
        (dproj, win_t, x, g, dh1), comm=comm)


def _adam_math(w, g, m, v):
    m2 = ADAM_B1 * m + (1.0 - ADAM_B1) * g
    v2 = ADAM_B2 * v + (1.0 - ADAM_B2) * (g * g)
    m_hat = m2 / (1.0 - ADAM_B1 ** ADAM_STEP)
    v_hat = v2 / (1.0 - ADAM_B2 ** ADAM_STEP)
    delta = -ADAM_LR * (m_hat / (jnp.sqrt(v_hat) + ADAM_EPS) + ADAM_WD * w)
    return delta, m2, v2


def _sum_slots(ref):
    tot = ref[0].astype(F32)
    for i in range(1, N_DEV):
        tot = tot + ref[i].astype(F32)
    return tot


def _sum_adamw(parts, w, m, v, tr, name):
    r, c = w.shape

    def body(p_ref, w_ref, m_ref, v_ref, g_ref, d_ref, m2_ref, v2_ref):
        g = _sum_slots(p_ref)
        g_ref[...] = g
        d_ref[...], m2_ref[...], v2_ref[...] = _adam_math(w_ref[...], g, m_ref[...], v_ref[...])

    spec = pl.BlockSpec((tr, c), lambda i: (i, 0))
    return _pcall(body, name, (r // tr,), [pl.BlockSpec((N_DEV, tr, c), lambda i: (0, i, 0)), spec, spec, spec],
                  [spec] * 4, [_sds((r, c), F32)] * 4, (parts, w, m, v))[0]


def _sum_halves_adamw(parts_a, parts_b, w, m, v, tr, name):
    _, r, c = parts_a.shape
    nt = r // tr

    def body(a_ref, b_ref, w_ref, m_ref, v_ref, g_ref, d_ref, m2_ref, v2_ref):
        i = pl.program_id(0)

        @pl.when(i < nt)
        def _():
            g_ref[...] = _sum_slots(a_ref)

        @pl.when(i >= nt)
        def _():
            g_ref[...] = _sum_slots(b_ref)

        d_ref[...], m2_ref[...], v2_ref[...] = _adam_math(w_ref[...], g_ref[...], m_ref[...], v_ref[...])

    spec = pl.BlockSpec((tr, c), lambda i: (i, 0))
    return _pcall(
        body, name, (2 * nt,),
        [pl.BlockSpec((N_DEV, tr, c), lambda i: (0, jnp.minimum(i, nt - 1), 0)),
         pl.BlockSpec((N_DEV, tr, c), lambda i: (0, jnp.maximum(i - nt, 0), 0)), spec, spec, spec],
        [spec] * 4, [_sds((2 * r, c), F32)] * 4, (parts_a, parts_b, w, m, v))[0]


def _sum_only(parts, tr, name):
    _, r, c = parts.shape

    def body(p_ref, g_ref):
        g_ref[...] = _sum_slots(p_ref)

    return _pcall(body, name, (r // tr,), [pl.BlockSpec((N_DEV, tr, c), lambda i: (0, i, 0))],
                  [pl.BlockSpec((tr, c), lambda i: (i, 0))], [_sds((r, c), F32)], (parts,))[0][0]


def _adamw(w, g, m, v, tr, name):
    r, c = w.shape

    def body(w_ref, g_ref, m_ref, v_ref, d_ref, m2_ref, v2_ref):
        d_ref[...], m2_ref[...], v2_ref[...] = _adam_math(w_ref[...], g_ref[...], m_ref[...], v_ref[...])

    spec = pl.BlockSpec((tr, c), lambda i: (i, 0))
    return _pcall(body, name, (r // tr,), [spec] * 4, [spec] * 3, [_sds((r, c), F32)] * 3, (w, g, m, v))[0]


def _pad_cols(a, c):
    return jnp.pad(a, ((0, 0), (0, c - a.shape[1])))


def _to_col_slabs(g):
    r = g.shape[0]
    return jnp.transpose(g.reshape(r, N_DEV, 128), (1, 0, 2)).reshape(N_DEV * r, 128)


def _from_col_slabs(t):
    r = t.shape[0] // N_DEV
    return jnp.transpose(t.reshape(N_DEV, r, 128), (1, 0, 2)).reshape(r, N_DEV * 128)


def _slots(t):
    return t.reshape(N_DEV, t.shape[0] // N_DEV, t.shape[1])


def kernel(x, mix_norm, w_in, b_in, sinks, conv_w, w_attn_branch, w_conv_branch, w_out, ffn_norm, w_up, ffn_conv_w, w_down, final_norm, loss_target, m_mix_norm, m_w_in, m_b_in, m_sinks, m_conv_w, m_w_attn_branch, m_w_conv_branch, m_w_out, m_ffn_norm, m_w_up, m_ffn_conv_w, m_w_down, m_final_norm, v_mix_norm, v_w_in, v_b_in, v_sinks, v_conv_w, v_w_attn_branch, v_w_conv_branch, v_w_out, v_ffn_norm, v_w_up, v_ffn_conv_w, v_w_down, v_final_norm):
    xs, tgt = x[0], loss_target[0]
    me = 4 * lax.axis_index("x") + 2 * lax.axis_index("y") + lax.axis_index("c")
    in_rows, up_rows = IN_W // N_DEV, 2 * D_FF // N_DEV

    conv_sh = jnp.concatenate([_pad_cols(ffn_conv_w[0], 768), _pad_cols(conv_w[0], 768),
                               jnp.zeros((2, 768), F32)], axis=0)
    win_sh, wup_sh = w_in[0].T.astype(BF16), w_up[0].T.astype(BF16)
    wout_sh, wdown_sh = w_out[0].astype(BF16), w_down[0].astype(BF16)
    wa_sh, wc_sh = w_attn_branch[0].astype(BF16), w_conv_branch[0].astype(BF16)

    (win_t,) = _exchange_only(_AllGather([win_sh]), "gather_w_in")
    (xn, qkv, cbx, gates), (wup_t,) = _norm_inproj(xs, mix_norm, win_t, b_in, _AllGather([wup_sh]))
    (attn, lse), (wa_s, wc_s, wout, conv_g) = _attn_fwd(qkv, sinks, _AllGather([wa_sh, wc_sh, wout_sh, conv_sh]))
    wa, wc = _from_col_slabs(wa_s), _from_col_slabs(wc_s)
    conv_g = conv_g.reshape(N_DEV, 8, 768)
    fcw = jnp.transpose(conv_g[:, 0:3, :up_rows], (1, 0, 2)).reshape(3, 2 * D_FF)
    cw = jnp.transpose(conv_g[:, 3:6, :CONV_W // N_DEV], (1, 0, 2)).reshape(3, CONV_W)
    (conv, ap, cp, merged, h1), (wdown,) = _mix_fwd(xs, cbx, gates, attn, cw, wa, wc, wout, _AllGather([wdown_sh]))
    hn, up_pre = _ffn_up(h1, ffn_norm, wup_t)
    up, act, dh2, loss_p, dfn_p = _ffn_down_loss(up_pre, fcw, wdown, h1, final_norm.reshape(1, D_MODEL), tgt)

    g_wdown = _matmul_tn(act, dh2, FF_CHUNK, "grad_w_down")
    (dup,), (r_wdown,) = _ffn_act_bwd(dh2, wdown, up, _ReduceScatter([(g_wdown, 0, D_FF // N_DEV)]))
    dup_pre, dfcw_p = _conv_bwd(dup, up_pre, fcw, 2 * D_FF, FF_CHUNK, "ffn_conv_bwd")
    g_wup_t = _matmul_tn(dup_pre, hn, FF_CHUNK, "grad_w_up")
    (dh1, dffn_p), (r_wup_a,) = _ffn_up_bwd(dup_pre, wup_t, h1, ffn_norm, dh2,
                                            _ReduceScatter([(g_wup_t, 0, up_rows // 2)]))
    g_wout = _matmul_tn(merged, dh1, D_MODEL, "grad_w_out")
    (dgates, da, dc, dattn, dcb, dcv), (r_wup_b,) = _mix_bwd(
        dh1, wout, gates, ap, cp, wa, wc, cbx, cw, _ReduceScatter([(g_wup_t, up_rows // 2, up_rows // 2)]))
    g_wa = _to_col_slabs(_matmul_tn(attn, da, ATTN_W, "grad_w_attn_branch"))
    g_wc = _to_col_slabs(_matmul_tn(conv, dc, CONV_W, "grad_w_conv_branch"))
    dcc, dcx, dcw_p = _conv_branch_bwd(dcv, cbx, cw)
    (dq, dk, dv, dsink_p), (r_wout, r_wa, r_wc) = _attn_bwd(
        qkv, sinks, attn, lse, dattn,
        _ReduceScatter([(g_wout, 0, D_MODEL // N_DEV), (g_wa, 0, ATTN_W), (g_wc, 0, CONV_W)]))
    dproj, dbin_p = _assemble_dproj(dq, dk, dv, dcb, dcc, dcx, dgates)
    g_win_t = _matmul_tn(dproj, xn, IN_W // 2, "grad_w_in")
    (dx, dmix_p), (r_win_a,) = _inproj_bwd(dproj, win_t, xs, mix_norm, dh1,
                                           _ReduceScatter([(g_win_t, 0, in_rows // 2)]))

    row = lambda a: _pad_cols(a.reshape(1, -1), D_MODEL)
    small = jnp.concatenate(
        [dmix_p, dffn_p, dfn_p, row(dsink_p[:, :N_HEADS]), row(loss_p[:, :1]),
         _pad_cols(dbin_p, 5 * D_MODEL).reshape(5, D_MODEL), _pad_cols(dcw_p, D_MODEL),
         _pad_cols(dfcw_p, 6 * D_MODEL).reshape(18, D_MODEL), jnp.zeros((1, D_MODEL), F32)], axis=0)
    r_win_b, r_small = _exchange_only(_ReduceScatter([(g_win_t, in_rows // 2, in_rows // 2)], [small]),
                                      "scatter_w_in_small")

    small_t = _sum_only(_slots(r_small), SMALL_ROWS, "sum_small")
    g_mix, g_ffn, g_fn = small_t[0:1], small_t[1:2], small_t[2:3]
    g_sinks, loss = small_t[3:4, :N_HEADS], small_t[4, 0]
    g_bin = small_t[5:10].reshape(1, 5 * D_MODEL)[:, :IN_W]
    g_cw_full = small_t[10:13, :CONV_W]
    g_fcw_full = small_t[13:31].reshape(3, 6 * D_MODEL)[:, :2 * D_FF]
    g_cw = lax.dynamic_slice_in_dim(g_cw_full, me * (CONV_W // N_DEV), CONV_W // N_DEV, axis=1)
    g_fcw = lax.dynamic_slice_in_dim(g_fcw_full, me * up_rows, up_rows, axis=1)

    big = {}
    big["w_in"] = tuple(t.T for t in _sum_halves_adamw(
        _slots(r_win_a), _slots(r_win_b), w_in[0].T, m_w_in[0].T, v_w_in[0].T, in_rows // 2, "adamw_w_in"))
    big["w_up"] = tuple(t.T for t in _sum_halves_adamw(
        _slots(r_wup_a), _slots(r_wup_b), w_up[0].T, m_w_up[0].T, v_w_up[0].T, up_rows // 2, "adamw_w_up"))
    big["w_out"] = _sum_adamw(_slots(r_wout), w_out[0], m_w_out[0], v_w_out[0], 128, "adamw_w_out")
    big["w_down"] = _sum_adamw(_slots(r_wdown), w_down[0], m_w_down[0], v_w_down[0], 176, "adamw_w_down")
    big["w_attn_branch"] = _sum_adamw(_slots(r_wa), w_attn_branch[0], m_w_attn_branch[0], v_w_attn_branch[0], 256,
                                      "adamw_w_attn_branch")
    big["w_conv_branch"] = _sum_adamw(_slots(r_wc), w_conv_branch[0], m_w_conv_branch[0], v_w_conv_branch[0], 256,
                                      "adamw_w_conv_branch")

    def small_adam(w, g, m, v, name):
        shp = w.shape
        w2, m2, v2 = (t.reshape(-1, shp[-1]) for t in (w, m, v))
        d, mm, vv = _adamw(w2, g.reshape(w2.shape), m2, v2, w2.shape[0], name)
        return g.reshape(shp), d.reshape(shp), mm.reshape(shp), vv.reshape(shp)

    res = {
        "mix_norm": small_adam(mix_norm, g_mix, m_mix_norm, v_mix_norm, "adamw_mix_norm"),
        "b_in": small_adam(b_in, g_bin, m_b_in, v_b_in, "adamw_b_in"),
        "sinks": small_adam(sinks, g_sinks, m_sinks, v_sinks, "adamw_sinks"),
        "conv_w": small_adam(conv_w, g_cw, m_conv_w, v_conv_w, "adamw_conv_w"),
        "ffn_norm": small_adam(ffn_norm, g_ffn, m_ffn_norm, v_ffn_norm, "adamw_ffn_norm"),
        "ffn_conv_w": small_adam(ffn_conv_w, g_fcw, m_ffn_conv_w, v_ffn_conv_w, "adamw_ffn_conv_w"),
        "final_norm": small_adam(final_norm, g_fn, m_final_norm, v_final_norm, "adamw_final_norm"),
    }
    for name, ref_w in (("w_in", w_in), ("w_up", w_up), ("w_out", w_out), ("w_down", w_down),
                        ("w_attn_branch", w_attn_branch), ("w_conv_branch", w_conv_branch)):
        res[name] = tuple(t.reshape(ref_w.shape) for t in big[name])

    order = ["mix_norm", "w_in", "b_in", "sinks", "conv_w", "w_attn_branch", "w_conv_branch", "w_out",
             "ffn_norm", "w_up", "ffn_conv_w", "w_down", "final_norm"]
    out = [loss, dx.reshape(x.shape)]
    for k in range(4):
        out += [res[name][k] for name in order]
    return tuple(out)
```

```python
import math

import jax
import jax.numpy as jnp
from jax import lax
from jax.experimental import pallas as pl
from jax.experimental.pallas import tpu as pltpu

F32 = jnp.float32
BF16 = jnp.bfloat16
MESH = pl.DeviceIdType.MESH
N_DEV = 8

D_MODEL = 1024
HEAD_DIM = 64
N_HEADS = 8
BLOCK = 128
ATTN_W = 512
KV_W = 128
CONV_W = 512
QKV_W = ATTN_W + 2 * KV_W
CBX_W = 3 * CONV_W
GATE_W = 2 * D_MODEL
IN_W = QKV_W + CBX_W + GATE_W
D_FF = 2816
FF_CHUNK = 1408
NORM_EPS = 1e-5
ATTN_SCALE = HEAD_DIM ** -0.5
NEG = -1e30
HALO = 16

ADAM_LR = 0.001
ADAM_B1 = 0.9
ADAM_B2 = 0.999
ADAM_EPS = 1e-08
ADAM_WD = 0.01
ADAM_STEP = 10

VMEM_LIMIT = 56 * 1024 * 1024
SMALL_ROWS = 32

NT = (((1,), (1,)), ((), ()))
TN = (((0,), (0,)), ((), ()))
ANY = pl.BlockSpec(memory_space=pl.ANY)


def _sig(v):
    return 1.0 / (1.0 + jnp.exp(-v))


def _row_tile(s, pref=256):
    return pref if s % pref == 0 else s


def _shifts_down(u, halo, ks):
    ext = jnp.concatenate([halo, u], axis=0)
    return [pltpu.roll(ext, k, axis=0)[HALO:, :] for k in ks]


def _shifts_up(u, halo, ks):
    n = u.shape[0]
    ext = jnp.concatenate([u, halo], axis=0)
    return [pltpu.roll(ext, n + HALO - k, axis=0)[:n, :] for k in ks]


def _prev_halo_map(tm):
    return lambda i: (jnp.maximum(i * (tm // HALO) - 1, 0), 0)


def _next_halo_map(tm, s):
    return lambda i: (jnp.minimum((i + 1) * (tm // HALO), s // HALO - 1), 0)


def _full(shape):
    return pl.BlockSpec(shape, lambda *_: (0,) * len(shape))


def _resident(shape):
    return pl.BlockSpec(shape, lambda *_: (0,) * len(shape), pipeline_mode=pl.Buffered(1))


def _rows(tm, c):
    return pl.BlockSpec((tm, c), lambda i: (i, 0))


def _sds(shape, dtype):
    return jax.ShapeDtypeStruct(shape, dtype)


def _my_place():
    x, y, c = lax.axis_index("x"), lax.axis_index("y"), lax.axis_index("c")
    return x, y, c


class _AllGather:
    def __init__(self, shards):
        self.ins = list(shards)
        n = len(shards)
        self.out_shape = [_sds((N_DEV * s.shape[0], s.shape[1]), s.dtype) for s in shards]
        self.sems = [pltpu.SemaphoreType.DMA((7 * n,)), pltpu.SemaphoreType.DMA((7 * n,)),
                     pltpu.SemaphoreType.DMA((n,))]

    def _parts(self, ins, outs, sems):
        send_sems, recv_sems, local_sems = sems
        x, y, c = _my_place()
        me, sibling = (x, y, c), (x, y, 1 - c)
        chips = [(1 - x, y), (x, 1 - y), (1 - x, 1 - y)]

        def rows(k, dev):
            r = ins[k].shape[0]
            start = pl.multiple_of((4 * dev[0] + 2 * dev[1] + dev[2]) * r, 8)
            return outs[k].at[pl.ds(start, r), :]

        def copy(k, j, block, to, src=None):
            return pltpu.make_async_remote_copy(
                src_ref=rows(k, block) if src is None else src, dst_ref=rows(k, block),
                send_sem=send_sems.at[7 * k + j], recv_sem=recv_sems.at[7 * k + j],
                device_id=to, device_id_type=MESH)

        n = len(ins)
        mine = [pltpu.make_async_copy(ins[k], rows(k, me), local_sems.at[k]) for k in range(n)]
        first = []
        for k in range(n):
            first.append(copy(k, 0, me, sibling, src=ins[k]))
            first += [copy(k, 1 + j, me, (*chip, c), src=ins[k]) for j, chip in enumerate(chips)]
        return me, sibling, chips, copy, mine, first

    def start(self, ins, outs, sems):
        _, _, _, _, mine, first = self._parts(ins, outs, sems)
        for cp in mine + first:
            cp.start()

    def finish(self, ins, outs, sems):
        me, sibling, chips, copy, mine, first = self._parts(ins, outs, sems)
        c = me[2]
        n = len(ins)
        passed = []
        for j, chip in enumerate(chips):
            for k in range(n):
                copy(k, 1 + j, (*chip, c), me).wait_recv()
                fwd = copy(k, 4 + j, (*chip, c), sibling)
                fwd.start()
                passed.append(fwd)
        for k in range(n):
            copy(k, 0, sibling, me).wait_recv()
            for j, chip in enumerate(chips):
                copy(k, 4 + j, (*chip, 1 - c), me).wait_recv()
        for cp in first + passed:
            cp.wait_send()
        for cp in mine:
            cp.wait()


class _ReduceScatter:
    def __init__(self, parts, bcast=()):
        self.parts = [(lo, cnt) for _, lo, cnt in parts]
        self.n_parts = len(parts)
        self.ins = [a for a, _, _ in parts] + list(bcast)
        self.out_shape = [_sds((N_DEV * cnt, a.shape[1]), a.dtype) for a, _, cnt in parts]
        self.out_shape += [_sds((N_DEV * b.shape[0], b.shape[1]), b.dtype) for b in bcast]
        n = len(self.ins)
        self.sems = [pltpu.SemaphoreType.DMA((7 * n,)), pltpu.SemaphoreType.DMA((7 * n,)),
                     pltpu.SemaphoreType.DMA((n,))]

    def _copies(self, ins, outs, sems):
        send_sems, recv_sems, local_sems = sems
        x, y, c = _my_place()
        me_idx = 4 * x + 2 * y + c
        remote, local = [], []
        for k in range(len(ins)):
            cnt = outs[k].shape[0] // N_DEV
            dst = outs[k].at[pl.ds(pl.multiple_of(me_idx * cnt, 8), cnt), :]
            if k < self.n_parts:
                lo, _ = self.parts[k]
                r = ins[k].shape[0] // N_DEV
                src_of = lambda idx: ins[k].at[pl.ds(pl.multiple_of(idx * r + lo, 8), cnt), :]
            else:
                src_of = lambda idx: ins[k]
            local.append(pltpu.make_async_copy(src_of(me_idx), dst, local_sems.at[k]))
            for j in range(1, N_DEV):
                peer = (x ^ (j >> 2), y ^ ((j >> 1) & 1), c ^ (j & 1))
                peer_idx = 4 * peer[0] + 2 * peer[1] + peer[2]
                remote.append(pltpu.make_async_remote_copy(
                    src_ref=src_of(peer_idx), dst_ref=dst,
                    send_sem=send_sems.at[7 * k + j - 1], recv_sem=recv_sems.at[7 * k + j - 1],
                    device_id=peer, device_id_type=MESH))
        return remote, local

    def start(self, ins, outs, sems):
        remote, local = self._copies(ins, outs, sems)
        for cp in local + remote:
            cp.start()

    def finish(self, ins, outs, sems):
        remote, local = self._copies(ins, outs, sems)
        for cp in remote:
            cp.wait_recv()
        for cp in remote:
            cp.wait_send()
        for cp in local:
            cp.wait()


def _pcall(body, name, grid, in_specs, out_specs, out_shape, args, scratch=(), comm=None):
    params = pltpu.CompilerParams(dimension_semantics=("arbitrary",) * len(grid), vmem_limit_bytes=VMEM_LIMIT)
    in_specs, out_specs, out_shape, scratch = list(in_specs), list(out_specs), list(out_shape), list(scratch)
    if comm is None:
        res = pl.pallas_call(body, name=name, grid=grid, in_specs=in_specs, out_specs=out_specs, out_shape=out_shape,
                             scratch_shapes=scratch, compiler_params=params)(*args)
        return list(res), []
    n_in, n_out, n_scr = len(in_specs), len(out_specs), len(scratch)
    ci, co = len(comm.ins), len(comm.out_shape)
    total = math.prod(grid)

    def carried(*refs):
        bounds = [0, n_in, n_in + ci, n_in + ci + n_out, n_in + ci + n_out + co, n_in + ci + n_out + co + n_scr]
        ins, cins, outs, couts, scr = (refs[a:b] for a, b in zip(bounds[:-1], bounds[1:]))
        sems = refs[bounds[-1]:]
        step = pl.program_id(0)
        for d in range(1, len(grid)):
            step = step * grid[d] + pl.program_id(d)

        @pl.when(step == 0)
        def _():
            comm.start(cins, couts, sems)

        body(*ins, *outs, *scr)

        @pl.when(step == total - 1)
        def _():
            comm.finish(cins, couts, sems)

    res = pl.pallas_call(
        carried, name=name, grid=grid, in_specs=in_specs + [ANY] * ci, out_specs=out_specs + [ANY] * co,
        out_shape=out_shape + comm.out_shape, scratch_shapes=scratch + comm.sems, compiler_params=params,
    )(*args, *comm.ins)
    return list(res[:n_out]), list(res[n_out:])


def _exchange_only(comm, name):
    def body(*refs):
        ci, co = len(comm.ins), len(comm.out_shape)
        comm.start(refs[:ci], refs[ci:ci + co], refs[ci + co:])
        comm.finish(refs[:ci], refs[ci:ci + co], refs[ci + co:])

    return pl.pallas_call(body, name=name, out_shape=comm.out_shape, in_specs=[ANY] * len(comm.ins),
                          out_specs=[ANY] * len(comm.out_shape), scratch_shapes=comm.sems)(*comm.ins)


def _norm_inproj(x, g, win_t, b_in, comm):
    s = x.shape[0]
    tm = _row_tile(s, 512)
    widths = (QKV_W, CBX_W, GATE_W)

    def body(x_ref, g_ref, w_ref, b_ref, xn_ref, qkv_ref, cbx_ref, gate_ref):
        xv = x_ref[...]
        r = lax.rsqrt(jnp.mean(xv * xv, axis=-1, keepdims=True) + NORM_EPS)
        xn = (xv * r * g_ref[...]).astype(BF16)
        xn_ref[...] = xn
        off = 0
        for o_ref, w in zip((qkv_ref, cbx_ref, gate_ref), widths):
            acc = lax.dot_general(xn, w_ref[off:off + w, :], NT, preferred_element_type=F32)
            o_ref[...] = (acc + b_ref[:, off:off + w]).astype(BF16)
            off += w

    return _pcall(
        body, "norm_inproj", (s // tm,),
        [_rows(tm, D_MODEL), _full((1, D_MODEL)), _resident((IN_W, D_MODEL)), _full((1, IN_W))],
        [_rows(tm, D_MODEL)] + [_rows(tm, w) for w in widths],
        [_sds((s, D_MODEL), BF16)] + [_sds((s, w), BF16) for w in widths],
        (x, g, win_t, b_in), comm=comm)


def _attn_specs():
    prev = lambda n: jnp.maximum(n - 1, 0)
    return [pl.BlockSpec((BLOCK, ATTN_W), lambda n: (n, 0)),
            pl.BlockSpec((BLOCK, KV_W), lambda n: (prev(n), ATTN_W // KV_W)),
            pl.BlockSpec((BLOCK, KV_W), lambda n: (n, ATTN_W // KV_W)),
            pl.BlockSpec((BLOCK, KV_W), lambda n: (prev(n), ATTN_W // KV_W + 1)),
            pl.BlockSpec((BLOCK, KV_W), lambda n: (n, ATTN_W // KV_W + 1))]


def _lower_lanes():
    return lax.broadcasted_iota(jnp.int32, (BLOCK, 128), 1) < HEAD_DIM


def _stack_heads(val, kh):
    lower = _lower_lanes()
    parts = []
    for g in range(4):
        h = kh * 4 + g
        blk = val[:, (h // 2) * 128:(h // 2 + 1) * 128]
        keep = lower if h % 2 == 0 else jnp.logical_not(lower)
        parts.append(jnp.where(keep, blk, jnp.zeros_like(blk)))
    return jnp.concatenate(parts, axis=0)


def _dup_kv(prev_ref, cur_ref, kh):
    t = jnp.concatenate([prev_ref[...], cur_ref[...]], axis=0).astype(F32)
    rolled = pltpu.roll(t, HEAD_DIM, axis=1)
    lower = lax.broadcasted_iota(jnp.int32, t.shape, 1) < HEAD_DIM
    dup = jnp.where(lower, t, rolled) if kh == 0 else jnp.where(lower, rolled, t)
    return dup.astype(BF16)


def _attn_mask(n):
    row = lax.broadcasted_iota(jnp.int32, (4 * BLOCK, 2 * BLOCK), 0)
    kj = lax.broadcasted_iota(jnp.int32, (4 * BLOCK, 2 * BLOCK), 1)
    dist = (row & (BLOCK - 1)) + BLOCK - kj
    band = jnp.logical_and(dist >= 0, dist < BLOCK)
    return jnp.logical_and(band, jnp.logical_or(kj >= BLOCK, n > 0))


def _sink_col(sinks_ref, kh):
    gi = lax.broadcasted_iota(jnp.int32, (4 * BLOCK, 1), 0) // BLOCK
    col = jnp.zeros((4 * BLOCK, 1), F32)
    for g in range(4):
        col = jnp.where(gi == g, sinks_ref[0, kh * 4 + g], col)
    return col


def _attn_fwd(qkv, sinks, comm):
    s = qkv.shape[0]

    def body(sinks_ref, q_ref, kp_ref, kc_ref, vp_ref, vc_ref, o_ref, lse_ref):
        n = pl.program_id(0)
        mask = _attn_mask(n)
        lower = _lower_lanes()
        lane = lax.broadcasted_iota(jnp.int32, (BLOCK, 128), 1)
        qv = q_ref[...]
        lse_out = jnp.zeros((BLOCK, 128), F32)
        for kh in range(2):
            qs = _stack_heads(qv, kh)
            kd, vd = _dup_kv(kp_ref, kc_ref, kh), _dup_kv(vp_ref, vc_ref, kh)
            sc = lax.dot_general(qs, kd, NT, preferred_element_type=F32) * ATTN_SCALE
            sc = jnp.where(mask, sc, NEG)
            sink = _sink_col(sinks_ref, kh)
            m = jnp.maximum(jnp.max(sc, axis=1, keepdims=True), sink)
            p = jnp.exp(sc - m)
            l = jnp.sum(p, axis=1, keepdims=True) + jnp.exp(sink - m)
            o = jnp.dot(p.astype(BF16), vd, preferred_element_type=F32) / l
            lse = m + jnp.log(l)
            for pair in range(2):
                lo = o[(2 * pair) * BLOCK:(2 * pair + 1) * BLOCK]
                hi = o[(2 * pair + 1) * BLOCK:(2 * pair + 2) * BLOCK]
                col = (kh * 2 + pair) * 128
                o_ref[:, col:col + 128] = jnp.where(lower, lo, hi).astype(BF16)
            for g in range(4):
                lse_out = jnp.where(lane == kh * 4 + g, lse[g * BLOCK:(g + 1) * BLOCK], lse_out)
        lse_ref[...] = lse_out

    return _pcall(
        body, "attn_fwd", (s // BLOCK,),
        [pl.BlockSpec(memory_space=pltpu.SMEM)] + _attn_specs(),
        [pl.BlockSpec((BLOCK, ATTN_W), lambda n: (n, 0)), pl.BlockSpec((BLOCK, 128), lambda n: (n, 0))],
        [_sds((s, ATTN_W), BF16), _sds((s, 128), F32)],
        (sinks, qkv, qkv, qkv, qkv, qkv), comm=comm)


def _conv_u(cbx_ref, halo_ref, w_ref, first):
    cb = cbx_ref[:, 0:CONV_W].astype(F32)
    cc = cbx_ref[:, CONV_W:2 * CONV_W].astype(F32)
    cx = cbx_ref[:, 2 * CONV_W:3 * CONV_W].astype(F32)
    u = cc * cx
    uh = halo_ref[:, CONV_W:2 * CONV_W].astype(F32) * halo_ref[:, 2 * CONV_W:3 * CONV_W].astype(F32)
    uh = jnp.where(first, 0.0, uh)
    u1, u2 = _shifts_down(u, uh, (1, 2))
    cv = w_ref[0:1, :] * u2 + w_ref[1:2, :] * u1 + w_ref[2:3, :] * u
    return cb, cc, cx, u, cv


def _mix_fwd(x, cbx, gates, attn, conv_w, wa, wc, wout, comm):
    s = x.shape[0]
    tm = _row_tile(s)

    def body(x_ref, cbx_ref, halo_ref, gate_ref, attn_ref, cw_ref, wa_ref, wc_ref, wo_ref,
             conv_ref, ap_ref, cp_ref, mg_ref, h1_ref):
        first = pl.program_id(0) == 0
        cb, _, _, _, cv = _conv_u(cbx_ref, halo_ref, cw_ref, first)
        conv = (cb * cv).astype(BF16)
        conv_ref[...] = conv
        ap = jnp.dot(attn_ref[...], wa_ref[...], preferred_element_type=F32)
        cp = jnp.dot(conv, wc_ref[...], preferred_element_type=F32)
        ap_ref[...] = ap.astype(BF16)
        cp_ref[...] = cp.astype(BF16)
        ga = gate_ref[:, 0:D_MODEL].astype(F32)
        gc = gate_ref[:, D_MODEL:2 * D_MODEL].astype(F32)
        merged = (_sig(ga) * ap + _sig(gc) * cp).astype(BF16)
        mg_ref[...] = merged
        h1_ref[...] = x_ref[...] + jnp.dot(merged, wo_ref[...], preferred_element_type=F32)

    return _pcall(
        body, "mix_fwd", (s // tm,),
        [_rows(tm, D_MODEL), _rows(tm, CBX_W), pl.BlockSpec((HALO, CBX_W), _prev_halo_map(tm)),
         _rows(tm, GATE_W), _rows(tm, ATTN_W), _full((3, CONV_W)), _full((ATTN_W, D_MODEL)),
         _full((CONV_W, D_MODEL)), _full((D_MODEL, D_MODEL))],
        [_rows(tm, CONV_W), _rows(tm, D_MODEL), _rows(tm, D_MODEL), _rows(tm, D_MODEL), _rows(tm, D_MODEL)],
        [_sds((s, CONV_W), BF16), _sds((s, D_MODEL), BF16), _sds((s, D_MODEL), BF16), _sds((s, D_MODEL), BF16),
         _sds((s, D_MODEL), F32)],
        (x, cbx, cbx, gates, attn, conv_w, wa, wc, wout), comm=comm)


def _ffn_up(h1, g, wup_lo, wup_hi, comm):
    s = h1.shape[0]
    tm = _row_tile(s, 512)
    half = D_MODEL // 2

    def body(h_ref, g_ref, wl_ref, wh_ref, hn_ref, up_ref):
        hv = h_ref[...]
        r = lax.rsqrt(jnp.mean(hv * hv, axis=-1, keepdims=True) + NORM_EPS)
        hn = (hv * r * g_ref[...]).astype(BF16)
        hn_ref[...] = hn
        for c in range(2 * D_FF // FF_CHUNK):
            sl = slice(c * FF_CHUNK, (c + 1) * FF_CHUNK)
            acc = lax.dot_general(hn[:, :half], wl_ref[sl, :], NT, preferred_element_type=F32)
            acc = acc + lax.dot_general(hn[:, half:], wh_ref[sl, :], NT, preferred_element_type=F32)
            up_ref[:, sl] = acc.astype(BF16)

    return _pcall(
        body, "ffn_up", (s // tm,),
        [_rows(tm, D_MODEL), _full((1, D_MODEL)), _resident((2 * D_FF, half)), _resident((2 * D_FF, half))],
        [_rows(tm, D_MODEL), _rows(tm, 2 * D_FF)],
        [_sds((s, D_MODEL), BF16), _sds((s, 2 * D_FF), BF16)],
        (h1, g, wup_lo, wup_hi), comm=comm)


def _ffn_conv_cols(up_ref, halo_ref, fcw_ref, first, off):
    u = up_ref[:, off:off + FF_CHUNK].astype(F32)
    uh = jnp.where(first, 0.0, halo_ref[:, off:off + FF_CHUNK].astype(F32))
    w = fcw_ref[:, off:off + FF_CHUNK]
    u1, u2 = _shifts_down(u, uh, (1, 2))
    return w[0:1] * u2 + w[1:2] * u1 + w[2:3] * u


def _ffn_down_loss(up_pre, fcw, wdown, h1, fnorm, target):
    s = h1.shape[0]
    tm = _row_tile(s)

    def body(up_ref, halo_ref, fcw_ref, wd_ref, h1_ref, fn_ref, t_ref, cu_ref, act_ref, dh2_ref, loss_ref, dfn_ref):
        i = pl.program_id(0)

        @pl.when(i == 0)
        def _():
            loss_ref[...] = jnp.zeros_like(loss_ref)
            dfn_ref[...] = jnp.zeros_like(dfn_ref)

        h2 = h1_ref[...]
        for c in range(D_FF // FF_CHUNK):
            gsl = slice(c * FF_CHUNK, (c + 1) * FF_CHUNK)
            vsl = slice(D_FF + c * FF_CHUNK, D_FF + (c + 1) * FF_CHUNK)
            gate = _ffn_conv_cols(up_ref, halo_ref, fcw_ref, i == 0, c * FF_CHUNK)
            cu_ref[:, gsl] = gate.astype(BF16)
            val = _ffn_conv_cols(up_ref, halo_ref, fcw_ref, i == 0, D_FF + c * FF_CHUNK)
            cu_ref[:, vsl] = val.astype(BF16)
            act = (gate * _sig(gate) * val).astype(BF16)
            act_ref[:, gsl] = act
            h2 = h2 + jnp.dot(act, wd_ref[gsl, :], preferred_element_type=F32)
        r = lax.rsqrt(jnp.mean(h2 * h2, axis=-1, keepdims=True) + NORM_EPS)
        yhat = h2 * r
        fn = fn_ref[...]
        diff = yhat * fn - t_ref[...]
        loss_ref[...] += 0.5 * jnp.sum(jnp.sum(diff * diff, axis=1, keepdims=True), axis=0, keepdims=True) / D_MODEL
        dy = diff * (1.0 / D_MODEL)
        dfn_ref[...] += jnp.sum(dy * yhat, axis=0, keepdims=True)
        dyh = dy * fn
        dh2_ref[...] = r * (dyh - yhat * jnp.mean(dyh * yhat, axis=-1, keepdims=True))

    return _pcall(
        body, "ffn_down_loss", (s // tm,),
        [_rows(tm, 2 * D_FF), pl.BlockSpec((HALO, 2 * D_FF), _prev_halo_map(tm)), _full((3, 2 * D_FF)),
         _resident((D_FF, D_MODEL)), _rows(tm, D_MODEL), _full((1, D_MODEL)), _rows(tm, D_MODEL)],
        [_rows(tm, 2 * D_FF), _rows(tm, D_FF), _rows(tm, D_MODEL), _full((1, 128)), _full((1, D_MODEL))],
        [_sds((s, 2 * D_FF), BF16), _sds((s, D_FF), BF16), _sds((s, D_MODEL), F32), _sds((1, 128), F32),
         _sds((1, D_MODEL), F32)],
        (up_pre, up_pre, fcw, wdown, h1, fnorm, target))[0]


def _ffn_act_bwd(dh2, wdown, up, comm):
    s = dh2.shape[0]
    tm = _row_tile(s)

    def body(dh_ref, wd_ref, up_ref, dup_ref):
        dh = dh_ref[...].astype(BF16)
        for c in range(D_FF // FF_CHUNK):
            gsl = slice(c * FF_CHUNK, (c + 1) * FF_CHUNK)
            vsl = slice(D_FF + c * FF_CHUNK, D_FF + (c + 1) * FF_CHUNK)
            dact = lax.dot_general(dh, wd_ref[gsl, :], NT, preferred_element_type=F32)
            gate = up_ref[:, gsl].astype(F32)
            val = up_ref[:, vsl].astype(F32)
            sg = _sig(gate)
            dup_ref[:, gsl] = (dact * val * (sg * (1.0 + gate * (1.0 - sg)))).astype(BF16)
            dup_ref[:, vsl] = (dact * gate * sg).astype(BF16)

    return _pcall(
        body, "ffn_act_bwd", (s // tm,),
        [_rows(tm, D_MODEL), _resident((D_FF, D_MODEL)), _rows(tm, 2 * D_FF)],
        [_rows(tm, 2 * D_FF)], [_sds((s, 2 * D_FF), BF16)],
        (dh2, wdown, up), comm=comm)


def _conv_bwd(dy, x, w, width, chunk, name, comm):
    s = dy.shape[0]
    tm = _row_tile(s)

    def body(dy_ref, dyn_ref, x_ref, w_ref, dx_ref, dw_ref):
        i = pl.program_id(0)

        @pl.when(i == 0)
        def _():
            dw_ref[...] = jnp.zeros_like(dw_ref)

        last = i == s // tm - 1
        for c in range(width // chunk):
            sl = slice(c * chunk, (c + 1) * chunk)
            d = dy_ref[:, sl].astype(F32)
            dn = jnp.where(last, 0.0, dyn_ref[:, sl].astype(F32))
            xv = x_ref[:, sl].astype(F32)
            wv = w_ref[:, sl]
            d1, d2 = _shifts_up(d, dn, (1, 2))
            dx = wv[2:3] * d + wv[1:2] * d1 + wv[0:1] * d2
            dx_ref[:, sl] = dx.astype(BF16)
            dw_ref[0:1, sl] += jnp.sum(d2 * xv, axis=0, keepdims=True)
            dw_ref[1:2, sl] += jnp.sum(d1 * xv, axis=0, keepdims=True)
            dw_ref[2:3, sl] += jnp.sum(d * xv, axis=0, keepdims=True)

    return _pcall(
        body, name, (s // tm,),
        [_rows(tm, width), pl.BlockSpec((HALO, width), _next_halo_map(tm, s)), _rows(tm, width),
         _full((3, width))],
        [_rows(tm, width), _full((3, width))],
        [_sds((s, width), BF16), _sds((3, width), F32)],
        (dy, dy, x, w), comm=comm)


def _matmul_tn(a, b, tk, name, ts=1024):
    s, ka = a.shape
    n = b.shape[1]
    ts = min(ts, s)
    steps = s // ts

    def body(a_ref, b_ref, o_ref, acc_ref):
        j = pl.program_id(1)

        @pl.when(j == 0)
        def _():
            acc_ref[...] = jnp.zeros_like(acc_ref)

        acc_ref[...] += lax.dot_general(a_ref[...].astype(BF16), b_ref[...].astype(BF16), TN,
                                        preferred_element_type=F32)

        @pl.when(j == steps - 1)
        def _():
            o_ref[...] = acc_ref[...].astype(BF16)

    return _pcall(
        body, name, (ka // tk, steps),
        [pl.BlockSpec((ts, tk), lambda i, j: (j, i)), pl.BlockSpec((ts, n), lambda i, j: (j, 0))],
        [pl.BlockSpec((tk, n), lambda i, j: (i, 0))], [_sds((ka, n), BF16)],
        (a, b), scratch=[pltpu.VMEM((tk, n), F32)])[0][0]


def _norm_bwd_tile(xv, g, dy):
    r = lax.rsqrt(jnp.mean(xv * xv, axis=-1, keepdims=True) + NORM_EPS)
    xhat = xv * r
    dg = jnp.sum(dy * xhat, axis=0, keepdims=True)
    dyh = dy * g
    return r * (dyh - xhat * jnp.mean(dyh * xhat, axis=-1, keepdims=True)), dg


def _ffn_up_bwd(dup_pre, wup_lo, wup_hi, h1, g, dh2, comm):
    s = h1.shape[0]
    tm = _row_tile(s, 512)
    half = D_MODEL // 2

    def body(du_ref, wl_ref, wh_ref, h_ref, g_ref, dh2_ref, dh1_ref, dg_ref):
        @pl.when(pl.program_id(0) == 0)
        def _():
            dg_ref[...] = jnp.zeros_like(dg_ref)

        du = du_ref[...]
        dhn = jnp.concatenate([jnp.dot(du, wl_ref[...], preferred_element_type=F32),
                               jnp.dot(du, wh_ref[...], preferred_element_type=F32)], axis=1)
        dx, dg = _norm_bwd_tile(h_ref[...], g_ref[...], dhn)
        dg_ref[...] += dg
        dh1_ref[...] = dh2_ref[...] + dx

    return _pcall(
        body, "ffn_up_bwd", (s // tm,),
        [_rows(tm, 2 * D_FF), _resident((2 * D_FF, half)), _resident((2 * D_FF, half)), _rows(tm, D_MODEL),
         _full((1, D_MODEL)), _rows(tm, D_MODEL)],
        [_rows(tm, D_MODEL), _full((1, D_MODEL))],
        [_sds((s, D_MODEL), F32), _sds((1, D_MODEL), F32)],
        (dup_pre, wup_lo, wup_hi, h1, g, dh2), comm=comm)


def _mix_bwd(dh1, wout, gates, ap, cp, wa, wc, cbx, conv_w, comm):
    s = dh1.shape[0]
    tm = _row_tile(s)

    def body(dh_ref, wo_ref, gate_ref, ap_ref, cp_ref, wa_ref, wc_ref, cbx_ref, halo_ref, cw_ref,
             dg_ref, da_ref, dc_ref, dattn_ref, dcb_ref, dcv_ref):
        first = pl.program_id(0) == 0
        dm = lax.dot_general(dh_ref[...].astype(BF16), wo_ref[...], NT, preferred_element_type=F32)
        sa = _sig(gate_ref[:, 0:D_MODEL].astype(F32))
        sc = _sig(gate_ref[:, D_MODEL:2 * D_MODEL].astype(F32))
        da = (dm * sa).astype(BF16)
        dc = (dm * sc).astype(BF16)
        da_ref[...] = da
        dc_ref[...] = dc
        dg_ref[:, 0:D_MODEL] = (dm * ap_ref[...].astype(F32) * sa * (1.0 - sa)).astype(BF16)
        dg_ref[:, D_MODEL:2 * D_MODEL] = (dm * cp_ref[...].astype(F32) * sc * (1.0 - sc)).astype(BF16)
        dattn_ref[...] = lax.dot_general(da, wa_ref[...], NT, preferred_element_type=F32).astype(BF16)
        dconv = lax.dot_general(dc, wc_ref[...], NT, preferred_element_type=F32)
        cb, _, _, _, cv = _conv_u(cbx_ref, halo_ref, cw_ref, first)
        dcb_ref[...] = (dconv * cv).astype(BF16)
        dcv_ref[...] = (dconv * cb).astype(BF16)

    return _pcall(
        body, "mix_bwd", (s // tm,),
        [_rows(tm, D_MODEL), _full((D_MODEL, D_MODEL)), _rows(tm, GATE_W), _rows(tm, D_MODEL),
         _rows(tm, D_MODEL), _full((ATTN_W, D_MODEL)), _full((CONV_W, D_MODEL)), _rows(tm, CBX_W),
         pl.BlockSpec((HALO, CBX_W), _prev_halo_map(tm)), _full((3, CONV_W))],
        [_rows(tm, GATE_W), _rows(tm, D_MODEL), _rows(tm, D_MODEL), _rows(tm, ATTN_W),
         _rows(tm, CONV_W), _rows(tm, CONV_W)],
        [_sds((s, GATE_W), BF16), _sds((s, D_MODEL), BF16), _sds((s, D_MODEL), BF16), _sds((s, ATTN_W), BF16),
         _sds((s, CONV_W), BF16), _sds((s, CONV_W), BF16)],
        (dh1, wout, gates, ap, cp, wa, wc, cbx, cbx, conv_w), comm=comm)


def _conv_branch_bwd(dcv, cbx, conv_w):
    s = dcv.shape[0]
    tm = _row_tile(s)

    def body(d_ref, dn_ref, cbx_ref, w_ref, dcc_ref, dcx_ref, dw_ref):
        i = pl.program_id(0)

        @pl.when(i == 0)
        def _():
            dw_ref[...] = jnp.zeros_like(dw_ref)

        last = i == s // tm - 1
        cc = cbx_ref[:, CONV_W:2 * CONV_W].astype(F32)
        cx = cbx_ref[:, 2 * CONV_W:3 * CONV_W].astype(F32)
        u = cc * cx
        d = d_ref[...].astype(F32)
        dn = jnp.where(last, 0.0, dn_ref[...].astype(F32))
        d1, d2 = _shifts_up(d, dn, (1, 2))
        du = w_ref[2:3, :] * d + w_ref[1:2, :] * d1 + w_ref[0:1, :] * d2
        dcc_ref[...] = (du * cx).astype(BF16)
        dcx_ref[...] = (du * cc).astype(BF16)
        dw_ref[0:1, :] += jnp.sum(d2 * u, axis=0, keepdims=True)
        dw_ref[1:2, :] += jnp.sum(d1 * u, axis=0, keepdims=True)
        dw_ref[2:3, :] += jnp.sum(d * u, axis=0, keepdims=True)

    return _pcall(
        body, "conv_branch_bwd", (s // tm,),
        [_rows(tm, CONV_W), pl.BlockSpec((HALO, CONV_W), _next_halo_map(tm, s)), _rows(tm, CBX_W),
         _full((3, CONV_W))],
        [_rows(tm, CONV_W), _rows(tm, CONV_W), _full((3, CONV_W))],
        [_sds((s, CONV_W), BF16), _sds((s, CONV_W), BF16), _sds((3, CONV_W), F32)],
        (dcv, dcv, cbx, conv_w))[0]


def _attn_bwd(qkv, sinks, attn, lse, dattn, comm):
    s = qkv.shape[0]

    def body(sinks_ref, q_ref, kp_ref, kc_ref, vp_ref, vc_ref, o_ref, lse_ref, do_ref,
             dq_ref, dk_ref, dv_ref, ds_ref):
        n = pl.program_id(0)

        @pl.when(n == 0)
        def _():
            dk_ref[...] = jnp.zeros_like(dk_ref)
            dv_ref[...] = jnp.zeros_like(dv_ref)
            ds_ref[...] = jnp.zeros_like(ds_ref)

        mask = _attn_mask(n)
        lower = _lower_lanes()
        lane = lax.broadcasted_iota(jnp.int32, (BLOCK, 128), 1)
        lower2 = lax.broadcasted_iota(jnp.int32, (2 * BLOCK, 128), 1) < HEAD_DIM
        lane1 = lax.broadcasted_iota(jnp.int32, (1, 128), 1)
        qv, ov, dov, lsev = q_ref[...], o_ref[...], do_ref[...], lse_ref[...]
        dk_fold, dv_fold = [], []
        dsink = jnp.zeros((1, 128), F32)
        for kh in range(2):
            qs = _stack_heads(qv, kh)
            dos = _stack_heads(dov, kh)
            os_ = _stack_heads(ov, kh)
            kd, vd = _dup_kv(kp_ref, kc_ref, kh), _dup_kv(vp_ref, vc_ref, kh)
            lse = jnp.concatenate(
                [jnp.sum(jnp.where(lane == kh * 4 + g, lsev, 0.0), axis=1, keepdims=True) for g in range(4)], axis=0)
            sc = lax.dot_general(qs, kd, NT, preferred_element_type=F32) * ATTN_SCALE
            p = jnp.exp(jnp.where(mask, sc, NEG) - lse)
            dp = lax.dot_general(dos, vd, NT, preferred_element_type=F32)
            delta = jnp.sum(dos.astype(F32) * os_.astype(F32), axis=1, keepdims=True)
            dsc = (p * (dp - delta) * ATTN_SCALE).astype(BF16)
            dqs = jnp.dot(dsc, kd, preferred_element_type=F32)
            for pair in range(2):
                lo = dqs[(2 * pair) * BLOCK:(2 * pair + 1) * BLOCK]
                hi = dqs[(2 * pair + 1) * BLOCK:(2 * pair + 2) * BLOCK]
                col = (kh * 2 + pair) * 128
                dq_ref[:, col:col + 128] = jnp.where(lower, lo, hi).astype(BF16)
            dkd = lax.dot_general(dsc, qs, TN, preferred_element_type=F32)
            dvd = lax.dot_general(p.astype(BF16), dos, TN, preferred_element_type=F32)
            dk_fold.append(dkd + pltpu.roll(dkd, HEAD_DIM, axis=1))
            dv_fold.append(dvd + pltpu.roll(dvd, HEAD_DIM, axis=1))
            psink = jnp.exp(_sink_col(sinks_ref, kh) - lse) * delta
            for g in range(4):
                tot = jnp.sum(psink[g * BLOCK:(g + 1) * BLOCK], axis=0, keepdims=True)
                dsink = dsink - jnp.where(lane1 == kh * 4 + g, tot, 0.0)
        dk2 = jnp.where(lower2, dk_fold[0], dk_fold[1])
        dv2 = jnp.where(lower2, dv_fold[0], dv_fold[1])
        ds_ref[...] += dsink
        cur = pl.ds(pl.multiple_of(n * BLOCK, BLOCK), BLOCK)
        dk_ref[cur, :] += dk2[BLOCK:]
        dv_ref[cur, :] += dv2[BLOCK:]

        @pl.when(n > 0)
        def _():
            prev = pl.ds(pl.multiple_of((n - 1) * BLOCK, BLOCK), BLOCK)
            dk_ref[prev, :] += dk2[:BLOCK]
            dv_ref[prev, :] += dv2[:BLOCK]

    blk = lambda w: pl.BlockSpec((BLOCK, w), lambda n: (n, 0))
    return _pcall(
        body, "attn_bwd", (s // BLOCK,),
        [pl.BlockSpec(memory_space=pltpu.SMEM)] + _attn_specs() + [blk(ATTN_W), blk(128), blk(ATTN_W)],
        [blk(ATTN_W), _full((s, KV_W)), _full((s, KV_W)), _full((1, 128))],
        [_sds((s, ATTN_W), BF16), _sds((s, KV_W), F32), _sds((s, KV_W), F32), _sds((1, 128), F32)],
        (sinks, qkv, qkv, qkv, qkv, qkv, attn, lse, dattn), comm=comm)


def _assemble_dproj(dq, dk, dv, dcb, dcc, dcx, dgates):
    s = dq.shape[0]
    tm = _row_tile(s)
    pieces = (ATTN_W, KV_W, KV_W, CONV_W, CONV_W, CONV_W, GATE_W)

    def body(*refs):
        srcs, dp_ref, db_ref = refs[:len(pieces)], refs[-2], refs[-1]

        @pl.when(pl.program_id(0) == 0)
        def _():
            db_ref[...] = jnp.zeros_like(db_ref)

        off = 0
        for ref, w in zip(srcs, pieces):
            v = ref[...].astype(BF16)
            dp_ref[:, off:off + w] = v
            db_ref[:, off:off + w] += jnp.sum(v.astype(F32), axis=0, keepdims=True)
            off += w

    return _pcall(
        body, "assemble_dproj", (s // tm,), [_rows(tm, w) for w in pieces],
        [_rows(tm, IN_W), _full((1, IN_W))], [_sds((s, IN_W), BF16), _sds((1, IN_W), F32)],
        (dq, dk, dv, dcb, dcc, dcx, dgates))[0]


def _inproj_bwd(dproj, win_t, x, g, dh1, comm):
    s = x.shape[0]
    tm = _row_tile(s, 512)

    def body(dp_ref, w_ref, x_ref, g_ref, dh_ref, dx_ref, dg_ref):
        @pl.when(pl.program_id(0) == 0)
        def _():
            dg_ref[...] = jnp.zeros_like(dg_ref)

        dxn = jnp.dot(dp_ref[...], w_ref[...], preferred_element_type=F32)
        dx, dg = _norm_bwd_tile(x_ref[...], g_ref[...], dxn)
        dg_ref[...] += dg
        dx_ref[...] = dh_ref[...] + dx

    return _pcall(
        body, "inproj_bwd", (s // tm,),
        [_rows(tm, IN_W), _resident((IN_W, D_MODEL)), _rows(tm, D_MODEL), _full((1, D_MODEL)), _rows(tm, D_MODEL)],
        [_rows(tm, D_MODEL), _full((1, D_MODEL))],
        [_sds((s, D_MODEL), F32), _sds((1, D_MODEL), F32)],
        (dproj, win_t, x, g, dh1), comm=comm)


def _adam_math(w, g, m, v):
    m2 = ADAM_B1 * m + (1.0 - ADAM_B1) * g
    v2 = ADAM_B2 * v + (1.0 - ADAM_B2) * (g * g)
    m_hat = m2 / (1.0 - ADAM_B1 ** ADAM_STEP)
    v_hat = v2 / (1.0 - ADAM_B2 ** ADAM_STEP)
    delta = -ADAM_LR * (m_hat / (jnp.sqrt(v_hat) + ADAM_EPS) + ADAM_WD * w)
    return delta, m2, v2


def _sum_slots(ref):
    tot = ref[0].astype(F32)
    for i in range(1, N_DEV):
        tot = tot + ref[i].astype(F32)
    return tot


def _sum_adamw(parts, w, m, v, tr, name):
    r, c = w.shape

    def body(p_ref, w_ref, m_ref, v_ref, g_ref, d_ref, m2_ref, v2_ref):
        g = _sum_slots(p_ref)
        g_ref[...] = g
        d_ref[...], m2_ref[...], v2_ref[...] = _adam_math(w_ref[...], g, m_ref[...], v_ref[...])

    spec = pl.BlockSpec((tr, c), lambda i: (i, 0))
    return _pcall(body, name, (r // tr,), [pl.BlockSpec((N_DEV, tr, c), lambda i: (0, i, 0)), spec, spec, spec],
                  [spec] * 4, [_sds((r, c), F32)] * 4, (parts, w, m, v))[0]


def _sum_parts_adamw(parts, w, m, v, tr, name):
    c = w.shape[1]
    tiles = [p.shape[1] // tr for p in parts]
    starts = [sum(tiles[:k]) for k in range(len(parts))]
    n_parts = len(parts)

    def body(*refs):
        p_refs = refs[:n_parts]
        w_ref, m_ref, v_ref, g_ref, d_ref, m2_ref, v2_ref = refs[n_parts:]
        i = pl.program_id(0)
        for p_ref, st, nt in zip(p_refs, starts, tiles):
            @pl.when(jnp.logical_and(i >= st, i < st + nt))
            def _(p_ref=p_ref):
                g_ref[...] = _sum_slots(p_ref)

        d_ref[...], m2_ref[...], v2_ref[...] = _adam_math(w_ref[...], g_ref[...], m_ref[...], v_ref[...])

    def part_spec(st, nt):
        return pl.BlockSpec((N_DEV, tr, c), lambda i: (0, jnp.clip(i - st, 0, nt - 1), 0))

    spec = pl.BlockSpec((tr, c), lambda i: (i, 0))
    return _pcall(
        body, name, (sum(tiles),),
        [part_spec(st, nt) for st, nt in zip(starts, tiles)] + [spec, spec, spec],
        [spec] * 4, [_sds(w.shape, F32)] * 4, (*parts, w, m, v))[0]


def _sum_only(parts, tr, name):
    _, r, c = parts.shape

    def body(p_ref, g_ref):
        g_ref[...] = _sum_slots(p_ref)

    return _pcall(body, name, (r // tr,), [pl.BlockSpec((N_DEV, tr, c), lambda i: (0, i, 0))],
                  [pl.BlockSpec((tr, c), lambda i: (i, 0))], [_sds((r, c), F32)], (parts,))[0][0]


def _adamw(w, g, m, v, tr, name):
    r, c = w.shape

    def body(w_ref, g_ref, m_ref, v_ref, d_ref, m2_ref, v2_ref):
        d_ref[...], m2_ref[...], v2_ref[...] = _adam_math(w_ref[...], g_ref[...], m_ref[...], v_ref[...])

    spec = pl.BlockSpec((tr, c), lambda i: (i, 0))
    return _pcall(body, name, (r // tr,), [spec] * 4, [spec] * 3, [_sds((r, c), F32)] * 3, (w, g, m, v))[0]


def _pad_cols(a, c):
    return jnp.pad(a, ((0, 0), (0, c - a.shape[1])))


def _to_col_slabs(g):
    r = g.shape[0]
    return jnp.transpose(g.reshape(r, N_DEV, 128), (1, 0, 2)).reshape(N_DEV * r, 128)


def _from_col_slabs(t):
    r = t.shape[0] // N_DEV
    return jnp.transpose(t.reshape(N_DEV, r, 128), (1, 0, 2)).reshape(r, N_DEV * 128)


def _slots(t):
    return t.reshape(N_DEV, t.shape[0] // N_DEV, t.shape[1])


def kernel(x, mix_norm, w_in, b_in, sinks, conv_w, w_attn_branch, w_conv_branch, w_out, ffn_norm, w_up, ffn_conv_w, w_down, final_norm, loss_target, m_mix_norm, m_w_in, m_b_in, m_sinks, m_conv_w, m_w_attn_branch, m_w_conv_branch, m_w_out, m_ffn_norm, m_w_up, m_ffn_conv_w, m_w_down, m_final_norm, v_mix_norm, v_w_in, v_b_in, v_sinks, v_conv_w, v_w_attn_branch, v_w_conv_branch, v_w_out, v_ffn_norm, v_w_up, v_ffn_conv_w, v_w_down, v_final_norm):
    xs, tgt = x[0], loss_target[0]
    me = 4 * lax.axis_index("x") + 2 * lax.axis_index("y") + lax.axis_index("c")
    in_rows, up_rows = IN_W // N_DEV, 2 * D_FF // N_DEV

    conv_sh = jnp.concatenate([_pad_cols(ffn_conv_w[0], 768), _pad_cols(conv_w[0], 768),
                               jnp.zeros((2, 768), F32)], axis=0)
    win_sh, wup_sh = w_in[0].T.astype(BF16), w_up[0].T.astype(BF16)
    wout_sh, wdown_sh = w_out[0].astype(BF16), w_down[0].astype(BF16)
    wa_sh, wc_sh = w_attn_branch[0].astype(BF16), w_conv_branch[0].astype(BF16)

    half = D_MODEL // 2
    (win_t,) = _exchange_only(_AllGather([win_sh]), "gather_w_in")
    (xn, qkv, cbx, gates), (wa_s, wc_s, wout, conv_g) = _norm_inproj(
        xs, mix_norm, win_t, b_in, _AllGather([wa_sh, wc_sh, wout_sh, conv_sh]))
    (attn, lse), (wup_lo,) = _attn_fwd(qkv, sinks, _AllGather([wup_sh[:, :half]]))
    wa, wc = _from_col_slabs(wa_s), _from_col_slabs(wc_s)
    conv_g = conv_g.reshape(N_DEV, 8, 768)
    fcw = jnp.transpose(conv_g[:, 0:3, :up_rows], (1, 0, 2)).reshape(3, 2 * D_FF)
    cw = jnp.transpose(conv_g[:, 3:6, :CONV_W // N_DEV], (1, 0, 2)).reshape(3, CONV_W)
    (conv, ap, cp, merged, h1), (wup_hi,) = _mix_fwd(xs, cbx, gates, attn, cw, wa, wc, wout,
                                                    _AllGather([wup_sh[:, half:]]))
    (hn, up_pre), (wdown,) = _ffn_up(h1, ffn_norm, wup_lo, wup_hi, _AllGather([wdown_sh]))
    up, act, dh2, loss_p, dfn_p = _ffn_down_loss(up_pre, fcw, wdown, h1, final_norm.reshape(1, D_MODEL), tgt)

    dn_rows, q_up = D_FF // N_DEV, up_rows // 4
    g_wdown = _matmul_tn(act, dh2, FF_CHUNK, "grad_w_down")
    (dup,), (r_wdown_a,) = _ffn_act_bwd(dh2, wdown, up, _ReduceScatter([(g_wdown, 0, dn_rows // 2)]))
    (dup_pre, dfcw_p), (r_wdown_b,) = _conv_bwd(dup, up_pre, fcw, 2 * D_FF, FF_CHUNK, "ffn_conv_bwd",
                                                _ReduceScatter([(g_wdown, dn_rows // 2, dn_rows // 2)]))
    g_wup_t = _matmul_tn(dup_pre, hn, FF_CHUNK, "grad_w_up")
    (dh1, dffn_p), (r_wup_a,) = _ffn_up_bwd(dup_pre, wup_lo, wup_hi, h1, ffn_norm, dh2,
                                            _ReduceScatter([(g_wup_t, 0, 2 * q_up)]))
    g_wout = _matmul_tn(merged, dh1, D_MODEL, "grad_w_out")
    (dgates, da, dc, dattn, dcb, dcv), (r_wup_b,) = _mix_bwd(
        dh1, wout, gates, ap, cp, wa, wc, cbx, cw, _ReduceScatter([(g_wup_t, 2 * q_up, q_up)]))
    g_wa = _to_col_slabs(_matmul_tn(attn, da, ATTN_W, "grad_w_attn_branch"))
    g_wc = _to_col_slabs(_matmul_tn(conv, dc, CONV_W, "grad_w_conv_branch"))
    dcc, dcx, dcw_p = _conv_branch_bwd(dcv, cbx, cw)
    (dq, dk, dv, dsink_p), (r_wup_c, r_wout, r_wa, r_wc) = _attn_bwd(
        qkv, sinks, attn, lse, dattn,
        _ReduceScatter([(g_wup_t, 3 * q_up, q_up), (g_wout, 0, D_MODEL // N_DEV), (g_wa, 0, ATTN_W),
                        (g_wc, 0, CONV_W)]))
    dproj, dbin_p = _assemble_dproj(dq, dk, dv, dcb, dcc, dcx, dgates)
    g_win_t = _matmul_tn(dproj, xn, IN_W // 2, "grad_w_in")
    (dx, dmix_p), (r_win_a,) = _inproj_bwd(dproj, win_t, xs, mix_norm, dh1,
                                           _ReduceScatter([(g_win_t, 0, in_rows // 2)]))

    row = lambda a: _pad_cols(a.reshape(1, -1), D_MODEL)
    small = jnp.concatenate(
        [dmix_p, dffn_p, dfn_p, row(dsink_p[:, :N_HEADS]), row(loss_p[:, :1]),
         _pad_cols(dbin_p, 5 * D_MODEL).reshape(5, D_MODEL), _pad_cols(dcw_p, D_MODEL),
         _pad_cols(dfcw_p, 6 * D_MODEL).reshape(18, D_MODEL), jnp.zeros((1, D_MODEL), F32)], axis=0)
    r_win_b, r_small = _exchange_only(_ReduceScatter([(g_win_t, in_rows // 2, in_rows // 2)], [small]),
                                      "scatter_w_in_small")

    small_t = _sum_only(_slots(r_small), SMALL_ROWS, "sum_small")
    g_mix, g_ffn, g_fn = small_t[0:1], small_t[1:2], small_t[2:3]
    g_sinks, loss = small_t[3:4, :N_HEADS], small_t[4, 0]
    g_bin = small_t[5:10].reshape(1, 5 * D_MODEL)[:, :IN_W]
    g_cw_full = small_t[10:13, :CONV_W]
    g_fcw_full = small_t[13:31].reshape(3, 6 * D_MODEL)[:, :2 * D_FF]
    g_cw = lax.dynamic_slice_in_dim(g_cw_full, me * (CONV_W // N_DEV), CONV_W // N_DEV, axis=1)
    g_fcw = lax.dynamic_slice_in_dim(g_fcw_full, me * up_rows, up_rows, axis=1)

    big = {}
    big["w_in"] = tuple(t.T for t in _sum_parts_adamw(
        [_slots(r_win_a), _slots(r_win_b)], w_in[0].T, m_w_in[0].T, v_w_in[0].T, in_rows // 2, "adamw_w_in"))
    big["w_up"] = tuple(t.T for t in _sum_parts_adamw(
        [_slots(r_wup_a), _slots(r_wup_b), _slots(r_wup_c)], w_up[0].T, m_w_up[0].T, v_w_up[0].T, q_up,
        "adamw_w_up"))
    big["w_out"] = _sum_adamw(_slots(r_wout), w_out[0], m_w_out[0], v_w_out[0], 128, "adamw_w_out")
    big["w_down"] = _sum_parts_adamw([_slots(r_wdown_a), _slots(r_wdown_b)], w_down[0], m_w_down[0], v_w_down[0],
                                     dn_rows // 2, "adamw_w_down")
    big["w_attn_branch"] = _sum_adamw(_slots(r_wa), w_attn_branch[0], m_w_attn_branch[0], v_w_attn_branch[0], 256,
                                      "adamw_w_attn_branch")
    big["w_conv_branch"] = _sum_adamw(_slots(r_wc), w_conv_branch[0], m_w_conv_branch[0], v_w_conv_branch[0], 256,
                                      "adamw_w_conv_branch")

    def small_adam(w, g, m, v, name):
        shp = w.shape
        w2, m2, v2 = (t.reshape(-1, shp[-1]) for t in (w, m, v))
        d, mm, vv = _adamw(w2, g.reshape(w2.shape), m2, v2, w2.shape[0], name)
        return g.reshape(shp), d.reshape(shp), mm.reshape(shp), vv.reshape(shp)

    res = {
        "mix_norm": small_adam(mix_norm, g_mix, m_mix_norm, v_mix_norm, "adamw_mix_norm"),
        "b_in": small_adam(b_in, g_bin, m_b_in, v_b_in, "adamw_b_in"),
        "sinks": small_adam(sinks, g_sinks, m_sinks, v_sinks, "adamw_sinks"),
        "conv_w": small_adam(conv_w, g_cw, m_conv_w, v_conv_w, "adamw_conv_w"),
        "ffn_norm": small_adam(ffn_norm, g_ffn, m_ffn_norm, v_ffn_norm, "adamw_ffn_norm"),
        "ffn_conv_w": small_adam(ffn_conv_w, g_fcw, m_ffn_conv_w, v_ffn_conv_w, "adamw_ffn_conv_w"),
        "final_norm": small_adam(final_norm, g_fn, m_final_norm, v_final_norm, "adamw_final_norm"),
    }
    for name, ref_w in (("w_in", w_in), ("w_up", w_up), ("w_out", w_out), ("w_down", w_down),
                        ("w_attn_branch", w_attn_branch), ("w_conv_branch", w_conv_branch)):
        res[name] = tuple(t.reshape(ref_w.shape) for t in big[name])

    order = ["mix_norm", "w_in", "b_in", "sinks", "conv_w", "w_attn_branch", "w_conv_branch", "w_out",
             "ffn_norm", "w_up", "ffn_conv_w", "w_down", "final_norm"]
    out = [loss, dx.reshape(x.shape)]
    for k in range(4):
        out += [res[name][k] for name in order]
    return tuple(out)
```

```python
import math

import jax
import jax.numpy as jnp
from jax import lax
from jax.experimental import pallas as pl
from jax.experimental.pallas import tpu as pltpu

F32 = jnp.float32
BF16 = jnp.bfloat16
MESH = pl.DeviceIdType.MESH
N_DEV = 8

D_MODEL = 1024
HEAD_DIM = 64
N_HEADS = 8
BLOCK = 128
ATTN_W = 512
KV_W = 128
CONV_W = 512
QKV_W = ATTN_W + 2 * KV_W
CBX_W = 3 * CONV_W
GATE_W = 2 * D_MODEL
IN_W = QKV_W + CBX_W + GATE_W
D_FF = 2816
FF_CHUNK = 1408
NORM_EPS = 1e-5
ATTN_SCALE = HEAD_DIM ** -0.5
NEG = -1e30
HALO = 16

ADAM_LR = 0.001
ADAM_B1 = 0.9
ADAM_B2 = 0.999
ADAM_EPS = 1e-08
ADAM_WD = 0.01
ADAM_STEP = 10

VMEM_LIMIT = 56 * 1024 * 1024
SMALL_ROWS = 32

NT = (((1,), (1,)), ((), ()))
TN = (((0,), (0,)), ((), ()))
ANY = pl.BlockSpec(memory_space=pl.ANY)


def _sig(v):
    return 1.0 / (1.0 + jnp.exp(-v))


def _row_tile(s, pref=256):
    return pref if s % pref == 0 else s


def _shifts_down(u, halo, ks):
    ext = jnp.concatenate([halo, u], axis=0)
    return [pltpu.roll(ext, k, axis=0)[HALO:, :] for k in ks]


def _shifts_up(u, halo, ks):
    n = u.shape[0]
    ext = jnp.concatenate([u, halo], axis=0)
    return [pltpu.roll(ext, n + HALO - k, axis=0)[:n, :] for k in ks]


def _shift_matrix(n, k):
    row = lax.broadcasted_iota(jnp.int32, (n, n), 0)
    col = lax.broadcasted_iota(jnp.int32, (n, n), 1)
    return jnp.where(col == row + k, 1.0, 0.0).astype(BF16)


def _mxu_shift_up(mat, ub, halo, k):
    n = ub.shape[0]
    v = jnp.dot(mat, ub, preferred_element_type=F32)
    row = lax.broadcasted_iota(jnp.int32, (8, ub.shape[1]), 0)
    tail = v[n - 8:, :]
    for t in range(k):
        tail = jnp.where(row == 8 - k + t, halo[t:t + 1, :], tail)
    return jnp.concatenate([v[:n - 8, :], tail], axis=0)


def _prev_halo_map(tm):
    return lambda i: (jnp.maximum(i * (tm // HALO) - 1, 0), 0)


def _next_halo_map(tm, s):
    return lambda i: (jnp.minimum((i + 1) * (tm // HALO), s // HALO - 1), 0)


def _full(shape):
    return pl.BlockSpec(shape, lambda *_: (0,) * len(shape))


def _resident(shape):
    return pl.BlockSpec(shape, lambda *_: (0,) * len(shape), pipeline_mode=pl.Buffered(1))


def _rows(tm, c):
    return pl.BlockSpec((tm, c), lambda i: (i, 0))


def _sds(shape, dtype):
    return jax.ShapeDtypeStruct(shape, dtype)


def _my_place():
    x, y, c = lax.axis_index("x"), lax.axis_index("y"), lax.axis_index("c")
    return x, y, c


class _AllGather:
    def __init__(self, shards):
        self.ins = list(shards)
        n = len(shards)
        self.out_shape = [_sds((N_DEV * s.shape[0], s.shape[1]), s.dtype) for s in shards]
        self.sems = [pltpu.SemaphoreType.DMA((7 * n,)), pltpu.SemaphoreType.DMA((7 * n,)),
                     pltpu.SemaphoreType.DMA((n,))]

    def _parts(self, ins, outs, sems):
        send_sems, recv_sems, local_sems = sems
        x, y, c = _my_place()
        me, sibling = (x, y, c), (x, y, 1 - c)
        chips = [(1 - x, y), (x, 1 - y), (1 - x, 1 - y)]

        def rows(k, dev):
            r = ins[k].shape[0]
            start = pl.multiple_of((4 * dev[0] + 2 * dev[1] + dev[2]) * r, 8)
            return outs[k].at[pl.ds(start, r), :]

        def copy(k, j, block, to, src=None):
            return pltpu.make_async_remote_copy(
                src_ref=rows(k, block) if src is None else src, dst_ref=rows(k, block),
                send_sem=send_sems.at[7 * k + j], recv_sem=recv_sems.at[7 * k + j],
                device_id=to, device_id_type=MESH)

        n = len(ins)
        mine = [pltpu.make_async_copy(ins[k], rows(k, me), local_sems.at[k]) for k in range(n)]
        first = []
        for k in range(n):
            first.append(copy(k, 0, me, sibling, src=ins[k]))
            first += [copy(k, 1 + j, me, (*chip, c), src=ins[k]) for j, chip in enumerate(chips)]
        return me, sibling, chips, copy, mine, first

    def start(self, ins, outs, sems):
        _, _, _, _, mine, first = self._parts(ins, outs, sems)
        for cp in mine + first:
            cp.start()

    def finish(self, ins, outs, sems):
        me, sibling, chips, copy, mine, first = self._parts(ins, outs, sems)
        c = me[2]
        n = len(ins)
        passed = []
        for j, chip in enumerate(chips):
            for k in range(n):
                copy(k, 1 + j, (*chip, c), me).wait_recv()
                fwd = copy(k, 4 + j, (*chip, c), sibling)
                fwd.start()
                passed.append(fwd)
        for k in range(n):
            copy(k, 0, sibling, me).wait_recv()
            for j, chip in enumerate(chips):
                copy(k, 4 + j, (*chip, 1 - c), me).wait_recv()
        for cp in first + passed:
            cp.wait_send()
        for cp in mine:
            cp.wait()


class _ReduceScatter:
    def __init__(self, parts, bcast=()):
        self.parts = [(lo, cnt) for _, lo, cnt in parts]
        self.n_parts = len(parts)
        self.ins = [a for a, _, _ in parts] + list(bcast)
        self.out_shape = [_sds((N_DEV * cnt, a.shape[1]), a.dtype) for a, _, cnt in parts]
        self.out_shape += [_sds((N_DEV * b.shape[0], b.shape[1]), b.dtype) for b in bcast]
        n = len(self.ins)
        self.sems = [pltpu.SemaphoreType.DMA((7 * n,)), pltpu.SemaphoreType.DMA((7 * n,)),
                     pltpu.SemaphoreType.DMA((n,))]

    def _copies(self, ins, outs, sems):
        send_sems, recv_sems, local_sems = sems
        x, y, c = _my_place()
        me_idx = 4 * x + 2 * y + c
        remote, local = [], []
        for k in range(len(ins)):
            cnt = outs[k].shape[0] // N_DEV
            dst = outs[k].at[pl.ds(pl.multiple_of(me_idx * cnt, 8), cnt), :]
            if k < self.n_parts:
                lo, _ = self.parts[k]
                r = ins[k].shape[0] // N_DEV
                src_of = lambda idx: ins[k].at[pl.ds(pl.multiple_of(idx * r + lo, 8), cnt), :]
            else:
                src_of = lambda idx: ins[k]
            local.append(pltpu.make_async_copy(src_of(me_idx), dst, local_sems.at[k]))
            for j in range(1, N_DEV):
                peer = (x ^ (j >> 2), y ^ ((j >> 1) & 1), c ^ (j & 1))
                peer_idx = 4 * peer[0] + 2 * peer[1] + peer[2]
                remote.append(pltpu.make_async_remote_copy(
                    src_ref=src_of(peer_idx), dst_ref=dst,
                    send_sem=send_sems.at[7 * k + j - 1], recv_sem=recv_sems.at[7 * k + j - 1],
                    device_id=peer, device_id_type=MESH))
        return remote, local

    def start(self, ins, outs, sems):
        remote, local = self._copies(ins, outs, sems)
        for cp in local + remote:
            cp.start()

    def finish(self, ins, outs, sems):
        remote, local = self._copies(ins, outs, sems)
        for cp in remote:
            cp.wait_recv()
        for cp in remote:
            cp.wait_send()
        for cp in local:
            cp.wait()


class _PairExchange:
    def __init__(self, arrays):
        self.ins = list(arrays)
        n = len(arrays)
        half = [_sds((a.shape[0] // 2, a.shape[1]), a.dtype) for a in arrays]
        self.out_shape = half + half
        self.sems = [pltpu.SemaphoreType.DMA((4 * n,)), pltpu.SemaphoreType.DMA((4 * n,)),
                     pltpu.SemaphoreType.DMA((4 * n,))]

    def _copies(self, ins, outs, sems):
        send_sems, recv_sems, local_sems = sems
        x, y, c = _my_place()
        n = len(ins)
        remote, local = [], []
        for k in range(n):
            r = ins[k].shape[0] // N_DEV
            for chip in range(4):
                slot = pl.ds(chip * r, r)
                own = ins[k].at[pl.ds(pl.multiple_of((2 * chip + c) * r, 8), r), :]
                sib = ins[k].at[pl.ds(pl.multiple_of((2 * chip + 1 - c) * r, 8), r), :]
                local.append(pltpu.make_async_copy(own, outs[k].at[slot, :], local_sems.at[4 * k + chip]))
                remote.append(pltpu.make_async_remote_copy(
                    src_ref=sib, dst_ref=outs[n + k].at[slot, :],
                    send_sem=send_sems.at[4 * k + chip], recv_sem=recv_sems.at[4 * k + chip],
                    device_id=(x, y, 1 - c), device_id_type=MESH))
        return remote, local

    def start(self, ins, outs, sems):
        remote, local = self._copies(ins, outs, sems)
        for cp in local + remote:
            cp.start()

    def finish(self, ins, outs, sems):
        remote, local = self._copies(ins, outs, sems)
        for cp in remote:
            cp.wait_recv()
        for cp in remote:
            cp.wait_send()
        for cp in local:
            cp.wait()


class _ChipExchange:
    def __init__(self, arrays):
        self.ins = list(arrays)
        self.out_shape = [_sds(a.shape, a.dtype) for a in arrays]
        n = len(self.ins)
        self.sems = [pltpu.SemaphoreType.DMA((3 * n,)), pltpu.SemaphoreType.DMA((3 * n,)),
                     pltpu.SemaphoreType.DMA((n,))]

    def _copies(self, ins, outs, sems):
        send_sems, recv_sems, local_sems = sems
        x, y, c = _my_place()
        my_chip = 2 * x + y
        remote, local = [], []
        for k in range(len(ins)):
            r = ins[k].shape[0] // 4
            dst = outs[k].at[pl.ds(pl.multiple_of(my_chip * r, 8), r), :]
            local.append(pltpu.make_async_copy(ins[k].at[pl.ds(pl.multiple_of(my_chip * r, 8), r), :], dst,
                                               local_sems.at[k]))
            for j in range(1, 4):
                px, py = x ^ (j >> 1), y ^ (j & 1)
                src = ins[k].at[pl.ds(pl.multiple_of((2 * px + py) * r, 8), r), :]
                remote.append(pltpu.make_async_remote_copy(
                    src_ref=src, dst_ref=dst, send_sem=send_sems.at[3 * k + j - 1],
                    recv_sem=recv_sems.at[3 * k + j - 1], device_id=(px, py, c), device_id_type=MESH))
        return remote, local

    def start(self, ins, outs, sems):
        remote, local = self._copies(ins, outs, sems)
        for cp in local + remote:
            cp.start()

    def finish(self, ins, outs, sems):
        remote, local = self._copies(ins, outs, sems)
        for cp in remote:
            cp.wait_recv()
        for cp in remote:
            cp.wait_send()
        for cp in local:
            cp.wait()


def _pcall(body, name, grid, in_specs, out_specs, out_shape, args, scratch=(), comm=None):
    params = pltpu.CompilerParams(dimension_semantics=("arbitrary",) * len(grid), vmem_limit_bytes=VMEM_LIMIT)
    in_specs, out_specs, out_shape, scratch = list(in_specs), list(out_specs), list(out_shape), list(scratch)
    if comm is None:
        res = pl.pallas_call(body, name=name, grid=grid, in_specs=in_specs, out_specs=out_specs, out_shape=out_shape,
                             scratch_shapes=scratch, compiler_params=params)(*args)
        return list(res), []
    n_in, n_out, n_scr = len(in_specs), len(out_specs), len(scratch)
    ci, co = len(comm.ins), len(comm.out_shape)
    total = math.prod(grid)

    def carried(*refs):
        bounds = [0, n_in, n_in + ci, n_in + ci + n_out, n_in + ci + n_out + co, n_in + ci + n_out + co + n_scr]
        ins, cins, outs, couts, scr = (refs[a:b] for a, b in zip(bounds[:-1], bounds[1:]))
        sems = refs[bounds[-1]:]
        step = pl.program_id(0)
        for d in range(1, len(grid)):
            step = step * grid[d] + pl.program_id(d)

        @pl.when(step == 0)
        def _():
            comm.start(cins, couts, sems)

        body(*ins, *outs, *scr)

        @pl.when(step == total - 1)
        def _():
            comm.finish(cins, couts, sems)

    res = pl.pallas_call(
        carried, name=name, grid=grid, in_specs=in_specs + [ANY] * ci, out_specs=out_specs + [ANY] * co,
        out_shape=out_shape + comm.out_shape, scratch_shapes=scratch + comm.sems, compiler_params=params,
    )(*args, *comm.ins)
    return list(res[:n_out]), list(res[n_out:])


def _exchange_only(comm, name):
    def body(*refs):
        ci, co = len(comm.ins), len(comm.out_shape)
        comm.start(refs[:ci], refs[ci:ci + co], refs[ci + co:])
        comm.finish(refs[:ci], refs[ci:ci + co], refs[ci + co:])

    return pl.pallas_call(body, name=name, out_shape=comm.out_shape, in_specs=[ANY] * len(comm.ins),
                          out_specs=[ANY] * len(comm.out_shape), scratch_shapes=comm.sems)(*comm.ins)


def _norm_inproj(x, g, win_t, b_in, comm):
    s = x.shape[0]
    tm = _row_tile(s, 512)
    widths = (QKV_W, CBX_W, GATE_W)

    def body(x_ref, g_ref, w_ref, b_ref, xn_ref, qkv_ref, cbx_ref, gate_ref):
        xv = x_ref[...]
        r = lax.rsqrt(jnp.mean(xv * xv, axis=-1, keepdims=True) + NORM_EPS)
        xn = (xv * r * g_ref[...]).astype(BF16)
        xn_ref[...] = xn
        off = 0
        for o_ref, w in zip((qkv_ref, cbx_ref, gate_ref), widths):
            acc = lax.dot_general(xn, w_ref[off:off + w, :], NT, preferred_element_type=F32)
            o_ref[...] = (acc + b_ref[:, off:off + w]).astype(BF16)
            off += w

    return _pcall(
        body, "norm_inproj", (s // tm,),
        [_rows(tm, D_MODEL), _full((1, D_MODEL)), _resident((IN_W, D_MODEL)), _full((1, IN_W))],
        [_rows(tm, D_MODEL)] + [_rows(tm, w) for w in widths],
        [_sds((s, D_MODEL), BF16)] + [_sds((s, w), BF16) for w in widths],
        (x, g, win_t, b_in), comm=comm)


def _attn_specs():
    prev = lambda n: jnp.maximum(n - 1, 0)
    return [pl.BlockSpec((BLOCK, ATTN_W), lambda n: (n, 0)),
            pl.BlockSpec((BLOCK, KV_W), lambda n: (prev(n), ATTN_W // KV_W)),
            pl.BlockSpec((BLOCK, KV_W), lambda n: (n, ATTN_W // KV_W)),
            pl.BlockSpec((BLOCK, KV_W), lambda n: (prev(n), ATTN_W // KV_W + 1)),
            pl.BlockSpec((BLOCK, KV_W), lambda n: (n, ATTN_W // KV_W + 1))]


def _lower_lanes():
    return lax.broadcasted_iota(jnp.int32, (BLOCK, 128), 1) < HEAD_DIM


def _stack_heads(val, kh):
    lower = _lower_lanes()
    parts = []
    for g in range(4):
        h = kh * 4 + g
        blk = val[:, (h // 2) * 128:(h // 2 + 1) * 128]
        keep = lower if h % 2 == 0 else jnp.logical_not(lower)
        parts.append(jnp.where(keep, blk, jnp.zeros_like(blk)))
    return jnp.concatenate(parts, axis=0)


def _dup_kv(prev_ref, cur_ref, kh):
    t = jnp.concatenate([prev_ref[...], cur_ref[...]], axis=0).astype(F32)
    rolled = pltpu.roll(t, HEAD_DIM, axis=1)
    lower = lax.broadcasted_iota(jnp.int32, t.shape, 1) < HEAD_DIM
    dup = jnp.where(lower, t, rolled) if kh == 0 else jnp.where(lower, rolled, t)
    return dup.astype(BF16)


def _attn_mask(n):
    row = lax.broadcasted_iota(jnp.int32, (4 * BLOCK, 2 * BLOCK), 0)
    kj = lax.broadcasted_iota(jnp.int32, (4 * BLOCK, 2 * BLOCK), 1)
    dist = (row & (BLOCK - 1)) + BLOCK - kj
    band = jnp.logical_and(dist >= 0, dist < BLOCK)
    return jnp.logical_and(band, jnp.logical_or(kj >= BLOCK, n > 0))


def _sink_col(sinks_ref, kh):
    gi = lax.broadcasted_iota(jnp.int32, (4 * BLOCK, 1), 0) // BLOCK
    col = jnp.zeros((4 * BLOCK, 1), F32)
    for g in range(4):
        col = jnp.where(gi == g, sinks_ref[0, kh * 4 + g], col)
    return col


def _attn_fwd(qkv, sinks, comm):
    s = qkv.shape[0]

    def body(sinks_ref, q_ref, kp_ref, kc_ref, vp_ref, vc_ref, o_ref, lse_ref):
        n = pl.program_id(0)
        mask = _attn_mask(n)
        lower = _lower_lanes()
        lane = lax.broadcasted_iota(jnp.int32, (BLOCK, 128), 1)
        qv = q_ref[...]
        lse_out = jnp.zeros((BLOCK, 128), F32)
        for kh in range(2):
            qs = _stack_heads(qv, kh)
            kd, vd = _dup_kv(kp_ref, kc_ref, kh), _dup_kv(vp_ref, vc_ref, kh)
            sc = lax.dot_general(qs, kd, NT, preferred_element_type=F32) * ATTN_SCALE
            sc = jnp.where(mask, sc, NEG)
            sink = _sink_col(sinks_ref, kh)
            m = jnp.maximum(jnp.max(sc, axis=1, keepdims=True), sink)
            p = jnp.exp(sc - m)
            l = jnp.sum(p, axis=1, keepdims=True) + jnp.exp(sink - m)
            o = jnp.dot(p.astype(BF16), vd, preferred_element_type=F32) / l
            lse = m + jnp.log(l)
            for pair in range(2):
                lo = o[(2 * pair) * BLOCK:(2 * pair + 1) * BLOCK]
                hi = o[(2 * pair + 1) * BLOCK:(2 * pair + 2) * BLOCK]
                col = (kh * 2 + pair) * 128
                o_ref[:, col:col + 128] = jnp.where(lower, lo, hi).astype(BF16)
            for g in range(4):
                lse_out = jnp.where(lane == kh * 4 + g, lse[g * BLOCK:(g + 1) * BLOCK], lse_out)
        lse_ref[...] = lse_out

    return _pcall(
        body, "attn_fwd", (s // BLOCK,),
        [pl.BlockSpec(memory_space=pltpu.SMEM)] + _attn_specs(),
        [pl.BlockSpec((BLOCK, ATTN_W), lambda n: (n, 0)), pl.BlockSpec((BLOCK, 128), lambda n: (n, 0))],
        [_sds((s, ATTN_W), BF16), _sds((s, 128), F32)],
        (sinks, qkv, qkv, qkv, qkv, qkv), comm=comm)


def _conv_u(cbx_ref, halo_ref, w_ref, first):
    cb = cbx_ref[:, 0:CONV_W].astype(F32)
    cc = cbx_ref[:, CONV_W:2 * CONV_W].astype(F32)
    cx = cbx_ref[:, 2 * CONV_W:3 * CONV_W].astype(F32)
    u = cc * cx
    uh = halo_ref[:, CONV_W:2 * CONV_W].astype(F32) * halo_ref[:, 2 * CONV_W:3 * CONV_W].astype(F32)
    uh = jnp.where(first, 0.0, uh)
    u1, u2 = _shifts_down(u, uh, (1, 2))
    cv = w_ref[0:1, :] * u2 + w_ref[1:2, :] * u1 + w_ref[2:3, :] * u
    return cb, cc, cx, u, cv


def _mix_fwd(x, cbx, gates, attn, conv_w, wa, wc, wout, comm):
    s = x.shape[0]
    tm = _row_tile(s)

    def body(x_ref, cbx_ref, halo_ref, gate_ref, attn_ref, cw_ref, wa_ref, wc_ref, wo_ref,
             conv_ref, ap_ref, cp_ref, mg_ref, h1_ref):
        first = pl.program_id(0) == 0
        cb, _, _, _, cv = _conv_u(cbx_ref, halo_ref, cw_ref, first)
        conv = (cb * cv).astype(BF16)
        conv_ref[...] = conv
        ap = jnp.dot(attn_ref[...], wa_ref[...], preferred_element_type=F32)
        cp = jnp.dot(conv, wc_ref[...], preferred_element_type=F32)
        ap_ref[...] = ap.astype(BF16)
        cp_ref[...] = cp.astype(BF16)
        ga = gate_ref[:, 0:D_MODEL].astype(F32)
        gc = gate_ref[:, D_MODEL:2 * D_MODEL].astype(F32)
        merged = (_sig(ga) * ap + _sig(gc) * cp).astype(BF16)
        mg_ref[...] = merged
        h1_ref[...] = x_ref[...] + jnp.dot(merged, wo_ref[...], preferred_element_type=F32)

    return _pcall(
        body, "mix_fwd", (s // tm,),
        [_rows(tm, D_MODEL), _rows(tm, CBX_W), pl.BlockSpec((HALO, CBX_W), _prev_halo_map(tm)),
         _rows(tm, GATE_W), _rows(tm, ATTN_W), _full((3, CONV_W)), _full((ATTN_W, D_MODEL)),
         _full((CONV_W, D_MODEL)), _full((D_MODEL, D_MODEL))],
        [_rows(tm, CONV_W), _rows(tm, D_MODEL), _rows(tm, D_MODEL), _rows(tm, D_MODEL), _rows(tm, D_MODEL)],
        [_sds((s, CONV_W), BF16), _sds((s, D_MODEL), BF16), _sds((s, D_MODEL), BF16), _sds((s, D_MODEL), BF16),
         _sds((s, D_MODEL), F32)],
        (x, cbx, cbx, gates, attn, conv_w, wa, wc, wout), comm=comm)


def _ffn_up(h1, g, wup_lo, wup_hi, comm):
    s = h1.shape[0]
    tm = _row_tile(s, 512)
    half = D_MODEL // 2

    def body(h_ref, g_ref, wl_ref, wh_ref, hn_ref, up_ref):
        hv = h_ref[...]
        r = lax.rsqrt(jnp.mean(hv * hv, axis=-1, keepdims=True) + NORM_EPS)
        hn = (hv * r * g_ref[...]).astype(BF16)
        hn_ref[...] = hn
        for c in range(2 * D_FF // FF_CHUNK):
            sl = slice(c * FF_CHUNK, (c + 1) * FF_CHUNK)
            acc = lax.dot_general(hn[:, :half], wl_ref[sl, :], NT, preferred_element_type=F32)
            acc = acc + lax.dot_general(hn[:, half:], wh_ref[sl, :], NT, preferred_element_type=F32)
            up_ref[:, sl] = acc.astype(BF16)

    return _pcall(
        body, "ffn_up", (s // tm,),
        [_rows(tm, D_MODEL), _full((1, D_MODEL)), _resident((2 * D_FF, half)), _resident((2 * D_FF, half))],
        [_rows(tm, D_MODEL), _rows(tm, 2 * D_FF)],
        [_sds((s, D_MODEL), BF16), _sds((s, 2 * D_FF), BF16)],
        (h1, g, wup_lo, wup_hi), comm=comm)


def _ffn_conv_cols(up_ref, halo_ref, fcw_ref, first, off):
    u = up_ref[:, off:off + FF_CHUNK].astype(F32)
    uh = jnp.where(first, 0.0, halo_ref[:, off:off + FF_CHUNK].astype(F32))
    w = fcw_ref[:, off:off + FF_CHUNK]
    u1, u2 = _shifts_down(u, uh, (1, 2))
    return w[0:1] * u2 + w[1:2] * u1 + w[2:3] * u


def _ffn_down_loss(up_pre, fcw, wdown, h1, fnorm, target):
    s = h1.shape[0]
    tm = _row_tile(s)

    def body(up_ref, halo_ref, fcw_ref, wd_ref, h1_ref, fn_ref, t_ref, cu_ref, act_ref, dh2_ref, loss_ref, dfn_ref):
        i = pl.program_id(0)

        @pl.when(i == 0)
        def _():
            loss_ref[...] = jnp.zeros_like(loss_ref)
            dfn_ref[...] = jnp.zeros_like(dfn_ref)

        h2 = h1_ref[...]
        for c in range(D_FF // FF_CHUNK):
            gsl = slice(c * FF_CHUNK, (c + 1) * FF_CHUNK)
            vsl = slice(D_FF + c * FF_CHUNK, D_FF + (c + 1) * FF_CHUNK)
            gate = _ffn_conv_cols(up_ref, halo_ref, fcw_ref, i == 0, c * FF_CHUNK)
            cu_ref[:, gsl] = gate.astype(BF16)
            val = _ffn_conv_cols(up_ref, halo_ref, fcw_ref, i == 0, D_FF + c * FF_CHUNK)
            cu_ref[:, vsl] = val.astype(BF16)
            act = (gate * _sig(gate) * val).astype(BF16)
            act_ref[:, gsl] = act
            h2 = h2 + jnp.dot(act, wd_ref[gsl, :], preferred_element_type=F32)
        r = lax.rsqrt(jnp.mean(h2 * h2, axis=-1, keepdims=True) + NORM_EPS)
        yhat = h2 * r
        fn = fn_ref[...]
        diff = yhat * fn - t_ref[...]
        loss_ref[...] += 0.5 * jnp.sum(jnp.sum(diff * diff, axis=1, keepdims=True), axis=0, keepdims=True) / D_MODEL
        dy = diff * (1.0 / D_MODEL)
        dfn_ref[...] += jnp.sum(dy * yhat, axis=0, keepdims=True)
        dyh = dy * fn
        dh2_ref[...] = r * (dyh - yhat * jnp.mean(dyh * yhat, axis=-1, keepdims=True))

    return _pcall(
        body, "ffn_down_loss", (s // tm,),
        [_rows(tm, 2 * D_FF), pl.BlockSpec((HALO, 2 * D_FF), _prev_halo_map(tm)), _full((3, 2 * D_FF)),
         _resident((D_FF, D_MODEL)), _rows(tm, D_MODEL), _full((1, D_MODEL)), _rows(tm, D_MODEL)],
        [_rows(tm, 2 * D_FF), _rows(tm, D_FF), _rows(tm, D_MODEL), _full((1, 128)), _full((1, D_MODEL))],
        [_sds((s, 2 * D_FF), BF16), _sds((s, D_FF), BF16), _sds((s, D_MODEL), F32), _sds((1, 128), F32),
         _sds((1, D_MODEL), F32)],
        (up_pre, up_pre, fcw, wdown, h1, fnorm, target))[0]


def _ffn_act_bwd(dh2, wdown, up, comm):
    s = dh2.shape[0]
    tm = _row_tile(s)

    def body(dh_ref, wd_ref, up_ref, dup_ref):
        dh = dh_ref[...].astype(BF16)
        for c in range(D_FF // FF_CHUNK):
            gsl = slice(c * FF_CHUNK, (c + 1) * FF_CHUNK)
            vsl = slice(D_FF + c * FF_CHUNK, D_FF + (c + 1) * FF_CHUNK)
            dact = lax.dot_general(dh, wd_ref[gsl, :], NT, preferred_element_type=F32)
            gate = up_ref[:, gsl].astype(F32)
            val = up_ref[:, vsl].astype(F32)
            sg = _sig(gate)
            dup_ref[:, gsl] = (dact * val * (sg * (1.0 + gate * (1.0 - sg)))).astype(BF16)
            dup_ref[:, vsl] = (dact * gate * sg).astype(BF16)

    return _pcall(
        body, "ffn_act_bwd", (s // tm,),
        [_rows(tm, D_MODEL), _resident((D_FF, D_MODEL)), _rows(tm, 2 * D_FF)],
        [_rows(tm, 2 * D_FF)], [_sds((s, 2 * D_FF), BF16)],
        (dh2, wdown, up), comm=comm)


def _conv_bwd(dy, x, w, width, chunk, name, comm):
    s = dy.shape[0]
    tm = _row_tile(s)

    def body(dy_ref, dyn_ref, x_ref, w_ref, dx_ref, dw_ref):
        i = pl.program_id(0)

        @pl.when(i == 0)
        def _():
            dw_ref[...] = jnp.zeros_like(dw_ref)

        last = i == s // tm - 1
        up1, up2 = _shift_matrix(tm, 1), _shift_matrix(tm, 2)
        for c in range(width // chunk):
            sl = slice(c * chunk, (c + 1) * chunk)
            db = dy_ref[:, sl]
            d = db.astype(F32)
            dn = jnp.where(last, 0.0, dyn_ref[:, sl].astype(F32))
            xv = x_ref[:, sl].astype(F32)
            wv = w_ref[:, sl]
            d1 = _mxu_shift_up(up1, db, dn, 1)
            d2 = _mxu_shift_up(up2, db, dn, 2)
            dx = wv[2:3] * d + wv[1:2] * d1 + wv[0:1] * d2
            dx_ref[:, sl] = dx.astype(BF16)
            dw_ref[0:1, sl] += jnp.sum(d2 * xv, axis=0, keepdims=True)
            dw_ref[1:2, sl] += jnp.sum(d1 * xv, axis=0, keepdims=True)
            dw_ref[2:3, sl] += jnp.sum(d * xv, axis=0, keepdims=True)

    return _pcall(
        body, name, (s // tm,),
        [_rows(tm, width), pl.BlockSpec((HALO, width), _next_halo_map(tm, s)), _rows(tm, width),
         _full((3, width))],
        [_rows(tm, width), _full((3, width))],
        [_sds((s, width), BF16), _sds((3, width), F32)],
        (dy, dy, x, w), comm=comm)


def _matmul_tn(a, b, tk, name, ts=1024, comm=None):
    s, ka = a.shape
    n = b.shape[1]
    ts = min(ts, s)
    steps = s // ts

    def body(a_ref, b_ref, o_ref, acc_ref):
        j = pl.program_id(1)

        @pl.when(j == 0)
        def _():
            acc_ref[...] = jnp.zeros_like(acc_ref)

        acc_ref[...] += lax.dot_general(a_ref[...].astype(BF16), b_ref[...].astype(BF16), TN,
                                        preferred_element_type=F32)

        @pl.when(j == steps - 1)
        def _():
            o_ref[...] = acc_ref[...].astype(BF16)

    outs, couts = _pcall(
        body, name, (ka // tk, steps),
        [pl.BlockSpec((ts, tk), lambda i, j: (j, i)), pl.BlockSpec((ts, n), lambda i, j: (j, 0))],
        [pl.BlockSpec((tk, n), lambda i, j: (i, 0))], [_sds((ka, n), BF16)],
        (a, b), scratch=[pltpu.VMEM((tk, n), F32)], comm=comm)
    return outs[0] if comm is None else (outs[0], couts)


def _norm_bwd_tile(xv, g, dy):
    r = lax.rsqrt(jnp.mean(xv * xv, axis=-1, keepdims=True) + NORM_EPS)
    xhat = xv * r
    dg = jnp.sum(dy * xhat, axis=0, keepdims=True)
    dyh = dy * g
    return r * (dyh - xhat * jnp.mean(dyh * xhat, axis=-1, keepdims=True)), dg


def _ffn_up_bwd(dup_pre, wup_lo, wup_hi, h1, g, dh2, comm):
    s = h1.shape[0]
    tm = _row_tile(s, 512)
    half = D_MODEL // 2

    def body(du_ref, wl_ref, wh_ref, h_ref, g_ref, dh2_ref, dh1_ref, dg_ref):
        @pl.when(pl.program_id(0) == 0)
        def _():
            dg_ref[...] = jnp.zeros_like(dg_ref)

        du = du_ref[...]
        dhn = jnp.concatenate([jnp.dot(du, wl_ref[...], preferred_element_type=F32),
                               jnp.dot(du, wh_ref[...], preferred_element_type=F32)], axis=1)
        dx, dg = _norm_bwd_tile(h_ref[...], g_ref[...], dhn)
        dg_ref[...] += dg
        dh1_ref[...] = dh2_ref[...] + dx

    return _pcall(
        body, "ffn_up_bwd", (s // tm,),
        [_rows(tm, 2 * D_FF), _resident((2 * D_FF, half)), _resident((2 * D_FF, half)), _rows(tm, D_MODEL),
         _full((1, D_MODEL)), _rows(tm, D_MODEL)],
        [_rows(tm, D_MODEL), _full((1, D_MODEL))],
        [_sds((s, D_MODEL), F32), _sds((1, D_MODEL), F32)],
        (dup_pre, wup_lo, wup_hi, h1, g, dh2), comm=comm)


def _mix_bwd(dh1, wout, gates, ap, cp, wa, wc, cbx, conv_w, comm):
    s = dh1.shape[0]
    tm = _row_tile(s)

    def body(dh_ref, wo_ref, gate_ref, ap_ref, cp_ref, wa_ref, wc_ref, cbx_ref, halo_ref, cw_ref,
             dg_ref, da_ref, dc_ref, dattn_ref, dcb_ref, dcv_ref):
        first = pl.program_id(0) == 0
        dm = lax.dot_general(dh_ref[...].astype(BF16), wo_ref[...], NT, preferred_element_type=F32)
        sa = _sig(gate_ref[:, 0:D_MODEL].astype(F32))
        sc = _sig(gate_ref[:, D_MODEL:2 * D_MODEL].astype(F32))
        da = (dm * sa).astype(BF16)
        dc = (dm * sc).astype(BF16)
        da_ref[...] = da
        dc_ref[...] = dc
        dg_ref[:, 0:D_MODEL] = (dm * ap_ref[...].astype(F32) * sa * (1.0 - sa)).astype(BF16)
        dg_ref[:, D_MODEL:2 * D_MODEL] = (dm * cp_ref[...].astype(F32) * sc * (1.0 - sc)).astype(BF16)
        dattn_ref[...] = lax.dot_general(da, wa_ref[...], NT, preferred_element_type=F32).astype(BF16)
        dconv = lax.dot_general(dc, wc_ref[...], NT, preferred_element_type=F32)
        cb, _, _, _, cv = _conv_u(cbx_ref, halo_ref, cw_ref, first)
        dcb_ref[...] = (dconv * cv).astype(BF16)
        dcv_ref[...] = (dconv * cb).astype(BF16)

    return _pcall(
        body, "mix_bwd", (s // tm,),
        [_rows(tm, D_MODEL), _full((D_MODEL, D_MODEL)), _rows(tm, GATE_W), _rows(tm, D_MODEL),
         _rows(tm, D_MODEL), _full((ATTN_W, D_MODEL)), _full((CONV_W, D_MODEL)), _rows(tm, CBX_W),
         pl.BlockSpec((HALO, CBX_W), _prev_halo_map(tm)), _full((3, CONV_W))],
        [_rows(tm, GATE_W), _rows(tm, D_MODEL), _rows(tm, D_MODEL), _rows(tm, ATTN_W),
         _rows(tm, CONV_W), _rows(tm, CONV_W)],
        [_sds((s, GATE_W), BF16), _sds((s, D_MODEL), BF16), _sds((s, D_MODEL), BF16), _sds((s, ATTN_W), BF16),
         _sds((s, CONV_W), BF16), _sds((s, CONV_W), BF16)],
        (dh1, wout, gates, ap, cp, wa, wc, cbx, cbx, conv_w), comm=comm)


def _conv_branch_bwd(dcv, cbx, conv_w):
    s = dcv.shape[0]
    tm = _row_tile(s)

    def body(d_ref, dn_ref, cbx_ref, w_ref, dcc_ref, dcx_ref, dw_ref):
        i = pl.program_id(0)

        @pl.when(i == 0)
        def _():
            dw_ref[...] = jnp.zeros_like(dw_ref)

        last = i == s // tm - 1
        cc = cbx_ref[:, CONV_W:2 * CONV_W].astype(F32)
        cx = cbx_ref[:, 2 * CONV_W:3 * CONV_W].astype(F32)
        u = cc * cx
        d = d_ref[...].astype(F32)
        dn = jnp.where(last, 0.0, dn_ref[...].astype(F32))
        d1, d2 = _shifts_up(d, dn, (1, 2))
        du = w_ref[2:3, :] * d + w_ref[1:2, :] * d1 + w_ref[0:1, :] * d2
        dcc_ref[...] = (du * cx).astype(BF16)
        dcx_ref[...] = (du * cc).astype(BF16)
        dw_ref[0:1, :] += jnp.sum(d2 * u, axis=0, keepdims=True)
        dw_ref[1:2, :] += jnp.sum(d1 * u, axis=0, keepdims=True)
        dw_ref[2:3, :] += jnp.sum(d * u, axis=0, keepdims=True)

    return _pcall(
        body, "conv_branch_bwd", (s // tm,),
        [_rows(tm, CONV_W), pl.BlockSpec((HALO, CONV_W), _next_halo_map(tm, s)), _rows(tm, CBX_W),
         _full((3, CONV_W))],
        [_rows(tm, CONV_W), _rows(tm, CONV_W), _full((3, CONV_W))],
        [_sds((s, CONV_W), BF16), _sds((s, CONV_W), BF16), _sds((3, CONV_W), F32)],
        (dcv, dcv, cbx, conv_w))[0]


def _attn_bwd(qkv, sinks, attn, lse, dattn, comm):
    s = qkv.shape[0]

    def body(sinks_ref, q_ref, kp_ref, kc_ref, vp_ref, vc_ref, o_ref, lse_ref, do_ref,
             dq_ref, dk_ref, dv_ref, ds_ref):
        n = pl.program_id(0)

        @pl.when(n == 0)
        def _():
            dk_ref[...] = jnp.zeros_like(dk_ref)
            dv_ref[...] = jnp.zeros_like(dv_ref)
            ds_ref[...] = jnp.zeros_like(ds_ref)

        mask = _attn_mask(n)
        lower = _lower_lanes()
        lane = lax.broadcasted_iota(jnp.int32, (BLOCK, 128), 1)
        lower2 = lax.broadcasted_iota(jnp.int32, (2 * BLOCK, 128), 1) < HEAD_DIM
        lane1 = lax.broadcasted_iota(jnp.int32, (1, 128), 1)
        qv, ov, dov, lsev = q_ref[...], o_ref[...], do_ref[...], lse_ref[...]
        dk_fold, dv_fold = [], []
        dsink = jnp.zeros((1, 128), F32)
        for kh in range(2):
            qs = _stack_heads(qv, kh)
            dos = _stack_heads(dov, kh)
            os_ = _stack_heads(ov, kh)
            kd, vd = _dup_kv(kp_ref, kc_ref, kh), _dup_kv(vp_ref, vc_ref, kh)
            lse = jnp.concatenate(
                [jnp.sum(jnp.where(lane == kh * 4 + g, lsev, 0.0), axis=1, keepdims=True) for g in range(4)], axis=0)
            sc = lax.dot_general(qs, kd, NT, preferred_element_type=F32) * ATTN_SCALE
            p = jnp.exp(jnp.where(mask, sc, NEG) - lse)
            dp = lax.dot_general(dos, vd, NT, preferred_element_type=F32)
            delta = jnp.sum(dos.astype(F32) * os_.astype(F32), axis=1, keepdims=True)
            dsc = (p * (dp - delta) * ATTN_SCALE).astype(BF16)
            dqs = jnp.dot(dsc, kd, preferred_element_type=F32)
            for pair in range(2):
                lo = dqs[(2 * pair) * BLOCK:(2 * pair + 1) * BLOCK]
                hi = dqs[(2 * pair + 1) * BLOCK:(2 * pair + 2) * BLOCK]
                col = (kh * 2 + pair) * 128
                dq_ref[:, col:col + 128] = jnp.where(lower, lo, hi).astype(BF16)
            dkd = lax.dot_general(dsc, qs, TN, preferred_element_type=F32)
            dvd = lax.dot_general(p.astype(BF16), dos, TN, preferred_element_type=F32)
            dk_fold.append(dkd + pltpu.roll(dkd, HEAD_DIM, axis=1))
            dv_fold.append(dvd + pltpu.roll(dvd, HEAD_DIM, axis=1))
            psink = jnp.exp(_sink_col(sinks_ref, kh) - lse) * delta
            for g in range(4):
                tot = jnp.sum(psink[g * BLOCK:(g + 1) * BLOCK], axis=0, keepdims=True)
                dsink = dsink - jnp.where(lane1 == kh * 4 + g, tot, 0.0)
        dk2 = jnp.where(lower2, dk_fold[0], dk_fold[1])
        dv2 = jnp.where(lower2, dv_fold[0], dv_fold[1])
        ds_ref[...] += dsink
        cur = pl.ds(pl.multiple_of(n * BLOCK, BLOCK), BLOCK)
        dk_ref[cur, :] += dk2[BLOCK:]
        dv_ref[cur, :] += dv2[BLOCK:]

        @pl.when(n > 0)
        def _():
            prev = pl.ds(pl.multiple_of((n - 1) * BLOCK, BLOCK), BLOCK)
            dk_ref[prev, :] += dk2[:BLOCK]
            dv_ref[prev, :] += dv2[:BLOCK]

    blk = lambda w: pl.BlockSpec((BLOCK, w), lambda n: (n, 0))
    return _pcall(
        body, "attn_bwd", (s // BLOCK,),
        [pl.BlockSpec(memory_space=pltpu.SMEM)] + _attn_specs() + [blk(ATTN_W), blk(128), blk(ATTN_W)],
        [blk(ATTN_W), _full((s, KV_W)), _full((s, KV_W)), _full((1, 128))],
        [_sds((s, ATTN_W), BF16), _sds((s, KV_W), F32), _sds((s, KV_W), F32), _sds((1, 128), F32)],
        (sinks, qkv, qkv, qkv, qkv, qkv, attn, lse, dattn), comm=comm)


def _assemble_dproj(dq, dk, dv, dcb, dcc, dcx, dgates):
    s = dq.shape[0]
    tm = _row_tile(s)
    pieces = (ATTN_W, KV_W, KV_W, CONV_W, CONV_W, CONV_W, GATE_W)

    def body(*refs):
        srcs, dp_ref, db_ref = refs[:len(pieces)], refs[-2], refs[-1]

        @pl.when(pl.program_id(0) == 0)
        def _():
            db_ref[...] = jnp.zeros_like(db_ref)

        off = 0
        for ref, w in zip(srcs, pieces):
            v = ref[...].astype(BF16)
            dp_ref[:, off:off + w] = v
            db_ref[:, off:off + w] += jnp.sum(v.astype(F32), axis=0, keepdims=True)
            off += w

    return _pcall(
        body, "assemble_dproj", (s // tm,), [_rows(tm, w) for w in pieces],
        [_rows(tm, IN_W), _full((1, IN_W))], [_sds((s, IN_W), BF16), _sds((1, IN_W), F32)],
        (dq, dk, dv, dcb, dcc, dcx, dgates))[0]


def _inproj_bwd(dproj, win_t, x, g, dh1, comm):
    s = x.shape[0]
    tm = _row_tile(s, 512)

    def body(dp_ref, w_ref, x_ref, g_ref, dh_ref, dx_ref, dg_ref):
        @pl.when(pl.program_id(0) == 0)
        def _():
            dg_ref[...] = jnp.zeros_like(dg_ref)

        dxn = jnp.dot(dp_ref[...], w_ref[...], preferred_element_type=F32)
        dx, dg = _norm_bwd_tile(x_ref[...], g_ref[...], dxn)
        dg_ref[...] += dg
        dx_ref[...] = dh_ref[...] + dx

    return _pcall(
        body, "inproj_bwd", (s // tm,),
        [_rows(tm, IN_W), _resident((IN_W, D_MODEL)), _rows(tm, D_MODEL), _full((1, D_MODEL)), _rows(tm, D_MODEL)],
        [_rows(tm, D_MODEL), _full((1, D_MODEL))],
        [_sds((s, D_MODEL), F32), _sds((1, D_MODEL), F32)],
        (dproj, win_t, x, g, dh1), comm=comm)


def _adam_math(w, g, m, v):
    m2 = ADAM_B1 * m + (1.0 - ADAM_B1) * g
    v2 = ADAM_B2 * v + (1.0 - ADAM_B2) * (g * g)
    m_hat = m2 / (1.0 - ADAM_B1 ** ADAM_STEP)
    v_hat = v2 / (1.0 - ADAM_B2 ** ADAM_STEP)
    delta = -ADAM_LR * (m_hat / (jnp.sqrt(v_hat) + ADAM_EPS) + ADAM_WD * w)
    return delta, m2, v2


def _sum_slots(ref):
    tot = ref[0].astype(F32)
    for i in range(1, ref.shape[0]):
        tot = tot + ref[i].astype(F32)
    return tot


def _pair_add(mine, theirs, tr, name):
    r, c = mine.shape

    def body(a_ref, b_ref, o_ref):
        o_ref[...] = (a_ref[...].astype(F32) + b_ref[...].astype(F32)).astype(BF16)

    spec = pl.BlockSpec((tr, c), lambda i: (i, 0))
    return _pcall(body, name, (r // tr,), [spec, spec], [spec], [_sds((r, c), BF16)], (mine, theirs))[0][0]


def _sum_adamw(parts, w, m, v, tr, name):
    r, c = w.shape

    def body(p_ref, w_ref, m_ref, v_ref, g_ref, d_ref, m2_ref, v2_ref):
        g = _sum_slots(p_ref)
        g_ref[...] = g
        d_ref[...], m2_ref[...], v2_ref[...] = _adam_math(w_ref[...], g, m_ref[...], v_ref[...])

    spec = pl.BlockSpec((tr, c), lambda i: (i, 0))
    return _pcall(body, name, (r // tr,), [pl.BlockSpec((N_DEV, tr, c), lambda i: (0, i, 0)), spec, spec, spec],
                  [spec] * 4, [_sds((r, c), F32)] * 4, (parts, w, m, v))[0]


def _sum_parts_adamw(parts, w, m, v, tr, name):
    c = w.shape[1]
    tiles = [p.shape[1] // tr for p in parts]
    starts = [sum(tiles[:k]) for k in range(len(parts))]
    n_parts = len(parts)

    def body(*refs):
        p_refs = refs[:n_parts]
        w_ref, m_ref, v_ref, g_ref, d_ref, m2_ref, v2_ref = refs[n_parts:]
        i = pl.program_id(0)
        for p_ref, st, nt in zip(p_refs, starts, tiles):
            @pl.when(jnp.logical_and(i >= st, i < st + nt))
            def _(p_ref=p_ref):
                g_ref[...] = _sum_slots(p_ref)

        d_ref[...], m2_ref[...], v2_ref[...] = _adam_math(w_ref[...], g_ref[...], m_ref[...], v_ref[...])

    def part_spec(p, st, nt):
        return pl.BlockSpec((p.shape[0], tr, c), lambda i: (0, jnp.clip(i - st, 0, nt - 1), 0))

    spec = pl.BlockSpec((tr, c), lambda i: (i, 0))
    return _pcall(
        body, name, (sum(tiles),),
        [part_spec(p, st, nt) for p, st, nt in zip(parts, starts, tiles)] + [spec, spec, spec],
        [spec] * 4, [_sds(w.shape, F32)] * 4, (*parts, w, m, v))[0]


def _sum_only(parts, tr, name):
    _, r, c = parts.shape

    def body(p_ref, g_ref):
        g_ref[...] = _sum_slots(p_ref)

    return _pcall(body, name, (r // tr,), [pl.BlockSpec((N_DEV, tr, c), lambda i: (0, i, 0))],
                  [pl.BlockSpec((tr, c), lambda i: (i, 0))], [_sds((r, c), F32)], (parts,))[0][0]


def _adamw(w, g, m, v, tr, name):
    r, c = w.shape

    def body(w_ref, g_ref, m_ref, v_ref, d_ref, m2_ref, v2_ref):
        d_ref[...], m2_ref[...], v2_ref[...] = _adam_math(w_ref[...], g_ref[...], m_ref[...], v_ref[...])

    spec = pl.BlockSpec((tr, c), lambda i: (i, 0))
    return _pcall(body, name, (r // tr,), [spec] * 4, [spec] * 3, [_sds((r, c), F32)] * 3, (w, g, m, v))[0]


def _pad_cols(a, c):
    return jnp.pad(a, ((0, 0), (0, c - a.shape[1])))


def _to_col_slabs(g):
    r = g.shape[0]
    return jnp.transpose(g.reshape(r, N_DEV, 128), (1, 0, 2)).reshape(N_DEV * r, 128)


def _from_col_slabs(t):
    r = t.shape[0] // N_DEV
    return jnp.transpose(t.reshape(N_DEV, r, 128), (1, 0, 2)).reshape(r, N_DEV * 128)


def _slots(t):
    return t.reshape(N_DEV, t.shape[0] // N_DEV, t.shape[1])


def kernel(x, mix_norm, w_in, b_in, sinks, conv_w, w_attn_branch, w_conv_branch, w_out, ffn_norm, w_up, ffn_conv_w, w_down, final_norm, loss_target, m_mix_norm, m_w_in, m_b_in, m_sinks, m_conv_w, m_w_attn_branch, m_w_conv_branch, m_w_out, m_ffn_norm, m_w_up, m_ffn_conv_w, m_w_down, m_final_norm, v_mix_norm, v_w_in, v_b_in, v_sinks, v_conv_w, v_w_attn_branch, v_w_conv_branch, v_w_out, v_ffn_norm, v_w_up, v_ffn_conv_w, v_w_down, v_final_norm):
    xs, tgt = x[0], loss_target[0]
    me = 4 * lax.axis_index("x") + 2 * lax.axis_index("y") + lax.axis_index("c")
    in_rows, up_rows = IN_W // N_DEV, 2 * D_FF // N_DEV

    conv_sh = jnp.concatenate([_pad_cols(ffn_conv_w[0], 768), _pad_cols(conv_w[0], 768),
                               jnp.zeros((2, 768), F32)], axis=0)
    win_sh, wup_sh = w_in[0].T.astype(BF16), w_up[0].T.astype(BF16)
    wout_sh, wdown_sh = w_out[0].astype(BF16), w_down[0].astype(BF16)
    wa_sh, wc_sh = w_attn_branch[0].astype(BF16), w_conv_branch[0].astype(BF16)

    half = D_MODEL // 2
    (win_t,) = _exchange_only(_AllGather([win_sh]), "gather_w_in")
    (xn, qkv, cbx, gates), (wa_s, wc_s, wout, conv_g) = _norm_inproj(
        xs, mix_norm, win_t, b_in, _AllGather([wa_sh, wc_sh, wout_sh, conv_sh]))
    (attn, lse), (wup_lo,) = _attn_fwd(qkv, sinks, _AllGather([wup_sh[:, :half]]))
    wa, wc = _from_col_slabs(wa_s), _from_col_slabs(wc_s)
    conv_g = conv_g.reshape(N_DEV, 8, 768)
    fcw = jnp.transpose(conv_g[:, 0:3, :up_rows], (1, 0, 2)).reshape(3, 2 * D_FF)
    cw = jnp.transpose(conv_g[:, 3:6, :CONV_W // N_DEV], (1, 0, 2)).reshape(3, CONV_W)
    (conv, ap, cp, merged, h1), (wup_hi,) = _mix_fwd(xs, cbx, gates, attn, cw, wa, wc, wout,
                                                    _AllGather([wup_sh[:, half:]]))
    (hn, up_pre), (wdown,) = _ffn_up(h1, ffn_norm, wup_lo, wup_hi, _AllGather([wdown_sh]))
    up, act, dh2, loss_p, dfn_p = _ffn_down_loss(up_pre, fcw, wdown, h1, final_norm.reshape(1, D_MODEL), tgt)

    dn_rows, q_up = D_FF // N_DEV, up_rows // 4
    g_wdown = _matmul_tn(act, dh2, FF_CHUNK, "grad_w_down")
    (dup,), (r_wdown_a,) = _ffn_act_bwd(dh2, wdown, up, _ReduceScatter([(g_wdown, 0, dn_rows // 2)]))
    (dup_pre, dfcw_p), (r_wdown_b,) = _conv_bwd(dup, up_pre, fcw, 2 * D_FF, FF_CHUNK, "ffn_conv_bwd",
                                                _ReduceScatter([(g_wdown, dn_rows // 2, dn_rows // 2)]))
    g_wup_t = _matmul_tn(dup_pre, hn, FF_CHUNK, "grad_w_up")
    (dh1, dffn_p), (r_wup_a,) = _ffn_up_bwd(dup_pre, wup_lo, wup_hi, h1, ffn_norm, dh2,
                                            _ReduceScatter([(g_wup_t, 0, q_up)]))
    g_wout = _matmul_tn(merged, dh1, D_MODEL, "grad_w_out")
    (dgates, da, dc, dattn, dcb, dcv), (r_wup_b,) = _mix_bwd(
        dh1, wout, gates, ap, cp, wa, wc, cbx, cw, _ReduceScatter([(g_wup_t, q_up, q_up)]))
    g_wa = _to_col_slabs(_matmul_tn(attn, da, ATTN_W, "grad_w_attn_branch"))
    g_wc = _to_col_slabs(_matmul_tn(conv, dc, CONV_W, "grad_w_conv_branch"))
    dcc, dcx, dcw_p = _conv_branch_bwd(dcv, cbx, cw)
    (dq, dk, dv, dsink_p), (r_wup_c, r_wout, r_wa, r_wc) = _attn_bwd(
        qkv, sinks, attn, lse, dattn,
        _ReduceScatter([(g_wup_t, 2 * q_up, q_up), (g_wout, 0, D_MODEL // N_DEV), (g_wa, 0, ATTN_W),
                        (g_wc, 0, CONV_W)]))
    dproj, dbin_p = _assemble_dproj(dq, dk, dv, dcb, dcc, dcx, dgates)
    g_win_t, (r_wup_d,) = _matmul_tn(dproj, xn, IN_W // 2, "grad_w_in",
                                     comm=_ReduceScatter([(g_wup_t, 3 * q_up, q_up)]))
    win_mine, win_theirs = _exchange_only(_PairExchange([g_win_t]), "pair_exchange_w_in")
    q_win = _pair_add(win_mine, win_theirs, in_rows // 2, "pair_add_w_in")
    (dx, dmix_p), (r_win,) = _inproj_bwd(dproj, win_t, xs, mix_norm, dh1, _ChipExchange([q_win]))

    row = lambda a: _pad_cols(a.reshape(1, -1), D_MODEL)
    small = jnp.concatenate(
        [dmix_p, dffn_p, dfn_p, row(dsink_p[:, :N_HEADS]), row(loss_p[:, :1]),
         _pad_cols(dbin_p, 5 * D_MODEL).reshape(5, D_MODEL), _pad_cols(dcw_p, D_MODEL),
         _pad_cols(dfcw_p, 6 * D_MODEL).reshape(18, D_MODEL), jnp.zeros((1, D_MODEL), F32)], axis=0)
    (r_small,) = _exchange_only(_ReduceScatter([], [small]), "exchange_small")

    small_t = _sum_only(_slots(r_small), SMALL_ROWS, "sum_small")
    g_mix, g_ffn, g_fn = small_t[0:1], small_t[1:2], small_t[2:3]
    g_sinks, loss = small_t[3:4, :N_HEADS], small_t[4, 0]
    g_bin = small_t[5:10].reshape(1, 5 * D_MODEL)[:, :IN_W]
    g_cw_full = small_t[10:13, :CONV_W]
    g_fcw_full = small_t[13:31].reshape(3, 6 * D_MODEL)[:, :2 * D_FF]
    g_cw = lax.dynamic_slice_in_dim(g_cw_full, me * (CONV_W // N_DEV), CONV_W // N_DEV, axis=1)
    g_fcw = lax.dynamic_slice_in_dim(g_fcw_full, me * up_rows, up_rows, axis=1)

    big = {}
    big["w_in"] = tuple(t.T for t in _sum_parts_adamw(
        [r_win.reshape(4, in_rows, D_MODEL)], w_in[0].T, m_w_in[0].T, v_w_in[0].T, in_rows // 2, "adamw_w_in"))
    big["w_up"] = tuple(t.T for t in _sum_parts_adamw(
        [_slots(r_wup_a), _slots(r_wup_b), _slots(r_wup_c), _slots(r_wup_d)], w_up[0].T, m_w_up[0].T, v_w_up[0].T, q_up,
        "adamw_w_up"))
    big["w_out"] = _sum_adamw(_slots(r_wout), w_out[0], m_w_out[0], v_w_out[0], 128, "adamw_w_out")
    big["w_down"] = _sum_parts_adamw([_slots(r_wdown_a), _slots(r_wdown_b)], w_down[0], m_w_down[0], v_w_down[0],
                                     dn_rows // 2, "adamw_w_down")
    big["w_attn_branch"] = _sum_adamw(_slots(r_wa), w_attn_branch[0], m_w_attn_branch[0], v_w_attn_branch[0], 256,
                                      "adamw_w_attn_branch")
    big["w_conv_branch"] = _sum_adamw(_slots(r_wc), w_conv_branch[0], m_w_conv_branch[0], v_w_conv_branch[0], 256,
                                      "adamw_w_conv_branch")

    def small_adam(w, g, m, v, name):
        shp = w.shape
        w2, m2, v2 = (t.reshape(-1, shp[-1]) for t in (w, m, v))
        d, mm, vv = _adamw(w2, g.reshape(w2.shape), m2, v2, w2.shape[0], name)
        return g.reshape(shp), d.reshape(shp), mm.reshape(shp), vv.reshape(shp)

    res = {
        "mix_norm": small_adam(mix_norm, g_mix, m_mix_norm, v_mix_norm, "adamw_mix_norm"),
        "b_in": small_adam(b_in, g_bin, m_b_in, v_b_in, "adamw_b_in"),
        "sinks": small_adam(sinks, g_sinks, m_sinks, v_sinks, "adamw_sinks"),
        "conv_w": small_adam(conv_w, g_cw, m_conv_w, v_conv_w, "adamw_conv_w"),
        "ffn_norm": small_adam(ffn_norm, g_ffn, m_ffn_norm, v_ffn_norm, "adamw_ffn_norm"),
        "ffn_conv_w": small_adam(ffn_conv_w, g_fcw, m_ffn_conv_w, v_ffn_conv_w, "adamw_ffn_conv_w"),
        "final_norm": small_adam(final_norm, g_fn, m_final_norm, v_final_norm, "adamw_final_norm"),
    }
    for name, ref_w in (("w_in", w_in), ("w_up", w_up), ("w_out", w_out), ("w_down", w_down),
                        ("w_attn_branch", w_attn_branch), ("w_conv_branch", w_conv_branch)):
        res[name] = tuple(t.reshape(ref_w.shape) for t in big[name])

    order = ["mix_norm", "w_in", "b_in", "sinks", "conv_w", "w_attn_branch", "w_conv_branch", "w_out",
             "ffn_norm", "w_up", "ffn_conv_w", "w_down", "final_norm"]
    out = [loss, dx.reshape(x.shape)]
    for k in range(4):
        out += [res[name][k] for name in order]
    return tuple(out)
```

```python
import math

import jax
import jax.numpy as jnp
from jax import lax
from jax.experimental import pallas as pl
from jax.experimental.pallas import tpu as pltpu

F32 = jnp.float32
BF16 = jnp.bfloat16
MESH = pl.DeviceIdType.MESH
N_DEV = 8

D_MODEL = 1024
HEAD_DIM = 64
N_HEADS = 8
BLOCK = 128
ATTN_W = 512
KV_W = 128
CONV_W = 512
QKV_W = ATTN_W + 2 * KV_W
CBX_W = 3 * CONV_W
GATE_W = 2 * D_MODEL
IN_W = QKV_W + CBX_W + GATE_W
D_FF = 2816
FF_CHUNK = 1408
NORM_EPS = 1e-5
ATTN_SCALE = HEAD_DIM ** -0.5
NEG = -1e30
HALO = 16

ADAM_LR = 0.001
ADAM_B1 = 0.9
ADAM_B2 = 0.999
ADAM_EPS = 1e-08
ADAM_WD = 0.01
ADAM_STEP = 10

VMEM_LIMIT = 56 * 1024 * 1024
SMALL_ROWS = 32

NT = (((1,), (1,)), ((), ()))
TN = (((0,), (0,)), ((), ()))
ANY = pl.BlockSpec(memory_space=pl.ANY)


def _sig(v):
    return 1.0 / (1.0 + jnp.exp(-v))


def _row_tile(s, pref=256):
    return pref if s % pref == 0 else s


def _shifts_down(u, halo, ks):
    ext = jnp.concatenate([halo, u], axis=0)
    return [pltpu.roll(ext, k, axis=0)[HALO:, :] for k in ks]


def _shifts_up(u, halo, ks):
    n = u.shape[0]
    ext = jnp.concatenate([u, halo], axis=0)
    return [pltpu.roll(ext, n + HALO - k, axis=0)[:n, :] for k in ks]


def _shift_matrix(n, k):
    row = lax.broadcasted_iota(jnp.int32, (n, n), 0)
    col = lax.broadcasted_iota(jnp.int32, (n, n), 1)
    return jnp.where(col == row + k, 1.0, 0.0).astype(BF16)


def _mxu_shift_up(mat, ub, halo, k):
    n = ub.shape[0]
    v = jnp.dot(mat, ub, preferred_element_type=F32)
    row = lax.broadcasted_iota(jnp.int32, (8, ub.shape[1]), 0)
    tail = v[n - 8:, :]
    for t in range(k):
        tail = jnp.where(row == 8 - k + t, halo[t:t + 1, :], tail)
    return jnp.concatenate([v[:n - 8, :], tail], axis=0)


def _prev_halo_map(tm):
    return lambda i: (jnp.maximum(i * (tm // HALO) - 1, 0), 0)


def _next_halo_map(tm, s):
    return lambda i: (jnp.minimum((i + 1) * (tm // HALO), s // HALO - 1), 0)


def _full(shape):
    return pl.BlockSpec(shape, lambda *_: (0,) * len(shape))


def _resident(shape):
    return pl.BlockSpec(shape, lambda *_: (0,) * len(shape), pipeline_mode=pl.Buffered(1))


def _rows(tm, c):
    return pl.BlockSpec((tm, c), lambda i: (i, 0))


def _sds(shape, dtype):
    return jax.ShapeDtypeStruct(shape, dtype)


def _my_place():
    x, y, c = lax.axis_index("x"), lax.axis_index("y"), lax.axis_index("c")
    return x, y, c


class _AllGather:
    def __init__(self, shards):
        self.ins = list(shards)
        n = len(shards)
        self.out_shape = [_sds((N_DEV * s.shape[0], s.shape[1]), s.dtype) for s in shards]
        self.sems = [pltpu.SemaphoreType.DMA((7 * n,)), pltpu.SemaphoreType.DMA((7 * n,)),
                     pltpu.SemaphoreType.DMA((n,))]

    def _parts(self, ins, outs, sems):
        send_sems, recv_sems, local_sems = sems
        x, y, c = _my_place()
        me, sibling = (x, y, c), (x, y, 1 - c)
        chips = [(1 - x, y), (x, 1 - y), (1 - x, 1 - y)]

        def rows(k, dev):
            r = ins[k].shape[0]
            start = pl.multiple_of((4 * dev[0] + 2 * dev[1] + dev[2]) * r, 8)
            return outs[k].at[pl.ds(start, r), :]

        def copy(k, j, block, to, src=None):
            return pltpu.make_async_remote_copy(
                src_ref=rows(k, block) if src is None else src, dst_ref=rows(k, block),
                send_sem=send_sems.at[7 * k + j], recv_sem=recv_sems.at[7 * k + j],
                device_id=to, device_id_type=MESH)

        n = len(ins)
        mine = [pltpu.make_async_copy(ins[k], rows(k, me), local_sems.at[k]) for k in range(n)]
        first = []
        for k in range(n):
            first.append(copy(k, 0, me, sibling, src=ins[k]))
            first += [copy(k, 1 + j, me, (*chip, c), src=ins[k]) for j, chip in enumerate(chips)]
        return me, sibling, chips, copy, mine, first

    def start(self, ins, outs, sems):
        _, _, _, _, mine, first = self._parts(ins, outs, sems)
        for cp in mine + first:
            cp.start()

    def finish(self, ins, outs, sems):
        me, sibling, chips, copy, mine, first = self._parts(ins, outs, sems)
        c = me[2]
        n = len(ins)
        passed = []
        for j, chip in enumerate(chips):
            for k in range(n):
                copy(k, 1 + j, (*chip, c), me).wait_recv()
                fwd = copy(k, 4 + j, (*chip, c), sibling)
                fwd.start()
                passed.append(fwd)
        for k in range(n):
            copy(k, 0, sibling, me).wait_recv()
            for j, chip in enumerate(chips):
                copy(k, 4 + j, (*chip, 1 - c), me).wait_recv()
        for cp in first + passed:
            cp.wait_send()
        for cp in mine:
            cp.wait()


class _ReduceScatter:
    def __init__(self, parts, bcast=()):
        self.parts = [(lo, cnt) for _, lo, cnt in parts]
        self.n_parts = len(parts)
        self.ins = [a for a, _, _ in parts] + list(bcast)
        self.out_shape = [_sds((N_DEV * cnt, a.shape[1]), a.dtype) for a, _, cnt in parts]
        self.out_shape += [_sds((N_DEV * b.shape[0], b.shape[1]), b.dtype) for b in bcast]
        n = len(self.ins)
        self.sems = [pltpu.SemaphoreType.DMA((7 * n,)), pltpu.SemaphoreType.DMA((7 * n,)),
                     pltpu.SemaphoreType.DMA((n,))]

    def _copies(self, ins, outs, sems):
        send_sems, recv_sems, local_sems = sems
        x, y, c = _my_place()
        me_idx = 4 * x + 2 * y + c
        remote, local = [], []
        for k in range(len(ins)):
            cnt = outs[k].shape[0] // N_DEV
            dst = outs[k].at[pl.ds(pl.multiple_of(me_idx * cnt, 8), cnt), :]
            if k < self.n_parts:
                lo, _ = self.parts[k]
                r = ins[k].shape[0] // N_DEV
                src_of = lambda idx: ins[k].at[pl.ds(pl.multiple_of(idx * r + lo, 8), cnt), :]
            else:
                src_of = lambda idx: ins[k]
            local.append(pltpu.make_async_copy(src_of(me_idx), dst, local_sems.at[k]))
            for j in range(1, N_DEV):
                peer = (x ^ (j >> 2), y ^ ((j >> 1) & 1), c ^ (j & 1))
                peer_idx = 4 * peer[0] + 2 * peer[1] + peer[2]
                remote.append(pltpu.make_async_remote_copy(
                    src_ref=src_of(peer_idx), dst_ref=dst,
                    send_sem=send_sems.at[7 * k + j - 1], recv_sem=recv_sems.at[7 * k + j - 1],
                    device_id=peer, device_id_type=MESH))
        return remote, local

    def start(self, ins, outs, sems):
        remote, local = self._copies(ins, outs, sems)
        for cp in local + remote:
            cp.start()

    def finish(self, ins, outs, sems):
        remote, local = self._copies(ins, outs, sems)
        for cp in remote:
            cp.wait_recv()
        for cp in remote:
            cp.wait_send()
        for cp in local:
            cp.wait()


class _PairExchange:
    def __init__(self, arrays):
        self.ins = list(arrays)
        n = len(arrays)
        self.out_shape = [_sds((a.shape[0] // 2, a.shape[1]), a.dtype) for a in arrays]
        self.sems = [pltpu.SemaphoreType.DMA((4 * n,)), pltpu.SemaphoreType.DMA((4 * n,))]

    def _copies(self, ins, outs, sems):
        send_sems, recv_sems = sems
        x, y, c = _my_place()
        remote = []
        for k in range(len(ins)):
            r = ins[k].shape[0] // N_DEV
            for chip in range(4):
                sib = ins[k].at[pl.ds(pl.multiple_of((2 * chip + 1 - c) * r, 8), r), :]
                remote.append(pltpu.make_async_remote_copy(
                    src_ref=sib, dst_ref=outs[k].at[pl.ds(chip * r, r), :],
                    send_sem=send_sems.at[4 * k + chip], recv_sem=recv_sems.at[4 * k + chip],
                    device_id=(x, y, 1 - c), device_id_type=MESH))
        return remote

    def start(self, ins, outs, sems):
        for cp in self._copies(ins, outs, sems):
            cp.start()

    def finish(self, ins, outs, sems):
        remote = self._copies(ins, outs, sems)
        for cp in remote:
            cp.wait_recv()
        for cp in remote:
            cp.wait_send()


class _ChipExchange:
    def __init__(self, arrays):
        self.ins = list(arrays)
        self.out_shape = [_sds(a.shape, a.dtype) for a in arrays]
        n = len(self.ins)
        self.sems = [pltpu.SemaphoreType.DMA((3 * n,)), pltpu.SemaphoreType.DMA((3 * n,)),
                     pltpu.SemaphoreType.DMA((n,))]

    def _copies(self, ins, outs, sems):
        send_sems, recv_sems, local_sems = sems
        x, y, c = _my_place()
        my_chip = 2 * x + y
        remote, local = [], []
        for k in range(len(ins)):
            r = ins[k].shape[0] // 4
            dst = outs[k].at[pl.ds(pl.multiple_of(my_chip * r, 8), r), :]
            local.append(pltpu.make_async_copy(ins[k].at[pl.ds(pl.multiple_of(my_chip * r, 8), r), :], dst,
                                               local_sems.at[k]))
            for j in range(1, 4):
                px, py = x ^ (j >> 1), y ^ (j & 1)
                src = ins[k].at[pl.ds(pl.multiple_of((2 * px + py) * r, 8), r), :]
                remote.append(pltpu.make_async_remote_copy(
                    src_ref=src, dst_ref=dst, send_sem=send_sems.at[3 * k + j - 1],
                    recv_sem=recv_sems.at[3 * k + j - 1], device_id=(px, py, c), device_id_type=MESH))
        return remote, local

    def start(self, ins, outs, sems):
        remote, local = self._copies(ins, outs, sems)
        for cp in local + remote:
            cp.start()

    def finish(self, ins, outs, sems):
        remote, local = self._copies(ins, outs, sems)
        for cp in remote:
            cp.wait_recv()
        for cp in remote:
            cp.wait_send()
        for cp in local:
            cp.wait()


def _pcall(body, name, grid, in_specs, out_specs, out_shape, args, scratch=(), comm=None):
    params = pltpu.CompilerParams(dimension_semantics=("arbitrary",) * len(grid), vmem_limit_bytes=VMEM_LIMIT)
    in_specs, out_specs, out_shape, scratch = list(in_specs), list(out_specs), list(out_shape), list(scratch)
    if comm is None:
        res = pl.pallas_call(body, name=name, grid=grid, in_specs=in_specs, out_specs=out_specs, out_shape=out_shape,
                             scratch_shapes=scratch, compiler_params=params)(*args)
        return list(res), []
    n_in, n_out, n_scr = len(in_specs), len(out_specs), len(scratch)
    ci, co = len(comm.ins), len(comm.out_shape)
    total = math.prod(grid)

    def carried(*refs):
        bounds = [0, n_in, n_in + ci, n_in + ci + n_out, n_in + ci + n_out + co, n_in + ci + n_out + co + n_scr]
        ins, cins, outs, couts, scr = (refs[a:b] for a, b in zip(bounds[:-1], bounds[1:]))
        sems = refs[bounds[-1]:]
        step = pl.program_id(0)
        for d in range(1, len(grid)):
            step = step * grid[d] + pl.program_id(d)

        @pl.when(step == 0)
        def _():
            comm.start(cins, couts, sems)

        body(*ins, *outs, *scr)

        @pl.when(step == total - 1)
        def _():
            comm.finish(cins, couts, sems)

    res = pl.pallas_call(
        carried, name=name, grid=grid, in_specs=in_specs + [ANY] * ci, out_specs=out_specs + [ANY] * co,
        out_shape=out_shape + comm.out_shape, scratch_shapes=scratch + comm.sems, compiler_params=params,
    )(*args, *comm.ins)
    return list(res[:n_out]), list(res[n_out:])


def _exchange_only(comm, name):
    def body(*refs):
        ci, co = len(comm.ins), len(comm.out_shape)
        comm.start(refs[:ci], refs[ci:ci + co], refs[ci + co:])
        comm.finish(refs[:ci], refs[ci:ci + co], refs[ci + co:])

    return pl.pallas_call(body, name=name, out_shape=comm.out_shape, in_specs=[ANY] * len(comm.ins),
                          out_specs=[ANY] * len(comm.out_shape), scratch_shapes=comm.sems)(*comm.ins)


def _norm_inproj(x, g, win_t, b_in, comm):
    s = x.shape[0]
    tm = _row_tile(s, 512)
    widths = (QKV_W, CBX_W, GATE_W)

    def body(x_ref, g_ref, w_ref, b_ref, xn_ref, qkv_ref, cbx_ref, gate_ref):
        xv = x_ref[...]
        r = lax.rsqrt(jnp.mean(xv * xv, axis=-1, keepdims=True) + NORM_EPS)
        xn = (xv * r * g_ref[...]).astype(BF16)
        xn_ref[...] = xn
        off = 0
        for o_ref, w in zip((qkv_ref, cbx_ref, gate_ref), widths):
            acc = lax.dot_general(xn, w_ref[off:off + w, :], NT, preferred_element_type=F32)
            o_ref[...] = (acc + b_ref[:, off:off + w]).astype(BF16)
            off += w

    return _pcall(
        body, "norm_inproj", (s // tm,),
        [_rows(tm, D_MODEL), _full((1, D_MODEL)), _resident((IN_W, D_MODEL)), _full((1, IN_W))],
        [_rows(tm, D_MODEL)] + [_rows(tm, w) for w in widths],
        [_sds((s, D_MODEL), BF16)] + [_sds((s, w), BF16) for w in widths],
        (x, g, win_t, b_in), comm=comm)


def _attn_specs():
    prev = lambda n: jnp.maximum(n - 1, 0)
    return [pl.BlockSpec((BLOCK, ATTN_W), lambda n: (n, 0)),
            pl.BlockSpec((BLOCK, KV_W), lambda n: (prev(n), ATTN_W // KV_W)),
            pl.BlockSpec((BLOCK, KV_W), lambda n: (n, ATTN_W // KV_W)),
            pl.BlockSpec((BLOCK, KV_W), lambda n: (prev(n), ATTN_W // KV_W + 1)),
            pl.BlockSpec((BLOCK, KV_W), lambda n: (n, ATTN_W // KV_W + 1))]


def _lower_lanes():
    return lax.broadcasted_iota(jnp.int32, (BLOCK, 128), 1) < HEAD_DIM


def _stack_heads(val, kh):
    lower = _lower_lanes()
    parts = []
    for g in range(4):
        h = kh * 4 + g
        blk = val[:, (h // 2) * 128:(h // 2 + 1) * 128]
        keep = lower if h % 2 == 0 else jnp.logical_not(lower)
        parts.append(jnp.where(keep, blk, jnp.zeros_like(blk)))
    return jnp.concatenate(parts, axis=0)


def _dup_kv(prev_ref, cur_ref, kh):
    t = jnp.concatenate([prev_ref[...], cur_ref[...]], axis=0).astype(F32)
    rolled = pltpu.roll(t, HEAD_DIM, axis=1)
    lower = lax.broadcasted_iota(jnp.int32, t.shape, 1) < HEAD_DIM
    dup = jnp.where(lower, t, rolled) if kh == 0 else jnp.where(lower, rolled, t)
    return dup.astype(BF16)


def _attn_mask(n):
    row = lax.broadcasted_iota(jnp.int32, (4 * BLOCK, 2 * BLOCK), 0)
    kj = lax.broadcasted_iota(jnp.int32, (4 * BLOCK, 2 * BLOCK), 1)
    dist = (row & (BLOCK - 1)) + BLOCK - kj
    band = jnp.logical_and(dist >= 0, dist < BLOCK)
    return jnp.logical_and(band, jnp.logical_or(kj >= BLOCK, n > 0))


def _sink_col(sinks_ref, kh):
    gi = lax.broadcasted_iota(jnp.int32, (4 * BLOCK, 1), 0) // BLOCK
    col = jnp.zeros((4 * BLOCK, 1), F32)
    for g in range(4):
        col = jnp.where(gi == g, sinks_ref[0, kh * 4 + g], col)
    return col


def _attn_fwd(qkv, sinks, comm):
    s = qkv.shape[0]

    def body(sinks_ref, q_ref, kp_ref, kc_ref, vp_ref, vc_ref, o_ref, lse_ref):
        n = pl.program_id(0)
        mask = _attn_mask(n)
        lower = _lower_lanes()
        lane = lax.broadcasted_iota(jnp.int32, (BLOCK, 128), 1)
        qv = q_ref[...]
        lse_out = jnp.zeros((BLOCK, 128), F32)
        for kh in range(2):
            qs = _stack_heads(qv, kh)
            kd, vd = _dup_kv(kp_ref, kc_ref, kh), _dup_kv(vp_ref, vc_ref, kh)
            sc = lax.dot_general(qs, kd, NT, preferred_element_type=F32) * ATTN_SCALE
            sc = jnp.where(mask, sc, NEG)
            sink = _sink_col(sinks_ref, kh)
            m = jnp.maximum(jnp.max(sc, axis=1, keepdims=True), sink)
            p = jnp.exp(sc - m)
            l = jnp.sum(p, axis=1, keepdims=True) + jnp.exp(sink - m)
            o = jnp.dot(p.astype(BF16), vd, preferred_element_type=F32) / l
            lse = m + jnp.log(l)
            for pair in range(2):
                lo = o[(2 * pair) * BLOCK:(2 * pair + 1) * BLOCK]
                hi = o[(2 * pair + 1) * BLOCK:(2 * pair + 2) * BLOCK]
                col = (kh * 2 + pair) * 128
                o_ref[:, col:col + 128] = jnp.where(lower, lo, hi).astype(BF16)
            for g in range(4):
                lse_out = jnp.where(lane == kh * 4 + g, lse[g * BLOCK:(g + 1) * BLOCK], lse_out)
        lse_ref[...] = lse_out

    return _pcall(
        body, "attn_fwd", (s // BLOCK,),
        [pl.BlockSpec(memory_space=pltpu.SMEM)] + _attn_specs(),
        [pl.BlockSpec((BLOCK, ATTN_W), lambda n: (n, 0)), pl.BlockSpec((BLOCK, 128), lambda n: (n, 0))],
        [_sds((s, ATTN_W), BF16), _sds((s, 128), F32)],
        (sinks, qkv, qkv, qkv, qkv, qkv), comm=comm)


def _conv_u(cbx_ref, halo_ref, w_ref, first):
    cb = cbx_ref[:, 0:CONV_W].astype(F32)
    cc = cbx_ref[:, CONV_W:2 * CONV_W].astype(F32)
    cx = cbx_ref[:, 2 * CONV_W:3 * CONV_W].astype(F32)
    u = cc * cx
    uh = halo_ref[:, CONV_W:2 * CONV_W].astype(F32) * halo_ref[:, 2 * CONV_W:3 * CONV_W].astype(F32)
    uh = jnp.where(first, 0.0, uh)
    u1, u2 = _shifts_down(u, uh, (1, 2))
    cv = w_ref[0:1, :] * u2 + w_ref[1:2, :] * u1 + w_ref[2:3, :] * u
    return cb, cc, cx, u, cv


def _mix_fwd(x, cbx, gates, attn, conv_w, wa, wc, wout, comm):
    s = x.shape[0]
    tm = _row_tile(s)

    def body(x_ref, cbx_ref, halo_ref, gate_ref, attn_ref, cw_ref, wa_ref, wc_ref, wo_ref,
             conv_ref, ap_ref, cp_ref, mg_ref, h1_ref):
        first = pl.program_id(0) == 0
        cb, _, _, _, cv = _conv_u(cbx_ref, halo_ref, cw_ref, first)
        conv = (cb * cv).astype(BF16)
        conv_ref[...] = conv
        ap = jnp.dot(attn_ref[...], wa_ref[...], preferred_element_type=F32)
        cp = jnp.dot(conv, wc_ref[...], preferred_element_type=F32)
        ap_ref[...] = ap.astype(BF16)
        cp_ref[...] = cp.astype(BF16)
        ga = gate_ref[:, 0:D_MODEL].astype(F32)
        gc = gate_ref[:, D_MODEL:2 * D_MODEL].astype(F32)
        merged = (_sig(ga) * ap + _sig(gc) * cp).astype(BF16)
        mg_ref[...] = merged
        h1_ref[...] = x_ref[...] + jnp.dot(merged, wo_ref[...], preferred_element_type=F32)

    return _pcall(
        body, "mix_fwd", (s // tm,),
        [_rows(tm, D_MODEL), _rows(tm, CBX_W), pl.BlockSpec((HALO, CBX_W), _prev_halo_map(tm)),
         _rows(tm, GATE_W), _rows(tm, ATTN_W), _full((3, CONV_W)), _full((ATTN_W, D_MODEL)),
         _full((CONV_W, D_MODEL)), _full((D_MODEL, D_MODEL))],
        [_rows(tm, CONV_W), _rows(tm, D_MODEL), _rows(tm, D_MODEL), _rows(tm, D_MODEL), _rows(tm, D_MODEL)],
        [_sds((s, CONV_W), BF16), _sds((s, D_MODEL), BF16), _sds((s, D_MODEL), BF16), _sds((s, D_MODEL), BF16),
         _sds((s, D_MODEL), F32)],
        (x, cbx, cbx, gates, attn, conv_w, wa, wc, wout), comm=comm)


def _ffn_up(h1, g, wup_lo, wup_hi, comm):
    s = h1.shape[0]
    tm = _row_tile(s, 512)
    half = D_MODEL // 2

    def body(h_ref, g_ref, wl_ref, wh_ref, hn_ref, up_ref):
        hv = h_ref[...]
        r = lax.rsqrt(jnp.mean(hv * hv, axis=-1, keepdims=True) + NORM_EPS)
        hn = (hv * r * g_ref[...]).astype(BF16)
        hn_ref[...] = hn
        for c in range(2 * D_FF // FF_CHUNK):
            sl = slice(c * FF_CHUNK, (c + 1) * FF_CHUNK)
            acc = lax.dot_general(hn[:, :half], wl_ref[sl, :], NT, preferred_element_type=F32)
            acc = acc + lax.dot_general(hn[:, half:], wh_ref[sl, :], NT, preferred_element_type=F32)
            up_ref[:, sl] = acc.astype(BF16)

    return _pcall(
        body, "ffn_up", (s // tm,),
        [_rows(tm, D_MODEL), _full((1, D_MODEL)), _resident((2 * D_FF, half)), _resident((2 * D_FF, half))],
        [_rows(tm, D_MODEL), _rows(tm, 2 * D_FF)],
        [_sds((s, D_MODEL), BF16), _sds((s, 2 * D_FF), BF16)],
        (h1, g, wup_lo, wup_hi), comm=comm)


def _ffn_conv_cols(up_ref, halo_ref, fcw_ref, first, off):
    u = up_ref[:, off:off + FF_CHUNK].astype(F32)
    uh = jnp.where(first, 0.0, halo_ref[:, off:off + FF_CHUNK].astype(F32))
    w = fcw_ref[:, off:off + FF_CHUNK]
    u1, u2 = _shifts_down(u, uh, (1, 2))
    return w[0:1] * u2 + w[1:2] * u1 + w[2:3] * u


def _ffn_down_loss(up_pre, fcw, wdown, h1, fnorm, target):
    s = h1.shape[0]
    tm = _row_tile(s)

    def body(up_ref, halo_ref, fcw_ref, wd_ref, h1_ref, fn_ref, t_ref, cu_ref, act_ref, dh2_ref, loss_ref, dfn_ref):
        i = pl.program_id(0)

        @pl.when(i == 0)
        def _():
            loss_ref[...] = jnp.zeros_like(loss_ref)
            dfn_ref[...] = jnp.zeros_like(dfn_ref)

        h2 = h1_ref[...]
        for c in range(D_FF // FF_CHUNK):
            gsl = slice(c * FF_CHUNK, (c + 1) * FF_CHUNK)
            vsl = slice(D_FF + c * FF_CHUNK, D_FF + (c + 1) * FF_CHUNK)
            gate = _ffn_conv_cols(up_ref, halo_ref, fcw_ref, i == 0, c * FF_CHUNK)
            cu_ref[:, gsl] = gate.astype(BF16)
            val = _ffn_conv_cols(up_ref, halo_ref, fcw_ref, i == 0, D_FF + c * FF_CHUNK)
            cu_ref[:, vsl] = val.astype(BF16)
            act = (gate * _sig(gate) * val).astype(BF16)
            act_ref[:, gsl] = act
            h2 = h2 + jnp.dot(act, wd_ref[gsl, :], preferred_element_type=F32)
        r = lax.rsqrt(jnp.mean(h2 * h2, axis=-1, keepdims=True) + NORM_EPS)
        yhat = h2 * r
        fn = fn_ref[...]
        diff = yhat * fn - t_ref[...]
        loss_ref[...] += 0.5 * jnp.sum(jnp.sum(diff * diff, axis=1, keepdims=True), axis=0, keepdims=True) / D_MODEL
        dy = diff * (1.0 / D_MODEL)
        dfn_ref[...] += jnp.sum(dy * yhat, axis=0, keepdims=True)
        dyh = dy * fn
        dh2_ref[...] = r * (dyh - yhat * jnp.mean(dyh * yhat, axis=-1, keepdims=True))

    return _pcall(
        body, "ffn_down_loss", (s // tm,),
        [_rows(tm, 2 * D_FF), pl.BlockSpec((HALO, 2 * D_FF), _prev_halo_map(tm)), _full((3, 2 * D_FF)),
         _resident((D_FF, D_MODEL)), _rows(tm, D_MODEL), _full((1, D_MODEL)), _rows(tm, D_MODEL)],
        [_rows(tm, 2 * D_FF), _rows(tm, D_FF), _rows(tm, D_MODEL), _full((1, 128)), _full((1, D_MODEL))],
        [_sds((s, 2 * D_FF), BF16), _sds((s, D_FF), BF16), _sds((s, D_MODEL), F32), _sds((1, 128), F32),
         _sds((1, D_MODEL), F32)],
        (up_pre, up_pre, fcw, wdown, h1, fnorm, target))[0]


def _ffn_act_bwd(dh2, wdown, up, comm):
    s = dh2.shape[0]
    tm = _row_tile(s)

    def body(dh_ref, wd_ref, up_ref, dup_ref):
        dh = dh_ref[...].astype(BF16)
        for c in range(D_FF // FF_CHUNK):
            gsl = slice(c * FF_CHUNK, (c + 1) * FF_CHUNK)
            vsl = slice(D_FF + c * FF_CHUNK, D_FF + (c + 1) * FF_CHUNK)
            dact = lax.dot_general(dh, wd_ref[gsl, :], NT, preferred_element_type=F32)
            gate = up_ref[:, gsl].astype(F32)
            val = up_ref[:, vsl].astype(F32)
            sg = _sig(gate)
            dup_ref[:, gsl] = (dact * val * (sg * (1.0 + gate * (1.0 - sg)))).astype(BF16)
            dup_ref[:, vsl] = (dact * gate * sg).astype(BF16)

    return _pcall(
        body, "ffn_act_bwd", (s // tm,),
        [_rows(tm, D_MODEL), _resident((D_FF, D_MODEL)), _rows(tm, 2 * D_FF)],
        [_rows(tm, 2 * D_FF)], [_sds((s, 2 * D_FF), BF16)],
        (dh2, wdown, up), comm=comm)


def _conv_bwd(dy, x, w, width, chunk, name, comm):
    s = dy.shape[0]
    tm = _row_tile(s)

    def body(dy_ref, dyn_ref, x_ref, w_ref, dx_ref, dw_ref):
        i = pl.program_id(0)

        @pl.when(i == 0)
        def _():
            dw_ref[...] = jnp.zeros_like(dw_ref)

        last = i == s // tm - 1
        up1, up2 = _shift_matrix(tm, 1), _shift_matrix(tm, 2)
        for c in range(width // chunk):
            sl = slice(c * chunk, (c + 1) * chunk)
            db = dy_ref[:, sl]
            d = db.astype(F32)
            dn = jnp.where(last, 0.0, dyn_ref[:, sl].astype(F32))
            xv = x_ref[:, sl].astype(F32)
            wv = w_ref[:, sl]
            d1 = _mxu_shift_up(up1, db, dn, 1)
            d2 = _mxu_shift_up(up2, db, dn, 2)
            dx = wv[2:3] * d + wv[1:2] * d1 + wv[0:1] * d2
            dx_ref[:, sl] = dx.astype(BF16)
            dw_ref[0:1, sl] += jnp.sum(d2 * xv, axis=0, keepdims=True)
            dw_ref[1:2, sl] += jnp.sum(d1 * xv, axis=0, keepdims=True)
            dw_ref[2:3, sl] += jnp.sum(d * xv, axis=0, keepdims=True)

    return _pcall(
        body, name, (s // tm,),
        [_rows(tm, width), pl.BlockSpec((HALO, width), _next_halo_map(tm, s)), _rows(tm, width),
         _full((3, width))],
        [_rows(tm, width), _full((3, width))],
        [_sds((s, width), BF16), _sds((3, width), F32)],
        (dy, dy, x, w), comm=comm)


def _matmul_tn(a, b, tk, name, ts=1024, comm=None):
    s, ka = a.shape
    n = b.shape[1]
    ts = min(ts, s)
    steps = s // ts

    def body(a_ref, b_ref, o_ref, acc_ref):
        j = pl.program_id(1)

        @pl.when(j == 0)
        def _():
            acc_ref[...] = jnp.zeros_like(acc_ref)

        acc_ref[...] += lax.dot_general(a_ref[...].astype(BF16), b_ref[...].astype(BF16), TN,
                                        preferred_element_type=F32)

        @pl.when(j == steps - 1)
        def _():
            o_ref[...] = acc_ref[...].astype(BF16)

    outs, couts = _pcall(
        body, name, (ka // tk, steps),
        [pl.BlockSpec((ts, tk), lambda i, j: (j, i)), pl.BlockSpec((ts, n), lambda i, j: (j, 0))],
        [pl.BlockSpec((tk, n), lambda i, j: (i, 0))], [_sds((ka, n), BF16)],
        (a, b), scratch=[pltpu.VMEM((tk, n), F32)], comm=comm)
    return outs[0] if comm is None else (outs[0], couts)


def _norm_bwd_tile(xv, g, dy):
    r = lax.rsqrt(jnp.mean(xv * xv, axis=-1, keepdims=True) + NORM_EPS)
    xhat = xv * r
    dg = jnp.sum(dy * xhat, axis=0, keepdims=True)
    dyh = dy * g
    return r * (dyh - xhat * jnp.mean(dyh * xhat, axis=-1, keepdims=True)), dg


def _ffn_up_bwd(dup_pre, wup_lo, wup_hi, h1, g, dh2, comm):
    s = h1.shape[0]
    tm = _row_tile(s, 512)
    half = D_MODEL // 2

    def body(du_ref, wl_ref, wh_ref, h_ref, g_ref, dh2_ref, dh1_ref, dg_ref):
        @pl.when(pl.program_id(0) == 0)
        def _():
            dg_ref[...] = jnp.zeros_like(dg_ref)

        du = du_ref[...]
        dhn = jnp.concatenate([jnp.dot(du, wl_ref[...], preferred_element_type=F32),
                               jnp.dot(du, wh_ref[...], preferred_element_type=F32)], axis=1)
        dx, dg = _norm_bwd_tile(h_ref[...], g_ref[...], dhn)
        dg_ref[...] += dg
        dh1_ref[...] = dh2_ref[...] + dx

    return _pcall(
        body, "ffn_up_bwd", (s // tm,),
        [_rows(tm, 2 * D_FF), _resident((2 * D_FF, half)), _resident((2 * D_FF, half)), _rows(tm, D_MODEL),
         _full((1, D_MODEL)), _rows(tm, D_MODEL)],
        [_rows(tm, D_MODEL), _full((1, D_MODEL))],
        [_sds((s, D_MODEL), F32), _sds((1, D_MODEL), F32)],
        (dup_pre, wup_lo, wup_hi, h1, g, dh2), comm=comm)


def _mix_bwd(dh1, wout, gates, ap, cp, wa, wc, cbx, conv_w, comm):
    s = dh1.shape[0]
    tm = _row_tile(s)

    def body(dh_ref, wo_ref, gate_ref, ap_ref, cp_ref, wa_ref, wc_ref, cbx_ref, halo_ref, cw_ref,
             dg_ref, da_ref, dc_ref, dattn_ref, dcb_ref, dcv_ref):
        first = pl.program_id(0) == 0
        dm = lax.dot_general(dh_ref[...].astype(BF16), wo_ref[...], NT, preferred_element_type=F32)
        sa = _sig(gate_ref[:, 0:D_MODEL].astype(F32))
        sc = _sig(gate_ref[:, D_MODEL:2 * D_MODEL].astype(F32))
        da = (dm * sa).astype(BF16)
        dc = (dm * sc).astype(BF16)
        da_ref[...] = da
        dc_ref[...] = dc
        dg_ref[:, 0:D_MODEL] = (dm * ap_ref[...].astype(F32) * sa * (1.0 - sa)).astype(BF16)
        dg_ref[:, D_MODEL:2 * D_MODEL] = (dm * cp_ref[...].astype(F32) * sc * (1.0 - sc)).astype(BF16)
        dattn_ref[...] = lax.dot_general(da, wa_ref[...], NT, preferred_element_type=F32).astype(BF16)
        dconv = lax.dot_general(dc, wc_ref[...], NT, preferred_element_type=F32)
        cb, _, _, _, cv = _conv_u(cbx_ref, halo_ref, cw_ref, first)
        dcb_ref[...] = (dconv * cv).astype(BF16)
        dcv_ref[...] = (dconv * cb).astype(BF16)

    return _pcall(
        body, "mix_bwd", (s // tm,),
        [_rows(tm, D_MODEL), _full((D_MODEL, D_MODEL)), _rows(tm, GATE_W), _rows(tm, D_MODEL),
         _rows(tm, D_MODEL), _full((ATTN_W, D_MODEL)), _full((CONV_W, D_MODEL)), _rows(tm, CBX_W),
         pl.BlockSpec((HALO, CBX_W), _prev_halo_map(tm)), _full((3, CONV_W))],
        [_rows(tm, GATE_W), _rows(tm, D_MODEL), _rows(tm, D_MODEL), _rows(tm, ATTN_W),
         _rows(tm, CONV_W), _rows(tm, CONV_W)],
        [_sds((s, GATE_W), BF16), _sds((s, D_MODEL), BF16), _sds((s, D_MODEL), BF16), _sds((s, ATTN_W), BF16),
         _sds((s, CONV_W), BF16), _sds((s, CONV_W), BF16)],
        (dh1, wout, gates, ap, cp, wa, wc, cbx, cbx, conv_w), comm=comm)


def _conv_branch_bwd(dcv, cbx, conv_w):
    s = dcv.shape[0]
    tm = _row_tile(s)

    def body(d_ref, dn_ref, cbx_ref, w_ref, dcc_ref, dcx_ref, dw_ref):
        i = pl.program_id(0)

        @pl.when(i == 0)
        def _():
            dw_ref[...] = jnp.zeros_like(dw_ref)

        last = i == s // tm - 1
        cc = cbx_ref[:, CONV_W:2 * CONV_W].astype(F32)
        cx = cbx_ref[:, 2 * CONV_W:3 * CONV_W].astype(F32)
        u = cc * cx
        d = d_ref[...].astype(F32)
        dn = jnp.where(last, 0.0, dn_ref[...].astype(F32))
        d1, d2 = _shifts_up(d, dn, (1, 2))
        du = w_ref[2:3, :] * d + w_ref[1:2, :] * d1 + w_ref[0:1, :] * d2
        dcc_ref[...] = (du * cx).astype(BF16)
        dcx_ref[...] = (du * cc).astype(BF16)
        dw_ref[0:1, :] += jnp.sum(d2 * u, axis=0, keepdims=True)
        dw_ref[1:2, :] += jnp.sum(d1 * u, axis=0, keepdims=True)
        dw_ref[2:3, :] += jnp.sum(d * u, axis=0, keepdims=True)

    return _pcall(
        body, "conv_branch_bwd", (s // tm,),
        [_rows(tm, CONV_W), pl.BlockSpec((HALO, CONV_W), _next_halo_map(tm, s)), _rows(tm, CBX_W),
         _full((3, CONV_W))],
        [_rows(tm, CONV_W), _rows(tm, CONV_W), _full((3, CONV_W))],
        [_sds((s, CONV_W), BF16), _sds((s, CONV_W), BF16), _sds((3, CONV_W), F32)],
        (dcv, dcv, cbx, conv_w))[0]


def _attn_bwd(qkv, sinks, attn, lse, dattn, comm):
    s = qkv.shape[0]

    def body(sinks_ref, q_ref, kp_ref, kc_ref, vp_ref, vc_ref, o_ref, lse_ref, do_ref,
             dq_ref, dk_ref, dv_ref, ds_ref):
        n = pl.program_id(0)

        @pl.when(n == 0)
        def _():
            dk_ref[...] = jnp.zeros_like(dk_ref)
            dv_ref[...] = jnp.zeros_like(dv_ref)
            ds_ref[...] = jnp.zeros_like(ds_ref)

        mask = _attn_mask(n)
        lower = _lower_lanes()
        lane = lax.broadcasted_iota(jnp.int32, (BLOCK, 128), 1)
        lower2 = lax.broadcasted_iota(jnp.int32, (2 * BLOCK, 128), 1) < HEAD_DIM
        lane1 = lax.broadcasted_iota(jnp.int32, (1, 128), 1)
        qv, ov, dov, lsev = q_ref[...], o_ref[...], do_ref[...], lse_ref[...]
        dk_fold, dv_fold = [], []
        dsink = jnp.zeros((1, 128), F32)
        for kh in range(2):
            qs = _stack_heads(qv, kh)
            dos = _stack_heads(dov, kh)
            os_ = _stack_heads(ov, kh)
            kd, vd = _dup_kv(kp_ref, kc_ref, kh), _dup_kv(vp_ref, vc_ref, kh)
            lse = jnp.concatenate(
                [jnp.sum(jnp.where(lane == kh * 4 + g, lsev, 0.0), axis=1, keepdims=True) for g in range(4)], axis=0)
            sc = lax.dot_general(qs, kd, NT, preferred_element_type=F32) * ATTN_SCALE
            p = jnp.exp(jnp.where(mask, sc, NEG) - lse)
            dp = lax.dot_general(dos, vd, NT, preferred_element_type=F32)
            delta = jnp.sum(dos.astype(F32) * os_.astype(F32), axis=1, keepdims=True)
            dsc = (p * (dp - delta) * ATTN_SCALE).astype(BF16)
            dqs = jnp.dot(dsc, kd, preferred_element_type=F32)
            for pair in range(2):
                lo = dqs[(2 * pair) * BLOCK:(2 * pair + 1) * BLOCK]
                hi = dqs[(2 * pair + 1) * BLOCK:(2 * pair + 2) * BLOCK]
                col = (kh * 2 + pair) * 128
                dq_ref[:, col:col + 128] = jnp.where(lower, lo, hi).astype(BF16)
            dkd = lax.dot_general(dsc, qs, TN, preferred_element_type=F32)
            dvd = lax.dot_general(p.astype(BF16), dos, TN, preferred_element_type=F32)
            dk_fold.append(dkd + pltpu.roll(dkd, HEAD_DIM, axis=1))
            dv_fold.append(dvd + pltpu.roll(dvd, HEAD_DIM, axis=1))
            psink = jnp.exp(_sink_col(sinks_ref, kh) - lse) * delta
            for g in range(4):
                tot = jnp.sum(psink[g * BLOCK:(g + 1) * BLOCK], axis=0, keepdims=True)
                dsink = dsink - jnp.where(lane1 == kh * 4 + g, tot, 0.0)
        dk2 = jnp.where(lower2, dk_fold[0], dk_fold[1])
        dv2 = jnp.where(lower2, dv_fold[0], dv_fold[1])
        ds_ref[...] += dsink
        cur = pl.ds(pl.multiple_of(n * BLOCK, BLOCK), BLOCK)
        dk_ref[cur, :] += dk2[BLOCK:]
        dv_ref[cur, :] += dv2[BLOCK:]

        @pl.when(n > 0)
        def _():
            prev = pl.ds(pl.multiple_of((n - 1) * BLOCK, BLOCK), BLOCK)
            dk_ref[prev, :] += dk2[:BLOCK]
            dv_ref[prev, :] += dv2[:BLOCK]

    blk = lambda w: pl.BlockSpec((BLOCK, w), lambda n: (n, 0))
    return _pcall(
        body, "attn_bwd", (s // BLOCK,),
        [pl.BlockSpec(memory_space=pltpu.SMEM)] + _attn_specs() + [blk(ATTN_W), blk(128), blk(ATTN_W)],
        [blk(ATTN_W), _full((s, KV_W)), _full((s, KV_W)), _full((1, 128))],
        [_sds((s, ATTN_W), BF16), _sds((s, KV_W), F32), _sds((s, KV_W), F32), _sds((1, 128), F32)],
        (sinks, qkv, qkv, qkv, qkv, qkv, attn, lse, dattn), comm=comm)


def _assemble_dproj(dq, dk, dv, dcb, dcc, dcx, dgates):
    s = dq.shape[0]
    tm = _row_tile(s)
    pieces = (ATTN_W, KV_W, KV_W, CONV_W, CONV_W, CONV_W, GATE_W)

    def body(*refs):
        srcs, dp_ref, db_ref = refs[:len(pieces)], refs[-2], refs[-1]

        @pl.when(pl.program_id(0) == 0)
        def _():
            db_ref[...] = jnp.zeros_like(db_ref)

        off = 0
        for ref, w in zip(srcs, pieces):
            v = ref[...].astype(BF16)
            dp_ref[:, off:off + w] = v
            db_ref[:, off:off + w] += jnp.sum(v.astype(F32), axis=0, keepdims=True)
            off += w

    return _pcall(
        body, "assemble_dproj", (s // tm,), [_rows(tm, w) for w in pieces],
        [_rows(tm, IN_W), _full((1, IN_W))], [_sds((s, IN_W), BF16), _sds((1, IN_W), F32)],
        (dq, dk, dv, dcb, dcc, dcx, dgates))[0]


def _inproj_bwd(dproj, win_t, x, g, dh1, comm):
    s = x.shape[0]
    tm = _row_tile(s, 512)

    def body(dp_ref, w_ref, x_ref, g_ref, dh_ref, dx_ref, dg_ref):
        @pl.when(pl.program_id(0) == 0)
        def _():
            dg_ref[...] = jnp.zeros_like(dg_ref)

        dxn = jnp.dot(dp_ref[...], w_ref[...], preferred_element_type=F32)
        dx, dg = _norm_bwd_tile(x_ref[...], g_ref[...], dxn)
        dg_ref[...] += dg
        dx_ref[...] = dh_ref[...] + dx

    return _pcall(
        body, "inproj_bwd", (s // tm,),
        [_rows(tm, IN_W), _resident((IN_W, D_MODEL)), _rows(tm, D_MODEL), _full((1, D_MODEL)), _rows(tm, D_MODEL)],
        [_rows(tm, D_MODEL), _full((1, D_MODEL))],
        [_sds((s, D_MODEL), F32), _sds((1, D_MODEL), F32)],
        (dproj, win_t, x, g, dh1), comm=comm)


def _adam_math(w, g, m, v):
    m2 = ADAM_B1 * m + (1.0 - ADAM_B1) * g
    v2 = ADAM_B2 * v + (1.0 - ADAM_B2) * (g * g)
    m_hat = m2 / (1.0 - ADAM_B1 ** ADAM_STEP)
    v_hat = v2 / (1.0 - ADAM_B2 ** ADAM_STEP)
    delta = -ADAM_LR * (m_hat / (jnp.sqrt(v_hat) + ADAM_EPS) + ADAM_WD * w)
    return delta, m2, v2


def _sum_slots(ref):
    tot = ref[0].astype(F32)
    for i in range(1, ref.shape[0]):
        tot = tot + ref[i].astype(F32)
    return tot


def _pair_add(mine, theirs, tr, name):
    r, c = mine.shape

    def body(a_ref, b_ref, o_ref):
        o_ref[...] = (a_ref[...].astype(F32) + b_ref[...].astype(F32)).astype(BF16)

    spec = pl.BlockSpec((tr, c), lambda i: (i, 0))
    return _pcall(body, name, (r // tr,), [spec, spec], [spec], [_sds((r, c), BF16)], (mine, theirs))[0][0]


def _sum_adamw(parts, w, m, v, tr, name):
    r, c = w.shape

    def body(p_ref, w_ref, m_ref, v_ref, g_ref, d_ref, m2_ref, v2_ref):
        g = _sum_slots(p_ref)
        g_ref[...] = g
        d_ref[...], m2_ref[...], v2_ref[...] = _adam_math(w_ref[...], g, m_ref[...], v_ref[...])

    spec = pl.BlockSpec((tr, c), lambda i: (i, 0))
    return _pcall(body, name, (r // tr,), [pl.BlockSpec((N_DEV, tr, c), lambda i: (0, i, 0)), spec, spec, spec],
                  [spec] * 4, [_sds((r, c), F32)] * 4, (parts, w, m, v))[0]


def _sum_parts_adamw(parts, w, m, v, tr, name):
    c = w.shape[1]
    tiles = [p.shape[1] // tr for p in parts]
    starts = [sum(tiles[:k]) for k in range(len(parts))]
    n_parts = len(parts)

    def body(*refs):
        p_refs = refs[:n_parts]
        w_ref, m_ref, v_ref, g_ref, d_ref, m2_ref, v2_ref = refs[n_parts:]
        i = pl.program_id(0)
        for p_ref, st, nt in zip(p_refs, starts, tiles):
            @pl.when(jnp.logical_and(i >= st, i < st + nt))
            def _(p_ref=p_ref):
                g_ref[...] = _sum_slots(p_ref)

        d_ref[...], m2_ref[...], v2_ref[...] = _adam_math(w_ref[...], g_ref[...], m_ref[...], v_ref[...])

    def part_spec(p, st, nt):
        return pl.BlockSpec((p.shape[0], tr, c), lambda i: (0, jnp.clip(i - st, 0, nt - 1), 0))

    spec = pl.BlockSpec((tr, c), lambda i: (i, 0))
    return _pcall(
        body, name, (sum(tiles),),
        [part_spec(p, st, nt) for p, st, nt in zip(parts, starts, tiles)] + [spec, spec, spec],
        [spec] * 4, [_sds(w.shape, F32)] * 4, (*parts, w, m, v))[0]


def _sum_only(parts, tr, name):
    _, r, c = parts.shape

    def body(p_ref, g_ref):
        g_ref[...] = _sum_slots(p_ref)

    return _pcall(body, name, (r // tr,), [pl.BlockSpec((N_DEV, tr, c), lambda i: (0, i, 0))],
                  [pl.BlockSpec((tr, c), lambda i: (i, 0))], [_sds((r, c), F32)], (parts,))[0][0]


def _adamw(w, g, m, v, tr, name):
    r, c = w.shape

    def body(w_ref, g_ref, m_ref, v_ref, d_ref, m2_ref, v2_ref):
        d_ref[...], m2_ref[...], v2_ref[...] = _adam_math(w_ref[...], g_ref[...], m_ref[...], v_ref[...])

    spec = pl.BlockSpec((tr, c), lambda i: (i, 0))
    return _pcall(body, name, (r // tr,), [spec] * 4, [spec] * 3, [_sds((r, c), F32)] * 3, (w, g, m, v))[0]


def _pad_cols(a, c):
    return jnp.pad(a, ((0, 0), (0, c - a.shape[1])))


def _to_col_slabs(g):
    r = g.shape[0]
    return jnp.transpose(g.reshape(r, N_DEV, 128), (1, 0, 2)).reshape(N_DEV * r, 128)


def _from_col_slabs(t):
    r = t.shape[0] // N_DEV
    return jnp.transpose(t.reshape(N_DEV, r, 128), (1, 0, 2)).reshape(r, N_DEV * 128)


def _slots(t):
    return t.reshape(N_DEV, t.shape[0] // N_DEV, t.shape[1])


def kernel(x, mix_norm, w_in, b_in, sinks, conv_w, w_attn_branch, w_conv_branch, w_out, ffn_norm, w_up, ffn_conv_w, w_down, final_norm, loss_target, m_mix_norm, m_w_in, m_b_in, m_sinks, m_conv_w, m_w_attn_branch, m_w_conv_branch, m_w_out, m_ffn_norm, m_w_up, m_ffn_conv_w, m_w_down, m_final_norm, v_mix_norm, v_w_in, v_b_in, v_sinks, v_conv_w, v_w_attn_branch, v_w_conv_branch, v_w_out, v_ffn_norm, v_w_up, v_ffn_conv_w, v_w_down, v_final_norm):
    xs, tgt = x[0], loss_target[0]
    me = 4 * lax.axis_index("x") + 2 * lax.axis_index("y") + lax.axis_index("c")
    in_rows, up_rows = IN_W // N_DEV, 2 * D_FF // N_DEV

    conv_sh = jnp.concatenate([_pad_cols(ffn_conv_w[0], 768), _pad_cols(conv_w[0], 768),
                               jnp.zeros((2, 768), F32)], axis=0)
    win_sh, wup_sh = w_in[0].T.astype(BF16), w_up[0].T.astype(BF16)
    wout_sh, wdown_sh = w_out[0].astype(BF16), w_down[0].astype(BF16)
    wa_sh, wc_sh = w_attn_branch[0].astype(BF16), w_conv_branch[0].astype(BF16)

    half = D_MODEL // 2
    (win_t,) = _exchange_only(_AllGather([win_sh]), "gather_w_in")
    (xn, qkv, cbx, gates), (wa_s, wc_s, wout, conv_g) = _norm_inproj(
        xs, mix_norm, win_t, b_in, _AllGather([wa_sh, wc_sh, wout_sh, conv_sh]))
    (attn, lse), (wup_lo,) = _attn_fwd(qkv, sinks, _AllGather([wup_sh[:, :half]]))
    wa, wc = _from_col_slabs(wa_s), _from_col_slabs(wc_s)
    conv_g = conv_g.reshape(N_DEV, 8, 768)
    fcw = jnp.transpose(conv_g[:, 0:3, :up_rows], (1, 0, 2)).reshape(3, 2 * D_FF)
    cw = jnp.transpose(conv_g[:, 3:6, :CONV_W // N_DEV], (1, 0, 2)).reshape(3, CONV_W)
    (conv, ap, cp, merged, h1), (wup_hi,) = _mix_fwd(xs, cbx, gates, attn, cw, wa, wc, wout,
                                                    _AllGather([wup_sh[:, half:]]))
    (hn, up_pre), (wdown,) = _ffn_up(h1, ffn_norm, wup_lo, wup_hi, _AllGather([wdown_sh]))
    up, act, dh2, loss_p, dfn_p = _ffn_down_loss(up_pre, fcw, wdown, h1, final_norm.reshape(1, D_MODEL), tgt)

    dn_rows, q_up = D_FF // N_DEV, up_rows // 4
    g_wdown = _matmul_tn(act, dh2, FF_CHUNK, "grad_w_down")
    (dup,), (r_wdown_a,) = _ffn_act_bwd(dh2, wdown, up, _ReduceScatter([(g_wdown, 0, dn_rows // 2)]))
    (dup_pre, dfcw_p), (r_wdown_b,) = _conv_bwd(dup, up_pre, fcw, 2 * D_FF, FF_CHUNK, "ffn_conv_bwd",
                                                _ReduceScatter([(g_wdown, dn_rows // 2, dn_rows // 2)]))
    g_wup_t = _matmul_tn(dup_pre, hn, FF_CHUNK, "grad_w_up")
    (dh1, dffn_p), (r_wup_a,) = _ffn_up_bwd(dup_pre, wup_lo, wup_hi, h1, ffn_norm, dh2,
                                            _ReduceScatter([(g_wup_t, 0, q_up)]))
    g_wout = _matmul_tn(merged, dh1, D_MODEL, "grad_w_out")
    (dgates, da, dc, dattn, dcb, dcv), (r_wup_b,) = _mix_bwd(
        dh1, wout, gates, ap, cp, wa, wc, cbx, cw, _ReduceScatter([(g_wup_t, q_up, q_up)]))
    g_wa = _to_col_slabs(_matmul_tn(attn, da, ATTN_W, "grad_w_attn_branch"))
    g_wc = _to_col_slabs(_matmul_tn(conv, dc, CONV_W, "grad_w_conv_branch"))
    dcc, dcx, dcw_p = _conv_branch_bwd(dcv, cbx, cw)
    (dq, dk, dv, dsink_p), (r_wup_c, r_wout, r_wa, r_wc) = _attn_bwd(
        qkv, sinks, attn, lse, dattn,
        _ReduceScatter([(g_wup_t, 2 * q_up, q_up), (g_wout, 0, D_MODEL // N_DEV), (g_wa, 0, ATTN_W),
                        (g_wc, 0, CONV_W)]))
    dproj, dbin_p = _assemble_dproj(dq, dk, dv, dcb, dcc, dcx, dgates)
    g_win_t, (r_wup_d,) = _matmul_tn(dproj, xn, IN_W // 2, "grad_w_in",
                                     comm=_ReduceScatter([(g_wup_t, 3 * q_up, q_up)]))
    (win_theirs,) = _exchange_only(_PairExchange([g_win_t]), "pair_exchange_w_in")
    win_mine = lax.dynamic_index_in_dim(g_win_t.reshape(4, 2, in_rows, D_MODEL), lax.axis_index("c"), axis=1,
                                        keepdims=False).reshape(4 * in_rows, D_MODEL)
    q_win = _pair_add(win_mine, win_theirs, in_rows // 2, "pair_add_w_in")
    (dx, dmix_p), (r_win,) = _inproj_bwd(dproj, win_t, xs, mix_norm, dh1, _ChipExchange([q_win]))

    row = lambda a: _pad_cols(a.reshape(1, -1), D_MODEL)
    small = jnp.concatenate(
        [dmix_p, dffn_p, dfn_p, row(dsink_p[:, :N_HEADS]), row(loss_p[:, :1]),
         _pad_cols(dbin_p, 5 * D_MODEL).reshape(5, D_MODEL), _pad_cols(dcw_p, D_MODEL),
         _pad_cols(dfcw_p, 6 * D_MODEL).reshape(18, D_MODEL), jnp.zeros((1, D_MODEL), F32)], axis=0)
    (r_small,) = _exchange_only(_ReduceScatter([], [small]), "exchange_small")

    small_t = _sum_only(_slots(r_small), SMALL_ROWS, "sum_small")
    g_mix, g_ffn, g_fn = small_t[0:1], small_t[1:2], small_t[2:3]
    g_sinks, loss = small_t[3:4, :N_HEADS], small_t[4, 0]
    g_bin = small_t[5:10].reshape(1, 5 * D_MODEL)[:, :IN_W]
    g_cw_full = small_t[10:13, :CONV_W]
    g_fcw_full = small_t[13:31].reshape(3, 6 * D_MODEL)[:, :2 * D_FF]
    g_cw = lax.dynamic_slice_in_dim(g_cw_full, me * (CONV_W // N_DEV), CONV_W // N_DEV, axis=1)
    g_fcw = lax.dynamic_slice_in_dim(g_fcw_full, me * up_rows, up_rows, axis=1)

    big = {}
    big["w_in"] = tuple(t.T for t in _sum_parts_adamw(
        [r_win.reshape(4, in_rows, D_MODEL)], w_in[0].T, m_w_in[0].T, v_w_in[0].T, in_rows // 2, "adamw_w_in"))
    big["w_up"] = tuple(t.T for t in _sum_parts_adamw(
        [_slots(r_wup_a), _slots(r_wup_b), _slots(r_wup_c), _slots(r_wup_d)], w_up[0].T, m_w_up[0].T, v_w_up[0].T, q_up,
        "adamw_w_up"))
    big["w_out"] = _sum_adamw(_slots(r_wout), w_out[0], m_w_out[0], v_w_out[0], 128, "adamw_w_out")
    big["w_down"] = _sum_parts_adamw([_slots(r_wdown_a), _slots(r_wdown_b)], w_down[0], m_w_down[0], v_w_down[0],
                                     dn_rows // 2, "adamw_w_down")
    big["w_attn_branch"] = _sum_adamw(_slots(r_wa), w_attn_branch[0], m_w_attn_branch[0], v_w_attn_branch[0], 256,
                                      "adamw_w_attn_branch")
    big["w_conv_branch"] = _sum_adamw(_slots(r_wc), w_conv_branch[0], m_w_conv_branch[0], v_w_conv_branch[0], 256,
                                      "adamw_w_conv_branch")

    def small_adam(w, g, m, v, name):
        shp = w.shape
        w2, m2, v2 = (t.reshape(-1, shp[-1]) for t in (w, m, v))
        d, mm, vv = _adamw(w2, g.reshape(w2.shape), m2, v2, w2.shape[0], name)
        return g.reshape(shp), d.reshape(shp), mm.reshape(shp), vv.reshape(shp)

    res = {
        "mix_norm": small_adam(mix_norm, g_mix, m_mix_norm, v_mix_norm, "adamw_mix_norm"),
        "b_in": small_adam(b_in, g_bin, m_b_in, v_b_in, "adamw_b_in"),
        "sinks": small_adam(sinks, g_sinks, m_sinks, v_sinks, "adamw_sinks"),
        "conv_w": small_adam(conv_w, g_cw, m_conv_w, v_conv_w, "adamw_conv_w"),
        "ffn_norm": small_adam(ffn_norm, g_ffn, m_ffn_norm, v_ffn_norm, "adamw_ffn_norm"),
        "ffn_conv_w": small_adam(ffn_conv_w, g_fcw, m_ffn_conv_w, v_ffn_conv_w, "adamw_ffn_conv_w"),
        "final_norm": small_adam(final_norm, g_fn, m_final_norm, v_final_norm, "adamw_final_norm"),
    }
    for name, ref_w in (("w_in", w_in), ("w_up", w_up), ("w_out", w_out), ("w_down", w_down),
                        ("w_attn_branch", w_attn_branch), ("w_conv_branch", w_conv_branch)):
        res[name] = tuple(t.reshape(ref_w.shape) for t in big[name])

    order = ["mix_norm", "w_in", "b_in", "sinks", "conv_w", "w_attn_branch", "w_conv_branch", "w_out",
             "ffn_norm", "w_up", "ffn_conv_w", "w_down", "final_norm"]
    out = [loss, dx.reshape(x.shape)]
    for k in range(4):
        out += [res[name][k] for name in order]
    return tuple(out)
```

```python
import math

import jax
import jax.numpy as jnp
from jax import lax
from jax.experimental import pallas as pl
from jax.experimental.pallas import tpu as pltpu

F32 = jnp.float32
BF16 = jnp.bfloat16
MESH = pl.DeviceIdType.MESH
N_DEV = 8

D_MODEL = 1024
HEAD_DIM = 64
N_HEADS = 8
BLOCK = 128
ATTN_W = 512
KV_W = 128
CONV_W = 512
QKV_W = ATTN_W + 2 * KV_W
CBX_W = 3 * CONV_W
GATE_W = 2 * D_MODEL
IN_W = QKV_W + CBX_W + GATE_W
D_FF = 2816
FF_CHUNK = 1408
NORM_EPS = 1e-5
ATTN_SCALE = HEAD_DIM ** -0.5
NEG = -1e30
HALO = 16

ADAM_LR = 0.001
ADAM_B1 = 0.9
ADAM_B2 = 0.999
ADAM_EPS = 1e-08
ADAM_WD = 0.01
ADAM_STEP = 10

VMEM_LIMIT = 56 * 1024 * 1024
SMALL_ROWS = 32

NT = (((1,), (1,)), ((), ()))
TN = (((0,), (0,)), ((), ()))
ANY = pl.BlockSpec(memory_space=pl.ANY)


def _sig(v):
    return 1.0 / (1.0 + jnp.exp(-v))


def _row_tile(s, pref=256):
    return pref if s % pref == 0 else s


def _shifts_down(u, halo, ks):
    ext = jnp.concatenate([halo, u], axis=0)
    return [pltpu.roll(ext, k, axis=0)[HALO:, :] for k in ks]


def _shifts_up(u, halo, ks):
    n = u.shape[0]
    ext = jnp.concatenate([u, halo], axis=0)
    return [pltpu.roll(ext, n + HALO - k, axis=0)[:n, :] for k in ks]


def _shift_matrix(n, k):
    row = lax.broadcasted_iota(jnp.int32, (n, n), 0)
    col = lax.broadcasted_iota(jnp.int32, (n, n), 1)
    return jnp.where(col == row + k, 1.0, 0.0).astype(BF16)


def _mxu_shift_up(mat, ub, halo, k):
    n = ub.shape[0]
    v = jnp.dot(mat, ub, preferred_element_type=F32)
    row = lax.broadcasted_iota(jnp.int32, (8, ub.shape[1]), 0)
    tail = v[n - 8:, :]
    for t in range(k):
        tail = jnp.where(row == 8 - k + t, halo[t:t + 1, :], tail)
    return jnp.concatenate([v[:n - 8, :], tail], axis=0)


def _prev_halo_map(tm):
    return lambda i: (jnp.maximum(i * (tm // HALO) - 1, 0), 0)


def _next_halo_map(tm, s):
    return lambda i: (jnp.minimum((i + 1) * (tm // HALO), s // HALO - 1), 0)


def _full(shape):
    return pl.BlockSpec(shape, lambda *_: (0,) * len(shape))


def _resident(shape):
    return pl.BlockSpec(shape, lambda *_: (0,) * len(shape), pipeline_mode=pl.Buffered(1))


def _rows(tm, c):
    return pl.BlockSpec((tm, c), lambda i: (i, 0))


def _sds(shape, dtype):
    return jax.ShapeDtypeStruct(shape, dtype)


def _my_place():
    x, y, c = lax.axis_index("x"), lax.axis_index("y"), lax.axis_index("c")
    return x, y, c


class _AllGather:
    def __init__(self, shards):
        self.ins = list(shards)
        n = len(shards)
        self.out_shape = [_sds((N_DEV * s.shape[0], s.shape[1]), s.dtype) for s in shards]
        self.sems = [pltpu.SemaphoreType.DMA((7 * n,)), pltpu.SemaphoreType.DMA((7 * n,)),
                     pltpu.SemaphoreType.DMA((n,))]

    def _parts(self, ins, outs, sems):
        send_sems, recv_sems, local_sems = sems
        x, y, c = _my_place()
        me, sibling = (x, y, c), (x, y, 1 - c)
        chips = [(1 - x, y), (x, 1 - y), (1 - x, 1 - y)]

        def rows(k, dev):
            r = ins[k].shape[0]
            start = pl.multiple_of((4 * dev[0] + 2 * dev[1] + dev[2]) * r, 8)
            return outs[k].at[pl.ds(start, r), :]

        def copy(k, j, block, to, src=None):
            return pltpu.make_async_remote_copy(
                src_ref=rows(k, block) if src is None else src, dst_ref=rows(k, block),
                send_sem=send_sems.at[7 * k + j], recv_sem=recv_sems.at[7 * k + j],
                device_id=to, device_id_type=MESH)

        n = len(ins)
        mine = [pltpu.make_async_copy(ins[k], rows(k, me), local_sems.at[k]) for k in range(n)]
        first = []
        for k in range(n):
            first.append(copy(k, 0, me, sibling, src=ins[k]))
            first += [copy(k, 1 + j, me, (*chip, c), src=ins[k]) for j, chip in enumerate(chips)]
        return me, sibling, chips, copy, mine, first

    def start(self, ins, outs, sems):
        _, _, _, _, mine, first = self._parts(ins, outs, sems)
        for cp in mine + first:
            cp.start()

    def finish(self, ins, outs, sems):
        me, sibling, chips, copy, mine, first = self._parts(ins, outs, sems)
        c = me[2]
        n = len(ins)
        passed = []
        for j, chip in enumerate(chips):
            for k in range(n):
                copy(k, 1 + j, (*chip, c), me).wait_recv()
                fwd = copy(k, 4 + j, (*chip, c), sibling)
                fwd.start()
                passed.append(fwd)
        for k in range(n):
            copy(k, 0, sibling, me).wait_recv()
            for j, chip in enumerate(chips):
                copy(k, 4 + j, (*chip, 1 - c), me).wait_recv()
        for cp in first + passed:
            cp.wait_send()
        for cp in mine:
            cp.wait()


class _ReduceScatter:
    def __init__(self, parts, bcast=()):
        self.parts = [(lo, cnt) for _, lo, cnt in parts]
        self.n_parts = len(parts)
        self.ins = [a for a, _, _ in parts] + list(bcast)
        self.out_shape = [_sds((N_DEV * cnt, a.shape[1]), a.dtype) for a, _, cnt in parts]
        self.out_shape += [_sds((N_DEV * b.shape[0], b.shape[1]), b.dtype) for b in bcast]
        n = len(self.ins)
        self.sems = [pltpu.SemaphoreType.DMA((7 * n,)), pltpu.SemaphoreType.DMA((7 * n,)),
                     pltpu.SemaphoreType.DMA((n,))]

    def _copies(self, ins, outs, sems):
        send_sems, recv_sems, local_sems = sems
        x, y, c = _my_place()
        me_idx = 4 * x + 2 * y + c
        remote, local = [], []
        for k in range(len(ins)):
            cnt = outs[k].shape[0] // N_DEV
            dst = outs[k].at[pl.ds(pl.multiple_of(me_idx * cnt, 8), cnt), :]
            if k < self.n_parts:
                lo, _ = self.parts[k]
                r = ins[k].shape[0] // N_DEV
                src_of = lambda idx: ins[k].at[pl.ds(pl.multiple_of(idx * r + lo, 8), cnt), :]
            else:
                src_of = lambda idx: ins[k]
            local.append(pltpu.make_async_copy(src_of(me_idx), dst, local_sems.at[k]))
            for j in range(1, N_DEV):
                peer = (x ^ (j >> 2), y ^ ((j >> 1) & 1), c ^ (j & 1))
                peer_idx = 4 * peer[0] + 2 * peer[1] + peer[2]
                remote.append(pltpu.make_async_remote_copy(
                    src_ref=src_of(peer_idx), dst_ref=dst,
                    send_sem=send_sems.at[7 * k + j - 1], recv_sem=recv_sems.at[7 * k + j - 1],
                    device_id=peer, device_id_type=MESH))
        return remote, local

    def start(self, ins, outs, sems):
        remote, local = self._copies(ins, outs, sems)
        for cp in local + remote:
            cp.start()

    def finish(self, ins, outs, sems):
        remote, local = self._copies(ins, outs, sems)
        for cp in remote:
            cp.wait_recv()
        for cp in remote:
            cp.wait_send()
        for cp in local:
            cp.wait()


class _PairExchange:
    def __init__(self, arrays):
        self.ins = list(arrays)
        n = len(arrays)
        self.out_shape = [_sds((a.shape[0] // 2, a.shape[1]), a.dtype) for a in arrays]
        self.sems = [pltpu.SemaphoreType.DMA((4 * n,)), pltpu.SemaphoreType.DMA((4 * n,))]

    def _copies(self, ins, outs, sems):
        send_sems, recv_sems = sems
        x, y, c = _my_place()
        remote = []
        for k in range(len(ins)):
            r = ins[k].shape[0] // N_DEV
            for chip in range(4):
                sib = ins[k].at[pl.ds(pl.multiple_of((2 * chip + 1 - c) * r, 8), r), :]
                remote.append(pltpu.make_async_remote_copy(
                    src_ref=sib, dst_ref=outs[k].at[pl.ds(chip * r, r), :],
                    send_sem=send_sems.at[4 * k + chip], recv_sem=recv_sems.at[4 * k + chip],
                    device_id=(x, y, 1 - c), device_id_type=MESH))
        return remote

    def start(self, ins, outs, sems):
        for cp in self._copies(ins, outs, sems):
            cp.start()

    def finish(self, ins, outs, sems):
        remote = self._copies(ins, outs, sems)
        for cp in remote:
            cp.wait_recv()
        for cp in remote:
            cp.wait_send()


class _ChipExchange:
    def __init__(self, arrays):
        self.ins = list(arrays)
        self.out_shape = [_sds(a.shape, a.dtype) for a in arrays]
        n = len(self.ins)
        self.sems = [pltpu.SemaphoreType.DMA((3 * n,)), pltpu.SemaphoreType.DMA((3 * n,)),
                     pltpu.SemaphoreType.DMA((n,))]

    def _copies(self, ins, outs, sems):
        send_sems, recv_sems, local_sems = sems
        x, y, c = _my_place()
        my_chip = 2 * x + y
        remote, local = [], []
        for k in range(len(ins)):
            r = ins[k].shape[0] // 4
            dst = outs[k].at[pl.ds(pl.multiple_of(my_chip * r, 8), r), :]
            local.append(pltpu.make_async_copy(ins[k].at[pl.ds(pl.multiple_of(my_chip * r, 8), r), :], dst,
                                               local_sems.at[k]))
            for j in range(1, 4):
                px, py = x ^ (j >> 1), y ^ (j & 1)
                src = ins[k].at[pl.ds(pl.multiple_of((2 * px + py) * r, 8), r), :]
                remote.append(pltpu.make_async_remote_copy(
                    src_ref=src, dst_ref=dst, send_sem=send_sems.at[3 * k + j - 1],
                    recv_sem=recv_sems.at[3 * k + j - 1], device_id=(px, py, c), device_id_type=MESH))
        return remote, local

    def start(self, ins, outs, sems):
        remote, local = self._copies(ins, outs, sems)
        for cp in local + remote:
            cp.start()

    def finish(self, ins, outs, sems):
        remote, local = self._copies(ins, outs, sems)
        for cp in remote:
            cp.wait_recv()
        for cp in remote:
            cp.wait_send()
        for cp in local:
            cp.wait()


def _pcall(body, name, grid, in_specs, out_specs, out_shape, args, scratch=(), comm=None):
    params = pltpu.CompilerParams(dimension_semantics=("arbitrary",) * len(grid), vmem_limit_bytes=VMEM_LIMIT)
    in_specs, out_specs, out_shape, scratch = list(in_specs), list(out_specs), list(out_shape), list(scratch)
    if comm is None:
        res = pl.pallas_call(body, name=name, grid=grid, in_specs=in_specs, out_specs=out_specs, out_shape=out_shape,
                             scratch_shapes=scratch, compiler_params=params)(*args)
        return list(res), []
    n_in, n_out, n_scr = len(in_specs), len(out_specs), len(scratch)
    ci, co = len(comm.ins), len(comm.out_shape)
    total = math.prod(grid)

    def carried(*refs):
        bounds = [0, n_in, n_in + ci, n_in + ci + n_out, n_in + ci + n_out + co, n_in + ci + n_out + co + n_scr]
        ins, cins, outs, couts, scr = (refs[a:b] for a, b in zip(bounds[:-1], bounds[1:]))
        sems = refs[bounds[-1]:]
        step = pl.program_id(0)
        for d in range(1, len(grid)):
            step = step * grid[d] + pl.program_id(d)

        @pl.when(step == 0)
        def _():
            comm.start(cins, couts, sems)

        body(*ins, *outs, *scr)

        @pl.when(step == total - 1)
        def _():
            comm.finish(cins, couts, sems)

    res = pl.pallas_call(
        carried, name=name, grid=grid, in_specs=in_specs + [ANY] * ci, out_specs=out_specs + [ANY] * co,
        out_shape=out_shape + comm.out_shape, scratch_shapes=scratch + comm.sems, compiler_params=params,
    )(*args, *comm.ins)
    return list(res[:n_out]), list(res[n_out:])


def _exchange_only(comm, name):
    def body(*refs):
        ci, co = len(comm.ins), len(comm.out_shape)
        comm.start(refs[:ci], refs[ci:ci + co], refs[ci + co:])
        comm.finish(refs[:ci], refs[ci:ci + co], refs[ci + co:])

    return pl.pallas_call(body, name=name, out_shape=comm.out_shape, in_specs=[ANY] * len(comm.ins),
                          out_specs=[ANY] * len(comm.out_shape), scratch_shapes=comm.sems)(*comm.ins)


def _norm_inproj(x, g, win_t, b_in, comm):
    s = x.shape[0]
    tm = _row_tile(s, 512)
    widths = (QKV_W, CBX_W, GATE_W)

    def body(x_ref, g_ref, w_ref, b_ref, xn_ref, qkv_ref, cbx_ref, gate_ref):
        xv = x_ref[...]
        r = lax.rsqrt(jnp.mean(xv * xv, axis=-1, keepdims=True) + NORM_EPS)
        xn = (xv * r * g_ref[...]).astype(BF16)
        xn_ref[...] = xn
        off = 0
        for o_ref, w in zip((qkv_ref, cbx_ref, gate_ref), widths):
            acc = lax.dot_general(xn, w_ref[off:off + w, :], NT, preferred_element_type=F32)
            o_ref[...] = (acc + b_ref[:, off:off + w]).astype(BF16)
            off += w

    return _pcall(
        body, "norm_inproj", (s // tm,),
        [_rows(tm, D_MODEL), _full((1, D_MODEL)), _resident((IN_W, D_MODEL)), _full((1, IN_W))],
        [_rows(tm, D_MODEL)] + [_rows(tm, w) for w in widths],
        [_sds((s, D_MODEL), BF16)] + [_sds((s, w), BF16) for w in widths],
        (x, g, win_t, b_in), comm=comm)


def _attn_specs():
    prev = lambda n: jnp.maximum(n - 1, 0)
    return [pl.BlockSpec((BLOCK, ATTN_W), lambda n: (n, 0)),
            pl.BlockSpec((BLOCK, KV_W), lambda n: (prev(n), ATTN_W // KV_W)),
            pl.BlockSpec((BLOCK, KV_W), lambda n: (n, ATTN_W // KV_W)),
            pl.BlockSpec((BLOCK, KV_W), lambda n: (prev(n), ATTN_W // KV_W + 1)),
            pl.BlockSpec((BLOCK, KV_W), lambda n: (n, ATTN_W // KV_W + 1))]


def _lower_lanes():
    return lax.broadcasted_iota(jnp.int32, (BLOCK, 128), 1) < HEAD_DIM


def _stack_heads(val, kh):
    lower = _lower_lanes()
    parts = []
    for g in range(4):
        h = kh * 4 + g
        blk = val[:, (h // 2) * 128:(h // 2 + 1) * 128]
        keep = lower if h % 2 == 0 else jnp.logical_not(lower)
        parts.append(jnp.where(keep, blk, jnp.zeros_like(blk)))
    return jnp.concatenate(parts, axis=0)


def _dup_kv(prev_ref, cur_ref, kh):
    t = jnp.concatenate([prev_ref[...], cur_ref[...]], axis=0).astype(F32)
    rolled = pltpu.roll(t, HEAD_DIM, axis=1)
    lower = lax.broadcasted_iota(jnp.int32, t.shape, 1) < HEAD_DIM
    dup = jnp.where(lower, t, rolled) if kh == 0 else jnp.where(lower, rolled, t)
    return dup.astype(BF16)


def _attn_mask(n):
    row = lax.broadcasted_iota(jnp.int32, (4 * BLOCK, 2 * BLOCK), 0)
    kj = lax.broadcasted_iota(jnp.int32, (4 * BLOCK, 2 * BLOCK), 1)
    dist = (row & (BLOCK - 1)) + BLOCK - kj
    band = jnp.logical_and(dist >= 0, dist < BLOCK)
    return jnp.logical_and(band, jnp.logical_or(kj >= BLOCK, n > 0))


def _sink_col(sinks_ref, kh):
    gi = lax.broadcasted_iota(jnp.int32, (4 * BLOCK, 1), 0) // BLOCK
    col = jnp.zeros((4 * BLOCK, 1), F32)
    for g in range(4):
        col = jnp.where(gi == g, sinks_ref[0, kh * 4 + g], col)
    return col


def _attn_fwd(qkv, sinks, comm):
    s = qkv.shape[0]

    def body(sinks_ref, q_ref, kp_ref, kc_ref, vp_ref, vc_ref, o_ref, lse_ref):
        n = pl.program_id(0)
        mask = _attn_mask(n)
        lower = _lower_lanes()
        lane = lax.broadcasted_iota(jnp.int32, (BLOCK, 128), 1)
        qv = q_ref[...]
        lse_out = jnp.zeros((BLOCK, 128), F32)
        for kh in range(2):
            qs = _stack_heads(qv, kh)
            kd, vd = _dup_kv(kp_ref, kc_ref, kh), _dup_kv(vp_ref, vc_ref, kh)
            sc = lax.dot_general(qs, kd, NT, preferred_element_type=F32) * ATTN_SCALE
            sc = jnp.where(mask, sc, NEG)
            sink = _sink_col(sinks_ref, kh)
            m = jnp.maximum(jnp.max(sc, axis=1, keepdims=True), sink)
            p = jnp.exp(sc - m)
            l = jnp.sum(p, axis=1, keepdims=True) + jnp.exp(sink - m)
            o = jnp.dot(p.astype(BF16), vd, preferred_element_type=F32) / l
            lse = m + jnp.log(l)
            for pair in range(2):
                lo = o[(2 * pair) * BLOCK:(2 * pair + 1) * BLOCK]
                hi = o[(2 * pair + 1) * BLOCK:(2 * pair + 2) * BLOCK]
                col = (kh * 2 + pair) * 128
                o_ref[:, col:col + 128] = jnp.where(lower, lo, hi).astype(BF16)
            for g in range(4):
                lse_out = jnp.where(lane == kh * 4 + g, lse[g * BLOCK:(g + 1) * BLOCK], lse_out)
        lse_ref[...] = lse_out

    return _pcall(
        body, "attn_fwd", (s // BLOCK,),
        [pl.BlockSpec(memory_space=pltpu.SMEM)] + _attn_specs(),
        [pl.BlockSpec((BLOCK, ATTN_W), lambda n: (n, 0)), pl.BlockSpec((BLOCK, 128), lambda n: (n, 0))],
        [_sds((s, ATTN_W), BF16), _sds((s, 128), F32)],
        (sinks, qkv, qkv, qkv, qkv, qkv), comm=comm)


def _conv_u(cbx_ref, halo_ref, w_ref, first):
    cb = cbx_ref[:, 0:CONV_W].astype(F32)
    cc = cbx_ref[:, CONV_W:2 * CONV_W].astype(F32)
    cx = cbx_ref[:, 2 * CONV_W:3 * CONV_W].astype(F32)
    u = cc * cx
    uh = halo_ref[:, CONV_W:2 * CONV_W].astype(F32) * halo_ref[:, 2 * CONV_W:3 * CONV_W].astype(F32)
    uh = jnp.where(first, 0.0, uh)
    u1, u2 = _shifts_down(u, uh, (1, 2))
    cv = w_ref[0:1, :] * u2 + w_ref[1:2, :] * u1 + w_ref[2:3, :] * u
    return cb, cc, cx, u, cv


def _mix_fwd(x, cbx, gates, attn, conv_w, wa, wc, wout, comm):
    s = x.shape[0]
    tm = _row_tile(s)

    def body(x_ref, cbx_ref, halo_ref, gate_ref, attn_ref, cw_ref, wa_ref, wc_ref, wo_ref,
             conv_ref, ap_ref, cp_ref, mg_ref, h1_ref):
        first = pl.program_id(0) == 0
        cb, _, _, _, cv = _conv_u(cbx_ref, halo_ref, cw_ref, first)
        conv = (cb * cv).astype(BF16)
        conv_ref[...] = conv
        ap = jnp.dot(attn_ref[...], wa_ref[...], preferred_element_type=F32)
        cp = jnp.dot(conv, wc_ref[...], preferred_element_type=F32)
        ap_ref[...] = ap.astype(BF16)
        cp_ref[...] = cp.astype(BF16)
        ga = gate_ref[:, 0:D_MODEL].astype(F32)
        gc = gate_ref[:, D_MODEL:2 * D_MODEL].astype(F32)
        merged = (_sig(ga) * ap + _sig(gc) * cp).astype(BF16)
        mg_ref[...] = merged
        h1_ref[...] = x_ref[...] + jnp.dot(merged, wo_ref[...], preferred_element_type=F32)

    return _pcall(
        body, "mix_fwd", (s // tm,),
        [_rows(tm, D_MODEL), _rows(tm, CBX_W), pl.BlockSpec((HALO, CBX_W), _prev_halo_map(tm)),
         _rows(tm, GATE_W), _rows(tm, ATTN_W), _full((3, CONV_W)), _full((ATTN_W, D_MODEL)),
         _full((CONV_W, D_MODEL)), _full((D_MODEL, D_MODEL))],
        [_rows(tm, CONV_W), _rows(tm, D_MODEL), _rows(tm, D_MODEL), _rows(tm, D_MODEL), _rows(tm, D_MODEL)],
        [_sds((s, CONV_W), BF16), _sds((s, D_MODEL), BF16), _sds((s, D_MODEL), BF16), _sds((s, D_MODEL), BF16),
         _sds((s, D_MODEL), F32)],
        (x, cbx, cbx, gates, attn, conv_w, wa, wc, wout), comm=comm)


def _ffn_up(h1, g, wup_lo, wup_hi, comm):
    s = h1.shape[0]
    tm = _row_tile(s, 512)
    half = D_MODEL // 2

    def body(h_ref, g_ref, wl_ref, wh_ref, hn_ref, up_ref):
        hv = h_ref[...]
        r = lax.rsqrt(jnp.mean(hv * hv, axis=-1, keepdims=True) + NORM_EPS)
        hn = (hv * r * g_ref[...]).astype(BF16)
        hn_ref[...] = hn
        for c in range(2 * D_FF // FF_CHUNK):
            sl = slice(c * FF_CHUNK, (c + 1) * FF_CHUNK)
            acc = lax.dot_general(hn[:, :half], wl_ref[sl, :], NT, preferred_element_type=F32)
            acc = acc + lax.dot_general(hn[:, half:], wh_ref[sl, :], NT, preferred_element_type=F32)
            up_ref[:, sl] = acc.astype(BF16)

    return _pcall(
        body, "ffn_up", (s // tm,),
        [_rows(tm, D_MODEL), _full((1, D_MODEL)), _resident((2 * D_FF, half)), _resident((2 * D_FF, half))],
        [_rows(tm, D_MODEL), _rows(tm, 2 * D_FF)],
        [_sds((s, D_MODEL), BF16), _sds((s, 2 * D_FF), BF16)],
        (h1, g, wup_lo, wup_hi), comm=comm)


def _ffn_conv_cols(up_ref, halo_ref, fcw_ref, first, off):
    u = up_ref[:, off:off + FF_CHUNK].astype(F32)
    uh = jnp.where(first, 0.0, halo_ref[:, off:off + FF_CHUNK].astype(F32))
    w = fcw_ref[:, off:off + FF_CHUNK]
    u1, u2 = _shifts_down(u, uh, (1, 2))
    return w[0:1] * u2 + w[1:2] * u1 + w[2:3] * u


def _ffn_down_loss(up_pre, fcw, wdown, h1, fnorm, target):
    s = h1.shape[0]
    tm = _row_tile(s)

    def body(up_ref, halo_ref, fcw_ref, wd_ref, h1_ref, fn_ref, t_ref, cu_ref, act_ref, dh2_ref, loss_ref, dfn_ref):
        i = pl.program_id(0)

        @pl.when(i == 0)
        def _():
            loss_ref[...] = jnp.zeros_like(loss_ref)
            dfn_ref[...] = jnp.zeros_like(dfn_ref)

        h2 = h1_ref[...]
        for c in range(D_FF // FF_CHUNK):
            gsl = slice(c * FF_CHUNK, (c + 1) * FF_CHUNK)
            vsl = slice(D_FF + c * FF_CHUNK, D_FF + (c + 1) * FF_CHUNK)
            gate = _ffn_conv_cols(up_ref, halo_ref, fcw_ref, i == 0, c * FF_CHUNK)
            cu_ref[:, gsl] = gate.astype(BF16)
            val = _ffn_conv_cols(up_ref, halo_ref, fcw_ref, i == 0, D_FF + c * FF_CHUNK)
            cu_ref[:, vsl] = val.astype(BF16)
            act = (gate * _sig(gate) * val).astype(BF16)
            act_ref[:, gsl] = act
            h2 = h2 + jnp.dot(act, wd_ref[gsl, :], preferred_element_type=F32)
        r = lax.rsqrt(jnp.mean(h2 * h2, axis=-1, keepdims=True) + NORM_EPS)
        yhat = h2 * r
        fn = fn_ref[...]
        diff = yhat * fn - t_ref[...]
        loss_ref[...] += 0.5 * jnp.sum(jnp.sum(diff * diff, axis=1, keepdims=True), axis=0, keepdims=True) / D_MODEL
        dy = diff * (1.0 / D_MODEL)
        dfn_ref[...] += jnp.sum(dy * yhat, axis=0, keepdims=True)
        dyh = dy * fn
        dh2_ref[...] = r * (dyh - yhat * jnp.mean(dyh * yhat, axis=-1, keepdims=True))

    return _pcall(
        body, "ffn_down_loss", (s // tm,),
        [_rows(tm, 2 * D_FF), pl.BlockSpec((HALO, 2 * D_FF), _prev_halo_map(tm)), _full((3, 2 * D_FF)),
         _resident((D_FF, D_MODEL)), _rows(tm, D_MODEL), _full((1, D_MODEL)), _rows(tm, D_MODEL)],
        [_rows(tm, 2 * D_FF), _rows(tm, D_FF), _rows(tm, D_MODEL), _full((1, 128)), _full((1, D_MODEL))],
        [_sds((s, 2 * D_FF), BF16), _sds((s, D_FF), BF16), _sds((s, D_MODEL), F32), _sds((1, 128), F32),
         _sds((1, D_MODEL), F32)],
        (up_pre, up_pre, fcw, wdown, h1, fnorm, target))[0]


def _ffn_act_bwd(dh2, wdown, up, comm):
    s = dh2.shape[0]
    tm = _row_tile(s)

    def body(dh_ref, wd_ref, up_ref, dup_ref):
        dh = dh_ref[...].astype(BF16)
        for c in range(D_FF // FF_CHUNK):
            gsl = slice(c * FF_CHUNK, (c + 1) * FF_CHUNK)
            vsl = slice(D_FF + c * FF_CHUNK, D_FF + (c + 1) * FF_CHUNK)
            dact = lax.dot_general(dh, wd_ref[gsl, :], NT, preferred_element_type=F32)
            gate = up_ref[:, gsl].astype(F32)
            val = up_ref[:, vsl].astype(F32)
            sg = _sig(gate)
            dup_ref[:, gsl] = (dact * val * (sg * (1.0 + gate * (1.0 - sg)))).astype(BF16)
            dup_ref[:, vsl] = (dact * gate * sg).astype(BF16)

    return _pcall(
        body, "ffn_act_bwd", (s // tm,),
        [_rows(tm, D_MODEL), _resident((D_FF, D_MODEL)), _rows(tm, 2 * D_FF)],
        [_rows(tm, 2 * D_FF)], [_sds((s, 2 * D_FF), BF16)],
        (dh2, wdown, up), comm=comm)


def _conv_bwd(dy, x, w, width, chunk, name, comm):
    s = dy.shape[0]
    tm = _row_tile(s)

    def body(dy_ref, dyn_ref, x_ref, w_ref, dx_ref, dw_ref):
        i = pl.program_id(0)

        @pl.when(i == 0)
        def _():
            dw_ref[...] = jnp.zeros_like(dw_ref)

        last = i == s // tm - 1
        up1, up2 = _shift_matrix(tm, 1), _shift_matrix(tm, 2)
        for c in range(width // chunk):
            sl = slice(c * chunk, (c + 1) * chunk)
            db = dy_ref[:, sl]
            d = db.astype(F32)
            dn = jnp.where(last, 0.0, dyn_ref[:, sl].astype(F32))
            xv = x_ref[:, sl].astype(F32)
            wv = w_ref[:, sl]
            d1 = _mxu_shift_up(up1, db, dn, 1)
            d2 = _mxu_shift_up(up2, db, dn, 2)
            dx = wv[2:3] * d + wv[1:2] * d1 + wv[0:1] * d2
            dx_ref[:, sl] = dx.astype(BF16)
            dw_ref[0:1, sl] += jnp.sum(d2 * xv, axis=0, keepdims=True)
            dw_ref[1:2, sl] += jnp.sum(d1 * xv, axis=0, keepdims=True)
            dw_ref[2:3, sl] += jnp.sum(d * xv, axis=0, keepdims=True)

    return _pcall(
        body, name, (s // tm,),
        [_rows(tm, width), pl.BlockSpec((HALO, width), _next_halo_map(tm, s)), _rows(tm, width),
         _full((3, width))],
        [_rows(tm, width), _full((3, width))],
        [_sds((s, width), BF16), _sds((3, width), F32)],
        (dy, dy, x, w), comm=comm)


def _matmul_tn(a, b, tk, name, ts=1024, comm=None):
    s, ka = a.shape
    n = b.shape[1]
    ts = min(ts, s)
    steps = s // ts

    def body(a_ref, b_ref, o_ref, acc_ref):
        j = pl.program_id(1)

        @pl.when(j == 0)
        def _():
            acc_ref[...] = jnp.zeros_like(acc_ref)

        acc_ref[...] += lax.dot_general(a_ref[...].astype(BF16), b_ref[...].astype(BF16), TN,
                                        preferred_element_type=F32)

        @pl.when(j == steps - 1)
        def _():
            o_ref[...] = acc_ref[...].astype(BF16)

    outs, couts = _pcall(
        body, name, (ka // tk, steps),
        [pl.BlockSpec((ts, tk), lambda i, j: (j, i)), pl.BlockSpec((ts, n), lambda i, j: (j, 0))],
        [pl.BlockSpec((tk, n), lambda i, j: (i, 0))], [_sds((ka, n), BF16)],
        (a, b), scratch=[pltpu.VMEM((tk, n), F32)], comm=comm)
    return outs[0] if comm is None else (outs[0], couts)


def _norm_bwd_tile(xv, g, dy):
    r = lax.rsqrt(jnp.mean(xv * xv, axis=-1, keepdims=True) + NORM_EPS)
    xhat = xv * r
    dg = jnp.sum(dy * xhat, axis=0, keepdims=True)
    dyh = dy * g
    return r * (dyh - xhat * jnp.mean(dyh * xhat, axis=-1, keepdims=True)), dg


def _ffn_up_bwd(dup_pre, wup_lo, wup_hi, h1, g, dh2, comm):
    s = h1.shape[0]
    tm = _row_tile(s, 512)
    half = D_MODEL // 2

    def body(du_ref, wl_ref, wh_ref, h_ref, g_ref, dh2_ref, dh1_ref, dg_ref):
        @pl.when(pl.program_id(0) == 0)
        def _():
            dg_ref[...] = jnp.zeros_like(dg_ref)

        du = du_ref[...]
        dhn = jnp.concatenate([jnp.dot(du, wl_ref[...], preferred_element_type=F32),
                               jnp.dot(du, wh_ref[...], preferred_element_type=F32)], axis=1)
        dx, dg = _norm_bwd_tile(h_ref[...], g_ref[...], dhn)
        dg_ref[...] += dg
        dh1_ref[...] = dh2_ref[...] + dx

    return _pcall(
        body, "ffn_up_bwd", (s // tm,),
        [_rows(tm, 2 * D_FF), _resident((2 * D_FF, half)), _resident((2 * D_FF, half)), _rows(tm, D_MODEL),
         _full((1, D_MODEL)), _rows(tm, D_MODEL)],
        [_rows(tm, D_MODEL), _full((1, D_MODEL))],
        [_sds((s, D_MODEL), F32), _sds((1, D_MODEL), F32)],
        (dup_pre, wup_lo, wup_hi, h1, g, dh2), comm=comm)


def _mix_bwd(dh1, wout, gates, ap, cp, wa, wc, cbx, conv_w, comm):
    s = dh1.shape[0]
    tm = _row_tile(s)

    def body(dh_ref, wo_ref, gate_ref, ap_ref, cp_ref, wa_ref, wc_ref, cbx_ref, halo_ref, cw_ref,
             dg_ref, da_ref, dc_ref, dattn_ref, dcb_ref, dcv_ref):
        first = pl.program_id(0) == 0
        dm = lax.dot_general(dh_ref[...].astype(BF16), wo_ref[...], NT, preferred_element_type=F32)
        sa = _sig(gate_ref[:, 0:D_MODEL].astype(F32))
        sc = _sig(gate_ref[:, D_MODEL:2 * D_MODEL].astype(F32))
        da = (dm * sa).astype(BF16)
        dc = (dm * sc).astype(BF16)
        da_ref[...] = da
        dc_ref[...] = dc
        dg_ref[:, 0:D_MODEL] = (dm * ap_ref[...].astype(F32) * sa * (1.0 - sa)).astype(BF16)
        dg_ref[:, D_MODEL:2 * D_MODEL] = (dm * cp_ref[...].astype(F32) * sc * (1.0 - sc)).astype(BF16)
        dattn_ref[...] = lax.dot_general(da, wa_ref[...], NT, preferred_element_type=F32).astype(BF16)
        dconv = lax.dot_general(dc, wc_ref[...], NT, preferred_element_type=F32)
        cb, _, _, _, cv = _conv_u(cbx_ref, halo_ref, cw_ref, first)
        dcb_ref[...] = (dconv * cv).astype(BF16)
        dcv_ref[...] = (dconv * cb).astype(BF16)

    return _pcall(
        body, "mix_bwd", (s // tm,),
        [_rows(tm, D_MODEL), _full((D_MODEL, D_MODEL)), _rows(tm, GATE_W), _rows(tm, D_MODEL),
         _rows(tm, D_MODEL), _full((ATTN_W, D_MODEL)), _full((CONV_W, D_MODEL)), _rows(tm, CBX_W),
         pl.BlockSpec((HALO, CBX_W), _prev_halo_map(tm)), _full((3, CONV_W))],
        [_rows(tm, GATE_W), _rows(tm, D_MODEL), _rows(tm, D_MODEL), _rows(tm, ATTN_W),
         _rows(tm, CONV_W), _rows(tm, CONV_W)],
        [_sds((s, GATE_W), BF16), _sds((s, D_MODEL), BF16), _sds((s, D_MODEL), BF16), _sds((s, ATTN_W), BF16),
         _sds((s, CONV_W), BF16), _sds((s, CONV_W), BF16)],
        (dh1, wout, gates, ap, cp, wa, wc, cbx, cbx, conv_w), comm=comm)


def _conv_branch_bwd(dcv, cbx, conv_w):
    s = dcv.shape[0]
    tm = _row_tile(s)

    def body(d_ref, dn_ref, cbx_ref, w_ref, dcc_ref, dcx_ref, dw_ref):
        i = pl.program_id(0)

        @pl.when(i == 0)
        def _():
            dw_ref[...] = jnp.zeros_like(dw_ref)

        last = i == s // tm - 1
        cc = cbx_ref[:, CONV_W:2 * CONV_W].astype(F32)
        cx = cbx_ref[:, 2 * CONV_W:3 * CONV_W].astype(F32)
        u = cc * cx
        d = d_ref[...].astype(F32)
        dn = jnp.where(last, 0.0, dn_ref[...].astype(F32))
        d1, d2 = _shifts_up(d, dn, (1, 2))
        du = w_ref[2:3, :] * d + w_ref[1:2, :] * d1 + w_ref[0:1, :] * d2
        dcc_ref[...] = (du * cx).astype(BF16)
        dcx_ref[...] = (du * cc).astype(BF16)
        dw_ref[0:1, :] += jnp.sum(d2 * u, axis=0, keepdims=True)
        dw_ref[1:2, :] += jnp.sum(d1 * u, axis=0, keepdims=True)
        dw_ref[2:3, :] += jnp.sum(d * u, axis=0, keepdims=True)

    return _pcall(
        body, "conv_branch_bwd", (s // tm,),
        [_rows(tm, CONV_W), pl.BlockSpec((HALO, CONV_W), _next_halo_map(tm, s)), _rows(tm, CBX_W),
         _full((3, CONV_W))],
        [_rows(tm, CONV_W), _rows(tm, CONV_W), _full((3, CONV_W))],
        [_sds((s, CONV_W), BF16), _sds((s, CONV_W), BF16), _sds((3, CONV_W), F32)],
        (dcv, dcv, cbx, conv_w))[0]


def _attn_bwd(qkv, sinks, attn, lse, dattn, comm):
    s = qkv.shape[0]

    def body(sinks_ref, q_ref, kp_ref, kc_ref, vp_ref, vc_ref, o_ref, lse_ref, do_ref,
             dq_ref, dk_ref, dv_ref, ds_ref):
        n = pl.program_id(0)

        @pl.when(n == 0)
        def _():
            dk_ref[...] = jnp.zeros_like(dk_ref)
            dv_ref[...] = jnp.zeros_like(dv_ref)
            ds_ref[...] = jnp.zeros_like(ds_ref)

        mask = _attn_mask(n)
        lower = _lower_lanes()
        lane = lax.broadcasted_iota(jnp.int32, (BLOCK, 128), 1)
        lower2 = lax.broadcasted_iota(jnp.int32, (2 * BLOCK, 128), 1) < HEAD_DIM
        lane1 = lax.broadcasted_iota(jnp.int32, (1, 128), 1)
        qv, ov, dov, lsev = q_ref[...], o_ref[...], do_ref[...], lse_ref[...]
        dk_fold, dv_fold = [], []
        dsink = jnp.zeros((1, 128), F32)
        for kh in range(2):
            qs = _stack_heads(qv, kh)
            dos = _stack_heads(dov, kh)
            os_ = _stack_heads(ov, kh)
            kd, vd = _dup_kv(kp_ref, kc_ref, kh), _dup_kv(vp_ref, vc_ref, kh)
            lse = jnp.concatenate(
                [jnp.sum(jnp.where(lane == kh * 4 + g, lsev, 0.0), axis=1, keepdims=True) for g in range(4)], axis=0)
            sc = lax.dot_general(qs, kd, NT, preferred_element_type=F32) * ATTN_SCALE
            p = jnp.exp(jnp.where(mask, sc, NEG) - lse)
            dp = lax.dot_general(dos, vd, NT, preferred_element_type=F32)
            delta = jnp.sum(dos.astype(F32) * os_.astype(F32), axis=1, keepdims=True)
            dsc = (p * (dp - delta) * ATTN_SCALE).astype(BF16)
            dqs = jnp.dot(dsc, kd, preferred_element_type=F32)
            for pair in range(2):
                lo = dqs[(2 * pair) * BLOCK:(2 * pair + 1) * BLOCK]
                hi = dqs[(2 * pair + 1) * BLOCK:(2 * pair + 2) * BLOCK]
                col = (kh * 2 + pair) * 128
                dq_ref[:, col:col + 128] = jnp.where(lower, lo, hi).astype(BF16)
            dkd = lax.dot_general(dsc, qs, TN, preferred_element_type=F32)
            dvd = lax.dot_general(p.astype(BF16), dos, TN, preferred_element_type=F32)
            dk_fold.append(dkd + pltpu.roll(dkd, HEAD_DIM, axis=1))
            dv_fold.append(dvd + pltpu.roll(dvd, HEAD_DIM, axis=1))
            psink = jnp.exp(_sink_col(sinks_ref, kh) - lse) * delta
            for g in range(4):
                tot = jnp.sum(psink[g * BLOCK:(g + 1) * BLOCK], axis=0, keepdims=True)
                dsink = dsink - jnp.where(lane1 == kh * 4 + g, tot, 0.0)
        dk2 = jnp.where(lower2, dk_fold[0], dk_fold[1])
        dv2 = jnp.where(lower2, dv_fold[0], dv_fold[1])
        ds_ref[...] += dsink
        cur = pl.ds(pl.multiple_of(n * BLOCK, BLOCK), BLOCK)
        dk_ref[cur, :] += dk2[BLOCK:]
        dv_ref[cur, :] += dv2[BLOCK:]

        @pl.when(n > 0)
        def _():
            prev = pl.ds(pl.multiple_of((n - 1) * BLOCK, BLOCK), BLOCK)
            dk_ref[prev, :] += dk2[:BLOCK]
            dv_ref[prev, :] += dv2[:BLOCK]

    blk = lambda w: pl.BlockSpec((BLOCK, w), lambda n: (n, 0))
    return _pcall(
        body, "attn_bwd", (s // BLOCK,),
        [pl.BlockSpec(memory_space=pltpu.SMEM)] + _attn_specs() + [blk(ATTN_W), blk(128), blk(ATTN_W)],
        [blk(ATTN_W), _full((s, KV_W)), _full((s, KV_W)), _full((1, 128))],
        [_sds((s, ATTN_W), BF16), _sds((s, KV_W), F32), _sds((s, KV_W), F32), _sds((1, 128), F32)],
        (sinks, qkv, qkv, qkv, qkv, qkv, attn, lse, dattn), comm=comm)


DPROJ_PIECES = (ATTN_W, KV_W, KV_W, CONV_W, CONV_W, CONV_W, GATE_W)
DPROJ_OFFSETS = tuple(sum(DPROJ_PIECES[:k]) for k in range(len(DPROJ_PIECES)))


def _grad_w_in(pieces, xn, comm):
    s = xn.shape[0]
    ts = min(1024, s)
    steps = s // ts
    rows0 = DPROJ_OFFSETS[6]

    def body(*refs):
        p_refs, b_ref, o_ref, acc_ref, stage_ref, sem = refs[:7], refs[7], refs[8], refs[9], refs[10], refs[11]
        i, j = pl.program_id(0), pl.program_id(1)

        @pl.when(j == 0)
        def _():
            acc_ref[...] = jnp.zeros_like(acc_ref)

        bv = b_ref[...]

        def flush(lo, n):
            stage_ref[0:n, :] = acc_ref[0:n, :].astype(BF16)
            cp = pltpu.make_async_copy(stage_ref.at[0:n, :], o_ref.at[lo:lo + n, :], sem)
            cp.start()
            cp.wait()

        @pl.when(i == 0)
        def _():
            for p_ref, off, w in zip(p_refs[:6], DPROJ_OFFSETS[:6], DPROJ_PIECES[:6]):
                acc_ref[off:off + w, :] += lax.dot_general(p_ref[...].astype(BF16), bv, TN,
                                                           preferred_element_type=F32)

            @pl.when(j == steps - 1)
            def _():
                flush(0, rows0)

        @pl.when(i == 1)
        def _():
            acc_ref[0:GATE_W, :] += lax.dot_general(p_refs[6][...], bv, TN, preferred_element_type=F32)

            @pl.when(j == steps - 1)
            def _():
                flush(rows0, GATE_W)

    def piece_spec(w, group):
        return pl.BlockSpec((ts, w), lambda i, j: (jnp.where(i == group, j, 0), 0))

    outs, couts = _pcall(
        body, "grad_w_in", (2, steps),
        [piece_spec(w, 0) for w in DPROJ_PIECES[:6]] + [piece_spec(GATE_W, 1),
                                                         pl.BlockSpec((ts, D_MODEL), lambda i, j: (j, 0))],
        [ANY], [_sds((IN_W, D_MODEL), BF16)], (*pieces, xn),
        scratch=[pltpu.VMEM((rows0, D_MODEL), F32), pltpu.VMEM((rows0, D_MODEL), BF16), pltpu.SemaphoreType.DMA],
        comm=comm)
    return outs[0], couts


def _inproj_bwd(pieces, win_t, x, g, dh1, comm):
    s = x.shape[0]
    tm = _row_tile(s, 512)

    def body(*refs):
        p_refs = refs[:7]
        w_ref, x_ref, g_ref, dh_ref, dx_ref, db_ref, dg_ref = refs[7:]

        @pl.when(pl.program_id(0) == 0)
        def _():
            db_ref[...] = jnp.zeros_like(db_ref)
            dg_ref[...] = jnp.zeros_like(dg_ref)

        dxn = jnp.zeros((tm, D_MODEL), F32)
        for p_ref, off, w in zip(p_refs, DPROJ_OFFSETS, DPROJ_PIECES):
            v = p_ref[...].astype(BF16)
            db_ref[:, off:off + w] += jnp.sum(v.astype(F32), axis=0, keepdims=True)
            dxn = dxn + jnp.dot(v, w_ref[off:off + w, :], preferred_element_type=F32)
        dx, dg = _norm_bwd_tile(x_ref[...], g_ref[...], dxn)
        dg_ref[...] += dg
        dx_ref[...] = dh_ref[...] + dx

    return _pcall(
        body, "inproj_bwd", (s // tm,),
        [_rows(tm, w) for w in DPROJ_PIECES] + [_resident((IN_W, D_MODEL)), _rows(tm, D_MODEL), _full((1, D_MODEL)),
                                                _rows(tm, D_MODEL)],
        [_rows(tm, D_MODEL), _full((1, IN_W)), _full((1, D_MODEL))],
        [_sds((s, D_MODEL), F32), _sds((1, IN_W), F32), _sds((1, D_MODEL), F32)],
        (*pieces, win_t, x, g, dh1), comm=comm)


def _adam_math(w, g, m, v):
    m2 = ADAM_B1 * m + (1.0 - ADAM_B1) * g
    v2 = ADAM_B2 * v + (1.0 - ADAM_B2) * (g * g)
    m_hat = m2 / (1.0 - ADAM_B1 ** ADAM_STEP)
    v_hat = v2 / (1.0 - ADAM_B2 ** ADAM_STEP)
    delta = -ADAM_LR * (m_hat / (jnp.sqrt(v_hat) + ADAM_EPS) + ADAM_WD * w)
    return delta, m2, v2


def _sum_slots(ref):
    tot = ref[0].astype(F32)
    for i in range(1, ref.shape[0]):
        tot = tot + ref[i].astype(F32)
    return tot


def _pair_add(mine, theirs, tr, name):
    r, c = mine.shape

    def body(a_ref, b_ref, o_ref):
        o_ref[...] = (a_ref[...].astype(F32) + b_ref[...].astype(F32)).astype(BF16)

    spec = pl.BlockSpec((tr, c), lambda i: (i, 0))
    return _pcall(body, name, (r // tr,), [spec, spec], [spec], [_sds((r, c), BF16)], (mine, theirs))[0][0]


def _sum_adamw(parts, w, m, v, tr, name):
    r, c = w.shape

    def body(p_ref, w_ref, m_ref, v_ref, g_ref, d_ref, m2_ref, v2_ref):
        g = _sum_slots(p_ref)
        g_ref[...] = g
        d_ref[...], m2_ref[...], v2_ref[...] = _adam_math(w_ref[...], g, m_ref[...], v_ref[...])

    spec = pl.BlockSpec((tr, c), lambda i: (i, 0))
    return _pcall(body, name, (r // tr,), [pl.BlockSpec((N_DEV, tr, c), lambda i: (0, i, 0)), spec, spec, spec],
                  [spec] * 4, [_sds((r, c), F32)] * 4, (parts, w, m, v))[0]


def _sum_parts_adamw(parts, w, m, v, tr, name):
    c = w.shape[1]
    tiles = [p.shape[1] // tr for p in parts]
    starts = [sum(tiles[:k]) for k in range(len(parts))]
    n_parts = len(parts)

    def body(*refs):
        p_refs = refs[:n_parts]
        w_ref, m_ref, v_ref, g_ref, d_ref, m2_ref, v2_ref = refs[n_parts:]
        i = pl.program_id(0)
        for p_ref, st, nt in zip(p_refs, starts, tiles):
            @pl.when(jnp.logical_and(i >= st, i < st + nt))
            def _(p_ref=p_ref):
                g_ref[...] = _sum_slots(p_ref)

        d_ref[...], m2_ref[...], v2_ref[...] = _adam_math(w_ref[...], g_ref[...], m_ref[...], v_ref[...])

    def part_spec(p, st, nt):
        return pl.BlockSpec((p.shape[0], tr, c), lambda i: (0, jnp.clip(i - st, 0, nt - 1), 0))

    spec = pl.BlockSpec((tr, c), lambda i: (i, 0))
    return _pcall(
        body, name, (sum(tiles),),
        [part_spec(p, st, nt) for p, st, nt in zip(parts, starts, tiles)] + [spec, spec, spec],
        [spec] * 4, [_sds(w.shape, F32)] * 4, (*parts, w, m, v))[0]


ROW_MIX, ROW_FFN, ROW_FINAL, ROW_SINKS, ROW_LOSS, ROW_BIN, ROW_CW, ROW_FCW = 0, 1, 2, 3, 4, 5, 10, 13
FCW_ROWS = 6


def _wide_pieces(width):
    return [(k * D_MODEL, min(D_MODEL, width - k * D_MODEL)) for k in range(-(-width // D_MODEL))]


def _pack_small(dmix, dffn, dfn, dsink, loss, dbin, dcw, dfcw):
    def body(mix_ref, ffn_ref, fn_ref, sink_ref, loss_ref, bin_ref, cw_ref, fcw_ref, o_ref):
        o_ref[...] = jnp.zeros_like(o_ref)
        o_ref[ROW_MIX:ROW_MIX + 1, :] = mix_ref[...]
        o_ref[ROW_FFN:ROW_FFN + 1, :] = ffn_ref[...]
        o_ref[ROW_FINAL:ROW_FINAL + 1, :] = fn_ref[...]
        o_ref[ROW_SINKS:ROW_SINKS + 1, 0:128] = sink_ref[...]
        o_ref[ROW_LOSS:ROW_LOSS + 1, 0:128] = loss_ref[...]
        for k, (off, w) in enumerate(_wide_pieces(IN_W)):
            o_ref[ROW_BIN + k:ROW_BIN + k + 1, 0:w] = bin_ref[:, off:off + w]
        o_ref[ROW_CW:ROW_CW + 3, 0:CONV_W] = cw_ref[...]
        for a in range(3):
            for k, (off, w) in enumerate(_wide_pieces(2 * D_FF)):
                row = ROW_FCW + FCW_ROWS * a + k
                o_ref[row:row + 1, 0:w] = fcw_ref[a:a + 1, off:off + w]

    return pl.pallas_call(body, name="pack_small", out_shape=_sds((SMALL_ROWS, D_MODEL), F32))(
        dmix, dffn, dfn, dsink, loss, dbin, dcw, dfcw)


def _small_sums_adamw(r_small, params):
    rows = (ROW_MIX, ROW_BIN, ROW_SINKS, ROW_FFN, ROW_FINAL)

    def body(*refs):
        r_ref, p_refs, o_refs = refs[0], refs[1:16], refs[16:]
        tot = _sum_slots(r_ref)
        for k, row in enumerate(rows):
            w_ref, m_ref, v_ref = p_refs[3 * k:3 * k + 3]
            g_ref, d_ref, m2_ref, v2_ref = o_refs[4 * k:4 * k + 4]
            for j, (off, w) in enumerate(_wide_pieces(w_ref.shape[1])):
                g_ref[:, off:off + w] = tot[row + j:row + j + 1, 0:w]
            d_ref[...], m2_ref[...], v2_ref[...] = _adam_math(w_ref[...], g_ref[...], m_ref[...], v_ref[...])
        cw_ref, fcw_ref, loss_ref = o_refs[20:]
        cw_ref[...] = tot[ROW_CW:ROW_CW + 3, 0:CONV_W]
        for a in range(3):
            for j, (off, w) in enumerate(_wide_pieces(2 * D_FF)):
                row = ROW_FCW + FCW_ROWS * a + j
                fcw_ref[a:a + 1, off:off + w] = tot[row:row + 1, 0:w]
        loss_ref[...] = tot[ROW_LOSS:ROW_LOSS + 1, 0:128]

    flat = [t for p in params for t in p]
    out_shape = [_sds(p[0].shape, F32) for p in params for _ in range(4)]
    out_shape += [_sds((3, CONV_W), F32), _sds((3, 2 * D_FF), F32), _sds((1, 128), F32)]
    res = pl.pallas_call(body, name="small_sums_adamw", out_shape=out_shape)(r_small, *flat)
    return [tuple(res[4 * k:4 * k + 4]) for k in range(5)], res[20], res[21], res[22]


def _adamw_pair(a, b):
    def body(*refs):
        for k in range(2):
            w_ref, g_ref, m_ref, v_ref = refs[4 * k:4 * k + 4]
            d_ref, m2_ref, v2_ref = refs[8 + 3 * k:8 + 3 * k + 3]
            d_ref[...], m2_ref[...], v2_ref[...] = _adam_math(w_ref[...], g_ref[...], m_ref[...], v_ref[...])

    out_shape = [_sds(a[0].shape, F32)] * 3 + [_sds(b[0].shape, F32)] * 3
    res = pl.pallas_call(body, name="adamw_conv_weights", out_shape=out_shape)(*a, *b)
    return tuple(res[:3]), tuple(res[3:])


def _pad_cols(a, c):
    return jnp.pad(a, ((0, 0), (0, c - a.shape[1])))


def _to_col_slabs(g):
    r = g.shape[0]
    return jnp.transpose(g.reshape(r, N_DEV, 128), (1, 0, 2)).reshape(N_DEV * r, 128)


def _from_col_slabs(t):
    r = t.shape[0] // N_DEV
    return jnp.transpose(t.reshape(N_DEV, r, 128), (1, 0, 2)).reshape(r, N_DEV * 128)


def _slots(t):
    return t.reshape(N_DEV, t.shape[0] // N_DEV, t.shape[1])


def kernel(x, mix_norm, w_in, b_in, sinks, conv_w, w_attn_branch, w_conv_branch, w_out, ffn_norm, w_up, ffn_conv_w, w_down, final_norm, loss_target, m_mix_norm, m_w_in, m_b_in, m_sinks, m_conv_w, m_w_attn_branch, m_w_conv_branch, m_w_out, m_ffn_norm, m_w_up, m_ffn_conv_w, m_w_down, m_final_norm, v_mix_norm, v_w_in, v_b_in, v_sinks, v_conv_w, v_w_attn_branch, v_w_conv_branch, v_w_out, v_ffn_norm, v_w_up, v_ffn_conv_w, v_w_down, v_final_norm):
    xs, tgt = x[0], loss_target[0]
    me = 4 * lax.axis_index("x") + 2 * lax.axis_index("y") + lax.axis_index("c")
    in_rows, up_rows = IN_W // N_DEV, 2 * D_FF // N_DEV

    conv_sh = jnp.concatenate([_pad_cols(ffn_conv_w[0], 768), _pad_cols(conv_w[0], 768),
                               jnp.zeros((2, 768), F32)], axis=0)
    win_sh, wup_sh = w_in[0].T.astype(BF16), w_up[0].T.astype(BF16)
    wout_sh, wdown_sh = w_out[0].astype(BF16), w_down[0].astype(BF16)
    wa_sh, wc_sh = w_attn_branch[0].astype(BF16), w_conv_branch[0].astype(BF16)

    half = D_MODEL // 2
    (win_t,) = _exchange_only(_AllGather([win_sh]), "gather_w_in")
    (xn, qkv, cbx, gates), (wa_s, wc_s, wout, conv_g) = _norm_inproj(
        xs, mix_norm, win_t, b_in, _AllGather([wa_sh, wc_sh, wout_sh, conv_sh]))
    (attn, lse), (wup_lo,) = _attn_fwd(qkv, sinks, _AllGather([wup_sh[:, :half]]))
    wa, wc = _from_col_slabs(wa_s), _from_col_slabs(wc_s)
    conv_g = conv_g.reshape(N_DEV, 8, 768)
    fcw = jnp.transpose(conv_g[:, 0:3, :up_rows], (1, 0, 2)).reshape(3, 2 * D_FF)
    cw = jnp.transpose(conv_g[:, 3:6, :CONV_W // N_DEV], (1, 0, 2)).reshape(3, CONV_W)
    (conv, ap, cp, merged, h1), (wup_hi,) = _mix_fwd(xs, cbx, gates, attn, cw, wa, wc, wout,
                                                    _AllGather([wup_sh[:, half:]]))
    (hn, up_pre), (wdown,) = _ffn_up(h1, ffn_norm, wup_lo, wup_hi, _AllGather([wdown_sh]))
    up, act, dh2, loss_p, dfn_p = _ffn_down_loss(up_pre, fcw, wdown, h1, final_norm.reshape(1, D_MODEL), tgt)

    dn_rows, q_up = D_FF // N_DEV, up_rows // 4
    g_wdown = _matmul_tn(act, dh2, FF_CHUNK, "grad_w_down")
    (dup,), (r_wdown_a,) = _ffn_act_bwd(dh2, wdown, up, _ReduceScatter([(g_wdown, 0, dn_rows // 2)]))
    (dup_pre, dfcw_p), (r_wdown_b,) = _conv_bwd(dup, up_pre, fcw, 2 * D_FF, FF_CHUNK, "ffn_conv_bwd",
                                                _ReduceScatter([(g_wdown, dn_rows // 2, dn_rows // 2)]))
    g_wup_t = _matmul_tn(dup_pre, hn, FF_CHUNK, "grad_w_up")
    (dh1, dffn_p), (r_wup_a,) = _ffn_up_bwd(dup_pre, wup_lo, wup_hi, h1, ffn_norm, dh2,
                                            _ReduceScatter([(g_wup_t, 0, q_up)]))
    g_wout = _matmul_tn(merged, dh1, D_MODEL, "grad_w_out")
    (dgates, da, dc, dattn, dcb, dcv), (r_wup_b,) = _mix_bwd(
        dh1, wout, gates, ap, cp, wa, wc, cbx, cw, _ReduceScatter([(g_wup_t, q_up, q_up)]))
    g_wa = _to_col_slabs(_matmul_tn(attn, da, ATTN_W, "grad_w_attn_branch"))
    g_wc = _to_col_slabs(_matmul_tn(conv, dc, CONV_W, "grad_w_conv_branch"))
    dcc, dcx, dcw_p = _conv_branch_bwd(dcv, cbx, cw)
    (dq, dk, dv, dsink_p), (r_wup_c, r_wout, r_wa, r_wc) = _attn_bwd(
        qkv, sinks, attn, lse, dattn,
        _ReduceScatter([(g_wup_t, 2 * q_up, q_up), (g_wout, 0, D_MODEL // N_DEV), (g_wa, 0, ATTN_W),
                        (g_wc, 0, CONV_W)]))
    dproj = (dq, dk, dv, dcb, dcc, dcx, dgates)
    g_win_t, (r_wup_d,) = _grad_w_in(dproj, xn, _ReduceScatter([(g_wup_t, 3 * q_up, q_up)]))
    (win_theirs,) = _exchange_only(_PairExchange([g_win_t]), "pair_exchange_w_in")
    win_mine = lax.dynamic_index_in_dim(g_win_t.reshape(4, 2, in_rows, D_MODEL), lax.axis_index("c"), axis=1,
                                        keepdims=False).reshape(4 * in_rows, D_MODEL)
    q_win = _pair_add(win_mine, win_theirs, in_rows // 2, "pair_add_w_in")
    (dx, dbin_p, dmix_p), (r_win,) = _inproj_bwd(dproj, win_t, xs, mix_norm, dh1, _ChipExchange([q_win]))

    small = _pack_small(dmix_p, dffn_p, dfn_p, dsink_p, loss_p, dbin_p, dcw_p, dfcw_p)
    (r_small,) = _exchange_only(_ReduceScatter([], [small]), "exchange_small")

    fn2, m_fn2, v_fn2 = (t.reshape(1, D_MODEL) for t in (final_norm, m_final_norm, v_final_norm))
    small_res, g_cw_full, g_fcw_full, loss_row = _small_sums_adamw(
        _slots(r_small), [(mix_norm, m_mix_norm, v_mix_norm), (b_in, m_b_in, v_b_in), (sinks, m_sinks, v_sinks),
                          (ffn_norm, m_ffn_norm, v_ffn_norm), (fn2, m_fn2, v_fn2)])
    loss = loss_row[0, 0]
    g_cw = lax.dynamic_slice_in_dim(g_cw_full, me * (CONV_W // N_DEV), CONV_W // N_DEV, axis=1)
    g_fcw = lax.dynamic_slice_in_dim(g_fcw_full, me * up_rows, up_rows, axis=1)
    cw_res, fcw_res = _adamw_pair((conv_w[0], g_cw, m_conv_w[0], v_conv_w[0]),
                                  (ffn_conv_w[0], g_fcw, m_ffn_conv_w[0], v_ffn_conv_w[0]))

    big = {}
    big["w_in"] = tuple(t.T for t in _sum_parts_adamw(
        [r_win.reshape(4, in_rows, D_MODEL)], w_in[0].T, m_w_in[0].T, v_w_in[0].T, in_rows // 2, "adamw_w_in"))
    big["w_up"] = tuple(t.T for t in _sum_parts_adamw(
        [_slots(r_wup_a), _slots(r_wup_b), _slots(r_wup_c), _slots(r_wup_d)], w_up[0].T, m_w_up[0].T, v_w_up[0].T, q_up,
        "adamw_w_up"))
    big["w_out"] = _sum_adamw(_slots(r_wout), w_out[0], m_w_out[0], v_w_out[0], 128, "adamw_w_out")
    big["w_down"] = _sum_parts_adamw([_slots(r_wdown_a), _slots(r_wdown_b)], w_down[0], m_w_down[0], v_w_down[0],
                                     dn_rows // 2, "adamw_w_down")
    big["w_attn_branch"] = _sum_adamw(_slots(r_wa), w_attn_branch[0], m_w_attn_branch[0], v_w_attn_branch[0], 256,
                                      "adamw_w_attn_branch")
    big["w_conv_branch"] = _sum_adamw(_slots(r_wc), w_conv_branch[0], m_w_conv_branch[0], v_w_conv_branch[0], 256,
                                      "adamw_w_conv_branch")

    res = dict(zip(("mix_norm", "b_in", "sinks", "ffn_norm"), small_res[:4]))
    res["final_norm"] = tuple(t.reshape(final_norm.shape) for t in small_res[4])
    res["conv_w"] = tuple(t.reshape(conv_w.shape) for t in (g_cw,) + cw_res)
    res["ffn_conv_w"] = tuple(t.reshape(ffn_conv_w.shape) for t in (g_fcw,) + fcw_res)
    for name, ref_w in (("w_in", w_in), ("w_up", w_up), ("w_out", w_out), ("w_down", w_down),
                        ("w_attn_branch", w_attn_branch), ("w_conv_branch", w_conv_branch)):
        res[name] = tuple(t.reshape(ref_w.shape) for t in big[name])

    order = ["mix_norm", "w_in", "b_in", "sinks", "conv_w", "w_attn_branch", "w_conv_branch", "w_out",
             "ffn_norm", "w_up", "ffn_conv_w", "w_down", "final_norm"]
    out = [loss, dx.reshape(x.shape)]
    for k in range(4):
        out += [res[name][k] for name in order]
    return tuple(out)
```

```python
import math

import jax
import jax.numpy as jnp
from jax import lax
from jax.experimental import pallas as pl
from jax.experimental.pallas import tpu as pltpu

F32 = jnp.float32
BF16 = jnp.bfloat16
MESH = pl.DeviceIdType.MESH
N_DEV = 8

D_MODEL = 1024
HEAD_DIM = 64
N_HEADS = 8
BLOCK = 128
ATTN_W = 512
KV_W = 128
CONV_W = 512
QKV_W = ATTN_W + 2 * KV_W
CBX_W = 3 * CONV_W
GATE_W = 2 * D_MODEL
IN_W = QKV_W + CBX_W + GATE_W
D_FF = 2816
FF_CHUNK = 1408
NORM_EPS = 1e-5
ATTN_SCALE = HEAD_DIM ** -0.5
NEG = -1e30
HALO = 16

ADAM_LR = 0.001
ADAM_B1 = 0.9
ADAM_B2 = 0.999
ADAM_EPS = 1e-08
ADAM_WD = 0.01
ADAM_STEP = 10

VMEM_LIMIT = 56 * 1024 * 1024
SMALL_ROWS = 32

NT = (((1,), (1,)), ((), ()))
TN = (((0,), (0,)), ((), ()))
ANY = pl.BlockSpec(memory_space=pl.ANY)


def _sig(v):
    return 1.0 / (1.0 + jnp.exp(-v))


def _row_tile(s, pref=256):
    return pref if s % pref == 0 else s


def _shifts_down(u, halo, ks):
    ext = jnp.concatenate([halo, u], axis=0)
    return [pltpu.roll(ext, k, axis=0)[HALO:, :] for k in ks]


def _shifts_up(u, halo, ks):
    n = u.shape[0]
    ext = jnp.concatenate([u, halo], axis=0)
    return [pltpu.roll(ext, n + HALO - k, axis=0)[:n, :] for k in ks]


def _shift_matrix(n, k):
    row = lax.broadcasted_iota(jnp.int32, (n, n), 0)
    col = lax.broadcasted_iota(jnp.int32, (n, n), 1)
    return jnp.where(col == row + k, 1.0, 0.0).astype(BF16)


def _mxu_shift_up(mat, ub, halo, k):
    n = ub.shape[0]
    v = jnp.dot(mat, ub, preferred_element_type=F32)
    row = lax.broadcasted_iota(jnp.int32, (8, ub.shape[1]), 0)
    tail = v[n - 8:, :]
    for t in range(k):
        tail = jnp.where(row == 8 - k + t, halo[t:t + 1, :], tail)
    return jnp.concatenate([v[:n - 8, :], tail], axis=0)


def _prev_halo_map(tm):
    return lambda i: (jnp.maximum(i * (tm // HALO) - 1, 0), 0)


def _next_halo_map(tm, s):
    return lambda i: (jnp.minimum((i + 1) * (tm // HALO), s // HALO - 1), 0)


def _full(shape):
    return pl.BlockSpec(shape, lambda *_: (0,) * len(shape))


def _resident(shape):
    return pl.BlockSpec(shape, lambda *_: (0,) * len(shape), pipeline_mode=pl.Buffered(1))


def _rows(tm, c):
    return pl.BlockSpec((tm, c), lambda i: (i, 0))


def _sds(shape, dtype):
    return jax.ShapeDtypeStruct(shape, dtype)


def _my_place():
    x, y, c = lax.axis_index("x"), lax.axis_index("y"), lax.axis_index("c")
    return x, y, c


class _AllGather:
    def __init__(self, shards):
        self.ins = list(shards)
        n = len(shards)
        self.out_shape = [_sds((N_DEV * s.shape[0], s.shape[1]), s.dtype) for s in shards]
        self.sems = [pltpu.SemaphoreType.DMA((7 * n,)), pltpu.SemaphoreType.DMA((7 * n,)),
                     pltpu.SemaphoreType.DMA((n,))]

    def _parts(self, ins, outs, sems):
        send_sems, recv_sems, local_sems = sems
        x, y, c = _my_place()
        me, sibling = (x, y, c), (x, y, 1 - c)
        chips = [(1 - x, y), (x, 1 - y), (1 - x, 1 - y)]

        def rows(k, dev):
            r = ins[k].shape[0]
            start = pl.multiple_of((4 * dev[0] + 2 * dev[1] + dev[2]) * r, 8)
            return outs[k].at[pl.ds(start, r), :]

        def copy(k, j, block, to, src=None):
            return pltpu.make_async_remote_copy(
                src_ref=rows(k, block) if src is None else src, dst_ref=rows(k, block),
                send_sem=send_sems.at[7 * k + j], recv_sem=recv_sems.at[7 * k + j],
                device_id=to, device_id_type=MESH)

        n = len(ins)
        mine = [pltpu.make_async_copy(ins[k], rows(k, me), local_sems.at[k]) for k in range(n)]
        first = []
        for k in range(n):
            first.append(copy(k, 0, me, sibling, src=ins[k]))
            first += [copy(k, 1 + j, me, (*chip, c), src=ins[k]) for j, chip in enumerate(chips)]
        return me, sibling, chips, copy, mine, first

    def start(self, ins, outs, sems):
        _, _, _, _, mine, first = self._parts(ins, outs, sems)
        for cp in mine + first:
            cp.start()

    def finish(self, ins, outs, sems):
        me, sibling, chips, copy, mine, first = self._parts(ins, outs, sems)
        c = me[2]
        n = len(ins)
        passed = []
        for j, chip in enumerate(chips):
            for k in range(n):
                copy(k, 1 + j, (*chip, c), me).wait_recv()
                fwd = copy(k, 4 + j, (*chip, c), sibling)
                fwd.start()
                passed.append(fwd)
        for k in range(n):
            copy(k, 0, sibling, me).wait_recv()
            for j, chip in enumerate(chips):
                copy(k, 4 + j, (*chip, 1 - c), me).wait_recv()
        for cp in first + passed:
            cp.wait_send()
        for cp in mine:
            cp.wait()


class _ReduceScatter:
    def __init__(self, parts, bcast=()):
        self.parts = [(lo, cnt) for _, lo, cnt in parts]
        self.n_parts = len(parts)
        self.ins = [a for a, _, _ in parts] + list(bcast)
        self.out_shape = [_sds((N_DEV * cnt, a.shape[1]), a.dtype) for a, _, cnt in parts]
        self.out_shape += [_sds((N_DEV * b.shape[0], b.shape[1]), b.dtype) for b in bcast]
        n = len(self.ins)
        self.sems = [pltpu.SemaphoreType.DMA((7 * n,)), pltpu.SemaphoreType.DMA((7 * n,)),
                     pltpu.SemaphoreType.DMA((n,))]

    def _copies(self, ins, outs, sems):
        send_sems, recv_sems, local_sems = sems
        x, y, c = _my_place()
        me_idx = 4 * x + 2 * y + c
        remote, local = [], []
        for k in range(len(ins)):
            cnt = outs[k].shape[0] // N_DEV
            dst = outs[k].at[pl.ds(pl.multiple_of(me_idx * cnt, 8), cnt), :]
            if k < self.n_parts:
                lo, _ = self.parts[k]
                r = ins[k].shape[0] // N_DEV
                src_of = lambda idx: ins[k].at[pl.ds(pl.multiple_of(idx * r + lo, 8), cnt), :]
            else:
                src_of = lambda idx: ins[k]
            local.append(pltpu.make_async_copy(src_of(me_idx), dst, local_sems.at[k]))
            for j in range(1, N_DEV):
                peer = (x ^ (j >> 2), y ^ ((j >> 1) & 1), c ^ (j & 1))
                peer_idx = 4 * peer[0] + 2 * peer[1] + peer[2]
                remote.append(pltpu.make_async_remote_copy(
                    src_ref=src_of(peer_idx), dst_ref=dst,
                    send_sem=send_sems.at[7 * k + j - 1], recv_sem=recv_sems.at[7 * k + j - 1],
                    device_id=peer, device_id_type=MESH))
        return remote, local

    def start(self, ins, outs, sems):
        remote, local = self._copies(ins, outs, sems)
        for cp in local + remote:
            cp.start()

    def finish(self, ins, outs, sems):
        remote, local = self._copies(ins, outs, sems)
        for cp in remote:
            cp.wait_recv()
        for cp in remote:
            cp.wait_send()
        for cp in local:
            cp.wait()


class _PairExchange:
    def __init__(self, arrays):
        self.ins = list(arrays)
        n = len(arrays)
        self.out_shape = [_sds((a.shape[0] // 2, a.shape[1]), a.dtype) for a in arrays]
        self.sems = [pltpu.SemaphoreType.DMA((4 * n,)), pltpu.SemaphoreType.DMA((4 * n,))]

    def _copies(self, ins, outs, sems):
        send_sems, recv_sems = sems
        x, y, c = _my_place()
        remote = []
        for k in range(len(ins)):
            r = ins[k].shape[0] // N_DEV
            for chip in range(4):
                sib = ins[k].at[pl.ds(pl.multiple_of((2 * chip + 1 - c) * r, 8), r), :]
                remote.append(pltpu.make_async_remote_copy(
                    src_ref=sib, dst_ref=outs[k].at[pl.ds(chip * r, r), :],
                    send_sem=send_sems.at[4 * k + chip], recv_sem=recv_sems.at[4 * k + chip],
                    device_id=(x, y, 1 - c), device_id_type=MESH))
        return remote

    def start(self, ins, outs, sems):
        for cp in self._copies(ins, outs, sems):
            cp.start()

    def finish(self, ins, outs, sems):
        remote = self._copies(ins, outs, sems)
        for cp in remote:
            cp.wait_recv()
        for cp in remote:
            cp.wait_send()


class _ChipExchange:
    def __init__(self, arrays):
        self.ins = list(arrays)
        self.out_shape = [_sds(a.shape, a.dtype) for a in arrays]
        n = len(self.ins)
        self.sems = [pltpu.SemaphoreType.DMA((3 * n,)), pltpu.SemaphoreType.DMA((3 * n,)),
                     pltpu.SemaphoreType.DMA((n,))]

    def _copies(self, ins, outs, sems):
        send_sems, recv_sems, local_sems = sems
        x, y, c = _my_place()
        my_chip = 2 * x + y
        remote, local = [], []
        for k in range(len(ins)):
            r = ins[k].shape[0] // 4
            dst = outs[k].at[pl.ds(pl.multiple_of(my_chip * r, 8), r), :]
            local.append(pltpu.make_async_copy(ins[k].at[pl.ds(pl.multiple_of(my_chip * r, 8), r), :], dst,
                                               local_sems.at[k]))
            for j in range(1, 4):
                px, py = x ^ (j >> 1), y ^ (j & 1)
                src = ins[k].at[pl.ds(pl.multiple_of((2 * px + py) * r, 8), r), :]
                remote.append(pltpu.make_async_remote_copy(
                    src_ref=src, dst_ref=dst, send_sem=send_sems.at[3 * k + j - 1],
                    recv_sem=recv_sems.at[3 * k + j - 1], device_id=(px, py, c), device_id_type=MESH))
        return remote, local

    def start(self, ins, outs, sems):
        remote, local = self._copies(ins, outs, sems)
        for cp in local + remote:
            cp.start()

    def finish(self, ins, outs, sems):
        remote, local = self._copies(ins, outs, sems)
        for cp in remote:
            cp.wait_recv()
        for cp in remote:
            cp.wait_send()
        for cp in local:
            cp.wait()


def _pcall(body, name, grid, in_specs, out_specs, out_shape, args, scratch=(), comm=None):
    params = pltpu.CompilerParams(dimension_semantics=("arbitrary",) * len(grid), vmem_limit_bytes=VMEM_LIMIT)
    in_specs, out_specs, out_shape, scratch = list(in_specs), list(out_specs), list(out_shape), list(scratch)
    if comm is None:
        res = pl.pallas_call(body, name=name, grid=grid, in_specs=in_specs, out_specs=out_specs, out_shape=out_shape,
                             scratch_shapes=scratch, compiler_params=params)(*args)
        return list(res), []
    n_in, n_out, n_scr = len(in_specs), len(out_specs), len(scratch)
    ci, co = len(comm.ins), len(comm.out_shape)
    total = math.prod(grid)

    def carried(*refs):
        bounds = [0, n_in, n_in + ci, n_in + ci + n_out, n_in + ci + n_out + co, n_in + ci + n_out + co + n_scr]
        ins, cins, outs, couts, scr = (refs[a:b] for a, b in zip(bounds[:-1], bounds[1:]))
        sems = refs[bounds[-1]:]
        step = pl.program_id(0)
        for d in range(1, len(grid)):
            step = step * grid[d] + pl.program_id(d)

        @pl.when(step == 0)
        def _():
            comm.start(cins, couts, sems)

        body(*ins, *outs, *scr)

        @pl.when(step == total - 1)
        def _():
            comm.finish(cins, couts, sems)

    res = pl.pallas_call(
        carried, name=name, grid=grid, in_specs=in_specs + [ANY] * ci, out_specs=out_specs + [ANY] * co,
        out_shape=out_shape + comm.out_shape, scratch_shapes=scratch + comm.sems, compiler_params=params,
    )(*args, *comm.ins)
    return list(res[:n_out]), list(res[n_out:])


def _exchange_only(comm, name):
    def body(*refs):
        ci, co = len(comm.ins), len(comm.out_shape)
        comm.start(refs[:ci], refs[ci:ci + co], refs[ci + co:])
        comm.finish(refs[:ci], refs[ci:ci + co], refs[ci + co:])

    return pl.pallas_call(body, name=name, out_shape=comm.out_shape, in_specs=[ANY] * len(comm.ins),
                          out_specs=[ANY] * len(comm.out_shape), scratch_shapes=comm.sems)(*comm.ins)


def _norm_inproj(x, g, win_t, b_in, comm):
    s = x.shape[0]
    tm = _row_tile(s, 512)
    widths = (QKV_W, CBX_W, GATE_W)

    def body(x_ref, g_ref, w_ref, b_ref, xn_ref, qkv_ref, cbx_ref, gate_ref):
        xv = x_ref[...]
        r = lax.rsqrt(jnp.mean(xv * xv, axis=-1, keepdims=True) + NORM_EPS)
        xn = (xv * r * g_ref[...]).astype(BF16)
        xn_ref[...] = xn
        off = 0
        for o_ref, w in zip((qkv_ref, cbx_ref, gate_ref), widths):
            acc = lax.dot_general(xn, w_ref[off:off + w, :], NT, preferred_element_type=F32)
            o_ref[...] = (acc + b_ref[:, off:off + w]).astype(BF16)
            off += w

    return _pcall(
        body, "norm_inproj", (s // tm,),
        [_rows(tm, D_MODEL), _full((1, D_MODEL)), _resident((IN_W, D_MODEL)), _full((1, IN_W))],
        [_rows(tm, D_MODEL)] + [_rows(tm, w) for w in widths],
        [_sds((s, D_MODEL), BF16)] + [_sds((s, w), BF16) for w in widths],
        (x, g, win_t, b_in), comm=comm)


def _attn_specs():
    prev = lambda n: jnp.maximum(n - 1, 0)
    return [pl.BlockSpec((BLOCK, ATTN_W), lambda n: (n, 0)),
            pl.BlockSpec((BLOCK, KV_W), lambda n: (prev(n), ATTN_W // KV_W)),
            pl.BlockSpec((BLOCK, KV_W), lambda n: (n, ATTN_W // KV_W)),
            pl.BlockSpec((BLOCK, KV_W), lambda n: (prev(n), ATTN_W // KV_W + 1)),
            pl.BlockSpec((BLOCK, KV_W), lambda n: (n, ATTN_W // KV_W + 1))]


def _lower_lanes():
    return lax.broadcasted_iota(jnp.int32, (BLOCK, 128), 1) < HEAD_DIM


def _stack_heads(val, kh):
    lower = _lower_lanes()
    parts = []
    for g in range(4):
        h = kh * 4 + g
        blk = val[:, (h // 2) * 128:(h // 2 + 1) * 128]
        keep = lower if h % 2 == 0 else jnp.logical_not(lower)
        parts.append(jnp.where(keep, blk, jnp.zeros_like(blk)))
    return jnp.concatenate(parts, axis=0)


def _dup_kv(prev_ref, cur_ref, kh):
    t = jnp.concatenate([prev_ref[...], cur_ref[...]], axis=0).astype(F32)
    rolled = pltpu.roll(t, HEAD_DIM, axis=1)
    lower = lax.broadcasted_iota(jnp.int32, t.shape, 1) < HEAD_DIM
    dup = jnp.where(lower, t, rolled) if kh == 0 else jnp.where(lower, rolled, t)
    return dup.astype(BF16)


def _attn_mask(n):
    row = lax.broadcasted_iota(jnp.int32, (4 * BLOCK, 2 * BLOCK), 0)
    kj = lax.broadcasted_iota(jnp.int32, (4 * BLOCK, 2 * BLOCK), 1)
    dist = (row & (BLOCK - 1)) + BLOCK - kj
    band = jnp.logical_and(dist >= 0, dist < BLOCK)
    return jnp.logical_and(band, jnp.logical_or(kj >= BLOCK, n > 0))


def _sink_col(sinks_ref, kh):
    gi = lax.broadcasted_iota(jnp.int32, (4 * BLOCK, 1), 0) // BLOCK
    col = jnp.zeros((4 * BLOCK, 1), F32)
    for g in range(4):
        col = jnp.where(gi == g, sinks_ref[0, kh * 4 + g], col)
    return col


def _attn_fwd(qkv, sinks, comm):
    s = qkv.shape[0]

    def body(sinks_ref, q_ref, kp_ref, kc_ref, vp_ref, vc_ref, o_ref, lse_ref):
        n = pl.program_id(0)
        mask = _attn_mask(n)
        lower = _lower_lanes()
        lane = lax.broadcasted_iota(jnp.int32, (BLOCK, 128), 1)
        qv = q_ref[...]
        lse_out = jnp.zeros((BLOCK, 128), F32)
        for kh in range(2):
            qs = _stack_heads(qv, kh)
            kd, vd = _dup_kv(kp_ref, kc_ref, kh), _dup_kv(vp_ref, vc_ref, kh)
            sc = lax.dot_general(qs, kd, NT, preferred_element_type=F32) * ATTN_SCALE
            sc = jnp.where(mask, sc, NEG)
            sink = _sink_col(sinks_ref, kh)
            m = jnp.maximum(jnp.max(sc, axis=1, keepdims=True), sink)
            p = jnp.exp(sc - m)
            l = jnp.sum(p, axis=1, keepdims=True) + jnp.exp(sink - m)
            o = jnp.dot(p.astype(BF16), vd, preferred_element_type=F32) / l
            lse = m + jnp.log(l)
            for pair in range(2):
                lo = o[(2 * pair) * BLOCK:(2 * pair + 1) * BLOCK]
                hi = o[(2 * pair + 1) * BLOCK:(2 * pair + 2) * BLOCK]
                col = (kh * 2 + pair) * 128
                o_ref[:, col:col + 128] = jnp.where(lower, lo, hi).astype(BF16)
            for g in range(4):
                lse_out = jnp.where(lane == kh * 4 + g, lse[g * BLOCK:(g + 1) * BLOCK], lse_out)
        lse_ref[...] = lse_out

    return _pcall(
        body, "attn_fwd", (s // BLOCK,),
        [pl.BlockSpec(memory_space=pltpu.SMEM)] + _attn_specs(),
        [pl.BlockSpec((BLOCK, ATTN_W), lambda n: (n, 0)), pl.BlockSpec((BLOCK, 128), lambda n: (n, 0))],
        [_sds((s, ATTN_W), BF16), _sds((s, 128), F32)],
        (sinks, qkv, qkv, qkv, qkv, qkv), comm=comm)


def _conv_u(cbx_ref, halo_ref, w_ref, first):
    cb = cbx_ref[:, 0:CONV_W].astype(F32)
    cc = cbx_ref[:, CONV_W:2 * CONV_W].astype(F32)
    cx = cbx_ref[:, 2 * CONV_W:3 * CONV_W].astype(F32)
    u = cc * cx
    uh = halo_ref[:, CONV_W:2 * CONV_W].astype(F32) * halo_ref[:, 2 * CONV_W:3 * CONV_W].astype(F32)
    uh = jnp.where(first, 0.0, uh)
    u1, u2 = _shifts_down(u, uh, (1, 2))
    cv = w_ref[0:1, :] * u2 + w_ref[1:2, :] * u1 + w_ref[2:3, :] * u
    return cb, cc, cx, u, cv


def _mix_fwd(x, cbx, gates, attn, conv_w, wa, wc, wout, comm):
    s = x.shape[0]
    tm = _row_tile(s)

    def body(x_ref, cbx_ref, halo_ref, gate_ref, attn_ref, cw_ref, wa_ref, wc_ref, wo_ref,
             conv_ref, mg_ref, h1_ref):
        first = pl.program_id(0) == 0
        cb, _, _, _, cv = _conv_u(cbx_ref, halo_ref, cw_ref, first)
        conv = (cb * cv).astype(BF16)
        conv_ref[...] = conv
        ap = jnp.dot(attn_ref[...], wa_ref[...], preferred_element_type=F32)
        cp = jnp.dot(conv, wc_ref[...], preferred_element_type=F32)
        ga = gate_ref[:, 0:D_MODEL].astype(F32)
        gc = gate_ref[:, D_MODEL:2 * D_MODEL].astype(F32)
        merged = (_sig(ga) * ap + _sig(gc) * cp).astype(BF16)
        mg_ref[...] = merged
        h1_ref[...] = x_ref[...] + jnp.dot(merged, wo_ref[...], preferred_element_type=F32)

    return _pcall(
        body, "mix_fwd", (s // tm,),
        [_rows(tm, D_MODEL), _rows(tm, CBX_W), pl.BlockSpec((HALO, CBX_W), _prev_halo_map(tm)),
         _rows(tm, GATE_W), _rows(tm, ATTN_W), _full((3, CONV_W)), _full((ATTN_W, D_MODEL)),
         _full((CONV_W, D_MODEL)), _full((D_MODEL, D_MODEL))],
        [_rows(tm, CONV_W), _rows(tm, D_MODEL), _rows(tm, D_MODEL)],
        [_sds((s, CONV_W), BF16), _sds((s, D_MODEL), BF16), _sds((s, D_MODEL), F32)],
        (x, cbx, cbx, gates, attn, conv_w, wa, wc, wout), comm=comm)


def _ffn_up(h1, g, wup_lo, wup_hi, comm):
    s = h1.shape[0]
    tm = _row_tile(s, 512)
    half = D_MODEL // 2

    def body(h_ref, g_ref, wl_ref, wh_ref, hn_ref, up_ref):
        hv = h_ref[...]
        r = lax.rsqrt(jnp.mean(hv * hv, axis=-1, keepdims=True) + NORM_EPS)
        hn = (hv * r * g_ref[...]).astype(BF16)
        hn_ref[...] = hn
        for c in range(2 * D_FF // FF_CHUNK):
            sl = slice(c * FF_CHUNK, (c + 1) * FF_CHUNK)
            acc = lax.dot_general(hn[:, :half], wl_ref[sl, :], NT, preferred_element_type=F32)
            acc = acc + lax.dot_general(hn[:, half:], wh_ref[sl, :], NT, preferred_element_type=F32)
            up_ref[:, sl] = acc.astype(BF16)

    return _pcall(
        body, "ffn_up", (s // tm,),
        [_rows(tm, D_MODEL), _full((1, D_MODEL)), _resident((2 * D_FF, half)), _resident((2 * D_FF, half))],
        [_rows(tm, D_MODEL), _rows(tm, 2 * D_FF)],
        [_sds((s, D_MODEL), BF16), _sds((s, 2 * D_FF), BF16)],
        (h1, g, wup_lo, wup_hi), comm=comm)


def _ffn_conv_cols(up_ref, halo_ref, fcw_ref, first, off):
    u = up_ref[:, off:off + FF_CHUNK].astype(F32)
    uh = jnp.where(first, 0.0, halo_ref[:, off:off + FF_CHUNK].astype(F32))
    w = fcw_ref[:, off:off + FF_CHUNK]
    u1, u2 = _shifts_down(u, uh, (1, 2))
    return w[0:1] * u2 + w[1:2] * u1 + w[2:3] * u


def _ffn_down_loss(up_pre, fcw, wdown, h1, fnorm, target):
    s = h1.shape[0]
    tm = _row_tile(s)

    def body(up_ref, halo_ref, fcw_ref, wd_ref, h1_ref, fn_ref, t_ref, cu_ref, act_ref, dh2_ref, loss_ref, dfn_ref):
        i = pl.program_id(0)

        @pl.when(i == 0)
        def _():
            loss_ref[...] = jnp.zeros_like(loss_ref)
            dfn_ref[...] = jnp.zeros_like(dfn_ref)

        h2 = h1_ref[...]
        for c in range(D_FF // FF_CHUNK):
            gsl = slice(c * FF_CHUNK, (c + 1) * FF_CHUNK)
            vsl = slice(D_FF + c * FF_CHUNK, D_FF + (c + 1) * FF_CHUNK)
            gate = _ffn_conv_cols(up_ref, halo_ref, fcw_ref, i == 0, c * FF_CHUNK)
            cu_ref[:, gsl] = gate.astype(BF16)
            val = _ffn_conv_cols(up_ref, halo_ref, fcw_ref, i == 0, D_FF + c * FF_CHUNK)
            cu_ref[:, vsl] = val.astype(BF16)
            act = (gate * _sig(gate) * val).astype(BF16)
            act_ref[:, gsl] = act
            h2 = h2 + jnp.dot(act, wd_ref[gsl, :], preferred_element_type=F32)
        r = lax.rsqrt(jnp.mean(h2 * h2, axis=-1, keepdims=True) + NORM_EPS)
        yhat = h2 * r
        fn = fn_ref[...]
        diff = yhat * fn - t_ref[...]
        loss_ref[...] += 0.5 * jnp.sum(jnp.sum(diff * diff, axis=1, keepdims=True), axis=0, keepdims=True) / D_MODEL
        dy = diff * (1.0 / D_MODEL)
        dfn_ref[...] += jnp.sum(dy * yhat, axis=0, keepdims=True)
        dyh = dy * fn
        dh2_ref[...] = r * (dyh - yhat * jnp.mean(dyh * yhat, axis=-1, keepdims=True))

    return _pcall(
        body, "ffn_down_loss", (s // tm,),
        [_rows(tm, 2 * D_FF), pl.BlockSpec((HALO, 2 * D_FF), _prev_halo_map(tm)), _full((3, 2 * D_FF)),
         _resident((D_FF, D_MODEL)), _rows(tm, D_MODEL), _full((1, D_MODEL)), _rows(tm, D_MODEL)],
        [_rows(tm, 2 * D_FF), _rows(tm, D_FF), _rows(tm, D_MODEL), _full((1, 128)), _full((1, D_MODEL))],
        [_sds((s, 2 * D_FF), BF16), _sds((s, D_FF), BF16), _sds((s, D_MODEL), F32), _sds((1, 128), F32),
         _sds((1, D_MODEL), F32)],
        (up_pre, up_pre, fcw, wdown, h1, fnorm, target))[0]


def _ffn_bwd(dh2, wdown, up, up_pre, fcw, comm):
    s = dh2.shape[0]
    tm = _row_tile(s)

    def dup_cols(dh, up_ref, wd_ref, c):
        gsl = slice(c * FF_CHUNK, (c + 1) * FF_CHUNK)
        vsl = slice(D_FF + c * FF_CHUNK, D_FF + (c + 1) * FF_CHUNK)
        dact = lax.dot_general(dh, wd_ref[gsl, :], NT, preferred_element_type=F32)
        gate = up_ref[:, gsl].astype(F32)
        val = up_ref[:, vsl].astype(F32)
        sg = _sig(gate)
        return dact * val * (sg * (1.0 + gate * (1.0 - sg))), dact * gate * sg

    def body(dh_ref, dhn_ref, wd_ref, up_ref, upn_ref, x_ref, w_ref, dx_ref, dw_ref):
        i = pl.program_id(0)

        @pl.when(i == 0)
        def _():
            dw_ref[...] = jnp.zeros_like(dw_ref)

        last = i == s // tm - 1
        up1, up2 = _shift_matrix(tm, 1), _shift_matrix(tm, 2)
        dh = dh_ref[...].astype(BF16)
        dhn = dhn_ref[...].astype(BF16)
        for c in range(D_FF // FF_CHUNK):
            halves = zip(dup_cols(dh, up_ref, wd_ref, c), dup_cols(dhn, upn_ref, wd_ref, c),
                         (c * FF_CHUNK, D_FF + c * FF_CHUNK))
            for d, dn, off in halves:
                sl = slice(off, off + FF_CHUNK)
                dn = jnp.where(last, 0.0, dn)
                xv = x_ref[:, sl].astype(F32)
                wv = w_ref[:, sl]
                db = d.astype(BF16)
                d1, d2 = _mxu_shift_up(up1, db, dn, 1), _mxu_shift_up(up2, db, dn, 2)
                dx_ref[:, sl] = (wv[2:3] * d + wv[1:2] * d1 + wv[0:1] * d2).astype(BF16)
                dw_ref[0:1, sl] += jnp.sum(d2 * xv, axis=0, keepdims=True)
                dw_ref[1:2, sl] += jnp.sum(d1 * xv, axis=0, keepdims=True)
                dw_ref[2:3, sl] += jnp.sum(d * xv, axis=0, keepdims=True)

    return _pcall(
        body, "ffn_bwd", (s // tm,),
        [_rows(tm, D_MODEL), pl.BlockSpec((HALO, D_MODEL), _next_halo_map(tm, s)), _resident((D_FF, D_MODEL)),
         _rows(tm, 2 * D_FF), pl.BlockSpec((HALO, 2 * D_FF), _next_halo_map(tm, s)), _rows(tm, 2 * D_FF),
         _full((3, 2 * D_FF))],
        [_rows(tm, 2 * D_FF), _full((3, 2 * D_FF))],
        [_sds((s, 2 * D_FF), BF16), _sds((3, 2 * D_FF), F32)],
        (dh2, dh2, wdown, up, up, up_pre, fcw), comm=comm)


def _matmul_tn(a, b, tk, name, ts=1024, comm=None):
    s, ka = a.shape
    n = b.shape[1]
    ts = min(ts, s)
    steps = s // ts

    def body(a_ref, b_ref, o_ref, acc_ref):
        j = pl.program_id(1)

        @pl.when(j == 0)
        def _():
            acc_ref[...] = jnp.zeros_like(acc_ref)

        acc_ref[...] += lax.dot_general(a_ref[...].astype(BF16), b_ref[...].astype(BF16), TN,
                                        preferred_element_type=F32)

        @pl.when(j == steps - 1)
        def _():
            o_ref[...] = acc_ref[...].astype(BF16)

    outs, couts = _pcall(
        body, name, (ka // tk, steps),
        [pl.BlockSpec((ts, tk), lambda i, j: (j, i)), pl.BlockSpec((ts, n), lambda i, j: (j, 0))],
        [pl.BlockSpec((tk, n), lambda i, j: (i, 0))], [_sds((ka, n), BF16)],
        (a, b), scratch=[pltpu.VMEM((tk, n), F32)], comm=comm)
    return outs[0] if comm is None else (outs[0], couts)


def _norm_bwd_tile(xv, g, dy):
    r = lax.rsqrt(jnp.mean(xv * xv, axis=-1, keepdims=True) + NORM_EPS)
    xhat = xv * r
    dg = jnp.sum(dy * xhat, axis=0, keepdims=True)
    dyh = dy * g
    return r * (dyh - xhat * jnp.mean(dyh * xhat, axis=-1, keepdims=True)), dg


def _ffn_up_bwd(dup_pre, wup_lo, wup_hi, h1, g, dh2, comm):
    s = h1.shape[0]
    tm = _row_tile(s, 512)
    half = D_MODEL // 2

    def body(du_ref, wl_ref, wh_ref, h_ref, g_ref, dh2_ref, dh1_ref, dg_ref):
        @pl.when(pl.program_id(0) == 0)
        def _():
            dg_ref[...] = jnp.zeros_like(dg_ref)

        du = du_ref[...]
        dhn = jnp.concatenate([jnp.dot(du, wl_ref[...], preferred_element_type=F32),
                               jnp.dot(du, wh_ref[...], preferred_element_type=F32)], axis=1)
        dx, dg = _norm_bwd_tile(h_ref[...], g_ref[...], dhn)
        dg_ref[...] += dg
        dh1_ref[...] = dh2_ref[...] + dx

    return _pcall(
        body, "ffn_up_bwd", (s // tm,),
        [_rows(tm, 2 * D_FF), _resident((2 * D_FF, half)), _resident((2 * D_FF, half)), _rows(tm, D_MODEL),
         _full((1, D_MODEL)), _rows(tm, D_MODEL)],
        [_rows(tm, D_MODEL), _full((1, D_MODEL))],
        [_sds((s, D_MODEL), F32), _sds((1, D_MODEL), F32)],
        (dup_pre, wup_lo, wup_hi, h1, g, dh2), comm=comm)


def _mix_bwd(dh1, wout, gates, attn, wa, wc, cbx, conv_w, comm):
    s = dh1.shape[0]
    tm = _row_tile(s)

    def body(dh_ref, wo_ref, gate_ref, attn_ref, wa_ref, wc_ref, cbx_ref, halo_ref, cw_ref,
             dg_ref, da_ref, dc_ref, dattn_ref, dcb_ref, dcv_ref):
        first = pl.program_id(0) == 0
        cb, _, _, _, cv = _conv_u(cbx_ref, halo_ref, cw_ref, first)
        ap = jnp.dot(attn_ref[...], wa_ref[...], preferred_element_type=F32)
        cp = jnp.dot((cb * cv).astype(BF16), wc_ref[...], preferred_element_type=F32)
        dm = lax.dot_general(dh_ref[...].astype(BF16), wo_ref[...], NT, preferred_element_type=F32)
        sa = _sig(gate_ref[:, 0:D_MODEL].astype(F32))
        sc = _sig(gate_ref[:, D_MODEL:2 * D_MODEL].astype(F32))
        da = (dm * sa).astype(BF16)
        dc = (dm * sc).astype(BF16)
        da_ref[...] = da
        dc_ref[...] = dc
        dg_ref[:, 0:D_MODEL] = (dm * ap * sa * (1.0 - sa)).astype(BF16)
        dg_ref[:, D_MODEL:2 * D_MODEL] = (dm * cp * sc * (1.0 - sc)).astype(BF16)
        dattn_ref[...] = lax.dot_general(da, wa_ref[...], NT, preferred_element_type=F32).astype(BF16)
        dconv = lax.dot_general(dc, wc_ref[...], NT, preferred_element_type=F32)
        dcb_ref[...] = (dconv * cv).astype(BF16)
        dcv_ref[...] = (dconv * cb).astype(BF16)

    return _pcall(
        body, "mix_bwd", (s // tm,),
        [_rows(tm, D_MODEL), _full((D_MODEL, D_MODEL)), _rows(tm, GATE_W), _rows(tm, ATTN_W),
         _full((ATTN_W, D_MODEL)), _full((CONV_W, D_MODEL)), _rows(tm, CBX_W),
         pl.BlockSpec((HALO, CBX_W), _prev_halo_map(tm)), _full((3, CONV_W))],
        [_rows(tm, GATE_W), _rows(tm, D_MODEL), _rows(tm, D_MODEL), _rows(tm, ATTN_W),
         _rows(tm, CONV_W), _rows(tm, CONV_W)],
        [_sds((s, GATE_W), BF16), _sds((s, D_MODEL), BF16), _sds((s, D_MODEL), BF16), _sds((s, ATTN_W), BF16),
         _sds((s, CONV_W), BF16), _sds((s, CONV_W), BF16)],
        (dh1, wout, gates, attn, wa, wc, cbx, cbx, conv_w), comm=comm)


def _conv_branch_bwd(dcv, cbx, conv_w):
    s = dcv.shape[0]
    tm = _row_tile(s)

    def body(d_ref, dn_ref, cbx_ref, w_ref, dcc_ref, dcx_ref, dw_ref):
        i = pl.program_id(0)

        @pl.when(i == 0)
        def _():
            dw_ref[...] = jnp.zeros_like(dw_ref)

        last = i == s // tm - 1
        cc = cbx_ref[:, CONV_W:2 * CONV_W].astype(F32)
        cx = cbx_ref[:, 2 * CONV_W:3 * CONV_W].astype(F32)
        u = cc * cx
        d = d_ref[...].astype(F32)
        dn = jnp.where(last, 0.0, dn_ref[...].astype(F32))
        d1, d2 = _shifts_up(d, dn, (1, 2))
        du = w_ref[2:3, :] * d + w_ref[1:2, :] * d1 + w_ref[0:1, :] * d2
        dcc_ref[...] = (du * cx).astype(BF16)
        dcx_ref[...] = (du * cc).astype(BF16)
        dw_ref[0:1, :] += jnp.sum(d2 * u, axis=0, keepdims=True)
        dw_ref[1:2, :] += jnp.sum(d1 * u, axis=0, keepdims=True)
        dw_ref[2:3, :] += jnp.sum(d * u, axis=0, keepdims=True)

    return _pcall(
        body, "conv_branch_bwd", (s // tm,),
        [_rows(tm, CONV_W), pl.BlockSpec((HALO, CONV_W), _next_halo_map(tm, s)), _rows(tm, CBX_W),
         _full((3, CONV_W))],
        [_rows(tm, CONV_W), _rows(tm, CONV_W), _full((3, CONV_W))],
        [_sds((s, CONV_W), BF16), _sds((s, CONV_W), BF16), _sds((3, CONV_W), F32)],
        (dcv, dcv, cbx, conv_w))[0]


def _attn_bwd(qkv, sinks, attn, lse, dattn, comm):
    s = qkv.shape[0]

    def body(sinks_ref, q_ref, kp_ref, kc_ref, vp_ref, vc_ref, o_ref, lse_ref, do_ref,
             dq_ref, dk_ref, dv_ref, ds_ref):
        n = pl.program_id(0)

        @pl.when(n == 0)
        def _():
            dk_ref[...] = jnp.zeros_like(dk_ref)
            dv_ref[...] = jnp.zeros_like(dv_ref)
            ds_ref[...] = jnp.zeros_like(ds_ref)

        mask = _attn_mask(n)
        lower = _lower_lanes()
        lane = lax.broadcasted_iota(jnp.int32, (BLOCK, 128), 1)
        lower2 = lax.broadcasted_iota(jnp.int32, (2 * BLOCK, 128), 1) < HEAD_DIM
        lane1 = lax.broadcasted_iota(jnp.int32, (1, 128), 1)
        qv, ov, dov, lsev = q_ref[...], o_ref[...], do_ref[...], lse_ref[...]
        dk_fold, dv_fold = [], []
        dsink = jnp.zeros((1, 128), F32)
        for kh in range(2):
            qs = _stack_heads(qv, kh)
            dos = _stack_heads(dov, kh)
            os_ = _stack_heads(ov, kh)
            kd, vd = _dup_kv(kp_ref, kc_ref, kh), _dup_kv(vp_ref, vc_ref, kh)
            lse = jnp.concatenate(
                [jnp.sum(jnp.where(lane == kh * 4 + g, lsev, 0.0), axis=1, keepdims=True) for g in range(4)], axis=0)
            sc = lax.dot_general(qs, kd, NT, preferred_element_type=F32) * ATTN_SCALE
            p = jnp.exp(jnp.where(mask, sc, NEG) - lse)
            dp = lax.dot_general(dos, vd, NT, preferred_element_type=F32)
            delta = jnp.sum(dos.astype(F32) * os_.astype(F32), axis=1, keepdims=True)
            dsc = (p * (dp - delta) * ATTN_SCALE).astype(BF16)
            dqs = jnp.dot(dsc, kd, preferred_element_type=F32)
            for pair in range(2):
                lo = dqs[(2 * pair) * BLOCK:(2 * pair + 1) * BLOCK]
                hi = dqs[(2 * pair + 1) * BLOCK:(2 * pair + 2) * BLOCK]
                col = (kh * 2 + pair) * 128
                dq_ref[:, col:col + 128] = jnp.where(lower, lo, hi).astype(BF16)
            dkd = lax.dot_general(dsc, qs, TN, preferred_element_type=F32)
            dvd = lax.dot_general(p.astype(BF16), dos, TN, preferred_element_type=F32)
            dk_fold.append(dkd + pltpu.roll(dkd, HEAD_DIM, axis=1))
            dv_fold.append(dvd + pltpu.roll(dvd, HEAD_DIM, axis=1))
            psink = jnp.exp(_sink_col(sinks_ref, kh) - lse) * delta
            for g in range(4):
                tot = jnp.sum(psink[g * BLOCK:(g + 1) * BLOCK], axis=0, keepdims=True)
                dsink = dsink - jnp.where(lane1 == kh * 4 + g, tot, 0.0)
        dk2 = jnp.where(lower2, dk_fold[0], dk_fold[1])
        dv2 = jnp.where(lower2, dv_fold[0], dv_fold[1])
        ds_ref[...] += dsink
        cur = pl.ds(pl.multiple_of(n * BLOCK, BLOCK), BLOCK)
        dk_ref[cur, :] += dk2[BLOCK:]
        dv_ref[cur, :] += dv2[BLOCK:]

        @pl.when(n > 0)
        def _():
            prev = pl.ds(pl.multiple_of((n - 1) * BLOCK, BLOCK), BLOCK)
            dk_ref[prev, :] += dk2[:BLOCK]
            dv_ref[prev, :] += dv2[:BLOCK]

    blk = lambda w: pl.BlockSpec((BLOCK, w), lambda n: (n, 0))
    return _pcall(
        body, "attn_bwd", (s // BLOCK,),
        [pl.BlockSpec(memory_space=pltpu.SMEM)] + _attn_specs() + [blk(ATTN_W), blk(128), blk(ATTN_W)],
        [blk(ATTN_W), _full((s, KV_W)), _full((s, KV_W)), _full((1, 128))],
        [_sds((s, ATTN_W), BF16), _sds((s, KV_W), F32), _sds((s, KV_W), F32), _sds((1, 128), F32)],
        (sinks, qkv, qkv, qkv, qkv, qkv, attn, lse, dattn), comm=comm)


DPROJ_PIECES = (ATTN_W, KV_W, KV_W, CONV_W, CONV_W, CONV_W, GATE_W)
DPROJ_OFFSETS = tuple(sum(DPROJ_PIECES[:k]) for k in range(len(DPROJ_PIECES)))


def _grad_w_in(pieces, xn, comm):
    s = xn.shape[0]
    ts = min(1024, s)
    steps = s // ts
    rows0 = DPROJ_OFFSETS[6]

    def body(*refs):
        p_refs, b_ref, o_ref, acc_ref, stage_ref, sem = refs[:7], refs[7], refs[8], refs[9], refs[10], refs[11]
        i, j = pl.program_id(0), pl.program_id(1)

        @pl.when(j == 0)
        def _():
            acc_ref[...] = jnp.zeros_like(acc_ref)

        bv = b_ref[...]

        def flush(lo, n):
            stage_ref[0:n, :] = acc_ref[0:n, :].astype(BF16)
            cp = pltpu.make_async_copy(stage_ref.at[0:n, :], o_ref.at[lo:lo + n, :], sem)
            cp.start()
            cp.wait()

        @pl.when(i == 0)
        def _():
            for p_ref, off, w in zip(p_refs[:6], DPROJ_OFFSETS[:6], DPROJ_PIECES[:6]):
                acc_ref[off:off + w, :] += lax.dot_general(p_ref[...].astype(BF16), bv, TN,
                                                           preferred_element_type=F32)

            @pl.when(j == steps - 1)
            def _():
                flush(0, rows0)

        @pl.when(i == 1)
        def _():
            acc_ref[0:GATE_W, :] += lax.dot_general(p_refs[6][...], bv, TN, preferred_element_type=F32)

            @pl.when(j == steps - 1)
            def _():
                flush(rows0, GATE_W)

    def piece_spec(w, group):
        return pl.BlockSpec((ts, w), lambda i, j: (jnp.where(i == group, j, 0), 0))

    outs, couts = _pcall(
        body, "grad_w_in", (2, steps),
        [piece_spec(w, 0) for w in DPROJ_PIECES[:6]] + [piece_spec(GATE_W, 1),
                                                         pl.BlockSpec((ts, D_MODEL), lambda i, j: (j, 0))],
        [ANY], [_sds((IN_W, D_MODEL), BF16)], (*pieces, xn),
        scratch=[pltpu.VMEM((rows0, D_MODEL), F32), pltpu.VMEM((rows0, D_MODEL), BF16), pltpu.SemaphoreType.DMA],
        comm=comm)
    return outs[0], couts


def _inproj_bwd(pieces, win_t, x, g, dh1, comm):
    s = x.shape[0]
    tm = _row_tile(s, 512)

    def body(*refs):
        p_refs = refs[:7]
        w_ref, x_ref, g_ref, dh_ref, dx_ref, db_ref, dg_ref = refs[7:]

        @pl.when(pl.program_id(0) == 0)
        def _():
            db_ref[...] = jnp.zeros_like(db_ref)
            dg_ref[...] = jnp.zeros_like(dg_ref)

        dxn = jnp.zeros((tm, D_MODEL), F32)
        for p_ref, off, w in zip(p_refs, DPROJ_OFFSETS, DPROJ_PIECES):
            v = p_ref[...].astype(BF16)
            db_ref[:, off:off + w] += jnp.sum(v.astype(F32), axis=0, keepdims=True)
            dxn = dxn + jnp.dot(v, w_ref[off:off + w, :], preferred_element_type=F32)
        dx, dg = _norm_bwd_tile(x_ref[...], g_ref[...], dxn)
        dg_ref[...] += dg
        dx_ref[...] = dh_ref[...] + dx

    return _pcall(
        body, "inproj_bwd", (s // tm,),
        [_rows(tm, w) for w in DPROJ_PIECES] + [_resident((IN_W, D_MODEL)), _rows(tm, D_MODEL), _full((1, D_MODEL)),
                                                _rows(tm, D_MODEL)],
        [_rows(tm, D_MODEL), _full((1, IN_W)), _full((1, D_MODEL))],
        [_sds((s, D_MODEL), F32), _sds((1, IN_W), F32), _sds((1, D_MODEL), F32)],
        (*pieces, win_t, x, g, dh1), comm=comm)


def _adam_math(w, g, m, v):
    m2 = ADAM_B1 * m + (1.0 - ADAM_B1) * g
    v2 = ADAM_B2 * v + (1.0 - ADAM_B2) * (g * g)
    m_hat = m2 / (1.0 - ADAM_B1 ** ADAM_STEP)
    v_hat = v2 / (1.0 - ADAM_B2 ** ADAM_STEP)
    delta = -ADAM_LR * (m_hat / (jnp.sqrt(v_hat) + ADAM_EPS) + ADAM_WD * w)
    return delta, m2, v2


def _sum_slots(ref):
    tot = ref[0].astype(F32)
    for i in range(1, ref.shape[0]):
        tot = tot + ref[i].astype(F32)
    return tot


def _pair_add(mine, theirs, tr, name):
    r, c = mine.shape

    def body(a_ref, b_ref, o_ref):
        o_ref[...] = (a_ref[...].astype(F32) + b_ref[...].astype(F32)).astype(BF16)

    spec = pl.BlockSpec((tr, c), lambda i: (i, 0))
    return _pcall(body, name, (r // tr,), [spec, spec], [spec], [_sds((r, c), BF16)], (mine, theirs))[0][0]


def _sum_adamw(parts, w, m, v, tr, name):
    r, c = w.shape

    def body(p_ref, w_ref, m_ref, v_ref, g_ref, d_ref, m2_ref, v2_ref):
        g = _sum_slots(p_ref)
        g_ref[...] = g
        d_ref[...], m2_ref[...], v2_ref[...] = _adam_math(w_ref[...], g, m_ref[...], v_ref[...])

    spec = pl.BlockSpec((tr, c), lambda i: (i, 0))
    return _pcall(body, name, (r // tr,), [pl.BlockSpec((N_DEV, tr, c), lambda i: (0, i, 0)), spec, spec, spec],
                  [spec] * 4, [_sds((r, c), F32)] * 4, (parts, w, m, v))[0]


def _sum_parts_adamw(parts, w, m, v, tr, name):
    c = w.shape[1]
    tiles = [p.shape[1] // tr for p in parts]
    starts = [sum(tiles[:k]) for k in range(len(parts))]
    n_parts = len(parts)

    def body(*refs):
        p_refs = refs[:n_parts]
        w_ref, m_ref, v_ref, g_ref, d_ref, m2_ref, v2_ref = refs[n_parts:]
        i = pl.program_id(0)
        for p_ref, st, nt in zip(p_refs, starts, tiles):
            @pl.when(jnp.logical_and(i >= st, i < st + nt))
            def _(p_ref=p_ref):
                g_ref[...] = _sum_slots(p_ref)

        d_ref[...], m2_ref[...], v2_ref[...] = _adam_math(w_ref[...], g_ref[...], m_ref[...], v_ref[...])

    def part_spec(p, st, nt):
        return pl.BlockSpec((p.shape[0], tr, c), lambda i: (0, jnp.clip(i - st, 0, nt - 1), 0))

    spec = pl.BlockSpec((tr, c), lambda i: (i, 0))
    return _pcall(
        body, name, (sum(tiles),),
        [part_spec(p, st, nt) for p, st, nt in zip(parts, starts, tiles)] + [spec, spec, spec],
        [spec] * 4, [_sds(w.shape, F32)] * 4, (*parts, w, m, v))[0]


ROW_MIX, ROW_FFN, ROW_FINAL, ROW_SINKS, ROW_LOSS, ROW_BIN, ROW_CW, ROW_FCW = 0, 1, 2, 3, 4, 5, 10, 13
FCW_ROWS = 6


def _wide_pieces(width):
    return [(k * D_MODEL, min(D_MODEL, width - k * D_MODEL)) for k in range(-(-width // D_MODEL))]


def _pack_small(dmix, dffn, dfn, dsink, loss, dbin, dcw, dfcw):
    def body(mix_ref, ffn_ref, fn_ref, sink_ref, loss_ref, bin_ref, cw_ref, fcw_ref, o_ref):
        o_ref[...] = jnp.zeros_like(o_ref)
        o_ref[ROW_MIX:ROW_MIX + 1, :] = mix_ref[...]
        o_ref[ROW_FFN:ROW_FFN + 1, :] = ffn_ref[...]
        o_ref[ROW_FINAL:ROW_FINAL + 1, :] = fn_ref[...]
        o_ref[ROW_SINKS:ROW_SINKS + 1, 0:128] = sink_ref[...]
        o_ref[ROW_LOSS:ROW_LOSS + 1, 0:128] = loss_ref[...]
        for k, (off, w) in enumerate(_wide_pieces(IN_W)):
            o_ref[ROW_BIN + k:ROW_BIN + k + 1, 0:w] = bin_ref[:, off:off + w]
        o_ref[ROW_CW:ROW_CW + 3, 0:CONV_W] = cw_ref[...]
        for a in range(3):
            for k, (off, w) in enumerate(_wide_pieces(2 * D_FF)):
                row = ROW_FCW + FCW_ROWS * a + k
                o_ref[row:row + 1, 0:w] = fcw_ref[a:a + 1, off:off + w]

    return pl.pallas_call(body, name="pack_small", out_shape=_sds((SMALL_ROWS, D_MODEL), F32))(
        dmix, dffn, dfn, dsink, loss, dbin, dcw, dfcw)


def _small_sums_adamw(r_small, params):
    rows = (ROW_MIX, ROW_BIN, ROW_SINKS, ROW_FFN, ROW_FINAL)

    def body(*refs):
        r_ref, p_refs, o_refs = refs[0], refs[1:16], refs[16:]
        tot = _sum_slots(r_ref)
        for k, row in enumerate(rows):
            w_ref, m_ref, v_ref = p_refs[3 * k:3 * k + 3]
            g_ref, d_ref, m2_ref, v2_ref = o_refs[4 * k:4 * k + 4]
            for j, (off, w) in enumerate(_wide_pieces(w_ref.shape[1])):
                g_ref[:, off:off + w] = tot[row + j:row + j + 1, 0:w]
            d_ref[...], m2_ref[...], v2_ref[...] = _adam_math(w_ref[...], g_ref[...], m_ref[...], v_ref[...])
        cw_ref, fcw_ref, loss_ref = o_refs[20:]
        cw_ref[...] = tot[ROW_CW:ROW_CW + 3, 0:CONV_W]
        for a in range(3):
            for j, (off, w) in enumerate(_wide_pieces(2 * D_FF)):
                row = ROW_FCW + FCW_ROWS * a + j
                fcw_ref[a:a + 1, off:off + w] = tot[row:row + 1, 0:w]
        loss_ref[...] = tot[ROW_LOSS:ROW_LOSS + 1, 0:128]

    flat = [t for p in params for t in p]
    out_shape = [_sds(p[0].shape, F32) for p in params for _ in range(4)]
    out_shape += [_sds((3, CONV_W), F32), _sds((3, 2 * D_FF), F32), _sds((1, 128), F32)]
    res = pl.pallas_call(body, name="small_sums_adamw", out_shape=out_shape)(r_small, *flat)
    return [tuple(res[4 * k:4 * k + 4]) for k in range(5)], res[20], res[21], res[22]


def _adamw_pair(a, b):
    def body(*refs):
        for k in range(2):
            w_ref, g_ref, m_ref, v_ref = refs[4 * k:4 * k + 4]
            d_ref, m2_ref, v2_ref = refs[8 + 3 * k:8 + 3 * k + 3]
            d_ref[...], m2_ref[...], v2_ref[...] = _adam_math(w_ref[...], g_ref[...], m_ref[...], v_ref[...])

    out_shape = [_sds(a[0].shape, F32)] * 3 + [_sds(b[0].shape, F32)] * 3
    res = pl.pallas_call(body, name="adamw_conv_weights", out_shape=out_shape)(*a, *b)
    return tuple(res[:3]), tuple(res[3:])


def _pad_cols(a, c):
    return jnp.pad(a, ((0, 0), (0, c - a.shape[1])))


def _to_col_slabs(g):
    r = g.shape[0]
    return jnp.transpose(g.reshape(r, N_DEV, 128), (1, 0, 2)).reshape(N_DEV * r, 128)


def _from_col_slabs(t):
    r = t.shape[0] // N_DEV
    return jnp.transpose(t.reshape(N_DEV, r, 128), (1, 0, 2)).reshape(r, N_DEV * 128)


def _slots(t):
    return t.reshape(N_DEV, t.shape[0] // N_DEV, t.shape[1])


def kernel(x, mix_norm, w_in, b_in, sinks, conv_w, w_attn_branch, w_conv_branch, w_out, ffn_norm, w_up, ffn_conv_w, w_down, final_norm, loss_target, m_mix_norm, m_w_in, m_b_in, m_sinks, m_conv_w, m_w_attn_branch, m_w_conv_branch, m_w_out, m_ffn_norm, m_w_up, m_ffn_conv_w, m_w_down, m_final_norm, v_mix_norm, v_w_in, v_b_in, v_sinks, v_conv_w, v_w_attn_branch, v_w_conv_branch, v_w_out, v_ffn_norm, v_w_up, v_ffn_conv_w, v_w_down, v_final_norm):
    xs, tgt = x[0], loss_target[0]
    me = 4 * lax.axis_index("x") + 2 * lax.axis_index("y") + lax.axis_index("c")
    in_rows, up_rows = IN_W // N_DEV, 2 * D_FF // N_DEV

    conv_sh = jnp.concatenate([_pad_cols(ffn_conv_w[0], 768), _pad_cols(conv_w[0], 768),
                               jnp.zeros((2, 768), F32)], axis=0)
    win_sh, wup_sh = w_in[0].T.astype(BF16), w_up[0].T.astype(BF16)
    wout_sh, wdown_sh = w_out[0].astype(BF16), w_down[0].astype(BF16)
    wa_sh, wc_sh = w_attn_branch[0].astype(BF16), w_conv_branch[0].astype(BF16)

    half = D_MODEL // 2
    (win_t,) = _exchange_only(_AllGather([win_sh]), "gather_w_in")
    (xn, qkv, cbx, gates), (wa_s, wc_s, wout, conv_g) = _norm_inproj(
        xs, mix_norm, win_t, b_in, _AllGather([wa_sh, wc_sh, wout_sh, conv_sh]))
    (attn, lse), (wup_lo,) = _attn_fwd(qkv, sinks, _AllGather([wup_sh[:, :half]]))
    wa, wc = _from_col_slabs(wa_s), _from_col_slabs(wc_s)
    conv_g = conv_g.reshape(N_DEV, 8, 768)
    fcw = jnp.transpose(conv_g[:, 0:3, :up_rows], (1, 0, 2)).reshape(3, 2 * D_FF)
    cw = jnp.transpose(conv_g[:, 3:6, :CONV_W // N_DEV], (1, 0, 2)).reshape(3, CONV_W)
    (conv, merged, h1), (wup_hi,) = _mix_fwd(xs, cbx, gates, attn, cw, wa, wc, wout,
                                            _AllGather([wup_sh[:, half:]]))
    (hn, up_pre), (wdown,) = _ffn_up(h1, ffn_norm, wup_lo, wup_hi, _AllGather([wdown_sh]))
    up, act, dh2, loss_p, dfn_p = _ffn_down_loss(up_pre, fcw, wdown, h1, final_norm.reshape(1, D_MODEL), tgt)

    dn_rows, q_up = D_FF // N_DEV, up_rows // 4
    g_wdown = _matmul_tn(act, dh2, FF_CHUNK, "grad_w_down")
    (dup_pre, dfcw_p), (r_wdown,) = _ffn_bwd(dh2, wdown, up, up_pre, fcw, _ReduceScatter([(g_wdown, 0, dn_rows)]))
    g_wup_t = _matmul_tn(dup_pre, hn, FF_CHUNK, "grad_w_up")
    (dh1, dffn_p), (r_wup_a,) = _ffn_up_bwd(dup_pre, wup_lo, wup_hi, h1, ffn_norm, dh2,
                                            _ReduceScatter([(g_wup_t, 0, q_up)]))
    g_wout = _matmul_tn(merged, dh1, D_MODEL, "grad_w_out")
    (dgates, da, dc, dattn, dcb, dcv), (r_wup_b,) = _mix_bwd(
        dh1, wout, gates, attn, wa, wc, cbx, cw, _ReduceScatter([(g_wup_t, q_up, q_up)]))
    g_wa = _to_col_slabs(_matmul_tn(attn, da, ATTN_W, "grad_w_attn_branch"))
    g_wc = _to_col_slabs(_matmul_tn(conv, dc, CONV_W, "grad_w_conv_branch"))
    dcc, dcx, dcw_p = _conv_branch_bwd(dcv, cbx, cw)
    (dq, dk, dv, dsink_p), (r_wup_c, r_wout, r_wa, r_wc) = _attn_bwd(
        qkv, sinks, attn, lse, dattn,
        _ReduceScatter([(g_wup_t, 2 * q_up, q_up), (g_wout, 0, D_MODEL // N_DEV), (g_wa, 0, ATTN_W),
                        (g_wc, 0, CONV_W)]))
    dproj = (dq, dk, dv, dcb, dcc, dcx, dgates)
    g_win_t, (r_wup_d,) = _grad_w_in(dproj, xn, _ReduceScatter([(g_wup_t, 3 * q_up, q_up)]))
    (win_theirs,) = _exchange_only(_PairExchange([g_win_t]), "pair_exchange_w_in")
    win_mine = lax.dynamic_index_in_dim(g_win_t.reshape(4, 2, in_rows, D_MODEL), lax.axis_index("c"), axis=1,
                                        keepdims=False).reshape(4 * in_rows, D_MODEL)
    q_win = _pair_add(win_mine, win_theirs, in_rows // 2, "pair_add_w_in")
    (dx, dbin_p, dmix_p), (r_win,) = _inproj_bwd(dproj, win_t, xs, mix_norm, dh1, _ChipExchange([q_win]))

    small = _pack_small(dmix_p, dffn_p, dfn_p, dsink_p, loss_p, dbin_p, dcw_p, dfcw_p)
    (r_small,) = _exchange_only(_ReduceScatter([], [small]), "exchange_small")

    fn2, m_fn2, v_fn2 = (t.reshape(1, D_MODEL) for t in (final_norm, m_final_norm, v_final_norm))
    small_res, g_cw_full, g_fcw_full, loss_row = _small_sums_adamw(
        _slots(r_small), [(mix_norm, m_mix_norm, v_mix_norm), (b_in, m_b_in, v_b_in), (sinks, m_sinks, v_sinks),
                          (ffn_norm, m_ffn_norm, v_ffn_norm), (fn2, m_fn2, v_fn2)])
    loss = loss_row[0, 0]
    g_cw = lax.dynamic_slice_in_dim(g_cw_full, me * (CONV_W // N_DEV), CONV_W // N_DEV, axis=1)
    g_fcw = lax.dynamic_slice_in_dim(g_fcw_full, me * up_rows, up_rows, axis=1)
    cw_res, fcw_res = _adamw_pair((conv_w[0], g_cw, m_conv_w[0], v_conv_w[0]),
                                  (ffn_conv_w[0], g_fcw, m_ffn_conv_w[0], v_ffn_conv_w[0]))

    big = {}
    big["w_in"] = tuple(t.T for t in _sum_parts_adamw(
        [r_win.reshape(4, in_rows, D_MODEL)], w_in[0].T, m_w_in[0].T, v_w_in[0].T, in_rows // 2, "adamw_w_in"))
    big["w_up"] = tuple(t.T for t in _sum_parts_adamw(
        [_slots(r_wup_a), _slots(r_wup_b), _slots(r_wup_c), _slots(r_wup_d)], w_up[0].T, m_w_up[0].T, v_w_up[0].T, q_up,
        "adamw_w_up"))
    big["w_out"] = _sum_adamw(_slots(r_wout), w_out[0], m_w_out[0], v_w_out[0], 128, "adamw_w_out")
    big["w_down"] = _sum_adamw(_slots(r_wdown), w_down[0], m_w_down[0], v_w_down[0], dn_rows // 2, "adamw_w_down")
    big["w_attn_branch"] = _sum_adamw(_slots(r_wa), w_attn_branch[0], m_w_attn_branch[0], v_w_attn_branch[0], 256,
                                      "adamw_w_attn_branch")
    big["w_conv_branch"] = _sum_adamw(_slots(r_wc), w_conv_branch[0], m_w_conv_branch[0], v_w_conv_branch[0], 256,
                                      "adamw_w_conv_branch")

    res = dict(zip(("mix_norm", "b_in", "sinks", "ffn_norm"), small_res[:4]))
    res["final_norm"] = tuple(t.reshape(final_norm.shape) for t in small_res[4])
    res["conv_w"] = tuple(t.reshape(conv_w.shape) for t in (g_cw,) + cw_res)
    res["ffn_conv_w"] = tuple(t.reshape(ffn_conv_w.shape) for t in (g_fcw,) + fcw_res)
    for name, ref_w in (("w_in", w_in), ("w_up", w_up), ("w_out", w_out), ("w_down", w_down),
                        ("w_attn_branch", w_attn_branch), ("w_conv_branch", w_conv_branch)):
        res[name] = tuple(t.reshape(ref_w.shape) for t in big[name])

    order = ["mix_norm", "w_in", "b_in", "sinks", "conv_w", "w_attn_branch", "w_conv_branch", "w_out",
             "ffn_norm", "w_up", "ffn_conv_w", "w_down", "final_norm"]
    out = [loss, dx.reshape(x.shape)]
    for k in range(4):
        out += [res[name][k] for name in order]
    return tuple(out)
```

```python
import math

import jax
import jax.numpy as jnp
from jax import lax
from jax.experimental import pallas as pl
from jax.experimental.pallas import tpu as pltpu

F32 = jnp.float32
BF16 = jnp.bfloat16
MESH = pl.DeviceIdType.MESH
N_DEV = 8

D_MODEL = 1024
HEAD_DIM = 64
N_HEADS = 8
BLOCK = 128
ATTN_W = 512
KV_W = 128
CONV_W = 512
QKV_W = ATTN_W + 2 * KV_W
CBX_W = 3 * CONV_W
GATE_W = 2 * D_MODEL
IN_W = QKV_W + CBX_W + GATE_W
D_FF = 2816
FF_CHUNK = 1408
NORM_EPS = 1e-5
ATTN_SCALE = HEAD_DIM ** -0.5
NEG = -1e30
HALO = 16

ADAM_LR = 0.001
ADAM_B1 = 0.9
ADAM_B2 = 0.999
ADAM_EPS = 1e-08
ADAM_WD = 0.01
ADAM_STEP = 10

VMEM_LIMIT = 56 * 1024 * 1024
SMALL_ROWS = 32

NT = (((1,), (1,)), ((), ()))
TN = (((0,), (0,)), ((), ()))
ANY = pl.BlockSpec(memory_space=pl.ANY)


def _sig(v):
    return 1.0 / (1.0 + jnp.exp(-v))


def _row_tile(s, pref=256):
    return pref if s % pref == 0 else s


def _shifts_down(u, halo, ks):
    ext = jnp.concatenate([halo, u], axis=0)
    return [pltpu.roll(ext, k, axis=0)[HALO:, :] for k in ks]


def _shifts_up(u, halo, ks):
    n = u.shape[0]
    ext = jnp.concatenate([u, halo], axis=0)
    return [pltpu.roll(ext, n + HALO - k, axis=0)[:n, :] for k in ks]


def _shift_matrix(n, k):
    row = lax.broadcasted_iota(jnp.int32, (n, n), 0)
    col = lax.broadcasted_iota(jnp.int32, (n, n), 1)
    return jnp.where(col == row + k, 1.0, 0.0).astype(BF16)


def _mxu_shift_up(mat, ub, halo, k):
    n = ub.shape[0]
    v = jnp.dot(mat, ub, preferred_element_type=F32)
    row = lax.broadcasted_iota(jnp.int32, (8, ub.shape[1]), 0)
    tail = v[n - 8:, :]
    for t in range(k):
        tail = jnp.where(row == 8 - k + t, halo[t:t + 1, :], tail)
    return jnp.concatenate([v[:n - 8, :], tail], axis=0)


def _prev_halo_map(tm):
    return lambda i: (jnp.maximum(i * (tm // HALO) - 1, 0), 0)


def _next_halo_map(tm, s):
    return lambda i: (jnp.minimum((i + 1) * (tm // HALO), s // HALO - 1), 0)


def _full(shape):
    return pl.BlockSpec(shape, lambda *_: (0,) * len(shape))


def _resident(shape):
    return pl.BlockSpec(shape, lambda *_: (0,) * len(shape), pipeline_mode=pl.Buffered(1))


def _rows(tm, c):
    return pl.BlockSpec((tm, c), lambda i: (i, 0))


def _sds(shape, dtype):
    return jax.ShapeDtypeStruct(shape, dtype)


def _my_place():
    x, y, c = lax.axis_index("x"), lax.axis_index("y"), lax.axis_index("c")
    return x, y, c


def _start_exchange(remote, local):
    for cp in local + remote:
        cp.start()


def _finish_exchange(remote, local):
    for cp in remote:
        cp.wait_recv()
    for cp in remote:
        cp.wait_send()
    for cp in local:
        cp.wait()


class _AllGather:
    def __init__(self, shards):
        self.ins = list(shards)
        n = len(shards)
        self.out_shape = [_sds((N_DEV * s.shape[0], s.shape[1]), s.dtype) for s in shards]
        self.sems = [pltpu.SemaphoreType.DMA((7 * n,)), pltpu.SemaphoreType.DMA((7 * n,)),
                     pltpu.SemaphoreType.DMA((n,))]

    def _parts(self, ins, outs, sems):
        send_sems, recv_sems, local_sems = sems
        x, y, c = _my_place()
        me, sibling = (x, y, c), (x, y, 1 - c)
        chips = [(1 - x, y), (x, 1 - y), (1 - x, 1 - y)]

        def rows(k, dev):
            r = ins[k].shape[0]
            start = pl.multiple_of((4 * dev[0] + 2 * dev[1] + dev[2]) * r, 8)
            return outs[k].at[pl.ds(start, r), :]

        def copy(k, j, block, to, src=None):
            return pltpu.make_async_remote_copy(
                src_ref=rows(k, block) if src is None else src, dst_ref=rows(k, block),
                send_sem=send_sems.at[7 * k + j], recv_sem=recv_sems.at[7 * k + j],
                device_id=to, device_id_type=MESH)

        n = len(ins)
        mine = [pltpu.make_async_copy(ins[k], rows(k, me), local_sems.at[k]) for k in range(n)]
        first = []
        for k in range(n):
            first.append(copy(k, 0, me, sibling, src=ins[k]))
            first += [copy(k, 1 + j, me, (*chip, c), src=ins[k]) for j, chip in enumerate(chips)]
        return me, sibling, chips, copy, mine, first

    def start(self, ins, outs, sems):
        _, _, _, _, mine, first = self._parts(ins, outs, sems)
        _start_exchange(first, mine)

    def finish(self, ins, outs, sems):
        me, sibling, chips, copy, mine, first = self._parts(ins, outs, sems)
        c = me[2]
        n = len(ins)
        passed = []
        for j, chip in enumerate(chips):
            for k in range(n):
                copy(k, 1 + j, (*chip, c), me).wait_recv()
                fwd = copy(k, 4 + j, (*chip, c), sibling)
                fwd.start()
                passed.append(fwd)
        for k in range(n):
            copy(k, 0, sibling, me).wait_recv()
            for j, chip in enumerate(chips):
                copy(k, 4 + j, (*chip, 1 - c), me).wait_recv()
        for cp in first + passed:
            cp.wait_send()
        for cp in mine:
            cp.wait()


class _ReduceScatter:
    def __init__(self, parts, bcast=()):
        self.parts = [(lo, cnt) for _, lo, cnt in parts]
        self.n_parts = len(parts)
        self.ins = [a for a, _, _ in parts] + list(bcast)
        self.out_shape = [_sds((N_DEV * cnt, a.shape[1]), a.dtype) for a, _, cnt in parts]
        self.out_shape += [_sds((N_DEV * b.shape[0], b.shape[1]), b.dtype) for b in bcast]
        n = len(self.ins)
        self.sems = [pltpu.SemaphoreType.DMA((7 * n,)), pltpu.SemaphoreType.DMA((7 * n,)),
                     pltpu.SemaphoreType.DMA((n,))]

    def _copies(self, ins, outs, sems):
        send_sems, recv_sems, local_sems = sems
        x, y, c = _my_place()
        me_idx = 4 * x + 2 * y + c
        remote, local = [], []
        for k in range(len(ins)):
            cnt = outs[k].shape[0] // N_DEV
            dst = outs[k].at[pl.ds(pl.multiple_of(me_idx * cnt, 8), cnt), :]
            if k < self.n_parts:
                lo, _ = self.parts[k]
                r = ins[k].shape[0] // N_DEV
                src_of = lambda idx: ins[k].at[pl.ds(pl.multiple_of(idx * r + lo, 8), cnt), :]
            else:
                src_of = lambda idx: ins[k]
            local.append(pltpu.make_async_copy(src_of(me_idx), dst, local_sems.at[k]))
            for j in range(1, N_DEV):
                peer = (x ^ (j >> 2), y ^ ((j >> 1) & 1), c ^ (j & 1))
                peer_idx = 4 * peer[0] + 2 * peer[1] + peer[2]
                remote.append(pltpu.make_async_remote_copy(
                    src_ref=src_of(peer_idx), dst_ref=dst,
                    send_sem=send_sems.at[7 * k + j - 1], recv_sem=recv_sems.at[7 * k + j - 1],
                    device_id=peer, device_id_type=MESH))
        return remote, local

    def start(self, ins, outs, sems):
        _start_exchange(*self._copies(ins, outs, sems))

    def finish(self, ins, outs, sems):
        _finish_exchange(*self._copies(ins, outs, sems))


class _PairExchange:
    def __init__(self, arrays):
        self.ins = list(arrays)
        n = len(arrays)
        self.out_shape = [_sds((a.shape[0] // 2, a.shape[1]), a.dtype) for a in arrays]
        self.sems = [pltpu.SemaphoreType.DMA((4 * n,)), pltpu.SemaphoreType.DMA((4 * n,))]

    def _copies(self, ins, outs, sems):
        send_sems, recv_sems = sems
        x, y, c = _my_place()
        remote = []
        for k in range(len(ins)):
            r = ins[k].shape[0] // N_DEV
            for chip in range(4):
                sib = ins[k].at[pl.ds(pl.multiple_of((2 * chip + 1 - c) * r, 8), r), :]
                remote.append(pltpu.make_async_remote_copy(
                    src_ref=sib, dst_ref=outs[k].at[pl.ds(chip * r, r), :],
                    send_sem=send_sems.at[4 * k + chip], recv_sem=recv_sems.at[4 * k + chip],
                    device_id=(x, y, 1 - c), device_id_type=MESH))
        return remote

    def start(self, ins, outs, sems):
        for cp in self._copies(ins, outs, sems):
            cp.start()

    def finish(self, ins, outs, sems):
        remote = self._copies(ins, outs, sems)
        for cp in remote:
            cp.wait_recv()
        for cp in remote:
            cp.wait_send()


class _ChipExchange:
    def __init__(self, arrays):
        self.ins = list(arrays)
        self.out_shape = [_sds(a.shape, a.dtype) for a in arrays]
        n = len(self.ins)
        self.sems = [pltpu.SemaphoreType.DMA((3 * n,)), pltpu.SemaphoreType.DMA((3 * n,)),
                     pltpu.SemaphoreType.DMA((n,))]

    def _copies(self, ins, outs, sems):
        send_sems, recv_sems, local_sems = sems
        x, y, c = _my_place()
        my_chip = 2 * x + y
        remote, local = [], []
        for k in range(len(ins)):
            r = ins[k].shape[0] // 4
            dst = outs[k].at[pl.ds(pl.multiple_of(my_chip * r, 8), r), :]
            local.append(pltpu.make_async_copy(ins[k].at[pl.ds(pl.multiple_of(my_chip * r, 8), r), :], dst,
                                               local_sems.at[k]))
            for j in range(1, 4):
                px, py = x ^ (j >> 1), y ^ (j & 1)
                src = ins[k].at[pl.ds(pl.multiple_of((2 * px + py) * r, 8), r), :]
                remote.append(pltpu.make_async_remote_copy(
                    src_ref=src, dst_ref=dst, send_sem=send_sems.at[3 * k + j - 1],
                    recv_sem=recv_sems.at[3 * k + j - 1], device_id=(px, py, c), device_id_type=MESH))
        return remote, local

    def start(self, ins, outs, sems):
        _start_exchange(*self._copies(ins, outs, sems))

    def finish(self, ins, outs, sems):
        _finish_exchange(*self._copies(ins, outs, sems))


def _pcall(body, name, grid, in_specs, out_specs, out_shape, args, scratch=(), comm=None):
    params = pltpu.CompilerParams(dimension_semantics=("arbitrary",) * len(grid), vmem_limit_bytes=VMEM_LIMIT)
    in_specs, out_specs, out_shape, scratch = list(in_specs), list(out_specs), list(out_shape), list(scratch)
    if comm is None:
        res = pl.pallas_call(body, name=name, grid=grid, in_specs=in_specs, out_specs=out_specs, out_shape=out_shape,
                             scratch_shapes=scratch, compiler_params=params)(*args)
        return list(res), []
    n_in, n_out, n_scr = len(in_specs), len(out_specs), len(scratch)
    ci, co = len(comm.ins), len(comm.out_shape)
    total = math.prod(grid)

    def carried(*refs):
        bounds = [0, n_in, n_in + ci, n_in + ci + n_out, n_in + ci + n_out + co, n_in + ci + n_out + co + n_scr]
        ins, cins, outs, couts, scr = (refs[a:b] for a, b in zip(bounds[:-1], bounds[1:]))
        sems = refs[bounds[-1]:]
        step = pl.program_id(0)
        for d in range(1, len(grid)):
            step = step * grid[d] + pl.program_id(d)

        @pl.when(step == 0)
        def _():
            comm.start(cins, couts, sems)

        body(*ins, *outs, *scr)

        @pl.when(step == total - 1)
        def _():
            comm.finish(cins, couts, sems)

    res = pl.pallas_call(
        carried, name=name, grid=grid, in_specs=in_specs + [ANY] * ci, out_specs=out_specs + [ANY] * co,
        out_shape=out_shape + comm.out_shape, scratch_shapes=scratch + comm.sems, compiler_params=params,
    )(*args, *comm.ins)
    return list(res[:n_out]), list(res[n_out:])


def _exchange_only(comm, name):
    def body(*refs):
        ci, co = len(comm.ins), len(comm.out_shape)
        comm.start(refs[:ci], refs[ci:ci + co], refs[ci + co:])
        comm.finish(refs[:ci], refs[ci:ci + co], refs[ci + co:])

    return pl.pallas_call(body, name=name, out_shape=comm.out_shape, in_specs=[ANY] * len(comm.ins),
                          out_specs=[ANY] * len(comm.out_shape), scratch_shapes=comm.sems)(*comm.ins)


def _norm_inproj(x, g, win_t, b_in, comm):
    s = x.shape[0]
    tm = _row_tile(s, 512)
    widths = (QKV_W, CBX_W, GATE_W)

    def body(x_ref, g_ref, w_ref, b_ref, xn_ref, qkv_ref, cbx_ref, gate_ref):
        xv = x_ref[...]
        r = lax.rsqrt(jnp.mean(xv * xv, axis=-1, keepdims=True) + NORM_EPS)
        xn = (xv * r * g_ref[...]).astype(BF16)
        xn_ref[...] = xn
        off = 0
        for o_ref, w in zip((qkv_ref, cbx_ref, gate_ref), widths):
            acc = lax.dot_general(xn, w_ref[off:off + w, :], NT, preferred_element_type=F32)
            o_ref[...] = (acc + b_ref[:, off:off + w]).astype(BF16)
            off += w

    return _pcall(
        body, "norm_inproj", (s // tm,),
        [_rows(tm, D_MODEL), _full((1, D_MODEL)), _resident((IN_W, D_MODEL)), _full((1, IN_W))],
        [_rows(tm, D_MODEL)] + [_rows(tm, w) for w in widths],
        [_sds((s, D_MODEL), BF16)] + [_sds((s, w), BF16) for w in widths],
        (x, g, win_t, b_in), comm=comm)


def _attn_specs():
    prev = lambda n: jnp.maximum(n - 1, 0)
    return [pl.BlockSpec((BLOCK, ATTN_W), lambda n: (n, 0)),
            pl.BlockSpec((BLOCK, KV_W), lambda n: (prev(n), ATTN_W // KV_W)),
            pl.BlockSpec((BLOCK, KV_W), lambda n: (n, ATTN_W // KV_W)),
            pl.BlockSpec((BLOCK, KV_W), lambda n: (prev(n), ATTN_W // KV_W + 1)),
            pl.BlockSpec((BLOCK, KV_W), lambda n: (n, ATTN_W // KV_W + 1))]


def _lower_lanes():
    return lax.broadcasted_iota(jnp.int32, (BLOCK, 128), 1) < HEAD_DIM


def _stack_heads(val, kh):
    lower = _lower_lanes()
    parts = []
    for g in range(4):
        h = kh * 4 + g
        blk = val[:, (h // 2) * 128:(h // 2 + 1) * 128]
        keep = lower if h % 2 == 0 else jnp.logical_not(lower)
        parts.append(jnp.where(keep, blk, jnp.zeros_like(blk)))
    return jnp.concatenate(parts, axis=0)


def _dup_kv(prev_ref, cur_ref, kh):
    t = jnp.concatenate([prev_ref[...], cur_ref[...]], axis=0).astype(F32)
    rolled = pltpu.roll(t, HEAD_DIM, axis=1)
    lower = lax.broadcasted_iota(jnp.int32, t.shape, 1) < HEAD_DIM
    dup = jnp.where(lower, t, rolled) if kh == 0 else jnp.where(lower, rolled, t)
    return dup.astype(BF16)


def _attn_mask(n):
    row = lax.broadcasted_iota(jnp.int32, (4 * BLOCK, 2 * BLOCK), 0)
    kj = lax.broadcasted_iota(jnp.int32, (4 * BLOCK, 2 * BLOCK), 1)
    dist = (row & (BLOCK - 1)) + BLOCK - kj
    band = jnp.logical_and(dist >= 0, dist < BLOCK)
    return jnp.logical_and(band, jnp.logical_or(kj >= BLOCK, n > 0))


def _sink_col(sinks_ref, kh):
    gi = lax.broadcasted_iota(jnp.int32, (4 * BLOCK, 1), 0) // BLOCK
    col = jnp.zeros((4 * BLOCK, 1), F32)
    for g in range(4):
        col = jnp.where(gi == g, sinks_ref[0, kh * 4 + g], col)
    return col


def _attn_fwd(qkv, sinks, comm):
    s = qkv.shape[0]

    def body(sinks_ref, q_ref, kp_ref, kc_ref, vp_ref, vc_ref, o_ref, lse_ref):
        n = pl.program_id(0)
        mask = _attn_mask(n)
        lower = _lower_lanes()
        lane = lax.broadcasted_iota(jnp.int32, (BLOCK, 128), 1)
        qv = q_ref[...]
        lse_out = jnp.zeros((BLOCK, 128), F32)
        for kh in range(2):
            qs = _stack_heads(qv, kh)
            kd, vd = _dup_kv(kp_ref, kc_ref, kh), _dup_kv(vp_ref, vc_ref, kh)
            sc = lax.dot_general(qs, kd, NT, preferred_element_type=F32) * ATTN_SCALE
            sc = jnp.where(mask, sc, NEG)
            sink = _sink_col(sinks_ref, kh)
            m = jnp.maximum(jnp.max(sc, axis=1, keepdims=True), sink)
            p = jnp.exp(sc - m)
            l = jnp.sum(p, axis=1, keepdims=True) + jnp.exp(sink - m)
            o = jnp.dot(p.astype(BF16), vd, preferred_element_type=F32) / l
            lse = m + jnp.log(l)
            for pair in range(2):
                lo = o[(2 * pair) * BLOCK:(2 * pair + 1) * BLOCK]
                hi = o[(2 * pair + 1) * BLOCK:(2 * pair + 2) * BLOCK]
                col = (kh * 2 + pair) * 128
                o_ref[:, col:col + 128] = jnp.where(lower, lo, hi).astype(BF16)
            for g in range(4):
                lse_out = jnp.where(lane == kh * 4 + g, lse[g * BLOCK:(g + 1) * BLOCK], lse_out)
        lse_ref[...] = lse_out

    return _pcall(
        body, "attn_fwd", (s // BLOCK,),
        [pl.BlockSpec(memory_space=pltpu.SMEM)] + _attn_specs(),
        [pl.BlockSpec((BLOCK, ATTN_W), lambda n: (n, 0)), pl.BlockSpec((BLOCK, 128), lambda n: (n, 0))],
        [_sds((s, ATTN_W), BF16), _sds((s, 128), F32)],
        (sinks, qkv, qkv, qkv, qkv, qkv), comm=comm)


def _conv_u(cbx_ref, halo_ref, w_ref, first):
    cb = cbx_ref[:, 0:CONV_W].astype(F32)
    cc = cbx_ref[:, CONV_W:2 * CONV_W].astype(F32)
    cx = cbx_ref[:, 2 * CONV_W:3 * CONV_W].astype(F32)
    u = cc * cx
    uh = halo_ref[:, CONV_W:2 * CONV_W].astype(F32) * halo_ref[:, 2 * CONV_W:3 * CONV_W].astype(F32)
    uh = jnp.where(first, 0.0, uh)
    u1, u2 = _shifts_down(u, uh, (1, 2))
    cv = w_ref[0:1, :] * u2 + w_ref[1:2, :] * u1 + w_ref[2:3, :] * u
    return cb, cc, cx, u, cv


def _mix_fwd(x, cbx, gates, attn, conv_w, wa, wc, wout, comm):
    s = x.shape[0]
    tm = _row_tile(s)

    def body(x_ref, cbx_ref, halo_ref, gate_ref, attn_ref, cw_ref, wa_ref, wc_ref, wo_ref,
             h1_ref):
        first = pl.program_id(0) == 0
        cb, _, _, _, cv = _conv_u(cbx_ref, halo_ref, cw_ref, first)
        conv = (cb * cv).astype(BF16)
        ap = jnp.dot(attn_ref[...], wa_ref[...], preferred_element_type=F32)
        cp = jnp.dot(conv, wc_ref[...], preferred_element_type=F32)
        ga = gate_ref[:, 0:D_MODEL].astype(F32)
        gc = gate_ref[:, D_MODEL:2 * D_MODEL].astype(F32)
        merged = (_sig(ga) * ap + _sig(gc) * cp).astype(BF16)
        h1_ref[...] = x_ref[...] + jnp.dot(merged, wo_ref[...], preferred_element_type=F32)

    return _pcall(
        body, "mix_fwd", (s // tm,),
        [_rows(tm, D_MODEL), _rows(tm, CBX_W), pl.BlockSpec((HALO, CBX_W), _prev_halo_map(tm)),
         _rows(tm, GATE_W), _rows(tm, ATTN_W), _full((3, CONV_W)), _full((ATTN_W, D_MODEL)),
         _full((CONV_W, D_MODEL)), _full((D_MODEL, D_MODEL))],
        [_rows(tm, D_MODEL)], [_sds((s, D_MODEL), F32)],
        (x, cbx, cbx, gates, attn, conv_w, wa, wc, wout), comm=comm)


def _ffn_up(h1, g, wup_lo, wup_hi, comm):
    s = h1.shape[0]
    tm = _row_tile(s, 512)
    half = D_MODEL // 2

    def body(h_ref, g_ref, wl_ref, wh_ref, hn_ref, up_ref):
        hv = h_ref[...]
        r = lax.rsqrt(jnp.mean(hv * hv, axis=-1, keepdims=True) + NORM_EPS)
        hn = (hv * r * g_ref[...]).astype(BF16)
        hn_ref[...] = hn
        for c in range(2 * D_FF // FF_CHUNK):
            sl = slice(c * FF_CHUNK, (c + 1) * FF_CHUNK)
            acc = lax.dot_general(hn[:, :half], wl_ref[sl, :], NT, preferred_element_type=F32)
            acc = acc + lax.dot_general(hn[:, half:], wh_ref[sl, :], NT, preferred_element_type=F32)
            up_ref[:, sl] = acc.astype(BF16)

    return _pcall(
        body, "ffn_up", (s // tm,),
        [_rows(tm, D_MODEL), _full((1, D_MODEL)), _resident((2 * D_FF, half)), _resident((2 * D_FF, half))],
        [_rows(tm, D_MODEL), _rows(tm, 2 * D_FF)],
        [_sds((s, D_MODEL), BF16), _sds((s, 2 * D_FF), BF16)],
        (h1, g, wup_lo, wup_hi), comm=comm)


def _ffn_conv_cols(up_ref, halo_ref, fcw_ref, first, off):
    u = up_ref[:, off:off + FF_CHUNK].astype(F32)
    uh = jnp.where(first, 0.0, halo_ref[:, off:off + FF_CHUNK].astype(F32))
    w = fcw_ref[:, off:off + FF_CHUNK]
    u1, u2 = _shifts_down(u, uh, (1, 2))
    return w[0:1] * u2 + w[1:2] * u1 + w[2:3] * u


def _ffn_down_loss(up_pre, fcw, wdown, h1, fnorm, target):
    s = h1.shape[0]
    tm = _row_tile(s)

    def body(up_ref, halo_ref, fcw_ref, wd_ref, h1_ref, fn_ref, t_ref, cu_ref, act_ref, dh2_ref, loss_ref, dfn_ref):
        i = pl.program_id(0)

        @pl.when(i == 0)
        def _():
            loss_ref[...] = jnp.zeros_like(loss_ref)
            dfn_ref[...] = jnp.zeros_like(dfn_ref)

        h2 = h1_ref[...]
        for c in range(D_FF // FF_CHUNK):
            gsl = slice(c * FF_CHUNK, (c + 1) * FF_CHUNK)
            vsl = slice(D_FF + c * FF_CHUNK, D_FF + (c + 1) * FF_CHUNK)
            gate = _ffn_conv_cols(up_ref, halo_ref, fcw_ref, i == 0, c * FF_CHUNK)
            cu_ref[:, gsl] = gate.astype(BF16)
            val = _ffn_conv_cols(up_ref, halo_ref, fcw_ref, i == 0, D_FF + c * FF_CHUNK)
            cu_ref[:, vsl] = val.astype(BF16)
            act = (gate * _sig(gate) * val).astype(BF16)
            act_ref[:, gsl] = act
            h2 = h2 + jnp.dot(act, wd_ref[gsl, :], preferred_element_type=F32)
        r = lax.rsqrt(jnp.mean(h2 * h2, axis=-1, keepdims=True) + NORM_EPS)
        yhat = h2 * r
        fn = fn_ref[...]
        diff = yhat * fn - t_ref[...]
        loss_ref[...] += 0.5 * jnp.sum(jnp.sum(diff * diff, axis=1, keepdims=True), axis=0, keepdims=True) / D_MODEL
        dy = diff * (1.0 / D_MODEL)
        dfn_ref[...] += jnp.sum(dy * yhat, axis=0, keepdims=True)
        dyh = dy * fn
        dh2_ref[...] = r * (dyh - yhat * jnp.mean(dyh * yhat, axis=-1, keepdims=True))

    return _pcall(
        body, "ffn_down_loss", (s // tm,),
        [_rows(tm, 2 * D_FF), pl.BlockSpec((HALO, 2 * D_FF), _prev_halo_map(tm)), _full((3, 2 * D_FF)),
         _resident((D_FF, D_MODEL)), _rows(tm, D_MODEL), _full((1, D_MODEL)), _rows(tm, D_MODEL)],
        [_rows(tm, 2 * D_FF), _rows(tm, D_FF), _rows(tm, D_MODEL), _full((1, 128)), _full((1, D_MODEL))],
        [_sds((s, 2 * D_FF), BF16), _sds((s, D_FF), BF16), _sds((s, D_MODEL), F32), _sds((1, 128), F32),
         _sds((1, D_MODEL), F32)],
        (up_pre, up_pre, fcw, wdown, h1, fnorm, target))[0]


def _ffn_bwd(dh2, wdown, up, up_pre, fcw, comm):
    s = dh2.shape[0]
    tm = _row_tile(s)

    def dup_cols(dh, up_ref, wd_ref, c):
        gsl = slice(c * FF_CHUNK, (c + 1) * FF_CHUNK)
        vsl = slice(D_FF + c * FF_CHUNK, D_FF + (c + 1) * FF_CHUNK)
        dact = lax.dot_general(dh, wd_ref[gsl, :], NT, preferred_element_type=F32)
        gate = up_ref[:, gsl].astype(F32)
        val = up_ref[:, vsl].astype(F32)
        sg = _sig(gate)
        return dact * val * (sg * (1.0 + gate * (1.0 - sg))), dact * gate * sg

    def body(dh_ref, dhn_ref, wd_ref, up_ref, upn_ref, x_ref, w_ref, dx_ref, dw_ref):
        i = pl.program_id(0)

        @pl.when(i == 0)
        def _():
            dw_ref[...] = jnp.zeros_like(dw_ref)

        last = i == s // tm - 1
        up1, up2 = _shift_matrix(tm, 1), _shift_matrix(tm, 2)
        dh = dh_ref[...].astype(BF16)
        dhn = dhn_ref[...].astype(BF16)
        for c in range(D_FF // FF_CHUNK):
            halves = zip(dup_cols(dh, up_ref, wd_ref, c), dup_cols(dhn, upn_ref, wd_ref, c),
                         (c * FF_CHUNK, D_FF + c * FF_CHUNK))
            for d, dn, off in halves:
                sl = slice(off, off + FF_CHUNK)
                dn = jnp.where(last, 0.0, dn)
                xv = x_ref[:, sl].astype(F32)
                wv = w_ref[:, sl]
                db = d.astype(BF16)
                d1, d2 = _mxu_shift_up(up1, db, dn, 1), _mxu_shift_up(up2, db, dn, 2)
                dx_ref[:, sl] = (wv[2:3] * d + wv[1:2] * d1 + wv[0:1] * d2).astype(BF16)
                dw_ref[0:1, sl] += jnp.sum(d2 * xv, axis=0, keepdims=True)
                dw_ref[1:2, sl] += jnp.sum(d1 * xv, axis=0, keepdims=True)
                dw_ref[2:3, sl] += jnp.sum(d * xv, axis=0, keepdims=True)

    return _pcall(
        body, "ffn_bwd", (s // tm,),
        [_rows(tm, D_MODEL), pl.BlockSpec((HALO, D_MODEL), _next_halo_map(tm, s)), _resident((D_FF, D_MODEL)),
         _rows(tm, 2 * D_FF), pl.BlockSpec((HALO, 2 * D_FF), _next_halo_map(tm, s)), _rows(tm, 2 * D_FF),
         _full((3, 2 * D_FF))],
        [_rows(tm, 2 * D_FF), _full((3, 2 * D_FF))],
        [_sds((s, 2 * D_FF), BF16), _sds((3, 2 * D_FF), F32)],
        (dh2, dh2, wdown, up, up, up_pre, fcw), comm=comm)


def _matmul_tn(a, b, tk, name, ts=1024, comm=None):
    s, ka = a.shape
    n = b.shape[1]
    ts = min(ts, s)
    steps = s // ts

    def body(a_ref, b_ref, o_ref, acc_ref):
        j = pl.program_id(1)

        @pl.when(j == 0)
        def _():
            acc_ref[...] = jnp.zeros_like(acc_ref)

        acc_ref[...] += lax.dot_general(a_ref[...].astype(BF16), b_ref[...].astype(BF16), TN,
                                        preferred_element_type=F32)

        @pl.when(j == steps - 1)
        def _():
            o_ref[...] = acc_ref[...].astype(BF16)

    outs, couts = _pcall(
        body, name, (ka // tk, steps),
        [pl.BlockSpec((ts, tk), lambda i, j: (j, i)), pl.BlockSpec((ts, n), lambda i, j: (j, 0))],
        [pl.BlockSpec((tk, n), lambda i, j: (i, 0))], [_sds((ka, n), BF16)],
        (a, b), scratch=[pltpu.VMEM((tk, n), F32)], comm=comm)
    return outs[0] if comm is None else (outs[0], couts)


def _norm_bwd_tile(xv, g, dy):
    r = lax.rsqrt(jnp.mean(xv * xv, axis=-1, keepdims=True) + NORM_EPS)
    xhat = xv * r
    dg = jnp.sum(dy * xhat, axis=0, keepdims=True)
    dyh = dy * g
    return r * (dyh - xhat * jnp.mean(dyh * xhat, axis=-1, keepdims=True)), dg


def _ffn_up_bwd(dup_pre, wup_lo, wup_hi, h1, g, dh2, comm):
    s = h1.shape[0]
    tm = _row_tile(s, 512)
    half = D_MODEL // 2

    def body(du_ref, wl_ref, wh_ref, h_ref, g_ref, dh2_ref, dh1_ref, dg_ref):
        @pl.when(pl.program_id(0) == 0)
        def _():
            dg_ref[...] = jnp.zeros_like(dg_ref)

        du = du_ref[...]
        dhn = jnp.concatenate([jnp.dot(du, wl_ref[...], preferred_element_type=F32),
                               jnp.dot(du, wh_ref[...], preferred_element_type=F32)], axis=1)
        dx, dg = _norm_bwd_tile(h_ref[...], g_ref[...], dhn)
        dg_ref[...] += dg
        dh1_ref[...] = dh2_ref[...] + dx

    return _pcall(
        body, "ffn_up_bwd", (s // tm,),
        [_rows(tm, 2 * D_FF), _resident((2 * D_FF, half)), _resident((2 * D_FF, half)), _rows(tm, D_MODEL),
         _full((1, D_MODEL)), _rows(tm, D_MODEL)],
        [_rows(tm, D_MODEL), _full((1, D_MODEL))],
        [_sds((s, D_MODEL), F32), _sds((1, D_MODEL), F32)],
        (dup_pre, wup_lo, wup_hi, h1, g, dh2), comm=comm)


def _mix_bwd(dh1, wout, gates, attn, wa, wc, cbx, conv_w, comm):
    s = dh1.shape[0]
    tm = _row_tile(s)
    steps = s // tm

    def body(dh_ref, wo_ref, gate_ref, attn_ref, wa_ref, wc_ref, cbx_ref, halo_ref, cw_ref,
             dg_ref, dattn_ref, dcb_ref, dcv_ref, gwo_ref, gwa_ref, gwc_ref, acc_o, acc_a, acc_c):
        i = pl.program_id(0)

        @pl.when(i == 0)
        def _():
            acc_o[...] = jnp.zeros_like(acc_o)
            acc_a[...] = jnp.zeros_like(acc_a)
            acc_c[...] = jnp.zeros_like(acc_c)

        cb, _, _, _, cv = _conv_u(cbx_ref, halo_ref, cw_ref, i == 0)
        attn = attn_ref[...]
        conv = (cb * cv).astype(BF16)
        ap = jnp.dot(attn, wa_ref[...], preferred_element_type=F32)
        cp = jnp.dot(conv, wc_ref[...], preferred_element_type=F32)
        dhb = dh_ref[...].astype(BF16)
        dm = lax.dot_general(dhb, wo_ref[...], NT, preferred_element_type=F32)
        sa = _sig(gate_ref[:, 0:D_MODEL].astype(F32))
        sc = _sig(gate_ref[:, D_MODEL:2 * D_MODEL].astype(F32))
        merged = (sa * ap + sc * cp).astype(BF16)
        da = (dm * sa).astype(BF16)
        dc = (dm * sc).astype(BF16)
        dg_ref[:, 0:D_MODEL] = (dm * ap * sa * (1.0 - sa)).astype(BF16)
        dg_ref[:, D_MODEL:2 * D_MODEL] = (dm * cp * sc * (1.0 - sc)).astype(BF16)
        dattn_ref[...] = lax.dot_general(da, wa_ref[...], NT, preferred_element_type=F32).astype(BF16)
        dconv = lax.dot_general(dc, wc_ref[...], NT, preferred_element_type=F32)
        dcb_ref[...] = (dconv * cv).astype(BF16)
        dcv_ref[...] = (dconv * cb).astype(BF16)
        acc_o[...] += lax.dot_general(merged, dhb, TN, preferred_element_type=F32)
        acc_a[...] += lax.dot_general(attn, da, TN, preferred_element_type=F32)
        acc_c[...] += lax.dot_general(conv, dc, TN, preferred_element_type=F32)

        @pl.when(i == steps - 1)
        def _():
            gwo_ref[...] = acc_o[...].astype(BF16)
            gwa_ref[...] = acc_a[...].astype(BF16)
            gwc_ref[...] = acc_c[...].astype(BF16)

    return _pcall(
        body, "mix_bwd", (steps,),
        [_rows(tm, D_MODEL), _full((D_MODEL, D_MODEL)), _rows(tm, GATE_W), _rows(tm, ATTN_W),
         _full((ATTN_W, D_MODEL)), _full((CONV_W, D_MODEL)), _rows(tm, CBX_W),
         pl.BlockSpec((HALO, CBX_W), _prev_halo_map(tm)), _full((3, CONV_W))],
        [_rows(tm, GATE_W), _rows(tm, ATTN_W), _rows(tm, CONV_W), _rows(tm, CONV_W),
         _full((D_MODEL, D_MODEL)), _full((ATTN_W, D_MODEL)), _full((CONV_W, D_MODEL))],
        [_sds((s, GATE_W), BF16), _sds((s, ATTN_W), BF16), _sds((s, CONV_W), BF16), _sds((s, CONV_W), BF16),
         _sds((D_MODEL, D_MODEL), BF16), _sds((ATTN_W, D_MODEL), BF16), _sds((CONV_W, D_MODEL), BF16)],
        (dh1, wout, gates, attn, wa, wc, cbx, cbx, conv_w),
        scratch=[pltpu.VMEM((D_MODEL, D_MODEL), F32), pltpu.VMEM((ATTN_W, D_MODEL), F32),
                 pltpu.VMEM((CONV_W, D_MODEL), F32)], comm=comm)


def _conv_branch_bwd(dcv, cbx, conv_w):
    s = dcv.shape[0]
    tm = _row_tile(s)

    def body(d_ref, dn_ref, cbx_ref, w_ref, dcc_ref, dcx_ref, dw_ref):
        i = pl.program_id(0)

        @pl.when(i == 0)
        def _():
            dw_ref[...] = jnp.zeros_like(dw_ref)

        last = i == s // tm - 1
        cc = cbx_ref[:, CONV_W:2 * CONV_W].astype(F32)
        cx = cbx_ref[:, 2 * CONV_W:3 * CONV_W].astype(F32)
        u = cc * cx
        d = d_ref[...].astype(F32)
        dn = jnp.where(last, 0.0, dn_ref[...].astype(F32))
        d1, d2 = _shifts_up(d, dn, (1, 2))
        du = w_ref[2:3, :] * d + w_ref[1:2, :] * d1 + w_ref[0:1, :] * d2
        dcc_ref[...] = (du * cx).astype(BF16)
        dcx_ref[...] = (du * cc).astype(BF16)
        dw_ref[0:1, :] += jnp.sum(d2 * u, axis=0, keepdims=True)
        dw_ref[1:2, :] += jnp.sum(d1 * u, axis=0, keepdims=True)
        dw_ref[2:3, :] += jnp.sum(d * u, axis=0, keepdims=True)

    return _pcall(
        body, "conv_branch_bwd", (s // tm,),
        [_rows(tm, CONV_W), pl.BlockSpec((HALO, CONV_W), _next_halo_map(tm, s)), _rows(tm, CBX_W),
         _full((3, CONV_W))],
        [_rows(tm, CONV_W), _rows(tm, CONV_W), _full((3, CONV_W))],
        [_sds((s, CONV_W), BF16), _sds((s, CONV_W), BF16), _sds((3, CONV_W), F32)],
        (dcv, dcv, cbx, conv_w))[0]


def _attn_bwd(qkv, sinks, attn, lse, dattn, comm):
    s = qkv.shape[0]

    def body(sinks_ref, q_ref, kp_ref, kc_ref, vp_ref, vc_ref, o_ref, lse_ref, do_ref,
             dq_ref, dk_ref, dv_ref, ds_ref):
        n = pl.program_id(0)

        @pl.when(n == 0)
        def _():
            dk_ref[...] = jnp.zeros_like(dk_ref)
            dv_ref[...] = jnp.zeros_like(dv_ref)
            ds_ref[...] = jnp.zeros_like(ds_ref)

        mask = _attn_mask(n)
        lower = _lower_lanes()
        lane = lax.broadcasted_iota(jnp.int32, (BLOCK, 128), 1)
        lower2 = lax.broadcasted_iota(jnp.int32, (2 * BLOCK, 128), 1) < HEAD_DIM
        lane1 = lax.broadcasted_iota(jnp.int32, (1, 128), 1)
        qv, ov, dov, lsev = q_ref[...], o_ref[...], do_ref[...], lse_ref[...]
        dk_fold, dv_fold = [], []
        dsink = jnp.zeros((1, 128), F32)
        for kh in range(2):
            qs = _stack_heads(qv, kh)
            dos = _stack_heads(dov, kh)
            os_ = _stack_heads(ov, kh)
            kd, vd = _dup_kv(kp_ref, kc_ref, kh), _dup_kv(vp_ref, vc_ref, kh)
            lse = jnp.concatenate(
                [jnp.sum(jnp.where(lane == kh * 4 + g, lsev, 0.0), axis=1, keepdims=True) for g in range(4)], axis=0)
            sc = lax.dot_general(qs, kd, NT, preferred_element_type=F32) * ATTN_SCALE
            p = jnp.exp(jnp.where(mask, sc, NEG) - lse)
            dp = lax.dot_general(dos, vd, NT, preferred_element_type=F32)
            delta = jnp.sum(dos.astype(F32) * os_.astype(F32), axis=1, keepdims=True)
            dsc = (p * (dp - delta) * ATTN_SCALE).astype(BF16)
            dqs = jnp.dot(dsc, kd, preferred_element_type=F32)
            for pair in range(2):
                lo = dqs[(2 * pair) * BLOCK:(2 * pair + 1) * BLOCK]
                hi = dqs[(2 * pair + 1) * BLOCK:(2 * pair + 2) * BLOCK]
                col = (kh * 2 + pair) * 128
                dq_ref[:, col:col + 128] = jnp.where(lower, lo, hi).astype(BF16)
            dkd = lax.dot_general(dsc, qs, TN, preferred_element_type=F32)
            dvd = lax.dot_general(p.astype(BF16), dos, TN, preferred_element_type=F32)
            dk_fold.append(dkd + pltpu.roll(dkd, HEAD_DIM, axis=1))
            dv_fold.append(dvd + pltpu.roll(dvd, HEAD_DIM, axis=1))
            psink = jnp.exp(_sink_col(sinks_ref, kh) - lse) * delta
            for g in range(4):
                tot = jnp.sum(psink[g * BLOCK:(g + 1) * BLOCK], axis=0, keepdims=True)
                dsink = dsink - jnp.where(lane1 == kh * 4 + g, tot, 0.0)
        dk2 = jnp.where(lower2, dk_fold[0], dk_fold[1])
        dv2 = jnp.where(lower2, dv_fold[0], dv_fold[1])
        ds_ref[...] += dsink
        cur = pl.ds(pl.multiple_of(n * BLOCK, BLOCK), BLOCK)
        dk_ref[cur, :] += dk2[BLOCK:]
        dv_ref[cur, :] += dv2[BLOCK:]

        @pl.when(n > 0)
        def _():
            prev = pl.ds(pl.multiple_of((n - 1) * BLOCK, BLOCK), BLOCK)
            dk_ref[prev, :] += dk2[:BLOCK]
            dv_ref[prev, :] += dv2[:BLOCK]

    blk = lambda w: pl.BlockSpec((BLOCK, w), lambda n: (n, 0))
    return _pcall(
        body, "attn_bwd", (s // BLOCK,),
        [pl.BlockSpec(memory_space=pltpu.SMEM)] + _attn_specs() + [blk(ATTN_W), blk(128), blk(ATTN_W)],
        [blk(ATTN_W), _full((s, KV_W)), _full((s, KV_W)), _full((1, 128))],
        [_sds((s, ATTN_W), BF16), _sds((s, KV_W), F32), _sds((s, KV_W), F32), _sds((1, 128), F32)],
        (sinks, qkv, qkv, qkv, qkv, qkv, attn, lse, dattn), comm=comm)


DPROJ_PIECES = (ATTN_W, KV_W, KV_W, CONV_W, CONV_W, CONV_W, GATE_W)
DPROJ_OFFSETS = tuple(sum(DPROJ_PIECES[:k]) for k in range(len(DPROJ_PIECES)))


def _grad_w_in(pieces, xn, comm):
    s = xn.shape[0]
    ts = min(1024, s)
    steps = s // ts
    rows0 = DPROJ_OFFSETS[6]

    def body(*refs):
        p_refs, b_ref, o_ref, acc_ref, stage_ref, sem = refs[:7], refs[7], refs[8], refs[9], refs[10], refs[11]
        i, j = pl.program_id(0), pl.program_id(1)

        @pl.when(j == 0)
        def _():
            acc_ref[...] = jnp.zeros_like(acc_ref)

        bv = b_ref[...]

        def flush(lo, n):
            stage_ref[0:n, :] = acc_ref[0:n, :].astype(BF16)
            cp = pltpu.make_async_copy(stage_ref.at[0:n, :], o_ref.at[lo:lo + n, :], sem)
            cp.start()
            cp.wait()

        @pl.when(i == 0)
        def _():
            for p_ref, off, w in zip(p_refs[:6], DPROJ_OFFSETS[:6], DPROJ_PIECES[:6]):
                acc_ref[off:off + w, :] += lax.dot_general(p_ref[...].astype(BF16), bv, TN,
                                                           preferred_element_type=F32)

            @pl.when(j == steps - 1)
            def _():
                flush(0, rows0)

        @pl.when(i == 1)
        def _():
            acc_ref[0:GATE_W, :] += lax.dot_general(p_refs[6][...], bv, TN, preferred_element_type=F32)

            @pl.when(j == steps - 1)
            def _():
                flush(rows0, GATE_W)

    def piece_spec(w, group):
        return pl.BlockSpec((ts, w), lambda i, j: (jnp.where(i == group, j, 0), 0))

    outs, couts = _pcall(
        body, "grad_w_in", (2, steps),
        [piece_spec(w, 0) for w in DPROJ_PIECES[:6]] + [piece_spec(GATE_W, 1),
                                                         pl.BlockSpec((ts, D_MODEL), lambda i, j: (j, 0))],
        [ANY], [_sds((IN_W, D_MODEL), BF16)], (*pieces, xn),
        scratch=[pltpu.VMEM((rows0, D_MODEL), F32), pltpu.VMEM((rows0, D_MODEL), BF16), pltpu.SemaphoreType.DMA],
        comm=comm)
    return outs[0], couts


def _inproj_bwd(pieces, win_t, x, g, dh1, comm):
    s = x.shape[0]
    tm = _row_tile(s, 512)

    def body(*refs):
        p_refs = refs[:7]
        w_ref, x_ref, g_ref, dh_ref, dx_ref, db_ref, dg_ref = refs[7:]

        @pl.when(pl.program_id(0) == 0)
        def _():
            db_ref[...] = jnp.zeros_like(db_ref)
            dg_ref[...] = jnp.zeros_like(dg_ref)

        dxn = jnp.zeros((tm, D_MODEL), F32)
        for p_ref, off, w in zip(p_refs, DPROJ_OFFSETS, DPROJ_PIECES):
            v = p_ref[...].astype(BF16)
            db_ref[:, off:off + w] += jnp.sum(v.astype(F32), axis=0, keepdims=True)
            dxn = dxn + jnp.dot(v, w_ref[off:off + w, :], preferred_element_type=F32)
        dx, dg = _norm_bwd_tile(x_ref[...], g_ref[...], dxn)
        dg_ref[...] += dg
        dx_ref[...] = dh_ref[...] + dx

    return _pcall(
        body, "inproj_bwd", (s // tm,),
        [_rows(tm, w) for w in DPROJ_PIECES] + [_resident((IN_W, D_MODEL)), _rows(tm, D_MODEL), _full((1, D_MODEL)),
                                                _rows(tm, D_MODEL)],
        [_rows(tm, D_MODEL), _full((1, IN_W)), _full((1, D_MODEL))],
        [_sds((s, D_MODEL), F32), _sds((1, IN_W), F32), _sds((1, D_MODEL), F32)],
        (*pieces, win_t, x, g, dh1), comm=comm)


def _adam_math(w, g, m, v):
    m2 = ADAM_B1 * m + (1.0 - ADAM_B1) * g
    v2 = ADAM_B2 * v + (1.0 - ADAM_B2) * (g * g)
    m_hat = m2 / (1.0 - ADAM_B1 ** ADAM_STEP)
    v_hat = v2 / (1.0 - ADAM_B2 ** ADAM_STEP)
    delta = -ADAM_LR * (m_hat / (jnp.sqrt(v_hat) + ADAM_EPS) + ADAM_WD * w)
    return delta, m2, v2


def _sum_slots(ref):
    tot = ref[0].astype(F32)
    for i in range(1, ref.shape[0]):
        tot = tot + ref[i].astype(F32)
    return tot


def _pair_add(mine, theirs, tr, name):
    r, c = mine.shape

    def body(a_ref, b_ref, o_ref):
        o_ref[...] = (a_ref[...].astype(F32) + b_ref[...].astype(F32)).astype(BF16)

    spec = pl.BlockSpec((tr, c), lambda i: (i, 0))
    return _pcall(body, name, (r // tr,), [spec, spec], [spec], [_sds((r, c), BF16)], (mine, theirs))[0][0]


def _sum_adamw(parts, w, m, v, tr, name):
    r, c = w.shape

    def body(p_ref, w_ref, m_ref, v_ref, g_ref, d_ref, m2_ref, v2_ref):
        g = _sum_slots(p_ref)
        g_ref[...] = g
        d_ref[...], m2_ref[...], v2_ref[...] = _adam_math(w_ref[...], g, m_ref[...], v_ref[...])

    spec = pl.BlockSpec((tr, c), lambda i: (i, 0))
    return _pcall(body, name, (r // tr,), [pl.BlockSpec((N_DEV, tr, c), lambda i: (0, i, 0)), spec, spec, spec],
                  [spec] * 4, [_sds((r, c), F32)] * 4, (parts, w, m, v))[0]


def _sum_parts_adamw(parts, w, m, v, tr, name):
    c = w.shape[1]
    tiles = [p.shape[1] // tr for p in parts]
    starts = [sum(tiles[:k]) for k in range(len(parts))]
    n_parts = len(parts)

    def body(*refs):
        p_refs = refs[:n_parts]
        w_ref, m_ref, v_ref, g_ref, d_ref, m2_ref, v2_ref = refs[n_parts:]
        i = pl.program_id(0)
        for p_ref, st, nt in zip(p_refs, starts, tiles):
            @pl.when(jnp.logical_and(i >= st, i < st + nt))
            def _(p_ref=p_ref):
                g_ref[...] = _sum_slots(p_ref)

        d_ref[...], m2_ref[...], v2_ref[...] = _adam_math(w_ref[...], g_ref[...], m_ref[...], v_ref[...])

    def part_spec(p, st, nt):
        return pl.BlockSpec((p.shape[0], tr, c), lambda i: (0, jnp.clip(i - st, 0, nt - 1), 0))

    spec = pl.BlockSpec((tr, c), lambda i: (i, 0))
    return _pcall(
        body, name, (sum(tiles),),
        [part_spec(p, st, nt) for p, st, nt in zip(parts, starts, tiles)] + [spec, spec, spec],
        [spec] * 4, [_sds(w.shape, F32)] * 4, (*parts, w, m, v))[0]


ROW_MIX, ROW_FFN, ROW_FINAL, ROW_SINKS, ROW_LOSS, ROW_BIN, ROW_CW, ROW_FCW = 0, 1, 2, 3, 4, 5, 10, 13
FCW_ROWS = 6


def _wide_pieces(width):
    return [(k * D_MODEL, min(D_MODEL, width - k * D_MODEL)) for k in range(-(-width // D_MODEL))]


def _pack_small(dmix, dffn, dfn, dsink, loss, dbin, dcw, dfcw):
    def body(mix_ref, ffn_ref, fn_ref, sink_ref, loss_ref, bin_ref, cw_ref, fcw_ref, o_ref):
        o_ref[...] = jnp.zeros_like(o_ref)
        o_ref[ROW_MIX:ROW_MIX + 1, :] = mix_ref[...]
        o_ref[ROW_FFN:ROW_FFN + 1, :] = ffn_ref[...]
        o_ref[ROW_FINAL:ROW_FINAL + 1, :] = fn_ref[...]
        o_ref[ROW_SINKS:ROW_SINKS + 1, 0:128] = sink_ref[...]
        o_ref[ROW_LOSS:ROW_LOSS + 1, 0:128] = loss_ref[...]
        for k, (off, w) in enumerate(_wide_pieces(IN_W)):
            o_ref[ROW_BIN + k:ROW_BIN + k + 1, 0:w] = bin_ref[:, off:off + w]
        o_ref[ROW_CW:ROW_CW + 3, 0:CONV_W] = cw_ref[...]
        for a in range(3):
            for k, (off, w) in enumerate(_wide_pieces(2 * D_FF)):
                row = ROW_FCW + FCW_ROWS * a + k
                o_ref[row:row + 1, 0:w] = fcw_ref[a:a + 1, off:off + w]

    return pl.pallas_call(body, name="pack_small", out_shape=_sds((SMALL_ROWS, D_MODEL), F32))(
        dmix, dffn, dfn, dsink, loss, dbin, dcw, dfcw)


def _small_sums_adamw(r_small, params):
    rows = (ROW_MIX, ROW_BIN, ROW_SINKS, ROW_FFN, ROW_FINAL)

    def body(*refs):
        r_ref, p_refs, o_refs = refs[0], refs[1:16], refs[16:]
        tot = _sum_slots(r_ref)
        for k, row in enumerate(rows):
            w_ref, m_ref, v_ref = p_refs[3 * k:3 * k + 3]
            g_ref, d_ref, m2_ref, v2_ref = o_refs[4 * k:4 * k + 4]
            for j, (off, w) in enumerate(_wide_pieces(w_ref.shape[1])):
                g_ref[:, off:off + w] = tot[row + j:row + j + 1, 0:w]
            d_ref[...], m2_ref[...], v2_ref[...] = _adam_math(w_ref[...], g_ref[...], m_ref[...], v_ref[...])
        cw_ref, fcw_ref, loss_ref = o_refs[20:]
        cw_ref[...] = tot[ROW_CW:ROW_CW + 3, 0:CONV_W]
        for a in range(3):
            for j, (off, w) in enumerate(_wide_pieces(2 * D_FF)):
                row = ROW_FCW + FCW_ROWS * a + j
                fcw_ref[a:a + 1, off:off + w] = tot[row:row + 1, 0:w]
        loss_ref[...] = tot[ROW_LOSS:ROW_LOSS + 1, 0:128]

    flat = [t for p in params for t in p]
    out_shape = [_sds(p[0].shape, F32) for p in params for _ in range(4)]
    out_shape += [_sds((3, CONV_W), F32), _sds((3, 2 * D_FF), F32), _sds((1, 128), F32)]
    res = pl.pallas_call(body, name="small_sums_adamw", out_shape=out_shape)(r_small, *flat)
    return [tuple(res[4 * k:4 * k + 4]) for k in range(5)], res[20], res[21], res[22]


def _adamw_pair(a, b):
    def body(*refs):
        for k in range(2):
            w_ref, g_ref, m_ref, v_ref = refs[4 * k:4 * k + 4]
            d_ref, m2_ref, v2_ref = refs[8 + 3 * k:8 + 3 * k + 3]
            d_ref[...], m2_ref[...], v2_ref[...] = _adam_math(w_ref[...], g_ref[...], m_ref[...], v_ref[...])

    out_shape = [_sds(a[0].shape, F32)] * 3 + [_sds(b[0].shape, F32)] * 3
    res = pl.pallas_call(body, name="adamw_conv_weights", out_shape=out_shape)(*a, *b)
    return tuple(res[:3]), tuple(res[3:])


def _pad_cols(a, c):
    return jnp.pad(a, ((0, 0), (0, c - a.shape[1])))


def _to_col_slabs(g):
    r = g.shape[0]
    return jnp.transpose(g.reshape(r, N_DEV, 128), (1, 0, 2)).reshape(N_DEV * r, 128)


def _from_col_slabs(t):
    r = t.shape[0] // N_DEV
    return jnp.transpose(t.reshape(N_DEV, r, 128), (1, 0, 2)).reshape(r, N_DEV * 128)


def _slots(t):
    return t.reshape(N_DEV, t.shape[0] // N_DEV, t.shape[1])


def kernel(x, mix_norm, w_in, b_in, sinks, conv_w, w_attn_branch, w_conv_branch, w_out, ffn_norm, w_up, ffn_conv_w, w_down, final_norm, loss_target, m_mix_norm, m_w_in, m_b_in, m_sinks, m_conv_w, m_w_attn_branch, m_w_conv_branch, m_w_out, m_ffn_norm, m_w_up, m_ffn_conv_w, m_w_down, m_final_norm, v_mix_norm, v_w_in, v_b_in, v_sinks, v_conv_w, v_w_attn_branch, v_w_conv_branch, v_w_out, v_ffn_norm, v_w_up, v_ffn_conv_w, v_w_down, v_final_norm):
    xs, tgt = x[0], loss_target[0]
    me = 4 * lax.axis_index("x") + 2 * lax.axis_index("y") + lax.axis_index("c")
    in_rows, up_rows = IN_W // N_DEV, 2 * D_FF // N_DEV

    conv_sh = jnp.concatenate([_pad_cols(ffn_conv_w[0], 768), _pad_cols(conv_w[0], 768),
                               jnp.zeros((2, 768), F32)], axis=0)
    win_sh, wup_sh = w_in[0].T.astype(BF16), w_up[0].T.astype(BF16)
    wout_sh, wdown_sh = w_out[0].astype(BF16), w_down[0].astype(BF16)
    wa_sh, wc_sh = w_attn_branch[0].astype(BF16), w_conv_branch[0].astype(BF16)

    half = D_MODEL // 2
    (win_t,) = _exchange_only(_AllGather([win_sh]), "gather_w_in")
    (xn, qkv, cbx, gates), (wa_s, wc_s, wout, conv_g) = _norm_inproj(
        xs, mix_norm, win_t, b_in, _AllGather([wa_sh, wc_sh, wout_sh, conv_sh]))
    (attn, lse), (wup_lo,) = _attn_fwd(qkv, sinks, _AllGather([wup_sh[:, :half]]))
    wa, wc = _from_col_slabs(wa_s), _from_col_slabs(wc_s)
    conv_g = conv_g.reshape(N_DEV, 8, 768)
    fcw = jnp.transpose(conv_g[:, 0:3, :up_rows], (1, 0, 2)).reshape(3, 2 * D_FF)
    cw = jnp.transpose(conv_g[:, 3:6, :CONV_W // N_DEV], (1, 0, 2)).reshape(3, CONV_W)
    (h1,), (wup_hi,) = _mix_fwd(xs, cbx, gates, attn, cw, wa, wc, wout, _AllGather([wup_sh[:, half:]]))
    (hn, up_pre), (wdown,) = _ffn_up(h1, ffn_norm, wup_lo, wup_hi, _AllGather([wdown_sh]))
    up, act, dh2, loss_p, dfn_p = _ffn_down_loss(up_pre, fcw, wdown, h1, final_norm.reshape(1, D_MODEL), tgt)

    dn_rows, q_up = D_FF // N_DEV, up_rows // 4
    g_wdown = _matmul_tn(act, dh2, FF_CHUNK, "grad_w_down")
    (dup_pre, dfcw_p), (r_wdown,) = _ffn_bwd(dh2, wdown, up, up_pre, fcw, _ReduceScatter([(g_wdown, 0, dn_rows)]))
    g_wup_t = _matmul_tn(dup_pre, hn, FF_CHUNK, "grad_w_up")
    (dh1, dffn_p), (r_wup_a,) = _ffn_up_bwd(dup_pre, wup_lo, wup_hi, h1, ffn_norm, dh2,
                                            _ReduceScatter([(g_wup_t, 0, q_up)]))
    (dgates, dattn, dcb, dcv, g_wout, g_wa_nat, g_wc_nat), (r_wup_b,) = _mix_bwd(
        dh1, wout, gates, attn, wa, wc, cbx, cw, _ReduceScatter([(g_wup_t, q_up, q_up)]))
    g_wa, g_wc = _to_col_slabs(g_wa_nat), _to_col_slabs(g_wc_nat)
    dcc, dcx, dcw_p = _conv_branch_bwd(dcv, cbx, cw)
    (dq, dk, dv, dsink_p), (r_wup_c, r_wout, r_wa, r_wc) = _attn_bwd(
        qkv, sinks, attn, lse, dattn,
        _ReduceScatter([(g_wup_t, 2 * q_up, q_up), (g_wout, 0, D_MODEL // N_DEV), (g_wa, 0, ATTN_W),
                        (g_wc, 0, CONV_W)]))
    dproj = (dq, dk, dv, dcb, dcc, dcx, dgates)
    g_win_t, (r_wup_d,) = _grad_w_in(dproj, xn, _ReduceScatter([(g_wup_t, 3 * q_up, q_up)]))
    (win_theirs,) = _exchange_only(_PairExchange([g_win_t]), "pair_exchange_w_in")
    win_mine = lax.dynamic_index_in_dim(g_win_t.reshape(4, 2, in_rows, D_MODEL), lax.axis_index("c"), axis=1,
                                        keepdims=False).reshape(4 * in_rows, D_MODEL)
    q_win = _pair_add(win_mine, win_theirs, in_rows // 2, "pair_add_w_in")
    (dx, dbin_p, dmix_p), (r_win,) = _inproj_bwd(dproj, win_t, xs, mix_norm, dh1, _ChipExchange([q_win]))

    small = _pack_small(dmix_p, dffn_p, dfn_p, dsink_p, loss_p, dbin_p, dcw_p, dfcw_p)
    (r_small,) = _exchange_only(_ReduceScatter([], [small]), "exchange_small")

    fn2, m_fn2, v_fn2 = (t.reshape(1, D_MODEL) for t in (final_norm, m_final_norm, v_final_norm))
    small_res, g_cw_full, g_fcw_full, loss_row = _small_sums_adamw(
        _slots(r_small), [(mix_norm, m_mix_norm, v_mix_norm), (b_in, m_b_in, v_b_in), (sinks, m_sinks, v_sinks),
                          (ffn_norm, m_ffn_norm, v_ffn_norm), (fn2, m_fn2, v_fn2)])
    loss = loss_row[0, 0]
    g_cw = lax.dynamic_slice_in_dim(g_cw_full, me * (CONV_W // N_DEV), CONV_W // N_DEV, axis=1)
    g_fcw = lax.dynamic_slice_in_dim(g_fcw_full, me * up_rows, up_rows, axis=1)
    cw_res, fcw_res = _adamw_pair((conv_w[0], g_cw, m_conv_w[0], v_conv_w[0]),
                                  (ffn_conv_w[0], g_fcw, m_ffn_conv_w[0], v_ffn_conv_w[0]))

    big = {}
    big["w_in"] = tuple(t.T for t in _sum_parts_adamw(
        [r_win.reshape(4, in_rows, D_MODEL)], w_in[0].T, m_w_in[0].T, v_w_in[0].T, in_rows // 2, "adamw_w_in"))
    big["w_up"] = tuple(t.T for t in _sum_parts_adamw(
        [_slots(r_wup_a), _slots(r_wup_b), _slots(r_wup_c), _slots(r_wup_d)], w_up[0].T, m_w_up[0].T, v_w_up[0].T, q_up,
        "adamw_w_up"))
    big["w_out"] = _sum_adamw(_slots(r_wout), w_out[0], m_w_out[0], v_w_out[0], 128, "adamw_w_out")
    big["w_down"] = _sum_adamw(_slots(r_wdown), w_down[0], m_w_down[0], v_w_down[0], dn_rows // 2, "adamw_w_down")
    big["w_attn_branch"] = _sum_adamw(_slots(r_wa), w_attn_branch[0], m_w_attn_branch[0], v_w_attn_branch[0], 256,
                                      "adamw_w_attn_branch")
    big["w_conv_branch"] = _sum_adamw(_slots(r_wc), w_conv_branch[0], m_w_conv_branch[0], v_w_conv_branch[0], 256,
                                      "adamw_w_conv_branch")

    res = dict(zip(("mix_norm", "b_in", "sinks", "ffn_norm"), small_res[:4]))
    res["final_norm"] = tuple(t.reshape(final_norm.shape) for t in small_res[4])
    res["conv_w"] = tuple(t.reshape(conv_w.shape) for t in (g_cw,) + cw_res)
    res["ffn_conv_w"] = tuple(t.reshape(ffn_conv_w.shape) for t in (g_fcw,) + fcw_res)
    for name, ref_w in (("w_in", w_in), ("w_up", w_up), ("w_out", w_out), ("w_down", w_down),
                        ("w_attn_branch", w_attn_branch), ("w_conv_branch", w_conv_branch)):
        res[name] = tuple(t.reshape(ref_w.shape) for t in big[name])

    order = ["mix_norm", "w_in", "b_in", "sinks", "conv_w", "w_attn_branch", "w_conv_branch", "w_out",
             "ffn_norm", "w_up", "ffn_conv_w", "w_down", "final_norm"]
    out = [loss, dx.reshape(x.shape)]
    for k in range(4):
        out += [res[name][k] for name in order]
    return tuple(out)
```

```python
import math

import jax
import jax.numpy as jnp
from jax import lax
from jax.experimental import pallas as pl
from jax.experimental.pallas import tpu as pltpu

F32 = jnp.float32
BF16 = jnp.bfloat16
MESH = pl.DeviceIdType.MESH
N_DEV = 8

D_MODEL = 1024
HEAD_DIM = 64
N_HEADS = 8
BLOCK = 128
ATTN_W = 512
KV_W = 128
CONV_W = 512
QKV_W = ATTN_W + 2 * KV_W
CBX_W = 3 * CONV_W
GATE_W = 2 * D_MODEL
IN_W = QKV_W + CBX_W + GATE_W
D_FF = 2816
FF_CHUNK = 1408
NORM_EPS = 1e-5
ATTN_SCALE = HEAD_DIM ** -0.5
NEG = -1e30
HALO = 16

ADAM_LR = 0.001
ADAM_B1 = 0.9
ADAM_B2 = 0.999
ADAM_EPS = 1e-08
ADAM_WD = 0.01
ADAM_STEP = 10

VMEM_LIMIT = 56 * 1024 * 1024
SMALL_ROWS = 32

NT = (((1,), (1,)), ((), ()))
TN = (((0,), (0,)), ((), ()))
ANY = pl.BlockSpec(memory_space=pl.ANY)


def _sig(v):
    return 1.0 / (1.0 + jnp.exp(-v))


def _row_tile(s, pref=256):
    return pref if s % pref == 0 else s


def _shifts_down(u, halo, ks):
    ext = jnp.concatenate([halo, u], axis=0)
    return [pltpu.roll(ext, k, axis=0)[HALO:, :] for k in ks]


def _shifts_up(u, halo, ks):
    n = u.shape[0]
    ext = jnp.concatenate([u, halo], axis=0)
    return [pltpu.roll(ext, n + HALO - k, axis=0)[:n, :] for k in ks]


def _shift_matrix(n, k):
    row = lax.broadcasted_iota(jnp.int32, (n, n), 0)
    col = lax.broadcasted_iota(jnp.int32, (n, n), 1)
    return jnp.where(col == row + k, 1.0, 0.0).astype(BF16)


def _mxu_shift_up(mat, ub, halo, k):
    n = ub.shape[0]
    v = jnp.dot(mat, ub, preferred_element_type=F32)
    row = lax.broadcasted_iota(jnp.int32, (8, ub.shape[1]), 0)
    tail = v[n - 8:, :]
    for t in range(k):
        tail = jnp.where(row == 8 - k + t, halo[t:t + 1, :], tail)
    return jnp.concatenate([v[:n - 8, :], tail], axis=0)


def _prev_halo_map(tm):
    return lambda i: (jnp.maximum(i * (tm // HALO) - 1, 0), 0)


def _next_halo_map(tm, s):
    return lambda i: (jnp.minimum((i + 1) * (tm // HALO), s // HALO - 1), 0)


def _full(shape):
    return pl.BlockSpec(shape, lambda *_: (0,) * len(shape))


def _resident(shape):
    return pl.BlockSpec(shape, lambda *_: (0,) * len(shape), pipeline_mode=pl.Buffered(1))


def _rows(tm, c):
    return pl.BlockSpec((tm, c), lambda i: (i, 0))


def _sds(shape, dtype):
    return jax.ShapeDtypeStruct(shape, dtype)


def _my_place():
    x, y, c = lax.axis_index("x"), lax.axis_index("y"), lax.axis_index("c")
    return x, y, c


ALL_DEVICES_BARRIER = 0


def _all_devices_barrier():
    x, y, c = _my_place()
    barrier = pltpu.get_barrier_semaphore()
    for j in range(1, N_DEV):
        pl.semaphore_signal(barrier, inc=1, device_id=(x ^ (j >> 2), y ^ ((j >> 1) & 1), c ^ (j & 1)),
                            device_id_type=MESH)
    pl.semaphore_wait(barrier, N_DEV - 1)


def _start_exchange(remote, local):
    for cp in local + remote:
        cp.start()


def _finish_exchange(remote, local):
    for cp in remote:
        cp.wait_recv()
    for cp in remote:
        cp.wait_send()
    for cp in local:
        cp.wait()


class _AllGather:
    own_barrier = True

    def __init__(self, shards):
        self.ins = list(shards)
        n = len(shards)
        self.out_shape = [_sds((N_DEV * s.shape[0], s.shape[1]), s.dtype) for s in shards]
        self.sems = [pltpu.SemaphoreType.DMA((7 * n,)), pltpu.SemaphoreType.DMA((7 * n,)),
                     pltpu.SemaphoreType.DMA((n,))]

    def _parts(self, ins, outs, sems):
        send_sems, recv_sems, local_sems = sems
        x, y, c = _my_place()
        me, sibling = (x, y, c), (x, y, 1 - c)
        chips = [(1 - x, y), (x, 1 - y), (1 - x, 1 - y)]

        def rows(k, dev):
            r = ins[k].shape[0]
            start = pl.multiple_of((4 * dev[0] + 2 * dev[1] + dev[2]) * r, 8)
            return outs[k].at[pl.ds(start, r), :]

        def copy(k, j, block, to, src=None):
            return pltpu.make_async_remote_copy(
                src_ref=rows(k, block) if src is None else src, dst_ref=rows(k, block),
                send_sem=send_sems.at[7 * k + j], recv_sem=recv_sems.at[7 * k + j],
                device_id=to, device_id_type=MESH)

        n = len(ins)
        mine = [pltpu.make_async_copy(ins[k], rows(k, me), local_sems.at[k]) for k in range(n)]
        first = []
        for k in range(n):
            first.append(copy(k, 0, me, sibling, src=ins[k]))
            first += [copy(k, 1 + j, me, (*chip, c), src=ins[k]) for j, chip in enumerate(chips)]
        return me, sibling, chips, copy, mine, first

    def start(self, ins, outs, sems):
        _, _, _, _, mine, first = self._parts(ins, outs, sems)
        _start_exchange(first, mine)

    def finish(self, ins, outs, sems):
        me, sibling, chips, copy, mine, first = self._parts(ins, outs, sems)
        c = me[2]
        n = len(ins)
        passed = []
        for j, chip in enumerate(chips):
            for k in range(n):
                copy(k, 1 + j, (*chip, c), me).wait_recv()
                fwd = copy(k, 4 + j, (*chip, c), sibling)
                fwd.start()
                passed.append(fwd)
        for k in range(n):
            copy(k, 0, sibling, me).wait_recv()
            for j, chip in enumerate(chips):
                copy(k, 4 + j, (*chip, 1 - c), me).wait_recv()
        for cp in first + passed:
            cp.wait_send()
        for cp in mine:
            cp.wait()


class _ReduceScatter:
    own_barrier = True

    def __init__(self, parts, bcast=()):
        self.parts = [(lo, cnt) for _, lo, cnt in parts]
        self.n_parts = len(parts)
        self.ins = [a for a, _, _ in parts] + list(bcast)
        self.out_shape = [_sds((N_DEV * cnt, a.shape[1]), a.dtype) for a, _, cnt in parts]
        self.out_shape += [_sds((N_DEV * b.shape[0], b.shape[1]), b.dtype) for b in bcast]
        n = len(self.ins)
        self.sems = [pltpu.SemaphoreType.DMA((7 * n,)), pltpu.SemaphoreType.DMA((7 * n,)),
                     pltpu.SemaphoreType.DMA((n,))]

    def _copies(self, ins, outs, sems):
        send_sems, recv_sems, local_sems = sems
        x, y, c = _my_place()
        me_idx = 4 * x + 2 * y + c
        remote, local = [], []
        for k in range(len(ins)):
            cnt = outs[k].shape[0] // N_DEV
            dst = outs[k].at[pl.ds(pl.multiple_of(me_idx * cnt, 8), cnt), :]
            if k < self.n_parts:
                lo, _ = self.parts[k]
                r = ins[k].shape[0] // N_DEV
                src_of = lambda idx: ins[k].at[pl.ds(pl.multiple_of(idx * r + lo, 8), cnt), :]
            else:
                src_of = lambda idx: ins[k]
            local.append(pltpu.make_async_copy(src_of(me_idx), dst, local_sems.at[k]))
            for j in range(1, N_DEV):
                peer = (x ^ (j >> 2), y ^ ((j >> 1) & 1), c ^ (j & 1))
                peer_idx = 4 * peer[0] + 2 * peer[1] + peer[2]
                remote.append(pltpu.make_async_remote_copy(
                    src_ref=src_of(peer_idx), dst_ref=dst,
                    send_sem=send_sems.at[7 * k + j - 1], recv_sem=recv_sems.at[7 * k + j - 1],
                    device_id=peer, device_id_type=MESH))
        return remote, local

    def start(self, ins, outs, sems):
        _start_exchange(*self._copies(ins, outs, sems))

    def finish(self, ins, outs, sems):
        _finish_exchange(*self._copies(ins, outs, sems))


class _PairExchange:
    own_barrier = False

    def __init__(self, arrays):
        self.ins = list(arrays)
        n = len(arrays)
        self.out_shape = [_sds((a.shape[0] // 2, a.shape[1]), a.dtype) for a in arrays]
        self.sems = [pltpu.SemaphoreType.DMA((4 * n,)), pltpu.SemaphoreType.DMA((4 * n,))]

    def _copies(self, ins, outs, sems):
        send_sems, recv_sems = sems
        x, y, c = _my_place()
        remote = []
        for k in range(len(ins)):
            r = ins[k].shape[0] // N_DEV
            for chip in range(4):
                sib = ins[k].at[pl.ds(pl.multiple_of((2 * chip + 1 - c) * r, 8), r), :]
                remote.append(pltpu.make_async_remote_copy(
                    src_ref=sib, dst_ref=outs[k].at[pl.ds(chip * r, r), :],
                    send_sem=send_sems.at[4 * k + chip], recv_sem=recv_sems.at[4 * k + chip],
                    device_id=(x, y, 1 - c), device_id_type=MESH))
        return remote

    def start(self, ins, outs, sems):
        for cp in self._copies(ins, outs, sems):
            cp.start()

    def finish(self, ins, outs, sems):
        remote = self._copies(ins, outs, sems)
        for cp in remote:
            cp.wait_recv()
        for cp in remote:
            cp.wait_send()


class _ChipExchange:
    own_barrier = False

    def __init__(self, arrays):
        self.ins = list(arrays)
        self.out_shape = [_sds(a.shape, a.dtype) for a in arrays]
        n = len(self.ins)
        self.sems = [pltpu.SemaphoreType.DMA((3 * n,)), pltpu.SemaphoreType.DMA((3 * n,)),
                     pltpu.SemaphoreType.DMA((n,))]

    def _copies(self, ins, outs, sems):
        send_sems, recv_sems, local_sems = sems
        x, y, c = _my_place()
        my_chip = 2 * x + y
        remote, local = [], []
        for k in range(len(ins)):
            r = ins[k].shape[0] // 4
            dst = outs[k].at[pl.ds(pl.multiple_of(my_chip * r, 8), r), :]
            local.append(pltpu.make_async_copy(ins[k].at[pl.ds(pl.multiple_of(my_chip * r, 8), r), :], dst,
                                               local_sems.at[k]))
            for j in range(1, 4):
                px, py = x ^ (j >> 1), y ^ (j & 1)
                src = ins[k].at[pl.ds(pl.multiple_of((2 * px + py) * r, 8), r), :]
                remote.append(pltpu.make_async_remote_copy(
                    src_ref=src, dst_ref=dst, send_sem=send_sems.at[3 * k + j - 1],
                    recv_sem=recv_sems.at[3 * k + j - 1], device_id=(px, py, c), device_id_type=MESH))
        return remote, local

    def start(self, ins, outs, sems):
        _start_exchange(*self._copies(ins, outs, sems))

    def finish(self, ins, outs, sems):
        _finish_exchange(*self._copies(ins, outs, sems))


def _pcall(body, name, grid, in_specs, out_specs, out_shape, args, scratch=(), comm=None):
    params = pltpu.CompilerParams(dimension_semantics=("arbitrary",) * len(grid), vmem_limit_bytes=VMEM_LIMIT)
    in_specs, out_specs, out_shape, scratch = list(in_specs), list(out_specs), list(out_shape), list(scratch)
    if comm is None:
        res = pl.pallas_call(body, name=name, grid=grid, in_specs=in_specs, out_specs=out_specs, out_shape=out_shape,
                             scratch_shapes=scratch, compiler_params=params)(*args)
        return list(res), []
    n_in, n_out, n_scr = len(in_specs), len(out_specs), len(scratch)
    ci, co = len(comm.ins), len(comm.out_shape)
    total = math.prod(grid)

    def carried(*refs):
        bounds = [0, n_in, n_in + ci, n_in + ci + n_out, n_in + ci + n_out + co, n_in + ci + n_out + co + n_scr]
        ins, cins, outs, couts, scr = (refs[a:b] for a, b in zip(bounds[:-1], bounds[1:]))
        sems = refs[bounds[-1]:]
        step = pl.program_id(0)
        for d in range(1, len(grid)):
            step = step * grid[d] + pl.program_id(d)

        @pl.when(step == 0)
        def _():
            if comm.own_barrier:
                _all_devices_barrier()
            comm.start(cins, couts, sems)

        body(*ins, *outs, *scr)

        @pl.when(step == total - 1)
        def _():
            comm.finish(cins, couts, sems)

    if comm.own_barrier:
        params = pltpu.CompilerParams(dimension_semantics=("arbitrary",) * len(grid), vmem_limit_bytes=VMEM_LIMIT,
                                      collective_id=ALL_DEVICES_BARRIER)
    res = pl.pallas_call(
        carried, name=name, grid=grid, in_specs=in_specs + [ANY] * ci, out_specs=out_specs + [ANY] * co,
        out_shape=out_shape + comm.out_shape, scratch_shapes=scratch + comm.sems, compiler_params=params,
    )(*args, *comm.ins)
    return list(res[:n_out]), list(res[n_out:])


def _exchange_only(comm, name):
    def body(*refs):
        ci, co = len(comm.ins), len(comm.out_shape)
        if comm.own_barrier:
            _all_devices_barrier()
        comm.start(refs[:ci], refs[ci:ci + co], refs[ci + co:])
        comm.finish(refs[:ci], refs[ci:ci + co], refs[ci + co:])

    params = pltpu.CompilerParams(collective_id=ALL_DEVICES_BARRIER if comm.own_barrier else None)
    return pl.pallas_call(body, name=name, out_shape=comm.out_shape, in_specs=[ANY] * len(comm.ins),
                          out_specs=[ANY] * len(comm.out_shape), scratch_shapes=comm.sems,
                          compiler_params=params)(*comm.ins)


def _norm_inproj(x, g, win_t, b_in, comm):
    s = x.shape[0]
    tm = _row_tile(s, 512)
    widths = (QKV_W, CBX_W, GATE_W)

    def body(x_ref, g_ref, w_ref, b_ref, xn_ref, qkv_ref, cbx_ref, gate_ref):
        xv = x_ref[...]
        r = lax.rsqrt(jnp.mean(xv * xv, axis=-1, keepdims=True) + NORM_EPS)
        xn = (xv * r * g_ref[...]).astype(BF16)
        xn_ref[...] = xn
        off = 0
        for o_ref, w in zip((qkv_ref, cbx_ref, gate_ref), widths):
            acc = lax.dot_general(xn, w_ref[off:off + w, :], NT, preferred_element_type=F32)
            o_ref[...] = (acc + b_ref[:, off:off + w]).astype(BF16)
            off += w

    return _pcall(
        body, "norm_inproj", (s // tm,),
        [_rows(tm, D_MODEL), _full((1, D_MODEL)), _resident((IN_W, D_MODEL)), _full((1, IN_W))],
        [_rows(tm, D_MODEL)] + [_rows(tm, w) for w in widths],
        [_sds((s, D_MODEL), BF16)] + [_sds((s, w), BF16) for w in widths],
        (x, g, win_t, b_in), comm=comm)


def _attn_specs():
    prev = lambda n: jnp.maximum(n - 1, 0)
    return [pl.BlockSpec((BLOCK, ATTN_W), lambda n: (n, 0)),
            pl.BlockSpec((BLOCK, KV_W), lambda n: (prev(n), ATTN_W // KV_W)),
            pl.BlockSpec((BLOCK, KV_W), lambda n: (n, ATTN_W // KV_W)),
            pl.BlockSpec((BLOCK, KV_W), lambda n: (prev(n), ATTN_W // KV_W + 1)),
            pl.BlockSpec((BLOCK, KV_W), lambda n: (n, ATTN_W // KV_W + 1))]


def _lower_lanes():
    return lax.broadcasted_iota(jnp.int32, (BLOCK, 128), 1) < HEAD_DIM


def _stack_heads(val, kh):
    lower = _lower_lanes()
    parts = []
    for g in range(4):
        h = kh * 4 + g
        blk = val[:, (h // 2) * 128:(h // 2 + 1) * 128]
        keep = lower if h % 2 == 0 else jnp.logical_not(lower)
        parts.append(jnp.where(keep, blk, jnp.zeros_like(blk)))
    return jnp.concatenate(parts, axis=0)


def _dup_kv(prev_ref, cur_ref, kh):
    t = jnp.concatenate([prev_ref[...], cur_ref[...]], axis=0).astype(F32)
    rolled = pltpu.roll(t, HEAD_DIM, axis=1)
    lower = lax.broadcasted_iota(jnp.int32, t.shape, 1) < HEAD_DIM
    dup = jnp.where(lower, t, rolled) if kh == 0 else jnp.where(lower, rolled, t)
    return dup.astype(BF16)


def _attn_mask(n):
    row = lax.broadcasted_iota(jnp.int32, (4 * BLOCK, 2 * BLOCK), 0)
    kj = lax.broadcasted_iota(jnp.int32, (4 * BLOCK, 2 * BLOCK), 1)
    dist = (row & (BLOCK - 1)) + BLOCK - kj
    band = jnp.logical_and(dist >= 0, dist < BLOCK)
    return jnp.logical_and(band, jnp.logical_or(kj >= BLOCK, n > 0))


def _sink_col(sinks_ref, kh):
    gi = lax.broadcasted_iota(jnp.int32, (4 * BLOCK, 1), 0) // BLOCK
    col = jnp.zeros((4 * BLOCK, 1), F32)
    for g in range(4):
        col = jnp.where(gi == g, sinks_ref[0, kh * 4 + g], col)
    return col


def _attn_fwd(qkv, sinks, comm):
    s = qkv.shape[0]

    def body(sinks_ref, q_ref, kp_ref, kc_ref, vp_ref, vc_ref, o_ref, lse_ref):
        n = pl.program_id(0)
        mask = _attn_mask(n)
        lower = _lower_lanes()
        lane = lax.broadcasted_iota(jnp.int32, (BLOCK, 128), 1)
        qv = q_ref[...]
        lse_out = jnp.zeros((BLOCK, 128), F32)
        for kh in range(2):
            qs = _stack_heads(qv, kh)
            kd, vd = _dup_kv(kp_ref, kc_ref, kh), _dup_kv(vp_ref, vc_ref, kh)
            sc = lax.dot_general(qs, kd, NT, preferred_element_type=F32) * ATTN_SCALE
            sc = jnp.where(mask, sc, NEG)
            sink = _sink_col(sinks_ref, kh)
            m = jnp.maximum(jnp.max(sc, axis=1, keepdims=True), sink)
            p = jnp.exp(sc - m)
            l = jnp.sum(p, axis=1, keepdims=True) + jnp.exp(sink - m)
            o = jnp.dot(p.astype(BF16), vd, preferred_element_type=F32) / l
            lse = m + jnp.log(l)
            for pair in range(2):
                lo = o[(2 * pair) * BLOCK:(2 * pair + 1) * BLOCK]
                hi = o[(2 * pair + 1) * BLOCK:(2 * pair + 2) * BLOCK]
                col = (kh * 2 + pair) * 128
                o_ref[:, col:col + 128] = jnp.where(lower, lo, hi).astype(BF16)
            for g in range(4):
                lse_out = jnp.where(lane == kh * 4 + g, lse[g * BLOCK:(g + 1) * BLOCK], lse_out)
        lse_ref[...] = lse_out

    return _pcall(
        body, "attn_fwd", (s // BLOCK,),
        [pl.BlockSpec(memory_space=pltpu.SMEM)] + _attn_specs(),
        [pl.BlockSpec((BLOCK, ATTN_W), lambda n: (n, 0)), pl.BlockSpec((BLOCK, 128), lambda n: (n, 0))],
        [_sds((s, ATTN_W), BF16), _sds((s, 128), F32)],
        (sinks, qkv, qkv, qkv, qkv, qkv), comm=comm)


def _conv_u(cbx_ref, halo_ref, w_ref, first):
    cb = cbx_ref[:, 0:CONV_W].astype(F32)
    cc = cbx_ref[:, CONV_W:2 * CONV_W].astype(F32)
    cx = cbx_ref[:, 2 * CONV_W:3 * CONV_W].astype(F32)
    u = cc * cx
    uh = halo_ref[:, CONV_W:2 * CONV_W].astype(F32) * halo_ref[:, 2 * CONV_W:3 * CONV_W].astype(F32)
    uh = jnp.where(first, 0.0, uh)
    u1, u2 = _shifts_down(u, uh, (1, 2))
    cv = w_ref[0:1, :] * u2 + w_ref[1:2, :] * u1 + w_ref[2:3, :] * u
    return cb, cc, cx, u, cv


def _mix_fwd(x, cbx, gates, attn, conv_w, wa, wc, wout, comm):
    s = x.shape[0]
    tm = _row_tile(s)

    def body(x_ref, cbx_ref, halo_ref, gate_ref, attn_ref, cw_ref, wa_ref, wc_ref, wo_ref,
             h1_ref):
        first = pl.program_id(0) == 0
        cb, _, _, _, cv = _conv_u(cbx_ref, halo_ref, cw_ref, first)
        conv = (cb * cv).astype(BF16)
        ap = jnp.dot(attn_ref[...], wa_ref[...], preferred_element_type=F32)
        cp = jnp.dot(conv, wc_ref[...], preferred_element_type=F32)
        ga = gate_ref[:, 0:D_MODEL].astype(F32)
        gc = gate_ref[:, D_MODEL:2 * D_MODEL].astype(F32)
        merged = (_sig(ga) * ap + _sig(gc) * cp).astype(BF16)
        h1_ref[...] = x_ref[...] + jnp.dot(merged, wo_ref[...], preferred_element_type=F32)

    return _pcall(
        body, "mix_fwd", (s // tm,),
        [_rows(tm, D_MODEL), _rows(tm, CBX_W), pl.BlockSpec((HALO, CBX_W), _prev_halo_map(tm)),
         _rows(tm, GATE_W), _rows(tm, ATTN_W), _full((3, CONV_W)), _full((ATTN_W, D_MODEL)),
         _full((CONV_W, D_MODEL)), _full((D_MODEL, D_MODEL))],
        [_rows(tm, D_MODEL)], [_sds((s, D_MODEL), F32)],
        (x, cbx, cbx, gates, attn, conv_w, wa, wc, wout), comm=comm)


def _ffn_up(h1, g, wup_lo, wup_hi, comm):
    s = h1.shape[0]
    tm = _row_tile(s, 512)
    half = D_MODEL // 2

    def body(h_ref, g_ref, wl_ref, wh_ref, hn_ref, up_ref):
        hv = h_ref[...]
        r = lax.rsqrt(jnp.mean(hv * hv, axis=-1, keepdims=True) + NORM_EPS)
        hn = (hv * r * g_ref[...]).astype(BF16)
        hn_ref[...] = hn
        for c in range(2 * D_FF // FF_CHUNK):
            sl = slice(c * FF_CHUNK, (c + 1) * FF_CHUNK)
            acc = lax.dot_general(hn[:, :half], wl_ref[sl, :], NT, preferred_element_type=F32)
            acc = acc + lax.dot_general(hn[:, half:], wh_ref[sl, :], NT, preferred_element_type=F32)
            up_ref[:, sl] = acc.astype(BF16)

    return _pcall(
        body, "ffn_up", (s // tm,),
        [_rows(tm, D_MODEL), _full((1, D_MODEL)), _resident((2 * D_FF, half)), _resident((2 * D_FF, half))],
        [_rows(tm, D_MODEL), _rows(tm, 2 * D_FF)],
        [_sds((s, D_MODEL), BF16), _sds((s, 2 * D_FF), BF16)],
        (h1, g, wup_lo, wup_hi), comm=comm)


def _ffn_conv_cols(up_ref, halo_ref, fcw_ref, first, off):
    u = up_ref[:, off:off + FF_CHUNK].astype(F32)
    uh = jnp.where(first, 0.0, halo_ref[:, off:off + FF_CHUNK].astype(F32))
    w = fcw_ref[:, off:off + FF_CHUNK]
    u1, u2 = _shifts_down(u, uh, (1, 2))
    return w[0:1] * u2 + w[1:2] * u1 + w[2:3] * u


def _ffn_down_loss(up_pre, fcw, wdown, h1, fnorm, target):
    s = h1.shape[0]
    tm = _row_tile(s)

    def body(up_ref, halo_ref, fcw_ref, wd_ref, h1_ref, fn_ref, t_ref, cu_ref, act_ref, dh2_ref, loss_ref, dfn_ref):
        i = pl.program_id(0)

        @pl.when(i == 0)
        def _():
            loss_ref[...] = jnp.zeros_like(loss_ref)
            dfn_ref[...] = jnp.zeros_like(dfn_ref)

        h2 = h1_ref[...]
        for c in range(D_FF // FF_CHUNK):
            gsl = slice(c * FF_CHUNK, (c + 1) * FF_CHUNK)
            vsl = slice(D_FF + c * FF_CHUNK, D_FF + (c + 1) * FF_CHUNK)
            gate = _ffn_conv_cols(up_ref, halo_ref, fcw_ref, i == 0, c * FF_CHUNK)
            cu_ref[:, gsl] = gate.astype(BF16)
            val = _ffn_conv_cols(up_ref, halo_ref, fcw_ref, i == 0, D_FF + c * FF_CHUNK)
            cu_ref[:, vsl] = val.astype(BF16)
            act = (gate * _sig(gate) * val).astype(BF16)
            act_ref[:, gsl] = act
            h2 = h2 + jnp.dot(act, wd_ref[gsl, :], preferred_element_type=F32)
        r = lax.rsqrt(jnp.mean(h2 * h2, axis=-1, keepdims=True) + NORM_EPS)
        yhat = h2 * r
        fn = fn_ref[...]
        diff = yhat * fn - t_ref[...]
        loss_ref[...] += 0.5 * jnp.sum(jnp.sum(diff * diff, axis=1, keepdims=True), axis=0, keepdims=True) / D_MODEL
        dy = diff * (1.0 / D_MODEL)
        dfn_ref[...] += jnp.sum(dy * yhat, axis=0, keepdims=True)
        dyh = dy * fn
        dh2_ref[...] = r * (dyh - yhat * jnp.mean(dyh * yhat, axis=-1, keepdims=True))

    return _pcall(
        body, "ffn_down_loss", (s // tm,),
        [_rows(tm, 2 * D_FF), pl.BlockSpec((HALO, 2 * D_FF), _prev_halo_map(tm)), _full((3, 2 * D_FF)),
         _resident((D_FF, D_MODEL)), _rows(tm, D_MODEL), _full((1, D_MODEL)), _rows(tm, D_MODEL)],
        [_rows(tm, 2 * D_FF), _rows(tm, D_FF), _rows(tm, D_MODEL), _full((1, 128)), _full((1, D_MODEL))],
        [_sds((s, 2 * D_FF), BF16), _sds((s, D_FF), BF16), _sds((s, D_MODEL), F32), _sds((1, 128), F32),
         _sds((1, D_MODEL), F32)],
        (up_pre, up_pre, fcw, wdown, h1, fnorm, target))[0]


def _ffn_bwd(dh2, wdown, up, up_pre, fcw, comm):
    s = dh2.shape[0]
    tm = _row_tile(s)

    def dup_cols(dh, up_ref, wd_ref, c):
        gsl = slice(c * FF_CHUNK, (c + 1) * FF_CHUNK)
        vsl = slice(D_FF + c * FF_CHUNK, D_FF + (c + 1) * FF_CHUNK)
        dact = lax.dot_general(dh, wd_ref[gsl, :], NT, preferred_element_type=F32)
        gate = up_ref[:, gsl].astype(F32)
        val = up_ref[:, vsl].astype(F32)
        sg = _sig(gate)
        return dact * val * (sg * (1.0 + gate * (1.0 - sg))), dact * gate * sg

    def body(dh_ref, dhn_ref, wd_ref, up_ref, upn_ref, x_ref, w_ref, dx_ref, dw_ref):
        i = pl.program_id(0)

        @pl.when(i == 0)
        def _():
            dw_ref[...] = jnp.zeros_like(dw_ref)

        last = i == s // tm - 1
        up1, up2 = _shift_matrix(tm, 1), _shift_matrix(tm, 2)
        dh = dh_ref[...].astype(BF16)
        dhn = dhn_ref[...].astype(BF16)
        for c in range(D_FF // FF_CHUNK):
            halves = zip(dup_cols(dh, up_ref, wd_ref, c), dup_cols(dhn, upn_ref, wd_ref, c),
                         (c * FF_CHUNK, D_FF + c * FF_CHUNK))
            for d, dn, off in halves:
                sl = slice(off, off + FF_CHUNK)
                dn = jnp.where(last, 0.0, dn)
                xv = x_ref[:, sl].astype(F32)
                wv = w_ref[:, sl]
                db = d.astype(BF16)
                d1, d2 = _mxu_shift_up(up1, db, dn, 1), _mxu_shift_up(up2, db, dn, 2)
                dx_ref[:, sl] = (wv[2:3] * d + wv[1:2] * d1 + wv[0:1] * d2).astype(BF16)
                dw_ref[0:1, sl] += jnp.sum(d2 * xv, axis=0, keepdims=True)
                dw_ref[1:2, sl] += jnp.sum(d1 * xv, axis=0, keepdims=True)
                dw_ref[2:3, sl] += jnp.sum(d * xv, axis=0, keepdims=True)

    return _pcall(
        body, "ffn_bwd", (s // tm,),
        [_rows(tm, D_MODEL), pl.BlockSpec((HALO, D_MODEL), _next_halo_map(tm, s)), _resident((D_FF, D_MODEL)),
         _rows(tm, 2 * D_FF), pl.BlockSpec((HALO, 2 * D_FF), _next_halo_map(tm, s)), _rows(tm, 2 * D_FF),
         _full((3, 2 * D_FF))],
        [_rows(tm, 2 * D_FF), _full((3, 2 * D_FF))],
        [_sds((s, 2 * D_FF), BF16), _sds((3, 2 * D_FF), F32)],
        (dh2, dh2, wdown, up, up, up_pre, fcw), comm=comm)


def _matmul_tn(a, b, tk, name, ts=1024, comm=None):
    s, ka = a.shape
    n = b.shape[1]
    ts = min(ts, s)
    steps = s // ts

    def body(a_ref, b_ref, o_ref, acc_ref):
        j = pl.program_id(1)

        @pl.when(j == 0)
        def _():
            acc_ref[...] = jnp.zeros_like(acc_ref)

        acc_ref[...] += lax.dot_general(a_ref[...].astype(BF16), b_ref[...].astype(BF16), TN,
                                        preferred_element_type=F32)

        @pl.when(j == steps - 1)
        def _():
            o_ref[...] = acc_ref[...].astype(BF16)

    outs, couts = _pcall(
        body, name, (ka // tk, steps),
        [pl.BlockSpec((ts, tk), lambda i, j: (j, i)), pl.BlockSpec((ts, n), lambda i, j: (j, 0))],
        [pl.BlockSpec((tk, n), lambda i, j: (i, 0))], [_sds((ka, n), BF16)],
        (a, b), scratch=[pltpu.VMEM((tk, n), F32)], comm=comm)
    return outs[0] if comm is None else (outs[0], couts)


def _norm_bwd_tile(xv, g, dy):
    r = lax.rsqrt(jnp.mean(xv * xv, axis=-1, keepdims=True) + NORM_EPS)
    xhat = xv * r
    dg = jnp.sum(dy * xhat, axis=0, keepdims=True)
    dyh = dy * g
    return r * (dyh - xhat * jnp.mean(dyh * xhat, axis=-1, keepdims=True)), dg


def _ffn_up_bwd(dup_pre, wup_lo, wup_hi, h1, g, dh2, comm):
    s = h1.shape[0]
    tm = _row_tile(s, 512)
    half = D_MODEL // 2

    def body(du_ref, wl_ref, wh_ref, h_ref, g_ref, dh2_ref, dh1_ref, dg_ref):
        @pl.when(pl.program_id(0) == 0)
        def _():
            dg_ref[...] = jnp.zeros_like(dg_ref)

        du = du_ref[...]
        dhn = jnp.concatenate([jnp.dot(du, wl_ref[...], preferred_element_type=F32),
                               jnp.dot(du, wh_ref[...], preferred_element_type=F32)], axis=1)
        dx, dg = _norm_bwd_tile(h_ref[...], g_ref[...], dhn)
        dg_ref[...] += dg
        dh1_ref[...] = dh2_ref[...] + dx

    return _pcall(
        body, "ffn_up_bwd", (s // tm,),
        [_rows(tm, 2 * D_FF), _resident((2 * D_FF, half)), _resident((2 * D_FF, half)), _rows(tm, D_MODEL),
         _full((1, D_MODEL)), _rows(tm, D_MODEL)],
        [_rows(tm, D_MODEL), _full((1, D_MODEL))],
        [_sds((s, D_MODEL), F32), _sds((1, D_MODEL), F32)],
        (dup_pre, wup_lo, wup_hi, h1, g, dh2), comm=comm)


def _mix_bwd(dh1, wout, gates, attn, wa, wc, cbx, conv_w, comm):
    s = dh1.shape[0]
    tm = _row_tile(s)
    steps = s // tm

    def body(dh_ref, wo_ref, gate_ref, attn_ref, wa_ref, wc_ref, cbx_ref, halo_ref, cw_ref,
             dg_ref, dattn_ref, dcb_ref, dcv_ref, gwo_ref, gwa_ref, gwc_ref, acc_o, acc_a, acc_c):
        i = pl.program_id(0)

        @pl.when(i == 0)
        def _():
            acc_o[...] = jnp.zeros_like(acc_o)
            acc_a[...] = jnp.zeros_like(acc_a)
            acc_c[...] = jnp.zeros_like(acc_c)

        cb, _, _, _, cv = _conv_u(cbx_ref, halo_ref, cw_ref, i == 0)
        attn = attn_ref[...]
        conv = (cb * cv).astype(BF16)
        ap = jnp.dot(attn, wa_ref[...], preferred_element_type=F32)
        cp = jnp.dot(conv, wc_ref[...], preferred_element_type=F32)
        dhb = dh_ref[...].astype(BF16)
        dm = lax.dot_general(dhb, wo_ref[...], NT, preferred_element_type=F32)
        sa = _sig(gate_ref[:, 0:D_MODEL].astype(F32))
        sc = _sig(gate_ref[:, D_MODEL:2 * D_MODEL].astype(F32))
        merged = (sa * ap + sc * cp).astype(BF16)
        da = (dm * sa).astype(BF16)
        dc = (dm * sc).astype(BF16)
        dg_ref[:, 0:D_MODEL] = (dm * ap * sa * (1.0 - sa)).astype(BF16)
        dg_ref[:, D_MODEL:2 * D_MODEL] = (dm * cp * sc * (1.0 - sc)).astype(BF16)
        dattn_ref[...] = lax.dot_general(da, wa_ref[...], NT, preferred_element_type=F32).astype(BF16)
        dconv = lax.dot_general(dc, wc_ref[...], NT, preferred_element_type=F32)
        dcb_ref[...] = (dconv * cv).astype(BF16)
        dcv_ref[...] = (dconv * cb).astype(BF16)
        acc_o[...] += lax.dot_general(merged, dhb, TN, preferred_element_type=F32)
        acc_a[...] += lax.dot_general(attn, da, TN, preferred_element_type=F32)
        acc_c[...] += lax.dot_general(conv, dc, TN, preferred_element_type=F32)

        @pl.when(i == steps - 1)
        def _():
            gwo_ref[...] = acc_o[...].astype(BF16)
            gwa_ref[...] = acc_a[...].astype(BF16)
            gwc_ref[...] = acc_c[...].astype(BF16)

    return _pcall(
        body, "mix_bwd", (steps,),
        [_rows(tm, D_MODEL), _full((D_MODEL, D_MODEL)), _rows(tm, GATE_W), _rows(tm, ATTN_W),
         _full((ATTN_W, D_MODEL)), _full((CONV_W, D_MODEL)), _rows(tm, CBX_W),
         pl.BlockSpec((HALO, CBX_W), _prev_halo_map(tm)), _full((3, CONV_W))],
        [_rows(tm, GATE_W), _rows(tm, ATTN_W), _rows(tm, CONV_W), _rows(tm, CONV_W),
         _full((D_MODEL, D_MODEL)), _full((ATTN_W, D_MODEL)), _full((CONV_W, D_MODEL))],
        [_sds((s, GATE_W), BF16), _sds((s, ATTN_W), BF16), _sds((s, CONV_W), BF16), _sds((s, CONV_W), BF16),
         _sds((D_MODEL, D_MODEL), BF16), _sds((ATTN_W, D_MODEL), BF16), _sds((CONV_W, D_MODEL), BF16)],
        (dh1, wout, gates, attn, wa, wc, cbx, cbx, conv_w),
        scratch=[pltpu.VMEM((D_MODEL, D_MODEL), F32), pltpu.VMEM((ATTN_W, D_MODEL), F32),
                 pltpu.VMEM((CONV_W, D_MODEL), F32)], comm=comm)


def _conv_branch_bwd(dcv, cbx, conv_w):
    s = dcv.shape[0]
    tm = _row_tile(s)

    def body(d_ref, dn_ref, cbx_ref, w_ref, dcc_ref, dcx_ref, dw_ref):
        i = pl.program_id(0)

        @pl.when(i == 0)
        def _():
            dw_ref[...] = jnp.zeros_like(dw_ref)

        last = i == s // tm - 1
        cc = cbx_ref[:, CONV_W:2 * CONV_W].astype(F32)
        cx = cbx_ref[:, 2 * CONV_W:3 * CONV_W].astype(F32)
        u = cc * cx
        d = d_ref[...].astype(F32)
        dn = jnp.where(last, 0.0, dn_ref[...].astype(F32))
        d1, d2 = _shifts_up(d, dn, (1, 2))
        du = w_ref[2:3, :] * d + w_ref[1:2, :] * d1 + w_ref[0:1, :] * d2
        dcc_ref[...] = (du * cx).astype(BF16)
        dcx_ref[...] = (du * cc).astype(BF16)
        dw_ref[0:1, :] += jnp.sum(d2 * u, axis=0, keepdims=True)
        dw_ref[1:2, :] += jnp.sum(d1 * u, axis=0, keepdims=True)
        dw_ref[2:3, :] += jnp.sum(d * u, axis=0, keepdims=True)

    return _pcall(
        body, "conv_branch_bwd", (s // tm,),
        [_rows(tm, CONV_W), pl.BlockSpec((HALO, CONV_W), _next_halo_map(tm, s)), _rows(tm, CBX_W),
         _full((3, CONV_W))],
        [_rows(tm, CONV_W), _rows(tm, CONV_W), _full((3, CONV_W))],
        [_sds((s, CONV_W), BF16), _sds((s, CONV_W), BF16), _sds((3, CONV_W), F32)],
        (dcv, dcv, cbx, conv_w))[0]


def _attn_bwd(qkv, sinks, attn, lse, dattn, comm):
    s = qkv.shape[0]

    def body(sinks_ref, q_ref, kp_ref, kc_ref, vp_ref, vc_ref, o_ref, lse_ref, do_ref,
             dq_ref, dk_ref, dv_ref, ds_ref):
        n = pl.program_id(0)

        @pl.when(n == 0)
        def _():
            dk_ref[...] = jnp.zeros_like(dk_ref)
            dv_ref[...] = jnp.zeros_like(dv_ref)
            ds_ref[...] = jnp.zeros_like(ds_ref)

        mask = _attn_mask(n)
        lower = _lower_lanes()
        lane = lax.broadcasted_iota(jnp.int32, (BLOCK, 128), 1)
        lower2 = lax.broadcasted_iota(jnp.int32, (2 * BLOCK, 128), 1) < HEAD_DIM
        lane1 = lax.broadcasted_iota(jnp.int32, (1, 128), 1)
        qv, ov, dov, lsev = q_ref[...], o_ref[...], do_ref[...], lse_ref[...]
        dk_fold, dv_fold = [], []
        dsink = jnp.zeros((1, 128), F32)
        for kh in range(2):
            qs = _stack_heads(qv, kh)
            dos = _stack_heads(dov, kh)
            os_ = _stack_heads(ov, kh)
            kd, vd = _dup_kv(kp_ref, kc_ref, kh), _dup_kv(vp_ref, vc_ref, kh)
            lse = jnp.concatenate(
                [jnp.sum(jnp.where(lane == kh * 4 + g, lsev, 0.0), axis=1, keepdims=True) for g in range(4)], axis=0)
            sc = lax.dot_general(qs, kd, NT, preferred_element_type=F32) * ATTN_SCALE
            p = jnp.exp(jnp.where(mask, sc, NEG) - lse)
            dp = lax.dot_general(dos, vd, NT, preferred_element_type=F32)
            delta = jnp.sum(dos.astype(F32) * os_.astype(F32), axis=1, keepdims=True)
            dsc = (p * (dp - delta) * ATTN_SCALE).astype(BF16)
            dqs = jnp.dot(dsc, kd, preferred_element_type=F32)
            for pair in range(2):
                lo = dqs[(2 * pair) * BLOCK:(2 * pair + 1) * BLOCK]
                hi = dqs[(2 * pair + 1) * BLOCK:(2 * pair + 2) * BLOCK]
                col = (kh * 2 + pair) * 128
                dq_ref[:, col:col + 128] = jnp.where(lower, lo, hi).astype(BF16)
            dkd = lax.dot_general(dsc, qs, TN, preferred_element_type=F32)
            dvd = lax.dot_general(p.astype(BF16), dos, TN, preferred_element_type=F32)
            dk_fold.append(dkd + pltpu.roll(dkd, HEAD_DIM, axis=1))
            dv_fold.append(dvd + pltpu.roll(dvd, HEAD_DIM, axis=1))
            psink = jnp.exp(_sink_col(sinks_ref, kh) - lse) * delta
            for g in range(4):
                tot = jnp.sum(psink[g * BLOCK:(g + 1) * BLOCK], axis=0, keepdims=True)
                dsink = dsink - jnp.where(lane1 == kh * 4 + g, tot, 0.0)
        dk2 = jnp.where(lower2, dk_fold[0], dk_fold[1])
        dv2 = jnp.where(lower2, dv_fold[0], dv_fold[1])
        ds_ref[...] += dsink
        cur = pl.ds(pl.multiple_of(n * BLOCK, BLOCK), BLOCK)
        dk_ref[cur, :] += dk2[BLOCK:]
        dv_ref[cur, :] += dv2[BLOCK:]

        @pl.when(n > 0)
        def _():
            prev = pl.ds(pl.multiple_of((n - 1) * BLOCK, BLOCK), BLOCK)
            dk_ref[prev, :] += dk2[:BLOCK]
            dv_ref[prev, :] += dv2[:BLOCK]

    blk = lambda w: pl.BlockSpec((BLOCK, w), lambda n: (n, 0))
    return _pcall(
        body, "attn_bwd", (s // BLOCK,),
        [pl.BlockSpec(memory_space=pltpu.SMEM)] + _attn_specs() + [blk(ATTN_W), blk(128), blk(ATTN_W)],
        [blk(ATTN_W), _full((s, KV_W)), _full((s, KV_W)), _full((1, 128))],
        [_sds((s, ATTN_W), BF16), _sds((s, KV_W), F32), _sds((s, KV_W), F32), _sds((1, 128), F32)],
        (sinks, qkv, qkv, qkv, qkv, qkv, attn, lse, dattn), comm=comm)


DPROJ_PIECES = (ATTN_W, KV_W, KV_W, CONV_W, CONV_W, CONV_W, GATE_W)
DPROJ_OFFSETS = tuple(sum(DPROJ_PIECES[:k]) for k in range(len(DPROJ_PIECES)))


def _grad_w_in(pieces, xn, comm):
    s = xn.shape[0]
    ts = min(1024, s)
    steps = s // ts
    rows0 = DPROJ_OFFSETS[6]

    def body(*refs):
        p_refs, b_ref, o_ref, acc_ref, stage_ref, sem = refs[:7], refs[7], refs[8], refs[9], refs[10], refs[11]
        i, j = pl.program_id(0), pl.program_id(1)

        @pl.when(j == 0)
        def _():
            acc_ref[...] = jnp.zeros_like(acc_ref)

        bv = b_ref[...]

        def flush(lo, n):
            stage_ref[0:n, :] = acc_ref[0:n, :].astype(BF16)
            cp = pltpu.make_async_copy(stage_ref.at[0:n, :], o_ref.at[lo:lo + n, :], sem)
            cp.start()
            cp.wait()

        @pl.when(i == 0)
        def _():
            for p_ref, off, w in zip(p_refs[:6], DPROJ_OFFSETS[:6], DPROJ_PIECES[:6]):
                acc_ref[off:off + w, :] += lax.dot_general(p_ref[...].astype(BF16), bv, TN,
                                                           preferred_element_type=F32)

            @pl.when(j == steps - 1)
            def _():
                flush(0, rows0)

        @pl.when(i == 1)
        def _():
            acc_ref[0:GATE_W, :] += lax.dot_general(p_refs[6][...], bv, TN, preferred_element_type=F32)

            @pl.when(j == steps - 1)
            def _():
                flush(rows0, GATE_W)

    def piece_spec(w, group):
        return pl.BlockSpec((ts, w), lambda i, j: (jnp.where(i == group, j, 0), 0))

    outs, couts = _pcall(
        body, "grad_w_in", (2, steps),
        [piece_spec(w, 0) for w in DPROJ_PIECES[:6]] + [piece_spec(GATE_W, 1),
                                                         pl.BlockSpec((ts, D_MODEL), lambda i, j: (j, 0))],
        [ANY], [_sds((IN_W, D_MODEL), BF16)], (*pieces, xn),
        scratch=[pltpu.VMEM((rows0, D_MODEL), F32), pltpu.VMEM((rows0, D_MODEL), BF16), pltpu.SemaphoreType.DMA],
        comm=comm)
    return outs[0], couts


def _inproj_bwd(pieces, win_t, x, g, dh1, comm):
    s = x.shape[0]
    tm = _row_tile(s, 512)

    def body(*refs):
        p_refs = refs[:7]
        w_ref, x_ref, g_ref, dh_ref, dx_ref, db_ref, dg_ref = refs[7:]

        @pl.when(pl.program_id(0) == 0)
        def _():
            db_ref[...] = jnp.zeros_like(db_ref)
            dg_ref[...] = jnp.zeros_like(dg_ref)

        dxn = jnp.zeros((tm, D_MODEL), F32)
        for p_ref, off, w in zip(p_refs, DPROJ_OFFSETS, DPROJ_PIECES):
            v = p_ref[...].astype(BF16)
            db_ref[:, off:off + w] += jnp.sum(v.astype(F32), axis=0, keepdims=True)
            dxn = dxn + jnp.dot(v, w_ref[off:off + w, :], preferred_element_type=F32)
        dx, dg = _norm_bwd_tile(x_ref[...], g_ref[...], dxn)
        dg_ref[...] += dg
        dx_ref[...] = dh_ref[...] + dx

    return _pcall(
        body, "inproj_bwd", (s // tm,),
        [_rows(tm, w) for w in DPROJ_PIECES] + [_resident((IN_W, D_MODEL)), _rows(tm, D_MODEL), _full((1, D_MODEL)),
                                                _rows(tm, D_MODEL)],
        [_rows(tm, D_MODEL), _full((1, IN_W)), _full((1, D_MODEL))],
        [_sds((s, D_MODEL), F32), _sds((1, IN_W), F32), _sds((1, D_MODEL), F32)],
        (*pieces, win_t, x, g, dh1), comm=comm)


def _adam_math(w, g, m, v):
    m2 = ADAM_B1 * m + (1.0 - ADAM_B1) * g
    v2 = ADAM_B2 * v + (1.0 - ADAM_B2) * (g * g)
    m_hat = m2 / (1.0 - ADAM_B1 ** ADAM_STEP)
    v_hat = v2 / (1.0 - ADAM_B2 ** ADAM_STEP)
    delta = -ADAM_LR * (m_hat / (jnp.sqrt(v_hat) + ADAM_EPS) + ADAM_WD * w)
    return delta, m2, v2


def _sum_slots(ref):
    tot = ref[0].astype(F32)
    for i in range(1, ref.shape[0]):
        tot = tot + ref[i].astype(F32)
    return tot


def _pair_add(mine, theirs, tr, name):
    r, c = mine.shape

    def body(a_ref, b_ref, o_ref):
        o_ref[...] = (a_ref[...].astype(F32) + b_ref[...].astype(F32)).astype(BF16)

    spec = pl.BlockSpec((tr, c), lambda i: (i, 0))
    return _pcall(body, name, (r // tr,), [spec, spec], [spec], [_sds((r, c), BF16)], (mine, theirs))[0][0]


def _sum_adamw(parts, w, m, v, tr, name):
    r, c = w.shape

    def body(p_ref, w_ref, m_ref, v_ref, g_ref, d_ref, m2_ref, v2_ref):
        g = _sum_slots(p_ref)
        g_ref[...] = g
        d_ref[...], m2_ref[...], v2_ref[...] = _adam_math(w_ref[...], g, m_ref[...], v_ref[...])

    spec = pl.BlockSpec((tr, c), lambda i: (i, 0))
    return _pcall(body, name, (r // tr,), [pl.BlockSpec((N_DEV, tr, c), lambda i: (0, i, 0)), spec, spec, spec],
                  [spec] * 4, [_sds((r, c), F32)] * 4, (parts, w, m, v))[0]


def _sum_parts_adamw(parts, w, m, v, tr, name):
    c = w.shape[1]
    tiles = [p.shape[1] // tr for p in parts]
    starts = [sum(tiles[:k]) for k in range(len(parts))]
    n_parts = len(parts)

    def body(*refs):
        p_refs = refs[:n_parts]
        w_ref, m_ref, v_ref, g_ref, d_ref, m2_ref, v2_ref = refs[n_parts:]
        i = pl.program_id(0)
        for p_ref, st, nt in zip(p_refs, starts, tiles):
            @pl.when(jnp.logical_and(i >= st, i < st + nt))
            def _(p_ref=p_ref):
                g_ref[...] = _sum_slots(p_ref)

        d_ref[...], m2_ref[...], v2_ref[...] = _adam_math(w_ref[...], g_ref[...], m_ref[...], v_ref[...])

    def part_spec(p, st, nt):
        return pl.BlockSpec((p.shape[0], tr, c), lambda i: (0, jnp.clip(i - st, 0, nt - 1), 0))

    spec = pl.BlockSpec((tr, c), lambda i: (i, 0))
    return _pcall(
        body, name, (sum(tiles),),
        [part_spec(p, st, nt) for p, st, nt in zip(parts, starts, tiles)] + [spec, spec, spec],
        [spec] * 4, [_sds(w.shape, F32)] * 4, (*parts, w, m, v))[0]


ROW_MIX, ROW_FFN, ROW_FINAL, ROW_SINKS, ROW_LOSS, ROW_BIN, ROW_CW, ROW_FCW = 0, 1, 2, 3, 4, 5, 10, 13
FCW_ROWS = 6


def _wide_pieces(width):
    return [(k * D_MODEL, min(D_MODEL, width - k * D_MODEL)) for k in range(-(-width // D_MODEL))]


def _pack_small(dmix, dffn, dfn, dsink, loss, dbin, dcw, dfcw):
    def body(mix_ref, ffn_ref, fn_ref, sink_ref, loss_ref, bin_ref, cw_ref, fcw_ref, o_ref):
        o_ref[...] = jnp.zeros_like(o_ref)
        o_ref[ROW_MIX:ROW_MIX + 1, :] = mix_ref[...]
        o_ref[ROW_FFN:ROW_FFN + 1, :] = ffn_ref[...]
        o_ref[ROW_FINAL:ROW_FINAL + 1, :] = fn_ref[...]
        o_ref[ROW_SINKS:ROW_SINKS + 1, 0:128] = sink_ref[...]
        o_ref[ROW_LOSS:ROW_LOSS + 1, 0:128] = loss_ref[...]
        for k, (off, w) in enumerate(_wide_pieces(IN_W)):
            o_ref[ROW_BIN + k:ROW_BIN + k + 1, 0:w] = bin_ref[:, off:off + w]
        o_ref[ROW_CW:ROW_CW + 3, 0:CONV_W] = cw_ref[...]
        for a in range(3):
            for k, (off, w) in enumerate(_wide_pieces(2 * D_FF)):
                row = ROW_FCW + FCW_ROWS * a + k
                o_ref[row:row + 1, 0:w] = fcw_ref[a:a + 1, off:off + w]

    return pl.pallas_call(body, name="pack_small", out_shape=_sds((SMALL_ROWS, D_MODEL), F32))(
        dmix, dffn, dfn, dsink, loss, dbin, dcw, dfcw)


def _small_sums_adamw(r_small, params):
    rows = (ROW_MIX, ROW_BIN, ROW_SINKS, ROW_FFN, ROW_FINAL)

    def body(*refs):
        r_ref, p_refs, o_refs = refs[0], refs[1:16], refs[16:]
        tot = _sum_slots(r_ref)
        for k, row in enumerate(rows):
            w_ref, m_ref, v_ref = p_refs[3 * k:3 * k + 3]
            g_ref, d_ref, m2_ref, v2_ref = o_refs[4 * k:4 * k + 4]
            for j, (off, w) in enumerate(_wide_pieces(w_ref.shape[1])):
                g_ref[:, off:off + w] = tot[row + j:row + j + 1, 0:w]
            d_ref[...], m2_ref[...], v2_ref[...] = _adam_math(w_ref[...], g_ref[...], m_ref[...], v_ref[...])
        cw_ref, fcw_ref, loss_ref = o_refs[20:]
        cw_ref[...] = tot[ROW_CW:ROW_CW + 3, 0:CONV_W]
        for a in range(3):
            for j, (off, w) in enumerate(_wide_pieces(2 * D_FF)):
                row = ROW_FCW + FCW_ROWS * a + j
                fcw_ref[a:a + 1, off:off + w] = tot[row:row + 1, 0:w]
        loss_ref[...] = tot[ROW_LOSS:ROW_LOSS + 1, 0:128]

    flat = [t for p in params for t in p]
    out_shape = [_sds(p[0].shape, F32) for p in params for _ in range(4)]
    out_shape += [_sds((3, CONV_W), F32), _sds((3, 2 * D_FF), F32), _sds((1, 128), F32)]
    res = pl.pallas_call(body, name="small_sums_adamw", out_shape=out_shape)(r_small, *flat)
    return [tuple(res[4 * k:4 * k + 4]) for k in range(5)], res[20], res[21], res[22]


def _adamw_pair(a, b):
    def body(*refs):
        for k in range(2):
            w_ref, g_ref, m_ref, v_ref = refs[4 * k:4 * k + 4]
            d_ref, m2_ref, v2_ref = refs[8 + 3 * k:8 + 3 * k + 3]
            d_ref[...], m2_ref[...], v2_ref[...] = _adam_math(w_ref[...], g_ref[...], m_ref[...], v_ref[...])

    out_shape = [_sds(a[0].shape, F32)] * 3 + [_sds(b[0].shape, F32)] * 3
    res = pl.pallas_call(body, name="adamw_conv_weights", out_shape=out_shape)(*a, *b)
    return tuple(res[:3]), tuple(res[3:])


def _pad_cols(a, c):
    return jnp.pad(a, ((0, 0), (0, c - a.shape[1])))


def _to_col_slabs(g):
    r = g.shape[0]
    return jnp.transpose(g.reshape(r, N_DEV, 128), (1, 0, 2)).reshape(N_DEV * r, 128)


def _from_col_slabs(t):
    r = t.shape[0] // N_DEV
    return jnp.transpose(t.reshape(N_DEV, r, 128), (1, 0, 2)).reshape(r, N_DEV * 128)


def _slots(t):
    return t.reshape(N_DEV, t.shape[0] // N_DEV, t.shape[1])


def kernel(x, mix_norm, w_in, b_in, sinks, conv_w, w_attn_branch, w_conv_branch, w_out, ffn_norm, w_up, ffn_conv_w, w_down, final_norm, loss_target, m_mix_norm, m_w_in, m_b_in, m_sinks, m_conv_w, m_w_attn_branch, m_w_conv_branch, m_w_out, m_ffn_norm, m_w_up, m_ffn_conv_w, m_w_down, m_final_norm, v_mix_norm, v_w_in, v_b_in, v_sinks, v_conv_w, v_w_attn_branch, v_w_conv_branch, v_w_out, v_ffn_norm, v_w_up, v_ffn_conv_w, v_w_down, v_final_norm):
    xs, tgt = x[0], loss_target[0]
    me = 4 * lax.axis_index("x") + 2 * lax.axis_index("y") + lax.axis_index("c")
    in_rows, up_rows = IN_W // N_DEV, 2 * D_FF // N_DEV

    conv_sh = jnp.concatenate([_pad_cols(ffn_conv_w[0], 768), _pad_cols(conv_w[0], 768),
                               jnp.zeros((2, 768), F32)], axis=0)
    win_sh, wup_sh = w_in[0].T.astype(BF16), w_up[0].T.astype(BF16)
    wout_sh, wdown_sh = w_out[0].astype(BF16), w_down[0].astype(BF16)
    wa_sh, wc_sh = w_attn_branch[0].astype(BF16), w_conv_branch[0].astype(BF16)

    half = D_MODEL // 2
    (win_t,) = _exchange_only(_AllGather([win_sh]), "gather_w_in")
    (xn, qkv, cbx, gates), (wa_s, wc_s, wout, conv_g) = _norm_inproj(
        xs, mix_norm, win_t, b_in, _AllGather([wa_sh, wc_sh, wout_sh, conv_sh]))
    (attn, lse), (wup_lo,) = _attn_fwd(qkv, sinks, _AllGather([wup_sh[:, :half]]))
    wa, wc = _from_col_slabs(wa_s), _from_col_slabs(wc_s)
    conv_g = conv_g.reshape(N_DEV, 8, 768)
    fcw = jnp.transpose(conv_g[:, 0:3, :up_rows], (1, 0, 2)).reshape(3, 2 * D_FF)
    cw = jnp.transpose(conv_g[:, 3:6, :CONV_W // N_DEV], (1, 0, 2)).reshape(3, CONV_W)
    (h1,), (wup_hi,) = _mix_fwd(xs, cbx, gates, attn, cw, wa, wc, wout, _AllGather([wup_sh[:, half:]]))
    (hn, up_pre), (wdown,) = _ffn_up(h1, ffn_norm, wup_lo, wup_hi, _AllGather([wdown_sh]))
    up, act, dh2, loss_p, dfn_p = _ffn_down_loss(up_pre, fcw, wdown, h1, final_norm.reshape(1, D_MODEL), tgt)

    dn_rows, q_up = D_FF // N_DEV, up_rows // 4
    g_wdown = _matmul_tn(act, dh2, FF_CHUNK, "grad_w_down")
    (dup_pre, dfcw_p), (r_wdown,) = _ffn_bwd(dh2, wdown, up, up_pre, fcw, _ReduceScatter([(g_wdown, 0, dn_rows)]))
    g_wup_t = _matmul_tn(dup_pre, hn, FF_CHUNK, "grad_w_up")
    (dh1, dffn_p), (r_wup_a,) = _ffn_up_bwd(dup_pre, wup_lo, wup_hi, h1, ffn_norm, dh2,
                                            _ReduceScatter([(g_wup_t, 0, q_up)]))
    (dgates, dattn, dcb, dcv, g_wout, g_wa_nat, g_wc_nat), (r_wup_b,) = _mix_bwd(
        dh1, wout, gates, attn, wa, wc, cbx, cw, _ReduceScatter([(g_wup_t, q_up, q_up)]))
    g_wa, g_wc = _to_col_slabs(g_wa_nat), _to_col_slabs(g_wc_nat)
    dcc, dcx, dcw_p = _conv_branch_bwd(dcv, cbx, cw)
    (dq, dk, dv, dsink_p), (r_wup_c, r_wout, r_wa, r_wc) = _attn_bwd(
        qkv, sinks, attn, lse, dattn,
        _ReduceScatter([(g_wup_t, 2 * q_up, q_up), (g_wout, 0, D_MODEL // N_DEV), (g_wa, 0, ATTN_W),
                        (g_wc, 0, CONV_W)]))
    dproj = (dq, dk, dv, dcb, dcc, dcx, dgates)
    g_win_t, (r_wup_d,) = _grad_w_in(dproj, xn, _ReduceScatter([(g_wup_t, 3 * q_up, q_up)]))
    (win_theirs,) = _exchange_only(_PairExchange([g_win_t]), "pair_exchange_w_in")
    win_mine = lax.dynamic_index_in_dim(g_win_t.reshape(4, 2, in_rows, D_MODEL), lax.axis_index("c"), axis=1,
                                        keepdims=False).reshape(4 * in_rows, D_MODEL)
    q_win = _pair_add(win_mine, win_theirs, in_rows // 2, "pair_add_w_in")
    (dx, dbin_p, dmix_p), (r_win,) = _inproj_bwd(dproj, win_t, xs, mix_norm, dh1, _ChipExchange([q_win]))

    small = _pack_small(dmix_p, dffn_p, dfn_p, dsink_p, loss_p, dbin_p, dcw_p, dfcw_p)
    (r_small,) = _exchange_only(_ReduceScatter([], [small]), "exchange_small")

    fn2, m_fn2, v_fn2 = (t.reshape(1, D_MODEL) for t in (final_norm, m_final_norm, v_final_norm))
    small_res, g_cw_full, g_fcw_full, loss_row = _small_sums_adamw(
        _slots(r_small), [(mix_norm, m_mix_norm, v_mix_norm), (b_in, m_b_in, v_b_in), (sinks, m_sinks, v_sinks),
                          (ffn_norm, m_ffn_norm, v_ffn_norm), (fn2, m_fn2, v_fn2)])
    loss = loss_row[0, 0]
    g_cw = lax.dynamic_slice_in_dim(g_cw_full, me * (CONV_W // N_DEV), CONV_W // N_DEV, axis=1)
    g_fcw = lax.dynamic_slice_in_dim(g_fcw_full, me * up_rows, up_rows, axis=1)
    cw_res, fcw_res = _adamw_pair((conv_w[0], g_cw, m_conv_w[0], v_conv_w[0]),
                                  (ffn_conv_w[0], g_fcw, m_ffn_conv_w[0], v_ffn_conv_w[0]))

    big = {}
    big["w_in"] = tuple(t.T for t in _sum_parts_adamw(
        [r_win.reshape(4, in_rows, D_MODEL)], w_in[0].T, m_w_in[0].T, v_w_in[0].T, in_rows // 2, "adamw_w_in"))
    big["w_up"] = tuple(t.T for t in _sum_parts_adamw(
        [_slots(r_wup_a), _slots(r_wup_b), _slots(r_wup_c), _slots(r_wup_d)], w_up[0].T, m_w_up[0].T, v_w_up[0].T, q_up,
        "adamw_w_up"))
    big["w_out"] = _sum_adamw(_slots(r_wout), w_out[0], m_w_out[0], v_w_out[0], 128, "adamw_w_out")
    big["w_down"] = _sum_adamw(_slots(r_wdown), w_down[0], m_w_down[0], v_w_down[0], dn_rows // 2, "adamw_w_down")
    big["w_attn_branch"] = _sum_adamw(_slots(r_wa), w_attn_branch[0], m_w_attn_branch[0], v_w_attn_branch[0], 256,
                                      "adamw_w_attn_branch")
    big["w_conv_branch"] = _sum_adamw(_slots(r_wc), w_conv_branch[0], m_w_conv_branch[0], v_w_conv_branch[0], 256,
                                      "adamw_w_conv_branch")

    res = dict(zip(("mix_norm", "b_in", "sinks", "ffn_norm"), small_res[:4]))
    res["final_norm"] = tuple(t.reshape(final_norm.shape) for t in small_res[4])
    res["conv_w"] = tuple(t.reshape(conv_w.shape) for t in (g_cw,) + cw_res)
    res["ffn_conv_w"] = tuple(t.reshape(ffn_conv_w.shape) for t in (g_fcw,) + fcw_res)
    for name, ref_w in (("w_in", w_in), ("w_up", w_up), ("w_out", w_out), ("w_down", w_down),
                        ("w_attn_branch", w_attn_branch), ("w_conv_branch", w_conv_branch)):
        res[name] = tuple(t.reshape(ref_w.shape) for t in big[name])

    order = ["mix_norm", "w_in", "b_in", "sinks", "conv_w", "w_attn_branch", "w_conv_branch", "w_out",
             "ffn_norm", "w_up", "ffn_conv_w", "w_down", "final_norm"]
    out = [loss, dx.reshape(x.shape)]
    for k in range(4):
        out += [res[name][k] for name in order]
    return tuple(out)
```

```python
import math

import jax
import jax.numpy as jnp
from jax import lax
from jax.experimental import pallas as pl
from jax.experimental.pallas import tpu as pltpu

F32 = jnp.float32
BF16 = jnp.bfloat16
MESH = pl.DeviceIdType.MESH
N_DEV = 8

D_MODEL = 1024
HEAD_DIM = 64
N_HEADS = 8
BLOCK = 128
ATTN_W = 512
KV_W = 128
CONV_W = 512
QKV_W = ATTN_W + 2 * KV_W
CBX_W = 3 * CONV_W
GATE_W = 2 * D_MODEL
IN_W = QKV_W + CBX_W + GATE_W
D_FF = 2816
FF_CHUNK = 1408
NORM_EPS = 1e-5
ATTN_SCALE = HEAD_DIM ** -0.5
NEG = -1e30
HALO = 16

ADAM_LR = 0.001
ADAM_B1 = 0.9
ADAM_B2 = 0.999
ADAM_EPS = 1e-08
ADAM_WD = 0.01
ADAM_STEP = 10

VMEM_LIMIT = 56 * 1024 * 1024
SMALL_ROWS = 32

NT = (((1,), (1,)), ((), ()))
TN = (((0,), (0,)), ((), ()))
ANY = pl.BlockSpec(memory_space=pl.ANY)


def _sig(v):
    return 1.0 / (1.0 + jnp.exp(-v))


def _row_tile(s, pref=256):
    return pref if s % pref == 0 else s


def _shifts_down(u, halo, ks):
    ext = jnp.concatenate([halo, u], axis=0)
    return [pltpu.roll(ext, k, axis=0)[HALO:, :] for k in ks]


def _shifts_up(u, halo, ks):
    n = u.shape[0]
    ext = jnp.concatenate([u, halo], axis=0)
    return [pltpu.roll(ext, n + HALO - k, axis=0)[:n, :] for k in ks]


def _shift_matrix(n, k):
    row = lax.broadcasted_iota(jnp.int32, (n, n), 0)
    col = lax.broadcasted_iota(jnp.int32, (n, n), 1)
    return jnp.where(col == row + k, 1.0, 0.0).astype(BF16)


def _mxu_shift_up(mat, ub, halo, k):
    n = ub.shape[0]
    v = jnp.dot(mat, ub, preferred_element_type=F32)
    row = lax.broadcasted_iota(jnp.int32, (8, ub.shape[1]), 0)
    tail = v[n - 8:, :]
    for t in range(k):
        tail = jnp.where(row == 8 - k + t, halo[t:t + 1, :], tail)
    return jnp.concatenate([v[:n - 8, :], tail], axis=0)


def _prev_halo_map(tm):
    return lambda i: (jnp.maximum(i * (tm // HALO) - 1, 0), 0)


def _next_halo_map(tm, s):
    return lambda i: (jnp.minimum((i + 1) * (tm // HALO), s // HALO - 1), 0)


def _full(shape):
    return pl.BlockSpec(shape, lambda *_: (0,) * len(shape))


def _resident(shape):
    return pl.BlockSpec(shape, lambda *_: (0,) * len(shape), pipeline_mode=pl.Buffered(1))


def _rows(tm, c):
    return pl.BlockSpec((tm, c), lambda i: (i, 0))


def _sds(shape, dtype):
    return jax.ShapeDtypeStruct(shape, dtype)


def _my_place():
    x, y, c = lax.axis_index("x"), lax.axis_index("y"), lax.axis_index("c")
    return x, y, c


ALL_PEERS = tuple((j >> 2, (j >> 1) & 1, j & 1) for j in range(1, N_DEV))
SIBLING_PEER = ((0, 0, 1),)
CHIP_PEERS = ((0, 1, 0), (1, 0, 0), (1, 1, 0))
BARRIER_ID = {ALL_PEERS: 0, SIBLING_PEER: 1, CHIP_PEERS: 2}


def _entry_barrier(peers):
    x, y, c = _my_place()
    barrier = pltpu.get_barrier_semaphore()
    for dx, dy, dc in peers:
        pl.semaphore_signal(barrier, inc=1, device_id=(x ^ dx, y ^ dy, c ^ dc), device_id_type=MESH)
    pl.semaphore_wait(barrier, len(peers))


def _start_exchange(remote, local):
    for cp in local + remote:
        cp.start()


def _finish_exchange(remote, local):
    for cp in remote:
        cp.wait_recv()
    for cp in remote:
        cp.wait_send()
    for cp in local:
        cp.wait()


class _AllGather:
    peers = ALL_PEERS

    def __init__(self, shards):
        self.ins = list(shards)
        n = len(shards)
        self.out_shape = [_sds((N_DEV * s.shape[0], s.shape[1]), s.dtype) for s in shards]
        self.sems = [pltpu.SemaphoreType.DMA((7 * n,)), pltpu.SemaphoreType.DMA((7 * n,)),
                     pltpu.SemaphoreType.DMA((n,))]

    def _parts(self, ins, outs, sems):
        send_sems, recv_sems, local_sems = sems
        x, y, c = _my_place()
        me, sibling = (x, y, c), (x, y, 1 - c)
        chips = [(1 - x, y), (x, 1 - y), (1 - x, 1 - y)]

        def rows(k, dev):
            r = ins[k].shape[0]
            start = pl.multiple_of((4 * dev[0] + 2 * dev[1] + dev[2]) * r, 8)
            return outs[k].at[pl.ds(start, r), :]

        def copy(k, j, block, to, src=None):
            return pltpu.make_async_remote_copy(
                src_ref=rows(k, block) if src is None else src, dst_ref=rows(k, block),
                send_sem=send_sems.at[7 * k + j], recv_sem=recv_sems.at[7 * k + j],
                device_id=to, device_id_type=MESH)

        n = len(ins)
        mine = [pltpu.make_async_copy(ins[k], rows(k, me), local_sems.at[k]) for k in range(n)]
        first = []
        for k in range(n):
            first.append(copy(k, 0, me, sibling, src=ins[k]))
            first += [copy(k, 1 + j, me, (*chip, c), src=ins[k]) for j, chip in enumerate(chips)]
        return me, sibling, chips, copy, mine, first

    def start(self, ins, outs, sems):
        _, _, _, _, mine, first = self._parts(ins, outs, sems)
        _start_exchange(first, mine)

    def finish(self, ins, outs, sems):
        me, sibling, chips, copy, mine, first = self._parts(ins, outs, sems)
        c = me[2]
        n = len(ins)
        passed = []
        for j, chip in enumerate(chips):
            for k in range(n):
                copy(k, 1 + j, (*chip, c), me).wait_recv()
                fwd = copy(k, 4 + j, (*chip, c), sibling)
                fwd.start()
                passed.append(fwd)
        for k in range(n):
            copy(k, 0, sibling, me).wait_recv()
            for j, chip in enumerate(chips):
                copy(k, 4 + j, (*chip, 1 - c), me).wait_recv()
        for cp in first + passed:
            cp.wait_send()
        for cp in mine:
            cp.wait()


class _ReduceScatter:
    peers = ALL_PEERS

    def __init__(self, parts, bcast=()):
        self.parts = [(lo, cnt) for _, lo, cnt in parts]
        self.n_parts = len(parts)
        self.ins = [a for a, _, _ in parts] + list(bcast)
        self.out_shape = [_sds((N_DEV * cnt, a.shape[1]), a.dtype) for a, _, cnt in parts]
        self.out_shape += [_sds((N_DEV * b.shape[0], b.shape[1]), b.dtype) for b in bcast]
        n = len(self.ins)
        self.sems = [pltpu.SemaphoreType.DMA((7 * n,)), pltpu.SemaphoreType.DMA((7 * n,)),
                     pltpu.SemaphoreType.DMA((n,))]

    def _copies(self, ins, outs, sems):
        send_sems, recv_sems, local_sems = sems
        x, y, c = _my_place()
        me_idx = 4 * x + 2 * y + c
        remote, local = [], []
        for k in range(len(ins)):
            cnt = outs[k].shape[0] // N_DEV
            dst = outs[k].at[pl.ds(pl.multiple_of(me_idx * cnt, 8), cnt), :]
            if k < self.n_parts:
                lo, _ = self.parts[k]
                r = ins[k].shape[0] // N_DEV
                src_of = lambda idx: ins[k].at[pl.ds(pl.multiple_of(idx * r + lo, 8), cnt), :]
            else:
                src_of = lambda idx: ins[k]
            local.append(pltpu.make_async_copy(src_of(me_idx), dst, local_sems.at[k]))
            for j in range(1, N_DEV):
                peer = (x ^ (j >> 2), y ^ ((j >> 1) & 1), c ^ (j & 1))
                peer_idx = 4 * peer[0] + 2 * peer[1] + peer[2]
                remote.append(pltpu.make_async_remote_copy(
                    src_ref=src_of(peer_idx), dst_ref=dst,
                    send_sem=send_sems.at[7 * k + j - 1], recv_sem=recv_sems.at[7 * k + j - 1],
                    device_id=peer, device_id_type=MESH))
        return remote, local

    def start(self, ins, outs, sems):
        _start_exchange(*self._copies(ins, outs, sems))

    def finish(self, ins, outs, sems):
        _finish_exchange(*self._copies(ins, outs, sems))


class _PairExchange:
    peers = SIBLING_PEER

    def __init__(self, arrays):
        self.ins = list(arrays)
        n = len(arrays)
        self.out_shape = [_sds((a.shape[0] // 2, a.shape[1]), a.dtype) for a in arrays]
        self.sems = [pltpu.SemaphoreType.DMA((4 * n,)), pltpu.SemaphoreType.DMA((4 * n,))]

    def _copies(self, ins, outs, sems):
        send_sems, recv_sems = sems
        x, y, c = _my_place()
        remote = []
        for k in range(len(ins)):
            r = ins[k].shape[0] // N_DEV
            for chip in range(4):
                sib = ins[k].at[pl.ds(pl.multiple_of((2 * chip + 1 - c) * r, 8), r), :]
                remote.append(pltpu.make_async_remote_copy(
                    src_ref=sib, dst_ref=outs[k].at[pl.ds(chip * r, r), :],
                    send_sem=send_sems.at[4 * k + chip], recv_sem=recv_sems.at[4 * k + chip],
                    device_id=(x, y, 1 - c), device_id_type=MESH))
        return remote

    def start(self, ins, outs, sems):
        for cp in self._copies(ins, outs, sems):
            cp.start()

    def finish(self, ins, outs, sems):
        remote = self._copies(ins, outs, sems)
        for cp in remote:
            cp.wait_recv()
        for cp in remote:
            cp.wait_send()


class _ChipExchange:
    peers = CHIP_PEERS

    def __init__(self, arrays):
        self.ins = list(arrays)
        self.out_shape = [_sds(a.shape, a.dtype) for a in arrays]
        n = len(self.ins)
        self.sems = [pltpu.SemaphoreType.DMA((3 * n,)), pltpu.SemaphoreType.DMA((3 * n,)),
                     pltpu.SemaphoreType.DMA((n,))]

    def _copies(self, ins, outs, sems):
        send_sems, recv_sems, local_sems = sems
        x, y, c = _my_place()
        my_chip = 2 * x + y
        remote, local = [], []
        for k in range(len(ins)):
            r = ins[k].shape[0] // 4
            dst = outs[k].at[pl.ds(pl.multiple_of(my_chip * r, 8), r), :]
            local.append(pltpu.make_async_copy(ins[k].at[pl.ds(pl.multiple_of(my_chip * r, 8), r), :], dst,
                                               local_sems.at[k]))
            for j in range(1, 4):
                px, py = x ^ (j >> 1), y ^ (j & 1)
                src = ins[k].at[pl.ds(pl.multiple_of((2 * px + py) * r, 8), r), :]
                remote.append(pltpu.make_async_remote_copy(
                    src_ref=src, dst_ref=dst, send_sem=send_sems.at[3 * k + j - 1],
                    recv_sem=recv_sems.at[3 * k + j - 1], device_id=(px, py, c), device_id_type=MESH))
        return remote, local

    def start(self, ins, outs, sems):
        _start_exchange(*self._copies(ins, outs, sems))

    def finish(self, ins, outs, sems):
        _finish_exchange(*self._copies(ins, outs, sems))


def _pcall(body, name, grid, in_specs, out_specs, out_shape, args, scratch=(), comm=None):
    params = pltpu.CompilerParams(dimension_semantics=("arbitrary",) * len(grid), vmem_limit_bytes=VMEM_LIMIT)
    in_specs, out_specs, out_shape, scratch = list(in_specs), list(out_specs), list(out_shape), list(scratch)
    if comm is None:
        res = pl.pallas_call(body, name=name, grid=grid, in_specs=in_specs, out_specs=out_specs, out_shape=out_shape,
                             scratch_shapes=scratch, compiler_params=params)(*args)
        return list(res), []
    n_in, n_out, n_scr = len(in_specs), len(out_specs), len(scratch)
    ci, co = len(comm.ins), len(comm.out_shape)
    total = math.prod(grid)

    def carried(*refs):
        bounds = [0, n_in, n_in + ci, n_in + ci + n_out, n_in + ci + n_out + co, n_in + ci + n_out + co + n_scr]
        ins, cins, outs, couts, scr = (refs[a:b] for a, b in zip(bounds[:-1], bounds[1:]))
        sems = refs[bounds[-1]:]
        step = pl.program_id(0)
        for d in range(1, len(grid)):
            step = step * grid[d] + pl.program_id(d)

        @pl.when(step == 0)
        def _():
            _entry_barrier(comm.peers)
            comm.start(cins, couts, sems)

        body(*ins, *outs, *scr)

        @pl.when(step == total - 1)
        def _():
            comm.finish(cins, couts, sems)

    params = pltpu.CompilerParams(dimension_semantics=("arbitrary",) * len(grid), vmem_limit_bytes=VMEM_LIMIT,
                                  collective_id=BARRIER_ID[comm.peers])
    res = pl.pallas_call(
        carried, name=name, grid=grid, in_specs=in_specs + [ANY] * ci, out_specs=out_specs + [ANY] * co,
        out_shape=out_shape + comm.out_shape, scratch_shapes=scratch + comm.sems, compiler_params=params,
    )(*args, *comm.ins)
    return list(res[:n_out]), list(res[n_out:])


def _exchange_only(comm, name):
    def body(*refs):
        ci, co = len(comm.ins), len(comm.out_shape)
        _entry_barrier(comm.peers)
        comm.start(refs[:ci], refs[ci:ci + co], refs[ci + co:])
        comm.finish(refs[:ci], refs[ci:ci + co], refs[ci + co:])

    params = pltpu.CompilerParams(collective_id=BARRIER_ID[comm.peers])
    return pl.pallas_call(body, name=name, out_shape=comm.out_shape, in_specs=[ANY] * len(comm.ins),
                          out_specs=[ANY] * len(comm.out_shape), scratch_shapes=comm.sems,
                          compiler_params=params)(*comm.ins)


def _norm_inproj(x, g, win_t, b_in, comm):
    s = x.shape[0]
    tm = _row_tile(s, 512)
    widths = (QKV_W, CBX_W, GATE_W)

    def body(x_ref, g_ref, w_ref, b_ref, xn_ref, qkv_ref, cbx_ref, gate_ref):
        xv = x_ref[...]
        r = lax.rsqrt(jnp.mean(xv * xv, axis=-1, keepdims=True) + NORM_EPS)
        xn = (xv * r * g_ref[...]).astype(BF16)
        xn_ref[...] = xn
        off = 0
        for o_ref, w in zip((qkv_ref, cbx_ref, gate_ref), widths):
            acc = lax.dot_general(xn, w_ref[off:off + w, :], NT, preferred_element_type=F32)
            o_ref[...] = (acc + b_ref[:, off:off + w]).astype(BF16)
            off += w

    return _pcall(
        body, "norm_inproj", (s // tm,),
        [_rows(tm, D_MODEL), _full((1, D_MODEL)), _resident((IN_W, D_MODEL)), _full((1, IN_W))],
        [_rows(tm, D_MODEL)] + [_rows(tm, w) for w in widths],
        [_sds((s, D_MODEL), BF16)] + [_sds((s, w), BF16) for w in widths],
        (x, g, win_t, b_in), comm=comm)


def _attn_specs():
    prev = lambda n: jnp.maximum(n - 1, 0)
    return [pl.BlockSpec((BLOCK, ATTN_W), lambda n: (n, 0)),
            pl.BlockSpec((BLOCK, KV_W), lambda n: (prev(n), ATTN_W // KV_W)),
            pl.BlockSpec((BLOCK, KV_W), lambda n: (n, ATTN_W // KV_W)),
            pl.BlockSpec((BLOCK, KV_W), lambda n: (prev(n), ATTN_W // KV_W + 1)),
            pl.BlockSpec((BLOCK, KV_W), lambda n: (n, ATTN_W // KV_W + 1))]


def _lower_lanes():
    return lax.broadcasted_iota(jnp.int32, (BLOCK, 128), 1) < HEAD_DIM


def _stack_heads(val, kh):
    lower = _lower_lanes()
    parts = []
    for g in range(4):
        h = kh * 4 + g
        blk = val[:, (h // 2) * 128:(h // 2 + 1) * 128]
        keep = lower if h % 2 == 0 else jnp.logical_not(lower)
        parts.append(jnp.where(keep, blk, jnp.zeros_like(blk)))
    return jnp.concatenate(parts, axis=0)


def _dup_kv(prev_ref, cur_ref, kh):
    t = jnp.concatenate([prev_ref[...], cur_ref[...]], axis=0).astype(F32)
    rolled = pltpu.roll(t, HEAD_DIM, axis=1)
    lower = lax.broadcasted_iota(jnp.int32, t.shape, 1) < HEAD_DIM
    dup = jnp.where(lower, t, rolled) if kh == 0 else jnp.where(lower, rolled, t)
    return dup.astype(BF16)


def _attn_mask(n):
    row = lax.broadcasted_iota(jnp.int32, (4 * BLOCK, 2 * BLOCK), 0)
    kj = lax.broadcasted_iota(jnp.int32, (4 * BLOCK, 2 * BLOCK), 1)
    dist = (row & (BLOCK - 1)) + BLOCK - kj
    band = jnp.logical_and(dist >= 0, dist < BLOCK)
    return jnp.logical_and(band, jnp.logical_or(kj >= BLOCK, n > 0))


def _sink_col(sinks_ref, kh):
    gi = lax.broadcasted_iota(jnp.int32, (4 * BLOCK, 1), 0) // BLOCK
    col = jnp.zeros((4 * BLOCK, 1), F32)
    for g in range(4):
        col = jnp.where(gi == g, sinks_ref[0, kh * 4 + g], col)
    return col


def _attn_fwd(qkv, sinks, comm):
    s = qkv.shape[0]

    def body(sinks_ref, q_ref, kp_ref, kc_ref, vp_ref, vc_ref, o_ref, lse_ref):
        n = pl.program_id(0)
        mask = _attn_mask(n)
        lower = _lower_lanes()
        lane = lax.broadcasted_iota(jnp.int32, (BLOCK, 128), 1)
        qv = q_ref[...]
        lse_out = jnp.zeros((BLOCK, 128), F32)
        for kh in range(2):
            qs = _stack_heads(qv, kh)
            kd, vd = _dup_kv(kp_ref, kc_ref, kh), _dup_kv(vp_ref, vc_ref, kh)
            sc = lax.dot_general(qs, kd, NT, preferred_element_type=F32) * ATTN_SCALE
            sc = jnp.where(mask, sc, NEG)
            sink = _sink_col(sinks_ref, kh)
            m = jnp.maximum(jnp.max(sc, axis=1, keepdims=True), sink)
            p = jnp.exp(sc - m)
            l = jnp.sum(p, axis=1, keepdims=True) + jnp.exp(sink - m)
            o = jnp.dot(p.astype(BF16), vd, preferred_element_type=F32) / l
            lse = m + jnp.log(l)
            for pair in range(2):
                lo = o[(2 * pair) * BLOCK:(2 * pair + 1) * BLOCK]
                hi = o[(2 * pair + 1) * BLOCK:(2 * pair + 2) * BLOCK]
                col = (kh * 2 + pair) * 128
                o_ref[:, col:col + 128] = jnp.where(lower, lo, hi).astype(BF16)
            for g in range(4):
                lse_out = jnp.where(lane == kh * 4 + g, lse[g * BLOCK:(g + 1) * BLOCK], lse_out)
        lse_ref[...] = lse_out

    return _pcall(
        body, "attn_fwd", (s // BLOCK,),
        [pl.BlockSpec(memory_space=pltpu.SMEM)] + _attn_specs(),
        [pl.BlockSpec((BLOCK, ATTN_W), lambda n: (n, 0)), pl.BlockSpec((BLOCK, 128), lambda n: (n, 0))],
        [_sds((s, ATTN_W), BF16), _sds((s, 128), F32)],
        (sinks, qkv, qkv, qkv, qkv, qkv), comm=comm)


def _conv_u(cbx_ref, halo_ref, w_ref, first):
    cb = cbx_ref[:, 0:CONV_W].astype(F32)
    cc = cbx_ref[:, CONV_W:2 * CONV_W].astype(F32)
    cx = cbx_ref[:, 2 * CONV_W:3 * CONV_W].astype(F32)
    u = cc * cx
    uh = halo_ref[:, CONV_W:2 * CONV_W].astype(F32) * halo_ref[:, 2 * CONV_W:3 * CONV_W].astype(F32)
    uh = jnp.where(first, 0.0, uh)
    u1, u2 = _shifts_down(u, uh, (1, 2))
    cv = w_ref[0:1, :] * u2 + w_ref[1:2, :] * u1 + w_ref[2:3, :] * u
    return cb, cc, cx, u, cv


def _mix_fwd(x, cbx, gates, attn, conv_w, wa, wc, wout, comm):
    s = x.shape[0]
    tm = _row_tile(s)

    def body(x_ref, cbx_ref, halo_ref, gate_ref, attn_ref, cw_ref, wa_ref, wc_ref, wo_ref,
             h1_ref):
        first = pl.program_id(0) == 0
        cb, _, _, _, cv = _conv_u(cbx_ref, halo_ref, cw_ref, first)
        conv = (cb * cv).astype(BF16)
        ap = jnp.dot(attn_ref[...], wa_ref[...], preferred_element_type=F32)
        cp = jnp.dot(conv, wc_ref[...], preferred_element_type=F32)
        ga = gate_ref[:, 0:D_MODEL].astype(F32)
        gc = gate_ref[:, D_MODEL:2 * D_MODEL].astype(F32)
        merged = (_sig(ga) * ap + _sig(gc) * cp).astype(BF16)
        h1_ref[...] = x_ref[...] + jnp.dot(merged, wo_ref[...], preferred_element_type=F32)

    return _pcall(
        body, "mix_fwd", (s // tm,),
        [_rows(tm, D_MODEL), _rows(tm, CBX_W), pl.BlockSpec((HALO, CBX_W), _prev_halo_map(tm)),
         _rows(tm, GATE_W), _rows(tm, ATTN_W), _full((3, CONV_W)), _full((ATTN_W, D_MODEL)),
         _full((CONV_W, D_MODEL)), _full((D_MODEL, D_MODEL))],
        [_rows(tm, D_MODEL)], [_sds((s, D_MODEL), F32)],
        (x, cbx, cbx, gates, attn, conv_w, wa, wc, wout), comm=comm)


def _ffn_up(h1, g, wup_lo, wup_hi, comm):
    s = h1.shape[0]
    tm = _row_tile(s, 512)
    half = D_MODEL // 2

    def body(h_ref, g_ref, wl_ref, wh_ref, hn_ref, up_ref):
        hv = h_ref[...]
        r = lax.rsqrt(jnp.mean(hv * hv, axis=-1, keepdims=True) + NORM_EPS)
        hn = (hv * r * g_ref[...]).astype(BF16)
        hn_ref[...] = hn
        for c in range(2 * D_FF // FF_CHUNK):
            sl = slice(c * FF_CHUNK, (c + 1) * FF_CHUNK)
            acc = lax.dot_general(hn[:, :half], wl_ref[sl, :], NT, preferred_element_type=F32)
            acc = acc + lax.dot_general(hn[:, half:], wh_ref[sl, :], NT, preferred_element_type=F32)
            up_ref[:, sl] = acc.astype(BF16)

    return _pcall(
        body, "ffn_up", (s // tm,),
        [_rows(tm, D_MODEL), _full((1, D_MODEL)), _resident((2 * D_FF, half)), _resident((2 * D_FF, half))],
        [_rows(tm, D_MODEL), _rows(tm, 2 * D_FF)],
        [_sds((s, D_MODEL), BF16), _sds((s, 2 * D_FF), BF16)],
        (h1, g, wup_lo, wup_hi), comm=comm)


def _ffn_conv_cols(up_ref, halo_ref, fcw_ref, first, off):
    u = up_ref[:, off:off + FF_CHUNK].astype(F32)
    uh = jnp.where(first, 0.0, halo_ref[:, off:off + FF_CHUNK].astype(F32))
    w = fcw_ref[:, off:off + FF_CHUNK]
    u1, u2 = _shifts_down(u, uh, (1, 2))
    return w[0:1] * u2 + w[1:2] * u1 + w[2:3] * u


def _ffn_down_loss(up_pre, fcw, wdown, h1, fnorm, target):
    s = h1.shape[0]
    tm = _row_tile(s)

    def body(up_ref, halo_ref, fcw_ref, wd_ref, h1_ref, fn_ref, t_ref, cu_ref, act_ref, dh2_ref, loss_ref, dfn_ref):
        i = pl.program_id(0)

        @pl.when(i == 0)
        def _():
            loss_ref[...] = jnp.zeros_like(loss_ref)
            dfn_ref[...] = jnp.zeros_like(dfn_ref)

        h2 = h1_ref[...]
        for c in range(D_FF // FF_CHUNK):
            gsl = slice(c * FF_CHUNK, (c + 1) * FF_CHUNK)
            vsl = slice(D_FF + c * FF_CHUNK, D_FF + (c + 1) * FF_CHUNK)
            gate = _ffn_conv_cols(up_ref, halo_ref, fcw_ref, i == 0, c * FF_CHUNK)
            cu_ref[:, gsl] = gate.astype(BF16)
            val = _ffn_conv_cols(up_ref, halo_ref, fcw_ref, i == 0, D_FF + c * FF_CHUNK)
            cu_ref[:, vsl] = val.astype(BF16)
            act = (gate * _sig(gate) * val).astype(BF16)
            act_ref[:, gsl] = act
            h2 = h2 + jnp.dot(act, wd_ref[gsl, :], preferred_element_type=F32)
        r = lax.rsqrt(jnp.mean(h2 * h2, axis=-1, keepdims=True) + NORM_EPS)
        yhat = h2 * r
        fn = fn_ref[...]
        diff = yhat * fn - t_ref[...]
        loss_ref[...] += 0.5 * jnp.sum(jnp.sum(diff * diff, axis=1, keepdims=True), axis=0, keepdims=True) / D_MODEL
        dy = diff * (1.0 / D_MODEL)
        dfn_ref[...] += jnp.sum(dy * yhat, axis=0, keepdims=True)
        dyh = dy * fn
        dh2_ref[...] = r * (dyh - yhat * jnp.mean(dyh * yhat, axis=-1, keepdims=True))

    return _pcall(
        body, "ffn_down_loss", (s // tm,),
        [_rows(tm, 2 * D_FF), pl.BlockSpec((HALO, 2 * D_FF), _prev_halo_map(tm)), _full((3, 2 * D_FF)),
         _resident((D_FF, D_MODEL)), _rows(tm, D_MODEL), _full((1, D_MODEL)), _rows(tm, D_MODEL)],
        [_rows(tm, 2 * D_FF), _rows(tm, D_FF), _rows(tm, D_MODEL), _full((1, 128)), _full((1, D_MODEL))],
        [_sds((s, 2 * D_FF), BF16), _sds((s, D_FF), BF16), _sds((s, D_MODEL), F32), _sds((1, 128), F32),
         _sds((1, D_MODEL), F32)],
        (up_pre, up_pre, fcw, wdown, h1, fnorm, target))[0]


def _ffn_bwd(dh2, wdown, up, up_pre, fcw, comm):
    s = dh2.shape[0]
    tm = _row_tile(s)

    def dup_cols(dh, up_ref, wd_ref, c):
        gsl = slice(c * FF_CHUNK, (c + 1) * FF_CHUNK)
        vsl = slice(D_FF + c * FF_CHUNK, D_FF + (c + 1) * FF_CHUNK)
        dact = lax.dot_general(dh, wd_ref[gsl, :], NT, preferred_element_type=F32)
        gate = up_ref[:, gsl].astype(F32)
        val = up_ref[:, vsl].astype(F32)
        sg = _sig(gate)
        return dact * val * (sg * (1.0 + gate * (1.0 - sg))), dact * gate * sg

    def body(dh_ref, dhn_ref, wd_ref, up_ref, upn_ref, x_ref, w_ref, dx_ref, dw_ref):
        i = pl.program_id(0)

        @pl.when(i == 0)
        def _():
            dw_ref[...] = jnp.zeros_like(dw_ref)

        last = i == s // tm - 1
        up1, up2 = _shift_matrix(tm, 1), _shift_matrix(tm, 2)
        dh = dh_ref[...].astype(BF16)
        dhn = dhn_ref[...].astype(BF16)
        for c in range(D_FF // FF_CHUNK):
            halves = zip(dup_cols(dh, up_ref, wd_ref, c), dup_cols(dhn, upn_ref, wd_ref, c),
                         (c * FF_CHUNK, D_FF + c * FF_CHUNK))
            for d, dn, off in halves:
                sl = slice(off, off + FF_CHUNK)
                dn = jnp.where(last, 0.0, dn)
                xv = x_ref[:, sl].astype(F32)
                wv = w_ref[:, sl]
                db = d.astype(BF16)
                d1, d2 = _mxu_shift_up(up1, db, dn, 1), _mxu_shift_up(up2, db, dn, 2)
                dx_ref[:, sl] = (wv[2:3] * d + wv[1:2] * d1 + wv[0:1] * d2).astype(BF16)
                dw_ref[0:1, sl] += jnp.sum(d2 * xv, axis=0, keepdims=True)
                dw_ref[1:2, sl] += jnp.sum(d1 * xv, axis=0, keepdims=True)
                dw_ref[2:3, sl] += jnp.sum(d * xv, axis=0, keepdims=True)

    return _pcall(
        body, "ffn_bwd", (s // tm,),
        [_rows(tm, D_MODEL), pl.BlockSpec((HALO, D_MODEL), _next_halo_map(tm, s)), _resident((D_FF, D_MODEL)),
         _rows(tm, 2 * D_FF), pl.BlockSpec((HALO, 2 * D_FF), _next_halo_map(tm, s)), _rows(tm, 2 * D_FF),
         _full((3, 2 * D_FF))],
        [_rows(tm, 2 * D_FF), _full((3, 2 * D_FF))],
        [_sds((s, 2 * D_FF), BF16), _sds((3, 2 * D_FF), F32)],
        (dh2, dh2, wdown, up, up, up_pre, fcw), comm=comm)


def _matmul_tn(a, b, tk, name, ts=1024, comm=None):
    s, ka = a.shape
    n = b.shape[1]
    ts = min(ts, s)
    steps = s // ts

    def body(a_ref, b_ref, o_ref, acc_ref):
        j = pl.program_id(1)

        @pl.when(j == 0)
        def _():
            acc_ref[...] = jnp.zeros_like(acc_ref)

        acc_ref[...] += lax.dot_general(a_ref[...].astype(BF16), b_ref[...].astype(BF16), TN,
                                        preferred_element_type=F32)

        @pl.when(j == steps - 1)
        def _():
            o_ref[...] = acc_ref[...].astype(BF16)

    outs, couts = _pcall(
        body, name, (ka // tk, steps),
        [pl.BlockSpec((ts, tk), lambda i, j: (j, i)), pl.BlockSpec((ts, n), lambda i, j: (j, 0))],
        [pl.BlockSpec((tk, n), lambda i, j: (i, 0))], [_sds((ka, n), BF16)],
        (a, b), scratch=[pltpu.VMEM((tk, n), F32)], comm=comm)
    return outs[0] if comm is None else (outs[0], couts)


def _norm_bwd_tile(xv, g, dy):
    r = lax.rsqrt(jnp.mean(xv * xv, axis=-1, keepdims=True) + NORM_EPS)
    xhat = xv * r
    dg = jnp.sum(dy * xhat, axis=0, keepdims=True)
    dyh = dy * g
    return r * (dyh - xhat * jnp.mean(dyh * xhat, axis=-1, keepdims=True)), dg


def _ffn_up_bwd(dup_pre, wup_lo, wup_hi, h1, g, dh2, comm):
    s = h1.shape[0]
    tm = _row_tile(s, 512)
    half = D_MODEL // 2

    def body(du_ref, wl_ref, wh_ref, h_ref, g_ref, dh2_ref, dh1_ref, dg_ref):
        @pl.when(pl.program_id(0) == 0)
        def _():
            dg_ref[...] = jnp.zeros_like(dg_ref)

        du = du_ref[...]
        dhn = jnp.concatenate([jnp.dot(du, wl_ref[...], preferred_element_type=F32),
                               jnp.dot(du, wh_ref[...], preferred_element_type=F32)], axis=1)
        dx, dg = _norm_bwd_tile(h_ref[...], g_ref[...], dhn)
        dg_ref[...] += dg
        dh1_ref[...] = dh2_ref[...] + dx

    return _pcall(
        body, "ffn_up_bwd", (s // tm,),
        [_rows(tm, 2 * D_FF), _resident((2 * D_FF, half)), _resident((2 * D_FF, half)), _rows(tm, D_MODEL),
         _full((1, D_MODEL)), _rows(tm, D_MODEL)],
        [_rows(tm, D_MODEL), _full((1, D_MODEL))],
        [_sds((s, D_MODEL), F32), _sds((1, D_MODEL), F32)],
        (dup_pre, wup_lo, wup_hi, h1, g, dh2), comm=comm)


def _mix_bwd(dh1, wout, gates, attn, wa, wc, cbx, conv_w, comm):
    s = dh1.shape[0]
    tm = _row_tile(s)
    steps = s // tm

    def body(dh_ref, wo_ref, gate_ref, attn_ref, wa_ref, wc_ref, cbx_ref, halo_ref, cw_ref,
             dg_ref, dattn_ref, dcb_ref, dcv_ref, gwo_ref, gwa_ref, gwc_ref, acc_o, acc_a, acc_c):
        i = pl.program_id(0)

        @pl.when(i == 0)
        def _():
            acc_o[...] = jnp.zeros_like(acc_o)
            acc_a[...] = jnp.zeros_like(acc_a)
            acc_c[...] = jnp.zeros_like(acc_c)

        cb, _, _, _, cv = _conv_u(cbx_ref, halo_ref, cw_ref, i == 0)
        attn = attn_ref[...]
        conv = (cb * cv).astype(BF16)
        ap = jnp.dot(attn, wa_ref[...], preferred_element_type=F32)
        cp = jnp.dot(conv, wc_ref[...], preferred_element_type=F32)
        dhb = dh_ref[...].astype(BF16)
        dm = lax.dot_general(dhb, wo_ref[...], NT, preferred_element_type=F32)
        sa = _sig(gate_ref[:, 0:D_MODEL].astype(F32))
        sc = _sig(gate_ref[:, D_MODEL:2 * D_MODEL].astype(F32))
        merged = (sa * ap + sc * cp).astype(BF16)
        da = (dm * sa).astype(BF16)
        dc = (dm * sc).astype(BF16)
        dg_ref[:, 0:D_MODEL] = (dm * ap * sa * (1.0 - sa)).astype(BF16)
        dg_ref[:, D_MODEL:2 * D_MODEL] = (dm * cp * sc * (1.0 - sc)).astype(BF16)
        dattn_ref[...] = lax.dot_general(da, wa_ref[...], NT, preferred_element_type=F32).astype(BF16)
        dconv = lax.dot_general(dc, wc_ref[...], NT, preferred_element_type=F32)
        dcb_ref[...] = (dconv * cv).astype(BF16)
        dcv_ref[...] = (dconv * cb).astype(BF16)
        acc_o[...] += lax.dot_general(merged, dhb, TN, preferred_element_type=F32)
        acc_a[...] += lax.dot_general(attn, da, TN, preferred_element_type=F32)
        acc_c[...] += lax.dot_general(conv, dc, TN, preferred_element_type=F32)

        @pl.when(i == steps - 1)
        def _():
            gwo_ref[...] = acc_o[...].astype(BF16)
            gwa_ref[...] = acc_a[...].astype(BF16)
            gwc_ref[...] = acc_c[...].astype(BF16)

    return _pcall(
        body, "mix_bwd", (steps,),
        [_rows(tm, D_MODEL), _full((D_MODEL, D_MODEL)), _rows(tm, GATE_W), _rows(tm, ATTN_W),
         _full((ATTN_W, D_MODEL)), _full((CONV_W, D_MODEL)), _rows(tm, CBX_W),
         pl.BlockSpec((HALO, CBX_W), _prev_halo_map(tm)), _full((3, CONV_W))],
        [_rows(tm, GATE_W), _rows(tm, ATTN_W), _rows(tm, CONV_W), _rows(tm, CONV_W),
         _full((D_MODEL, D_MODEL)), _full((ATTN_W, D_MODEL)), _full((CONV_W, D_MODEL))],
        [_sds((s, GATE_W), BF16), _sds((s, ATTN_W), BF16), _sds((s, CONV_W), BF16), _sds((s, CONV_W), BF16),
         _sds((D_MODEL, D_MODEL), BF16), _sds((ATTN_W, D_MODEL), BF16), _sds((CONV_W, D_MODEL), BF16)],
        (dh1, wout, gates, attn, wa, wc, cbx, cbx, conv_w),
        scratch=[pltpu.VMEM((D_MODEL, D_MODEL), F32), pltpu.VMEM((ATTN_W, D_MODEL), F32),
                 pltpu.VMEM((CONV_W, D_MODEL), F32)], comm=comm)


def _conv_branch_bwd(dcv, cbx, conv_w):
    s = dcv.shape[0]
    tm = _row_tile(s)

    def body(d_ref, dn_ref, cbx_ref, w_ref, dcc_ref, dcx_ref, dw_ref):
        i = pl.program_id(0)

        @pl.when(i == 0)
        def _():
            dw_ref[...] = jnp.zeros_like(dw_ref)

        last = i == s // tm - 1
        cc = cbx_ref[:, CONV_W:2 * CONV_W].astype(F32)
        cx = cbx_ref[:, 2 * CONV_W:3 * CONV_W].astype(F32)
        u = cc * cx
        d = d_ref[...].astype(F32)
        dn = jnp.where(last, 0.0, dn_ref[...].astype(F32))
        d1, d2 = _shifts_up(d, dn, (1, 2))
        du = w_ref[2:3, :] * d + w_ref[1:2, :] * d1 + w_ref[0:1, :] * d2
        dcc_ref[...] = (du * cx).astype(BF16)
        dcx_ref[...] = (du * cc).astype(BF16)
        dw_ref[0:1, :] += jnp.sum(d2 * u, axis=0, keepdims=True)
        dw_ref[1:2, :] += jnp.sum(d1 * u, axis=0, keepdims=True)
        dw_ref[2:3, :] += jnp.sum(d * u, axis=0, keepdims=True)

    return _pcall(
        body, "conv_branch_bwd", (s // tm,),
        [_rows(tm, CONV_W), pl.BlockSpec((HALO, CONV_W), _next_halo_map(tm, s)), _rows(tm, CBX_W),
         _full((3, CONV_W))],
        [_rows(tm, CONV_W), _rows(tm, CONV_W), _full((3, CONV_W))],
        [_sds((s, CONV_W), BF16), _sds((s, CONV_W), BF16), _sds((3, CONV_W), F32)],
        (dcv, dcv, cbx, conv_w))[0]


def _attn_bwd(qkv, sinks, attn, lse, dattn, comm):
    s = qkv.shape[0]

    def body(sinks_ref, q_ref, kp_ref, kc_ref, vp_ref, vc_ref, o_ref, lse_ref, do_ref,
             dq_ref, dk_ref, dv_ref, ds_ref):
        n = pl.program_id(0)

        @pl.when(n == 0)
        def _():
            dk_ref[...] = jnp.zeros_like(dk_ref)
            dv_ref[...] = jnp.zeros_like(dv_ref)
            ds_ref[...] = jnp.zeros_like(ds_ref)

        mask = _attn_mask(n)
        lower = _lower_lanes()
        lane = lax.broadcasted_iota(jnp.int32, (BLOCK, 128), 1)
        lower2 = lax.broadcasted_iota(jnp.int32, (2 * BLOCK, 128), 1) < HEAD_DIM
        lane1 = lax.broadcasted_iota(jnp.int32, (1, 128), 1)
        qv, ov, dov, lsev = q_ref[...], o_ref[...], do_ref[...], lse_ref[...]
        dk_fold, dv_fold = [], []
        dsink = jnp.zeros((1, 128), F32)
        for kh in range(2):
            qs = _stack_heads(qv, kh)
            dos = _stack_heads(dov, kh)
            os_ = _stack_heads(ov, kh)
            kd, vd = _dup_kv(kp_ref, kc_ref, kh), _dup_kv(vp_ref, vc_ref, kh)
            lse = jnp.concatenate(
                [jnp.sum(jnp.where(lane == kh * 4 + g, lsev, 0.0), axis=1, keepdims=True) for g in range(4)], axis=0)
            sc = lax.dot_general(qs, kd, NT, preferred_element_type=F32) * ATTN_SCALE
            p = jnp.exp(jnp.where(mask, sc, NEG) - lse)
            dp = lax.dot_general(dos, vd, NT, preferred_element_type=F32)
            delta = jnp.sum(dos.astype(F32) * os_.astype(F32), axis=1, keepdims=True)
            dsc = (p * (dp - delta) * ATTN_SCALE).astype(BF16)
            dqs = jnp.dot(dsc, kd, preferred_element_type=F32)
            for pair in range(2):
                lo = dqs[(2 * pair) * BLOCK:(2 * pair + 1) * BLOCK]
                hi = dqs[(2 * pair + 1) * BLOCK:(2 * pair + 2) * BLOCK]
                col = (kh * 2 + pair) * 128
                dq_ref[:, col:col + 128] = jnp.where(lower, lo, hi).astype(BF16)
            dkd = lax.dot_general(dsc, qs, TN, preferred_element_type=F32)
            dvd = lax.dot_general(p.astype(BF16), dos, TN, preferred_element_type=F32)
            dk_fold.append(dkd + pltpu.roll(dkd, HEAD_DIM, axis=1))
            dv_fold.append(dvd + pltpu.roll(dvd, HEAD_DIM, axis=1))
            psink = jnp.exp(_sink_col(sinks_ref, kh) - lse) * delta
            for g in range(4):
                tot = jnp.sum(psink[g * BLOCK:(g + 1) * BLOCK], axis=0, keepdims=True)
                dsink = dsink - jnp.where(lane1 == kh * 4 + g, tot, 0.0)
        dk2 = jnp.where(lower2, dk_fold[0], dk_fold[1])
        dv2 = jnp.where(lower2, dv_fold[0], dv_fold[1])
        ds_ref[...] += dsink
        cur = pl.ds(pl.multiple_of(n * BLOCK, BLOCK), BLOCK)
        dk_ref[cur, :] += dk2[BLOCK:]
        dv_ref[cur, :] += dv2[BLOCK:]

        @pl.when(n > 0)
        def _():
            prev = pl.ds(pl.multiple_of((n - 1) * BLOCK, BLOCK), BLOCK)
            dk_ref[prev, :] += dk2[:BLOCK]
            dv_ref[prev, :] += dv2[:BLOCK]

    blk = lambda w: pl.BlockSpec((BLOCK, w), lambda n: (n, 0))
    return _pcall(
        body, "attn_bwd", (s // BLOCK,),
        [pl.BlockSpec(memory_space=pltpu.SMEM)] + _attn_specs() + [blk(ATTN_W), blk(128), blk(ATTN_W)],
        [blk(ATTN_W), _full((s, KV_W)), _full((s, KV_W)), _full((1, 128))],
        [_sds((s, ATTN_W), BF16), _sds((s, KV_W), F32), _sds((s, KV_W), F32), _sds((1, 128), F32)],
        (sinks, qkv, qkv, qkv, qkv, qkv, attn, lse, dattn), comm=comm)


DPROJ_PIECES = (ATTN_W, KV_W, KV_W, CONV_W, CONV_W, CONV_W, GATE_W)
DPROJ_OFFSETS = tuple(sum(DPROJ_PIECES[:k]) for k in range(len(DPROJ_PIECES)))


def _grad_w_in(pieces, xn, comm):
    s = xn.shape[0]
    ts = min(1024, s)
    steps = s // ts
    rows0 = DPROJ_OFFSETS[6]

    def body(*refs):
        p_refs, b_ref, o_ref, acc_ref, stage_ref, sem = refs[:7], refs[7], refs[8], refs[9], refs[10], refs[11]
        i, j = pl.program_id(0), pl.program_id(1)

        @pl.when(j == 0)
        def _():
            acc_ref[...] = jnp.zeros_like(acc_ref)

        bv = b_ref[...]

        def flush(lo, n):
            stage_ref[0:n, :] = acc_ref[0:n, :].astype(BF16)
            cp = pltpu.make_async_copy(stage_ref.at[0:n, :], o_ref.at[lo:lo + n, :], sem)
            cp.start()
            cp.wait()

        @pl.when(i == 0)
        def _():
            for p_ref, off, w in zip(p_refs[:6], DPROJ_OFFSETS[:6], DPROJ_PIECES[:6]):
                acc_ref[off:off + w, :] += lax.dot_general(p_ref[...].astype(BF16), bv, TN,
                                                           preferred_element_type=F32)

            @pl.when(j == steps - 1)
            def _():
                flush(0, rows0)

        @pl.when(i == 1)
        def _():
            acc_ref[0:GATE_W, :] += lax.dot_general(p_refs[6][...], bv, TN, preferred_element_type=F32)

            @pl.when(j == steps - 1)
            def _():
                flush(rows0, GATE_W)

    def piece_spec(w, group):
        return pl.BlockSpec((ts, w), lambda i, j: (jnp.where(i == group, j, 0), 0))

    outs, couts = _pcall(
        body, "grad_w_in", (2, steps),
        [piece_spec(w, 0) for w in DPROJ_PIECES[:6]] + [piece_spec(GATE_W, 1),
                                                         pl.BlockSpec((ts, D_MODEL), lambda i, j: (j, 0))],
        [ANY], [_sds((IN_W, D_MODEL), BF16)], (*pieces, xn),
        scratch=[pltpu.VMEM((rows0, D_MODEL), F32), pltpu.VMEM((rows0, D_MODEL), BF16), pltpu.SemaphoreType.DMA],
        comm=comm)
    return outs[0], couts


def _inproj_bwd(pieces, win_t, x, g, dh1, comm):
    s = x.shape[0]
    tm = _row_tile(s, 512)

    def body(*refs):
        p_refs = refs[:7]
        w_ref, x_ref, g_ref, dh_ref, dx_ref, db_ref, dg_ref = refs[7:]

        @pl.when(pl.program_id(0) == 0)
        def _():
            db_ref[...] = jnp.zeros_like(db_ref)
            dg_ref[...] = jnp.zeros_like(dg_ref)

        dxn = jnp.zeros((tm, D_MODEL), F32)
        for p_ref, off, w in zip(p_refs, DPROJ_OFFSETS, DPROJ_PIECES):
            v = p_ref[...].astype(BF16)
            db_ref[:, off:off + w] += jnp.sum(v.astype(F32), axis=0, keepdims=True)
            dxn = dxn + jnp.dot(v, w_ref[off:off + w, :], preferred_element_type=F32)
        dx, dg = _norm_bwd_tile(x_ref[...], g_ref[...], dxn)
        dg_ref[...] += dg
        dx_ref[...] = dh_ref[...] + dx

    return _pcall(
        body, "inproj_bwd", (s // tm,),
        [_rows(tm, w) for w in DPROJ_PIECES] + [_resident((IN_W, D_MODEL)), _rows(tm, D_MODEL), _full((1, D_MODEL)),
                                                _rows(tm, D_MODEL)],
        [_rows(tm, D_MODEL), _full((1, IN_W)), _full((1, D_MODEL))],
        [_sds((s, D_MODEL), F32), _sds((1, IN_W), F32), _sds((1, D_MODEL), F32)],
        (*pieces, win_t, x, g, dh1), comm=comm)


def _adam_math(w, g, m, v):
    m2 = ADAM_B1 * m + (1.0 - ADAM_B1) * g
    v2 = ADAM_B2 * v + (1.0 - ADAM_B2) * (g * g)
    m_hat = m2 / (1.0 - ADAM_B1 ** ADAM_STEP)
    v_hat = v2 / (1.0 - ADAM_B2 ** ADAM_STEP)
    delta = -ADAM_LR * (m_hat / (jnp.sqrt(v_hat) + ADAM_EPS) + ADAM_WD * w)
    return delta, m2, v2


def _sum_slots(ref):
    tot = ref[0].astype(F32)
    for i in range(1, ref.shape[0]):
        tot = tot + ref[i].astype(F32)
    return tot


def _pair_add(partials, theirs, tr, name):
    r = partials.shape[0] // N_DEV
    c = partials.shape[1]
    nt = r // tr
    core = lax.axis_index("c").astype(jnp.int32).reshape(1)

    def body(core_ref, a_ref, b_ref, o_ref):
        o_ref[...] = (a_ref[...].astype(F32) + b_ref[...].astype(F32)).astype(BF16)

    grid_spec = pltpu.PrefetchScalarGridSpec(
        num_scalar_prefetch=1, grid=(4 * nt,),
        in_specs=[pl.BlockSpec((None, None, tr, c), lambda i, core_ref: (i // nt, core_ref[0], i % nt, 0)),
                  pl.BlockSpec((tr, c), lambda i, core_ref: (i, 0))],
        out_specs=pl.BlockSpec((tr, c), lambda i, core_ref: (i, 0)))
    return pl.pallas_call(body, name=name, grid_spec=grid_spec, out_shape=_sds((4 * r, c), BF16))(
        core, partials.reshape(4, 2, r, c), theirs)


def _sum_adamw(parts, w, m, v, tr, name):
    r, c = w.shape

    def body(p_ref, w_ref, m_ref, v_ref, g_ref, d_ref, m2_ref, v2_ref):
        g = _sum_slots(p_ref)
        g_ref[...] = g
        d_ref[...], m2_ref[...], v2_ref[...] = _adam_math(w_ref[...], g, m_ref[...], v_ref[...])

    spec = pl.BlockSpec((tr, c), lambda i: (i, 0))
    return _pcall(body, name, (r // tr,), [pl.BlockSpec((N_DEV, tr, c), lambda i: (0, i, 0)), spec, spec, spec],
                  [spec] * 4, [_sds((r, c), F32)] * 4, (parts, w, m, v))[0]


def _sum_parts_adamw(parts, w, m, v, tr, name):
    c = w.shape[1]
    tiles = [p.shape[1] // tr for p in parts]
    starts = [sum(tiles[:k]) for k in range(len(parts))]
    n_parts = len(parts)

    def body(*refs):
        p_refs = refs[:n_parts]
        w_ref, m_ref, v_ref, g_ref, d_ref, m2_ref, v2_ref = refs[n_parts:]
        i = pl.program_id(0)
        for p_ref, st, nt in zip(p_refs, starts, tiles):
            @pl.when(jnp.logical_and(i >= st, i < st + nt))
            def _(p_ref=p_ref):
                g_ref[...] = _sum_slots(p_ref)

        d_ref[...], m2_ref[...], v2_ref[...] = _adam_math(w_ref[...], g_ref[...], m_ref[...], v_ref[...])

    def part_spec(p, st, nt):
        return pl.BlockSpec((p.shape[0], tr, c), lambda i: (0, jnp.clip(i - st, 0, nt - 1), 0))

    spec = pl.BlockSpec((tr, c), lambda i: (i, 0))
    return _pcall(
        body, name, (sum(tiles),),
        [part_spec(p, st, nt) for p, st, nt in zip(parts, starts, tiles)] + [spec, spec, spec],
        [spec] * 4, [_sds(w.shape, F32)] * 4, (*parts, w, m, v))[0]


ROW_MIX, ROW_FFN, ROW_FINAL, ROW_SINKS, ROW_LOSS, ROW_BIN, ROW_CW, ROW_FCW = 0, 1, 2, 3, 4, 5, 10, 13
FCW_ROWS = 6


def _wide_pieces(width):
    return [(k * D_MODEL, min(D_MODEL, width - k * D_MODEL)) for k in range(-(-width // D_MODEL))]


def _pack_small(dmix, dffn, dfn, dsink, loss, dbin, dcw, dfcw):
    def body(mix_ref, ffn_ref, fn_ref, sink_ref, loss_ref, bin_ref, cw_ref, fcw_ref, o_ref):
        o_ref[...] = jnp.zeros_like(o_ref)
        o_ref[ROW_MIX:ROW_MIX + 1, :] = mix_ref[...]
        o_ref[ROW_FFN:ROW_FFN + 1, :] = ffn_ref[...]
        o_ref[ROW_FINAL:ROW_FINAL + 1, :] = fn_ref[...]
        o_ref[ROW_SINKS:ROW_SINKS + 1, 0:128] = sink_ref[...]
        o_ref[ROW_LOSS:ROW_LOSS + 1, 0:128] = loss_ref[...]
        for k, (off, w) in enumerate(_wide_pieces(IN_W)):
            o_ref[ROW_BIN + k:ROW_BIN + k + 1, 0:w] = bin_ref[:, off:off + w]
        o_ref[ROW_CW:ROW_CW + 3, 0:CONV_W] = cw_ref[...]
        for a in range(3):
            for k, (off, w) in enumerate(_wide_pieces(2 * D_FF)):
                row = ROW_FCW + FCW_ROWS * a + k
                o_ref[row:row + 1, 0:w] = fcw_ref[a:a + 1, off:off + w]

    return pl.pallas_call(body, name="pack_small", out_shape=_sds((SMALL_ROWS, D_MODEL), F32))(
        dmix, dffn, dfn, dsink, loss, dbin, dcw, dfcw)


def _small_sums_adamw(r_small, params):
    rows = (ROW_MIX, ROW_BIN, ROW_SINKS, ROW_FFN, ROW_FINAL)

    def body(*refs):
        r_ref, p_refs, o_refs = refs[0], refs[1:16], refs[16:]
        tot = _sum_slots(r_ref)
        for k, row in enumerate(rows):
            w_ref, m_ref, v_ref = p_refs[3 * k:3 * k + 3]
            g_ref, d_ref, m2_ref, v2_ref = o_refs[4 * k:4 * k + 4]
            for j, (off, w) in enumerate(_wide_pieces(w_ref.shape[1])):
                g_ref[:, off:off + w] = tot[row + j:row + j + 1, 0:w]
            d_ref[...], m2_ref[...], v2_ref[...] = _adam_math(w_ref[...], g_ref[...], m_ref[...], v_ref[...])
        cw_ref, fcw_ref, loss_ref = o_refs[20:]
        cw_ref[...] = tot[ROW_CW:ROW_CW + 3, 0:CONV_W]
        for a in range(3):
            for j, (off, w) in enumerate(_wide_pieces(2 * D_FF)):
                row = ROW_FCW + FCW_ROWS * a + j
                fcw_ref[a:a + 1, off:off + w] = tot[row:row + 1, 0:w]
        loss_ref[...] = tot[ROW_LOSS:ROW_LOSS + 1, 0:128]

    flat = [t for p in params for t in p]
    out_shape = [_sds(p[0].shape, F32) for p in params for _ in range(4)]
    out_shape += [_sds((3, CONV_W), F32), _sds((3, 2 * D_FF), F32), _sds((1, 128), F32)]
    res = pl.pallas_call(body, name="small_sums_adamw", out_shape=out_shape)(r_small, *flat)
    return [tuple(res[4 * k:4 * k + 4]) for k in range(5)], res[20], res[21], res[22]


def _adamw_pair(a, b):
    def body(*refs):
        for k in range(2):
            w_ref, g_ref, m_ref, v_ref = refs[4 * k:4 * k + 4]
            d_ref, m2_ref, v2_ref = refs[8 + 3 * k:8 + 3 * k + 3]
            d_ref[...], m2_ref[...], v2_ref[...] = _adam_math(w_ref[...], g_ref[...], m_ref[...], v_ref[...])

    out_shape = [_sds(a[0].shape, F32)] * 3 + [_sds(b[0].shape, F32)] * 3
    res = pl.pallas_call(body, name="adamw_conv_weights", out_shape=out_shape)(*a, *b)
    return tuple(res[:3]), tuple(res[3:])


def _pad_cols(a, c):
    return jnp.pad(a, ((0, 0), (0, c - a.shape[1])))


def _to_col_slabs(g):
    r = g.shape[0]
    return jnp.transpose(g.reshape(r, N_DEV, 128), (1, 0, 2)).reshape(N_DEV * r, 128)


def _from_col_slabs(t):
    r = t.shape[0] // N_DEV
    return jnp.transpose(t.reshape(N_DEV, r, 128), (1, 0, 2)).reshape(r, N_DEV * 128)


def _slots(t):
    return t.reshape(N_DEV, t.shape[0] // N_DEV, t.shape[1])


def kernel(x, mix_norm, w_in, b_in, sinks, conv_w, w_attn_branch, w_conv_branch, w_out, ffn_norm, w_up, ffn_conv_w, w_down, final_norm, loss_target, m_mix_norm, m_w_in, m_b_in, m_sinks, m_conv_w, m_w_attn_branch, m_w_conv_branch, m_w_out, m_ffn_norm, m_w_up, m_ffn_conv_w, m_w_down, m_final_norm, v_mix_norm, v_w_in, v_b_in, v_sinks, v_conv_w, v_w_attn_branch, v_w_conv_branch, v_w_out, v_ffn_norm, v_w_up, v_ffn_conv_w, v_w_down, v_final_norm):
    xs, tgt = x[0], loss_target[0]
    me = 4 * lax.axis_index("x") + 2 * lax.axis_index("y") + lax.axis_index("c")
    in_rows, up_rows = IN_W // N_DEV, 2 * D_FF // N_DEV

    conv_sh = jnp.concatenate([_pad_cols(ffn_conv_w[0], 768), _pad_cols(conv_w[0], 768),
                               jnp.zeros((2, 768), F32)], axis=0)
    win_sh, wup_sh = w_in[0].T.astype(BF16), w_up[0].T.astype(BF16)
    wout_sh, wdown_sh = w_out[0].astype(BF16), w_down[0].astype(BF16)
    wa_sh, wc_sh = w_attn_branch[0].astype(BF16), w_conv_branch[0].astype(BF16)

    half = D_MODEL // 2
    (win_t,) = _exchange_only(_AllGather([win_sh]), "gather_w_in")
    (xn, qkv, cbx, gates), (wa_s, wc_s, wout, conv_g) = _norm_inproj(
        xs, mix_norm, win_t, b_in, _AllGather([wa_sh, wc_sh, wout_sh, conv_sh]))
    (attn, lse), (wup_lo,) = _attn_fwd(qkv, sinks, _AllGather([wup_sh[:, :half]]))
    wa, wc = _from_col_slabs(wa_s), _from_col_slabs(wc_s)
    conv_g = conv_g.reshape(N_DEV, 8, 768)
    fcw = jnp.transpose(conv_g[:, 0:3, :up_rows], (1, 0, 2)).reshape(3, 2 * D_FF)
    cw = jnp.transpose(conv_g[:, 3:6, :CONV_W // N_DEV], (1, 0, 2)).reshape(3, CONV_W)
    (h1,), (wup_hi,) = _mix_fwd(xs, cbx, gates, attn, cw, wa, wc, wout, _AllGather([wup_sh[:, half:]]))
    (hn, up_pre), (wdown,) = _ffn_up(h1, ffn_norm, wup_lo, wup_hi, _AllGather([wdown_sh]))
    up, act, dh2, loss_p, dfn_p = _ffn_down_loss(up_pre, fcw, wdown, h1, final_norm.reshape(1, D_MODEL), tgt)

    dn_rows, q_up = D_FF // N_DEV, up_rows // 4
    g_wdown = _matmul_tn(act, dh2, FF_CHUNK, "grad_w_down")
    (dup_pre, dfcw_p), (r_wdown,) = _ffn_bwd(dh2, wdown, up, up_pre, fcw, _ReduceScatter([(g_wdown, 0, dn_rows)]))
    g_wup_t = _matmul_tn(dup_pre, hn, FF_CHUNK, "grad_w_up")
    (dh1, dffn_p), (r_wup_a,) = _ffn_up_bwd(dup_pre, wup_lo, wup_hi, h1, ffn_norm, dh2,
                                            _ReduceScatter([(g_wup_t, 0, q_up)]))
    (dgates, dattn, dcb, dcv, g_wout, g_wa_nat, g_wc_nat), (r_wup_b,) = _mix_bwd(
        dh1, wout, gates, attn, wa, wc, cbx, cw, _ReduceScatter([(g_wup_t, q_up, q_up)]))
    g_wa, g_wc = _to_col_slabs(g_wa_nat), _to_col_slabs(g_wc_nat)
    dcc, dcx, dcw_p = _conv_branch_bwd(dcv, cbx, cw)
    (dq, dk, dv, dsink_p), (r_wup_c, r_wout, r_wa, r_wc) = _attn_bwd(
        qkv, sinks, attn, lse, dattn,
        _ReduceScatter([(g_wup_t, 2 * q_up, q_up), (g_wout, 0, D_MODEL // N_DEV), (g_wa, 0, ATTN_W),
                        (g_wc, 0, CONV_W)]))
    dproj = (dq, dk, dv, dcb, dcc, dcx, dgates)
    g_win_t, (r_wup_d,) = _grad_w_in(dproj, xn, _ReduceScatter([(g_wup_t, 3 * q_up, q_up)]))
    (win_theirs,) = _exchange_only(_PairExchange([g_win_t]), "pair_exchange_w_in")
    q_win = _pair_add(g_win_t, win_theirs, in_rows // 2, "pair_add_w_in")
    (dx, dbin_p, dmix_p), (r_win,) = _inproj_bwd(dproj, win_t, xs, mix_norm, dh1, _ChipExchange([q_win]))

    small = _pack_small(dmix_p, dffn_p, dfn_p, dsink_p, loss_p, dbin_p, dcw_p, dfcw_p)
    (r_small,) = _exchange_only(_ReduceScatter([], [small]), "exchange_small")

    fn2, m_fn2, v_fn2 = (t.reshape(1, D_MODEL) for t in (final_norm, m_final_norm, v_final_norm))
    small_res, g_cw_full, g_fcw_full, loss_row = _small_sums_adamw(
        _slots(r_small), [(mix_norm, m_mix_norm, v_mix_norm), (b_in, m_b_in, v_b_in), (sinks, m_sinks, v_sinks),
                          (ffn_norm, m_ffn_norm, v_ffn_norm), (fn2, m_fn2, v_fn2)])
    loss = loss_row[0, 0]
    g_cw = lax.dynamic_slice_in_dim(g_cw_full, me * (CONV_W // N_DEV), CONV_W // N_DEV, axis=1)
    g_fcw = lax.dynamic_slice_in_dim(g_fcw_full, me * up_rows, up_rows, axis=1)
    cw_res, fcw_res = _adamw_pair((conv_w[0], g_cw, m_conv_w[0], v_conv_w[0]),
                                  (ffn_conv_w[0], g_fcw, m_ffn_conv_w[0], v_ffn_conv_w[0]))

    big = {}
    big["w_in"] = tuple(t.T for t in _sum_parts_adamw(
        [r_win.reshape(4, in_rows, D_MODEL)], w_in[0].T, m_w_in[0].T, v_w_in[0].T, in_rows // 2, "adamw_w_in"))
    big["w_up"] = tuple(t.T for t in _sum_parts_adamw(
        [_slots(r_wup_a), _slots(r_wup_b), _slots(r_wup_c), _slots(r_wup_d)], w_up[0].T, m_w_up[0].T, v_w_up[0].T, q_up,
        "adamw_w_up"))
    big["w_out"] = _sum_adamw(_slots(r_wout), w_out[0], m_w_out[0], v_w_out[0], 128, "adamw_w_out")
    big["w_down"] = _sum_adamw(_slots(r_wdown), w_down[0], m_w_down[0], v_w_down[0], dn_rows // 2, "adamw_w_down")
    big["w_attn_branch"] = _sum_adamw(_slots(r_wa), w_attn_branch[0], m_w_attn_branch[0], v_w_attn_branch[0], 256,
                                      "adamw_w_attn_branch")
    big["w_conv_branch"] = _sum_adamw(_slots(r_wc), w_conv_branch[0], m_w_conv_branch[0], v_w_conv_branch[0], 256,
                                      "adamw_w_conv_branch")

    res = dict(zip(("mix_norm", "b_in", "sinks", "ffn_norm"), small_res[:4]))
    res["final_norm"] = tuple(t.reshape(final_norm.shape) for t in small_res[4])
    res["conv_w"] = tuple(t.reshape(conv_w.shape) for t in (g_cw,) + cw_res)
    res["ffn_conv_w"] = tuple(t.reshape(ffn_conv_w.shape) for t in (g_fcw,) + fcw_res)
    for name, ref_w in (("w_in", w_in), ("w_up", w_up), ("w_out", w_out), ("w_down", w_down),
                        ("w_attn_branch", w_attn_branch), ("w_conv_branch", w_conv_branch)):
        res[name] = tuple(t.reshape(ref_w.shape) for t in big[name])

    order = ["mix_norm", "w_in", "b_in", "sinks", "conv_w", "w_attn_branch", "w_conv_branch", "w_out",
             "ffn_norm", "w_up", "ffn_conv_w", "w_down", "final_norm"]
    out = [loss, dx.reshape(x.shape)]
    for k in range(4):
        out += [res[name][k] for name in order]
    return tuple(out)
```

```python
import math

import jax
import jax.numpy as jnp
from jax import lax
from jax.experimental import pallas as pl
from jax.experimental.pallas import tpu as pltpu

F32 = jnp.float32
BF16 = jnp.bfloat16
MESH = pl.DeviceIdType.MESH
N_DEV = 8

D_MODEL = 1024
HEAD_DIM = 64
N_HEADS = 8
BLOCK = 128
ATTN_W = 512
KV_W = 128
CONV_W = 512
QKV_W = ATTN_W + 2 * KV_W
CBX_W = 3 * CONV_W
GATE_W = 2 * D_MODEL
IN_W = QKV_W + CBX_W + GATE_W
D_FF = 2816
FF_CHUNK = 1408
NORM_EPS = 1e-5
ATTN_SCALE = HEAD_DIM ** -0.5
NEG = -1e30
HALO = 16

ADAM_LR = 0.001
ADAM_B1 = 0.9
ADAM_B2 = 0.999
ADAM_EPS = 1e-08
ADAM_WD = 0.01
ADAM_STEP = 10

VMEM_LIMIT = 56 * 1024 * 1024
SMALL_ROWS = 32

NT = (((1,), (1,)), ((), ()))
TN = (((0,), (0,)), ((), ()))
ANY = pl.BlockSpec(memory_space=pl.ANY)


def _sig(v):
    return 1.0 / (1.0 + jnp.exp(-v))


def _row_tile(s, pref=256):
    return pref if s % pref == 0 else s


def _shifts_down(u, halo, ks):
    ext = jnp.concatenate([halo, u], axis=0)
    return [pltpu.roll(ext, k, axis=0)[HALO:, :] for k in ks]


def _shifts_up(u, halo, ks):
    n = u.shape[0]
    ext = jnp.concatenate([u, halo], axis=0)
    return [pltpu.roll(ext, n + HALO - k, axis=0)[:n, :] for k in ks]


def _shift_matrix(n, k):
    row = lax.broadcasted_iota(jnp.int32, (n, n), 0)
    col = lax.broadcasted_iota(jnp.int32, (n, n), 1)
    return jnp.where(col == row + k, 1.0, 0.0).astype(BF16)


def _mxu_shift_up(mat, ub, halo, k):
    n = ub.shape[0]
    v = jnp.dot(mat, ub, preferred_element_type=F32)
    row = lax.broadcasted_iota(jnp.int32, (8, ub.shape[1]), 0)
    tail = v[n - 8:, :]
    for t in range(k):
        tail = jnp.where(row == 8 - k + t, halo[t:t + 1, :], tail)
    return jnp.concatenate([v[:n - 8, :], tail], axis=0)


def _prev_halo_map(tm):
    return lambda i: (jnp.maximum(i * (tm // HALO) - 1, 0), 0)


def _next_halo_map(tm, s):
    return lambda i: (jnp.minimum((i + 1) * (tm // HALO), s // HALO - 1), 0)


def _full(shape):
    return pl.BlockSpec(shape, lambda *_: (0,) * len(shape))


def _resident(shape):
    return pl.BlockSpec(shape, lambda *_: (0,) * len(shape), pipeline_mode=pl.Buffered(1))


def _rows(tm, c):
    return pl.BlockSpec((tm, c), lambda i: (i, 0))


def _sds(shape, dtype):
    return jax.ShapeDtypeStruct(shape, dtype)


def _my_place():
    x, y, c = lax.axis_index("x"), lax.axis_index("y"), lax.axis_index("c")
    return x, y, c


ALL_PEERS = tuple((j >> 2, (j >> 1) & 1, j & 1) for j in range(1, N_DEV))
SIBLING_PEER = ((0, 0, 1),)
CHIP_PEERS = ((0, 1, 0), (1, 0, 0), (1, 1, 0))
BARRIER_ID = {ALL_PEERS: 0, SIBLING_PEER: 1, CHIP_PEERS: 2}


def _entry_barrier(peers):
    x, y, c = _my_place()
    barrier = pltpu.get_barrier_semaphore()
    for dx, dy, dc in peers:
        pl.semaphore_signal(barrier, inc=1, device_id=(x ^ dx, y ^ dy, c ^ dc), device_id_type=MESH)
    pl.semaphore_wait(barrier, len(peers))


def _start_exchange(remote, local):
    for cp in local + remote:
        cp.start()


def _finish_exchange(remote, local):
    for cp in remote:
        cp.wait_recv()
    for cp in remote:
        cp.wait_send()
    for cp in local:
        cp.wait()


class _AllGather:
    peers = ALL_PEERS

    def __init__(self, shards):
        self.ins = list(shards)
        n = len(shards)
        self.out_shape = [_sds((N_DEV * s.shape[0], s.shape[1]), s.dtype) for s in shards]
        self.sems = [pltpu.SemaphoreType.DMA((7 * n,)), pltpu.SemaphoreType.DMA((7 * n,)),
                     pltpu.SemaphoreType.DMA((n,))]

    def _parts(self, ins, outs, sems):
        send_sems, recv_sems, local_sems = sems
        x, y, c = _my_place()
        me, sibling = (x, y, c), (x, y, 1 - c)
        chips = [(1 - x, y), (x, 1 - y), (1 - x, 1 - y)]

        def rows(k, dev):
            r = ins[k].shape[0]
            start = pl.multiple_of((4 * dev[0] + 2 * dev[1] + dev[2]) * r, 8)
            return outs[k].at[pl.ds(start, r), :]

        def copy(k, j, block, to, src=None):
            return pltpu.make_async_remote_copy(
                src_ref=rows(k, block) if src is None else src, dst_ref=rows(k, block),
                send_sem=send_sems.at[7 * k + j], recv_sem=recv_sems.at[7 * k + j],
                device_id=to, device_id_type=MESH)

        n = len(ins)
        mine = [pltpu.make_async_copy(ins[k], rows(k, me), local_sems.at[k]) for k in range(n)]
        first = []
        for k in range(n):
            first.append(copy(k, 0, me, sibling, src=ins[k]))
            first += [copy(k, 1 + j, me, (*chip, c), src=ins[k]) for j, chip in enumerate(chips)]
        return me, sibling, chips, copy, mine, first

    def start(self, ins, outs, sems):
        _, _, _, _, mine, first = self._parts(ins, outs, sems)
        _start_exchange(first, mine)

    def finish(self, ins, outs, sems):
        me, sibling, chips, copy, mine, first = self._parts(ins, outs, sems)
        c = me[2]
        n = len(ins)
        passed = []
        for j, chip in enumerate(chips):
            for k in range(n):
                copy(k, 1 + j, (*chip, c), me).wait_recv()
                fwd = copy(k, 4 + j, (*chip, c), sibling)
                fwd.start()
                passed.append(fwd)
        for k in range(n):
            copy(k, 0, sibling, me).wait_recv()
            for j, chip in enumerate(chips):
                copy(k, 4 + j, (*chip, 1 - c), me).wait_recv()
        for cp in first + passed:
            cp.wait_send()
        for cp in mine:
            cp.wait()


class _ReduceScatter:
    peers = ALL_PEERS

    def __init__(self, parts, bcast=()):
        self.parts = [(lo, cnt) for _, lo, cnt in parts]
        self.n_parts = len(parts)
        self.ins = [a for a, _, _ in parts] + list(bcast)
        self.out_shape = [_sds((N_DEV * cnt, a.shape[1]), a.dtype) for a, _, cnt in parts]
        self.out_shape += [_sds((N_DEV * b.shape[0], b.shape[1]), b.dtype) for b in bcast]
        n = len(self.ins)
        self.sems = [pltpu.SemaphoreType.DMA((7 * n,)), pltpu.SemaphoreType.DMA((7 * n,)),
                     pltpu.SemaphoreType.DMA((n,))]

    def _copies(self, ins, outs, sems):
        send_sems, recv_sems, local_sems = sems
        x, y, c = _my_place()
        me_idx = 4 * x + 2 * y + c
        remote, local = [], []
        for k in range(len(ins)):
            cnt = outs[k].shape[0] // N_DEV
            dst = outs[k].at[pl.ds(pl.multiple_of(me_idx * cnt, 8), cnt), :]
            if k < self.n_parts:
                lo, _ = self.parts[k]
                r = ins[k].shape[0] // N_DEV
                src_of = lambda idx: ins[k].at[pl.ds(pl.multiple_of(idx * r + lo, 8), cnt), :]
            else:
                src_of = lambda idx: ins[k]
            local.append(pltpu.make_async_copy(src_of(me_idx), dst, local_sems.at[k]))
            for j in range(1, N_DEV):
                peer = (x ^ (j >> 2), y ^ ((j >> 1) & 1), c ^ (j & 1))
                peer_idx = 4 * peer[0] + 2 * peer[1] + peer[2]
                remote.append(pltpu.make_async_remote_copy(
                    src_ref=src_of(peer_idx), dst_ref=dst,
                    send_sem=send_sems.at[7 * k + j - 1], recv_sem=recv_sems.at[7 * k + j - 1],
                    device_id=peer, device_id_type=MESH))
        return remote, local

    def start(self, ins, outs, sems):
        _start_exchange(*self._copies(ins, outs, sems))

    def finish(self, ins, outs, sems):
        _finish_exchange(*self._copies(ins, outs, sems))


class _PairExchange:
    peers = SIBLING_PEER

    def __init__(self, arrays):
        self.ins = list(arrays)
        n = len(arrays)
        self.out_shape = [_sds((a.shape[0] // 2, a.shape[1]), a.dtype) for a in arrays]
        self.sems = [pltpu.SemaphoreType.DMA((4 * n,)), pltpu.SemaphoreType.DMA((4 * n,))]

    def _copies(self, ins, outs, sems):
        send_sems, recv_sems = sems
        x, y, c = _my_place()
        remote = []
        for k in range(len(ins)):
            r = ins[k].shape[0] // N_DEV
            for chip in range(4):
                sib = ins[k].at[pl.ds(pl.multiple_of((2 * chip + 1 - c) * r, 8), r), :]
                remote.append(pltpu.make_async_remote_copy(
                    src_ref=sib, dst_ref=outs[k].at[pl.ds(chip * r, r), :],
                    send_sem=send_sems.at[4 * k + chip], recv_sem=recv_sems.at[4 * k + chip],
                    device_id=(x, y, 1 - c), device_id_type=MESH))
        return remote

    def start(self, ins, outs, sems):
        for cp in self._copies(ins, outs, sems):
            cp.start()

    def finish(self, ins, outs, sems):
        remote = self._copies(ins, outs, sems)
        for cp in remote:
            cp.wait_recv()
        for cp in remote:
            cp.wait_send()


class _ChipExchange:
    peers = CHIP_PEERS

    def __init__(self, arrays):
        self.ins = list(arrays)
        self.out_shape = [_sds(a.shape, a.dtype) for a in arrays]
        n = len(self.ins)
        self.sems = [pltpu.SemaphoreType.DMA((3 * n,)), pltpu.SemaphoreType.DMA((3 * n,)),
                     pltpu.SemaphoreType.DMA((n,))]

    def _copies(self, ins, outs, sems):
        send_sems, recv_sems, local_sems = sems
        x, y, c = _my_place()
        my_chip = 2 * x + y
        remote, local = [], []
        for k in range(len(ins)):
            r = ins[k].shape[0] // 4
            dst = outs[k].at[pl.ds(pl.multiple_of(my_chip * r, 8), r), :]
            local.append(pltpu.make_async_copy(ins[k].at[pl.ds(pl.multiple_of(my_chip * r, 8), r), :], dst,
                                               local_sems.at[k]))
            for j in range(1, 4):
                px, py = x ^ (j >> 1), y ^ (j & 1)
                src = ins[k].at[pl.ds(pl.multiple_of((2 * px + py) * r, 8), r), :]
                remote.append(pltpu.make_async_remote_copy(
                    src_ref=src, dst_ref=dst, send_sem=send_sems.at[3 * k + j - 1],
                    recv_sem=recv_sems.at[3 * k + j - 1], device_id=(px, py, c), device_id_type=MESH))
        return remote, local

    def start(self, ins, outs, sems):
        _start_exchange(*self._copies(ins, outs, sems))

    def finish(self, ins, outs, sems):
        _finish_exchange(*self._copies(ins, outs, sems))


def _pcall(body, name, grid, in_specs, out_specs, out_shape, args, scratch=(), comm=None):
    params = pltpu.CompilerParams(dimension_semantics=("arbitrary",) * len(grid), vmem_limit_bytes=VMEM_LIMIT)
    in_specs, out_specs, out_shape, scratch = list(in_specs), list(out_specs), list(out_shape), list(scratch)
    if comm is None:
        res = pl.pallas_call(body, name=name, grid=grid, in_specs=in_specs, out_specs=out_specs, out_shape=out_shape,
                             scratch_shapes=scratch, compiler_params=params)(*args)
        return list(res), []
    n_in, n_out, n_scr = len(in_specs), len(out_specs), len(scratch)
    ci, co = len(comm.ins), len(comm.out_shape)
    total = math.prod(grid)

    def carried(*refs):
        bounds = [0, n_in, n_in + ci, n_in + ci + n_out, n_in + ci + n_out + co, n_in + ci + n_out + co + n_scr]
        ins, cins, outs, couts, scr = (refs[a:b] for a, b in zip(bounds[:-1], bounds[1:]))
        sems = refs[bounds[-1]:]
        step = pl.program_id(0)
        for d in range(1, len(grid)):
            step = step * grid[d] + pl.program_id(d)

        @pl.when(step == 0)
        def _():
            _entry_barrier(comm.peers)
            comm.start(cins, couts, sems)

        body(*ins, *outs, *scr)

        @pl.when(step == total - 1)
        def _():
            comm.finish(cins, couts, sems)

    params = pltpu.CompilerParams(dimension_semantics=("arbitrary",) * len(grid), vmem_limit_bytes=VMEM_LIMIT,
                                  collective_id=BARRIER_ID[comm.peers])
    res = pl.pallas_call(
        carried, name=name, grid=grid, in_specs=in_specs + [ANY] * ci, out_specs=out_specs + [ANY] * co,
        out_shape=out_shape + comm.out_shape, scratch_shapes=scratch + comm.sems, compiler_params=params,
    )(*args, *comm.ins)
    return list(res[:n_out]), list(res[n_out:])


def _exchange_only(comm, name):
    def body(*refs):
        ci, co = len(comm.ins), len(comm.out_shape)
        _entry_barrier(comm.peers)
        comm.start(refs[:ci], refs[ci:ci + co], refs[ci + co:])
        comm.finish(refs[:ci], refs[ci:ci + co], refs[ci + co:])

    params = pltpu.CompilerParams(collective_id=BARRIER_ID[comm.peers])
    return pl.pallas_call(body, name=name, out_shape=comm.out_shape, in_specs=[ANY] * len(comm.ins),
                          out_specs=[ANY] * len(comm.out_shape), scratch_shapes=comm.sems,
                          compiler_params=params)(*comm.ins)


def _norm_inproj(x, g, win_t, b_in, comm):
    s = x.shape[0]
    tm = _row_tile(s, 512)
    widths = (QKV_W, CBX_W, GATE_W)

    def body(x_ref, g_ref, w_ref, b_ref, xn_ref, qkv_ref, cbx_ref, gate_ref):
        xv = x_ref[...]
        r = lax.rsqrt(jnp.mean(xv * xv, axis=-1, keepdims=True) + NORM_EPS)
        xn = (xv * r * g_ref[...]).astype(BF16)
        xn_ref[...] = xn
        off = 0
        for o_ref, w in zip((qkv_ref, cbx_ref, gate_ref), widths):
            acc = lax.dot_general(xn, w_ref[off:off + w, :], NT, preferred_element_type=F32)
            o_ref[...] = (acc + b_ref[:, off:off + w]).astype(BF16)
            off += w

    return _pcall(
        body, "norm_inproj", (s // tm,),
        [_rows(tm, D_MODEL), _full((1, D_MODEL)), _resident((IN_W, D_MODEL)), _full((1, IN_W))],
        [_rows(tm, D_MODEL)] + [_rows(tm, w) for w in widths],
        [_sds((s, D_MODEL), BF16)] + [_sds((s, w), BF16) for w in widths],
        (x, g, win_t, b_in), comm=comm)


def _attn_specs():
    prev = lambda n: jnp.maximum(n - 1, 0)
    return [pl.BlockSpec((BLOCK, ATTN_W), lambda n: (n, 0)),
            pl.BlockSpec((BLOCK, KV_W), lambda n: (prev(n), ATTN_W // KV_W)),
            pl.BlockSpec((BLOCK, KV_W), lambda n: (n, ATTN_W // KV_W)),
            pl.BlockSpec((BLOCK, KV_W), lambda n: (prev(n), ATTN_W // KV_W + 1)),
            pl.BlockSpec((BLOCK, KV_W), lambda n: (n, ATTN_W // KV_W + 1))]


def _lower_lanes():
    return lax.broadcasted_iota(jnp.int32, (BLOCK, 128), 1) < HEAD_DIM


def _stack_heads(val, kh):
    lower = _lower_lanes()
    parts = []
    for g in range(4):
        h = kh * 4 + g
        blk = val[:, (h // 2) * 128:(h // 2 + 1) * 128]
        keep = lower if h % 2 == 0 else jnp.logical_not(lower)
        parts.append(jnp.where(keep, blk, jnp.zeros_like(blk)))
    return jnp.concatenate(parts, axis=0)


def _dup_kv(prev_ref, cur_ref, kh):
    t = jnp.concatenate([prev_ref[...], cur_ref[...]], axis=0).astype(F32)
    rolled = pltpu.roll(t, HEAD_DIM, axis=1)
    lower = lax.broadcasted_iota(jnp.int32, t.shape, 1) < HEAD_DIM
    dup = jnp.where(lower, t, rolled) if kh == 0 else jnp.where(lower, rolled, t)
    return dup.astype(BF16)


def _attn_mask(n):
    row = lax.broadcasted_iota(jnp.int32, (4 * BLOCK, 2 * BLOCK), 0)
    kj = lax.broadcasted_iota(jnp.int32, (4 * BLOCK, 2 * BLOCK), 1)
    dist = (row & (BLOCK - 1)) + BLOCK - kj
    band = jnp.logical_and(dist >= 0, dist < BLOCK)
    return jnp.logical_and(band, jnp.logical_or(kj >= BLOCK, n > 0))


def _sink_col(sinks_ref, kh):
    gi = lax.broadcasted_iota(jnp.int32, (4 * BLOCK, 1), 0) // BLOCK
    col = jnp.zeros((4 * BLOCK, 1), F32)
    for g in range(4):
        col = jnp.where(gi == g, sinks_ref[0, kh * 4 + g], col)
    return col


def _attn_fwd(qkv, sinks, comm):
    s = qkv.shape[0]

    def body(sinks_ref, q_ref, kp_ref, kc_ref, vp_ref, vc_ref, o_ref, lse_ref):
        n = pl.program_id(0)
        mask = _attn_mask(n)
        lower = _lower_lanes()
        lane = lax.broadcasted_iota(jnp.int32, (BLOCK, 128), 1)
        qv = q_ref[...]
        lse_out = jnp.zeros((BLOCK, 128), F32)
        for kh in range(2):
            qs = _stack_heads(qv, kh)
            kd, vd = _dup_kv(kp_ref, kc_ref, kh), _dup_kv(vp_ref, vc_ref, kh)
            sc = lax.dot_general(qs, kd, NT, preferred_element_type=F32) * ATTN_SCALE
            sc = jnp.where(mask, sc, NEG)
            sink = _sink_col(sinks_ref, kh)
            m = jnp.maximum(jnp.max(sc, axis=1, keepdims=True), sink)
            p = jnp.exp(sc - m)
            l = jnp.sum(p, axis=1, keepdims=True) + jnp.exp(sink - m)
            o = jnp.dot(p.astype(BF16), vd, preferred_element_type=F32) / l
            lse = m + jnp.log(l)
            for pair in range(2):
                lo = o[(2 * pair) * BLOCK:(2 * pair + 1) * BLOCK]
                hi = o[(2 * pair + 1) * BLOCK:(2 * pair + 2) * BLOCK]
                col = (kh * 2 + pair) * 128
                o_ref[:, col:col + 128] = jnp.where(lower, lo, hi).astype(BF16)
            for g in range(4):
                lse_out = jnp.where(lane == kh * 4 + g, lse[g * BLOCK:(g + 1) * BLOCK], lse_out)
        lse_ref[...] = lse_out

    return _pcall(
        body, "attn_fwd", (s // BLOCK,),
        [pl.BlockSpec(memory_space=pltpu.SMEM)] + _attn_specs(),
        [pl.BlockSpec((BLOCK, ATTN_W), lambda n: (n, 0)), pl.BlockSpec((BLOCK, 128), lambda n: (n, 0))],
        [_sds((s, ATTN_W), BF16), _sds((s, 128), F32)],
        (sinks, qkv, qkv, qkv, qkv, qkv), comm=comm)


def _conv_u(cbx_ref, halo_ref, w_ref, first):
    cb = cbx_ref[:, 0:CONV_W].astype(F32)
    cc = cbx_ref[:, CONV_W:2 * CONV_W].astype(F32)
    cx = cbx_ref[:, 2 * CONV_W:3 * CONV_W].astype(F32)
    u = cc * cx
    uh = halo_ref[:, CONV_W:2 * CONV_W].astype(F32) * halo_ref[:, 2 * CONV_W:3 * CONV_W].astype(F32)
    uh = jnp.where(first, 0.0, uh)
    u1, u2 = _shifts_down(u, uh, (1, 2))
    cv = w_ref[0:1, :] * u2 + w_ref[1:2, :] * u1 + w_ref[2:3, :] * u
    return cb, cc, cx, u, cv


def _mix_fwd(x, cbx, gates, attn, conv_w, wa, wc, wout, comm):
    s = x.shape[0]
    tm = _row_tile(s)

    def body(x_ref, cbx_ref, halo_ref, gate_ref, attn_ref, cw_ref, wa_ref, wc_ref, wo_ref,
             h1_ref):
        first = pl.program_id(0) == 0
        cb, _, _, _, cv = _conv_u(cbx_ref, halo_ref, cw_ref, first)
        conv = (cb * cv).astype(BF16)
        ap = jnp.dot(attn_ref[...], wa_ref[...], preferred_element_type=F32)
        cp = jnp.dot(conv, wc_ref[...], preferred_element_type=F32)
        ga = gate_ref[:, 0:D_MODEL].astype(F32)
        gc = gate_ref[:, D_MODEL:2 * D_MODEL].astype(F32)
        merged = (_sig(ga) * ap + _sig(gc) * cp).astype(BF16)
        h1_ref[...] = x_ref[...] + jnp.dot(merged, wo_ref[...], preferred_element_type=F32)

    return _pcall(
        body, "mix_fwd", (s // tm,),
        [_rows(tm, D_MODEL), _rows(tm, CBX_W), pl.BlockSpec((HALO, CBX_W), _prev_halo_map(tm)),
         _rows(tm, GATE_W), _rows(tm, ATTN_W), _full((3, CONV_W)), _full((ATTN_W, D_MODEL)),
         _full((CONV_W, D_MODEL)), _full((D_MODEL, D_MODEL))],
        [_rows(tm, D_MODEL)], [_sds((s, D_MODEL), F32)],
        (x, cbx, cbx, gates, attn, conv_w, wa, wc, wout), comm=comm)


def _ffn_up(h1, g, wup_lo, wup_hi, comm):
    s = h1.shape[0]
    tm = _row_tile(s, 512)
    half = D_MODEL // 2

    def body(h_ref, g_ref, wl_ref, wh_ref, hn_ref, up_ref):
        hv = h_ref[...]
        r = lax.rsqrt(jnp.mean(hv * hv, axis=-1, keepdims=True) + NORM_EPS)
        hn = (hv * r * g_ref[...]).astype(BF16)
        hn_ref[...] = hn
        for c in range(2 * D_FF // FF_CHUNK):
            sl = slice(c * FF_CHUNK, (c + 1) * FF_CHUNK)
            acc = lax.dot_general(hn[:, :half], wl_ref[sl, :], NT, preferred_element_type=F32)
            acc = acc + lax.dot_general(hn[:, half:], wh_ref[sl, :], NT, preferred_element_type=F32)
            up_ref[:, sl] = acc.astype(BF16)

    return _pcall(
        body, "ffn_up", (s // tm,),
        [_rows(tm, D_MODEL), _full((1, D_MODEL)), _resident((2 * D_FF, half)), _resident((2 * D_FF, half))],
        [_rows(tm, D_MODEL), _rows(tm, 2 * D_FF)],
        [_sds((s, D_MODEL), BF16), _sds((s, 2 * D_FF), BF16)],
        (h1, g, wup_lo, wup_hi), comm=comm)


def _ffn_conv_cols(up_ref, halo_ref, fcw_ref, first, off):
    u = up_ref[:, off:off + FF_CHUNK].astype(F32)
    uh = jnp.where(first, 0.0, halo_ref[:, off:off + FF_CHUNK].astype(F32))
    w = fcw_ref[:, off:off + FF_CHUNK]
    u1, u2 = _shifts_down(u, uh, (1, 2))
    return w[0:1] * u2 + w[1:2] * u1 + w[2:3] * u


def _ffn_down_loss(up_pre, fcw, wdown, h1, fnorm, target):
    s = h1.shape[0]
    tm = _row_tile(s)

    def body(up_ref, halo_ref, fcw_ref, wd_ref, h1_ref, fn_ref, t_ref, cu_ref, act_ref, dh2_ref, loss_ref, dfn_ref):
        i = pl.program_id(0)

        @pl.when(i == 0)
        def _():
            loss_ref[...] = jnp.zeros_like(loss_ref)
            dfn_ref[...] = jnp.zeros_like(dfn_ref)

        h2 = h1_ref[...]
        for c in range(D_FF // FF_CHUNK):
            gsl = slice(c * FF_CHUNK, (c + 1) * FF_CHUNK)
            vsl = slice(D_FF + c * FF_CHUNK, D_FF + (c + 1) * FF_CHUNK)
            gate = _ffn_conv_cols(up_ref, halo_ref, fcw_ref, i == 0, c * FF_CHUNK)
            cu_ref[:, gsl] = gate.astype(BF16)
            val = _ffn_conv_cols(up_ref, halo_ref, fcw_ref, i == 0, D_FF + c * FF_CHUNK)
            cu_ref[:, vsl] = val.astype(BF16)
            act = (gate * _sig(gate) * val).astype(BF16)
            act_ref[:, gsl] = act
            h2 = h2 + jnp.dot(act, wd_ref[gsl, :], preferred_element_type=F32)
        r = lax.rsqrt(jnp.mean(h2 * h2, axis=-1, keepdims=True) + NORM_EPS)
        yhat = h2 * r
        fn = fn_ref[...]
        diff = yhat * fn - t_ref[...]
        loss_ref[...] += 0.5 * jnp.sum(jnp.sum(diff * diff, axis=1, keepdims=True), axis=0, keepdims=True) / D_MODEL
        dy = diff * (1.0 / D_MODEL)
        dfn_ref[...] += jnp.sum(dy * yhat, axis=0, keepdims=True)
        dyh = dy * fn
        dh2_ref[...] = r * (dyh - yhat * jnp.mean(dyh * yhat, axis=-1, keepdims=True))

    return _pcall(
        body, "ffn_down_loss", (s // tm,),
        [_rows(tm, 2 * D_FF), pl.BlockSpec((HALO, 2 * D_FF), _prev_halo_map(tm)), _full((3, 2 * D_FF)),
         _resident((D_FF, D_MODEL)), _rows(tm, D_MODEL), _full((1, D_MODEL)), _rows(tm, D_MODEL)],
        [_rows(tm, 2 * D_FF), _rows(tm, D_FF), _rows(tm, D_MODEL), _full((1, 128)), _full((1, D_MODEL))],
        [_sds((s, 2 * D_FF), BF16), _sds((s, D_FF), BF16), _sds((s, D_MODEL), F32), _sds((1, 128), F32),
         _sds((1, D_MODEL), F32)],
        (up_pre, up_pre, fcw, wdown, h1, fnorm, target))[0]


def _ffn_bwd(dh2, wdown, up, up_pre, fcw, wup_lo, wup_hi, h1, g, comm):
    s = dh2.shape[0]
    tm = _row_tile(s)
    half = D_MODEL // 2

    def dup_cols(dh, up_ref, wd_ref, c):
        gsl = slice(c * FF_CHUNK, (c + 1) * FF_CHUNK)
        vsl = slice(D_FF + c * FF_CHUNK, D_FF + (c + 1) * FF_CHUNK)
        dact = lax.dot_general(dh, wd_ref[gsl, :], NT, preferred_element_type=F32)
        gate = up_ref[:, gsl].astype(F32)
        val = up_ref[:, vsl].astype(F32)
        sg = _sig(gate)
        return dact * val * (sg * (1.0 + gate * (1.0 - sg))), dact * gate * sg

    def body(dh_ref, dhn_ref, wd_ref, up_ref, upn_ref, x_ref, w_ref, wl_ref, wh_ref, h_ref, g_ref,
             dx_ref, dw_ref, dh1_ref, dg_ref):
        i = pl.program_id(0)

        @pl.when(i == 0)
        def _():
            dw_ref[...] = jnp.zeros_like(dw_ref)
            dg_ref[...] = jnp.zeros_like(dg_ref)

        last = i == s // tm - 1
        dh2v = dh_ref[...]
        dh = dh2v.astype(BF16)
        dhn = dhn_ref[...].astype(BF16)
        dhn_lo = jnp.zeros((tm, half), F32)
        dhn_hi = jnp.zeros((tm, half), F32)
        for c in range(D_FF // FF_CHUNK):
            halves = zip(dup_cols(dh, up_ref, wd_ref, c), dup_cols(dhn, upn_ref, wd_ref, c),
                         (c * FF_CHUNK, D_FF + c * FF_CHUNK))
            for d, dn, off in halves:
                sl = slice(off, off + FF_CHUNK)
                dn = jnp.where(last, 0.0, dn)
                xv = x_ref[:, sl].astype(F32)
                wv = w_ref[:, sl]
                d1, d2 = _shifts_up(d, dn, (1, 2))
                dx = (wv[2:3] * d + wv[1:2] * d1 + wv[0:1] * d2).astype(BF16)
                dx_ref[:, sl] = dx
                dhn_lo = dhn_lo + jnp.dot(dx, wl_ref[sl, :], preferred_element_type=F32)
                dhn_hi = dhn_hi + jnp.dot(dx, wh_ref[sl, :], preferred_element_type=F32)
                dw_ref[0:1, sl] += jnp.sum(d2 * xv, axis=0, keepdims=True)
                dw_ref[1:2, sl] += jnp.sum(d1 * xv, axis=0, keepdims=True)
                dw_ref[2:3, sl] += jnp.sum(d * xv, axis=0, keepdims=True)
        dx1, dg = _norm_bwd_tile(h_ref[...], g_ref[...], jnp.concatenate([dhn_lo, dhn_hi], axis=1))
        dg_ref[...] += dg
        dh1_ref[...] = dh2v + dx1

    return _pcall(
        body, "ffn_bwd", (s // tm,),
        [_rows(tm, D_MODEL), pl.BlockSpec((HALO, D_MODEL), _next_halo_map(tm, s)), _resident((D_FF, D_MODEL)),
         _rows(tm, 2 * D_FF), pl.BlockSpec((HALO, 2 * D_FF), _next_halo_map(tm, s)), _rows(tm, 2 * D_FF),
         _full((3, 2 * D_FF)), _resident((2 * D_FF, half)), _resident((2 * D_FF, half)), _rows(tm, D_MODEL),
         _full((1, D_MODEL))],
        [_rows(tm, 2 * D_FF), _full((3, 2 * D_FF)), _rows(tm, D_MODEL), _full((1, D_MODEL))],
        [_sds((s, 2 * D_FF), BF16), _sds((3, 2 * D_FF), F32), _sds((s, D_MODEL), F32), _sds((1, D_MODEL), F32)],
        (dh2, dh2, wdown, up, up, up_pre, fcw, wup_lo, wup_hi, h1, g), comm=comm)


def _matmul_tn(a, b, tk, name, ts=1024, comm=None):
    s, ka = a.shape
    n = b.shape[1]
    ts = min(ts, s)
    steps = s // ts

    def body(a_ref, b_ref, o_ref, acc_ref):
        j = pl.program_id(1)

        @pl.when(j == 0)
        def _():
            acc_ref[...] = jnp.zeros_like(acc_ref)

        acc_ref[...] += lax.dot_general(a_ref[...].astype(BF16), b_ref[...].astype(BF16), TN,
                                        preferred_element_type=F32)

        @pl.when(j == steps - 1)
        def _():
            o_ref[...] = acc_ref[...].astype(BF16)

    outs, couts = _pcall(
        body, name, (ka // tk, steps),
        [pl.BlockSpec((ts, tk), lambda i, j: (j, i)), pl.BlockSpec((ts, n), lambda i, j: (j, 0))],
        [pl.BlockSpec((tk, n), lambda i, j: (i, 0))], [_sds((ka, n), BF16)],
        (a, b), scratch=[pltpu.VMEM((tk, n), F32)], comm=comm)
    return outs[0] if comm is None else (outs[0], couts)


def _norm_bwd_tile(xv, g, dy):
    r = lax.rsqrt(jnp.mean(xv * xv, axis=-1, keepdims=True) + NORM_EPS)
    xhat = xv * r
    dg = jnp.sum(dy * xhat, axis=0, keepdims=True)
    dyh = dy * g
    return r * (dyh - xhat * jnp.mean(dyh * xhat, axis=-1, keepdims=True)), dg


def _ffn_up_bwd(dup_pre, wup_lo, wup_hi, h1, g, dh2, comm):
    s = h1.shape[0]
    tm = _row_tile(s, 512)
    half = D_MODEL // 2

    def body(du_ref, wl_ref, wh_ref, h_ref, g_ref, dh2_ref, dh1_ref, dg_ref):
        @pl.when(pl.program_id(0) == 0)
        def _():
            dg_ref[...] = jnp.zeros_like(dg_ref)

        du = du_ref[...]
        dhn = jnp.concatenate([jnp.dot(du, wl_ref[...], preferred_element_type=F32),
                               jnp.dot(du, wh_ref[...], preferred_element_type=F32)], axis=1)
        dx, dg = _norm_bwd_tile(h_ref[...], g_ref[...], dhn)
        dg_ref[...] += dg
        dh1_ref[...] = dh2_ref[...] + dx

    return _pcall(
        body, "ffn_up_bwd", (s // tm,),
        [_rows(tm, 2 * D_FF), _resident((2 * D_FF, half)), _resident((2 * D_FF, half)), _rows(tm, D_MODEL),
         _full((1, D_MODEL)), _rows(tm, D_MODEL)],
        [_rows(tm, D_MODEL), _full((1, D_MODEL))],
        [_sds((s, D_MODEL), F32), _sds((1, D_MODEL), F32)],
        (dup_pre, wup_lo, wup_hi, h1, g, dh2), comm=comm)


def _mix_bwd(dh1, wout, gates, attn, wa, wc, cbx, conv_w, comm):
    s = dh1.shape[0]
    tm = _row_tile(s)
    steps = s // tm

    def body(dh_ref, wo_ref, gate_ref, attn_ref, wa_ref, wc_ref, cbx_ref, halo_ref, cw_ref,
             dg_ref, dattn_ref, dcb_ref, dcv_ref, gwo_ref, gwa_ref, gwc_ref, acc_o, acc_a, acc_c):
        i = pl.program_id(0)

        @pl.when(i == 0)
        def _():
            acc_o[...] = jnp.zeros_like(acc_o)
            acc_a[...] = jnp.zeros_like(acc_a)
            acc_c[...] = jnp.zeros_like(acc_c)

        cb, _, _, _, cv = _conv_u(cbx_ref, halo_ref, cw_ref, i == 0)
        attn = attn_ref[...]
        conv = (cb * cv).astype(BF16)
        ap = jnp.dot(attn, wa_ref[...], preferred_element_type=F32)
        cp = jnp.dot(conv, wc_ref[...], preferred_element_type=F32)
        dhb = dh_ref[...].astype(BF16)
        dm = lax.dot_general(dhb, wo_ref[...], NT, preferred_element_type=F32)
        sa = _sig(gate_ref[:, 0:D_MODEL].astype(F32))
        sc = _sig(gate_ref[:, D_MODEL:2 * D_MODEL].astype(F32))
        merged = (sa * ap + sc * cp).astype(BF16)
        da = (dm * sa).astype(BF16)
        dc = (dm * sc).astype(BF16)
        dg_ref[:, 0:D_MODEL] = (dm * ap * sa * (1.0 - sa)).astype(BF16)
        dg_ref[:, D_MODEL:2 * D_MODEL] = (dm * cp * sc * (1.0 - sc)).astype(BF16)
        dattn_ref[...] = lax.dot_general(da, wa_ref[...], NT, preferred_element_type=F32).astype(BF16)
        dconv = lax.dot_general(dc, wc_ref[...], NT, preferred_element_type=F32)
        dcb_ref[...] = (dconv * cv).astype(BF16)
        dcv_ref[...] = (dconv * cb).astype(BF16)
        acc_o[...] += lax.dot_general(merged, dhb, TN, preferred_element_type=F32)
        acc_a[...] += lax.dot_general(attn, da, TN, preferred_element_type=F32)
        acc_c[...] += lax.dot_general(conv, dc, TN, preferred_element_type=F32)

        @pl.when(i == steps - 1)
        def _():
            gwo_ref[...] = acc_o[...].astype(BF16)
            gwa_ref[...] = acc_a[...].astype(BF16)
            gwc_ref[...] = acc_c[...].astype(BF16)

    return _pcall(
        body, "mix_bwd", (steps,),
        [_rows(tm, D_MODEL), _full((D_MODEL, D_MODEL)), _rows(tm, GATE_W), _rows(tm, ATTN_W),
         _full((ATTN_W, D_MODEL)), _full((CONV_W, D_MODEL)), _rows(tm, CBX_W),
         pl.BlockSpec((HALO, CBX_W), _prev_halo_map(tm)), _full((3, CONV_W))],
        [_rows(tm, GATE_W), _rows(tm, ATTN_W), _rows(tm, CONV_W), _rows(tm, CONV_W),
         _full((D_MODEL, D_MODEL)), _full((ATTN_W, D_MODEL)), _full((CONV_W, D_MODEL))],
        [_sds((s, GATE_W), BF16), _sds((s, ATTN_W), BF16), _sds((s, CONV_W), BF16), _sds((s, CONV_W), BF16),
         _sds((D_MODEL, D_MODEL), BF16), _sds((ATTN_W, D_MODEL), BF16), _sds((CONV_W, D_MODEL), BF16)],
        (dh1, wout, gates, attn, wa, wc, cbx, cbx, conv_w),
        scratch=[pltpu.VMEM((D_MODEL, D_MODEL), F32), pltpu.VMEM((ATTN_W, D_MODEL), F32),
                 pltpu.VMEM((CONV_W, D_MODEL), F32)], comm=comm)


def _conv_branch_bwd(dcv, cbx, conv_w):
    s = dcv.shape[0]
    tm = _row_tile(s)

    def body(d_ref, dn_ref, cbx_ref, w_ref, dcc_ref, dcx_ref, dw_ref):
        i = pl.program_id(0)

        @pl.when(i == 0)
        def _():
            dw_ref[...] = jnp.zeros_like(dw_ref)

        last = i == s // tm - 1
        cc = cbx_ref[:, CONV_W:2 * CONV_W].astype(F32)
        cx = cbx_ref[:, 2 * CONV_W:3 * CONV_W].astype(F32)
        u = cc * cx
        d = d_ref[...].astype(F32)
        dn = jnp.where(last, 0.0, dn_ref[...].astype(F32))
        d1, d2 = _shifts_up(d, dn, (1, 2))
        du = w_ref[2:3, :] * d + w_ref[1:2, :] * d1 + w_ref[0:1, :] * d2
        dcc_ref[...] = (du * cx).astype(BF16)
        dcx_ref[...] = (du * cc).astype(BF16)
        dw_ref[0:1, :] += jnp.sum(d2 * u, axis=0, keepdims=True)
        dw_ref[1:2, :] += jnp.sum(d1 * u, axis=0, keepdims=True)
        dw_ref[2:3, :] += jnp.sum(d * u, axis=0, keepdims=True)

    return _pcall(
        body, "conv_branch_bwd", (s // tm,),
        [_rows(tm, CONV_W), pl.BlockSpec((HALO, CONV_W), _next_halo_map(tm, s)), _rows(tm, CBX_W),
         _full((3, CONV_W))],
        [_rows(tm, CONV_W), _rows(tm, CONV_W), _full((3, CONV_W))],
        [_sds((s, CONV_W), BF16), _sds((s, CONV_W), BF16), _sds((3, CONV_W), F32)],
        (dcv, dcv, cbx, conv_w))[0]


def _attn_bwd(qkv, sinks, attn, lse, dattn, comm):
    s = qkv.shape[0]

    def body(sinks_ref, q_ref, kp_ref, kc_ref, vp_ref, vc_ref, o_ref, lse_ref, do_ref,
             dq_ref, dk_ref, dv_ref, ds_ref):
        n = pl.program_id(0)

        @pl.when(n == 0)
        def _():
            dk_ref[...] = jnp.zeros_like(dk_ref)
            dv_ref[...] = jnp.zeros_like(dv_ref)
            ds_ref[...] = jnp.zeros_like(ds_ref)

        mask = _attn_mask(n)
        lower = _lower_lanes()
        lane = lax.broadcasted_iota(jnp.int32, (BLOCK, 128), 1)
        lower2 = lax.broadcasted_iota(jnp.int32, (2 * BLOCK, 128), 1) < HEAD_DIM
        lane1 = lax.broadcasted_iota(jnp.int32, (1, 128), 1)
        qv, ov, dov, lsev = q_ref[...], o_ref[...], do_ref[...], lse_ref[...]
        dk_fold, dv_fold = [], []
        dsink = jnp.zeros((1, 128), F32)
        for kh in range(2):
            qs = _stack_heads(qv, kh)
            dos = _stack_heads(dov, kh)
            os_ = _stack_heads(ov, kh)
            kd, vd = _dup_kv(kp_ref, kc_ref, kh), _dup_kv(vp_ref, vc_ref, kh)
            lse = jnp.concatenate(
                [jnp.sum(jnp.where(lane == kh * 4 + g, lsev, 0.0), axis=1, keepdims=True) for g in range(4)], axis=0)
            sc = lax.dot_general(qs, kd, NT, preferred_element_type=F32) * ATTN_SCALE
            p = jnp.exp(jnp.where(mask, sc, NEG) - lse)
            dp = lax.dot_general(dos, vd, NT, preferred_element_type=F32)
            delta = jnp.sum(dos.astype(F32) * os_.astype(F32), axis=1, keepdims=True)
            dsc = (p * (dp - delta) * ATTN_SCALE).astype(BF16)
            dqs = jnp.dot(dsc, kd, preferred_element_type=F32)
            for pair in range(2):
                lo = dqs[(2 * pair) * BLOCK:(2 * pair + 1) * BLOCK]
                hi = dqs[(2 * pair + 1) * BLOCK:(2 * pair + 2) * BLOCK]
                col = (kh * 2 + pair) * 128
                dq_ref[:, col:col + 128] = jnp.where(lower, lo, hi).astype(BF16)
            dkd = lax.dot_general(dsc, qs, TN, preferred_element_type=F32)
            dvd = lax.dot_general(p.astype(BF16), dos, TN, preferred_element_type=F32)
            dk_fold.append(dkd + pltpu.roll(dkd, HEAD_DIM, axis=1))
            dv_fold.append(dvd + pltpu.roll(dvd, HEAD_DIM, axis=1))
            psink = jnp.exp(_sink_col(sinks_ref, kh) - lse) * delta
            for g in range(4):
                tot = jnp.sum(psink[g * BLOCK:(g + 1) * BLOCK], axis=0, keepdims=True)
                dsink = dsink - jnp.where(lane1 == kh * 4 + g, tot, 0.0)
        dk2 = jnp.where(lower2, dk_fold[0], dk_fold[1])
        dv2 = jnp.where(lower2, dv_fold[0], dv_fold[1])
        ds_ref[...] += dsink
        cur = pl.ds(pl.multiple_of(n * BLOCK, BLOCK), BLOCK)
        dk_ref[cur, :] += dk2[BLOCK:]
        dv_ref[cur, :] += dv2[BLOCK:]

        @pl.when(n > 0)
        def _():
            prev = pl.ds(pl.multiple_of((n - 1) * BLOCK, BLOCK), BLOCK)
            dk_ref[prev, :] += dk2[:BLOCK]
            dv_ref[prev, :] += dv2[:BLOCK]

    blk = lambda w: pl.BlockSpec((BLOCK, w), lambda n: (n, 0))
    return _pcall(
        body, "attn_bwd", (s // BLOCK,),
        [pl.BlockSpec(memory_space=pltpu.SMEM)] + _attn_specs() + [blk(ATTN_W), blk(128), blk(ATTN_W)],
        [blk(ATTN_W), _full((s, KV_W)), _full((s, KV_W)), _full((1, 128))],
        [_sds((s, ATTN_W), BF16), _sds((s, KV_W), F32), _sds((s, KV_W), F32), _sds((1, 128), F32)],
        (sinks, qkv, qkv, qkv, qkv, qkv, attn, lse, dattn), comm=comm)


DPROJ_PIECES = (ATTN_W, KV_W, KV_W, CONV_W, CONV_W, CONV_W, GATE_W)
DPROJ_OFFSETS = tuple(sum(DPROJ_PIECES[:k]) for k in range(len(DPROJ_PIECES)))


def _grad_w_in(pieces, xn, comm):
    s = xn.shape[0]
    ts = min(1024, s)
    steps = s // ts
    rows0 = DPROJ_OFFSETS[6]

    def body(*refs):
        p_refs, b_ref, o_ref, acc_ref, stage_ref, sem = refs[:7], refs[7], refs[8], refs[9], refs[10], refs[11]
        i, j = pl.program_id(0), pl.program_id(1)

        @pl.when(j == 0)
        def _():
            acc_ref[...] = jnp.zeros_like(acc_ref)

        bv = b_ref[...]

        def flush(lo, n):
            stage_ref[0:n, :] = acc_ref[0:n, :].astype(BF16)
            cp = pltpu.make_async_copy(stage_ref.at[0:n, :], o_ref.at[lo:lo + n, :], sem)
            cp.start()
            cp.wait()

        @pl.when(i == 0)
        def _():
            for p_ref, off, w in zip(p_refs[:6], DPROJ_OFFSETS[:6], DPROJ_PIECES[:6]):
                acc_ref[off:off + w, :] += lax.dot_general(p_ref[...].astype(BF16), bv, TN,
                                                           preferred_element_type=F32)

            @pl.when(j == steps - 1)
            def _():
                flush(0, rows0)

        @pl.when(i == 1)
        def _():
            acc_ref[0:GATE_W, :] += lax.dot_general(p_refs[6][...], bv, TN, preferred_element_type=F32)

            @pl.when(j == steps - 1)
            def _():
                flush(rows0, GATE_W)

    def piece_spec(w, group):
        return pl.BlockSpec((ts, w), lambda i, j: (jnp.where(i == group, j, 0), 0))

    outs, couts = _pcall(
        body, "grad_w_in", (2, steps),
        [piece_spec(w, 0) for w in DPROJ_PIECES[:6]] + [piece_spec(GATE_W, 1),
                                                         pl.BlockSpec((ts, D_MODEL), lambda i, j: (j, 0))],
        [ANY], [_sds((IN_W, D_MODEL), BF16)], (*pieces, xn),
        scratch=[pltpu.VMEM((rows0, D_MODEL), F32), pltpu.VMEM((rows0, D_MODEL), BF16), pltpu.SemaphoreType.DMA],
        comm=comm)
    return outs[0], couts


def _inproj_bwd(pieces, win_t, x, g, dh1, comm):
    s = x.shape[0]
    tm = _row_tile(s, 512)

    def body(*refs):
        p_refs = refs[:7]
        w_ref, x_ref, g_ref, dh_ref, dx_ref, db_ref, dg_ref = refs[7:]

        @pl.when(pl.program_id(0) == 0)
        def _():
            db_ref[...] = jnp.zeros_like(db_ref)
            dg_ref[...] = jnp.zeros_like(dg_ref)

        dxn = jnp.zeros((tm, D_MODEL), F32)
        for p_ref, off, w in zip(p_refs, DPROJ_OFFSETS, DPROJ_PIECES):
            v = p_ref[...].astype(BF16)
            db_ref[:, off:off + w] += jnp.sum(v.astype(F32), axis=0, keepdims=True)
            dxn = dxn + jnp.dot(v, w_ref[off:off + w, :], preferred_element_type=F32)
        dx, dg = _norm_bwd_tile(x_ref[...], g_ref[...], dxn)
        dg_ref[...] += dg
        dx_ref[...] = dh_ref[...] + dx

    return _pcall(
        body, "inproj_bwd", (s // tm,),
        [_rows(tm, w) for w in DPROJ_PIECES] + [_resident((IN_W, D_MODEL)), _rows(tm, D_MODEL), _full((1, D_MODEL)),
                                                _rows(tm, D_MODEL)],
        [_rows(tm, D_MODEL), _full((1, IN_W)), _full((1, D_MODEL))],
        [_sds((s, D_MODEL), F32), _sds((1, IN_W), F32), _sds((1, D_MODEL), F32)],
        (*pieces, win_t, x, g, dh1), comm=comm)


def _adam_math(w, g, m, v):
    m2 = ADAM_B1 * m + (1.0 - ADAM_B1) * g
    v2 = ADAM_B2 * v + (1.0 - ADAM_B2) * (g * g)
    m_hat = m2 / (1.0 - ADAM_B1 ** ADAM_STEP)
    v_hat = v2 / (1.0 - ADAM_B2 ** ADAM_STEP)
    delta = -ADAM_LR * (m_hat / (jnp.sqrt(v_hat) + ADAM_EPS) + ADAM_WD * w)
    return delta, m2, v2


def _sum_slots(ref):
    tot = ref[0].astype(F32)
    for i in range(1, ref.shape[0]):
        tot = tot + ref[i].astype(F32)
    return tot


def _pair_add(partials, theirs, tr, name):
    r = partials.shape[0] // N_DEV
    c = partials.shape[1]
    nt = r // tr
    core = lax.axis_index("c").astype(jnp.int32).reshape(1)

    def body(core_ref, a_ref, b_ref, o_ref):
        o_ref[...] = (a_ref[...].astype(F32) + b_ref[...].astype(F32)).astype(BF16)

    grid_spec = pltpu.PrefetchScalarGridSpec(
        num_scalar_prefetch=1, grid=(4 * nt,),
        in_specs=[pl.BlockSpec((None, None, tr, c), lambda i, core_ref: (i // nt, core_ref[0], i % nt, 0)),
                  pl.BlockSpec((tr, c), lambda i, core_ref: (i, 0))],
        out_specs=pl.BlockSpec((tr, c), lambda i, core_ref: (i, 0)))
    return pl.pallas_call(body, name=name, grid_spec=grid_spec, out_shape=_sds((4 * r, c), BF16))(
        core, partials.reshape(4, 2, r, c), theirs)


def _sum_adamw(parts, w, m, v, tr, name):
    r, c = w.shape

    def body(p_ref, w_ref, m_ref, v_ref, g_ref, d_ref, m2_ref, v2_ref):
        g = _sum_slots(p_ref)
        g_ref[...] = g
        d_ref[...], m2_ref[...], v2_ref[...] = _adam_math(w_ref[...], g, m_ref[...], v_ref[...])

    spec = pl.BlockSpec((tr, c), lambda i: (i, 0))
    return _pcall(body, name, (r // tr,), [pl.BlockSpec((N_DEV, tr, c), lambda i: (0, i, 0)), spec, spec, spec],
                  [spec] * 4, [_sds((r, c), F32)] * 4, (parts, w, m, v))[0]


def _sum_parts_adamw(parts, w, m, v, tr, name):
    c = w.shape[1]
    tiles = [p.shape[1] // tr for p in parts]
    starts = [sum(tiles[:k]) for k in range(len(parts))]
    n_parts = len(parts)

    def body(*refs):
        p_refs = refs[:n_parts]
        w_ref, m_ref, v_ref, g_ref, d_ref, m2_ref, v2_ref = refs[n_parts:]
        i = pl.program_id(0)
        for p_ref, st, nt in zip(p_refs, starts, tiles):
            @pl.when(jnp.logical_and(i >= st, i < st + nt))
            def _(p_ref=p_ref):
                g_ref[...] = _sum_slots(p_ref)

        d_ref[...], m2_ref[...], v2_ref[...] = _adam_math(w_ref[...], g_ref[...], m_ref[...], v_ref[...])

    def part_spec(p, st, nt):
        return pl.BlockSpec((p.shape[0], tr, c), lambda i: (0, jnp.clip(i - st, 0, nt - 1), 0))

    spec = pl.BlockSpec((tr, c), lambda i: (i, 0))
    return _pcall(
        body, name, (sum(tiles),),
        [part_spec(p, st, nt) for p, st, nt in zip(parts, starts, tiles)] + [spec, spec, spec],
        [spec] * 4, [_sds(w.shape, F32)] * 4, (*parts, w, m, v))[0]


ROW_MIX, ROW_FFN, ROW_FINAL, ROW_SINKS, ROW_LOSS, ROW_BIN, ROW_CW, ROW_FCW = 0, 1, 2, 3, 4, 5, 10, 13
FCW_ROWS = 6


def _wide_pieces(width):
    return [(k * D_MODEL, min(D_MODEL, width - k * D_MODEL)) for k in range(-(-width // D_MODEL))]


def _pack_small(dmix, dffn, dfn, dsink, loss, dbin, dcw, dfcw):
    def body(mix_ref, ffn_ref, fn_ref, sink_ref, loss_ref, bin_ref, cw_ref, fcw_ref, o_ref):
        o_ref[...] = jnp.zeros_like(o_ref)
        o_ref[ROW_MIX:ROW_MIX + 1, :] = mix_ref[...]
        o_ref[ROW_FFN:ROW_FFN + 1, :] = ffn_ref[...]
        o_ref[ROW_FINAL:ROW_FINAL + 1, :] = fn_ref[...]
        o_ref[ROW_SINKS:ROW_SINKS + 1, 0:128] = sink_ref[...]
        o_ref[ROW_LOSS:ROW_LOSS + 1, 0:128] = loss_ref[...]
        for k, (off, w) in enumerate(_wide_pieces(IN_W)):
            o_ref[ROW_BIN + k:ROW_BIN + k + 1, 0:w] = bin_ref[:, off:off + w]
        o_ref[ROW_CW:ROW_CW + 3, 0:CONV_W] = cw_ref[...]
        for a in range(3):
            for k, (off, w) in enumerate(_wide_pieces(2 * D_FF)):
                row = ROW_FCW + FCW_ROWS * a + k
                o_ref[row:row + 1, 0:w] = fcw_ref[a:a + 1, off:off + w]

    return pl.pallas_call(body, name="pack_small", out_shape=_sds((SMALL_ROWS, D_MODEL), F32))(
        dmix, dffn, dfn, dsink, loss, dbin, dcw, dfcw)


def _small_sums_adamw(r_small, params):
    rows = (ROW_MIX, ROW_BIN, ROW_SINKS, ROW_FFN, ROW_FINAL)

    def body(*refs):
        r_ref, p_refs, o_refs = refs[0], refs[1:16], refs[16:]
        tot = _sum_slots(r_ref)
        for k, row in enumerate(rows):
            w_ref, m_ref, v_ref = p_refs[3 * k:3 * k + 3]
            g_ref, d_ref, m2_ref, v2_ref = o_refs[4 * k:4 * k + 4]
            for j, (off, w) in enumerate(_wide_pieces(w_ref.shape[1])):
                g_ref[:, off:off + w] = tot[row + j:row + j + 1, 0:w]
            d_ref[...], m2_ref[...], v2_ref[...] = _adam_math(w_ref[...], g_ref[...], m_ref[...], v_ref[...])
        cw_ref, fcw_ref, loss_ref = o_refs[20:]
        cw_ref[...] = tot[ROW_CW:ROW_CW + 3, 0:CONV_W]
        for a in range(3):
            for j, (off, w) in enumerate(_wide_pieces(2 * D_FF)):
                row = ROW_FCW + FCW_ROWS * a + j
                fcw_ref[a:a + 1, off:off + w] = tot[row:row + 1, 0:w]
        loss_ref[...] = tot[ROW_LOSS:ROW_LOSS + 1, 0:128]

    flat = [t for p in params for t in p]
    out_shape = [_sds(p[0].shape, F32) for p in params for _ in range(4)]
    out_shape += [_sds((3, CONV_W), F32), _sds((3, 2 * D_FF), F32), _sds((1, 128), F32)]
    res = pl.pallas_call(body, name="small_sums_adamw", out_shape=out_shape)(r_small, *flat)
    return [tuple(res[4 * k:4 * k + 4]) for k in range(5)], res[20], res[21], res[22]


def _adamw_pair(a, b):
    def body(*refs):
        for k in range(2):
            w_ref, g_ref, m_ref, v_ref = refs[4 * k:4 * k + 4]
            d_ref, m2_ref, v2_ref = refs[8 + 3 * k:8 + 3 * k + 3]
            d_ref[...], m2_ref[...], v2_ref[...] = _adam_math(w_ref[...], g_ref[...], m_ref[...], v_ref[...])

    out_shape = [_sds(a[0].shape, F32)] * 3 + [_sds(b[0].shape, F32)] * 3
    res = pl.pallas_call(body, name="adamw_conv_weights", out_shape=out_shape)(*a, *b)
    return tuple(res[:3]), tuple(res[3:])


def _pad_cols(a, c):
    return jnp.pad(a, ((0, 0), (0, c - a.shape[1])))


def _to_col_slabs(g):
    r = g.shape[0]
    return jnp.transpose(g.reshape(r, N_DEV, 128), (1, 0, 2)).reshape(N_DEV * r, 128)


def _from_col_slabs(t):
    r = t.shape[0] // N_DEV
    return jnp.transpose(t.reshape(N_DEV, r, 128), (1, 0, 2)).reshape(r, N_DEV * 128)


def _slots(t):
    return t.reshape(N_DEV, t.shape[0] // N_DEV, t.shape[1])


def kernel(x, mix_norm, w_in, b_in, sinks, conv_w, w_attn_branch, w_conv_branch, w_out, ffn_norm, w_up, ffn_conv_w, w_down, final_norm, loss_target, m_mix_norm, m_w_in, m_b_in, m_sinks, m_conv_w, m_w_attn_branch, m_w_conv_branch, m_w_out, m_ffn_norm, m_w_up, m_ffn_conv_w, m_w_down, m_final_norm, v_mix_norm, v_w_in, v_b_in, v_sinks, v_conv_w, v_w_attn_branch, v_w_conv_branch, v_w_out, v_ffn_norm, v_w_up, v_ffn_conv_w, v_w_down, v_final_norm):
    xs, tgt = x[0], loss_target[0]
    me = 4 * lax.axis_index("x") + 2 * lax.axis_index("y") + lax.axis_index("c")
    in_rows, up_rows = IN_W // N_DEV, 2 * D_FF // N_DEV

    conv_sh = jnp.concatenate([_pad_cols(ffn_conv_w[0], 768), _pad_cols(conv_w[0], 768),
                               jnp.zeros((2, 768), F32)], axis=0)
    win_sh, wup_sh = w_in[0].T.astype(BF16), w_up[0].T.astype(BF16)
    wout_sh, wdown_sh = w_out[0].astype(BF16), w_down[0].astype(BF16)
    wa_sh, wc_sh = w_attn_branch[0].astype(BF16), w_conv_branch[0].astype(BF16)

    half = D_MODEL // 2
    (win_t,) = _exchange_only(_AllGather([win_sh]), "gather_w_in")
    (xn, qkv, cbx, gates), (wa_s, wc_s, wout, conv_g) = _norm_inproj(
        xs, mix_norm, win_t, b_in, _AllGather([wa_sh, wc_sh, wout_sh, conv_sh]))
    (attn, lse), (wup_lo,) = _attn_fwd(qkv, sinks, _AllGather([wup_sh[:, :half]]))
    wa, wc = _from_col_slabs(wa_s), _from_col_slabs(wc_s)
    conv_g = conv_g.reshape(N_DEV, 8, 768)
    fcw = jnp.transpose(conv_g[:, 0:3, :up_rows], (1, 0, 2)).reshape(3, 2 * D_FF)
    cw = jnp.transpose(conv_g[:, 3:6, :CONV_W // N_DEV], (1, 0, 2)).reshape(3, CONV_W)
    (h1,), (wup_hi,) = _mix_fwd(xs, cbx, gates, attn, cw, wa, wc, wout, _AllGather([wup_sh[:, half:]]))
    (hn, up_pre), (wdown,) = _ffn_up(h1, ffn_norm, wup_lo, wup_hi, _AllGather([wdown_sh]))
    up, act, dh2, loss_p, dfn_p = _ffn_down_loss(up_pre, fcw, wdown, h1, final_norm.reshape(1, D_MODEL), tgt)

    dn_rows, q_up = D_FF // N_DEV, up_rows // 4
    g_wdown = _matmul_tn(act, dh2, FF_CHUNK, "grad_w_down")
    (dup_pre, dfcw_p, dh1, dffn_p), (r_wdown,) = _ffn_bwd(dh2, wdown, up, up_pre, fcw, wup_lo, wup_hi, h1, ffn_norm,
                                                         _ReduceScatter([(g_wdown, 0, dn_rows)]))
    g_wup_t = _matmul_tn(dup_pre, hn, FF_CHUNK, "grad_w_up")
    (dgates, dattn, dcb, dcv, g_wout, g_wa_nat, g_wc_nat), (r_wup_ab,) = _mix_bwd(
        dh1, wout, gates, attn, wa, wc, cbx, cw, _ReduceScatter([(g_wup_t, 0, 2 * q_up)]))
    g_wa, g_wc = _to_col_slabs(g_wa_nat), _to_col_slabs(g_wc_nat)
    dcc, dcx, dcw_p = _conv_branch_bwd(dcv, cbx, cw)
    (dq, dk, dv, dsink_p), (r_wup_c, r_wout, r_wa, r_wc) = _attn_bwd(
        qkv, sinks, attn, lse, dattn,
        _ReduceScatter([(g_wup_t, 2 * q_up, q_up), (g_wout, 0, D_MODEL // N_DEV), (g_wa, 0, ATTN_W),
                        (g_wc, 0, CONV_W)]))
    dproj = (dq, dk, dv, dcb, dcc, dcx, dgates)
    g_win_t, (r_wup_d,) = _grad_w_in(dproj, xn, _ReduceScatter([(g_wup_t, 3 * q_up, q_up)]))
    (win_theirs,) = _exchange_only(_PairExchange([g_win_t]), "pair_exchange_w_in")
    q_win = _pair_add(g_win_t, win_theirs, in_rows // 2, "pair_add_w_in")
    (dx, dbin_p, dmix_p), (r_win,) = _inproj_bwd(dproj, win_t, xs, mix_norm, dh1, _ChipExchange([q_win]))

    small = _pack_small(dmix_p, dffn_p, dfn_p, dsink_p, loss_p, dbin_p, dcw_p, dfcw_p)
    (r_small,) = _exchange_only(_ReduceScatter([], [small]), "exchange_small")

    fn2, m_fn2, v_fn2 = (t.reshape(1, D_MODEL) for t in (final_norm, m_final_norm, v_final_norm))
    small_res, g_cw_full, g_fcw_full, loss_row = _small_sums_adamw(
        _slots(r_small), [(mix_norm, m_mix_norm, v_mix_norm), (b_in, m_b_in, v_b_in), (sinks, m_sinks, v_sinks),
                          (ffn_norm, m_ffn_norm, v_ffn_norm), (fn2, m_fn2, v_fn2)])
    loss = loss_row[0, 0]
    g_cw = lax.dynamic_slice_in_dim(g_cw_full, me * (CONV_W // N_DEV), CONV_W // N_DEV, axis=1)
    g_fcw = lax.dynamic_slice_in_dim(g_fcw_full, me * up_rows, up_rows, axis=1)
    cw_res, fcw_res = _adamw_pair((conv_w[0], g_cw, m_conv_w[0], v_conv_w[0]),
                                  (ffn_conv_w[0], g_fcw, m_ffn_conv_w[0], v_ffn_conv_w[0]))

    big = {}
    big["w_in"] = tuple(t.T for t in _sum_parts_adamw(
        [r_win.reshape(4, in_rows, D_MODEL)], w_in[0].T, m_w_in[0].T, v_w_in[0].T, in_rows // 2, "adamw_w_in"))
    big["w_up"] = tuple(t.T for t in _sum_parts_adamw(
        [_slots(r_wup_ab), _slots(r_wup_c), _slots(r_wup_d)], w_up[0].T, m_w_up[0].T, v_w_up[0].T, q_up,
        "adamw_w_up"))
    big["w_out"] = _sum_adamw(_slots(r_wout), w_out[0], m_w_out[0], v_w_out[0], 128, "adamw_w_out")
    big["w_down"] = _sum_adamw(_slots(r_wdown), w_down[0], m_w_down[0], v_w_down[0], dn_rows // 2, "adamw_w_down")
    big["w_attn_branch"] = _sum_adamw(_slots(r_wa), w_attn_branch[0], m_w_attn_branch[0], v_w_attn_branch[0], 256,
                                      "adamw_w_attn_branch")
    big["w_conv_branch"] = _sum_adamw(_slots(r_wc), w_conv_branch[0], m_w_conv_branch[0], v_w_conv_branch[0], 256,
                                      "adamw_w_conv_branch")

    res = dict(zip(("mix_norm", "b_in", "sinks", "ffn_norm"), small_res[:4]))
    res["final_norm"] = tuple(t.reshape(final_norm.shape) for t in small_res[4])
    res["conv_w"] = tuple(t.reshape(conv_w.shape) for t in (g_cw,) + cw_res)
    res["ffn_conv_w"] = tuple(t.reshape(ffn_conv_w.shape) for t in (g_fcw,) + fcw_res)
    for name, ref_w in (("w_in", w_in), ("w_up", w_up), ("w_out", w_out), ("w_down", w_down),
                        ("w_attn_branch", w_attn_branch), ("w_conv_branch", w_conv_branch)):
        res[name] = tuple(t.reshape(ref_w.shape) for t in big[name])

    order = ["mix_norm", "w_in", "b_in", "sinks", "conv_w", "w_attn_branch", "w_conv_branch", "w_out",
             "ffn_norm", "w_up", "ffn_conv_w", "w_down", "final_norm"]
    out = [loss, dx.reshape(x.shape)]
    for k in range(4):
        out += [res[name][k] for name in order]
    return tuple(out)
```

```python
import math

import jax
import jax.numpy as jnp
from jax import lax
from jax.experimental import pallas as pl
from jax.experimental.pallas import tpu as pltpu

F32 = jnp.float32
BF16 = jnp.bfloat16
MESH = pl.DeviceIdType.MESH
N_DEV = 8

D_MODEL = 1024
HEAD_DIM = 64
N_HEADS = 8
BLOCK = 128
ATTN_W = 512
KV_W = 128
CONV_W = 512
QKV_W = ATTN_W + 2 * KV_W
CBX_W = 3 * CONV_W
GATE_W = 2 * D_MODEL
IN_W = QKV_W + CBX_W + GATE_W
D_FF = 2816
FF_CHUNK = 1408
NORM_EPS = 1e-5
ATTN_SCALE = HEAD_DIM ** -0.5
NEG = -1e30
HALO = 16

ADAM_LR = 0.001
ADAM_B1 = 0.9
ADAM_B2 = 0.999
ADAM_EPS = 1e-08
ADAM_WD = 0.01
ADAM_STEP = 10

VMEM_LIMIT = 56 * 1024 * 1024
SMALL_ROWS = 32

NT = (((1,), (1,)), ((), ()))
TN = (((0,), (0,)), ((), ()))
ANY = pl.BlockSpec(memory_space=pl.ANY)


def _sig(v):
    return 1.0 / (1.0 + jnp.exp(-v))


def _row_tile(s, pref=256):
    return pref if s % pref == 0 else s


def _shifts_down(u, halo, ks):
    ext = jnp.concatenate([halo, u], axis=0)
    return [pltpu.roll(ext, k, axis=0)[HALO:, :] for k in ks]


def _shifts_up(u, halo, ks):
    n = u.shape[0]
    ext = jnp.concatenate([u, halo], axis=0)
    return [pltpu.roll(ext, n + HALO - k, axis=0)[:n, :] for k in ks]


def _shift_matrix(n, k):
    row = lax.broadcasted_iota(jnp.int32, (n, n), 0)
    col = lax.broadcasted_iota(jnp.int32, (n, n), 1)
    return jnp.where(col == row + k, 1.0, 0.0).astype(BF16)


def _mxu_shift_up(mat, ub, halo, k):
    n = ub.shape[0]
    v = jnp.dot(mat, ub, preferred_element_type=F32)
    row = lax.broadcasted_iota(jnp.int32, (8, ub.shape[1]), 0)
    tail = v[n - 8:, :]
    for t in range(k):
        tail = jnp.where(row == 8 - k + t, halo[t:t + 1, :], tail)
    return jnp.concatenate([v[:n - 8, :], tail], axis=0)


def _rows_reversed(tm, c, steps):
    return pl.BlockSpec((tm, c), lambda i: (steps - 1 - i, 0))


def _prev_halo_map_reversed(tm, steps):
    return lambda i: (jnp.maximum((steps - 1 - i) * (tm // HALO) - 1, 0), 0)


def _prev_halo_map(tm):
    return lambda i: (jnp.maximum(i * (tm // HALO) - 1, 0), 0)


def _next_halo_map(tm, s):
    return lambda i: (jnp.minimum((i + 1) * (tm // HALO), s // HALO - 1), 0)


def _full(shape):
    return pl.BlockSpec(shape, lambda *_: (0,) * len(shape))


def _resident(shape):
    return pl.BlockSpec(shape, lambda *_: (0,) * len(shape), pipeline_mode=pl.Buffered(1))


def _rows(tm, c):
    return pl.BlockSpec((tm, c), lambda i: (i, 0))


def _sds(shape, dtype):
    return jax.ShapeDtypeStruct(shape, dtype)


def _my_place():
    x, y, c = lax.axis_index("x"), lax.axis_index("y"), lax.axis_index("c")
    return x, y, c


ALL_PEERS = tuple((j >> 2, (j >> 1) & 1, j & 1) for j in range(1, N_DEV))
SIBLING_PEER = ((0, 0, 1),)
CHIP_PEERS = ((0, 1, 0), (1, 0, 0), (1, 1, 0))
BARRIER_ID = {ALL_PEERS: 0, SIBLING_PEER: 1, CHIP_PEERS: 2}


def _entry_barrier(peers):
    x, y, c = _my_place()
    barrier = pltpu.get_barrier_semaphore()
    for dx, dy, dc in peers:
        pl.semaphore_signal(barrier, inc=1, device_id=(x ^ dx, y ^ dy, c ^ dc), device_id_type=MESH)
    pl.semaphore_wait(barrier, len(peers))


def _start_exchange(remote, local):
    for cp in local + remote:
        cp.start()


def _finish_exchange(remote, local):
    for cp in remote:
        cp.wait_recv()
    for cp in remote:
        cp.wait_send()
    for cp in local:
        cp.wait()


class _AllGather:
    peers = ALL_PEERS

    def __init__(self, shards):
        self.ins = list(shards)
        n = len(shards)
        self.out_shape = [_sds((N_DEV * s.shape[0], s.shape[1]), s.dtype) for s in shards]
        self.sems = [pltpu.SemaphoreType.DMA((7 * n,)), pltpu.SemaphoreType.DMA((7 * n,)),
                     pltpu.SemaphoreType.DMA((n,))]

    def _parts(self, ins, outs, sems):
        send_sems, recv_sems, local_sems = sems
        x, y, c = _my_place()
        me, sibling = (x, y, c), (x, y, 1 - c)
        chips = [(1 - x, y), (x, 1 - y), (1 - x, 1 - y)]

        def rows(k, dev):
            r = ins[k].shape[0]
            start = pl.multiple_of((4 * dev[0] + 2 * dev[1] + dev[2]) * r, 8)
            return outs[k].at[pl.ds(start, r), :]

        def copy(k, j, block, to, src=None):
            return pltpu.make_async_remote_copy(
                src_ref=rows(k, block) if src is None else src, dst_ref=rows(k, block),
                send_sem=send_sems.at[7 * k + j], recv_sem=recv_sems.at[7 * k + j],
                device_id=to, device_id_type=MESH)

        n = len(ins)
        mine = [pltpu.make_async_copy(ins[k], rows(k, me), local_sems.at[k]) for k in range(n)]
        first = []
        for k in range(n):
            first.append(copy(k, 0, me, sibling, src=ins[k]))
            first += [copy(k, 1 + j, me, (*chip, c), src=ins[k]) for j, chip in enumerate(chips)]
        return me, sibling, chips, copy, mine, first

    def start(self, ins, outs, sems):
        _, _, _, _, mine, first = self._parts(ins, outs, sems)
        _start_exchange(first, mine)

    def finish(self, ins, outs, sems):
        me, sibling, chips, copy, mine, first = self._parts(ins, outs, sems)
        c = me[2]
        n = len(ins)
        passed = []
        for j, chip in enumerate(chips):
            for k in range(n):
                copy(k, 1 + j, (*chip, c), me).wait_recv()
                fwd = copy(k, 4 + j, (*chip, c), sibling)
                fwd.start()
                passed.append(fwd)
        for k in range(n):
            copy(k, 0, sibling, me).wait_recv()
            for j, chip in enumerate(chips):
                copy(k, 4 + j, (*chip, 1 - c), me).wait_recv()
        for cp in first + passed:
            cp.wait_send()
        for cp in mine:
            cp.wait()


class _ReduceScatter:
    peers = ALL_PEERS

    def __init__(self, parts, bcast=()):
        self.parts = [(lo, cnt) for _, lo, cnt in parts]
        self.n_parts = len(parts)
        self.ins = [a for a, _, _ in parts] + list(bcast)
        self.out_shape = [_sds((N_DEV * cnt, a.shape[1]), a.dtype) for a, _, cnt in parts]
        self.out_shape += [_sds((N_DEV * b.shape[0], b.shape[1]), b.dtype) for b in bcast]
        n = len(self.ins)
        self.sems = [pltpu.SemaphoreType.DMA((7 * n,)), pltpu.SemaphoreType.DMA((7 * n,)),
                     pltpu.SemaphoreType.DMA((n,))]

    def _copies(self, ins, outs, sems):
        send_sems, recv_sems, local_sems = sems
        x, y, c = _my_place()
        me_idx = 4 * x + 2 * y + c
        remote, local = [], []
        for k in range(len(ins)):
            cnt = outs[k].shape[0] // N_DEV
            dst = outs[k].at[pl.ds(pl.multiple_of(me_idx * cnt, 8), cnt), :]
            if k < self.n_parts:
                lo, _ = self.parts[k]
                r = ins[k].shape[0] // N_DEV
                src_of = lambda idx: ins[k].at[pl.ds(pl.multiple_of(idx * r + lo, 8), cnt), :]
            else:
                src_of = lambda idx: ins[k]
            local.append(pltpu.make_async_copy(src_of(me_idx), dst, local_sems.at[k]))
            for j in range(1, N_DEV):
                peer = (x ^ (j >> 2), y ^ ((j >> 1) & 1), c ^ (j & 1))
                peer_idx = 4 * peer[0] + 2 * peer[1] + peer[2]
                remote.append(pltpu.make_async_remote_copy(
                    src_ref=src_of(peer_idx), dst_ref=dst,
                    send_sem=send_sems.at[7 * k + j - 1], recv_sem=recv_sems.at[7 * k + j - 1],
                    device_id=peer, device_id_type=MESH))
        return remote, local

    def start(self, ins, outs, sems):
        _start_exchange(*self._copies(ins, outs, sems))

    def finish(self, ins, outs, sems):
        _finish_exchange(*self._copies(ins, outs, sems))


class _PairExchange:
    peers = SIBLING_PEER

    def __init__(self, arrays):
        self.ins = list(arrays)
        n = len(arrays)
        self.out_shape = [_sds((a.shape[0] // 2, a.shape[1]), a.dtype) for a in arrays]
        self.sems = [pltpu.SemaphoreType.DMA((4 * n,)), pltpu.SemaphoreType.DMA((4 * n,))]

    def _copies(self, ins, outs, sems):
        send_sems, recv_sems = sems
        x, y, c = _my_place()
        remote = []
        for k in range(len(ins)):
            r = ins[k].shape[0] // N_DEV
            for chip in range(4):
                sib = ins[k].at[pl.ds(pl.multiple_of((2 * chip + 1 - c) * r, 8), r), :]
                remote.append(pltpu.make_async_remote_copy(
                    src_ref=sib, dst_ref=outs[k].at[pl.ds(chip * r, r), :],
                    send_sem=send_sems.at[4 * k + chip], recv_sem=recv_sems.at[4 * k + chip],
                    device_id=(x, y, 1 - c), device_id_type=MESH))
        return remote

    def start(self, ins, outs, sems):
        for cp in self._copies(ins, outs, sems):
            cp.start()

    def finish(self, ins, outs, sems):
        remote = self._copies(ins, outs, sems)
        for cp in remote:
            cp.wait_recv()
        for cp in remote:
            cp.wait_send()


class _ChipExchange:
    peers = CHIP_PEERS

    def __init__(self, arrays):
        self.ins = list(arrays)
        self.out_shape = [_sds(a.shape, a.dtype) for a in arrays]
        n = len(self.ins)
        self.sems = [pltpu.SemaphoreType.DMA((3 * n,)), pltpu.SemaphoreType.DMA((3 * n,)),
                     pltpu.SemaphoreType.DMA((n,))]

    def _copies(self, ins, outs, sems):
        send_sems, recv_sems, local_sems = sems
        x, y, c = _my_place()
        my_chip = 2 * x + y
        remote, local = [], []
        for k in range(len(ins)):
            r = ins[k].shape[0] // 4
            dst = outs[k].at[pl.ds(pl.multiple_of(my_chip * r, 8), r), :]
            local.append(pltpu.make_async_copy(ins[k].at[pl.ds(pl.multiple_of(my_chip * r, 8), r), :], dst,
                                               local_sems.at[k]))
            for j in range(1, 4):
                px, py = x ^ (j >> 1), y ^ (j & 1)
                src = ins[k].at[pl.ds(pl.multiple_of((2 * px + py) * r, 8), r), :]
                remote.append(pltpu.make_async_remote_copy(
                    src_ref=src, dst_ref=dst, send_sem=send_sems.at[3 * k + j - 1],
                    recv_sem=recv_sems.at[3 * k + j - 1], device_id=(px, py, c), device_id_type=MESH))
        return remote, local

    def start(self, ins, outs, sems):
        _start_exchange(*self._copies(ins, outs, sems))

    def finish(self, ins, outs, sems):
        _finish_exchange(*self._copies(ins, outs, sems))


def _pcall(body, name, grid, in_specs, out_specs, out_shape, args, scratch=(), comm=None):
    params = pltpu.CompilerParams(dimension_semantics=("arbitrary",) * len(grid), vmem_limit_bytes=VMEM_LIMIT)
    in_specs, out_specs, out_shape, scratch = list(in_specs), list(out_specs), list(out_shape), list(scratch)
    if comm is None:
        res = pl.pallas_call(body, name=name, grid=grid, in_specs=in_specs, out_specs=out_specs, out_shape=out_shape,
                             scratch_shapes=scratch, compiler_params=params)(*args)
        return list(res), []
    n_in, n_out, n_scr = len(in_specs), len(out_specs), len(scratch)
    ci, co = len(comm.ins), len(comm.out_shape)
    total = math.prod(grid)

    def carried(*refs):
        bounds = [0, n_in, n_in + ci, n_in + ci + n_out, n_in + ci + n_out + co, n_in + ci + n_out + co + n_scr]
        ins, cins, outs, couts, scr = (refs[a:b] for a, b in zip(bounds[:-1], bounds[1:]))
        sems = refs[bounds[-1]:]
        step = pl.program_id(0)
        for d in range(1, len(grid)):
            step = step * grid[d] + pl.program_id(d)

        @pl.when(step == 0)
        def _():
            _entry_barrier(comm.peers)
            comm.start(cins, couts, sems)

        body(*ins, *outs, *scr)

        @pl.when(step == total - 1)
        def _():
            comm.finish(cins, couts, sems)

    params = pltpu.CompilerParams(dimension_semantics=("arbitrary",) * len(grid), vmem_limit_bytes=VMEM_LIMIT,
                                  collective_id=BARRIER_ID[comm.peers])
    res = pl.pallas_call(
        carried, name=name, grid=grid, in_specs=in_specs + [ANY] * ci, out_specs=out_specs + [ANY] * co,
        out_shape=out_shape + comm.out_shape, scratch_shapes=scratch + comm.sems, compiler_params=params,
    )(*args, *comm.ins)
    return list(res[:n_out]), list(res[n_out:])


def _exchange_only(comm, name):
    def body(*refs):
        ci, co = len(comm.ins), len(comm.out_shape)
        _entry_barrier(comm.peers)
        comm.start(refs[:ci], refs[ci:ci + co], refs[ci + co:])
        comm.finish(refs[:ci], refs[ci:ci + co], refs[ci + co:])

    params = pltpu.CompilerParams(collective_id=BARRIER_ID[comm.peers])
    return pl.pallas_call(body, name=name, out_shape=comm.out_shape, in_specs=[ANY] * len(comm.ins),
                          out_specs=[ANY] * len(comm.out_shape), scratch_shapes=comm.sems,
                          compiler_params=params)(*comm.ins)


def _norm_inproj(x, g, win_t, b_in, comm):
    s = x.shape[0]
    tm = _row_tile(s, 512)
    widths = (QKV_W, CBX_W, GATE_W)

    def body(x_ref, g_ref, w_ref, b_ref, xn_ref, qkv_ref, cbx_ref, gate_ref):
        xv = x_ref[...]
        r = lax.rsqrt(jnp.mean(xv * xv, axis=-1, keepdims=True) + NORM_EPS)
        xn = (xv * r * g_ref[...]).astype(BF16)
        xn_ref[...] = xn
        off = 0
        for o_ref, w in zip((qkv_ref, cbx_ref, gate_ref), widths):
            acc = lax.dot_general(xn, w_ref[off:off + w, :], NT, preferred_element_type=F32)
            o_ref[...] = (acc + b_ref[:, off:off + w]).astype(BF16)
            off += w

    return _pcall(
        body, "norm_inproj", (s // tm,),
        [_rows(tm, D_MODEL), _full((1, D_MODEL)), _resident((IN_W, D_MODEL)), _full((1, IN_W))],
        [_rows(tm, D_MODEL)] + [_rows(tm, w) for w in widths],
        [_sds((s, D_MODEL), BF16)] + [_sds((s, w), BF16) for w in widths],
        (x, g, win_t, b_in), comm=comm)


def _attn_specs():
    prev = lambda n: jnp.maximum(n - 1, 0)
    return [pl.BlockSpec((BLOCK, ATTN_W), lambda n: (n, 0)),
            pl.BlockSpec((BLOCK, KV_W), lambda n: (prev(n), ATTN_W // KV_W)),
            pl.BlockSpec((BLOCK, KV_W), lambda n: (n, ATTN_W // KV_W)),
            pl.BlockSpec((BLOCK, KV_W), lambda n: (prev(n), ATTN_W // KV_W + 1)),
            pl.BlockSpec((BLOCK, KV_W), lambda n: (n, ATTN_W // KV_W + 1))]


def _lower_lanes():
    return lax.broadcasted_iota(jnp.int32, (BLOCK, 128), 1) < HEAD_DIM


def _stack_heads(val, kh):
    lower = _lower_lanes()
    parts = []
    for g in range(4):
        h = kh * 4 + g
        blk = val[:, (h // 2) * 128:(h // 2 + 1) * 128]
        keep = lower if h % 2 == 0 else jnp.logical_not(lower)
        parts.append(jnp.where(keep, blk, jnp.zeros_like(blk)))
    return jnp.concatenate(parts, axis=0)


def _dup_kv(prev_ref, cur_ref, kh):
    t = jnp.concatenate([prev_ref[...], cur_ref[...]], axis=0).astype(F32)
    rolled = pltpu.roll(t, HEAD_DIM, axis=1)
    lower = lax.broadcasted_iota(jnp.int32, t.shape, 1) < HEAD_DIM
    dup = jnp.where(lower, t, rolled) if kh == 0 else jnp.where(lower, rolled, t)
    return dup.astype(BF16)


def _attn_mask(n):
    row = lax.broadcasted_iota(jnp.int32, (4 * BLOCK, 2 * BLOCK), 0)
    kj = lax.broadcasted_iota(jnp.int32, (4 * BLOCK, 2 * BLOCK), 1)
    dist = (row & (BLOCK - 1)) + BLOCK - kj
    band = jnp.logical_and(dist >= 0, dist < BLOCK)
    return jnp.logical_and(band, jnp.logical_or(kj >= BLOCK, n > 0))


def _sink_col(sinks_ref, kh):
    gi = lax.broadcasted_iota(jnp.int32, (4 * BLOCK, 1), 0) // BLOCK
    col = jnp.zeros((4 * BLOCK, 1), F32)
    for g in range(4):
        col = jnp.where(gi == g, sinks_ref[0, kh * 4 + g], col)
    return col


def _attn_fwd(qkv, sinks, comm):
    s = qkv.shape[0]

    def body(sinks_ref, q_ref, kp_ref, kc_ref, vp_ref, vc_ref, o_ref, lse_ref):
        n = pl.program_id(0)
        mask = _attn_mask(n)
        lower = _lower_lanes()
        lane = lax.broadcasted_iota(jnp.int32, (BLOCK, 128), 1)
        qv = q_ref[...]
        lse_out = jnp.zeros((BLOCK, 128), F32)
        for kh in range(2):
            qs = _stack_heads(qv, kh)
            kd, vd = _dup_kv(kp_ref, kc_ref, kh), _dup_kv(vp_ref, vc_ref, kh)
            sc = lax.dot_general(qs, kd, NT, preferred_element_type=F32) * ATTN_SCALE
            sc = jnp.where(mask, sc, NEG)
            sink = _sink_col(sinks_ref, kh)
            m = jnp.maximum(jnp.max(sc, axis=1, keepdims=True), sink)
            p = jnp.exp(sc - m)
            l = jnp.sum(p, axis=1, keepdims=True) + jnp.exp(sink - m)
            o = jnp.dot(p.astype(BF16), vd, preferred_element_type=F32) / l
            lse = m + jnp.log(l)
            for pair in range(2):
                lo = o[(2 * pair) * BLOCK:(2 * pair + 1) * BLOCK]
                hi = o[(2 * pair + 1) * BLOCK:(2 * pair + 2) * BLOCK]
                col = (kh * 2 + pair) * 128
                o_ref[:, col:col + 128] = jnp.where(lower, lo, hi).astype(BF16)
            for g in range(4):
                lse_out = jnp.where(lane == kh * 4 + g, lse[g * BLOCK:(g + 1) * BLOCK], lse_out)
        lse_ref[...] = lse_out

    return _pcall(
        body, "attn_fwd", (s // BLOCK,),
        [pl.BlockSpec(memory_space=pltpu.SMEM)] + _attn_specs(),
        [pl.BlockSpec((BLOCK, ATTN_W), lambda n: (n, 0)), pl.BlockSpec((BLOCK, 128), lambda n: (n, 0))],
        [_sds((s, ATTN_W), BF16), _sds((s, 128), F32)],
        (sinks, qkv, qkv, qkv, qkv, qkv), comm=comm)


def _conv_u(cbx_ref, halo_ref, w_ref, first):
    cb = cbx_ref[:, 0:CONV_W].astype(F32)
    cc = cbx_ref[:, CONV_W:2 * CONV_W].astype(F32)
    cx = cbx_ref[:, 2 * CONV_W:3 * CONV_W].astype(F32)
    u = cc * cx
    uh = halo_ref[:, CONV_W:2 * CONV_W].astype(F32) * halo_ref[:, 2 * CONV_W:3 * CONV_W].astype(F32)
    uh = jnp.where(first, 0.0, uh)
    u1, u2 = _shifts_down(u, uh, (1, 2))
    cv = w_ref[0:1, :] * u2 + w_ref[1:2, :] * u1 + w_ref[2:3, :] * u
    return cb, cc, cx, u, cv


def _mix_fwd(x, cbx, gates, attn, conv_w, wa, wc, wout, comm):
    s = x.shape[0]
    tm = _row_tile(s)

    def body(x_ref, cbx_ref, halo_ref, gate_ref, attn_ref, cw_ref, wa_ref, wc_ref, wo_ref,
             h1_ref):
        first = pl.program_id(0) == 0
        cb, _, _, _, cv = _conv_u(cbx_ref, halo_ref, cw_ref, first)
        conv = (cb * cv).astype(BF16)
        ap = jnp.dot(attn_ref[...], wa_ref[...], preferred_element_type=F32)
        cp = jnp.dot(conv, wc_ref[...], preferred_element_type=F32)
        ga = gate_ref[:, 0:D_MODEL].astype(F32)
        gc = gate_ref[:, D_MODEL:2 * D_MODEL].astype(F32)
        merged = (_sig(ga) * ap + _sig(gc) * cp).astype(BF16)
        h1_ref[...] = x_ref[...] + jnp.dot(merged, wo_ref[...], preferred_element_type=F32)

    return _pcall(
        body, "mix_fwd", (s // tm,),
        [_rows(tm, D_MODEL), _rows(tm, CBX_W), pl.BlockSpec((HALO, CBX_W), _prev_halo_map(tm)),
         _rows(tm, GATE_W), _rows(tm, ATTN_W), _full((3, CONV_W)), _full((ATTN_W, D_MODEL)),
         _full((CONV_W, D_MODEL)), _full((D_MODEL, D_MODEL))],
        [_rows(tm, D_MODEL)], [_sds((s, D_MODEL), F32)],
        (x, cbx, cbx, gates, attn, conv_w, wa, wc, wout), comm=comm)


def _ffn_up(h1, g, wup_lo, wup_hi, comm):
    s = h1.shape[0]
    tm = _row_tile(s, 512)
    half = D_MODEL // 2

    def body(h_ref, g_ref, wl_ref, wh_ref, hn_ref, up_ref):
        hv = h_ref[...]
        r = lax.rsqrt(jnp.mean(hv * hv, axis=-1, keepdims=True) + NORM_EPS)
        hn = (hv * r * g_ref[...]).astype(BF16)
        hn_ref[...] = hn
        for c in range(2 * D_FF // FF_CHUNK):
            sl = slice(c * FF_CHUNK, (c + 1) * FF_CHUNK)
            acc = lax.dot_general(hn[:, :half], wl_ref[sl, :], NT, preferred_element_type=F32)
            acc = acc + lax.dot_general(hn[:, half:], wh_ref[sl, :], NT, preferred_element_type=F32)
            up_ref[:, sl] = acc.astype(BF16)

    return _pcall(
        body, "ffn_up", (s // tm,),
        [_rows(tm, D_MODEL), _full((1, D_MODEL)), _resident((2 * D_FF, half)), _resident((2 * D_FF, half))],
        [_rows(tm, D_MODEL), _rows(tm, 2 * D_FF)],
        [_sds((s, D_MODEL), BF16), _sds((s, 2 * D_FF), BF16)],
        (h1, g, wup_lo, wup_hi), comm=comm)


def _ffn_conv_cols(up_ref, halo_ref, fcw_ref, first, off):
    u = up_ref[:, off:off + FF_CHUNK].astype(F32)
    uh = jnp.where(first, 0.0, halo_ref[:, off:off + FF_CHUNK].astype(F32))
    w = fcw_ref[:, off:off + FF_CHUNK]
    u1, u2 = _shifts_down(u, uh, (1, 2))
    return w[0:1] * u2 + w[1:2] * u1 + w[2:3] * u


def _ffn_down_loss(up_pre, fcw, wdown, h1, fnorm, target):
    s = h1.shape[0]
    tm = _row_tile(s)

    def body(up_ref, halo_ref, fcw_ref, wd_ref, h1_ref, fn_ref, t_ref, cu_ref, act_ref, dh2_ref, loss_ref, dfn_ref):
        i = pl.program_id(0)

        @pl.when(i == 0)
        def _():
            loss_ref[...] = jnp.zeros_like(loss_ref)
            dfn_ref[...] = jnp.zeros_like(dfn_ref)

        h2 = h1_ref[...]
        for c in range(D_FF // FF_CHUNK):
            gsl = slice(c * FF_CHUNK, (c + 1) * FF_CHUNK)
            vsl = slice(D_FF + c * FF_CHUNK, D_FF + (c + 1) * FF_CHUNK)
            gate = _ffn_conv_cols(up_ref, halo_ref, fcw_ref, i == 0, c * FF_CHUNK)
            cu_ref[:, gsl] = gate.astype(BF16)
            val = _ffn_conv_cols(up_ref, halo_ref, fcw_ref, i == 0, D_FF + c * FF_CHUNK)
            cu_ref[:, vsl] = val.astype(BF16)
            act = (gate * _sig(gate) * val).astype(BF16)
            act_ref[:, gsl] = act
            h2 = h2 + jnp.dot(act, wd_ref[gsl, :], preferred_element_type=F32)
        r = lax.rsqrt(jnp.mean(h2 * h2, axis=-1, keepdims=True) + NORM_EPS)
        yhat = h2 * r
        fn = fn_ref[...]
        diff = yhat * fn - t_ref[...]
        loss_ref[...] += 0.5 * jnp.sum(jnp.sum(diff * diff, axis=1, keepdims=True), axis=0, keepdims=True) / D_MODEL
        dy = diff * (1.0 / D_MODEL)
        dfn_ref[...] += jnp.sum(dy * yhat, axis=0, keepdims=True)
        dyh = dy * fn
        dh2_ref[...] = r * (dyh - yhat * jnp.mean(dyh * yhat, axis=-1, keepdims=True))

    return _pcall(
        body, "ffn_down_loss", (s // tm,),
        [_rows(tm, 2 * D_FF), pl.BlockSpec((HALO, 2 * D_FF), _prev_halo_map(tm)), _full((3, 2 * D_FF)),
         _resident((D_FF, D_MODEL)), _rows(tm, D_MODEL), _full((1, D_MODEL)), _rows(tm, D_MODEL)],
        [_rows(tm, 2 * D_FF), _rows(tm, D_FF), _rows(tm, D_MODEL), _full((1, 128)), _full((1, D_MODEL))],
        [_sds((s, 2 * D_FF), BF16), _sds((s, D_FF), BF16), _sds((s, D_MODEL), F32), _sds((1, 128), F32),
         _sds((1, D_MODEL), F32)],
        (up_pre, up_pre, fcw, wdown, h1, fnorm, target))[0]


def _ffn_bwd(dh2, wdown, up, up_pre, fcw, wup_lo, wup_hi, h1, g, comm):
    s = dh2.shape[0]
    tm = _row_tile(s)
    half = D_MODEL // 2

    def dup_cols(dh, up_ref, wd_ref, c):
        gsl = slice(c * FF_CHUNK, (c + 1) * FF_CHUNK)
        vsl = slice(D_FF + c * FF_CHUNK, D_FF + (c + 1) * FF_CHUNK)
        dact = lax.dot_general(dh, wd_ref[gsl, :], NT, preferred_element_type=F32)
        gate = up_ref[:, gsl].astype(F32)
        val = up_ref[:, vsl].astype(F32)
        sg = _sig(gate)
        return dact * val * (sg * (1.0 + gate * (1.0 - sg))), dact * gate * sg

    def body(dh_ref, wd_ref, up_ref, x_ref, w_ref, wl_ref, wh_ref, h_ref, g_ref,
             dx_ref, dw_ref, dh1_ref, dg_ref, carry_ref):
        @pl.when(pl.program_id(0) == 0)
        def _():
            dw_ref[...] = jnp.zeros_like(dw_ref)
            dg_ref[...] = jnp.zeros_like(dg_ref)
            carry_ref[...] = jnp.zeros_like(carry_ref)

        dh2v = dh_ref[...]
        dh = dh2v.astype(BF16)
        dhn_lo = jnp.zeros((tm, half), F32)
        dhn_hi = jnp.zeros((tm, half), F32)
        for c in range(D_FF // FF_CHUNK):
            for d, off in zip(dup_cols(dh, up_ref, wd_ref, c), (c * FF_CHUNK, D_FF + c * FF_CHUNK)):
                sl = slice(off, off + FF_CHUNK)
                dn = carry_ref[:, sl]
                carry_ref[:, sl] = d[0:HALO, :]
                xv = x_ref[:, sl].astype(F32)
                wv = w_ref[:, sl]
                d1, d2 = _shifts_up(d, dn, (1, 2))
                dx = (wv[2:3] * d + wv[1:2] * d1 + wv[0:1] * d2).astype(BF16)
                dx_ref[:, sl] = dx
                dhn_lo = dhn_lo + jnp.dot(dx, wl_ref[sl, :], preferred_element_type=F32)
                dhn_hi = dhn_hi + jnp.dot(dx, wh_ref[sl, :], preferred_element_type=F32)
                dw_ref[0:1, sl] += jnp.sum(d2 * xv, axis=0, keepdims=True)
                dw_ref[1:2, sl] += jnp.sum(d1 * xv, axis=0, keepdims=True)
                dw_ref[2:3, sl] += jnp.sum(d * xv, axis=0, keepdims=True)
        dx1, dg = _norm_bwd_tile(h_ref[...], g_ref[...], jnp.concatenate([dhn_lo, dhn_hi], axis=1))
        dg_ref[...] += dg
        dh1_ref[...] = dh2v + dx1

    rows = lambda c: _rows_reversed(tm, c, s // tm)
    return _pcall(
        body, "ffn_bwd", (s // tm,),
        [rows(D_MODEL), _resident((D_FF, D_MODEL)), rows(2 * D_FF), rows(2 * D_FF), _full((3, 2 * D_FF)),
         _resident((2 * D_FF, half)), _resident((2 * D_FF, half)), rows(D_MODEL), _full((1, D_MODEL))],
        [rows(2 * D_FF), _full((3, 2 * D_FF)), rows(D_MODEL), _full((1, D_MODEL))],
        [_sds((s, 2 * D_FF), BF16), _sds((3, 2 * D_FF), F32), _sds((s, D_MODEL), F32), _sds((1, D_MODEL), F32)],
        (dh2, wdown, up, up_pre, fcw, wup_lo, wup_hi, h1, g),
        scratch=[pltpu.VMEM((HALO, 2 * D_FF), F32)], comm=comm)


def _matmul_tn(a, b, tk, name, ts=1024, comm=None):
    s, ka = a.shape
    n = b.shape[1]
    ts = min(ts, s)
    steps = s // ts

    def body(a_ref, b_ref, o_ref, acc_ref):
        j = pl.program_id(1)

        @pl.when(j == 0)
        def _():
            acc_ref[...] = jnp.zeros_like(acc_ref)

        acc_ref[...] += lax.dot_general(a_ref[...].astype(BF16), b_ref[...].astype(BF16), TN,
                                        preferred_element_type=F32)

        @pl.when(j == steps - 1)
        def _():
            o_ref[...] = acc_ref[...].astype(BF16)

    outs, couts = _pcall(
        body, name, (ka // tk, steps),
        [pl.BlockSpec((ts, tk), lambda i, j: (j, i)), pl.BlockSpec((ts, n), lambda i, j: (j, 0))],
        [pl.BlockSpec((tk, n), lambda i, j: (i, 0))], [_sds((ka, n), BF16)],
        (a, b), scratch=[pltpu.VMEM((tk, n), F32)], comm=comm)
    return outs[0] if comm is None else (outs[0], couts)


def _norm_bwd_tile(xv, g, dy):
    r = lax.rsqrt(jnp.mean(xv * xv, axis=-1, keepdims=True) + NORM_EPS)
    xhat = xv * r
    dg = jnp.sum(dy * xhat, axis=0, keepdims=True)
    dyh = dy * g
    return r * (dyh - xhat * jnp.mean(dyh * xhat, axis=-1, keepdims=True)), dg


def _ffn_up_bwd(dup_pre, wup_lo, wup_hi, h1, g, dh2, comm):
    s = h1.shape[0]
    tm = _row_tile(s, 512)
    half = D_MODEL // 2

    def body(du_ref, wl_ref, wh_ref, h_ref, g_ref, dh2_ref, dh1_ref, dg_ref):
        @pl.when(pl.program_id(0) == 0)
        def _():
            dg_ref[...] = jnp.zeros_like(dg_ref)

        du = du_ref[...]
        dhn = jnp.concatenate([jnp.dot(du, wl_ref[...], preferred_element_type=F32),
                               jnp.dot(du, wh_ref[...], preferred_element_type=F32)], axis=1)
        dx, dg = _norm_bwd_tile(h_ref[...], g_ref[...], dhn)
        dg_ref[...] += dg
        dh1_ref[...] = dh2_ref[...] + dx

    return _pcall(
        body, "ffn_up_bwd", (s // tm,),
        [_rows(tm, 2 * D_FF), _resident((2 * D_FF, half)), _resident((2 * D_FF, half)), _rows(tm, D_MODEL),
         _full((1, D_MODEL)), _rows(tm, D_MODEL)],
        [_rows(tm, D_MODEL), _full((1, D_MODEL))],
        [_sds((s, D_MODEL), F32), _sds((1, D_MODEL), F32)],
        (dup_pre, wup_lo, wup_hi, h1, g, dh2), comm=comm)


def _mix_bwd(dh1, wout, gates, attn, wa, wc, cbx, conv_w, comm):
    s = dh1.shape[0]
    tm = _row_tile(s)
    steps = s // tm

    def body(dh_ref, wo_ref, gate_ref, attn_ref, wa_ref, wc_ref, cbx_ref, halo_ref,
             cw_ref, dg_ref, dattn_ref, dcb_ref, dcc_ref, dcx_ref, dw_ref, gwo_ref, gwa_ref, gwc_ref,
             acc_o, acc_a, acc_c, carry_ref):
        i = pl.program_id(0)

        @pl.when(i == 0)
        def _():
            dw_ref[...] = jnp.zeros_like(dw_ref)
            acc_o[...] = jnp.zeros_like(acc_o)
            acc_a[...] = jnp.zeros_like(acc_a)
            acc_c[...] = jnp.zeros_like(acc_c)
            carry_ref[...] = jnp.zeros_like(carry_ref)

        cb, cc, cx, u, cv = _conv_u(cbx_ref, halo_ref, cw_ref, i == steps - 1)
        attn = attn_ref[...]
        conv = (cb * cv).astype(BF16)
        ap = jnp.dot(attn, wa_ref[...], preferred_element_type=F32)
        cp = jnp.dot(conv, wc_ref[...], preferred_element_type=F32)
        dhb = dh_ref[...].astype(BF16)
        dm = lax.dot_general(dhb, wo_ref[...], NT, preferred_element_type=F32)
        sa = _sig(gate_ref[:, 0:D_MODEL].astype(F32))
        sc = _sig(gate_ref[:, D_MODEL:2 * D_MODEL].astype(F32))
        merged = (sa * ap + sc * cp).astype(BF16)
        da = (dm * sa).astype(BF16)
        dc = (dm * sc).astype(BF16)
        dg_ref[:, 0:D_MODEL] = (dm * ap * sa * (1.0 - sa)).astype(BF16)
        dg_ref[:, D_MODEL:2 * D_MODEL] = (dm * cp * sc * (1.0 - sc)).astype(BF16)
        dattn_ref[...] = lax.dot_general(da, wa_ref[...], NT, preferred_element_type=F32).astype(BF16)
        dconv = lax.dot_general(dc, wc_ref[...], NT, preferred_element_type=F32)
        dcb_ref[...] = (dconv * cv).astype(BF16)
        d = dconv * cb
        dn = carry_ref[...]
        carry_ref[...] = d[0:HALO, :]
        d1, d2 = _shifts_up(d, dn, (1, 2))
        du = cw_ref[2:3, :] * d + cw_ref[1:2, :] * d1 + cw_ref[0:1, :] * d2
        dcc_ref[...] = (du * cx).astype(BF16)
        dcx_ref[...] = (du * cc).astype(BF16)
        dw_ref[0:1, :] += jnp.sum(d2 * u, axis=0, keepdims=True)
        dw_ref[1:2, :] += jnp.sum(d1 * u, axis=0, keepdims=True)
        dw_ref[2:3, :] += jnp.sum(d * u, axis=0, keepdims=True)
        acc_o[...] += lax.dot_general(merged, dhb, TN, preferred_element_type=F32)
        acc_a[...] += lax.dot_general(attn, da, TN, preferred_element_type=F32)
        acc_c[...] += lax.dot_general(conv, dc, TN, preferred_element_type=F32)

        @pl.when(i == steps - 1)
        def _():
            gwo_ref[...] = acc_o[...].astype(BF16)
            gwa_ref[...] = acc_a[...].astype(BF16)
            gwc_ref[...] = acc_c[...].astype(BF16)

    rows = lambda c: _rows_reversed(tm, c, steps)
    return _pcall(
        body, "mix_bwd", (steps,),
        [rows(D_MODEL), _full((D_MODEL, D_MODEL)), rows(GATE_W), rows(ATTN_W), _full((ATTN_W, D_MODEL)),
         _full((CONV_W, D_MODEL)), rows(CBX_W), pl.BlockSpec((HALO, CBX_W), _prev_halo_map_reversed(tm, steps)),
         _full((3, CONV_W))],
        [rows(GATE_W), rows(ATTN_W), rows(CONV_W), rows(CONV_W), rows(CONV_W),
         _full((3, CONV_W)), _full((D_MODEL, D_MODEL)), _full((ATTN_W, D_MODEL)), _full((CONV_W, D_MODEL))],
        [_sds((s, GATE_W), BF16), _sds((s, ATTN_W), BF16), _sds((s, CONV_W), BF16), _sds((s, CONV_W), BF16),
         _sds((s, CONV_W), BF16), _sds((3, CONV_W), F32), _sds((D_MODEL, D_MODEL), BF16),
         _sds((ATTN_W, D_MODEL), BF16), _sds((CONV_W, D_MODEL), BF16)],
        (dh1, wout, gates, attn, wa, wc, cbx, cbx, conv_w),
        scratch=[pltpu.VMEM((D_MODEL, D_MODEL), F32), pltpu.VMEM((ATTN_W, D_MODEL), F32),
                 pltpu.VMEM((CONV_W, D_MODEL), F32), pltpu.VMEM((HALO, CONV_W), F32)], comm=comm)


def _attn_bwd(qkv, sinks, attn, lse, dattn, comm):
    s = qkv.shape[0]

    def body(sinks_ref, q_ref, kp_ref, kc_ref, vp_ref, vc_ref, o_ref, lse_ref, do_ref,
             dq_ref, dk_ref, dv_ref, ds_ref):
        n = pl.program_id(0)

        @pl.when(n == 0)
        def _():
            dk_ref[...] = jnp.zeros_like(dk_ref)
            dv_ref[...] = jnp.zeros_like(dv_ref)
            ds_ref[...] = jnp.zeros_like(ds_ref)

        mask = _attn_mask(n)
        lower = _lower_lanes()
        lane = lax.broadcasted_iota(jnp.int32, (BLOCK, 128), 1)
        lower2 = lax.broadcasted_iota(jnp.int32, (2 * BLOCK, 128), 1) < HEAD_DIM
        lane1 = lax.broadcasted_iota(jnp.int32, (1, 128), 1)
        qv, ov, dov, lsev = q_ref[...], o_ref[...], do_ref[...], lse_ref[...]
        dk_fold, dv_fold = [], []
        dsink = jnp.zeros((1, 128), F32)
        for kh in range(2):
            qs = _stack_heads(qv, kh)
            dos = _stack_heads(dov, kh)
            os_ = _stack_heads(ov, kh)
            kd, vd = _dup_kv(kp_ref, kc_ref, kh), _dup_kv(vp_ref, vc_ref, kh)
            lse = jnp.concatenate(
                [jnp.sum(jnp.where(lane == kh * 4 + g, lsev, 0.0), axis=1, keepdims=True) for g in range(4)], axis=0)
            sc = lax.dot_general(qs, kd, NT, preferred_element_type=F32) * ATTN_SCALE
            p = jnp.exp(jnp.where(mask, sc, NEG) - lse)
            dp = lax.dot_general(dos, vd, NT, preferred_element_type=F32)
            delta = jnp.sum(dos.astype(F32) * os_.astype(F32), axis=1, keepdims=True)
            dsc = (p * (dp - delta) * ATTN_SCALE).astype(BF16)
            dqs = jnp.dot(dsc, kd, preferred_element_type=F32)
            for pair in range(2):
                lo = dqs[(2 * pair) * BLOCK:(2 * pair + 1) * BLOCK]
                hi = dqs[(2 * pair + 1) * BLOCK:(2 * pair + 2) * BLOCK]
                col = (kh * 2 + pair) * 128
                dq_ref[:, col:col + 128] = jnp.where(lower, lo, hi).astype(BF16)
            dkd = lax.dot_general(dsc, qs, TN, preferred_element_type=F32)
            dvd = lax.dot_general(p.astype(BF16), dos, TN, preferred_element_type=F32)
            dk_fold.append(dkd + pltpu.roll(dkd, HEAD_DIM, axis=1))
            dv_fold.append(dvd + pltpu.roll(dvd, HEAD_DIM, axis=1))
            psink = jnp.exp(_sink_col(sinks_ref, kh) - lse) * delta
            for g in range(4):
                tot = jnp.sum(psink[g * BLOCK:(g + 1) * BLOCK], axis=0, keepdims=True)
                dsink = dsink - jnp.where(lane1 == kh * 4 + g, tot, 0.0)
        dk2 = jnp.where(lower2, dk_fold[0], dk_fold[1])
        dv2 = jnp.where(lower2, dv_fold[0], dv_fold[1])
        ds_ref[...] += dsink
        cur = pl.ds(pl.multiple_of(n * BLOCK, BLOCK), BLOCK)
        dk_ref[cur, :] += dk2[BLOCK:]
        dv_ref[cur, :] += dv2[BLOCK:]

        @pl.when(n > 0)
        def _():
            prev = pl.ds(pl.multiple_of((n - 1) * BLOCK, BLOCK), BLOCK)
            dk_ref[prev, :] += dk2[:BLOCK]
            dv_ref[prev, :] += dv2[:BLOCK]

    blk = lambda w: pl.BlockSpec((BLOCK, w), lambda n: (n, 0))
    return _pcall(
        body, "attn_bwd", (s // BLOCK,),
        [pl.BlockSpec(memory_space=pltpu.SMEM)] + _attn_specs() + [blk(ATTN_W), blk(128), blk(ATTN_W)],
        [blk(ATTN_W), _full((s, KV_W)), _full((s, KV_W)), _full((1, 128))],
        [_sds((s, ATTN_W), BF16), _sds((s, KV_W), F32), _sds((s, KV_W), F32), _sds((1, 128), F32)],
        (sinks, qkv, qkv, qkv, qkv, qkv, attn, lse, dattn), comm=comm)


DPROJ_PIECES = (ATTN_W, KV_W, KV_W, CONV_W, CONV_W, CONV_W, GATE_W)
DPROJ_OFFSETS = tuple(sum(DPROJ_PIECES[:k]) for k in range(len(DPROJ_PIECES)))


def _grad_w_in(pieces, xn, comm):
    s = xn.shape[0]
    ts = min(1024, s)
    steps = s // ts
    rows0 = DPROJ_OFFSETS[6]

    def body(*refs):
        p_refs, b_ref, o_ref, acc_ref, stage_ref, sem = refs[:7], refs[7], refs[8], refs[9], refs[10], refs[11]
        i, j = pl.program_id(0), pl.program_id(1)

        @pl.when(j == 0)
        def _():
            acc_ref[...] = jnp.zeros_like(acc_ref)

        bv = b_ref[...]

        def flush(lo, n):
            stage_ref[0:n, :] = acc_ref[0:n, :].astype(BF16)
            cp = pltpu.make_async_copy(stage_ref.at[0:n, :], o_ref.at[lo:lo + n, :], sem)
            cp.start()
            cp.wait()

        @pl.when(i == 0)
        def _():
            for p_ref, off, w in zip(p_refs[:6], DPROJ_OFFSETS[:6], DPROJ_PIECES[:6]):
                acc_ref[off:off + w, :] += lax.dot_general(p_ref[...].astype(BF16), bv, TN,
                                                           preferred_element_type=F32)

            @pl.when(j == steps - 1)
            def _():
                flush(0, rows0)

        @pl.when(i == 1)
        def _():
            acc_ref[0:GATE_W, :] += lax.dot_general(p_refs[6][...], bv, TN, preferred_element_type=F32)

            @pl.when(j == steps - 1)
            def _():
                flush(rows0, GATE_W)

    def piece_spec(w, group):
        return pl.BlockSpec((ts, w), lambda i, j: (jnp.where(i == group, j, 0), 0))

    outs, couts = _pcall(
        body, "grad_w_in", (2, steps),
        [piece_spec(w, 0) for w in DPROJ_PIECES[:6]] + [piece_spec(GATE_W, 1),
                                                         pl.BlockSpec((ts, D_MODEL), lambda i, j: (j, 0))],
        [ANY], [_sds((IN_W, D_MODEL), BF16)], (*pieces, xn),
        scratch=[pltpu.VMEM((rows0, D_MODEL), F32), pltpu.VMEM((rows0, D_MODEL), BF16), pltpu.SemaphoreType.DMA],
        comm=comm)
    return outs[0], couts


def _inproj_bwd(pieces, win_t, x, g, dh1, comm):
    s = x.shape[0]
    tm = _row_tile(s, 512)

    def body(*refs):
        p_refs = refs[:7]
        w_ref, x_ref, g_ref, dh_ref, dx_ref, db_ref, dg_ref = refs[7:]

        @pl.when(pl.program_id(0) == 0)
        def _():
            db_ref[...] = jnp.zeros_like(db_ref)
            dg_ref[...] = jnp.zeros_like(dg_ref)

        dxn = jnp.zeros((tm, D_MODEL), F32)
        for p_ref, off, w in zip(p_refs, DPROJ_OFFSETS, DPROJ_PIECES):
            v = p_ref[...].astype(BF16)
            db_ref[:, off:off + w] += jnp.sum(v.astype(F32), axis=0, keepdims=True)
            dxn = dxn + jnp.dot(v, w_ref[off:off + w, :], preferred_element_type=F32)
        dx, dg = _norm_bwd_tile(x_ref[...], g_ref[...], dxn)
        dg_ref[...] += dg
        dx_ref[...] = dh_ref[...] + dx

    return _pcall(
        body, "inproj_bwd", (s // tm,),
        [_rows(tm, w) for w in DPROJ_PIECES] + [_resident((IN_W, D_MODEL)), _rows(tm, D_MODEL), _full((1, D_MODEL)),
                                                _rows(tm, D_MODEL)],
        [_rows(tm, D_MODEL), _full((1, IN_W)), _full((1, D_MODEL))],
        [_sds((s, D_MODEL), F32), _sds((1, IN_W), F32), _sds((1, D_MODEL), F32)],
        (*pieces, win_t, x, g, dh1), comm=comm)


def _adam_math(w, g, m, v):
    m2 = ADAM_B1 * m + (1.0 - ADAM_B1) * g
    v2 = ADAM_B2 * v + (1.0 - ADAM_B2) * (g * g)
    m_hat = m2 / (1.0 - ADAM_B1 ** ADAM_STEP)
    v_hat = v2 / (1.0 - ADAM_B2 ** ADAM_STEP)
    delta = -ADAM_LR * (m_hat / (jnp.sqrt(v_hat) + ADAM_EPS) + ADAM_WD * w)
    return delta, m2, v2


def _sum_slots(ref):
    tot = ref[0].astype(F32)
    for i in range(1, ref.shape[0]):
        tot = tot + ref[i].astype(F32)
    return tot


def _pair_add(partials, theirs, tr, name):
    r = partials.shape[0] // N_DEV
    c = partials.shape[1]
    nt = r // tr
    core = lax.axis_index("c").astype(jnp.int32).reshape(1)

    def body(core_ref, a_ref, b_ref, o_ref):
        o_ref[...] = (a_ref[...].astype(F32) + b_ref[...].astype(F32)).astype(BF16)

    grid_spec = pltpu.PrefetchScalarGridSpec(
        num_scalar_prefetch=1, grid=(4 * nt,),
        in_specs=[pl.BlockSpec((None, None, tr, c), lambda i, core_ref: (i // nt, core_ref[0], i % nt, 0)),
                  pl.BlockSpec((tr, c), lambda i, core_ref: (i, 0))],
        out_specs=pl.BlockSpec((tr, c), lambda i, core_ref: (i, 0)))
    return pl.pallas_call(body, name=name, grid_spec=grid_spec, out_shape=_sds((4 * r, c), BF16))(
        core, partials.reshape(4, 2, r, c), theirs)


def _sum_adamw(parts, w, m, v, tr, name):
    r, c = w.shape

    def body(p_ref, w_ref, m_ref, v_ref, g_ref, d_ref, m2_ref, v2_ref):
        g = _sum_slots(p_ref)
        g_ref[...] = g
        d_ref[...], m2_ref[...], v2_ref[...] = _adam_math(w_ref[...], g, m_ref[...], v_ref[...])

    spec = pl.BlockSpec((tr, c), lambda i: (i, 0))
    return _pcall(body, name, (r // tr,), [pl.BlockSpec((N_DEV, tr, c), lambda i: (0, i, 0)), spec, spec, spec],
                  [spec] * 4, [_sds((r, c), F32)] * 4, (parts, w, m, v))[0]


def _sum_parts_adamw(parts, w, m, v, tr, name):
    c = w.shape[1]
    tiles = [p.shape[1] // tr for p in parts]
    starts = [sum(tiles[:k]) for k in range(len(parts))]
    n_parts = len(parts)

    def body(*refs):
        p_refs = refs[:n_parts]
        w_ref, m_ref, v_ref, g_ref, d_ref, m2_ref, v2_ref = refs[n_parts:]
        i = pl.program_id(0)
        for p_ref, st, nt in zip(p_refs, starts, tiles):
            @pl.when(jnp.logical_and(i >= st, i < st + nt))
            def _(p_ref=p_ref):
                g_ref[...] = _sum_slots(p_ref)

        d_ref[...], m2_ref[...], v2_ref[...] = _adam_math(w_ref[...], g_ref[...], m_ref[...], v_ref[...])

    def part_spec(p, st, nt):
        return pl.BlockSpec((p.shape[0], tr, c), lambda i: (0, jnp.clip(i - st, 0, nt - 1), 0))

    spec = pl.BlockSpec((tr, c), lambda i: (i, 0))
    return _pcall(
        body, name, (sum(tiles),),
        [part_spec(p, st, nt) for p, st, nt in zip(parts, starts, tiles)] + [spec, spec, spec],
        [spec] * 4, [_sds(w.shape, F32)] * 4, (*parts, w, m, v))[0]


ROW_MIX, ROW_FFN, ROW_FINAL, ROW_SINKS, ROW_LOSS, ROW_BIN, ROW_CW, ROW_FCW = 0, 1, 2, 3, 4, 5, 10, 13
FCW_ROWS = 6


def _wide_pieces(width):
    return [(k * D_MODEL, min(D_MODEL, width - k * D_MODEL)) for k in range(-(-width // D_MODEL))]


def _pack_small(dmix, dffn, dfn, dsink, loss, dbin, dcw, dfcw):
    def body(mix_ref, ffn_ref, fn_ref, sink_ref, loss_ref, bin_ref, cw_ref, fcw_ref, o_ref):
        o_ref[...] = jnp.zeros_like(o_ref)
        o_ref[ROW_MIX:ROW_MIX + 1, :] = mix_ref[...]
        o_ref[ROW_FFN:ROW_FFN + 1, :] = ffn_ref[...]
        o_ref[ROW_FINAL:ROW_FINAL + 1, :] = fn_ref[...]
        o_ref[ROW_SINKS:ROW_SINKS + 1, 0:128] = sink_ref[...]
        o_ref[ROW_LOSS:ROW_LOSS + 1, 0:128] = loss_ref[...]
        for k, (off, w) in enumerate(_wide_pieces(IN_W)):
            o_ref[ROW_BIN + k:ROW_BIN + k + 1, 0:w] = bin_ref[:, off:off + w]
        o_ref[ROW_CW:ROW_CW + 3, 0:CONV_W] = cw_ref[...]
        for a in range(3):
            for k, (off, w) in enumerate(_wide_pieces(2 * D_FF)):
                row = ROW_FCW + FCW_ROWS * a + k
                o_ref[row:row + 1, 0:w] = fcw_ref[a:a + 1, off:off + w]

    return pl.pallas_call(body, name="pack_small", out_shape=_sds((SMALL_ROWS, D_MODEL), F32))(
        dmix, dffn, dfn, dsink, loss, dbin, dcw, dfcw)


def _small_sums_adamw(r_small, params):
    rows = (ROW_MIX, ROW_BIN, ROW_SINKS, ROW_FFN, ROW_FINAL)

    def body(*refs):
        r_ref, p_refs, o_refs = refs[0], refs[1:16], refs[16:]
        tot = _sum_slots(r_ref)
        for k, row in enumerate(rows):
            w_ref, m_ref, v_ref = p_refs[3 * k:3 * k + 3]
            g_ref, d_ref, m2_ref, v2_ref = o_refs[4 * k:4 * k + 4]
            for j, (off, w) in enumerate(_wide_pieces(w_ref.shape[1])):
                g_ref[:, off:off + w] = tot[row + j:row + j + 1, 0:w]
            d_ref[...], m2_ref[...], v2_ref[...] = _adam_math(w_ref[...], g_ref[...], m_ref[...], v_ref[...])
        cw_ref, fcw_ref, loss_ref = o_refs[20:]
        cw_ref[...] = tot[ROW_CW:ROW_CW + 3, 0:CONV_W]
        for a in range(3):
            for j, (off, w) in enumerate(_wide_pieces(2 * D_FF)):
                row = ROW_FCW + FCW_ROWS * a + j
                fcw_ref[a:a + 1, off:off + w] = tot[row:row + 1, 0:w]
        loss_ref[...] = tot[ROW_LOSS:ROW_LOSS + 1, 0:128]

    flat = [t for p in params for t in p]
    out_shape = [_sds(p[0].shape, F32) for p in params for _ in range(4)]
    out_shape += [_sds((3, CONV_W), F32), _sds((3, 2 * D_FF), F32), _sds((1, 128), F32)]
    res = pl.pallas_call(body, name="small_sums_adamw", out_shape=out_shape)(r_small, *flat)
    return [tuple(res[4 * k:4 * k + 4]) for k in range(5)], res[20], res[21], res[22]


def _adamw_pair(a, b):
    def body(*refs):
        for k in range(2):
            w_ref, g_ref, m_ref, v_ref = refs[4 * k:4 * k + 4]
            d_ref, m2_ref, v2_ref = refs[8 + 3 * k:8 + 3 * k + 3]
            d_ref[...], m2_ref[...], v2_ref[...] = _adam_math(w_ref[...], g_ref[...], m_ref[...], v_ref[...])

    out_shape = [_sds(a[0].shape, F32)] * 3 + [_sds(b[0].shape, F32)] * 3
    res = pl.pallas_call(body, name="adamw_conv_weights", out_shape=out_shape)(*a, *b)
    return tuple(res[:3]), tuple(res[3:])


def _pad_cols(a, c):
    return jnp.pad(a, ((0, 0), (0, c - a.shape[1])))


def _to_col_slabs(g):
    r = g.shape[0]
    return jnp.transpose(g.reshape(r, N_DEV, 128), (1, 0, 2)).reshape(N_DEV * r, 128)


def _from_col_slabs(t):
    r = t.shape[0] // N_DEV
    return jnp.transpose(t.reshape(N_DEV, r, 128), (1, 0, 2)).reshape(r, N_DEV * 128)


def _slots(t):
    return t.reshape(N_DEV, t.shape[0] // N_DEV, t.shape[1])


def kernel(x, mix_norm, w_in, b_in, sinks, conv_w, w_attn_branch, w_conv_branch, w_out, ffn_norm, w_up, ffn_conv_w, w_down, final_norm, loss_target, m_mix_norm, m_w_in, m_b_in, m_sinks, m_conv_w, m_w_attn_branch, m_w_conv_branch, m_w_out, m_ffn_norm, m_w_up, m_ffn_conv_w, m_w_down, m_final_norm, v_mix_norm, v_w_in, v_b_in, v_sinks, v_conv_w, v_w_attn_branch, v_w_conv_branch, v_w_out, v_ffn_norm, v_w_up, v_ffn_conv_w, v_w_down, v_final_norm):
    xs, tgt = x[0], loss_target[0]
    me = 4 * lax.axis_index("x") + 2 * lax.axis_index("y") + lax.axis_index("c")
    in_rows, up_rows = IN_W // N_DEV, 2 * D_FF // N_DEV

    conv_sh = jnp.concatenate([_pad_cols(ffn_conv_w[0], 768), _pad_cols(conv_w[0], 768),
                               jnp.zeros((2, 768), F32)], axis=0)
    win_sh, wup_sh = w_in[0].T.astype(BF16), w_up[0].T.astype(BF16)
    wout_sh, wdown_sh = w_out[0].astype(BF16), w_down[0].astype(BF16)
    wa_sh, wc_sh = w_attn_branch[0].astype(BF16), w_conv_branch[0].astype(BF16)

    half = D_MODEL // 2
    (win_t,) = _exchange_only(_AllGather([win_sh]), "gather_w_in")
    (xn, qkv, cbx, gates), (wa_s, wc_s, wout, conv_g) = _norm_inproj(
        xs, mix_norm, win_t, b_in, _AllGather([wa_sh, wc_sh, wout_sh, conv_sh]))
    (attn, lse), (wup_lo,) = _attn_fwd(qkv, sinks, _AllGather([wup_sh[:, :half]]))
    wa, wc = _from_col_slabs(wa_s), _from_col_slabs(wc_s)
    conv_g = conv_g.reshape(N_DEV, 8, 768)
    fcw = jnp.transpose(conv_g[:, 0:3, :up_rows], (1, 0, 2)).reshape(3, 2 * D_FF)
    cw = jnp.transpose(conv_g[:, 3:6, :CONV_W // N_DEV], (1, 0, 2)).reshape(3, CONV_W)
    (h1,), (wup_hi,) = _mix_fwd(xs, cbx, gates, attn, cw, wa, wc, wout, _AllGather([wup_sh[:, half:]]))
    (hn, up_pre), (wdown,) = _ffn_up(h1, ffn_norm, wup_lo, wup_hi, _AllGather([wdown_sh]))
    up, act, dh2, loss_p, dfn_p = _ffn_down_loss(up_pre, fcw, wdown, h1, final_norm.reshape(1, D_MODEL), tgt)

    dn_rows, q_up = D_FF // N_DEV, up_rows // 4
    g_wdown = _matmul_tn(act, dh2, FF_CHUNK, "grad_w_down")
    (dup_pre, dfcw_p, dh1, dffn_p), (r_wdown,) = _ffn_bwd(dh2, wdown, up, up_pre, fcw, wup_lo, wup_hi, h1, ffn_norm,
                                                         _ReduceScatter([(g_wdown, 0, dn_rows)]))
    g_wup_t = _matmul_tn(dup_pre, hn, FF_CHUNK, "grad_w_up")
    (dgates, dattn, dcb, dcc, dcx, dcw_p, g_wout, g_wa_nat, g_wc_nat), (r_wup_ab,) = _mix_bwd(
        dh1, wout, gates, attn, wa, wc, cbx, cw, _ReduceScatter([(g_wup_t, 0, 2 * q_up)]))
    g_wa, g_wc = _to_col_slabs(g_wa_nat), _to_col_slabs(g_wc_nat)
    (dq, dk, dv, dsink_p), (r_wup_c, r_wout, r_wa, r_wc) = _attn_bwd(
        qkv, sinks, attn, lse, dattn,
        _ReduceScatter([(g_wup_t, 2 * q_up, q_up), (g_wout, 0, D_MODEL // N_DEV), (g_wa, 0, ATTN_W),
                        (g_wc, 0, CONV_W)]))
    dproj = (dq, dk, dv, dcb, dcc, dcx, dgates)
    g_win_t, (r_wup_d,) = _grad_w_in(dproj, xn, _ReduceScatter([(g_wup_t, 3 * q_up, q_up)]))
    (win_theirs,) = _exchange_only(_PairExchange([g_win_t]), "pair_exchange_w_in")
    q_win = _pair_add(g_win_t, win_theirs, in_rows // 2, "pair_add_w_in")
    (dx, dbin_p, dmix_p), (r_win,) = _inproj_bwd(dproj, win_t, xs, mix_norm, dh1, _ChipExchange([q_win]))

    small = _pack_small(dmix_p, dffn_p, dfn_p, dsink_p, loss_p, dbin_p, dcw_p, dfcw_p)
    (r_small,) = _exchange_only(_ReduceScatter([], [small]), "exchange_small")

    fn2, m_fn2, v_fn2 = (t.reshape(1, D_MODEL) for t in (final_norm, m_final_norm, v_final_norm))
    small_res, g_cw_full, g_fcw_full, loss_row = _small_sums_adamw(
        _slots(r_small), [(mix_norm, m_mix_norm, v_mix_norm), (b_in, m_b_in, v_b_in), (sinks, m_sinks, v_sinks),
                          (ffn_norm, m_ffn_norm, v_ffn_norm), (fn2, m_fn2, v_fn2)])
    loss = loss_row[0, 0]
    g_cw = lax.dynamic_slice_in_dim(g_cw_full, me * (CONV_W // N_DEV), CONV_W // N_DEV, axis=1)
    g_fcw = lax.dynamic_slice_in_dim(g_fcw_full, me * up_rows, up_rows, axis=1)
    cw_res, fcw_res = _adamw_pair((conv_w[0], g_cw, m_conv_w[0], v_conv_w[0]),
                                  (ffn_conv_w[0], g_fcw, m_ffn_conv_w[0], v_ffn_conv_w[0]))

    big = {}
    big["w_in"] = tuple(t.T for t in _sum_parts_adamw(
        [r_win.reshape(4, in_rows, D_MODEL)], w_in[0].T, m_w_in[0].T, v_w_in[0].T, in_rows // 2, "adamw_w_in"))
    big["w_up"] = tuple(t.T for t in _sum_parts_adamw(
        [_slots(r_wup_ab), _slots(r_wup_c), _slots(r_wup_d)], w_up[0].T, m_w_up[0].T, v_w_up[0].T, q_up,
        "adamw_w_up"))
    big["w_out"] = _sum_adamw(_slots(r_wout), w_out[0], m_w_out[0], v_w_out[0], 128, "adamw_w_out")
    big["w_down"] = _sum_adamw(_slots(r_wdown), w_down[0], m_w_down[0], v_w_down[0], dn_rows // 2, "adamw_w_down")
    big["w_attn_branch"] = _sum_adamw(_slots(r_wa), w_attn_branch[0], m_w_attn_branch[0], v_w_attn_branch[0], 256,
                                      "adamw_w_attn_branch")
    big["w_conv_branch"] = _sum_adamw(_slots(r_wc), w_conv_branch[0], m_w_conv_branch[0], v_w_conv_branch[0], 256,
                                      "adamw_w_conv_branch")

    res = dict(zip(("mix_norm", "b_in", "sinks", "ffn_norm"), small_res[:4]))
    res["final_norm"] = tuple(t.reshape(final_norm.shape) for t in small_res[4])
    res["conv_w"] = tuple(t.reshape(conv_w.shape) for t in (g_cw,) + cw_res)
    res["ffn_conv_w"] = tuple(t.reshape(ffn_conv_w.shape) for t in (g_fcw,) + fcw_res)
    for name, ref_w in (("w_in", w_in), ("w_up", w_up), ("w_out", w_out), ("w_down", w_down),
                        ("w_attn_branch", w_attn_branch), ("w_conv_branch", w_conv_branch)):
        res[name] = tuple(t.reshape(ref_w.shape) for t in big[name])

    order = ["mix_norm", "w_in", "b_in", "sinks", "conv_w", "w_attn_branch", "w_conv_branch", "w_out",
             "ffn_norm", "w_up", "ffn_conv_w", "w_down", "final_norm"]
    out = [loss, dx.reshape(x.shape)]
    for k in range(4):
        out += [res[name][k] for name in order]
    return tuple(out)
```

```python
import math

import jax
import jax.numpy as jnp
from jax import lax
from jax.experimental import pallas as pl
from jax.experimental.pallas import tpu as pltpu

F32 = jnp.float32
BF16 = jnp.bfloat16
MESH = pl.DeviceIdType.MESH
N_DEV = 8

D_MODEL = 1024
HEAD_DIM = 64
N_HEADS = 8
BLOCK = 128
ATTN_W = 512
KV_W = 128
CONV_W = 512
QKV_W = ATTN_W + 2 * KV_W
CBX_W = 3 * CONV_W
GATE_W = 2 * D_MODEL
IN_W = QKV_W + CBX_W + GATE_W
D_FF = 2816
FF_CHUNK = 1408
NORM_EPS = 1e-5
ATTN_SCALE = HEAD_DIM ** -0.5
NEG = -1e30
HALO = 16

ADAM_LR = 0.001
ADAM_B1 = 0.9
ADAM_B2 = 0.999
ADAM_EPS = 1e-08
ADAM_WD = 0.01
ADAM_STEP = 10

VMEM_LIMIT = 56 * 1024 * 1024
SMALL_ROWS = 32

NT = (((1,), (1,)), ((), ()))
TN = (((0,), (0,)), ((), ()))
ANY = pl.BlockSpec(memory_space=pl.ANY)


def _sig(v):
    return 1.0 / (1.0 + jnp.exp(-v))


def _row_tile(s, pref=256):
    return pref if s % pref == 0 else s


def _shifts_down(u, halo, ks):
    ext = jnp.concatenate([halo, u], axis=0)
    return [pltpu.roll(ext, k, axis=0)[HALO:, :] for k in ks]


def _shifts_up(u, halo, ks):
    n = u.shape[0]
    ext = jnp.concatenate([u, halo], axis=0)
    return [pltpu.roll(ext, n + HALO - k, axis=0)[:n, :] for k in ks]


def _shift_matrix(n, k):
    row = lax.broadcasted_iota(jnp.int32, (n, n), 0)
    col = lax.broadcasted_iota(jnp.int32, (n, n), 1)
    return jnp.where(col == row + k, 1.0, 0.0).astype(BF16)


def _mxu_shift_up(mat, ub, halo, k):
    n = ub.shape[0]
    v = jnp.dot(mat, ub, preferred_element_type=F32)
    row = lax.broadcasted_iota(jnp.int32, (8, ub.shape[1]), 0)
    tail = v[n - 8:, :]
    for t in range(k):
        tail = jnp.where(row == 8 - k + t, halo[t:t + 1, :], tail)
    return jnp.concatenate([v[:n - 8, :], tail], axis=0)


def _rows_reversed(tm, c, steps):
    return pl.BlockSpec((tm, c), lambda i: (steps - 1 - i, 0))


def _prev_halo_map_reversed(tm, steps):
    return lambda i: (jnp.maximum((steps - 1 - i) * (tm // HALO) - 1, 0), 0)


def _prev_halo_map(tm):
    return lambda i: (jnp.maximum(i * (tm // HALO) - 1, 0), 0)


def _next_halo_map(tm, s):
    return lambda i: (jnp.minimum((i + 1) * (tm // HALO), s // HALO - 1), 0)


def _full(shape):
    return pl.BlockSpec(shape, lambda *_: (0,) * len(shape))


def _resident(shape):
    return pl.BlockSpec(shape, lambda *_: (0,) * len(shape), pipeline_mode=pl.Buffered(1))


def _rows(tm, c):
    return pl.BlockSpec((tm, c), lambda i: (i, 0))


def _sds(shape, dtype):
    return jax.ShapeDtypeStruct(shape, dtype)


def _my_place():
    x, y, c = lax.axis_index("x"), lax.axis_index("y"), lax.axis_index("c")
    return x, y, c


ALL_PEERS = tuple((j >> 2, (j >> 1) & 1, j & 1) for j in range(1, N_DEV))
SIBLING_PEER = ((0, 0, 1),)
CHIP_PEERS = ((0, 1, 0), (1, 0, 0), (1, 1, 0))
BARRIER_ID = {ALL_PEERS: 0, SIBLING_PEER: 1, CHIP_PEERS: 2}


def _entry_barrier(peers):
    x, y, c = _my_place()
    barrier = pltpu.get_barrier_semaphore()
    for dx, dy, dc in peers:
        pl.semaphore_signal(barrier, inc=1, device_id=(x ^ dx, y ^ dy, c ^ dc), device_id_type=MESH)
    pl.semaphore_wait(barrier, len(peers))


def _start_exchange(remote, local):
    for cp in local + remote:
        cp.start()


def _finish_exchange(remote, local):
    for cp in remote:
        cp.wait_recv()
    for cp in remote:
        cp.wait_send()
    for cp in local:
        cp.wait()


class _AllGather:
    peers = ALL_PEERS

    def __init__(self, shards):
        self.ins = list(shards)
        n = len(shards)
        self.out_shape = [_sds((N_DEV * s.shape[0], s.shape[1]), s.dtype) for s in shards]
        self.sems = [pltpu.SemaphoreType.DMA((7 * n,)), pltpu.SemaphoreType.DMA((7 * n,)),
                     pltpu.SemaphoreType.DMA((n,))]

    def _parts(self, ins, outs, sems):
        send_sems, recv_sems, local_sems = sems
        x, y, c = _my_place()
        me, sibling = (x, y, c), (x, y, 1 - c)
        chips = [(1 - x, y), (x, 1 - y), (1 - x, 1 - y)]

        def rows(k, dev):
            r = ins[k].shape[0]
            start = pl.multiple_of((4 * dev[0] + 2 * dev[1] + dev[2]) * r, 8)
            return outs[k].at[pl.ds(start, r), :]

        def copy(k, j, block, to, src=None):
            return pltpu.make_async_remote_copy(
                src_ref=rows(k, block) if src is None else src, dst_ref=rows(k, block),
                send_sem=send_sems.at[7 * k + j], recv_sem=recv_sems.at[7 * k + j],
                device_id=to, device_id_type=MESH)

        n = len(ins)
        mine = [pltpu.make_async_copy(ins[k], rows(k, me), local_sems.at[k]) for k in range(n)]
        first = []
        for k in range(n):
            first.append(copy(k, 0, me, sibling, src=ins[k]))
            first += [copy(k, 1 + j, me, (*chip, c), src=ins[k]) for j, chip in enumerate(chips)]
        return me, sibling, chips, copy, mine, first

    def start(self, ins, outs, sems):
        _, _, _, _, mine, first = self._parts(ins, outs, sems)
        _start_exchange(first, mine)

    def finish(self, ins, outs, sems):
        me, sibling, chips, copy, mine, first = self._parts(ins, outs, sems)
        c = me[2]
        n = len(ins)
        passed = []
        for j, chip in enumerate(chips):
            for k in range(n):
                copy(k, 1 + j, (*chip, c), me).wait_recv()
                fwd = copy(k, 4 + j, (*chip, c), sibling)
                fwd.start()
                passed.append(fwd)
        for k in range(n):
            copy(k, 0, sibling, me).wait_recv()
            for j, chip in enumerate(chips):
                copy(k, 4 + j, (*chip, 1 - c), me).wait_recv()
        for cp in first + passed:
            cp.wait_send()
        for cp in mine:
            cp.wait()


class _ReduceScatter:
    peers = ALL_PEERS

    def __init__(self, parts, bcast=()):
        self.parts = [(lo, cnt) for _, lo, cnt in parts]
        self.n_parts = len(parts)
        self.ins = [a for a, _, _ in parts] + list(bcast)
        self.out_shape = [_sds((N_DEV * cnt, a.shape[1]), a.dtype) for a, _, cnt in parts]
        self.out_shape += [_sds((N_DEV * b.shape[0], b.shape[1]), b.dtype) for b in bcast]
        n = len(self.ins)
        self.sems = [pltpu.SemaphoreType.DMA((7 * n,)), pltpu.SemaphoreType.DMA((7 * n,)),
                     pltpu.SemaphoreType.DMA((n,))]

    def _copies(self, ins, outs, sems):
        send_sems, recv_sems, local_sems = sems
        x, y, c = _my_place()
        me_idx = 4 * x + 2 * y + c
        remote, local = [], []
        for k in range(len(ins)):
            cnt = outs[k].shape[0] // N_DEV
            dst = outs[k].at[pl.ds(pl.multiple_of(me_idx * cnt, 8), cnt), :]
            if k < self.n_parts:
                lo, _ = self.parts[k]
                r = ins[k].shape[0] // N_DEV
                src_of = lambda idx: ins[k].at[pl.ds(pl.multiple_of(idx * r + lo, 8), cnt), :]
            else:
                src_of = lambda idx: ins[k]
            local.append(pltpu.make_async_copy(src_of(me_idx), dst, local_sems.at[k]))
            for j in range(1, N_DEV):
                peer = (x ^ (j >> 2), y ^ ((j >> 1) & 1), c ^ (j & 1))
                peer_idx = 4 * peer[0] + 2 * peer[1] + peer[2]
                remote.append(pltpu.make_async_remote_copy(
                    src_ref=src_of(peer_idx), dst_ref=dst,
                    send_sem=send_sems.at[7 * k + j - 1], recv_sem=recv_sems.at[7 * k + j - 1],
                    device_id=peer, device_id_type=MESH))
        return remote, local

    def start(self, ins, outs, sems):
        _start_exchange(*self._copies(ins, outs, sems))

    def finish(self, ins, outs, sems):
        _finish_exchange(*self._copies(ins, outs, sems))


class _PairExchange:
    peers = SIBLING_PEER

    def __init__(self, arrays):
        self.ins = list(arrays)
        n = len(arrays)
        self.out_shape = [_sds((a.shape[0] // 2, a.shape[1]), a.dtype) for a in arrays]
        self.sems = [pltpu.SemaphoreType.DMA((4 * n,)), pltpu.SemaphoreType.DMA((4 * n,))]

    def _copies(self, ins, outs, sems):
        send_sems, recv_sems = sems
        x, y, c = _my_place()
        remote = []
        for k in range(len(ins)):
            r = ins[k].shape[0] // N_DEV
            for chip in range(4):
                sib = ins[k].at[pl.ds(pl.multiple_of((2 * chip + 1 - c) * r, 8), r), :]
                remote.append(pltpu.make_async_remote_copy(
                    src_ref=sib, dst_ref=outs[k].at[pl.ds(chip * r, r), :],
                    send_sem=send_sems.at[4 * k + chip], recv_sem=recv_sems.at[4 * k + chip],
                    device_id=(x, y, 1 - c), device_id_type=MESH))
        return remote

    def start(self, ins, outs, sems):
        for cp in self._copies(ins, outs, sems):
            cp.start()

    def finish(self, ins, outs, sems):
        remote = self._copies(ins, outs, sems)
        for cp in remote:
            cp.wait_recv()
        for cp in remote:
            cp.wait_send()


class _ChipExchange:
    peers = CHIP_PEERS

    def __init__(self, arrays):
        self.ins = list(arrays)
        self.out_shape = [_sds(a.shape, a.dtype) for a in arrays]
        n = len(self.ins)
        self.sems = [pltpu.SemaphoreType.DMA((3 * n,)), pltpu.SemaphoreType.DMA((3 * n,)),
                     pltpu.SemaphoreType.DMA((n,))]

    def _copies(self, ins, outs, sems):
        send_sems, recv_sems, local_sems = sems
        x, y, c = _my_place()
        my_chip = 2 * x + y
        remote, local = [], []
        for k in range(len(ins)):
            r = ins[k].shape[0] // 4
            dst = outs[k].at[pl.ds(pl.multiple_of(my_chip * r, 8), r), :]
            local.append(pltpu.make_async_copy(ins[k].at[pl.ds(pl.multiple_of(my_chip * r, 8), r), :], dst,
                                               local_sems.at[k]))
            for j in range(1, 4):
                px, py = x ^ (j >> 1), y ^ (j & 1)
                src = ins[k].at[pl.ds(pl.multiple_of((2 * px + py) * r, 8), r), :]
                remote.append(pltpu.make_async_remote_copy(
                    src_ref=src, dst_ref=dst, send_sem=send_sems.at[3 * k + j - 1],
                    recv_sem=recv_sems.at[3 * k + j - 1], device_id=(px, py, c), device_id_type=MESH))
        return remote, local

    def start(self, ins, outs, sems):
        _start_exchange(*self._copies(ins, outs, sems))

    def finish(self, ins, outs, sems):
        _finish_exchange(*self._copies(ins, outs, sems))


class _ChipExchangeThenBroadcast(_ChipExchange):
    peers = ALL_PEERS

    def __init__(self, arrays, late_from, late_shapes):
        super().__init__(arrays)
        self.n_chip = len(arrays)
        self.late_from = tuple(late_from)
        self.out_shape += [_sds((N_DEV * r, c), F32) for r, c in late_shapes]
        m = len(late_shapes)
        self.sems += [pltpu.SemaphoreType.DMA((7 * m,)), pltpu.SemaphoreType.DMA((7 * m,)),
                      pltpu.SemaphoreType.DMA((m,))]

    def _late_copies(self, srcs, outs, sems):
        send_sems, recv_sems, local_sems = sems
        x, y, c = _my_place()
        me_idx = 4 * x + 2 * y + c
        remote, local = [], []
        for k, src in enumerate(srcs):
            r = src.shape[0]
            dst = outs[k].at[pl.ds(pl.multiple_of(me_idx * r, 8), r), :]
            local.append(pltpu.make_async_copy(src, dst, local_sems.at[k]))
            for j, (dx, dy, dc) in enumerate(ALL_PEERS):
                remote.append(pltpu.make_async_remote_copy(
                    src_ref=src, dst_ref=dst, send_sem=send_sems.at[7 * k + j], recv_sem=recv_sems.at[7 * k + j],
                    device_id=(x ^ dx, y ^ dy, c ^ dc), device_id_type=MESH))
        return remote, local

    def start(self, ins, outs, sems):
        _start_exchange(*self._copies(ins, outs[:self.n_chip], sems[:3]))

    def finish(self, ins, outs, sems, late_srcs):
        late = self._late_copies(late_srcs, outs[self.n_chip:], sems[3:])
        _start_exchange(*late)
        _finish_exchange(*self._copies(ins, outs[:self.n_chip], sems[:3]))
        _finish_exchange(*late)


def _pcall(body, name, grid, in_specs, out_specs, out_shape, args, scratch=(), comm=None):
    params = pltpu.CompilerParams(dimension_semantics=("arbitrary",) * len(grid), vmem_limit_bytes=VMEM_LIMIT)
    in_specs, out_specs, out_shape, scratch = list(in_specs), list(out_specs), list(out_shape), list(scratch)
    if comm is None:
        res = pl.pallas_call(body, name=name, grid=grid, in_specs=in_specs, out_specs=out_specs, out_shape=out_shape,
                             scratch_shapes=scratch, compiler_params=params)(*args)
        return list(res), []
    n_in, n_out, n_scr = len(in_specs), len(out_specs), len(scratch)
    ci, co = len(comm.ins), len(comm.out_shape)
    total = math.prod(grid)

    def carried(*refs):
        bounds = [0, n_in, n_in + ci, n_in + ci + n_out, n_in + ci + n_out + co, n_in + ci + n_out + co + n_scr]
        ins, cins, outs, couts, scr = (refs[a:b] for a, b in zip(bounds[:-1], bounds[1:]))
        sems = refs[bounds[-1]:]
        step = pl.program_id(0)
        for d in range(1, len(grid)):
            step = step * grid[d] + pl.program_id(d)

        @pl.when(step == 0)
        def _():
            _entry_barrier(comm.peers)
            comm.start(cins, couts, sems)

        body(*ins, *outs, *scr)

        @pl.when(step == total - 1)
        def _():
            late_from = getattr(comm, "late_from", None)
            if late_from is None:
                comm.finish(cins, couts, sems)
            else:
                comm.finish(cins, couts, sems, [outs[k] for k in late_from])

    params = pltpu.CompilerParams(dimension_semantics=("arbitrary",) * len(grid), vmem_limit_bytes=VMEM_LIMIT,
                                  collective_id=BARRIER_ID[comm.peers])
    res = pl.pallas_call(
        carried, name=name, grid=grid, in_specs=in_specs + [ANY] * ci, out_specs=out_specs + [ANY] * co,
        out_shape=out_shape + comm.out_shape, scratch_shapes=scratch + comm.sems, compiler_params=params,
    )(*args, *comm.ins)
    return list(res[:n_out]), list(res[n_out:])


def _exchange_only(comm, name):
    def body(*refs):
        ci, co = len(comm.ins), len(comm.out_shape)
        _entry_barrier(comm.peers)
        comm.start(refs[:ci], refs[ci:ci + co], refs[ci + co:])
        comm.finish(refs[:ci], refs[ci:ci + co], refs[ci + co:])

    params = pltpu.CompilerParams(collective_id=BARRIER_ID[comm.peers])
    return pl.pallas_call(body, name=name, out_shape=comm.out_shape, in_specs=[ANY] * len(comm.ins),
                          out_specs=[ANY] * len(comm.out_shape), scratch_shapes=comm.sems,
                          compiler_params=params)(*comm.ins)


def _norm_inproj(x, g, win_t, b_in, comm):
    s = x.shape[0]
    tm = _row_tile(s, 512)
    widths = (QKV_W, CBX_W, GATE_W)

    def body(x_ref, g_ref, w_ref, b_ref, xn_ref, qkv_ref, cbx_ref, gate_ref):
        xv = x_ref[...]
        r = lax.rsqrt(jnp.mean(xv * xv, axis=-1, keepdims=True) + NORM_EPS)
        xn = (xv * r * g_ref[...]).astype(BF16)
        xn_ref[...] = xn
        off = 0
        for o_ref, w in zip((qkv_ref, cbx_ref, gate_ref), widths):
            acc = lax.dot_general(xn, w_ref[off:off + w, :], NT, preferred_element_type=F32)
            o_ref[...] = (acc + b_ref[:, off:off + w]).astype(BF16)
            off += w

    return _pcall(
        body, "norm_inproj", (s // tm,),
        [_rows(tm, D_MODEL), _full((1, D_MODEL)), _resident((IN_W, D_MODEL)), _full((1, IN_W))],
        [_rows(tm, D_MODEL)] + [_rows(tm, w) for w in widths],
        [_sds((s, D_MODEL), BF16)] + [_sds((s, w), BF16) for w in widths],
        (x, g, win_t, b_in), comm=comm)


def _attn_specs():
    prev = lambda n: jnp.maximum(n - 1, 0)
    return [pl.BlockSpec((BLOCK, ATTN_W), lambda n: (n, 0)),
            pl.BlockSpec((BLOCK, KV_W), lambda n: (prev(n), ATTN_W // KV_W)),
            pl.BlockSpec((BLOCK, KV_W), lambda n: (n, ATTN_W // KV_W)),
            pl.BlockSpec((BLOCK, KV_W), lambda n: (prev(n), ATTN_W // KV_W + 1)),
            pl.BlockSpec((BLOCK, KV_W), lambda n: (n, ATTN_W // KV_W + 1))]


def _lower_lanes():
    return lax.broadcasted_iota(jnp.int32, (BLOCK, 128), 1) < HEAD_DIM


def _stack_heads(val, kh):
    lower = _lower_lanes()
    parts = []
    for g in range(4):
        h = kh * 4 + g
        blk = val[:, (h // 2) * 128:(h // 2 + 1) * 128]
        keep = lower if h % 2 == 0 else jnp.logical_not(lower)
        parts.append(jnp.where(keep, blk, jnp.zeros_like(blk)))
    return jnp.concatenate(parts, axis=0)


def _dup_kv(prev_ref, cur_ref, kh):
    t = jnp.concatenate([prev_ref[...], cur_ref[...]], axis=0).astype(F32)
    rolled = pltpu.roll(t, HEAD_DIM, axis=1)
    lower = lax.broadcasted_iota(jnp.int32, t.shape, 1) < HEAD_DIM
    dup = jnp.where(lower, t, rolled) if kh == 0 else jnp.where(lower, rolled, t)
    return dup.astype(BF16)


def _attn_mask(n):
    row = lax.broadcasted_iota(jnp.int32, (4 * BLOCK, 2 * BLOCK), 0)
    kj = lax.broadcasted_iota(jnp.int32, (4 * BLOCK, 2 * BLOCK), 1)
    dist = (row & (BLOCK - 1)) + BLOCK - kj
    band = jnp.logical_and(dist >= 0, dist < BLOCK)
    return jnp.logical_and(band, jnp.logical_or(kj >= BLOCK, n > 0))


def _sink_col(sinks_ref, kh):
    gi = lax.broadcasted_iota(jnp.int32, (4 * BLOCK, 1), 0) // BLOCK
    col = jnp.zeros((4 * BLOCK, 1), F32)
    for g in range(4):
        col = jnp.where(gi == g, sinks_ref[0, kh * 4 + g], col)
    return col


def _attn_fwd(qkv, sinks, comm):
    s = qkv.shape[0]

    def body(sinks_ref, q_ref, kp_ref, kc_ref, vp_ref, vc_ref, o_ref, lse_ref):
        n = pl.program_id(0)
        mask = _attn_mask(n)
        lower = _lower_lanes()
        lane = lax.broadcasted_iota(jnp.int32, (BLOCK, 128), 1)
        qv = q_ref[...]
        lse_out = jnp.zeros((BLOCK, 128), F32)
        for kh in range(2):
            qs = _stack_heads(qv, kh)
            kd, vd = _dup_kv(kp_ref, kc_ref, kh), _dup_kv(vp_ref, vc_ref, kh)
            sc = lax.dot_general(qs, kd, NT, preferred_element_type=F32) * ATTN_SCALE
            sc = jnp.where(mask, sc, NEG)
            sink = _sink_col(sinks_ref, kh)
            m = jnp.maximum(jnp.max(sc, axis=1, keepdims=True), sink)
            p = jnp.exp(sc - m)
            l = jnp.sum(p, axis=1, keepdims=True) + jnp.exp(sink - m)
            o = jnp.dot(p.astype(BF16), vd, preferred_element_type=F32) / l
            lse = m + jnp.log(l)
            for pair in range(2):
                lo = o[(2 * pair) * BLOCK:(2 * pair + 1) * BLOCK]
                hi = o[(2 * pair + 1) * BLOCK:(2 * pair + 2) * BLOCK]
                col = (kh * 2 + pair) * 128
                o_ref[:, col:col + 128] = jnp.where(lower, lo, hi).astype(BF16)
            for g in range(4):
                lse_out = jnp.where(lane == kh * 4 + g, lse[g * BLOCK:(g + 1) * BLOCK], lse_out)
        lse_ref[...] = lse_out

    return _pcall(
        body, "attn_fwd", (s // BLOCK,),
        [pl.BlockSpec(memory_space=pltpu.SMEM)] + _attn_specs(),
        [pl.BlockSpec((BLOCK, ATTN_W), lambda n: (n, 0)), pl.BlockSpec((BLOCK, 128), lambda n: (n, 0))],
        [_sds((s, ATTN_W), BF16), _sds((s, 128), F32)],
        (sinks, qkv, qkv, qkv, qkv, qkv), comm=comm)


def _conv_u(cbx_ref, halo_ref, w_ref, first):
    cb = cbx_ref[:, 0:CONV_W].astype(F32)
    cc = cbx_ref[:, CONV_W:2 * CONV_W].astype(F32)
    cx = cbx_ref[:, 2 * CONV_W:3 * CONV_W].astype(F32)
    u = cc * cx
    uh = halo_ref[:, CONV_W:2 * CONV_W].astype(F32) * halo_ref[:, 2 * CONV_W:3 * CONV_W].astype(F32)
    uh = jnp.where(first, 0.0, uh)
    u1, u2 = _shifts_down(u, uh, (1, 2))
    cv = w_ref[0:1, :] * u2 + w_ref[1:2, :] * u1 + w_ref[2:3, :] * u
    return cb, cc, cx, u, cv


def _mix_fwd(x, cbx, gates, attn, conv_w, wa, wc, wout, comm):
    s = x.shape[0]
    tm = _row_tile(s)

    def body(x_ref, cbx_ref, halo_ref, gate_ref, attn_ref, cw_ref, wa_ref, wc_ref, wo_ref,
             h1_ref):
        first = pl.program_id(0) == 0
        cb, _, _, _, cv = _conv_u(cbx_ref, halo_ref, cw_ref, first)
        conv = (cb * cv).astype(BF16)
        ap = jnp.dot(attn_ref[...], wa_ref[...], preferred_element_type=F32)
        cp = jnp.dot(conv, wc_ref[...], preferred_element_type=F32)
        ga = gate_ref[:, 0:D_MODEL].astype(F32)
        gc = gate_ref[:, D_MODEL:2 * D_MODEL].astype(F32)
        merged = (_sig(ga) * ap + _sig(gc) * cp).astype(BF16)
        h1_ref[...] = x_ref[...] + jnp.dot(merged, wo_ref[...], preferred_element_type=F32)

    return _pcall(
        body, "mix_fwd", (s // tm,),
        [_rows(tm, D_MODEL), _rows(tm, CBX_W), pl.BlockSpec((HALO, CBX_W), _prev_halo_map(tm)),
         _rows(tm, GATE_W), _rows(tm, ATTN_W), _full((3, CONV_W)), _full((ATTN_W, D_MODEL)),
         _full((CONV_W, D_MODEL)), _full((D_MODEL, D_MODEL))],
        [_rows(tm, D_MODEL)], [_sds((s, D_MODEL), F32)],
        (x, cbx, cbx, gates, attn, conv_w, wa, wc, wout), comm=comm)


def _ffn_up(h1, g, wup_lo, wup_hi, comm):
    s = h1.shape[0]
    tm = _row_tile(s, 512)
    half = D_MODEL // 2

    def body(h_ref, g_ref, wl_ref, wh_ref, hn_ref, up_ref):
        hv = h_ref[...]
        r = lax.rsqrt(jnp.mean(hv * hv, axis=-1, keepdims=True) + NORM_EPS)
        hn = (hv * r * g_ref[...]).astype(BF16)
        hn_ref[...] = hn
        for c in range(2 * D_FF // FF_CHUNK):
            sl = slice(c * FF_CHUNK, (c + 1) * FF_CHUNK)
            acc = lax.dot_general(hn[:, :half], wl_ref[sl, :], NT, preferred_element_type=F32)
            acc = acc + lax.dot_general(hn[:, half:], wh_ref[sl, :], NT, preferred_element_type=F32)
            up_ref[:, sl] = acc.astype(BF16)

    return _pcall(
        body, "ffn_up", (s // tm,),
        [_rows(tm, D_MODEL), _full((1, D_MODEL)), _resident((2 * D_FF, half)), _resident((2 * D_FF, half))],
        [_rows(tm, D_MODEL), _rows(tm, 2 * D_FF)],
        [_sds((s, D_MODEL), BF16), _sds((s, 2 * D_FF), BF16)],
        (h1, g, wup_lo, wup_hi), comm=comm)


def _ffn_conv_cols(up_ref, halo_ref, fcw_ref, first, off):
    u = up_ref[:, off:off + FF_CHUNK].astype(F32)
    uh = jnp.where(first, 0.0, halo_ref[:, off:off + FF_CHUNK].astype(F32))
    w = fcw_ref[:, off:off + FF_CHUNK]
    u1, u2 = _shifts_down(u, uh, (1, 2))
    return w[0:1] * u2 + w[1:2] * u1 + w[2:3] * u


def _ffn_down_loss(up_pre, fcw, wdown, h1, fnorm, target):
    s = h1.shape[0]
    tm = _row_tile(s)

    def body(up_ref, halo_ref, fcw_ref, wd_ref, h1_ref, fn_ref, t_ref, cu_ref, act_ref, dh2_ref, loss_ref, dfn_ref):
        i = pl.program_id(0)

        @pl.when(i == 0)
        def _():
            loss_ref[...] = jnp.zeros_like(loss_ref)
            dfn_ref[...] = jnp.zeros_like(dfn_ref)

        h2 = h1_ref[...]
        for c in range(D_FF // FF_CHUNK):
            gsl = slice(c * FF_CHUNK, (c + 1) * FF_CHUNK)
            vsl = slice(D_FF + c * FF_CHUNK, D_FF + (c + 1) * FF_CHUNK)
            gate = _ffn_conv_cols(up_ref, halo_ref, fcw_ref, i == 0, c * FF_CHUNK)
            cu_ref[:, gsl] = gate.astype(BF16)
            val = _ffn_conv_cols(up_ref, halo_ref, fcw_ref, i == 0, D_FF + c * FF_CHUNK)
            cu_ref[:, vsl] = val.astype(BF16)
            act = (gate * _sig(gate) * val).astype(BF16)
            act_ref[:, gsl] = act
            h2 = h2 + jnp.dot(act, wd_ref[gsl, :], preferred_element_type=F32)
        r = lax.rsqrt(jnp.mean(h2 * h2, axis=-1, keepdims=True) + NORM_EPS)
        yhat = h2 * r
        fn = fn_ref[...]
        diff = yhat * fn - t_ref[...]
        loss_ref[...] += 0.5 * jnp.sum(jnp.sum(diff * diff, axis=1, keepdims=True), axis=0, keepdims=True) / D_MODEL
        dy = diff * (1.0 / D_MODEL)
        dfn_ref[...] += jnp.sum(dy * yhat, axis=0, keepdims=True)
        dyh = dy * fn
        dh2_ref[...] = r * (dyh - yhat * jnp.mean(dyh * yhat, axis=-1, keepdims=True))

    return _pcall(
        body, "ffn_down_loss", (s // tm,),
        [_rows(tm, 2 * D_FF), pl.BlockSpec((HALO, 2 * D_FF), _prev_halo_map(tm)), _full((3, 2 * D_FF)),
         _resident((D_FF, D_MODEL)), _rows(tm, D_MODEL), _full((1, D_MODEL)), _rows(tm, D_MODEL)],
        [_rows(tm, 2 * D_FF), _rows(tm, D_FF), _rows(tm, D_MODEL), _full((1, 128)), _full((1, D_MODEL))],
        [_sds((s, 2 * D_FF), BF16), _sds((s, D_FF), BF16), _sds((s, D_MODEL), F32), _sds((1, 128), F32),
         _sds((1, D_MODEL), F32)],
        (up_pre, up_pre, fcw, wdown, h1, fnorm, target))[0]


def _ffn_bwd(dh2, wdown, up, up_pre, fcw, wup_lo, wup_hi, h1, g, comm):
    s = dh2.shape[0]
    tm = _row_tile(s)
    half = D_MODEL // 2

    def dup_cols(dh, up_ref, wd_ref, c):
        gsl = slice(c * FF_CHUNK, (c + 1) * FF_CHUNK)
        vsl = slice(D_FF + c * FF_CHUNK, D_FF + (c + 1) * FF_CHUNK)
        dact = lax.dot_general(dh, wd_ref[gsl, :], NT, preferred_element_type=F32)
        gate = up_ref[:, gsl].astype(F32)
        val = up_ref[:, vsl].astype(F32)
        sg = _sig(gate)
        return dact * val * (sg * (1.0 + gate * (1.0 - sg))), dact * gate * sg

    def body(dh_ref, wd_ref, up_ref, x_ref, w_ref, wl_ref, wh_ref, h_ref, g_ref,
             dx_ref, dw_ref, dh1_ref, dg_ref, carry_ref):
        @pl.when(pl.program_id(0) == 0)
        def _():
            dw_ref[...] = jnp.zeros_like(dw_ref)
            dg_ref[...] = jnp.zeros_like(dg_ref)
            carry_ref[...] = jnp.zeros_like(carry_ref)

        dh2v = dh_ref[...]
        dh = dh2v.astype(BF16)
        dhn_lo = jnp.zeros((tm, half), F32)
        dhn_hi = jnp.zeros((tm, half), F32)
        for c in range(D_FF // FF_CHUNK):
            for d, off in zip(dup_cols(dh, up_ref, wd_ref, c), (c * FF_CHUNK, D_FF + c * FF_CHUNK)):
                sl = slice(off, off + FF_CHUNK)
                dn = carry_ref[:, sl]
                carry_ref[:, sl] = d[0:HALO, :]
                xv = x_ref[:, sl].astype(F32)
                wv = w_ref[:, sl]
                d1, d2 = _shifts_up(d, dn, (1, 2))
                dx = (wv[2:3] * d + wv[1:2] * d1 + wv[0:1] * d2).astype(BF16)
                dx_ref[:, sl] = dx
                dhn_lo = dhn_lo + jnp.dot(dx, wl_ref[sl, :], preferred_element_type=F32)
                dhn_hi = dhn_hi + jnp.dot(dx, wh_ref[sl, :], preferred_element_type=F32)
                dw_ref[0:1, sl] += jnp.sum(d2 * xv, axis=0, keepdims=True)
                dw_ref[1:2, sl] += jnp.sum(d1 * xv, axis=0, keepdims=True)
                dw_ref[2:3, sl] += jnp.sum(d * xv, axis=0, keepdims=True)
        dx1, dg = _norm_bwd_tile(h_ref[...], g_ref[...], jnp.concatenate([dhn_lo, dhn_hi], axis=1))
        dg_ref[...] += dg
        dh1_ref[...] = dh2v + dx1

    rows = lambda c: _rows_reversed(tm, c, s // tm)
    return _pcall(
        body, "ffn_bwd", (s // tm,),
        [rows(D_MODEL), _resident((D_FF, D_MODEL)), rows(2 * D_FF), rows(2 * D_FF), _full((3, 2 * D_FF)),
         _resident((2 * D_FF, half)), _resident((2 * D_FF, half)), rows(D_MODEL), _full((1, D_MODEL))],
        [rows(2 * D_FF), _full((3, 2 * D_FF)), rows(D_MODEL), _full((1, D_MODEL))],
        [_sds((s, 2 * D_FF), BF16), _sds((3, 2 * D_FF), F32), _sds((s, D_MODEL), F32), _sds((1, D_MODEL), F32)],
        (dh2, wdown, up, up_pre, fcw, wup_lo, wup_hi, h1, g),
        scratch=[pltpu.VMEM((HALO, 2 * D_FF), F32)], comm=comm)


def _matmul_tn(a, b, tk, name, ts=1024, comm=None):
    s, ka = a.shape
    n = b.shape[1]
    ts = min(ts, s)
    steps = s // ts

    def body(a_ref, b_ref, o_ref, acc_ref):
        j = pl.program_id(1)

        @pl.when(j == 0)
        def _():
            acc_ref[...] = jnp.zeros_like(acc_ref)

        acc_ref[...] += lax.dot_general(a_ref[...].astype(BF16), b_ref[...].astype(BF16), TN,
                                        preferred_element_type=F32)

        @pl.when(j == steps - 1)
        def _():
            o_ref[...] = acc_ref[...].astype(BF16)

    outs, couts = _pcall(
        body, name, (ka // tk, steps),
        [pl.BlockSpec((ts, tk), lambda i, j: (j, i)), pl.BlockSpec((ts, n), lambda i, j: (j, 0))],
        [pl.BlockSpec((tk, n), lambda i, j: (i, 0))], [_sds((ka, n), BF16)],
        (a, b), scratch=[pltpu.VMEM((tk, n), F32)], comm=comm)
    return outs[0] if comm is None else (outs[0], couts)


def _norm_bwd_tile(xv, g, dy):
    r = lax.rsqrt(jnp.mean(xv * xv, axis=-1, keepdims=True) + NORM_EPS)
    xhat = xv * r
    dg = jnp.sum(dy * xhat, axis=0, keepdims=True)
    dyh = dy * g
    return r * (dyh - xhat * jnp.mean(dyh * xhat, axis=-1, keepdims=True)), dg


def _ffn_up_bwd(dup_pre, wup_lo, wup_hi, h1, g, dh2, comm):
    s = h1.shape[0]
    tm = _row_tile(s, 512)
    half = D_MODEL // 2

    def body(du_ref, wl_ref, wh_ref, h_ref, g_ref, dh2_ref, dh1_ref, dg_ref):
        @pl.when(pl.program_id(0) == 0)
        def _():
            dg_ref[...] = jnp.zeros_like(dg_ref)

        du = du_ref[...]
        dhn = jnp.concatenate([jnp.dot(du, wl_ref[...], preferred_element_type=F32),
                               jnp.dot(du, wh_ref[...], preferred_element_type=F32)], axis=1)
        dx, dg = _norm_bwd_tile(h_ref[...], g_ref[...], dhn)
        dg_ref[...] += dg
        dh1_ref[...] = dh2_ref[...] + dx

    return _pcall(
        body, "ffn_up_bwd", (s // tm,),
        [_rows(tm, 2 * D_FF), _resident((2 * D_FF, half)), _resident((2 * D_FF, half)), _rows(tm, D_MODEL),
         _full((1, D_MODEL)), _rows(tm, D_MODEL)],
        [_rows(tm, D_MODEL), _full((1, D_MODEL))],
        [_sds((s, D_MODEL), F32), _sds((1, D_MODEL), F32)],
        (dup_pre, wup_lo, wup_hi, h1, g, dh2), comm=comm)


def _mix_bwd(dh1, wout, gates, attn, wa, wc, cbx, conv_w, comm):
    s = dh1.shape[0]
    tm = _row_tile(s)
    steps = s // tm

    def body(dh_ref, wo_ref, gate_ref, attn_ref, wa_ref, wc_ref, cbx_ref, halo_ref,
             cw_ref, dg_ref, dattn_ref, dcb_ref, dcc_ref, dcx_ref, dw_ref, gwo_ref, gwa_ref, gwc_ref,
             acc_o, acc_a, acc_c, carry_ref):
        i = pl.program_id(0)

        @pl.when(i == 0)
        def _():
            dw_ref[...] = jnp.zeros_like(dw_ref)
            acc_o[...] = jnp.zeros_like(acc_o)
            acc_a[...] = jnp.zeros_like(acc_a)
            acc_c[...] = jnp.zeros_like(acc_c)
            carry_ref[...] = jnp.zeros_like(carry_ref)

        cb, cc, cx, u, cv = _conv_u(cbx_ref, halo_ref, cw_ref, i == steps - 1)
        attn = attn_ref[...]
        conv = (cb * cv).astype(BF16)
        ap = jnp.dot(attn, wa_ref[...], preferred_element_type=F32)
        cp = jnp.dot(conv, wc_ref[...], preferred_element_type=F32)
        dhb = dh_ref[...].astype(BF16)
        dm = lax.dot_general(dhb, wo_ref[...], NT, preferred_element_type=F32)
        sa = _sig(gate_ref[:, 0:D_MODEL].astype(F32))
        sc = _sig(gate_ref[:, D_MODEL:2 * D_MODEL].astype(F32))
        merged = (sa * ap + sc * cp).astype(BF16)
        da = (dm * sa).astype(BF16)
        dc = (dm * sc).astype(BF16)
        dg_ref[:, 0:D_MODEL] = (dm * ap * sa * (1.0 - sa)).astype(BF16)
        dg_ref[:, D_MODEL:2 * D_MODEL] = (dm * cp * sc * (1.0 - sc)).astype(BF16)
        dattn_ref[...] = lax.dot_general(da, wa_ref[...], NT, preferred_element_type=F32).astype(BF16)
        dconv = lax.dot_general(dc, wc_ref[...], NT, preferred_element_type=F32)
        dcb_ref[...] = (dconv * cv).astype(BF16)
        d = dconv * cb
        dn = carry_ref[...]
        carry_ref[...] = d[0:HALO, :]
        d1, d2 = _shifts_up(d, dn, (1, 2))
        du = cw_ref[2:3, :] * d + cw_ref[1:2, :] * d1 + cw_ref[0:1, :] * d2
        dcc_ref[...] = (du * cx).astype(BF16)
        dcx_ref[...] = (du * cc).astype(BF16)
        dw_ref[0:1, :] += jnp.sum(d2 * u, axis=0, keepdims=True)
        dw_ref[1:2, :] += jnp.sum(d1 * u, axis=0, keepdims=True)
        dw_ref[2:3, :] += jnp.sum(d * u, axis=0, keepdims=True)
        acc_o[...] += lax.dot_general(merged, dhb, TN, preferred_element_type=F32)
        acc_a[...] += lax.dot_general(attn, da, TN, preferred_element_type=F32)
        acc_c[...] += lax.dot_general(conv, dc, TN, preferred_element_type=F32)

        @pl.when(i == steps - 1)
        def _():
            gwo_ref[...] = acc_o[...].astype(BF16)
            gwa_ref[...] = acc_a[...].astype(BF16)
            gwc_ref[...] = acc_c[...].astype(BF16)

    rows = lambda c: _rows_reversed(tm, c, steps)
    return _pcall(
        body, "mix_bwd", (steps,),
        [rows(D_MODEL), _full((D_MODEL, D_MODEL)), rows(GATE_W), rows(ATTN_W), _full((ATTN_W, D_MODEL)),
         _full((CONV_W, D_MODEL)), rows(CBX_W), pl.BlockSpec((HALO, CBX_W), _prev_halo_map_reversed(tm, steps)),
         _full((3, CONV_W))],
        [rows(GATE_W), rows(ATTN_W), rows(CONV_W), rows(CONV_W), rows(CONV_W),
         _full((3, CONV_W)), _full((D_MODEL, D_MODEL)), _full((ATTN_W, D_MODEL)), _full((CONV_W, D_MODEL))],
        [_sds((s, GATE_W), BF16), _sds((s, ATTN_W), BF16), _sds((s, CONV_W), BF16), _sds((s, CONV_W), BF16),
         _sds((s, CONV_W), BF16), _sds((3, CONV_W), F32), _sds((D_MODEL, D_MODEL), BF16),
         _sds((ATTN_W, D_MODEL), BF16), _sds((CONV_W, D_MODEL), BF16)],
        (dh1, wout, gates, attn, wa, wc, cbx, cbx, conv_w),
        scratch=[pltpu.VMEM((D_MODEL, D_MODEL), F32), pltpu.VMEM((ATTN_W, D_MODEL), F32),
                 pltpu.VMEM((CONV_W, D_MODEL), F32), pltpu.VMEM((HALO, CONV_W), F32)], comm=comm)


def _attn_bwd(qkv, sinks, attn, lse, dattn, comm):
    s = qkv.shape[0]

    def body(sinks_ref, q_ref, kp_ref, kc_ref, vp_ref, vc_ref, o_ref, lse_ref, do_ref,
             dq_ref, dk_ref, dv_ref, ds_ref):
        n = pl.program_id(0)

        @pl.when(n == 0)
        def _():
            dk_ref[...] = jnp.zeros_like(dk_ref)
            dv_ref[...] = jnp.zeros_like(dv_ref)
            ds_ref[...] = jnp.zeros_like(ds_ref)

        mask = _attn_mask(n)
        lower = _lower_lanes()
        lane = lax.broadcasted_iota(jnp.int32, (BLOCK, 128), 1)
        lower2 = lax.broadcasted_iota(jnp.int32, (2 * BLOCK, 128), 1) < HEAD_DIM
        lane1 = lax.broadcasted_iota(jnp.int32, (1, 128), 1)
        qv, ov, dov, lsev = q_ref[...], o_ref[...], do_ref[...], lse_ref[...]
        dk_fold, dv_fold = [], []
        dsink = jnp.zeros((1, 128), F32)
        for kh in range(2):
            qs = _stack_heads(qv, kh)
            dos = _stack_heads(dov, kh)
            os_ = _stack_heads(ov, kh)
            kd, vd = _dup_kv(kp_ref, kc_ref, kh), _dup_kv(vp_ref, vc_ref, kh)
            lse = jnp.concatenate(
                [jnp.sum(jnp.where(lane == kh * 4 + g, lsev, 0.0), axis=1, keepdims=True) for g in range(4)], axis=0)
            sc = lax.dot_general(qs, kd, NT, preferred_element_type=F32) * ATTN_SCALE
            p = jnp.exp(jnp.where(mask, sc, NEG) - lse)
            dp = lax.dot_general(dos, vd, NT, preferred_element_type=F32)
            delta = jnp.sum(dos.astype(F32) * os_.astype(F32), axis=1, keepdims=True)
            dsc = (p * (dp - delta) * ATTN_SCALE).astype(BF16)
            dqs = jnp.dot(dsc, kd, preferred_element_type=F32)
            for pair in range(2):
                lo = dqs[(2 * pair) * BLOCK:(2 * pair + 1) * BLOCK]
                hi = dqs[(2 * pair + 1) * BLOCK:(2 * pair + 2) * BLOCK]
                col = (kh * 2 + pair) * 128
                dq_ref[:, col:col + 128] = jnp.where(lower, lo, hi).astype(BF16)
            dkd = lax.dot_general(dsc, qs, TN, preferred_element_type=F32)
            dvd = lax.dot_general(p.astype(BF16), dos, TN, preferred_element_type=F32)
            dk_fold.append(dkd + pltpu.roll(dkd, HEAD_DIM, axis=1))
            dv_fold.append(dvd + pltpu.roll(dvd, HEAD_DIM, axis=1))
            psink = jnp.exp(_sink_col(sinks_ref, kh) - lse) * delta
            for g in range(4):
                tot = jnp.sum(psink[g * BLOCK:(g + 1) * BLOCK], axis=0, keepdims=True)
                dsink = dsink - jnp.where(lane1 == kh * 4 + g, tot, 0.0)
        dk2 = jnp.where(lower2, dk_fold[0], dk_fold[1])
        dv2 = jnp.where(lower2, dv_fold[0], dv_fold[1])
        ds_ref[...] += dsink
        cur = pl.ds(pl.multiple_of(n * BLOCK, BLOCK), BLOCK)
        dk_ref[cur, :] += dk2[BLOCK:]
        dv_ref[cur, :] += dv2[BLOCK:]

        @pl.when(n > 0)
        def _():
            prev = pl.ds(pl.multiple_of((n - 1) * BLOCK, BLOCK), BLOCK)
            dk_ref[prev, :] += dk2[:BLOCK]
            dv_ref[prev, :] += dv2[:BLOCK]

    blk = lambda w: pl.BlockSpec((BLOCK, w), lambda n: (n, 0))
    return _pcall(
        body, "attn_bwd", (s // BLOCK,),
        [pl.BlockSpec(memory_space=pltpu.SMEM)] + _attn_specs() + [blk(ATTN_W), blk(128), blk(ATTN_W)],
        [blk(ATTN_W), _full((s, KV_W)), _full((s, KV_W)), _full((1, 128))],
        [_sds((s, ATTN_W), BF16), _sds((s, KV_W), F32), _sds((s, KV_W), F32), _sds((1, 128), F32)],
        (sinks, qkv, qkv, qkv, qkv, qkv, attn, lse, dattn), comm=comm)


DPROJ_PIECES = (ATTN_W, KV_W, KV_W, CONV_W, CONV_W, CONV_W, GATE_W)
DPROJ_OFFSETS = tuple(sum(DPROJ_PIECES[:k]) for k in range(len(DPROJ_PIECES)))


def _grad_w_in(pieces, xn, comm):
    s = xn.shape[0]
    ts = min(1024, s)
    steps = s // ts
    rows0 = DPROJ_OFFSETS[6]

    def body(*refs):
        p_refs, b_ref, o_ref, acc_ref, stage_ref, sem = refs[:7], refs[7], refs[8], refs[9], refs[10], refs[11]
        i, j = pl.program_id(0), pl.program_id(1)

        @pl.when(j == 0)
        def _():
            acc_ref[...] = jnp.zeros_like(acc_ref)

        bv = b_ref[...]

        def flush(lo, n):
            stage_ref[0:n, :] = acc_ref[0:n, :].astype(BF16)
            cp = pltpu.make_async_copy(stage_ref.at[0:n, :], o_ref.at[lo:lo + n, :], sem)
            cp.start()
            cp.wait()

        @pl.when(i == 0)
        def _():
            for p_ref, off, w in zip(p_refs[:6], DPROJ_OFFSETS[:6], DPROJ_PIECES[:6]):
                acc_ref[off:off + w, :] += lax.dot_general(p_ref[...].astype(BF16), bv, TN,
                                                           preferred_element_type=F32)

            @pl.when(j == steps - 1)
            def _():
                flush(0, rows0)

        @pl.when(i == 1)
        def _():
            acc_ref[0:GATE_W, :] += lax.dot_general(p_refs[6][...], bv, TN, preferred_element_type=F32)

            @pl.when(j == steps - 1)
            def _():
                flush(rows0, GATE_W)

    def piece_spec(w, group):
        return pl.BlockSpec((ts, w), lambda i, j: (jnp.where(i == group, j, 0), 0))

    outs, couts = _pcall(
        body, "grad_w_in", (2, steps),
        [piece_spec(w, 0) for w in DPROJ_PIECES[:6]] + [piece_spec(GATE_W, 1),
                                                         pl.BlockSpec((ts, D_MODEL), lambda i, j: (j, 0))],
        [ANY], [_sds((IN_W, D_MODEL), BF16)], (*pieces, xn),
        scratch=[pltpu.VMEM((rows0, D_MODEL), F32), pltpu.VMEM((rows0, D_MODEL), BF16), pltpu.SemaphoreType.DMA],
        comm=comm)
    return outs[0], couts


def _inproj_bwd(pieces, win_t, x, g, dh1, comm):
    s = x.shape[0]
    tm = _row_tile(s, 512)

    def body(*refs):
        p_refs = refs[:7]
        w_ref, x_ref, g_ref, dh_ref, dx_ref, db_ref, dg_ref = refs[7:]

        @pl.when(pl.program_id(0) == 0)
        def _():
            db_ref[...] = jnp.zeros_like(db_ref)
            dg_ref[...] = jnp.zeros_like(dg_ref)

        dxn = jnp.zeros((tm, D_MODEL), F32)
        for p_ref, off, w in zip(p_refs, DPROJ_OFFSETS, DPROJ_PIECES):
            v = p_ref[...].astype(BF16)
            db_ref[:, off:off + w] += jnp.sum(v.astype(F32), axis=0, keepdims=True)
            dxn = dxn + jnp.dot(v, w_ref[off:off + w, :], preferred_element_type=F32)
        dx, dg = _norm_bwd_tile(x_ref[...], g_ref[...], dxn)
        dg_ref[...] += dg
        dx_ref[...] = dh_ref[...] + dx

    return _pcall(
        body, "inproj_bwd", (s // tm,),
        [_rows(tm, w) for w in DPROJ_PIECES] + [_resident((IN_W, D_MODEL)), _rows(tm, D_MODEL), _full((1, D_MODEL)),
                                                _rows(tm, D_MODEL)],
        [_rows(tm, D_MODEL), _full((8, IN_W)), _full((8, D_MODEL))],
        [_sds((s, D_MODEL), F32), _sds((8, IN_W), F32), _sds((8, D_MODEL), F32)],
        (*pieces, win_t, x, g, dh1), comm=comm)


def _adam_math(w, g, m, v):
    m2 = ADAM_B1 * m + (1.0 - ADAM_B1) * g
    v2 = ADAM_B2 * v + (1.0 - ADAM_B2) * (g * g)
    m_hat = m2 / (1.0 - ADAM_B1 ** ADAM_STEP)
    v_hat = v2 / (1.0 - ADAM_B2 ** ADAM_STEP)
    delta = -ADAM_LR * (m_hat / (jnp.sqrt(v_hat) + ADAM_EPS) + ADAM_WD * w)
    return delta, m2, v2


def _sum_slots(ref):
    tot = ref[0].astype(F32)
    for i in range(1, ref.shape[0]):
        tot = tot + ref[i].astype(F32)
    return tot


def _pair_add(partials, theirs, tr, name):
    r = partials.shape[0] // N_DEV
    c = partials.shape[1]
    nt = r // tr
    core = lax.axis_index("c").astype(jnp.int32).reshape(1)

    def body(core_ref, a_ref, b_ref, o_ref):
        o_ref[...] = (a_ref[...].astype(F32) + b_ref[...].astype(F32)).astype(BF16)

    grid_spec = pltpu.PrefetchScalarGridSpec(
        num_scalar_prefetch=1, grid=(4 * nt,),
        in_specs=[pl.BlockSpec((None, None, tr, c), lambda i, core_ref: (i // nt, core_ref[0], i % nt, 0)),
                  pl.BlockSpec((tr, c), lambda i, core_ref: (i, 0))],
        out_specs=pl.BlockSpec((tr, c), lambda i, core_ref: (i, 0)))
    return pl.pallas_call(body, name=name, grid_spec=grid_spec, out_shape=_sds((4 * r, c), BF16))(
        core, partials.reshape(4, 2, r, c), theirs)


def _sum_adamw(parts, w, m, v, tr, name):
    r, c = w.shape

    def body(p_ref, w_ref, m_ref, v_ref, g_ref, d_ref, m2_ref, v2_ref):
        g = _sum_slots(p_ref)
        g_ref[...] = g
        d_ref[...], m2_ref[...], v2_ref[...] = _adam_math(w_ref[...], g, m_ref[...], v_ref[...])

    spec = pl.BlockSpec((tr, c), lambda i: (i, 0))
    return _pcall(body, name, (r // tr,), [pl.BlockSpec((N_DEV, tr, c), lambda i: (0, i, 0)), spec, spec, spec],
                  [spec] * 4, [_sds((r, c), F32)] * 4, (parts, w, m, v))[0]


def _sum_parts_adamw(parts, w, m, v, tr, name):
    c = w.shape[1]
    tiles = [p.shape[1] // tr for p in parts]
    starts = [sum(tiles[:k]) for k in range(len(parts))]
    n_parts = len(parts)

    def body(*refs):
        p_refs = refs[:n_parts]
        w_ref, m_ref, v_ref, g_ref, d_ref, m2_ref, v2_ref = refs[n_parts:]
        i = pl.program_id(0)
        for p_ref, st, nt in zip(p_refs, starts, tiles):
            @pl.when(jnp.logical_and(i >= st, i < st + nt))
            def _(p_ref=p_ref):
                g_ref[...] = _sum_slots(p_ref)

        d_ref[...], m2_ref[...], v2_ref[...] = _adam_math(w_ref[...], g_ref[...], m_ref[...], v_ref[...])

    def part_spec(p, st, nt):
        return pl.BlockSpec((p.shape[0], tr, c), lambda i: (0, jnp.clip(i - st, 0, nt - 1), 0))

    spec = pl.BlockSpec((tr, c), lambda i: (i, 0))
    return _pcall(
        body, name, (sum(tiles),),
        [part_spec(p, st, nt) for p, st, nt in zip(parts, starts, tiles)] + [spec, spec, spec],
        [spec] * 4, [_sds(w.shape, F32)] * 4, (*parts, w, m, v))[0]


ROW_MIX, ROW_FFN, ROW_FINAL, ROW_SINKS, ROW_LOSS, ROW_BIN, ROW_CW, ROW_FCW = 0, 1, 2, 3, 4, 5, 10, 13
FCW_ROWS = 6


def _wide_pieces(width):
    return [(k * D_MODEL, min(D_MODEL, width - k * D_MODEL)) for k in range(-(-width // D_MODEL))]


def _pack_small(dffn, dfn, dsink, loss, dcw, dfcw):
    def body(ffn_ref, fn_ref, sink_ref, loss_ref, cw_ref, fcw_ref, o_ref):
        o_ref[...] = jnp.zeros_like(o_ref)
        o_ref[ROW_FFN:ROW_FFN + 1, :] = ffn_ref[...]
        o_ref[ROW_FINAL:ROW_FINAL + 1, :] = fn_ref[...]
        o_ref[ROW_SINKS:ROW_SINKS + 1, 0:128] = sink_ref[...]
        o_ref[ROW_LOSS:ROW_LOSS + 1, 0:128] = loss_ref[...]
        o_ref[ROW_CW:ROW_CW + 3, 0:CONV_W] = cw_ref[...]
        for a in range(3):
            for k, (off, w) in enumerate(_wide_pieces(2 * D_FF)):
                row = ROW_FCW + FCW_ROWS * a + k
                o_ref[row:row + 1, 0:w] = fcw_ref[a:a + 1, off:off + w]

    return pl.pallas_call(body, name="pack_small", out_shape=_sds((SMALL_ROWS, D_MODEL), F32))(
        dffn, dfn, dsink, loss, dcw, dfcw)


def _small_sums_adamw(r_small, r_dmix, r_dbin, params):
    rows = (None, None, ROW_SINKS, ROW_FFN, ROW_FINAL)

    def sum_row0(ref):
        tot = ref[0:1, :]
        for i in range(1, N_DEV):
            tot = tot + ref[8 * i:8 * i + 1, :]
        return tot

    def body(*refs):
        r_ref, late_refs, p_refs, o_refs = refs[0], refs[1:3], refs[3:18], refs[18:]
        tot = _sum_slots(r_ref)
        for k, row in enumerate(rows):
            w_ref, m_ref, v_ref = p_refs[3 * k:3 * k + 3]
            g_ref, d_ref, m2_ref, v2_ref = o_refs[4 * k:4 * k + 4]
            if row is None:
                g_ref[...] = sum_row0(late_refs[k])
            else:
                for j, (off, w) in enumerate(_wide_pieces(w_ref.shape[1])):
                    g_ref[:, off:off + w] = tot[row + j:row + j + 1, 0:w]
            d_ref[...], m2_ref[...], v2_ref[...] = _adam_math(w_ref[...], g_ref[...], m_ref[...], v_ref[...])
        cw_ref, fcw_ref, loss_ref = o_refs[20:]
        cw_ref[...] = tot[ROW_CW:ROW_CW + 3, 0:CONV_W]
        for a in range(3):
            for j, (off, w) in enumerate(_wide_pieces(2 * D_FF)):
                row = ROW_FCW + FCW_ROWS * a + j
                fcw_ref[a:a + 1, off:off + w] = tot[row:row + 1, 0:w]
        loss_ref[...] = tot[ROW_LOSS:ROW_LOSS + 1, 0:128]

    flat = [t for p in params for t in p]
    out_shape = [_sds(p[0].shape, F32) for p in params for _ in range(4)]
    out_shape += [_sds((3, CONV_W), F32), _sds((3, 2 * D_FF), F32), _sds((1, 128), F32)]
    res = pl.pallas_call(body, name="small_sums_adamw", out_shape=out_shape)(r_small, r_dmix, r_dbin, *flat)
    return [tuple(res[4 * k:4 * k + 4]) for k in range(5)], res[20], res[21], res[22]


def _adamw_pair(a, b):
    def body(*refs):
        for k in range(2):
            w_ref, g_ref, m_ref, v_ref = refs[4 * k:4 * k + 4]
            d_ref, m2_ref, v2_ref = refs[8 + 3 * k:8 + 3 * k + 3]
            d_ref[...], m2_ref[...], v2_ref[...] = _adam_math(w_ref[...], g_ref[...], m_ref[...], v_ref[...])

    out_shape = [_sds(a[0].shape, F32)] * 3 + [_sds(b[0].shape, F32)] * 3
    res = pl.pallas_call(body, name="adamw_conv_weights", out_shape=out_shape)(*a, *b)
    return tuple(res[:3]), tuple(res[3:])


def _pad_cols(a, c):
    return jnp.pad(a, ((0, 0), (0, c - a.shape[1])))


def _to_col_slabs(g):
    r = g.shape[0]
    return jnp.transpose(g.reshape(r, N_DEV, 128), (1, 0, 2)).reshape(N_DEV * r, 128)


def _from_col_slabs(t):
    r = t.shape[0] // N_DEV
    return jnp.transpose(t.reshape(N_DEV, r, 128), (1, 0, 2)).reshape(r, N_DEV * 128)


def _slots(t):
    return t.reshape(N_DEV, t.shape[0] // N_DEV, t.shape[1])


def kernel(x, mix_norm, w_in, b_in, sinks, conv_w, w_attn_branch, w_conv_branch, w_out, ffn_norm, w_up, ffn_conv_w, w_down, final_norm, loss_target, m_mix_norm, m_w_in, m_b_in, m_sinks, m_conv_w, m_w_attn_branch, m_w_conv_branch, m_w_out, m_ffn_norm, m_w_up, m_ffn_conv_w, m_w_down, m_final_norm, v_mix_norm, v_w_in, v_b_in, v_sinks, v_conv_w, v_w_attn_branch, v_w_conv_branch, v_w_out, v_ffn_norm, v_w_up, v_ffn_conv_w, v_w_down, v_final_norm):
    xs, tgt = x[0], loss_target[0]
    me = 4 * lax.axis_index("x") + 2 * lax.axis_index("y") + lax.axis_index("c")
    in_rows, up_rows = IN_W // N_DEV, 2 * D_FF // N_DEV

    conv_sh = jnp.concatenate([_pad_cols(ffn_conv_w[0], 768), _pad_cols(conv_w[0], 768),
                               jnp.zeros((2, 768), F32)], axis=0)
    win_sh, wup_sh = w_in[0].T.astype(BF16), w_up[0].T.astype(BF16)
    wout_sh, wdown_sh = w_out[0].astype(BF16), w_down[0].astype(BF16)
    wa_sh, wc_sh = w_attn_branch[0].astype(BF16), w_conv_branch[0].astype(BF16)

    half = D_MODEL // 2
    (win_t,) = _exchange_only(_AllGather([win_sh]), "gather_w_in")
    (xn, qkv, cbx, gates), (wa_s, wc_s, wout, conv_g) = _norm_inproj(
        xs, mix_norm, win_t, b_in, _AllGather([wa_sh, wc_sh, wout_sh, conv_sh]))
    (attn, lse), (wup_lo,) = _attn_fwd(qkv, sinks, _AllGather([wup_sh[:, :half]]))
    wa, wc = _from_col_slabs(wa_s), _from_col_slabs(wc_s)
    conv_g = conv_g.reshape(N_DEV, 8, 768)
    fcw = jnp.transpose(conv_g[:, 0:3, :up_rows], (1, 0, 2)).reshape(3, 2 * D_FF)
    cw = jnp.transpose(conv_g[:, 3:6, :CONV_W // N_DEV], (1, 0, 2)).reshape(3, CONV_W)
    (h1,), (wup_hi,) = _mix_fwd(xs, cbx, gates, attn, cw, wa, wc, wout, _AllGather([wup_sh[:, half:]]))
    (hn, up_pre), (wdown,) = _ffn_up(h1, ffn_norm, wup_lo, wup_hi, _AllGather([wdown_sh]))
    up, act, dh2, loss_p, dfn_p = _ffn_down_loss(up_pre, fcw, wdown, h1, final_norm.reshape(1, D_MODEL), tgt)

    dn_rows, q_up = D_FF // N_DEV, up_rows // 4
    g_wdown = _matmul_tn(act, dh2, FF_CHUNK, "grad_w_down")
    (dup_pre, dfcw_p, dh1, dffn_p), (r_wdown,) = _ffn_bwd(dh2, wdown, up, up_pre, fcw, wup_lo, wup_hi, h1, ffn_norm,
                                                         _ReduceScatter([(g_wdown, 0, dn_rows)]))
    g_wup_t = _matmul_tn(dup_pre, hn, FF_CHUNK, "grad_w_up")
    (dgates, dattn, dcb, dcc, dcx, dcw_p, g_wout, g_wa_nat, g_wc_nat), (r_wup_ab,) = _mix_bwd(
        dh1, wout, gates, attn, wa, wc, cbx, cw, _ReduceScatter([(g_wup_t, 0, 2 * q_up)]))
    g_wa, g_wc = _to_col_slabs(g_wa_nat), _to_col_slabs(g_wc_nat)
    (dq, dk, dv, dsink_p), (r_wup_c, r_wout, r_wa, r_wc) = _attn_bwd(
        qkv, sinks, attn, lse, dattn,
        _ReduceScatter([(g_wup_t, 2 * q_up, q_up), (g_wout, 0, D_MODEL // N_DEV), (g_wa, 0, ATTN_W),
                        (g_wc, 0, CONV_W)]))
    dproj = (dq, dk, dv, dcb, dcc, dcx, dgates)
    small = _pack_small(dffn_p, dfn_p, dsink_p, loss_p, dcw_p, dfcw_p)
    g_win_t, (r_wup_d, r_small) = _grad_w_in(dproj, xn, _ReduceScatter([(g_wup_t, 3 * q_up, q_up)], [small]))
    (win_theirs,) = _exchange_only(_PairExchange([g_win_t]), "pair_exchange_w_in")
    q_win = _pair_add(g_win_t, win_theirs, in_rows // 2, "pair_add_w_in")
    (dx, _, _), (r_win, r_dbin, r_dmix) = _inproj_bwd(
        dproj, win_t, xs, mix_norm, dh1,
        _ChipExchangeThenBroadcast([q_win], late_from=(1, 2), late_shapes=[(8, IN_W), (8, D_MODEL)]))

    fn2, m_fn2, v_fn2 = (t.reshape(1, D_MODEL) for t in (final_norm, m_final_norm, v_final_norm))
    small_res, g_cw_full, g_fcw_full, loss_row = _small_sums_adamw(
        _slots(r_small), r_dmix, r_dbin,
        [(mix_norm, m_mix_norm, v_mix_norm), (b_in, m_b_in, v_b_in), (sinks, m_sinks, v_sinks),
         (ffn_norm, m_ffn_norm, v_ffn_norm), (fn2, m_fn2, v_fn2)])
    loss = loss_row[0, 0]
    g_cw = lax.dynamic_slice_in_dim(g_cw_full, me * (CONV_W // N_DEV), CONV_W // N_DEV, axis=1)
    g_fcw = lax.dynamic_slice_in_dim(g_fcw_full, me * up_rows, up_rows, axis=1)
    cw_res, fcw_res = _adamw_pair((conv_w[0], g_cw, m_conv_w[0], v_conv_w[0]),
                                  (ffn_conv_w[0], g_fcw, m_ffn_conv_w[0], v_ffn_conv_w[0]))

    big = {}
    big["w_in"] = tuple(t.T for t in _sum_parts_adamw(
        [r_win.reshape(4, in_rows, D_MODEL)], w_in[0].T, m_w_in[0].T, v_w_in[0].T, in_rows // 2, "adamw_w_in"))
    big["w_up"] = tuple(t.T for t in _sum_parts_adamw(
        [_slots(r_wup_ab), _slots(r_wup_c), _slots(r_wup_d)], w_up[0].T, m_w_up[0].T, v_w_up[0].T, q_up,
        "adamw_w_up"))
    big["w_out"] = _sum_adamw(_slots(r_wout), w_out[0], m_w_out[0], v_w_out[0], 128, "adamw_w_out")
    big["w_down"] = _sum_adamw(_slots(r_wdown), w_down[0], m_w_down[0], v_w_down[0], dn_rows // 2, "adamw_w_down")
    big["w_attn_branch"] = _sum_adamw(_slots(r_wa), w_attn_branch[0], m_w_attn_branch[0], v_w_attn_branch[0], 256,
                                      "adamw_w_attn_branch")
    big["w_conv_branch"] = _sum_adamw(_slots(r_wc), w_conv_branch[0], m_w_conv_branch[0], v_w_conv_branch[0], 256,
                                      "adamw_w_conv_branch")

    res = dict(zip(("mix_norm", "b_in", "sinks", "ffn_norm"), small_res[:4]))
    res["final_norm"] = tuple(t.reshape(final_norm.shape) for t in small_res[4])
    res["conv_w"] = tuple(t.reshape(conv_w.shape) for t in (g_cw,) + cw_res)
    res["ffn_conv_w"] = tuple(t.reshape(ffn_conv_w.shape) for t in (g_fcw,) + fcw_res)
    for name, ref_w in (("w_in", w_in), ("w_up", w_up), ("w_out", w_out), ("w_down", w_down),
                        ("w_attn_branch", w_attn_branch), ("w_conv_branch", w_conv_branch)):
        res[name] = tuple(t.reshape(ref_w.shape) for t in big[name])

    order = ["mix_norm", "w_in", "b_in", "sinks", "conv_w", "w_attn_branch", "w_conv_branch", "w_out",
             "ffn_norm", "w_up", "ffn_conv_w", "w_down", "final_norm"]
    out = [loss, dx.reshape(x.shape)]
    for k in range(4):
        out += [res[name][k] for name in order]
    return tuple(out)
```

```python
import math

import jax
import jax.numpy as jnp
from jax import lax
from jax.experimental import pallas as pl
from jax.experimental.pallas import tpu as pltpu

F32 = jnp.float32
BF16 = jnp.bfloat16
MESH = pl.DeviceIdType.MESH
N_DEV = 8

D_MODEL = 1024
HEAD_DIM = 64
N_HEADS = 8
BLOCK = 128
ATTN_W = 512
KV_W = 128
CONV_W = 512
QKV_W = ATTN_W + 2 * KV_W
CBX_W = 3 * CONV_W
GATE_W = 2 * D_MODEL
IN_W = QKV_W + CBX_W + GATE_W
D_FF = 2816
FF_CHUNK = 1408
FF_ROW_CHUNKS = [(k * FF_CHUNK, FF_CHUNK) for k in range(4)]
NORM_EPS = 1e-5
ATTN_SCALE = HEAD_DIM ** -0.5
NEG = -1e30
HALO = 16

ADAM_LR = 0.001
ADAM_B1 = 0.9
ADAM_B2 = 0.999
ADAM_EPS = 1e-08
ADAM_WD = 0.01
ADAM_STEP = 10

VMEM_LIMIT = 56 * 1024 * 1024
SMALL_ROWS = 32

NT = (((1,), (1,)), ((), ()))
TN = (((0,), (0,)), ((), ()))
ANY = pl.BlockSpec(memory_space=pl.ANY)


def _sig(v):
    return 1.0 / (1.0 + jnp.exp(-v))


def _row_tile(s, pref=256):
    return pref if s % pref == 0 else s


def _shifts_down(u, halo, ks):
    ext = jnp.concatenate([halo, u], axis=0)
    return [pltpu.roll(ext, k, axis=0)[HALO:, :] for k in ks]


def _shifts_up(u, halo, ks):
    n = u.shape[0]
    ext = jnp.concatenate([u, halo], axis=0)
    return [pltpu.roll(ext, n + HALO - k, axis=0)[:n, :] for k in ks]


def _rows_reversed(tm, c, steps):
    return pl.BlockSpec((tm, c), lambda i: (steps - 1 - i, 0))


def _prev_halo_map_reversed(tm, steps):
    return lambda i: (jnp.maximum((steps - 1 - i) * (tm // HALO) - 1, 0), 0)


def _prev_halo_map(tm):
    return lambda i: (jnp.maximum(i * (tm // HALO) - 1, 0), 0)


def _full(shape):
    return pl.BlockSpec(shape, lambda *_: (0,) * len(shape))


def _stream_weight(w_hbm, buf, sems, bounds, first):
    def copy(c):
        off, rows = bounds[c]
        return pltpu.make_async_copy(w_hbm.at[off:off + rows, :], buf.at[off:off + rows, :], sems.at[c])

    @pl.when(first)
    def _():
        for c in range(len(bounds)):
            copy(c).start()

    def chunk(c):
        @pl.when(first)
        def _():
            copy(c).wait()

        off, rows = bounds[c]
        return buf[off:off + rows, :]

    return chunk


def _streamed(shape, n_chunks):
    return [pltpu.VMEM(shape, BF16), pltpu.SemaphoreType.DMA((n_chunks,))]


def _rows(tm, c):
    return pl.BlockSpec((tm, c), lambda i: (i, 0))


def _sds(shape, dtype):
    return jax.ShapeDtypeStruct(shape, dtype)


def _my_place():
    x, y, c = lax.axis_index("x"), lax.axis_index("y"), lax.axis_index("c")
    return x, y, c


ALL_PEERS = tuple((j >> 2, (j >> 1) & 1, j & 1) for j in range(1, N_DEV))
SIBLING_PEER = ((0, 0, 1),)
CHIP_PEERS = ((0, 1, 0), (1, 0, 0), (1, 1, 0))
BARRIER_ID = {ALL_PEERS: 0, SIBLING_PEER: 1, CHIP_PEERS: 2}


def _entry_barrier(peers):
    x, y, c = _my_place()
    barrier = pltpu.get_barrier_semaphore()
    for dx, dy, dc in peers:
        pl.semaphore_signal(barrier, inc=1, device_id=(x ^ dx, y ^ dy, c ^ dc), device_id_type=MESH)
    pl.semaphore_wait(barrier, len(peers))


def _start_exchange(remote, local):
    for cp in local + remote:
        cp.start()


def _finish_exchange(remote, local):
    for cp in remote:
        cp.wait_recv()
    for cp in remote:
        cp.wait_send()
    for cp in local:
        cp.wait()


class _AllGather:
    peers = ALL_PEERS

    def __init__(self, shards):
        self.ins = list(shards)
        n = len(shards)
        self.out_shape = [_sds((N_DEV * s.shape[0], s.shape[1]), s.dtype) for s in shards]
        self.sems = [pltpu.SemaphoreType.DMA((7 * n,)), pltpu.SemaphoreType.DMA((7 * n,)),
                     pltpu.SemaphoreType.DMA((n,))]

    def _parts(self, ins, outs, sems):
        send_sems, recv_sems, local_sems = sems
        x, y, c = _my_place()
        me, sibling = (x, y, c), (x, y, 1 - c)
        chips = [(1 - x, y), (x, 1 - y), (1 - x, 1 - y)]

        def rows(k, dev):
            r = ins[k].shape[0]
            start = pl.multiple_of((4 * dev[0] + 2 * dev[1] + dev[2]) * r, 8)
            return outs[k].at[pl.ds(start, r), :]

        def copy(k, j, block, to, src=None):
            return pltpu.make_async_remote_copy(
                src_ref=rows(k, block) if src is None else src, dst_ref=rows(k, block),
                send_sem=send_sems.at[7 * k + j], recv_sem=recv_sems.at[7 * k + j],
                device_id=to, device_id_type=MESH)

        n = len(ins)
        mine = [pltpu.make_async_copy(ins[k], rows(k, me), local_sems.at[k]) for k in range(n)]
        first = []
        for k in range(n):
            first.append(copy(k, 0, me, sibling, src=ins[k]))
            first += [copy(k, 1 + j, me, (*chip, c), src=ins[k]) for j, chip in enumerate(chips)]
        return me, sibling, chips, copy, mine, first

    def start(self, ins, outs, sems):
        _, _, _, _, mine, first = self._parts(ins, outs, sems)
        _start_exchange(first, mine)

    def finish(self, ins, outs, sems):
        me, sibling, chips, copy, mine, first = self._parts(ins, outs, sems)
        c = me[2]
        n = len(ins)
        passed = []
        for j, chip in enumerate(chips):
            for k in range(n):
                copy(k, 1 + j, (*chip, c), me).wait_recv()
                fwd = copy(k, 4 + j, (*chip, c), sibling)
                fwd.start()
                passed.append(fwd)
        for k in range(n):
            copy(k, 0, sibling, me).wait_recv()
            for j, chip in enumerate(chips):
                copy(k, 4 + j, (*chip, 1 - c), me).wait_recv()
        for cp in first + passed:
            cp.wait_send()
        for cp in mine:
            cp.wait()


class _ReduceScatter:
    peers = ALL_PEERS

    def __init__(self, parts, bcast=()):
        self.parts = [(lo, cnt) for _, lo, cnt in parts]
        self.n_parts = len(parts)
        self.ins = [a for a, _, _ in parts] + list(bcast)
        self.out_shape = [_sds((N_DEV * cnt, a.shape[1]), a.dtype) for a, _, cnt in parts]
        self.out_shape += [_sds((N_DEV * b.shape[0], b.shape[1]), b.dtype) for b in bcast]
        n = len(self.ins)
        self.sems = [pltpu.SemaphoreType.DMA((7 * n,)), pltpu.SemaphoreType.DMA((7 * n,)),
                     pltpu.SemaphoreType.DMA((n,))]

    def _copies(self, ins, outs, sems):
        send_sems, recv_sems, local_sems = sems
        x, y, c = _my_place()
        me_idx = 4 * x + 2 * y + c
        remote, local = [], []
        for k in range(len(ins)):
            cnt = outs[k].shape[0] // N_DEV
            dst = outs[k].at[pl.ds(pl.multiple_of(me_idx * cnt, 8), cnt), :]
            if k < self.n_parts:
                lo, _ = self.parts[k]
                r = ins[k].shape[0] // N_DEV
                src_of = lambda idx: ins[k].at[pl.ds(pl.multiple_of(idx * r + lo, 8), cnt), :]
            else:
                src_of = lambda idx: ins[k]
            local.append(pltpu.make_async_copy(src_of(me_idx), dst, local_sems.at[k]))
            for j in range(1, N_DEV):
                peer = (x ^ (j >> 2), y ^ ((j >> 1) & 1), c ^ (j & 1))
                peer_idx = 4 * peer[0] + 2 * peer[1] + peer[2]
                remote.append(pltpu.make_async_remote_copy(
                    src_ref=src_of(peer_idx), dst_ref=dst,
                    send_sem=send_sems.at[7 * k + j - 1], recv_sem=recv_sems.at[7 * k + j - 1],
                    device_id=peer, device_id_type=MESH))
        return remote, local

    def start(self, ins, outs, sems):
        _start_exchange(*self._copies(ins, outs, sems))

    def finish(self, ins, outs, sems):
        _finish_exchange(*self._copies(ins, outs, sems))


class _PairExchange:
    peers = SIBLING_PEER

    def __init__(self, arrays):
        self.ins = list(arrays)
        n = len(arrays)
        self.out_shape = [_sds((a.shape[0] // 2, a.shape[1]), a.dtype) for a in arrays]
        self.sems = [pltpu.SemaphoreType.DMA((4 * n,)), pltpu.SemaphoreType.DMA((4 * n,))]

    def _copies(self, ins, outs, sems):
        send_sems, recv_sems = sems
        x, y, c = _my_place()
        remote = []
        for k in range(len(ins)):
            r = ins[k].shape[0] // N_DEV
            for chip in range(4):
                sib = ins[k].at[pl.ds(pl.multiple_of((2 * chip + 1 - c) * r, 8), r), :]
                remote.append(pltpu.make_async_remote_copy(
                    src_ref=sib, dst_ref=outs[k].at[pl.ds(chip * r, r), :],
                    send_sem=send_sems.at[4 * k + chip], recv_sem=recv_sems.at[4 * k + chip],
                    device_id=(x, y, 1 - c), device_id_type=MESH))
        return remote

    def start(self, ins, outs, sems):
        for cp in self._copies(ins, outs, sems):
            cp.start()

    def finish(self, ins, outs, sems):
        remote = self._copies(ins, outs, sems)
        for cp in remote:
            cp.wait_recv()
        for cp in remote:
            cp.wait_send()


class _ChipExchange:
    peers = CHIP_PEERS

    def __init__(self, arrays):
        self.ins = list(arrays)
        self.out_shape = [_sds(a.shape, a.dtype) for a in arrays]
        n = len(self.ins)
        self.sems = [pltpu.SemaphoreType.DMA((3 * n,)), pltpu.SemaphoreType.DMA((3 * n,)),
                     pltpu.SemaphoreType.DMA((n,))]

    def _copies(self, ins, outs, sems):
        send_sems, recv_sems, local_sems = sems
        x, y, c = _my_place()
        my_chip = 2 * x + y
        remote, local = [], []
        for k in range(len(ins)):
            r = ins[k].shape[0] // 4
            dst = outs[k].at[pl.ds(pl.multiple_of(my_chip * r, 8), r), :]
            local.append(pltpu.make_async_copy(ins[k].at[pl.ds(pl.multiple_of(my_chip * r, 8), r), :], dst,
                                               local_sems.at[k]))
            for j in range(1, 4):
                px, py = x ^ (j >> 1), y ^ (j & 1)
                src = ins[k].at[pl.ds(pl.multiple_of((2 * px + py) * r, 8), r), :]
                remote.append(pltpu.make_async_remote_copy(
                    src_ref=src, dst_ref=dst, send_sem=send_sems.at[3 * k + j - 1],
                    recv_sem=recv_sems.at[3 * k + j - 1], device_id=(px, py, c), device_id_type=MESH))
        return remote, local

    def start(self, ins, outs, sems):
        _start_exchange(*self._copies(ins, outs, sems))

    def finish(self, ins, outs, sems):
        _finish_exchange(*self._copies(ins, outs, sems))


class _ChipExchangeThenBroadcast(_ChipExchange):
    peers = ALL_PEERS

    def __init__(self, arrays, late_from, late_shapes):
        super().__init__(arrays)
        self.n_chip = len(arrays)
        self.late_from = tuple(late_from)
        self.out_shape += [_sds((N_DEV * r, c), F32) for r, c in late_shapes]
        m = len(late_shapes)
        self.sems += [pltpu.SemaphoreType.DMA((7 * m,)), pltpu.SemaphoreType.DMA((7 * m,)),
                      pltpu.SemaphoreType.DMA((m,))]

    def _late_copies(self, srcs, outs, sems):
        send_sems, recv_sems, local_sems = sems
        x, y, c = _my_place()
        me_idx = 4 * x + 2 * y + c
        remote, local = [], []
        for k, src in enumerate(srcs):
            r = src.shape[0]
            dst = outs[k].at[pl.ds(pl.multiple_of(me_idx * r, 8), r), :]
            local.append(pltpu.make_async_copy(src, dst, local_sems.at[k]))
            for j, (dx, dy, dc) in enumerate(ALL_PEERS):
                remote.append(pltpu.make_async_remote_copy(
                    src_ref=src, dst_ref=dst, send_sem=send_sems.at[7 * k + j], recv_sem=recv_sems.at[7 * k + j],
                    device_id=(x ^ dx, y ^ dy, c ^ dc), device_id_type=MESH))
        return remote, local

    def start(self, ins, outs, sems):
        _start_exchange(*self._copies(ins, outs[:self.n_chip], sems[:3]))

    def finish(self, ins, outs, sems, late_srcs):
        late = self._late_copies(late_srcs, outs[self.n_chip:], sems[3:])
        _start_exchange(*late)
        _finish_exchange(*self._copies(ins, outs[:self.n_chip], sems[:3]))
        _finish_exchange(*late)


def _pcall(body, name, grid, in_specs, out_specs, out_shape, args, scratch=(), comm=None):
    params = pltpu.CompilerParams(dimension_semantics=("arbitrary",) * len(grid), vmem_limit_bytes=VMEM_LIMIT)
    in_specs, out_specs, out_shape, scratch = list(in_specs), list(out_specs), list(out_shape), list(scratch)
    if comm is None:
        res = pl.pallas_call(body, name=name, grid=grid, in_specs=in_specs, out_specs=out_specs, out_shape=out_shape,
                             scratch_shapes=scratch, compiler_params=params)(*args)
        return list(res), []
    n_in, n_out, n_scr = len(in_specs), len(out_specs), len(scratch)
    ci, co = len(comm.ins), len(comm.out_shape)
    total = math.prod(grid)

    def carried(*refs):
        bounds = [0, n_in, n_in + ci, n_in + ci + n_out, n_in + ci + n_out + co, n_in + ci + n_out + co + n_scr]
        ins, cins, outs, couts, scr = (refs[a:b] for a, b in zip(bounds[:-1], bounds[1:]))
        sems = refs[bounds[-1]:]
        step = pl.program_id(0)
        for d in range(1, len(grid)):
            step = step * grid[d] + pl.program_id(d)

        @pl.when(step == 0)
        def _():
            _entry_barrier(comm.peers)
            comm.start(cins, couts, sems)

        body(*ins, *outs, *scr)

        @pl.when(step == total - 1)
        def _():
            late_from = getattr(comm, "late_from", None)
            if late_from is None:
                comm.finish(cins, couts, sems)
            else:
                comm.finish(cins, couts, sems, [outs[k] for k in late_from])

    params = pltpu.CompilerParams(dimension_semantics=("arbitrary",) * len(grid), vmem_limit_bytes=VMEM_LIMIT,
                                  collective_id=BARRIER_ID[comm.peers])
    res = pl.pallas_call(
        carried, name=name, grid=grid, in_specs=in_specs + [ANY] * ci, out_specs=out_specs + [ANY] * co,
        out_shape=out_shape + comm.out_shape, scratch_shapes=scratch + comm.sems, compiler_params=params,
    )(*args, *comm.ins)
    return list(res[:n_out]), list(res[n_out:])


def _exchange_only(comm, name):
    def body(*refs):
        ci, co = len(comm.ins), len(comm.out_shape)
        _entry_barrier(comm.peers)
        comm.start(refs[:ci], refs[ci:ci + co], refs[ci + co:])
        comm.finish(refs[:ci], refs[ci:ci + co], refs[ci + co:])

    params = pltpu.CompilerParams(collective_id=BARRIER_ID[comm.peers])
    return pl.pallas_call(body, name=name, out_shape=comm.out_shape, in_specs=[ANY] * len(comm.ins),
                          out_specs=[ANY] * len(comm.out_shape), scratch_shapes=comm.sems,
                          compiler_params=params)(*comm.ins)


def _norm_inproj(x, g, win_t, b_in, comm):
    s = x.shape[0]
    tm = _row_tile(s, 512)
    widths = (QKV_W, CBX_W, GATE_W)

    bounds = [(sum(widths[:k]), w) for k, w in enumerate(widths)]

    def body(x_ref, g_ref, w_hbm, b_ref, xn_ref, qkv_ref, cbx_ref, gate_ref, w_buf, w_sems):
        w_chunk = _stream_weight(w_hbm, w_buf, w_sems, bounds, pl.program_id(0) == 0)
        xv = x_ref[...]
        r = lax.rsqrt(jnp.mean(xv * xv, axis=-1, keepdims=True) + NORM_EPS)
        xn = (xv * r * g_ref[...]).astype(BF16)
        xn_ref[...] = xn
        off = 0
        for o_ref, w in zip((qkv_ref, cbx_ref, gate_ref), widths):
            acc = lax.dot_general(xn, w_chunk(bounds.index((off, w))), NT, preferred_element_type=F32)
            o_ref[...] = (acc + b_ref[:, off:off + w]).astype(BF16)
            off += w

    return _pcall(
        body, "norm_inproj", (s // tm,),
        [_rows(tm, D_MODEL), _full((1, D_MODEL)), ANY, _full((1, IN_W))],
        [_rows(tm, D_MODEL)] + [_rows(tm, w) for w in widths],
        [_sds((s, D_MODEL), BF16)] + [_sds((s, w), BF16) for w in widths],
        (x, g, win_t, b_in), scratch=_streamed((IN_W, D_MODEL), len(widths)), comm=comm)


def _attn_specs():
    prev = lambda n: jnp.maximum(n - 1, 0)
    return [pl.BlockSpec((BLOCK, ATTN_W), lambda n: (n, 0)),
            pl.BlockSpec((BLOCK, KV_W), lambda n: (prev(n), ATTN_W // KV_W)),
            pl.BlockSpec((BLOCK, KV_W), lambda n: (n, ATTN_W // KV_W)),
            pl.BlockSpec((BLOCK, KV_W), lambda n: (prev(n), ATTN_W // KV_W + 1)),
            pl.BlockSpec((BLOCK, KV_W), lambda n: (n, ATTN_W // KV_W + 1))]


def _lower_lanes():
    return lax.broadcasted_iota(jnp.int32, (BLOCK, 128), 1) < HEAD_DIM


def _stack_heads(val, kh):
    lower = _lower_lanes()
    parts = []
    for g in range(4):
        h = kh * 4 + g
        blk = val[:, (h // 2) * 128:(h // 2 + 1) * 128]
        keep = lower if h % 2 == 0 else jnp.logical_not(lower)
        parts.append(jnp.where(keep, blk, jnp.zeros_like(blk)))
    return jnp.concatenate(parts, axis=0)


def _dup_kv(prev_ref, cur_ref, kh):
    t = jnp.concatenate([prev_ref[...], cur_ref[...]], axis=0).astype(F32)
    rolled = pltpu.roll(t, HEAD_DIM, axis=1)
    lower = lax.broadcasted_iota(jnp.int32, t.shape, 1) < HEAD_DIM
    dup = jnp.where(lower, t, rolled) if kh == 0 else jnp.where(lower, rolled, t)
    return dup.astype(BF16)


def _attn_mask(n):
    row = lax.broadcasted_iota(jnp.int32, (4 * BLOCK, 2 * BLOCK), 0)
    kj = lax.broadcasted_iota(jnp.int32, (4 * BLOCK, 2 * BLOCK), 1)
    dist = (row & (BLOCK - 1)) + BLOCK - kj
    band = jnp.logical_and(dist >= 0, dist < BLOCK)
    return jnp.logical_and(band, jnp.logical_or(kj >= BLOCK, n > 0))


def _sink_col(sinks_ref, kh):
    gi = lax.broadcasted_iota(jnp.int32, (4 * BLOCK, 1), 0) // BLOCK
    col = jnp.zeros((4 * BLOCK, 1), F32)
    for g in range(4):
        col = jnp.where(gi == g, sinks_ref[0, kh * 4 + g], col)
    return col


def _attn_fwd(qkv, sinks, comm):
    s = qkv.shape[0]

    def body(sinks_ref, q_ref, kp_ref, kc_ref, vp_ref, vc_ref, o_ref, lse_ref):
        n = pl.program_id(0)
        mask = _attn_mask(n)
        lower = _lower_lanes()
        lane = lax.broadcasted_iota(jnp.int32, (BLOCK, 128), 1)
        qv = q_ref[...]
        lse_out = jnp.zeros((BLOCK, 128), F32)
        for kh in range(2):
            qs = _stack_heads(qv, kh)
            kd, vd = _dup_kv(kp_ref, kc_ref, kh), _dup_kv(vp_ref, vc_ref, kh)
            sc = lax.dot_general(qs, kd, NT, preferred_element_type=F32) * ATTN_SCALE
            sc = jnp.where(mask, sc, NEG)
            sink = _sink_col(sinks_ref, kh)
            m = jnp.maximum(jnp.max(sc, axis=1, keepdims=True), sink)
            p = jnp.exp(sc - m)
            l = jnp.sum(p, axis=1, keepdims=True) + jnp.exp(sink - m)
            o = jnp.dot(p.astype(BF16), vd, preferred_element_type=F32) / l
            lse = m + jnp.log(l)
            for pair in range(2):
                lo = o[(2 * pair) * BLOCK:(2 * pair + 1) * BLOCK]
                hi = o[(2 * pair + 1) * BLOCK:(2 * pair + 2) * BLOCK]
                col = (kh * 2 + pair) * 128
                o_ref[:, col:col + 128] = jnp.where(lower, lo, hi).astype(BF16)
            for g in range(4):
                lse_out = jnp.where(lane == kh * 4 + g, lse[g * BLOCK:(g + 1) * BLOCK], lse_out)
        lse_ref[...] = lse_out

    return _pcall(
        body, "attn_fwd", (s // BLOCK,),
        [pl.BlockSpec(memory_space=pltpu.SMEM)] + _attn_specs(),
        [pl.BlockSpec((BLOCK, ATTN_W), lambda n: (n, 0)), pl.BlockSpec((BLOCK, 128), lambda n: (n, 0))],
        [_sds((s, ATTN_W), BF16), _sds((s, 128), F32)],
        (sinks, qkv, qkv, qkv, qkv, qkv), comm=comm)


def _conv_u(cbx_ref, halo_ref, w_ref, first):
    cb = cbx_ref[:, 0:CONV_W].astype(F32)
    cc = cbx_ref[:, CONV_W:2 * CONV_W].astype(F32)
    cx = cbx_ref[:, 2 * CONV_W:3 * CONV_W].astype(F32)
    u = cc * cx
    uh = halo_ref[:, CONV_W:2 * CONV_W].astype(F32) * halo_ref[:, 2 * CONV_W:3 * CONV_W].astype(F32)
    uh = jnp.where(first, 0.0, uh)
    u1, u2 = _shifts_down(u, uh, (1, 2))
    cv = w_ref[0:1, :] * u2 + w_ref[1:2, :] * u1 + w_ref[2:3, :] * u
    return cb, cc, cx, u, cv


def _mix_fwd(x, cbx, gates, attn, conv_w, wa, wc, wout, comm):
    s = x.shape[0]
    tm = _row_tile(s)

    def body(x_ref, cbx_ref, halo_ref, gate_ref, attn_ref, cw_ref, wa_ref, wc_ref, wo_ref,
             h1_ref):
        first = pl.program_id(0) == 0
        cb, _, _, _, cv = _conv_u(cbx_ref, halo_ref, cw_ref, first)
        conv = (cb * cv).astype(BF16)
        ap = jnp.dot(attn_ref[...], wa_ref[...], preferred_element_type=F32)
        cp = jnp.dot(conv, wc_ref[...], preferred_element_type=F32)
        ga = gate_ref[:, 0:D_MODEL].astype(F32)
        gc = gate_ref[:, D_MODEL:2 * D_MODEL].astype(F32)
        merged = (_sig(ga) * ap + _sig(gc) * cp).astype(BF16)
        h1_ref[...] = x_ref[...] + jnp.dot(merged, wo_ref[...], preferred_element_type=F32)

    return _pcall(
        body, "mix_fwd", (s // tm,),
        [_rows(tm, D_MODEL), _rows(tm, CBX_W), pl.BlockSpec((HALO, CBX_W), _prev_halo_map(tm)),
         _rows(tm, GATE_W), _rows(tm, ATTN_W), _full((3, CONV_W)), _full((ATTN_W, D_MODEL)),
         _full((CONV_W, D_MODEL)), _full((D_MODEL, D_MODEL))],
        [_rows(tm, D_MODEL)], [_sds((s, D_MODEL), F32)],
        (x, cbx, cbx, gates, attn, conv_w, wa, wc, wout), comm=comm)


def _ffn_up(h1, g, wup_lo, wup_hi, comm):
    s = h1.shape[0]
    tm = _row_tile(s, 512)
    half = D_MODEL // 2

    def body(h_ref, g_ref, wl_hbm, wh_hbm, hn_ref, up_ref, wl_buf, wl_sems, wh_buf, wh_sems):
        first = pl.program_id(0) == 0
        wl = _stream_weight(wl_hbm, wl_buf, wl_sems, FF_ROW_CHUNKS, first)
        wh = _stream_weight(wh_hbm, wh_buf, wh_sems, FF_ROW_CHUNKS, first)
        hv = h_ref[...]
        r = lax.rsqrt(jnp.mean(hv * hv, axis=-1, keepdims=True) + NORM_EPS)
        hn = (hv * r * g_ref[...]).astype(BF16)
        hn_ref[...] = hn
        for c in range(2 * D_FF // FF_CHUNK):
            sl = slice(c * FF_CHUNK, (c + 1) * FF_CHUNK)
            acc = lax.dot_general(hn[:, :half], wl(c), NT, preferred_element_type=F32)
            acc = acc + lax.dot_general(hn[:, half:], wh(c), NT, preferred_element_type=F32)
            up_ref[:, sl] = acc.astype(BF16)

    return _pcall(
        body, "ffn_up", (s // tm,),
        [_rows(tm, D_MODEL), _full((1, D_MODEL)), ANY, ANY],
        [_rows(tm, D_MODEL), _rows(tm, 2 * D_FF)],
        [_sds((s, D_MODEL), BF16), _sds((s, 2 * D_FF), BF16)],
        (h1, g, wup_lo, wup_hi), scratch=_streamed((2 * D_FF, half), 4) + _streamed((2 * D_FF, half), 4),
        comm=comm)


def _ffn_conv_cols(up_ref, halo_ref, fcw_ref, first, off):
    u = up_ref[:, off:off + FF_CHUNK].astype(F32)
    uh = jnp.where(first, 0.0, halo_ref[:, off:off + FF_CHUNK].astype(F32))
    w = fcw_ref[:, off:off + FF_CHUNK]
    u1, u2 = _shifts_down(u, uh, (1, 2))
    return w[0:1] * u2 + w[1:2] * u1 + w[2:3] * u


def _ffn_down_loss(up_pre, fcw, wdown, h1, fnorm, target):
    s = h1.shape[0]
    tm = _row_tile(s)

    def body(up_ref, halo_ref, fcw_ref, wd_hbm, h1_ref, fn_ref, t_ref, cu_ref, act_ref, dh2_ref, loss_ref, dfn_ref,
             wd_buf, wd_sems):
        i = pl.program_id(0)
        wd = _stream_weight(wd_hbm, wd_buf, wd_sems, FF_ROW_CHUNKS[:2], i == 0)

        @pl.when(i == 0)
        def _():
            loss_ref[...] = jnp.zeros_like(loss_ref)
            dfn_ref[...] = jnp.zeros_like(dfn_ref)

        h2 = h1_ref[...]
        for c in range(D_FF // FF_CHUNK):
            gsl = slice(c * FF_CHUNK, (c + 1) * FF_CHUNK)
            vsl = slice(D_FF + c * FF_CHUNK, D_FF + (c + 1) * FF_CHUNK)
            gate = _ffn_conv_cols(up_ref, halo_ref, fcw_ref, i == 0, c * FF_CHUNK)
            cu_ref[:, gsl] = gate.astype(BF16)
            val = _ffn_conv_cols(up_ref, halo_ref, fcw_ref, i == 0, D_FF + c * FF_CHUNK)
            cu_ref[:, vsl] = val.astype(BF16)
            act = (gate * _sig(gate) * val).astype(BF16)
            act_ref[:, gsl] = act
            h2 = h2 + jnp.dot(act, wd(c), preferred_element_type=F32)
        r = lax.rsqrt(jnp.mean(h2 * h2, axis=-1, keepdims=True) + NORM_EPS)
        yhat = h2 * r
        fn = fn_ref[...]
        diff = yhat * fn - t_ref[...]
        loss_ref[...] += 0.5 * jnp.sum(jnp.sum(diff * diff, axis=1, keepdims=True), axis=0, keepdims=True) / D_MODEL
        dy = diff * (1.0 / D_MODEL)
        dfn_ref[...] += jnp.sum(dy * yhat, axis=0, keepdims=True)
        dyh = dy * fn
        dh2_ref[...] = r * (dyh - yhat * jnp.mean(dyh * yhat, axis=-1, keepdims=True))

    return _pcall(
        body, "ffn_down_loss", (s // tm,),
        [_rows(tm, 2 * D_FF), pl.BlockSpec((HALO, 2 * D_FF), _prev_halo_map(tm)), _full((3, 2 * D_FF)),
         ANY, _rows(tm, D_MODEL), _full((1, D_MODEL)), _rows(tm, D_MODEL)],
        [_rows(tm, 2 * D_FF), _rows(tm, D_FF), _rows(tm, D_MODEL), _full((1, 128)), _full((1, D_MODEL))],
        [_sds((s, 2 * D_FF), BF16), _sds((s, D_FF), BF16), _sds((s, D_MODEL), F32), _sds((1, 128), F32),
         _sds((1, D_MODEL), F32)],
        (up_pre, up_pre, fcw, wdown, h1, fnorm, target), scratch=_streamed((D_FF, D_MODEL), 2))[0]


def _ffn_bwd(dh2, wdown, up, up_pre, fcw, wup_lo, wup_hi, h1, g, comm):
    s = dh2.shape[0]
    tm = _row_tile(s)
    half = D_MODEL // 2

    def dup_cols(dh, up_ref, wd_c, c):
        gsl = slice(c * FF_CHUNK, (c + 1) * FF_CHUNK)
        vsl = slice(D_FF + c * FF_CHUNK, D_FF + (c + 1) * FF_CHUNK)
        dact = lax.dot_general(dh, wd_c, NT, preferred_element_type=F32)
        gate = up_ref[:, gsl].astype(F32)
        val = up_ref[:, vsl].astype(F32)
        sg = _sig(gate)
        return dact * val * (sg * (1.0 + gate * (1.0 - sg))), dact * gate * sg

    def body(dh_ref, wd_hbm, up_ref, x_ref, w_ref, wl_hbm, wh_hbm, h_ref, g_ref,
             dx_ref, dw_ref, dh1_ref, dg_ref, carry_ref, wd_buf, wd_sems, wl_buf, wl_sems, wh_buf, wh_sems):
        first = pl.program_id(0) == 0
        wd = _stream_weight(wd_hbm, wd_buf, wd_sems, FF_ROW_CHUNKS[:2], first)
        wl = _stream_weight(wl_hbm, wl_buf, wl_sems, FF_ROW_CHUNKS, first)
        wh = _stream_weight(wh_hbm, wh_buf, wh_sems, FF_ROW_CHUNKS, first)

        @pl.when(first)
        def _():
            dw_ref[...] = jnp.zeros_like(dw_ref)
            dg_ref[...] = jnp.zeros_like(dg_ref)
            carry_ref[...] = jnp.zeros_like(carry_ref)

        dh2v = dh_ref[...]
        dh = dh2v.astype(BF16)
        dhn_lo = jnp.zeros((tm, half), F32)
        dhn_hi = jnp.zeros((tm, half), F32)
        for c in range(D_FF // FF_CHUNK):
            for d, off in zip(dup_cols(dh, up_ref, wd(c), c), (c * FF_CHUNK, D_FF + c * FF_CHUNK)):
                sl = slice(off, off + FF_CHUNK)
                dn = carry_ref[:, sl]
                carry_ref[:, sl] = d[0:HALO, :]
                xv = x_ref[:, sl].astype(F32)
                wv = w_ref[:, sl]
                d1, d2 = _shifts_up(d, dn, (1, 2))
                dx = (wv[2:3] * d + wv[1:2] * d1 + wv[0:1] * d2).astype(BF16)
                dx_ref[:, sl] = dx
                dhn_lo = dhn_lo + jnp.dot(dx, wl(off // FF_CHUNK), preferred_element_type=F32)
                dhn_hi = dhn_hi + jnp.dot(dx, wh(off // FF_CHUNK), preferred_element_type=F32)
                dw_ref[0:1, sl] += jnp.sum(d2 * xv, axis=0, keepdims=True)
                dw_ref[1:2, sl] += jnp.sum(d1 * xv, axis=0, keepdims=True)
                dw_ref[2:3, sl] += jnp.sum(d * xv, axis=0, keepdims=True)
        dx1, dg = _norm_bwd_tile(h_ref[...], g_ref[...], jnp.concatenate([dhn_lo, dhn_hi], axis=1))
        dg_ref[...] += dg
        dh1_ref[...] = dh2v + dx1

    rows = lambda c: _rows_reversed(tm, c, s // tm)
    return _pcall(
        body, "ffn_bwd", (s // tm,),
        [rows(D_MODEL), ANY, rows(2 * D_FF), rows(2 * D_FF), _full((3, 2 * D_FF)), ANY, ANY, rows(D_MODEL),
         _full((1, D_MODEL))],
        [rows(2 * D_FF), _full((3, 2 * D_FF)), rows(D_MODEL), _full((1, D_MODEL))],
        [_sds((s, 2 * D_FF), BF16), _sds((3, 2 * D_FF), F32), _sds((s, D_MODEL), F32), _sds((1, D_MODEL), F32)],
        (dh2, wdown, up, up_pre, fcw, wup_lo, wup_hi, h1, g),
        scratch=[pltpu.VMEM((HALO, 2 * D_FF), F32)] + _streamed((D_FF, D_MODEL), 2)
        + _streamed((2 * D_FF, half), 4) + _streamed((2 * D_FF, half), 4), comm=comm)


def _matmul_tn(a, b, tk, name, ts=1024, comm=None):
    s, ka = a.shape
    n = b.shape[1]
    ts = min(ts, s)
    steps = s // ts

    def body(a_ref, b_ref, o_ref, acc_ref):
        j = pl.program_id(1)

        @pl.when(j == 0)
        def _():
            acc_ref[...] = jnp.zeros_like(acc_ref)

        acc_ref[...] += lax.dot_general(a_ref[...].astype(BF16), b_ref[...].astype(BF16), TN,
                                        preferred_element_type=F32)

        @pl.when(j == steps - 1)
        def _():
            o_ref[...] = acc_ref[...].astype(BF16)

    outs, couts = _pcall(
        body, name, (ka // tk, steps),
        [pl.BlockSpec((ts, tk), lambda i, j: (j, i)), pl.BlockSpec((ts, n), lambda i, j: (j, 0))],
        [pl.BlockSpec((tk, n), lambda i, j: (i, 0))], [_sds((ka, n), BF16)],
        (a, b), scratch=[pltpu.VMEM((tk, n), F32)], comm=comm)
    return outs[0] if comm is None else (outs[0], couts)


def _norm_bwd_tile(xv, g, dy):
    r = lax.rsqrt(jnp.mean(xv * xv, axis=-1, keepdims=True) + NORM_EPS)
    xhat = xv * r
    dg = jnp.sum(dy * xhat, axis=0, keepdims=True)
    dyh = dy * g
    return r * (dyh - xhat * jnp.mean(dyh * xhat, axis=-1, keepdims=True)), dg


def _mix_bwd(dh1, wout, gates, attn, wa, wc, cbx, conv_w, comm):
    s = dh1.shape[0]
    tm = _row_tile(s)
    steps = s // tm

    def body(dh_ref, wo_ref, gate_ref, attn_ref, wa_ref, wc_ref, cbx_ref, halo_ref,
             cw_ref, dg_ref, dattn_ref, dcb_ref, dcc_ref, dcx_ref, dw_ref, gwo_ref, gwa_ref, gwc_ref,
             acc_o, acc_a, acc_c, carry_ref):
        i = pl.program_id(0)

        @pl.when(i == 0)
        def _():
            dw_ref[...] = jnp.zeros_like(dw_ref)
            acc_o[...] = jnp.zeros_like(acc_o)
            acc_a[...] = jnp.zeros_like(acc_a)
            acc_c[...] = jnp.zeros_like(acc_c)
            carry_ref[...] = jnp.zeros_like(carry_ref)

        cb, cc, cx, u, cv = _conv_u(cbx_ref, halo_ref, cw_ref, i == steps - 1)
        attn = attn_ref[...]
        conv = (cb * cv).astype(BF16)
        ap = jnp.dot(attn, wa_ref[...], preferred_element_type=F32)
        cp = jnp.dot(conv, wc_ref[...], preferred_element_type=F32)
        dhb = dh_ref[...].astype(BF16)
        dm = lax.dot_general(dhb, wo_ref[...], NT, preferred_element_type=F32)
        sa = _sig(gate_ref[:, 0:D_MODEL].astype(F32))
        sc = _sig(gate_ref[:, D_MODEL:2 * D_MODEL].astype(F32))
        merged = (sa * ap + sc * cp).astype(BF16)
        da = (dm * sa).astype(BF16)
        dc = (dm * sc).astype(BF16)
        dg_ref[:, 0:D_MODEL] = (dm * ap * sa * (1.0 - sa)).astype(BF16)
        dg_ref[:, D_MODEL:2 * D_MODEL] = (dm * cp * sc * (1.0 - sc)).astype(BF16)
        dattn_ref[...] = lax.dot_general(da, wa_ref[...], NT, preferred_element_type=F32).astype(BF16)
        dconv = lax.dot_general(dc, wc_ref[...], NT, preferred_element_type=F32)
        dcb_ref[...] = (dconv * cv).astype(BF16)
        d = dconv * cb
        dn = carry_ref[...]
        carry_ref[...] = d[0:HALO, :]
        d1, d2 = _shifts_up(d, dn, (1, 2))
        du = cw_ref[2:3, :] * d + cw_ref[1:2, :] * d1 + cw_ref[0:1, :] * d2
        dcc_ref[...] = (du * cx).astype(BF16)
        dcx_ref[...] = (du * cc).astype(BF16)
        dw_ref[0:1, :] += jnp.sum(d2 * u, axis=0, keepdims=True)
        dw_ref[1:2, :] += jnp.sum(d1 * u, axis=0, keepdims=True)
        dw_ref[2:3, :] += jnp.sum(d * u, axis=0, keepdims=True)
        acc_o[...] += lax.dot_general(merged, dhb, TN, preferred_element_type=F32)
        acc_a[...] += lax.dot_general(attn, da, TN, preferred_element_type=F32)
        acc_c[...] += lax.dot_general(conv, dc, TN, preferred_element_type=F32)

        @pl.when(i == steps - 1)
        def _():
            gwo_ref[...] = acc_o[...].astype(BF16)
            gwa_ref[...] = acc_a[...].astype(BF16)
            gwc_ref[...] = acc_c[...].astype(BF16)

    rows = lambda c: _rows_reversed(tm, c, steps)
    return _pcall(
        body, "mix_bwd", (steps,),
        [rows(D_MODEL), _full((D_MODEL, D_MODEL)), rows(GATE_W), rows(ATTN_W), _full((ATTN_W, D_MODEL)),
         _full((CONV_W, D_MODEL)), rows(CBX_W), pl.BlockSpec((HALO, CBX_W), _prev_halo_map_reversed(tm, steps)),
         _full((3, CONV_W))],
        [rows(GATE_W), rows(ATTN_W), rows(CONV_W), rows(CONV_W), rows(CONV_W),
         _full((3, CONV_W)), _full((D_MODEL, D_MODEL)), _full((ATTN_W, D_MODEL)), _full((CONV_W, D_MODEL))],
        [_sds((s, GATE_W), BF16), _sds((s, ATTN_W), BF16), _sds((s, CONV_W), BF16), _sds((s, CONV_W), BF16),
         _sds((s, CONV_W), BF16), _sds((3, CONV_W), F32), _sds((D_MODEL, D_MODEL), BF16),
         _sds((ATTN_W, D_MODEL), BF16), _sds((CONV_W, D_MODEL), BF16)],
        (dh1, wout, gates, attn, wa, wc, cbx, cbx, conv_w),
        scratch=[pltpu.VMEM((D_MODEL, D_MODEL), F32), pltpu.VMEM((ATTN_W, D_MODEL), F32),
                 pltpu.VMEM((CONV_W, D_MODEL), F32), pltpu.VMEM((HALO, CONV_W), F32)], comm=comm)


def _attn_bwd(qkv, sinks, attn, lse, dattn, comm):
    s = qkv.shape[0]

    def body(sinks_ref, q_ref, kp_ref, kc_ref, vp_ref, vc_ref, o_ref, lse_ref, do_ref,
             dq_ref, dk_ref, dv_ref, ds_ref):
        n = pl.program_id(0)

        @pl.when(n == 0)
        def _():
            dk_ref[...] = jnp.zeros_like(dk_ref)
            dv_ref[...] = jnp.zeros_like(dv_ref)
            ds_ref[...] = jnp.zeros_like(ds_ref)

        mask = _attn_mask(n)
        lower = _lower_lanes()
        lane = lax.broadcasted_iota(jnp.int32, (BLOCK, 128), 1)
        lower2 = lax.broadcasted_iota(jnp.int32, (2 * BLOCK, 128), 1) < HEAD_DIM
        lane1 = lax.broadcasted_iota(jnp.int32, (1, 128), 1)
        qv, ov, dov, lsev = q_ref[...], o_ref[...], do_ref[...], lse_ref[...]
        dk_fold, dv_fold = [], []
        dsink = jnp.zeros((1, 128), F32)
        for kh in range(2):
            qs = _stack_heads(qv, kh)
            dos = _stack_heads(dov, kh)
            os_ = _stack_heads(ov, kh)
            kd, vd = _dup_kv(kp_ref, kc_ref, kh), _dup_kv(vp_ref, vc_ref, kh)
            lse = jnp.concatenate(
                [jnp.sum(jnp.where(lane == kh * 4 + g, lsev, 0.0), axis=1, keepdims=True) for g in range(4)], axis=0)
            sc = lax.dot_general(qs, kd, NT, preferred_element_type=F32) * ATTN_SCALE
            p = jnp.exp(jnp.where(mask, sc, NEG) - lse)
            dp = lax.dot_general(dos, vd, NT, preferred_element_type=F32)
            delta = jnp.sum(dos.astype(F32) * os_.astype(F32), axis=1, keepdims=True)
            dsc = (p * (dp - delta) * ATTN_SCALE).astype(BF16)
            dqs = jnp.dot(dsc, kd, preferred_element_type=F32)
            for pair in range(2):
                lo = dqs[(2 * pair) * BLOCK:(2 * pair + 1) * BLOCK]
                hi = dqs[(2 * pair + 1) * BLOCK:(2 * pair + 2) * BLOCK]
                col = (kh * 2 + pair) * 128
                dq_ref[:, col:col + 128] = jnp.where(lower, lo, hi).astype(BF16)
            dkd = lax.dot_general(dsc, qs, TN, preferred_element_type=F32)
            dvd = lax.dot_general(p.astype(BF16), dos, TN, preferred_element_type=F32)
            dk_fold.append(dkd + pltpu.roll(dkd, HEAD_DIM, axis=1))
            dv_fold.append(dvd + pltpu.roll(dvd, HEAD_DIM, axis=1))
            psink = jnp.exp(_sink_col(sinks_ref, kh) - lse) * delta
            for g in range(4):
                tot = jnp.sum(psink[g * BLOCK:(g + 1) * BLOCK], axis=0, keepdims=True)
                dsink = dsink - jnp.where(lane1 == kh * 4 + g, tot, 0.0)
        dk2 = jnp.where(lower2, dk_fold[0], dk_fold[1])
        dv2 = jnp.where(lower2, dv_fold[0], dv_fold[1])
        ds_ref[...] += dsink
        cur = pl.ds(pl.multiple_of(n * BLOCK, BLOCK), BLOCK)
        dk_ref[cur, :] += dk2[BLOCK:]
        dv_ref[cur, :] += dv2[BLOCK:]

        @pl.when(n > 0)
        def _():
            prev = pl.ds(pl.multiple_of((n - 1) * BLOCK, BLOCK), BLOCK)
            dk_ref[prev, :] += dk2[:BLOCK]
            dv_ref[prev, :] += dv2[:BLOCK]

    blk = lambda w: pl.BlockSpec((BLOCK, w), lambda n: (n, 0))
    return _pcall(
        body, "attn_bwd", (s // BLOCK,),
        [pl.BlockSpec(memory_space=pltpu.SMEM)] + _attn_specs() + [blk(ATTN_W), blk(128), blk(ATTN_W)],
        [blk(ATTN_W), _full((s, KV_W)), _full((s, KV_W)), _full((1, 128))],
        [_sds((s, ATTN_W), BF16), _sds((s, KV_W), F32), _sds((s, KV_W), F32), _sds((1, 128), F32)],
        (sinks, qkv, qkv, qkv, qkv, qkv, attn, lse, dattn), comm=comm)


DPROJ_PIECES = (ATTN_W, KV_W, KV_W, CONV_W, CONV_W, CONV_W, GATE_W)
DPROJ_OFFSETS = tuple(sum(DPROJ_PIECES[:k]) for k in range(len(DPROJ_PIECES)))


def _grad_w_in(pieces, xn, comm):
    s = xn.shape[0]
    ts = min(1024, s)
    steps = s // ts
    rows0 = DPROJ_OFFSETS[6]

    def body(*refs):
        p_refs, b_ref, o_ref, acc_ref, stage_ref, sem = refs[:7], refs[7], refs[8], refs[9], refs[10], refs[11]
        i, j = pl.program_id(0), pl.program_id(1)

        @pl.when(j == 0)
        def _():
            acc_ref[...] = jnp.zeros_like(acc_ref)

        bv = b_ref[...]

        def flush(lo, n):
            stage_ref[0:n, :] = acc_ref[0:n, :].astype(BF16)
            cp = pltpu.make_async_copy(stage_ref.at[0:n, :], o_ref.at[lo:lo + n, :], sem)
            cp.start()
            cp.wait()

        @pl.when(i == 0)
        def _():
            for p_ref, off, w in zip(p_refs[:6], DPROJ_OFFSETS[:6], DPROJ_PIECES[:6]):
                acc_ref[off:off + w, :] += lax.dot_general(p_ref[...].astype(BF16), bv, TN,
                                                           preferred_element_type=F32)

            @pl.when(j == steps - 1)
            def _():
                flush(0, rows0)

        @pl.when(i == 1)
        def _():
            acc_ref[0:GATE_W, :] += lax.dot_general(p_refs[6][...], bv, TN, preferred_element_type=F32)

            @pl.when(j == steps - 1)
            def _():
                flush(rows0, GATE_W)

    def piece_spec(w, group):
        return pl.BlockSpec((ts, w), lambda i, j: (jnp.where(i == group, j, 0), 0))

    outs, couts = _pcall(
        body, "grad_w_in", (2, steps),
        [piece_spec(w, 0) for w in DPROJ_PIECES[:6]] + [piece_spec(GATE_W, 1),
                                                         pl.BlockSpec((ts, D_MODEL), lambda i, j: (j, 0))],
        [ANY], [_sds((IN_W, D_MODEL), BF16)], (*pieces, xn),
        scratch=[pltpu.VMEM((rows0, D_MODEL), F32), pltpu.VMEM((rows0, D_MODEL), BF16), pltpu.SemaphoreType.DMA],
        comm=comm)
    return outs[0], couts


def _inproj_bwd(pieces, win_t, x, g, dh1, comm):
    s = x.shape[0]
    tm = _row_tile(s, 512)

    def body(*refs):
        p_refs = refs[:7]
        w_hbm, x_ref, g_ref, dh_ref, dx_ref, db_ref, dg_ref, w_buf, w_sems = refs[7:]
        w_chunk = _stream_weight(w_hbm, w_buf, w_sems, list(zip(DPROJ_OFFSETS, DPROJ_PIECES)),
                                 pl.program_id(0) == 0)

        @pl.when(pl.program_id(0) == 0)
        def _():
            db_ref[...] = jnp.zeros_like(db_ref)
            dg_ref[...] = jnp.zeros_like(dg_ref)

        dxn = jnp.zeros((tm, D_MODEL), F32)
        for k, (p_ref, off, w) in enumerate(zip(p_refs, DPROJ_OFFSETS, DPROJ_PIECES)):
            v = p_ref[...].astype(BF16)
            db_ref[:, off:off + w] += jnp.sum(v.astype(F32), axis=0, keepdims=True)
            dxn = dxn + jnp.dot(v, w_chunk(k), preferred_element_type=F32)
        dx, dg = _norm_bwd_tile(x_ref[...], g_ref[...], dxn)
        dg_ref[...] += dg
        dx_ref[...] = dh_ref[...] + dx

    return _pcall(
        body, "inproj_bwd", (s // tm,),
        [_rows(tm, w) for w in DPROJ_PIECES] + [ANY, _rows(tm, D_MODEL), _full((1, D_MODEL)), _rows(tm, D_MODEL)],
        [_rows(tm, D_MODEL), _full((8, IN_W)), _full((8, D_MODEL))],
        [_sds((s, D_MODEL), F32), _sds((8, IN_W), F32), _sds((8, D_MODEL), F32)],
        (*pieces, win_t, x, g, dh1), scratch=_streamed((IN_W, D_MODEL), len(DPROJ_PIECES)), comm=comm)


def _adam_math(w, g, m, v):
    m2 = ADAM_B1 * m + (1.0 - ADAM_B1) * g
    v2 = ADAM_B2 * v + (1.0 - ADAM_B2) * (g * g)
    m_hat = m2 / (1.0 - ADAM_B1 ** ADAM_STEP)
    v_hat = v2 / (1.0 - ADAM_B2 ** ADAM_STEP)
    delta = -ADAM_LR * (m_hat / (jnp.sqrt(v_hat) + ADAM_EPS) + ADAM_WD * w)
    return delta, m2, v2


def _sum_slots(ref):
    tot = ref[0].astype(F32)
    for i in range(1, ref.shape[0]):
        tot = tot + ref[i].astype(F32)
    return tot


def _pair_add(partials, theirs, tr, name):
    r = partials.shape[0] // N_DEV
    c = partials.shape[1]
    nt = r // tr
    core = lax.axis_index("c").astype(jnp.int32).reshape(1)

    def body(core_ref, a_ref, b_ref, o_ref):
        o_ref[...] = (a_ref[...].astype(F32) + b_ref[...].astype(F32)).astype(BF16)

    grid_spec = pltpu.PrefetchScalarGridSpec(
        num_scalar_prefetch=1, grid=(4 * nt,),
        in_specs=[pl.BlockSpec((None, None, tr, c), lambda i, core_ref: (i // nt, core_ref[0], i % nt, 0)),
                  pl.BlockSpec((tr, c), lambda i, core_ref: (i, 0))],
        out_specs=pl.BlockSpec((tr, c), lambda i, core_ref: (i, 0)))
    return pl.pallas_call(body, name=name, grid_spec=grid_spec, out_shape=_sds((4 * r, c), BF16))(
        core, partials.reshape(4, 2, r, c), theirs)


def _sum_adamw(parts, w, m, v, tr, name):
    r, c = w.shape

    def body(p_ref, w_ref, m_ref, v_ref, g_ref, d_ref, m2_ref, v2_ref):
        g = _sum_slots(p_ref)
        g_ref[...] = g
        d_ref[...], m2_ref[...], v2_ref[...] = _adam_math(w_ref[...], g, m_ref[...], v_ref[...])

    spec = pl.BlockSpec((tr, c), lambda i: (i, 0))
    return _pcall(body, name, (r // tr,), [pl.BlockSpec((N_DEV, tr, c), lambda i: (0, i, 0)), spec, spec, spec],
                  [spec] * 4, [_sds((r, c), F32)] * 4, (parts, w, m, v))[0]


def _sum_parts_adamw(parts, w, m, v, tr, name):
    c = w.shape[1]
    tiles = [p.shape[1] // tr for p in parts]
    starts = [sum(tiles[:k]) for k in range(len(parts))]
    n_parts = len(parts)

    def body(*refs):
        p_refs = refs[:n_parts]
        w_ref, m_ref, v_ref, g_ref, d_ref, m2_ref, v2_ref = refs[n_parts:]
        i = pl.program_id(0)
        for p_ref, st, nt in zip(p_refs, starts, tiles):
            @pl.when(jnp.logical_and(i >= st, i < st + nt))
            def _(p_ref=p_ref):
                g_ref[...] = _sum_slots(p_ref)

        d_ref[...], m2_ref[...], v2_ref[...] = _adam_math(w_ref[...], g_ref[...], m_ref[...], v_ref[...])

    def part_spec(p, st, nt):
        return pl.BlockSpec((p.shape[0], tr, c), lambda i: (0, jnp.clip(i - st, 0, nt - 1), 0))

    spec = pl.BlockSpec((tr, c), lambda i: (i, 0))
    return _pcall(
        body, name, (sum(tiles),),
        [part_spec(p, st, nt) for p, st, nt in zip(parts, starts, tiles)] + [spec, spec, spec],
        [spec] * 4, [_sds(w.shape, F32)] * 4, (*parts, w, m, v))[0]


ROW_MIX, ROW_FFN, ROW_FINAL, ROW_SINKS, ROW_LOSS, ROW_BIN, ROW_CW, ROW_FCW = 0, 1, 2, 3, 4, 5, 10, 13
FCW_ROWS = 6


def _wide_pieces(width):
    return [(k * D_MODEL, min(D_MODEL, width - k * D_MODEL)) for k in range(-(-width // D_MODEL))]


def _pack_small(dffn, dfn, dsink, loss, dcw, dfcw):
    def body(ffn_ref, fn_ref, sink_ref, loss_ref, cw_ref, fcw_ref, o_ref):
        o_ref[...] = jnp.zeros_like(o_ref)
        o_ref[ROW_FFN:ROW_FFN + 1, :] = ffn_ref[...]
        o_ref[ROW_FINAL:ROW_FINAL + 1, :] = fn_ref[...]
        o_ref[ROW_SINKS:ROW_SINKS + 1, 0:128] = sink_ref[...]
        o_ref[ROW_LOSS:ROW_LOSS + 1, 0:128] = loss_ref[...]
        o_ref[ROW_CW:ROW_CW + 3, 0:CONV_W] = cw_ref[...]
        for a in range(3):
            for k, (off, w) in enumerate(_wide_pieces(2 * D_FF)):
                row = ROW_FCW + FCW_ROWS * a + k
                o_ref[row:row + 1, 0:w] = fcw_ref[a:a + 1, off:off + w]

    return pl.pallas_call(body, name="pack_small", out_shape=_sds((SMALL_ROWS, D_MODEL), F32))(
        dffn, dfn, dsink, loss, dcw, dfcw)


def _small_sums_adamw(r_small, r_dmix, r_dbin, params):
    rows = (None, None, ROW_SINKS, ROW_FFN, ROW_FINAL)

    def sum_row0(ref):
        tot = ref[0:1, :]
        for i in range(1, N_DEV):
            tot = tot + ref[8 * i:8 * i + 1, :]
        return tot

    def body(*refs):
        r_ref, late_refs, p_refs, o_refs = refs[0], refs[1:3], refs[3:18], refs[18:]
        tot = _sum_slots(r_ref)
        for k, row in enumerate(rows):
            w_ref, m_ref, v_ref = p_refs[3 * k:3 * k + 3]
            g_ref, d_ref, m2_ref, v2_ref = o_refs[4 * k:4 * k + 4]
            if row is None:
                g_ref[...] = sum_row0(late_refs[k])
            else:
                for j, (off, w) in enumerate(_wide_pieces(w_ref.shape[1])):
                    g_ref[:, off:off + w] = tot[row + j:row + j + 1, 0:w]
            d_ref[...], m2_ref[...], v2_ref[...] = _adam_math(w_ref[...], g_ref[...], m_ref[...], v_ref[...])
        cw_ref, fcw_ref, loss_ref = o_refs[20:]
        cw_ref[...] = tot[ROW_CW:ROW_CW + 3, 0:CONV_W]
        for a in range(3):
            for j, (off, w) in enumerate(_wide_pieces(2 * D_FF)):
                row = ROW_FCW + FCW_ROWS * a + j
                fcw_ref[a:a + 1, off:off + w] = tot[row:row + 1, 0:w]
        loss_ref[...] = tot[ROW_LOSS:ROW_LOSS + 1, 0:128]

    flat = [t for p in params for t in p]
    out_shape = [_sds(p[0].shape, F32) for p in params for _ in range(4)]
    out_shape += [_sds((3, CONV_W), F32), _sds((3, 2 * D_FF), F32), _sds((1, 128), F32)]
    res = pl.pallas_call(body, name="small_sums_adamw", out_shape=out_shape)(r_small, r_dmix, r_dbin, *flat)
    return [tuple(res[4 * k:4 * k + 4]) for k in range(5)], res[20], res[21], res[22]


def _adamw_pair(a, b):
    def body(*refs):
        for k in range(2):
            w_ref, g_ref, m_ref, v_ref = refs[4 * k:4 * k + 4]
            d_ref, m2_ref, v2_ref = refs[8 + 3 * k:8 + 3 * k + 3]
            d_ref[...], m2_ref[...], v2_ref[...] = _adam_math(w_ref[...], g_ref[...], m_ref[...], v_ref[...])

    out_shape = [_sds(a[0].shape, F32)] * 3 + [_sds(b[0].shape, F32)] * 3
    res = pl.pallas_call(body, name="adamw_conv_weights", out_shape=out_shape)(*a, *b)
    return tuple(res[:3]), tuple(res[3:])


def _pad_cols(a, c):
    return jnp.pad(a, ((0, 0), (0, c - a.shape[1])))


def _to_col_slabs(g):
    r = g.shape[0]
    return jnp.transpose(g.reshape(r, N_DEV, 128), (1, 0, 2)).reshape(N_DEV * r, 128)


def _from_col_slabs(t):
    r = t.shape[0] // N_DEV
    return jnp.transpose(t.reshape(N_DEV, r, 128), (1, 0, 2)).reshape(r, N_DEV * 128)


def _slots(t):
    return t.reshape(N_DEV, t.shape[0] // N_DEV, t.shape[1])


def kernel(x, mix_norm, w_in, b_in, sinks, conv_w, w_attn_branch, w_conv_branch, w_out, ffn_norm, w_up, ffn_conv_w, w_down, final_norm, loss_target, m_mix_norm, m_w_in, m_b_in, m_sinks, m_conv_w, m_w_attn_branch, m_w_conv_branch, m_w_out, m_ffn_norm, m_w_up, m_ffn_conv_w, m_w_down, m_final_norm, v_mix_norm, v_w_in, v_b_in, v_sinks, v_conv_w, v_w_attn_branch, v_w_conv_branch, v_w_out, v_ffn_norm, v_w_up, v_ffn_conv_w, v_w_down, v_final_norm):
    xs, tgt = x[0], loss_target[0]
    me = 4 * lax.axis_index("x") + 2 * lax.axis_index("y") + lax.axis_index("c")
    in_rows, up_rows = IN_W // N_DEV, 2 * D_FF // N_DEV

    conv_sh = jnp.concatenate([_pad_cols(ffn_conv_w[0], 768), _pad_cols(conv_w[0], 768),
                               jnp.zeros((2, 768), F32)], axis=0)
    win_sh, wup_sh = w_in[0].T.astype(BF16), w_up[0].T.astype(BF16)
    wout_sh, wdown_sh = w_out[0].astype(BF16), w_down[0].astype(BF16)
    wa_sh, wc_sh = w_attn_branch[0].astype(BF16), w_conv_branch[0].astype(BF16)

    half = D_MODEL // 2
    (win_t,) = _exchange_only(_AllGather([win_sh]), "gather_w_in")
    (xn, qkv, cbx, gates), (wa_s, wc_s, wout, conv_g) = _norm_inproj(
        xs, mix_norm, win_t, b_in, _AllGather([wa_sh, wc_sh, wout_sh, conv_sh]))
    (attn, lse), (wup_lo,) = _attn_fwd(qkv, sinks, _AllGather([wup_sh[:, :half]]))
    wa, wc = _from_col_slabs(wa_s), _from_col_slabs(wc_s)
    conv_g = conv_g.reshape(N_DEV, 8, 768)
    fcw = jnp.transpose(conv_g[:, 0:3, :up_rows], (1, 0, 2)).reshape(3, 2 * D_FF)
    cw = jnp.transpose(conv_g[:, 3:6, :CONV_W // N_DEV], (1, 0, 2)).reshape(3, CONV_W)
    (h1,), (wup_hi,) = _mix_fwd(xs, cbx, gates, attn, cw, wa, wc, wout, _AllGather([wup_sh[:, half:]]))
    (hn, up_pre), (wdown,) = _ffn_up(h1, ffn_norm, wup_lo, wup_hi, _AllGather([wdown_sh]))
    up, act, dh2, loss_p, dfn_p = _ffn_down_loss(up_pre, fcw, wdown, h1, final_norm.reshape(1, D_MODEL), tgt)

    dn_rows, q_up = D_FF // N_DEV, up_rows // 4
    g_wdown = _matmul_tn(act, dh2, FF_CHUNK, "grad_w_down")
    (dup_pre, dfcw_p, dh1, dffn_p), (r_wdown,) = _ffn_bwd(dh2, wdown, up, up_pre, fcw, wup_lo, wup_hi, h1, ffn_norm,
                                                         _ReduceScatter([(g_wdown, 0, dn_rows)]))
    g_wup_t = _matmul_tn(dup_pre, hn, FF_CHUNK, "grad_w_up")
    (dgates, dattn, dcb, dcc, dcx, dcw_p, g_wout, g_wa_nat, g_wc_nat), (r_wup_ab,) = _mix_bwd(
        dh1, wout, gates, attn, wa, wc, cbx, cw, _ReduceScatter([(g_wup_t, 0, 2 * q_up)]))
    g_wa, g_wc = _to_col_slabs(g_wa_nat), _to_col_slabs(g_wc_nat)
    (dq, dk, dv, dsink_p), (r_wup_c, r_wout, r_wa, r_wc) = _attn_bwd(
        qkv, sinks, attn, lse, dattn,
        _ReduceScatter([(g_wup_t, 2 * q_up, q_up), (g_wout, 0, D_MODEL // N_DEV), (g_wa, 0, ATTN_W),
                        (g_wc, 0, CONV_W)]))
    dproj = (dq, dk, dv, dcb, dcc, dcx, dgates)
    small = _pack_small(dffn_p, dfn_p, dsink_p, loss_p, dcw_p, dfcw_p)
    g_win_t, (r_wup_d, r_small) = _grad_w_in(dproj, xn, _ReduceScatter([(g_wup_t, 3 * q_up, q_up)], [small]))
    (win_theirs,) = _exchange_only(_PairExchange([g_win_t]), "pair_exchange_w_in")
    q_win = _pair_add(g_win_t, win_theirs, in_rows // 2, "pair_add_w_in")
    (dx, _, _), (r_win, r_dbin, r_dmix) = _inproj_bwd(
        dproj, win_t, xs, mix_norm, dh1,
        _ChipExchangeThenBroadcast([q_win], late_from=(1, 2), late_shapes=[(8, IN_W), (8, D_MODEL)]))

    fn2, m_fn2, v_fn2 = (t.reshape(1, D_MODEL) for t in (final_norm, m_final_norm, v_final_norm))
    small_res, g_cw_full, g_fcw_full, loss_row = _small_sums_adamw(
        _slots(r_small), r_dmix, r_dbin,
        [(mix_norm, m_mix_norm, v_mix_norm), (b_in, m_b_in, v_b_in), (sinks, m_sinks, v_sinks),
         (ffn_norm, m_ffn_norm, v_ffn_norm), (fn2, m_fn2, v_fn2)])
    loss = loss_row[0, 0]
    g_cw = lax.dynamic_slice_in_dim(g_cw_full, me * (CONV_W // N_DEV), CONV_W // N_DEV, axis=1)
    g_fcw = lax.dynamic_slice_in_dim(g_fcw_full, me * up_rows, up_rows, axis=1)
    cw_res, fcw_res = _adamw_pair((conv_w[0], g_cw, m_conv_w[0], v_conv_w[0]),
                                  (ffn_conv_w[0], g_fcw, m_ffn_conv_w[0], v_ffn_conv_w[0]))

    big = {}
    big["w_in"] = tuple(t.T for t in _sum_parts_adamw(
        [r_win.reshape(4, in_rows, D_MODEL)], w_in[0].T, m_w_in[0].T, v_w_in[0].T, in_rows // 2, "adamw_w_in"))
    big["w_up"] = tuple(t.T for t in _sum_parts_adamw(
        [_slots(r_wup_ab), _slots(r_wup_c), _slots(r_wup_d)], w_up[0].T, m_w_up[0].T, v_w_up[0].T, q_up,
        "adamw_w_up"))
    big["w_out"] = _sum_adamw(_slots(r_wout), w_out[0], m_w_out[0], v_w_out[0], 128, "adamw_w_out")
    big["w_down"] = _sum_adamw(_slots(r_wdown), w_down[0], m_w_down[0], v_w_down[0], dn_rows // 2, "adamw_w_down")
    big["w_attn_branch"] = _sum_adamw(_slots(r_wa), w_attn_branch[0], m_w_attn_branch[0], v_w_attn_branch[0], 256,
                                      "adamw_w_attn_branch")
    big["w_conv_branch"] = _sum_adamw(_slots(r_wc), w_conv_branch[0], m_w_conv_branch[0], v_w_conv_branch[0], 256,
                                      "adamw_w_conv_branch")

    res = dict(zip(("mix_norm", "b_in", "sinks", "ffn_norm"), small_res[:4]))
    res["final_norm"] = tuple(t.reshape(final_norm.shape) for t in small_res[4])
    res["conv_w"] = tuple(t.reshape(conv_w.shape) for t in (g_cw,) + cw_res)
    res["ffn_conv_w"] = tuple(t.reshape(ffn_conv_w.shape) for t in (g_fcw,) + fcw_res)
    for name, ref_w in (("w_in", w_in), ("w_up", w_up), ("w_out", w_out), ("w_down", w_down),
                        ("w_attn_branch", w_attn_branch), ("w_conv_branch", w_conv_branch)):
        res[name] = tuple(t.reshape(ref_w.shape) for t in big[name])

    order = ["mix_norm", "w_in", "b_in", "sinks", "conv_w", "w_attn_branch", "w_conv_branch", "w_out",
             "ffn_norm", "w_up", "ffn_conv_w", "w_down", "final_norm"]
    out = [loss, dx.reshape(x.shape)]
    for k in range(4):
        out += [res[name][k] for name in order]
    return tuple(out)
```

```python
import math

import jax
import jax.numpy as jnp
from jax import lax
from jax.experimental import pallas as pl
from jax.experimental.pallas import tpu as pltpu

F32 = jnp.float32
BF16 = jnp.bfloat16
MESH = pl.DeviceIdType.MESH
N_DEV = 8

D_MODEL = 1024
HEAD_DIM = 64
N_HEADS = 8
BLOCK = 128
ATTN_W = 512
KV_W = 128
CONV_W = 512
QKV_W = ATTN_W + 2 * KV_W
CBX_W = 3 * CONV_W
GATE_W = 2 * D_MODEL
IN_W = QKV_W + CBX_W + GATE_W
D_FF = 2816
FF_CHUNK = 1408
NORM_EPS = 1e-5
ATTN_SCALE = HEAD_DIM ** -0.5
NEG = -1e30
HALO = 16

ADAM_LR = 0.001
ADAM_B1 = 0.9
ADAM_B2 = 0.999
ADAM_EPS = 1e-08
ADAM_WD = 0.01
ADAM_STEP = 10

VMEM_LIMIT = 56 * 1024 * 1024
SMALL_ROWS = 32

NT = (((1,), (1,)), ((), ()))
TN = (((0,), (0,)), ((), ()))
ANY = pl.BlockSpec(memory_space=pl.ANY)


def _sig(v):
    return 1.0 / (1.0 + jnp.exp(-v))


def _row_tile(s, pref=256):
    return pref if s % pref == 0 else s


def _shifts_down(u, halo, ks):
    ext = jnp.concatenate([halo, u], axis=0)
    return [pltpu.roll(ext, k, axis=0)[HALO:, :] for k in ks]


def _shifts_up(u, halo, ks):
    n = u.shape[0]
    ext = jnp.concatenate([u, halo], axis=0)
    return [pltpu.roll(ext, n + HALO - k, axis=0)[:n, :] for k in ks]


def _rows_reversed(tm, c, steps):
    return pl.BlockSpec((tm, c), lambda i: (steps - 1 - i, 0))


def _prev_halo_map_reversed(tm, steps):
    return lambda i: (jnp.maximum((steps - 1 - i) * (tm // HALO) - 1, 0), 0)


def _prev_halo_map(tm):
    return lambda i: (jnp.maximum(i * (tm // HALO) - 1, 0), 0)


def _full(shape):
    return pl.BlockSpec(shape, lambda *_: (0,) * len(shape))


def _resident(shape):
    return pl.BlockSpec(shape, lambda *_: (0,) * len(shape), pipeline_mode=pl.Buffered(1))


def _rows(tm, c):
    return pl.BlockSpec((tm, c), lambda i: (i, 0))


def _sds(shape, dtype):
    return jax.ShapeDtypeStruct(shape, dtype)


def _my_place():
    x, y, c = lax.axis_index("x"), lax.axis_index("y"), lax.axis_index("c")
    return x, y, c


ALL_PEERS = tuple((j >> 2, (j >> 1) & 1, j & 1) for j in range(1, N_DEV))
SIBLING_PEER = ((0, 0, 1),)
CHIP_PEERS = ((0, 1, 0), (1, 0, 0), (1, 1, 0))
BARRIER_ID = {ALL_PEERS: 0, SIBLING_PEER: 1, CHIP_PEERS: 2}


def _entry_barrier(peers):
    x, y, c = _my_place()
    barrier = pltpu.get_barrier_semaphore()
    for dx, dy, dc in peers:
        pl.semaphore_signal(barrier, inc=1, device_id=(x ^ dx, y ^ dy, c ^ dc), device_id_type=MESH)
    pl.semaphore_wait(barrier, len(peers))


def _start_exchange(remote, local):
    for cp in local + remote:
        cp.start()


def _finish_exchange(remote, local):
    for cp in remote:
        cp.wait_recv()
    for cp in remote:
        cp.wait_send()
    for cp in local:
        cp.wait()


class _AllGather:
    peers = ALL_PEERS

    def __init__(self, shards):
        self.ins = list(shards)
        n = len(shards)
        self.out_shape = [_sds((N_DEV * s.shape[0], s.shape[1]), s.dtype) for s in shards]
        self.sems = [pltpu.SemaphoreType.DMA((7 * n,)), pltpu.SemaphoreType.DMA((7 * n,)),
                     pltpu.SemaphoreType.DMA((n,))]

    def _parts(self, ins, outs, sems):
        send_sems, recv_sems, local_sems = sems
        x, y, c = _my_place()
        me, sibling = (x, y, c), (x, y, 1 - c)
        chips = [(1 - x, y), (x, 1 - y), (1 - x, 1 - y)]

        def rows(k, dev):
            r = ins[k].shape[0]
            start = pl.multiple_of((4 * dev[0] + 2 * dev[1] + dev[2]) * r, 8)
            return outs[k].at[pl.ds(start, r), :]

        def copy(k, j, block, to, src=None):
            return pltpu.make_async_remote_copy(
                src_ref=rows(k, block) if src is None else src, dst_ref=rows(k, block),
                send_sem=send_sems.at[7 * k + j], recv_sem=recv_sems.at[7 * k + j],
                device_id=to, device_id_type=MESH)

        n = len(ins)
        mine = [pltpu.make_async_copy(ins[k], rows(k, me), local_sems.at[k]) for k in range(n)]
        first = []
        for k in range(n):
            first.append(copy(k, 0, me, sibling, src=ins[k]))
            first += [copy(k, 1 + j, me, (*chip, c), src=ins[k]) for j, chip in enumerate(chips)]
        return me, sibling, chips, copy, mine, first

    def start(self, ins, outs, sems):
        _, _, _, _, mine, first = self._parts(ins, outs, sems)
        _start_exchange(first, mine)

    def finish(self, ins, outs, sems):
        me, sibling, chips, copy, mine, first = self._parts(ins, outs, sems)
        c = me[2]
        n = len(ins)
        passed = []
        for j, chip in enumerate(chips):
            for k in range(n):
                copy(k, 1 + j, (*chip, c), me).wait_recv()
                fwd = copy(k, 4 + j, (*chip, c), sibling)
                fwd.start()
                passed.append(fwd)
        for k in range(n):
            copy(k, 0, sibling, me).wait_recv()
            for j, chip in enumerate(chips):
                copy(k, 4 + j, (*chip, 1 - c), me).wait_recv()
        for cp in first + passed:
            cp.wait_send()
        for cp in mine:
            cp.wait()


class _ReduceScatter:
    peers = ALL_PEERS

    def __init__(self, parts, bcast=()):
        self.parts = [(lo, cnt) for _, lo, cnt in parts]
        self.n_parts = len(parts)
        self.ins = [a for a, _, _ in parts] + list(bcast)
        self.out_shape = [_sds((N_DEV * cnt, a.shape[1]), a.dtype) for a, _, cnt in parts]
        self.out_shape += [_sds((N_DEV * b.shape[0], b.shape[1]), b.dtype) for b in bcast]
        n = len(self.ins)
        self.sems = [pltpu.SemaphoreType.DMA((7 * n,)), pltpu.SemaphoreType.DMA((7 * n,)),
                     pltpu.SemaphoreType.DMA((n,))]

    def _copies(self, ins, outs, sems):
        send_sems, recv_sems, local_sems = sems
        x, y, c = _my_place()
        me_idx = 4 * x + 2 * y + c
        remote, local = [], []
        for k in range(len(ins)):
            cnt = outs[k].shape[0] // N_DEV
            dst = outs[k].at[pl.ds(pl.multiple_of(me_idx * cnt, 8), cnt), :]
            if k < self.n_parts:
                lo, _ = self.parts[k]
                r = ins[k].shape[0] // N_DEV
                src_of = lambda idx: ins[k].at[pl.ds(pl.multiple_of(idx * r + lo, 8), cnt), :]
            else:
                src_of = lambda idx: ins[k]
            local.append(pltpu.make_async_copy(src_of(me_idx), dst, local_sems.at[k]))
            for j in range(1, N_DEV):
                peer = (x ^ (j >> 2), y ^ ((j >> 1) & 1), c ^ (j & 1))
                peer_idx = 4 * peer[0] + 2 * peer[1] + peer[2]
                remote.append(pltpu.make_async_remote_copy(
                    src_ref=src_of(peer_idx), dst_ref=dst,
                    send_sem=send_sems.at[7 * k + j - 1], recv_sem=recv_sems.at[7 * k + j - 1],
                    device_id=peer, device_id_type=MESH))
        return remote, local

    def start(self, ins, outs, sems):
        _start_exchange(*self._copies(ins, outs, sems))

    def finish(self, ins, outs, sems):
        _finish_exchange(*self._copies(ins, outs, sems))


class _PairExchange:
    peers = SIBLING_PEER

    def __init__(self, arrays):
        self.ins = list(arrays)
        n = len(arrays)
        self.out_shape = [_sds((a.shape[0] // 2, a.shape[1]), a.dtype) for a in arrays]
        self.sems = [pltpu.SemaphoreType.DMA((4 * n,)), pltpu.SemaphoreType.DMA((4 * n,))]

    def _copies(self, ins, outs, sems):
        send_sems, recv_sems = sems
        x, y, c = _my_place()
        remote = []
        for k in range(len(ins)):
            r = ins[k].shape[0] // N_DEV
            for chip in range(4):
                sib = ins[k].at[pl.ds(pl.multiple_of((2 * chip + 1 - c) * r, 8), r), :]
                remote.append(pltpu.make_async_remote_copy(
                    src_ref=sib, dst_ref=outs[k].at[pl.ds(chip * r, r), :],
                    send_sem=send_sems.at[4 * k + chip], recv_sem=recv_sems.at[4 * k + chip],
                    device_id=(x, y, 1 - c), device_id_type=MESH))
        return remote

    def start(self, ins, outs, sems):
        for cp in self._copies(ins, outs, sems):
            cp.start()

    def finish(self, ins, outs, sems):
        remote = self._copies(ins, outs, sems)
        for cp in remote:
            cp.wait_recv()
        for cp in remote:
            cp.wait_send()


class _ChipExchange:
    peers = CHIP_PEERS

    def __init__(self, arrays):
        self.ins = list(arrays)
        self.out_shape = [_sds(a.shape, a.dtype) for a in arrays]
        n = len(self.ins)
        self.sems = [pltpu.SemaphoreType.DMA((3 * n,)), pltpu.SemaphoreType.DMA((3 * n,)),
                     pltpu.SemaphoreType.DMA((n,))]

    def _copies(self, ins, outs, sems):
        send_sems, recv_sems, local_sems = sems
        x, y, c = _my_place()
        my_chip = 2 * x + y
        remote, local = [], []
        for k in range(len(ins)):
            r = ins[k].shape[0] // 4
            dst = outs[k].at[pl.ds(pl.multiple_of(my_chip * r, 8), r), :]
            local.append(pltpu.make_async_copy(ins[k].at[pl.ds(pl.multiple_of(my_chip * r, 8), r), :], dst,
                                               local_sems.at[k]))
            for j in range(1, 4):
                px, py = x ^ (j >> 1), y ^ (j & 1)
                src = ins[k].at[pl.ds(pl.multiple_of((2 * px + py) * r, 8), r), :]
                remote.append(pltpu.make_async_remote_copy(
                    src_ref=src, dst_ref=dst, send_sem=send_sems.at[3 * k + j - 1],
                    recv_sem=recv_sems.at[3 * k + j - 1], device_id=(px, py, c), device_id_type=MESH))
        return remote, local

    def start(self, ins, outs, sems):
        _start_exchange(*self._copies(ins, outs, sems))

    def finish(self, ins, outs, sems):
        _finish_exchange(*self._copies(ins, outs, sems))


class _ChipExchangeThenBroadcast(_ChipExchange):
    peers = ALL_PEERS

    def __init__(self, arrays, late_from, late_shapes):
        super().__init__(arrays)
        self.n_chip = len(arrays)
        self.late_from = tuple(late_from)
        self.out_shape += [_sds((N_DEV * r, c), F32) for r, c in late_shapes]
        m = len(late_shapes)
        self.sems += [pltpu.SemaphoreType.DMA((7 * m,)), pltpu.SemaphoreType.DMA((7 * m,)),
                      pltpu.SemaphoreType.DMA((m,))]

    def _late_copies(self, srcs, outs, sems):
        send_sems, recv_sems, local_sems = sems
        x, y, c = _my_place()
        me_idx = 4 * x + 2 * y + c
        remote, local = [], []
        for k, src in enumerate(srcs):
            r = src.shape[0]
            dst = outs[k].at[pl.ds(pl.multiple_of(me_idx * r, 8), r), :]
            local.append(pltpu.make_async_copy(src, dst, local_sems.at[k]))
            for j, (dx, dy, dc) in enumerate(ALL_PEERS):
                remote.append(pltpu.make_async_remote_copy(
                    src_ref=src, dst_ref=dst, send_sem=send_sems.at[7 * k + j], recv_sem=recv_sems.at[7 * k + j],
                    device_id=(x ^ dx, y ^ dy, c ^ dc), device_id_type=MESH))
        return remote, local

    def start(self, ins, outs, sems):
        _start_exchange(*self._copies(ins, outs[:self.n_chip], sems[:3]))

    def finish(self, ins, outs, sems, late_srcs):
        late = self._late_copies(late_srcs, outs[self.n_chip:], sems[3:])
        _start_exchange(*late)
        _finish_exchange(*self._copies(ins, outs[:self.n_chip], sems[:3]))
        _finish_exchange(*late)


def _pcall(body, name, grid, in_specs, out_specs, out_shape, args, scratch=(), comm=None):
    params = pltpu.CompilerParams(dimension_semantics=("arbitrary",) * len(grid), vmem_limit_bytes=VMEM_LIMIT)
    in_specs, out_specs, out_shape, scratch = list(in_specs), list(out_specs), list(out_shape), list(scratch)
    if comm is None:
        res = pl.pallas_call(body, name=name, grid=grid, in_specs=in_specs, out_specs=out_specs, out_shape=out_shape,
                             scratch_shapes=scratch, compiler_params=params)(*args)
        return list(res), []
    n_in, n_out, n_scr = len(in_specs), len(out_specs), len(scratch)
    ci, co = len(comm.ins), len(comm.out_shape)
    total = math.prod(grid)

    def carried(*refs):
        bounds = [0, n_in, n_in + ci, n_in + ci + n_out, n_in + ci + n_out + co, n_in + ci + n_out + co + n_scr]
        ins, cins, outs, couts, scr = (refs[a:b] for a, b in zip(bounds[:-1], bounds[1:]))
        sems = refs[bounds[-1]:]
        step = pl.program_id(0)
        for d in range(1, len(grid)):
            step = step * grid[d] + pl.program_id(d)

        @pl.when(step == 0)
        def _():
            _entry_barrier(comm.peers)
            comm.start(cins, couts, sems)

        body(*ins, *outs, *scr)

        @pl.when(step == total - 1)
        def _():
            late_from = getattr(comm, "late_from", None)
            if late_from is None:
                comm.finish(cins, couts, sems)
            else:
                comm.finish(cins, couts, sems, [outs[k] for k in late_from])

    params = pltpu.CompilerParams(dimension_semantics=("arbitrary",) * len(grid), vmem_limit_bytes=VMEM_LIMIT,
                                  collective_id=BARRIER_ID[comm.peers])
    res = pl.pallas_call(
        carried, name=name, grid=grid, in_specs=in_specs + [ANY] * ci, out_specs=out_specs + [ANY] * co,
        out_shape=out_shape + comm.out_shape, scratch_shapes=scratch + comm.sems, compiler_params=params,
    )(*args, *comm.ins)
    return list(res[:n_out]), list(res[n_out:])


def _exchange_only(comm, name):
    def body(*refs):
        ci, co = len(comm.ins), len(comm.out_shape)
        _entry_barrier(comm.peers)
        comm.start(refs[:ci], refs[ci:ci + co], refs[ci + co:])
        comm.finish(refs[:ci], refs[ci:ci + co], refs[ci + co:])

    params = pltpu.CompilerParams(collective_id=BARRIER_ID[comm.peers])
    return pl.pallas_call(body, name=name, out_shape=comm.out_shape, in_specs=[ANY] * len(comm.ins),
                          out_specs=[ANY] * len(comm.out_shape), scratch_shapes=comm.sems,
                          compiler_params=params)(*comm.ins)


def _norm_inproj(x, g, win_t, b_in, comm):
    s = x.shape[0]
    tm = _row_tile(s, 512)
    widths = (QKV_W, CBX_W, GATE_W)

    def body(x_ref, g_ref, w_ref, b_ref, xn_ref, qkv_ref, cbx_ref, gate_ref):
        xv = x_ref[...]
        r = lax.rsqrt(jnp.mean(xv * xv, axis=-1, keepdims=True) + NORM_EPS)
        xn = (xv * r * g_ref[...]).astype(BF16)
        xn_ref[...] = xn
        off = 0
        for o_ref, w in zip((qkv_ref, cbx_ref, gate_ref), widths):
            acc = lax.dot_general(xn, w_ref[off:off + w, :], NT, preferred_element_type=F32)
            o_ref[...] = (acc + b_ref[:, off:off + w]).astype(BF16)
            off += w

    return _pcall(
        body, "norm_inproj", (s // tm,),
        [_rows(tm, D_MODEL), _full((1, D_MODEL)), _resident((IN_W, D_MODEL)), _full((1, IN_W))],
        [_rows(tm, D_MODEL)] + [_rows(tm, w) for w in widths],
        [_sds((s, D_MODEL), BF16)] + [_sds((s, w), BF16) for w in widths],
        (x, g, win_t, b_in), comm=comm)


def _attn_specs():
    prev = lambda n: jnp.maximum(n - 1, 0)
    return [pl.BlockSpec((BLOCK, ATTN_W), lambda n: (n, 0)),
            pl.BlockSpec((BLOCK, KV_W), lambda n: (prev(n), ATTN_W // KV_W)),
            pl.BlockSpec((BLOCK, KV_W), lambda n: (n, ATTN_W // KV_W)),
            pl.BlockSpec((BLOCK, KV_W), lambda n: (prev(n), ATTN_W // KV_W + 1)),
            pl.BlockSpec((BLOCK, KV_W), lambda n: (n, ATTN_W // KV_W + 1))]


def _lower_lanes():
    return lax.broadcasted_iota(jnp.int32, (BLOCK, 128), 1) < HEAD_DIM


def _stack_heads(val, kh):
    lower = _lower_lanes()
    parts = []
    for g in range(4):
        h = kh * 4 + g
        blk = val[:, (h // 2) * 128:(h // 2 + 1) * 128]
        keep = lower if h % 2 == 0 else jnp.logical_not(lower)
        parts.append(jnp.where(keep, blk, jnp.zeros_like(blk)))
    return jnp.concatenate(parts, axis=0)


def _dup_kv(prev_ref, cur_ref, kh):
    t = jnp.concatenate([prev_ref[...], cur_ref[...]], axis=0).astype(F32)
    rolled = pltpu.roll(t, HEAD_DIM, axis=1)
    lower = lax.broadcasted_iota(jnp.int32, t.shape, 1) < HEAD_DIM
    dup = jnp.where(lower, t, rolled) if kh == 0 else jnp.where(lower, rolled, t)
    return dup.astype(BF16)


def _attn_mask(n):
    row = lax.broadcasted_iota(jnp.int32, (4 * BLOCK, 2 * BLOCK), 0)
    kj = lax.broadcasted_iota(jnp.int32, (4 * BLOCK, 2 * BLOCK), 1)
    dist = (row & (BLOCK - 1)) + BLOCK - kj
    band = jnp.logical_and(dist >= 0, dist < BLOCK)
    return jnp.logical_and(band, jnp.logical_or(kj >= BLOCK, n > 0))


def _sink_col(sinks_ref, kh):
    gi = lax.broadcasted_iota(jnp.int32, (4 * BLOCK, 1), 0) // BLOCK
    col = jnp.zeros((4 * BLOCK, 1), F32)
    for g in range(4):
        col = jnp.where(gi == g, sinks_ref[0, kh * 4 + g], col)
    return col


def _attn_fwd(qkv, sinks, comm):
    s = qkv.shape[0]

    def body(sinks_ref, q_ref, kp_ref, kc_ref, vp_ref, vc_ref, o_ref, lse_ref):
        n = pl.program_id(0)
        mask = _attn_mask(n)
        lower = _lower_lanes()
        lane = lax.broadcasted_iota(jnp.int32, (BLOCK, 128), 1)
        qv = q_ref[...]
        lse_out = jnp.zeros((BLOCK, 128), F32)
        for kh in range(2):
            qs = _stack_heads(qv, kh)
            kd, vd = _dup_kv(kp_ref, kc_ref, kh), _dup_kv(vp_ref, vc_ref, kh)
            sc = lax.dot_general(qs, kd, NT, preferred_element_type=F32) * ATTN_SCALE
            sc = jnp.where(mask, sc, NEG)
            sink = _sink_col(sinks_ref, kh)
            m = jnp.maximum(jnp.max(sc, axis=1, keepdims=True), sink)
            p = jnp.exp(sc - m)
            l = jnp.sum(p, axis=1, keepdims=True) + jnp.exp(sink - m)
            o = jnp.dot(p.astype(BF16), vd, preferred_element_type=F32) / l
            lse = m + jnp.log(l)
            for pair in range(2):
                lo = o[(2 * pair) * BLOCK:(2 * pair + 1) * BLOCK]
                hi = o[(2 * pair + 1) * BLOCK:(2 * pair + 2) * BLOCK]
                col = (kh * 2 + pair) * 128
                o_ref[:, col:col + 128] = jnp.where(lower, lo, hi).astype(BF16)
            for g in range(4):
                lse_out = jnp.where(lane == kh * 4 + g, lse[g * BLOCK:(g + 1) * BLOCK], lse_out)
        lse_ref[...] = lse_out

    return _pcall(
        body, "attn_fwd", (s // BLOCK,),
        [pl.BlockSpec(memory_space=pltpu.SMEM)] + _attn_specs(),
        [pl.BlockSpec((BLOCK, ATTN_W), lambda n: (n, 0)), pl.BlockSpec((BLOCK, 128), lambda n: (n, 0))],
        [_sds((s, ATTN_W), BF16), _sds((s, 128), F32)],
        (sinks, qkv, qkv, qkv, qkv, qkv), comm=comm)


def _conv_u(cbx_ref, halo_ref, w_ref, first):
    cb = cbx_ref[:, 0:CONV_W].astype(F32)
    cc = cbx_ref[:, CONV_W:2 * CONV_W].astype(F32)
    cx = cbx_ref[:, 2 * CONV_W:3 * CONV_W].astype(F32)
    u = cc * cx
    uh = halo_ref[:, CONV_W:2 * CONV_W].astype(F32) * halo_ref[:, 2 * CONV_W:3 * CONV_W].astype(F32)
    uh = jnp.where(first, 0.0, uh)
    u1, u2 = _shifts_down(u, uh, (1, 2))
    cv = w_ref[0:1, :] * u2 + w_ref[1:2, :] * u1 + w_ref[2:3, :] * u
    return cb, cc, cx, u, cv


def _mix_fwd(x, cbx, gates, attn, conv_w, wa, wc, wout, comm):
    s = x.shape[0]
    tm = _row_tile(s)

    def body(x_ref, cbx_ref, halo_ref, gate_ref, attn_ref, cw_ref, wa_ref, wc_ref, wo_ref,
             h1_ref):
        first = pl.program_id(0) == 0
        cb, _, _, _, cv = _conv_u(cbx_ref, halo_ref, cw_ref, first)
        conv = (cb * cv).astype(BF16)
        ap = jnp.dot(attn_ref[...], wa_ref[...], preferred_element_type=F32)
        cp = jnp.dot(conv, wc_ref[...], preferred_element_type=F32)
        ga = gate_ref[:, 0:D_MODEL].astype(F32)
        gc = gate_ref[:, D_MODEL:2 * D_MODEL].astype(F32)
        merged = (_sig(ga) * ap + _sig(gc) * cp).astype(BF16)
        h1_ref[...] = x_ref[...] + jnp.dot(merged, wo_ref[...], preferred_element_type=F32)

    return _pcall(
        body, "mix_fwd", (s // tm,),
        [_rows(tm, D_MODEL), _rows(tm, CBX_W), pl.BlockSpec((HALO, CBX_W), _prev_halo_map(tm)),
         _rows(tm, GATE_W), _rows(tm, ATTN_W), _full((3, CONV_W)), _full((ATTN_W, D_MODEL)),
         _full((CONV_W, D_MODEL)), _full((D_MODEL, D_MODEL))],
        [_rows(tm, D_MODEL)], [_sds((s, D_MODEL), F32)],
        (x, cbx, cbx, gates, attn, conv_w, wa, wc, wout), comm=comm)


def _ffn_up(h1, g, wup_lo, wup_hi, comm):
    s = h1.shape[0]
    tm = _row_tile(s, 512)
    half = D_MODEL // 2

    def body(h_ref, g_ref, wl_ref, wh_ref, hn_ref, up_ref):
        hv = h_ref[...]
        r = lax.rsqrt(jnp.mean(hv * hv, axis=-1, keepdims=True) + NORM_EPS)
        hn = (hv * r * g_ref[...]).astype(BF16)
        hn_ref[...] = hn
        for c in range(2 * D_FF // FF_CHUNK):
            sl = slice(c * FF_CHUNK, (c + 1) * FF_CHUNK)
            acc = lax.dot_general(hn[:, :half], wl_ref[sl, :], NT, preferred_element_type=F32)
            acc = acc + lax.dot_general(hn[:, half:], wh_ref[sl, :], NT, preferred_element_type=F32)
            up_ref[:, sl] = acc.astype(BF16)

    return _pcall(
        body, "ffn_up", (s // tm,),
        [_rows(tm, D_MODEL), _full((1, D_MODEL)), _resident((2 * D_FF, half)), _resident((2 * D_FF, half))],
        [_rows(tm, D_MODEL), _rows(tm, 2 * D_FF)],
        [_sds((s, D_MODEL), BF16), _sds((s, 2 * D_FF), BF16)],
        (h1, g, wup_lo, wup_hi), comm=comm)


def _ffn_conv_cols(up_ref, halo_ref, fcw_ref, first, off):
    u = up_ref[:, off:off + FF_CHUNK].astype(F32)
    uh = jnp.where(first, 0.0, halo_ref[:, off:off + FF_CHUNK].astype(F32))
    w = fcw_ref[:, off:off + FF_CHUNK]
    u1, u2 = _shifts_down(u, uh, (1, 2))
    return w[0:1] * u2 + w[1:2] * u1 + w[2:3] * u


def _ffn_down_loss(up_pre, fcw, wdown, h1, fnorm, target):
    s = h1.shape[0]
    tm = _row_tile(s)

    def body(up_ref, halo_ref, fcw_ref, wd_ref, h1_ref, fn_ref, t_ref, cu_ref, act_ref, dh2_ref, loss_ref, dfn_ref):
        i = pl.program_id(0)

        @pl.when(i == 0)
        def _():
            loss_ref[...] = jnp.zeros_like(loss_ref)
            dfn_ref[...] = jnp.zeros_like(dfn_ref)

        h2 = h1_ref[...]
        for c in range(D_FF // FF_CHUNK):
            gsl = slice(c * FF_CHUNK, (c + 1) * FF_CHUNK)
            vsl = slice(D_FF + c * FF_CHUNK, D_FF + (c + 1) * FF_CHUNK)
            gate = _ffn_conv_cols(up_ref, halo_ref, fcw_ref, i == 0, c * FF_CHUNK)
            cu_ref[:, gsl] = gate.astype(BF16)
            val = _ffn_conv_cols(up_ref, halo_ref, fcw_ref, i == 0, D_FF + c * FF_CHUNK)
            cu_ref[:, vsl] = val.astype(BF16)
            act = (gate * _sig(gate) * val).astype(BF16)
            act_ref[:, gsl] = act
            h2 = h2 + jnp.dot(act, wd_ref[gsl, :], preferred_element_type=F32)
        r = lax.rsqrt(jnp.mean(h2 * h2, axis=-1, keepdims=True) + NORM_EPS)
        yhat = h2 * r
        fn = fn_ref[...]
        diff = yhat * fn - t_ref[...]
        loss_ref[...] += 0.5 * jnp.sum(jnp.sum(diff * diff, axis=1, keepdims=True), axis=0, keepdims=True) / D_MODEL
        dy = diff * (1.0 / D_MODEL)
        dfn_ref[...] += jnp.sum(dy * yhat, axis=0, keepdims=True)
        dyh = dy * fn
        dh2_ref[...] = r * (dyh - yhat * jnp.mean(dyh * yhat, axis=-1, keepdims=True))

    return _pcall(
        body, "ffn_down_loss", (s // tm,),
        [_rows(tm, 2 * D_FF), pl.BlockSpec((HALO, 2 * D_FF), _prev_halo_map(tm)), _full((3, 2 * D_FF)),
         _resident((D_FF, D_MODEL)), _rows(tm, D_MODEL), _full((1, D_MODEL)), _rows(tm, D_MODEL)],
        [_rows(tm, 2 * D_FF), _rows(tm, D_FF), _rows(tm, D_MODEL), _full((1, 128)), _full((1, D_MODEL))],
        [_sds((s, 2 * D_FF), BF16), _sds((s, D_FF), BF16), _sds((s, D_MODEL), F32), _sds((1, 128), F32),
         _sds((1, D_MODEL), F32)],
        (up_pre, up_pre, fcw, wdown, h1, fnorm, target))[0]


def _ffn_bwd(dh2, wdown, up, up_pre, fcw, wup_lo, wup_hi, h1, g, comm):
    s = dh2.shape[0]
    tm = _row_tile(s)
    half = D_MODEL // 2

    def dup_cols(dh, up_ref, wd_ref, c):
        gsl = slice(c * FF_CHUNK, (c + 1) * FF_CHUNK)
        vsl = slice(D_FF + c * FF_CHUNK, D_FF + (c + 1) * FF_CHUNK)
        dact = lax.dot_general(dh, wd_ref[gsl, :], NT, preferred_element_type=F32)
        gate = up_ref[:, gsl].astype(F32)
        val = up_ref[:, vsl].astype(F32)
        sg = _sig(gate)
        return dact * val * (sg * (1.0 + gate * (1.0 - sg))), dact * gate * sg

    def body(dh_ref, wd_ref, up_ref, x_ref, w_ref, wl_ref, wh_ref, h_ref, g_ref,
             dx_ref, dw_ref, dh1_ref, dg_ref, carry_ref):
        @pl.when(pl.program_id(0) == 0)
        def _():
            dw_ref[...] = jnp.zeros_like(dw_ref)
            dg_ref[...] = jnp.zeros_like(dg_ref)
            carry_ref[...] = jnp.zeros_like(carry_ref)

        dh2v = dh_ref[...]
        dh = dh2v.astype(BF16)
        dhn_lo = jnp.zeros((tm, half), F32)
        dhn_hi = jnp.zeros((tm, half), F32)
        for c in range(D_FF // FF_CHUNK):
            for d, off in zip(dup_cols(dh, up_ref, wd_ref, c), (c * FF_CHUNK, D_FF + c * FF_CHUNK)):
                sl = slice(off, off + FF_CHUNK)
                dn = carry_ref[:, sl]
                carry_ref[:, sl] = d[0:HALO, :]
                xv = x_ref[:, sl].astype(F32)
                wv = w_ref[:, sl]
                d1, d2 = _shifts_up(d, dn, (1, 2))
                dx = (wv[2:3] * d + wv[1:2] * d1 + wv[0:1] * d2).astype(BF16)
                dx_ref[:, sl] = dx
                dhn_lo = dhn_lo + jnp.dot(dx, wl_ref[sl, :], preferred_element_type=F32)
                dhn_hi = dhn_hi + jnp.dot(dx, wh_ref[sl, :], preferred_element_type=F32)
                dw_ref[0:1, sl] += jnp.sum(d2 * xv, axis=0, keepdims=True)
                dw_ref[1:2, sl] += jnp.sum(d1 * xv, axis=0, keepdims=True)
                dw_ref[2:3, sl] += jnp.sum(d * xv, axis=0, keepdims=True)
        dx1, dg = _norm_bwd_tile(h_ref[...], g_ref[...], jnp.concatenate([dhn_lo, dhn_hi], axis=1))
        dg_ref[...] += dg
        dh1_ref[...] = dh2v + dx1

    rows = lambda c: _rows_reversed(tm, c, s // tm)
    return _pcall(
        body, "ffn_bwd", (s // tm,),
        [rows(D_MODEL), _resident((D_FF, D_MODEL)), rows(2 * D_FF), rows(2 * D_FF), _full((3, 2 * D_FF)),
         _resident((2 * D_FF, half)), _resident((2 * D_FF, half)), rows(D_MODEL), _full((1, D_MODEL))],
        [rows(2 * D_FF), _full((3, 2 * D_FF)), rows(D_MODEL), _full((1, D_MODEL))],
        [_sds((s, 2 * D_FF), BF16), _sds((3, 2 * D_FF), F32), _sds((s, D_MODEL), F32), _sds((1, D_MODEL), F32)],
        (dh2, wdown, up, up_pre, fcw, wup_lo, wup_hi, h1, g),
        scratch=[pltpu.VMEM((HALO, 2 * D_FF), F32)], comm=comm)


def _matmul_tn(a, b, tk, name, ts=1024, comm=None):
    s, ka = a.shape
    n = b.shape[1]
    ts = min(ts, s)
    steps = s // ts

    def body(a_ref, b_ref, o_ref, acc_ref):
        j = pl.program_id(1)

        @pl.when(j == 0)
        def _():
            acc_ref[...] = jnp.zeros_like(acc_ref)

        acc_ref[...] += lax.dot_general(a_ref[...].astype(BF16), b_ref[...].astype(BF16), TN,
                                        preferred_element_type=F32)

        @pl.when(j == steps - 1)
        def _():
            o_ref[...] = acc_ref[...].astype(BF16)

    outs, couts = _pcall(
        body, name, (ka // tk, steps),
        [pl.BlockSpec((ts, tk), lambda i, j: (j, i)), pl.BlockSpec((ts, n), lambda i, j: (j, 0))],
        [pl.BlockSpec((tk, n), lambda i, j: (i, 0))], [_sds((ka, n), BF16)],
        (a, b), scratch=[pltpu.VMEM((tk, n), F32)], comm=comm)
    return outs[0] if comm is None else (outs[0], couts)


def _norm_bwd_tile(xv, g, dy):
    r = lax.rsqrt(jnp.mean(xv * xv, axis=-1, keepdims=True) + NORM_EPS)
    xhat = xv * r
    dg = jnp.sum(dy * xhat, axis=0, keepdims=True)
    dyh = dy * g
    return r * (dyh - xhat * jnp.mean(dyh * xhat, axis=-1, keepdims=True)), dg


def _mix_bwd(dh1, wout, gates, attn, wa, wc, cbx, conv_w, comm):
    s = dh1.shape[0]
    tm = _row_tile(s)
    steps = s // tm

    def body(dh_ref, wo_ref, gate_ref, attn_ref, wa_ref, wc_ref, cbx_ref, halo_ref,
             cw_ref, dg_ref, dattn_ref, dcb_ref, dcc_ref, dcx_ref, dw_ref, gwo_ref, gwa_ref, gwc_ref,
             acc_o, acc_a, acc_c, carry_ref):
        i = pl.program_id(0)

        @pl.when(i == 0)
        def _():
            dw_ref[...] = jnp.zeros_like(dw_ref)
            acc_o[...] = jnp.zeros_like(acc_o)
            acc_a[...] = jnp.zeros_like(acc_a)
            acc_c[...] = jnp.zeros_like(acc_c)
            carry_ref[...] = jnp.zeros_like(carry_ref)

        cb, cc, cx, u, cv = _conv_u(cbx_ref, halo_ref, cw_ref, i == steps - 1)
        attn = attn_ref[...]
        conv = (cb * cv).astype(BF16)
        ap = jnp.dot(attn, wa_ref[...], preferred_element_type=F32)
        cp = jnp.dot(conv, wc_ref[...], preferred_element_type=F32)
        dhb = dh_ref[...].astype(BF16)
        dm = lax.dot_general(dhb, wo_ref[...], NT, preferred_element_type=F32)
        sa = _sig(gate_ref[:, 0:D_MODEL].astype(F32))
        sc = _sig(gate_ref[:, D_MODEL:2 * D_MODEL].astype(F32))
        merged = (sa * ap + sc * cp).astype(BF16)
        da = (dm * sa).astype(BF16)
        dc = (dm * sc).astype(BF16)
        dg_ref[:, 0:D_MODEL] = (dm * ap * sa * (1.0 - sa)).astype(BF16)
        dg_ref[:, D_MODEL:2 * D_MODEL] = (dm * cp * sc * (1.0 - sc)).astype(BF16)
        dattn_ref[...] = lax.dot_general(da, wa_ref[...], NT, preferred_element_type=F32).astype(BF16)
        dconv = lax.dot_general(dc, wc_ref[...], NT, preferred_element_type=F32)
        dcb_ref[...] = (dconv * cv).astype(BF16)
        d = dconv * cb
        dn = carry_ref[...]
        carry_ref[...] = d[0:HALO, :]
        d1, d2 = _shifts_up(d, dn, (1, 2))
        du = cw_ref[2:3, :] * d + cw_ref[1:2, :] * d1 + cw_ref[0:1, :] * d2
        dcc_ref[...] = (du * cx).astype(BF16)
        dcx_ref[...] = (du * cc).astype(BF16)
        dw_ref[0:1, :] += jnp.sum(d2 * u, axis=0, keepdims=True)
        dw_ref[1:2, :] += jnp.sum(d1 * u, axis=0, keepdims=True)
        dw_ref[2:3, :] += jnp.sum(d * u, axis=0, keepdims=True)
        acc_o[...] += lax.dot_general(merged, dhb, TN, preferred_element_type=F32)
        acc_a[...] += lax.dot_general(attn, da, TN, preferred_element_type=F32)
        acc_c[...] += lax.dot_general(conv, dc, TN, preferred_element_type=F32)

        @pl.when(i == steps - 1)
        def _():
            gwo_ref[...] = acc_o[...].astype(BF16)
            gwa_ref[...] = acc_a[...].astype(BF16)
            gwc_ref[...] = acc_c[...].astype(BF16)

    rows = lambda c: _rows_reversed(tm, c, steps)
    return _pcall(
        body, "mix_bwd", (steps,),
        [rows(D_MODEL), _full((D_MODEL, D_MODEL)), rows(GATE_W), rows(ATTN_W), _full((ATTN_W, D_MODEL)),
         _full((CONV_W, D_MODEL)), rows(CBX_W), pl.BlockSpec((HALO, CBX_W), _prev_halo_map_reversed(tm, steps)),
         _full((3, CONV_W))],
        [rows(GATE_W), rows(ATTN_W), rows(CONV_W), rows(CONV_W), rows(CONV_W),
         _full((3, CONV_W)), _full((D_MODEL, D_MODEL)), _full((ATTN_W, D_MODEL)), _full((CONV_W, D_MODEL))],
        [_sds((s, GATE_W), BF16), _sds((s, ATTN_W), BF16), _sds((s, CONV_W), BF16), _sds((s, CONV_W), BF16),
         _sds((s, CONV_W), BF16), _sds((3, CONV_W), F32), _sds((D_MODEL, D_MODEL), BF16),
         _sds((ATTN_W, D_MODEL), BF16), _sds((CONV_W, D_MODEL), BF16)],
        (dh1, wout, gates, attn, wa, wc, cbx, cbx, conv_w),
        scratch=[pltpu.VMEM((D_MODEL, D_MODEL), F32), pltpu.VMEM((ATTN_W, D_MODEL), F32),
                 pltpu.VMEM((CONV_W, D_MODEL), F32), pltpu.VMEM((HALO, CONV_W), F32)], comm=comm)


def _attn_bwd(qkv, sinks, attn, lse, dattn, comm):
    s = qkv.shape[0]

    def body(sinks_ref, q_ref, kp_ref, kc_ref, vp_ref, vc_ref, o_ref, lse_ref, do_ref,
             dq_ref, dk_ref, dv_ref, ds_ref):
        n = pl.program_id(0)

        @pl.when(n == 0)
        def _():
            dk_ref[...] = jnp.zeros_like(dk_ref)
            dv_ref[...] = jnp.zeros_like(dv_ref)
            ds_ref[...] = jnp.zeros_like(ds_ref)

        mask = _attn_mask(n)
        lower = _lower_lanes()
        lane = lax.broadcasted_iota(jnp.int32, (BLOCK, 128), 1)
        lower2 = lax.broadcasted_iota(jnp.int32, (2 * BLOCK, 128), 1) < HEAD_DIM
        lane1 = lax.broadcasted_iota(jnp.int32, (1, 128), 1)
        qv, ov, dov, lsev = q_ref[...], o_ref[...], do_ref[...], lse_ref[...]
        dk_fold, dv_fold = [], []
        dsink = jnp.zeros((1, 128), F32)
        for kh in range(2):
            qs = _stack_heads(qv, kh)
            dos = _stack_heads(dov, kh)
            os_ = _stack_heads(ov, kh)
            kd, vd = _dup_kv(kp_ref, kc_ref, kh), _dup_kv(vp_ref, vc_ref, kh)
            lse = jnp.concatenate(
                [jnp.sum(jnp.where(lane == kh * 4 + g, lsev, 0.0), axis=1, keepdims=True) for g in range(4)], axis=0)
            sc = lax.dot_general(qs, kd, NT, preferred_element_type=F32) * ATTN_SCALE
            p = jnp.exp(jnp.where(mask, sc, NEG) - lse)
            dp = lax.dot_general(dos, vd, NT, preferred_element_type=F32)
            delta = jnp.sum(dos.astype(F32) * os_.astype(F32), axis=1, keepdims=True)
            dsc = (p * (dp - delta) * ATTN_SCALE).astype(BF16)
            dqs = jnp.dot(dsc, kd, preferred_element_type=F32)
            for pair in range(2):
                lo = dqs[(2 * pair) * BLOCK:(2 * pair + 1) * BLOCK]
                hi = dqs[(2 * pair + 1) * BLOCK:(2 * pair + 2) * BLOCK]
                col = (kh * 2 + pair) * 128
                dq_ref[:, col:col + 128] = jnp.where(lower, lo, hi).astype(BF16)
            dkd = lax.dot_general(dsc, qs, TN, preferred_element_type=F32)
            dvd = lax.dot_general(p.astype(BF16), dos, TN, preferred_element_type=F32)
            dk_fold.append(dkd + pltpu.roll(dkd, HEAD_DIM, axis=1))
            dv_fold.append(dvd + pltpu.roll(dvd, HEAD_DIM, axis=1))
            psink = jnp.exp(_sink_col(sinks_ref, kh) - lse) * delta
            for g in range(4):
                tot = jnp.sum(psink[g * BLOCK:(g + 1) * BLOCK], axis=0, keepdims=True)
                dsink = dsink - jnp.where(lane1 == kh * 4 + g, tot, 0.0)
        dk2 = jnp.where(lower2, dk_fold[0], dk_fold[1])
        dv2 = jnp.where(lower2, dv_fold[0], dv_fold[1])
        ds_ref[...] += dsink
        cur = pl.ds(pl.multiple_of(n * BLOCK, BLOCK), BLOCK)
        dk_ref[cur, :] += dk2[BLOCK:]
        dv_ref[cur, :] += dv2[BLOCK:]

        @pl.when(n > 0)
        def _():
            prev = pl.ds(pl.multiple_of((n - 1) * BLOCK, BLOCK), BLOCK)
            dk_ref[prev, :] += dk2[:BLOCK]
            dv_ref[prev, :] += dv2[:BLOCK]

    blk = lambda w: pl.BlockSpec((BLOCK, w), lambda n: (n, 0))
    return _pcall(
        body, "attn_bwd", (s // BLOCK,),
        [pl.BlockSpec(memory_space=pltpu.SMEM)] + _attn_specs() + [blk(ATTN_W), blk(128), blk(ATTN_W)],
        [blk(ATTN_W), _full((s, KV_W)), _full((s, KV_W)), _full((1, 128))],
        [_sds((s, ATTN_W), BF16), _sds((s, KV_W), F32), _sds((s, KV_W), F32), _sds((1, 128), F32)],
        (sinks, qkv, qkv, qkv, qkv, qkv, attn, lse, dattn), comm=comm)


DPROJ_PIECES = (ATTN_W, KV_W, KV_W, CONV_W, CONV_W, CONV_W, GATE_W)
DPROJ_OFFSETS = tuple(sum(DPROJ_PIECES[:k]) for k in range(len(DPROJ_PIECES)))


def _grad_w_in(pieces, xn, comm):
    s = xn.shape[0]
    ts = min(1024, s)
    steps = s // ts
    rows0 = DPROJ_OFFSETS[6]

    def body(*refs):
        p_refs, b_ref, o_ref, acc_ref, stage_ref, sem = refs[:7], refs[7], refs[8], refs[9], refs[10], refs[11]
        i, j = pl.program_id(0), pl.program_id(1)

        @pl.when(j == 0)
        def _():
            acc_ref[...] = jnp.zeros_like(acc_ref)

        bv = b_ref[...]

        def flush(lo, n):
            stage_ref[0:n, :] = acc_ref[0:n, :].astype(BF16)
            cp = pltpu.make_async_copy(stage_ref.at[0:n, :], o_ref.at[lo:lo + n, :], sem)
            cp.start()
            cp.wait()

        @pl.when(i == 0)
        def _():
            for p_ref, off, w in zip(p_refs[:6], DPROJ_OFFSETS[:6], DPROJ_PIECES[:6]):
                acc_ref[off:off + w, :] += lax.dot_general(p_ref[...].astype(BF16), bv, TN,
                                                           preferred_element_type=F32)

            @pl.when(j == steps - 1)
            def _():
                flush(0, rows0)

        @pl.when(i == 1)
        def _():
            acc_ref[0:GATE_W, :] += lax.dot_general(p_refs[6][...], bv, TN, preferred_element_type=F32)

            @pl.when(j == steps - 1)
            def _():
                flush(rows0, GATE_W)

    def piece_spec(w, group):
        return pl.BlockSpec((ts, w), lambda i, j: (jnp.where(i == group, j, 0), 0))

    outs, couts = _pcall(
        body, "grad_w_in", (2, steps),
        [piece_spec(w, 0) for w in DPROJ_PIECES[:6]] + [piece_spec(GATE_W, 1),
                                                         pl.BlockSpec((ts, D_MODEL), lambda i, j: (j, 0))],
        [ANY], [_sds((IN_W, D_MODEL), BF16)], (*pieces, xn),
        scratch=[pltpu.VMEM((rows0, D_MODEL), F32), pltpu.VMEM((rows0, D_MODEL), BF16), pltpu.SemaphoreType.DMA],
        comm=comm)
    return outs[0], couts


def _inproj_bwd(pieces, win_t, x, g, dh1, comm):
    s = x.shape[0]
    tm = _row_tile(s, 512)

    def body(*refs):
        p_refs = refs[:7]
        w_ref, x_ref, g_ref, dh_ref, dx_ref, db_ref, dg_ref = refs[7:]

        @pl.when(pl.program_id(0) == 0)
        def _():
            db_ref[...] = jnp.zeros_like(db_ref)
            dg_ref[...] = jnp.zeros_like(dg_ref)

        dxn = jnp.zeros((tm, D_MODEL), F32)
        for p_ref, off, w in zip(p_refs, DPROJ_OFFSETS, DPROJ_PIECES):
            v = p_ref[...].astype(BF16)
            db_ref[:, off:off + w] += jnp.sum(v.astype(F32), axis=0, keepdims=True)
            dxn = dxn + jnp.dot(v, w_ref[off:off + w, :], preferred_element_type=F32)
        dx, dg = _norm_bwd_tile(x_ref[...], g_ref[...], dxn)
        dg_ref[...] += dg
        dx_ref[...] = dh_ref[...] + dx

    return _pcall(
        body, "inproj_bwd", (s // tm,),
        [_rows(tm, w) for w in DPROJ_PIECES] + [_resident((IN_W, D_MODEL)), _rows(tm, D_MODEL), _full((1, D_MODEL)),
                                                _rows(tm, D_MODEL)],
        [_rows(tm, D_MODEL), _full((8, IN_W)), _full((8, D_MODEL))],
        [_sds((s, D_MODEL), F32), _sds((8, IN_W), F32), _sds((8, D_MODEL), F32)],
        (*pieces, win_t, x, g, dh1), comm=comm)


def _adam_math(w, g, m, v):
    m2 = ADAM_B1 * m + (1.0 - ADAM_B1) * g
    v2 = ADAM_B2 * v + (1.0 - ADAM_B2) * (g * g)
    m_hat = m2 / (1.0 - ADAM_B1 ** ADAM_STEP)
    v_hat = v2 / (1.0 - ADAM_B2 ** ADAM_STEP)
    delta = -ADAM_LR * (m_hat / (jnp.sqrt(v_hat) + ADAM_EPS) + ADAM_WD * w)
    return delta, m2, v2


def _sum_slots(ref):
    tot = ref[0].astype(F32)
    for i in range(1, ref.shape[0]):
        tot = tot + ref[i].astype(F32)
    return tot


def _pair_add(partials, theirs, tr, name):
    r = partials.shape[0] // N_DEV
    c = partials.shape[1]
    nt = r // tr
    core = lax.axis_index("c").astype(jnp.int32).reshape(1)

    def body(core_ref, a_ref, b_ref, o_ref):
        o_ref[...] = (a_ref[...].astype(F32) + b_ref[...].astype(F32)).astype(BF16)

    grid_spec = pltpu.PrefetchScalarGridSpec(
        num_scalar_prefetch=1, grid=(4 * nt,),
        in_specs=[pl.BlockSpec((None, None, tr, c), lambda i, core_ref: (i // nt, core_ref[0], i % nt, 0)),
                  pl.BlockSpec((tr, c), lambda i, core_ref: (i, 0))],
        out_specs=pl.BlockSpec((tr, c), lambda i, core_ref: (i, 0)))
    return pl.pallas_call(body, name=name, grid_spec=grid_spec, out_shape=_sds((4 * r, c), BF16))(
        core, partials.reshape(4, 2, r, c), theirs)


def _sum_adamw(parts, w, m, v, tr, name):
    r, c = w.shape

    def body(p_ref, w_ref, m_ref, v_ref, g_ref, d_ref, m2_ref, v2_ref):
        g = _sum_slots(p_ref)
        g_ref[...] = g
        d_ref[...], m2_ref[...], v2_ref[...] = _adam_math(w_ref[...], g, m_ref[...], v_ref[...])

    spec = pl.BlockSpec((tr, c), lambda i: (i, 0))
    return _pcall(body, name, (r // tr,), [pl.BlockSpec((N_DEV, tr, c), lambda i: (0, i, 0)), spec, spec, spec],
                  [spec] * 4, [_sds((r, c), F32)] * 4, (parts, w, m, v))[0]


def _sum_parts_adamw(parts, w, m, v, tr, name):
    c = w.shape[1]
    tiles = [p.shape[1] // tr for p in parts]
    starts = [sum(tiles[:k]) for k in range(len(parts))]
    n_parts = len(parts)

    def body(*refs):
        p_refs = refs[:n_parts]
        w_ref, m_ref, v_ref, g_ref, d_ref, m2_ref, v2_ref = refs[n_parts:]
        i = pl.program_id(0)
        for p_ref, st, nt in zip(p_refs, starts, tiles):
            @pl.when(jnp.logical_and(i >= st, i < st + nt))
            def _(p_ref=p_ref):
                g_ref[...] = _sum_slots(p_ref)

        d_ref[...], m2_ref[...], v2_ref[...] = _adam_math(w_ref[...], g_ref[...], m_ref[...], v_ref[...])

    def part_spec(p, st, nt):
        return pl.BlockSpec((p.shape[0], tr, c), lambda i: (0, jnp.clip(i - st, 0, nt - 1), 0))

    spec = pl.BlockSpec((tr, c), lambda i: (i, 0))
    return _pcall(
        body, name, (sum(tiles),),
        [part_spec(p, st, nt) for p, st, nt in zip(parts, starts, tiles)] + [spec, spec, spec],
        [spec] * 4, [_sds(w.shape, F32)] * 4, (*parts, w, m, v))[0]


ROW_MIX, ROW_FFN, ROW_FINAL, ROW_SINKS, ROW_LOSS, ROW_BIN, ROW_CW, ROW_FCW = 0, 1, 2, 3, 4, 5, 10, 13
FCW_ROWS = 6


def _wide_pieces(width):
    return [(k * D_MODEL, min(D_MODEL, width - k * D_MODEL)) for k in range(-(-width // D_MODEL))]


def _pack_small(dffn, dfn, dsink, loss, dcw, dfcw):
    def body(ffn_ref, fn_ref, sink_ref, loss_ref, cw_ref, fcw_ref, o_ref):
        o_ref[...] = jnp.zeros_like(o_ref)
        o_ref[ROW_FFN:ROW_FFN + 1, :] = ffn_ref[...]
        o_ref[ROW_FINAL:ROW_FINAL + 1, :] = fn_ref[...]
        o_ref[ROW_SINKS:ROW_SINKS + 1, 0:128] = sink_ref[...]
        o_ref[ROW_LOSS:ROW_LOSS + 1, 0:128] = loss_ref[...]
        o_ref[ROW_CW:ROW_CW + 3, 0:CONV_W] = cw_ref[...]
        for a in range(3):
            for k, (off, w) in enumerate(_wide_pieces(2 * D_FF)):
                row = ROW_FCW + FCW_ROWS * a + k
                o_ref[row:row + 1, 0:w] = fcw_ref[a:a + 1, off:off + w]

    return pl.pallas_call(body, name="pack_small", out_shape=_sds((SMALL_ROWS, D_MODEL), F32))(
        dffn, dfn, dsink, loss, dcw, dfcw)


def _small_sums_adamw(r_small, r_dmix, r_dbin, params):
    rows = (None, None, ROW_SINKS, ROW_FFN, ROW_FINAL)

    def sum_row0(ref):
        tot = ref[0:1, :]
        for i in range(1, N_DEV):
            tot = tot + ref[8 * i:8 * i + 1, :]
        return tot

    def body(*refs):
        r_ref, late_refs, p_refs, o_refs = refs[0], refs[1:3], refs[3:18], refs[18:]
        tot = _sum_slots(r_ref)
        for k, row in enumerate(rows):
            w_ref, m_ref, v_ref = p_refs[3 * k:3 * k + 3]
            g_ref, d_ref, m2_ref, v2_ref = o_refs[4 * k:4 * k + 4]
            if row is None:
                g_ref[...] = sum_row0(late_refs[k])
            else:
                for j, (off, w) in enumerate(_wide_pieces(w_ref.shape[1])):
                    g_ref[:, off:off + w] = tot[row + j:row + j + 1, 0:w]
            d_ref[...], m2_ref[...], v2_ref[...] = _adam_math(w_ref[...], g_ref[...], m_ref[...], v_ref[...])
        cw_ref, fcw_ref, loss_ref = o_refs[20:]
        cw_ref[...] = tot[ROW_CW:ROW_CW + 3, 0:CONV_W]
        for a in range(3):
            for j, (off, w) in enumerate(_wide_pieces(2 * D_FF)):
                row = ROW_FCW + FCW_ROWS * a + j
                fcw_ref[a:a + 1, off:off + w] = tot[row:row + 1, 0:w]
        loss_ref[...] = tot[ROW_LOSS:ROW_LOSS + 1, 0:128]

    flat = [t for p in params for t in p]
    out_shape = [_sds(p[0].shape, F32) for p in params for _ in range(4)]
    out_shape += [_sds((3, CONV_W), F32), _sds((3, 2 * D_FF), F32), _sds((1, 128), F32)]
    res = pl.pallas_call(body, name="small_sums_adamw", out_shape=out_shape)(r_small, r_dmix, r_dbin, *flat)
    return [tuple(res[4 * k:4 * k + 4]) for k in range(5)], res[20], res[21], res[22]


def _adamw_pair(a, b):
    def body(*refs):
        for k in range(2):
            w_ref, g_ref, m_ref, v_ref = refs[4 * k:4 * k + 4]
            d_ref, m2_ref, v2_ref = refs[8 + 3 * k:8 + 3 * k + 3]
            d_ref[...], m2_ref[...], v2_ref[...] = _adam_math(w_ref[...], g_ref[...], m_ref[...], v_ref[...])

    out_shape = [_sds(a[0].shape, F32)] * 3 + [_sds(b[0].shape, F32)] * 3
    res = pl.pallas_call(body, name="adamw_conv_weights", out_shape=out_shape)(*a, *b)
    return tuple(res[:3]), tuple(res[3:])


def _pad_cols(a, c):
    return jnp.pad(a, ((0, 0), (0, c - a.shape[1])))


def _to_col_slabs(g):
    r = g.shape[0]
    return jnp.transpose(g.reshape(r, N_DEV, 128), (1, 0, 2)).reshape(N_DEV * r, 128)


def _from_col_slabs(t):
    r = t.shape[0] // N_DEV
    return jnp.transpose(t.reshape(N_DEV, r, 128), (1, 0, 2)).reshape(r, N_DEV * 128)


def _slots(t):
    return t.reshape(N_DEV, t.shape[0] // N_DEV, t.shape[1])


def kernel(x, mix_norm, w_in, b_in, sinks, conv_w, w_attn_branch, w_conv_branch, w_out, ffn_norm, w_up, ffn_conv_w, w_down, final_norm, loss_target, m_mix_norm, m_w_in, m_b_in, m_sinks, m_conv_w, m_w_attn_branch, m_w_conv_branch, m_w_out, m_ffn_norm, m_w_up, m_ffn_conv_w, m_w_down, m_final_norm, v_mix_norm, v_w_in, v_b_in, v_sinks, v_conv_w, v_w_attn_branch, v_w_conv_branch, v_w_out, v_ffn_norm, v_w_up, v_ffn_conv_w, v_w_down, v_final_norm):
    xs, tgt = x[0], loss_target[0]
    me = 4 * lax.axis_index("x") + 2 * lax.axis_index("y") + lax.axis_index("c")
    in_rows, up_rows = IN_W // N_DEV, 2 * D_FF // N_DEV

    conv_sh = jnp.concatenate([_pad_cols(ffn_conv_w[0], 768), _pad_cols(conv_w[0], 768),
                               jnp.zeros((2, 768), F32)], axis=0)
    win_sh, wup_sh = w_in[0].T.astype(BF16), w_up[0].T.astype(BF16)
    wout_sh, wdown_sh = w_out[0].astype(BF16), w_down[0].astype(BF16)
    wa_sh, wc_sh = w_attn_branch[0].astype(BF16), w_conv_branch[0].astype(BF16)

    half = D_MODEL // 2
    (win_t,) = _exchange_only(_AllGather([win_sh]), "gather_w_in")
    (xn, qkv, cbx, gates), (wa_s, wc_s, wout, conv_g) = _norm_inproj(
        xs, mix_norm, win_t, b_in, _AllGather([wa_sh, wc_sh, wout_sh, conv_sh]))
    (attn, lse), (wup_lo,) = _attn_fwd(qkv, sinks, _AllGather([wup_sh[:, :half]]))
    wa, wc = _from_col_slabs(wa_s), _from_col_slabs(wc_s)
    conv_g = conv_g.reshape(N_DEV, 8, 768)
    fcw = jnp.transpose(conv_g[:, 0:3, :up_rows], (1, 0, 2)).reshape(3, 2 * D_FF)
    cw = jnp.transpose(conv_g[:, 3:6, :CONV_W // N_DEV], (1, 0, 2)).reshape(3, CONV_W)
    (h1,), (wup_hi,) = _mix_fwd(xs, cbx, gates, attn, cw, wa, wc, wout, _AllGather([wup_sh[:, half:]]))
    (hn, up_pre), (wdown,) = _ffn_up(h1, ffn_norm, wup_lo, wup_hi, _AllGather([wdown_sh]))
    up, act, dh2, loss_p, dfn_p = _ffn_down_loss(up_pre, fcw, wdown, h1, final_norm.reshape(1, D_MODEL), tgt)

    dn_rows, q_up = D_FF // N_DEV, up_rows // 4
    g_wdown = _matmul_tn(act, dh2, FF_CHUNK, "grad_w_down")
    (dup_pre, dfcw_p, dh1, dffn_p), (r_wdown,) = _ffn_bwd(dh2, wdown, up, up_pre, fcw, wup_lo, wup_hi, h1, ffn_norm,
                                                         _ReduceScatter([(g_wdown, 0, dn_rows)]))
    g_wup_t = _matmul_tn(dup_pre, hn, FF_CHUNK, "grad_w_up")
    (dgates, dattn, dcb, dcc, dcx, dcw_p, g_wout, g_wa_nat, g_wc_nat), (r_wup_ab,) = _mix_bwd(
        dh1, wout, gates, attn, wa, wc, cbx, cw, _ReduceScatter([(g_wup_t, 0, 2 * q_up)]))
    g_wa, g_wc = _to_col_slabs(g_wa_nat), _to_col_slabs(g_wc_nat)
    (dq, dk, dv, dsink_p), (r_wup_c, r_wout, r_wa, r_wc) = _attn_bwd(
        qkv, sinks, attn, lse, dattn,
        _ReduceScatter([(g_wup_t, 2 * q_up, q_up), (g_wout, 0, D_MODEL // N_DEV), (g_wa, 0, ATTN_W),
                        (g_wc, 0, CONV_W)]))
    dproj = (dq, dk, dv, dcb, dcc, dcx, dgates)
    small = _pack_small(dffn_p, dfn_p, dsink_p, loss_p, dcw_p, dfcw_p)
    g_win_t, (r_wup_d, r_small) = _grad_w_in(dproj, xn, _ReduceScatter([(g_wup_t, 3 * q_up, q_up)], [small]))
    (win_theirs,) = _exchange_only(_PairExchange([g_win_t]), "pair_exchange_w_in")
    q_win = _pair_add(g_win_t, win_theirs, in_rows // 2, "pair_add_w_in")
    (dx, _, _), (r_win, r_dbin, r_dmix) = _inproj_bwd(
        dproj, win_t, xs, mix_norm, dh1,
        _ChipExchangeThenBroadcast([q_win], late_from=(1, 2), late_shapes=[(8, IN_W), (8, D_MODEL)]))

    fn2, m_fn2, v_fn2 = (t.reshape(1, D_MODEL) for t in (final_norm, m_final_norm, v_final_norm))
    small_res, g_cw_full, g_fcw_full, loss_row = _small_sums_adamw(
        _slots(r_small), r_dmix, r_dbin,
        [(mix_norm, m_mix_norm, v_mix_norm), (b_in, m_b_in, v_b_in), (sinks, m_sinks, v_sinks),
         (ffn_norm, m_ffn_norm, v_ffn_norm), (fn2, m_fn2, v_fn2)])
    loss = loss_row[0, 0]
    g_cw = lax.dynamic_slice_in_dim(g_cw_full, me * (CONV_W // N_DEV), CONV_W // N_DEV, axis=1)
    g_fcw = lax.dynamic_slice_in_dim(g_fcw_full, me * up_rows, up_rows, axis=1)
    taps = lambda t: jnp.transpose(t, (1, 0, 2))
    g_cw, g_fcw = g_cw[:, None, :], g_fcw[:, None, :]
    cw_res, fcw_res = _adamw_pair((taps(conv_w), g_cw, taps(m_conv_w), taps(v_conv_w)),
                                  (taps(ffn_conv_w), g_fcw, taps(m_ffn_conv_w), taps(v_ffn_conv_w)))

    big = {}
    big["w_in"] = tuple(t.T for t in _sum_parts_adamw(
        [r_win.reshape(4, in_rows, D_MODEL)], w_in[0].T, m_w_in[0].T, v_w_in[0].T, in_rows // 2, "adamw_w_in"))
    big["w_up"] = tuple(t.T for t in _sum_parts_adamw(
        [_slots(r_wup_ab), _slots(r_wup_c), _slots(r_wup_d)], w_up[0].T, m_w_up[0].T, v_w_up[0].T, q_up,
        "adamw_w_up"))
    big["w_out"] = _sum_adamw(_slots(r_wout), w_out[0], m_w_out[0], v_w_out[0], 128, "adamw_w_out")
    big["w_down"] = _sum_adamw(_slots(r_wdown), w_down[0], m_w_down[0], v_w_down[0], dn_rows // 2, "adamw_w_down")
    big["w_attn_branch"] = _sum_adamw(_slots(r_wa), w_attn_branch[0], m_w_attn_branch[0], v_w_attn_branch[0], 256,
                                      "adamw_w_attn_branch")
    big["w_conv_branch"] = _sum_adamw(_slots(r_wc), w_conv_branch[0], m_w_conv_branch[0], v_w_conv_branch[0], 256,
                                      "adamw_w_conv_branch")

    res = dict(zip(("mix_norm", "b_in", "sinks", "ffn_norm"), small_res[:4]))
    res["final_norm"] = tuple(t.reshape(final_norm.shape) for t in small_res[4])
    res["conv_w"] = tuple(jnp.transpose(t, (1, 0, 2)) for t in (g_cw,) + cw_res)
    res["ffn_conv_w"] = tuple(jnp.transpose(t, (1, 0, 2)) for t in (g_fcw,) + fcw_res)
    for name, ref_w in (("w_in", w_in), ("w_up", w_up), ("w_out", w_out), ("w_down", w_down),
                        ("w_attn_branch", w_attn_branch), ("w_conv_branch", w_conv_branch)):
        res[name] = tuple(t.reshape(ref_w.shape) for t in big[name])

    order = ["mix_norm", "w_in", "b_in", "sinks", "conv_w", "w_attn_branch", "w_conv_branch", "w_out",
             "ffn_norm", "w_up", "ffn_conv_w", "w_down", "final_norm"]
    out = [loss, dx.reshape(x.shape)]
    for k in range(4):
        out += [res[name][k] for name in order]
    return tuple(out)
```

```python
import math

import jax
import jax.numpy as jnp
from jax import lax
from jax.experimental import pallas as pl
from jax.experimental.pallas import tpu as pltpu

F32 = jnp.float32
BF16 = jnp.bfloat16
MESH = pl.DeviceIdType.MESH
N_DEV = 8

D_MODEL = 1024
HEAD_DIM = 64
N_HEADS = 8
BLOCK = 128
ATTN_W = 512
KV_W = 128
CONV_W = 512
QKV_W = ATTN_W + 2 * KV_W
CBX_W = 3 * CONV_W
GATE_W = 2 * D_MODEL
IN_W = QKV_W + CBX_W + GATE_W
D_FF = 2816
FF_CHUNK = 1408
NORM_EPS = 1e-5
ATTN_SCALE = HEAD_DIM ** -0.5
NEG = -1e30
HALO = 16

ADAM_LR = 0.001
ADAM_B1 = 0.9
ADAM_B2 = 0.999
ADAM_EPS = 1e-08
ADAM_WD = 0.01
ADAM_STEP = 10

VMEM_LIMIT = 56 * 1024 * 1024
SMALL_ROWS = 32

NT = (((1,), (1,)), ((), ()))
TN = (((0,), (0,)), ((), ()))
ANY = pl.BlockSpec(memory_space=pl.ANY)


def _sig(v):
    return 1.0 / (1.0 + jnp.exp(-v))


def _row_tile(s, pref=256):
    return pref if s % pref == 0 else s


def _shifts_down(u, halo, ks):
    ext = jnp.concatenate([halo, u], axis=0)
    return [pltpu.roll(ext, k, axis=0)[HALO:, :] for k in ks]


def _shifts_up(u, halo, ks):
    n = u.shape[0]
    ext = jnp.concatenate([u, halo], axis=0)
    return [pltpu.roll(ext, n + HALO - k, axis=0)[:n, :] for k in ks]


def _rows_reversed(tm, c, steps):
    return pl.BlockSpec((tm, c), lambda i: (steps - 1 - i, 0))


def _prev_halo_map_reversed(tm, steps):
    return lambda i: (jnp.maximum((steps - 1 - i) * (tm // HALO) - 1, 0), 0)


def _prev_halo_map(tm):
    return lambda i: (jnp.maximum(i * (tm // HALO) - 1, 0), 0)


def _full(shape):
    return pl.BlockSpec(shape, lambda *_: (0,) * len(shape))


def _resident(shape):
    return pl.BlockSpec(shape, lambda *_: (0,) * len(shape), pipeline_mode=pl.Buffered(1))


def _rows(tm, c):
    return pl.BlockSpec((tm, c), lambda i: (i, 0))


def _sds(shape, dtype):
    return jax.ShapeDtypeStruct(shape, dtype)


def _my_place():
    x, y, c = lax.axis_index("x"), lax.axis_index("y"), lax.axis_index("c")
    return x, y, c


ALL_PEERS = tuple((j >> 2, (j >> 1) & 1, j & 1) for j in range(1, N_DEV))
SIBLING_PEER = ((0, 0, 1),)
CHIP_PEERS = ((0, 1, 0), (1, 0, 0), (1, 1, 0))
BARRIER_ID = {ALL_PEERS: 0, SIBLING_PEER: 1, CHIP_PEERS: 2}


def _entry_barrier(peers):
    x, y, c = _my_place()
    barrier = pltpu.get_barrier_semaphore()
    for dx, dy, dc in peers:
        pl.semaphore_signal(barrier, inc=1, device_id=(x ^ dx, y ^ dy, c ^ dc), device_id_type=MESH)
    pl.semaphore_wait(barrier, len(peers))


def _start_exchange(remote, local):
    for cp in local + remote:
        cp.start()


def _finish_exchange(remote, local):
    for cp in remote:
        cp.wait_recv()
    for cp in remote:
        cp.wait_send()
    for cp in local:
        cp.wait()


class _AllGather:
    peers = ALL_PEERS

    def __init__(self, shards):
        self.ins = list(shards)
        n = len(shards)
        self.out_shape = [_sds((N_DEV * s.shape[0], s.shape[1]), s.dtype) for s in shards]
        self.sems = [pltpu.SemaphoreType.DMA((7 * n,)), pltpu.SemaphoreType.DMA((7 * n,)),
                     pltpu.SemaphoreType.DMA((n,))]

    def _parts(self, ins, outs, sems):
        send_sems, recv_sems, local_sems = sems
        x, y, c = _my_place()
        me, sibling = (x, y, c), (x, y, 1 - c)
        chips = [(1 - x, y), (x, 1 - y), (1 - x, 1 - y)]

        def rows(k, dev):
            r = ins[k].shape[0]
            start = pl.multiple_of((4 * dev[0] + 2 * dev[1] + dev[2]) * r, 8)
            return outs[k].at[pl.ds(start, r), :]

        def copy(k, j, block, to, src=None):
            return pltpu.make_async_remote_copy(
                src_ref=rows(k, block) if src is None else src, dst_ref=rows(k, block),
                send_sem=send_sems.at[7 * k + j], recv_sem=recv_sems.at[7 * k + j],
                device_id=to, device_id_type=MESH)

        n = len(ins)
        mine = [pltpu.make_async_copy(ins[k], rows(k, me), local_sems.at[k]) for k in range(n)]
        first = []
        for k in range(n):
            first.append(copy(k, 0, me, sibling, src=ins[k]))
            first += [copy(k, 1 + j, me, (*chip, c), src=ins[k]) for j, chip in enumerate(chips)]
        return me, sibling, chips, copy, mine, first

    def start(self, ins, outs, sems):
        _, _, _, _, mine, first = self._parts(ins, outs, sems)
        _start_exchange(first, mine)

    def finish(self, ins, outs, sems):
        me, sibling, chips, copy, mine, first = self._parts(ins, outs, sems)
        c = me[2]
        n = len(ins)
        passed = []
        for j, chip in enumerate(chips):
            for k in range(n):
                copy(k, 1 + j, (*chip, c), me).wait_recv()
                fwd = copy(k, 4 + j, (*chip, c), sibling)
                fwd.start()
                passed.append(fwd)
        for k in range(n):
            copy(k, 0, sibling, me).wait_recv()
            for j, chip in enumerate(chips):
                copy(k, 4 + j, (*chip, 1 - c), me).wait_recv()
        for cp in first + passed:
            cp.wait_send()
        for cp in mine:
            cp.wait()


class _ReduceScatter:
    peers = ALL_PEERS

    def __init__(self, parts, bcast=()):
        self.parts = [(lo, cnt) for _, lo, cnt in parts]
        self.n_parts = len(parts)
        self.ins = [a for a, _, _ in parts] + list(bcast)
        self.out_shape = [_sds((N_DEV * cnt, a.shape[1]), a.dtype) for a, _, cnt in parts]
        self.out_shape += [_sds((N_DEV * b.shape[0], b.shape[1]), b.dtype) for b in bcast]
        n = len(self.ins)
        self.sems = [pltpu.SemaphoreType.DMA((7 * n,)), pltpu.SemaphoreType.DMA((7 * n,)),
                     pltpu.SemaphoreType.DMA((n,))]

    def _copies(self, ins, outs, sems):
        send_sems, recv_sems, local_sems = sems
        x, y, c = _my_place()
        me_idx = 4 * x + 2 * y + c
        remote, local = [], []
        for k in range(len(ins)):
            cnt = outs[k].shape[0] // N_DEV
            dst = outs[k].at[pl.ds(pl.multiple_of(me_idx * cnt, 8), cnt), :]
            if k < self.n_parts:
                lo, _ = self.parts[k]
                r = ins[k].shape[0] // N_DEV
                src_of = lambda idx: ins[k].at[pl.ds(pl.multiple_of(idx * r + lo, 8), cnt), :]
            else:
                src_of = lambda idx: ins[k]
            local.append(pltpu.make_async_copy(src_of(me_idx), dst, local_sems.at[k]))
            for j in range(1, N_DEV):
                peer = (x ^ (j >> 2), y ^ ((j >> 1) & 1), c ^ (j & 1))
                peer_idx = 4 * peer[0] + 2 * peer[1] + peer[2]
                remote.append(pltpu.make_async_remote_copy(
                    src_ref=src_of(peer_idx), dst_ref=dst,
                    send_sem=send_sems.at[7 * k + j - 1], recv_sem=recv_sems.at[7 * k + j - 1],
                    device_id=peer, device_id_type=MESH))
        return remote, local

    def start(self, ins, outs, sems):
        _start_exchange(*self._copies(ins, outs, sems))

    def finish(self, ins, outs, sems):
        _finish_exchange(*self._copies(ins, outs, sems))


class _PairExchange:
    peers = SIBLING_PEER

    def __init__(self, arrays):
        self.ins = list(arrays)
        n = len(arrays)
        self.out_shape = [_sds((a.shape[0] // 2, a.shape[1]), a.dtype) for a in arrays]
        self.sems = [pltpu.SemaphoreType.DMA((4 * n,)), pltpu.SemaphoreType.DMA((4 * n,))]

    def _copies(self, ins, outs, sems):
        send_sems, recv_sems = sems
        x, y, c = _my_place()
        remote = []
        for k in range(len(ins)):
            r = ins[k].shape[0] // N_DEV
            for chip in range(4):
                sib = ins[k].at[pl.ds(pl.multiple_of((2 * chip + 1 - c) * r, 8), r), :]
                remote.append(pltpu.make_async_remote_copy(
                    src_ref=sib, dst_ref=outs[k].at[pl.ds(chip * r, r), :],
                    send_sem=send_sems.at[4 * k + chip], recv_sem=recv_sems.at[4 * k + chip],
                    device_id=(x, y, 1 - c), device_id_type=MESH))
        return remote

    def start(self, ins, outs, sems):
        for cp in self._copies(ins, outs, sems):
            cp.start()

    def finish(self, ins, outs, sems):
        remote = self._copies(ins, outs, sems)
        for cp in remote:
            cp.wait_recv()
        for cp in remote:
            cp.wait_send()


class _ChipExchange:
    peers = CHIP_PEERS

    def __init__(self, arrays):
        self.ins = list(arrays)
        self.out_shape = [_sds(a.shape, a.dtype) for a in arrays]
        n = len(self.ins)
        self.sems = [pltpu.SemaphoreType.DMA((3 * n,)), pltpu.SemaphoreType.DMA((3 * n,)),
                     pltpu.SemaphoreType.DMA((n,))]

    def _copies(self, ins, outs, sems):
        send_sems, recv_sems, local_sems = sems
        x, y, c = _my_place()
        my_chip = 2 * x + y
        remote, local = [], []
        for k in range(len(ins)):
            r = ins[k].shape[0] // 4
            dst = outs[k].at[pl.ds(pl.multiple_of(my_chip * r, 8), r), :]
            local.append(pltpu.make_async_copy(ins[k].at[pl.ds(pl.multiple_of(my_chip * r, 8), r), :], dst,
                                               local_sems.at[k]))
            for j in range(1, 4):
                px, py = x ^ (j >> 1), y ^ (j & 1)
                src = ins[k].at[pl.ds(pl.multiple_of((2 * px + py) * r, 8), r), :]
                remote.append(pltpu.make_async_remote_copy(
                    src_ref=src, dst_ref=dst, send_sem=send_sems.at[3 * k + j - 1],
                    recv_sem=recv_sems.at[3 * k + j - 1], device_id=(px, py, c), device_id_type=MESH))
        return remote, local

    def start(self, ins, outs, sems):
        _start_exchange(*self._copies(ins, outs, sems))

    def finish(self, ins, outs, sems):
        _finish_exchange(*self._copies(ins, outs, sems))


class _ChipExchangeThenBroadcast(_ChipExchange):
    peers = ALL_PEERS

    def __init__(self, arrays, late_from, late_shapes):
        super().__init__(arrays)
        self.n_chip = len(arrays)
        self.late_from = tuple(late_from)
        self.out_shape += [_sds((N_DEV * r, c), F32) for r, c in late_shapes]
        m = len(late_shapes)
        self.sems += [pltpu.SemaphoreType.DMA((7 * m,)), pltpu.SemaphoreType.DMA((7 * m,)),
                      pltpu.SemaphoreType.DMA((m,))]

    def _late_copies(self, srcs, outs, sems):
        send_sems, recv_sems, local_sems = sems
        x, y, c = _my_place()
        me_idx = 4 * x + 2 * y + c
        remote, local = [], []
        for k, src in enumerate(srcs):
            r = src.shape[0]
            dst = outs[k].at[pl.ds(pl.multiple_of(me_idx * r, 8), r), :]
            local.append(pltpu.make_async_copy(src, dst, local_sems.at[k]))
            for j, (dx, dy, dc) in enumerate(ALL_PEERS):
                remote.append(pltpu.make_async_remote_copy(
                    src_ref=src, dst_ref=dst, send_sem=send_sems.at[7 * k + j], recv_sem=recv_sems.at[7 * k + j],
                    device_id=(x ^ dx, y ^ dy, c ^ dc), device_id_type=MESH))
        return remote, local

    def start(self, ins, outs, sems):
        _start_exchange(*self._copies(ins, outs[:self.n_chip], sems[:3]))

    def finish(self, ins, outs, sems, late_srcs):
        late = self._late_copies(late_srcs, outs[self.n_chip:], sems[3:])
        _start_exchange(*late)
        _finish_exchange(*self._copies(ins, outs[:self.n_chip], sems[:3]))
        _finish_exchange(*late)


def _pcall(body, name, grid, in_specs, out_specs, out_shape, args, scratch=(), comm=None):
    params = pltpu.CompilerParams(dimension_semantics=("arbitrary",) * len(grid), vmem_limit_bytes=VMEM_LIMIT)
    in_specs, out_specs, out_shape, scratch = list(in_specs), list(out_specs), list(out_shape), list(scratch)
    if comm is None:
        res = pl.pallas_call(body, name=name, grid=grid, in_specs=in_specs, out_specs=out_specs, out_shape=out_shape,
                             scratch_shapes=scratch, compiler_params=params)(*args)
        return list(res), []
    n_in, n_out, n_scr = len(in_specs), len(out_specs), len(scratch)
    ci, co = len(comm.ins), len(comm.out_shape)
    total = math.prod(grid)

    def carried(*refs):
        bounds = [0, n_in, n_in + ci, n_in + ci + n_out, n_in + ci + n_out + co, n_in + ci + n_out + co + n_scr]
        ins, cins, outs, couts, scr = (refs[a:b] for a, b in zip(bounds[:-1], bounds[1:]))
        sems = refs[bounds[-1]:]
        step = pl.program_id(0)
        for d in range(1, len(grid)):
            step = step * grid[d] + pl.program_id(d)

        @pl.when(step == 0)
        def _():
            _entry_barrier(comm.peers)
            comm.start(cins, couts, sems)

        body(*ins, *outs, *scr)

        @pl.when(step == total - 1)
        def _():
            late_from = getattr(comm, "late_from", None)
            if late_from is None:
                comm.finish(cins, couts, sems)
            else:
                comm.finish(cins, couts, sems, [outs[k] for k in late_from])

    params = pltpu.CompilerParams(dimension_semantics=("arbitrary",) * len(grid), vmem_limit_bytes=VMEM_LIMIT,
                                  collective_id=BARRIER_ID[comm.peers])
    res = pl.pallas_call(
        carried, name=name, grid=grid, in_specs=in_specs + [ANY] * ci, out_specs=out_specs + [ANY] * co,
        out_shape=out_shape + comm.out_shape, scratch_shapes=scratch + comm.sems, compiler_params=params,
    )(*args, *comm.ins)
    return list(res[:n_out]), list(res[n_out:])


def _exchange_only(comm, name):
    def body(*refs):
        ci, co = len(comm.ins), len(comm.out_shape)
        _entry_barrier(comm.peers)
        comm.start(refs[:ci], refs[ci:ci + co], refs[ci + co:])
        comm.finish(refs[:ci], refs[ci:ci + co], refs[ci + co:])

    params = pltpu.CompilerParams(collective_id=BARRIER_ID[comm.peers])
    return pl.pallas_call(body, name=name, out_shape=comm.out_shape, in_specs=[ANY] * len(comm.ins),
                          out_specs=[ANY] * len(comm.out_shape), scratch_shapes=comm.sems,
                          compiler_params=params)(*comm.ins)


def _norm_inproj(x, g, win_t, b_in, comm):
    s = x.shape[0]
    tm = _row_tile(s, 512)
    widths = (QKV_W, CBX_W, GATE_W)

    def body(x_ref, g_ref, w_ref, b_ref, xn_ref, qkv_ref, cbx_ref, gate_ref):
        xv = x_ref[...]
        r = lax.rsqrt(jnp.mean(xv * xv, axis=-1, keepdims=True) + NORM_EPS)
        xn = (xv * r * g_ref[...]).astype(BF16)
        xn_ref[...] = xn
        off = 0
        for o_ref, w in zip((qkv_ref, cbx_ref, gate_ref), widths):
            acc = lax.dot_general(xn, w_ref[off:off + w, :], NT, preferred_element_type=F32)
            o_ref[...] = (acc + b_ref[:, off:off + w]).astype(BF16)
            off += w

    return _pcall(
        body, "norm_inproj", (s // tm,),
        [_rows(tm, D_MODEL), _full((1, D_MODEL)), _resident((IN_W, D_MODEL)), _full((1, IN_W))],
        [_rows(tm, D_MODEL)] + [_rows(tm, w) for w in widths],
        [_sds((s, D_MODEL), BF16)] + [_sds((s, w), BF16) for w in widths],
        (x, g, win_t, b_in), comm=comm)


def _attn_specs():
    prev = lambda n: jnp.maximum(n - 1, 0)
    return [pl.BlockSpec((BLOCK, ATTN_W), lambda n: (n, 0)),
            pl.BlockSpec((BLOCK, KV_W), lambda n: (prev(n), ATTN_W // KV_W)),
            pl.BlockSpec((BLOCK, KV_W), lambda n: (n, ATTN_W // KV_W)),
            pl.BlockSpec((BLOCK, KV_W), lambda n: (prev(n), ATTN_W // KV_W + 1)),
            pl.BlockSpec((BLOCK, KV_W), lambda n: (n, ATTN_W // KV_W + 1))]


def _lower_lanes():
    return lax.broadcasted_iota(jnp.int32, (BLOCK, 128), 1) < HEAD_DIM


def _stack_heads(val, kh):
    lower = _lower_lanes()
    parts = []
    for g in range(4):
        h = kh * 4 + g
        blk = val[:, (h // 2) * 128:(h // 2 + 1) * 128]
        keep = lower if h % 2 == 0 else jnp.logical_not(lower)
        parts.append(jnp.where(keep, blk, jnp.zeros_like(blk)))
    return jnp.concatenate(parts, axis=0)


def _dup_kv(prev_ref, cur_ref, kh):
    t = jnp.concatenate([prev_ref[...], cur_ref[...]], axis=0).astype(F32)
    rolled = pltpu.roll(t, HEAD_DIM, axis=1)
    lower = lax.broadcasted_iota(jnp.int32, t.shape, 1) < HEAD_DIM
    dup = jnp.where(lower, t, rolled) if kh == 0 else jnp.where(lower, rolled, t)
    return dup.astype(BF16)


def _attn_mask(n):
    row = lax.broadcasted_iota(jnp.int32, (4 * BLOCK, 2 * BLOCK), 0)
    kj = lax.broadcasted_iota(jnp.int32, (4 * BLOCK, 2 * BLOCK), 1)
    dist = (row & (BLOCK - 1)) + BLOCK - kj
    band = jnp.logical_and(dist >= 0, dist < BLOCK)
    return jnp.logical_and(band, jnp.logical_or(kj >= BLOCK, n > 0))


def _sink_col(sinks_ref, kh):
    gi = lax.broadcasted_iota(jnp.int32, (4 * BLOCK, 1), 0) // BLOCK
    col = jnp.zeros((4 * BLOCK, 1), F32)
    for g in range(4):
        col = jnp.where(gi == g, sinks_ref[0, kh * 4 + g], col)
    return col


def _attn_fwd(qkv, sinks, comm):
    s = qkv.shape[0]

    def body(sinks_ref, q_ref, kp_ref, kc_ref, vp_ref, vc_ref, o_ref, lse_ref):
        n = pl.program_id(0)
        mask = _attn_mask(n)
        lower = _lower_lanes()
        lane = lax.broadcasted_iota(jnp.int32, (BLOCK, 128), 1)
        qv = q_ref[...]
        lse_out = jnp.zeros((BLOCK, 128), F32)
        for kh in range(2):
            qs = _stack_heads(qv, kh)
            kd, vd = _dup_kv(kp_ref, kc_ref, kh), _dup_kv(vp_ref, vc_ref, kh)
            sc = lax.dot_general(qs, kd, NT, preferred_element_type=F32) * ATTN_SCALE
            sc = jnp.where(mask, sc, NEG)
            sink = _sink_col(sinks_ref, kh)
            m = jnp.maximum(jnp.max(sc, axis=1, keepdims=True), sink)
            p = jnp.exp(sc - m)
            l = jnp.sum(p, axis=1, keepdims=True) + jnp.exp(sink - m)
            o = jnp.dot(p.astype(BF16), vd, preferred_element_type=F32) / l
            lse = m + jnp.log(l)
            for pair in range(2):
                lo = o[(2 * pair) * BLOCK:(2 * pair + 1) * BLOCK]
                hi = o[(2 * pair + 1) * BLOCK:(2 * pair + 2) * BLOCK]
                col = (kh * 2 + pair) * 128
                o_ref[:, col:col + 128] = jnp.where(lower, lo, hi).astype(BF16)
            for g in range(4):
                lse_out = jnp.where(lane == kh * 4 + g, lse[g * BLOCK:(g + 1) * BLOCK], lse_out)
        lse_ref[...] = lse_out

    return _pcall(
        body, "attn_fwd", (s // BLOCK,),
        [pl.BlockSpec(memory_space=pltpu.SMEM)] + _attn_specs(),
        [pl.BlockSpec((BLOCK, ATTN_W), lambda n: (n, 0)), pl.BlockSpec((BLOCK, 128), lambda n: (n, 0))],
        [_sds((s, ATTN_W), BF16), _sds((s, 128), F32)],
        (sinks, qkv, qkv, qkv, qkv, qkv), comm=comm)


def _conv_u(cbx_ref, halo_ref, w_ref, first):
    cb = cbx_ref[:, 0:CONV_W].astype(F32)
    cc = cbx_ref[:, CONV_W:2 * CONV_W].astype(F32)
    cx = cbx_ref[:, 2 * CONV_W:3 * CONV_W].astype(F32)
    u = cc * cx
    uh = halo_ref[:, CONV_W:2 * CONV_W].astype(F32) * halo_ref[:, 2 * CONV_W:3 * CONV_W].astype(F32)
    uh = jnp.where(first, 0.0, uh)
    u1, u2 = _shifts_down(u, uh, (1, 2))
    cv = w_ref[0:1, :] * u2 + w_ref[1:2, :] * u1 + w_ref[2:3, :] * u
    return cb, cc, cx, u, cv


def _mix_fwd(x, cbx, gates, attn, conv_w, wa, wc, wout, comm):
    s = x.shape[0]
    tm = _row_tile(s)

    def body(x_ref, cbx_ref, halo_ref, gate_ref, attn_ref, cw_ref, wa_ref, wc_ref, wo_ref,
             h1_ref):
        first = pl.program_id(0) == 0
        cb, _, _, _, cv = _conv_u(cbx_ref, halo_ref, cw_ref, first)
        conv = (cb * cv).astype(BF16)
        ap = jnp.dot(attn_ref[...], wa_ref[...], preferred_element_type=F32)
        cp = jnp.dot(conv, wc_ref[...], preferred_element_type=F32)
        ga = gate_ref[:, 0:D_MODEL].astype(F32)
        gc = gate_ref[:, D_MODEL:2 * D_MODEL].astype(F32)
        merged = (_sig(ga) * ap + _sig(gc) * cp).astype(BF16)
        h1_ref[...] = x_ref[...] + jnp.dot(merged, wo_ref[...], preferred_element_type=F32)

    return _pcall(
        body, "mix_fwd", (s // tm,),
        [_rows(tm, D_MODEL), _rows(tm, CBX_W), pl.BlockSpec((HALO, CBX_W), _prev_halo_map(tm)),
         _rows(tm, GATE_W), _rows(tm, ATTN_W), _full((3, CONV_W)), _full((ATTN_W, D_MODEL)),
         _full((CONV_W, D_MODEL)), _full((D_MODEL, D_MODEL))],
        [_rows(tm, D_MODEL)], [_sds((s, D_MODEL), F32)],
        (x, cbx, cbx, gates, attn, conv_w, wa, wc, wout), comm=comm)


def _ffn_up(h1, g, wup_lo, wup_hi, fcw, comm):
    s = h1.shape[0]
    tm = _row_tile(s, 512)
    half = D_MODEL // 2

    def body(h_ref, g_ref, wl_ref, wh_ref, fcw_ref, hn_ref, pre_ref, up_ref, carry_ref):
        @pl.when(pl.program_id(0) == 0)
        def _():
            carry_ref[...] = jnp.zeros_like(carry_ref)

        hv = h_ref[...]
        r = lax.rsqrt(jnp.mean(hv * hv, axis=-1, keepdims=True) + NORM_EPS)
        hn = (hv * r * g_ref[...]).astype(BF16)
        hn_ref[...] = hn
        for c in range(2 * D_FF // FF_CHUNK):
            sl = slice(c * FF_CHUNK, (c + 1) * FF_CHUNK)
            acc = lax.dot_general(hn[:, :half], wl_ref[sl, :], NT, preferred_element_type=F32)
            acc = acc + lax.dot_general(hn[:, half:], wh_ref[sl, :], NT, preferred_element_type=F32)
            pre_ref[:, sl] = acc.astype(BF16)
            halo = carry_ref[:, sl]
            carry_ref[:, sl] = acc[tm - HALO:, :]
            u1, u2 = _shifts_down(acc, halo, (1, 2))
            w = fcw_ref[:, sl]
            up_ref[:, sl] = (w[0:1] * u2 + w[1:2] * u1 + w[2:3] * acc).astype(BF16)

    return _pcall(
        body, "ffn_up", (s // tm,),
        [_rows(tm, D_MODEL), _full((1, D_MODEL)), _resident((2 * D_FF, half)), _resident((2 * D_FF, half)),
         _full((3, 2 * D_FF))],
        [_rows(tm, D_MODEL), _rows(tm, 2 * D_FF), _rows(tm, 2 * D_FF)],
        [_sds((s, D_MODEL), BF16), _sds((s, 2 * D_FF), BF16), _sds((s, 2 * D_FF), BF16)],
        (h1, g, wup_lo, wup_hi, fcw), scratch=[pltpu.VMEM((HALO, 2 * D_FF), F32)], comm=comm)


def _ffn_down_loss(up, wdown, h1, fnorm, target):
    s = h1.shape[0]
    tm = _row_tile(s)

    def body(up_ref, wd_ref, h1_ref, fn_ref, t_ref, act_ref, dh2_ref, loss_ref, dfn_ref):
        i = pl.program_id(0)

        @pl.when(i == 0)
        def _():
            loss_ref[...] = jnp.zeros_like(loss_ref)
            dfn_ref[...] = jnp.zeros_like(dfn_ref)

        h2 = h1_ref[...]
        for c in range(D_FF // FF_CHUNK):
            gsl = slice(c * FF_CHUNK, (c + 1) * FF_CHUNK)
            vsl = slice(D_FF + c * FF_CHUNK, D_FF + (c + 1) * FF_CHUNK)
            gate = up_ref[:, gsl].astype(F32)
            val = up_ref[:, vsl].astype(F32)
            act = (gate * _sig(gate) * val).astype(BF16)
            act_ref[:, gsl] = act
            h2 = h2 + jnp.dot(act, wd_ref[gsl, :], preferred_element_type=F32)
        r = lax.rsqrt(jnp.mean(h2 * h2, axis=-1, keepdims=True) + NORM_EPS)
        yhat = h2 * r
        fn = fn_ref[...]
        diff = yhat * fn - t_ref[...]
        loss_ref[...] += 0.5 * jnp.sum(jnp.sum(diff * diff, axis=1, keepdims=True), axis=0, keepdims=True) / D_MODEL
        dy = diff * (1.0 / D_MODEL)
        dfn_ref[...] += jnp.sum(dy * yhat, axis=0, keepdims=True)
        dyh = dy * fn
        dh2_ref[...] = r * (dyh - yhat * jnp.mean(dyh * yhat, axis=-1, keepdims=True))

    return _pcall(
        body, "ffn_down_loss", (s // tm,),
        [_rows(tm, 2 * D_FF), _resident((D_FF, D_MODEL)), _rows(tm, D_MODEL), _full((1, D_MODEL)),
         _rows(tm, D_MODEL)],
        [_rows(tm, D_FF), _rows(tm, D_MODEL), _full((1, 128)), _full((1, D_MODEL))],
        [_sds((s, D_FF), BF16), _sds((s, D_MODEL), F32), _sds((1, 128), F32), _sds((1, D_MODEL), F32)],
        (up, wdown, h1, fnorm, target))[0]


def _ffn_bwd(dh2, wdown, up, up_pre, fcw, wup_lo, wup_hi, h1, g, comm):
    s = dh2.shape[0]
    tm = _row_tile(s)
    half = D_MODEL // 2

    def dup_cols(dh, up_ref, wd_ref, c):
        gsl = slice(c * FF_CHUNK, (c + 1) * FF_CHUNK)
        vsl = slice(D_FF + c * FF_CHUNK, D_FF + (c + 1) * FF_CHUNK)
        dact = lax.dot_general(dh, wd_ref[gsl, :], NT, preferred_element_type=F32)
        gate = up_ref[:, gsl].astype(F32)
        val = up_ref[:, vsl].astype(F32)
        sg = _sig(gate)
        return dact * val * (sg * (1.0 + gate * (1.0 - sg))), dact * gate * sg

    def body(dh_ref, wd_ref, up_ref, x_ref, w_ref, wl_ref, wh_ref, h_ref, g_ref,
             dx_ref, dw_ref, dh1_ref, dg_ref, carry_ref):
        @pl.when(pl.program_id(0) == 0)
        def _():
            dw_ref[...] = jnp.zeros_like(dw_ref)
            dg_ref[...] = jnp.zeros_like(dg_ref)
            carry_ref[...] = jnp.zeros_like(carry_ref)

        dh2v = dh_ref[...]
        dh = dh2v.astype(BF16)
        dhn_lo = jnp.zeros((tm, half), F32)
        dhn_hi = jnp.zeros((tm, half), F32)
        for c in range(D_FF // FF_CHUNK):
            for d, off in zip(dup_cols(dh, up_ref, wd_ref, c), (c * FF_CHUNK, D_FF + c * FF_CHUNK)):
                sl = slice(off, off + FF_CHUNK)
                dn = carry_ref[:, sl]
                carry_ref[:, sl] = d[0:HALO, :]
                xv = x_ref[:, sl].astype(F32)
                wv = w_ref[:, sl]
                d1, d2 = _shifts_up(d, dn, (1, 2))
                dx = (wv[2:3] * d + wv[1:2] * d1 + wv[0:1] * d2).astype(BF16)
                dx_ref[:, sl] = dx
                dhn_lo = dhn_lo + jnp.dot(dx, wl_ref[sl, :], preferred_element_type=F32)
                dhn_hi = dhn_hi + jnp.dot(dx, wh_ref[sl, :], preferred_element_type=F32)
                dw_ref[0:1, sl] += jnp.sum(d2 * xv, axis=0, keepdims=True)
                dw_ref[1:2, sl] += jnp.sum(d1 * xv, axis=0, keepdims=True)
                dw_ref[2:3, sl] += jnp.sum(d * xv, axis=0, keepdims=True)
        dx1, dg = _norm_bwd_tile(h_ref[...], g_ref[...], jnp.concatenate([dhn_lo, dhn_hi], axis=1))
        dg_ref[...] += dg
        dh1_ref[...] = dh2v + dx1

    rows = lambda c: _rows_reversed(tm, c, s // tm)
    return _pcall(
        body, "ffn_bwd", (s // tm,),
        [rows(D_MODEL), _resident((D_FF, D_MODEL)), rows(2 * D_FF), rows(2 * D_FF), _full((3, 2 * D_FF)),
         _resident((2 * D_FF, half)), _resident((2 * D_FF, half)), rows(D_MODEL), _full((1, D_MODEL))],
        [rows(2 * D_FF), _full((3, 2 * D_FF)), rows(D_MODEL), _full((1, D_MODEL))],
        [_sds((s, 2 * D_FF), BF16), _sds((3, 2 * D_FF), F32), _sds((s, D_MODEL), F32), _sds((1, D_MODEL), F32)],
        (dh2, wdown, up, up_pre, fcw, wup_lo, wup_hi, h1, g),
        scratch=[pltpu.VMEM((HALO, 2 * D_FF), F32)], comm=comm)


def _matmul_tn(a, b, tk, name, ts=1024, comm=None):
    s, ka = a.shape
    n = b.shape[1]
    ts = min(ts, s)
    steps = s // ts

    def body(a_ref, b_ref, o_ref, acc_ref):
        j = pl.program_id(1)

        @pl.when(j == 0)
        def _():
            acc_ref[...] = jnp.zeros_like(acc_ref)

        acc_ref[...] += lax.dot_general(a_ref[...].astype(BF16), b_ref[...].astype(BF16), TN,
                                        preferred_element_type=F32)

        @pl.when(j == steps - 1)
        def _():
            o_ref[...] = acc_ref[...].astype(BF16)

    outs, couts = _pcall(
        body, name, (ka // tk, steps),
        [pl.BlockSpec((ts, tk), lambda i, j: (j, i)), pl.BlockSpec((ts, n), lambda i, j: (j, 0))],
        [pl.BlockSpec((tk, n), lambda i, j: (i, 0))], [_sds((ka, n), BF16)],
        (a, b), scratch=[pltpu.VMEM((tk, n), F32)], comm=comm)
    return outs[0] if comm is None else (outs[0], couts)


def _norm_bwd_tile(xv, g, dy):
    r = lax.rsqrt(jnp.mean(xv * xv, axis=-1, keepdims=True) + NORM_EPS)
    xhat = xv * r
    dg = jnp.sum(dy * xhat, axis=0, keepdims=True)
    dyh = dy * g
    return r * (dyh - xhat * jnp.mean(dyh * xhat, axis=-1, keepdims=True)), dg


def _mix_bwd(dh1, wout, gates, attn, wa, wc, cbx, conv_w, comm):
    s = dh1.shape[0]
    tm = _row_tile(s)
    steps = s // tm

    def body(dh_ref, wo_ref, gate_ref, attn_ref, wa_ref, wc_ref, cbx_ref, halo_ref,
             cw_ref, dg_ref, dattn_ref, dcb_ref, dcc_ref, dcx_ref, dw_ref, gwo_ref, gwa_ref, gwc_ref,
             acc_o, acc_a, acc_c, carry_ref):
        i = pl.program_id(0)

        @pl.when(i == 0)
        def _():
            dw_ref[...] = jnp.zeros_like(dw_ref)
            acc_o[...] = jnp.zeros_like(acc_o)
            acc_a[...] = jnp.zeros_like(acc_a)
            acc_c[...] = jnp.zeros_like(acc_c)
            carry_ref[...] = jnp.zeros_like(carry_ref)

        cb, cc, cx, u, cv = _conv_u(cbx_ref, halo_ref, cw_ref, i == steps - 1)
        attn = attn_ref[...]
        conv = (cb * cv).astype(BF16)
        ap = jnp.dot(attn, wa_ref[...], preferred_element_type=F32)
        cp = jnp.dot(conv, wc_ref[...], preferred_element_type=F32)
        dhb = dh_ref[...].astype(BF16)
        dm = lax.dot_general(dhb, wo_ref[...], NT, preferred_element_type=F32)
        sa = _sig(gate_ref[:, 0:D_MODEL].astype(F32))
        sc = _sig(gate_ref[:, D_MODEL:2 * D_MODEL].astype(F32))
        merged = (sa * ap + sc * cp).astype(BF16)
        da = (dm * sa).astype(BF16)
        dc = (dm * sc).astype(BF16)
        dg_ref[:, 0:D_MODEL] = (dm * ap * sa * (1.0 - sa)).astype(BF16)
        dg_ref[:, D_MODEL:2 * D_MODEL] = (dm * cp * sc * (1.0 - sc)).astype(BF16)
        dattn_ref[...] = lax.dot_general(da, wa_ref[...], NT, preferred_element_type=F32).astype(BF16)
        dconv = lax.dot_general(dc, wc_ref[...], NT, preferred_element_type=F32)
        dcb_ref[...] = (dconv * cv).astype(BF16)
        d = dconv * cb
        dn = carry_ref[...]
        carry_ref[...] = d[0:HALO, :]
        d1, d2 = _shifts_up(d, dn, (1, 2))
        du = cw_ref[2:3, :] * d + cw_ref[1:2, :] * d1 + cw_ref[0:1, :] * d2
        dcc_ref[...] = (du * cx).astype(BF16)
        dcx_ref[...] = (du * cc).astype(BF16)
        dw_ref[0:1, :] += jnp.sum(d2 * u, axis=0, keepdims=True)
        dw_ref[1:2, :] += jnp.sum(d1 * u, axis=0, keepdims=True)
        dw_ref[2:3, :] += jnp.sum(d * u, axis=0, keepdims=True)
        acc_o[...] += lax.dot_general(merged, dhb, TN, preferred_element_type=F32)
        acc_a[...] += lax.dot_general(attn, da, TN, preferred_element_type=F32)
        acc_c[...] += lax.dot_general(conv, dc, TN, preferred_element_type=F32)

        @pl.when(i == steps - 1)
        def _():
            gwo_ref[...] = acc_o[...].astype(BF16)
            gwa_ref[...] = acc_a[...].astype(BF16)
            gwc_ref[...] = acc_c[...].astype(BF16)

    rows = lambda c: _rows_reversed(tm, c, steps)
    return _pcall(
        body, "mix_bwd", (steps,),
        [rows(D_MODEL), _full((D_MODEL, D_MODEL)), rows(GATE_W), rows(ATTN_W), _full((ATTN_W, D_MODEL)),
         _full((CONV_W, D_MODEL)), rows(CBX_W), pl.BlockSpec((HALO, CBX_W), _prev_halo_map_reversed(tm, steps)),
         _full((3, CONV_W))],
        [rows(GATE_W), rows(ATTN_W), rows(CONV_W), rows(CONV_W), rows(CONV_W),
         _full((3, CONV_W)), _full((D_MODEL, D_MODEL)), _full((ATTN_W, D_MODEL)), _full((CONV_W, D_MODEL))],
        [_sds((s, GATE_W), BF16), _sds((s, ATTN_W), BF16), _sds((s, CONV_W), BF16), _sds((s, CONV_W), BF16),
         _sds((s, CONV_W), BF16), _sds((3, CONV_W), F32), _sds((D_MODEL, D_MODEL), BF16),
         _sds((ATTN_W, D_MODEL), BF16), _sds((CONV_W, D_MODEL), BF16)],
        (dh1, wout, gates, attn, wa, wc, cbx, cbx, conv_w),
        scratch=[pltpu.VMEM((D_MODEL, D_MODEL), F32), pltpu.VMEM((ATTN_W, D_MODEL), F32),
                 pltpu.VMEM((CONV_W, D_MODEL), F32), pltpu.VMEM((HALO, CONV_W), F32)], comm=comm)


def _attn_bwd(qkv, sinks, attn, lse, dattn, comm):
    s = qkv.shape[0]

    def body(sinks_ref, q_ref, kp_ref, kc_ref, vp_ref, vc_ref, o_ref, lse_ref, do_ref,
             dq_ref, dk_ref, dv_ref, ds_ref):
        n = pl.program_id(0)

        @pl.when(n == 0)
        def _():
            dk_ref[...] = jnp.zeros_like(dk_ref)
            dv_ref[...] = jnp.zeros_like(dv_ref)
            ds_ref[...] = jnp.zeros_like(ds_ref)

        mask = _attn_mask(n)
        lower = _lower_lanes()
        lane = lax.broadcasted_iota(jnp.int32, (BLOCK, 128), 1)
        lower2 = lax.broadcasted_iota(jnp.int32, (2 * BLOCK, 128), 1) < HEAD_DIM
        lane1 = lax.broadcasted_iota(jnp.int32, (1, 128), 1)
        qv, ov, dov, lsev = q_ref[...], o_ref[...], do_ref[...], lse_ref[...]
        dk_fold, dv_fold = [], []
        dsink = jnp.zeros((1, 128), F32)
        for kh in range(2):
            qs = _stack_heads(qv, kh)
            dos = _stack_heads(dov, kh)
            os_ = _stack_heads(ov, kh)
            kd, vd = _dup_kv(kp_ref, kc_ref, kh), _dup_kv(vp_ref, vc_ref, kh)
            lse = jnp.concatenate(
                [jnp.sum(jnp.where(lane == kh * 4 + g, lsev, 0.0), axis=1, keepdims=True) for g in range(4)], axis=0)
            sc = lax.dot_general(qs, kd, NT, preferred_element_type=F32) * ATTN_SCALE
            p = jnp.exp(jnp.where(mask, sc, NEG) - lse)
            dp = lax.dot_general(dos, vd, NT, preferred_element_type=F32)
            delta = jnp.sum(dos.astype(F32) * os_.astype(F32), axis=1, keepdims=True)
            dsc = (p * (dp - delta) * ATTN_SCALE).astype(BF16)
            dqs = jnp.dot(dsc, kd, preferred_element_type=F32)
            for pair in range(2):
                lo = dqs[(2 * pair) * BLOCK:(2 * pair + 1) * BLOCK]
                hi = dqs[(2 * pair + 1) * BLOCK:(2 * pair + 2) * BLOCK]
                col = (kh * 2 + pair) * 128
                dq_ref[:, col:col + 128] = jnp.where(lower, lo, hi).astype(BF16)
            dkd = lax.dot_general(dsc, qs, TN, preferred_element_type=F32)
            dvd = lax.dot_general(p.astype(BF16), dos, TN, preferred_element_type=F32)
            dk_fold.append(dkd + pltpu.roll(dkd, HEAD_DIM, axis=1))
            dv_fold.append(dvd + pltpu.roll(dvd, HEAD_DIM, axis=1))
            psink = jnp.exp(_sink_col(sinks_ref, kh) - lse) * delta
            for g in range(4):
                tot = jnp.sum(psink[g * BLOCK:(g + 1) * BLOCK], axis=0, keepdims=True)
                dsink = dsink - jnp.where(lane1 == kh * 4 + g, tot, 0.0)
        dk2 = jnp.where(lower2, dk_fold[0], dk_fold[1])
        dv2 = jnp.where(lower2, dv_fold[0], dv_fold[1])
        ds_ref[...] += dsink
        cur = pl.ds(pl.multiple_of(n * BLOCK, BLOCK), BLOCK)
        dk_ref[cur, :] += dk2[BLOCK:]
        dv_ref[cur, :] += dv2[BLOCK:]

        @pl.when(n > 0)
        def _():
            prev = pl.ds(pl.multiple_of((n - 1) * BLOCK, BLOCK), BLOCK)
            dk_ref[prev, :] += dk2[:BLOCK]
            dv_ref[prev, :] += dv2[:BLOCK]

    blk = lambda w: pl.BlockSpec((BLOCK, w), lambda n: (n, 0))
    return _pcall(
        body, "attn_bwd", (s // BLOCK,),
        [pl.BlockSpec(memory_space=pltpu.SMEM)] + _attn_specs() + [blk(ATTN_W), blk(128), blk(ATTN_W)],
        [blk(ATTN_W), _full((s, KV_W)), _full((s, KV_W)), _full((1, 128))],
        [_sds((s, ATTN_W), BF16), _sds((s, KV_W), F32), _sds((s, KV_W), F32), _sds((1, 128), F32)],
        (sinks, qkv, qkv, qkv, qkv, qkv, attn, lse, dattn), comm=comm)


DPROJ_PIECES = (ATTN_W, KV_W, KV_W, CONV_W, CONV_W, CONV_W, GATE_W)
DPROJ_OFFSETS = tuple(sum(DPROJ_PIECES[:k]) for k in range(len(DPROJ_PIECES)))


def _grad_w_in(pieces, xn, comm):
    s = xn.shape[0]
    ts = min(1024, s)
    steps = s // ts
    rows0 = DPROJ_OFFSETS[6]

    def body(*refs):
        p_refs, b_ref, o_ref, acc_ref, stage_ref, sem = refs[:7], refs[7], refs[8], refs[9], refs[10], refs[11]
        i, j = pl.program_id(0), pl.program_id(1)

        @pl.when(j == 0)
        def _():
            acc_ref[...] = jnp.zeros_like(acc_ref)

        bv = b_ref[...]

        def flush(lo, n):
            stage_ref[0:n, :] = acc_ref[0:n, :].astype(BF16)
            cp = pltpu.make_async_copy(stage_ref.at[0:n, :], o_ref.at[lo:lo + n, :], sem)
            cp.start()
            cp.wait()

        @pl.when(i == 0)
        def _():
            for p_ref, off, w in zip(p_refs[:6], DPROJ_OFFSETS[:6], DPROJ_PIECES[:6]):
                acc_ref[off:off + w, :] += lax.dot_general(p_ref[...].astype(BF16), bv, TN,
                                                           preferred_element_type=F32)

            @pl.when(j == steps - 1)
            def _():
                flush(0, rows0)

        @pl.when(i == 1)
        def _():
            acc_ref[0:GATE_W, :] += lax.dot_general(p_refs[6][...], bv, TN, preferred_element_type=F32)

            @pl.when(j == steps - 1)
            def _():
                flush(rows0, GATE_W)

    def piece_spec(w, group):
        return pl.BlockSpec((ts, w), lambda i, j: (jnp.where(i == group, j, 0), 0))

    outs, couts = _pcall(
        body, "grad_w_in", (2, steps),
        [piece_spec(w, 0) for w in DPROJ_PIECES[:6]] + [piece_spec(GATE_W, 1),
                                                         pl.BlockSpec((ts, D_MODEL), lambda i, j: (j, 0))],
        [ANY], [_sds((IN_W, D_MODEL), BF16)], (*pieces, xn),
        scratch=[pltpu.VMEM((rows0, D_MODEL), F32), pltpu.VMEM((rows0, D_MODEL), BF16), pltpu.SemaphoreType.DMA],
        comm=comm)
    return outs[0], couts


def _inproj_bwd(pieces, win_t, x, g, dh1, comm):
    s = x.shape[0]
    tm = _row_tile(s, 512)

    def body(*refs):
        p_refs = refs[:7]
        w_ref, x_ref, g_ref, dh_ref, dx_ref, db_ref, dg_ref = refs[7:]

        @pl.when(pl.program_id(0) == 0)
        def _():
            db_ref[...] = jnp.zeros_like(db_ref)
            dg_ref[...] = jnp.zeros_like(dg_ref)

        dxn = jnp.zeros((tm, D_MODEL), F32)
        for p_ref, off, w in zip(p_refs, DPROJ_OFFSETS, DPROJ_PIECES):
            v = p_ref[...].astype(BF16)
            db_ref[:, off:off + w] += jnp.sum(v.astype(F32), axis=0, keepdims=True)
            dxn = dxn + jnp.dot(v, w_ref[off:off + w, :], preferred_element_type=F32)
        dx, dg = _norm_bwd_tile(x_ref[...], g_ref[...], dxn)
        dg_ref[...] += dg
        dx_ref[...] = dh_ref[...] + dx

    return _pcall(
        body, "inproj_bwd", (s // tm,),
        [_rows(tm, w) for w in DPROJ_PIECES] + [_resident((IN_W, D_MODEL)), _rows(tm, D_MODEL), _full((1, D_MODEL)),
                                                _rows(tm, D_MODEL)],
        [_rows(tm, D_MODEL), _full((8, IN_W)), _full((8, D_MODEL))],
        [_sds((s, D_MODEL), F32), _sds((8, IN_W), F32), _sds((8, D_MODEL), F32)],
        (*pieces, win_t, x, g, dh1), comm=comm)


def _adam_math(w, g, m, v):
    m2 = ADAM_B1 * m + (1.0 - ADAM_B1) * g
    v2 = ADAM_B2 * v + (1.0 - ADAM_B2) * (g * g)
    m_hat = m2 / (1.0 - ADAM_B1 ** ADAM_STEP)
    v_hat = v2 / (1.0 - ADAM_B2 ** ADAM_STEP)
    delta = -ADAM_LR * (m_hat / (jnp.sqrt(v_hat) + ADAM_EPS) + ADAM_WD * w)
    return delta, m2, v2


def _sum_slots(ref):
    tot = ref[0].astype(F32)
    for i in range(1, ref.shape[0]):
        tot = tot + ref[i].astype(F32)
    return tot


def _pair_add(partials, theirs, tr, name):
    r = partials.shape[0] // N_DEV
    c = partials.shape[1]
    nt = r // tr
    core = lax.axis_index("c").astype(jnp.int32).reshape(1)

    def body(core_ref, a_ref, b_ref, o_ref):
        o_ref[...] = (a_ref[...].astype(F32) + b_ref[...].astype(F32)).astype(BF16)

    grid_spec = pltpu.PrefetchScalarGridSpec(
        num_scalar_prefetch=1, grid=(4 * nt,),
        in_specs=[pl.BlockSpec((None, None, tr, c), lambda i, core_ref: (i // nt, core_ref[0], i % nt, 0)),
                  pl.BlockSpec((tr, c), lambda i, core_ref: (i, 0))],
        out_specs=pl.BlockSpec((tr, c), lambda i, core_ref: (i, 0)))
    return pl.pallas_call(body, name=name, grid_spec=grid_spec, out_shape=_sds((4 * r, c), BF16))(
        core, partials.reshape(4, 2, r, c), theirs)


def _sum_adamw(parts, w, m, v, tr, name):
    r, c = w.shape

    def body(p_ref, w_ref, m_ref, v_ref, g_ref, d_ref, m2_ref, v2_ref):
        g = _sum_slots(p_ref)
        g_ref[...] = g
        d_ref[...], m2_ref[...], v2_ref[...] = _adam_math(w_ref[...], g, m_ref[...], v_ref[...])

    spec = pl.BlockSpec((tr, c), lambda i: (i, 0))
    return _pcall(body, name, (r // tr,), [pl.BlockSpec((N_DEV, tr, c), lambda i: (0, i, 0)), spec, spec, spec],
                  [spec] * 4, [_sds((r, c), F32)] * 4, (parts, w, m, v))[0]


def _sum_parts_adamw(parts, w, m, v, tr, name):
    c = w.shape[1]
    tiles = [p.shape[1] // tr for p in parts]
    starts = [sum(tiles[:k]) for k in range(len(parts))]
    n_parts = len(parts)

    def body(*refs):
        p_refs = refs[:n_parts]
        w_ref, m_ref, v_ref, g_ref, d_ref, m2_ref, v2_ref = refs[n_parts:]
        i = pl.program_id(0)
        for p_ref, st, nt in zip(p_refs, starts, tiles):
            @pl.when(jnp.logical_and(i >= st, i < st + nt))
            def _(p_ref=p_ref):
                g_ref[...] = _sum_slots(p_ref)

        d_ref[...], m2_ref[...], v2_ref[...] = _adam_math(w_ref[...], g_ref[...], m_ref[...], v_ref[...])

    def part_spec(p, st, nt):
        return pl.BlockSpec((p.shape[0], tr, c), lambda i: (0, jnp.clip(i - st, 0, nt - 1), 0))

    spec = pl.BlockSpec((tr, c), lambda i: (i, 0))
    return _pcall(
        body, name, (sum(tiles),),
        [part_spec(p, st, nt) for p, st, nt in zip(parts, starts, tiles)] + [spec, spec, spec],
        [spec] * 4, [_sds(w.shape, F32)] * 4, (*parts, w, m, v))[0]


ROW_MIX, ROW_FFN, ROW_FINAL, ROW_SINKS, ROW_LOSS, ROW_BIN, ROW_CW, ROW_FCW = 0, 1, 2, 3, 4, 5, 10, 13
FCW_ROWS = 6


def _wide_pieces(width):
    return [(k * D_MODEL, min(D_MODEL, width - k * D_MODEL)) for k in range(-(-width // D_MODEL))]


def _pack_small(dffn, dfn, dsink, loss, dcw, dfcw):
    def body(ffn_ref, fn_ref, sink_ref, loss_ref, cw_ref, fcw_ref, o_ref):
        o_ref[...] = jnp.zeros_like(o_ref)
        o_ref[ROW_FFN:ROW_FFN + 1, :] = ffn_ref[...]
        o_ref[ROW_FINAL:ROW_FINAL + 1, :] = fn_ref[...]
        o_ref[ROW_SINKS:ROW_SINKS + 1, 0:128] = sink_ref[...]
        o_ref[ROW_LOSS:ROW_LOSS + 1, 0:128] = loss_ref[...]
        o_ref[ROW_CW:ROW_CW + 3, 0:CONV_W] = cw_ref[...]
        for a in range(3):
            for k, (off, w) in enumerate(_wide_pieces(2 * D_FF)):
                row = ROW_FCW + FCW_ROWS * a + k
                o_ref[row:row + 1, 0:w] = fcw_ref[a:a + 1, off:off + w]

    return pl.pallas_call(body, name="pack_small", out_shape=_sds((SMALL_ROWS, D_MODEL), F32))(
        dffn, dfn, dsink, loss, dcw, dfcw)


def _small_sums_adamw(r_small, r_dmix, r_dbin, params):
    rows = (None, None, ROW_SINKS, ROW_FFN, ROW_FINAL)

    def sum_row0(ref):
        tot = ref[0:1, :]
        for i in range(1, N_DEV):
            tot = tot + ref[8 * i:8 * i + 1, :]
        return tot

    def body(*refs):
        r_ref, late_refs, p_refs, o_refs = refs[0], refs[1:3], refs[3:18], refs[18:]
        tot = _sum_slots(r_ref)
        for k, row in enumerate(rows):
            w_ref, m_ref, v_ref = p_refs[3 * k:3 * k + 3]
            g_ref, d_ref, m2_ref, v2_ref = o_refs[4 * k:4 * k + 4]
            if row is None:
                g_ref[...] = sum_row0(late_refs[k])
            else:
                for j, (off, w) in enumerate(_wide_pieces(w_ref.shape[1])):
                    g_ref[:, off:off + w] = tot[row + j:row + j + 1, 0:w]
            d_ref[...], m2_ref[...], v2_ref[...] = _adam_math(w_ref[...], g_ref[...], m_ref[...], v_ref[...])
        cw_ref, fcw_ref, loss_ref = o_refs[20:]
        cw_ref[...] = tot[ROW_CW:ROW_CW + 3, 0:CONV_W]
        for a in range(3):
            for j, (off, w) in enumerate(_wide_pieces(2 * D_FF)):
                row = ROW_FCW + FCW_ROWS * a + j
                fcw_ref[a:a + 1, off:off + w] = tot[row:row + 1, 0:w]
        loss_ref[...] = tot[ROW_LOSS:ROW_LOSS + 1, 0:128]

    flat = [t for p in params for t in p]
    out_shape = [_sds(p[0].shape, F32) for p in params for _ in range(4)]
    out_shape += [_sds((3, CONV_W), F32), _sds((3, 2 * D_FF), F32), _sds((1, 128), F32)]
    res = pl.pallas_call(body, name="small_sums_adamw", out_shape=out_shape)(r_small, r_dmix, r_dbin, *flat)
    return [tuple(res[4 * k:4 * k + 4]) for k in range(5)], res[20], res[21], res[22]


def _adamw_pair(a, b):
    def body(*refs):
        for k in range(2):
            w_ref, g_ref, m_ref, v_ref = refs[4 * k:4 * k + 4]
            d_ref, m2_ref, v2_ref = refs[8 + 3 * k:8 + 3 * k + 3]
            d_ref[...], m2_ref[...], v2_ref[...] = _adam_math(w_ref[...], g_ref[...], m_ref[...], v_ref[...])

    out_shape = [_sds(a[0].shape, F32)] * 3 + [_sds(b[0].shape, F32)] * 3
    res = pl.pallas_call(body, name="adamw_conv_weights", out_shape=out_shape)(*a, *b)
    return tuple(res[:3]), tuple(res[3:])


def _pad_cols(a, c):
    return jnp.pad(a, ((0, 0), (0, c - a.shape[1])))


def _to_col_slabs(g):
    r = g.shape[0]
    return jnp.transpose(g.reshape(r, N_DEV, 128), (1, 0, 2)).reshape(N_DEV * r, 128)


def _from_col_slabs(t):
    r = t.shape[0] // N_DEV
    return jnp.transpose(t.reshape(N_DEV, r, 128), (1, 0, 2)).reshape(r, N_DEV * 128)


def _slots(t):
    return t.reshape(N_DEV, t.shape[0] // N_DEV, t.shape[1])


def kernel(x, mix_norm, w_in, b_in, sinks, conv_w, w_attn_branch, w_conv_branch, w_out, ffn_norm, w_up, ffn_conv_w, w_down, final_norm, loss_target, m_mix_norm, m_w_in, m_b_in, m_sinks, m_conv_w, m_w_attn_branch, m_w_conv_branch, m_w_out, m_ffn_norm, m_w_up, m_ffn_conv_w, m_w_down, m_final_norm, v_mix_norm, v_w_in, v_b_in, v_sinks, v_conv_w, v_w_attn_branch, v_w_conv_branch, v_w_out, v_ffn_norm, v_w_up, v_ffn_conv_w, v_w_down, v_final_norm):
    xs, tgt = x[0], loss_target[0]
    me = 4 * lax.axis_index("x") + 2 * lax.axis_index("y") + lax.axis_index("c")
    in_rows, up_rows = IN_W // N_DEV, 2 * D_FF // N_DEV

    conv_sh = jnp.concatenate([_pad_cols(ffn_conv_w[0], 768), _pad_cols(conv_w[0], 768),
                               jnp.zeros((2, 768), F32)], axis=0)
    win_sh, wup_sh = w_in[0].T.astype(BF16), w_up[0].T.astype(BF16)
    wout_sh, wdown_sh = w_out[0].astype(BF16), w_down[0].astype(BF16)
    wa_sh, wc_sh = w_attn_branch[0].astype(BF16), w_conv_branch[0].astype(BF16)

    half = D_MODEL // 2
    (win_t,) = _exchange_only(_AllGather([win_sh]), "gather_w_in")
    (xn, qkv, cbx, gates), (wa_s, wc_s, wout, conv_g) = _norm_inproj(
        xs, mix_norm, win_t, b_in, _AllGather([wa_sh, wc_sh, wout_sh, conv_sh]))
    (attn, lse), (wup_lo,) = _attn_fwd(qkv, sinks, _AllGather([wup_sh[:, :half]]))
    wa, wc = _from_col_slabs(wa_s), _from_col_slabs(wc_s)
    conv_g = conv_g.reshape(N_DEV, 8, 768)
    fcw = jnp.transpose(conv_g[:, 0:3, :up_rows], (1, 0, 2)).reshape(3, 2 * D_FF)
    cw = jnp.transpose(conv_g[:, 3:6, :CONV_W // N_DEV], (1, 0, 2)).reshape(3, CONV_W)
    (h1,), (wup_hi,) = _mix_fwd(xs, cbx, gates, attn, cw, wa, wc, wout, _AllGather([wup_sh[:, half:]]))
    (hn, up_pre, up), (wdown,) = _ffn_up(h1, ffn_norm, wup_lo, wup_hi, fcw, _AllGather([wdown_sh]))
    act, dh2, loss_p, dfn_p = _ffn_down_loss(up, wdown, h1, final_norm.reshape(1, D_MODEL), tgt)

    dn_rows, q_up = D_FF // N_DEV, up_rows // 4
    g_wdown = _matmul_tn(act, dh2, FF_CHUNK, "grad_w_down")
    (dup_pre, dfcw_p, dh1, dffn_p), (r_wdown,) = _ffn_bwd(dh2, wdown, up, up_pre, fcw, wup_lo, wup_hi, h1, ffn_norm,
                                                         _ReduceScatter([(g_wdown, 0, dn_rows)]))
    g_wup_t = _matmul_tn(dup_pre, hn, FF_CHUNK, "grad_w_up")
    (dgates, dattn, dcb, dcc, dcx, dcw_p, g_wout, g_wa_nat, g_wc_nat), (r_wup_ab,) = _mix_bwd(
        dh1, wout, gates, attn, wa, wc, cbx, cw, _ReduceScatter([(g_wup_t, 0, 2 * q_up)]))
    g_wa, g_wc = _to_col_slabs(g_wa_nat), _to_col_slabs(g_wc_nat)
    (dq, dk, dv, dsink_p), (r_wup_c, r_wout, r_wa, r_wc) = _attn_bwd(
        qkv, sinks, attn, lse, dattn,
        _ReduceScatter([(g_wup_t, 2 * q_up, q_up), (g_wout, 0, D_MODEL // N_DEV), (g_wa, 0, ATTN_W),
                        (g_wc, 0, CONV_W)]))
    dproj = (dq, dk, dv, dcb, dcc, dcx, dgates)
    small = _pack_small(dffn_p, dfn_p, dsink_p, loss_p, dcw_p, dfcw_p)
    g_win_t, (r_wup_d, r_small) = _grad_w_in(dproj, xn, _ReduceScatter([(g_wup_t, 3 * q_up, q_up)], [small]))
    (win_theirs,) = _exchange_only(_PairExchange([g_win_t]), "pair_exchange_w_in")
    q_win = _pair_add(g_win_t, win_theirs, in_rows // 2, "pair_add_w_in")
    (dx, _, _), (r_win, r_dbin, r_dmix) = _inproj_bwd(
        dproj, win_t, xs, mix_norm, dh1,
        _ChipExchangeThenBroadcast([q_win], late_from=(1, 2), late_shapes=[(8, IN_W), (8, D_MODEL)]))

    fn2, m_fn2, v_fn2 = (t.reshape(1, D_MODEL) for t in (final_norm, m_final_norm, v_final_norm))
    small_res, g_cw_full, g_fcw_full, loss_row = _small_sums_adamw(
        _slots(r_small), r_dmix, r_dbin,
        [(mix_norm, m_mix_norm, v_mix_norm), (b_in, m_b_in, v_b_in), (sinks, m_sinks, v_sinks),
         (ffn_norm, m_ffn_norm, v_ffn_norm), (fn2, m_fn2, v_fn2)])
    loss = loss_row[0, 0]
    g_cw = lax.dynamic_slice_in_dim(g_cw_full, me * (CONV_W // N_DEV), CONV_W // N_DEV, axis=1)
    g_fcw = lax.dynamic_slice_in_dim(g_fcw_full, me * up_rows, up_rows, axis=1)
    taps = lambda t: jnp.transpose(t, (1, 0, 2))
    g_cw, g_fcw = g_cw[:, None, :], g_fcw[:, None, :]
    cw_res, fcw_res = _adamw_pair((taps(conv_w), g_cw, taps(m_conv_w), taps(v_conv_w)),
                                  (taps(ffn_conv_w), g_fcw, taps(m_ffn_conv_w), taps(v_ffn_conv_w)))

    big = {}
    big["w_in"] = tuple(t.T for t in _sum_parts_adamw(
        [r_win.reshape(4, in_rows, D_MODEL)], w_in[0].T, m_w_in[0].T, v_w_in[0].T, in_rows // 2, "adamw_w_in"))
    big["w_up"] = tuple(t.T for t in _sum_parts_adamw(
        [_slots(r_wup_ab), _slots(r_wup_c), _slots(r_wup_d)], w_up[0].T, m_w_up[0].T, v_w_up[0].T, q_up,
        "adamw_w_up"))
    big["w_out"] = _sum_adamw(_slots(r_wout), w_out[0], m_w_out[0], v_w_out[0], 128, "adamw_w_out")
    big["w_down"] = _sum_adamw(_slots(r_wdown), w_down[0], m_w_down[0], v_w_down[0], dn_rows // 2, "adamw_w_down")
    big["w_attn_branch"] = _sum_adamw(_slots(r_wa), w_attn_branch[0], m_w_attn_branch[0], v_w_attn_branch[0], 256,
                                      "adamw_w_attn_branch")
    big["w_conv_branch"] = _sum_adamw(_slots(r_wc), w_conv_branch[0], m_w_conv_branch[0], v_w_conv_branch[0], 256,
                                      "adamw_w_conv_branch")

    res = dict(zip(("mix_norm", "b_in", "sinks", "ffn_norm"), small_res[:4]))
    res["final_norm"] = tuple(t.reshape(final_norm.shape) for t in small_res[4])
    res["conv_w"] = tuple(jnp.transpose(t, (1, 0, 2)) for t in (g_cw,) + cw_res)
    res["ffn_conv_w"] = tuple(jnp.transpose(t, (1, 0, 2)) for t in (g_fcw,) + fcw_res)
    for name, ref_w in (("w_in", w_in), ("w_up", w_up), ("w_out", w_out), ("w_down", w_down),
                        ("w_attn_branch", w_attn_branch), ("w_conv_branch", w_conv_branch)):
        res[name] = tuple(t.reshape(ref_w.shape) for t in big[name])

    order = ["mix_norm", "w_in", "b_in", "sinks", "conv_w", "w_attn_branch", "w_conv_branch", "w_out",
             "ffn_norm", "w_up", "ffn_conv_w", "w_down", "final_norm"]
    out = [loss, dx.reshape(x.shape)]
    for k in range(4):
        out += [res[name][k] for name in order]
    return tuple(out)
```

```python
import math

import jax
import jax.numpy as jnp
from jax import lax
from jax.experimental import pallas as pl
from jax.experimental.pallas import tpu as pltpu

F32 = jnp.float32
BF16 = jnp.bfloat16
MESH = pl.DeviceIdType.MESH
N_DEV = 8

D_MODEL = 1024
HEAD_DIM = 64
N_HEADS = 8
BLOCK = 128
ATTN_W = 512
KV_W = 128
CONV_W = 512
QKV_W = ATTN_W + 2 * KV_W
CBX_W = 3 * CONV_W
GATE_W = 2 * D_MODEL
IN_W = QKV_W + CBX_W + GATE_W
D_FF = 2816
FF_CHUNK = 1408
NORM_EPS = 1e-5
ATTN_SCALE = HEAD_DIM ** -0.5
NEG = -1e30
HALO = 16

ADAM_LR = 0.001
ADAM_B1 = 0.9
ADAM_B2 = 0.999
ADAM_EPS = 1e-08
ADAM_WD = 0.01
ADAM_STEP = 10

VMEM_LIMIT = 56 * 1024 * 1024
SMALL_ROWS = 32

NT = (((1,), (1,)), ((), ()))
TN = (((0,), (0,)), ((), ()))
ANY = pl.BlockSpec(memory_space=pl.ANY)


def _sig(v):
    return 1.0 / (1.0 + jnp.exp(-v))


def _row_tile(s, pref=256):
    return pref if s % pref == 0 else s


def _shifts_down(u, halo, ks):
    ext = jnp.concatenate([halo, u], axis=0)
    return [pltpu.roll(ext, k, axis=0)[HALO:, :] for k in ks]


def _shifts_up(u, halo, ks):
    n = u.shape[0]
    ext = jnp.concatenate([u, halo], axis=0)
    return [pltpu.roll(ext, n + HALO - k, axis=0)[:n, :] for k in ks]


def _rows_reversed(tm, c, steps):
    return pl.BlockSpec((tm, c), lambda i: (steps - 1 - i, 0))


def _prev_halo_map_reversed(tm, steps):
    return lambda i: (jnp.maximum((steps - 1 - i) * (tm // HALO) - 1, 0), 0)


def _prev_halo_map(tm):
    return lambda i: (jnp.maximum(i * (tm // HALO) - 1, 0), 0)


def _full(shape):
    return pl.BlockSpec(shape, lambda *_: (0,) * len(shape))


def _resident(shape):
    return pl.BlockSpec(shape, lambda *_: (0,) * len(shape), pipeline_mode=pl.Buffered(1))


def _rows(tm, c):
    return pl.BlockSpec((tm, c), lambda i: (i, 0))


def _sds(shape, dtype):
    return jax.ShapeDtypeStruct(shape, dtype)


def _my_place():
    x, y, c = lax.axis_index("x"), lax.axis_index("y"), lax.axis_index("c")
    return x, y, c


ALL_PEERS = tuple((j >> 2, (j >> 1) & 1, j & 1) for j in range(1, N_DEV))
SIBLING_PEER = ((0, 0, 1),)
CHIP_PEERS = ((0, 1, 0), (1, 0, 0), (1, 1, 0))
BARRIER_ID = {ALL_PEERS: 0, SIBLING_PEER: 1, CHIP_PEERS: 2}


def _entry_barrier(peers):
    x, y, c = _my_place()
    barrier = pltpu.get_barrier_semaphore()
    for dx, dy, dc in peers:
        pl.semaphore_signal(barrier, inc=1, device_id=(x ^ dx, y ^ dy, c ^ dc), device_id_type=MESH)
    pl.semaphore_wait(barrier, len(peers))


def _start_exchange(remote, local):
    for cp in local + remote:
        cp.start()


def _finish_exchange(remote, local):
    for cp in remote:
        cp.wait_recv()
    for cp in remote:
        cp.wait_send()
    for cp in local:
        cp.wait()


class _AllGather:
    peers = ALL_PEERS

    def __init__(self, shards):
        self.ins = list(shards)
        n = len(shards)
        self.out_shape = [_sds((N_DEV * s.shape[0], s.shape[1]), s.dtype) for s in shards]
        self.sems = [pltpu.SemaphoreType.DMA((7 * n,)), pltpu.SemaphoreType.DMA((7 * n,)),
                     pltpu.SemaphoreType.DMA((n,))]

    def _parts(self, ins, outs, sems):
        send_sems, recv_sems, local_sems = sems
        x, y, c = _my_place()
        me, sibling = (x, y, c), (x, y, 1 - c)
        chips = [(1 - x, y), (x, 1 - y), (1 - x, 1 - y)]

        def rows(k, dev):
            r = ins[k].shape[0]
            start = pl.multiple_of((4 * dev[0] + 2 * dev[1] + dev[2]) * r, 8)
            return outs[k].at[pl.ds(start, r), :]

        def copy(k, j, block, to, src=None):
            return pltpu.make_async_remote_copy(
                src_ref=rows(k, block) if src is None else src, dst_ref=rows(k, block),
                send_sem=send_sems.at[7 * k + j], recv_sem=recv_sems.at[7 * k + j],
                device_id=to, device_id_type=MESH)

        n = len(ins)
        mine = [pltpu.make_async_copy(ins[k], rows(k, me), local_sems.at[k]) for k in range(n)]
        first = []
        for k in range(n):
            first.append(copy(k, 0, me, sibling, src=ins[k]))
            first += [copy(k, 1 + j, me, (*chip, c), src=ins[k]) for j, chip in enumerate(chips)]
        return me, sibling, chips, copy, mine, first

    def start(self, ins, outs, sems):
        _, _, _, _, mine, first = self._parts(ins, outs, sems)
        _start_exchange(first, mine)

    def finish(self, ins, outs, sems):
        me, sibling, chips, copy, mine, first = self._parts(ins, outs, sems)
        c = me[2]
        n = len(ins)
        passed = []
        for j, chip in enumerate(chips):
            for k in range(n):
                copy(k, 1 + j, (*chip, c), me).wait_recv()
                fwd = copy(k, 4 + j, (*chip, c), sibling)
                fwd.start()
                passed.append(fwd)
        for k in range(n):
            copy(k, 0, sibling, me).wait_recv()
            for j, chip in enumerate(chips):
                copy(k, 4 + j, (*chip, 1 - c), me).wait_recv()
        for cp in first + passed:
            cp.wait_send()
        for cp in mine:
            cp.wait()


class _ReduceScatter:
    peers = ALL_PEERS

    def __init__(self, parts, bcast=()):
        self.parts = [(lo, cnt) for _, lo, cnt in parts]
        self.n_parts = len(parts)
        self.ins = [a for a, _, _ in parts] + list(bcast)
        self.out_shape = [_sds((N_DEV * cnt, a.shape[1]), a.dtype) for a, _, cnt in parts]
        self.out_shape += [_sds((N_DEV * b.shape[0], b.shape[1]), b.dtype) for b in bcast]
        n = len(self.ins)
        self.sems = [pltpu.SemaphoreType.DMA((7 * n,)), pltpu.SemaphoreType.DMA((7 * n,)),
                     pltpu.SemaphoreType.DMA((n,))]

    def _copies(self, ins, outs, sems):
        send_sems, recv_sems, local_sems = sems
        x, y, c = _my_place()
        me_idx = 4 * x + 2 * y + c
        remote, local = [], []
        for k in range(len(ins)):
            cnt = outs[k].shape[0] // N_DEV
            dst = outs[k].at[pl.ds(pl.multiple_of(me_idx * cnt, 8), cnt), :]
            if k < self.n_parts:
                lo, _ = self.parts[k]
                r = ins[k].shape[0] // N_DEV
                src_of = lambda idx: ins[k].at[pl.ds(pl.multiple_of(idx * r + lo, 8), cnt), :]
            else:
                src_of = lambda idx: ins[k]
            local.append(pltpu.make_async_copy(src_of(me_idx), dst, local_sems.at[k]))
            for j in range(1, N_DEV):
                peer = (x ^ (j >> 2), y ^ ((j >> 1) & 1), c ^ (j & 1))
                peer_idx = 4 * peer[0] + 2 * peer[1] + peer[2]
                remote.append(pltpu.make_async_remote_copy(
                    src_ref=src_of(peer_idx), dst_ref=dst,
                    send_sem=send_sems.at[7 * k + j - 1], recv_sem=recv_sems.at[7 * k + j - 1],
                    device_id=peer, device_id_type=MESH))
        return remote, local

    def start(self, ins, outs, sems):
        _start_exchange(*self._copies(ins, outs, sems))

    def finish(self, ins, outs, sems):
        _finish_exchange(*self._copies(ins, outs, sems))


class _PairExchange:
    peers = SIBLING_PEER

    def __init__(self, arrays):
        self.ins = list(arrays)
        n = len(arrays)
        self.out_shape = [_sds((a.shape[0] // 2, a.shape[1]), a.dtype) for a in arrays]
        self.sems = [pltpu.SemaphoreType.DMA((4 * n,)), pltpu.SemaphoreType.DMA((4 * n,))]

    def _copies(self, ins, outs, sems):
        send_sems, recv_sems = sems
        x, y, c = _my_place()
        remote = []
        for k in range(len(ins)):
            r = ins[k].shape[0] // N_DEV
            for chip in range(4):
                sib = ins[k].at[pl.ds(pl.multiple_of((2 * chip + 1 - c) * r, 8), r), :]
                remote.append(pltpu.make_async_remote_copy(
                    src_ref=sib, dst_ref=outs[k].at[pl.ds(chip * r, r), :],
                    send_sem=send_sems.at[4 * k + chip], recv_sem=recv_sems.at[4 * k + chip],
                    device_id=(x, y, 1 - c), device_id_type=MESH))
        return remote

    def start(self, ins, outs, sems):
        for cp in self._copies(ins, outs, sems):
            cp.start()

    def finish(self, ins, outs, sems):
        remote = self._copies(ins, outs, sems)
        for cp in remote:
            cp.wait_recv()
        for cp in remote:
            cp.wait_send()


class _ChipExchange:
    peers = CHIP_PEERS

    def __init__(self, arrays):
        self.ins = list(arrays)
        self.out_shape = [_sds(a.shape, a.dtype) for a in arrays]
        n = len(self.ins)
        self.sems = [pltpu.SemaphoreType.DMA((3 * n,)), pltpu.SemaphoreType.DMA((3 * n,)),
                     pltpu.SemaphoreType.DMA((n,))]

    def _copies(self, ins, outs, sems):
        send_sems, recv_sems, local_sems = sems
        x, y, c = _my_place()
        my_chip = 2 * x + y
        remote, local = [], []
        for k in range(len(ins)):
            r = ins[k].shape[0] // 4
            dst = outs[k].at[pl.ds(pl.multiple_of(my_chip * r, 8), r), :]
            local.append(pltpu.make_async_copy(ins[k].at[pl.ds(pl.multiple_of(my_chip * r, 8), r), :], dst,
                                               local_sems.at[k]))
            for j in range(1, 4):
                px, py = x ^ (j >> 1), y ^ (j & 1)
                src = ins[k].at[pl.ds(pl.multiple_of((2 * px + py) * r, 8), r), :]
                remote.append(pltpu.make_async_remote_copy(
                    src_ref=src, dst_ref=dst, send_sem=send_sems.at[3 * k + j - 1],
                    recv_sem=recv_sems.at[3 * k + j - 1], device_id=(px, py, c), device_id_type=MESH))
        return remote, local

    def start(self, ins, outs, sems):
        _start_exchange(*self._copies(ins, outs, sems))

    def finish(self, ins, outs, sems):
        _finish_exchange(*self._copies(ins, outs, sems))


class _ChipExchangeThenBroadcast(_ChipExchange):
    peers = ALL_PEERS

    def __init__(self, arrays, late_from, late_shapes):
        super().__init__(arrays)
        self.n_chip = len(arrays)
        self.late_from = tuple(late_from)
        self.out_shape += [_sds((N_DEV * r, c), F32) for r, c in late_shapes]
        m = len(late_shapes)
        self.sems += [pltpu.SemaphoreType.DMA((7 * m,)), pltpu.SemaphoreType.DMA((7 * m,)),
                      pltpu.SemaphoreType.DMA((m,))]

    def _late_copies(self, srcs, outs, sems):
        send_sems, recv_sems, local_sems = sems
        x, y, c = _my_place()
        me_idx = 4 * x + 2 * y + c
        remote, local = [], []
        for k, src in enumerate(srcs):
            r = src.shape[0]
            dst = outs[k].at[pl.ds(pl.multiple_of(me_idx * r, 8), r), :]
            local.append(pltpu.make_async_copy(src, dst, local_sems.at[k]))
            for j, (dx, dy, dc) in enumerate(ALL_PEERS):
                remote.append(pltpu.make_async_remote_copy(
                    src_ref=src, dst_ref=dst, send_sem=send_sems.at[7 * k + j], recv_sem=recv_sems.at[7 * k + j],
                    device_id=(x ^ dx, y ^ dy, c ^ dc), device_id_type=MESH))
        return remote, local

    def start(self, ins, outs, sems):
        _start_exchange(*self._copies(ins, outs[:self.n_chip], sems[:3]))

    def finish(self, ins, outs, sems, late_srcs):
        late = self._late_copies(late_srcs, outs[self.n_chip:], sems[3:])
        _start_exchange(*late)
        _finish_exchange(*self._copies(ins, outs[:self.n_chip], sems[:3]))
        _finish_exchange(*late)


def _pcall(body, name, grid, in_specs, out_specs, out_shape, args, scratch=(), comm=None):
    params = pltpu.CompilerParams(dimension_semantics=("arbitrary",) * len(grid), vmem_limit_bytes=VMEM_LIMIT)
    in_specs, out_specs, out_shape, scratch = list(in_specs), list(out_specs), list(out_shape), list(scratch)
    if comm is None:
        res = pl.pallas_call(body, name=name, grid=grid, in_specs=in_specs, out_specs=out_specs, out_shape=out_shape,
                             scratch_shapes=scratch, compiler_params=params)(*args)
        return list(res), []
    n_in, n_out, n_scr = len(in_specs), len(out_specs), len(scratch)
    ci, co = len(comm.ins), len(comm.out_shape)
    total = math.prod(grid)

    def carried(*refs):
        bounds = [0, n_in, n_in + ci, n_in + ci + n_out, n_in + ci + n_out + co, n_in + ci + n_out + co + n_scr]
        ins, cins, outs, couts, scr = (refs[a:b] for a, b in zip(bounds[:-1], bounds[1:]))
        sems = refs[bounds[-1]:]
        step = pl.program_id(0)
        for d in range(1, len(grid)):
            step = step * grid[d] + pl.program_id(d)

        @pl.when(step == 0)
        def _():
            _entry_barrier(comm.peers)
            comm.start(cins, couts, sems)

        body(*ins, *outs, *scr)

        @pl.when(step == total - 1)
        def _():
            late_from = getattr(comm, "late_from", None)
            if late_from is None:
                comm.finish(cins, couts, sems)
            else:
                comm.finish(cins, couts, sems, [outs[k] for k in late_from])

    params = pltpu.CompilerParams(dimension_semantics=("arbitrary",) * len(grid), vmem_limit_bytes=VMEM_LIMIT,
                                  collective_id=BARRIER_ID[comm.peers])
    res = pl.pallas_call(
        carried, name=name, grid=grid, in_specs=in_specs + [ANY] * ci, out_specs=out_specs + [ANY] * co,
        out_shape=out_shape + comm.out_shape, scratch_shapes=scratch + comm.sems, compiler_params=params,
    )(*args, *comm.ins)
    return list(res[:n_out]), list(res[n_out:])


def _exchange_only(comm, name):
    def body(*refs):
        ci, co = len(comm.ins), len(comm.out_shape)
        _entry_barrier(comm.peers)
        comm.start(refs[:ci], refs[ci:ci + co], refs[ci + co:])
        comm.finish(refs[:ci], refs[ci:ci + co], refs[ci + co:])

    params = pltpu.CompilerParams(collective_id=BARRIER_ID[comm.peers])
    return pl.pallas_call(body, name=name, out_shape=comm.out_shape, in_specs=[ANY] * len(comm.ins),
                          out_specs=[ANY] * len(comm.out_shape), scratch_shapes=comm.sems,
                          compiler_params=params)(*comm.ins)


def _norm_inproj(x, g, win_t, b_in, comm):
    s = x.shape[0]
    tm = _row_tile(s, 512)
    widths = (QKV_W, CBX_W, GATE_W)

    def body(x_ref, g_ref, w_ref, b_ref, xn_ref, qkv_ref, cbx_ref, gate_ref):
        xv = x_ref[...]
        r = lax.rsqrt(jnp.mean(xv * xv, axis=-1, keepdims=True) + NORM_EPS)
        xn = (xv * r * g_ref[...]).astype(BF16)
        xn_ref[...] = xn
        off = 0
        for o_ref, w in zip((qkv_ref, cbx_ref, gate_ref), widths):
            acc = lax.dot_general(xn, w_ref[off:off + w, :], NT, preferred_element_type=F32)
            o_ref[...] = (acc + b_ref[:, off:off + w]).astype(BF16)
            off += w

    return _pcall(
        body, "norm_inproj", (s // tm,),
        [_rows(tm, D_MODEL), _full((1, D_MODEL)), _resident((IN_W, D_MODEL)), _full((1, IN_W))],
        [_rows(tm, D_MODEL)] + [_rows(tm, w) for w in widths],
        [_sds((s, D_MODEL), BF16)] + [_sds((s, w), BF16) for w in widths],
        (x, g, win_t, b_in), comm=comm)


def _attn_specs():
    prev = lambda n: jnp.maximum(n - 1, 0)
    return [pl.BlockSpec((BLOCK, ATTN_W), lambda n: (n, 0)),
            pl.BlockSpec((BLOCK, KV_W), lambda n: (prev(n), ATTN_W // KV_W)),
            pl.BlockSpec((BLOCK, KV_W), lambda n: (n, ATTN_W // KV_W)),
            pl.BlockSpec((BLOCK, KV_W), lambda n: (prev(n), ATTN_W // KV_W + 1)),
            pl.BlockSpec((BLOCK, KV_W), lambda n: (n, ATTN_W // KV_W + 1))]


def _lower_lanes():
    return lax.broadcasted_iota(jnp.int32, (BLOCK, 128), 1) < HEAD_DIM


def _stack_heads(val, kh):
    lower = _lower_lanes()
    parts = []
    for g in range(4):
        h = kh * 4 + g
        blk = val[:, (h // 2) * 128:(h // 2 + 1) * 128]
        keep = lower if h % 2 == 0 else jnp.logical_not(lower)
        parts.append(jnp.where(keep, blk, jnp.zeros_like(blk)))
    return jnp.concatenate(parts, axis=0)


def _dup_kv(prev_ref, cur_ref, kh):
    t = jnp.concatenate([prev_ref[...], cur_ref[...]], axis=0).astype(F32)
    rolled = pltpu.roll(t, HEAD_DIM, axis=1)
    lower = lax.broadcasted_iota(jnp.int32, t.shape, 1) < HEAD_DIM
    dup = jnp.where(lower, t, rolled) if kh == 0 else jnp.where(lower, rolled, t)
    return dup.astype(BF16)


def _attn_mask(n):
    row = lax.broadcasted_iota(jnp.int32, (4 * BLOCK, 2 * BLOCK), 0)
    kj = lax.broadcasted_iota(jnp.int32, (4 * BLOCK, 2 * BLOCK), 1)
    dist = (row & (BLOCK - 1)) + BLOCK - kj
    band = jnp.logical_and(dist >= 0, dist < BLOCK)
    return jnp.logical_and(band, jnp.logical_or(kj >= BLOCK, n > 0))


def _sink_col(sinks_ref, kh):
    gi = lax.broadcasted_iota(jnp.int32, (4 * BLOCK, 1), 0) // BLOCK
    col = jnp.zeros((4 * BLOCK, 1), F32)
    for g in range(4):
        col = jnp.where(gi == g, sinks_ref[0, kh * 4 + g], col)
    return col


def _attn_fwd(qkv, sinks, comm):
    s = qkv.shape[0]

    def body(sinks_ref, q_ref, kp_ref, kc_ref, vp_ref, vc_ref, o_ref, lse_ref):
        n = pl.program_id(0)
        mask = _attn_mask(n)
        lower = _lower_lanes()
        lane = lax.broadcasted_iota(jnp.int32, (BLOCK, 128), 1)
        qv = q_ref[...]
        lse_out = jnp.zeros((BLOCK, 128), F32)
        for kh in range(2):
            qs = _stack_heads(qv, kh)
            kd, vd = _dup_kv(kp_ref, kc_ref, kh), _dup_kv(vp_ref, vc_ref, kh)
            sc = lax.dot_general(qs, kd, NT, preferred_element_type=F32) * ATTN_SCALE
            sc = jnp.where(mask, sc, NEG)
            sink = _sink_col(sinks_ref, kh)
            m = jnp.maximum(jnp.max(sc, axis=1, keepdims=True), sink)
            p = jnp.exp(sc - m)
            l = jnp.sum(p, axis=1, keepdims=True) + jnp.exp(sink - m)
            o = jnp.dot(p.astype(BF16), vd, preferred_element_type=F32) / l
            lse = m + jnp.log(l)
            for pair in range(2):
                lo = o[(2 * pair) * BLOCK:(2 * pair + 1) * BLOCK]
                hi = o[(2 * pair + 1) * BLOCK:(2 * pair + 2) * BLOCK]
                col = (kh * 2 + pair) * 128
                o_ref[:, col:col + 128] = jnp.where(lower, lo, hi).astype(BF16)
            for g in range(4):
                lse_out = jnp.where(lane == kh * 4 + g, lse[g * BLOCK:(g + 1) * BLOCK], lse_out)
        lse_ref[...] = lse_out

    return _pcall(
        body, "attn_fwd", (s // BLOCK,),
        [pl.BlockSpec(memory_space=pltpu.SMEM)] + _attn_specs(),
        [pl.BlockSpec((BLOCK, ATTN_W), lambda n: (n, 0)), pl.BlockSpec((BLOCK, 128), lambda n: (n, 0))],
        [_sds((s, ATTN_W), BF16), _sds((s, 128), F32)],
        (sinks, qkv, qkv, qkv, qkv, qkv), comm=comm)


def _conv_u(cbx_ref, halo_ref, w_ref, first):
    cb = cbx_ref[:, 0:CONV_W].astype(F32)
    cc = cbx_ref[:, CONV_W:2 * CONV_W].astype(F32)
    cx = cbx_ref[:, 2 * CONV_W:3 * CONV_W].astype(F32)
    u = cc * cx
    uh = halo_ref[:, CONV_W:2 * CONV_W].astype(F32) * halo_ref[:, 2 * CONV_W:3 * CONV_W].astype(F32)
    uh = jnp.where(first, 0.0, uh)
    u1, u2 = _shifts_down(u, uh, (1, 2))
    cv = w_ref[0:1, :] * u2 + w_ref[1:2, :] * u1 + w_ref[2:3, :] * u
    return cb, cc, cx, u, cv


def _mix_fwd(x, cbx, gates, attn, conv_w, wa, wc, wout, comm):
    s = x.shape[0]
    tm = _row_tile(s)

    def body(x_ref, cbx_ref, halo_ref, gate_ref, attn_ref, cw_ref, wa_ref, wc_ref, wo_ref,
             h1_ref):
        first = pl.program_id(0) == 0
        cb, _, _, _, cv = _conv_u(cbx_ref, halo_ref, cw_ref, first)
        conv = (cb * cv).astype(BF16)
        ap = jnp.dot(attn_ref[...], wa_ref[...], preferred_element_type=F32)
        cp = jnp.dot(conv, wc_ref[...], preferred_element_type=F32)
        ga = gate_ref[:, 0:D_MODEL].astype(F32)
        gc = gate_ref[:, D_MODEL:2 * D_MODEL].astype(F32)
        merged = (_sig(ga) * ap + _sig(gc) * cp).astype(BF16)
        h1_ref[...] = x_ref[...] + jnp.dot(merged, wo_ref[...], preferred_element_type=F32)

    return _pcall(
        body, "mix_fwd", (s // tm,),
        [_rows(tm, D_MODEL), _rows(tm, CBX_W), pl.BlockSpec((HALO, CBX_W), _prev_halo_map(tm)),
         _rows(tm, GATE_W), _rows(tm, ATTN_W), _full((3, CONV_W)), _full((ATTN_W, D_MODEL)),
         _full((CONV_W, D_MODEL)), _full((D_MODEL, D_MODEL))],
        [_rows(tm, D_MODEL)], [_sds((s, D_MODEL), F32)],
        (x, cbx, cbx, gates, attn, conv_w, wa, wc, wout), comm=comm)


def _ffn_up(h1, g, wup_lo, wup_hi, fcw, comm):
    s = h1.shape[0]
    tm = _row_tile(s)
    half = D_MODEL // 2

    def body(h_ref, g_ref, wl_ref, wh_ref, fcw_ref, hn_ref, pre_ref, up_ref, carry_ref):
        @pl.when(pl.program_id(0) == 0)
        def _():
            carry_ref[...] = jnp.zeros_like(carry_ref)

        hv = h_ref[...]
        r = lax.rsqrt(jnp.mean(hv * hv, axis=-1, keepdims=True) + NORM_EPS)
        hn = (hv * r * g_ref[...]).astype(BF16)
        hn_ref[...] = hn
        for c in range(2 * D_FF // FF_CHUNK):
            sl = slice(c * FF_CHUNK, (c + 1) * FF_CHUNK)
            acc = lax.dot_general(hn[:, :half], wl_ref[sl, :], NT, preferred_element_type=F32)
            acc = acc + lax.dot_general(hn[:, half:], wh_ref[sl, :], NT, preferred_element_type=F32)
            pre_ref[:, sl] = acc.astype(BF16)
            halo = carry_ref[:, sl]
            carry_ref[:, sl] = acc[tm - HALO:, :]
            u1, u2 = _shifts_down(acc, halo, (1, 2))
            w = fcw_ref[:, sl]
            up_ref[:, sl] = (w[0:1] * u2 + w[1:2] * u1 + w[2:3] * acc).astype(BF16)

    return _pcall(
        body, "ffn_up", (s // tm,),
        [_rows(tm, D_MODEL), _full((1, D_MODEL)), _resident((2 * D_FF, half)), _resident((2 * D_FF, half)),
         _full((3, 2 * D_FF))],
        [_rows(tm, D_MODEL), _rows(tm, 2 * D_FF), _rows(tm, 2 * D_FF)],
        [_sds((s, D_MODEL), BF16), _sds((s, 2 * D_FF), BF16), _sds((s, 2 * D_FF), BF16)],
        (h1, g, wup_lo, wup_hi, fcw), scratch=[pltpu.VMEM((HALO, 2 * D_FF), F32)], comm=comm)


def _ffn_down_loss(up, wdown, h1, fnorm, target):
    s = h1.shape[0]
    tm = _row_tile(s)

    def body(up_ref, wd_ref, h1_ref, fn_ref, t_ref, act_ref, dh2_ref, loss_ref, dfn_ref):
        i = pl.program_id(0)

        @pl.when(i == 0)
        def _():
            loss_ref[...] = jnp.zeros_like(loss_ref)
            dfn_ref[...] = jnp.zeros_like(dfn_ref)

        h2 = h1_ref[...]
        for c in range(D_FF // FF_CHUNK):
            gsl = slice(c * FF_CHUNK, (c + 1) * FF_CHUNK)
            vsl = slice(D_FF + c * FF_CHUNK, D_FF + (c + 1) * FF_CHUNK)
            gate = up_ref[:, gsl].astype(F32)
            val = up_ref[:, vsl].astype(F32)
            act = (gate * _sig(gate) * val).astype(BF16)
            act_ref[:, gsl] = act
            h2 = h2 + jnp.dot(act, wd_ref[gsl, :], preferred_element_type=F32)
        r = lax.rsqrt(jnp.mean(h2 * h2, axis=-1, keepdims=True) + NORM_EPS)
        yhat = h2 * r
        fn = fn_ref[...]
        diff = yhat * fn - t_ref[...]
        loss_ref[...] += 0.5 * jnp.sum(jnp.sum(diff * diff, axis=1, keepdims=True), axis=0, keepdims=True) / D_MODEL
        dy = diff * (1.0 / D_MODEL)
        dfn_ref[...] += jnp.sum(dy * yhat, axis=0, keepdims=True)
        dyh = dy * fn
        dh2_ref[...] = r * (dyh - yhat * jnp.mean(dyh * yhat, axis=-1, keepdims=True))

    return _pcall(
        body, "ffn_down_loss", (s // tm,),
        [_rows(tm, 2 * D_FF), _resident((D_FF, D_MODEL)), _rows(tm, D_MODEL), _full((1, D_MODEL)),
         _rows(tm, D_MODEL)],
        [_rows(tm, D_FF), _rows(tm, D_MODEL), _full((1, 128)), _full((1, D_MODEL))],
        [_sds((s, D_FF), BF16), _sds((s, D_MODEL), F32), _sds((1, 128), F32), _sds((1, D_MODEL), F32)],
        (up, wdown, h1, fnorm, target))[0]


def _ffn_bwd(dh2, wdown, up, up_pre, fcw, wup_lo, wup_hi, h1, g, comm):
    s = dh2.shape[0]
    tm = _row_tile(s)
    half = D_MODEL // 2

    def dup_cols(dh, up_ref, wd_ref, c):
        gsl = slice(c * FF_CHUNK, (c + 1) * FF_CHUNK)
        vsl = slice(D_FF + c * FF_CHUNK, D_FF + (c + 1) * FF_CHUNK)
        dact = lax.dot_general(dh, wd_ref[gsl, :], NT, preferred_element_type=F32)
        gate = up_ref[:, gsl].astype(F32)
        val = up_ref[:, vsl].astype(F32)
        sg = _sig(gate)
        return dact * val * (sg * (1.0 + gate * (1.0 - sg))), dact * gate * sg

    def body(dh_ref, wd_ref, up_ref, x_ref, w_ref, wl_ref, wh_ref, h_ref, g_ref,
             dx_ref, dw_ref, dh1_ref, dg_ref, carry_ref):
        @pl.when(pl.program_id(0) == 0)
        def _():
            dw_ref[...] = jnp.zeros_like(dw_ref)
            dg_ref[...] = jnp.zeros_like(dg_ref)
            carry_ref[...] = jnp.zeros_like(carry_ref)

        dh2v = dh_ref[...]
        dh = dh2v.astype(BF16)
        dhn_lo = jnp.zeros((tm, half), F32)
        dhn_hi = jnp.zeros((tm, half), F32)
        for c in range(D_FF // FF_CHUNK):
            for d, off in zip(dup_cols(dh, up_ref, wd_ref, c), (c * FF_CHUNK, D_FF + c * FF_CHUNK)):
                sl = slice(off, off + FF_CHUNK)
                dn = carry_ref[:, sl]
                carry_ref[:, sl] = d[0:HALO, :]
                xv = x_ref[:, sl].astype(F32)
                wv = w_ref[:, sl]
                d1, d2 = _shifts_up(d, dn, (1, 2))
                dx = (wv[2:3] * d + wv[1:2] * d1 + wv[0:1] * d2).astype(BF16)
                dx_ref[:, sl] = dx
                dhn_lo = dhn_lo + jnp.dot(dx, wl_ref[sl, :], preferred_element_type=F32)
                dhn_hi = dhn_hi + jnp.dot(dx, wh_ref[sl, :], preferred_element_type=F32)
                dw_ref[0:1, sl] += jnp.sum(d2 * xv, axis=0, keepdims=True)
                dw_ref[1:2, sl] += jnp.sum(d1 * xv, axis=0, keepdims=True)
                dw_ref[2:3, sl] += jnp.sum(d * xv, axis=0, keepdims=True)
        dx1, dg = _norm_bwd_tile(h_ref[...], g_ref[...], jnp.concatenate([dhn_lo, dhn_hi], axis=1))
        dg_ref[...] += dg
        dh1_ref[...] = dh2v + dx1

    rows = lambda c: _rows_reversed(tm, c, s // tm)
    return _pcall(
        body, "ffn_bwd", (s // tm,),
        [rows(D_MODEL), _resident((D_FF, D_MODEL)), rows(2 * D_FF), rows(2 * D_FF), _full((3, 2 * D_FF)),
         _resident((2 * D_FF, half)), _resident((2 * D_FF, half)), rows(D_MODEL), _full((1, D_MODEL))],
        [rows(2 * D_FF), _full((3, 2 * D_FF)), rows(D_MODEL), _full((1, D_MODEL))],
        [_sds((s, 2 * D_FF), BF16), _sds((3, 2 * D_FF), F32), _sds((s, D_MODEL), F32), _sds((1, D_MODEL), F32)],
        (dh2, wdown, up, up_pre, fcw, wup_lo, wup_hi, h1, g),
        scratch=[pltpu.VMEM((HALO, 2 * D_FF), F32)], comm=comm)


def _matmul_tn(a, b, tk, name, ts=1024, comm=None):
    s, ka = a.shape
    n = b.shape[1]
    ts = min(ts, s)
    steps = s // ts

    def body(a_ref, b_ref, o_ref, acc_ref):
        j = pl.program_id(1)

        @pl.when(j == 0)
        def _():
            acc_ref[...] = jnp.zeros_like(acc_ref)

        acc_ref[...] += lax.dot_general(a_ref[...].astype(BF16), b_ref[...].astype(BF16), TN,
                                        preferred_element_type=F32)

        @pl.when(j == steps - 1)
        def _():
            o_ref[...] = acc_ref[...].astype(BF16)

    outs, couts = _pcall(
        body, name, (ka // tk, steps),
        [pl.BlockSpec((ts, tk), lambda i, j: (j, i)), pl.BlockSpec((ts, n), lambda i, j: (j, 0))],
        [pl.BlockSpec((tk, n), lambda i, j: (i, 0))], [_sds((ka, n), BF16)],
        (a, b), scratch=[pltpu.VMEM((tk, n), F32)], comm=comm)
    return outs[0] if comm is None else (outs[0], couts)


def _norm_bwd_tile(xv, g, dy):
    r = lax.rsqrt(jnp.mean(xv * xv, axis=-1, keepdims=True) + NORM_EPS)
    xhat = xv * r
    dg = jnp.sum(dy * xhat, axis=0, keepdims=True)
    dyh = dy * g
    return r * (dyh - xhat * jnp.mean(dyh * xhat, axis=-1, keepdims=True)), dg


def _mix_bwd(dh1, wout, gates, attn, wa, wc, cbx, conv_w, comm):
    s = dh1.shape[0]
    tm = _row_tile(s)
    steps = s // tm

    def body(dh_ref, wo_ref, gate_ref, attn_ref, wa_ref, wc_ref, cbx_ref, halo_ref,
             cw_ref, dg_ref, dattn_ref, dcb_ref, dcc_ref, dcx_ref, dw_ref, gwo_ref, gwa_ref, gwc_ref,
             acc_o, acc_a, acc_c, carry_ref):
        i = pl.program_id(0)

        @pl.when(i == 0)
        def _():
            dw_ref[...] = jnp.zeros_like(dw_ref)
            acc_o[...] = jnp.zeros_like(acc_o)
            acc_a[...] = jnp.zeros_like(acc_a)
            acc_c[...] = jnp.zeros_like(acc_c)
            carry_ref[...] = jnp.zeros_like(carry_ref)

        cb, cc, cx, u, cv = _conv_u(cbx_ref, halo_ref, cw_ref, i == steps - 1)
        attn = attn_ref[...]
        conv = (cb * cv).astype(BF16)
        ap = jnp.dot(attn, wa_ref[...], preferred_element_type=F32)
        cp = jnp.dot(conv, wc_ref[...], preferred_element_type=F32)
        dhb = dh_ref[...].astype(BF16)
        dm = lax.dot_general(dhb, wo_ref[...], NT, preferred_element_type=F32)
        sa = _sig(gate_ref[:, 0:D_MODEL].astype(F32))
        sc = _sig(gate_ref[:, D_MODEL:2 * D_MODEL].astype(F32))
        merged = (sa * ap + sc * cp).astype(BF16)
        da = (dm * sa).astype(BF16)
        dc = (dm * sc).astype(BF16)
        dg_ref[:, 0:D_MODEL] = (dm * ap * sa * (1.0 - sa)).astype(BF16)
        dg_ref[:, D_MODEL:2 * D_MODEL] = (dm * cp * sc * (1.0 - sc)).astype(BF16)
        dattn_ref[...] = lax.dot_general(da, wa_ref[...], NT, preferred_element_type=F32).astype(BF16)
        dconv = lax.dot_general(dc, wc_ref[...], NT, preferred_element_type=F32)
        dcb_ref[...] = (dconv * cv).astype(BF16)
        d = dconv * cb
        dn = carry_ref[...]
        carry_ref[...] = d[0:HALO, :]
        d1, d2 = _shifts_up(d, dn, (1, 2))
        du = cw_ref[2:3, :] * d + cw_ref[1:2, :] * d1 + cw_ref[0:1, :] * d2
        dcc_ref[...] = (du * cx).astype(BF16)
        dcx_ref[...] = (du * cc).astype(BF16)
        dw_ref[0:1, :] += jnp.sum(d2 * u, axis=0, keepdims=True)
        dw_ref[1:2, :] += jnp.sum(d1 * u, axis=0, keepdims=True)
        dw_ref[2:3, :] += jnp.sum(d * u, axis=0, keepdims=True)
        acc_o[...] += lax.dot_general(merged, dhb, TN, preferred_element_type=F32)
        acc_a[...] += lax.dot_general(attn, da, TN, preferred_element_type=F32)
        acc_c[...] += lax.dot_general(conv, dc, TN, preferred_element_type=F32)

        @pl.when(i == steps - 1)
        def _():
            gwo_ref[...] = acc_o[...].astype(BF16)
            gwa_ref[...] = acc_a[...].astype(BF16)
            gwc_ref[...] = acc_c[...].astype(BF16)

    rows = lambda c: _rows_reversed(tm, c, steps)
    return _pcall(
        body, "mix_bwd", (steps,),
        [rows(D_MODEL), _full((D_MODEL, D_MODEL)), rows(GATE_W), rows(ATTN_W), _full((ATTN_W, D_MODEL)),
         _full((CONV_W, D_MODEL)), rows(CBX_W), pl.BlockSpec((HALO, CBX_W), _prev_halo_map_reversed(tm, steps)),
         _full((3, CONV_W))],
        [rows(GATE_W), rows(ATTN_W), rows(CONV_W), rows(CONV_W), rows(CONV_W),
         _full((3, CONV_W)), _full((D_MODEL, D_MODEL)), _full((ATTN_W, D_MODEL)), _full((CONV_W, D_MODEL))],
        [_sds((s, GATE_W), BF16), _sds((s, ATTN_W), BF16), _sds((s, CONV_W), BF16), _sds((s, CONV_W), BF16),
         _sds((s, CONV_W), BF16), _sds((3, CONV_W), F32), _sds((D_MODEL, D_MODEL), BF16),
         _sds((ATTN_W, D_MODEL), BF16), _sds((CONV_W, D_MODEL), BF16)],
        (dh1, wout, gates, attn, wa, wc, cbx, cbx, conv_w),
        scratch=[pltpu.VMEM((D_MODEL, D_MODEL), F32), pltpu.VMEM((ATTN_W, D_MODEL), F32),
                 pltpu.VMEM((CONV_W, D_MODEL), F32), pltpu.VMEM((HALO, CONV_W), F32)], comm=comm)


def _attn_bwd(qkv, sinks, attn, lse, dattn, comm):
    s = qkv.shape[0]

    def body(sinks_ref, q_ref, kp_ref, kc_ref, vp_ref, vc_ref, o_ref, lse_ref, do_ref,
             dq_ref, dk_ref, dv_ref, ds_ref):
        n = pl.program_id(0)

        @pl.when(n == 0)
        def _():
            dk_ref[...] = jnp.zeros_like(dk_ref)
            dv_ref[...] = jnp.zeros_like(dv_ref)
            ds_ref[...] = jnp.zeros_like(ds_ref)

        mask = _attn_mask(n)
        lower = _lower_lanes()
        lane = lax.broadcasted_iota(jnp.int32, (BLOCK, 128), 1)
        lower2 = lax.broadcasted_iota(jnp.int32, (2 * BLOCK, 128), 1) < HEAD_DIM
        lane1 = lax.broadcasted_iota(jnp.int32, (1, 128), 1)
        qv, ov, dov, lsev = q_ref[...], o_ref[...], do_ref[...], lse_ref[...]
        dk_fold, dv_fold = [], []
        dsink = jnp.zeros((1, 128), F32)
        for kh in range(2):
            qs = _stack_heads(qv, kh)
            dos = _stack_heads(dov, kh)
            os_ = _stack_heads(ov, kh)
            kd, vd = _dup_kv(kp_ref, kc_ref, kh), _dup_kv(vp_ref, vc_ref, kh)
            lse = jnp.concatenate(
                [jnp.sum(jnp.where(lane == kh * 4 + g, lsev, 0.0), axis=1, keepdims=True) for g in range(4)], axis=0)
            sc = lax.dot_general(qs, kd, NT, preferred_element_type=F32) * ATTN_SCALE
            p = jnp.exp(jnp.where(mask, sc, NEG) - lse)
            dp = lax.dot_general(dos, vd, NT, preferred_element_type=F32)
            delta = jnp.sum(dos.astype(F32) * os_.astype(F32), axis=1, keepdims=True)
            dsc = (p * (dp - delta) * ATTN_SCALE).astype(BF16)
            dqs = jnp.dot(dsc, kd, preferred_element_type=F32)
            for pair in range(2):
                lo = dqs[(2 * pair) * BLOCK:(2 * pair + 1) * BLOCK]
                hi = dqs[(2 * pair + 1) * BLOCK:(2 * pair + 2) * BLOCK]
                col = (kh * 2 + pair) * 128
                dq_ref[:, col:col + 128] = jnp.where(lower, lo, hi).astype(BF16)
            dkd = lax.dot_general(dsc, qs, TN, preferred_element_type=F32)
            dvd = lax.dot_general(p.astype(BF16), dos, TN, preferred_element_type=F32)
            dk_fold.append(dkd + pltpu.roll(dkd, HEAD_DIM, axis=1))
            dv_fold.append(dvd + pltpu.roll(dvd, HEAD_DIM, axis=1))
            psink = jnp.exp(_sink_col(sinks_ref, kh) - lse) * delta
            for g in range(4):
                tot = jnp.sum(psink[g * BLOCK:(g + 1) * BLOCK], axis=0, keepdims=True)
                dsink = dsink - jnp.where(lane1 == kh * 4 + g, tot, 0.0)
        dk2 = jnp.where(lower2, dk_fold[0], dk_fold[1])
        dv2 = jnp.where(lower2, dv_fold[0], dv_fold[1])
        ds_ref[...] += dsink
        cur = pl.ds(pl.multiple_of(n * BLOCK, BLOCK), BLOCK)
        dk_ref[cur, :] += dk2[BLOCK:]
        dv_ref[cur, :] += dv2[BLOCK:]

        @pl.when(n > 0)
        def _():
            prev = pl.ds(pl.multiple_of((n - 1) * BLOCK, BLOCK), BLOCK)
            dk_ref[prev, :] += dk2[:BLOCK]
            dv_ref[prev, :] += dv2[:BLOCK]

    blk = lambda w: pl.BlockSpec((BLOCK, w), lambda n: (n, 0))
    return _pcall(
        body, "attn_bwd", (s // BLOCK,),
        [pl.BlockSpec(memory_space=pltpu.SMEM)] + _attn_specs() + [blk(ATTN_W), blk(128), blk(ATTN_W)],
        [blk(ATTN_W), _full((s, KV_W)), _full((s, KV_W)), _full((1, 128))],
        [_sds((s, ATTN_W), BF16), _sds((s, KV_W), F32), _sds((s, KV_W), F32), _sds((1, 128), F32)],
        (sinks, qkv, qkv, qkv, qkv, qkv, attn, lse, dattn), comm=comm)


DPROJ_PIECES = (ATTN_W, KV_W, KV_W, CONV_W, CONV_W, CONV_W, GATE_W)
DPROJ_OFFSETS = tuple(sum(DPROJ_PIECES[:k]) for k in range(len(DPROJ_PIECES)))


def _grad_w_in(pieces, xn, comm):
    s = xn.shape[0]
    ts = min(1024, s)
    steps = s // ts
    rows0 = DPROJ_OFFSETS[6]

    def body(*refs):
        p_refs, b_ref, o_ref, acc_ref, stage_ref, sem = refs[:7], refs[7], refs[8], refs[9], refs[10], refs[11]
        i, j = pl.program_id(0), pl.program_id(1)

        @pl.when(j == 0)
        def _():
            acc_ref[...] = jnp.zeros_like(acc_ref)

        bv = b_ref[...]

        def flush(lo, n):
            stage_ref[0:n, :] = acc_ref[0:n, :].astype(BF16)
            cp = pltpu.make_async_copy(stage_ref.at[0:n, :], o_ref.at[lo:lo + n, :], sem)
            cp.start()
            cp.wait()

        @pl.when(i == 0)
        def _():
            for p_ref, off, w in zip(p_refs[:6], DPROJ_OFFSETS[:6], DPROJ_PIECES[:6]):
                acc_ref[off:off + w, :] += lax.dot_general(p_ref[...].astype(BF16), bv, TN,
                                                           preferred_element_type=F32)

            @pl.when(j == steps - 1)
            def _():
                flush(0, rows0)

        @pl.when(i == 1)
        def _():
            acc_ref[0:GATE_W, :] += lax.dot_general(p_refs[6][...], bv, TN, preferred_element_type=F32)

            @pl.when(j == steps - 1)
            def _():
                flush(rows0, GATE_W)

    def piece_spec(w, group):
        return pl.BlockSpec((ts, w), lambda i, j: (jnp.where(i == group, j, 0), 0))

    outs, couts = _pcall(
        body, "grad_w_in", (2, steps),
        [piece_spec(w, 0) for w in DPROJ_PIECES[:6]] + [piece_spec(GATE_W, 1),
                                                         pl.BlockSpec((ts, D_MODEL), lambda i, j: (j, 0))],
        [ANY], [_sds((IN_W, D_MODEL), BF16)], (*pieces, xn),
        scratch=[pltpu.VMEM((rows0, D_MODEL), F32), pltpu.VMEM((rows0, D_MODEL), BF16), pltpu.SemaphoreType.DMA],
        comm=comm)
    return outs[0], couts


def _inproj_bwd(pieces, win_t, x, g, dh1, comm):
    s = x.shape[0]
    tm = _row_tile(s, 512)

    def body(*refs):
        p_refs = refs[:7]
        w_ref, x_ref, g_ref, dh_ref, dx_ref, db_ref, dg_ref = refs[7:]

        @pl.when(pl.program_id(0) == 0)
        def _():
            db_ref[...] = jnp.zeros_like(db_ref)
            dg_ref[...] = jnp.zeros_like(dg_ref)

        dxn = jnp.zeros((tm, D_MODEL), F32)
        for p_ref, off, w in zip(p_refs, DPROJ_OFFSETS, DPROJ_PIECES):
            v = p_ref[...].astype(BF16)
            db_ref[:, off:off + w] += jnp.sum(v.astype(F32), axis=0, keepdims=True)
            dxn = dxn + jnp.dot(v, w_ref[off:off + w, :], preferred_element_type=F32)
        dx, dg = _norm_bwd_tile(x_ref[...], g_ref[...], dxn)
        dg_ref[...] += dg
        dx_ref[...] = dh_ref[...] + dx

    return _pcall(
        body, "inproj_bwd", (s // tm,),
        [_rows(tm, w) for w in DPROJ_PIECES] + [_resident((IN_W, D_MODEL)), _rows(tm, D_MODEL), _full((1, D_MODEL)),
                                                _rows(tm, D_MODEL)],
        [_rows(tm, D_MODEL), _full((8, IN_W)), _full((8, D_MODEL))],
        [_sds((s, D_MODEL), F32), _sds((8, IN_W), F32), _sds((8, D_MODEL), F32)],
        (*pieces, win_t, x, g, dh1), comm=comm)


def _adam_math(w, g, m, v):
    m2 = ADAM_B1 * m + (1.0 - ADAM_B1) * g
    v2 = ADAM_B2 * v + (1.0 - ADAM_B2) * (g * g)
    m_hat = m2 / (1.0 - ADAM_B1 ** ADAM_STEP)
    v_hat = v2 / (1.0 - ADAM_B2 ** ADAM_STEP)
    delta = -ADAM_LR * (m_hat / (jnp.sqrt(v_hat) + ADAM_EPS) + ADAM_WD * w)
    return delta, m2, v2


def _sum_slots(ref):
    tot = ref[0].astype(F32)
    for i in range(1, ref.shape[0]):
        tot = tot + ref[i].astype(F32)
    return tot


def _pair_add(partials, theirs, tr, name):
    r = partials.shape[0] // N_DEV
    c = partials.shape[1]
    nt = r // tr
    core = lax.axis_index("c").astype(jnp.int32).reshape(1)

    def body(core_ref, a_ref, b_ref, o_ref):
        o_ref[...] = (a_ref[...].astype(F32) + b_ref[...].astype(F32)).astype(BF16)

    grid_spec = pltpu.PrefetchScalarGridSpec(
        num_scalar_prefetch=1, grid=(4 * nt,),
        in_specs=[pl.BlockSpec((None, None, tr, c), lambda i, core_ref: (i // nt, core_ref[0], i % nt, 0)),
                  pl.BlockSpec((tr, c), lambda i, core_ref: (i, 0))],
        out_specs=pl.BlockSpec((tr, c), lambda i, core_ref: (i, 0)))
    return pl.pallas_call(body, name=name, grid_spec=grid_spec, out_shape=_sds((4 * r, c), BF16))(
        core, partials.reshape(4, 2, r, c), theirs)


def _sum_adamw(parts, w, m, v, tr, name):
    r, c = w.shape

    def body(p_ref, w_ref, m_ref, v_ref, g_ref, d_ref, m2_ref, v2_ref):
        g = _sum_slots(p_ref)
        g_ref[...] = g
        d_ref[...], m2_ref[...], v2_ref[...] = _adam_math(w_ref[...], g, m_ref[...], v_ref[...])

    spec = pl.BlockSpec((tr, c), lambda i: (i, 0))
    return _pcall(body, name, (r // tr,), [pl.BlockSpec((N_DEV, tr, c), lambda i: (0, i, 0)), spec, spec, spec],
                  [spec] * 4, [_sds((r, c), F32)] * 4, (parts, w, m, v))[0]


def _sum_parts_adamw(parts, w, m, v, tr, name):
    c = w.shape[1]
    tiles = [p.shape[1] // tr for p in parts]
    starts = [sum(tiles[:k]) for k in range(len(parts))]
    n_parts = len(parts)

    def body(*refs):
        p_refs = refs[:n_parts]
        w_ref, m_ref, v_ref, g_ref, d_ref, m2_ref, v2_ref = refs[n_parts:]
        i = pl.program_id(0)
        for p_ref, st, nt in zip(p_refs, starts, tiles):
            @pl.when(jnp.logical_and(i >= st, i < st + nt))
            def _(p_ref=p_ref):
                g_ref[...] = _sum_slots(p_ref)

        d_ref[...], m2_ref[...], v2_ref[...] = _adam_math(w_ref[...], g_ref[...], m_ref[...], v_ref[...])

    def part_spec(p, st, nt):
        return pl.BlockSpec((p.shape[0], tr, c), lambda i: (0, jnp.clip(i - st, 0, nt - 1), 0))

    spec = pl.BlockSpec((tr, c), lambda i: (i, 0))
    return _pcall(
        body, name, (sum(tiles),),
        [part_spec(p, st, nt) for p, st, nt in zip(parts, starts, tiles)] + [spec, spec, spec],
        [spec] * 4, [_sds(w.shape, F32)] * 4, (*parts, w, m, v))[0]


ROW_MIX, ROW_FFN, ROW_FINAL, ROW_SINKS, ROW_LOSS, ROW_BIN, ROW_CW, ROW_FCW = 0, 1, 2, 3, 4, 5, 10, 13
FCW_ROWS = 6


def _wide_pieces(width):
    return [(k * D_MODEL, min(D_MODEL, width - k * D_MODEL)) for k in range(-(-width // D_MODEL))]


def _pack_small(dffn, dfn, dsink, loss, dcw, dfcw):
    def body(ffn_ref, fn_ref, sink_ref, loss_ref, cw_ref, fcw_ref, o_ref):
        o_ref[...] = jnp.zeros_like(o_ref)
        o_ref[ROW_FFN:ROW_FFN + 1, :] = ffn_ref[...]
        o_ref[ROW_FINAL:ROW_FINAL + 1, :] = fn_ref[...]
        o_ref[ROW_SINKS:ROW_SINKS + 1, 0:128] = sink_ref[...]
        o_ref[ROW_LOSS:ROW_LOSS + 1, 0:128] = loss_ref[...]
        o_ref[ROW_CW:ROW_CW + 3, 0:CONV_W] = cw_ref[...]
        for a in range(3):
            for k, (off, w) in enumerate(_wide_pieces(2 * D_FF)):
                row = ROW_FCW + FCW_ROWS * a + k
                o_ref[row:row + 1, 0:w] = fcw_ref[a:a + 1, off:off + w]

    return pl.pallas_call(body, name="pack_small", out_shape=_sds((SMALL_ROWS, D_MODEL), F32))(
        dffn, dfn, dsink, loss, dcw, dfcw)


def _small_sums_adamw(r_small, r_dmix, r_dbin, params):
    rows = (None, None, ROW_SINKS, ROW_FFN, ROW_FINAL)

    def sum_row0(ref):
        tot = ref[0:1, :]
        for i in range(1, N_DEV):
            tot = tot + ref[8 * i:8 * i + 1, :]
        return tot

    def body(*refs):
        r_ref, late_refs, p_refs, o_refs = refs[0], refs[1:3], refs[3:18], refs[18:]
        tot = _sum_slots(r_ref)
        for k, row in enumerate(rows):
            w_ref, m_ref, v_ref = p_refs[3 * k:3 * k + 3]
            g_ref, d_ref, m2_ref, v2_ref = o_refs[4 * k:4 * k + 4]
            if row is None:
                g_ref[...] = sum_row0(late_refs[k])
            else:
                for j, (off, w) in enumerate(_wide_pieces(w_ref.shape[1])):
                    g_ref[:, off:off + w] = tot[row + j:row + j + 1, 0:w]
            d_ref[...], m2_ref[...], v2_ref[...] = _adam_math(w_ref[...], g_ref[...], m_ref[...], v_ref[...])
        cw_ref, fcw_ref, loss_ref = o_refs[20:]
        cw_ref[...] = tot[ROW_CW:ROW_CW + 3, 0:CONV_W]
        for a in range(3):
            for j, (off, w) in enumerate(_wide_pieces(2 * D_FF)):
                row = ROW_FCW + FCW_ROWS * a + j
                fcw_ref[a:a + 1, off:off + w] = tot[row:row + 1, 0:w]
        loss_ref[...] = tot[ROW_LOSS:ROW_LOSS + 1, 0:128]

    flat = [t for p in params for t in p]
    out_shape = [_sds(p[0].shape, F32) for p in params for _ in range(4)]
    out_shape += [_sds((3, CONV_W), F32), _sds((3, 2 * D_FF), F32), _sds((1, 128), F32)]
    res = pl.pallas_call(body, name="small_sums_adamw", out_shape=out_shape)(r_small, r_dmix, r_dbin, *flat)
    return [tuple(res[4 * k:4 * k + 4]) for k in range(5)], res[20], res[21], res[22]


def _adamw_pair(a, b):
    def body(*refs):
        for k in range(2):
            w_ref, g_ref, m_ref, v_ref = refs[4 * k:4 * k + 4]
            d_ref, m2_ref, v2_ref = refs[8 + 3 * k:8 + 3 * k + 3]
            d_ref[...], m2_ref[...], v2_ref[...] = _adam_math(w_ref[...], g_ref[...], m_ref[...], v_ref[...])

    out_shape = [_sds(a[0].shape, F32)] * 3 + [_sds(b[0].shape, F32)] * 3
    res = pl.pallas_call(body, name="adamw_conv_weights", out_shape=out_shape)(*a, *b)
    return tuple(res[:3]), tuple(res[3:])


def _pad_cols(a, c):
    return jnp.pad(a, ((0, 0), (0, c - a.shape[1])))


def _to_col_slabs(g):
    r = g.shape[0]
    return jnp.transpose(g.reshape(r, N_DEV, 128), (1, 0, 2)).reshape(N_DEV * r, 128)


def _from_col_slabs(t):
    r = t.shape[0] // N_DEV
    return jnp.transpose(t.reshape(N_DEV, r, 128), (1, 0, 2)).reshape(r, N_DEV * 128)


def _slots(t):
    return t.reshape(N_DEV, t.shape[0] // N_DEV, t.shape[1])


def kernel(x, mix_norm, w_in, b_in, sinks, conv_w, w_attn_branch, w_conv_branch, w_out, ffn_norm, w_up, ffn_conv_w, w_down, final_norm, loss_target, m_mix_norm, m_w_in, m_b_in, m_sinks, m_conv_w, m_w_attn_branch, m_w_conv_branch, m_w_out, m_ffn_norm, m_w_up, m_ffn_conv_w, m_w_down, m_final_norm, v_mix_norm, v_w_in, v_b_in, v_sinks, v_conv_w, v_w_attn_branch, v_w_conv_branch, v_w_out, v_ffn_norm, v_w_up, v_ffn_conv_w, v_w_down, v_final_norm):
    xs, tgt = x[0], loss_target[0]
    me = 4 * lax.axis_index("x") + 2 * lax.axis_index("y") + lax.axis_index("c")
    in_rows, up_rows = IN_W // N_DEV, 2 * D_FF // N_DEV

    conv_sh = jnp.concatenate([_pad_cols(ffn_conv_w[0], 768), _pad_cols(conv_w[0], 768),
                               jnp.zeros((2, 768), F32)], axis=0)
    win_sh, wup_sh = w_in[0].T.astype(BF16), w_up[0].T.astype(BF16)
    wout_sh, wdown_sh = w_out[0].astype(BF16), w_down[0].astype(BF16)
    wa_sh, wc_sh = w_attn_branch[0].astype(BF16), w_conv_branch[0].astype(BF16)

    half = D_MODEL // 2
    (win_t,) = _exchange_only(_AllGather([win_sh]), "gather_w_in")
    (xn, qkv, cbx, gates), (wa_s, wc_s, wout, conv_g) = _norm_inproj(
        xs, mix_norm, win_t, b_in, _AllGather([wa_sh, wc_sh, wout_sh, conv_sh]))
    (attn, lse), (wup_lo,) = _attn_fwd(qkv, sinks, _AllGather([wup_sh[:, :half]]))
    wa, wc = _from_col_slabs(wa_s), _from_col_slabs(wc_s)
    conv_g = conv_g.reshape(N_DEV, 8, 768)
    fcw = jnp.transpose(conv_g[:, 0:3, :up_rows], (1, 0, 2)).reshape(3, 2 * D_FF)
    cw = jnp.transpose(conv_g[:, 3:6, :CONV_W // N_DEV], (1, 0, 2)).reshape(3, CONV_W)
    (h1,), (wup_hi,) = _mix_fwd(xs, cbx, gates, attn, cw, wa, wc, wout, _AllGather([wup_sh[:, half:]]))
    (hn, up_pre, up), (wdown,) = _ffn_up(h1, ffn_norm, wup_lo, wup_hi, fcw, _AllGather([wdown_sh]))
    act, dh2, loss_p, dfn_p = _ffn_down_loss(up, wdown, h1, final_norm.reshape(1, D_MODEL), tgt)

    dn_rows, q_up = D_FF // N_DEV, up_rows // 4
    g_wdown = _matmul_tn(act, dh2, FF_CHUNK, "grad_w_down")
    (dup_pre, dfcw_p, dh1, dffn_p), (r_wdown,) = _ffn_bwd(dh2, wdown, up, up_pre, fcw, wup_lo, wup_hi, h1, ffn_norm,
                                                         _ReduceScatter([(g_wdown, 0, dn_rows)]))
    g_wup_t = _matmul_tn(dup_pre, hn, FF_CHUNK, "grad_w_up")
    (dgates, dattn, dcb, dcc, dcx, dcw_p, g_wout, g_wa_nat, g_wc_nat), (r_wup_ab,) = _mix_bwd(
        dh1, wout, gates, attn, wa, wc, cbx, cw, _ReduceScatter([(g_wup_t, 0, 2 * q_up)]))
    g_wa, g_wc = _to_col_slabs(g_wa_nat), _to_col_slabs(g_wc_nat)
    (dq, dk, dv, dsink_p), (r_wup_c, r_wout, r_wa, r_wc) = _attn_bwd(
        qkv, sinks, attn, lse, dattn,
        _ReduceScatter([(g_wup_t, 2 * q_up, q_up), (g_wout, 0, D_MODEL // N_DEV), (g_wa, 0, ATTN_W),
                        (g_wc, 0, CONV_W)]))
    dproj = (dq, dk, dv, dcb, dcc, dcx, dgates)
    small = _pack_small(dffn_p, dfn_p, dsink_p, loss_p, dcw_p, dfcw_p)
    g_win_t, (r_wup_d, r_small) = _grad_w_in(dproj, xn, _ReduceScatter([(g_wup_t, 3 * q_up, q_up)], [small]))
    (win_theirs,) = _exchange_only(_PairExchange([g_win_t]), "pair_exchange_w_in")
    q_win = _pair_add(g_win_t, win_theirs, in_rows // 2, "pair_add_w_in")
    (dx, _, _), (r_win, r_dbin, r_dmix) = _inproj_bwd(
        dproj, win_t, xs, mix_norm, dh1,
        _ChipExchangeThenBroadcast([q_win], late_from=(1, 2), late_shapes=[(8, IN_W), (8, D_MODEL)]))

    fn2, m_fn2, v_fn2 = (t.reshape(1, D_MODEL) for t in (final_norm, m_final_norm, v_final_norm))
    small_res, g_cw_full, g_fcw_full, loss_row = _small_sums_adamw(
        _slots(r_small), r_dmix, r_dbin,
        [(mix_norm, m_mix_norm, v_mix_norm), (b_in, m_b_in, v_b_in), (sinks, m_sinks, v_sinks),
         (ffn_norm, m_ffn_norm, v_ffn_norm), (fn2, m_fn2, v_fn2)])
    loss = loss_row[0, 0]
    g_cw = lax.dynamic_slice_in_dim(g_cw_full, me * (CONV_W // N_DEV), CONV_W // N_DEV, axis=1)
    g_fcw = lax.dynamic_slice_in_dim(g_fcw_full, me * up_rows, up_rows, axis=1)
    taps = lambda t: jnp.transpose(t, (1, 0, 2))
    g_cw, g_fcw = g_cw[:, None, :], g_fcw[:, None, :]
    cw_res, fcw_res = _adamw_pair((taps(conv_w), g_cw, taps(m_conv_w), taps(v_conv_w)),
                                  (taps(ffn_conv_w), g_fcw, taps(m_ffn_conv_w), taps(v_ffn_conv_w)))

    big = {}
    big["w_in"] = tuple(t.T for t in _sum_parts_adamw(
        [r_win.reshape(4, in_rows, D_MODEL)], w_in[0].T, m_w_in[0].T, v_w_in[0].T, in_rows // 2, "adamw_w_in"))
    big["w_up"] = tuple(t.T for t in _sum_parts_adamw(
        [_slots(r_wup_ab), _slots(r_wup_c), _slots(r_wup_d)], w_up[0].T, m_w_up[0].T, v_w_up[0].T, q_up,
        "adamw_w_up"))
    big["w_out"] = _sum_adamw(_slots(r_wout), w_out[0], m_w_out[0], v_w_out[0], 128, "adamw_w_out")
    big["w_down"] = _sum_adamw(_slots(r_wdown), w_down[0], m_w_down[0], v_w_down[0], dn_rows // 2, "adamw_w_down")
    big["w_attn_branch"] = _sum_adamw(_slots(r_wa), w_attn_branch[0], m_w_attn_branch[0], v_w_attn_branch[0], 256,
                                      "adamw_w_attn_branch")
    big["w_conv_branch"] = _sum_adamw(_slots(r_wc), w_conv_branch[0], m_w_conv_branch[0], v_w_conv_branch[0], 256,
                                      "adamw_w_conv_branch")

    res = dict(zip(("mix_norm", "b_in", "sinks", "ffn_norm"), small_res[:4]))
    res["final_norm"] = tuple(t.reshape(final_norm.shape) for t in small_res[4])
    res["conv_w"] = tuple(jnp.transpose(t, (1, 0, 2)) for t in (g_cw,) + cw_res)
    res["ffn_conv_w"] = tuple(jnp.transpose(t, (1, 0, 2)) for t in (g_fcw,) + fcw_res)
    for name, ref_w in (("w_in", w_in), ("w_up", w_up), ("w_out", w_out), ("w_down", w_down),
                        ("w_attn_branch", w_attn_branch), ("w_conv_branch", w_conv_branch)):
        res[name] = tuple(t.reshape(ref_w.shape) for t in big[name])

    order = ["mix_norm", "w_in", "b_in", "sinks", "conv_w", "w_attn_branch", "w_conv_branch", "w_out",
             "ffn_norm", "w_up", "ffn_conv_w", "w_down", "final_norm"]
    out = [loss, dx.reshape(x.shape)]
    for k in range(4):
        out += [res[name][k] for name in order]
    return tuple(out)
```

```python
import math

import jax
import jax.numpy as jnp
from jax import lax
from jax.experimental import pallas as pl
from jax.experimental.pallas import tpu as pltpu

F32 = jnp.float32
BF16 = jnp.bfloat16
MESH = pl.DeviceIdType.MESH
N_DEV = 8

D_MODEL = 1024
HEAD_DIM = 64
N_HEADS = 8
BLOCK = 128
ATTN_W = 512
KV_W = 128
CONV_W = 512
QKV_W = ATTN_W + 2 * KV_W
CBX_W = 3 * CONV_W
GATE_W = 2 * D_MODEL
IN_W = QKV_W + CBX_W + GATE_W
D_FF = 2816
FF_CHUNK = 256
FF_GRAD_ROWS = 1408
NORM_EPS = 1e-5
ATTN_SCALE = HEAD_DIM ** -0.5
NEG = -1e30
HALO = 16

ADAM_LR = 0.001
ADAM_B1 = 0.9
ADAM_B2 = 0.999
ADAM_EPS = 1e-08
ADAM_WD = 0.01
ADAM_STEP = 10

VMEM_LIMIT = 56 * 1024 * 1024
SMALL_ROWS = 32

NT = (((1,), (1,)), ((), ()))
TN = (((0,), (0,)), ((), ()))
ANY = pl.BlockSpec(memory_space=pl.ANY)


def _sig(v):
    return 1.0 / (1.0 + jnp.exp(-v))


def _row_tile(s, pref=256):
    return pref if s % pref == 0 else s


def _shifts_down(u, halo, ks):
    ext = jnp.concatenate([halo, u], axis=0)
    return [pltpu.roll(ext, k, axis=0)[HALO:, :] for k in ks]


def _shifts_up(u, halo, ks):
    n = u.shape[0]
    ext = jnp.concatenate([u, halo], axis=0)
    return [pltpu.roll(ext, n + HALO - k, axis=0)[:n, :] for k in ks]


def _rows_reversed(tm, c, steps):
    return pl.BlockSpec((tm, c), lambda i: (steps - 1 - i, 0))


def _prev_halo_map_reversed(tm, steps):
    return lambda i: (jnp.maximum((steps - 1 - i) * (tm // HALO) - 1, 0), 0)


def _prev_halo_map(tm):
    return lambda i: (jnp.maximum(i * (tm // HALO) - 1, 0), 0)


def _full(shape):
    return pl.BlockSpec(shape, lambda *_: (0,) * len(shape))


def _resident(shape):
    return pl.BlockSpec(shape, lambda *_: (0,) * len(shape), pipeline_mode=pl.Buffered(1))


def _rows(tm, c):
    return pl.BlockSpec((tm, c), lambda i: (i, 0))


def _sds(shape, dtype):
    return jax.ShapeDtypeStruct(shape, dtype)


def _my_place():
    x, y, c = lax.axis_index("x"), lax.axis_index("y"), lax.axis_index("c")
    return x, y, c


ALL_PEERS = tuple((j >> 2, (j >> 1) & 1, j & 1) for j in range(1, N_DEV))
SIBLING_PEER = ((0, 0, 1),)
CHIP_PEERS = ((0, 1, 0), (1, 0, 0), (1, 1, 0))
BARRIER_ID = {ALL_PEERS: 0, SIBLING_PEER: 1, CHIP_PEERS: 2}


def _entry_barrier(peers):
    x, y, c = _my_place()
    barrier = pltpu.get_barrier_semaphore()
    for dx, dy, dc in peers:
        pl.semaphore_signal(barrier, inc=1, device_id=(x ^ dx, y ^ dy, c ^ dc), device_id_type=MESH)
    pl.semaphore_wait(barrier, len(peers))


def _start_exchange(remote, local):
    for cp in local + remote:
        cp.start()


def _finish_exchange(remote, local):
    for cp in remote:
        cp.wait_recv()
    for cp in remote:
        cp.wait_send()
    for cp in local:
        cp.wait()


class _AllGather:
    peers = ALL_PEERS

    def __init__(self, shards, pass_on_at=None):
        self.ins = list(shards)
        self.middle_at = pass_on_at
        n = len(shards)
        self.out_shape = [_sds((N_DEV * s.shape[0], s.shape[1]), s.dtype) for s in shards]
        self.sems = [pltpu.SemaphoreType.DMA((7 * n,)), pltpu.SemaphoreType.DMA((7 * n,)),
                     pltpu.SemaphoreType.DMA((n,))]

    def _parts(self, ins, outs, sems):
        send_sems, recv_sems, local_sems = sems
        x, y, c = _my_place()
        me, sibling = (x, y, c), (x, y, 1 - c)
        chips = [(1 - x, y), (x, 1 - y), (1 - x, 1 - y)]

        def rows(k, dev):
            r = ins[k].shape[0]
            start = pl.multiple_of((4 * dev[0] + 2 * dev[1] + dev[2]) * r, 8)
            return outs[k].at[pl.ds(start, r), :]

        def copy(k, j, block, to, src=None):
            return pltpu.make_async_remote_copy(
                src_ref=rows(k, block) if src is None else src, dst_ref=rows(k, block),
                send_sem=send_sems.at[7 * k + j], recv_sem=recv_sems.at[7 * k + j],
                device_id=to, device_id_type=MESH)

        n = len(ins)
        mine = [pltpu.make_async_copy(ins[k], rows(k, me), local_sems.at[k]) for k in range(n)]
        first = []
        for k in range(n):
            first.append(copy(k, 0, me, sibling, src=ins[k]))
            first += [copy(k, 1 + j, me, (*chip, c), src=ins[k]) for j, chip in enumerate(chips)]
        return me, sibling, chips, copy, mine, first

    def start(self, ins, outs, sems):
        _, _, _, _, mine, first = self._parts(ins, outs, sems)
        _start_exchange(first, mine)

    def middle(self, ins, outs, sems):
        me, sibling, chips, copy, _, _ = self._parts(ins, outs, sems)
        for j, chip in enumerate(chips):
            for k in range(len(ins)):
                copy(k, 1 + j, (*chip, me[2]), me).wait_recv()
                copy(k, 4 + j, (*chip, me[2]), sibling).start()

    def finish(self, ins, outs, sems):
        if self.middle_at is None:
            self.middle(ins, outs, sems)
        me, sibling, chips, copy, mine, first = self._parts(ins, outs, sems)
        c = me[2]
        n = len(ins)
        passed = [copy(k, 4 + j, (*chip, c), sibling) for j, chip in enumerate(chips) for k in range(n)]
        for k in range(n):
            copy(k, 0, sibling, me).wait_recv()
            for j, chip in enumerate(chips):
                copy(k, 4 + j, (*chip, 1 - c), me).wait_recv()
        for cp in first + passed:
            cp.wait_send()
        for cp in mine:
            cp.wait()


class _ReduceScatter:
    peers = ALL_PEERS

    def __init__(self, parts, bcast=()):
        self.parts = [(lo, cnt) for _, lo, cnt in parts]
        self.n_parts = len(parts)
        self.ins = [a for a, _, _ in parts] + list(bcast)
        self.out_shape = [_sds((N_DEV * cnt, a.shape[1]), a.dtype) for a, _, cnt in parts]
        self.out_shape += [_sds((N_DEV * b.shape[0], b.shape[1]), b.dtype) for b in bcast]
        n = len(self.ins)
        self.sems = [pltpu.SemaphoreType.DMA((7 * n,)), pltpu.SemaphoreType.DMA((7 * n,)),
                     pltpu.SemaphoreType.DMA((n,))]

    def _copies(self, ins, outs, sems):
        send_sems, recv_sems, local_sems = sems
        x, y, c = _my_place()
        me_idx = 4 * x + 2 * y + c
        remote, local = [], []
        for k in range(len(ins)):
            cnt = outs[k].shape[0] // N_DEV
            dst = outs[k].at[pl.ds(pl.multiple_of(me_idx * cnt, 8), cnt), :]
            if k < self.n_parts:
                lo, _ = self.parts[k]
                r = ins[k].shape[0] // N_DEV
                src_of = lambda idx: ins[k].at[pl.ds(pl.multiple_of(idx * r + lo, 8), cnt), :]
            else:
                src_of = lambda idx: ins[k]
            local.append(pltpu.make_async_copy(src_of(me_idx), dst, local_sems.at[k]))
            for j in range(1, N_DEV):
                peer = (x ^ (j >> 2), y ^ ((j >> 1) & 1), c ^ (j & 1))
                peer_idx = 4 * peer[0] + 2 * peer[1] + peer[2]
                remote.append(pltpu.make_async_remote_copy(
                    src_ref=src_of(peer_idx), dst_ref=dst,
                    send_sem=send_sems.at[7 * k + j - 1], recv_sem=recv_sems.at[7 * k + j - 1],
                    device_id=peer, device_id_type=MESH))
        return remote, local

    def start(self, ins, outs, sems):
        _start_exchange(*self._copies(ins, outs, sems))

    def finish(self, ins, outs, sems):
        _finish_exchange(*self._copies(ins, outs, sems))


class _PairExchange:
    peers = SIBLING_PEER

    def __init__(self, arrays):
        self.ins = list(arrays)
        n = len(arrays)
        self.out_shape = [_sds((a.shape[0] // 2, a.shape[1]), a.dtype) for a in arrays]
        self.sems = [pltpu.SemaphoreType.DMA((4 * n,)), pltpu.SemaphoreType.DMA((4 * n,))]

    def _copies(self, ins, outs, sems):
        send_sems, recv_sems = sems
        x, y, c = _my_place()
        remote = []
        for k in range(len(ins)):
            r = ins[k].shape[0] // N_DEV
            for chip in range(4):
                sib = ins[k].at[pl.ds(pl.multiple_of((2 * chip + 1 - c) * r, 8), r), :]
                remote.append(pltpu.make_async_remote_copy(
                    src_ref=sib, dst_ref=outs[k].at[pl.ds(chip * r, r), :],
                    send_sem=send_sems.at[4 * k + chip], recv_sem=recv_sems.at[4 * k + chip],
                    device_id=(x, y, 1 - c), device_id_type=MESH))
        return remote

    def start(self, ins, outs, sems):
        for cp in self._copies(ins, outs, sems):
            cp.start()

    def finish(self, ins, outs, sems):
        remote = self._copies(ins, outs, sems)
        for cp in remote:
            cp.wait_recv()
        for cp in remote:
            cp.wait_send()


class _ChipExchange:
    peers = CHIP_PEERS

    def __init__(self, arrays):
        self.ins = list(arrays)
        self.out_shape = [_sds(a.shape, a.dtype) for a in arrays]
        n = len(self.ins)
        self.sems = [pltpu.SemaphoreType.DMA((3 * n,)), pltpu.SemaphoreType.DMA((3 * n,)),
                     pltpu.SemaphoreType.DMA((n,))]

    def _copies(self, ins, outs, sems):
        send_sems, recv_sems, local_sems = sems
        x, y, c = _my_place()
        my_chip = 2 * x + y
        remote, local = [], []
        for k in range(len(ins)):
            r = ins[k].shape[0] // 4
            dst = outs[k].at[pl.ds(pl.multiple_of(my_chip * r, 8), r), :]
            local.append(pltpu.make_async_copy(ins[k].at[pl.ds(pl.multiple_of(my_chip * r, 8), r), :], dst,
                                               local_sems.at[k]))
            for j in range(1, 4):
                px, py = x ^ (j >> 1), y ^ (j & 1)
                src = ins[k].at[pl.ds(pl.multiple_of((2 * px + py) * r, 8), r), :]
                remote.append(pltpu.make_async_remote_copy(
                    src_ref=src, dst_ref=dst, send_sem=send_sems.at[3 * k + j - 1],
                    recv_sem=recv_sems.at[3 * k + j - 1], device_id=(px, py, c), device_id_type=MESH))
        return remote, local

    def start(self, ins, outs, sems):
        _start_exchange(*self._copies(ins, outs, sems))

    def finish(self, ins, outs, sems):
        _finish_exchange(*self._copies(ins, outs, sems))


class _ChipExchangeThenBroadcast(_ChipExchange):
    peers = ALL_PEERS

    def __init__(self, arrays, late_from, late_shapes):
        super().__init__(arrays)
        self.n_chip = len(arrays)
        self.late_from = tuple(late_from)
        self.out_shape += [_sds((N_DEV * r, c), F32) for r, c in late_shapes]
        m = len(late_shapes)
        self.sems += [pltpu.SemaphoreType.DMA((7 * m,)), pltpu.SemaphoreType.DMA((7 * m,)),
                      pltpu.SemaphoreType.DMA((m,))]

    def _late_copies(self, srcs, outs, sems):
        send_sems, recv_sems, local_sems = sems
        x, y, c = _my_place()
        me_idx = 4 * x + 2 * y + c
        remote, local = [], []
        for k, src in enumerate(srcs):
            r = src.shape[0]
            dst = outs[k].at[pl.ds(pl.multiple_of(me_idx * r, 8), r), :]
            local.append(pltpu.make_async_copy(src, dst, local_sems.at[k]))
            for j, (dx, dy, dc) in enumerate(ALL_PEERS):
                remote.append(pltpu.make_async_remote_copy(
                    src_ref=src, dst_ref=dst, send_sem=send_sems.at[7 * k + j], recv_sem=recv_sems.at[7 * k + j],
                    device_id=(x ^ dx, y ^ dy, c ^ dc), device_id_type=MESH))
        return remote, local

    def start(self, ins, outs, sems):
        _start_exchange(*self._copies(ins, outs[:self.n_chip], sems[:3]))

    def finish(self, ins, outs, sems, late_srcs):
        late = self._late_copies(late_srcs, outs[self.n_chip:], sems[3:])
        _start_exchange(*late)
        _finish_exchange(*self._copies(ins, outs[:self.n_chip], sems[:3]))
        _finish_exchange(*late)


def _pcall(body, name, grid, in_specs, out_specs, out_shape, args, scratch=(), comm=None):
    params = pltpu.CompilerParams(dimension_semantics=("arbitrary",) * len(grid), vmem_limit_bytes=VMEM_LIMIT)
    in_specs, out_specs, out_shape, scratch = list(in_specs), list(out_specs), list(out_shape), list(scratch)
    if comm is None:
        res = pl.pallas_call(body, name=name, grid=grid, in_specs=in_specs, out_specs=out_specs, out_shape=out_shape,
                             scratch_shapes=scratch, compiler_params=params)(*args)
        return list(res), []
    n_in, n_out, n_scr = len(in_specs), len(out_specs), len(scratch)
    ci, co = len(comm.ins), len(comm.out_shape)
    total = math.prod(grid)

    def carried(*refs):
        bounds = [0, n_in, n_in + ci, n_in + ci + n_out, n_in + ci + n_out + co, n_in + ci + n_out + co + n_scr]
        ins, cins, outs, couts, scr = (refs[a:b] for a, b in zip(bounds[:-1], bounds[1:]))
        sems = refs[bounds[-1]:]
        step = pl.program_id(0)
        for d in range(1, len(grid)):
            step = step * grid[d] + pl.program_id(d)

        @pl.when(step == 0)
        def _():
            _entry_barrier(comm.peers)
            comm.start(cins, couts, sems)

        middle_at = getattr(comm, "middle_at", None)
        if middle_at is not None:
            @pl.when(step == int(middle_at * total))
            def _():
                comm.middle(cins, couts, sems)

        body(*ins, *outs, *scr)

        @pl.when(step == total - 1)
        def _():
            late_from = getattr(comm, "late_from", None)
            if late_from is None:
                comm.finish(cins, couts, sems)
            else:
                comm.finish(cins, couts, sems, [outs[k] for k in late_from])

    params = pltpu.CompilerParams(dimension_semantics=("arbitrary",) * len(grid), vmem_limit_bytes=VMEM_LIMIT,
                                  collective_id=BARRIER_ID[comm.peers])
    res = pl.pallas_call(
        carried, name=name, grid=grid, in_specs=in_specs + [ANY] * ci, out_specs=out_specs + [ANY] * co,
        out_shape=out_shape + comm.out_shape, scratch_shapes=scratch + comm.sems, compiler_params=params,
    )(*args, *comm.ins)
    return list(res[:n_out]), list(res[n_out:])


def _exchange_only(comm, name):
    def body(*refs):
        ci, co = len(comm.ins), len(comm.out_shape)
        _entry_barrier(comm.peers)
        comm.start(refs[:ci], refs[ci:ci + co], refs[ci + co:])
        comm.finish(refs[:ci], refs[ci:ci + co], refs[ci + co:])

    params = pltpu.CompilerParams(collective_id=BARRIER_ID[comm.peers])
    return pl.pallas_call(body, name=name, out_shape=comm.out_shape, in_specs=[ANY] * len(comm.ins),
                          out_specs=[ANY] * len(comm.out_shape), scratch_shapes=comm.sems,
                          compiler_params=params)(*comm.ins)


def _norm_inproj(x, g, win_t, b_in, comm):
    s = x.shape[0]
    tm = _row_tile(s, 512)
    widths = (QKV_W, CBX_W, GATE_W)

    def body(x_ref, g_ref, w_ref, b_ref, xn_ref, qkv_ref, cbx_ref, gate_ref):
        xv = x_ref[...]
        r = lax.rsqrt(jnp.mean(xv * xv, axis=-1, keepdims=True) + NORM_EPS)
        xn = (xv * r * g_ref[...]).astype(BF16)
        xn_ref[...] = xn
        off = 0
        for o_ref, w in zip((qkv_ref, cbx_ref, gate_ref), widths):
            acc = lax.dot_general(xn, w_ref[off:off + w, :], NT, preferred_element_type=F32)
            o_ref[...] = (acc + b_ref[:, off:off + w]).astype(BF16)
            off += w

    return _pcall(
        body, "norm_inproj", (s // tm,),
        [_rows(tm, D_MODEL), _full((1, D_MODEL)), _resident((IN_W, D_MODEL)), _full((1, IN_W))],
        [_rows(tm, D_MODEL)] + [_rows(tm, w) for w in widths],
        [_sds((s, D_MODEL), BF16)] + [_sds((s, w), BF16) for w in widths],
        (x, g, win_t, b_in), comm=comm)


def _attn_specs():
    prev = lambda n: jnp.maximum(n - 1, 0)
    return [pl.BlockSpec((BLOCK, ATTN_W), lambda n: (n, 0)),
            pl.BlockSpec((BLOCK, KV_W), lambda n: (prev(n), ATTN_W // KV_W)),
            pl.BlockSpec((BLOCK, KV_W), lambda n: (n, ATTN_W // KV_W)),
            pl.BlockSpec((BLOCK, KV_W), lambda n: (prev(n), ATTN_W // KV_W + 1)),
            pl.BlockSpec((BLOCK, KV_W), lambda n: (n, ATTN_W // KV_W + 1))]


def _lower_lanes():
    return lax.broadcasted_iota(jnp.int32, (BLOCK, 128), 1) < HEAD_DIM


def _stack_heads(val, kh):
    lower = _lower_lanes()
    parts = []
    for g in range(4):
        h = kh * 4 + g
        blk = val[:, (h // 2) * 128:(h // 2 + 1) * 128]
        keep = lower if h % 2 == 0 else jnp.logical_not(lower)
        parts.append(jnp.where(keep, blk, jnp.zeros_like(blk)))
    return jnp.concatenate(parts, axis=0)


def _dup_kv(prev_ref, cur_ref, kh):
    t = jnp.concatenate([prev_ref[...], cur_ref[...]], axis=0).astype(F32)
    rolled = pltpu.roll(t, HEAD_DIM, axis=1)
    lower = lax.broadcasted_iota(jnp.int32, t.shape, 1) < HEAD_DIM
    dup = jnp.where(lower, t, rolled) if kh == 0 else jnp.where(lower, rolled, t)
    return dup.astype(BF16)


def _attn_mask(n):
    row = lax.broadcasted_iota(jnp.int32, (4 * BLOCK, 2 * BLOCK), 0)
    kj = lax.broadcasted_iota(jnp.int32, (4 * BLOCK, 2 * BLOCK), 1)
    dist = (row & (BLOCK - 1)) + BLOCK - kj
    band = jnp.logical_and(dist >= 0, dist < BLOCK)
    return jnp.logical_and(band, jnp.logical_or(kj >= BLOCK, n > 0))


def _sink_col(sinks_ref, kh):
    gi = lax.broadcasted_iota(jnp.int32, (4 * BLOCK, 1), 0) // BLOCK
    col = jnp.zeros((4 * BLOCK, 1), F32)
    for g in range(4):
        col = jnp.where(gi == g, sinks_ref[0, kh * 4 + g], col)
    return col


def _attn_fwd(qkv, sinks, comm):
    s = qkv.shape[0]

    def body(sinks_ref, q_ref, kp_ref, kc_ref, vp_ref, vc_ref, o_ref, lse_ref):
        n = pl.program_id(0)
        mask = _attn_mask(n)
        lower = _lower_lanes()
        lane = lax.broadcasted_iota(jnp.int32, (BLOCK, 128), 1)
        qv = q_ref[...]
        lse_out = jnp.zeros((BLOCK, 128), F32)
        for kh in range(2):
            qs = _stack_heads(qv, kh)
            kd, vd = _dup_kv(kp_ref, kc_ref, kh), _dup_kv(vp_ref, vc_ref, kh)
            sc = lax.dot_general(qs, kd, NT, preferred_element_type=F32) * ATTN_SCALE
            sc = jnp.where(mask, sc, NEG)
            sink = _sink_col(sinks_ref, kh)
            m = jnp.maximum(jnp.max(sc, axis=1, keepdims=True), sink)
            p = jnp.exp(sc - m)
            l = jnp.sum(p, axis=1, keepdims=True) + jnp.exp(sink - m)
            o = jnp.dot(p.astype(BF16), vd, preferred_element_type=F32) / l
            lse = m + jnp.log(l)
            for pair in range(2):
                lo = o[(2 * pair) * BLOCK:(2 * pair + 1) * BLOCK]
                hi = o[(2 * pair + 1) * BLOCK:(2 * pair + 2) * BLOCK]
                col = (kh * 2 + pair) * 128
                o_ref[:, col:col + 128] = jnp.where(lower, lo, hi).astype(BF16)
            for g in range(4):
                lse_out = jnp.where(lane == kh * 4 + g, lse[g * BLOCK:(g + 1) * BLOCK], lse_out)
        lse_ref[...] = lse_out

    return _pcall(
        body, "attn_fwd", (s // BLOCK,),
        [pl.BlockSpec(memory_space=pltpu.SMEM)] + _attn_specs(),
        [pl.BlockSpec((BLOCK, ATTN_W), lambda n: (n, 0)), pl.BlockSpec((BLOCK, 128), lambda n: (n, 0))],
        [_sds((s, ATTN_W), BF16), _sds((s, 128), F32)],
        (sinks, qkv, qkv, qkv, qkv, qkv), comm=comm)


def _conv_u(cbx_ref, halo_ref, w_ref, first):
    cb = cbx_ref[:, 0:CONV_W].astype(F32)
    cc = cbx_ref[:, CONV_W:2 * CONV_W].astype(F32)
    cx = cbx_ref[:, 2 * CONV_W:3 * CONV_W].astype(F32)
    u = cc * cx
    uh = halo_ref[:, CONV_W:2 * CONV_W].astype(F32) * halo_ref[:, 2 * CONV_W:3 * CONV_W].astype(F32)
    uh = jnp.where(first, 0.0, uh)
    u1, u2 = _shifts_down(u, uh, (1, 2))
    cv = w_ref[0:1, :] * u2 + w_ref[1:2, :] * u1 + w_ref[2:3, :] * u
    return cb, cc, cx, u, cv


def _mix_fwd(x, cbx, gates, attn, conv_w, wa, wc, wout, comm):
    s = x.shape[0]
    tm = _row_tile(s)

    def body(x_ref, cbx_ref, halo_ref, gate_ref, attn_ref, cw_ref, wa_ref, wc_ref, wo_ref,
             h1_ref):
        first = pl.program_id(0) == 0
        cb, _, _, _, cv = _conv_u(cbx_ref, halo_ref, cw_ref, first)
        conv = (cb * cv).astype(BF16)
        ap = jnp.dot(attn_ref[...], wa_ref[...], preferred_element_type=F32)
        cp = jnp.dot(conv, wc_ref[...], preferred_element_type=F32)
        ga = gate_ref[:, 0:D_MODEL].astype(F32)
        gc = gate_ref[:, D_MODEL:2 * D_MODEL].astype(F32)
        merged = (_sig(ga) * ap + _sig(gc) * cp).astype(BF16)
        h1_ref[...] = x_ref[...] + jnp.dot(merged, wo_ref[...], preferred_element_type=F32)

    return _pcall(
        body, "mix_fwd", (s // tm,),
        [_rows(tm, D_MODEL), _rows(tm, CBX_W), pl.BlockSpec((HALO, CBX_W), _prev_halo_map(tm)),
         _rows(tm, GATE_W), _rows(tm, ATTN_W), _full((3, CONV_W)), _full((ATTN_W, D_MODEL)),
         _full((CONV_W, D_MODEL)), _full((D_MODEL, D_MODEL))],
        [_rows(tm, D_MODEL)], [_sds((s, D_MODEL), F32)],
        (x, cbx, cbx, gates, attn, conv_w, wa, wc, wout), comm=comm)


def _ffn_up(h1, g, wup_lo, wup_hi, fcw, comm):
    s = h1.shape[0]
    tm = _row_tile(s)
    half = D_MODEL // 2

    def body(h_ref, g_ref, wl_ref, wh_ref, fcw_ref, hn_ref, pre_ref, up_ref, carry_ref):
        @pl.when(pl.program_id(0) == 0)
        def _():
            carry_ref[...] = jnp.zeros_like(carry_ref)

        hv = h_ref[...]
        r = lax.rsqrt(jnp.mean(hv * hv, axis=-1, keepdims=True) + NORM_EPS)
        hn = (hv * r * g_ref[...]).astype(BF16)
        hn_ref[...] = hn
        for c in range(2 * D_FF // FF_CHUNK):
            sl = slice(c * FF_CHUNK, (c + 1) * FF_CHUNK)
            acc = lax.dot_general(hn[:, :half], wl_ref[sl, :], NT, preferred_element_type=F32)
            acc = acc + lax.dot_general(hn[:, half:], wh_ref[sl, :], NT, preferred_element_type=F32)
            pre_ref[:, sl] = acc.astype(BF16)
            halo = carry_ref[:, sl]
            carry_ref[:, sl] = acc[tm - HALO:, :]
            u1, u2 = _shifts_down(acc, halo, (1, 2))
            w = fcw_ref[:, sl]
            up_ref[:, sl] = (w[0:1] * u2 + w[1:2] * u1 + w[2:3] * acc).astype(BF16)

    return _pcall(
        body, "ffn_up", (s // tm,),
        [_rows(tm, D_MODEL), _full((1, D_MODEL)), _resident((2 * D_FF, half)), _resident((2 * D_FF, half)),
         _full((3, 2 * D_FF))],
        [_rows(tm, D_MODEL), _rows(tm, 2 * D_FF), _rows(tm, 2 * D_FF)],
        [_sds((s, D_MODEL), BF16), _sds((s, 2 * D_FF), BF16), _sds((s, 2 * D_FF), BF16)],
        (h1, g, wup_lo, wup_hi, fcw), scratch=[pltpu.VMEM((HALO, 2 * D_FF), F32)], comm=comm)


def _ffn_down_loss(up, wdown, h1, fnorm, target):
    s = h1.shape[0]
    tm = _row_tile(s)

    def body(up_ref, wd_ref, h1_ref, fn_ref, t_ref, act_ref, dh2_ref, loss_ref, dfn_ref):
        i = pl.program_id(0)

        @pl.when(i == 0)
        def _():
            loss_ref[...] = jnp.zeros_like(loss_ref)
            dfn_ref[...] = jnp.zeros_like(dfn_ref)

        h2 = h1_ref[...]
        for c in range(D_FF // FF_CHUNK):
            gsl = slice(c * FF_CHUNK, (c + 1) * FF_CHUNK)
            vsl = slice(D_FF + c * FF_CHUNK, D_FF + (c + 1) * FF_CHUNK)
            gate = up_ref[:, gsl].astype(F32)
            val = up_ref[:, vsl].astype(F32)
            act = (gate * _sig(gate) * val).astype(BF16)
            act_ref[:, gsl] = act
            h2 = h2 + jnp.dot(act, wd_ref[gsl, :], preferred_element_type=F32)
        r = lax.rsqrt(jnp.mean(h2 * h2, axis=-1, keepdims=True) + NORM_EPS)
        yhat = h2 * r
        fn = fn_ref[...]
        diff = yhat * fn - t_ref[...]
        loss_ref[...] += 0.5 * jnp.sum(jnp.sum(diff * diff, axis=1, keepdims=True), axis=0, keepdims=True) / D_MODEL
        dy = diff * (1.0 / D_MODEL)
        dfn_ref[...] += jnp.sum(dy * yhat, axis=0, keepdims=True)
        dyh = dy * fn
        dh2_ref[...] = r * (dyh - yhat * jnp.mean(dyh * yhat, axis=-1, keepdims=True))

    return _pcall(
        body, "ffn_down_loss", (s // tm,),
        [_rows(tm, 2 * D_FF), _resident((D_FF, D_MODEL)), _rows(tm, D_MODEL), _full((1, D_MODEL)),
         _rows(tm, D_MODEL)],
        [_rows(tm, D_FF), _rows(tm, D_MODEL), _full((1, 128)), _full((1, D_MODEL))],
        [_sds((s, D_FF), BF16), _sds((s, D_MODEL), F32), _sds((1, 128), F32), _sds((1, D_MODEL), F32)],
        (up, wdown, h1, fnorm, target))[0]


def _ffn_bwd(dh2, wdown, up, up_pre, fcw, wup_lo, wup_hi, h1, g, comm):
    s = dh2.shape[0]
    tm = _row_tile(s)
    half = D_MODEL // 2

    chunk = FF_GRAD_ROWS

    def dup_cols(dh, up_ref, wd_ref, c):
        gsl = slice(c * chunk, (c + 1) * chunk)
        vsl = slice(D_FF + c * chunk, D_FF + (c + 1) * chunk)
        dact = lax.dot_general(dh, wd_ref[gsl, :], NT, preferred_element_type=F32)
        gate = up_ref[:, gsl].astype(F32)
        val = up_ref[:, vsl].astype(F32)
        sg = _sig(gate)
        return dact * val * (sg * (1.0 + gate * (1.0 - sg))), dact * gate * sg

    def body(dh_ref, wd_ref, up_ref, x_ref, w_ref, wl_ref, wh_ref, h_ref, g_ref,
             dx_ref, dw_ref, dh1_ref, dg_ref, carry_ref):
        @pl.when(pl.program_id(0) == 0)
        def _():
            dw_ref[...] = jnp.zeros_like(dw_ref)
            dg_ref[...] = jnp.zeros_like(dg_ref)
            carry_ref[...] = jnp.zeros_like(carry_ref)

        dh2v = dh_ref[...]
        dh = dh2v.astype(BF16)
        dhn_lo = jnp.zeros((tm, half), F32)
        dhn_hi = jnp.zeros((tm, half), F32)
        for c in range(D_FF // chunk):
            for d, off in zip(dup_cols(dh, up_ref, wd_ref, c), (c * chunk, D_FF + c * chunk)):
                sl = slice(off, off + chunk)
                dn = carry_ref[:, sl]
                carry_ref[:, sl] = d[0:HALO, :]
                xv = x_ref[:, sl].astype(F32)
                wv = w_ref[:, sl]
                d1, d2 = _shifts_up(d, dn, (1, 2))
                dx = (wv[2:3] * d + wv[1:2] * d1 + wv[0:1] * d2).astype(BF16)
                dx_ref[:, sl] = dx
                dhn_lo = dhn_lo + jnp.dot(dx, wl_ref[sl, :], preferred_element_type=F32)
                dhn_hi = dhn_hi + jnp.dot(dx, wh_ref[sl, :], preferred_element_type=F32)
                dw_ref[0:1, sl] += jnp.sum(d2 * xv, axis=0, keepdims=True)
                dw_ref[1:2, sl] += jnp.sum(d1 * xv, axis=0, keepdims=True)
                dw_ref[2:3, sl] += jnp.sum(d * xv, axis=0, keepdims=True)
        dx1, dg = _norm_bwd_tile(h_ref[...], g_ref[...], jnp.concatenate([dhn_lo, dhn_hi], axis=1))
        dg_ref[...] += dg
        dh1_ref[...] = dh2v + dx1

    rows = lambda c: _rows_reversed(tm, c, s // tm)
    return _pcall(
        body, "ffn_bwd", (s // tm,),
        [rows(D_MODEL), _resident((D_FF, D_MODEL)), rows(2 * D_FF), rows(2 * D_FF), _full((3, 2 * D_FF)),
         _resident((2 * D_FF, half)), _resident((2 * D_FF, half)), rows(D_MODEL), _full((1, D_MODEL))],
        [rows(2 * D_FF), _full((3, 2 * D_FF)), rows(D_MODEL), _full((1, D_MODEL))],
        [_sds((s, 2 * D_FF), BF16), _sds((3, 2 * D_FF), F32), _sds((s, D_MODEL), F32), _sds((1, D_MODEL), F32)],
        (dh2, wdown, up, up_pre, fcw, wup_lo, wup_hi, h1, g),
        scratch=[pltpu.VMEM((HALO, 2 * D_FF), F32)], comm=comm)


def _matmul_tn(a, b, tk, name, ts=1024, comm=None):
    s, ka = a.shape
    n = b.shape[1]
    ts = min(ts, s)
    steps = s // ts

    def body(a_ref, b_ref, o_ref, acc_ref):
        j = pl.program_id(1)

        @pl.when(j == 0)
        def _():
            acc_ref[...] = jnp.zeros_like(acc_ref)

        acc_ref[...] += lax.dot_general(a_ref[...].astype(BF16), b_ref[...].astype(BF16), TN,
                                        preferred_element_type=F32)

        @pl.when(j == steps - 1)
        def _():
            o_ref[...] = acc_ref[...].astype(BF16)

    outs, couts = _pcall(
        body, name, (ka // tk, steps),
        [pl.BlockSpec((ts, tk), lambda i, j: (j, i)), pl.BlockSpec((ts, n), lambda i, j: (j, 0))],
        [pl.BlockSpec((tk, n), lambda i, j: (i, 0))], [_sds((ka, n), BF16)],
        (a, b), scratch=[pltpu.VMEM((tk, n), F32)], comm=comm)
    return outs[0] if comm is None else (outs[0], couts)


def _norm_bwd_tile(xv, g, dy):
    r = lax.rsqrt(jnp.mean(xv * xv, axis=-1, keepdims=True) + NORM_EPS)
    xhat = xv * r
    dg = jnp.sum(dy * xhat, axis=0, keepdims=True)
    dyh = dy * g
    return r * (dyh - xhat * jnp.mean(dyh * xhat, axis=-1, keepdims=True)), dg


def _mix_bwd(dh1, wout, gates, attn, wa, wc, cbx, conv_w, comm):
    s = dh1.shape[0]
    tm = _row_tile(s)
    steps = s // tm

    def body(dh_ref, wo_ref, gate_ref, attn_ref, wa_ref, wc_ref, cbx_ref, halo_ref,
             cw_ref, dg_ref, dattn_ref, dcb_ref, dcc_ref, dcx_ref, dw_ref, gwo_ref, gwa_ref, gwc_ref,
             acc_o, acc_a, acc_c, carry_ref):
        i = pl.program_id(0)

        @pl.when(i == 0)
        def _():
            dw_ref[...] = jnp.zeros_like(dw_ref)
            acc_o[...] = jnp.zeros_like(acc_o)
            acc_a[...] = jnp.zeros_like(acc_a)
            acc_c[...] = jnp.zeros_like(acc_c)
            carry_ref[...] = jnp.zeros_like(carry_ref)

        cb, cc, cx, u, cv = _conv_u(cbx_ref, halo_ref, cw_ref, i == steps - 1)
        attn = attn_ref[...]
        conv = (cb * cv).astype(BF16)
        ap = jnp.dot(attn, wa_ref[...], preferred_element_type=F32)
        cp = jnp.dot(conv, wc_ref[...], preferred_element_type=F32)
        dhb = dh_ref[...].astype(BF16)
        dm = lax.dot_general(dhb, wo_ref[...], NT, preferred_element_type=F32)
        sa = _sig(gate_ref[:, 0:D_MODEL].astype(F32))
        sc = _sig(gate_ref[:, D_MODEL:2 * D_MODEL].astype(F32))
        merged = (sa * ap + sc * cp).astype(BF16)
        da = (dm * sa).astype(BF16)
        dc = (dm * sc).astype(BF16)
        dg_ref[:, 0:D_MODEL] = (dm * ap * sa * (1.0 - sa)).astype(BF16)
        dg_ref[:, D_MODEL:2 * D_MODEL] = (dm * cp * sc * (1.0 - sc)).astype(BF16)
        dattn_ref[...] = lax.dot_general(da, wa_ref[...], NT, preferred_element_type=F32).astype(BF16)
        dconv = lax.dot_general(dc, wc_ref[...], NT, preferred_element_type=F32)
        dcb_ref[...] = (dconv * cv).astype(BF16)
        d = dconv * cb
        dn = carry_ref[...]
        carry_ref[...] = d[0:HALO, :]
        d1, d2 = _shifts_up(d, dn, (1, 2))
        du = cw_ref[2:3, :] * d + cw_ref[1:2, :] * d1 + cw_ref[0:1, :] * d2
        dcc_ref[...] = (du * cx).astype(BF16)
        dcx_ref[...] = (du * cc).astype(BF16)
        dw_ref[0:1, :] += jnp.sum(d2 * u, axis=0, keepdims=True)
        dw_ref[1:2, :] += jnp.sum(d1 * u, axis=0, keepdims=True)
        dw_ref[2:3, :] += jnp.sum(d * u, axis=0, keepdims=True)
        acc_o[...] += lax.dot_general(merged, dhb, TN, preferred_element_type=F32)
        acc_a[...] += lax.dot_general(attn, da, TN, preferred_element_type=F32)
        acc_c[...] += lax.dot_general(conv, dc, TN, preferred_element_type=F32)

        @pl.when(i == steps - 1)
        def _():
            gwo_ref[...] = acc_o[...].astype(BF16)
            gwa_ref[...] = acc_a[...].astype(BF16)
            gwc_ref[...] = acc_c[...].astype(BF16)

    rows = lambda c: _rows_reversed(tm, c, steps)
    return _pcall(
        body, "mix_bwd", (steps,),
        [rows(D_MODEL), _full((D_MODEL, D_MODEL)), rows(GATE_W), rows(ATTN_W), _full((ATTN_W, D_MODEL)),
         _full((CONV_W, D_MODEL)), rows(CBX_W), pl.BlockSpec((HALO, CBX_W), _prev_halo_map_reversed(tm, steps)),
         _full((3, CONV_W))],
        [rows(GATE_W), rows(ATTN_W), rows(CONV_W), rows(CONV_W), rows(CONV_W),
         _full((3, CONV_W)), _full((D_MODEL, D_MODEL)), _full((ATTN_W, D_MODEL)), _full((CONV_W, D_MODEL))],
        [_sds((s, GATE_W), BF16), _sds((s, ATTN_W), BF16), _sds((s, CONV_W), BF16), _sds((s, CONV_W), BF16),
         _sds((s, CONV_W), BF16), _sds((3, CONV_W), F32), _sds((D_MODEL, D_MODEL), BF16),
         _sds((ATTN_W, D_MODEL), BF16), _sds((CONV_W, D_MODEL), BF16)],
        (dh1, wout, gates, attn, wa, wc, cbx, cbx, conv_w),
        scratch=[pltpu.VMEM((D_MODEL, D_MODEL), F32), pltpu.VMEM((ATTN_W, D_MODEL), F32),
                 pltpu.VMEM((CONV_W, D_MODEL), F32), pltpu.VMEM((HALO, CONV_W), F32)], comm=comm)


def _attn_bwd(qkv, sinks, attn, lse, dattn, comm):
    s = qkv.shape[0]

    def body(sinks_ref, q_ref, kp_ref, kc_ref, vp_ref, vc_ref, o_ref, lse_ref, do_ref,
             dq_ref, dk_ref, dv_ref, ds_ref):
        n = pl.program_id(0)

        @pl.when(n == 0)
        def _():
            dk_ref[...] = jnp.zeros_like(dk_ref)
            dv_ref[...] = jnp.zeros_like(dv_ref)
            ds_ref[...] = jnp.zeros_like(ds_ref)

        mask = _attn_mask(n)
        lower = _lower_lanes()
        lane = lax.broadcasted_iota(jnp.int32, (BLOCK, 128), 1)
        lower2 = lax.broadcasted_iota(jnp.int32, (2 * BLOCK, 128), 1) < HEAD_DIM
        lane1 = lax.broadcasted_iota(jnp.int32, (1, 128), 1)
        qv, ov, dov, lsev = q_ref[...], o_ref[...], do_ref[...], lse_ref[...]
        dk_fold, dv_fold = [], []
        dsink = jnp.zeros((1, 128), F32)
        for kh in range(2):
            qs = _stack_heads(qv, kh)
            dos = _stack_heads(dov, kh)
            os_ = _stack_heads(ov, kh)
            kd, vd = _dup_kv(kp_ref, kc_ref, kh), _dup_kv(vp_ref, vc_ref, kh)
            lse = jnp.concatenate(
                [jnp.sum(jnp.where(lane == kh * 4 + g, lsev, 0.0), axis=1, keepdims=True) for g in range(4)], axis=0)
            sc = lax.dot_general(qs, kd, NT, preferred_element_type=F32) * ATTN_SCALE
            p = jnp.exp(jnp.where(mask, sc, NEG) - lse)
            dp = lax.dot_general(dos, vd, NT, preferred_element_type=F32)
            delta = jnp.sum(dos.astype(F32) * os_.astype(F32), axis=1, keepdims=True)
            dsc = (p * (dp - delta) * ATTN_SCALE).astype(BF16)
            dqs = jnp.dot(dsc, kd, preferred_element_type=F32)
            for pair in range(2):
                lo = dqs[(2 * pair) * BLOCK:(2 * pair + 1) * BLOCK]
                hi = dqs[(2 * pair + 1) * BLOCK:(2 * pair + 2) * BLOCK]
                col = (kh * 2 + pair) * 128
                dq_ref[:, col:col + 128] = jnp.where(lower, lo, hi).astype(BF16)
            dkd = lax.dot_general(dsc, qs, TN, preferred_element_type=F32)
            dvd = lax.dot_general(p.astype(BF16), dos, TN, preferred_element_type=F32)
            dk_fold.append(dkd + pltpu.roll(dkd, HEAD_DIM, axis=1))
            dv_fold.append(dvd + pltpu.roll(dvd, HEAD_DIM, axis=1))
            psink = jnp.exp(_sink_col(sinks_ref, kh) - lse) * delta
            for g in range(4):
                tot = jnp.sum(psink[g * BLOCK:(g + 1) * BLOCK], axis=0, keepdims=True)
                dsink = dsink - jnp.where(lane1 == kh * 4 + g, tot, 0.0)
        dk2 = jnp.where(lower2, dk_fold[0], dk_fold[1])
        dv2 = jnp.where(lower2, dv_fold[0], dv_fold[1])
        ds_ref[...] += dsink
        cur = pl.ds(pl.multiple_of(n * BLOCK, BLOCK), BLOCK)
        dk_ref[cur, :] += dk2[BLOCK:]
        dv_ref[cur, :] += dv2[BLOCK:]

        @pl.when(n > 0)
        def _():
            prev = pl.ds(pl.multiple_of((n - 1) * BLOCK, BLOCK), BLOCK)
            dk_ref[prev, :] += dk2[:BLOCK]
            dv_ref[prev, :] += dv2[:BLOCK]

    blk = lambda w: pl.BlockSpec((BLOCK, w), lambda n: (n, 0))
    return _pcall(
        body, "attn_bwd", (s // BLOCK,),
        [pl.BlockSpec(memory_space=pltpu.SMEM)] + _attn_specs() + [blk(ATTN_W), blk(128), blk(ATTN_W)],
        [blk(ATTN_W), _full((s, KV_W)), _full((s, KV_W)), _full((1, 128))],
        [_sds((s, ATTN_W), BF16), _sds((s, KV_W), F32), _sds((s, KV_W), F32), _sds((1, 128), F32)],
        (sinks, qkv, qkv, qkv, qkv, qkv, attn, lse, dattn), comm=comm)


DPROJ_PIECES = (ATTN_W, KV_W, KV_W, CONV_W, CONV_W, CONV_W, GATE_W)
DPROJ_OFFSETS = tuple(sum(DPROJ_PIECES[:k]) for k in range(len(DPROJ_PIECES)))


def _grad_w_in(pieces, xn, comm):
    s = xn.shape[0]
    ts = min(1024, s)
    steps = s // ts
    rows0 = DPROJ_OFFSETS[6]

    def body(*refs):
        p_refs, b_ref, o_ref, acc_ref, stage_ref, sem = refs[:7], refs[7], refs[8], refs[9], refs[10], refs[11]
        i, j = pl.program_id(0), pl.program_id(1)

        @pl.when(j == 0)
        def _():
            acc_ref[...] = jnp.zeros_like(acc_ref)

        bv = b_ref[...]

        def flush(lo, n):
            stage_ref[0:n, :] = acc_ref[0:n, :].astype(BF16)
            cp = pltpu.make_async_copy(stage_ref.at[0:n, :], o_ref.at[lo:lo + n, :], sem)
            cp.start()
            cp.wait()

        @pl.when(i == 0)
        def _():
            for p_ref, off, w in zip(p_refs[:6], DPROJ_OFFSETS[:6], DPROJ_PIECES[:6]):
                acc_ref[off:off + w, :] += lax.dot_general(p_ref[...].astype(BF16), bv, TN,
                                                           preferred_element_type=F32)

            @pl.when(j == steps - 1)
            def _():
                flush(0, rows0)

        @pl.when(i == 1)
        def _():
            acc_ref[0:GATE_W, :] += lax.dot_general(p_refs[6][...], bv, TN, preferred_element_type=F32)

            @pl.when(j == steps - 1)
            def _():
                flush(rows0, GATE_W)

    def piece_spec(w, group):
        return pl.BlockSpec((ts, w), lambda i, j: (jnp.where(i == group, j, 0), 0))

    outs, couts = _pcall(
        body, "grad_w_in", (2, steps),
        [piece_spec(w, 0) for w in DPROJ_PIECES[:6]] + [piece_spec(GATE_W, 1),
                                                         pl.BlockSpec((ts, D_MODEL), lambda i, j: (j, 0))],
        [ANY], [_sds((IN_W, D_MODEL), BF16)], (*pieces, xn),
        scratch=[pltpu.VMEM((rows0, D_MODEL), F32), pltpu.VMEM((rows0, D_MODEL), BF16), pltpu.SemaphoreType.DMA],
        comm=comm)
    return outs[0], couts


def _inproj_bwd(pieces, win_t, x, g, dh1, comm):
    s = x.shape[0]
    tm = _row_tile(s, 512)

    def body(*refs):
        p_refs = refs[:7]
        w_ref, x_ref, g_ref, dh_ref, dx_ref, db_ref, dg_ref = refs[7:]

        @pl.when(pl.program_id(0) == 0)
        def _():
            db_ref[...] = jnp.zeros_like(db_ref)
            dg_ref[...] = jnp.zeros_like(dg_ref)

        dxn = jnp.zeros((tm, D_MODEL), F32)
        for p_ref, off, w in zip(p_refs, DPROJ_OFFSETS, DPROJ_PIECES):
            v = p_ref[...].astype(BF16)
            db_ref[:, off:off + w] += jnp.sum(v.astype(F32), axis=0, keepdims=True)
            dxn = dxn + jnp.dot(v, w_ref[off:off + w, :], preferred_element_type=F32)
        dx, dg = _norm_bwd_tile(x_ref[...], g_ref[...], dxn)
        dg_ref[...] += dg
        dx_ref[...] = dh_ref[...] + dx

    return _pcall(
        body, "inproj_bwd", (s // tm,),
        [_rows(tm, w) for w in DPROJ_PIECES] + [_resident((IN_W, D_MODEL)), _rows(tm, D_MODEL), _full((1, D_MODEL)),
                                                _rows(tm, D_MODEL)],
        [_rows(tm, D_MODEL), _full((8, IN_W)), _full((8, D_MODEL))],
        [_sds((s, D_MODEL), F32), _sds((8, IN_W), F32), _sds((8, D_MODEL), F32)],
        (*pieces, win_t, x, g, dh1), comm=comm)


def _adam_math(w, g, m, v):
    m2 = ADAM_B1 * m + (1.0 - ADAM_B1) * g
    v2 = ADAM_B2 * v + (1.0 - ADAM_B2) * (g * g)
    m_hat = m2 / (1.0 - ADAM_B1 ** ADAM_STEP)
    v_hat = v2 / (1.0 - ADAM_B2 ** ADAM_STEP)
    delta = -ADAM_LR * (m_hat / (jnp.sqrt(v_hat) + ADAM_EPS) + ADAM_WD * w)
    return delta, m2, v2


def _sum_slots(ref):
    tot = ref[0].astype(F32)
    for i in range(1, ref.shape[0]):
        tot = tot + ref[i].astype(F32)
    return tot


def _pair_add(partials, theirs, tr, name):
    r = partials.shape[0] // N_DEV
    c = partials.shape[1]
    nt = r // tr
    core = lax.axis_index("c").astype(jnp.int32).reshape(1)

    def body(core_ref, a_ref, b_ref, o_ref):
        o_ref[...] = (a_ref[...].astype(F32) + b_ref[...].astype(F32)).astype(BF16)

    grid_spec = pltpu.PrefetchScalarGridSpec(
        num_scalar_prefetch=1, grid=(4 * nt,),
        in_specs=[pl.BlockSpec((None, None, tr, c), lambda i, core_ref: (i // nt, core_ref[0], i % nt, 0)),
                  pl.BlockSpec((tr, c), lambda i, core_ref: (i, 0))],
        out_specs=pl.BlockSpec((tr, c), lambda i, core_ref: (i, 0)))
    return pl.pallas_call(body, name=name, grid_spec=grid_spec, out_shape=_sds((4 * r, c), BF16))(
        core, partials.reshape(4, 2, r, c), theirs)


def _sum_adamw(parts, w, m, v, tr, name):
    r, c = w.shape

    def body(p_ref, w_ref, m_ref, v_ref, g_ref, d_ref, m2_ref, v2_ref):
        g = _sum_slots(p_ref)
        g_ref[...] = g
        d_ref[...], m2_ref[...], v2_ref[...] = _adam_math(w_ref[...], g, m_ref[...], v_ref[...])

    spec = pl.BlockSpec((tr, c), lambda i: (i, 0))
    return _pcall(body, name, (r // tr,), [pl.BlockSpec((N_DEV, tr, c), lambda i: (0, i, 0)), spec, spec, spec],
                  [spec] * 4, [_sds((r, c), F32)] * 4, (parts, w, m, v))[0]


def _sum_parts_adamw(parts, w, m, v, tr, name):
    c = w.shape[1]
    tiles = [p.shape[1] // tr for p in parts]
    starts = [sum(tiles[:k]) for k in range(len(parts))]
    n_parts = len(parts)

    def body(*refs):
        p_refs = refs[:n_parts]
        w_ref, m_ref, v_ref, g_ref, d_ref, m2_ref, v2_ref = refs[n_parts:]
        i = pl.program_id(0)
        for p_ref, st, nt in zip(p_refs, starts, tiles):
            @pl.when(jnp.logical_and(i >= st, i < st + nt))
            def _(p_ref=p_ref):
                g_ref[...] = _sum_slots(p_ref)

        d_ref[...], m2_ref[...], v2_ref[...] = _adam_math(w_ref[...], g_ref[...], m_ref[...], v_ref[...])

    def part_spec(p, st, nt):
        return pl.BlockSpec((p.shape[0], tr, c), lambda i: (0, jnp.clip(i - st, 0, nt - 1), 0))

    spec = pl.BlockSpec((tr, c), lambda i: (i, 0))
    return _pcall(
        body, name, (sum(tiles),),
        [part_spec(p, st, nt) for p, st, nt in zip(parts, starts, tiles)] + [spec, spec, spec],
        [spec] * 4, [_sds(w.shape, F32)] * 4, (*parts, w, m, v))[0]


ROW_MIX, ROW_FFN, ROW_FINAL, ROW_SINKS, ROW_LOSS, ROW_BIN, ROW_CW, ROW_FCW = 0, 1, 2, 3, 4, 5, 10, 13
FCW_ROWS = 6


def _wide_pieces(width):
    return [(k * D_MODEL, min(D_MODEL, width - k * D_MODEL)) for k in range(-(-width // D_MODEL))]


def _pack_small(dffn, dfn, dsink, loss, dcw, dfcw):
    def body(ffn_ref, fn_ref, sink_ref, loss_ref, cw_ref, fcw_ref, o_ref):
        o_ref[...] = jnp.zeros_like(o_ref)
        o_ref[ROW_FFN:ROW_FFN + 1, :] = ffn_ref[...]
        o_ref[ROW_FINAL:ROW_FINAL + 1, :] = fn_ref[...]
        o_ref[ROW_SINKS:ROW_SINKS + 1, 0:128] = sink_ref[...]
        o_ref[ROW_LOSS:ROW_LOSS + 1, 0:128] = loss_ref[...]
        o_ref[ROW_CW:ROW_CW + 3, 0:CONV_W] = cw_ref[...]
        for a in range(3):
            for k, (off, w) in enumerate(_wide_pieces(2 * D_FF)):
                row = ROW_FCW + FCW_ROWS * a + k
                o_ref[row:row + 1, 0:w] = fcw_ref[a:a + 1, off:off + w]

    return pl.pallas_call(body, name="pack_small", out_shape=_sds((SMALL_ROWS, D_MODEL), F32))(
        dffn, dfn, dsink, loss, dcw, dfcw)


def _small_sums_adamw(r_small, r_dmix, r_dbin, params):
    rows = (None, None, ROW_SINKS, ROW_FFN, ROW_FINAL)

    def sum_row0(ref):
        tot = ref[0:1, :]
        for i in range(1, N_DEV):
            tot = tot + ref[8 * i:8 * i + 1, :]
        return tot

    def body(*refs):
        r_ref, late_refs, p_refs, o_refs = refs[0], refs[1:3], refs[3:18], refs[18:]
        tot = _sum_slots(r_ref)
        for k, row in enumerate(rows):
            w_ref, m_ref, v_ref = p_refs[3 * k:3 * k + 3]
            g_ref, d_ref, m2_ref, v2_ref = o_refs[4 * k:4 * k + 4]
            if row is None:
                g_ref[...] = sum_row0(late_refs[k])
            else:
                for j, (off, w) in enumerate(_wide_pieces(w_ref.shape[1])):
                    g_ref[:, off:off + w] = tot[row + j:row + j + 1, 0:w]
            d_ref[...], m2_ref[...], v2_ref[...] = _adam_math(w_ref[...], g_ref[...], m_ref[...], v_ref[...])
        cw_ref, fcw_ref, loss_ref = o_refs[20:]
        cw_ref[...] = tot[ROW_CW:ROW_CW + 3, 0:CONV_W]
        for a in range(3):
            for j, (off, w) in enumerate(_wide_pieces(2 * D_FF)):
                row = ROW_FCW + FCW_ROWS * a + j
                fcw_ref[a:a + 1, off:off + w] = tot[row:row + 1, 0:w]
        loss_ref[...] = tot[ROW_LOSS:ROW_LOSS + 1, 0:128]

    flat = [t for p in params for t in p]
    out_shape = [_sds(p[0].shape, F32) for p in params for _ in range(4)]
    out_shape += [_sds((3, CONV_W), F32), _sds((3, 2 * D_FF), F32), _sds((1, 128), F32)]
    res = pl.pallas_call(body, name="small_sums_adamw", out_shape=out_shape)(r_small, r_dmix, r_dbin, *flat)
    return [tuple(res[4 * k:4 * k + 4]) for k in range(5)], res[20], res[21], res[22]


def _adamw_pair(a, b):
    def body(*refs):
        for k in range(2):
            w_ref, g_ref, m_ref, v_ref = refs[4 * k:4 * k + 4]
            d_ref, m2_ref, v2_ref = refs[8 + 3 * k:8 + 3 * k + 3]
            d_ref[...], m2_ref[...], v2_ref[...] = _adam_math(w_ref[...], g_ref[...], m_ref[...], v_ref[...])

    out_shape = [_sds(a[0].shape, F32)] * 3 + [_sds(b[0].shape, F32)] * 3
    res = pl.pallas_call(body, name="adamw_conv_weights", out_shape=out_shape)(*a, *b)
    return tuple(res[:3]), tuple(res[3:])


def _pad_cols(a, c):
    return jnp.pad(a, ((0, 0), (0, c - a.shape[1])))


def _to_col_slabs(g):
    r = g.shape[0]
    return jnp.transpose(g.reshape(r, N_DEV, 128), (1, 0, 2)).reshape(N_DEV * r, 128)


def _from_col_slabs(t):
    r = t.shape[0] // N_DEV
    return jnp.transpose(t.reshape(N_DEV, r, 128), (1, 0, 2)).reshape(r, N_DEV * 128)


def _slots(t):
    return t.reshape(N_DEV, t.shape[0] // N_DEV, t.shape[1])


def kernel(x, mix_norm, w_in, b_in, sinks, conv_w, w_attn_branch, w_conv_branch, w_out, ffn_norm, w_up, ffn_conv_w, w_down, final_norm, loss_target, m_mix_norm, m_w_in, m_b_in, m_sinks, m_conv_w, m_w_attn_branch, m_w_conv_branch, m_w_out, m_ffn_norm, m_w_up, m_ffn_conv_w, m_w_down, m_final_norm, v_mix_norm, v_w_in, v_b_in, v_sinks, v_conv_w, v_w_attn_branch, v_w_conv_branch, v_w_out, v_ffn_norm, v_w_up, v_ffn_conv_w, v_w_down, v_final_norm):
    xs, tgt = x[0], loss_target[0]
    me = 4 * lax.axis_index("x") + 2 * lax.axis_index("y") + lax.axis_index("c")
    in_rows, up_rows = IN_W // N_DEV, 2 * D_FF // N_DEV

    conv_sh = jnp.concatenate([_pad_cols(ffn_conv_w[0], 768), _pad_cols(conv_w[0], 768),
                               jnp.zeros((2, 768), F32)], axis=0)
    win_sh, wup_sh = w_in[0].T.astype(BF16), w_up[0].T.astype(BF16)
    wout_sh, wdown_sh = w_out[0].astype(BF16), w_down[0].astype(BF16)
    wa_sh, wc_sh = w_attn_branch[0].astype(BF16), w_conv_branch[0].astype(BF16)

    half = D_MODEL // 2
    (win_t,) = _exchange_only(_AllGather([win_sh]), "gather_w_in")
    (xn, qkv, cbx, gates), (wa_s, wc_s, wout, conv_g) = _norm_inproj(
        xs, mix_norm, win_t, b_in, _AllGather([wa_sh, wc_sh, wout_sh, conv_sh], pass_on_at=0.875))
    (attn, lse), (wup_lo,) = _attn_fwd(qkv, sinks, _AllGather([wup_sh[:, :half]], pass_on_at=0.875))
    wa, wc = _from_col_slabs(wa_s), _from_col_slabs(wc_s)
    conv_g = conv_g.reshape(N_DEV, 8, 768)
    fcw = jnp.transpose(conv_g[:, 0:3, :up_rows], (1, 0, 2)).reshape(3, 2 * D_FF)
    cw = jnp.transpose(conv_g[:, 3:6, :CONV_W // N_DEV], (1, 0, 2)).reshape(3, CONV_W)
    (h1,), (wup_hi,) = _mix_fwd(xs, cbx, gates, attn, cw, wa, wc, wout,
                                _AllGather([wup_sh[:, half:]], pass_on_at=0.875))
    (hn, up_pre, up), (wdown,) = _ffn_up(h1, ffn_norm, wup_lo, wup_hi, fcw,
                                         _AllGather([wdown_sh], pass_on_at=0.75))
    act, dh2, loss_p, dfn_p = _ffn_down_loss(up, wdown, h1, final_norm.reshape(1, D_MODEL), tgt)

    dn_rows, q_up = D_FF // N_DEV, up_rows // 4
    g_wdown = _matmul_tn(act, dh2, FF_GRAD_ROWS, "grad_w_down")
    (dup_pre, dfcw_p, dh1, dffn_p), (r_wdown,) = _ffn_bwd(dh2, wdown, up, up_pre, fcw, wup_lo, wup_hi, h1, ffn_norm,
                                                         _ReduceScatter([(g_wdown, 0, dn_rows)]))
    g_wup_t = _matmul_tn(dup_pre, hn, FF_GRAD_ROWS, "grad_w_up")
    (dgates, dattn, dcb, dcc, dcx, dcw_p, g_wout, g_wa_nat, g_wc_nat), (r_wup_ab,) = _mix_bwd(
        dh1, wout, gates, attn, wa, wc, cbx, cw, _ReduceScatter([(g_wup_t, 0, 2 * q_up)]))
    g_wa, g_wc = _to_col_slabs(g_wa_nat), _to_col_slabs(g_wc_nat)
    (dq, dk, dv, dsink_p), (r_wup_c, r_wout, r_wa, r_wc) = _attn_bwd(
        qkv, sinks, attn, lse, dattn,
        _ReduceScatter([(g_wup_t, 2 * q_up, q_up), (g_wout, 0, D_MODEL // N_DEV), (g_wa, 0, ATTN_W),
                        (g_wc, 0, CONV_W)]))
    dproj = (dq, dk, dv, dcb, dcc, dcx, dgates)
    small = _pack_small(dffn_p, dfn_p, dsink_p, loss_p, dcw_p, dfcw_p)
    g_win_t, (r_wup_d, r_small) = _grad_w_in(dproj, xn, _ReduceScatter([(g_wup_t, 3 * q_up, q_up)], [small]))
    (win_theirs,) = _exchange_only(_PairExchange([g_win_t]), "pair_exchange_w_in")
    q_win = _pair_add(g_win_t, win_theirs, in_rows // 2, "pair_add_w_in")
    (dx, _, _), (r_win, r_dbin, r_dmix) = _inproj_bwd(
        dproj, win_t, xs, mix_norm, dh1,
        _ChipExchangeThenBroadcast([q_win], late_from=(1, 2), late_shapes=[(8, IN_W), (8, D_MODEL)]))

    fn2, m_fn2, v_fn2 = (t.reshape(1, D_MODEL) for t in (final_norm, m_final_norm, v_final_norm))
    small_res, g_cw_full, g_fcw_full, loss_row = _small_sums_adamw(
        _slots(r_small), r_dmix, r_dbin,
        [(mix_norm, m_mix_norm, v_mix_norm), (b_in, m_b_in, v_b_in), (sinks, m_sinks, v_sinks),
         (ffn_norm, m_ffn_norm, v_ffn_norm), (fn2, m_fn2, v_fn2)])
    loss = loss_row[0, 0]
    g_cw = lax.dynamic_slice_in_dim(g_cw_full, me * (CONV_W // N_DEV), CONV_W // N_DEV, axis=1)
    g_fcw = lax.dynamic_slice_in_dim(g_fcw_full, me * up_rows, up_rows, axis=1)
    taps = lambda t: jnp.transpose(t, (1, 0, 2))
    g_cw, g_fcw = g_cw[:, None, :], g_fcw[:, None, :]
    cw_res, fcw_res = _adamw_pair((taps(conv_w), g_cw, taps(m_conv_w), taps(v_conv_w)),
                                  (taps(ffn_conv_w), g_fcw, taps(m_ffn_conv_w), taps(v_ffn_conv_w)))

    big = {}
    big["w_in"] = tuple(t.T for t in _sum_parts_adamw(
        [r_win.reshape(4, in_rows, D_MODEL)], w_in[0].T, m_w_in[0].T, v_w_in[0].T, in_rows // 2, "adamw_w_in"))
    big["w_up"] = tuple(t.T for t in _sum_parts_adamw(
        [_slots(r_wup_ab), _slots(r_wup_c), _slots(r_wup_d)], w_up[0].T, m_w_up[0].T, v_w_up[0].T, q_up,
        "adamw_w_up"))
    big["w_out"] = _sum_adamw(_slots(r_wout), w_out[0], m_w_out[0], v_w_out[0], 128, "adamw_w_out")
    big["w_down"] = _sum_adamw(_slots(r_wdown), w_down[0], m_w_down[0], v_w_down[0], dn_rows // 2, "adamw_w_down")
    big["w_attn_branch"] = _sum_adamw(_slots(r_wa), w_attn_branch[0], m_w_attn_branch[0], v_w_attn_branch[0], 256,
                                      "adamw_w_attn_branch")
    big["w_conv_branch"] = _sum_adamw(_slots(r_wc), w_conv_branch[0], m_w_conv_branch[0], v_w_conv_branch[0], 256,
                                      "adamw_w_conv_branch")

    res = dict(zip(("mix_norm", "b_in", "sinks", "ffn_norm"), small_res[:4]))
    res["final_norm"] = tuple(t.reshape(final_norm.shape) for t in small_res[4])
    res["conv_w"] = tuple(jnp.transpose(t, (1, 0, 2)) for t in (g_cw,) + cw_res)
    res["ffn_conv_w"] = tuple(jnp.transpose(t, (1, 0, 2)) for t in (g_fcw,) + fcw_res)
    for name, ref_w in (("w_in", w_in), ("w_up", w_up), ("w_out", w_out), ("w_down", w_down),
                        ("w_attn_branch", w_attn_branch), ("w_conv_branch", w_conv_branch)):
        res[name] = tuple(t.reshape(ref_w.shape) for t in big[name])

    order = ["mix_norm", "w_in", "b_in", "sinks", "conv_w", "w_attn_branch", "w_conv_branch", "w_out",
             "ffn_norm", "w_up", "ffn_conv_w", "w_down", "final_norm"]
    out = [loss, dx.reshape(x.shape)]
    for k in range(4):
        out += [res[name][k] for name in order]
    return tuple(out)
```

```python
import math

import jax
import jax.numpy as jnp
from jax import lax
from jax.experimental import pallas as pl
from jax.experimental.pallas import tpu as pltpu

F32 = jnp.float32
BF16 = jnp.bfloat16
MESH = pl.DeviceIdType.MESH
N_DEV = 8

D_MODEL = 1024
HEAD_DIM = 64
N_HEADS = 8
BLOCK = 128
ATTN_W = 512
KV_W = 128
CONV_W = 512
QKV_W = ATTN_W + 2 * KV_W
CBX_W = 3 * CONV_W
GATE_W = 2 * D_MODEL
IN_W = QKV_W + CBX_W + GATE_W
D_FF = 2816
FF_CHUNK = 256
FF_GRAD_ROWS = 1408
NORM_EPS = 1e-5
ATTN_SCALE = HEAD_DIM ** -0.5
NEG = -1e30
HALO = 16

ADAM_LR = 0.001
ADAM_B1 = 0.9
ADAM_B2 = 0.999
ADAM_EPS = 1e-08
ADAM_WD = 0.01
ADAM_STEP = 10

VMEM_LIMIT = 56 * 1024 * 1024
SMALL_ROWS = 32

NT = (((1,), (1,)), ((), ()))
TN = (((0,), (0,)), ((), ()))
ANY = pl.BlockSpec(memory_space=pl.ANY)


def _sig(v):
    return 1.0 / (1.0 + jnp.exp(-v))


def _row_tile(s, pref=256):
    return pref if s % pref == 0 else s


def _shifts_down(u, halo, ks):
    ext = jnp.concatenate([halo, u], axis=0)
    return [pltpu.roll(ext, k, axis=0)[HALO:, :] for k in ks]


def _shifts_up(u, halo, ks):
    n = u.shape[0]
    ext = jnp.concatenate([u, halo], axis=0)
    return [pltpu.roll(ext, n + HALO - k, axis=0)[:n, :] for k in ks]


def _rows_reversed(tm, c, steps):
    return pl.BlockSpec((tm, c), lambda i: (steps - 1 - i, 0))


def _prev_halo_map_reversed(tm, steps):
    return lambda i: (jnp.maximum((steps - 1 - i) * (tm // HALO) - 1, 0), 0)


def _prev_halo_map(tm):
    return lambda i: (jnp.maximum(i * (tm // HALO) - 1, 0), 0)


def _full(shape):
    return pl.BlockSpec(shape, lambda *_: (0,) * len(shape))


def _resident(shape):
    return pl.BlockSpec(shape, lambda *_: (0,) * len(shape), pipeline_mode=pl.Buffered(1))


def _rows(tm, c):
    return pl.BlockSpec((tm, c), lambda i: (i, 0))


def _sds(shape, dtype):
    return jax.ShapeDtypeStruct(shape, dtype)


def _my_place():
    x, y, c = lax.axis_index("x"), lax.axis_index("y"), lax.axis_index("c")
    return x, y, c


ALL_PEERS = tuple((j >> 2, (j >> 1) & 1, j & 1) for j in range(1, N_DEV))
SIBLING_PEER = ((0, 0, 1),)
CHIP_PEERS = ((0, 1, 0), (1, 0, 0), (1, 1, 0))
BARRIER_ID = {ALL_PEERS: 0, SIBLING_PEER: 1, CHIP_PEERS: 2}


def _barrier_signal(peers):
    x, y, c = _my_place()
    barrier = pltpu.get_barrier_semaphore()
    for dx, dy, dc in peers:
        pl.semaphore_signal(barrier, inc=1, device_id=(x ^ dx, y ^ dy, c ^ dc), device_id_type=MESH)


def _barrier_wait(peers):
    pl.semaphore_wait(pltpu.get_barrier_semaphore(), len(peers))


def _start_exchange(remote, local):
    for cp in local + remote:
        cp.start()


def _finish_exchange(remote, local):
    for cp in remote:
        cp.wait_recv()
    for cp in remote:
        cp.wait_send()
    for cp in local:
        cp.wait()


class _AllGather:
    peers = ALL_PEERS

    def __init__(self, shards, pass_on_at=None, cols=None):
        self.ins = list(shards)
        self.middle_at = pass_on_at
        self.cols = cols
        n = len(shards)
        width = (lambda s: s.shape[1]) if cols is None else (lambda s: cols[1])
        self.out_shape = [_sds((N_DEV * s.shape[0], width(s)), s.dtype) for s in shards]
        self.sems = [pltpu.SemaphoreType.DMA((7 * n,)), pltpu.SemaphoreType.DMA((7 * n,)),
                     pltpu.SemaphoreType.DMA((n,))]

    def _parts(self, ins, outs, sems):
        send_sems, recv_sems, local_sems = sems
        x, y, c = _my_place()
        me, sibling = (x, y, c), (x, y, 1 - c)
        chips = [(1 - x, y), (x, 1 - y), (1 - x, 1 - y)]

        def rows(k, dev):
            r = ins[k].shape[0]
            start = pl.multiple_of((4 * dev[0] + 2 * dev[1] + dev[2]) * r, 8)
            return outs[k].at[pl.ds(start, r), :]

        def copy(k, j, block, to, src=None):
            return pltpu.make_async_remote_copy(
                src_ref=rows(k, block) if src is None else src, dst_ref=rows(k, block),
                send_sem=send_sems.at[7 * k + j], recv_sem=recv_sems.at[7 * k + j],
                device_id=to, device_id_type=MESH)

        n = len(ins)
        srcs = ins if self.cols is None else [t.at[:, pl.ds(self.cols[0], self.cols[1])] for t in ins]
        mine = [pltpu.make_async_copy(srcs[k], rows(k, me), local_sems.at[k]) for k in range(n)]
        first = []
        for k in range(n):
            first.append(copy(k, 0, me, sibling, src=srcs[k]))
            first += [copy(k, 1 + j, me, (*chip, c), src=srcs[k]) for j, chip in enumerate(chips)]
        return me, sibling, chips, copy, mine, first

    def start(self, ins, outs, sems):
        _, _, _, _, mine, first = self._parts(ins, outs, sems)
        _start_exchange(first, mine)

    def middle(self, ins, outs, sems):
        me, sibling, chips, copy, _, _ = self._parts(ins, outs, sems)
        for j, chip in enumerate(chips):
            for k in range(len(ins)):
                copy(k, 1 + j, (*chip, me[2]), me).wait_recv()
                copy(k, 4 + j, (*chip, me[2]), sibling).start()

    def finish(self, ins, outs, sems):
        if self.middle_at is None:
            self.middle(ins, outs, sems)
        me, sibling, chips, copy, mine, first = self._parts(ins, outs, sems)
        c = me[2]
        n = len(ins)
        passed = [copy(k, 4 + j, (*chip, c), sibling) for j, chip in enumerate(chips) for k in range(n)]
        for k in range(n):
            copy(k, 0, sibling, me).wait_recv()
            for j, chip in enumerate(chips):
                copy(k, 4 + j, (*chip, 1 - c), me).wait_recv()
        for cp in first + passed:
            cp.wait_send()
        for cp in mine:
            cp.wait()


class _ReduceScatter:
    peers = ALL_PEERS

    def __init__(self, parts, bcast=()):
        self.parts = [(lo, cnt) for _, lo, cnt in parts]
        self.n_parts = len(parts)
        self.ins = [a for a, _, _ in parts] + list(bcast)
        self.out_shape = [_sds((N_DEV * cnt, a.shape[1]), a.dtype) for a, _, cnt in parts]
        self.out_shape += [_sds((N_DEV * b.shape[0], b.shape[1]), b.dtype) for b in bcast]
        n = len(self.ins)
        self.sems = [pltpu.SemaphoreType.DMA((7 * n,)), pltpu.SemaphoreType.DMA((7 * n,)),
                     pltpu.SemaphoreType.DMA((n,))]

    def _copies(self, ins, outs, sems):
        send_sems, recv_sems, local_sems = sems
        x, y, c = _my_place()
        me_idx = 4 * x + 2 * y + c
        remote, local = [], []
        for k in range(len(ins)):
            cnt = outs[k].shape[0] // N_DEV
            dst = outs[k].at[pl.ds(pl.multiple_of(me_idx * cnt, 8), cnt), :]
            if k < self.n_parts:
                lo, _ = self.parts[k]
                r = ins[k].shape[0] // N_DEV
                src_of = lambda idx: ins[k].at[pl.ds(pl.multiple_of(idx * r + lo, 8), cnt), :]
            else:
                src_of = lambda idx: ins[k]
            local.append(pltpu.make_async_copy(src_of(me_idx), dst, local_sems.at[k]))
            for j in range(1, N_DEV):
                peer = (x ^ (j >> 2), y ^ ((j >> 1) & 1), c ^ (j & 1))
                peer_idx = 4 * peer[0] + 2 * peer[1] + peer[2]
                remote.append(pltpu.make_async_remote_copy(
                    src_ref=src_of(peer_idx), dst_ref=dst,
                    send_sem=send_sems.at[7 * k + j - 1], recv_sem=recv_sems.at[7 * k + j - 1],
                    device_id=peer, device_id_type=MESH))
        return remote, local

    def start(self, ins, outs, sems):
        _start_exchange(*self._copies(ins, outs, sems))

    def finish(self, ins, outs, sems):
        _finish_exchange(*self._copies(ins, outs, sems))


class _PairExchange:
    peers = SIBLING_PEER

    def __init__(self, arrays):
        self.ins = list(arrays)
        n = len(arrays)
        self.out_shape = [_sds((a.shape[0] // 2, a.shape[1]), a.dtype) for a in arrays]
        self.sems = [pltpu.SemaphoreType.DMA((4 * n,)), pltpu.SemaphoreType.DMA((4 * n,))]

    def _copies(self, ins, outs, sems):
        send_sems, recv_sems = sems
        x, y, c = _my_place()
        remote = []
        for k in range(len(ins)):
            r = ins[k].shape[0] // N_DEV
            for chip in range(4):
                sib = ins[k].at[pl.ds(pl.multiple_of((2 * chip + 1 - c) * r, 8), r), :]
                remote.append(pltpu.make_async_remote_copy(
                    src_ref=sib, dst_ref=outs[k].at[pl.ds(chip * r, r), :],
                    send_sem=send_sems.at[4 * k + chip], recv_sem=recv_sems.at[4 * k + chip],
                    device_id=(x, y, 1 - c), device_id_type=MESH))
        return remote

    def start(self, ins, outs, sems):
        for cp in self._copies(ins, outs, sems):
            cp.start()

    def finish(self, ins, outs, sems):
        remote = self._copies(ins, outs, sems)
        for cp in remote:
            cp.wait_recv()
        for cp in remote:
            cp.wait_send()


class _ChipExchange:
    peers = CHIP_PEERS

    def __init__(self, arrays):
        self.ins = list(arrays)
        self.out_shape = [_sds(a.shape, a.dtype) for a in arrays]
        n = len(self.ins)
        self.sems = [pltpu.SemaphoreType.DMA((3 * n,)), pltpu.SemaphoreType.DMA((3 * n,)),
                     pltpu.SemaphoreType.DMA((n,))]

    def _copies(self, ins, outs, sems):
        send_sems, recv_sems, local_sems = sems
        x, y, c = _my_place()
        my_chip = 2 * x + y
        remote, local = [], []
        for k in range(len(ins)):
            r = ins[k].shape[0] // 4
            dst = outs[k].at[pl.ds(pl.multiple_of(my_chip * r, 8), r), :]
            local.append(pltpu.make_async_copy(ins[k].at[pl.ds(pl.multiple_of(my_chip * r, 8), r), :], dst,
                                               local_sems.at[k]))
            for j in range(1, 4):
                px, py = x ^ (j >> 1), y ^ (j & 1)
                src = ins[k].at[pl.ds(pl.multiple_of((2 * px + py) * r, 8), r), :]
                remote.append(pltpu.make_async_remote_copy(
                    src_ref=src, dst_ref=dst, send_sem=send_sems.at[3 * k + j - 1],
                    recv_sem=recv_sems.at[3 * k + j - 1], device_id=(px, py, c), device_id_type=MESH))
        return remote, local

    def start(self, ins, outs, sems):
        _start_exchange(*self._copies(ins, outs, sems))

    def finish(self, ins, outs, sems):
        _finish_exchange(*self._copies(ins, outs, sems))


class _ChipExchangeThenBroadcast(_ChipExchange):
    peers = ALL_PEERS
    defer_start = False

    def __init__(self, arrays, late_from, late_shapes):
        super().__init__(arrays)
        self.n_chip = len(arrays)
        self.late_from = tuple(late_from)
        self.out_shape += [_sds((N_DEV * r, c), F32) for r, c in late_shapes]
        m = len(late_shapes)
        self.sems += [pltpu.SemaphoreType.DMA((7 * m,)), pltpu.SemaphoreType.DMA((7 * m,)),
                      pltpu.SemaphoreType.DMA((m,))]

    def _late_copies(self, srcs, outs, sems):
        send_sems, recv_sems, local_sems = sems
        x, y, c = _my_place()
        me_idx = 4 * x + 2 * y + c
        remote, local = [], []
        for k, src in enumerate(srcs):
            r = src.shape[0]
            dst = outs[k].at[pl.ds(pl.multiple_of(me_idx * r, 8), r), :]
            local.append(pltpu.make_async_copy(src, dst, local_sems.at[k]))
            for j, (dx, dy, dc) in enumerate(ALL_PEERS):
                remote.append(pltpu.make_async_remote_copy(
                    src_ref=src, dst_ref=dst, send_sem=send_sems.at[7 * k + j], recv_sem=recv_sems.at[7 * k + j],
                    device_id=(x ^ dx, y ^ dy, c ^ dc), device_id_type=MESH))
        return remote, local

    def start(self, ins, outs, sems):
        _start_exchange(*self._copies(ins, outs[:self.n_chip], sems[:3]))

    def finish(self, ins, outs, sems, late_srcs):
        late = self._late_copies(late_srcs, outs[self.n_chip:], sems[3:])
        _start_exchange(*late)
        _finish_exchange(*self._copies(ins, outs[:self.n_chip], sems[:3]))
        _finish_exchange(*late)


def _pcall(body, name, grid, in_specs, out_specs, out_shape, args, scratch=(), comm=None):
    params = pltpu.CompilerParams(dimension_semantics=("arbitrary",) * len(grid), vmem_limit_bytes=VMEM_LIMIT)
    in_specs, out_specs, out_shape, scratch = list(in_specs), list(out_specs), list(out_shape), list(scratch)
    if comm is None:
        res = pl.pallas_call(body, name=name, grid=grid, in_specs=in_specs, out_specs=out_specs, out_shape=out_shape,
                             scratch_shapes=scratch, compiler_params=params)(*args)
        return list(res), []
    n_in, n_out, n_scr = len(in_specs), len(out_specs), len(scratch)
    ci, co = len(comm.ins), len(comm.out_shape)
    total = math.prod(grid)

    def carried(*refs):
        bounds = [0, n_in, n_in + ci, n_in + ci + n_out, n_in + ci + n_out + co, n_in + ci + n_out + co + n_scr]
        ins, cins, outs, couts, scr = (refs[a:b] for a, b in zip(bounds[:-1], bounds[1:]))
        sems = refs[bounds[-1]:]
        step = pl.program_id(0)
        for d in range(1, len(grid)):
            step = step * grid[d] + pl.program_id(d)

        start_step = min(1, total - 1) if getattr(comm, "defer_start", True) else 0

        @pl.when(step == 0)
        def _():
            _barrier_signal(comm.peers)

        @pl.when(step == start_step)
        def _():
            _barrier_wait(comm.peers)
            comm.start(cins, couts, sems)

        middle_at = getattr(comm, "middle_at", None)
        if middle_at is not None:
            @pl.when(step == int(middle_at * total))
            def _():
                comm.middle(cins, couts, sems)

        body(*ins, *outs, *scr)

        @pl.when(step == total - 1)
        def _():
            late_from = getattr(comm, "late_from", None)
            if late_from is None:
                comm.finish(cins, couts, sems)
            else:
                comm.finish(cins, couts, sems, [outs[k] for k in late_from])

    params = pltpu.CompilerParams(dimension_semantics=("arbitrary",) * len(grid), vmem_limit_bytes=VMEM_LIMIT,
                                  collective_id=BARRIER_ID[comm.peers])
    res = pl.pallas_call(
        carried, name=name, grid=grid, in_specs=in_specs + [ANY] * ci, out_specs=out_specs + [ANY] * co,
        out_shape=out_shape + comm.out_shape, scratch_shapes=scratch + comm.sems, compiler_params=params,
    )(*args, *comm.ins)
    return list(res[:n_out]), list(res[n_out:])


def _exchange_only(comm, name):
    def body(*refs):
        ci, co = len(comm.ins), len(comm.out_shape)
        _barrier_signal(comm.peers)
        _barrier_wait(comm.peers)
        comm.start(refs[:ci], refs[ci:ci + co], refs[ci + co:])
        comm.finish(refs[:ci], refs[ci:ci + co], refs[ci + co:])

    params = pltpu.CompilerParams(collective_id=BARRIER_ID[comm.peers])
    return pl.pallas_call(body, name=name, out_shape=comm.out_shape, in_specs=[ANY] * len(comm.ins),
                          out_specs=[ANY] * len(comm.out_shape), scratch_shapes=comm.sems,
                          compiler_params=params)(*comm.ins)


def _norm_inproj(x, g, win_t, b_in, comm):
    s = x.shape[0]
    tm = _row_tile(s, 512)
    widths = (QKV_W, CBX_W, GATE_W)

    def body(x_ref, g_ref, w_ref, b_ref, xn_ref, qkv_ref, cbx_ref, gate_ref):
        xv = x_ref[...]
        r = lax.rsqrt(jnp.mean(xv * xv, axis=-1, keepdims=True) + NORM_EPS)
        xn = (xv * r * g_ref[...]).astype(BF16)
        xn_ref[...] = xn
        off = 0
        for o_ref, w in zip((qkv_ref, cbx_ref, gate_ref), widths):
            acc = lax.dot_general(xn, w_ref[off:off + w, :], NT, preferred_element_type=F32)
            o_ref[...] = (acc + b_ref[:, off:off + w]).astype(BF16)
            off += w

    return _pcall(
        body, "norm_inproj", (s // tm,),
        [_rows(tm, D_MODEL), _full((1, D_MODEL)), _resident((IN_W, D_MODEL)), _full((1, IN_W))],
        [_rows(tm, D_MODEL)] + [_rows(tm, w) for w in widths],
        [_sds((s, D_MODEL), BF16)] + [_sds((s, w), BF16) for w in widths],
        (x, g, win_t, b_in), comm=comm)


def _attn_specs():
    prev = lambda n: jnp.maximum(n - 1, 0)
    return [pl.BlockSpec((BLOCK, ATTN_W), lambda n: (n, 0)),
            pl.BlockSpec((BLOCK, KV_W), lambda n: (prev(n), ATTN_W // KV_W)),
            pl.BlockSpec((BLOCK, KV_W), lambda n: (n, ATTN_W // KV_W)),
            pl.BlockSpec((BLOCK, KV_W), lambda n: (prev(n), ATTN_W // KV_W + 1)),
            pl.BlockSpec((BLOCK, KV_W), lambda n: (n, ATTN_W // KV_W + 1))]


def _lower_lanes():
    return lax.broadcasted_iota(jnp.int32, (BLOCK, 128), 1) < HEAD_DIM


def _stack_heads(val, kh):
    lower = _lower_lanes()
    parts = []
    for g in range(4):
        h = kh * 4 + g
        blk = val[:, (h // 2) * 128:(h // 2 + 1) * 128]
        keep = lower if h % 2 == 0 else jnp.logical_not(lower)
        parts.append(jnp.where(keep, blk, jnp.zeros_like(blk)))
    return jnp.concatenate(parts, axis=0)


def _dup_kv(prev_ref, cur_ref, kh):
    t = jnp.concatenate([prev_ref[...], cur_ref[...]], axis=0).astype(F32)
    rolled = pltpu.roll(t, HEAD_DIM, axis=1)
    lower = lax.broadcasted_iota(jnp.int32, t.shape, 1) < HEAD_DIM
    dup = jnp.where(lower, t, rolled) if kh == 0 else jnp.where(lower, rolled, t)
    return dup.astype(BF16)


def _attn_mask(n):
    row = lax.broadcasted_iota(jnp.int32, (4 * BLOCK, 2 * BLOCK), 0)
    kj = lax.broadcasted_iota(jnp.int32, (4 * BLOCK, 2 * BLOCK), 1)
    dist = (row & (BLOCK - 1)) + BLOCK - kj
    band = jnp.logical_and(dist >= 0, dist < BLOCK)
    return jnp.logical_and(band, jnp.logical_or(kj >= BLOCK, n > 0))


def _sink_col(sinks_ref, kh):
    gi = lax.broadcasted_iota(jnp.int32, (4 * BLOCK, 1), 0) // BLOCK
    col = jnp.zeros((4 * BLOCK, 1), F32)
    for g in range(4):
        col = jnp.where(gi == g, sinks_ref[0, kh * 4 + g], col)
    return col


def _attn_fwd(qkv, sinks, comm):
    s = qkv.shape[0]

    def body(sinks_ref, q_ref, kp_ref, kc_ref, vp_ref, vc_ref, o_ref, lse_ref):
        n = pl.program_id(0)
        mask = _attn_mask(n)
        lower = _lower_lanes()
        lane = lax.broadcasted_iota(jnp.int32, (BLOCK, 128), 1)
        qv = q_ref[...]
        lse_out = jnp.zeros((BLOCK, 128), F32)
        for kh in range(2):
            qs = _stack_heads(qv, kh)
            kd, vd = _dup_kv(kp_ref, kc_ref, kh), _dup_kv(vp_ref, vc_ref, kh)
            sc = lax.dot_general(qs, kd, NT, preferred_element_type=F32) * ATTN_SCALE
            sc = jnp.where(mask, sc, NEG)
            sink = _sink_col(sinks_ref, kh)
            m = jnp.maximum(jnp.max(sc, axis=1, keepdims=True), sink)
            p = jnp.exp(sc - m)
            l = jnp.sum(p, axis=1, keepdims=True) + jnp.exp(sink - m)
            o = jnp.dot(p.astype(BF16), vd, preferred_element_type=F32) / l
            lse = m + jnp.log(l)
            for pair in range(2):
                lo = o[(2 * pair) * BLOCK:(2 * pair + 1) * BLOCK]
                hi = o[(2 * pair + 1) * BLOCK:(2 * pair + 2) * BLOCK]
                col = (kh * 2 + pair) * 128
                o_ref[:, col:col + 128] = jnp.where(lower, lo, hi).astype(BF16)
            for g in range(4):
                lse_out = jnp.where(lane == kh * 4 + g, lse[g * BLOCK:(g + 1) * BLOCK], lse_out)
        lse_ref[...] = lse_out

    return _pcall(
        body, "attn_fwd", (s // BLOCK,),
        [pl.BlockSpec(memory_space=pltpu.SMEM)] + _attn_specs(),
        [pl.BlockSpec((BLOCK, ATTN_W), lambda n: (n, 0)), pl.BlockSpec((BLOCK, 128), lambda n: (n, 0))],
        [_sds((s, ATTN_W), BF16), _sds((s, 128), F32)],
        (sinks, qkv, qkv, qkv, qkv, qkv), comm=comm)


def _conv_u(cbx_ref, halo_ref, w_ref, first):
    cb = cbx_ref[:, 0:CONV_W].astype(F32)
    cc = cbx_ref[:, CONV_W:2 * CONV_W].astype(F32)
    cx = cbx_ref[:, 2 * CONV_W:3 * CONV_W].astype(F32)
    u = cc * cx
    uh = halo_ref[:, CONV_W:2 * CONV_W].astype(F32) * halo_ref[:, 2 * CONV_W:3 * CONV_W].astype(F32)
    uh = jnp.where(first, 0.0, uh)
    u1, u2 = _shifts_down(u, uh, (1, 2))
    cv = w_ref[0:1, :] * u2 + w_ref[1:2, :] * u1 + w_ref[2:3, :] * u
    return cb, cc, cx, u, cv


def _mix_fwd(x, cbx, gates, attn, conv_w, wa, wc, wout, comm):
    s = x.shape[0]
    tm = _row_tile(s)

    def body(x_ref, cbx_ref, halo_ref, gate_ref, attn_ref, cw_ref, wa_ref, wc_ref, wo_ref,
             h1_ref):
        first = pl.program_id(0) == 0
        cb, _, _, _, cv = _conv_u(cbx_ref, halo_ref, cw_ref, first)
        conv = (cb * cv).astype(BF16)
        ap = jnp.dot(attn_ref[...], wa_ref[...], preferred_element_type=F32)
        cp = jnp.dot(conv, wc_ref[...], preferred_element_type=F32)
        ga = gate_ref[:, 0:D_MODEL].astype(F32)
        gc = gate_ref[:, D_MODEL:2 * D_MODEL].astype(F32)
        merged = (_sig(ga) * ap + _sig(gc) * cp).astype(BF16)
        h1_ref[...] = x_ref[...] + jnp.dot(merged, wo_ref[...], preferred_element_type=F32)

    return _pcall(
        body, "mix_fwd", (s // tm,),
        [_rows(tm, D_MODEL), _rows(tm, CBX_W), pl.BlockSpec((HALO, CBX_W), _prev_halo_map(tm)),
         _rows(tm, GATE_W), _rows(tm, ATTN_W), _full((3, CONV_W)), _full((ATTN_W, D_MODEL)),
         _full((CONV_W, D_MODEL)), _full((D_MODEL, D_MODEL))],
        [_rows(tm, D_MODEL)], [_sds((s, D_MODEL), F32)],
        (x, cbx, cbx, gates, attn, conv_w, wa, wc, wout), comm=comm)


def _ffn_up(h1, g, wup_lo, wup_hi, fcw, comm):
    s = h1.shape[0]
    tm = _row_tile(s)
    half = D_MODEL // 2

    def body(h_ref, g_ref, wl_ref, wh_ref, fcw_ref, hn_ref, pre_ref, up_ref, carry_ref):
        @pl.when(pl.program_id(0) == 0)
        def _():
            carry_ref[...] = jnp.zeros_like(carry_ref)

        hv = h_ref[...]
        r = lax.rsqrt(jnp.mean(hv * hv, axis=-1, keepdims=True) + NORM_EPS)
        hn = (hv * r * g_ref[...]).astype(BF16)
        hn_ref[...] = hn
        for c in range(2 * D_FF // FF_CHUNK):
            sl = slice(c * FF_CHUNK, (c + 1) * FF_CHUNK)
            acc = lax.dot_general(hn[:, :half], wl_ref[sl, :], NT, preferred_element_type=F32)
            acc = acc + lax.dot_general(hn[:, half:], wh_ref[sl, :], NT, preferred_element_type=F32)
            pre_ref[:, sl] = acc.astype(BF16)
            halo = carry_ref[:, sl]
            carry_ref[:, sl] = acc[tm - HALO:, :]
            u1, u2 = _shifts_down(acc, halo, (1, 2))
            w = fcw_ref[:, sl]
            up_ref[:, sl] = (w[0:1] * u2 + w[1:2] * u1 + w[2:3] * acc).astype(BF16)

    return _pcall(
        body, "ffn_up", (s // tm,),
        [_rows(tm, D_MODEL), _full((1, D_MODEL)), _resident((2 * D_FF, half)), _resident((2 * D_FF, half)),
         _full((3, 2 * D_FF))],
        [_rows(tm, D_MODEL), _rows(tm, 2 * D_FF), _rows(tm, 2 * D_FF)],
        [_sds((s, D_MODEL), BF16), _sds((s, 2 * D_FF), BF16), _sds((s, 2 * D_FF), BF16)],
        (h1, g, wup_lo, wup_hi, fcw), scratch=[pltpu.VMEM((HALO, 2 * D_FF), F32)], comm=comm)


def _ffn_down_loss(up, wdown, h1, fnorm, target):
    s = h1.shape[0]
    tm = _row_tile(s)

    def body(up_ref, wd_ref, h1_ref, fn_ref, t_ref, act_ref, dh2_ref, loss_ref, dfn_ref):
        i = pl.program_id(0)

        @pl.when(i == 0)
        def _():
            loss_ref[...] = jnp.zeros_like(loss_ref)
            dfn_ref[...] = jnp.zeros_like(dfn_ref)

        h2 = h1_ref[...]
        for c in range(D_FF // FF_CHUNK):
            gsl = slice(c * FF_CHUNK, (c + 1) * FF_CHUNK)
            vsl = slice(D_FF + c * FF_CHUNK, D_FF + (c + 1) * FF_CHUNK)
            gate = up_ref[:, gsl].astype(F32)
            val = up_ref[:, vsl].astype(F32)
            act = (gate * _sig(gate) * val).astype(BF16)
            act_ref[:, gsl] = act
            h2 = h2 + jnp.dot(act, wd_ref[gsl, :], preferred_element_type=F32)
        r = lax.rsqrt(jnp.mean(h2 * h2, axis=-1, keepdims=True) + NORM_EPS)
        yhat = h2 * r
        fn = fn_ref[...]
        diff = yhat * fn - t_ref[...]
        loss_ref[...] += 0.5 * jnp.sum(jnp.sum(diff * diff, axis=1, keepdims=True), axis=0, keepdims=True) / D_MODEL
        dy = diff * (1.0 / D_MODEL)
        dfn_ref[...] += jnp.sum(dy * yhat, axis=0, keepdims=True)
        dyh = dy * fn
        dh2_ref[...] = r * (dyh - yhat * jnp.mean(dyh * yhat, axis=-1, keepdims=True))

    return _pcall(
        body, "ffn_down_loss", (s // tm,),
        [_rows(tm, 2 * D_FF), _resident((D_FF, D_MODEL)), _rows(tm, D_MODEL), _full((1, D_MODEL)),
         _rows(tm, D_MODEL)],
        [_rows(tm, D_FF), _rows(tm, D_MODEL), _full((1, 128)), _full((1, D_MODEL))],
        [_sds((s, D_FF), BF16), _sds((s, D_MODEL), F32), _sds((1, 128), F32), _sds((1, D_MODEL), F32)],
        (up, wdown, h1, fnorm, target))[0]


def _ffn_bwd(dh2, wdown, up, up_pre, fcw, wup_lo, wup_hi, h1, g, comm):
    s = dh2.shape[0]
    tm = _row_tile(s)
    half = D_MODEL // 2

    chunk = FF_GRAD_ROWS

    def dup_cols(dh, up_ref, wd_ref, c):
        gsl = slice(c * chunk, (c + 1) * chunk)
        vsl = slice(D_FF + c * chunk, D_FF + (c + 1) * chunk)
        dact = lax.dot_general(dh, wd_ref[gsl, :], NT, preferred_element_type=F32)
        gate = up_ref[:, gsl].astype(F32)
        val = up_ref[:, vsl].astype(F32)
        sg = _sig(gate)
        return dact * val * (sg * (1.0 + gate * (1.0 - sg))), dact * gate * sg

    def body(dh_ref, wd_ref, up_ref, x_ref, w_ref, wl_ref, wh_ref, h_ref, g_ref,
             dx_ref, dw_ref, dh1_ref, dg_ref, carry_ref):
        @pl.when(pl.program_id(0) == 0)
        def _():
            dw_ref[...] = jnp.zeros_like(dw_ref)
            dg_ref[...] = jnp.zeros_like(dg_ref)
            carry_ref[...] = jnp.zeros_like(carry_ref)

        dh2v = dh_ref[...]
        dh = dh2v.astype(BF16)
        dhn_lo = jnp.zeros((tm, half), F32)
        dhn_hi = jnp.zeros((tm, half), F32)
        for c in range(D_FF // chunk):
            for d, off in zip(dup_cols(dh, up_ref, wd_ref, c), (c * chunk, D_FF + c * chunk)):
                sl = slice(off, off + chunk)
                dn = carry_ref[:, sl]
                carry_ref[:, sl] = d[0:HALO, :]
                xv = x_ref[:, sl].astype(F32)
                wv = w_ref[:, sl]
                d1, d2 = _shifts_up(d, dn, (1, 2))
                dx = (wv[2:3] * d + wv[1:2] * d1 + wv[0:1] * d2).astype(BF16)
                dx_ref[:, sl] = dx
                dhn_lo = dhn_lo + jnp.dot(dx, wl_ref[sl, :], preferred_element_type=F32)
                dhn_hi = dhn_hi + jnp.dot(dx, wh_ref[sl, :], preferred_element_type=F32)
                dw_ref[0:1, sl] += jnp.sum(d2 * xv, axis=0, keepdims=True)
                dw_ref[1:2, sl] += jnp.sum(d1 * xv, axis=0, keepdims=True)
                dw_ref[2:3, sl] += jnp.sum(d * xv, axis=0, keepdims=True)
        dx1, dg = _norm_bwd_tile(h_ref[...], g_ref[...], jnp.concatenate([dhn_lo, dhn_hi], axis=1))
        dg_ref[...] += dg
        dh1_ref[...] = dh2v + dx1

    rows = lambda c: _rows_reversed(tm, c, s // tm)
    return _pcall(
        body, "ffn_bwd", (s // tm,),
        [rows(D_MODEL), _resident((D_FF, D_MODEL)), rows(2 * D_FF), rows(2 * D_FF), _full((3, 2 * D_FF)),
         _resident((2 * D_FF, half)), _resident((2 * D_FF, half)), rows(D_MODEL), _full((1, D_MODEL))],
        [rows(2 * D_FF), _full((3, 2 * D_FF)), rows(D_MODEL), _full((1, D_MODEL))],
        [_sds((s, 2 * D_FF), BF16), _sds((3, 2 * D_FF), F32), _sds((s, D_MODEL), F32), _sds((1, D_MODEL), F32)],
        (dh2, wdown, up, up_pre, fcw, wup_lo, wup_hi, h1, g),
        scratch=[pltpu.VMEM((HALO, 2 * D_FF), F32)], comm=comm)


def _matmul_tn(a, b, tk, name, ts=1024, comm=None):
    s, ka = a.shape
    n = b.shape[1]
    ts = min(ts, s)
    steps = s // ts

    def body(a_ref, b_ref, o_ref, acc_ref):
        j = pl.program_id(1)

        @pl.when(j == 0)
        def _():
            acc_ref[...] = jnp.zeros_like(acc_ref)

        acc_ref[...] += lax.dot_general(a_ref[...].astype(BF16), b_ref[...].astype(BF16), TN,
                                        preferred_element_type=F32)

        @pl.when(j == steps - 1)
        def _():
            o_ref[...] = acc_ref[...].astype(BF16)

    outs, couts = _pcall(
        body, name, (ka // tk, steps),
        [pl.BlockSpec((ts, tk), lambda i, j: (j, i)), pl.BlockSpec((ts, n), lambda i, j: (j, 0))],
        [pl.BlockSpec((tk, n), lambda i, j: (i, 0))], [_sds((ka, n), BF16)],
        (a, b), scratch=[pltpu.VMEM((tk, n), F32)], comm=comm)
    return outs[0] if comm is None else (outs[0], couts)


def _norm_bwd_tile(xv, g, dy):
    r = lax.rsqrt(jnp.mean(xv * xv, axis=-1, keepdims=True) + NORM_EPS)
    xhat = xv * r
    dg = jnp.sum(dy * xhat, axis=0, keepdims=True)
    dyh = dy * g
    return r * (dyh - xhat * jnp.mean(dyh * xhat, axis=-1, keepdims=True)), dg


def _mix_bwd(dh1, wout, gates, attn, wa, wc, cbx, conv_w, comm):
    s = dh1.shape[0]
    tm = _row_tile(s)
    steps = s // tm

    def body(dh_ref, wo_ref, gate_ref, attn_ref, wa_ref, wc_ref, cbx_ref, halo_ref,
             cw_ref, dg_ref, dattn_ref, dcb_ref, dcc_ref, dcx_ref, dw_ref, gwo_ref, gwa_ref, gwc_ref,
             acc_o, acc_a, acc_c, carry_ref):
        i = pl.program_id(0)

        @pl.when(i == 0)
        def _():
            dw_ref[...] = jnp.zeros_like(dw_ref)
            acc_o[...] = jnp.zeros_like(acc_o)
            acc_a[...] = jnp.zeros_like(acc_a)
            acc_c[...] = jnp.zeros_like(acc_c)
            carry_ref[...] = jnp.zeros_like(carry_ref)

        cb, cc, cx, u, cv = _conv_u(cbx_ref, halo_ref, cw_ref, i == steps - 1)
        attn = attn_ref[...]
        conv = (cb * cv).astype(BF16)
        ap = jnp.dot(attn, wa_ref[...], preferred_element_type=F32)
        cp = jnp.dot(conv, wc_ref[...], preferred_element_type=F32)
        dhb = dh_ref[...].astype(BF16)
        dm = lax.dot_general(dhb, wo_ref[...], NT, preferred_element_type=F32)
        sa = _sig(gate_ref[:, 0:D_MODEL].astype(F32))
        sc = _sig(gate_ref[:, D_MODEL:2 * D_MODEL].astype(F32))
        merged = (sa * ap + sc * cp).astype(BF16)
        da = (dm * sa).astype(BF16)
        dc = (dm * sc).astype(BF16)
        dg_ref[:, 0:D_MODEL] = (dm * ap * sa * (1.0 - sa)).astype(BF16)
        dg_ref[:, D_MODEL:2 * D_MODEL] = (dm * cp * sc * (1.0 - sc)).astype(BF16)
        dattn_ref[...] = lax.dot_general(da, wa_ref[...], NT, preferred_element_type=F32).astype(BF16)
        dconv = lax.dot_general(dc, wc_ref[...], NT, preferred_element_type=F32)
        dcb_ref[...] = (dconv * cv).astype(BF16)
        d = dconv * cb
        dn = carry_ref[...]
        carry_ref[...] = d[0:HALO, :]
        d1, d2 = _shifts_up(d, dn, (1, 2))
        du = cw_ref[2:3, :] * d + cw_ref[1:2, :] * d1 + cw_ref[0:1, :] * d2
        dcc_ref[...] = (du * cx).astype(BF16)
        dcx_ref[...] = (du * cc).astype(BF16)
        dw_ref[0:1, :] += jnp.sum(d2 * u, axis=0, keepdims=True)
        dw_ref[1:2, :] += jnp.sum(d1 * u, axis=0, keepdims=True)
        dw_ref[2:3, :] += jnp.sum(d * u, axis=0, keepdims=True)
        acc_o[...] += lax.dot_general(merged, dhb, TN, preferred_element_type=F32)
        acc_a[...] += lax.dot_general(attn, da, TN, preferred_element_type=F32)
        acc_c[...] += lax.dot_general(conv, dc, TN, preferred_element_type=F32)

        @pl.when(i == steps - 1)
        def _():
            gwo_ref[...] = acc_o[...].astype(BF16)
            gwa_ref[...] = acc_a[...].astype(BF16)
            gwc_ref[...] = acc_c[...].astype(BF16)

    rows = lambda c: _rows_reversed(tm, c, steps)
    return _pcall(
        body, "mix_bwd", (steps,),
        [rows(D_MODEL), _full((D_MODEL, D_MODEL)), rows(GATE_W), rows(ATTN_W), _full((ATTN_W, D_MODEL)),
         _full((CONV_W, D_MODEL)), rows(CBX_W), pl.BlockSpec((HALO, CBX_W), _prev_halo_map_reversed(tm, steps)),
         _full((3, CONV_W))],
        [rows(GATE_W), rows(ATTN_W), rows(CONV_W), rows(CONV_W), rows(CONV_W),
         _full((3, CONV_W)), _full((D_MODEL, D_MODEL)), _full((ATTN_W, D_MODEL)), _full((CONV_W, D_MODEL))],
        [_sds((s, GATE_W), BF16), _sds((s, ATTN_W), BF16), _sds((s, CONV_W), BF16), _sds((s, CONV_W), BF16),
         _sds((s, CONV_W), BF16), _sds((3, CONV_W), F32), _sds((D_MODEL, D_MODEL), BF16),
         _sds((ATTN_W, D_MODEL), BF16), _sds((CONV_W, D_MODEL), BF16)],
        (dh1, wout, gates, attn, wa, wc, cbx, cbx, conv_w),
        scratch=[pltpu.VMEM((D_MODEL, D_MODEL), F32), pltpu.VMEM((ATTN_W, D_MODEL), F32),
                 pltpu.VMEM((CONV_W, D_MODEL), F32), pltpu.VMEM((HALO, CONV_W), F32)], comm=comm)


def _attn_bwd(qkv, sinks, attn, lse, dattn, comm):
    s = qkv.shape[0]

    def body(sinks_ref, q_ref, kp_ref, kc_ref, vp_ref, vc_ref, o_ref, lse_ref, do_ref,
             dq_ref, dk_ref, dv_ref, ds_ref):
        n = pl.program_id(0)

        @pl.when(n == 0)
        def _():
            dk_ref[...] = jnp.zeros_like(dk_ref)
            dv_ref[...] = jnp.zeros_like(dv_ref)
            ds_ref[...] = jnp.zeros_like(ds_ref)

        mask = _attn_mask(n)
        lower = _lower_lanes()
        lane = lax.broadcasted_iota(jnp.int32, (BLOCK, 128), 1)
        lower2 = lax.broadcasted_iota(jnp.int32, (2 * BLOCK, 128), 1) < HEAD_DIM
        lane1 = lax.broadcasted_iota(jnp.int32, (1, 128), 1)
        qv, ov, dov, lsev = q_ref[...], o_ref[...], do_ref[...], lse_ref[...]
        dk_fold, dv_fold = [], []
        dsink = jnp.zeros((1, 128), F32)
        for kh in range(2):
            qs = _stack_heads(qv, kh)
            dos = _stack_heads(dov, kh)
            os_ = _stack_heads(ov, kh)
            kd, vd = _dup_kv(kp_ref, kc_ref, kh), _dup_kv(vp_ref, vc_ref, kh)
            lse = jnp.concatenate(
                [jnp.sum(jnp.where(lane == kh * 4 + g, lsev, 0.0), axis=1, keepdims=True) for g in range(4)], axis=0)
            sc = lax.dot_general(qs, kd, NT, preferred_element_type=F32) * ATTN_SCALE
            p = jnp.exp(jnp.where(mask, sc, NEG) - lse)
            dp = lax.dot_general(dos, vd, NT, preferred_element_type=F32)
            delta = jnp.sum(dos.astype(F32) * os_.astype(F32), axis=1, keepdims=True)
            dsc = (p * (dp - delta) * ATTN_SCALE).astype(BF16)
            dqs = jnp.dot(dsc, kd, preferred_element_type=F32)
            for pair in range(2):
                lo = dqs[(2 * pair) * BLOCK:(2 * pair + 1) * BLOCK]
                hi = dqs[(2 * pair + 1) * BLOCK:(2 * pair + 2) * BLOCK]
                col = (kh * 2 + pair) * 128
                dq_ref[:, col:col + 128] = jnp.where(lower, lo, hi).astype(BF16)
            dkd = lax.dot_general(dsc, qs, TN, preferred_element_type=F32)
            dvd = lax.dot_general(p.astype(BF16), dos, TN, preferred_element_type=F32)
            dk_fold.append(dkd + pltpu.roll(dkd, HEAD_DIM, axis=1))
            dv_fold.append(dvd + pltpu.roll(dvd, HEAD_DIM, axis=1))
            psink = jnp.exp(_sink_col(sinks_ref, kh) - lse) * delta
            for g in range(4):
                tot = jnp.sum(psink[g * BLOCK:(g + 1) * BLOCK], axis=0, keepdims=True)
                dsink = dsink - jnp.where(lane1 == kh * 4 + g, tot, 0.0)
        dk2 = jnp.where(lower2, dk_fold[0], dk_fold[1])
        dv2 = jnp.where(lower2, dv_fold[0], dv_fold[1])
        ds_ref[...] += dsink
        cur = pl.ds(pl.multiple_of(n * BLOCK, BLOCK), BLOCK)
        dk_ref[cur, :] += dk2[BLOCK:]
        dv_ref[cur, :] += dv2[BLOCK:]

        @pl.when(n > 0)
        def _():
            prev = pl.ds(pl.multiple_of((n - 1) * BLOCK, BLOCK), BLOCK)
            dk_ref[prev, :] += dk2[:BLOCK]
            dv_ref[prev, :] += dv2[:BLOCK]

    blk = lambda w: pl.BlockSpec((BLOCK, w), lambda n: (n, 0))
    return _pcall(
        body, "attn_bwd", (s // BLOCK,),
        [pl.BlockSpec(memory_space=pltpu.SMEM)] + _attn_specs() + [blk(ATTN_W), blk(128), blk(ATTN_W)],
        [blk(ATTN_W), _full((s, KV_W)), _full((s, KV_W)), _full((1, 128))],
        [_sds((s, ATTN_W), BF16), _sds((s, KV_W), F32), _sds((s, KV_W), F32), _sds((1, 128), F32)],
        (sinks, qkv, qkv, qkv, qkv, qkv, attn, lse, dattn), comm=comm)


DPROJ_PIECES = (ATTN_W, KV_W, KV_W, CONV_W, CONV_W, CONV_W, GATE_W)
DPROJ_OFFSETS = tuple(sum(DPROJ_PIECES[:k]) for k in range(len(DPROJ_PIECES)))


def _grad_w_in(pieces, xn, comm):
    s = xn.shape[0]
    ts = min(1024, s)
    steps = s // ts
    rows0 = DPROJ_OFFSETS[6]

    def body(*refs):
        p_refs, b_ref, o_ref, acc_ref, stage_ref, sem = refs[:7], refs[7], refs[8], refs[9], refs[10], refs[11]
        i, j = pl.program_id(0), pl.program_id(1)

        @pl.when(j == 0)
        def _():
            acc_ref[...] = jnp.zeros_like(acc_ref)

        bv = b_ref[...]

        def flush(lo, n):
            stage_ref[0:n, :] = acc_ref[0:n, :].astype(BF16)
            cp = pltpu.make_async_copy(stage_ref.at[0:n, :], o_ref.at[lo:lo + n, :], sem)
            cp.start()
            cp.wait()

        @pl.when(i == 0)
        def _():
            for p_ref, off, w in zip(p_refs[:6], DPROJ_OFFSETS[:6], DPROJ_PIECES[:6]):
                acc_ref[off:off + w, :] += lax.dot_general(p_ref[...].astype(BF16), bv, TN,
                                                           preferred_element_type=F32)

            @pl.when(j == steps - 1)
            def _():
                flush(0, rows0)

        @pl.when(i == 1)
        def _():
            acc_ref[0:GATE_W, :] += lax.dot_general(p_refs[6][...], bv, TN, preferred_element_type=F32)

            @pl.when(j == steps - 1)
            def _():
                flush(rows0, GATE_W)

    def piece_spec(w, group):
        return pl.BlockSpec((ts, w), lambda i, j: (jnp.where(i == group, j, 0), 0))

    outs, couts = _pcall(
        body, "grad_w_in", (2, steps),
        [piece_spec(w, 0) for w in DPROJ_PIECES[:6]] + [piece_spec(GATE_W, 1),
                                                         pl.BlockSpec((ts, D_MODEL), lambda i, j: (j, 0))],
        [ANY], [_sds((IN_W, D_MODEL), BF16)], (*pieces, xn),
        scratch=[pltpu.VMEM((rows0, D_MODEL), F32), pltpu.VMEM((rows0, D_MODEL), BF16), pltpu.SemaphoreType.DMA],
        comm=comm)
    return outs[0], couts


def _inproj_bwd(pieces, win_t, x, g, dh1, comm):
    s = x.shape[0]
    tm = _row_tile(s, 512)

    def body(*refs):
        p_refs = refs[:7]
        w_ref, x_ref, g_ref, dh_ref, dx_ref, db_ref, dg_ref = refs[7:]

        @pl.when(pl.program_id(0) == 0)
        def _():
            db_ref[...] = jnp.zeros_like(db_ref)
            dg_ref[...] = jnp.zeros_like(dg_ref)

        dxn = jnp.zeros((tm, D_MODEL), F32)
        for p_ref, off, w in zip(p_refs, DPROJ_OFFSETS, DPROJ_PIECES):
            v = p_ref[...].astype(BF16)
            db_ref[:, off:off + w] += jnp.sum(v.astype(F32), axis=0, keepdims=True)
            dxn = dxn + jnp.dot(v, w_ref[off:off + w, :], preferred_element_type=F32)
        dx, dg = _norm_bwd_tile(x_ref[...], g_ref[...], dxn)
        dg_ref[...] += dg
        dx_ref[...] = dh_ref[...] + dx

    return _pcall(
        body, "inproj_bwd", (s // tm,),
        [_rows(tm, w) for w in DPROJ_PIECES] + [_resident((IN_W, D_MODEL)), _rows(tm, D_MODEL), _full((1, D_MODEL)),
                                                _rows(tm, D_MODEL)],
        [_rows(tm, D_MODEL), _full((8, IN_W)), _full((8, D_MODEL))],
        [_sds((s, D_MODEL), F32), _sds((8, IN_W), F32), _sds((8, D_MODEL), F32)],
        (*pieces, win_t, x, g, dh1), comm=comm)


def _adam_math(w, g, m, v):
    m2 = ADAM_B1 * m + (1.0 - ADAM_B1) * g
    v2 = ADAM_B2 * v + (1.0 - ADAM_B2) * (g * g)
    m_hat = m2 / (1.0 - ADAM_B1 ** ADAM_STEP)
    v_hat = v2 / (1.0 - ADAM_B2 ** ADAM_STEP)
    delta = -ADAM_LR * (m_hat / (jnp.sqrt(v_hat) + ADAM_EPS) + ADAM_WD * w)
    return delta, m2, v2


def _sum_slots(ref):
    tot = ref[0].astype(F32)
    for i in range(1, ref.shape[0]):
        tot = tot + ref[i].astype(F32)
    return tot


def _pair_add(partials, theirs, tr, name):
    r = partials.shape[0] // N_DEV
    c = partials.shape[1]
    nt = r // tr
    core = lax.axis_index("c").astype(jnp.int32).reshape(1)

    def body(core_ref, a_ref, b_ref, o_ref):
        o_ref[...] = (a_ref[...].astype(F32) + b_ref[...].astype(F32)).astype(BF16)

    grid_spec = pltpu.PrefetchScalarGridSpec(
        num_scalar_prefetch=1, grid=(4 * nt,),
        in_specs=[pl.BlockSpec((None, None, tr, c), lambda i, core_ref: (i // nt, core_ref[0], i % nt, 0)),
                  pl.BlockSpec((tr, c), lambda i, core_ref: (i, 0))],
        out_specs=pl.BlockSpec((tr, c), lambda i, core_ref: (i, 0)))
    return pl.pallas_call(body, name=name, grid_spec=grid_spec, out_shape=_sds((4 * r, c), BF16))(
        core, partials.reshape(4, 2, r, c), theirs)


def _sum_adamw(parts, w, m, v, tr, name):
    r, c = w.shape

    def body(p_ref, w_ref, m_ref, v_ref, g_ref, d_ref, m2_ref, v2_ref):
        g = _sum_slots(p_ref)
        g_ref[...] = g
        d_ref[...], m2_ref[...], v2_ref[...] = _adam_math(w_ref[...], g, m_ref[...], v_ref[...])

    spec = pl.BlockSpec((tr, c), lambda i: (i, 0))
    return _pcall(body, name, (r // tr,), [pl.BlockSpec((N_DEV, tr, c), lambda i: (0, i, 0)), spec, spec, spec],
                  [spec] * 4, [_sds((r, c), F32)] * 4, (parts, w, m, v))[0]


def _sum_parts_adamw(parts, w, m, v, tr, name):
    c = w.shape[1]
    tiles = [p.shape[1] // tr for p in parts]
    starts = [sum(tiles[:k]) for k in range(len(parts))]
    n_parts = len(parts)

    def body(*refs):
        p_refs = refs[:n_parts]
        w_ref, m_ref, v_ref, g_ref, d_ref, m2_ref, v2_ref = refs[n_parts:]
        i = pl.program_id(0)
        for p_ref, st, nt in zip(p_refs, starts, tiles):
            @pl.when(jnp.logical_and(i >= st, i < st + nt))
            def _(p_ref=p_ref):
                g_ref[...] = _sum_slots(p_ref)

        d_ref[...], m2_ref[...], v2_ref[...] = _adam_math(w_ref[...], g_ref[...], m_ref[...], v_ref[...])

    def part_spec(p, st, nt):
        return pl.BlockSpec((p.shape[0], tr, c), lambda i: (0, jnp.clip(i - st, 0, nt - 1), 0))

    spec = pl.BlockSpec((tr, c), lambda i: (i, 0))
    return _pcall(
        body, name, (sum(tiles),),
        [part_spec(p, st, nt) for p, st, nt in zip(parts, starts, tiles)] + [spec, spec, spec],
        [spec] * 4, [_sds(w.shape, F32)] * 4, (*parts, w, m, v))[0]


ROW_MIX, ROW_FFN, ROW_FINAL, ROW_SINKS, ROW_LOSS, ROW_BIN, ROW_CW, ROW_FCW = 0, 1, 2, 3, 4, 5, 10, 13
FCW_ROWS = 6


def _wide_pieces(width):
    return [(k * D_MODEL, min(D_MODEL, width - k * D_MODEL)) for k in range(-(-width // D_MODEL))]


def _pack_small(dffn, dfn, dsink, loss, dcw, dfcw):
    def body(ffn_ref, fn_ref, sink_ref, loss_ref, cw_ref, fcw_ref, o_ref):
        o_ref[...] = jnp.zeros_like(o_ref)
        o_ref[ROW_FFN:ROW_FFN + 1, :] = ffn_ref[...]
        o_ref[ROW_FINAL:ROW_FINAL + 1, :] = fn_ref[...]
        o_ref[ROW_SINKS:ROW_SINKS + 1, 0:128] = sink_ref[...]
        o_ref[ROW_LOSS:ROW_LOSS + 1, 0:128] = loss_ref[...]
        o_ref[ROW_CW:ROW_CW + 3, 0:CONV_W] = cw_ref[...]
        for a in range(3):
            for k, (off, w) in enumerate(_wide_pieces(2 * D_FF)):
                row = ROW_FCW + FCW_ROWS * a + k
                o_ref[row:row + 1, 0:w] = fcw_ref[a:a + 1, off:off + w]

    return pl.pallas_call(body, name="pack_small", out_shape=_sds((SMALL_ROWS, D_MODEL), F32))(
        dffn, dfn, dsink, loss, dcw, dfcw)


def _small_sums_adamw(r_small, r_dmix, r_dbin, params):
    rows = (None, None, ROW_SINKS, ROW_FFN, ROW_FINAL)

    def sum_row0(ref):
        tot = ref[0:1, :]
        for i in range(1, N_DEV):
            tot = tot + ref[8 * i:8 * i + 1, :]
        return tot

    def body(*refs):
        r_ref, late_refs, p_refs, o_refs = refs[0], refs[1:3], refs[3:18], refs[18:]
        tot = _sum_slots(r_ref)
        for k, row in enumerate(rows):
            w_ref, m_ref, v_ref = p_refs[3 * k:3 * k + 3]
            g_ref, d_ref, m2_ref, v2_ref = o_refs[4 * k:4 * k + 4]
            if row is None:
                g_ref[...] = sum_row0(late_refs[k])
            else:
                for j, (off, w) in enumerate(_wide_pieces(w_ref.shape[1])):
                    g_ref[:, off:off + w] = tot[row + j:row + j + 1, 0:w]
            d_ref[...], m2_ref[...], v2_ref[...] = _adam_math(w_ref[...], g_ref[...], m_ref[...], v_ref[...])
        cw_ref, fcw_ref, loss_ref = o_refs[20:]
        cw_ref[...] = tot[ROW_CW:ROW_CW + 3, 0:CONV_W]
        for a in range(3):
            for j, (off, w) in enumerate(_wide_pieces(2 * D_FF)):
                row = ROW_FCW + FCW_ROWS * a + j
                fcw_ref[a:a + 1, off:off + w] = tot[row:row + 1, 0:w]
        loss_ref[...] = tot[ROW_LOSS:ROW_LOSS + 1, 0:128]

    flat = [t for p in params for t in p]
    out_shape = [_sds(p[0].shape, F32) for p in params for _ in range(4)]
    out_shape += [_sds((3, CONV_W), F32), _sds((3, 2 * D_FF), F32), _sds((1, 128), F32)]
    res = pl.pallas_call(body, name="small_sums_adamw", out_shape=out_shape)(r_small, r_dmix, r_dbin, *flat)
    return [tuple(res[4 * k:4 * k + 4]) for k in range(5)], res[20], res[21], res[22]


def _adamw_pair(a, b):
    def body(*refs):
        for k in range(2):
            w_ref, g_ref, m_ref, v_ref = refs[4 * k:4 * k + 4]
            d_ref, m2_ref, v2_ref = refs[8 + 3 * k:8 + 3 * k + 3]
            d_ref[...], m2_ref[...], v2_ref[...] = _adam_math(w_ref[...], g_ref[...], m_ref[...], v_ref[...])

    out_shape = [_sds(a[0].shape, F32)] * 3 + [_sds(b[0].shape, F32)] * 3
    res = pl.pallas_call(body, name="adamw_conv_weights", out_shape=out_shape)(*a, *b)
    return tuple(res[:3]), tuple(res[3:])


def _pad_cols(a, c):
    return jnp.pad(a, ((0, 0), (0, c - a.shape[1])))


def _to_col_slabs(g):
    r = g.shape[0]
    return jnp.transpose(g.reshape(r, N_DEV, 128), (1, 0, 2)).reshape(N_DEV * r, 128)


def _from_col_slabs(t):
    r = t.shape[0] // N_DEV
    return jnp.transpose(t.reshape(N_DEV, r, 128), (1, 0, 2)).reshape(r, N_DEV * 128)


def _slots(t):
    return t.reshape(N_DEV, t.shape[0] // N_DEV, t.shape[1])


def kernel(x, mix_norm, w_in, b_in, sinks, conv_w, w_attn_branch, w_conv_branch, w_out, ffn_norm, w_up, ffn_conv_w, w_down, final_norm, loss_target, m_mix_norm, m_w_in, m_b_in, m_sinks, m_conv_w, m_w_attn_branch, m_w_conv_branch, m_w_out, m_ffn_norm, m_w_up, m_ffn_conv_w, m_w_down, m_final_norm, v_mix_norm, v_w_in, v_b_in, v_sinks, v_conv_w, v_w_attn_branch, v_w_conv_branch, v_w_out, v_ffn_norm, v_w_up, v_ffn_conv_w, v_w_down, v_final_norm):
    xs, tgt = x[0], loss_target[0]
    me = 4 * lax.axis_index("x") + 2 * lax.axis_index("y") + lax.axis_index("c")
    in_rows, up_rows = IN_W // N_DEV, 2 * D_FF // N_DEV

    conv_sh = jnp.concatenate([_pad_cols(ffn_conv_w[0], 768), _pad_cols(conv_w[0], 768),
                               jnp.zeros((2, 768), F32)], axis=0)
    win_sh, wup_sh = w_in[0].T.astype(BF16), w_up[0].T.astype(BF16)
    wout_sh, wdown_sh = w_out[0].astype(BF16), w_down[0].astype(BF16)
    wa_sh, wc_sh = w_attn_branch[0].astype(BF16), w_conv_branch[0].astype(BF16)

    half = D_MODEL // 2
    (win_t,) = _exchange_only(_AllGather([win_sh]), "gather_w_in")
    (xn, qkv, cbx, gates), (wa_s, wc_s, wout, conv_g) = _norm_inproj(
        xs, mix_norm, win_t, b_in, _AllGather([wa_sh, wc_sh, wout_sh, conv_sh], pass_on_at=0.875))
    (attn, lse), (wup_lo,) = _attn_fwd(qkv, sinks, _AllGather([wup_sh], pass_on_at=0.875, cols=(0, half)))
    wa, wc = _from_col_slabs(wa_s), _from_col_slabs(wc_s)
    conv_g = conv_g.reshape(N_DEV, 8, 768)
    fcw = jnp.transpose(conv_g[:, 0:3, :up_rows], (1, 0, 2)).reshape(3, 2 * D_FF)
    cw = jnp.transpose(conv_g[:, 3:6, :CONV_W // N_DEV], (1, 0, 2)).reshape(3, CONV_W)
    (h1,), (wup_hi,) = _mix_fwd(xs, cbx, gates, attn, cw, wa, wc, wout,
                                _AllGather([wup_sh], pass_on_at=0.875, cols=(half, half)))
    (hn, up_pre, up), (wdown,) = _ffn_up(h1, ffn_norm, wup_lo, wup_hi, fcw,
                                         _AllGather([wdown_sh], pass_on_at=0.75))
    act, dh2, loss_p, dfn_p = _ffn_down_loss(up, wdown, h1, final_norm.reshape(1, D_MODEL), tgt)

    dn_rows, q_up = D_FF // N_DEV, up_rows // 4
    g_wdown = _matmul_tn(act, dh2, FF_GRAD_ROWS, "grad_w_down")
    (dup_pre, dfcw_p, dh1, dffn_p), (r_wdown,) = _ffn_bwd(dh2, wdown, up, up_pre, fcw, wup_lo, wup_hi, h1, ffn_norm,
                                                         _ReduceScatter([(g_wdown, 0, dn_rows)]))
    g_wup_t = _matmul_tn(dup_pre, hn, FF_GRAD_ROWS, "grad_w_up")
    (dgates, dattn, dcb, dcc, dcx, dcw_p, g_wout, g_wa_nat, g_wc_nat), (r_wup_ab,) = _mix_bwd(
        dh1, wout, gates, attn, wa, wc, cbx, cw, _ReduceScatter([(g_wup_t, 0, 2 * q_up)]))
    g_wa, g_wc = _to_col_slabs(g_wa_nat), _to_col_slabs(g_wc_nat)
    (dq, dk, dv, dsink_p), (r_wup_c, r_wout, r_wa, r_wc) = _attn_bwd(
        qkv, sinks, attn, lse, dattn,
        _ReduceScatter([(g_wup_t, 2 * q_up, q_up), (g_wout, 0, D_MODEL // N_DEV), (g_wa, 0, ATTN_W),
                        (g_wc, 0, CONV_W)]))
    dproj = (dq, dk, dv, dcb, dcc, dcx, dgates)
    small = _pack_small(dffn_p, dfn_p, dsink_p, loss_p, dcw_p, dfcw_p)
    g_win_t, (r_wup_d, r_small) = _grad_w_in(dproj, xn, _ReduceScatter([(g_wup_t, 3 * q_up, q_up)], [small]))
    (win_theirs,) = _exchange_only(_PairExchange([g_win_t]), "pair_exchange_w_in")
    q_win = _pair_add(g_win_t, win_theirs, in_rows // 2, "pair_add_w_in")
    (dx, _, _), (r_win, r_dbin, r_dmix) = _inproj_bwd(
        dproj, win_t, xs, mix_norm, dh1,
        _ChipExchangeThenBroadcast([q_win], late_from=(1, 2), late_shapes=[(8, IN_W), (8, D_MODEL)]))

    fn2, m_fn2, v_fn2 = (t.reshape(1, D_MODEL) for t in (final_norm, m_final_norm, v_final_norm))
    small_res, g_cw_full, g_fcw_full, loss_row = _small_sums_adamw(
        _slots(r_small), r_dmix, r_dbin,
        [(mix_norm, m_mix_norm, v_mix_norm), (b_in, m_b_in, v_b_in), (sinks, m_sinks, v_sinks),
         (ffn_norm, m_ffn_norm, v_ffn_norm), (fn2, m_fn2, v_fn2)])
    loss = loss_row[0, 0]
    g_cw = lax.dynamic_slice_in_dim(g_cw_full, me * (CONV_W // N_DEV), CONV_W // N_DEV, axis=1)
    g_fcw = lax.dynamic_slice_in_dim(g_fcw_full, me * up_rows, up_rows, axis=1)
    taps = lambda t: jnp.transpose(t, (1, 0, 2))
    g_cw, g_fcw = g_cw[:, None, :], g_fcw[:, None, :]
    cw_res, fcw_res = _adamw_pair((taps(conv_w), g_cw, taps(m_conv_w), taps(v_conv_w)),
                                  (taps(ffn_conv_w), g_fcw, taps(m_ffn_conv_w), taps(v_ffn_conv_w)))

    big = {}
    big["w_in"] = tuple(t.T for t in _sum_parts_adamw(
        [r_win.reshape(4, in_rows, D_MODEL)], w_in[0].T, m_w_in[0].T, v_w_in[0].T, in_rows // 2, "adamw_w_in"))
    big["w_up"] = tuple(t.T for t in _sum_parts_adamw(
        [_slots(r_wup_ab), _slots(r_wup_c), _slots(r_wup_d)], w_up[0].T, m_w_up[0].T, v_w_up[0].T, q_up,
        "adamw_w_up"))
    big["w_out"] = _sum_adamw(_slots(r_wout), w_out[0], m_w_out[0], v_w_out[0], 128, "adamw_w_out")
    big["w_down"] = _sum_adamw(_slots(r_wdown), w_down[0], m_w_down[0], v_w_down[0], dn_rows // 2, "adamw_w_down")
    big["w_attn_branch"] = _sum_adamw(_slots(r_wa), w_attn_branch[0], m_w_attn_branch[0], v_w_attn_branch[0], 256,
                                      "adamw_w_attn_branch")
    big["w_conv_branch"] = _sum_adamw(_slots(r_wc), w_conv_branch[0], m_w_conv_branch[0], v_w_conv_branch[0], 256,
                                      "adamw_w_conv_branch")

    res = dict(zip(("mix_norm", "b_in", "sinks", "ffn_norm"), small_res[:4]))
    res["final_norm"] = tuple(t.reshape(final_norm.shape) for t in small_res[4])
    res["conv_w"] = tuple(jnp.transpose(t, (1, 0, 2)) for t in (g_cw,) + cw_res)
    res["ffn_conv_w"] = tuple(jnp.transpose(t, (1, 0, 2)) for t in (g_fcw,) + fcw_res)
    for name, ref_w in (("w_in", w_in), ("w_up", w_up), ("w_out", w_out), ("w_down", w_down),
                        ("w_attn_branch", w_attn_branch), ("w_conv_branch", w_conv_branch)):
        res[name] = tuple(t.reshape(ref_w.shape) for t in big[name])

    order = ["mix_norm", "w_in", "b_in", "sinks", "conv_w", "w_attn_branch", "w_conv_branch", "w_out",
             "ffn_norm", "w_up", "ffn_conv_w", "w_down", "final_norm"]
    out = [loss, dx.reshape(x.shape)]
    for k in range(4):
        out += [res[name][k] for name in order]
    return tuple(out)
```

```python
import math

import jax
import jax.numpy as jnp
from jax import lax
from jax.experimental import pallas as pl
from jax.experimental.pallas import tpu as pltpu

F32 = jnp.float32
BF16 = jnp.bfloat16
MESH = pl.DeviceIdType.MESH
N_DEV = 8

D_MODEL = 1024
HEAD_DIM = 64
N_HEADS = 8
BLOCK = 128
ATTN_W = 512
KV_W = 128
CONV_W = 512
QKV_W = ATTN_W + 2 * KV_W
CBX_W = 3 * CONV_W
GATE_W = 2 * D_MODEL
IN_W = QKV_W + CBX_W + GATE_W
D_FF = 2816
FF_CHUNK = 256
FF_GRAD_ROWS = 1408
NORM_EPS = 1e-5
ATTN_SCALE = HEAD_DIM ** -0.5
NEG = -1e30
HALO = 16

ADAM_LR = 0.001
ADAM_B1 = 0.9
ADAM_B2 = 0.999
ADAM_EPS = 1e-08
ADAM_WD = 0.01
ADAM_STEP = 10

VMEM_LIMIT = 56 * 1024 * 1024
SMALL_ROWS = 32

NT = (((1,), (1,)), ((), ()))
TN = (((0,), (0,)), ((), ()))
ANY = pl.BlockSpec(memory_space=pl.ANY)


def _sig(v):
    return 1.0 / (1.0 + jnp.exp(-v))


def _row_tile(s, pref=256):
    return pref if s % pref == 0 else s


def _shifts_down(u, halo, ks):
    ext = jnp.concatenate([halo, u], axis=0)
    return [pltpu.roll(ext, k, axis=0)[HALO:, :] for k in ks]


def _shifts_up(u, halo, ks):
    n = u.shape[0]
    ext = jnp.concatenate([u, halo], axis=0)
    return [pltpu.roll(ext, n + HALO - k, axis=0)[:n, :] for k in ks]


def _rows_reversed(tm, c, steps):
    return pl.BlockSpec((tm, c), lambda i: (steps - 1 - i, 0))


def _prev_halo_map_reversed(tm, steps):
    return lambda i: (jnp.maximum((steps - 1 - i) * (tm // HALO) - 1, 0), 0)


def _prev_halo_map(tm):
    return lambda i: (jnp.maximum(i * (tm // HALO) - 1, 0), 0)


def _full(shape):
    return pl.BlockSpec(shape, lambda *_: (0,) * len(shape))


def _resident(shape):
    return pl.BlockSpec(shape, lambda *_: (0,) * len(shape), pipeline_mode=pl.Buffered(1))


def _rows(tm, c):
    return pl.BlockSpec((tm, c), lambda i: (i, 0))


def _sds(shape, dtype):
    return jax.ShapeDtypeStruct(shape, dtype)


def _my_place():
    x, y, c = lax.axis_index("x"), lax.axis_index("y"), lax.axis_index("c")
    return x, y, c


ALL_PEERS = tuple((j >> 2, (j >> 1) & 1, j & 1) for j in range(1, N_DEV))
SIBLING_PEER = ((0, 0, 1),)
CHIP_PEERS = ((0, 1, 0), (1, 0, 0), (1, 1, 0))
BARRIER_ID = {ALL_PEERS: 0, SIBLING_PEER: 1, CHIP_PEERS: 2}


def _barrier_signal(peers):
    x, y, c = _my_place()
    barrier = pltpu.get_barrier_semaphore()
    for dx, dy, dc in peers:
        pl.semaphore_signal(barrier, inc=1, device_id=(x ^ dx, y ^ dy, c ^ dc), device_id_type=MESH)


def _barrier_wait(peers):
    pl.semaphore_wait(pltpu.get_barrier_semaphore(), len(peers))


def _start_exchange(remote, local):
    for cp in local + remote:
        cp.start()


def _finish_exchange(remote, local):
    for cp in remote:
        cp.wait_recv()
    for cp in remote:
        cp.wait_send()
    for cp in local:
        cp.wait()


class _AllGather:
    peers = ALL_PEERS
    SLOTS = 10

    def __init__(self, shards, pass_on_at=None, forward_at=None):
        self.ins = [s[0] if isinstance(s, tuple) else s for s in shards]
        self.cols = [s[1:] if isinstance(s, tuple) else None for s in shards]
        self.middle_at, self.forward_at = pass_on_at, forward_at
        assert pass_on_at is None or forward_at is not None
        n = len(shards)
        self.out_shape = [_sds((N_DEV * s.shape[0], s.shape[1] if c is None else c[1]), s.dtype)
                          for s, c in zip(self.ins, self.cols)]
        self.sems = [pltpu.SemaphoreType.DMA((self.SLOTS * n,)), pltpu.SemaphoreType.DMA((self.SLOTS * n,)),
                     pltpu.SemaphoreType.DMA((n,))]

    def _plan(self, ins, outs, sems):
        send_sems, recv_sems, local_sems = sems
        x, y, c = _my_place()
        me, sibling = (x, y, c), (x, y, 1 - c)
        x_chip, y_chip, far_chip = (1 - x, y), (x, 1 - y), (1 - x, 1 - y)
        sends, lands, mine = [], [], []
        for k in range(len(ins)):
            r = ins[k].shape[0]
            h = (r // 2) // 16 * 16
            whole, first, second = (0, r), (0, h), (h, r - h)

            def rows(dev, rng, k=k, r=r):
                start = pl.multiple_of((4 * dev[0] + 2 * dev[1] + dev[2]) * r + rng[0], 8)
                return outs[k].at[pl.ds(start, rng[1]), :]

            def own(rng, k=k):
                cols = self.cols[k]
                if cols is None:
                    return ins[k].at[pl.ds(rng[0], rng[1]), :]
                return ins[k].at[pl.ds(rng[0], rng[1]), pl.ds(cols[0], cols[1])]

            def copy(slot, block, rng, to, mine_src=False, k=k, rows=rows, own=own):
                if rng[1] == 0:
                    return None
                return pltpu.make_async_remote_copy(
                    src_ref=own(rng) if mine_src else rows(block, rng), dst_ref=rows(block, rng),
                    send_sem=send_sems.at[self.SLOTS * k + slot], recv_sem=recv_sems.at[self.SLOTS * k + slot],
                    device_id=to, device_id_type=MESH)

            sends.append([
                copy(0, me, whole, sibling, True),
                copy(1, me, first, (*x_chip, c), True),
                copy(2, me, second, (*x_chip, c), True),
                copy(3, me, second, (*y_chip, c), True),
                copy(4, me, first, (*y_chip, c), True),
                copy(5, (*x_chip, c), first, (*y_chip, c)),
                copy(6, (*y_chip, c), second, (*x_chip, c)),
                copy(7, (*x_chip, c), whole, sibling),
                copy(8, (*y_chip, c), whole, sibling),
                copy(9, (*far_chip, c), whole, sibling)])
            lands.append([
                copy(0, sibling, whole, me),
                copy(1, (*x_chip, c), first, me), copy(2, (*x_chip, c), second, me),
                copy(3, (*y_chip, c), second, me), copy(4, (*y_chip, c), first, me),
                copy(5, (*far_chip, c), first, me), copy(6, (*far_chip, c), second, me),
                copy(7, (*x_chip, 1 - c), whole, me), copy(8, (*y_chip, 1 - c), whole, me),
                copy(9, (*far_chip, 1 - c), whole, me)])
            mine.append(pltpu.make_async_copy(own(whole), rows(me, whole), local_sems.at[k]))
        return sends, lands, mine

    @staticmethod
    def _then(lands, waits, sends, starts):
        for slot in waits:
            if lands[slot] is not None:
                lands[slot].wait_recv()
        for slot in starts:
            if sends[slot] is not None:
                sends[slot].start()

    def start(self, ins, outs, sems):
        sends, lands, mine = self._plan(ins, outs, sems)
        for cp in mine:
            cp.start()
        for slot in (1, 3, 0, 2, 4):
            for s in sends:
                self._then(None, (), s, (slot,))

    def forward(self, ins, outs, sems):
        sends, lands, _ = self._plan(ins, outs, sems)
        for s, l in zip(sends, lands):
            self._then(l, (1,), s, (5,))
            self._then(l, (3,), s, (6,))

    def middle(self, ins, outs, sems):
        sends, lands, _ = self._plan(ins, outs, sems)
        for s, l in zip(sends, lands):
            self._then(l, (2,), s, (7,))
            self._then(l, (4,), s, (8,))
        for s, l in zip(sends, lands):
            self._then(l, (5, 6), s, (9,))

    def finish(self, ins, outs, sems):
        if self.forward_at is None:
            self.forward(ins, outs, sems)
        if self.middle_at is None:
            self.middle(ins, outs, sems)
        sends, lands, mine = self._plan(ins, outs, sems)
        for s, l in zip(sends, lands):
            self._then(l, (0, 7, 8, 9), s, ())
        for s in sends:
            for cp in s:
                if cp is not None:
                    cp.wait_send()
        for cp in mine:
            cp.wait()


class _ReduceScatter:
    peers = ALL_PEERS

    def __init__(self, parts, bcast=()):
        self.parts = [(lo, cnt) for _, lo, cnt in parts]
        self.n_parts = len(parts)
        self.ins = [a for a, _, _ in parts] + list(bcast)
        self.out_shape = [_sds((N_DEV * cnt, a.shape[1]), a.dtype) for a, _, cnt in parts]
        self.out_shape += [_sds((N_DEV * b.shape[0], b.shape[1]), b.dtype) for b in bcast]
        n = len(self.ins)
        self.sems = [pltpu.SemaphoreType.DMA((7 * n,)), pltpu.SemaphoreType.DMA((7 * n,)),
                     pltpu.SemaphoreType.DMA((n,))]

    def _copies(self, ins, outs, sems):
        send_sems, recv_sems, local_sems = sems
        x, y, c = _my_place()
        me_idx = 4 * x + 2 * y + c
        remote, local = [], []
        for k in range(len(ins)):
            cnt = outs[k].shape[0] // N_DEV
            dst = outs[k].at[pl.ds(pl.multiple_of(me_idx * cnt, 8), cnt), :]
            if k < self.n_parts:
                lo, _ = self.parts[k]
                r = ins[k].shape[0] // N_DEV
                src_of = lambda idx: ins[k].at[pl.ds(pl.multiple_of(idx * r + lo, 8), cnt), :]
            else:
                src_of = lambda idx: ins[k]
            local.append(pltpu.make_async_copy(src_of(me_idx), dst, local_sems.at[k]))
            for j in range(1, N_DEV):
                peer = (x ^ (j >> 2), y ^ ((j >> 1) & 1), c ^ (j & 1))
                peer_idx = 4 * peer[0] + 2 * peer[1] + peer[2]
                remote.append(pltpu.make_async_remote_copy(
                    src_ref=src_of(peer_idx), dst_ref=dst,
                    send_sem=send_sems.at[7 * k + j - 1], recv_sem=recv_sems.at[7 * k + j - 1],
                    device_id=peer, device_id_type=MESH))
        return remote, local

    def start(self, ins, outs, sems):
        _start_exchange(*self._copies(ins, outs, sems))

    def finish(self, ins, outs, sems):
        _finish_exchange(*self._copies(ins, outs, sems))


class _PairExchange:
    peers = SIBLING_PEER

    def __init__(self, arrays):
        self.ins = list(arrays)
        n = len(arrays)
        self.out_shape = [_sds((a.shape[0] // 2, a.shape[1]), a.dtype) for a in arrays]
        self.sems = [pltpu.SemaphoreType.DMA((4 * n,)), pltpu.SemaphoreType.DMA((4 * n,))]

    def _copies(self, ins, outs, sems):
        send_sems, recv_sems = sems
        x, y, c = _my_place()
        remote = []
        for k in range(len(ins)):
            r = ins[k].shape[0] // N_DEV
            for chip in range(4):
                sib = ins[k].at[pl.ds(pl.multiple_of((2 * chip + 1 - c) * r, 8), r), :]
                remote.append(pltpu.make_async_remote_copy(
                    src_ref=sib, dst_ref=outs[k].at[pl.ds(chip * r, r), :],
                    send_sem=send_sems.at[4 * k + chip], recv_sem=recv_sems.at[4 * k + chip],
                    device_id=(x, y, 1 - c), device_id_type=MESH))
        return remote

    def start(self, ins, outs, sems):
        for cp in self._copies(ins, outs, sems):
            cp.start()

    def finish(self, ins, outs, sems):
        remote = self._copies(ins, outs, sems)
        for cp in remote:
            cp.wait_recv()
        for cp in remote:
            cp.wait_send()


class _ChipExchange:
    peers = CHIP_PEERS

    def __init__(self, arrays):
        self.ins = list(arrays)
        self.out_shape = [_sds(a.shape, a.dtype) for a in arrays]
        n = len(self.ins)
        self.sems = [pltpu.SemaphoreType.DMA((3 * n,)), pltpu.SemaphoreType.DMA((3 * n,)),
                     pltpu.SemaphoreType.DMA((n,))]

    def _copies(self, ins, outs, sems):
        send_sems, recv_sems, local_sems = sems
        x, y, c = _my_place()
        my_chip = 2 * x + y
        remote, local = [], []
        for k in range(len(ins)):
            r = ins[k].shape[0] // 4
            dst = outs[k].at[pl.ds(pl.multiple_of(my_chip * r, 8), r), :]
            local.append(pltpu.make_async_copy(ins[k].at[pl.ds(pl.multiple_of(my_chip * r, 8), r), :], dst,
                                               local_sems.at[k]))
            for j in range(1, 4):
                px, py = x ^ (j >> 1), y ^ (j & 1)
                src = ins[k].at[pl.ds(pl.multiple_of((2 * px + py) * r, 8), r), :]
                remote.append(pltpu.make_async_remote_copy(
                    src_ref=src, dst_ref=dst, send_sem=send_sems.at[3 * k + j - 1],
                    recv_sem=recv_sems.at[3 * k + j - 1], device_id=(px, py, c), device_id_type=MESH))
        return remote, local

    def start(self, ins, outs, sems):
        _start_exchange(*self._copies(ins, outs, sems))

    def finish(self, ins, outs, sems):
        _finish_exchange(*self._copies(ins, outs, sems))


class _ChipExchangeThenBroadcast(_ChipExchange):
    peers = ALL_PEERS
    defer_start = False

    def __init__(self, arrays, late_from, late_shapes):
        super().__init__(arrays)
        self.n_chip = len(arrays)
        self.late_from = tuple(late_from)
        self.out_shape += [_sds((N_DEV * r, c), F32) for r, c in late_shapes]
        m = len(late_shapes)
        self.sems += [pltpu.SemaphoreType.DMA((7 * m,)), pltpu.SemaphoreType.DMA((7 * m,)),
                      pltpu.SemaphoreType.DMA((m,))]

    def _late_copies(self, srcs, outs, sems):
        send_sems, recv_sems, local_sems = sems
        x, y, c = _my_place()
        me_idx = 4 * x + 2 * y + c
        remote, local = [], []
        for k, src in enumerate(srcs):
            r = src.shape[0]
            dst = outs[k].at[pl.ds(pl.multiple_of(me_idx * r, 8), r), :]
            local.append(pltpu.make_async_copy(src, dst, local_sems.at[k]))
            for j, (dx, dy, dc) in enumerate(ALL_PEERS):
                remote.append(pltpu.make_async_remote_copy(
                    src_ref=src, dst_ref=dst, send_sem=send_sems.at[7 * k + j], recv_sem=recv_sems.at[7 * k + j],
                    device_id=(x ^ dx, y ^ dy, c ^ dc), device_id_type=MESH))
        return remote, local

    def start(self, ins, outs, sems):
        _start_exchange(*self._copies(ins, outs[:self.n_chip], sems[:3]))

    def finish(self, ins, outs, sems, late_srcs):
        late = self._late_copies(late_srcs, outs[self.n_chip:], sems[3:])
        _start_exchange(*late)
        _finish_exchange(*self._copies(ins, outs[:self.n_chip], sems[:3]))
        _finish_exchange(*late)


def _pcall(body, name, grid, in_specs, out_specs, out_shape, args, scratch=(), comm=None):
    params = pltpu.CompilerParams(dimension_semantics=("arbitrary",) * len(grid), vmem_limit_bytes=VMEM_LIMIT)
    in_specs, out_specs, out_shape, scratch = list(in_specs), list(out_specs), list(out_shape), list(scratch)
    if comm is None:
        res = pl.pallas_call(body, name=name, grid=grid, in_specs=in_specs, out_specs=out_specs, out_shape=out_shape,
                             scratch_shapes=scratch, compiler_params=params)(*args)
        return list(res), []
    n_in, n_out, n_scr = len(in_specs), len(out_specs), len(scratch)
    ci, co = len(comm.ins), len(comm.out_shape)
    total = math.prod(grid)

    def carried(*refs):
        bounds = [0, n_in, n_in + ci, n_in + ci + n_out, n_in + ci + n_out + co, n_in + ci + n_out + co + n_scr]
        ins, cins, outs, couts, scr = (refs[a:b] for a, b in zip(bounds[:-1], bounds[1:]))
        sems = refs[bounds[-1]:]
        step = pl.program_id(0)
        for d in range(1, len(grid)):
            step = step * grid[d] + pl.program_id(d)

        start_step = min(1, total - 1) if getattr(comm, "defer_start", True) else 0

        @pl.when(step == 0)
        def _():
            _barrier_signal(comm.peers)

        @pl.when(step == start_step)
        def _():
            _barrier_wait(comm.peers)
            comm.start(cins, couts, sems)

        forward_at = getattr(comm, "forward_at", None)
        if forward_at is not None and int(forward_at * total) <= start_step:
            forward_at = comm.forward_at = comm.middle_at = None
        if forward_at is not None:
            @pl.when(step == int(forward_at * total))
            def _():
                comm.forward(cins, couts, sems)

        middle_at = getattr(comm, "middle_at", None)
        if middle_at is not None:
            assert forward_at is None or forward_at <= middle_at
            @pl.when(step == int(middle_at * total))
            def _():
                comm.middle(cins, couts, sems)

        body(*ins, *outs, *scr)

        @pl.when(step == total - 1)
        def _():
            late_from = getattr(comm, "late_from", None)
            if late_from is None:
                comm.finish(cins, couts, sems)
            else:
                comm.finish(cins, couts, sems, [outs[k] for k in late_from])

    params = pltpu.CompilerParams(dimension_semantics=("arbitrary",) * len(grid), vmem_limit_bytes=VMEM_LIMIT,
                                  collective_id=BARRIER_ID[comm.peers])
    res = pl.pallas_call(
        carried, name=name, grid=grid, in_specs=in_specs + [ANY] * ci, out_specs=out_specs + [ANY] * co,
        out_shape=out_shape + comm.out_shape, scratch_shapes=scratch + comm.sems, compiler_params=params,
    )(*args, *comm.ins)
    return list(res[:n_out]), list(res[n_out:])


def _exchange_only(comm, name):
    def body(*refs):
        ci, co = len(comm.ins), len(comm.out_shape)
        _barrier_signal(comm.peers)
        _barrier_wait(comm.peers)
        comm.start(refs[:ci], refs[ci:ci + co], refs[ci + co:])
        comm.finish(refs[:ci], refs[ci:ci + co], refs[ci + co:])

    params = pltpu.CompilerParams(collective_id=BARRIER_ID[comm.peers])
    return pl.pallas_call(body, name=name, out_shape=comm.out_shape, in_specs=[ANY] * len(comm.ins),
                          out_specs=[ANY] * len(comm.out_shape), scratch_shapes=comm.sems,
                          compiler_params=params)(*comm.ins)


def _norm_inproj(x, g, win_t, b_in, comm):
    s = x.shape[0]
    tm = _row_tile(s, 512)
    widths = (QKV_W, CBX_W, GATE_W)

    def body(x_ref, g_ref, w_ref, b_ref, xn_ref, qkv_ref, cbx_ref, gate_ref):
        xv = x_ref[...]
        r = lax.rsqrt(jnp.mean(xv * xv, axis=-1, keepdims=True) + NORM_EPS)
        xn = (xv * r * g_ref[...]).astype(BF16)
        xn_ref[...] = xn
        off = 0
        for o_ref, w in zip((qkv_ref, cbx_ref, gate_ref), widths):
            acc = lax.dot_general(xn, w_ref[off:off + w, :], NT, preferred_element_type=F32)
            o_ref[...] = (acc + b_ref[:, off:off + w]).astype(BF16)
            off += w

    return _pcall(
        body, "norm_inproj", (s // tm,),
        [_rows(tm, D_MODEL), _full((1, D_MODEL)), _resident((IN_W, D_MODEL)), _full((1, IN_W))],
        [_rows(tm, D_MODEL)] + [_rows(tm, w) for w in widths],
        [_sds((s, D_MODEL), BF16)] + [_sds((s, w), BF16) for w in widths],
        (x, g, win_t, b_in), comm=comm)


Q_BLOCKS = 4


def _attn_specs():
    tq = Q_BLOCKS * BLOCK
    prev = lambda n: jnp.maximum(Q_BLOCKS * n - 1, 0)
    return [pl.BlockSpec((tq, ATTN_W), lambda n: (n, 0)),
            pl.BlockSpec((BLOCK, KV_W), lambda n: (prev(n), ATTN_W // KV_W)),
            pl.BlockSpec((tq, KV_W), lambda n: (n, ATTN_W // KV_W)),
            pl.BlockSpec((BLOCK, KV_W), lambda n: (prev(n), ATTN_W // KV_W + 1)),
            pl.BlockSpec((tq, KV_W), lambda n: (n, ATTN_W // KV_W + 1))]


def _window(prev_ref, cur_ref, sub):
    if sub == 0:
        return jnp.concatenate([prev_ref[...], cur_ref[0:BLOCK, :]], axis=0)
    return cur_ref[(sub - 1) * BLOCK:(sub + 1) * BLOCK, :]


def _lower_lanes():
    return lax.broadcasted_iota(jnp.int32, (BLOCK, 128), 1) < HEAD_DIM


def _stack_heads(val, kh):
    lower = _lower_lanes()
    parts = []
    for g in range(4):
        h = kh * 4 + g
        blk = val[:, (h // 2) * 128:(h // 2 + 1) * 128]
        keep = lower if h % 2 == 0 else jnp.logical_not(lower)
        parts.append(jnp.where(keep, blk, jnp.zeros_like(blk)))
    return jnp.concatenate(parts, axis=0)


def _dup_kv(window, kh):
    t = window.astype(F32)
    rolled = pltpu.roll(t, HEAD_DIM, axis=1)
    lower = lax.broadcasted_iota(jnp.int32, t.shape, 1) < HEAD_DIM
    dup = jnp.where(lower, t, rolled) if kh == 0 else jnp.where(lower, rolled, t)
    return dup.astype(BF16)


def _attn_mask(real_prev):
    row = lax.broadcasted_iota(jnp.int32, (4 * BLOCK, 2 * BLOCK), 0)
    kj = lax.broadcasted_iota(jnp.int32, (4 * BLOCK, 2 * BLOCK), 1)
    dist = (row & (BLOCK - 1)) + BLOCK - kj
    band = jnp.logical_and(dist >= 0, dist < BLOCK)
    return jnp.logical_and(band, jnp.logical_or(kj >= BLOCK, real_prev))


def _sink_col(sinks_ref, kh):
    gi = lax.broadcasted_iota(jnp.int32, (4 * BLOCK, 1), 0) // BLOCK
    col = jnp.zeros((4 * BLOCK, 1), F32)
    for g in range(4):
        col = jnp.where(gi == g, sinks_ref[0, kh * 4 + g], col)
    return col


def _attn_fwd(qkv, sinks, comm):
    s = qkv.shape[0]
    tq = Q_BLOCKS * BLOCK

    def body(sinks_ref, q_ref, kp_ref, kc_ref, vp_ref, vc_ref, o_ref, lse_ref):
        n = pl.program_id(0)
        lower = _lower_lanes()
        lane = lax.broadcasted_iota(jnp.int32, (BLOCK, 128), 1)
        for sub in range(Q_BLOCKS):
            rows = slice(sub * BLOCK, (sub + 1) * BLOCK)
            mask = _attn_mask(n > 0 if sub == 0 else True)
            kw, vw = _window(kp_ref, kc_ref, sub), _window(vp_ref, vc_ref, sub)
            qv = q_ref[rows, :]
            lse_out = jnp.zeros((BLOCK, 128), F32)
            for kh in range(2):
                qs = _stack_heads(qv, kh)
                kd, vd = _dup_kv(kw, kh), _dup_kv(vw, kh)
                sc = lax.dot_general(qs, kd, NT, preferred_element_type=F32) * ATTN_SCALE
                sc = jnp.where(mask, sc, NEG)
                sink = _sink_col(sinks_ref, kh)
                m = jnp.maximum(jnp.max(sc, axis=1, keepdims=True), sink)
                p = jnp.exp(sc - m)
                l = jnp.sum(p, axis=1, keepdims=True) + jnp.exp(sink - m)
                o = jnp.dot(p.astype(BF16), vd, preferred_element_type=F32) / l
                lse = m + jnp.log(l)
                for pair in range(2):
                    lo = o[(2 * pair) * BLOCK:(2 * pair + 1) * BLOCK]
                    hi = o[(2 * pair + 1) * BLOCK:(2 * pair + 2) * BLOCK]
                    col = (kh * 2 + pair) * 128
                    o_ref[rows, col:col + 128] = jnp.where(lower, lo, hi).astype(BF16)
                for g in range(4):
                    lse_out = jnp.where(lane == kh * 4 + g, lse[g * BLOCK:(g + 1) * BLOCK], lse_out)
            lse_ref[rows, :] = lse_out

    return _pcall(
        body, "attn_fwd", (s // tq,),
        [pl.BlockSpec(memory_space=pltpu.SMEM)] + _attn_specs(),
        [pl.BlockSpec((tq, ATTN_W), lambda n: (n, 0)), pl.BlockSpec((tq, 128), lambda n: (n, 0))],
        [_sds((s, ATTN_W), BF16), _sds((s, 128), F32)],
        (sinks, qkv, qkv, qkv, qkv, qkv), comm=comm)


def _conv_u(cbx_ref, halo_ref, w_ref, first):
    cb = cbx_ref[:, 0:CONV_W].astype(F32)
    cc = cbx_ref[:, CONV_W:2 * CONV_W].astype(F32)
    cx = cbx_ref[:, 2 * CONV_W:3 * CONV_W].astype(F32)
    u = cc * cx
    uh = halo_ref[:, CONV_W:2 * CONV_W].astype(F32) * halo_ref[:, 2 * CONV_W:3 * CONV_W].astype(F32)
    uh = jnp.where(first, 0.0, uh)
    u1, u2 = _shifts_down(u, uh, (1, 2))
    cv = w_ref[0:1, :] * u2 + w_ref[1:2, :] * u1 + w_ref[2:3, :] * u
    return cb, cc, cx, u, cv


def _mix_fwd(x, cbx, gates, attn, conv_w, wa, wc, wout, comm):
    s = x.shape[0]
    tm = _row_tile(s)

    def body(x_ref, cbx_ref, halo_ref, gate_ref, attn_ref, cw_ref, wa_ref, wc_ref, wo_ref,
             h1_ref):
        first = pl.program_id(0) == 0
        cb, _, _, _, cv = _conv_u(cbx_ref, halo_ref, cw_ref, first)
        conv = (cb * cv).astype(BF16)
        ap = jnp.dot(attn_ref[...], wa_ref[...], preferred_element_type=F32)
        cp = jnp.dot(conv, wc_ref[...], preferred_element_type=F32)
        ga = gate_ref[:, 0:D_MODEL].astype(F32)
        gc = gate_ref[:, D_MODEL:2 * D_MODEL].astype(F32)
        merged = (_sig(ga) * ap + _sig(gc) * cp).astype(BF16)
        h1_ref[...] = x_ref[...] + jnp.dot(merged, wo_ref[...], preferred_element_type=F32)

    return _pcall(
        body, "mix_fwd", (s // tm,),
        [_rows(tm, D_MODEL), _rows(tm, CBX_W), pl.BlockSpec((HALO, CBX_W), _prev_halo_map(tm)),
         _rows(tm, GATE_W), _rows(tm, ATTN_W), _full((3, CONV_W)), _full((ATTN_W, D_MODEL)),
         _full((CONV_W, D_MODEL)), _full((D_MODEL, D_MODEL))],
        [_rows(tm, D_MODEL)], [_sds((s, D_MODEL), F32)],
        (x, cbx, cbx, gates, attn, conv_w, wa, wc, wout), comm=comm)


def _col_offsets(wup_parts):
    widths = [p.shape[1] for p in wup_parts]
    assert sum(widths) == D_MODEL
    return [(sum(widths[:k]), w) for k, w in enumerate(widths)]


def _ffn_up(h1, g, wup_parts, fcw, comm):
    s = h1.shape[0]
    tm = _row_tile(s)
    cols = _col_offsets(wup_parts)

    def body(h_ref, g_ref, *refs):
        w_refs = refs[:len(cols)]
        fcw_ref, hn_ref, pre_ref, up_ref, carry_ref = refs[len(cols):]

        @pl.when(pl.program_id(0) == 0)
        def _():
            carry_ref[...] = jnp.zeros_like(carry_ref)

        hv = h_ref[...]
        r = lax.rsqrt(jnp.mean(hv * hv, axis=-1, keepdims=True) + NORM_EPS)
        hn = (hv * r * g_ref[...]).astype(BF16)
        hn_ref[...] = hn
        for c in range(2 * D_FF // FF_CHUNK):
            sl = slice(c * FF_CHUNK, (c + 1) * FF_CHUNK)
            acc = None
            for w_ref, (off, w) in zip(w_refs, cols):
                part = lax.dot_general(hn[:, off:off + w], w_ref[sl, :], NT, preferred_element_type=F32)
                acc = part if acc is None else acc + part
            pre_ref[:, sl] = acc.astype(BF16)
            halo = carry_ref[:, sl]
            carry_ref[:, sl] = acc[tm - HALO:, :]
            u1, u2 = _shifts_down(acc, halo, (1, 2))
            w = fcw_ref[:, sl]
            up_ref[:, sl] = (w[0:1] * u2 + w[1:2] * u1 + w[2:3] * acc).astype(BF16)

    return _pcall(
        body, "ffn_up", (s // tm,),
        [_rows(tm, D_MODEL), _full((1, D_MODEL))] + [_resident((2 * D_FF, w)) for _, w in cols]
        + [_full((3, 2 * D_FF))],
        [_rows(tm, D_MODEL), _rows(tm, 2 * D_FF), _rows(tm, 2 * D_FF)],
        [_sds((s, D_MODEL), BF16), _sds((s, 2 * D_FF), BF16), _sds((s, 2 * D_FF), BF16)],
        (h1, g, *wup_parts, fcw), scratch=[pltpu.VMEM((HALO, 2 * D_FF), F32)], comm=comm)


def _ffn_down_loss(up, wdown, h1, fnorm, target):
    s = h1.shape[0]
    tm = _row_tile(s)

    def body(up_ref, wd_ref, h1_ref, fn_ref, t_ref, act_ref, dh2_ref, loss_ref, dfn_ref):
        i = pl.program_id(0)

        @pl.when(i == 0)
        def _():
            loss_ref[...] = jnp.zeros_like(loss_ref)
            dfn_ref[...] = jnp.zeros_like(dfn_ref)

        h2 = h1_ref[...]
        for c in range(D_FF // FF_CHUNK):
            gsl = slice(c * FF_CHUNK, (c + 1) * FF_CHUNK)
            vsl = slice(D_FF + c * FF_CHUNK, D_FF + (c + 1) * FF_CHUNK)
            gate = up_ref[:, gsl].astype(F32)
            val = up_ref[:, vsl].astype(F32)
            act = (gate * _sig(gate) * val).astype(BF16)
            act_ref[:, gsl] = act
            h2 = h2 + jnp.dot(act, wd_ref[gsl, :], preferred_element_type=F32)
        r = lax.rsqrt(jnp.mean(h2 * h2, axis=-1, keepdims=True) + NORM_EPS)
        yhat = h2 * r
        fn = fn_ref[...]
        diff = yhat * fn - t_ref[...]
        loss_ref[...] += 0.5 * jnp.sum(jnp.sum(diff * diff, axis=1, keepdims=True), axis=0, keepdims=True) / D_MODEL
        dy = diff * (1.0 / D_MODEL)
        dfn_ref[...] += jnp.sum(dy * yhat, axis=0, keepdims=True)
        dyh = dy * fn
        dh2_ref[...] = r * (dyh - yhat * jnp.mean(dyh * yhat, axis=-1, keepdims=True))

    return _pcall(
        body, "ffn_down_loss", (s // tm,),
        [_rows(tm, 2 * D_FF), _resident((D_FF, D_MODEL)), _rows(tm, D_MODEL), _full((1, D_MODEL)),
         _rows(tm, D_MODEL)],
        [_rows(tm, D_FF), _rows(tm, D_MODEL), _full((1, 128)), _full((1, D_MODEL))],
        [_sds((s, D_FF), BF16), _sds((s, D_MODEL), F32), _sds((1, 128), F32), _sds((1, D_MODEL), F32)],
        (up, wdown, h1, fnorm, target))[0]


def _ffn_bwd(dh2, wdown, up, up_pre, fcw, wup_parts, h1, g, comm):
    s = dh2.shape[0]
    tm = _row_tile(s)
    cols = _col_offsets(wup_parts)

    chunk = FF_GRAD_ROWS

    def dup_cols(dh, up_ref, wd_ref, c):
        gsl = slice(c * chunk, (c + 1) * chunk)
        vsl = slice(D_FF + c * chunk, D_FF + (c + 1) * chunk)
        dact = lax.dot_general(dh, wd_ref[gsl, :], NT, preferred_element_type=F32)
        gate = up_ref[:, gsl].astype(F32)
        val = up_ref[:, vsl].astype(F32)
        sg = _sig(gate)
        return dact * val * (sg * (1.0 + gate * (1.0 - sg))), dact * gate * sg

    def body(dh_ref, wd_ref, up_ref, x_ref, w_ref, *refs):
        wup_refs = refs[:len(cols)]
        h_ref, g_ref, dx_ref, dw_ref, dh1_ref, dg_ref, carry_ref = refs[len(cols):]

        @pl.when(pl.program_id(0) == 0)
        def _():
            dw_ref[...] = jnp.zeros_like(dw_ref)
            dg_ref[...] = jnp.zeros_like(dg_ref)
            carry_ref[...] = jnp.zeros_like(carry_ref)

        dh2v = dh_ref[...]
        dh = dh2v.astype(BF16)
        dhn = [jnp.zeros((tm, w), F32) for _, w in cols]
        for c in range(D_FF // chunk):
            for d, off in zip(dup_cols(dh, up_ref, wd_ref, c), (c * chunk, D_FF + c * chunk)):
                sl = slice(off, off + chunk)
                dn = carry_ref[:, sl]
                carry_ref[:, sl] = d[0:HALO, :]
                xv = x_ref[:, sl].astype(F32)
                wv = w_ref[:, sl]
                d1, d2 = _shifts_up(d, dn, (1, 2))
                dx = (wv[2:3] * d + wv[1:2] * d1 + wv[0:1] * d2).astype(BF16)
                dx_ref[:, sl] = dx
                dhn = [a + jnp.dot(dx, wup_ref[sl, :], preferred_element_type=F32)
                       for a, wup_ref in zip(dhn, wup_refs)]
                dw_ref[0:1, sl] += jnp.sum(d2 * xv, axis=0, keepdims=True)
                dw_ref[1:2, sl] += jnp.sum(d1 * xv, axis=0, keepdims=True)
                dw_ref[2:3, sl] += jnp.sum(d * xv, axis=0, keepdims=True)
        dx1, dg = _norm_bwd_tile(h_ref[...], g_ref[...], jnp.concatenate(dhn, axis=1))
        dg_ref[...] += dg
        dh1_ref[...] = dh2v + dx1

    rows = lambda c: _rows_reversed(tm, c, s // tm)
    return _pcall(
        body, "ffn_bwd", (s // tm,),
        [rows(D_MODEL), _resident((D_FF, D_MODEL)), rows(2 * D_FF), rows(2 * D_FF), _full((3, 2 * D_FF))]
        + [_resident((2 * D_FF, w)) for _, w in cols] + [rows(D_MODEL), _full((1, D_MODEL))],
        [rows(2 * D_FF), _full((3, 2 * D_FF)), rows(D_MODEL), _full((1, D_MODEL))],
        [_sds((s, 2 * D_FF), BF16), _sds((3, 2 * D_FF), F32), _sds((s, D_MODEL), F32), _sds((1, D_MODEL), F32)],
        (dh2, wdown, up, up_pre, fcw, *wup_parts, h1, g),
        scratch=[pltpu.VMEM((HALO, 2 * D_FF), F32)], comm=comm)


def _matmul_tn(a, b, tk, name, ts=1024, comm=None):
    s, ka = a.shape
    n = b.shape[1]
    ts = min(ts, s)
    steps = s // ts

    def body(a_ref, b_ref, o_ref, acc_ref):
        j = pl.program_id(1)

        @pl.when(j == 0)
        def _():
            acc_ref[...] = jnp.zeros_like(acc_ref)

        acc_ref[...] += lax.dot_general(a_ref[...].astype(BF16), b_ref[...].astype(BF16), TN,
                                        preferred_element_type=F32)

        @pl.when(j == steps - 1)
        def _():
            o_ref[...] = acc_ref[...].astype(BF16)

    outs, couts = _pcall(
        body, name, (ka // tk, steps),
        [pl.BlockSpec((ts, tk), lambda i, j: (j, i)), pl.BlockSpec((ts, n), lambda i, j: (j, 0))],
        [pl.BlockSpec((tk, n), lambda i, j: (i, 0))], [_sds((ka, n), BF16)],
        (a, b), scratch=[pltpu.VMEM((tk, n), F32)], comm=comm)
    return outs[0] if comm is None else (outs[0], couts)


def _norm_bwd_tile(xv, g, dy):
    r = lax.rsqrt(jnp.mean(xv * xv, axis=-1, keepdims=True) + NORM_EPS)
    xhat = xv * r
    dg = jnp.sum(dy * xhat, axis=0, keepdims=True)
    dyh = dy * g
    return r * (dyh - xhat * jnp.mean(dyh * xhat, axis=-1, keepdims=True)), dg


def _mix_bwd(dh1, wout, gates, attn, wa, wc, cbx, conv_w, comm):
    s = dh1.shape[0]
    tm = _row_tile(s)
    steps = s // tm

    def body(dh_ref, wo_ref, gate_ref, attn_ref, wa_ref, wc_ref, cbx_ref, halo_ref,
             cw_ref, dg_ref, dattn_ref, dcb_ref, dcc_ref, dcx_ref, dw_ref, gwo_ref, gwa_ref, gwc_ref,
             acc_o, acc_a, acc_c, carry_ref):
        i = pl.program_id(0)

        @pl.when(i == 0)
        def _():
            dw_ref[...] = jnp.zeros_like(dw_ref)
            acc_o[...] = jnp.zeros_like(acc_o)
            acc_a[...] = jnp.zeros_like(acc_a)
            acc_c[...] = jnp.zeros_like(acc_c)
            carry_ref[...] = jnp.zeros_like(carry_ref)

        cb, cc, cx, u, cv = _conv_u(cbx_ref, halo_ref, cw_ref, i == steps - 1)
        attn = attn_ref[...]
        conv = (cb * cv).astype(BF16)
        ap = jnp.dot(attn, wa_ref[...], preferred_element_type=F32)
        cp = jnp.dot(conv, wc_ref[...], preferred_element_type=F32)
        dhb = dh_ref[...].astype(BF16)
        dm = lax.dot_general(dhb, wo_ref[...], NT, preferred_element_type=F32)
        sa = _sig(gate_ref[:, 0:D_MODEL].astype(F32))
        sc = _sig(gate_ref[:, D_MODEL:2 * D_MODEL].astype(F32))
        merged = (sa * ap + sc * cp).astype(BF16)
        da = (dm * sa).astype(BF16)
        dc = (dm * sc).astype(BF16)
        dg_ref[:, 0:D_MODEL] = (dm * ap * sa * (1.0 - sa)).astype(BF16)
        dg_ref[:, D_MODEL:2 * D_MODEL] = (dm * cp * sc * (1.0 - sc)).astype(BF16)
        dattn_ref[...] = lax.dot_general(da, wa_ref[...], NT, preferred_element_type=F32).astype(BF16)
        dconv = lax.dot_general(dc, wc_ref[...], NT, preferred_element_type=F32)
        dcb_ref[...] = (dconv * cv).astype(BF16)
        d = dconv * cb
        dn = carry_ref[...]
        carry_ref[...] = d[0:HALO, :]
        d1, d2 = _shifts_up(d, dn, (1, 2))
        du = cw_ref[2:3, :] * d + cw_ref[1:2, :] * d1 + cw_ref[0:1, :] * d2
        dcc_ref[...] = (du * cx).astype(BF16)
        dcx_ref[...] = (du * cc).astype(BF16)
        dw_ref[0:1, :] += jnp.sum(d2 * u, axis=0, keepdims=True)
        dw_ref[1:2, :] += jnp.sum(d1 * u, axis=0, keepdims=True)
        dw_ref[2:3, :] += jnp.sum(d * u, axis=0, keepdims=True)
        acc_o[...] += lax.dot_general(merged, dhb, TN, preferred_element_type=F32)
        acc_a[...] += lax.dot_general(attn, da, TN, preferred_element_type=F32)
        acc_c[...] += lax.dot_general(conv, dc, TN, preferred_element_type=F32)

        @pl.when(i == steps - 1)
        def _():
            gwo_ref[...] = acc_o[...].astype(BF16)
            gwa_ref[...] = acc_a[...].astype(BF16)
            gwc_ref[...] = acc_c[...].astype(BF16)

    rows = lambda c: _rows_reversed(tm, c, steps)
    return _pcall(
        body, "mix_bwd", (steps,),
        [rows(D_MODEL), _full((D_MODEL, D_MODEL)), rows(GATE_W), rows(ATTN_W), _full((ATTN_W, D_MODEL)),
         _full((CONV_W, D_MODEL)), rows(CBX_W), pl.BlockSpec((HALO, CBX_W), _prev_halo_map_reversed(tm, steps)),
         _full((3, CONV_W))],
        [rows(GATE_W), rows(ATTN_W), rows(CONV_W), rows(CONV_W), rows(CONV_W),
         _full((3, CONV_W)), _full((D_MODEL, D_MODEL)), _full((ATTN_W, D_MODEL)), _full((CONV_W, D_MODEL))],
        [_sds((s, GATE_W), BF16), _sds((s, ATTN_W), BF16), _sds((s, CONV_W), BF16), _sds((s, CONV_W), BF16),
         _sds((s, CONV_W), BF16), _sds((3, CONV_W), F32), _sds((D_MODEL, D_MODEL), BF16),
         _sds((ATTN_W, D_MODEL), BF16), _sds((CONV_W, D_MODEL), BF16)],
        (dh1, wout, gates, attn, wa, wc, cbx, cbx, conv_w),
        scratch=[pltpu.VMEM((D_MODEL, D_MODEL), F32), pltpu.VMEM((ATTN_W, D_MODEL), F32),
                 pltpu.VMEM((CONV_W, D_MODEL), F32), pltpu.VMEM((HALO, CONV_W), F32)], comm=comm)


def _attn_bwd(qkv, sinks, attn, lse, dattn, comm):
    s = qkv.shape[0]
    tq = Q_BLOCKS * BLOCK

    def body(sinks_ref, q_ref, kp_ref, kc_ref, vp_ref, vc_ref, o_ref, lse_ref, do_ref,
             dq_ref, dk_ref, dv_ref, ds_ref):
        n = pl.program_id(0)

        @pl.when(n == 0)
        def _():
            dk_ref[...] = jnp.zeros_like(dk_ref)
            dv_ref[...] = jnp.zeros_like(dv_ref)
            ds_ref[...] = jnp.zeros_like(ds_ref)

        lower = _lower_lanes()
        lane = lax.broadcasted_iota(jnp.int32, (BLOCK, 128), 1)
        lower2 = lax.broadcasted_iota(jnp.int32, (2 * BLOCK, 128), 1) < HEAD_DIM
        lane1 = lax.broadcasted_iota(jnp.int32, (1, 128), 1)
        dsink = jnp.zeros((1, 128), F32)
        for sub in reversed(range(Q_BLOCKS)):
            rows = slice(sub * BLOCK, (sub + 1) * BLOCK)
            mask = _attn_mask(n > 0 if sub == 0 else True)
            kw, vw = _window(kp_ref, kc_ref, sub), _window(vp_ref, vc_ref, sub)
            qv, ov, dov, lsev = q_ref[rows, :], o_ref[rows, :], do_ref[rows, :], lse_ref[rows, :]
            dk_fold, dv_fold = [], []
            for kh in range(2):
                qs = _stack_heads(qv, kh)
                dos = _stack_heads(dov, kh)
                os_ = _stack_heads(ov, kh)
                kd, vd = _dup_kv(kw, kh), _dup_kv(vw, kh)
                lse = jnp.concatenate(
                    [jnp.sum(jnp.where(lane == kh * 4 + g, lsev, 0.0), axis=1, keepdims=True) for g in range(4)],
                    axis=0)
                sc = lax.dot_general(qs, kd, NT, preferred_element_type=F32) * ATTN_SCALE
                p = jnp.exp(jnp.where(mask, sc, NEG) - lse)
                dp = lax.dot_general(dos, vd, NT, preferred_element_type=F32)
                delta = jnp.sum(dos.astype(F32) * os_.astype(F32), axis=1, keepdims=True)
                dsc = (p * (dp - delta) * ATTN_SCALE).astype(BF16)
                dqs = jnp.dot(dsc, kd, preferred_element_type=F32)
                for pair in range(2):
                    lo = dqs[(2 * pair) * BLOCK:(2 * pair + 1) * BLOCK]
                    hi = dqs[(2 * pair + 1) * BLOCK:(2 * pair + 2) * BLOCK]
                    col = (kh * 2 + pair) * 128
                    dq_ref[rows, col:col + 128] = jnp.where(lower, lo, hi).astype(BF16)
                dkd = lax.dot_general(dsc, qs, TN, preferred_element_type=F32)
                dvd = lax.dot_general(p.astype(BF16), dos, TN, preferred_element_type=F32)
                dk_fold.append(dkd + pltpu.roll(dkd, HEAD_DIM, axis=1))
                dv_fold.append(dvd + pltpu.roll(dvd, HEAD_DIM, axis=1))
                psink = jnp.exp(_sink_col(sinks_ref, kh) - lse) * delta
                for g in range(4):
                    tot = jnp.sum(psink[g * BLOCK:(g + 1) * BLOCK], axis=0, keepdims=True)
                    dsink = dsink - jnp.where(lane1 == kh * 4 + g, tot, 0.0)
            dk2 = jnp.where(lower2, dk_fold[0], dk_fold[1])
            dv2 = jnp.where(lower2, dv_fold[0], dv_fold[1])
            cur = pl.ds(pl.multiple_of((Q_BLOCKS * n + sub) * BLOCK, BLOCK), BLOCK)
            dk_ref[cur, :] += dk2[BLOCK:]
            dv_ref[cur, :] += dv2[BLOCK:]
            if sub > 0:
                prev = pl.ds(pl.multiple_of((Q_BLOCKS * n + sub - 1) * BLOCK, BLOCK), BLOCK)
                dk_ref[prev, :] += dk2[:BLOCK]
                dv_ref[prev, :] += dv2[:BLOCK]
        ds_ref[...] += dsink

        @pl.when(n > 0)
        def _():
            prev = pl.ds(pl.multiple_of((Q_BLOCKS * n - 1) * BLOCK, BLOCK), BLOCK)
            dk_ref[prev, :] += dk2[:BLOCK]
            dv_ref[prev, :] += dv2[:BLOCK]

    blk = lambda w: pl.BlockSpec((tq, w), lambda n: (n, 0))
    return _pcall(
        body, "attn_bwd", (s // tq,),
        [pl.BlockSpec(memory_space=pltpu.SMEM)] + _attn_specs() + [blk(ATTN_W), blk(128), blk(ATTN_W)],
        [blk(ATTN_W), _full((s, KV_W)), _full((s, KV_W)), _full((1, 128))],
        [_sds((s, ATTN_W), BF16), _sds((s, KV_W), F32), _sds((s, KV_W), F32), _sds((1, 128), F32)],
        (sinks, qkv, qkv, qkv, qkv, qkv, attn, lse, dattn), comm=comm)


DPROJ_PIECES = (ATTN_W, KV_W, KV_W, CONV_W, CONV_W, CONV_W, GATE_W)
DPROJ_OFFSETS = tuple(sum(DPROJ_PIECES[:k]) for k in range(len(DPROJ_PIECES)))


def _grad_w_in(pieces, xn, comm):
    s = xn.shape[0]
    ts = min(1024, s)
    steps = s // ts
    rows0 = DPROJ_OFFSETS[6]

    def body(*refs):
        p_refs, b_ref, o_ref, acc_ref, stage_ref, sem = refs[:7], refs[7], refs[8], refs[9], refs[10], refs[11]
        i, j = pl.program_id(0), pl.program_id(1)

        @pl.when(j == 0)
        def _():
            acc_ref[...] = jnp.zeros_like(acc_ref)

        bv = b_ref[...]

        def flush(lo, n):
            stage_ref[0:n, :] = acc_ref[0:n, :].astype(BF16)
            cp = pltpu.make_async_copy(stage_ref.at[0:n, :], o_ref.at[lo:lo + n, :], sem)
            cp.start()
            cp.wait()

        @pl.when(i == 0)
        def _():
            for p_ref, off, w in zip(p_refs[:6], DPROJ_OFFSETS[:6], DPROJ_PIECES[:6]):
                acc_ref[off:off + w, :] += lax.dot_general(p_ref[...].astype(BF16), bv, TN,
                                                           preferred_element_type=F32)

            @pl.when(j == steps - 1)
            def _():
                flush(0, rows0)

        @pl.when(i == 1)
        def _():
            acc_ref[0:GATE_W, :] += lax.dot_general(p_refs[6][...], bv, TN, preferred_element_type=F32)

            @pl.when(j == steps - 1)
            def _():
                flush(rows0, GATE_W)

    def piece_spec(w, group):
        return pl.BlockSpec((ts, w), lambda i, j: (jnp.where(i == group, j, 0), 0))

    outs, couts = _pcall(
        body, "grad_w_in", (2, steps),
        [piece_spec(w, 0) for w in DPROJ_PIECES[:6]] + [piece_spec(GATE_W, 1),
                                                         pl.BlockSpec((ts, D_MODEL), lambda i, j: (j, 0))],
        [ANY], [_sds((IN_W, D_MODEL), BF16)], (*pieces, xn),
        scratch=[pltpu.VMEM((rows0, D_MODEL), F32), pltpu.VMEM((rows0, D_MODEL), BF16), pltpu.SemaphoreType.DMA],
        comm=comm)
    return outs[0], couts


def _inproj_bwd(pieces, win_t, x, g, dh1, comm):
    s = x.shape[0]
    tm = _row_tile(s, 512)

    def body(*refs):
        p_refs = refs[:7]
        w_ref, x_ref, g_ref, dh_ref, dx_ref, db_ref, dg_ref = refs[7:]

        @pl.when(pl.program_id(0) == 0)
        def _():
            db_ref[...] = jnp.zeros_like(db_ref)
            dg_ref[...] = jnp.zeros_like(dg_ref)

        dxn = jnp.zeros((tm, D_MODEL), F32)
        for p_ref, off, w in zip(p_refs, DPROJ_OFFSETS, DPROJ_PIECES):
            v = p_ref[...].astype(BF16)
            db_ref[:, off:off + w] += jnp.sum(v.astype(F32), axis=0, keepdims=True)
            dxn = dxn + jnp.dot(v, w_ref[off:off + w, :], preferred_element_type=F32)
        dx, dg = _norm_bwd_tile(x_ref[...], g_ref[...], dxn)
        dg_ref[...] += dg
        dx_ref[...] = dh_ref[...] + dx

    return _pcall(
        body, "inproj_bwd", (s // tm,),
        [_rows(tm, w) for w in DPROJ_PIECES] + [_resident((IN_W, D_MODEL)), _rows(tm, D_MODEL), _full((1, D_MODEL)),
                                                _rows(tm, D_MODEL)],
        [_rows(tm, D_MODEL), _full((8, IN_W)), _full((8, D_MODEL))],
        [_sds((s, D_MODEL), F32), _sds((8, IN_W), F32), _sds((8, D_MODEL), F32)],
        (*pieces, win_t, x, g, dh1), comm=comm)


def _adam_math(w, g, m, v):
    m2 = ADAM_B1 * m + (1.0 - ADAM_B1) * g
    v2 = ADAM_B2 * v + (1.0 - ADAM_B2) * (g * g)
    m_hat = m2 / (1.0 - ADAM_B1 ** ADAM_STEP)
    v_hat = v2 / (1.0 - ADAM_B2 ** ADAM_STEP)
    delta = -ADAM_LR * (m_hat / (jnp.sqrt(v_hat) + ADAM_EPS) + ADAM_WD * w)
    return delta, m2, v2


def _sum_slots(ref):
    tot = ref[0].astype(F32)
    for i in range(1, ref.shape[0]):
        tot = tot + ref[i].astype(F32)
    return tot


def _pair_add(partials, theirs, tr, name):
    r = partials.shape[0] // N_DEV
    c = partials.shape[1]
    nt = r // tr
    core = lax.axis_index("c").astype(jnp.int32).reshape(1)

    def body(core_ref, a_ref, b_ref, o_ref):
        o_ref[...] = (a_ref[...].astype(F32) + b_ref[...].astype(F32)).astype(BF16)

    grid_spec = pltpu.PrefetchScalarGridSpec(
        num_scalar_prefetch=1, grid=(4 * nt,),
        in_specs=[pl.BlockSpec((None, None, tr, c), lambda i, core_ref: (i // nt, core_ref[0], i % nt, 0)),
                  pl.BlockSpec((tr, c), lambda i, core_ref: (i, 0))],
        out_specs=pl.BlockSpec((tr, c), lambda i, core_ref: (i, 0)))
    return pl.pallas_call(body, name=name, grid_spec=grid_spec, out_shape=_sds((4 * r, c), BF16))(
        core, partials.reshape(4, 2, r, c), theirs)


def _sum_adamw(parts, w, m, v, tr, name):
    r, c = w.shape

    def body(p_ref, w_ref, m_ref, v_ref, g_ref, d_ref, m2_ref, v2_ref):
        g = _sum_slots(p_ref)
        g_ref[...] = g
        d_ref[...], m2_ref[...], v2_ref[...] = _adam_math(w_ref[...], g, m_ref[...], v_ref[...])

    spec = pl.BlockSpec((tr, c), lambda i: (i, 0))
    return _pcall(body, name, (r // tr,), [pl.BlockSpec((N_DEV, tr, c), lambda i: (0, i, 0)), spec, spec, spec],
                  [spec] * 4, [_sds((r, c), F32)] * 4, (parts, w, m, v))[0]


def _sum_parts_adamw(parts, w, m, v, tr, name):
    c = w.shape[1]
    tiles = [p.shape[1] // tr for p in parts]
    starts = [sum(tiles[:k]) for k in range(len(parts))]
    n_parts = len(parts)

    def body(*refs):
        p_refs = refs[:n_parts]
        w_ref, m_ref, v_ref, g_ref, d_ref, m2_ref, v2_ref = refs[n_parts:]
        i = pl.program_id(0)
        for p_ref, st, nt in zip(p_refs, starts, tiles):
            @pl.when(jnp.logical_and(i >= st, i < st + nt))
            def _(p_ref=p_ref):
                g_ref[...] = _sum_slots(p_ref)

        d_ref[...], m2_ref[...], v2_ref[...] = _adam_math(w_ref[...], g_ref[...], m_ref[...], v_ref[...])

    def part_spec(p, st, nt):
        return pl.BlockSpec((p.shape[0], tr, c), lambda i: (0, jnp.clip(i - st, 0, nt - 1), 0))

    spec = pl.BlockSpec((tr, c), lambda i: (i, 0))
    return _pcall(
        body, name, (sum(tiles),),
        [part_spec(p, st, nt) for p, st, nt in zip(parts, starts, tiles)] + [spec, spec, spec],
        [spec] * 4, [_sds(w.shape, F32)] * 4, (*parts, w, m, v))[0]


ROW_MIX, ROW_FFN, ROW_FINAL, ROW_SINKS, ROW_LOSS, ROW_BIN, ROW_CW, ROW_FCW = 0, 1, 2, 3, 4, 5, 10, 13
FCW_ROWS = 6


def _wide_pieces(width):
    return [(k * D_MODEL, min(D_MODEL, width - k * D_MODEL)) for k in range(-(-width // D_MODEL))]


def _pack_small(dffn, dfn, dsink, loss, dcw, dfcw):
    def body(ffn_ref, fn_ref, sink_ref, loss_ref, cw_ref, fcw_ref, o_ref):
        o_ref[...] = jnp.zeros_like(o_ref)
        o_ref[ROW_FFN:ROW_FFN + 1, :] = ffn_ref[...]
        o_ref[ROW_FINAL:ROW_FINAL + 1, :] = fn_ref[...]
        o_ref[ROW_SINKS:ROW_SINKS + 1, 0:128] = sink_ref[...]
        o_ref[ROW_LOSS:ROW_LOSS + 1, 0:128] = loss_ref[...]
        o_ref[ROW_CW:ROW_CW + 3, 0:CONV_W] = cw_ref[...]
        for a in range(3):
            for k, (off, w) in enumerate(_wide_pieces(2 * D_FF)):
                row = ROW_FCW + FCW_ROWS * a + k
                o_ref[row:row + 1, 0:w] = fcw_ref[a:a + 1, off:off + w]

    return pl.pallas_call(body, name="pack_small", out_shape=_sds((SMALL_ROWS, D_MODEL), F32))(
        dffn, dfn, dsink, loss, dcw, dfcw)


def _small_sums_adamw(r_small, r_dmix, r_dbin, params):
    rows = (None, None, ROW_SINKS, ROW_FFN, ROW_FINAL)

    def sum_row0(ref):
        tot = ref[0:1, :]
        for i in range(1, N_DEV):
            tot = tot + ref[8 * i:8 * i + 1, :]
        return tot

    def body(*refs):
        r_ref, late_refs, p_refs, o_refs = refs[0], refs[1:3], refs[3:18], refs[18:]
        tot = _sum_slots(r_ref)
        for k, row in enumerate(rows):
            w_ref, m_ref, v_ref = p_refs[3 * k:3 * k + 3]
            g_ref, d_ref, m2_ref, v2_ref = o_refs[4 * k:4 * k + 4]
            if row is None:
                g_ref[...] = sum_row0(late_refs[k])
            else:
                for j, (off, w) in enumerate(_wide_pieces(w_ref.shape[1])):
                    g_ref[:, off:off + w] = tot[row + j:row + j + 1, 0:w]
            d_ref[...], m2_ref[...], v2_ref[...] = _adam_math(w_ref[...], g_ref[...], m_ref[...], v_ref[...])
        cw_ref, fcw_ref, loss_ref = o_refs[20:]
        cw_ref[...] = tot[ROW_CW:ROW_CW + 3, 0:CONV_W]
        for a in range(3):
            for j, (off, w) in enumerate(_wide_pieces(2 * D_FF)):
                row = ROW_FCW + FCW_ROWS * a + j
                fcw_ref[a:a + 1, off:off + w] = tot[row:row + 1, 0:w]
        loss_ref[...] = tot[ROW_LOSS:ROW_LOSS + 1, 0:128]

    flat = [t for p in params for t in p]
    out_shape = [_sds(p[0].shape, F32) for p in params for _ in range(4)]
    out_shape += [_sds((3, CONV_W), F32), _sds((3, 2 * D_FF), F32), _sds((1, 128), F32)]
    res = pl.pallas_call(body, name="small_sums_adamw", out_shape=out_shape)(r_small, r_dmix, r_dbin, *flat)
    return [tuple(res[4 * k:4 * k + 4]) for k in range(5)], res[20], res[21], res[22]


def _adamw_pair(a, b):
    def body(*refs):
        for k in range(2):
            w_ref, g_ref, m_ref, v_ref = refs[4 * k:4 * k + 4]
            d_ref, m2_ref, v2_ref = refs[8 + 3 * k:8 + 3 * k + 3]
            d_ref[...], m2_ref[...], v2_ref[...] = _adam_math(w_ref[...], g_ref[...], m_ref[...], v_ref[...])

    out_shape = [_sds(a[0].shape, F32)] * 3 + [_sds(b[0].shape, F32)] * 3
    res = pl.pallas_call(body, name="adamw_conv_weights", out_shape=out_shape)(*a, *b)
    return tuple(res[:3]), tuple(res[3:])


def _pad_cols(a, c):
    return jnp.pad(a, ((0, 0), (0, c - a.shape[1])))


def _to_col_slabs(g):
    r = g.shape[0]
    return jnp.transpose(g.reshape(r, N_DEV, 128), (1, 0, 2)).reshape(N_DEV * r, 128)


def _from_col_slabs(t):
    r = t.shape[0] // N_DEV
    return jnp.transpose(t.reshape(N_DEV, r, 128), (1, 0, 2)).reshape(r, N_DEV * 128)


def _slots(t):
    return t.reshape(N_DEV, t.shape[0] // N_DEV, t.shape[1])


def kernel(x, mix_norm, w_in, b_in, sinks, conv_w, w_attn_branch, w_conv_branch, w_out, ffn_norm, w_up, ffn_conv_w, w_down, final_norm, loss_target, m_mix_norm, m_w_in, m_b_in, m_sinks, m_conv_w, m_w_attn_branch, m_w_conv_branch, m_w_out, m_ffn_norm, m_w_up, m_ffn_conv_w, m_w_down, m_final_norm, v_mix_norm, v_w_in, v_b_in, v_sinks, v_conv_w, v_w_attn_branch, v_w_conv_branch, v_w_out, v_ffn_norm, v_w_up, v_ffn_conv_w, v_w_down, v_final_norm):
    xs, tgt = x[0], loss_target[0]
    me = 4 * lax.axis_index("x") + 2 * lax.axis_index("y") + lax.axis_index("c")
    in_rows, up_rows = IN_W // N_DEV, 2 * D_FF // N_DEV

    conv_sh = jnp.concatenate([_pad_cols(ffn_conv_w[0], 768), _pad_cols(conv_w[0], 768),
                               jnp.zeros((2, 768), F32)], axis=0)
    win_sh, wup_sh = w_in[0].T.astype(BF16), w_up[0].T.astype(BF16)
    wout_sh, wdown_sh = w_out[0].astype(BF16), w_down[0].astype(BF16)
    wa_sh, wc_sh = w_attn_branch[0].astype(BF16), w_conv_branch[0].astype(BF16)

    quarter, half = D_MODEL // 4, D_MODEL // 2
    phases = dict(forward_at=0.375, pass_on_at=0.875)
    (win_t,) = _exchange_only(_AllGather([win_sh]), "gather_w_in")
    (xn, qkv, cbx, gates), (wa_s, wc_s, wout, conv_g, wup_a) = _norm_inproj(
        xs, mix_norm, win_t, b_in,
        _AllGather([wa_sh, wc_sh, wout_sh, conv_sh, (wup_sh, 0, quarter)], **phases))
    (attn, lse), (wup_b,) = _attn_fwd(qkv, sinks, _AllGather([(wup_sh, quarter, quarter)], **phases))
    wa, wc = _from_col_slabs(wa_s), _from_col_slabs(wc_s)
    conv_g = conv_g.reshape(N_DEV, 8, 768)
    fcw = jnp.transpose(conv_g[:, 0:3, :up_rows], (1, 0, 2)).reshape(3, 2 * D_FF)
    cw = jnp.transpose(conv_g[:, 3:6, :CONV_W // N_DEV], (1, 0, 2)).reshape(3, CONV_W)
    (h1,), (wup_c,) = _mix_fwd(xs, cbx, gates, attn, cw, wa, wc, wout,
                               _AllGather([(wup_sh, half, half)], **phases))
    wup_parts = (wup_a, wup_b, wup_c)
    (hn, up_pre, up), (wdown,) = _ffn_up(h1, ffn_norm, wup_parts, fcw,
                                         _AllGather([wdown_sh], forward_at=0.25, pass_on_at=0.75))
    act, dh2, loss_p, dfn_p = _ffn_down_loss(up, wdown, h1, final_norm.reshape(1, D_MODEL), tgt)

    dn_rows, q_up = D_FF // N_DEV, up_rows // 4
    g_wdown = _matmul_tn(act, dh2, FF_GRAD_ROWS, "grad_w_down")
    (dup_pre, dfcw_p, dh1, dffn_p), (r_wdown,) = _ffn_bwd(dh2, wdown, up, up_pre, fcw, wup_parts, h1, ffn_norm,
                                                         _ReduceScatter([(g_wdown, 0, dn_rows)]))
    g_wup_t = _matmul_tn(dup_pre, hn, FF_GRAD_ROWS, "grad_w_up")
    (dgates, dattn, dcb, dcc, dcx, dcw_p, g_wout, g_wa_nat, g_wc_nat), (r_wup_ab,) = _mix_bwd(
        dh1, wout, gates, attn, wa, wc, cbx, cw, _ReduceScatter([(g_wup_t, 0, 2 * q_up)]))
    g_wa, g_wc = _to_col_slabs(g_wa_nat), _to_col_slabs(g_wc_nat)
    (dq, dk, dv, dsink_p), (r_wup_c, r_wout, r_wa, r_wc) = _attn_bwd(
        qkv, sinks, attn, lse, dattn,
        _ReduceScatter([(g_wup_t, 2 * q_up, q_up), (g_wout, 0, D_MODEL // N_DEV), (g_wa, 0, ATTN_W),
                        (g_wc, 0, CONV_W)]))
    dproj = (dq, dk, dv, dcb, dcc, dcx, dgates)
    small = _pack_small(dffn_p, dfn_p, dsink_p, loss_p, dcw_p, dfcw_p)
    g_win_t, (r_wup_d, r_small) = _grad_w_in(dproj, xn, _ReduceScatter([(g_wup_t, 3 * q_up, q_up)], [small]))
    (win_theirs,) = _exchange_only(_PairExchange([g_win_t]), "pair_exchange_w_in")
    q_win = _pair_add(g_win_t, win_theirs, in_rows // 2, "pair_add_w_in")
    (dx, _, _), (r_win, r_dbin, r_dmix) = _inproj_bwd(
        dproj, win_t, xs, mix_norm, dh1,
        _ChipExchangeThenBroadcast([q_win], late_from=(1, 2), late_shapes=[(8, IN_W), (8, D_MODEL)]))

    fn2, m_fn2, v_fn2 = (t.reshape(1, D_MODEL) for t in (final_norm, m_final_norm, v_final_norm))
    small_res, g_cw_full, g_fcw_full, loss_row = _small_sums_adamw(
        _slots(r_small), r_dmix, r_dbin,
        [(mix_norm, m_mix_norm, v_mix_norm), (b_in, m_b_in, v_b_in), (sinks, m_sinks, v_sinks),
         (ffn_norm, m_ffn_norm, v_ffn_norm), (fn2, m_fn2, v_fn2)])
    loss = loss_row[0, 0]
    g_cw = lax.dynamic_slice_in_dim(g_cw_full, me * (CONV_W // N_DEV), CONV_W // N_DEV, axis=1)
    g_fcw = lax.dynamic_slice_in_dim(g_fcw_full, me * up_rows, up_rows, axis=1)
    taps = lambda t: jnp.transpose(t, (1, 0, 2))
    g_cw, g_fcw = g_cw[:, None, :], g_fcw[:, None, :]
    cw_res, fcw_res = _adamw_pair((taps(conv_w), g_cw, taps(m_conv_w), taps(v_conv_w)),
                                  (taps(ffn_conv_w), g_fcw, taps(m_ffn_conv_w), taps(v_ffn_conv_w)))

    big = {}
    big["w_in"] = tuple(t.T for t in _sum_parts_adamw(
        [r_win.reshape(4, in_rows, D_MODEL)], w_in[0].T, m_w_in[0].T, v_w_in[0].T, in_rows // 2, "adamw_w_in"))
    big["w_up"] = tuple(t.T for t in _sum_parts_adamw(
        [_slots(r_wup_ab), _slots(r_wup_c), _slots(r_wup_d)], w_up[0].T, m_w_up[0].T, v_w_up[0].T, q_up,
        "adamw_w_up"))
    big["w_out"] = _sum_adamw(_slots(r_wout), w_out[0], m_w_out[0], v_w_out[0], 128, "adamw_w_out")
    big["w_down"] = _sum_adamw(_slots(r_wdown), w_down[0], m_w_down[0], v_w_down[0], dn_rows // 2, "adamw_w_down")
    big["w_attn_branch"] = _sum_adamw(_slots(r_wa), w_attn_branch[0], m_w_attn_branch[0], v_w_attn_branch[0], 256,
                                      "adamw_w_attn_branch")
    big["w_conv_branch"] = _sum_adamw(_slots(r_wc), w_conv_branch[0], m_w_conv_branch[0], v_w_conv_branch[0], 256,
                                      "adamw_w_conv_branch")

    res = dict(zip(("mix_norm", "b_in", "sinks", "ffn_norm"), small_res[:4]))
    res["final_norm"] = tuple(t.reshape(final_norm.shape) for t in small_res[4])
    res["conv_w"] = tuple(jnp.transpose(t, (1, 0, 2)) for t in (g_cw,) + cw_res)
    res["ffn_conv_w"] = tuple(jnp.transpose(t, (1, 0, 2)) for t in (g_fcw,) + fcw_res)
    for name, ref_w in (("w_in", w_in), ("w_up", w_up), ("w_out", w_out), ("w_down", w_down),
                        ("w_attn_branch", w_attn_branch), ("w_conv_branch", w_conv_branch)):
        res[name] = tuple(t.reshape(ref_w.shape) for t in big[name])

    order = ["mix_norm", "w_in", "b_in", "sinks", "conv_w", "w_attn_branch", "w_conv_branch", "w_out",
             "ffn_norm", "w_up", "ffn_conv_w", "w_down", "final_norm"]
    out = [loss, dx.reshape(x.shape)]
    for k in range(4):
        out += [res[name][k] for name in order]
    return tuple(out)
```

```python
import math

import jax
import jax.numpy as jnp
from jax import lax
from jax.experimental import pallas as pl
from jax.experimental.pallas import tpu as pltpu

F32 = jnp.float32
BF16 = jnp.bfloat16
MESH = pl.DeviceIdType.MESH
N_DEV = 8

D_MODEL = 1024
HEAD_DIM = 64
N_HEADS = 8
BLOCK = 128
ATTN_W = 512
KV_W = 128
CONV_W = 512
QKV_W = ATTN_W + 2 * KV_W
CBX_W = 3 * CONV_W
GATE_W = 2 * D_MODEL
IN_W = QKV_W + CBX_W + GATE_W
D_FF = 2816
FF_CHUNK = 256
FF_GRAD_ROWS = 1408
NORM_EPS = 1e-5
ATTN_SCALE = HEAD_DIM ** -0.5
NEG = -1e30
HALO = 16

ADAM_LR = 0.001
ADAM_B1 = 0.9
ADAM_B2 = 0.999
ADAM_EPS = 1e-08
ADAM_WD = 0.01
ADAM_STEP = 10

VMEM_LIMIT = 56 * 1024 * 1024
SMALL_ROWS = 32

NT = (((1,), (1,)), ((), ()))
TN = (((0,), (0,)), ((), ()))
ANY = pl.BlockSpec(memory_space=pl.ANY)


def _sig(v):
    return 1.0 / (1.0 + jnp.exp(-v))


def _row_tile(s, pref=256):
    return pref if s % pref == 0 else s


def _shifts_down(u, halo, ks):
    ext = jnp.concatenate([halo, u], axis=0)
    return [pltpu.roll(ext, k, axis=0)[HALO:, :] for k in ks]


def _shifts_up(u, halo, ks):
    n = u.shape[0]
    ext = jnp.concatenate([u, halo], axis=0)
    return [pltpu.roll(ext, n + HALO - k, axis=0)[:n, :] for k in ks]


def _rows_reversed(tm, c, steps):
    return pl.BlockSpec((tm, c), lambda i: (steps - 1 - i, 0))


def _prev_halo_map_reversed(tm, steps):
    return lambda i: (jnp.maximum((steps - 1 - i) * (tm // HALO) - 1, 0), 0)


def _prev_halo_map(tm):
    return lambda i: (jnp.maximum(i * (tm // HALO) - 1, 0), 0)


def _full(shape):
    return pl.BlockSpec(shape, lambda *_: (0,) * len(shape))


def _resident(shape):
    return pl.BlockSpec(shape, lambda *_: (0,) * len(shape), pipeline_mode=pl.Buffered(1))


def _rows(tm, c):
    return pl.BlockSpec((tm, c), lambda i: (i, 0))


def _sds(shape, dtype):
    return jax.ShapeDtypeStruct(shape, dtype)


def _my_place():
    x, y, c = lax.axis_index("x"), lax.axis_index("y"), lax.axis_index("c")
    return x, y, c


ALL_PEERS = tuple((j >> 2, (j >> 1) & 1, j & 1) for j in range(1, N_DEV))
SIBLING_PEER = ((0, 0, 1),)
CHIP_PEERS = ((0, 1, 0), (1, 0, 0), (1, 1, 0))
BARRIER_ID = {ALL_PEERS: 0, SIBLING_PEER: 1, CHIP_PEERS: 2}


def _barrier_signal(peers):
    x, y, c = _my_place()
    barrier = pltpu.get_barrier_semaphore()
    for dx, dy, dc in peers:
        pl.semaphore_signal(barrier, inc=1, device_id=(x ^ dx, y ^ dy, c ^ dc), device_id_type=MESH)


def _barrier_wait(peers):
    pl.semaphore_wait(pltpu.get_barrier_semaphore(), len(peers))


def _start_exchange(remote, local):
    for cp in local + remote:
        cp.start()


def _finish_exchange(remote, local):
    for cp in remote:
        cp.wait_recv()
    for cp in remote:
        cp.wait_send()
    for cp in local:
        cp.wait()


class _AllGather:
    peers = ALL_PEERS
    SLOTS = 10

    def __init__(self, shards, pass_on_at=None, forward_at=None):
        self.ins = [s[0] if isinstance(s, tuple) else s for s in shards]
        self.cols = [s[1:] if isinstance(s, tuple) else None for s in shards]
        self.middle_at, self.forward_at = pass_on_at, forward_at
        assert pass_on_at is None or forward_at is not None
        n = len(shards)
        self.out_shape = [_sds((N_DEV * s.shape[0], s.shape[1] if c is None else c[1]), s.dtype)
                          for s, c in zip(self.ins, self.cols)]
        self.sems = [pltpu.SemaphoreType.DMA((self.SLOTS * n,)), pltpu.SemaphoreType.DMA((self.SLOTS * n,)),
                     pltpu.SemaphoreType.DMA((n,))]

    def _plan(self, ins, outs, sems):
        send_sems, recv_sems, local_sems = sems
        x, y, c = _my_place()
        me, sibling = (x, y, c), (x, y, 1 - c)
        x_chip, y_chip, far_chip = (1 - x, y), (x, 1 - y), (1 - x, 1 - y)
        sends, lands, mine = [], [], []
        for k in range(len(ins)):
            r = ins[k].shape[0]
            h = (r // 2) // 16 * 16
            whole, first, second = (0, r), (0, h), (h, r - h)

            def rows(dev, rng, k=k, r=r):
                start = pl.multiple_of((4 * dev[0] + 2 * dev[1] + dev[2]) * r + rng[0], 8)
                return outs[k].at[pl.ds(start, rng[1]), :]

            def own(rng, k=k):
                cols = self.cols[k]
                if cols is None:
                    return ins[k].at[pl.ds(rng[0], rng[1]), :]
                return ins[k].at[pl.ds(rng[0], rng[1]), pl.ds(cols[0], cols[1])]

            def copy(slot, block, rng, to, mine_src=False, k=k, rows=rows, own=own):
                if rng[1] == 0:
                    return None
                return pltpu.make_async_remote_copy(
                    src_ref=own(rng) if mine_src else rows(block, rng), dst_ref=rows(block, rng),
                    send_sem=send_sems.at[self.SLOTS * k + slot], recv_sem=recv_sems.at[self.SLOTS * k + slot],
                    device_id=to, device_id_type=MESH)

            sends.append([
                copy(0, me, whole, sibling, True),
                copy(1, me, first, (*x_chip, c), True),
                copy(2, me, second, (*x_chip, c), True),
                copy(3, me, second, (*y_chip, c), True),
                copy(4, me, first, (*y_chip, c), True),
                copy(5, (*x_chip, c), first, (*y_chip, c)),
                copy(6, (*y_chip, c), second, (*x_chip, c)),
                copy(7, (*x_chip, c), whole, sibling),
                copy(8, (*y_chip, c), whole, sibling),
                copy(9, (*far_chip, c), whole, sibling)])
            lands.append([
                copy(0, sibling, whole, me),
                copy(1, (*x_chip, c), first, me), copy(2, (*x_chip, c), second, me),
                copy(3, (*y_chip, c), second, me), copy(4, (*y_chip, c), first, me),
                copy(5, (*far_chip, c), first, me), copy(6, (*far_chip, c), second, me),
                copy(7, (*x_chip, 1 - c), whole, me), copy(8, (*y_chip, 1 - c), whole, me),
                copy(9, (*far_chip, 1 - c), whole, me)])
            mine.append(pltpu.make_async_copy(own(whole), rows(me, whole), local_sems.at[k]))
        return sends, lands, mine

    @staticmethod
    def _then(lands, waits, sends, starts):
        for slot in waits:
            if lands[slot] is not None:
                lands[slot].wait_recv()
        for slot in starts:
            if sends[slot] is not None:
                sends[slot].start()

    def start(self, ins, outs, sems):
        sends, lands, mine = self._plan(ins, outs, sems)
        for cp in mine:
            cp.start()
        for slot in (1, 3, 0, 2, 4):
            for s in sends:
                self._then(None, (), s, (slot,))

    def forward(self, ins, outs, sems):
        sends, lands, _ = self._plan(ins, outs, sems)
        for s, l in zip(sends, lands):
            self._then(l, (1,), s, (5,))
            self._then(l, (3,), s, (6,))

    def middle(self, ins, outs, sems):
        sends, lands, _ = self._plan(ins, outs, sems)
        for s, l in zip(sends, lands):
            self._then(l, (2,), s, (7,))
            self._then(l, (4,), s, (8,))
        for s, l in zip(sends, lands):
            self._then(l, (5, 6), s, (9,))

    def finish(self, ins, outs, sems):
        if self.forward_at is None:
            self.forward(ins, outs, sems)
        if self.middle_at is None:
            self.middle(ins, outs, sems)
        sends, lands, mine = self._plan(ins, outs, sems)
        for s, l in zip(sends, lands):
            self._then(l, (0, 7, 8, 9), s, ())
        for s in sends:
            for cp in s:
                if cp is not None:
                    cp.wait_send()
        for cp in mine:
            cp.wait()


class _ReduceScatter:
    peers = ALL_PEERS

    def __init__(self, parts, bcast=()):
        self.parts = [(lo, cnt) for _, lo, cnt in parts]
        self.n_parts = len(parts)
        self.ins = [a for a, _, _ in parts] + list(bcast)
        self.out_shape = [_sds((N_DEV * cnt, a.shape[1]), a.dtype) for a, _, cnt in parts]
        self.out_shape += [_sds((N_DEV * b.shape[0], b.shape[1]), b.dtype) for b in bcast]
        n = len(self.ins)
        self.sems = [pltpu.SemaphoreType.DMA((7 * n,)), pltpu.SemaphoreType.DMA((7 * n,)),
                     pltpu.SemaphoreType.DMA((n,))]

    def _copies(self, ins, outs, sems):
        send_sems, recv_sems, local_sems = sems
        x, y, c = _my_place()
        me_idx = 4 * x + 2 * y + c
        remote, local = [], []
        for k in range(len(ins)):
            cnt = outs[k].shape[0] // N_DEV
            dst = outs[k].at[pl.ds(pl.multiple_of(me_idx * cnt, 8), cnt), :]
            if k < self.n_parts:
                lo, _ = self.parts[k]
                r = ins[k].shape[0] // N_DEV
                src_of = lambda idx: ins[k].at[pl.ds(pl.multiple_of(idx * r + lo, 8), cnt), :]
            else:
                src_of = lambda idx: ins[k]
            local.append(pltpu.make_async_copy(src_of(me_idx), dst, local_sems.at[k]))
            for j in range(1, N_DEV):
                peer = (x ^ (j >> 2), y ^ ((j >> 1) & 1), c ^ (j & 1))
                peer_idx = 4 * peer[0] + 2 * peer[1] + peer[2]
                remote.append(pltpu.make_async_remote_copy(
                    src_ref=src_of(peer_idx), dst_ref=dst,
                    send_sem=send_sems.at[7 * k + j - 1], recv_sem=recv_sems.at[7 * k + j - 1],
                    device_id=peer, device_id_type=MESH))
        return remote, local

    def start(self, ins, outs, sems):
        _start_exchange(*self._copies(ins, outs, sems))

    def finish(self, ins, outs, sems):
        _finish_exchange(*self._copies(ins, outs, sems))


class _PairExchange:
    peers = SIBLING_PEER

    def __init__(self, arrays):
        self.ins = list(arrays)
        n = len(arrays)
        self.out_shape = [_sds((a.shape[0] // 2, a.shape[1]), a.dtype) for a in arrays]
        self.sems = [pltpu.SemaphoreType.DMA((4 * n,)), pltpu.SemaphoreType.DMA((4 * n,))]

    def _copies(self, ins, outs, sems):
        send_sems, recv_sems = sems
        x, y, c = _my_place()
        remote = []
        for k in range(len(ins)):
            r = ins[k].shape[0] // N_DEV
            for chip in range(4):
                sib = ins[k].at[pl.ds(pl.multiple_of((2 * chip + 1 - c) * r, 8), r), :]
                remote.append(pltpu.make_async_remote_copy(
                    src_ref=sib, dst_ref=outs[k].at[pl.ds(chip * r, r), :],
                    send_sem=send_sems.at[4 * k + chip], recv_sem=recv_sems.at[4 * k + chip],
                    device_id=(x, y, 1 - c), device_id_type=MESH))
        return remote

    def start(self, ins, outs, sems):
        for cp in self._copies(ins, outs, sems):
            cp.start()

    def finish(self, ins, outs, sems):
        remote = self._copies(ins, outs, sems)
        for cp in remote:
            cp.wait_recv()
        for cp in remote:
            cp.wait_send()


class _ChipExchange:
    peers = CHIP_PEERS

    def __init__(self, arrays):
        self.ins = list(arrays)
        self.out_shape = [_sds(a.shape, a.dtype) for a in arrays]
        n = len(self.ins)
        self.sems = [pltpu.SemaphoreType.DMA((3 * n,)), pltpu.SemaphoreType.DMA((3 * n,)),
                     pltpu.SemaphoreType.DMA((n,))]

    def _copies(self, ins, outs, sems):
        send_sems, recv_sems, local_sems = sems
        x, y, c = _my_place()
        my_chip = 2 * x + y
        remote, local = [], []
        for k in range(len(ins)):
            r = ins[k].shape[0] // 4
            dst = outs[k].at[pl.ds(pl.multiple_of(my_chip * r, 8), r), :]
            local.append(pltpu.make_async_copy(ins[k].at[pl.ds(pl.multiple_of(my_chip * r, 8), r), :], dst,
                                               local_sems.at[k]))
            for j in range(1, 4):
                px, py = x ^ (j >> 1), y ^ (j & 1)
                src = ins[k].at[pl.ds(pl.multiple_of((2 * px + py) * r, 8), r), :]
                remote.append(pltpu.make_async_remote_copy(
                    src_ref=src, dst_ref=dst, send_sem=send_sems.at[3 * k + j - 1],
                    recv_sem=recv_sems.at[3 * k + j - 1], device_id=(px, py, c), device_id_type=MESH))
        return remote, local

    def start(self, ins, outs, sems):
        _start_exchange(*self._copies(ins, outs, sems))

    def finish(self, ins, outs, sems):
        _finish_exchange(*self._copies(ins, outs, sems))


class _ChipExchangeThenBroadcast(_ChipExchange):
    peers = ALL_PEERS
    defer_start = False

    def __init__(self, arrays, late_from, late_shapes):
        super().__init__(arrays)
        self.n_chip = len(arrays)
        self.late_from = tuple(late_from)
        self.out_shape += [_sds((N_DEV * r, c), F32) for r, c in late_shapes]
        m = len(late_shapes)
        self.sems += [pltpu.SemaphoreType.DMA((7 * m,)), pltpu.SemaphoreType.DMA((7 * m,)),
                      pltpu.SemaphoreType.DMA((m,))]

    def _late_copies(self, srcs, outs, sems):
        send_sems, recv_sems, local_sems = sems
        x, y, c = _my_place()
        me_idx = 4 * x + 2 * y + c
        remote, local = [], []
        for k, src in enumerate(srcs):
            r = src.shape[0]
            dst = outs[k].at[pl.ds(pl.multiple_of(me_idx * r, 8), r), :]
            local.append(pltpu.make_async_copy(src, dst, local_sems.at[k]))
            for j, (dx, dy, dc) in enumerate(ALL_PEERS):
                remote.append(pltpu.make_async_remote_copy(
                    src_ref=src, dst_ref=dst, send_sem=send_sems.at[7 * k + j], recv_sem=recv_sems.at[7 * k + j],
                    device_id=(x ^ dx, y ^ dy, c ^ dc), device_id_type=MESH))
        return remote, local

    def start(self, ins, outs, sems):
        _start_exchange(*self._copies(ins, outs[:self.n_chip], sems[:3]))

    def finish(self, ins, outs, sems, late_srcs):
        late = self._late_copies(late_srcs, outs[self.n_chip:], sems[3:])
        _start_exchange(*late)
        _finish_exchange(*self._copies(ins, outs[:self.n_chip], sems[:3]))
        _finish_exchange(*late)


def _pcall(body, name, grid, in_specs, out_specs, out_shape, args, scratch=(), comm=None):
    params = pltpu.CompilerParams(dimension_semantics=("arbitrary",) * len(grid), vmem_limit_bytes=VMEM_LIMIT)
    in_specs, out_specs, out_shape, scratch = list(in_specs), list(out_specs), list(out_shape), list(scratch)
    if comm is None:
        res = pl.pallas_call(body, name=name, grid=grid, in_specs=in_specs, out_specs=out_specs, out_shape=out_shape,
                             scratch_shapes=scratch, compiler_params=params)(*args)
        return list(res), []
    n_in, n_out, n_scr = len(in_specs), len(out_specs), len(scratch)
    ci, co = len(comm.ins), len(comm.out_shape)
    total = math.prod(grid)

    def carried(*refs):
        bounds = [0, n_in, n_in + ci, n_in + ci + n_out, n_in + ci + n_out + co, n_in + ci + n_out + co + n_scr]
        ins, cins, outs, couts, scr = (refs[a:b] for a, b in zip(bounds[:-1], bounds[1:]))
        sems = refs[bounds[-1]:]
        step = pl.program_id(0)
        for d in range(1, len(grid)):
            step = step * grid[d] + pl.program_id(d)

        start_step = min(1, total - 1) if getattr(comm, "defer_start", True) else 0

        @pl.when(step == 0)
        def _():
            _barrier_signal(comm.peers)

        @pl.when(step == start_step)
        def _():
            _barrier_wait(comm.peers)
            comm.start(cins, couts, sems)

        forward_at = getattr(comm, "forward_at", None)
        if forward_at is not None and int(forward_at * total) <= start_step:
            forward_at = comm.forward_at = comm.middle_at = None
        if forward_at is not None:
            @pl.when(step == int(forward_at * total))
            def _():
                comm.forward(cins, couts, sems)

        middle_at = getattr(comm, "middle_at", None)
        if middle_at is not None:
            assert forward_at is None or forward_at <= middle_at
            @pl.when(step == int(middle_at * total))
            def _():
                comm.middle(cins, couts, sems)

        body(*ins, *outs, *scr)

        @pl.when(step == total - 1)
        def _():
            late_from = getattr(comm, "late_from", None)
            if late_from is None:
                comm.finish(cins, couts, sems)
            else:
                comm.finish(cins, couts, sems, [outs[k] for k in late_from])

    params = pltpu.CompilerParams(dimension_semantics=("arbitrary",) * len(grid), vmem_limit_bytes=VMEM_LIMIT,
                                  collective_id=BARRIER_ID[comm.peers])
    res = pl.pallas_call(
        carried, name=name, grid=grid, in_specs=in_specs + [ANY] * ci, out_specs=out_specs + [ANY] * co,
        out_shape=out_shape + comm.out_shape, scratch_shapes=scratch + comm.sems, compiler_params=params,
    )(*args, *comm.ins)
    return list(res[:n_out]), list(res[n_out:])


def _exchange_only(comm, name):
    def body(*refs):
        ci, co = len(comm.ins), len(comm.out_shape)
        _barrier_signal(comm.peers)
        _barrier_wait(comm.peers)
        comm.start(refs[:ci], refs[ci:ci + co], refs[ci + co:])
        comm.finish(refs[:ci], refs[ci:ci + co], refs[ci + co:])

    params = pltpu.CompilerParams(collective_id=BARRIER_ID[comm.peers])
    return pl.pallas_call(body, name=name, out_shape=comm.out_shape, in_specs=[ANY] * len(comm.ins),
                          out_specs=[ANY] * len(comm.out_shape), scratch_shapes=comm.sems,
                          compiler_params=params)(*comm.ins)


def _norm_inproj(x, g, win_t, b_in, comm):
    s = x.shape[0]
    tm = _row_tile(s, 512)
    widths = (QKV_W, CBX_W, GATE_W)

    def body(x_ref, g_ref, w_ref, b_ref, xn_ref, qkv_ref, cbx_ref, gate_ref):
        xv = x_ref[...]
        r = lax.rsqrt(jnp.mean(xv * xv, axis=-1, keepdims=True) + NORM_EPS)
        xn = (xv * r * g_ref[...]).astype(BF16)
        xn_ref[...] = xn
        off = 0
        for o_ref, w in zip((qkv_ref, cbx_ref, gate_ref), widths):
            acc = lax.dot_general(xn, w_ref[off:off + w, :], NT, preferred_element_type=F32)
            o_ref[...] = (acc + b_ref[:, off:off + w]).astype(BF16)
            off += w

    return _pcall(
        body, "norm_inproj", (s // tm,),
        [_rows(tm, D_MODEL), _full((1, D_MODEL)), _resident((IN_W, D_MODEL)), _full((1, IN_W))],
        [_rows(tm, D_MODEL)] + [_rows(tm, w) for w in widths],
        [_sds((s, D_MODEL), BF16)] + [_sds((s, w), BF16) for w in widths],
        (x, g, win_t, b_in), comm=comm)


Q_BLOCKS = 4


def _attn_specs():
    tq = Q_BLOCKS * BLOCK
    prev = lambda n: jnp.maximum(Q_BLOCKS * n - 1, 0)
    return [pl.BlockSpec((tq, ATTN_W), lambda n: (n, 0)),
            pl.BlockSpec((BLOCK, KV_W), lambda n: (prev(n), ATTN_W // KV_W)),
            pl.BlockSpec((tq, KV_W), lambda n: (n, ATTN_W // KV_W)),
            pl.BlockSpec((BLOCK, KV_W), lambda n: (prev(n), ATTN_W // KV_W + 1)),
            pl.BlockSpec((tq, KV_W), lambda n: (n, ATTN_W // KV_W + 1))]


def _window(prev_ref, cur_ref, sub):
    if sub == 0:
        return jnp.concatenate([prev_ref[...], cur_ref[0:BLOCK, :]], axis=0)
    return cur_ref[(sub - 1) * BLOCK:(sub + 1) * BLOCK, :]


def _lower_lanes():
    return lax.broadcasted_iota(jnp.int32, (BLOCK, 128), 1) < HEAD_DIM


def _stack_heads(val, kh):
    lower = _lower_lanes()
    parts = []
    for g in range(4):
        h = kh * 4 + g
        blk = val[:, (h // 2) * 128:(h // 2 + 1) * 128]
        keep = lower if h % 2 == 0 else jnp.logical_not(lower)
        parts.append(jnp.where(keep, blk, jnp.zeros_like(blk)))
    return jnp.concatenate(parts, axis=0)


def _dup_kv(window, kh):
    t = window.astype(F32)
    rolled = pltpu.roll(t, HEAD_DIM, axis=1)
    lower = lax.broadcasted_iota(jnp.int32, t.shape, 1) < HEAD_DIM
    dup = jnp.where(lower, t, rolled) if kh == 0 else jnp.where(lower, rolled, t)
    return dup.astype(BF16)


def _attn_mask(real_prev):
    row = lax.broadcasted_iota(jnp.int32, (4 * BLOCK, 2 * BLOCK), 0)
    kj = lax.broadcasted_iota(jnp.int32, (4 * BLOCK, 2 * BLOCK), 1)
    dist = (row & (BLOCK - 1)) + BLOCK - kj
    band = jnp.logical_and(dist >= 0, dist < BLOCK)
    return jnp.logical_and(band, jnp.logical_or(kj >= BLOCK, real_prev))


def _sink_col(sinks_ref, kh):
    gi = lax.broadcasted_iota(jnp.int32, (4 * BLOCK, 1), 0) // BLOCK
    col = jnp.zeros((4 * BLOCK, 1), F32)
    for g in range(4):
        col = jnp.where(gi == g, sinks_ref[0, kh * 4 + g], col)
    return col


def _attn_fwd(qkv, sinks, comm):
    s = qkv.shape[0]
    tq = Q_BLOCKS * BLOCK

    def body(sinks_ref, q_ref, kp_ref, kc_ref, vp_ref, vc_ref, o_ref, lse_ref):
        n = pl.program_id(0)
        lower = _lower_lanes()
        lane = lax.broadcasted_iota(jnp.int32, (BLOCK, 128), 1)
        for sub in range(Q_BLOCKS):
            rows = slice(sub * BLOCK, (sub + 1) * BLOCK)
            mask = _attn_mask(n > 0 if sub == 0 else True)
            kw, vw = _window(kp_ref, kc_ref, sub), _window(vp_ref, vc_ref, sub)
            qv = q_ref[rows, :]
            lse_out = jnp.zeros((BLOCK, 128), F32)
            for kh in range(2):
                qs = _stack_heads(qv, kh)
                kd, vd = _dup_kv(kw, kh), _dup_kv(vw, kh)
                sc = lax.dot_general(qs, kd, NT, preferred_element_type=F32) * ATTN_SCALE
                sc = jnp.where(mask, sc, NEG)
                sink = _sink_col(sinks_ref, kh)
                m = jnp.maximum(jnp.max(sc, axis=1, keepdims=True), sink)
                p = jnp.exp(sc - m)
                l = jnp.sum(p, axis=1, keepdims=True) + jnp.exp(sink - m)
                o = jnp.dot(p.astype(BF16), vd, preferred_element_type=F32) / l
                lse = m + jnp.log(l)
                for pair in range(2):
                    lo = o[(2 * pair) * BLOCK:(2 * pair + 1) * BLOCK]
                    hi = o[(2 * pair + 1) * BLOCK:(2 * pair + 2) * BLOCK]
                    col = (kh * 2 + pair) * 128
                    o_ref[rows, col:col + 128] = jnp.where(lower, lo, hi).astype(BF16)
                for g in range(4):
                    lse_out = jnp.where(lane == kh * 4 + g, lse[g * BLOCK:(g + 1) * BLOCK], lse_out)
            lse_ref[rows, :] = lse_out

    return _pcall(
        body, "attn_fwd", (s // tq,),
        [pl.BlockSpec(memory_space=pltpu.SMEM)] + _attn_specs(),
        [pl.BlockSpec((tq, ATTN_W), lambda n: (n, 0)), pl.BlockSpec((tq, 128), lambda n: (n, 0))],
        [_sds((s, ATTN_W), BF16), _sds((s, 128), F32)],
        (sinks, qkv, qkv, qkv, qkv, qkv), comm=comm)


def _conv_u(cbx_ref, halo_ref, w_ref, first):
    cb = cbx_ref[:, 0:CONV_W].astype(F32)
    cc = cbx_ref[:, CONV_W:2 * CONV_W].astype(F32)
    cx = cbx_ref[:, 2 * CONV_W:3 * CONV_W].astype(F32)
    u = cc * cx
    uh = halo_ref[:, CONV_W:2 * CONV_W].astype(F32) * halo_ref[:, 2 * CONV_W:3 * CONV_W].astype(F32)
    uh = jnp.where(first, 0.0, uh)
    u1, u2 = _shifts_down(u, uh, (1, 2))
    cv = w_ref[0:1, :] * u2 + w_ref[1:2, :] * u1 + w_ref[2:3, :] * u
    return cb, cc, cx, u, cv


def _mix_fwd(x, cbx, gates, attn, conv_w, wa, wc, wout, comm):
    s = x.shape[0]
    tm = _row_tile(s)

    def body(x_ref, cbx_ref, halo_ref, gate_ref, attn_ref, cw_ref, wa_ref, wc_ref, wo_ref,
             h1_ref):
        first = pl.program_id(0) == 0
        cb, _, _, _, cv = _conv_u(cbx_ref, halo_ref, cw_ref, first)
        conv = (cb * cv).astype(BF16)
        ap = jnp.dot(attn_ref[...], wa_ref[...], preferred_element_type=F32)
        cp = jnp.dot(conv, wc_ref[...], preferred_element_type=F32)
        ga = gate_ref[:, 0:D_MODEL].astype(F32)
        gc = gate_ref[:, D_MODEL:2 * D_MODEL].astype(F32)
        merged = (_sig(ga) * ap + _sig(gc) * cp).astype(BF16)
        h1_ref[...] = x_ref[...] + jnp.dot(merged, wo_ref[...], preferred_element_type=F32)

    return _pcall(
        body, "mix_fwd", (s // tm,),
        [_rows(tm, D_MODEL), _rows(tm, CBX_W), pl.BlockSpec((HALO, CBX_W), _prev_halo_map(tm)),
         _rows(tm, GATE_W), _rows(tm, ATTN_W), _full((3, CONV_W)), _full((ATTN_W, D_MODEL)),
         _full((CONV_W, D_MODEL)), _full((D_MODEL, D_MODEL))],
        [_rows(tm, D_MODEL)], [_sds((s, D_MODEL), F32)],
        (x, cbx, cbx, gates, attn, conv_w, wa, wc, wout), comm=comm)


def _col_offsets(wup_parts):
    widths = [p.shape[1] for p in wup_parts]
    assert sum(widths) == D_MODEL
    return [(sum(widths[:k]), w) for k, w in enumerate(widths)]


def _ffn_up(h1, g, wup_parts, fcw, comm):
    s = h1.shape[0]
    tm = _row_tile(s)
    cols = _col_offsets(wup_parts)

    def body(h_ref, g_ref, *refs):
        w_refs = refs[:len(cols)]
        fcw_ref, hn_ref, pre_ref, up_ref, carry_ref = refs[len(cols):]

        @pl.when(pl.program_id(0) == 0)
        def _():
            carry_ref[...] = jnp.zeros_like(carry_ref)

        hv = h_ref[...]
        r = lax.rsqrt(jnp.mean(hv * hv, axis=-1, keepdims=True) + NORM_EPS)
        hn = (hv * r * g_ref[...]).astype(BF16)
        hn_ref[...] = hn
        for c in range(2 * D_FF // FF_CHUNK):
            sl = slice(c * FF_CHUNK, (c + 1) * FF_CHUNK)
            acc = None
            for w_ref, (off, w) in zip(w_refs, cols):
                part = lax.dot_general(hn[:, off:off + w], w_ref[sl, :], NT, preferred_element_type=F32)
                acc = part if acc is None else acc + part
            pre_ref[:, sl] = acc.astype(BF16)
            halo = carry_ref[:, sl]
            carry_ref[:, sl] = acc[tm - HALO:, :]
            u1, u2 = _shifts_down(acc, halo, (1, 2))
            w = fcw_ref[:, sl]
            up_ref[:, sl] = (w[0:1] * u2 + w[1:2] * u1 + w[2:3] * acc).astype(BF16)

    return _pcall(
        body, "ffn_up", (s // tm,),
        [_rows(tm, D_MODEL), _full((1, D_MODEL))] + [_resident((2 * D_FF, w)) for _, w in cols]
        + [_full((3, 2 * D_FF))],
        [_rows(tm, D_MODEL), _rows(tm, 2 * D_FF), _rows(tm, 2 * D_FF)],
        [_sds((s, D_MODEL), BF16), _sds((s, 2 * D_FF), BF16), _sds((s, 2 * D_FF), BF16)],
        (h1, g, *wup_parts, fcw), scratch=[pltpu.VMEM((HALO, 2 * D_FF), F32)], comm=comm)


def _ffn_down_loss(up, wdown, h1, fnorm, target):
    s = h1.shape[0]
    tm = _row_tile(s)

    def body(up_ref, wd_ref, h1_ref, fn_ref, t_ref, act_ref, dh2_ref, loss_ref, dfn_ref):
        i = pl.program_id(0)

        @pl.when(i == 0)
        def _():
            loss_ref[...] = jnp.zeros_like(loss_ref)
            dfn_ref[...] = jnp.zeros_like(dfn_ref)

        h2 = h1_ref[...]
        for c in range(D_FF // FF_CHUNK):
            gsl = slice(c * FF_CHUNK, (c + 1) * FF_CHUNK)
            vsl = slice(D_FF + c * FF_CHUNK, D_FF + (c + 1) * FF_CHUNK)
            gate = up_ref[:, gsl].astype(F32)
            val = up_ref[:, vsl].astype(F32)
            act = (gate * _sig(gate) * val).astype(BF16)
            act_ref[:, gsl] = act
            h2 = h2 + jnp.dot(act, wd_ref[gsl, :], preferred_element_type=F32)
        r = lax.rsqrt(jnp.mean(h2 * h2, axis=-1, keepdims=True) + NORM_EPS)
        yhat = h2 * r
        fn = fn_ref[...]
        diff = yhat * fn - t_ref[...]
        loss_ref[...] += 0.5 * jnp.sum(jnp.sum(diff * diff, axis=1, keepdims=True), axis=0, keepdims=True) / D_MODEL
        dy = diff * (1.0 / D_MODEL)
        dfn_ref[...] += jnp.sum(dy * yhat, axis=0, keepdims=True)
        dyh = dy * fn
        dh2_ref[...] = r * (dyh - yhat * jnp.mean(dyh * yhat, axis=-1, keepdims=True))

    return _pcall(
        body, "ffn_down_loss", (s // tm,),
        [_rows(tm, 2 * D_FF), _resident((D_FF, D_MODEL)), _rows(tm, D_MODEL), _full((1, D_MODEL)),
         _rows(tm, D_MODEL)],
        [_rows(tm, D_FF), _rows(tm, D_MODEL), _full((1, 128)), _full((1, D_MODEL))],
        [_sds((s, D_FF), BF16), _sds((s, D_MODEL), F32), _sds((1, 128), F32), _sds((1, D_MODEL), F32)],
        (up, wdown, h1, fnorm, target))[0]


def _ffn_bwd(dh2, wdown, up, up_pre, fcw, wup_parts, h1, g, comm):
    s = dh2.shape[0]
    tm = _row_tile(s)
    cols = _col_offsets(wup_parts)

    chunk = FF_GRAD_ROWS

    def dup_cols(dh, up_ref, wd_ref, c):
        gsl = slice(c * chunk, (c + 1) * chunk)
        vsl = slice(D_FF + c * chunk, D_FF + (c + 1) * chunk)
        dact = lax.dot_general(dh, wd_ref[gsl, :], NT, preferred_element_type=F32)
        gate = up_ref[:, gsl].astype(F32)
        val = up_ref[:, vsl].astype(F32)
        sg = _sig(gate)
        return dact * val * (sg * (1.0 + gate * (1.0 - sg))), dact * gate * sg

    def body(dh_ref, wd_ref, up_ref, x_ref, w_ref, *refs):
        wup_refs = refs[:len(cols)]
        h_ref, g_ref, dx_ref, dw_ref, dh1_ref, dg_ref, carry_ref = refs[len(cols):]

        @pl.when(pl.program_id(0) == 0)
        def _():
            dw_ref[...] = jnp.zeros_like(dw_ref)
            dg_ref[...] = jnp.zeros_like(dg_ref)
            carry_ref[...] = jnp.zeros_like(carry_ref)

        dh2v = dh_ref[...]
        dh = dh2v.astype(BF16)
        dhn = [jnp.zeros((tm, w), F32) for _, w in cols]
        for c in range(D_FF // chunk):
            for d, off in zip(dup_cols(dh, up_ref, wd_ref, c), (c * chunk, D_FF + c * chunk)):
                sl = slice(off, off + chunk)
                dn = carry_ref[:, sl]
                carry_ref[:, sl] = d[0:HALO, :]
                xv = x_ref[:, sl].astype(F32)
                wv = w_ref[:, sl]
                d1, d2 = _shifts_up(d, dn, (1, 2))
                dx = (wv[2:3] * d + wv[1:2] * d1 + wv[0:1] * d2).astype(BF16)
                dx_ref[:, sl] = dx
                dhn = [a + jnp.dot(dx, wup_ref[sl, :], preferred_element_type=F32)
                       for a, wup_ref in zip(dhn, wup_refs)]
                dw_ref[0:1, sl] += jnp.sum(d2 * xv, axis=0, keepdims=True)
                dw_ref[1:2, sl] += jnp.sum(d1 * xv, axis=0, keepdims=True)
                dw_ref[2:3, sl] += jnp.sum(d * xv, axis=0, keepdims=True)
        dx1, dg = _norm_bwd_tile(h_ref[...], g_ref[...], jnp.concatenate(dhn, axis=1))
        dg_ref[...] += dg
        dh1_ref[...] = dh2v + dx1

    rows = lambda c: _rows_reversed(tm, c, s // tm)
    return _pcall(
        body, "ffn_bwd", (s // tm,),
        [rows(D_MODEL), _resident((D_FF, D_MODEL)), rows(2 * D_FF), rows(2 * D_FF), _full((3, 2 * D_FF))]
        + [_resident((2 * D_FF, w)) for _, w in cols] + [rows(D_MODEL), _full((1, D_MODEL))],
        [rows(2 * D_FF), _full((3, 2 * D_FF)), rows(D_MODEL), _full((1, D_MODEL))],
        [_sds((s, 2 * D_FF), BF16), _sds((3, 2 * D_FF), F32), _sds((s, D_MODEL), F32), _sds((1, D_MODEL), F32)],
        (dh2, wdown, up, up_pre, fcw, *wup_parts, h1, g),
        scratch=[pltpu.VMEM((HALO, 2 * D_FF), F32)], comm=comm)


def _matmul_tn(a, b, tk, name, ts=1024, comm=None):
    s, ka = a.shape
    n = b.shape[1]
    ts = min(ts, s)
    steps = s // ts

    def body(a_ref, b_ref, o_ref, acc_ref):
        j = pl.program_id(1)

        @pl.when(j == 0)
        def _():
            acc_ref[...] = jnp.zeros_like(acc_ref)

        acc_ref[...] += lax.dot_general(a_ref[...].astype(BF16), b_ref[...].astype(BF16), TN,
                                        preferred_element_type=F32)

        @pl.when(j == steps - 1)
        def _():
            o_ref[...] = acc_ref[...].astype(BF16)

    outs, couts = _pcall(
        body, name, (ka // tk, steps),
        [pl.BlockSpec((ts, tk), lambda i, j: (j, i)), pl.BlockSpec((ts, n), lambda i, j: (j, 0))],
        [pl.BlockSpec((tk, n), lambda i, j: (i, 0))], [_sds((ka, n), BF16)],
        (a, b), scratch=[pltpu.VMEM((tk, n), F32)], comm=comm)
    return outs[0] if comm is None else (outs[0], couts)


def _norm_bwd_tile(xv, g, dy):
    r = lax.rsqrt(jnp.mean(xv * xv, axis=-1, keepdims=True) + NORM_EPS)
    xhat = xv * r
    dg = jnp.sum(dy * xhat, axis=0, keepdims=True)
    dyh = dy * g
    return r * (dyh - xhat * jnp.mean(dyh * xhat, axis=-1, keepdims=True)), dg


def _mix_bwd(dh1, wout, gates, attn, wa, wc, cbx, conv_w, comm):
    s = dh1.shape[0]
    tm = _row_tile(s)
    steps = s // tm

    def body(dh_ref, wo_ref, gate_ref, attn_ref, wa_ref, wc_ref, cbx_ref, halo_ref,
             cw_ref, dg_ref, dattn_ref, dcb_ref, dcc_ref, dcx_ref, dw_ref, gwo_ref, gwa_ref, gwc_ref,
             acc_o, acc_a, acc_c, carry_ref):
        i = pl.program_id(0)

        @pl.when(i == 0)
        def _():
            dw_ref[...] = jnp.zeros_like(dw_ref)
            acc_o[...] = jnp.zeros_like(acc_o)
            acc_a[...] = jnp.zeros_like(acc_a)
            acc_c[...] = jnp.zeros_like(acc_c)
            carry_ref[...] = jnp.zeros_like(carry_ref)

        cb, cc, cx, u, cv = _conv_u(cbx_ref, halo_ref, cw_ref, i == steps - 1)
        attn = attn_ref[...]
        conv = (cb * cv).astype(BF16)
        ap = jnp.dot(attn, wa_ref[...], preferred_element_type=F32)
        cp = jnp.dot(conv, wc_ref[...], preferred_element_type=F32)
        dhb = dh_ref[...].astype(BF16)
        dm = lax.dot_general(dhb, wo_ref[...], NT, preferred_element_type=F32)
        sa = _sig(gate_ref[:, 0:D_MODEL].astype(F32))
        sc = _sig(gate_ref[:, D_MODEL:2 * D_MODEL].astype(F32))
        merged = (sa * ap + sc * cp).astype(BF16)
        da = (dm * sa).astype(BF16)
        dc = (dm * sc).astype(BF16)
        dg_ref[:, 0:D_MODEL] = (dm * ap * sa * (1.0 - sa)).astype(BF16)
        dg_ref[:, D_MODEL:2 * D_MODEL] = (dm * cp * sc * (1.0 - sc)).astype(BF16)
        dattn_ref[...] = lax.dot_general(da, wa_ref[...], NT, preferred_element_type=F32).astype(BF16)
        dconv = lax.dot_general(dc, wc_ref[...], NT, preferred_element_type=F32)
        dcb_ref[...] = (dconv * cv).astype(BF16)
        d = dconv * cb
        dn = carry_ref[...]
        carry_ref[...] = d[0:HALO, :]
        d1, d2 = _shifts_up(d, dn, (1, 2))
        du = cw_ref[2:3, :] * d + cw_ref[1:2, :] * d1 + cw_ref[0:1, :] * d2
        dcc_ref[...] = (du * cx).astype(BF16)
        dcx_ref[...] = (du * cc).astype(BF16)
        dw_ref[0:1, :] += jnp.sum(d2 * u, axis=0, keepdims=True)
        dw_ref[1:2, :] += jnp.sum(d1 * u, axis=0, keepdims=True)
        dw_ref[2:3, :] += jnp.sum(d * u, axis=0, keepdims=True)
        acc_o[...] += lax.dot_general(merged, dhb, TN, preferred_element_type=F32)
        acc_a[...] += lax.dot_general(attn, da, TN, preferred_element_type=F32)
        acc_c[...] += lax.dot_general(conv, dc, TN, preferred_element_type=F32)

        @pl.when(i == steps - 1)
        def _():
            gwo_ref[...] = acc_o[...].astype(BF16)
            gwa_ref[...] = acc_a[...].astype(BF16)
            gwc_ref[...] = acc_c[...].astype(BF16)

    rows = lambda c: _rows_reversed(tm, c, steps)
    return _pcall(
        body, "mix_bwd", (steps,),
        [rows(D_MODEL), _full((D_MODEL, D_MODEL)), rows(GATE_W), rows(ATTN_W), _full((ATTN_W, D_MODEL)),
         _full((CONV_W, D_MODEL)), rows(CBX_W), pl.BlockSpec((HALO, CBX_W), _prev_halo_map_reversed(tm, steps)),
         _full((3, CONV_W))],
        [rows(GATE_W), rows(ATTN_W), rows(CONV_W), rows(CONV_W), rows(CONV_W),
         _full((3, CONV_W)), _full((D_MODEL, D_MODEL)), _full((ATTN_W, D_MODEL)), _full((CONV_W, D_MODEL))],
        [_sds((s, GATE_W), BF16), _sds((s, ATTN_W), BF16), _sds((s, CONV_W), BF16), _sds((s, CONV_W), BF16),
         _sds((s, CONV_W), BF16), _sds((3, CONV_W), F32), _sds((D_MODEL, D_MODEL), BF16),
         _sds((ATTN_W, D_MODEL), BF16), _sds((CONV_W, D_MODEL), BF16)],
        (dh1, wout, gates, attn, wa, wc, cbx, cbx, conv_w),
        scratch=[pltpu.VMEM((D_MODEL, D_MODEL), F32), pltpu.VMEM((ATTN_W, D_MODEL), F32),
                 pltpu.VMEM((CONV_W, D_MODEL), F32), pltpu.VMEM((HALO, CONV_W), F32)], comm=comm)


def _attn_bwd(qkv, sinks, attn, lse, dattn, comm):
    s = qkv.shape[0]
    tq = Q_BLOCKS * BLOCK

    def body(sinks_ref, q_ref, kp_ref, kc_ref, vp_ref, vc_ref, o_ref, lse_ref, do_ref,
             dq_ref, dk_ref, dv_ref, ds_ref):
        n = pl.program_id(0)

        @pl.when(n == 0)
        def _():
            dk_ref[...] = jnp.zeros_like(dk_ref)
            dv_ref[...] = jnp.zeros_like(dv_ref)
            ds_ref[...] = jnp.zeros_like(ds_ref)

        lower = _lower_lanes()
        lane = lax.broadcasted_iota(jnp.int32, (BLOCK, 128), 1)
        lower2 = lax.broadcasted_iota(jnp.int32, (2 * BLOCK, 128), 1) < HEAD_DIM
        lane1 = lax.broadcasted_iota(jnp.int32, (1, 128), 1)
        dsink = jnp.zeros((1, 128), F32)
        for sub in reversed(range(Q_BLOCKS)):
            rows = slice(sub * BLOCK, (sub + 1) * BLOCK)
            mask = _attn_mask(n > 0 if sub == 0 else True)
            kw, vw = _window(kp_ref, kc_ref, sub), _window(vp_ref, vc_ref, sub)
            qv, ov, dov, lsev = q_ref[rows, :], o_ref[rows, :], do_ref[rows, :], lse_ref[rows, :]
            dk_fold, dv_fold = [], []
            for kh in range(2):
                qs = _stack_heads(qv, kh)
                dos = _stack_heads(dov, kh)
                os_ = _stack_heads(ov, kh)
                kd, vd = _dup_kv(kw, kh), _dup_kv(vw, kh)
                lse = jnp.concatenate(
                    [jnp.sum(jnp.where(lane == kh * 4 + g, lsev, 0.0), axis=1, keepdims=True) for g in range(4)],
                    axis=0)
                sc = lax.dot_general(qs, kd, NT, preferred_element_type=F32) * ATTN_SCALE
                p = jnp.exp(jnp.where(mask, sc, NEG) - lse)
                dp = lax.dot_general(dos, vd, NT, preferred_element_type=F32)
                delta = jnp.sum(dos.astype(F32) * os_.astype(F32), axis=1, keepdims=True)
                dsc = (p * (dp - delta) * ATTN_SCALE).astype(BF16)
                dqs = jnp.dot(dsc, kd, preferred_element_type=F32)
                for pair in range(2):
                    lo = dqs[(2 * pair) * BLOCK:(2 * pair + 1) * BLOCK]
                    hi = dqs[(2 * pair + 1) * BLOCK:(2 * pair + 2) * BLOCK]
                    col = (kh * 2 + pair) * 128
                    dq_ref[rows, col:col + 128] = jnp.where(lower, lo, hi).astype(BF16)
                dkd = lax.dot_general(dsc, qs, TN, preferred_element_type=F32)
                dvd = lax.dot_general(p.astype(BF16), dos, TN, preferred_element_type=F32)
                dk_fold.append(dkd + pltpu.roll(dkd, HEAD_DIM, axis=1))
                dv_fold.append(dvd + pltpu.roll(dvd, HEAD_DIM, axis=1))
                psink = jnp.exp(_sink_col(sinks_ref, kh) - lse) * delta
                for g in range(4):
                    tot = jnp.sum(psink[g * BLOCK:(g + 1) * BLOCK], axis=0, keepdims=True)
                    dsink = dsink - jnp.where(lane1 == kh * 4 + g, tot, 0.0)
            dk2 = jnp.where(lower2, dk_fold[0], dk_fold[1])
            dv2 = jnp.where(lower2, dv_fold[0], dv_fold[1])
            cur = pl.ds(pl.multiple_of((Q_BLOCKS * n + sub) * BLOCK, BLOCK), BLOCK)
            dk_ref[cur, :] += dk2[BLOCK:]
            dv_ref[cur, :] += dv2[BLOCK:]
            if sub > 0:
                prev = pl.ds(pl.multiple_of((Q_BLOCKS * n + sub - 1) * BLOCK, BLOCK), BLOCK)
                dk_ref[prev, :] += dk2[:BLOCK]
                dv_ref[prev, :] += dv2[:BLOCK]
        ds_ref[...] += dsink

        @pl.when(n > 0)
        def _():
            prev = pl.ds(pl.multiple_of((Q_BLOCKS * n - 1) * BLOCK, BLOCK), BLOCK)
            dk_ref[prev, :] += dk2[:BLOCK]
            dv_ref[prev, :] += dv2[:BLOCK]

    blk = lambda w: pl.BlockSpec((tq, w), lambda n: (n, 0))
    return _pcall(
        body, "attn_bwd", (s // tq,),
        [pl.BlockSpec(memory_space=pltpu.SMEM)] + _attn_specs() + [blk(ATTN_W), blk(128), blk(ATTN_W)],
        [blk(ATTN_W), _full((s, KV_W)), _full((s, KV_W)), _full((1, 128))],
        [_sds((s, ATTN_W), BF16), _sds((s, KV_W), F32), _sds((s, KV_W), F32), _sds((1, 128), F32)],
        (sinks, qkv, qkv, qkv, qkv, qkv, attn, lse, dattn), comm=comm)


DPROJ_PIECES = (ATTN_W, KV_W, KV_W, CONV_W, CONV_W, CONV_W, GATE_W)
DPROJ_OFFSETS = tuple(sum(DPROJ_PIECES[:k]) for k in range(len(DPROJ_PIECES)))


def _grad_w_in(pieces, xn, comm):
    s = xn.shape[0]
    ts = min(1024, s)
    steps = s // ts
    rows0 = DPROJ_OFFSETS[6]

    def body(*refs):
        p_refs, b_ref, o_ref, acc_ref, stage_ref, sem = refs[:7], refs[7], refs[8], refs[9], refs[10], refs[11]
        i, j = pl.program_id(0), pl.program_id(1)

        @pl.when(j == 0)
        def _():
            acc_ref[...] = jnp.zeros_like(acc_ref)

        bv = b_ref[...]

        def flush(lo, n):
            stage_ref[0:n, :] = acc_ref[0:n, :].astype(BF16)
            cp = pltpu.make_async_copy(stage_ref.at[0:n, :], o_ref.at[lo:lo + n, :], sem)
            cp.start()
            cp.wait()

        @pl.when(i == 0)
        def _():
            for p_ref, off, w in zip(p_refs[:6], DPROJ_OFFSETS[:6], DPROJ_PIECES[:6]):
                acc_ref[off:off + w, :] += lax.dot_general(p_ref[...].astype(BF16), bv, TN,
                                                           preferred_element_type=F32)

            @pl.when(j == steps - 1)
            def _():
                flush(0, rows0)

        @pl.when(i == 1)
        def _():
            acc_ref[0:GATE_W, :] += lax.dot_general(p_refs[6][...], bv, TN, preferred_element_type=F32)

            @pl.when(j == steps - 1)
            def _():
                flush(rows0, GATE_W)

    def piece_spec(w, group):
        return pl.BlockSpec((ts, w), lambda i, j: (jnp.where(i == group, j, 0), 0))

    outs, couts = _pcall(
        body, "grad_w_in", (2, steps),
        [piece_spec(w, 0) for w in DPROJ_PIECES[:6]] + [piece_spec(GATE_W, 1),
                                                         pl.BlockSpec((ts, D_MODEL), lambda i, j: (j, 0))],
        [ANY], [_sds((IN_W, D_MODEL), BF16)], (*pieces, xn),
        scratch=[pltpu.VMEM((rows0, D_MODEL), F32), pltpu.VMEM((rows0, D_MODEL), BF16), pltpu.SemaphoreType.DMA],
        comm=comm)
    return outs[0], couts


def _inproj_bwd(pieces, win_t, x, g, dh1, comm):
    s = x.shape[0]
    tm = _row_tile(s, 512)

    def body(*refs):
        p_refs = refs[:7]
        w_ref, x_ref, g_ref, dh_ref, dx_ref, db_ref, dg_ref = refs[7:]

        @pl.when(pl.program_id(0) == 0)
        def _():
            db_ref[...] = jnp.zeros_like(db_ref)
            dg_ref[...] = jnp.zeros_like(dg_ref)

        dxn = jnp.zeros((tm, D_MODEL), F32)
        for p_ref, off, w in zip(p_refs, DPROJ_OFFSETS, DPROJ_PIECES):
            v = p_ref[...].astype(BF16)
            db_ref[:, off:off + w] += jnp.sum(v.astype(F32), axis=0, keepdims=True)
            dxn = dxn + jnp.dot(v, w_ref[off:off + w, :], preferred_element_type=F32)
        dx, dg = _norm_bwd_tile(x_ref[...], g_ref[...], dxn)
        dg_ref[...] += dg
        dx_ref[...] = dh_ref[...] + dx

    return _pcall(
        body, "inproj_bwd", (s // tm,),
        [_rows(tm, w) for w in DPROJ_PIECES] + [_resident((IN_W, D_MODEL)), _rows(tm, D_MODEL), _full((1, D_MODEL)),
                                                _rows(tm, D_MODEL)],
        [_rows(tm, D_MODEL), _full((8, IN_W)), _full((8, D_MODEL))],
        [_sds((s, D_MODEL), F32), _sds((8, IN_W), F32), _sds((8, D_MODEL), F32)],
        (*pieces, win_t, x, g, dh1), comm=comm)


def _adam_math(w, g, m, v):
    m2 = ADAM_B1 * m + (1.0 - ADAM_B1) * g
    v2 = ADAM_B2 * v + (1.0 - ADAM_B2) * (g * g)
    m_hat = m2 / (1.0 - ADAM_B1 ** ADAM_STEP)
    v_hat = v2 / (1.0 - ADAM_B2 ** ADAM_STEP)
    delta = -ADAM_LR * (m_hat / (jnp.sqrt(v_hat) + ADAM_EPS) + ADAM_WD * w)
    return delta, m2, v2


def _sum_slots(ref):
    tot = ref[0].astype(F32)
    for i in range(1, ref.shape[0]):
        tot = tot + ref[i].astype(F32)
    return tot


def _pair_add(partials, theirs, tr, name):
    r = partials.shape[0] // N_DEV
    c = partials.shape[1]
    nt = r // tr
    core = lax.axis_index("c").astype(jnp.int32).reshape(1)

    def body(core_ref, a_ref, b_ref, o_ref):
        o_ref[...] = (a_ref[...].astype(F32) + b_ref[...].astype(F32)).astype(BF16)

    grid_spec = pltpu.PrefetchScalarGridSpec(
        num_scalar_prefetch=1, grid=(4 * nt,),
        in_specs=[pl.BlockSpec((None, None, tr, c), lambda i, core_ref: (i // nt, core_ref[0], i % nt, 0)),
                  pl.BlockSpec((tr, c), lambda i, core_ref: (i, 0))],
        out_specs=pl.BlockSpec((tr, c), lambda i, core_ref: (i, 0)))
    return pl.pallas_call(body, name=name, grid_spec=grid_spec, out_shape=_sds((4 * r, c), BF16))(
        core, partials.reshape(4, 2, r, c), theirs)


def _sum_adamw(parts, w, m, v, tr, name):
    r, c = w.shape

    def body(p_ref, w_ref, m_ref, v_ref, g_ref, d_ref, m2_ref, v2_ref):
        g = _sum_slots(p_ref)
        g_ref[...] = g
        d_ref[...], m2_ref[...], v2_ref[...] = _adam_math(w_ref[...], g, m_ref[...], v_ref[...])

    spec = pl.BlockSpec((tr, c), lambda i: (i, 0))
    return _pcall(body, name, (r // tr,), [pl.BlockSpec((N_DEV, tr, c), lambda i: (0, i, 0)), spec, spec, spec],
                  [spec] * 4, [_sds((r, c), F32)] * 4, (parts, w, m, v))[0]


def _sum_parts_adamw(parts, w, m, v, tr, name):
    c = w.shape[1]
    tiles = [p.shape[1] // tr for p in parts]
    starts = [sum(tiles[:k]) for k in range(len(parts))]
    n_parts = len(parts)

    def body(*refs):
        p_refs = refs[:n_parts]
        w_ref, m_ref, v_ref, g_ref, d_ref, m2_ref, v2_ref = refs[n_parts:]
        i = pl.program_id(0)
        for p_ref, st, nt in zip(p_refs, starts, tiles):
            @pl.when(jnp.logical_and(i >= st, i < st + nt))
            def _(p_ref=p_ref):
                g_ref[...] = _sum_slots(p_ref)

        d_ref[...], m2_ref[...], v2_ref[...] = _adam_math(w_ref[...], g_ref[...], m_ref[...], v_ref[...])

    def part_spec(p, st, nt):
        return pl.BlockSpec((p.shape[0], tr, c), lambda i: (0, jnp.clip(i - st, 0, nt - 1), 0))

    spec = pl.BlockSpec((tr, c), lambda i: (i, 0))
    return _pcall(
        body, name, (sum(tiles),),
        [part_spec(p, st, nt) for p, st, nt in zip(parts, starts, tiles)] + [spec, spec, spec],
        [spec] * 4, [_sds(w.shape, F32)] * 4, (*parts, w, m, v))[0]


ROW_MIX, ROW_FFN, ROW_FINAL, ROW_SINKS, ROW_LOSS, ROW_BIN, ROW_CW, ROW_FCW = 0, 1, 2, 3, 4, 5, 10, 13
FCW_ROWS = 6


def _wide_pieces(width):
    return [(k * D_MODEL, min(D_MODEL, width - k * D_MODEL)) for k in range(-(-width // D_MODEL))]


def _pack_small(dffn, dfn, dsink, loss, dcw, dfcw):
    def body(ffn_ref, fn_ref, sink_ref, loss_ref, cw_ref, fcw_ref, o_ref):
        o_ref[...] = jnp.zeros_like(o_ref)
        o_ref[ROW_FFN:ROW_FFN + 1, :] = ffn_ref[...]
        o_ref[ROW_FINAL:ROW_FINAL + 1, :] = fn_ref[...]
        o_ref[ROW_SINKS:ROW_SINKS + 1, 0:128] = sink_ref[...]
        o_ref[ROW_LOSS:ROW_LOSS + 1, 0:128] = loss_ref[...]
        o_ref[ROW_CW:ROW_CW + 3, 0:CONV_W] = cw_ref[...]
        for a in range(3):
            for k, (off, w) in enumerate(_wide_pieces(2 * D_FF)):
                row = ROW_FCW + FCW_ROWS * a + k
                o_ref[row:row + 1, 0:w] = fcw_ref[a:a + 1, off:off + w]

    return pl.pallas_call(body, name="pack_small", out_shape=_sds((SMALL_ROWS, D_MODEL), F32))(
        dffn, dfn, dsink, loss, dcw, dfcw)


def _small_sums_adamw(r_small, r_dmix, r_dbin, params):
    rows = (None, None, ROW_SINKS, ROW_FFN, ROW_FINAL)

    def sum_row0(ref):
        tot = ref[0:1, :]
        for i in range(1, N_DEV):
            tot = tot + ref[8 * i:8 * i + 1, :]
        return tot

    def body(*refs):
        r_ref, late_refs, p_refs, o_refs = refs[0], refs[1:3], refs[3:18], refs[18:]
        tot = _sum_slots(r_ref)
        for k, row in enumerate(rows):
            w_ref, m_ref, v_ref = p_refs[3 * k:3 * k + 3]
            g_ref, d_ref, m2_ref, v2_ref = o_refs[4 * k:4 * k + 4]
            if row is None:
                g_ref[...] = sum_row0(late_refs[k])
            else:
                for j, (off, w) in enumerate(_wide_pieces(w_ref.shape[1])):
                    g_ref[:, off:off + w] = tot[row + j:row + j + 1, 0:w]
            d_ref[...], m2_ref[...], v2_ref[...] = _adam_math(w_ref[...], g_ref[...], m_ref[...], v_ref[...])
        cw_ref, fcw_ref, loss_ref = o_refs[20:]
        cw_ref[...] = tot[ROW_CW:ROW_CW + 3, 0:CONV_W]
        for a in range(3):
            for j, (off, w) in enumerate(_wide_pieces(2 * D_FF)):
                row = ROW_FCW + FCW_ROWS * a + j
                fcw_ref[a:a + 1, off:off + w] = tot[row:row + 1, 0:w]
        loss_ref[...] = tot[ROW_LOSS:ROW_LOSS + 1, 0:128]

    flat = [t for p in params for t in p]
    out_shape = [_sds(p[0].shape, F32) for p in params for _ in range(4)]
    out_shape += [_sds((3, CONV_W), F32), _sds((3, 2 * D_FF), F32), _sds((1, 128), F32)]
    res = pl.pallas_call(body, name="small_sums_adamw", out_shape=out_shape)(r_small, r_dmix, r_dbin, *flat)
    return [tuple(res[4 * k:4 * k + 4]) for k in range(5)], res[20], res[21], res[22]


def _adamw_pair(a, b):
    def body(*refs):
        for k in range(2):
            w_ref, g_ref, m_ref, v_ref = refs[4 * k:4 * k + 4]
            d_ref, m2_ref, v2_ref = refs[8 + 3 * k:8 + 3 * k + 3]
            d_ref[...], m2_ref[...], v2_ref[...] = _adam_math(w_ref[...], g_ref[...], m_ref[...], v_ref[...])

    out_shape = [_sds(a[0].shape, F32)] * 3 + [_sds(b[0].shape, F32)] * 3
    res = pl.pallas_call(body, name="adamw_conv_weights", out_shape=out_shape)(*a, *b)
    return tuple(res[:3]), tuple(res[3:])


def _pad_cols(a, c):
    return jnp.pad(a, ((0, 0), (0, c - a.shape[1])))


def _to_col_slabs(g):
    r = g.shape[0]
    return jnp.transpose(g.reshape(r, N_DEV, 128), (1, 0, 2)).reshape(N_DEV * r, 128)


def _from_col_slabs(t):
    r = t.shape[0] // N_DEV
    return jnp.transpose(t.reshape(N_DEV, r, 128), (1, 0, 2)).reshape(r, N_DEV * 128)


def _slots(t):
    return t.reshape(N_DEV, t.shape[0] // N_DEV, t.shape[1])


def kernel(x, mix_norm, w_in, b_in, sinks, conv_w, w_attn_branch, w_conv_branch, w_out, ffn_norm, w_up, ffn_conv_w, w_down, final_norm, loss_target, m_mix_norm, m_w_in, m_b_in, m_sinks, m_conv_w, m_w_attn_branch, m_w_conv_branch, m_w_out, m_ffn_norm, m_w_up, m_ffn_conv_w, m_w_down, m_final_norm, v_mix_norm, v_w_in, v_b_in, v_sinks, v_conv_w, v_w_attn_branch, v_w_conv_branch, v_w_out, v_ffn_norm, v_w_up, v_ffn_conv_w, v_w_down, v_final_norm):
    xs, tgt = x[0], loss_target[0]
    me = 4 * lax.axis_index("x") + 2 * lax.axis_index("y") + lax.axis_index("c")
    in_rows, up_rows = IN_W // N_DEV, 2 * D_FF // N_DEV

    conv_sh = jnp.concatenate([_pad_cols(ffn_conv_w[0], 768), _pad_cols(conv_w[0], 768),
                               jnp.zeros((2, 768), F32)], axis=0)
    win_sh, wup_sh = w_in[0].T.astype(BF16), w_up[0].T.astype(BF16)
    wout_sh, wdown_sh = w_out[0].astype(BF16), w_down[0].astype(BF16)
    wa_sh, wc_sh = w_attn_branch[0].astype(BF16), w_conv_branch[0].astype(BF16)

    quarter, half = D_MODEL // 4, D_MODEL // 2
    phases = dict(forward_at=0.375, pass_on_at=0.875)
    (win_t,) = _exchange_only(_AllGather([win_sh]), "gather_w_in")
    (xn, qkv, cbx, gates), (wa_s, wc_s, conv_g, wup_a) = _norm_inproj(
        xs, mix_norm, win_t, b_in, _AllGather([wa_sh, wc_sh, conv_sh, (wup_sh, 0, quarter)], **phases))
    (attn, lse), (wup_b, wout) = _attn_fwd(qkv, sinks,
                                           _AllGather([(wup_sh, quarter, quarter), wout_sh], **phases))
    wa, wc = _from_col_slabs(wa_s), _from_col_slabs(wc_s)
    conv_g = conv_g.reshape(N_DEV, 8, 768)
    fcw = jnp.transpose(conv_g[:, 0:3, :up_rows], (1, 0, 2)).reshape(3, 2 * D_FF)
    cw = jnp.transpose(conv_g[:, 3:6, :CONV_W // N_DEV], (1, 0, 2)).reshape(3, CONV_W)
    (h1,), (wup_c,) = _mix_fwd(xs, cbx, gates, attn, cw, wa, wc, wout,
                               _AllGather([(wup_sh, half, half)], **phases))
    wup_parts = (wup_a, wup_b, wup_c)
    (hn, up_pre, up), (wdown,) = _ffn_up(h1, ffn_norm, wup_parts, fcw,
                                         _AllGather([wdown_sh], forward_at=0.25, pass_on_at=0.75))
    act, dh2, loss_p, dfn_p = _ffn_down_loss(up, wdown, h1, final_norm.reshape(1, D_MODEL), tgt)

    dn_rows, q_up = D_FF // N_DEV, up_rows // 4
    g_wdown = _matmul_tn(act, dh2, FF_GRAD_ROWS, "grad_w_down")
    (dup_pre, dfcw_p, dh1, dffn_p), (r_wdown,) = _ffn_bwd(dh2, wdown, up, up_pre, fcw, wup_parts, h1, ffn_norm,
                                                         _ReduceScatter([(g_wdown, 0, dn_rows)]))
    g_wup_t = _matmul_tn(dup_pre, hn, FF_GRAD_ROWS, "grad_w_up")
    (dgates, dattn, dcb, dcc, dcx, dcw_p, g_wout, g_wa_nat, g_wc_nat), (r_wup_ab,) = _mix_bwd(
        dh1, wout, gates, attn, wa, wc, cbx, cw, _ReduceScatter([(g_wup_t, 0, 2 * q_up)]))
    g_wa, g_wc = _to_col_slabs(g_wa_nat), _to_col_slabs(g_wc_nat)
    (dq, dk, dv, dsink_p), (r_wup_c, r_wout) = _attn_bwd(
        qkv, sinks, attn, lse, dattn,
        _ReduceScatter([(g_wup_t, 2 * q_up, q_up), (g_wout, 0, D_MODEL // N_DEV)]))
    dproj = (dq, dk, dv, dcb, dcc, dcx, dgates)
    small = _pack_small(dffn_p, dfn_p, dsink_p, loss_p, dcw_p, dfcw_p)
    g_win_t, (r_wup_d, r_wa, r_wc, r_small) = _grad_w_in(
        dproj, xn, _ReduceScatter([(g_wup_t, 3 * q_up, q_up), (g_wa, 0, ATTN_W), (g_wc, 0, CONV_W)], [small]))
    (win_theirs,) = _exchange_only(_PairExchange([g_win_t]), "pair_exchange_w_in")
    q_win = _pair_add(g_win_t, win_theirs, in_rows // 2, "pair_add_w_in")
    (dx, _, _), (r_win, r_dbin, r_dmix) = _inproj_bwd(
        dproj, win_t, xs, mix_norm, dh1,
        _ChipExchangeThenBroadcast([q_win], late_from=(1, 2), late_shapes=[(8, IN_W), (8, D_MODEL)]))

    fn2, m_fn2, v_fn2 = (t.reshape(1, D_MODEL) for t in (final_norm, m_final_norm, v_final_norm))
    small_res, g_cw_full, g_fcw_full, loss_row = _small_sums_adamw(
        _slots(r_small), r_dmix, r_dbin,
        [(mix_norm, m_mix_norm, v_mix_norm), (b_in, m_b_in, v_b_in), (sinks, m_sinks, v_sinks),
         (ffn_norm, m_ffn_norm, v_ffn_norm), (fn2, m_fn2, v_fn2)])
    loss = loss_row[0, 0]
    g_cw = lax.dynamic_slice_in_dim(g_cw_full, me * (CONV_W // N_DEV), CONV_W // N_DEV, axis=1)
    g_fcw = lax.dynamic_slice_in_dim(g_fcw_full, me * up_rows, up_rows, axis=1)
    taps = lambda t: jnp.transpose(t, (1, 0, 2))
    g_cw, g_fcw = g_cw[:, None, :], g_fcw[:, None, :]
    cw_res, fcw_res = _adamw_pair((taps(conv_w), g_cw, taps(m_conv_w), taps(v_conv_w)),
                                  (taps(ffn_conv_w), g_fcw, taps(m_ffn_conv_w), taps(v_ffn_conv_w)))

    big = {}
    big["w_in"] = tuple(t.T for t in _sum_parts_adamw(
        [r_win.reshape(4, in_rows, D_MODEL)], w_in[0].T, m_w_in[0].T, v_w_in[0].T, in_rows // 2, "adamw_w_in"))
    big["w_up"] = tuple(t.T for t in _sum_parts_adamw(
        [_slots(r_wup_ab), _slots(r_wup_c), _slots(r_wup_d)], w_up[0].T, m_w_up[0].T, v_w_up[0].T, q_up,
        "adamw_w_up"))
    big["w_out"] = _sum_adamw(_slots(r_wout), w_out[0], m_w_out[0], v_w_out[0], 128, "adamw_w_out")
    big["w_down"] = _sum_adamw(_slots(r_wdown), w_down[0], m_w_down[0], v_w_down[0], dn_rows // 2, "adamw_w_down")
    big["w_attn_branch"] = _sum_adamw(_slots(r_wa), w_attn_branch[0], m_w_attn_branch[0], v_w_attn_branch[0], 256,
                                      "adamw_w_attn_branch")
    big["w_conv_branch"] = _sum_adamw(_slots(r_wc), w_conv_branch[0], m_w_conv_branch[0], v_w_conv_branch[0], 256,
                                      "adamw_w_conv_branch")

    res = dict(zip(("mix_norm", "b_in", "sinks", "ffn_norm"), small_res[:4]))
    res["final_norm"] = tuple(t.reshape(final_norm.shape) for t in small_res[4])
    res["conv_w"] = tuple(jnp.transpose(t, (1, 0, 2)) for t in (g_cw,) + cw_res)
    res["ffn_conv_w"] = tuple(jnp.transpose(t, (1, 0, 2)) for t in (g_fcw,) + fcw_res)
    for name, ref_w in (("w_in", w_in), ("w_up", w_up), ("w_out", w_out), ("w_down", w_down),
                        ("w_attn_branch", w_attn_branch), ("w_conv_branch", w_conv_branch)):
        res[name] = tuple(t.reshape(ref_w.shape) for t in big[name])

    order = ["mix_norm", "w_in", "b_in", "sinks", "conv_w", "w_attn_branch", "w_conv_branch", "w_out",
             "ffn_norm", "w_up", "ffn_conv_w", "w_down", "final_norm"]
    out = [loss, dx.reshape(x.shape)]
    for k in range(4):
        out += [res[name][k] for name in order]
    return tuple(out)
```

```python
import math

import jax
import jax.numpy as jnp
from jax import lax
from jax.experimental import pallas as pl
from jax.experimental.pallas import tpu as pltpu

F32 = jnp.float32
BF16 = jnp.bfloat16
MESH = pl.DeviceIdType.MESH
N_DEV = 8

D_MODEL = 1024
HEAD_DIM = 64
N_HEADS = 8
BLOCK = 128
ATTN_W = 512
KV_W = 128
CONV_W = 512
QKV_W = ATTN_W + 2 * KV_W
CBX_W = 3 * CONV_W
GATE_W = 2 * D_MODEL
IN_W = QKV_W + CBX_W + GATE_W
D_FF = 2816
FF_CHUNK = 256
FF_GRAD_ROWS = 1408
NORM_EPS = 1e-5
ATTN_SCALE = HEAD_DIM ** -0.5
NEG = -1e30
HALO = 16

ADAM_LR = 0.001
ADAM_B1 = 0.9
ADAM_B2 = 0.999
ADAM_EPS = 1e-08
ADAM_WD = 0.01
ADAM_STEP = 10

VMEM_LIMIT = 56 * 1024 * 1024
SMALL_ROWS = 32

NT = (((1,), (1,)), ((), ()))
TN = (((0,), (0,)), ((), ()))
ANY = pl.BlockSpec(memory_space=pl.ANY)


def _sig(v):
    return 1.0 / (1.0 + jnp.exp(-v))


def _row_tile(s, pref=256):
    return pref if s % pref == 0 else s


def _shifts_down(u, halo, ks):
    ext = jnp.concatenate([halo, u], axis=0)
    return [pltpu.roll(ext, k, axis=0)[HALO:, :] for k in ks]


def _shifts_up(u, halo, ks):
    n = u.shape[0]
    ext = jnp.concatenate([u, halo], axis=0)
    return [pltpu.roll(ext, n + HALO - k, axis=0)[:n, :] for k in ks]


def _rows_reversed(tm, c, steps):
    return pl.BlockSpec((tm, c), lambda i: (steps - 1 - i, 0))


def _prev_halo_map_reversed(tm, steps):
    return lambda i: (jnp.maximum((steps - 1 - i) * (tm // HALO) - 1, 0), 0)


def _prev_halo_map(tm):
    return lambda i: (jnp.maximum(i * (tm // HALO) - 1, 0), 0)


def _full(shape):
    return pl.BlockSpec(shape, lambda *_: (0,) * len(shape))


def _resident(shape):
    return pl.BlockSpec(shape, lambda *_: (0,) * len(shape), pipeline_mode=pl.Buffered(1))


def _rows(tm, c):
    return pl.BlockSpec((tm, c), lambda i: (i, 0))


def _sds(shape, dtype):
    return jax.ShapeDtypeStruct(shape, dtype)


def _my_place():
    x, y, c = lax.axis_index("x"), lax.axis_index("y"), lax.axis_index("c")
    return x, y, c


ALL_PEERS = tuple((j >> 2, (j >> 1) & 1, j & 1) for j in range(1, N_DEV))
SIBLING_PEER = ((0, 0, 1),)
CHIP_PEERS = ((0, 1, 0), (1, 0, 0), (1, 1, 0))
BARRIER_ID = {ALL_PEERS: 0, SIBLING_PEER: 1, CHIP_PEERS: 2}


def _barrier_signal(peers):
    x, y, c = _my_place()
    barrier = pltpu.get_barrier_semaphore()
    for dx, dy, dc in peers:
        pl.semaphore_signal(barrier, inc=1, device_id=(x ^ dx, y ^ dy, c ^ dc), device_id_type=MESH)


def _barrier_wait(peers):
    pl.semaphore_wait(pltpu.get_barrier_semaphore(), len(peers))


def _start_exchange(remote, local):
    for cp in local + remote:
        cp.start()


def _finish_exchange(remote, local):
    for cp in remote:
        cp.wait_recv()
    for cp in remote:
        cp.wait_send()
    for cp in local:
        cp.wait()


class _AllGather:
    peers = ALL_PEERS
    SLOTS = 10

    def __init__(self, shards, pass_on_at=None, forward_at=None):
        self.ins = [s[0] if isinstance(s, tuple) else s for s in shards]
        self.cols = [s[1:] if isinstance(s, tuple) else None for s in shards]
        self.middle_at, self.forward_at = pass_on_at, forward_at
        assert pass_on_at is None or forward_at is not None
        n = len(shards)
        self.out_shape = [_sds((N_DEV * s.shape[0], s.shape[1] if c is None else c[1]), s.dtype)
                          for s, c in zip(self.ins, self.cols)]
        self.sems = [pltpu.SemaphoreType.DMA((self.SLOTS * n,)), pltpu.SemaphoreType.DMA((self.SLOTS * n,)),
                     pltpu.SemaphoreType.DMA((n,))]

    def _plan(self, ins, outs, sems):
        send_sems, recv_sems, local_sems = sems
        x, y, c = _my_place()
        me, sibling = (x, y, c), (x, y, 1 - c)
        x_chip, y_chip, far_chip = (1 - x, y), (x, 1 - y), (1 - x, 1 - y)
        sends, lands, mine = [], [], []
        for k in range(len(ins)):
            r = ins[k].shape[0]
            h = (r // 2) // 16 * 16
            whole, first, second = (0, r), (0, h), (h, r - h)

            def rows(dev, rng, k=k, r=r):
                start = pl.multiple_of((4 * dev[0] + 2 * dev[1] + dev[2]) * r + rng[0], 8)
                return outs[k].at[pl.ds(start, rng[1]), :]

            def own(rng, k=k):
                cols = self.cols[k]
                if cols is None:
                    return ins[k].at[pl.ds(rng[0], rng[1]), :]
                return ins[k].at[pl.ds(rng[0], rng[1]), pl.ds(cols[0], cols[1])]

            def copy(slot, block, rng, to, mine_src=False, k=k, rows=rows, own=own):
                if rng[1] == 0:
                    return None
                return pltpu.make_async_remote_copy(
                    src_ref=own(rng) if mine_src else rows(block, rng), dst_ref=rows(block, rng),
                    send_sem=send_sems.at[self.SLOTS * k + slot], recv_sem=recv_sems.at[self.SLOTS * k + slot],
                    device_id=to, device_id_type=MESH)

            sends.append([
                copy(0, me, whole, sibling, True),
                copy(1, me, first, (*x_chip, c), True),
                copy(2, me, second, (*x_chip, c), True),
                copy(3, me, second, (*y_chip, c), True),
                copy(4, me, first, (*y_chip, c), True),
                copy(5, (*x_chip, c), first, (*y_chip, c)),
                copy(6, (*y_chip, c), second, (*x_chip, c)),
                copy(7, (*x_chip, c), whole, sibling),
                copy(8, (*y_chip, c), whole, sibling),
                copy(9, (*far_chip, c), whole, sibling)])
            lands.append([
                copy(0, sibling, whole, me),
                copy(1, (*x_chip, c), first, me), copy(2, (*x_chip, c), second, me),
                copy(3, (*y_chip, c), second, me), copy(4, (*y_chip, c), first, me),
                copy(5, (*far_chip, c), first, me), copy(6, (*far_chip, c), second, me),
                copy(7, (*x_chip, 1 - c), whole, me), copy(8, (*y_chip, 1 - c), whole, me),
                copy(9, (*far_chip, 1 - c), whole, me)])
            mine.append(pltpu.make_async_copy(own(whole), rows(me, whole), local_sems.at[k]))
        return sends, lands, mine

    @staticmethod
    def _then(lands, waits, sends, starts):
        for slot in waits:
            if lands[slot] is not None:
                lands[slot].wait_recv()
        for slot in starts:
            if sends[slot] is not None:
                sends[slot].start()

    def start(self, ins, outs, sems):
        sends, lands, mine = self._plan(ins, outs, sems)
        for cp in mine:
            cp.start()
        for slot in (1, 3, 0, 2, 4):
            for s in sends:
                self._then(None, (), s, (slot,))

    def forward(self, ins, outs, sems):
        sends, lands, _ = self._plan(ins, outs, sems)
        for s, l in zip(sends, lands):
            self._then(l, (1,), s, (5,))
            self._then(l, (3,), s, (6,))

    def middle(self, ins, outs, sems):
        sends, lands, _ = self._plan(ins, outs, sems)
        for s, l in zip(sends, lands):
            self._then(l, (2,), s, (7,))
            self._then(l, (4,), s, (8,))
        for s, l in zip(sends, lands):
            self._then(l, (5, 6), s, (9,))

    def finish(self, ins, outs, sems):
        if self.forward_at is None:
            self.forward(ins, outs, sems)
        if self.middle_at is None:
            self.middle(ins, outs, sems)
        sends, lands, mine = self._plan(ins, outs, sems)
        for s, l in zip(sends, lands):
            self._then(l, (0, 7, 8, 9), s, ())
        for s in sends:
            for cp in s:
                if cp is not None:
                    cp.wait_send()
        for cp in mine:
            cp.wait()


class _ReduceScatter:
    peers = ALL_PEERS

    def __init__(self, parts, bcast=()):
        self.parts = [(lo, cnt) for _, lo, cnt in parts]
        self.n_parts = len(parts)
        self.ins = [a for a, _, _ in parts] + list(bcast)
        self.out_shape = [_sds((N_DEV * cnt, a.shape[1]), a.dtype) for a, _, cnt in parts]
        self.out_shape += [_sds((N_DEV * b.shape[0], b.shape[1]), b.dtype) for b in bcast]
        n = len(self.ins)
        self.sems = [pltpu.SemaphoreType.DMA((7 * n,)), pltpu.SemaphoreType.DMA((7 * n,)),
                     pltpu.SemaphoreType.DMA((n,))]

    def _copies(self, ins, outs, sems):
        send_sems, recv_sems, local_sems = sems
        x, y, c = _my_place()
        me_idx = 4 * x + 2 * y + c
        remote, local = [], []
        for k in range(len(ins)):
            cnt = outs[k].shape[0] // N_DEV
            dst = outs[k].at[pl.ds(pl.multiple_of(me_idx * cnt, 8), cnt), :]
            if k < self.n_parts:
                lo, _ = self.parts[k]
                r = ins[k].shape[0] // N_DEV
                src_of = lambda idx: ins[k].at[pl.ds(pl.multiple_of(idx * r + lo, 8), cnt), :]
            else:
                src_of = lambda idx: ins[k]
            local.append(pltpu.make_async_copy(src_of(me_idx), dst, local_sems.at[k]))
            for j in range(1, N_DEV):
                peer = (x ^ (j >> 2), y ^ ((j >> 1) & 1), c ^ (j & 1))
                peer_idx = 4 * peer[0] + 2 * peer[1] + peer[2]
                remote.append(pltpu.make_async_remote_copy(
                    src_ref=src_of(peer_idx), dst_ref=dst,
                    send_sem=send_sems.at[7 * k + j - 1], recv_sem=recv_sems.at[7 * k + j - 1],
                    device_id=peer, device_id_type=MESH))
        return remote, local

    def start(self, ins, outs, sems):
        _start_exchange(*self._copies(ins, outs, sems))

    def finish(self, ins, outs, sems):
        _finish_exchange(*self._copies(ins, outs, sems))


class _PairExchange:
    peers = SIBLING_PEER

    def __init__(self, arrays):
        self.ins = list(arrays)
        n = len(arrays)
        self.out_shape = [_sds((a.shape[0] // 2, a.shape[1]), a.dtype) for a in arrays]
        self.sems = [pltpu.SemaphoreType.DMA((4 * n,)), pltpu.SemaphoreType.DMA((4 * n,))]

    def _copies(self, ins, outs, sems):
        send_sems, recv_sems = sems
        x, y, c = _my_place()
        remote = []
        for k in range(len(ins)):
            r = ins[k].shape[0] // N_DEV
            for chip in range(4):
                sib = ins[k].at[pl.ds(pl.multiple_of((2 * chip + 1 - c) * r, 8), r), :]
                remote.append(pltpu.make_async_remote_copy(
                    src_ref=sib, dst_ref=outs[k].at[pl.ds(chip * r, r), :],
                    send_sem=send_sems.at[4 * k + chip], recv_sem=recv_sems.at[4 * k + chip],
                    device_id=(x, y, 1 - c), device_id_type=MESH))
        return remote

    def start(self, ins, outs, sems):
        for cp in self._copies(ins, outs, sems):
            cp.start()

    def finish(self, ins, outs, sems):
        remote = self._copies(ins, outs, sems)
        for cp in remote:
            cp.wait_recv()
        for cp in remote:
            cp.wait_send()


class _ChipExchange:
    peers = CHIP_PEERS

    def __init__(self, arrays):
        self.ins = list(arrays)
        self.out_shape = [_sds(a.shape, a.dtype) for a in arrays]
        n = len(self.ins)
        self.sems = [pltpu.SemaphoreType.DMA((3 * n,)), pltpu.SemaphoreType.DMA((3 * n,)),
                     pltpu.SemaphoreType.DMA((n,))]

    def _copies(self, ins, outs, sems):
        send_sems, recv_sems, local_sems = sems
        x, y, c = _my_place()
        my_chip = 2 * x + y
        remote, local = [], []
        for k in range(len(ins)):
            r = ins[k].shape[0] // 4
            dst = outs[k].at[pl.ds(pl.multiple_of(my_chip * r, 8), r), :]
            local.append(pltpu.make_async_copy(ins[k].at[pl.ds(pl.multiple_of(my_chip * r, 8), r), :], dst,
                                               local_sems.at[k]))
            for j in range(1, 4):
                px, py = x ^ (j >> 1), y ^ (j & 1)
                src = ins[k].at[pl.ds(pl.multiple_of((2 * px + py) * r, 8), r), :]
                remote.append(pltpu.make_async_remote_copy(
                    src_ref=src, dst_ref=dst, send_sem=send_sems.at[3 * k + j - 1],
                    recv_sem=recv_sems.at[3 * k + j - 1], device_id=(px, py, c), device_id_type=MESH))
        return remote, local

    def start(self, ins, outs, sems):
        _start_exchange(*self._copies(ins, outs, sems))

    def finish(self, ins, outs, sems):
        _finish_exchange(*self._copies(ins, outs, sems))


class _ChipExchangeThenBroadcast(_ChipExchange):
    peers = ALL_PEERS
    defer_start = False

    def __init__(self, arrays, late_from, late_shapes):
        super().__init__(arrays)
        self.n_chip = len(arrays)
        self.late_from = tuple(late_from)
        self.out_shape += [_sds((N_DEV * r, c), F32) for r, c in late_shapes]
        m = len(late_shapes)
        self.sems += [pltpu.SemaphoreType.DMA((7 * m,)), pltpu.SemaphoreType.DMA((7 * m,)),
                      pltpu.SemaphoreType.DMA((m,))]

    def _late_copies(self, srcs, outs, sems):
        send_sems, recv_sems, local_sems = sems
        x, y, c = _my_place()
        me_idx = 4 * x + 2 * y + c
        remote, local = [], []
        for k, src in enumerate(srcs):
            r = src.shape[0]
            dst = outs[k].at[pl.ds(pl.multiple_of(me_idx * r, 8), r), :]
            local.append(pltpu.make_async_copy(src, dst, local_sems.at[k]))
            for j, (dx, dy, dc) in enumerate(ALL_PEERS):
                remote.append(pltpu.make_async_remote_copy(
                    src_ref=src, dst_ref=dst, send_sem=send_sems.at[7 * k + j], recv_sem=recv_sems.at[7 * k + j],
                    device_id=(x ^ dx, y ^ dy, c ^ dc), device_id_type=MESH))
        return remote, local

    def start(self, ins, outs, sems):
        _start_exchange(*self._copies(ins, outs[:self.n_chip], sems[:3]))

    def finish(self, ins, outs, sems, late_srcs):
        late = self._late_copies(late_srcs, outs[self.n_chip:], sems[3:])
        _start_exchange(*late)
        _finish_exchange(*self._copies(ins, outs[:self.n_chip], sems[:3]))
        _finish_exchange(*late)


def _pcall(body, name, grid, in_specs, out_specs, out_shape, args, scratch=(), comm=None):
    params = pltpu.CompilerParams(dimension_semantics=("arbitrary",) * len(grid), vmem_limit_bytes=VMEM_LIMIT)
    in_specs, out_specs, out_shape, scratch = list(in_specs), list(out_specs), list(out_shape), list(scratch)
    if comm is None:
        res = pl.pallas_call(body, name=name, grid=grid, in_specs=in_specs, out_specs=out_specs, out_shape=out_shape,
                             scratch_shapes=scratch, compiler_params=params)(*args)
        return list(res), []
    n_in, n_out, n_scr = len(in_specs), len(out_specs), len(scratch)
    ci, co = len(comm.ins), len(comm.out_shape)
    total = math.prod(grid)

    def carried(*refs):
        bounds = [0, n_in, n_in + ci, n_in + ci + n_out, n_in + ci + n_out + co, n_in + ci + n_out + co + n_scr]
        ins, cins, outs, couts, scr = (refs[a:b] for a, b in zip(bounds[:-1], bounds[1:]))
        sems = refs[bounds[-1]:]
        step = pl.program_id(0)
        for d in range(1, len(grid)):
            step = step * grid[d] + pl.program_id(d)

        start_step = min(1, total - 1) if getattr(comm, "defer_start", True) else 0

        @pl.when(step == 0)
        def _():
            _barrier_signal(comm.peers)

        @pl.when(step == start_step)
        def _():
            _barrier_wait(comm.peers)
            comm.start(cins, couts, sems)

        forward_at = getattr(comm, "forward_at", None)
        if forward_at is not None and int(forward_at * total) <= start_step:
            forward_at = comm.forward_at = comm.middle_at = None
        if forward_at is not None:
            @pl.when(step == int(forward_at * total))
            def _():
                comm.forward(cins, couts, sems)

        middle_at = getattr(comm, "middle_at", None)
        if middle_at is not None:
            assert forward_at is None or forward_at <= middle_at
            @pl.when(step == int(middle_at * total))
            def _():
                comm.middle(cins, couts, sems)

        body(*ins, *outs, *scr)

        @pl.when(step == total - 1)
        def _():
            late_from = getattr(comm, "late_from", None)
            if late_from is None:
                comm.finish(cins, couts, sems)
            else:
                comm.finish(cins, couts, sems, [outs[k] for k in late_from])

    params = pltpu.CompilerParams(dimension_semantics=("arbitrary",) * len(grid), vmem_limit_bytes=VMEM_LIMIT,
                                  collective_id=BARRIER_ID[comm.peers])
    res = pl.pallas_call(
        carried, name=name, grid=grid, in_specs=in_specs + [ANY] * ci, out_specs=out_specs + [ANY] * co,
        out_shape=out_shape + comm.out_shape, scratch_shapes=scratch + comm.sems, compiler_params=params,
    )(*args, *comm.ins)
    return list(res[:n_out]), list(res[n_out:])


def _exchange_only(comm, name):
    def body(*refs):
        ci, co = len(comm.ins), len(comm.out_shape)
        _barrier_signal(comm.peers)
        _barrier_wait(comm.peers)
        comm.start(refs[:ci], refs[ci:ci + co], refs[ci + co:])
        comm.finish(refs[:ci], refs[ci:ci + co], refs[ci + co:])

    params = pltpu.CompilerParams(collective_id=BARRIER_ID[comm.peers])
    return pl.pallas_call(body, name=name, out_shape=comm.out_shape, in_specs=[ANY] * len(comm.ins),
                          out_specs=[ANY] * len(comm.out_shape), scratch_shapes=comm.sems,
                          compiler_params=params)(*comm.ins)


def _norm_inproj(x, g, win_t, b_in, comm):
    s = x.shape[0]
    tm = _row_tile(s, 512)
    widths = (QKV_W, CBX_W, GATE_W)

    def body(x_ref, g_ref, w_ref, b_ref, xn_ref, qkv_ref, cbx_ref, gate_ref):
        xv = x_ref[...]
        r = lax.rsqrt(jnp.mean(xv * xv, axis=-1, keepdims=True) + NORM_EPS)
        xn = (xv * r * g_ref[...]).astype(BF16)
        xn_ref[...] = xn
        off = 0
        for o_ref, w in zip((qkv_ref, cbx_ref, gate_ref), widths):
            acc = lax.dot_general(xn, w_ref[off:off + w, :], NT, preferred_element_type=F32)
            o_ref[...] = (acc + b_ref[:, off:off + w]).astype(BF16)
            off += w

    return _pcall(
        body, "norm_inproj", (s // tm,),
        [_rows(tm, D_MODEL), _full((1, D_MODEL)), _resident((IN_W, D_MODEL)), _full((1, IN_W))],
        [_rows(tm, D_MODEL)] + [_rows(tm, w) for w in widths],
        [_sds((s, D_MODEL), BF16)] + [_sds((s, w), BF16) for w in widths],
        (x, g, win_t, b_in), comm=comm)


Q_BLOCKS = 4


def _attn_specs():
    tq = Q_BLOCKS * BLOCK
    prev = lambda n: jnp.maximum(Q_BLOCKS * n - 1, 0)
    return [pl.BlockSpec((tq, ATTN_W), lambda n: (n, 0)),
            pl.BlockSpec((BLOCK, KV_W), lambda n: (prev(n), ATTN_W // KV_W)),
            pl.BlockSpec((tq, KV_W), lambda n: (n, ATTN_W // KV_W)),
            pl.BlockSpec((BLOCK, KV_W), lambda n: (prev(n), ATTN_W // KV_W + 1)),
            pl.BlockSpec((tq, KV_W), lambda n: (n, ATTN_W // KV_W + 1))]


def _window(prev_ref, cur_ref, sub):
    if sub == 0:
        return jnp.concatenate([prev_ref[...], cur_ref[0:BLOCK, :]], axis=0)
    return cur_ref[(sub - 1) * BLOCK:(sub + 1) * BLOCK, :]


def _lower_lanes():
    return lax.broadcasted_iota(jnp.int32, (BLOCK, 128), 1) < HEAD_DIM


def _stack_heads(val, kh):
    lower = _lower_lanes()
    parts = []
    for g in range(4):
        h = kh * 4 + g
        blk = val[:, (h // 2) * 128:(h // 2 + 1) * 128]
        keep = lower if h % 2 == 0 else jnp.logical_not(lower)
        parts.append(jnp.where(keep, blk, jnp.zeros_like(blk)))
    return jnp.concatenate(parts, axis=0)


def _dup_kv(window, kh):
    t = window.astype(F32)
    rolled = pltpu.roll(t, HEAD_DIM, axis=1)
    lower = lax.broadcasted_iota(jnp.int32, t.shape, 1) < HEAD_DIM
    dup = jnp.where(lower, t, rolled) if kh == 0 else jnp.where(lower, rolled, t)
    return dup.astype(BF16)


def _attn_mask(real_prev):
    row = lax.broadcasted_iota(jnp.int32, (4 * BLOCK, 2 * BLOCK), 0)
    kj = lax.broadcasted_iota(jnp.int32, (4 * BLOCK, 2 * BLOCK), 1)
    dist = (row & (BLOCK - 1)) + BLOCK - kj
    band = jnp.logical_and(dist >= 0, dist < BLOCK)
    return jnp.logical_and(band, jnp.logical_or(kj >= BLOCK, real_prev))


def _sink_col(sinks_ref, kh):
    gi = lax.broadcasted_iota(jnp.int32, (4 * BLOCK, 1), 0) // BLOCK
    col = jnp.zeros((4 * BLOCK, 1), F32)
    for g in range(4):
        col = jnp.where(gi == g, sinks_ref[0, kh * 4 + g], col)
    return col


def _attn_fwd(qkv, sinks, comm):
    s = qkv.shape[0]
    tq = Q_BLOCKS * BLOCK

    def body(sinks_ref, q_ref, kp_ref, kc_ref, vp_ref, vc_ref, o_ref, lse_ref):
        n = pl.program_id(0)
        lower = _lower_lanes()
        lane = lax.broadcasted_iota(jnp.int32, (BLOCK, 128), 1)
        for sub in range(Q_BLOCKS):
            rows = slice(sub * BLOCK, (sub + 1) * BLOCK)
            mask = _attn_mask(n > 0 if sub == 0 else True)
            kw, vw = _window(kp_ref, kc_ref, sub), _window(vp_ref, vc_ref, sub)
            qv = q_ref[rows, :]
            lse_out = jnp.zeros((BLOCK, 128), F32)
            for kh in range(2):
                qs = _stack_heads(qv, kh)
                kd, vd = _dup_kv(kw, kh), _dup_kv(vw, kh)
                sc = lax.dot_general(qs, kd, NT, preferred_element_type=F32) * ATTN_SCALE
                sc = jnp.where(mask, sc, NEG)
                sink = _sink_col(sinks_ref, kh)
                m = jnp.maximum(jnp.max(sc, axis=1, keepdims=True), sink)
                p = jnp.exp(sc - m)
                l = jnp.sum(p, axis=1, keepdims=True) + jnp.exp(sink - m)
                o = jnp.dot(p.astype(BF16), vd, preferred_element_type=F32) / l
                lse = m + jnp.log(l)
                for pair in range(2):
                    lo = o[(2 * pair) * BLOCK:(2 * pair + 1) * BLOCK]
                    hi = o[(2 * pair + 1) * BLOCK:(2 * pair + 2) * BLOCK]
                    col = (kh * 2 + pair) * 128
                    o_ref[rows, col:col + 128] = jnp.where(lower, lo, hi).astype(BF16)
                for g in range(4):
                    lse_out = jnp.where(lane == kh * 4 + g, lse[g * BLOCK:(g + 1) * BLOCK], lse_out)
            lse_ref[rows, :] = lse_out

    return _pcall(
        body, "attn_fwd", (s // tq,),
        [pl.BlockSpec(memory_space=pltpu.SMEM)] + _attn_specs(),
        [pl.BlockSpec((tq, ATTN_W), lambda n: (n, 0)), pl.BlockSpec((tq, 128), lambda n: (n, 0))],
        [_sds((s, ATTN_W), BF16), _sds((s, 128), F32)],
        (sinks, qkv, qkv, qkv, qkv, qkv), comm=comm)


def _conv_u(cbx_ref, halo_ref, w_ref, first):
    cb = cbx_ref[:, 0:CONV_W].astype(F32)
    cc = cbx_ref[:, CONV_W:2 * CONV_W].astype(F32)
    cx = cbx_ref[:, 2 * CONV_W:3 * CONV_W].astype(F32)
    u = cc * cx
    uh = halo_ref[:, CONV_W:2 * CONV_W].astype(F32) * halo_ref[:, 2 * CONV_W:3 * CONV_W].astype(F32)
    uh = jnp.where(first, 0.0, uh)
    u1, u2 = _shifts_down(u, uh, (1, 2))
    cv = w_ref[0:1, :] * u2 + w_ref[1:2, :] * u1 + w_ref[2:3, :] * u
    return cb, cc, cx, u, cv


def _mix_fwd(x, cbx, gates, attn, conv_w, wa, wc, wout, comm):
    s = x.shape[0]
    tm = _row_tile(s)

    def body(x_ref, cbx_ref, halo_ref, gate_ref, attn_ref, cw_ref, wa_ref, wc_ref, wo_ref,
             h1_ref):
        first = pl.program_id(0) == 0
        cb, _, _, _, cv = _conv_u(cbx_ref, halo_ref, cw_ref, first)
        conv = (cb * cv).astype(BF16)
        ap = jnp.dot(attn_ref[...], wa_ref[...], preferred_element_type=F32)
        cp = jnp.dot(conv, wc_ref[...], preferred_element_type=F32)
        ga = gate_ref[:, 0:D_MODEL].astype(F32)
        gc = gate_ref[:, D_MODEL:2 * D_MODEL].astype(F32)
        merged = (_sig(ga) * ap + _sig(gc) * cp).astype(BF16)
        h1_ref[...] = x_ref[...] + jnp.dot(merged, wo_ref[...], preferred_element_type=F32)

    return _pcall(
        body, "mix_fwd", (s // tm,),
        [_rows(tm, D_MODEL), _rows(tm, CBX_W), pl.BlockSpec((HALO, CBX_W), _prev_halo_map(tm)),
         _rows(tm, GATE_W), _rows(tm, ATTN_W), _full((3, CONV_W)), _full((ATTN_W, D_MODEL)),
         _full((CONV_W, D_MODEL)), _full((D_MODEL, D_MODEL))],
        [_rows(tm, D_MODEL)], [_sds((s, D_MODEL), F32)],
        (x, cbx, cbx, gates, attn, conv_w, wa, wc, wout), comm=comm)


def _col_offsets(wup_parts):
    widths = [p.shape[1] for p in wup_parts]
    assert sum(widths) == D_MODEL
    return [(sum(widths[:k]), w) for k, w in enumerate(widths)]


def _ffn_up(h1, g, wup_parts, fcw, comm):
    s = h1.shape[0]
    tm = _row_tile(s)
    cols = _col_offsets(wup_parts)

    def body(h_ref, g_ref, *refs):
        w_refs = refs[:len(cols)]
        fcw_ref, hn_ref, pre_ref, up_ref, carry_ref = refs[len(cols):]

        @pl.when(pl.program_id(0) == 0)
        def _():
            carry_ref[...] = jnp.zeros_like(carry_ref)

        hv = h_ref[...]
        r = lax.rsqrt(jnp.mean(hv * hv, axis=-1, keepdims=True) + NORM_EPS)
        hn = (hv * r * g_ref[...]).astype(BF16)
        hn_ref[...] = hn
        for c in range(2 * D_FF // FF_CHUNK):
            sl = slice(c * FF_CHUNK, (c + 1) * FF_CHUNK)
            acc = None
            for w_ref, (off, w) in zip(w_refs, cols):
                part = lax.dot_general(hn[:, off:off + w], w_ref[sl, :], NT, preferred_element_type=F32)
                acc = part if acc is None else acc + part
            pre_ref[:, sl] = acc.astype(BF16)
            halo = carry_ref[:, sl]
            carry_ref[:, sl] = acc[tm - HALO:, :]
            u1, u2 = _shifts_down(acc, halo, (1, 2))
            w = fcw_ref[:, sl]
            up_ref[:, sl] = (w[0:1] * u2 + w[1:2] * u1 + w[2:3] * acc).astype(BF16)

    return _pcall(
        body, "ffn_up", (s // tm,),
        [_rows(tm, D_MODEL), _full((1, D_MODEL))] + [_resident((2 * D_FF, w)) for _, w in cols]
        + [_full((3, 2 * D_FF))],
        [_rows(tm, D_MODEL), _rows(tm, 2 * D_FF), _rows(tm, 2 * D_FF)],
        [_sds((s, D_MODEL), BF16), _sds((s, 2 * D_FF), BF16), _sds((s, 2 * D_FF), BF16)],
        (h1, g, *wup_parts, fcw), scratch=[pltpu.VMEM((HALO, 2 * D_FF), F32)], comm=comm)


def _ffn_down_loss(up, wdown, h1, fnorm, target):
    s = h1.shape[0]
    tm = _row_tile(s)

    def body(up_ref, wd_ref, h1_ref, fn_ref, t_ref, act_ref, dh2_ref, loss_ref, dfn_ref):
        i = pl.program_id(0)

        @pl.when(i == 0)
        def _():
            loss_ref[...] = jnp.zeros_like(loss_ref)
            dfn_ref[...] = jnp.zeros_like(dfn_ref)

        h2 = h1_ref[...]
        for c in range(D_FF // FF_CHUNK):
            gsl = slice(c * FF_CHUNK, (c + 1) * FF_CHUNK)
            vsl = slice(D_FF + c * FF_CHUNK, D_FF + (c + 1) * FF_CHUNK)
            gate = up_ref[:, gsl].astype(F32)
            val = up_ref[:, vsl].astype(F32)
            act = (gate * _sig(gate) * val).astype(BF16)
            act_ref[:, gsl] = act
            h2 = h2 + jnp.dot(act, wd_ref[gsl, :], preferred_element_type=F32)
        r = lax.rsqrt(jnp.mean(h2 * h2, axis=-1, keepdims=True) + NORM_EPS)
        yhat = h2 * r
        fn = fn_ref[...]
        diff = yhat * fn - t_ref[...]
        loss_ref[...] += 0.5 * jnp.sum(jnp.sum(diff * diff, axis=1, keepdims=True), axis=0, keepdims=True) / D_MODEL
        dy = diff * (1.0 / D_MODEL)
        dfn_ref[...] += jnp.sum(dy * yhat, axis=0, keepdims=True)
        dyh = dy * fn
        dh2_ref[...] = r * (dyh - yhat * jnp.mean(dyh * yhat, axis=-1, keepdims=True))

    return _pcall(
        body, "ffn_down_loss", (s // tm,),
        [_rows(tm, 2 * D_FF), _resident((D_FF, D_MODEL)), _rows(tm, D_MODEL), _full((1, D_MODEL)),
         _rows(tm, D_MODEL)],
        [_rows(tm, D_FF), _rows(tm, D_MODEL), _full((1, 128)), _full((1, D_MODEL))],
        [_sds((s, D_FF), BF16), _sds((s, D_MODEL), F32), _sds((1, 128), F32), _sds((1, D_MODEL), F32)],
        (up, wdown, h1, fnorm, target))[0]


def _ffn_bwd(dh2, wdown, up, up_pre, fcw, wup_parts, h1, g, comm):
    s = dh2.shape[0]
    tm = _row_tile(s)
    cols = _col_offsets(wup_parts)

    chunk = FF_GRAD_ROWS

    def dup_cols(dh, up_ref, wd_ref, c):
        gsl = slice(c * chunk, (c + 1) * chunk)
        vsl = slice(D_FF + c * chunk, D_FF + (c + 1) * chunk)
        dact = lax.dot_general(dh, wd_ref[gsl, :], NT, preferred_element_type=F32)
        gate = up_ref[:, gsl].astype(F32)
        val = up_ref[:, vsl].astype(F32)
        sg = _sig(gate)
        return dact * val * (sg * (1.0 + gate * (1.0 - sg))), dact * gate * sg

    def body(dh_ref, wd_ref, up_ref, x_ref, w_ref, *refs):
        wup_refs = refs[:len(cols)]
        h_ref, g_ref, dx_ref, dw_ref, dh1_ref, dg_ref, carry_ref = refs[len(cols):]

        @pl.when(pl.program_id(0) == 0)
        def _():
            dw_ref[...] = jnp.zeros_like(dw_ref)
            dg_ref[...] = jnp.zeros_like(dg_ref)
            carry_ref[...] = jnp.zeros_like(carry_ref)

        dh2v = dh_ref[...]
        dh = dh2v.astype(BF16)
        dhn = [jnp.zeros((tm, w), F32) for _, w in cols]
        for c in range(D_FF // chunk):
            for d, off in zip(dup_cols(dh, up_ref, wd_ref, c), (c * chunk, D_FF + c * chunk)):
                sl = slice(off, off + chunk)
                dn = carry_ref[:, sl]
                carry_ref[:, sl] = d[0:HALO, :]
                xv = x_ref[:, sl].astype(F32)
                wv = w_ref[:, sl]
                d1, d2 = _shifts_up(d, dn, (1, 2))
                dx = (wv[2:3] * d + wv[1:2] * d1 + wv[0:1] * d2).astype(BF16)
                dx_ref[:, sl] = dx
                dhn = [a + jnp.dot(dx, wup_ref[sl, :], preferred_element_type=F32)
                       for a, wup_ref in zip(dhn, wup_refs)]
                dw_ref[0:1, sl] += jnp.sum(d2 * xv, axis=0, keepdims=True)
                dw_ref[1:2, sl] += jnp.sum(d1 * xv, axis=0, keepdims=True)
                dw_ref[2:3, sl] += jnp.sum(d * xv, axis=0, keepdims=True)
        dx1, dg = _norm_bwd_tile(h_ref[...], g_ref[...], jnp.concatenate(dhn, axis=1))
        dg_ref[...] += dg
        dh1_ref[...] = dh2v + dx1

    rows = lambda c: _rows_reversed(tm, c, s // tm)
    return _pcall(
        body, "ffn_bwd", (s // tm,),
        [rows(D_MODEL), _resident((D_FF, D_MODEL)), rows(2 * D_FF), rows(2 * D_FF), _full((3, 2 * D_FF))]
        + [_resident((2 * D_FF, w)) for _, w in cols] + [rows(D_MODEL), _full((1, D_MODEL))],
        [rows(2 * D_FF), _full((3, 2 * D_FF)), rows(D_MODEL), _full((1, D_MODEL))],
        [_sds((s, 2 * D_FF), BF16), _sds((3, 2 * D_FF), F32), _sds((s, D_MODEL), F32), _sds((1, D_MODEL), F32)],
        (dh2, wdown, up, up_pre, fcw, *wup_parts, h1, g),
        scratch=[pltpu.VMEM((HALO, 2 * D_FF), F32)], comm=comm)


def _matmul_tn(a, b, tk, name, ts=1024, comm=None):
    s, ka = a.shape
    n = b.shape[1]
    ts = min(ts, s)
    steps = s // ts

    def body(a_ref, b_ref, o_ref, acc_ref):
        j = pl.program_id(1)

        @pl.when(j == 0)
        def _():
            acc_ref[...] = jnp.zeros_like(acc_ref)

        acc_ref[...] += lax.dot_general(a_ref[...].astype(BF16), b_ref[...].astype(BF16), TN,
                                        preferred_element_type=F32)

        @pl.when(j == steps - 1)
        def _():
            o_ref[...] = acc_ref[...].astype(BF16)

    outs, couts = _pcall(
        body, name, (ka // tk, steps),
        [pl.BlockSpec((ts, tk), lambda i, j: (j, i)), pl.BlockSpec((ts, n), lambda i, j: (j, 0))],
        [pl.BlockSpec((tk, n), lambda i, j: (i, 0))], [_sds((ka, n), BF16)],
        (a, b), scratch=[pltpu.VMEM((tk, n), F32)], comm=comm)
    return outs[0] if comm is None else (outs[0], couts)


def _norm_bwd_tile(xv, g, dy):
    r = lax.rsqrt(jnp.mean(xv * xv, axis=-1, keepdims=True) + NORM_EPS)
    xhat = xv * r
    dg = jnp.sum(dy * xhat, axis=0, keepdims=True)
    dyh = dy * g
    return r * (dyh - xhat * jnp.mean(dyh * xhat, axis=-1, keepdims=True)), dg


def _mix_bwd(dh1, wout, gates, attn, wa, wc, cbx, conv_w, comm):
    s = dh1.shape[0]
    tm = _row_tile(s)
    steps = s // tm

    def body(dh_ref, wo_ref, gate_ref, attn_ref, wa_ref, wc_ref, cbx_ref, halo_ref,
             cw_ref, dg_ref, dattn_ref, dcb_ref, dcc_ref, dcx_ref, dw_ref, gwo_ref, gwa_ref, gwc_ref,
             acc_o, acc_a, acc_c, carry_ref):
        i = pl.program_id(0)

        @pl.when(i == 0)
        def _():
            dw_ref[...] = jnp.zeros_like(dw_ref)
            acc_o[...] = jnp.zeros_like(acc_o)
            acc_a[...] = jnp.zeros_like(acc_a)
            acc_c[...] = jnp.zeros_like(acc_c)
            carry_ref[...] = jnp.zeros_like(carry_ref)

        cb, cc, cx, u, cv = _conv_u(cbx_ref, halo_ref, cw_ref, i == steps - 1)
        attn = attn_ref[...]
        conv = (cb * cv).astype(BF16)
        ap = jnp.dot(attn, wa_ref[...], preferred_element_type=F32)
        cp = jnp.dot(conv, wc_ref[...], preferred_element_type=F32)
        dhb = dh_ref[...].astype(BF16)
        dm = lax.dot_general(dhb, wo_ref[...], NT, preferred_element_type=F32)
        sa = _sig(gate_ref[:, 0:D_MODEL].astype(F32))
        sc = _sig(gate_ref[:, D_MODEL:2 * D_MODEL].astype(F32))
        merged = (sa * ap + sc * cp).astype(BF16)
        da = (dm * sa).astype(BF16)
        dc = (dm * sc).astype(BF16)
        dg_ref[:, 0:D_MODEL] = (dm * ap * sa * (1.0 - sa)).astype(BF16)
        dg_ref[:, D_MODEL:2 * D_MODEL] = (dm * cp * sc * (1.0 - sc)).astype(BF16)
        dattn_ref[...] = lax.dot_general(da, wa_ref[...], NT, preferred_element_type=F32).astype(BF16)
        dconv = lax.dot_general(dc, wc_ref[...], NT, preferred_element_type=F32)
        dcb_ref[...] = (dconv * cv).astype(BF16)
        d = dconv * cb
        dn = carry_ref[...]
        carry_ref[...] = d[0:HALO, :]
        d1, d2 = _shifts_up(d, dn, (1, 2))
        du = cw_ref[2:3, :] * d + cw_ref[1:2, :] * d1 + cw_ref[0:1, :] * d2
        dcc_ref[...] = (du * cx).astype(BF16)
        dcx_ref[...] = (du * cc).astype(BF16)
        dw_ref[0:1, :] += jnp.sum(d2 * u, axis=0, keepdims=True)
        dw_ref[1:2, :] += jnp.sum(d1 * u, axis=0, keepdims=True)
        dw_ref[2:3, :] += jnp.sum(d * u, axis=0, keepdims=True)
        acc_o[...] += lax.dot_general(merged, dhb, TN, preferred_element_type=F32)
        acc_a[...] += lax.dot_general(attn, da, TN, preferred_element_type=F32)
        acc_c[...] += lax.dot_general(conv, dc, TN, preferred_element_type=F32)

        @pl.when(i == steps - 1)
        def _():
            gwo_ref[...] = acc_o[...].astype(BF16)
            gwa_ref[...] = acc_a[...].astype(BF16)
            gwc_ref[...] = acc_c[...].astype(BF16)

    rows = lambda c: _rows_reversed(tm, c, steps)
    return _pcall(
        body, "mix_bwd", (steps,),
        [rows(D_MODEL), _full((D_MODEL, D_MODEL)), rows(GATE_W), rows(ATTN_W), _full((ATTN_W, D_MODEL)),
         _full((CONV_W, D_MODEL)), rows(CBX_W), pl.BlockSpec((HALO, CBX_W), _prev_halo_map_reversed(tm, steps)),
         _full((3, CONV_W))],
        [rows(GATE_W), rows(ATTN_W), rows(CONV_W), rows(CONV_W), rows(CONV_W),
         _full((3, CONV_W)), _full((D_MODEL, D_MODEL)), _full((ATTN_W, D_MODEL)), _full((CONV_W, D_MODEL))],
        [_sds((s, GATE_W), BF16), _sds((s, ATTN_W), BF16), _sds((s, CONV_W), BF16), _sds((s, CONV_W), BF16),
         _sds((s, CONV_W), BF16), _sds((3, CONV_W), F32), _sds((D_MODEL, D_MODEL), BF16),
         _sds((ATTN_W, D_MODEL), BF16), _sds((CONV_W, D_MODEL), BF16)],
        (dh1, wout, gates, attn, wa, wc, cbx, cbx, conv_w),
        scratch=[pltpu.VMEM((D_MODEL, D_MODEL), F32), pltpu.VMEM((ATTN_W, D_MODEL), F32),
                 pltpu.VMEM((CONV_W, D_MODEL), F32), pltpu.VMEM((HALO, CONV_W), F32)], comm=comm)


def _attn_bwd(qkv, sinks, attn, lse, dattn, comm):
    s = qkv.shape[0]
    tq = Q_BLOCKS * BLOCK

    def body(sinks_ref, q_ref, kp_ref, kc_ref, vp_ref, vc_ref, o_ref, lse_ref, do_ref,
             dq_ref, dk_ref, dv_ref, ds_ref):
        n = pl.program_id(0)

        @pl.when(n == 0)
        def _():
            dk_ref[...] = jnp.zeros_like(dk_ref)
            dv_ref[...] = jnp.zeros_like(dv_ref)
            ds_ref[...] = jnp.zeros_like(ds_ref)

        lower = _lower_lanes()
        lane = lax.broadcasted_iota(jnp.int32, (BLOCK, 128), 1)
        lower2 = lax.broadcasted_iota(jnp.int32, (2 * BLOCK, 128), 1) < HEAD_DIM
        lane1 = lax.broadcasted_iota(jnp.int32, (1, 128), 1)
        dsink = jnp.zeros((1, 128), F32)
        for sub in reversed(range(Q_BLOCKS)):
            rows = slice(sub * BLOCK, (sub + 1) * BLOCK)
            mask = _attn_mask(n > 0 if sub == 0 else True)
            kw, vw = _window(kp_ref, kc_ref, sub), _window(vp_ref, vc_ref, sub)
            qv, ov, dov, lsev = q_ref[rows, :], o_ref[rows, :], do_ref[rows, :], lse_ref[rows, :]
            dk_fold, dv_fold = [], []
            for kh in range(2):
                qs = _stack_heads(qv, kh)
                dos = _stack_heads(dov, kh)
                os_ = _stack_heads(ov, kh)
                kd, vd = _dup_kv(kw, kh), _dup_kv(vw, kh)
                lse = jnp.concatenate(
                    [jnp.sum(jnp.where(lane == kh * 4 + g, lsev, 0.0), axis=1, keepdims=True) for g in range(4)],
                    axis=0)
                sc = lax.dot_general(qs, kd, NT, preferred_element_type=F32) * ATTN_SCALE
                p = jnp.exp(jnp.where(mask, sc, NEG) - lse)
                dp = lax.dot_general(dos, vd, NT, preferred_element_type=F32)
                delta = jnp.sum(dos.astype(F32) * os_.astype(F32), axis=1, keepdims=True)
                dsc = (p * (dp - delta) * ATTN_SCALE).astype(BF16)
                dqs = jnp.dot(dsc, kd, preferred_element_type=F32)
                for pair in range(2):
                    lo = dqs[(2 * pair) * BLOCK:(2 * pair + 1) * BLOCK]
                    hi = dqs[(2 * pair + 1) * BLOCK:(2 * pair + 2) * BLOCK]
                    col = (kh * 2 + pair) * 128
                    dq_ref[rows, col:col + 128] = jnp.where(lower, lo, hi).astype(BF16)
                dkd = lax.dot_general(dsc, qs, TN, preferred_element_type=F32)
                dvd = lax.dot_general(p.astype(BF16), dos, TN, preferred_element_type=F32)
                dk_fold.append(dkd + pltpu.roll(dkd, HEAD_DIM, axis=1))
                dv_fold.append(dvd + pltpu.roll(dvd, HEAD_DIM, axis=1))
                psink = jnp.exp(_sink_col(sinks_ref, kh) - lse) * delta
                for g in range(4):
                    tot = jnp.sum(psink[g * BLOCK:(g + 1) * BLOCK], axis=0, keepdims=True)
                    dsink = dsink - jnp.where(lane1 == kh * 4 + g, tot, 0.0)
            dk2 = jnp.where(lower2, dk_fold[0], dk_fold[1])
            dv2 = jnp.where(lower2, dv_fold[0], dv_fold[1])
            cur = pl.ds(pl.multiple_of((Q_BLOCKS * n + sub) * BLOCK, BLOCK), BLOCK)
            dk_ref[cur, :] += dk2[BLOCK:]
            dv_ref[cur, :] += dv2[BLOCK:]
            if sub > 0:
                prev = pl.ds(pl.multiple_of((Q_BLOCKS * n + sub - 1) * BLOCK, BLOCK), BLOCK)
                dk_ref[prev, :] += dk2[:BLOCK]
                dv_ref[prev, :] += dv2[:BLOCK]
        ds_ref[...] += dsink

        @pl.when(n > 0)
        def _():
            prev = pl.ds(pl.multiple_of((Q_BLOCKS * n - 1) * BLOCK, BLOCK), BLOCK)
            dk_ref[prev, :] += dk2[:BLOCK]
            dv_ref[prev, :] += dv2[:BLOCK]

    blk = lambda w: pl.BlockSpec((tq, w), lambda n: (n, 0))
    return _pcall(
        body, "attn_bwd", (s // tq,),
        [pl.BlockSpec(memory_space=pltpu.SMEM)] + _attn_specs() + [blk(ATTN_W), blk(128), blk(ATTN_W)],
        [blk(ATTN_W), _full((s, KV_W)), _full((s, KV_W)), _full((1, 128))],
        [_sds((s, ATTN_W), BF16), _sds((s, KV_W), F32), _sds((s, KV_W), F32), _sds((1, 128), F32)],
        (sinks, qkv, qkv, qkv, qkv, qkv, attn, lse, dattn), comm=comm)


DPROJ_PIECES = (ATTN_W, KV_W, KV_W, CONV_W, CONV_W, CONV_W, GATE_W)
DPROJ_OFFSETS = tuple(sum(DPROJ_PIECES[:k]) for k in range(len(DPROJ_PIECES)))


def _grad_w_in(pieces, xn, comm):
    s = xn.shape[0]
    ts = min(1024, s)
    steps = s // ts
    rows0 = DPROJ_OFFSETS[6]

    def body(*refs):
        p_refs, b_ref, o_ref, acc_ref, stage_ref, sem = refs[:7], refs[7], refs[8], refs[9], refs[10], refs[11]
        i, j = pl.program_id(0), pl.program_id(1)

        @pl.when(j == 0)
        def _():
            acc_ref[...] = jnp.zeros_like(acc_ref)

        bv = b_ref[...]

        def flush(lo, n):
            stage_ref[0:n, :] = acc_ref[0:n, :].astype(BF16)
            cp = pltpu.make_async_copy(stage_ref.at[0:n, :], o_ref.at[lo:lo + n, :], sem)
            cp.start()
            cp.wait()

        @pl.when(i == 0)
        def _():
            for p_ref, off, w in zip(p_refs[:6], DPROJ_OFFSETS[:6], DPROJ_PIECES[:6]):
                acc_ref[off:off + w, :] += lax.dot_general(p_ref[...].astype(BF16), bv, TN,
                                                           preferred_element_type=F32)

            @pl.when(j == steps - 1)
            def _():
                flush(0, rows0)

        @pl.when(i == 1)
        def _():
            acc_ref[0:GATE_W, :] += lax.dot_general(p_refs[6][...], bv, TN, preferred_element_type=F32)

            @pl.when(j == steps - 1)
            def _():
                flush(rows0, GATE_W)

    def piece_spec(w, group):
        return pl.BlockSpec((ts, w), lambda i, j: (jnp.where(i == group, j, 0), 0))

    outs, couts = _pcall(
        body, "grad_w_in", (2, steps),
        [piece_spec(w, 0) for w in DPROJ_PIECES[:6]] + [piece_spec(GATE_W, 1),
                                                         pl.BlockSpec((ts, D_MODEL), lambda i, j: (j, 0))],
        [ANY], [_sds((IN_W, D_MODEL), BF16)], (*pieces, xn),
        scratch=[pltpu.VMEM((rows0, D_MODEL), F32), pltpu.VMEM((rows0, D_MODEL), BF16), pltpu.SemaphoreType.DMA],
        comm=comm)
    return outs[0], couts


def _inproj_bwd(pieces, win_t, x, g, dh1, comm):
    s = x.shape[0]
    tm = _row_tile(s, 512)

    def body(*refs):
        p_refs = refs[:7]
        w_ref, x_ref, g_ref, dh_ref, dx_ref, db_ref, dg_ref = refs[7:]

        @pl.when(pl.program_id(0) == 0)
        def _():
            db_ref[...] = jnp.zeros_like(db_ref)
            dg_ref[...] = jnp.zeros_like(dg_ref)

        dxn = jnp.zeros((tm, D_MODEL), F32)
        for p_ref, off, w in zip(p_refs, DPROJ_OFFSETS, DPROJ_PIECES):
            v = p_ref[...].astype(BF16)
            db_ref[:, off:off + w] += jnp.sum(v.astype(F32), axis=0, keepdims=True)
            dxn = dxn + jnp.dot(v, w_ref[off:off + w, :], preferred_element_type=F32)
        dx, dg = _norm_bwd_tile(x_ref[...], g_ref[...], dxn)
        dg_ref[...] += dg
        dx_ref[...] = dh_ref[...] + dx

    return _pcall(
        body, "inproj_bwd", (s // tm,),
        [_rows(tm, w) for w in DPROJ_PIECES] + [_resident((IN_W, D_MODEL)), _rows(tm, D_MODEL), _full((1, D_MODEL)),
                                                _rows(tm, D_MODEL)],
        [_rows(tm, D_MODEL), _full((8, IN_W)), _full((8, D_MODEL))],
        [_sds((s, D_MODEL), F32), _sds((8, IN_W), F32), _sds((8, D_MODEL), F32)],
        (*pieces, win_t, x, g, dh1), comm=comm)


def _adam_math(w, g, m, v):
    m2 = ADAM_B1 * m + (1.0 - ADAM_B1) * g
    v2 = ADAM_B2 * v + (1.0 - ADAM_B2) * (g * g)
    m_hat = m2 / (1.0 - ADAM_B1 ** ADAM_STEP)
    v_hat = v2 / (1.0 - ADAM_B2 ** ADAM_STEP)
    delta = -ADAM_LR * (m_hat / (jnp.sqrt(v_hat) + ADAM_EPS) + ADAM_WD * w)
    return delta, m2, v2


def _sum_slots(ref):
    tot = ref[0].astype(F32)
    for i in range(1, ref.shape[0]):
        tot = tot + ref[i].astype(F32)
    return tot


def _pair_add(partials, theirs, tr, name):
    r = partials.shape[0] // N_DEV
    c = partials.shape[1]
    nt = r // tr
    core = lax.axis_index("c").astype(jnp.int32).reshape(1)

    def body(core_ref, a_ref, b_ref, o_ref):
        o_ref[...] = (a_ref[...].astype(F32) + b_ref[...].astype(F32)).astype(BF16)

    grid_spec = pltpu.PrefetchScalarGridSpec(
        num_scalar_prefetch=1, grid=(4 * nt,),
        in_specs=[pl.BlockSpec((None, None, tr, c), lambda i, core_ref: (i // nt, core_ref[0], i % nt, 0)),
                  pl.BlockSpec((tr, c), lambda i, core_ref: (i, 0))],
        out_specs=pl.BlockSpec((tr, c), lambda i, core_ref: (i, 0)))
    return pl.pallas_call(body, name=name, grid_spec=grid_spec, out_shape=_sds((4 * r, c), BF16))(
        core, partials.reshape(4, 2, r, c), theirs)


def _sum_adamw(parts, w, m, v, tr, name):
    r, c = w.shape

    def body(p_ref, w_ref, m_ref, v_ref, g_ref, d_ref, m2_ref, v2_ref):
        g = _sum_slots(p_ref)
        g_ref[...] = g
        d_ref[...], m2_ref[...], v2_ref[...] = _adam_math(w_ref[...], g, m_ref[...], v_ref[...])

    spec = pl.BlockSpec((tr, c), lambda i: (i, 0))
    return _pcall(body, name, (r // tr,), [pl.BlockSpec((N_DEV, tr, c), lambda i: (0, i, 0)), spec, spec, spec],
                  [spec] * 4, [_sds((r, c), F32)] * 4, (parts, w, m, v))[0]


def _sum_parts_adamw(parts, w, m, v, tr, name):
    c = w.shape[1]
    tiles = [p.shape[1] // tr for p in parts]
    starts = [sum(tiles[:k]) for k in range(len(parts))]
    n_parts = len(parts)

    def body(*refs):
        p_refs = refs[:n_parts]
        w_ref, m_ref, v_ref, g_ref, d_ref, m2_ref, v2_ref = refs[n_parts:]
        i = pl.program_id(0)
        for p_ref, st, nt in zip(p_refs, starts, tiles):
            @pl.when(jnp.logical_and(i >= st, i < st + nt))
            def _(p_ref=p_ref):
                g_ref[...] = _sum_slots(p_ref)

        d_ref[...], m2_ref[...], v2_ref[...] = _adam_math(w_ref[...], g_ref[...], m_ref[...], v_ref[...])

    def part_spec(p, st, nt):
        return pl.BlockSpec((p.shape[0], tr, c), lambda i: (0, jnp.clip(i - st, 0, nt - 1), 0))

    spec = pl.BlockSpec((tr, c), lambda i: (i, 0))
    return _pcall(
        body, name, (sum(tiles),),
        [part_spec(p, st, nt) for p, st, nt in zip(parts, starts, tiles)] + [spec, spec, spec],
        [spec] * 4, [_sds(w.shape, F32)] * 4, (*parts, w, m, v))[0]


ROW_MIX, ROW_FFN, ROW_FINAL, ROW_SINKS, ROW_LOSS, ROW_BIN, ROW_CW, ROW_FCW = 0, 1, 2, 3, 4, 5, 10, 13
FCW_ROWS = 6


def _wide_pieces(width):
    return [(k * D_MODEL, min(D_MODEL, width - k * D_MODEL)) for k in range(-(-width // D_MODEL))]


def _pack_small(dffn, dfn, dsink, loss, dcw, dfcw):
    def body(ffn_ref, fn_ref, sink_ref, loss_ref, cw_ref, fcw_ref, o_ref):
        o_ref[...] = jnp.zeros_like(o_ref)
        o_ref[ROW_FFN:ROW_FFN + 1, :] = ffn_ref[...]
        o_ref[ROW_FINAL:ROW_FINAL + 1, :] = fn_ref[...]
        o_ref[ROW_SINKS:ROW_SINKS + 1, 0:128] = sink_ref[...]
        o_ref[ROW_LOSS:ROW_LOSS + 1, 0:128] = loss_ref[...]
        o_ref[ROW_CW:ROW_CW + 3, 0:CONV_W] = cw_ref[...]
        for a in range(3):
            for k, (off, w) in enumerate(_wide_pieces(2 * D_FF)):
                row = ROW_FCW + FCW_ROWS * a + k
                o_ref[row:row + 1, 0:w] = fcw_ref[a:a + 1, off:off + w]

    return pl.pallas_call(body, name="pack_small", out_shape=_sds((SMALL_ROWS, D_MODEL), F32))(
        dffn, dfn, dsink, loss, dcw, dfcw)


def _small_sums_adamw(r_small, r_dmix, r_dbin, params):
    rows = (None, None, ROW_SINKS, ROW_FFN, ROW_FINAL)

    def sum_row0(ref):
        tot = ref[0:1, :]
        for i in range(1, N_DEV):
            tot = tot + ref[8 * i:8 * i + 1, :]
        return tot

    def body(*refs):
        r_ref, late_refs, p_refs, o_refs = refs[0], refs[1:3], refs[3:18], refs[18:]
        tot = _sum_slots(r_ref)
        for k, row in enumerate(rows):
            w_ref, m_ref, v_ref = p_refs[3 * k:3 * k + 3]
            g_ref, d_ref, m2_ref, v2_ref = o_refs[4 * k:4 * k + 4]
            if row is None:
                g_ref[...] = sum_row0(late_refs[k])
            else:
                for j, (off, w) in enumerate(_wide_pieces(w_ref.shape[1])):
                    g_ref[:, off:off + w] = tot[row + j:row + j + 1, 0:w]
            d_ref[...], m2_ref[...], v2_ref[...] = _adam_math(w_ref[...], g_ref[...], m_ref[...], v_ref[...])
        cw_ref, fcw_ref, loss_ref = o_refs[20:]
        cw_ref[...] = tot[ROW_CW:ROW_CW + 3, 0:CONV_W]
        for a in range(3):
            for j, (off, w) in enumerate(_wide_pieces(2 * D_FF)):
                row = ROW_FCW + FCW_ROWS * a + j
                fcw_ref[a:a + 1, off:off + w] = tot[row:row + 1, 0:w]
        loss_ref[...] = tot[ROW_LOSS:ROW_LOSS + 1, 0:128]

    flat = [t for p in params for t in p]
    out_shape = [_sds(p[0].shape, F32) for p in params for _ in range(4)]
    out_shape += [_sds((3, CONV_W), F32), _sds((3, 2 * D_FF), F32), _sds((1, 128), F32)]
    res = pl.pallas_call(body, name="small_sums_adamw", out_shape=out_shape)(r_small, r_dmix, r_dbin, *flat)
    return [tuple(res[4 * k:4 * k + 4]) for k in range(5)], res[20], res[21], res[22]


def _adamw_pair(a, b):
    def body(*refs):
        for k in range(2):
            w_ref, g_ref, m_ref, v_ref = refs[4 * k:4 * k + 4]
            d_ref, m2_ref, v2_ref = refs[8 + 3 * k:8 + 3 * k + 3]
            d_ref[...], m2_ref[...], v2_ref[...] = _adam_math(w_ref[...], g_ref[...], m_ref[...], v_ref[...])

    out_shape = [_sds(a[0].shape, F32)] * 3 + [_sds(b[0].shape, F32)] * 3
    res = pl.pallas_call(body, name="adamw_conv_weights", out_shape=out_shape)(*a, *b)
    return tuple(res[:3]), tuple(res[3:])


def _pad_cols(a, c):
    return jnp.pad(a, ((0, 0), (0, c - a.shape[1])))


def _to_col_slabs(g):
    r = g.shape[0]
    return jnp.transpose(g.reshape(r, N_DEV, 128), (1, 0, 2)).reshape(N_DEV * r, 128)


def _from_col_slabs(t):
    r = t.shape[0] // N_DEV
    return jnp.transpose(t.reshape(N_DEV, r, 128), (1, 0, 2)).reshape(r, N_DEV * 128)


def _slots(t):
    return t.reshape(N_DEV, t.shape[0] // N_DEV, t.shape[1])


def kernel(x, mix_norm, w_in, b_in, sinks, conv_w, w_attn_branch, w_conv_branch, w_out, ffn_norm, w_up, ffn_conv_w, w_down, final_norm, loss_target, m_mix_norm, m_w_in, m_b_in, m_sinks, m_conv_w, m_w_attn_branch, m_w_conv_branch, m_w_out, m_ffn_norm, m_w_up, m_ffn_conv_w, m_w_down, m_final_norm, v_mix_norm, v_w_in, v_b_in, v_sinks, v_conv_w, v_w_attn_branch, v_w_conv_branch, v_w_out, v_ffn_norm, v_w_up, v_ffn_conv_w, v_w_down, v_final_norm):
    xs, tgt = x[0], loss_target[0]
    me = 4 * lax.axis_index("x") + 2 * lax.axis_index("y") + lax.axis_index("c")
    in_rows, up_rows = IN_W // N_DEV, 2 * D_FF // N_DEV

    conv_sh = jnp.concatenate([_pad_cols(ffn_conv_w[0], 768), _pad_cols(conv_w[0], 768),
                               jnp.zeros((2, 768), F32)], axis=0)
    win_sh, wup_sh = w_in[0].T.astype(BF16), w_up[0].T.astype(BF16)
    wout_sh, wdown_sh = w_out[0].astype(BF16), w_down[0].astype(BF16)
    wa_sh, wc_sh = w_attn_branch[0].astype(BF16), w_conv_branch[0].astype(BF16)

    quarter, half = D_MODEL // 4, D_MODEL // 2
    phases = dict(forward_at=0.375, pass_on_at=0.875)
    (win_t,) = _exchange_only(_AllGather([win_sh]), "gather_w_in")
    (xn, qkv, cbx, gates), (wa_s, wc_s, conv_g, wup_a) = _norm_inproj(
        xs, mix_norm, win_t, b_in, _AllGather([wa_sh, wc_sh, conv_sh, (wup_sh, 0, quarter)], **phases))
    (attn, lse), (wup_b, wout) = _attn_fwd(qkv, sinks,
                                           _AllGather([(wup_sh, quarter, quarter), wout_sh], **phases))
    wa, wc = _from_col_slabs(wa_s), _from_col_slabs(wc_s)
    conv_g = conv_g.reshape(N_DEV, 8, 768)
    fcw = jnp.transpose(conv_g[:, 0:3, :up_rows], (1, 0, 2)).reshape(3, 2 * D_FF)
    cw = jnp.transpose(conv_g[:, 3:6, :CONV_W // N_DEV], (1, 0, 2)).reshape(3, CONV_W)
    (h1,), (wup_c,) = _mix_fwd(xs, cbx, gates, attn, cw, wa, wc, wout,
                               _AllGather([(wup_sh, half, half)], **phases))
    wup_parts = (wup_a, wup_b, wup_c)
    (hn, up_pre, up), (wdown,) = _ffn_up(h1, ffn_norm, wup_parts, fcw,
                                         _AllGather([wdown_sh], forward_at=0.25, pass_on_at=0.75))
    act, dh2, loss_p, dfn_p = _ffn_down_loss(up, wdown, h1, final_norm.reshape(1, D_MODEL), tgt)

    dn_rows = D_FF // N_DEV
    g_wdown = _matmul_tn(act, dh2, FF_GRAD_ROWS, "grad_w_down")
    (dup_pre, dfcw_p, dh1, dffn_p), (r_wdown,) = _ffn_bwd(dh2, wdown, up, up_pre, fcw, wup_parts, h1, ffn_norm,
                                                         _ReduceScatter([(g_wdown, 0, dn_rows)]))
    g_wup_t = _matmul_tn(dup_pre, hn, FF_GRAD_ROWS, "grad_w_up")
    (wup_theirs,) = _exchange_only(_PairExchange([g_wup_t]), "pair_exchange_w_up")
    q_wup = _pair_add(g_wup_t, wup_theirs, up_rows // 2, "pair_add_w_up")
    (dgates, dattn, dcb, dcc, dcx, dcw_p, g_wout, g_wa_nat, g_wc_nat), (r_wup,) = _mix_bwd(
        dh1, wout, gates, attn, wa, wc, cbx, cw, _ChipExchange([q_wup]))
    g_wa, g_wc = _to_col_slabs(g_wa_nat), _to_col_slabs(g_wc_nat)
    (dq, dk, dv, dsink_p), (r_wout,) = _attn_bwd(
        qkv, sinks, attn, lse, dattn, _ReduceScatter([(g_wout, 0, D_MODEL // N_DEV)]))
    dproj = (dq, dk, dv, dcb, dcc, dcx, dgates)
    small = _pack_small(dffn_p, dfn_p, dsink_p, loss_p, dcw_p, dfcw_p)
    g_win_t, (r_wa, r_wc, r_small) = _grad_w_in(
        dproj, xn, _ReduceScatter([(g_wa, 0, ATTN_W), (g_wc, 0, CONV_W)], [small]))
    (win_theirs,) = _exchange_only(_PairExchange([g_win_t]), "pair_exchange_w_in")
    q_win = _pair_add(g_win_t, win_theirs, in_rows // 2, "pair_add_w_in")
    (dx, _, _), (r_win, r_dbin, r_dmix) = _inproj_bwd(
        dproj, win_t, xs, mix_norm, dh1,
        _ChipExchangeThenBroadcast([q_win], late_from=(1, 2), late_shapes=[(8, IN_W), (8, D_MODEL)]))

    fn2, m_fn2, v_fn2 = (t.reshape(1, D_MODEL) for t in (final_norm, m_final_norm, v_final_norm))
    small_res, g_cw_full, g_fcw_full, loss_row = _small_sums_adamw(
        _slots(r_small), r_dmix, r_dbin,
        [(mix_norm, m_mix_norm, v_mix_norm), (b_in, m_b_in, v_b_in), (sinks, m_sinks, v_sinks),
         (ffn_norm, m_ffn_norm, v_ffn_norm), (fn2, m_fn2, v_fn2)])
    loss = loss_row[0, 0]
    g_cw = lax.dynamic_slice_in_dim(g_cw_full, me * (CONV_W // N_DEV), CONV_W // N_DEV, axis=1)
    g_fcw = lax.dynamic_slice_in_dim(g_fcw_full, me * up_rows, up_rows, axis=1)
    taps = lambda t: jnp.transpose(t, (1, 0, 2))
    g_cw, g_fcw = g_cw[:, None, :], g_fcw[:, None, :]
    cw_res, fcw_res = _adamw_pair((taps(conv_w), g_cw, taps(m_conv_w), taps(v_conv_w)),
                                  (taps(ffn_conv_w), g_fcw, taps(m_ffn_conv_w), taps(v_ffn_conv_w)))

    big = {}
    big["w_in"] = tuple(t.T for t in _sum_parts_adamw(
        [r_win.reshape(4, in_rows, D_MODEL)], w_in[0].T, m_w_in[0].T, v_w_in[0].T, in_rows // 2, "adamw_w_in"))
    big["w_up"] = tuple(t.T for t in _sum_parts_adamw(
        [r_wup.reshape(4, up_rows, D_MODEL)], w_up[0].T, m_w_up[0].T, v_w_up[0].T, up_rows // 4, "adamw_w_up"))
    big["w_out"] = _sum_adamw(_slots(r_wout), w_out[0], m_w_out[0], v_w_out[0], 128, "adamw_w_out")
    big["w_down"] = _sum_adamw(_slots(r_wdown), w_down[0], m_w_down[0], v_w_down[0], dn_rows // 2, "adamw_w_down")
    big["w_attn_branch"] = _sum_adamw(_slots(r_wa), w_attn_branch[0], m_w_attn_branch[0], v_w_attn_branch[0], 256,
                                      "adamw_w_attn_branch")
    big["w_conv_branch"] = _sum_adamw(_slots(r_wc), w_conv_branch[0], m_w_conv_branch[0], v_w_conv_branch[0], 256,
                                      "adamw_w_conv_branch")

    res = dict(zip(("mix_norm", "b_in", "sinks", "ffn_norm"), small_res[:4]))
    res["final_norm"] = tuple(t.reshape(final_norm.shape) for t in small_res[4])
    res["conv_w"] = tuple(jnp.transpose(t, (1, 0, 2)) for t in (g_cw,) + cw_res)
    res["ffn_conv_w"] = tuple(jnp.transpose(t, (1, 0, 2)) for t in (g_fcw,) + fcw_res)
    for name, ref_w in (("w_in", w_in), ("w_up", w_up), ("w_out", w_out), ("w_down", w_down),
                        ("w_attn_branch", w_attn_branch), ("w_conv_branch", w_conv_branch)):
        res[name] = tuple(t.reshape(ref_w.shape) for t in big[name])

    order = ["mix_norm", "w_in", "b_in", "sinks", "conv_w", "w_attn_branch", "w_conv_branch", "w_out",
             "ffn_norm", "w_up", "ffn_conv_w", "w_down", "final_norm"]
    out = [loss, dx.reshape(x.shape)]
    for k in range(4):
        out += [res[name][k] for name in order]
    return tuple(out)
```

```python
import math

import jax
import jax.numpy as jnp
from jax import lax
from jax.experimental import pallas as pl
from jax.experimental.pallas import tpu as pltpu

F32 = jnp.float32
BF16 = jnp.bfloat16
MESH = pl.DeviceIdType.MESH
N_DEV = 8

D_MODEL = 1024
HEAD_DIM = 64
N_HEADS = 8
BLOCK = 128
ATTN_W = 512
KV_W = 128
CONV_W = 512
QKV_W = ATTN_W + 2 * KV_W
CBX_W = 3 * CONV_W
GATE_W = 2 * D_MODEL
IN_W = QKV_W + CBX_W + GATE_W
D_FF = 2816
FF_CHUNK = 256
FF_GRAD_ROWS = 1408
NORM_EPS = 1e-5
ATTN_SCALE = HEAD_DIM ** -0.5
NEG = -1e30
HALO = 16

ADAM_LR = 0.001
ADAM_B1 = 0.9
ADAM_B2 = 0.999
ADAM_EPS = 1e-08
ADAM_WD = 0.01
ADAM_STEP = 10

VMEM_LIMIT = 56 * 1024 * 1024
SMALL_ROWS = 32

NT = (((1,), (1,)), ((), ()))
TN = (((0,), (0,)), ((), ()))
ANY = pl.BlockSpec(memory_space=pl.ANY)


def _sig(v):
    return 1.0 / (1.0 + jnp.exp(-v))


def _row_tile(s, pref=256):
    return pref if s % pref == 0 else s


def _shifts_down(u, halo, ks):
    ext = jnp.concatenate([halo, u], axis=0)
    return [pltpu.roll(ext, k, axis=0)[HALO:, :] for k in ks]


def _shifts_up(u, halo, ks):
    n = u.shape[0]
    ext = jnp.concatenate([u, halo], axis=0)
    return [pltpu.roll(ext, n + HALO - k, axis=0)[:n, :] for k in ks]


def _rows_reversed(tm, c, steps):
    return pl.BlockSpec((tm, c), lambda i: (steps - 1 - i, 0))


def _prev_halo_map_reversed(tm, steps):
    return lambda i: (jnp.maximum((steps - 1 - i) * (tm // HALO) - 1, 0), 0)


def _prev_halo_map(tm):
    return lambda i: (jnp.maximum(i * (tm // HALO) - 1, 0), 0)


def _full(shape):
    return pl.BlockSpec(shape, lambda *_: (0,) * len(shape))


def _resident(shape):
    return pl.BlockSpec(shape, lambda *_: (0,) * len(shape), pipeline_mode=pl.Buffered(1))


def _rows(tm, c):
    return pl.BlockSpec((tm, c), lambda i: (i, 0))


def _sds(shape, dtype):
    return jax.ShapeDtypeStruct(shape, dtype)


def _my_place():
    x, y, c = lax.axis_index("x"), lax.axis_index("y"), lax.axis_index("c")
    return x, y, c


ALL_PEERS = tuple((j >> 2, (j >> 1) & 1, j & 1) for j in range(1, N_DEV))
SIBLING_PEER = ((0, 0, 1),)
CHIP_PEERS = ((0, 1, 0), (1, 0, 0), (1, 1, 0))
BARRIER_ID = {ALL_PEERS: 0, SIBLING_PEER: 1, CHIP_PEERS: 2}


def _barrier_signal(peers):
    x, y, c = _my_place()
    barrier = pltpu.get_barrier_semaphore()
    for dx, dy, dc in peers:
        pl.semaphore_signal(barrier, inc=1, device_id=(x ^ dx, y ^ dy, c ^ dc), device_id_type=MESH)


def _barrier_wait(peers):
    pl.semaphore_wait(pltpu.get_barrier_semaphore(), len(peers))


def _start_exchange(remote, local):
    for cp in local + remote:
        cp.start()


def _finish_exchange(remote, local):
    for cp in remote:
        cp.wait_recv()
    for cp in remote:
        cp.wait_send()
    for cp in local:
        cp.wait()


class _AllGather:
    peers = ALL_PEERS
    SLOTS = 10

    def __init__(self, shards, pass_on_at=None, forward_at=None):
        self.ins = [s[0] if isinstance(s, tuple) else s for s in shards]
        self.cols = [s[1:] if isinstance(s, tuple) else None for s in shards]
        self.middle_at, self.forward_at = pass_on_at, forward_at
        assert pass_on_at is None or forward_at is not None
        n = len(shards)
        self.out_shape = [_sds((N_DEV * s.shape[0], s.shape[1] if c is None else c[1]), s.dtype)
                          for s, c in zip(self.ins, self.cols)]
        self.sems = [pltpu.SemaphoreType.DMA((self.SLOTS * n,)), pltpu.SemaphoreType.DMA((self.SLOTS * n,)),
                     pltpu.SemaphoreType.DMA((n,))]

    def _plan(self, ins, outs, sems):
        send_sems, recv_sems, local_sems = sems
        x, y, c = _my_place()
        me, sibling = (x, y, c), (x, y, 1 - c)
        x_chip, y_chip, far_chip = (1 - x, y), (x, 1 - y), (1 - x, 1 - y)
        sends, lands, mine = [], [], []
        for k in range(len(ins)):
            r = ins[k].shape[0]
            h = (r // 2) // 16 * 16
            whole, first, second = (0, r), (0, h), (h, r - h)

            def rows(dev, rng, k=k, r=r):
                start = pl.multiple_of((4 * dev[0] + 2 * dev[1] + dev[2]) * r + rng[0], 8)
                return outs[k].at[pl.ds(start, rng[1]), :]

            def own(rng, k=k):
                cols = self.cols[k]
                if cols is None:
                    return ins[k].at[pl.ds(rng[0], rng[1]), :]
                return ins[k].at[pl.ds(rng[0], rng[1]), pl.ds(cols[0], cols[1])]

            def copy(slot, block, rng, to, mine_src=False, k=k, rows=rows, own=own):
                if rng[1] == 0:
                    return None
                return pltpu.make_async_remote_copy(
                    src_ref=own(rng) if mine_src else rows(block, rng), dst_ref=rows(block, rng),
                    send_sem=send_sems.at[self.SLOTS * k + slot], recv_sem=recv_sems.at[self.SLOTS * k + slot],
                    device_id=to, device_id_type=MESH)

            sends.append([
                copy(0, me, whole, sibling, True),
                copy(1, me, first, (*x_chip, c), True),
                copy(2, me, second, (*x_chip, c), True),
                copy(3, me, second, (*y_chip, c), True),
                copy(4, me, first, (*y_chip, c), True),
                copy(5, (*x_chip, c), first, (*y_chip, c)),
                copy(6, (*y_chip, c), second, (*x_chip, c)),
                copy(7, (*x_chip, c), whole, sibling),
                copy(8, (*y_chip, c), whole, sibling),
                copy(9, (*far_chip, c), whole, sibling)])
            lands.append([
                copy(0, sibling, whole, me),
                copy(1, (*x_chip, c), first, me), copy(2, (*x_chip, c), second, me),
                copy(3, (*y_chip, c), second, me), copy(4, (*y_chip, c), first, me),
                copy(5, (*far_chip, c), first, me), copy(6, (*far_chip, c), second, me),
                copy(7, (*x_chip, 1 - c), whole, me), copy(8, (*y_chip, 1 - c), whole, me),
                copy(9, (*far_chip, 1 - c), whole, me)])
            mine.append(pltpu.make_async_copy(own(whole), rows(me, whole), local_sems.at[k]))
        return sends, lands, mine

    @staticmethod
    def _then(lands, waits, sends, starts):
        for slot in waits:
            if lands[slot] is not None:
                lands[slot].wait_recv()
        for slot in starts:
            if sends[slot] is not None:
                sends[slot].start()

    def start(self, ins, outs, sems):
        sends, lands, mine = self._plan(ins, outs, sems)
        for cp in mine:
            cp.start()
        for slot in (1, 3, 0, 2, 4):
            for s in sends:
                self._then(None, (), s, (slot,))

    def forward(self, ins, outs, sems):
        sends, lands, _ = self._plan(ins, outs, sems)
        for s, l in zip(sends, lands):
            self._then(l, (1,), s, (5,))
            self._then(l, (3,), s, (6,))

    def middle(self, ins, outs, sems):
        sends, lands, _ = self._plan(ins, outs, sems)
        for s, l in zip(sends, lands):
            self._then(l, (2,), s, (7,))
            self._then(l, (4,), s, (8,))
        for s, l in zip(sends, lands):
            self._then(l, (5, 6), s, (9,))

    def finish(self, ins, outs, sems):
        if self.forward_at is None:
            self.forward(ins, outs, sems)
        if self.middle_at is None:
            self.middle(ins, outs, sems)
        sends, lands, mine = self._plan(ins, outs, sems)
        for s, l in zip(sends, lands):
            self._then(l, (0, 7, 8, 9), s, ())
        for s in sends:
            for cp in s:
                if cp is not None:
                    cp.wait_send()
        for cp in mine:
            cp.wait()


class _ReduceScatter:
    peers = ALL_PEERS

    def __init__(self, parts, bcast=()):
        self.parts = [(lo, cnt) for _, lo, cnt in parts]
        self.n_parts = len(parts)
        self.ins = [a for a, _, _ in parts] + list(bcast)
        self.out_shape = [_sds((N_DEV * cnt, a.shape[1]), a.dtype) for a, _, cnt in parts]
        self.out_shape += [_sds((N_DEV * b.shape[0], b.shape[1]), b.dtype) for b in bcast]
        n = len(self.ins)
        self.sems = [pltpu.SemaphoreType.DMA((7 * n,)), pltpu.SemaphoreType.DMA((7 * n,)),
                     pltpu.SemaphoreType.DMA((n,))]

    def _copies(self, ins, outs, sems):
        send_sems, recv_sems, local_sems = sems
        x, y, c = _my_place()
        me_idx = 4 * x + 2 * y + c
        remote, local = [], []
        for k in range(len(ins)):
            cnt = outs[k].shape[0] // N_DEV
            dst = outs[k].at[pl.ds(pl.multiple_of(me_idx * cnt, 8), cnt), :]
            if k < self.n_parts:
                lo, _ = self.parts[k]
                r = ins[k].shape[0] // N_DEV
                src_of = lambda idx: ins[k].at[pl.ds(pl.multiple_of(idx * r + lo, 8), cnt), :]
            else:
                src_of = lambda idx: ins[k]
            local.append(pltpu.make_async_copy(src_of(me_idx), dst, local_sems.at[k]))
            for j in range(1, N_DEV):
                peer = (x ^ (j >> 2), y ^ ((j >> 1) & 1), c ^ (j & 1))
                peer_idx = 4 * peer[0] + 2 * peer[1] + peer[2]
                remote.append(pltpu.make_async_remote_copy(
                    src_ref=src_of(peer_idx), dst_ref=dst,
                    send_sem=send_sems.at[7 * k + j - 1], recv_sem=recv_sems.at[7 * k + j - 1],
                    device_id=peer, device_id_type=MESH))
        return remote, local

    def start(self, ins, outs, sems):
        _start_exchange(*self._copies(ins, outs, sems))

    def finish(self, ins, outs, sems):
        _finish_exchange(*self._copies(ins, outs, sems))


class _ChipExchange:
    peers = CHIP_PEERS

    def __init__(self, arrays):
        self.ins = list(arrays)
        self.out_shape = [_sds(a.shape, a.dtype) for a in arrays]
        n = len(self.ins)
        self.sems = [pltpu.SemaphoreType.DMA((3 * n,)), pltpu.SemaphoreType.DMA((3 * n,)),
                     pltpu.SemaphoreType.DMA((n,))]

    def _copies(self, ins, outs, sems):
        send_sems, recv_sems, local_sems = sems
        x, y, c = _my_place()
        my_chip = 2 * x + y
        remote, local = [], []
        for k in range(len(ins)):
            r = ins[k].shape[0] // 4
            dst = outs[k].at[pl.ds(pl.multiple_of(my_chip * r, 8), r), :]
            local.append(pltpu.make_async_copy(ins[k].at[pl.ds(pl.multiple_of(my_chip * r, 8), r), :], dst,
                                               local_sems.at[k]))
            for j in range(1, 4):
                px, py = x ^ (j >> 1), y ^ (j & 1)
                src = ins[k].at[pl.ds(pl.multiple_of((2 * px + py) * r, 8), r), :]
                remote.append(pltpu.make_async_remote_copy(
                    src_ref=src, dst_ref=dst, send_sem=send_sems.at[3 * k + j - 1],
                    recv_sem=recv_sems.at[3 * k + j - 1], device_id=(px, py, c), device_id_type=MESH))
        return remote, local

    def start(self, ins, outs, sems):
        _start_exchange(*self._copies(ins, outs, sems))

    def finish(self, ins, outs, sems):
        _finish_exchange(*self._copies(ins, outs, sems))


class _ChipExchangeThenBroadcast(_ChipExchange):
    peers = ALL_PEERS
    defer_start = False

    def __init__(self, arrays, late_from, late_shapes):
        super().__init__(arrays)
        self.n_chip = len(arrays)
        self.late_from = tuple(late_from)
        self.out_shape += [_sds((N_DEV * r, c), F32) for r, c in late_shapes]
        m = len(late_shapes)
        self.sems += [pltpu.SemaphoreType.DMA((7 * m,)), pltpu.SemaphoreType.DMA((7 * m,)),
                      pltpu.SemaphoreType.DMA((m,))]

    def _late_copies(self, srcs, outs, sems):
        send_sems, recv_sems, local_sems = sems
        x, y, c = _my_place()
        me_idx = 4 * x + 2 * y + c
        remote, local = [], []
        for k, src in enumerate(srcs):
            r = src.shape[0]
            dst = outs[k].at[pl.ds(pl.multiple_of(me_idx * r, 8), r), :]
            local.append(pltpu.make_async_copy(src, dst, local_sems.at[k]))
            for j, (dx, dy, dc) in enumerate(ALL_PEERS):
                remote.append(pltpu.make_async_remote_copy(
                    src_ref=src, dst_ref=dst, send_sem=send_sems.at[7 * k + j], recv_sem=recv_sems.at[7 * k + j],
                    device_id=(x ^ dx, y ^ dy, c ^ dc), device_id_type=MESH))
        return remote, local

    def start(self, ins, outs, sems):
        _start_exchange(*self._copies(ins, outs[:self.n_chip], sems[:3]))

    def finish(self, ins, outs, sems, late_srcs):
        late = self._late_copies(late_srcs, outs[self.n_chip:], sems[3:])
        _start_exchange(*late)
        _finish_exchange(*self._copies(ins, outs[:self.n_chip], sems[:3]))
        _finish_exchange(*late)


def _pcall(body, name, grid, in_specs, out_specs, out_shape, args, scratch=(), comm=None):
    params = pltpu.CompilerParams(dimension_semantics=("arbitrary",) * len(grid), vmem_limit_bytes=VMEM_LIMIT)
    in_specs, out_specs, out_shape, scratch = list(in_specs), list(out_specs), list(out_shape), list(scratch)
    if comm is None:
        res = pl.pallas_call(body, name=name, grid=grid, in_specs=in_specs, out_specs=out_specs, out_shape=out_shape,
                             scratch_shapes=scratch, compiler_params=params)(*args)
        return list(res), []
    n_in, n_out, n_scr = len(in_specs), len(out_specs), len(scratch)
    ci, co = len(comm.ins), len(comm.out_shape)
    total = math.prod(grid)

    def carried(*refs):
        bounds = [0, n_in, n_in + ci, n_in + ci + n_out, n_in + ci + n_out + co, n_in + ci + n_out + co + n_scr]
        ins, cins, outs, couts, scr = (refs[a:b] for a, b in zip(bounds[:-1], bounds[1:]))
        sems = refs[bounds[-1]:]
        step = pl.program_id(0)
        for d in range(1, len(grid)):
            step = step * grid[d] + pl.program_id(d)

        start_step = min(1, total - 1) if getattr(comm, "defer_start", True) else 0

        @pl.when(step == 0)
        def _():
            _barrier_signal(comm.peers)

        @pl.when(step == start_step)
        def _():
            _barrier_wait(comm.peers)
            comm.start(cins, couts, sems)

        forward_at = getattr(comm, "forward_at", None)
        if forward_at is not None and int(forward_at * total) <= start_step:
            forward_at = comm.forward_at = comm.middle_at = None
        if forward_at is not None:
            @pl.when(step == int(forward_at * total))
            def _():
                comm.forward(cins, couts, sems)

        middle_at = getattr(comm, "middle_at", None)
        if middle_at is not None:
            assert forward_at is None or forward_at <= middle_at
            @pl.when(step == int(middle_at * total))
            def _():
                comm.middle(cins, couts, sems)

        body(*ins, *outs, *scr)

        @pl.when(step == total - 1)
        def _():
            late_from = getattr(comm, "late_from", None)
            if late_from is None:
                comm.finish(cins, couts, sems)
            else:
                comm.finish(cins, couts, sems, [outs[k] for k in late_from])

    params = pltpu.CompilerParams(dimension_semantics=("arbitrary",) * len(grid), vmem_limit_bytes=VMEM_LIMIT,
                                  collective_id=BARRIER_ID[comm.peers])
    res = pl.pallas_call(
        carried, name=name, grid=grid, in_specs=in_specs + [ANY] * ci, out_specs=out_specs + [ANY] * co,
        out_shape=out_shape + comm.out_shape, scratch_shapes=scratch + comm.sems, compiler_params=params,
    )(*args, *comm.ins)
    return list(res[:n_out]), list(res[n_out:])


def _exchange_only(comm, name):
    def body(*refs):
        ci, co = len(comm.ins), len(comm.out_shape)
        _barrier_signal(comm.peers)
        _barrier_wait(comm.peers)
        comm.start(refs[:ci], refs[ci:ci + co], refs[ci + co:])
        comm.finish(refs[:ci], refs[ci:ci + co], refs[ci + co:])

    params = pltpu.CompilerParams(collective_id=BARRIER_ID[comm.peers])
    return pl.pallas_call(body, name=name, out_shape=comm.out_shape, in_specs=[ANY] * len(comm.ins),
                          out_specs=[ANY] * len(comm.out_shape), scratch_shapes=comm.sems,
                          compiler_params=params)(*comm.ins)


def _norm_inproj(x, g, win_t, b_in, comm):
    s = x.shape[0]
    tm = _row_tile(s, 512)
    widths = (QKV_W, CBX_W, GATE_W)

    def body(x_ref, g_ref, w_ref, b_ref, xn_ref, qkv_ref, cbx_ref, gate_ref):
        xv = x_ref[...]
        r = lax.rsqrt(jnp.mean(xv * xv, axis=-1, keepdims=True) + NORM_EPS)
        xn = (xv * r * g_ref[...]).astype(BF16)
        xn_ref[...] = xn
        off = 0
        for o_ref, w in zip((qkv_ref, cbx_ref, gate_ref), widths):
            acc = lax.dot_general(xn, w_ref[off:off + w, :], NT, preferred_element_type=F32)
            o_ref[...] = (acc + b_ref[:, off:off + w]).astype(BF16)
            off += w

    return _pcall(
        body, "norm_inproj", (s // tm,),
        [_rows(tm, D_MODEL), _full((1, D_MODEL)), _resident((IN_W, D_MODEL)), _full((1, IN_W))],
        [_rows(tm, D_MODEL)] + [_rows(tm, w) for w in widths],
        [_sds((s, D_MODEL), BF16)] + [_sds((s, w), BF16) for w in widths],
        (x, g, win_t, b_in), comm=comm)


Q_BLOCKS = 4


def _attn_specs():
    tq = Q_BLOCKS * BLOCK
    prev = lambda n: jnp.maximum(Q_BLOCKS * n - 1, 0)
    return [pl.BlockSpec((tq, ATTN_W), lambda n: (n, 0)),
            pl.BlockSpec((BLOCK, KV_W), lambda n: (prev(n), ATTN_W // KV_W)),
            pl.BlockSpec((tq, KV_W), lambda n: (n, ATTN_W // KV_W)),
            pl.BlockSpec((BLOCK, KV_W), lambda n: (prev(n), ATTN_W // KV_W + 1)),
            pl.BlockSpec((tq, KV_W), lambda n: (n, ATTN_W // KV_W + 1))]


def _window(prev_ref, cur_ref, sub):
    if sub == 0:
        return jnp.concatenate([prev_ref[...], cur_ref[0:BLOCK, :]], axis=0)
    return cur_ref[(sub - 1) * BLOCK:(sub + 1) * BLOCK, :]


def _lower_lanes():
    return lax.broadcasted_iota(jnp.int32, (BLOCK, 128), 1) < HEAD_DIM


def _stack_heads(val, kh):
    lower = _lower_lanes()
    parts = []
    for g in range(4):
        h = kh * 4 + g
        blk = val[:, (h // 2) * 128:(h // 2 + 1) * 128]
        keep = lower if h % 2 == 0 else jnp.logical_not(lower)
        parts.append(jnp.where(keep, blk, jnp.zeros_like(blk)))
    return jnp.concatenate(parts, axis=0)


def _dup_kv(window, kh):
    t = window.astype(F32)
    rolled = pltpu.roll(t, HEAD_DIM, axis=1)
    lower = lax.broadcasted_iota(jnp.int32, t.shape, 1) < HEAD_DIM
    dup = jnp.where(lower, t, rolled) if kh == 0 else jnp.where(lower, rolled, t)
    return dup.astype(BF16)


def _attn_mask(real_prev):
    row = lax.broadcasted_iota(jnp.int32, (4 * BLOCK, 2 * BLOCK), 0)
    kj = lax.broadcasted_iota(jnp.int32, (4 * BLOCK, 2 * BLOCK), 1)
    dist = (row & (BLOCK - 1)) + BLOCK - kj
    band = jnp.logical_and(dist >= 0, dist < BLOCK)
    return jnp.logical_and(band, jnp.logical_or(kj >= BLOCK, real_prev))


def _sink_col(sinks_ref, kh):
    gi = lax.broadcasted_iota(jnp.int32, (4 * BLOCK, 1), 0) // BLOCK
    col = jnp.zeros((4 * BLOCK, 1), F32)
    for g in range(4):
        col = jnp.where(gi == g, sinks_ref[0, kh * 4 + g], col)
    return col


def _attn_fwd(qkv, sinks, comm):
    s = qkv.shape[0]
    tq = Q_BLOCKS * BLOCK

    def body(sinks_ref, q_ref, kp_ref, kc_ref, vp_ref, vc_ref, o_ref, lse_ref):
        n = pl.program_id(0)
        lower = _lower_lanes()
        lane = lax.broadcasted_iota(jnp.int32, (BLOCK, 128), 1)
        for sub in range(Q_BLOCKS):
            rows = slice(sub * BLOCK, (sub + 1) * BLOCK)
            mask = _attn_mask(n > 0 if sub == 0 else True)
            kw, vw = _window(kp_ref, kc_ref, sub), _window(vp_ref, vc_ref, sub)
            qv = q_ref[rows, :]
            lse_out = jnp.zeros((BLOCK, 128), F32)
            for kh in range(2):
                qs = _stack_heads(qv, kh)
                kd, vd = _dup_kv(kw, kh), _dup_kv(vw, kh)
                sc = lax.dot_general(qs, kd, NT, preferred_element_type=F32) * ATTN_SCALE
                sc = jnp.where(mask, sc, NEG)
                sink = _sink_col(sinks_ref, kh)
                m = jnp.maximum(jnp.max(sc, axis=1, keepdims=True), sink)
                p = jnp.exp(sc - m)
                l = jnp.sum(p, axis=1, keepdims=True) + jnp.exp(sink - m)
                o = jnp.dot(p.astype(BF16), vd, preferred_element_type=F32) / l
                lse = m + jnp.log(l)
                for pair in range(2):
                    lo = o[(2 * pair) * BLOCK:(2 * pair + 1) * BLOCK]
                    hi = o[(2 * pair + 1) * BLOCK:(2 * pair + 2) * BLOCK]
                    col = (kh * 2 + pair) * 128
                    o_ref[rows, col:col + 128] = jnp.where(lower, lo, hi).astype(BF16)
                for g in range(4):
                    lse_out = jnp.where(lane == kh * 4 + g, lse[g * BLOCK:(g + 1) * BLOCK], lse_out)
            lse_ref[rows, :] = lse_out

    return _pcall(
        body, "attn_fwd", (s // tq,),
        [pl.BlockSpec(memory_space=pltpu.SMEM)] + _attn_specs(),
        [pl.BlockSpec((tq, ATTN_W), lambda n: (n, 0)), pl.BlockSpec((tq, 128), lambda n: (n, 0))],
        [_sds((s, ATTN_W), BF16), _sds((s, 128), F32)],
        (sinks, qkv, qkv, qkv, qkv, qkv), comm=comm)


def _conv_u(cbx_ref, halo_ref, w_ref, first):
    cb = cbx_ref[:, 0:CONV_W].astype(F32)
    cc = cbx_ref[:, CONV_W:2 * CONV_W].astype(F32)
    cx = cbx_ref[:, 2 * CONV_W:3 * CONV_W].astype(F32)
    u = cc * cx
    uh = halo_ref[:, CONV_W:2 * CONV_W].astype(F32) * halo_ref[:, 2 * CONV_W:3 * CONV_W].astype(F32)
    uh = jnp.where(first, 0.0, uh)
    u1, u2 = _shifts_down(u, uh, (1, 2))
    cv = w_ref[0:1, :] * u2 + w_ref[1:2, :] * u1 + w_ref[2:3, :] * u
    return cb, cc, cx, u, cv


def _mix_fwd(x, cbx, gates, attn, conv_w, wa, wc, wout, comm):
    s = x.shape[0]
    tm = _row_tile(s)

    def body(x_ref, cbx_ref, halo_ref, gate_ref, attn_ref, cw_ref, wa_ref, wc_ref, wo_ref,
             h1_ref):
        first = pl.program_id(0) == 0
        cb, _, _, _, cv = _conv_u(cbx_ref, halo_ref, cw_ref, first)
        conv = (cb * cv).astype(BF16)
        ap = jnp.dot(attn_ref[...], wa_ref[...], preferred_element_type=F32)
        cp = jnp.dot(conv, wc_ref[...], preferred_element_type=F32)
        ga = gate_ref[:, 0:D_MODEL].astype(F32)
        gc = gate_ref[:, D_MODEL:2 * D_MODEL].astype(F32)
        merged = (_sig(ga) * ap + _sig(gc) * cp).astype(BF16)
        h1_ref[...] = x_ref[...] + jnp.dot(merged, wo_ref[...], preferred_element_type=F32)

    return _pcall(
        body, "mix_fwd", (s // tm,),
        [_rows(tm, D_MODEL), _rows(tm, CBX_W), pl.BlockSpec((HALO, CBX_W), _prev_halo_map(tm)),
         _rows(tm, GATE_W), _rows(tm, ATTN_W), _full((3, CONV_W)), _full((ATTN_W, D_MODEL)),
         _full((CONV_W, D_MODEL)), _full((D_MODEL, D_MODEL))],
        [_rows(tm, D_MODEL)], [_sds((s, D_MODEL), F32)],
        (x, cbx, cbx, gates, attn, conv_w, wa, wc, wout), comm=comm)


def _col_offsets(wup_parts):
    widths = [p.shape[1] for p in wup_parts]
    assert sum(widths) == D_MODEL
    return [(sum(widths[:k]), w) for k, w in enumerate(widths)]


def _ffn_up(h1, g, wup_parts, fcw, comm):
    s = h1.shape[0]
    tm = _row_tile(s)
    cols = _col_offsets(wup_parts)

    def body(h_ref, g_ref, *refs):
        w_refs = refs[:len(cols)]
        fcw_ref, hn_ref, pre_ref, up_ref, carry_ref = refs[len(cols):]

        @pl.when(pl.program_id(0) == 0)
        def _():
            carry_ref[...] = jnp.zeros_like(carry_ref)

        hv = h_ref[...]
        r = lax.rsqrt(jnp.mean(hv * hv, axis=-1, keepdims=True) + NORM_EPS)
        hn = (hv * r * g_ref[...]).astype(BF16)
        hn_ref[...] = hn
        for c in range(2 * D_FF // FF_CHUNK):
            sl = slice(c * FF_CHUNK, (c + 1) * FF_CHUNK)
            acc = None
            for w_ref, (off, w) in zip(w_refs, cols):
                part = lax.dot_general(hn[:, off:off + w], w_ref[sl, :], NT, preferred_element_type=F32)
                acc = part if acc is None else acc + part
            pre_ref[:, sl] = acc.astype(BF16)
            halo = carry_ref[:, sl]
            carry_ref[:, sl] = acc[tm - HALO:, :]
            u1, u2 = _shifts_down(acc, halo, (1, 2))
            w = fcw_ref[:, sl]
            up_ref[:, sl] = (w[0:1] * u2 + w[1:2] * u1 + w[2:3] * acc).astype(BF16)

    return _pcall(
        body, "ffn_up", (s // tm,),
        [_rows(tm, D_MODEL), _full((1, D_MODEL))] + [_resident((2 * D_FF, w)) for _, w in cols]
        + [_full((3, 2 * D_FF))],
        [_rows(tm, D_MODEL), _rows(tm, 2 * D_FF), _rows(tm, 2 * D_FF)],
        [_sds((s, D_MODEL), BF16), _sds((s, 2 * D_FF), BF16), _sds((s, 2 * D_FF), BF16)],
        (h1, g, *wup_parts, fcw), scratch=[pltpu.VMEM((HALO, 2 * D_FF), F32)], comm=comm)


def _ffn_down_loss(up, wdown, h1, fnorm, target):
    s = h1.shape[0]
    tm = _row_tile(s)

    def body(up_ref, wd_ref, h1_ref, fn_ref, t_ref, act_ref, dh2_ref, loss_ref, dfn_ref):
        i = pl.program_id(0)

        @pl.when(i == 0)
        def _():
            loss_ref[...] = jnp.zeros_like(loss_ref)
            dfn_ref[...] = jnp.zeros_like(dfn_ref)

        h2 = h1_ref[...]
        for c in range(D_FF // FF_CHUNK):
            gsl = slice(c * FF_CHUNK, (c + 1) * FF_CHUNK)
            vsl = slice(D_FF + c * FF_CHUNK, D_FF + (c + 1) * FF_CHUNK)
            gate = up_ref[:, gsl].astype(F32)
            val = up_ref[:, vsl].astype(F32)
            act = (gate * _sig(gate) * val).astype(BF16)
            act_ref[:, gsl] = act
            h2 = h2 + jnp.dot(act, wd_ref[gsl, :], preferred_element_type=F32)
        r = lax.rsqrt(jnp.mean(h2 * h2, axis=-1, keepdims=True) + NORM_EPS)
        yhat = h2 * r
        fn = fn_ref[...]
        diff = yhat * fn - t_ref[...]
        loss_ref[...] += 0.5 * jnp.sum(jnp.sum(diff * diff, axis=1, keepdims=True), axis=0, keepdims=True) / D_MODEL
        dy = diff * (1.0 / D_MODEL)
        dfn_ref[...] += jnp.sum(dy * yhat, axis=0, keepdims=True)
        dyh = dy * fn
        dh2_ref[...] = r * (dyh - yhat * jnp.mean(dyh * yhat, axis=-1, keepdims=True))

    return _pcall(
        body, "ffn_down_loss", (s // tm,),
        [_rows(tm, 2 * D_FF), _resident((D_FF, D_MODEL)), _rows(tm, D_MODEL), _full((1, D_MODEL)),
         _rows(tm, D_MODEL)],
        [_rows(tm, D_FF), _rows(tm, D_MODEL), _full((1, 128)), _full((1, D_MODEL))],
        [_sds((s, D_FF), BF16), _sds((s, D_MODEL), F32), _sds((1, 128), F32), _sds((1, D_MODEL), F32)],
        (up, wdown, h1, fnorm, target))[0]


def _ffn_bwd(dh2, wdown, up, up_pre, fcw, wup_parts, h1, g, comm):
    s = dh2.shape[0]
    tm = _row_tile(s)
    cols = _col_offsets(wup_parts)

    chunk = FF_GRAD_ROWS

    def dup_cols(dh, up_ref, wd_ref, c):
        gsl = slice(c * chunk, (c + 1) * chunk)
        vsl = slice(D_FF + c * chunk, D_FF + (c + 1) * chunk)
        dact = lax.dot_general(dh, wd_ref[gsl, :], NT, preferred_element_type=F32)
        gate = up_ref[:, gsl].astype(F32)
        val = up_ref[:, vsl].astype(F32)
        sg = _sig(gate)
        return dact * val * (sg * (1.0 + gate * (1.0 - sg))), dact * gate * sg

    def body(dh_ref, wd_ref, up_ref, x_ref, w_ref, *refs):
        wup_refs = refs[:len(cols)]
        h_ref, g_ref, dx_ref, dw_ref, dh1_ref, dg_ref, carry_ref = refs[len(cols):]

        @pl.when(pl.program_id(0) == 0)
        def _():
            dw_ref[...] = jnp.zeros_like(dw_ref)
            dg_ref[...] = jnp.zeros_like(dg_ref)
            carry_ref[...] = jnp.zeros_like(carry_ref)

        dh2v = dh_ref[...]
        dh = dh2v.astype(BF16)
        dhn = [jnp.zeros((tm, w), F32) for _, w in cols]
        for c in range(D_FF // chunk):
            for d, off in zip(dup_cols(dh, up_ref, wd_ref, c), (c * chunk, D_FF + c * chunk)):
                sl = slice(off, off + chunk)
                dn = carry_ref[:, sl]
                carry_ref[:, sl] = d[0:HALO, :]
                xv = x_ref[:, sl].astype(F32)
                wv = w_ref[:, sl]
                d1, d2 = _shifts_up(d, dn, (1, 2))
                dx = (wv[2:3] * d + wv[1:2] * d1 + wv[0:1] * d2).astype(BF16)
                dx_ref[:, sl] = dx
                dhn = [a + jnp.dot(dx, wup_ref[sl, :], preferred_element_type=F32)
                       for a, wup_ref in zip(dhn, wup_refs)]
                dw_ref[0:1, sl] += jnp.sum(d2 * xv, axis=0, keepdims=True)
                dw_ref[1:2, sl] += jnp.sum(d1 * xv, axis=0, keepdims=True)
                dw_ref[2:3, sl] += jnp.sum(d * xv, axis=0, keepdims=True)
        dx1, dg = _norm_bwd_tile(h_ref[...], g_ref[...], jnp.concatenate(dhn, axis=1))
        dg_ref[...] += dg
        dh1_ref[...] = dh2v + dx1

    rows = lambda c: _rows_reversed(tm, c, s // tm)
    return _pcall(
        body, "ffn_bwd", (s // tm,),
        [rows(D_MODEL), _resident((D_FF, D_MODEL)), rows(2 * D_FF), rows(2 * D_FF), _full((3, 2 * D_FF))]
        + [_resident((2 * D_FF, w)) for _, w in cols] + [rows(D_MODEL), _full((1, D_MODEL))],
        [rows(2 * D_FF), _full((3, 2 * D_FF)), rows(D_MODEL), _full((1, D_MODEL))],
        [_sds((s, 2 * D_FF), BF16), _sds((3, 2 * D_FF), F32), _sds((s, D_MODEL), F32), _sds((1, D_MODEL), F32)],
        (dh2, wdown, up, up_pre, fcw, *wup_parts, h1, g),
        scratch=[pltpu.VMEM((HALO, 2 * D_FF), F32)], comm=comm)


def _matmul_tn(a, b, tk, name, ts=1024, comm=None):
    s, ka = a.shape
    n = b.shape[1]
    ts = min(ts, s)
    steps = s // ts

    def body(a_ref, b_ref, o_ref, acc_ref):
        j = pl.program_id(1)

        @pl.when(j == 0)
        def _():
            acc_ref[...] = jnp.zeros_like(acc_ref)

        acc_ref[...] += lax.dot_general(a_ref[...].astype(BF16), b_ref[...].astype(BF16), TN,
                                        preferred_element_type=F32)

        @pl.when(j == steps - 1)
        def _():
            o_ref[...] = acc_ref[...].astype(BF16)

    outs, couts = _pcall(
        body, name, (ka // tk, steps),
        [pl.BlockSpec((ts, tk), lambda i, j: (j, i)), pl.BlockSpec((ts, n), lambda i, j: (j, 0))],
        [pl.BlockSpec((tk, n), lambda i, j: (i, 0))], [_sds((ka, n), BF16)],
        (a, b), scratch=[pltpu.VMEM((tk, n), F32)], comm=comm)
    return outs[0] if comm is None else (outs[0], couts)


def _norm_bwd_tile(xv, g, dy):
    r = lax.rsqrt(jnp.mean(xv * xv, axis=-1, keepdims=True) + NORM_EPS)
    xhat = xv * r
    dg = jnp.sum(dy * xhat, axis=0, keepdims=True)
    dyh = dy * g
    return r * (dyh - xhat * jnp.mean(dyh * xhat, axis=-1, keepdims=True)), dg


def _mix_bwd(dh1, wout, gates, attn, wa, wc, cbx, conv_w, comm):
    s = dh1.shape[0]
    tm = _row_tile(s)
    steps = s // tm

    def body(dh_ref, wo_ref, gate_ref, attn_ref, wa_ref, wc_ref, cbx_ref, halo_ref,
             cw_ref, dg_ref, dattn_ref, dcb_ref, dcc_ref, dcx_ref, dw_ref, gwo_ref, gwa_ref, gwc_ref,
             acc_o, acc_a, acc_c, carry_ref):
        i = pl.program_id(0)

        @pl.when(i == 0)
        def _():
            dw_ref[...] = jnp.zeros_like(dw_ref)
            acc_o[...] = jnp.zeros_like(acc_o)
            acc_a[...] = jnp.zeros_like(acc_a)
            acc_c[...] = jnp.zeros_like(acc_c)
            carry_ref[...] = jnp.zeros_like(carry_ref)

        cb, cc, cx, u, cv = _conv_u(cbx_ref, halo_ref, cw_ref, i == steps - 1)
        attn = attn_ref[...]
        conv = (cb * cv).astype(BF16)
        ap = jnp.dot(attn, wa_ref[...], preferred_element_type=F32)
        cp = jnp.dot(conv, wc_ref[...], preferred_element_type=F32)
        dhb = dh_ref[...].astype(BF16)
        dm = lax.dot_general(dhb, wo_ref[...], NT, preferred_element_type=F32)
        sa = _sig(gate_ref[:, 0:D_MODEL].astype(F32))
        sc = _sig(gate_ref[:, D_MODEL:2 * D_MODEL].astype(F32))
        merged = (sa * ap + sc * cp).astype(BF16)
        da = (dm * sa).astype(BF16)
        dc = (dm * sc).astype(BF16)
        dg_ref[:, 0:D_MODEL] = (dm * ap * sa * (1.0 - sa)).astype(BF16)
        dg_ref[:, D_MODEL:2 * D_MODEL] = (dm * cp * sc * (1.0 - sc)).astype(BF16)
        dattn_ref[...] = lax.dot_general(da, wa_ref[...], NT, preferred_element_type=F32).astype(BF16)
        dconv = lax.dot_general(dc, wc_ref[...], NT, preferred_element_type=F32)
        dcb_ref[...] = (dconv * cv).astype(BF16)
        d = dconv * cb
        dn = carry_ref[...]
        carry_ref[...] = d[0:HALO, :]
        d1, d2 = _shifts_up(d, dn, (1, 2))
        du = cw_ref[2:3, :] * d + cw_ref[1:2, :] * d1 + cw_ref[0:1, :] * d2
        dcc_ref[...] = (du * cx).astype(BF16)
        dcx_ref[...] = (du * cc).astype(BF16)
        dw_ref[0:1, :] += jnp.sum(d2 * u, axis=0, keepdims=True)
        dw_ref[1:2, :] += jnp.sum(d1 * u, axis=0, keepdims=True)
        dw_ref[2:3, :] += jnp.sum(d * u, axis=0, keepdims=True)
        acc_o[...] += lax.dot_general(merged, dhb, TN, preferred_element_type=F32)
        acc_a[...] += lax.dot_general(attn, da, TN, preferred_element_type=F32)
        acc_c[...] += lax.dot_general(conv, dc, TN, preferred_element_type=F32)

        @pl.when(i == steps - 1)
        def _():
            gwo_ref[...] = acc_o[...].astype(BF16)
            gwa_ref[...] = acc_a[...].astype(BF16)
            gwc_ref[...] = acc_c[...].astype(BF16)

    rows = lambda c: _rows_reversed(tm, c, steps)
    return _pcall(
        body, "mix_bwd", (steps,),
        [rows(D_MODEL), _full((D_MODEL, D_MODEL)), rows(GATE_W), rows(ATTN_W), _full((ATTN_W, D_MODEL)),
         _full((CONV_W, D_MODEL)), rows(CBX_W), pl.BlockSpec((HALO, CBX_W), _prev_halo_map_reversed(tm, steps)),
         _full((3, CONV_W))],
        [rows(GATE_W), rows(ATTN_W), rows(CONV_W), rows(CONV_W), rows(CONV_W),
         _full((3, CONV_W)), _full((D_MODEL, D_MODEL)), _full((ATTN_W, D_MODEL)), _full((CONV_W, D_MODEL))],
        [_sds((s, GATE_W), BF16), _sds((s, ATTN_W), BF16), _sds((s, CONV_W), BF16), _sds((s, CONV_W), BF16),
         _sds((s, CONV_W), BF16), _sds((3, CONV_W), F32), _sds((D_MODEL, D_MODEL), BF16),
         _sds((ATTN_W, D_MODEL), BF16), _sds((CONV_W, D_MODEL), BF16)],
        (dh1, wout, gates, attn, wa, wc, cbx, cbx, conv_w),
        scratch=[pltpu.VMEM((D_MODEL, D_MODEL), F32), pltpu.VMEM((ATTN_W, D_MODEL), F32),
                 pltpu.VMEM((CONV_W, D_MODEL), F32), pltpu.VMEM((HALO, CONV_W), F32)], comm=comm)


def _attn_bwd(qkv, sinks, attn, lse, dattn, comm):
    s = qkv.shape[0]
    tq = Q_BLOCKS * BLOCK

    def body(sinks_ref, q_ref, kp_ref, kc_ref, vp_ref, vc_ref, o_ref, lse_ref, do_ref,
             dq_ref, dk_ref, dv_ref, ds_ref):
        n = pl.program_id(0)

        @pl.when(n == 0)
        def _():
            dk_ref[...] = jnp.zeros_like(dk_ref)
            dv_ref[...] = jnp.zeros_like(dv_ref)
            ds_ref[...] = jnp.zeros_like(ds_ref)

        lower = _lower_lanes()
        lane = lax.broadcasted_iota(jnp.int32, (BLOCK, 128), 1)
        lower2 = lax.broadcasted_iota(jnp.int32, (2 * BLOCK, 128), 1) < HEAD_DIM
        lane1 = lax.broadcasted_iota(jnp.int32, (1, 128), 1)
        dsink = jnp.zeros((1, 128), F32)
        for sub in reversed(range(Q_BLOCKS)):
            rows = slice(sub * BLOCK, (sub + 1) * BLOCK)
            mask = _attn_mask(n > 0 if sub == 0 else True)
            kw, vw = _window(kp_ref, kc_ref, sub), _window(vp_ref, vc_ref, sub)
            qv, ov, dov, lsev = q_ref[rows, :], o_ref[rows, :], do_ref[rows, :], lse_ref[rows, :]
            dk_fold, dv_fold = [], []
            for kh in range(2):
                qs = _stack_heads(qv, kh)
                dos = _stack_heads(dov, kh)
                os_ = _stack_heads(ov, kh)
                kd, vd = _dup_kv(kw, kh), _dup_kv(vw, kh)
                lse = jnp.concatenate(
                    [jnp.sum(jnp.where(lane == kh * 4 + g, lsev, 0.0), axis=1, keepdims=True) for g in range(4)],
                    axis=0)
                sc = lax.dot_general(qs, kd, NT, preferred_element_type=F32) * ATTN_SCALE
                p = jnp.exp(jnp.where(mask, sc, NEG) - lse)
                dp = lax.dot_general(dos, vd, NT, preferred_element_type=F32)
                delta = jnp.sum(dos.astype(F32) * os_.astype(F32), axis=1, keepdims=True)
                dsc = (p * (dp - delta) * ATTN_SCALE).astype(BF16)
                dqs = jnp.dot(dsc, kd, preferred_element_type=F32)
                for pair in range(2):
                    lo = dqs[(2 * pair) * BLOCK:(2 * pair + 1) * BLOCK]
                    hi = dqs[(2 * pair + 1) * BLOCK:(2 * pair + 2) * BLOCK]
                    col = (kh * 2 + pair) * 128
                    dq_ref[rows, col:col + 128] = jnp.where(lower, lo, hi).astype(BF16)
                dkd = lax.dot_general(dsc, qs, TN, preferred_element_type=F32)
                dvd = lax.dot_general(p.astype(BF16), dos, TN, preferred_element_type=F32)
                dk_fold.append(dkd + pltpu.roll(dkd, HEAD_DIM, axis=1))
                dv_fold.append(dvd + pltpu.roll(dvd, HEAD_DIM, axis=1))
                psink = jnp.exp(_sink_col(sinks_ref, kh) - lse) * delta
                for g in range(4):
                    tot = jnp.sum(psink[g * BLOCK:(g + 1) * BLOCK], axis=0, keepdims=True)
                    dsink = dsink - jnp.where(lane1 == kh * 4 + g, tot, 0.0)
            dk2 = jnp.where(lower2, dk_fold[0], dk_fold[1])
            dv2 = jnp.where(lower2, dv_fold[0], dv_fold[1])
            cur = pl.ds(pl.multiple_of((Q_BLOCKS * n + sub) * BLOCK, BLOCK), BLOCK)
            dk_ref[cur, :] += dk2[BLOCK:]
            dv_ref[cur, :] += dv2[BLOCK:]
            if sub > 0:
                prev = pl.ds(pl.multiple_of((Q_BLOCKS * n + sub - 1) * BLOCK, BLOCK), BLOCK)
                dk_ref[prev, :] += dk2[:BLOCK]
                dv_ref[prev, :] += dv2[:BLOCK]
        ds_ref[...] += dsink

        @pl.when(n > 0)
        def _():
            prev = pl.ds(pl.multiple_of((Q_BLOCKS * n - 1) * BLOCK, BLOCK), BLOCK)
            dk_ref[prev, :] += dk2[:BLOCK]
            dv_ref[prev, :] += dv2[:BLOCK]

    blk = lambda w: pl.BlockSpec((tq, w), lambda n: (n, 0))
    return _pcall(
        body, "attn_bwd", (s // tq,),
        [pl.BlockSpec(memory_space=pltpu.SMEM)] + _attn_specs() + [blk(ATTN_W), blk(128), blk(ATTN_W)],
        [blk(ATTN_W), _full((s, KV_W)), _full((s, KV_W)), _full((1, 128))],
        [_sds((s, ATTN_W), BF16), _sds((s, KV_W), F32), _sds((s, KV_W), F32), _sds((1, 128), F32)],
        (sinks, qkv, qkv, qkv, qkv, qkv, attn, lse, dattn), comm=comm)


DPROJ_PIECES = (ATTN_W, KV_W, KV_W, CONV_W, CONV_W, CONV_W, GATE_W)
DPROJ_OFFSETS = tuple(sum(DPROJ_PIECES[:k]) for k in range(len(DPROJ_PIECES)))


def _grad_w_in(pieces, xn, comm):
    s = xn.shape[0]
    ts = min(1024, s)
    steps = s // ts
    rows0 = DPROJ_OFFSETS[6]

    def body(*refs):
        p_refs, b_ref, o_ref, acc_ref, stage_ref, sem = refs[:7], refs[7], refs[8], refs[9], refs[10], refs[11]
        i, j = pl.program_id(0), pl.program_id(1)

        @pl.when(j == 0)
        def _():
            acc_ref[...] = jnp.zeros_like(acc_ref)

        bv = b_ref[...]

        def flush(lo, n):
            stage_ref[0:n, :] = acc_ref[0:n, :].astype(BF16)
            cp = pltpu.make_async_copy(stage_ref.at[0:n, :], o_ref.at[lo:lo + n, :], sem)
            cp.start()
            cp.wait()

        @pl.when(i == 0)
        def _():
            for p_ref, off, w in zip(p_refs[:6], DPROJ_OFFSETS[:6], DPROJ_PIECES[:6]):
                acc_ref[off:off + w, :] += lax.dot_general(p_ref[...].astype(BF16), bv, TN,
                                                           preferred_element_type=F32)

            @pl.when(j == steps - 1)
            def _():
                flush(0, rows0)

        @pl.when(i == 1)
        def _():
            acc_ref[0:GATE_W, :] += lax.dot_general(p_refs[6][...], bv, TN, preferred_element_type=F32)

            @pl.when(j == steps - 1)
            def _():
                flush(rows0, GATE_W)

    def piece_spec(w, group):
        return pl.BlockSpec((ts, w), lambda i, j: (jnp.where(i == group, j, 0), 0))

    outs, couts = _pcall(
        body, "grad_w_in", (2, steps),
        [piece_spec(w, 0) for w in DPROJ_PIECES[:6]] + [piece_spec(GATE_W, 1),
                                                         pl.BlockSpec((ts, D_MODEL), lambda i, j: (j, 0))],
        [ANY], [_sds((IN_W, D_MODEL), BF16)], (*pieces, xn),
        scratch=[pltpu.VMEM((rows0, D_MODEL), F32), pltpu.VMEM((rows0, D_MODEL), BF16), pltpu.SemaphoreType.DMA],
        comm=comm)
    return outs[0], couts


def _inproj_bwd(pieces, win_t, x, g, dh1, comm):
    s = x.shape[0]
    tm = _row_tile(s, 512)

    def body(*refs):
        p_refs = refs[:7]
        w_ref, x_ref, g_ref, dh_ref, dx_ref, db_ref, dg_ref = refs[7:]

        @pl.when(pl.program_id(0) == 0)
        def _():
            db_ref[...] = jnp.zeros_like(db_ref)
            dg_ref[...] = jnp.zeros_like(dg_ref)

        dxn = jnp.zeros((tm, D_MODEL), F32)
        for p_ref, off, w in zip(p_refs, DPROJ_OFFSETS, DPROJ_PIECES):
            v = p_ref[...].astype(BF16)
            db_ref[:, off:off + w] += jnp.sum(v.astype(F32), axis=0, keepdims=True)
            dxn = dxn + jnp.dot(v, w_ref[off:off + w, :], preferred_element_type=F32)
        dx, dg = _norm_bwd_tile(x_ref[...], g_ref[...], dxn)
        dg_ref[...] += dg
        dx_ref[...] = dh_ref[...] + dx

    return _pcall(
        body, "inproj_bwd", (s // tm,),
        [_rows(tm, w) for w in DPROJ_PIECES] + [_resident((IN_W, D_MODEL)), _rows(tm, D_MODEL), _full((1, D_MODEL)),
                                                _rows(tm, D_MODEL)],
        [_rows(tm, D_MODEL), _full((8, IN_W)), _full((8, D_MODEL))],
        [_sds((s, D_MODEL), F32), _sds((8, IN_W), F32), _sds((8, D_MODEL), F32)],
        (*pieces, win_t, x, g, dh1), comm=comm)


def _adam_math(w, g, m, v):
    m2 = ADAM_B1 * m + (1.0 - ADAM_B1) * g
    v2 = ADAM_B2 * v + (1.0 - ADAM_B2) * (g * g)
    m_hat = m2 / (1.0 - ADAM_B1 ** ADAM_STEP)
    v_hat = v2 / (1.0 - ADAM_B2 ** ADAM_STEP)
    delta = -ADAM_LR * (m_hat / (jnp.sqrt(v_hat) + ADAM_EPS) + ADAM_WD * w)
    return delta, m2, v2


def _sum_slots(ref):
    tot = ref[0].astype(F32)
    for i in range(1, ref.shape[0]):
        tot = tot + ref[i].astype(F32)
    return tot


def _pair_sum(partials, name):
    r, c = partials.shape[0] // N_DEV, partials.shape[1]
    core = lax.axis_index("c").astype(jnp.int32).reshape(1)

    def body(core_ref, mine_ref, all_ref, o_ref, theirs_ref, send_sems, recv_sems):
        k = pl.program_id(0)
        x, y, cc = _my_place()

        def copy(chip):
            return pltpu.make_async_remote_copy(
                src_ref=all_ref.at[pl.ds(pl.multiple_of((2 * chip + 1 - cc) * r, 8), r), :],
                dst_ref=theirs_ref.at[chip], send_sem=send_sems.at[chip], recv_sem=recv_sems.at[chip],
                device_id=(x, y, 1 - cc), device_id_type=MESH)

        @pl.when(k == 0)
        def _():
            _barrier_signal(SIBLING_PEER)
            _barrier_wait(SIBLING_PEER)
            for chip in range(4):
                copy(chip).start()

        for chip in range(4):
            @pl.when(k == chip)
            def _(chip=chip):
                copy(chip).wait_recv()
                o_ref[...] = (mine_ref[...].astype(F32) + theirs_ref[chip].astype(F32)).astype(BF16)

        @pl.when(k == 3)
        def _():
            for chip in range(4):
                copy(chip).wait_send()

    grid_spec = pltpu.PrefetchScalarGridSpec(
        num_scalar_prefetch=1, grid=(4,),
        in_specs=[pl.BlockSpec((None, None, r, c), lambda k, core_ref: (k, core_ref[0], 0, 0)), ANY],
        out_specs=pl.BlockSpec((r, c), lambda k, core_ref: (k, 0)),
        scratch_shapes=[pltpu.VMEM((4, r, c), BF16), pltpu.SemaphoreType.DMA((4,)), pltpu.SemaphoreType.DMA((4,))])
    params = pltpu.CompilerParams(dimension_semantics=("arbitrary",), vmem_limit_bytes=VMEM_LIMIT,
                                  collective_id=BARRIER_ID[SIBLING_PEER])
    return pl.pallas_call(body, name=name, grid_spec=grid_spec, out_shape=_sds((4 * r, c), BF16),
                          compiler_params=params)(core, partials.reshape(4, 2, r, c), partials)


def _sum_adamw(parts, w, m, v, tr, name):
    r, c = w.shape

    def body(p_ref, w_ref, m_ref, v_ref, g_ref, d_ref, m2_ref, v2_ref):
        g = _sum_slots(p_ref)
        g_ref[...] = g
        d_ref[...], m2_ref[...], v2_ref[...] = _adam_math(w_ref[...], g, m_ref[...], v_ref[...])

    spec = pl.BlockSpec((tr, c), lambda i: (i, 0))
    return _pcall(body, name, (r // tr,), [pl.BlockSpec((N_DEV, tr, c), lambda i: (0, i, 0)), spec, spec, spec],
                  [spec] * 4, [_sds((r, c), F32)] * 4, (parts, w, m, v))[0]


def _sum_parts_adamw(parts, w, m, v, tr, name):
    c = w.shape[1]
    tiles = [p.shape[1] // tr for p in parts]
    starts = [sum(tiles[:k]) for k in range(len(parts))]
    n_parts = len(parts)

    def body(*refs):
        p_refs = refs[:n_parts]
        w_ref, m_ref, v_ref, g_ref, d_ref, m2_ref, v2_ref = refs[n_parts:]
        i = pl.program_id(0)
        for p_ref, st, nt in zip(p_refs, starts, tiles):
            @pl.when(jnp.logical_and(i >= st, i < st + nt))
            def _(p_ref=p_ref):
                g_ref[...] = _sum_slots(p_ref)

        d_ref[...], m2_ref[...], v2_ref[...] = _adam_math(w_ref[...], g_ref[...], m_ref[...], v_ref[...])

    def part_spec(p, st, nt):
        return pl.BlockSpec((p.shape[0], tr, c), lambda i: (0, jnp.clip(i - st, 0, nt - 1), 0))

    spec = pl.BlockSpec((tr, c), lambda i: (i, 0))
    return _pcall(
        body, name, (sum(tiles),),
        [part_spec(p, st, nt) for p, st, nt in zip(parts, starts, tiles)] + [spec, spec, spec],
        [spec] * 4, [_sds(w.shape, F32)] * 4, (*parts, w, m, v))[0]


ROW_MIX, ROW_FFN, ROW_FINAL, ROW_SINKS, ROW_LOSS, ROW_BIN, ROW_CW, ROW_FCW = 0, 1, 2, 3, 4, 5, 10, 13
FCW_ROWS = 6


def _wide_pieces(width):
    return [(k * D_MODEL, min(D_MODEL, width - k * D_MODEL)) for k in range(-(-width // D_MODEL))]


def _pack_small(dffn, dfn, dsink, loss, dcw, dfcw):
    def body(ffn_ref, fn_ref, sink_ref, loss_ref, cw_ref, fcw_ref, o_ref):
        o_ref[...] = jnp.zeros_like(o_ref)
        o_ref[ROW_FFN:ROW_FFN + 1, :] = ffn_ref[...]
        o_ref[ROW_FINAL:ROW_FINAL + 1, :] = fn_ref[...]
        o_ref[ROW_SINKS:ROW_SINKS + 1, 0:128] = sink_ref[...]
        o_ref[ROW_LOSS:ROW_LOSS + 1, 0:128] = loss_ref[...]
        o_ref[ROW_CW:ROW_CW + 3, 0:CONV_W] = cw_ref[...]
        for a in range(3):
            for k, (off, w) in enumerate(_wide_pieces(2 * D_FF)):
                row = ROW_FCW + FCW_ROWS * a + k
                o_ref[row:row + 1, 0:w] = fcw_ref[a:a + 1, off:off + w]

    return pl.pallas_call(body, name="pack_small", out_shape=_sds((SMALL_ROWS, D_MODEL), F32))(
        dffn, dfn, dsink, loss, dcw, dfcw)


def _small_sums_adamw(r_small, r_dmix, r_dbin, params):
    rows = (None, None, ROW_SINKS, ROW_FFN, ROW_FINAL)

    def sum_row0(ref):
        tot = ref[0:1, :]
        for i in range(1, N_DEV):
            tot = tot + ref[8 * i:8 * i + 1, :]
        return tot

    def body(*refs):
        r_ref, late_refs, p_refs, o_refs = refs[0], refs[1:3], refs[3:18], refs[18:]
        tot = _sum_slots(r_ref)
        for k, row in enumerate(rows):
            w_ref, m_ref, v_ref = p_refs[3 * k:3 * k + 3]
            g_ref, d_ref, m2_ref, v2_ref = o_refs[4 * k:4 * k + 4]
            if row is None:
                g_ref[...] = sum_row0(late_refs[k])
            else:
                for j, (off, w) in enumerate(_wide_pieces(w_ref.shape[1])):
                    g_ref[:, off:off + w] = tot[row + j:row + j + 1, 0:w]
            d_ref[...], m2_ref[...], v2_ref[...] = _adam_math(w_ref[...], g_ref[...], m_ref[...], v_ref[...])
        cw_ref, fcw_ref, loss_ref = o_refs[20:]
        cw_ref[...] = tot[ROW_CW:ROW_CW + 3, 0:CONV_W]
        for a in range(3):
            for j, (off, w) in enumerate(_wide_pieces(2 * D_FF)):
                row = ROW_FCW + FCW_ROWS * a + j
                fcw_ref[a:a + 1, off:off + w] = tot[row:row + 1, 0:w]
        loss_ref[...] = tot[ROW_LOSS:ROW_LOSS + 1, 0:128]

    flat = [t for p in params for t in p]
    out_shape = [_sds(p[0].shape, F32) for p in params for _ in range(4)]
    out_shape += [_sds((3, CONV_W), F32), _sds((3, 2 * D_FF), F32), _sds((1, 128), F32)]
    res = pl.pallas_call(body, name="small_sums_adamw", out_shape=out_shape)(r_small, r_dmix, r_dbin, *flat)
    return [tuple(res[4 * k:4 * k + 4]) for k in range(5)], res[20], res[21], res[22]


def _adamw_pair(a, b):
    def body(*refs):
        for k in range(2):
            w_ref, g_ref, m_ref, v_ref = refs[4 * k:4 * k + 4]
            d_ref, m2_ref, v2_ref = refs[8 + 3 * k:8 + 3 * k + 3]
            d_ref[...], m2_ref[...], v2_ref[...] = _adam_math(w_ref[...], g_ref[...], m_ref[...], v_ref[...])

    out_shape = [_sds(a[0].shape, F32)] * 3 + [_sds(b[0].shape, F32)] * 3
    res = pl.pallas_call(body, name="adamw_conv_weights", out_shape=out_shape)(*a, *b)
    return tuple(res[:3]), tuple(res[3:])


def _pad_cols(a, c):
    return jnp.pad(a, ((0, 0), (0, c - a.shape[1])))


def _to_col_slabs(g):
    r = g.shape[0]
    return jnp.transpose(g.reshape(r, N_DEV, 128), (1, 0, 2)).reshape(N_DEV * r, 128)


def _from_col_slabs(t):
    r = t.shape[0] // N_DEV
    return jnp.transpose(t.reshape(N_DEV, r, 128), (1, 0, 2)).reshape(r, N_DEV * 128)


def _slots(t):
    return t.reshape(N_DEV, t.shape[0] // N_DEV, t.shape[1])


def kernel(x, mix_norm, w_in, b_in, sinks, conv_w, w_attn_branch, w_conv_branch, w_out, ffn_norm, w_up, ffn_conv_w, w_down, final_norm, loss_target, m_mix_norm, m_w_in, m_b_in, m_sinks, m_conv_w, m_w_attn_branch, m_w_conv_branch, m_w_out, m_ffn_norm, m_w_up, m_ffn_conv_w, m_w_down, m_final_norm, v_mix_norm, v_w_in, v_b_in, v_sinks, v_conv_w, v_w_attn_branch, v_w_conv_branch, v_w_out, v_ffn_norm, v_w_up, v_ffn_conv_w, v_w_down, v_final_norm):
    xs, tgt = x[0], loss_target[0]
    me = 4 * lax.axis_index("x") + 2 * lax.axis_index("y") + lax.axis_index("c")
    in_rows, up_rows = IN_W // N_DEV, 2 * D_FF // N_DEV

    conv_sh = jnp.concatenate([_pad_cols(ffn_conv_w[0], 768), _pad_cols(conv_w[0], 768),
                               jnp.zeros((2, 768), F32)], axis=0)
    win_sh, wup_sh = w_in[0].T.astype(BF16), w_up[0].T.astype(BF16)
    wout_sh, wdown_sh = w_out[0].astype(BF16), w_down[0].astype(BF16)
    wa_sh, wc_sh = w_attn_branch[0].astype(BF16), w_conv_branch[0].astype(BF16)

    quarter, half = D_MODEL // 4, D_MODEL // 2
    phases = dict(forward_at=0.375, pass_on_at=0.875)
    (win_t,) = _exchange_only(_AllGather([win_sh]), "gather_w_in")
    (xn, qkv, cbx, gates), (wa_s, wc_s, conv_g, wup_a) = _norm_inproj(
        xs, mix_norm, win_t, b_in, _AllGather([wa_sh, wc_sh, conv_sh, (wup_sh, 0, quarter)], **phases))
    (attn, lse), (wup_b, wout) = _attn_fwd(qkv, sinks,
                                           _AllGather([(wup_sh, quarter, quarter), wout_sh], **phases))
    wa, wc = _from_col_slabs(wa_s), _from_col_slabs(wc_s)
    conv_g = conv_g.reshape(N_DEV, 8, 768)
    fcw = jnp.transpose(conv_g[:, 0:3, :up_rows], (1, 0, 2)).reshape(3, 2 * D_FF)
    cw = jnp.transpose(conv_g[:, 3:6, :CONV_W // N_DEV], (1, 0, 2)).reshape(3, CONV_W)
    (h1,), (wup_c,) = _mix_fwd(xs, cbx, gates, attn, cw, wa, wc, wout,
                               _AllGather([(wup_sh, half, half)], **phases))
    wup_parts = (wup_a, wup_b, wup_c)
    (hn, up_pre, up), (wdown,) = _ffn_up(h1, ffn_norm, wup_parts, fcw,
                                         _AllGather([wdown_sh], forward_at=0.25, pass_on_at=0.75))
    act, dh2, loss_p, dfn_p = _ffn_down_loss(up, wdown, h1, final_norm.reshape(1, D_MODEL), tgt)

    dn_rows = D_FF // N_DEV
    g_wdown = _matmul_tn(act, dh2, FF_GRAD_ROWS, "grad_w_down")
    (dup_pre, dfcw_p, dh1, dffn_p), (r_wdown,) = _ffn_bwd(dh2, wdown, up, up_pre, fcw, wup_parts, h1, ffn_norm,
                                                         _ReduceScatter([(g_wdown, 0, dn_rows)]))
    g_wup_t = _matmul_tn(dup_pre, hn, FF_GRAD_ROWS, "grad_w_up")
    q_wup = _pair_sum(g_wup_t, "pair_sum_w_up")
    (dgates, dattn, dcb, dcc, dcx, dcw_p, g_wout, g_wa_nat, g_wc_nat), (r_wup,) = _mix_bwd(
        dh1, wout, gates, attn, wa, wc, cbx, cw, _ChipExchange([q_wup]))
    g_wa, g_wc = _to_col_slabs(g_wa_nat), _to_col_slabs(g_wc_nat)
    (dq, dk, dv, dsink_p), (r_wout,) = _attn_bwd(
        qkv, sinks, attn, lse, dattn, _ReduceScatter([(g_wout, 0, D_MODEL // N_DEV)]))
    dproj = (dq, dk, dv, dcb, dcc, dcx, dgates)
    small = _pack_small(dffn_p, dfn_p, dsink_p, loss_p, dcw_p, dfcw_p)
    g_win_t, (r_wa, r_wc, r_small) = _grad_w_in(
        dproj, xn, _ReduceScatter([(g_wa, 0, ATTN_W), (g_wc, 0, CONV_W)], [small]))
    q_win = _pair_sum(g_win_t, "pair_sum_w_in")
    (dx, _, _), (r_win, r_dbin, r_dmix) = _inproj_bwd(
        dproj, win_t, xs, mix_norm, dh1,
        _ChipExchangeThenBroadcast([q_win], late_from=(1, 2), late_shapes=[(8, IN_W), (8, D_MODEL)]))

    fn2, m_fn2, v_fn2 = (t.reshape(1, D_MODEL) for t in (final_norm, m_final_norm, v_final_norm))
    small_res, g_cw_full, g_fcw_full, loss_row = _small_sums_adamw(
        _slots(r_small), r_dmix, r_dbin,
        [(mix_norm, m_mix_norm, v_mix_norm), (b_in, m_b_in, v_b_in), (sinks, m_sinks, v_sinks),
         (ffn_norm, m_ffn_norm, v_ffn_norm), (fn2, m_fn2, v_fn2)])
    loss = loss_row[0, 0]
    g_cw = lax.dynamic_slice_in_dim(g_cw_full, me * (CONV_W // N_DEV), CONV_W // N_DEV, axis=1)
    g_fcw = lax.dynamic_slice_in_dim(g_fcw_full, me * up_rows, up_rows, axis=1)
    taps = lambda t: jnp.transpose(t, (1, 0, 2))
    g_cw, g_fcw = g_cw[:, None, :], g_fcw[:, None, :]
    cw_res, fcw_res = _adamw_pair((taps(conv_w), g_cw, taps(m_conv_w), taps(v_conv_w)),
                                  (taps(ffn_conv_w), g_fcw, taps(m_ffn_conv_w), taps(v_ffn_conv_w)))

    big = {}
    big["w_in"] = tuple(t.T for t in _sum_parts_adamw(
        [r_win.reshape(4, in_rows, D_MODEL)], w_in[0].T, m_w_in[0].T, v_w_in[0].T, in_rows // 2, "adamw_w_in"))
    big["w_up"] = tuple(t.T for t in _sum_parts_adamw(
        [r_wup.reshape(4, up_rows, D_MODEL)], w_up[0].T, m_w_up[0].T, v_w_up[0].T, up_rows // 4, "adamw_w_up"))
    big["w_out"] = _sum_adamw(_slots(r_wout), w_out[0], m_w_out[0], v_w_out[0], 128, "adamw_w_out")
    big["w_down"] = _sum_adamw(_slots(r_wdown), w_down[0], m_w_down[0], v_w_down[0], dn_rows // 2, "adamw_w_down")
    big["w_attn_branch"] = _sum_adamw(_slots(r_wa), w_attn_branch[0], m_w_attn_branch[0], v_w_attn_branch[0], 256,
                                      "adamw_w_attn_branch")
    big["w_conv_branch"] = _sum_adamw(_slots(r_wc), w_conv_branch[0], m_w_conv_branch[0], v_w_conv_branch[0], 256,
                                      "adamw_w_conv_branch")

    res = dict(zip(("mix_norm", "b_in", "sinks", "ffn_norm"), small_res[:4]))
    res["final_norm"] = tuple(t.reshape(final_norm.shape) for t in small_res[4])
    res["conv_w"] = tuple(jnp.transpose(t, (1, 0, 2)) for t in (g_cw,) + cw_res)
    res["ffn_conv_w"] = tuple(jnp.transpose(t, (1, 0, 2)) for t in (g_fcw,) + fcw_res)
    for name, ref_w in (("w_in", w_in), ("w_up", w_up), ("w_out", w_out), ("w_down", w_down),
                        ("w_attn_branch", w_attn_branch), ("w_conv_branch", w_conv_branch)):
        res[name] = tuple(t.reshape(ref_w.shape) for t in big[name])

    order = ["mix_norm", "w_in", "b_in", "sinks", "conv_w", "w_attn_branch", "w_conv_branch", "w_out",
             "ffn_norm", "w_up", "ffn_conv_w", "w_down", "final_norm"]
    out = [loss, dx.reshape(x.shape)]
    for k in range(4):
        out += [res[name][k] for name in order]
    return tuple(out)
```

```python
import math

import jax
import jax.numpy as jnp
from jax import lax
from jax.experimental import pallas as pl
from jax.experimental.pallas import tpu as pltpu

F32 = jnp.float32
BF16 = jnp.bfloat16
MESH = pl.DeviceIdType.MESH
N_DEV = 8

D_MODEL = 1024
HEAD_DIM = 64
N_HEADS = 8
BLOCK = 128
ATTN_W = 512
KV_W = 128
CONV_W = 512
QKV_W = ATTN_W + 2 * KV_W
CBX_W = 3 * CONV_W
GATE_W = 2 * D_MODEL
IN_W = QKV_W + CBX_W + GATE_W
D_FF = 2816
FF_CHUNK = 256
FF_GRAD_ROWS = 1408
NORM_EPS = 1e-5
ATTN_SCALE = HEAD_DIM ** -0.5
NEG = -1e30
HALO = 16

ADAM_LR = 0.001
ADAM_B1 = 0.9
ADAM_B2 = 0.999
ADAM_EPS = 1e-08
ADAM_WD = 0.01
ADAM_STEP = 10

VMEM_LIMIT = 56 * 1024 * 1024
SMALL_ROWS = 32

NT = (((1,), (1,)), ((), ()))
TN = (((0,), (0,)), ((), ()))
ANY = pl.BlockSpec(memory_space=pl.ANY)


def _sig(v):
    return 1.0 / (1.0 + jnp.exp(-v))


def _row_tile(s, pref=256):
    return pref if s % pref == 0 else s


def _shifts_down(u, halo, ks):
    ext = jnp.concatenate([halo, u], axis=0)
    return [pltpu.roll(ext, k, axis=0)[HALO:, :] for k in ks]


def _shifts_up(u, halo, ks):
    n = u.shape[0]
    ext = jnp.concatenate([u, halo], axis=0)
    return [pltpu.roll(ext, n + HALO - k, axis=0)[:n, :] for k in ks]


def _rows_reversed(tm, c, steps):
    return pl.BlockSpec((tm, c), lambda i: (steps - 1 - i, 0))


def _prev_halo_map_reversed(tm, steps):
    return lambda i: (jnp.maximum((steps - 1 - i) * (tm // HALO) - 1, 0), 0)


def _prev_halo_map(tm):
    return lambda i: (jnp.maximum(i * (tm // HALO) - 1, 0), 0)


def _full(shape):
    return pl.BlockSpec(shape, lambda *_: (0,) * len(shape))


def _resident(shape):
    return pl.BlockSpec(shape, lambda *_: (0,) * len(shape), pipeline_mode=pl.Buffered(1))


def _rows(tm, c):
    return pl.BlockSpec((tm, c), lambda i: (i, 0))


def _sds(shape, dtype):
    return jax.ShapeDtypeStruct(shape, dtype)


def _my_place():
    x, y, c = lax.axis_index("x"), lax.axis_index("y"), lax.axis_index("c")
    return x, y, c


ALL_PEERS = tuple((j >> 2, (j >> 1) & 1, j & 1) for j in range(1, N_DEV))
SIBLING_PEER = ((0, 0, 1),)
CHIP_PEERS = ((0, 1, 0), (1, 0, 0), (1, 1, 0))
BARRIER_ID = {ALL_PEERS: 0, SIBLING_PEER: 1, CHIP_PEERS: 2}


def _barrier_signal(peers):
    x, y, c = _my_place()
    barrier = pltpu.get_barrier_semaphore()
    for dx, dy, dc in peers:
        pl.semaphore_signal(barrier, inc=1, device_id=(x ^ dx, y ^ dy, c ^ dc), device_id_type=MESH)


def _barrier_wait(peers):
    pl.semaphore_wait(pltpu.get_barrier_semaphore(), len(peers))


def _start_exchange(remote, local):
    for cp in local + remote:
        cp.start()


def _finish_exchange(remote, local):
    for cp in remote:
        cp.wait_recv()
    for cp in remote:
        cp.wait_send()
    for cp in local:
        cp.wait()


class _AllGather:
    peers = ALL_PEERS
    SLOTS = 10

    def __init__(self, shards, pass_on_at=None, forward_at=None):
        self.ins = [s[0] if isinstance(s, tuple) else s for s in shards]
        self.cols = [s[1:] if isinstance(s, tuple) else None for s in shards]
        self.middle_at, self.forward_at = pass_on_at, forward_at
        assert pass_on_at is None or forward_at is not None
        n = len(shards)
        self.out_shape = [_sds((N_DEV * s.shape[0], s.shape[1] if c is None else c[1]), s.dtype)
                          for s, c in zip(self.ins, self.cols)]
        self.sems = [pltpu.SemaphoreType.DMA((self.SLOTS * n,)), pltpu.SemaphoreType.DMA((self.SLOTS * n,)),
                     pltpu.SemaphoreType.DMA((n,))]

    def _plan(self, ins, outs, sems):
        send_sems, recv_sems, local_sems = sems
        x, y, c = _my_place()
        me, sibling = (x, y, c), (x, y, 1 - c)
        x_chip, y_chip, far_chip = (1 - x, y), (x, 1 - y), (1 - x, 1 - y)
        sends, lands, mine = [], [], []
        for k in range(len(ins)):
            r = ins[k].shape[0]
            h = (r // 2) // 16 * 16
            whole, first, second = (0, r), (0, h), (h, r - h)

            def rows(dev, rng, k=k, r=r):
                start = pl.multiple_of((4 * dev[0] + 2 * dev[1] + dev[2]) * r + rng[0], 8)
                return outs[k].at[pl.ds(start, rng[1]), :]

            def own(rng, k=k):
                cols = self.cols[k]
                if cols is None:
                    return ins[k].at[pl.ds(rng[0], rng[1]), :]
                return ins[k].at[pl.ds(rng[0], rng[1]), pl.ds(cols[0], cols[1])]

            def copy(slot, block, rng, to, mine_src=False, k=k, rows=rows, own=own):
                if rng[1] == 0:
                    return None
                return pltpu.make_async_remote_copy(
                    src_ref=own(rng) if mine_src else rows(block, rng), dst_ref=rows(block, rng),
                    send_sem=send_sems.at[self.SLOTS * k + slot], recv_sem=recv_sems.at[self.SLOTS * k + slot],
                    device_id=to, device_id_type=MESH)

            sends.append([
                copy(0, me, whole, sibling, True),
                copy(1, me, first, (*x_chip, c), True),
                copy(2, me, second, (*x_chip, c), True),
                copy(3, me, second, (*y_chip, c), True),
                copy(4, me, first, (*y_chip, c), True),
                copy(5, (*x_chip, c), first, (*y_chip, c)),
                copy(6, (*y_chip, c), second, (*x_chip, c)),
                copy(7, (*x_chip, c), whole, sibling),
                copy(8, (*y_chip, c), whole, sibling),
                copy(9, (*far_chip, c), whole, sibling)])
            lands.append([
                copy(0, sibling, whole, me),
                copy(1, (*x_chip, c), first, me), copy(2, (*x_chip, c), second, me),
                copy(3, (*y_chip, c), second, me), copy(4, (*y_chip, c), first, me),
                copy(5, (*far_chip, c), first, me), copy(6, (*far_chip, c), second, me),
                copy(7, (*x_chip, 1 - c), whole, me), copy(8, (*y_chip, 1 - c), whole, me),
                copy(9, (*far_chip, 1 - c), whole, me)])
            mine.append(pltpu.make_async_copy(own(whole), rows(me, whole), local_sems.at[k]))
        return sends, lands, mine

    @staticmethod
    def _then(lands, waits, sends, starts):
        for slot in waits:
            if lands[slot] is not None:
                lands[slot].wait_recv()
        for slot in starts:
            if sends[slot] is not None:
                sends[slot].start()

    def start(self, ins, outs, sems):
        sends, lands, mine = self._plan(ins, outs, sems)
        for cp in mine:
            cp.start()
        for slot in (1, 3, 0, 2, 4):
            for s in sends:
                self._then(None, (), s, (slot,))

    def forward(self, ins, outs, sems):
        sends, lands, _ = self._plan(ins, outs, sems)
        for s, l in zip(sends, lands):
            self._then(l, (1,), s, (5,))
            self._then(l, (3,), s, (6,))

    def middle(self, ins, outs, sems):
        sends, lands, _ = self._plan(ins, outs, sems)
        for s, l in zip(sends, lands):
            self._then(l, (2,), s, (7,))
            self._then(l, (4,), s, (8,))
        for s, l in zip(sends, lands):
            self._then(l, (5, 6), s, (9,))

    def finish(self, ins, outs, sems):
        if self.forward_at is None:
            self.forward(ins, outs, sems)
        if self.middle_at is None:
            self.middle(ins, outs, sems)
        sends, lands, mine = self._plan(ins, outs, sems)
        for s, l in zip(sends, lands):
            self._then(l, (0, 7, 8, 9), s, ())
        for s in sends:
            for cp in s:
                if cp is not None:
                    cp.wait_send()
        for cp in mine:
            cp.wait()


class _ReduceScatter:
    peers = ALL_PEERS

    def __init__(self, parts, bcast=()):
        self.parts = [(lo, cnt) for _, lo, cnt in parts]
        self.n_parts = len(parts)
        self.ins = [a for a, _, _ in parts] + list(bcast)
        self.out_shape = [_sds((N_DEV * cnt, a.shape[1]), a.dtype) for a, _, cnt in parts]
        self.out_shape += [_sds((N_DEV * b.shape[0], b.shape[1]), b.dtype) for b in bcast]
        n = len(self.ins)
        self.sems = [pltpu.SemaphoreType.DMA((7 * n,)), pltpu.SemaphoreType.DMA((7 * n,)),
                     pltpu.SemaphoreType.DMA((n,))]

    def _copies(self, ins, outs, sems):
        send_sems, recv_sems, local_sems = sems
        x, y, c = _my_place()
        me_idx = 4 * x + 2 * y + c
        remote, local = [], []
        for k in range(len(ins)):
            cnt = outs[k].shape[0] // N_DEV
            dst = outs[k].at[pl.ds(pl.multiple_of(me_idx * cnt, 8), cnt), :]
            if k < self.n_parts:
                lo, _ = self.parts[k]
                r = ins[k].shape[0] // N_DEV
                src_of = lambda idx: ins[k].at[pl.ds(pl.multiple_of(idx * r + lo, 8), cnt), :]
            else:
                src_of = lambda idx: ins[k]
            local.append(pltpu.make_async_copy(src_of(me_idx), dst, local_sems.at[k]))
            for j in range(1, N_DEV):
                peer = (x ^ (j >> 2), y ^ ((j >> 1) & 1), c ^ (j & 1))
                peer_idx = 4 * peer[0] + 2 * peer[1] + peer[2]
                remote.append(pltpu.make_async_remote_copy(
                    src_ref=src_of(peer_idx), dst_ref=dst,
                    send_sem=send_sems.at[7 * k + j - 1], recv_sem=recv_sems.at[7 * k + j - 1],
                    device_id=peer, device_id_type=MESH))
        return remote, local

    def start(self, ins, outs, sems):
        _start_exchange(*self._copies(ins, outs, sems))

    def finish(self, ins, outs, sems):
        _finish_exchange(*self._copies(ins, outs, sems))


class _ChipExchange:
    peers = CHIP_PEERS

    def __init__(self, arrays):
        self.ins = list(arrays)
        self.out_shape = [_sds(a.shape, a.dtype) for a in arrays]
        n = len(self.ins)
        self.sems = [pltpu.SemaphoreType.DMA((3 * n,)), pltpu.SemaphoreType.DMA((3 * n,)),
                     pltpu.SemaphoreType.DMA((n,))]

    def _copies(self, ins, outs, sems):
        send_sems, recv_sems, local_sems = sems
        x, y, c = _my_place()
        my_chip = 2 * x + y
        remote, local = [], []
        for k in range(len(ins)):
            r = ins[k].shape[0] // 4
            dst = outs[k].at[pl.ds(pl.multiple_of(my_chip * r, 8), r), :]
            local.append(pltpu.make_async_copy(ins[k].at[pl.ds(pl.multiple_of(my_chip * r, 8), r), :], dst,
                                               local_sems.at[k]))
            for j in range(1, 4):
                px, py = x ^ (j >> 1), y ^ (j & 1)
                src = ins[k].at[pl.ds(pl.multiple_of((2 * px + py) * r, 8), r), :]
                remote.append(pltpu.make_async_remote_copy(
                    src_ref=src, dst_ref=dst, send_sem=send_sems.at[3 * k + j - 1],
                    recv_sem=recv_sems.at[3 * k + j - 1], device_id=(px, py, c), device_id_type=MESH))
        return remote, local

    def start(self, ins, outs, sems):
        _start_exchange(*self._copies(ins, outs, sems))

    def finish(self, ins, outs, sems):
        _finish_exchange(*self._copies(ins, outs, sems))


class _ChipExchangeThenBroadcast(_ChipExchange):
    peers = ALL_PEERS
    defer_start = False

    def __init__(self, arrays, late_from, late_shapes):
        super().__init__(arrays)
        self.n_chip = len(arrays)
        self.late_from = tuple(late_from)
        self.out_shape += [_sds((N_DEV * r, c), F32) for r, c in late_shapes]
        m = len(late_shapes)
        self.sems += [pltpu.SemaphoreType.DMA((7 * m,)), pltpu.SemaphoreType.DMA((7 * m,)),
                      pltpu.SemaphoreType.DMA((m,))]

    def _late_copies(self, srcs, outs, sems):
        send_sems, recv_sems, local_sems = sems
        x, y, c = _my_place()
        me_idx = 4 * x + 2 * y + c
        remote, local = [], []
        for k, src in enumerate(srcs):
            r = src.shape[0]
            dst = outs[k].at[pl.ds(pl.multiple_of(me_idx * r, 8), r), :]
            local.append(pltpu.make_async_copy(src, dst, local_sems.at[k]))
            for j, (dx, dy, dc) in enumerate(ALL_PEERS):
                remote.append(pltpu.make_async_remote_copy(
                    src_ref=src, dst_ref=dst, send_sem=send_sems.at[7 * k + j], recv_sem=recv_sems.at[7 * k + j],
                    device_id=(x ^ dx, y ^ dy, c ^ dc), device_id_type=MESH))
        return remote, local

    def start(self, ins, outs, sems):
        _start_exchange(*self._copies(ins, outs[:self.n_chip], sems[:3]))

    def finish(self, ins, outs, sems, late_srcs):
        late = self._late_copies(late_srcs, outs[self.n_chip:], sems[3:])
        _start_exchange(*late)
        _finish_exchange(*self._copies(ins, outs[:self.n_chip], sems[:3]))
        _finish_exchange(*late)


def _pcall(body, name, grid, in_specs, out_specs, out_shape, args, scratch=(), comm=None):
    params = pltpu.CompilerParams(dimension_semantics=("arbitrary",) * len(grid), vmem_limit_bytes=VMEM_LIMIT)
    in_specs, out_specs, out_shape, scratch = list(in_specs), list(out_specs), list(out_shape), list(scratch)
    if comm is None:
        res = pl.pallas_call(body, name=name, grid=grid, in_specs=in_specs, out_specs=out_specs, out_shape=out_shape,
                             scratch_shapes=scratch, compiler_params=params)(*args)
        return list(res), []
    n_in, n_out, n_scr = len(in_specs), len(out_specs), len(scratch)
    ci, co = len(comm.ins), len(comm.out_shape)
    total = math.prod(grid)

    def carried(*refs):
        bounds = [0, n_in, n_in + ci, n_in + ci + n_out, n_in + ci + n_out + co, n_in + ci + n_out + co + n_scr]
        ins, cins, outs, couts, scr = (refs[a:b] for a, b in zip(bounds[:-1], bounds[1:]))
        sems = refs[bounds[-1]:]
        step = pl.program_id(0)
        for d in range(1, len(grid)):
            step = step * grid[d] + pl.program_id(d)

        start_step = min(1, total - 1) if getattr(comm, "defer_start", True) else 0

        @pl.when(step == 0)
        def _():
            _barrier_signal(comm.peers)

        @pl.when(step == start_step)
        def _():
            _barrier_wait(comm.peers)
            comm.start(cins, couts, sems)

        forward_at = getattr(comm, "forward_at", None)
        if forward_at is not None and int(forward_at * total) <= start_step:
            forward_at = comm.forward_at = comm.middle_at = None
        if forward_at is not None:
            @pl.when(step == int(forward_at * total))
            def _():
                comm.forward(cins, couts, sems)

        middle_at = getattr(comm, "middle_at", None)
        if middle_at is not None:
            assert forward_at is None or forward_at <= middle_at
            @pl.when(step == int(middle_at * total))
            def _():
                comm.middle(cins, couts, sems)

        body(*ins, *outs, *scr)

        @pl.when(step == total - 1)
        def _():
            late_from = getattr(comm, "late_from", None)
            if late_from is None:
                comm.finish(cins, couts, sems)
            else:
                comm.finish(cins, couts, sems, [outs[k] for k in late_from])

    params = pltpu.CompilerParams(dimension_semantics=("arbitrary",) * len(grid), vmem_limit_bytes=VMEM_LIMIT,
                                  collective_id=BARRIER_ID[comm.peers])
    res = pl.pallas_call(
        carried, name=name, grid=grid, in_specs=in_specs + [ANY] * ci, out_specs=out_specs + [ANY] * co,
        out_shape=out_shape + comm.out_shape, scratch_shapes=scratch + comm.sems, compiler_params=params,
    )(*args, *comm.ins)
    return list(res[:n_out]), list(res[n_out:])


def _exchange_only(comm, name):
    def body(*refs):
        ci, co = len(comm.ins), len(comm.out_shape)
        _barrier_signal(comm.peers)
        _barrier_wait(comm.peers)
        comm.start(refs[:ci], refs[ci:ci + co], refs[ci + co:])
        comm.finish(refs[:ci], refs[ci:ci + co], refs[ci + co:])

    params = pltpu.CompilerParams(collective_id=BARRIER_ID[comm.peers])
    return pl.pallas_call(body, name=name, out_shape=comm.out_shape, in_specs=[ANY] * len(comm.ins),
                          out_specs=[ANY] * len(comm.out_shape), scratch_shapes=comm.sems,
                          compiler_params=params)(*comm.ins)


def _norm_inproj(x, g, win_t, b_in, comm):
    s = x.shape[0]
    tm = _row_tile(s, 512)
    widths = (QKV_W, CBX_W, GATE_W)

    def body(x_ref, g_ref, w_ref, b_ref, xn_ref, qkv_ref, cbx_ref, gate_ref):
        xv = x_ref[...]
        r = lax.rsqrt(jnp.mean(xv * xv, axis=-1, keepdims=True) + NORM_EPS)
        xn = (xv * r * g_ref[...]).astype(BF16)
        xn_ref[...] = xn
        off = 0
        for o_ref, w in zip((qkv_ref, cbx_ref, gate_ref), widths):
            acc = lax.dot_general(xn, w_ref[off:off + w, :], NT, preferred_element_type=F32)
            o_ref[...] = (acc + b_ref[:, off:off + w]).astype(BF16)
            off += w

    return _pcall(
        body, "norm_inproj", (s // tm,),
        [_rows(tm, D_MODEL), _full((1, D_MODEL)), _resident((IN_W, D_MODEL)), _full((1, IN_W))],
        [_rows(tm, D_MODEL)] + [_rows(tm, w) for w in widths],
        [_sds((s, D_MODEL), BF16)] + [_sds((s, w), BF16) for w in widths],
        (x, g, win_t, b_in), comm=comm)


Q_BLOCKS = 4


def _attn_specs():
    tq = Q_BLOCKS * BLOCK
    prev = lambda n: jnp.maximum(Q_BLOCKS * n - 1, 0)
    return [pl.BlockSpec((tq, ATTN_W), lambda n: (n, 0)),
            pl.BlockSpec((BLOCK, KV_W), lambda n: (prev(n), ATTN_W // KV_W)),
            pl.BlockSpec((tq, KV_W), lambda n: (n, ATTN_W // KV_W)),
            pl.BlockSpec((BLOCK, KV_W), lambda n: (prev(n), ATTN_W // KV_W + 1)),
            pl.BlockSpec((tq, KV_W), lambda n: (n, ATTN_W // KV_W + 1))]


def _window(prev_ref, cur_ref, sub):
    if sub == 0:
        return jnp.concatenate([prev_ref[...], cur_ref[0:BLOCK, :]], axis=0)
    return cur_ref[(sub - 1) * BLOCK:(sub + 1) * BLOCK, :]


def _lower_lanes():
    return lax.broadcasted_iota(jnp.int32, (BLOCK, 128), 1) < HEAD_DIM


def _stack_heads(val, kh):
    lower = _lower_lanes()
    parts = []
    for g in range(4):
        h = kh * 4 + g
        blk = val[:, (h // 2) * 128:(h // 2 + 1) * 128]
        keep = lower if h % 2 == 0 else jnp.logical_not(lower)
        parts.append(jnp.where(keep, blk, jnp.zeros_like(blk)))
    return jnp.concatenate(parts, axis=0)


def _dup_kv(window, kh):
    t = window.astype(F32)
    rolled = pltpu.roll(t, HEAD_DIM, axis=1)
    lower = lax.broadcasted_iota(jnp.int32, t.shape, 1) < HEAD_DIM
    dup = jnp.where(lower, t, rolled) if kh == 0 else jnp.where(lower, rolled, t)
    return dup.astype(BF16)


def _attn_mask(real_prev):
    row = lax.broadcasted_iota(jnp.int32, (4 * BLOCK, 2 * BLOCK), 0)
    kj = lax.broadcasted_iota(jnp.int32, (4 * BLOCK, 2 * BLOCK), 1)
    dist = (row & (BLOCK - 1)) + BLOCK - kj
    band = jnp.logical_and(dist >= 0, dist < BLOCK)
    return jnp.logical_and(band, jnp.logical_or(kj >= BLOCK, real_prev))


def _sink_col(sinks_ref, kh):
    gi = lax.broadcasted_iota(jnp.int32, (4 * BLOCK, 1), 0) // BLOCK
    col = jnp.zeros((4 * BLOCK, 1), F32)
    for g in range(4):
        col = jnp.where(gi == g, sinks_ref[0, kh * 4 + g], col)
    return col


def _attn_fwd(qkv, sinks, comm):
    s = qkv.shape[0]
    tq = Q_BLOCKS * BLOCK

    def body(sinks_ref, q_ref, kp_ref, kc_ref, vp_ref, vc_ref, o_ref, lse_ref):
        n = pl.program_id(0)
        lower = _lower_lanes()
        lane = lax.broadcasted_iota(jnp.int32, (BLOCK, 128), 1)
        for sub in range(Q_BLOCKS):
            rows = slice(sub * BLOCK, (sub + 1) * BLOCK)
            mask = _attn_mask(n > 0 if sub == 0 else True)
            kw, vw = _window(kp_ref, kc_ref, sub), _window(vp_ref, vc_ref, sub)
            qv = q_ref[rows, :]
            lse_out = jnp.zeros((BLOCK, 128), F32)
            for kh in range(2):
                qs = _stack_heads(qv, kh)
                kd, vd = _dup_kv(kw, kh), _dup_kv(vw, kh)
                sc = lax.dot_general(qs, kd, NT, preferred_element_type=F32) * ATTN_SCALE
                sc = jnp.where(mask, sc, NEG)
                sink = _sink_col(sinks_ref, kh)
                m = jnp.maximum(jnp.max(sc, axis=1, keepdims=True), sink)
                p = jnp.exp(sc - m)
                l = jnp.sum(p, axis=1, keepdims=True) + jnp.exp(sink - m)
                o = jnp.dot(p.astype(BF16), vd, preferred_element_type=F32) / l
                lse = m + jnp.log(l)
                for pair in range(2):
                    lo = o[(2 * pair) * BLOCK:(2 * pair + 1) * BLOCK]
                    hi = o[(2 * pair + 1) * BLOCK:(2 * pair + 2) * BLOCK]
                    col = (kh * 2 + pair) * 128
                    o_ref[rows, col:col + 128] = jnp.where(lower, lo, hi).astype(BF16)
                for g in range(4):
                    lse_out = jnp.where(lane == kh * 4 + g, lse[g * BLOCK:(g + 1) * BLOCK], lse_out)
            lse_ref[rows, :] = lse_out

    return _pcall(
        body, "attn_fwd", (s // tq,),
        [pl.BlockSpec(memory_space=pltpu.SMEM)] + _attn_specs(),
        [pl.BlockSpec((tq, ATTN_W), lambda n: (n, 0)), pl.BlockSpec((tq, 128), lambda n: (n, 0))],
        [_sds((s, ATTN_W), BF16), _sds((s, 128), F32)],
        (sinks, qkv, qkv, qkv, qkv, qkv), comm=comm)


def _conv_u(cbx_ref, halo_ref, w_ref, first):
    cb = cbx_ref[:, 0:CONV_W].astype(F32)
    cc = cbx_ref[:, CONV_W:2 * CONV_W].astype(F32)
    cx = cbx_ref[:, 2 * CONV_W:3 * CONV_W].astype(F32)
    u = cc * cx
    uh = halo_ref[:, CONV_W:2 * CONV_W].astype(F32) * halo_ref[:, 2 * CONV_W:3 * CONV_W].astype(F32)
    uh = jnp.where(first, 0.0, uh)
    u1, u2 = _shifts_down(u, uh, (1, 2))
    cv = w_ref[0:1, :] * u2 + w_ref[1:2, :] * u1 + w_ref[2:3, :] * u
    return cb, cc, cx, u, cv


def _mix_fwd(x, cbx, gates, attn, conv_w, wa, wc, wout, comm):
    s = x.shape[0]
    tm = _row_tile(s)

    def body(x_ref, cbx_ref, halo_ref, gate_ref, attn_ref, cw_ref, wa_ref, wc_ref, wo_ref,
             h1_ref):
        first = pl.program_id(0) == 0
        cb, _, _, _, cv = _conv_u(cbx_ref, halo_ref, cw_ref, first)
        conv = (cb * cv).astype(BF16)
        ap = jnp.dot(attn_ref[...], wa_ref[...], preferred_element_type=F32)
        cp = jnp.dot(conv, wc_ref[...], preferred_element_type=F32)
        ga = gate_ref[:, 0:D_MODEL].astype(F32)
        gc = gate_ref[:, D_MODEL:2 * D_MODEL].astype(F32)
        merged = (_sig(ga) * ap + _sig(gc) * cp).astype(BF16)
        h1_ref[...] = x_ref[...] + jnp.dot(merged, wo_ref[...], preferred_element_type=F32)

    return _pcall(
        body, "mix_fwd", (s // tm,),
        [_rows(tm, D_MODEL), _rows(tm, CBX_W), pl.BlockSpec((HALO, CBX_W), _prev_halo_map(tm)),
         _rows(tm, GATE_W), _rows(tm, ATTN_W), _full((3, CONV_W)), _full((ATTN_W, D_MODEL)),
         _full((CONV_W, D_MODEL)), _full((D_MODEL, D_MODEL))],
        [_rows(tm, D_MODEL)], [_sds((s, D_MODEL), F32)],
        (x, cbx, cbx, gates, attn, conv_w, wa, wc, wout), comm=comm)


def _col_offsets(wup_parts):
    widths = [p.shape[1] for p in wup_parts]
    assert sum(widths) == D_MODEL
    return [(sum(widths[:k]), w) for k, w in enumerate(widths)]


def _ffn_up(h1, g, wup_parts, fcw, comm):
    s = h1.shape[0]
    tm = _row_tile(s)
    cols = _col_offsets(wup_parts)
    n_w, n_chunks, n_groups = len(cols), 2 * D_FF // FF_CHUNK, 4
    starts = [j * n_chunks // n_groups for j in range(n_groups + 1)]

    def body(h_ref, g_ref, *refs):
        w_hbm = refs[:n_w]
        fcw_ref, hn_ref, pre_ref, up_ref, carry_ref = refs[n_w:n_w + 5]
        w_refs, sems = refs[n_w + 5:2 * n_w + 5], refs[2 * n_w + 5]

        def fetch(j, k):
            rows = pl.ds(starts[j] * FF_CHUNK, (starts[j + 1] - starts[j]) * FF_CHUNK)
            return pltpu.make_async_copy(w_hbm[k].at[rows, :], w_refs[k].at[rows, :], sems.at[j * n_w + k])

        def tile(first):
            hv = h_ref[...]
            r = lax.rsqrt(jnp.mean(hv * hv, axis=-1, keepdims=True) + NORM_EPS)
            hn = (hv * r * g_ref[...]).astype(BF16)
            hn_ref[...] = hn
            for c in range(n_chunks):
                sl = slice(c * FF_CHUNK, (c + 1) * FF_CHUNK)
                if first and c in starts:
                    for k in range(n_w):
                        fetch(starts.index(c), k).wait()
                acc = None
                for w_ref, (off, w) in zip(w_refs, cols):
                    part = lax.dot_general(hn[:, off:off + w], w_ref[sl, :], NT, preferred_element_type=F32)
                    acc = part if acc is None else acc + part
                pre_ref[:, sl] = acc.astype(BF16)
                halo = carry_ref[:, sl]
                carry_ref[:, sl] = acc[tm - HALO:, :]
                u1, u2 = _shifts_down(acc, halo, (1, 2))
                w = fcw_ref[:, sl]
                up_ref[:, sl] = (w[0:1] * u2 + w[1:2] * u1 + w[2:3] * acc).astype(BF16)

        @pl.when(pl.program_id(0) == 0)
        def _():
            for j in range(n_groups):
                for k in range(n_w):
                    fetch(j, k).start()
            carry_ref[...] = jnp.zeros_like(carry_ref)
            tile(True)

        @pl.when(pl.program_id(0) > 0)
        def _():
            tile(False)

    return _pcall(
        body, "ffn_up", (s // tm,),
        [_rows(tm, D_MODEL), _full((1, D_MODEL))] + [ANY] * n_w + [_full((3, 2 * D_FF))],
        [_rows(tm, D_MODEL), _rows(tm, 2 * D_FF), _rows(tm, 2 * D_FF)],
        [_sds((s, D_MODEL), BF16), _sds((s, 2 * D_FF), BF16), _sds((s, 2 * D_FF), BF16)],
        (h1, g, *wup_parts, fcw),
        scratch=[pltpu.VMEM((HALO, 2 * D_FF), F32)] + [pltpu.VMEM((2 * D_FF, w), BF16) for _, w in cols]
        + [pltpu.SemaphoreType.DMA((n_groups * n_w,))], comm=comm)


def _ffn_down_loss(up, wdown, h1, fnorm, target):
    s = h1.shape[0]
    tm = _row_tile(s)

    def body(up_ref, wd_ref, h1_ref, fn_ref, t_ref, act_ref, dh2_ref, loss_ref, dfn_ref):
        i = pl.program_id(0)

        @pl.when(i == 0)
        def _():
            loss_ref[...] = jnp.zeros_like(loss_ref)
            dfn_ref[...] = jnp.zeros_like(dfn_ref)

        h2 = h1_ref[...]
        for c in range(D_FF // FF_CHUNK):
            gsl = slice(c * FF_CHUNK, (c + 1) * FF_CHUNK)
            vsl = slice(D_FF + c * FF_CHUNK, D_FF + (c + 1) * FF_CHUNK)
            gate = up_ref[:, gsl].astype(F32)
            val = up_ref[:, vsl].astype(F32)
            act = (gate * _sig(gate) * val).astype(BF16)
            act_ref[:, gsl] = act
            h2 = h2 + jnp.dot(act, wd_ref[gsl, :], preferred_element_type=F32)
        r = lax.rsqrt(jnp.mean(h2 * h2, axis=-1, keepdims=True) + NORM_EPS)
        yhat = h2 * r
        fn = fn_ref[...]
        diff = yhat * fn - t_ref[...]
        loss_ref[...] += 0.5 * jnp.sum(jnp.sum(diff * diff, axis=1, keepdims=True), axis=0, keepdims=True) / D_MODEL
        dy = diff * (1.0 / D_MODEL)
        dfn_ref[...] += jnp.sum(dy * yhat, axis=0, keepdims=True)
        dyh = dy * fn
        dh2_ref[...] = r * (dyh - yhat * jnp.mean(dyh * yhat, axis=-1, keepdims=True))

    return _pcall(
        body, "ffn_down_loss", (s // tm,),
        [_rows(tm, 2 * D_FF), _resident((D_FF, D_MODEL)), _rows(tm, D_MODEL), _full((1, D_MODEL)),
         _rows(tm, D_MODEL)],
        [_rows(tm, D_FF), _rows(tm, D_MODEL), _full((1, 128)), _full((1, D_MODEL))],
        [_sds((s, D_FF), BF16), _sds((s, D_MODEL), F32), _sds((1, 128), F32), _sds((1, D_MODEL), F32)],
        (up, wdown, h1, fnorm, target))[0]


def _ffn_bwd(dh2, wdown, up, up_pre, fcw, wup_parts, h1, g, comm):
    s = dh2.shape[0]
    tm = _row_tile(s)
    cols = _col_offsets(wup_parts)

    chunk = FF_GRAD_ROWS

    def dup_cols(dh, up_ref, wd_ref, c):
        gsl = slice(c * chunk, (c + 1) * chunk)
        vsl = slice(D_FF + c * chunk, D_FF + (c + 1) * chunk)
        dact = lax.dot_general(dh, wd_ref[gsl, :], NT, preferred_element_type=F32)
        gate = up_ref[:, gsl].astype(F32)
        val = up_ref[:, vsl].astype(F32)
        sg = _sig(gate)
        return dact * val * (sg * (1.0 + gate * (1.0 - sg))), dact * gate * sg

    def body(dh_ref, wd_ref, up_ref, x_ref, w_ref, *refs):
        wup_refs = refs[:len(cols)]
        h_ref, g_ref, dx_ref, dw_ref, dh1_ref, dg_ref, carry_ref = refs[len(cols):]

        @pl.when(pl.program_id(0) == 0)
        def _():
            dw_ref[...] = jnp.zeros_like(dw_ref)
            dg_ref[...] = jnp.zeros_like(dg_ref)
            carry_ref[...] = jnp.zeros_like(carry_ref)

        dh2v = dh_ref[...]
        dh = dh2v.astype(BF16)
        dhn = [jnp.zeros((tm, w), F32) for _, w in cols]
        for c in range(D_FF // chunk):
            for d, off in zip(dup_cols(dh, up_ref, wd_ref, c), (c * chunk, D_FF + c * chunk)):
                sl = slice(off, off + chunk)
                dn = carry_ref[:, sl]
                carry_ref[:, sl] = d[0:HALO, :]
                xv = x_ref[:, sl].astype(F32)
                wv = w_ref[:, sl]
                d1, d2 = _shifts_up(d, dn, (1, 2))
                dx = (wv[2:3] * d + wv[1:2] * d1 + wv[0:1] * d2).astype(BF16)
                dx_ref[:, sl] = dx
                dhn = [a + jnp.dot(dx, wup_ref[sl, :], preferred_element_type=F32)
                       for a, wup_ref in zip(dhn, wup_refs)]
                dw_ref[0:1, sl] += jnp.sum(d2 * xv, axis=0, keepdims=True)
                dw_ref[1:2, sl] += jnp.sum(d1 * xv, axis=0, keepdims=True)
                dw_ref[2:3, sl] += jnp.sum(d * xv, axis=0, keepdims=True)
        dx1, dg = _norm_bwd_tile(h_ref[...], g_ref[...], jnp.concatenate(dhn, axis=1))
        dg_ref[...] += dg
        dh1_ref[...] = dh2v + dx1

    rows = lambda c: _rows_reversed(tm, c, s // tm)
    return _pcall(
        body, "ffn_bwd", (s // tm,),
        [rows(D_MODEL), _resident((D_FF, D_MODEL)), rows(2 * D_FF), rows(2 * D_FF), _full((3, 2 * D_FF))]
        + [_resident((2 * D_FF, w)) for _, w in cols] + [rows(D_MODEL), _full((1, D_MODEL))],
        [rows(2 * D_FF), _full((3, 2 * D_FF)), rows(D_MODEL), _full((1, D_MODEL))],
        [_sds((s, 2 * D_FF), BF16), _sds((3, 2 * D_FF), F32), _sds((s, D_MODEL), F32), _sds((1, D_MODEL), F32)],
        (dh2, wdown, up, up_pre, fcw, *wup_parts, h1, g),
        scratch=[pltpu.VMEM((HALO, 2 * D_FF), F32)], comm=comm)


def _matmul_tn(a, b, tk, name, ts=1024, comm=None):
    s, ka = a.shape
    n = b.shape[1]
    ts = min(ts, s)
    steps = s // ts

    def body(a_ref, b_ref, o_ref, acc_ref):
        j = pl.program_id(1)

        @pl.when(j == 0)
        def _():
            acc_ref[...] = jnp.zeros_like(acc_ref)

        acc_ref[...] += lax.dot_general(a_ref[...].astype(BF16), b_ref[...].astype(BF16), TN,
                                        preferred_element_type=F32)

        @pl.when(j == steps - 1)
        def _():
            o_ref[...] = acc_ref[...].astype(BF16)

    outs, couts = _pcall(
        body, name, (ka // tk, steps),
        [pl.BlockSpec((ts, tk), lambda i, j: (j, i)), pl.BlockSpec((ts, n), lambda i, j: (j, 0))],
        [pl.BlockSpec((tk, n), lambda i, j: (i, 0))], [_sds((ka, n), BF16)],
        (a, b), scratch=[pltpu.VMEM((tk, n), F32)], comm=comm)
    return outs[0] if comm is None else (outs[0], couts)


def _norm_bwd_tile(xv, g, dy):
    r = lax.rsqrt(jnp.mean(xv * xv, axis=-1, keepdims=True) + NORM_EPS)
    xhat = xv * r
    dg = jnp.sum(dy * xhat, axis=0, keepdims=True)
    dyh = dy * g
    return r * (dyh - xhat * jnp.mean(dyh * xhat, axis=-1, keepdims=True)), dg


def _mix_bwd(dh1, wout, gates, attn, wa, wc, cbx, conv_w, comm):
    s = dh1.shape[0]
    tm = _row_tile(s)
    steps = s // tm

    def body(dh_ref, wo_ref, gate_ref, attn_ref, wa_ref, wc_ref, cbx_ref, halo_ref,
             cw_ref, dg_ref, dattn_ref, dcb_ref, dcc_ref, dcx_ref, dw_ref, gwo_ref, gwa_ref, gwc_ref,
             acc_o, acc_a, acc_c, carry_ref):
        i = pl.program_id(0)

        @pl.when(i == 0)
        def _():
            dw_ref[...] = jnp.zeros_like(dw_ref)
            acc_o[...] = jnp.zeros_like(acc_o)
            acc_a[...] = jnp.zeros_like(acc_a)
            acc_c[...] = jnp.zeros_like(acc_c)
            carry_ref[...] = jnp.zeros_like(carry_ref)

        cb, cc, cx, u, cv = _conv_u(cbx_ref, halo_ref, cw_ref, i == steps - 1)
        attn = attn_ref[...]
        conv = (cb * cv).astype(BF16)
        ap = jnp.dot(attn, wa_ref[...], preferred_element_type=F32)
        cp = jnp.dot(conv, wc_ref[...], preferred_element_type=F32)
        dhb = dh_ref[...].astype(BF16)
        dm = lax.dot_general(dhb, wo_ref[...], NT, preferred_element_type=F32)
        sa = _sig(gate_ref[:, 0:D_MODEL].astype(F32))
        sc = _sig(gate_ref[:, D_MODEL:2 * D_MODEL].astype(F32))
        merged = (sa * ap + sc * cp).astype(BF16)
        da = (dm * sa).astype(BF16)
        dc = (dm * sc).astype(BF16)
        dg_ref[:, 0:D_MODEL] = (dm * ap * sa * (1.0 - sa)).astype(BF16)
        dg_ref[:, D_MODEL:2 * D_MODEL] = (dm * cp * sc * (1.0 - sc)).astype(BF16)
        dattn_ref[...] = lax.dot_general(da, wa_ref[...], NT, preferred_element_type=F32).astype(BF16)
        dconv = lax.dot_general(dc, wc_ref[...], NT, preferred_element_type=F32)
        dcb_ref[...] = (dconv * cv).astype(BF16)
        d = dconv * cb
        dn = carry_ref[...]
        carry_ref[...] = d[0:HALO, :]
        d1, d2 = _shifts_up(d, dn, (1, 2))
        du = cw_ref[2:3, :] * d + cw_ref[1:2, :] * d1 + cw_ref[0:1, :] * d2
        dcc_ref[...] = (du * cx).astype(BF16)
        dcx_ref[...] = (du * cc).astype(BF16)
        dw_ref[0:1, :] += jnp.sum(d2 * u, axis=0, keepdims=True)
        dw_ref[1:2, :] += jnp.sum(d1 * u, axis=0, keepdims=True)
        dw_ref[2:3, :] += jnp.sum(d * u, axis=0, keepdims=True)
        acc_o[...] += lax.dot_general(merged, dhb, TN, preferred_element_type=F32)
        acc_a[...] += lax.dot_general(attn, da, TN, preferred_element_type=F32)
        acc_c[...] += lax.dot_general(conv, dc, TN, preferred_element_type=F32)

        @pl.when(i == steps - 1)
        def _():
            gwo_ref[...] = acc_o[...].astype(BF16)
            gwa_ref[...] = acc_a[...].astype(BF16)
            gwc_ref[...] = acc_c[...].astype(BF16)

    rows = lambda c: _rows_reversed(tm, c, steps)
    return _pcall(
        body, "mix_bwd", (steps,),
        [rows(D_MODEL), _full((D_MODEL, D_MODEL)), rows(GATE_W), rows(ATTN_W), _full((ATTN_W, D_MODEL)),
         _full((CONV_W, D_MODEL)), rows(CBX_W), pl.BlockSpec((HALO, CBX_W), _prev_halo_map_reversed(tm, steps)),
         _full((3, CONV_W))],
        [rows(GATE_W), rows(ATTN_W), rows(CONV_W), rows(CONV_W), rows(CONV_W),
         _full((3, CONV_W)), _full((D_MODEL, D_MODEL)), _full((ATTN_W, D_MODEL)), _full((CONV_W, D_MODEL))],
        [_sds((s, GATE_W), BF16), _sds((s, ATTN_W), BF16), _sds((s, CONV_W), BF16), _sds((s, CONV_W), BF16),
         _sds((s, CONV_W), BF16), _sds((3, CONV_W), F32), _sds((D_MODEL, D_MODEL), BF16),
         _sds((ATTN_W, D_MODEL), BF16), _sds((CONV_W, D_MODEL), BF16)],
        (dh1, wout, gates, attn, wa, wc, cbx, cbx, conv_w),
        scratch=[pltpu.VMEM((D_MODEL, D_MODEL), F32), pltpu.VMEM((ATTN_W, D_MODEL), F32),
                 pltpu.VMEM((CONV_W, D_MODEL), F32), pltpu.VMEM((HALO, CONV_W), F32)], comm=comm)


def _attn_bwd(qkv, sinks, attn, lse, dattn, comm):
    s = qkv.shape[0]
    tq = Q_BLOCKS * BLOCK

    def body(sinks_ref, q_ref, kp_ref, kc_ref, vp_ref, vc_ref, o_ref, lse_ref, do_ref,
             dq_ref, dk_ref, dv_ref, ds_ref):
        n = pl.program_id(0)

        @pl.when(n == 0)
        def _():
            dk_ref[...] = jnp.zeros_like(dk_ref)
            dv_ref[...] = jnp.zeros_like(dv_ref)
            ds_ref[...] = jnp.zeros_like(ds_ref)

        lower = _lower_lanes()
        lane = lax.broadcasted_iota(jnp.int32, (BLOCK, 128), 1)
        lower2 = lax.broadcasted_iota(jnp.int32, (2 * BLOCK, 128), 1) < HEAD_DIM
        lane1 = lax.broadcasted_iota(jnp.int32, (1, 128), 1)
        dsink = jnp.zeros((1, 128), F32)
        for sub in reversed(range(Q_BLOCKS)):
            rows = slice(sub * BLOCK, (sub + 1) * BLOCK)
            mask = _attn_mask(n > 0 if sub == 0 else True)
            kw, vw = _window(kp_ref, kc_ref, sub), _window(vp_ref, vc_ref, sub)
            qv, ov, dov, lsev = q_ref[rows, :], o_ref[rows, :], do_ref[rows, :], lse_ref[rows, :]
            dk_fold, dv_fold = [], []
            for kh in range(2):
                qs = _stack_heads(qv, kh)
                dos = _stack_heads(dov, kh)
                os_ = _stack_heads(ov, kh)
                kd, vd = _dup_kv(kw, kh), _dup_kv(vw, kh)
                lse = jnp.concatenate(
                    [jnp.sum(jnp.where(lane == kh * 4 + g, lsev, 0.0), axis=1, keepdims=True) for g in range(4)],
                    axis=0)
                sc = lax.dot_general(qs, kd, NT, preferred_element_type=F32) * ATTN_SCALE
                p = jnp.exp(jnp.where(mask, sc, NEG) - lse)
                dp = lax.dot_general(dos, vd, NT, preferred_element_type=F32)
                delta = jnp.sum(dos.astype(F32) * os_.astype(F32), axis=1, keepdims=True)
                dsc = (p * (dp - delta) * ATTN_SCALE).astype(BF16)
                dqs = jnp.dot(dsc, kd, preferred_element_type=F32)
                for pair in range(2):
                    lo = dqs[(2 * pair) * BLOCK:(2 * pair + 1) * BLOCK]
                    hi = dqs[(2 * pair + 1) * BLOCK:(2 * pair + 2) * BLOCK]
                    col = (kh * 2 + pair) * 128
                    dq_ref[rows, col:col + 128] = jnp.where(lower, lo, hi).astype(BF16)
                dkd = lax.dot_general(dsc, qs, TN, preferred_element_type=F32)
                dvd = lax.dot_general(p.astype(BF16), dos, TN, preferred_element_type=F32)
                dk_fold.append(dkd + pltpu.roll(dkd, HEAD_DIM, axis=1))
                dv_fold.append(dvd + pltpu.roll(dvd, HEAD_DIM, axis=1))
                psink = jnp.exp(_sink_col(sinks_ref, kh) - lse) * delta
                for g in range(4):
                    tot = jnp.sum(psink[g * BLOCK:(g + 1) * BLOCK], axis=0, keepdims=True)
                    dsink = dsink - jnp.where(lane1 == kh * 4 + g, tot, 0.0)
            dk2 = jnp.where(lower2, dk_fold[0], dk_fold[1])
            dv2 = jnp.where(lower2, dv_fold[0], dv_fold[1])
            cur = pl.ds(pl.multiple_of((Q_BLOCKS * n + sub) * BLOCK, BLOCK), BLOCK)
            dk_ref[cur, :] += dk2[BLOCK:]
            dv_ref[cur, :] += dv2[BLOCK:]
            if sub > 0:
                prev = pl.ds(pl.multiple_of((Q_BLOCKS * n + sub - 1) * BLOCK, BLOCK), BLOCK)
                dk_ref[prev, :] += dk2[:BLOCK]
                dv_ref[prev, :] += dv2[:BLOCK]
        ds_ref[...] += dsink

        @pl.when(n > 0)
        def _():
            prev = pl.ds(pl.multiple_of((Q_BLOCKS * n - 1) * BLOCK, BLOCK), BLOCK)
            dk_ref[prev, :] += dk2[:BLOCK]
            dv_ref[prev, :] += dv2[:BLOCK]

    blk = lambda w: pl.BlockSpec((tq, w), lambda n: (n, 0))
    return _pcall(
        body, "attn_bwd", (s // tq,),
        [pl.BlockSpec(memory_space=pltpu.SMEM)] + _attn_specs() + [blk(ATTN_W), blk(128), blk(ATTN_W)],
        [blk(ATTN_W), _full((s, KV_W)), _full((s, KV_W)), _full((1, 128))],
        [_sds((s, ATTN_W), BF16), _sds((s, KV_W), F32), _sds((s, KV_W), F32), _sds((1, 128), F32)],
        (sinks, qkv, qkv, qkv, qkv, qkv, attn, lse, dattn), comm=comm)


DPROJ_PIECES = (ATTN_W, KV_W, KV_W, CONV_W, CONV_W, CONV_W, GATE_W)
DPROJ_OFFSETS = tuple(sum(DPROJ_PIECES[:k]) for k in range(len(DPROJ_PIECES)))


def _grad_w_in(pieces, xn, comm):
    s = xn.shape[0]
    ts = min(1024, s)
    steps = s // ts
    rows0 = DPROJ_OFFSETS[6]

    def body(*refs):
        p_refs, b_ref, o_ref, acc_ref, stage_ref, sem = refs[:7], refs[7], refs[8], refs[9], refs[10], refs[11]
        i, j = pl.program_id(0), pl.program_id(1)

        @pl.when(j == 0)
        def _():
            acc_ref[...] = jnp.zeros_like(acc_ref)

        bv = b_ref[...]

        def flush(lo, n):
            stage_ref[0:n, :] = acc_ref[0:n, :].astype(BF16)
            cp = pltpu.make_async_copy(stage_ref.at[0:n, :], o_ref.at[lo:lo + n, :], sem)
            cp.start()
            cp.wait()

        @pl.when(i == 0)
        def _():
            for p_ref, off, w in zip(p_refs[:6], DPROJ_OFFSETS[:6], DPROJ_PIECES[:6]):
                acc_ref[off:off + w, :] += lax.dot_general(p_ref[...].astype(BF16), bv, TN,
                                                           preferred_element_type=F32)

            @pl.when(j == steps - 1)
            def _():
                flush(0, rows0)

        @pl.when(i == 1)
        def _():
            acc_ref[0:GATE_W, :] += lax.dot_general(p_refs[6][...], bv, TN, preferred_element_type=F32)

            @pl.when(j == steps - 1)
            def _():
                flush(rows0, GATE_W)

    def piece_spec(w, group):
        return pl.BlockSpec((ts, w), lambda i, j: (jnp.where(i == group, j, 0), 0))

    outs, couts = _pcall(
        body, "grad_w_in", (2, steps),
        [piece_spec(w, 0) for w in DPROJ_PIECES[:6]] + [piece_spec(GATE_W, 1),
                                                         pl.BlockSpec((ts, D_MODEL), lambda i, j: (j, 0))],
        [ANY], [_sds((IN_W, D_MODEL), BF16)], (*pieces, xn),
        scratch=[pltpu.VMEM((rows0, D_MODEL), F32), pltpu.VMEM((rows0, D_MODEL), BF16), pltpu.SemaphoreType.DMA],
        comm=comm)
    return outs[0], couts


def _inproj_bwd(pieces, win_t, x, g, dh1, comm):
    s = x.shape[0]
    tm = _row_tile(s, 512)

    def body(*refs):
        p_refs = refs[:7]
        w_ref, x_ref, g_ref, dh_ref, dx_ref, db_ref, dg_ref = refs[7:]

        @pl.when(pl.program_id(0) == 0)
        def _():
            db_ref[...] = jnp.zeros_like(db_ref)
            dg_ref[...] = jnp.zeros_like(dg_ref)

        dxn = jnp.zeros((tm, D_MODEL), F32)
        for p_ref, off, w in zip(p_refs, DPROJ_OFFSETS, DPROJ_PIECES):
            v = p_ref[...].astype(BF16)
            db_ref[:, off:off + w] += jnp.sum(v.astype(F32), axis=0, keepdims=True)
            dxn = dxn + jnp.dot(v, w_ref[off:off + w, :], preferred_element_type=F32)
        dx, dg = _norm_bwd_tile(x_ref[...], g_ref[...], dxn)
        dg_ref[...] += dg
        dx_ref[...] = dh_ref[...] + dx

    return _pcall(
        body, "inproj_bwd", (s // tm,),
        [_rows(tm, w) for w in DPROJ_PIECES] + [_resident((IN_W, D_MODEL)), _rows(tm, D_MODEL), _full((1, D_MODEL)),
                                                _rows(tm, D_MODEL)],
        [_rows(tm, D_MODEL), _full((8, IN_W)), _full((8, D_MODEL))],
        [_sds((s, D_MODEL), F32), _sds((8, IN_W), F32), _sds((8, D_MODEL), F32)],
        (*pieces, win_t, x, g, dh1), comm=comm)


def _adam_math(w, g, m, v):
    m2 = ADAM_B1 * m + (1.0 - ADAM_B1) * g
    v2 = ADAM_B2 * v + (1.0 - ADAM_B2) * (g * g)
    m_hat = m2 / (1.0 - ADAM_B1 ** ADAM_STEP)
    v_hat = v2 / (1.0 - ADAM_B2 ** ADAM_STEP)
    delta = -ADAM_LR * (m_hat / (jnp.sqrt(v_hat) + ADAM_EPS) + ADAM_WD * w)
    return delta, m2, v2


def _sum_slots(ref):
    tot = ref[0].astype(F32)
    for i in range(1, ref.shape[0]):
        tot = tot + ref[i].astype(F32)
    return tot


def _pair_sum(partials, name):
    r, c = partials.shape[0] // N_DEV, partials.shape[1]
    core = lax.axis_index("c").astype(jnp.int32).reshape(1)

    def body(core_ref, mine_ref, all_ref, o_ref, theirs_ref, send_sems, recv_sems):
        k = pl.program_id(0)
        x, y, cc = _my_place()

        def copy(chip):
            return pltpu.make_async_remote_copy(
                src_ref=all_ref.at[pl.ds(pl.multiple_of((2 * chip + 1 - cc) * r, 8), r), :],
                dst_ref=theirs_ref.at[chip], send_sem=send_sems.at[chip], recv_sem=recv_sems.at[chip],
                device_id=(x, y, 1 - cc), device_id_type=MESH)

        @pl.when(k == 0)
        def _():
            _barrier_signal(SIBLING_PEER)
            _barrier_wait(SIBLING_PEER)
            for chip in range(4):
                copy(chip).start()

        for chip in range(4):
            @pl.when(k == chip)
            def _(chip=chip):
                copy(chip).wait_recv()
                o_ref[...] = (mine_ref[...].astype(F32) + theirs_ref[chip].astype(F32)).astype(BF16)

        @pl.when(k == 3)
        def _():
            for chip in range(4):
                copy(chip).wait_send()

    grid_spec = pltpu.PrefetchScalarGridSpec(
        num_scalar_prefetch=1, grid=(4,),
        in_specs=[pl.BlockSpec((None, None, r, c), lambda k, core_ref: (k, core_ref[0], 0, 0)), ANY],
        out_specs=pl.BlockSpec((r, c), lambda k, core_ref: (k, 0)),
        scratch_shapes=[pltpu.VMEM((4, r, c), BF16), pltpu.SemaphoreType.DMA((4,)), pltpu.SemaphoreType.DMA((4,))])
    params = pltpu.CompilerParams(dimension_semantics=("arbitrary",), vmem_limit_bytes=VMEM_LIMIT,
                                  collective_id=BARRIER_ID[SIBLING_PEER])
    return pl.pallas_call(body, name=name, grid_spec=grid_spec, out_shape=_sds((4 * r, c), BF16),
                          compiler_params=params)(core, partials.reshape(4, 2, r, c), partials)


def _sum_adamw(parts, w, m, v, tr, name):
    r, c = w.shape

    def body(p_ref, w_ref, m_ref, v_ref, g_ref, d_ref, m2_ref, v2_ref):
        g = _sum_slots(p_ref)
        g_ref[...] = g
        d_ref[...], m2_ref[...], v2_ref[...] = _adam_math(w_ref[...], g, m_ref[...], v_ref[...])

    spec = pl.BlockSpec((tr, c), lambda i: (i, 0))
    return _pcall(body, name, (r // tr,), [pl.BlockSpec((N_DEV, tr, c), lambda i: (0, i, 0)), spec, spec, spec],
                  [spec] * 4, [_sds((r, c), F32)] * 4, (parts, w, m, v))[0]


def _sum_parts_adamw(parts, w, m, v, tr, name):
    c = w.shape[1]
    tiles = [p.shape[1] // tr for p in parts]
    starts = [sum(tiles[:k]) for k in range(len(parts))]
    n_parts = len(parts)

    def body(*refs):
        p_refs = refs[:n_parts]
        w_ref, m_ref, v_ref, g_ref, d_ref, m2_ref, v2_ref = refs[n_parts:]
        i = pl.program_id(0)
        for p_ref, st, nt in zip(p_refs, starts, tiles):
            @pl.when(jnp.logical_and(i >= st, i < st + nt))
            def _(p_ref=p_ref):
                g_ref[...] = _sum_slots(p_ref)

        d_ref[...], m2_ref[...], v2_ref[...] = _adam_math(w_ref[...], g_ref[...], m_ref[...], v_ref[...])

    def part_spec(p, st, nt):
        return pl.BlockSpec((p.shape[0], tr, c), lambda i: (0, jnp.clip(i - st, 0, nt - 1), 0))

    spec = pl.BlockSpec((tr, c), lambda i: (i, 0))
    return _pcall(
        body, name, (sum(tiles),),
        [part_spec(p, st, nt) for p, st, nt in zip(parts, starts, tiles)] + [spec, spec, spec],
        [spec] * 4, [_sds(w.shape, F32)] * 4, (*parts, w, m, v))[0]


ROW_MIX, ROW_FFN, ROW_FINAL, ROW_SINKS, ROW_LOSS, ROW_BIN, ROW_CW, ROW_FCW = 0, 1, 2, 3, 4, 5, 10, 13
FCW_ROWS = 6


def _wide_pieces(width):
    return [(k * D_MODEL, min(D_MODEL, width - k * D_MODEL)) for k in range(-(-width // D_MODEL))]


def _pack_small(dffn, dfn, dsink, loss, dcw, dfcw):
    def body(ffn_ref, fn_ref, sink_ref, loss_ref, cw_ref, fcw_ref, o_ref):
        o_ref[...] = jnp.zeros_like(o_ref)
        o_ref[ROW_FFN:ROW_FFN + 1, :] = ffn_ref[...]
        o_ref[ROW_FINAL:ROW_FINAL + 1, :] = fn_ref[...]
        o_ref[ROW_SINKS:ROW_SINKS + 1, 0:128] = sink_ref[...]
        o_ref[ROW_LOSS:ROW_LOSS + 1, 0:128] = loss_ref[...]
        o_ref[ROW_CW:ROW_CW + 3, 0:CONV_W] = cw_ref[...]
        for a in range(3):
            for k, (off, w) in enumerate(_wide_pieces(2 * D_FF)):
                row = ROW_FCW + FCW_ROWS * a + k
                o_ref[row:row + 1, 0:w] = fcw_ref[a:a + 1, off:off + w]

    return pl.pallas_call(body, name="pack_small", out_shape=_sds((SMALL_ROWS, D_MODEL), F32))(
        dffn, dfn, dsink, loss, dcw, dfcw)


def _small_sums_adamw(r_small, r_dmix, r_dbin, params):
    rows = (None, None, ROW_SINKS, ROW_FFN, ROW_FINAL)

    def sum_row0(ref):
        tot = ref[0:1, :]
        for i in range(1, N_DEV):
            tot = tot + ref[8 * i:8 * i + 1, :]
        return tot

    def body(*refs):
        r_ref, late_refs, p_refs, o_refs = refs[0], refs[1:3], refs[3:18], refs[18:]
        tot = _sum_slots(r_ref)
        for k, row in enumerate(rows):
            w_ref, m_ref, v_ref = p_refs[3 * k:3 * k + 3]
            g_ref, d_ref, m2_ref, v2_ref = o_refs[4 * k:4 * k + 4]
            if row is None:
                g_ref[...] = sum_row0(late_refs[k])
            else:
                for j, (off, w) in enumerate(_wide_pieces(w_ref.shape[1])):
                    g_ref[:, off:off + w] = tot[row + j:row + j + 1, 0:w]
            d_ref[...], m2_ref[...], v2_ref[...] = _adam_math(w_ref[...], g_ref[...], m_ref[...], v_ref[...])
        cw_ref, fcw_ref, loss_ref = o_refs[20:]
        cw_ref[...] = tot[ROW_CW:ROW_CW + 3, 0:CONV_W]
        for a in range(3):
            for j, (off, w) in enumerate(_wide_pieces(2 * D_FF)):
                row = ROW_FCW + FCW_ROWS * a + j
                fcw_ref[a:a + 1, off:off + w] = tot[row:row + 1, 0:w]
        loss_ref[...] = tot[ROW_LOSS:ROW_LOSS + 1, 0:128]

    flat = [t for p in params for t in p]
    out_shape = [_sds(p[0].shape, F32) for p in params for _ in range(4)]
    out_shape += [_sds((3, CONV_W), F32), _sds((3, 2 * D_FF), F32), _sds((1, 128), F32)]
    res = pl.pallas_call(body, name="small_sums_adamw", out_shape=out_shape)(r_small, r_dmix, r_dbin, *flat)
    return [tuple(res[4 * k:4 * k + 4]) for k in range(5)], res[20], res[21], res[22]


def _adamw_pair(a, b):
    def body(*refs):
        for k in range(2):
            w_ref, g_ref, m_ref, v_ref = refs[4 * k:4 * k + 4]
            d_ref, m2_ref, v2_ref = refs[8 + 3 * k:8 + 3 * k + 3]
            d_ref[...], m2_ref[...], v2_ref[...] = _adam_math(w_ref[...], g_ref[...], m_ref[...], v_ref[...])

    out_shape = [_sds(a[0].shape, F32)] * 3 + [_sds(b[0].shape, F32)] * 3
    res = pl.pallas_call(body, name="adamw_conv_weights", out_shape=out_shape)(*a, *b)
    return tuple(res[:3]), tuple(res[3:])


def _pad_cols(a, c):
    return jnp.pad(a, ((0, 0), (0, c - a.shape[1])))


def _to_col_slabs(g):
    r = g.shape[0]
    return jnp.transpose(g.reshape(r, N_DEV, 128), (1, 0, 2)).reshape(N_DEV * r, 128)


def _from_col_slabs(t):
    r = t.shape[0] // N_DEV
    return jnp.transpose(t.reshape(N_DEV, r, 128), (1, 0, 2)).reshape(r, N_DEV * 128)


def _slots(t):
    return t.reshape(N_DEV, t.shape[0] // N_DEV, t.shape[1])


def kernel(x, mix_norm, w_in, b_in, sinks, conv_w, w_attn_branch, w_conv_branch, w_out, ffn_norm, w_up, ffn_conv_w, w_down, final_norm, loss_target, m_mix_norm, m_w_in, m_b_in, m_sinks, m_conv_w, m_w_attn_branch, m_w_conv_branch, m_w_out, m_ffn_norm, m_w_up, m_ffn_conv_w, m_w_down, m_final_norm, v_mix_norm, v_w_in, v_b_in, v_sinks, v_conv_w, v_w_attn_branch, v_w_conv_branch, v_w_out, v_ffn_norm, v_w_up, v_ffn_conv_w, v_w_down, v_final_norm):
    xs, tgt = x[0], loss_target[0]
    me = 4 * lax.axis_index("x") + 2 * lax.axis_index("y") + lax.axis_index("c")
    in_rows, up_rows = IN_W // N_DEV, 2 * D_FF // N_DEV

    conv_sh = jnp.concatenate([_pad_cols(ffn_conv_w[0], 768), _pad_cols(conv_w[0], 768),
                               jnp.zeros((2, 768), F32)], axis=0)
    win_sh, wup_sh = w_in[0].T.astype(BF16), w_up[0].T.astype(BF16)
    wout_sh, wdown_sh = w_out[0].astype(BF16), w_down[0].astype(BF16)
    wa_sh, wc_sh = w_attn_branch[0].astype(BF16), w_conv_branch[0].astype(BF16)

    quarter, half = D_MODEL // 4, D_MODEL // 2
    phases = dict(forward_at=0.375, pass_on_at=0.875)
    (win_t,) = _exchange_only(_AllGather([win_sh]), "gather_w_in")
    (xn, qkv, cbx, gates), (wa_s, wc_s, conv_g, wup_a) = _norm_inproj(
        xs, mix_norm, win_t, b_in, _AllGather([wa_sh, wc_sh, conv_sh, (wup_sh, 0, quarter)], **phases))
    (attn, lse), (wup_b, wout) = _attn_fwd(qkv, sinks,
                                           _AllGather([(wup_sh, quarter, quarter), wout_sh], **phases))
    wa, wc = _from_col_slabs(wa_s), _from_col_slabs(wc_s)
    conv_g = conv_g.reshape(N_DEV, 8, 768)
    fcw = jnp.transpose(conv_g[:, 0:3, :up_rows], (1, 0, 2)).reshape(3, 2 * D_FF)
    cw = jnp.transpose(conv_g[:, 3:6, :CONV_W // N_DEV], (1, 0, 2)).reshape(3, CONV_W)
    (h1,), (wup_c,) = _mix_fwd(xs, cbx, gates, attn, cw, wa, wc, wout,
                               _AllGather([(wup_sh, half, half)], **phases))
    wup_parts = (wup_a, wup_b, wup_c)
    (hn, up_pre, up), (wdown,) = _ffn_up(h1, ffn_norm, wup_parts, fcw,
                                         _AllGather([wdown_sh], forward_at=0.25, pass_on_at=0.75))
    act, dh2, loss_p, dfn_p = _ffn_down_loss(up, wdown, h1, final_norm.reshape(1, D_MODEL), tgt)

    dn_rows = D_FF // N_DEV
    g_wdown = _matmul_tn(act, dh2, FF_GRAD_ROWS, "grad_w_down")
    (dup_pre, dfcw_p, dh1, dffn_p), (r_wdown,) = _ffn_bwd(dh2, wdown, up, up_pre, fcw, wup_parts, h1, ffn_norm,
                                                         _ReduceScatter([(g_wdown, 0, dn_rows)]))
    g_wup_t = _matmul_tn(dup_pre, hn, FF_GRAD_ROWS, "grad_w_up")
    q_wup = _pair_sum(g_wup_t, "pair_sum_w_up")
    (dgates, dattn, dcb, dcc, dcx, dcw_p, g_wout, g_wa_nat, g_wc_nat), (r_wup,) = _mix_bwd(
        dh1, wout, gates, attn, wa, wc, cbx, cw, _ChipExchange([q_wup]))
    g_wa, g_wc = _to_col_slabs(g_wa_nat), _to_col_slabs(g_wc_nat)
    (dq, dk, dv, dsink_p), (r_wout,) = _attn_bwd(
        qkv, sinks, attn, lse, dattn, _ReduceScatter([(g_wout, 0, D_MODEL // N_DEV)]))
    dproj = (dq, dk, dv, dcb, dcc, dcx, dgates)
    small = _pack_small(dffn_p, dfn_p, dsink_p, loss_p, dcw_p, dfcw_p)
    g_win_t, (r_wa, r_wc, r_small) = _grad_w_in(
        dproj, xn, _ReduceScatter([(g_wa, 0, ATTN_W), (g_wc, 0, CONV_W)], [small]))
    q_win = _pair_sum(g_win_t, "pair_sum_w_in")
    (dx, _, _), (r_win, r_dbin, r_dmix) = _inproj_bwd(
        dproj, win_t, xs, mix_norm, dh1,
        _ChipExchangeThenBroadcast([q_win], late_from=(1, 2), late_shapes=[(8, IN_W), (8, D_MODEL)]))

    fn2, m_fn2, v_fn2 = (t.reshape(1, D_MODEL) for t in (final_norm, m_final_norm, v_final_norm))
    small_res, g_cw_full, g_fcw_full, loss_row = _small_sums_adamw(
        _slots(r_small), r_dmix, r_dbin,
        [(mix_norm, m_mix_norm, v_mix_norm), (b_in, m_b_in, v_b_in), (sinks, m_sinks, v_sinks),
         (ffn_norm, m_ffn_norm, v_ffn_norm), (fn2, m_fn2, v_fn2)])
    loss = loss_row[0, 0]
    g_cw = lax.dynamic_slice_in_dim(g_cw_full, me * (CONV_W // N_DEV), CONV_W // N_DEV, axis=1)
    g_fcw = lax.dynamic_slice_in_dim(g_fcw_full, me * up_rows, up_rows, axis=1)
    taps = lambda t: jnp.transpose(t, (1, 0, 2))
    g_cw, g_fcw = g_cw[:, None, :], g_fcw[:, None, :]
    cw_res, fcw_res = _adamw_pair((taps(conv_w), g_cw, taps(m_conv_w), taps(v_conv_w)),
                                  (taps(ffn_conv_w), g_fcw, taps(m_ffn_conv_w), taps(v_ffn_conv_w)))

    big = {}
    big["w_in"] = tuple(t.T for t in _sum_parts_adamw(
        [r_win.reshape(4, in_rows, D_MODEL)], w_in[0].T, m_w_in[0].T, v_w_in[0].T, in_rows // 2, "adamw_w_in"))
    big["w_up"] = tuple(t.T for t in _sum_parts_adamw(
        [r_wup.reshape(4, up_rows, D_MODEL)], w_up[0].T, m_w_up[0].T, v_w_up[0].T, up_rows // 4, "adamw_w_up"))
    big["w_out"] = _sum_adamw(_slots(r_wout), w_out[0], m_w_out[0], v_w_out[0], 128, "adamw_w_out")
    big["w_down"] = _sum_adamw(_slots(r_wdown), w_down[0], m_w_down[0], v_w_down[0], dn_rows // 2, "adamw_w_down")
    big["w_attn_branch"] = _sum_adamw(_slots(r_wa), w_attn_branch[0], m_w_attn_branch[0], v_w_attn_branch[0], 256,
                                      "adamw_w_attn_branch")
    big["w_conv_branch"] = _sum_adamw(_slots(r_wc), w_conv_branch[0], m_w_conv_branch[0], v_w_conv_branch[0], 256,
                                      "adamw_w_conv_branch")

    res = dict(zip(("mix_norm", "b_in", "sinks", "ffn_norm"), small_res[:4]))
    res["final_norm"] = tuple(t.reshape(final_norm.shape) for t in small_res[4])
    res["conv_w"] = tuple(jnp.transpose(t, (1, 0, 2)) for t in (g_cw,) + cw_res)
    res["ffn_conv_w"] = tuple(jnp.transpose(t, (1, 0, 2)) for t in (g_fcw,) + fcw_res)
    for name, ref_w in (("w_in", w_in), ("w_up", w_up), ("w_out", w_out), ("w_down", w_down),
                        ("w_attn_branch", w_attn_branch), ("w_conv_branch", w_conv_branch)):
        res[name] = tuple(t.reshape(ref_w.shape) for t in big[name])

    order = ["mix_norm", "w_in", "b_in", "sinks", "conv_w", "w_attn_branch", "w_conv_branch", "w_out",
             "ffn_norm", "w_up", "ffn_conv_w", "w_down", "final_norm"]
    out = [loss, dx.reshape(x.shape)]
    for k in range(4):
        out += [res[name][k] for name in order]
    return tuple(out)
```

```python
import math

import jax
import jax.numpy as jnp
from jax import lax
from jax.experimental import pallas as pl
from jax.experimental.pallas import tpu as pltpu

F32 = jnp.float32
BF16 = jnp.bfloat16
MESH = pl.DeviceIdType.MESH
N_DEV = 8

D_MODEL = 1024
HEAD_DIM = 64
N_HEADS = 8
BLOCK = 128
ATTN_W = 512
KV_W = 128
CONV_W = 512
QKV_W = ATTN_W + 2 * KV_W
CBX_W = 3 * CONV_W
GATE_W = 2 * D_MODEL
IN_W = QKV_W + CBX_W + GATE_W
D_FF = 2816
FF_CHUNK = 256
FF_GRAD_ROWS = 1408
NORM_EPS = 1e-5
ATTN_SCALE = HEAD_DIM ** -0.5
NEG = -1e30
HALO = 16

ADAM_LR = 0.001
ADAM_B1 = 0.9
ADAM_B2 = 0.999
ADAM_EPS = 1e-08
ADAM_WD = 0.01
ADAM_STEP = 10

VMEM_LIMIT = 56 * 1024 * 1024
SMALL_ROWS = 32

NT = (((1,), (1,)), ((), ()))
TN = (((0,), (0,)), ((), ()))
ANY = pl.BlockSpec(memory_space=pl.ANY)


def _sig(v):
    return 1.0 / (1.0 + jnp.exp(-v))


def _row_tile(s, pref=256):
    return pref if s % pref == 0 else s


def _shifts_down(u, halo, ks):
    ext = jnp.concatenate([halo, u], axis=0)
    return [pltpu.roll(ext, k, axis=0)[HALO:, :] for k in ks]


def _shifts_up(u, halo, ks):
    n = u.shape[0]
    ext = jnp.concatenate([u, halo], axis=0)
    return [pltpu.roll(ext, n + HALO - k, axis=0)[:n, :] for k in ks]


def _rows_reversed(tm, c, steps):
    return pl.BlockSpec((tm, c), lambda i: (steps - 1 - i, 0))


def _prev_halo_map_reversed(tm, steps):
    return lambda i: (jnp.maximum((steps - 1 - i) * (tm // HALO) - 1, 0), 0)


def _prev_halo_map(tm):
    return lambda i: (jnp.maximum(i * (tm // HALO) - 1, 0), 0)


def _full(shape):
    return pl.BlockSpec(shape, lambda *_: (0,) * len(shape))


def _resident(shape):
    return pl.BlockSpec(shape, lambda *_: (0,) * len(shape), pipeline_mode=pl.Buffered(1))


def _rows(tm, c):
    return pl.BlockSpec((tm, c), lambda i: (i, 0))


def _sds(shape, dtype):
    return jax.ShapeDtypeStruct(shape, dtype)


def _my_place():
    x, y, c = lax.axis_index("x"), lax.axis_index("y"), lax.axis_index("c")
    return x, y, c


ALL_PEERS = tuple((j >> 2, (j >> 1) & 1, j & 1) for j in range(1, N_DEV))
SIBLING_PEER = ((0, 0, 1),)
CHIP_PEERS = ((0, 1, 0), (1, 0, 0), (1, 1, 0))
BARRIER_ID = {ALL_PEERS: 0, SIBLING_PEER: 1, CHIP_PEERS: 2}


def _barrier_signal(peers):
    x, y, c = _my_place()
    barrier = pltpu.get_barrier_semaphore()
    for dx, dy, dc in peers:
        pl.semaphore_signal(barrier, inc=1, device_id=(x ^ dx, y ^ dy, c ^ dc), device_id_type=MESH)


def _barrier_wait(peers):
    pl.semaphore_wait(pltpu.get_barrier_semaphore(), len(peers))


def _start_exchange(remote, local):
    for cp in local + remote:
        cp.start()


def _finish_exchange(remote, local):
    for cp in remote:
        cp.wait_recv()
    for cp in remote:
        cp.wait_send()
    for cp in local:
        cp.wait()


class _AllGather:
    peers = ALL_PEERS
    SLOTS = 10

    def __init__(self, shards, pass_on_at=None, forward_at=None):
        self.ins = [s[0] if isinstance(s, tuple) else s for s in shards]
        self.cols = [s[1:] if isinstance(s, tuple) else None for s in shards]
        self.middle_at, self.forward_at = pass_on_at, forward_at
        assert pass_on_at is None or forward_at is not None
        n = len(shards)
        self.out_shape = [_sds((N_DEV * s.shape[0], s.shape[1] if c is None else c[1]), s.dtype)
                          for s, c in zip(self.ins, self.cols)]
        self.sems = [pltpu.SemaphoreType.DMA((self.SLOTS * n,)), pltpu.SemaphoreType.DMA((self.SLOTS * n,)),
                     pltpu.SemaphoreType.DMA((n,))]

    def _plan(self, ins, outs, sems):
        send_sems, recv_sems, local_sems = sems
        x, y, c = _my_place()
        me, sibling = (x, y, c), (x, y, 1 - c)
        x_chip, y_chip, far_chip = (1 - x, y), (x, 1 - y), (1 - x, 1 - y)
        sends, lands, mine = [], [], []
        for k in range(len(ins)):
            r = ins[k].shape[0]
            h = (r // 2) // 16 * 16
            whole, first, second = (0, r), (0, h), (h, r - h)

            def rows(dev, rng, k=k, r=r):
                start = pl.multiple_of((4 * dev[0] + 2 * dev[1] + dev[2]) * r + rng[0], 8)
                return outs[k].at[pl.ds(start, rng[1]), :]

            def own(rng, k=k):
                cols = self.cols[k]
                if cols is None:
                    return ins[k].at[pl.ds(rng[0], rng[1]), :]
                return ins[k].at[pl.ds(rng[0], rng[1]), pl.ds(cols[0], cols[1])]

            def copy(slot, block, rng, to, mine_src=False, k=k, rows=rows, own=own):
                if rng[1] == 0:
                    return None
                return pltpu.make_async_remote_copy(
                    src_ref=own(rng) if mine_src else rows(block, rng), dst_ref=rows(block, rng),
                    send_sem=send_sems.at[self.SLOTS * k + slot], recv_sem=recv_sems.at[self.SLOTS * k + slot],
                    device_id=to, device_id_type=MESH)

            sends.append([
                copy(0, me, whole, sibling, True),
                copy(1, me, first, (*x_chip, c), True),
                copy(2, me, second, (*x_chip, c), True),
                copy(3, me, second, (*y_chip, c), True),
                copy(4, me, first, (*y_chip, c), True),
                copy(5, (*x_chip, c), first, (*y_chip, c)),
                copy(6, (*y_chip, c), second, (*x_chip, c)),
                copy(7, (*x_chip, c), whole, sibling),
                copy(8, (*y_chip, c), whole, sibling),
                copy(9, (*far_chip, c), whole, sibling)])
            lands.append([
                copy(0, sibling, whole, me),
                copy(1, (*x_chip, c), first, me), copy(2, (*x_chip, c), second, me),
                copy(3, (*y_chip, c), second, me), copy(4, (*y_chip, c), first, me),
                copy(5, (*far_chip, c), first, me), copy(6, (*far_chip, c), second, me),
                copy(7, (*x_chip, 1 - c), whole, me), copy(8, (*y_chip, 1 - c), whole, me),
                copy(9, (*far_chip, 1 - c), whole, me)])
            mine.append(pltpu.make_async_copy(own(whole), rows(me, whole), local_sems.at[k]))
        return sends, lands, mine

    @staticmethod
    def _then(lands, waits, sends, starts):
        for slot in waits:
            if lands[slot] is not None:
                lands[slot].wait_recv()
        for slot in starts:
            if sends[slot] is not None:
                sends[slot].start()

    def start(self, ins, outs, sems):
        sends, lands, mine = self._plan(ins, outs, sems)
        for cp in mine:
            cp.start()
        for slot in (1, 3, 0, 2, 4):
            for s in sends:
                self._then(None, (), s, (slot,))

    def forward(self, ins, outs, sems):
        sends, lands, _ = self._plan(ins, outs, sems)
        for s, l in zip(sends, lands):
            self._then(l, (1,), s, (5,))
            self._then(l, (3,), s, (6,))

    def middle(self, ins, outs, sems):
        sends, lands, _ = self._plan(ins, outs, sems)
        for s, l in zip(sends, lands):
            self._then(l, (2,), s, (7,))
            self._then(l, (4,), s, (8,))
        for s, l in zip(sends, lands):
            self._then(l, (5, 6), s, (9,))

    def finish(self, ins, outs, sems):
        if self.forward_at is None:
            self.forward(ins, outs, sems)
        if self.middle_at is None:
            self.middle(ins, outs, sems)
        sends, lands, mine = self._plan(ins, outs, sems)
        for s, l in zip(sends, lands):
            self._then(l, (0, 7, 8, 9), s, ())
        for s in sends:
            for cp in s:
                if cp is not None:
                    cp.wait_send()
        for cp in mine:
            cp.wait()


class _ReduceScatter:
    peers = ALL_PEERS

    def __init__(self, parts, bcast=()):
        self.parts = [(lo, cnt) for _, lo, cnt in parts]
        self.n_parts = len(parts)
        self.ins = [a for a, _, _ in parts] + list(bcast)
        self.out_shape = [_sds((N_DEV * cnt, a.shape[1]), a.dtype) for a, _, cnt in parts]
        self.out_shape += [_sds((N_DEV * b.shape[0], b.shape[1]), b.dtype) for b in bcast]
        n = len(self.ins)
        self.sems = [pltpu.SemaphoreType.DMA((7 * n,)), pltpu.SemaphoreType.DMA((7 * n,)),
                     pltpu.SemaphoreType.DMA((n,))]

    def _copies(self, ins, outs, sems):
        send_sems, recv_sems, local_sems = sems
        x, y, c = _my_place()
        me_idx = 4 * x + 2 * y + c
        remote, local = [], []
        for k in range(len(ins)):
            cnt = outs[k].shape[0] // N_DEV
            dst = outs[k].at[pl.ds(pl.multiple_of(me_idx * cnt, 8), cnt), :]
            if k < self.n_parts:
                lo, _ = self.parts[k]
                r = ins[k].shape[0] // N_DEV
                src_of = lambda idx: ins[k].at[pl.ds(pl.multiple_of(idx * r + lo, 8), cnt), :]
            else:
                src_of = lambda idx: ins[k]
            local.append(pltpu.make_async_copy(src_of(me_idx), dst, local_sems.at[k]))
            for j in range(1, N_DEV):
                peer = (x ^ (j >> 2), y ^ ((j >> 1) & 1), c ^ (j & 1))
                peer_idx = 4 * peer[0] + 2 * peer[1] + peer[2]
                remote.append(pltpu.make_async_remote_copy(
                    src_ref=src_of(peer_idx), dst_ref=dst,
                    send_sem=send_sems.at[7 * k + j - 1], recv_sem=recv_sems.at[7 * k + j - 1],
                    device_id=peer, device_id_type=MESH))
        return remote, local

    def start(self, ins, outs, sems):
        _start_exchange(*self._copies(ins, outs, sems))

    def finish(self, ins, outs, sems):
        _finish_exchange(*self._copies(ins, outs, sems))


class _ChipExchange:
    peers = CHIP_PEERS

    def __init__(self, arrays):
        self.ins = list(arrays)
        self.out_shape = [_sds(a.shape, a.dtype) for a in arrays]
        n = len(self.ins)
        self.sems = [pltpu.SemaphoreType.DMA((3 * n,)), pltpu.SemaphoreType.DMA((3 * n,)),
                     pltpu.SemaphoreType.DMA((n,))]

    def _copies(self, ins, outs, sems):
        send_sems, recv_sems, local_sems = sems
        x, y, c = _my_place()
        my_chip = 2 * x + y
        remote, local = [], []
        for k in range(len(ins)):
            r = ins[k].shape[0] // 4
            dst = outs[k].at[pl.ds(pl.multiple_of(my_chip * r, 8), r), :]
            local.append(pltpu.make_async_copy(ins[k].at[pl.ds(pl.multiple_of(my_chip * r, 8), r), :], dst,
                                               local_sems.at[k]))
            for j in range(1, 4):
                px, py = x ^ (j >> 1), y ^ (j & 1)
                src = ins[k].at[pl.ds(pl.multiple_of((2 * px + py) * r, 8), r), :]
                remote.append(pltpu.make_async_remote_copy(
                    src_ref=src, dst_ref=dst, send_sem=send_sems.at[3 * k + j - 1],
                    recv_sem=recv_sems.at[3 * k + j - 1], device_id=(px, py, c), device_id_type=MESH))
        return remote, local

    def start(self, ins, outs, sems):
        _start_exchange(*self._copies(ins, outs, sems))

    def finish(self, ins, outs, sems):
        _finish_exchange(*self._copies(ins, outs, sems))


class _ChipExchangeThenBroadcast(_ChipExchange):
    peers = ALL_PEERS
    defer_start = False

    def __init__(self, arrays, late_from, late_shapes):
        super().__init__(arrays)
        self.n_chip = len(arrays)
        self.late_from = tuple(late_from)
        self.out_shape += [_sds((N_DEV * r, c), F32) for r, c in late_shapes]
        m = len(late_shapes)
        self.sems += [pltpu.SemaphoreType.DMA((7 * m,)), pltpu.SemaphoreType.DMA((7 * m,)),
                      pltpu.SemaphoreType.DMA((m,))]

    def _late_copies(self, srcs, outs, sems):
        send_sems, recv_sems, local_sems = sems
        x, y, c = _my_place()
        me_idx = 4 * x + 2 * y + c
        remote, local = [], []
        for k, src in enumerate(srcs):
            r = src.shape[0]
            dst = outs[k].at[pl.ds(pl.multiple_of(me_idx * r, 8), r), :]
            local.append(pltpu.make_async_copy(src, dst, local_sems.at[k]))
            for j, (dx, dy, dc) in enumerate(ALL_PEERS):
                remote.append(pltpu.make_async_remote_copy(
                    src_ref=src, dst_ref=dst, send_sem=send_sems.at[7 * k + j], recv_sem=recv_sems.at[7 * k + j],
                    device_id=(x ^ dx, y ^ dy, c ^ dc), device_id_type=MESH))
        return remote, local

    def start(self, ins, outs, sems):
        _start_exchange(*self._copies(ins, outs[:self.n_chip], sems[:3]))

    def finish(self, ins, outs, sems, late_srcs):
        late = self._late_copies(late_srcs, outs[self.n_chip:], sems[3:])
        _start_exchange(*late)
        _finish_exchange(*self._copies(ins, outs[:self.n_chip], sems[:3]))
        _finish_exchange(*late)


def _pcall(body, name, grid, in_specs, out_specs, out_shape, args, scratch=(), comm=None):
    params = pltpu.CompilerParams(dimension_semantics=("arbitrary",) * len(grid), vmem_limit_bytes=VMEM_LIMIT)
    in_specs, out_specs, out_shape, scratch = list(in_specs), list(out_specs), list(out_shape), list(scratch)
    if comm is None:
        res = pl.pallas_call(body, name=name, grid=grid, in_specs=in_specs, out_specs=out_specs, out_shape=out_shape,
                             scratch_shapes=scratch, compiler_params=params)(*args)
        return list(res), []
    n_in, n_out, n_scr = len(in_specs), len(out_specs), len(scratch)
    ci, co = len(comm.ins), len(comm.out_shape)
    total = math.prod(grid)

    def carried(*refs):
        bounds = [0, n_in, n_in + ci, n_in + ci + n_out, n_in + ci + n_out + co, n_in + ci + n_out + co + n_scr]
        ins, cins, outs, couts, scr = (refs[a:b] for a, b in zip(bounds[:-1], bounds[1:]))
        sems = refs[bounds[-1]:]
        step = pl.program_id(0)
        for d in range(1, len(grid)):
            step = step * grid[d] + pl.program_id(d)

        start_step = min(1, total - 1) if getattr(comm, "defer_start", True) else 0

        @pl.when(step == 0)
        def _():
            _barrier_signal(comm.peers)

        @pl.when(step == start_step)
        def _():
            _barrier_wait(comm.peers)
            comm.start(cins, couts, sems)

        forward_at = getattr(comm, "forward_at", None)
        if forward_at is not None and int(forward_at * total) <= start_step:
            forward_at = comm.forward_at = comm.middle_at = None
        if forward_at is not None:
            @pl.when(step == int(forward_at * total))
            def _():
                comm.forward(cins, couts, sems)

        middle_at = getattr(comm, "middle_at", None)
        if middle_at is not None:
            assert forward_at is None or forward_at <= middle_at
            @pl.when(step == int(middle_at * total))
            def _():
                comm.middle(cins, couts, sems)

        body(*ins, *outs, *scr)

        @pl.when(step == total - 1)
        def _():
            late_from = getattr(comm, "late_from", None)
            if late_from is None:
                comm.finish(cins, couts, sems)
            else:
                comm.finish(cins, couts, sems, [outs[k] for k in late_from])

    params = pltpu.CompilerParams(dimension_semantics=("arbitrary",) * len(grid), vmem_limit_bytes=VMEM_LIMIT,
                                  collective_id=BARRIER_ID[comm.peers])
    res = pl.pallas_call(
        carried, name=name, grid=grid, in_specs=in_specs + [ANY] * ci, out_specs=out_specs + [ANY] * co,
        out_shape=out_shape + comm.out_shape, scratch_shapes=scratch + comm.sems, compiler_params=params,
    )(*args, *comm.ins)
    return list(res[:n_out]), list(res[n_out:])


def _exchange_only(comm, name):
    def body(*refs):
        ci, co = len(comm.ins), len(comm.out_shape)
        _barrier_signal(comm.peers)
        _barrier_wait(comm.peers)
        comm.start(refs[:ci], refs[ci:ci + co], refs[ci + co:])
        comm.finish(refs[:ci], refs[ci:ci + co], refs[ci + co:])

    params = pltpu.CompilerParams(collective_id=BARRIER_ID[comm.peers])
    return pl.pallas_call(body, name=name, out_shape=comm.out_shape, in_specs=[ANY] * len(comm.ins),
                          out_specs=[ANY] * len(comm.out_shape), scratch_shapes=comm.sems,
                          compiler_params=params)(*comm.ins)


def _norm_inproj(x, g, win_t, b_in, comm):
    s = x.shape[0]
    tm = _row_tile(s, 512)
    widths = (QKV_W, CBX_W, GATE_W)

    def body(x_ref, g_ref, w_ref, b_ref, xn_ref, qkv_ref, cbx_ref, gate_ref):
        xv = x_ref[...]
        r = lax.rsqrt(jnp.mean(xv * xv, axis=-1, keepdims=True) + NORM_EPS)
        xn = (xv * r * g_ref[...]).astype(BF16)
        xn_ref[...] = xn
        off = 0
        for o_ref, w in zip((qkv_ref, cbx_ref, gate_ref), widths):
            acc = lax.dot_general(xn, w_ref[off:off + w, :], NT, preferred_element_type=F32)
            o_ref[...] = (acc + b_ref[:, off:off + w]).astype(BF16)
            off += w

    return _pcall(
        body, "norm_inproj", (s // tm,),
        [_rows(tm, D_MODEL), _full((1, D_MODEL)), _resident((IN_W, D_MODEL)), _full((1, IN_W))],
        [_rows(tm, D_MODEL)] + [_rows(tm, w) for w in widths],
        [_sds((s, D_MODEL), BF16)] + [_sds((s, w), BF16) for w in widths],
        (x, g, win_t, b_in), comm=comm)


Q_BLOCKS = 4


def _attn_specs():
    tq = Q_BLOCKS * BLOCK
    prev = lambda n: jnp.maximum(Q_BLOCKS * n - 1, 0)
    return [pl.BlockSpec((tq, ATTN_W), lambda n: (n, 0)),
            pl.BlockSpec((BLOCK, KV_W), lambda n: (prev(n), ATTN_W // KV_W)),
            pl.BlockSpec((tq, KV_W), lambda n: (n, ATTN_W // KV_W)),
            pl.BlockSpec((BLOCK, KV_W), lambda n: (prev(n), ATTN_W // KV_W + 1)),
            pl.BlockSpec((tq, KV_W), lambda n: (n, ATTN_W // KV_W + 1))]


def _window(prev_ref, cur_ref, sub):
    if sub == 0:
        return jnp.concatenate([prev_ref[...], cur_ref[0:BLOCK, :]], axis=0)
    return cur_ref[(sub - 1) * BLOCK:(sub + 1) * BLOCK, :]


def _lower_lanes():
    return lax.broadcasted_iota(jnp.int32, (BLOCK, 128), 1) < HEAD_DIM


def _stack_heads(val, kh):
    lower = _lower_lanes()
    parts = []
    for g in range(4):
        h = kh * 4 + g
        blk = val[:, (h // 2) * 128:(h // 2 + 1) * 128]
        keep = lower if h % 2 == 0 else jnp.logical_not(lower)
        parts.append(jnp.where(keep, blk, jnp.zeros_like(blk)))
    return jnp.concatenate(parts, axis=0)


def _dup_kv(window, kh):
    t = window.astype(F32)
    rolled = pltpu.roll(t, HEAD_DIM, axis=1)
    lower = lax.broadcasted_iota(jnp.int32, t.shape, 1) < HEAD_DIM
    dup = jnp.where(lower, t, rolled) if kh == 0 else jnp.where(lower, rolled, t)
    return dup.astype(BF16)


def _attn_mask(real_prev):
    row = lax.broadcasted_iota(jnp.int32, (4 * BLOCK, 2 * BLOCK), 0)
    kj = lax.broadcasted_iota(jnp.int32, (4 * BLOCK, 2 * BLOCK), 1)
    dist = (row & (BLOCK - 1)) + BLOCK - kj
    band = jnp.logical_and(dist >= 0, dist < BLOCK)
    return jnp.logical_and(band, jnp.logical_or(kj >= BLOCK, real_prev))


def _sink_col(sinks_ref, kh):
    gi = lax.broadcasted_iota(jnp.int32, (4 * BLOCK, 1), 0) // BLOCK
    col = jnp.zeros((4 * BLOCK, 1), F32)
    for g in range(4):
        col = jnp.where(gi == g, sinks_ref[0, kh * 4 + g], col)
    return col


def _attn_fwd(qkv, sinks, comm):
    s = qkv.shape[0]
    tq = Q_BLOCKS * BLOCK

    def body(sinks_ref, q_ref, kp_ref, kc_ref, vp_ref, vc_ref, o_ref, lse_ref):
        n = pl.program_id(0)
        lower = _lower_lanes()
        lane = lax.broadcasted_iota(jnp.int32, (BLOCK, 128), 1)
        for sub in range(Q_BLOCKS):
            rows = slice(sub * BLOCK, (sub + 1) * BLOCK)
            mask = _attn_mask(n > 0 if sub == 0 else True)
            kw, vw = _window(kp_ref, kc_ref, sub), _window(vp_ref, vc_ref, sub)
            qv = q_ref[rows, :]
            lse_out = jnp.zeros((BLOCK, 128), F32)
            for kh in range(2):
                qs = _stack_heads(qv, kh)
                kd, vd = _dup_kv(kw, kh), _dup_kv(vw, kh)
                sc = lax.dot_general(qs, kd, NT, preferred_element_type=F32) * ATTN_SCALE
                sc = jnp.where(mask, sc, NEG)
                sink = _sink_col(sinks_ref, kh)
                m = jnp.maximum(jnp.max(sc, axis=1, keepdims=True), sink)
                p = jnp.exp(sc - m)
                l = jnp.sum(p, axis=1, keepdims=True) + jnp.exp(sink - m)
                o = jnp.dot(p.astype(BF16), vd, preferred_element_type=F32) / l
                lse = m + jnp.log(l)
                for pair in range(2):
                    lo = o[(2 * pair) * BLOCK:(2 * pair + 1) * BLOCK]
                    hi = o[(2 * pair + 1) * BLOCK:(2 * pair + 2) * BLOCK]
                    col = (kh * 2 + pair) * 128
                    o_ref[rows, col:col + 128] = jnp.where(lower, lo, hi).astype(BF16)
                for g in range(4):
                    lse_out = jnp.where(lane == kh * 4 + g, lse[g * BLOCK:(g + 1) * BLOCK], lse_out)
            lse_ref[rows, :] = lse_out

    return _pcall(
        body, "attn_fwd", (s // tq,),
        [pl.BlockSpec(memory_space=pltpu.SMEM)] + _attn_specs(),
        [pl.BlockSpec((tq, ATTN_W), lambda n: (n, 0)), pl.BlockSpec((tq, 128), lambda n: (n, 0))],
        [_sds((s, ATTN_W), BF16), _sds((s, 128), F32)],
        (sinks, qkv, qkv, qkv, qkv, qkv), comm=comm)


def _conv_u(cbx_ref, halo_ref, w_ref, first):
    cb = cbx_ref[:, 0:CONV_W].astype(F32)
    cc = cbx_ref[:, CONV_W:2 * CONV_W].astype(F32)
    cx = cbx_ref[:, 2 * CONV_W:3 * CONV_W].astype(F32)
    u = cc * cx
    uh = halo_ref[:, CONV_W:2 * CONV_W].astype(F32) * halo_ref[:, 2 * CONV_W:3 * CONV_W].astype(F32)
    uh = jnp.where(first, 0.0, uh)
    u1, u2 = _shifts_down(u, uh, (1, 2))
    cv = w_ref[0:1, :] * u2 + w_ref[1:2, :] * u1 + w_ref[2:3, :] * u
    return cb, cc, cx, u, cv


def _mix_fwd(x, cbx, gates, attn, conv_w, wa, wc, wout, comm):
    s = x.shape[0]
    tm = _row_tile(s)

    def body(x_ref, cbx_ref, halo_ref, gate_ref, attn_ref, cw_ref, wa_ref, wc_ref, wo_ref,
             h1_ref):
        first = pl.program_id(0) == 0
        cb, _, _, _, cv = _conv_u(cbx_ref, halo_ref, cw_ref, first)
        conv = (cb * cv).astype(BF16)
        ap = jnp.dot(attn_ref[...], wa_ref[...], preferred_element_type=F32)
        cp = jnp.dot(conv, wc_ref[...], preferred_element_type=F32)
        ga = gate_ref[:, 0:D_MODEL].astype(F32)
        gc = gate_ref[:, D_MODEL:2 * D_MODEL].astype(F32)
        merged = (_sig(ga) * ap + _sig(gc) * cp).astype(BF16)
        h1_ref[...] = x_ref[...] + jnp.dot(merged, wo_ref[...], preferred_element_type=F32)

    return _pcall(
        body, "mix_fwd", (s // tm,),
        [_rows(tm, D_MODEL), _rows(tm, CBX_W), pl.BlockSpec((HALO, CBX_W), _prev_halo_map(tm)),
         _rows(tm, GATE_W), _rows(tm, ATTN_W), _full((3, CONV_W)), _full((ATTN_W, D_MODEL)),
         _full((CONV_W, D_MODEL)), _full((D_MODEL, D_MODEL))],
        [_rows(tm, D_MODEL)], [_sds((s, D_MODEL), F32)],
        (x, cbx, cbx, gates, attn, conv_w, wa, wc, wout), comm=comm)


def _col_offsets(wup_parts):
    widths = [p.shape[1] for p in wup_parts]
    assert sum(widths) == D_MODEL
    return [(sum(widths[:k]), w) for k, w in enumerate(widths)]


def _ffn_up(h1, g, wup_parts, fcw, comm):
    s = h1.shape[0]
    tm = _row_tile(s)
    cols = _col_offsets(wup_parts)

    def body(h_ref, g_ref, *refs):
        w_refs = refs[:len(cols)]
        fcw_ref, hn_ref, pre_ref, up_ref, carry_ref = refs[len(cols):]

        @pl.when(pl.program_id(0) == 0)
        def _():
            carry_ref[...] = jnp.zeros_like(carry_ref)

        hv = h_ref[...]
        r = lax.rsqrt(jnp.mean(hv * hv, axis=-1, keepdims=True) + NORM_EPS)
        hn = (hv * r * g_ref[...]).astype(BF16)
        hn_ref[...] = hn
        for c in range(2 * D_FF // FF_CHUNK):
            sl = slice(c * FF_CHUNK, (c + 1) * FF_CHUNK)
            acc = None
            for w_ref, (off, w) in zip(w_refs, cols):
                part = lax.dot_general(hn[:, off:off + w], w_ref[sl, :], NT, preferred_element_type=F32)
                acc = part if acc is None else acc + part
            pre_ref[:, sl] = acc.astype(BF16)
            halo = carry_ref[:, sl]
            carry_ref[:, sl] = acc[tm - HALO:, :]
            u1, u2 = _shifts_down(acc, halo, (1, 2))
            w = fcw_ref[:, sl]
            up_ref[:, sl] = (w[0:1] * u2 + w[1:2] * u1 + w[2:3] * acc).astype(BF16)

    return _pcall(
        body, "ffn_up", (s // tm,),
        [_rows(tm, D_MODEL), _full((1, D_MODEL))] + [_resident((2 * D_FF, w)) for _, w in cols]
        + [_full((3, 2 * D_FF))],
        [_rows(tm, D_MODEL), _rows(tm, 2 * D_FF), _rows(tm, 2 * D_FF)],
        [_sds((s, D_MODEL), BF16), _sds((s, 2 * D_FF), BF16), _sds((s, 2 * D_FF), BF16)],
        (h1, g, *wup_parts, fcw), scratch=[pltpu.VMEM((HALO, 2 * D_FF), F32)], comm=comm)


def _ffn_down_loss(up, wdown, h1, fnorm, target):
    s = h1.shape[0]
    tm = _row_tile(s)

    steps = s // tm
    assert steps >= 2

    def body(up_hbm, wd_ref, h1_ref, fn_ref, t_ref, act_ref, dh2_ref, loss_ref, dfn_ref, ring_ref, sems):
        i = pl.program_id(0)

        def fetch(step):
            slot = step % 3
            return pltpu.make_async_copy(up_hbm.at[pl.ds(pl.multiple_of(step * tm, tm), tm), :],
                                         ring_ref.at[slot], sems.at[slot])

        @pl.when(i == 0)
        def _():
            fetch(0).start()
            fetch(1).start()
            loss_ref[...] = jnp.zeros_like(loss_ref)
            dfn_ref[...] = jnp.zeros_like(dfn_ref)

        @pl.when(i + 2 < steps)
        def _():
            fetch(i + 2).start()

        fetch(i).wait()
        up_ref = ring_ref.at[i % 3]
        h2 = h1_ref[...]
        for c in range(D_FF // FF_CHUNK):
            gsl = slice(c * FF_CHUNK, (c + 1) * FF_CHUNK)
            vsl = slice(D_FF + c * FF_CHUNK, D_FF + (c + 1) * FF_CHUNK)
            gate = up_ref[:, gsl].astype(F32)
            val = up_ref[:, vsl].astype(F32)
            act = (gate * _sig(gate) * val).astype(BF16)
            act_ref[:, gsl] = act
            h2 = h2 + jnp.dot(act, wd_ref[gsl, :], preferred_element_type=F32)
        r = lax.rsqrt(jnp.mean(h2 * h2, axis=-1, keepdims=True) + NORM_EPS)
        yhat = h2 * r
        fn = fn_ref[...]
        diff = yhat * fn - t_ref[...]
        loss_ref[...] += 0.5 * jnp.sum(jnp.sum(diff * diff, axis=1, keepdims=True), axis=0, keepdims=True) / D_MODEL
        dy = diff * (1.0 / D_MODEL)
        dfn_ref[...] += jnp.sum(dy * yhat, axis=0, keepdims=True)
        dyh = dy * fn
        dh2_ref[...] = r * (dyh - yhat * jnp.mean(dyh * yhat, axis=-1, keepdims=True))

    return _pcall(
        body, "ffn_down_loss", (s // tm,),
        [ANY, _resident((D_FF, D_MODEL)), _rows(tm, D_MODEL), _full((1, D_MODEL)), _rows(tm, D_MODEL)],
        [_rows(tm, D_FF), _rows(tm, D_MODEL), _full((1, 128)), _full((1, D_MODEL))],
        [_sds((s, D_FF), BF16), _sds((s, D_MODEL), F32), _sds((1, 128), F32), _sds((1, D_MODEL), F32)],
        (up, wdown, h1, fnorm, target),
        scratch=[pltpu.VMEM((3, tm, 2 * D_FF), BF16), pltpu.SemaphoreType.DMA((3,))])[0]


def _ffn_bwd(dh2, wdown, up, up_pre, fcw, wup_parts, h1, g, comm):
    s = dh2.shape[0]
    tm = _row_tile(s)
    cols = _col_offsets(wup_parts)

    chunk = FF_GRAD_ROWS

    def dup_cols(dh, up_ref, wd_ref, c):
        gsl = slice(c * chunk, (c + 1) * chunk)
        vsl = slice(D_FF + c * chunk, D_FF + (c + 1) * chunk)
        dact = lax.dot_general(dh, wd_ref[gsl, :], NT, preferred_element_type=F32)
        gate = up_ref[:, gsl].astype(F32)
        val = up_ref[:, vsl].astype(F32)
        sg = _sig(gate)
        return dact * val * (sg * (1.0 + gate * (1.0 - sg))), dact * gate * sg

    def body(dh_ref, wd_ref, up_ref, x_ref, w_ref, *refs):
        wup_refs = refs[:len(cols)]
        h_ref, g_ref, dx_ref, dw_ref, dh1_ref, dg_ref, carry_ref = refs[len(cols):]

        @pl.when(pl.program_id(0) == 0)
        def _():
            dw_ref[...] = jnp.zeros_like(dw_ref)
            dg_ref[...] = jnp.zeros_like(dg_ref)
            carry_ref[...] = jnp.zeros_like(carry_ref)

        dh2v = dh_ref[...]
        dh = dh2v.astype(BF16)
        dhn = [jnp.zeros((tm, w), F32) for _, w in cols]
        for c in range(D_FF // chunk):
            for d, off in zip(dup_cols(dh, up_ref, wd_ref, c), (c * chunk, D_FF + c * chunk)):
                sl = slice(off, off + chunk)
                dn = carry_ref[:, sl]
                carry_ref[:, sl] = d[0:HALO, :]
                xv = x_ref[:, sl].astype(F32)
                wv = w_ref[:, sl]
                d1, d2 = _shifts_up(d, dn, (1, 2))
                dx = (wv[2:3] * d + wv[1:2] * d1 + wv[0:1] * d2).astype(BF16)
                dx_ref[:, sl] = dx
                dhn = [a + jnp.dot(dx, wup_ref[sl, :], preferred_element_type=F32)
                       for a, wup_ref in zip(dhn, wup_refs)]
                dw_ref[0:1, sl] += jnp.sum(d2 * xv, axis=0, keepdims=True)
                dw_ref[1:2, sl] += jnp.sum(d1 * xv, axis=0, keepdims=True)
                dw_ref[2:3, sl] += jnp.sum(d * xv, axis=0, keepdims=True)
        dx1, dg = _norm_bwd_tile(h_ref[...], g_ref[...], jnp.concatenate(dhn, axis=1))
        dg_ref[...] += dg
        dh1_ref[...] = dh2v + dx1

    rows = lambda c: _rows_reversed(tm, c, s // tm)
    return _pcall(
        body, "ffn_bwd", (s // tm,),
        [rows(D_MODEL), _resident((D_FF, D_MODEL)), rows(2 * D_FF), rows(2 * D_FF), _full((3, 2 * D_FF))]
        + [_resident((2 * D_FF, w)) for _, w in cols] + [rows(D_MODEL), _full((1, D_MODEL))],
        [rows(2 * D_FF), _full((3, 2 * D_FF)), rows(D_MODEL), _full((1, D_MODEL))],
        [_sds((s, 2 * D_FF), BF16), _sds((3, 2 * D_FF), F32), _sds((s, D_MODEL), F32), _sds((1, D_MODEL), F32)],
        (dh2, wdown, up, up_pre, fcw, *wup_parts, h1, g),
        scratch=[pltpu.VMEM((HALO, 2 * D_FF), F32)], comm=comm)


def _matmul_tn(a, b, tk, name, ts=1024, comm=None):
    s, ka = a.shape
    n = b.shape[1]
    ts = min(ts, s)
    steps = s // ts

    def body(a_ref, b_ref, o_ref, acc_ref):
        j = pl.program_id(1)

        @pl.when(j == 0)
        def _():
            acc_ref[...] = jnp.zeros_like(acc_ref)

        acc_ref[...] += lax.dot_general(a_ref[...].astype(BF16), b_ref[...].astype(BF16), TN,
                                        preferred_element_type=F32)

        @pl.when(j == steps - 1)
        def _():
            o_ref[...] = acc_ref[...].astype(BF16)

    outs, couts = _pcall(
        body, name, (ka // tk, steps),
        [pl.BlockSpec((ts, tk), lambda i, j: (j, i)), pl.BlockSpec((ts, n), lambda i, j: (j, 0))],
        [pl.BlockSpec((tk, n), lambda i, j: (i, 0))], [_sds((ka, n), BF16)],
        (a, b), scratch=[pltpu.VMEM((tk, n), F32)], comm=comm)
    return outs[0] if comm is None else (outs[0], couts)


def _norm_bwd_tile(xv, g, dy):
    r = lax.rsqrt(jnp.mean(xv * xv, axis=-1, keepdims=True) + NORM_EPS)
    xhat = xv * r
    dg = jnp.sum(dy * xhat, axis=0, keepdims=True)
    dyh = dy * g
    return r * (dyh - xhat * jnp.mean(dyh * xhat, axis=-1, keepdims=True)), dg


def _mix_bwd(dh1, wout, gates, attn, wa, wc, cbx, conv_w, comm):
    s = dh1.shape[0]
    tm = _row_tile(s)
    steps = s // tm

    def body(dh_ref, wo_ref, gate_ref, attn_ref, wa_ref, wc_ref, cbx_ref, halo_ref,
             cw_ref, dg_ref, dattn_ref, dcb_ref, dcc_ref, dcx_ref, dw_ref, gwo_ref, gwa_ref, gwc_ref,
             acc_o, acc_a, acc_c, carry_ref):
        i = pl.program_id(0)

        @pl.when(i == 0)
        def _():
            dw_ref[...] = jnp.zeros_like(dw_ref)
            acc_o[...] = jnp.zeros_like(acc_o)
            acc_a[...] = jnp.zeros_like(acc_a)
            acc_c[...] = jnp.zeros_like(acc_c)
            carry_ref[...] = jnp.zeros_like(carry_ref)

        cb, cc, cx, u, cv = _conv_u(cbx_ref, halo_ref, cw_ref, i == steps - 1)
        attn = attn_ref[...]
        conv = (cb * cv).astype(BF16)
        ap = jnp.dot(attn, wa_ref[...], preferred_element_type=F32)
        cp = jnp.dot(conv, wc_ref[...], preferred_element_type=F32)
        dhb = dh_ref[...].astype(BF16)
        dm = lax.dot_general(dhb, wo_ref[...], NT, preferred_element_type=F32)
        sa = _sig(gate_ref[:, 0:D_MODEL].astype(F32))
        sc = _sig(gate_ref[:, D_MODEL:2 * D_MODEL].astype(F32))
        merged = (sa * ap + sc * cp).astype(BF16)
        da = (dm * sa).astype(BF16)
        dc = (dm * sc).astype(BF16)
        dg_ref[:, 0:D_MODEL] = (dm * ap * sa * (1.0 - sa)).astype(BF16)
        dg_ref[:, D_MODEL:2 * D_MODEL] = (dm * cp * sc * (1.0 - sc)).astype(BF16)
        dattn_ref[...] = lax.dot_general(da, wa_ref[...], NT, preferred_element_type=F32).astype(BF16)
        dconv = lax.dot_general(dc, wc_ref[...], NT, preferred_element_type=F32)
        dcb_ref[...] = (dconv * cv).astype(BF16)
        d = dconv * cb
        dn = carry_ref[...]
        carry_ref[...] = d[0:HALO, :]
        d1, d2 = _shifts_up(d, dn, (1, 2))
        du = cw_ref[2:3, :] * d + cw_ref[1:2, :] * d1 + cw_ref[0:1, :] * d2
        dcc_ref[...] = (du * cx).astype(BF16)
        dcx_ref[...] = (du * cc).astype(BF16)
        dw_ref[0:1, :] += jnp.sum(d2 * u, axis=0, keepdims=True)
        dw_ref[1:2, :] += jnp.sum(d1 * u, axis=0, keepdims=True)
        dw_ref[2:3, :] += jnp.sum(d * u, axis=0, keepdims=True)
        acc_o[...] += lax.dot_general(merged, dhb, TN, preferred_element_type=F32)
        acc_a[...] += lax.dot_general(attn, da, TN, preferred_element_type=F32)
        acc_c[...] += lax.dot_general(conv, dc, TN, preferred_element_type=F32)

        @pl.when(i == steps - 1)
        def _():
            gwo_ref[...] = acc_o[...].astype(BF16)
            gwa_ref[...] = acc_a[...].astype(BF16)
            gwc_ref[...] = acc_c[...].astype(BF16)

    rows = lambda c: _rows_reversed(tm, c, steps)
    return _pcall(
        body, "mix_bwd", (steps,),
        [rows(D_MODEL), _full((D_MODEL, D_MODEL)), rows(GATE_W), rows(ATTN_W), _full((ATTN_W, D_MODEL)),
         _full((CONV_W, D_MODEL)), rows(CBX_W), pl.BlockSpec((HALO, CBX_W), _prev_halo_map_reversed(tm, steps)),
         _full((3, CONV_W))],
        [rows(GATE_W), rows(ATTN_W), rows(CONV_W), rows(CONV_W), rows(CONV_W),
         _full((3, CONV_W)), _full((D_MODEL, D_MODEL)), _full((ATTN_W, D_MODEL)), _full((CONV_W, D_MODEL))],
        [_sds((s, GATE_W), BF16), _sds((s, ATTN_W), BF16), _sds((s, CONV_W), BF16), _sds((s, CONV_W), BF16),
         _sds((s, CONV_W), BF16), _sds((3, CONV_W), F32), _sds((D_MODEL, D_MODEL), BF16),
         _sds((ATTN_W, D_MODEL), BF16), _sds((CONV_W, D_MODEL), BF16)],
        (dh1, wout, gates, attn, wa, wc, cbx, cbx, conv_w),
        scratch=[pltpu.VMEM((D_MODEL, D_MODEL), F32), pltpu.VMEM((ATTN_W, D_MODEL), F32),
                 pltpu.VMEM((CONV_W, D_MODEL), F32), pltpu.VMEM((HALO, CONV_W), F32)], comm=comm)


def _attn_bwd(qkv, sinks, attn, lse, dattn, comm):
    s = qkv.shape[0]
    tq = Q_BLOCKS * BLOCK

    def body(sinks_ref, q_ref, kp_ref, kc_ref, vp_ref, vc_ref, o_ref, lse_ref, do_ref,
             dq_ref, dk_ref, dv_ref, ds_ref):
        n = pl.program_id(0)

        @pl.when(n == 0)
        def _():
            dk_ref[...] = jnp.zeros_like(dk_ref)
            dv_ref[...] = jnp.zeros_like(dv_ref)
            ds_ref[...] = jnp.zeros_like(ds_ref)

        lower = _lower_lanes()
        lane = lax.broadcasted_iota(jnp.int32, (BLOCK, 128), 1)
        lower2 = lax.broadcasted_iota(jnp.int32, (2 * BLOCK, 128), 1) < HEAD_DIM
        lane1 = lax.broadcasted_iota(jnp.int32, (1, 128), 1)
        dsink = jnp.zeros((1, 128), F32)
        for sub in reversed(range(Q_BLOCKS)):
            rows = slice(sub * BLOCK, (sub + 1) * BLOCK)
            mask = _attn_mask(n > 0 if sub == 0 else True)
            kw, vw = _window(kp_ref, kc_ref, sub), _window(vp_ref, vc_ref, sub)
            qv, ov, dov, lsev = q_ref[rows, :], o_ref[rows, :], do_ref[rows, :], lse_ref[rows, :]
            dk_fold, dv_fold = [], []
            for kh in range(2):
                qs = _stack_heads(qv, kh)
                dos = _stack_heads(dov, kh)
                os_ = _stack_heads(ov, kh)
                kd, vd = _dup_kv(kw, kh), _dup_kv(vw, kh)
                lse = jnp.concatenate(
                    [jnp.sum(jnp.where(lane == kh * 4 + g, lsev, 0.0), axis=1, keepdims=True) for g in range(4)],
                    axis=0)
                sc = lax.dot_general(qs, kd, NT, preferred_element_type=F32) * ATTN_SCALE
                p = jnp.exp(jnp.where(mask, sc, NEG) - lse)
                dp = lax.dot_general(dos, vd, NT, preferred_element_type=F32)
                delta = jnp.sum(dos.astype(F32) * os_.astype(F32), axis=1, keepdims=True)
                dsc = (p * (dp - delta) * ATTN_SCALE).astype(BF16)
                dqs = jnp.dot(dsc, kd, preferred_element_type=F32)
                for pair in range(2):
                    lo = dqs[(2 * pair) * BLOCK:(2 * pair + 1) * BLOCK]
                    hi = dqs[(2 * pair + 1) * BLOCK:(2 * pair + 2) * BLOCK]
                    col = (kh * 2 + pair) * 128
                    dq_ref[rows, col:col + 128] = jnp.where(lower, lo, hi).astype(BF16)
                dkd = lax.dot_general(dsc, qs, TN, preferred_element_type=F32)
                dvd = lax.dot_general(p.astype(BF16), dos, TN, preferred_element_type=F32)
                dk_fold.append(dkd + pltpu.roll(dkd, HEAD_DIM, axis=1))
                dv_fold.append(dvd + pltpu.roll(dvd, HEAD_DIM, axis=1))
                psink = jnp.exp(_sink_col(sinks_ref, kh) - lse) * delta
                for g in range(4):
                    tot = jnp.sum(psink[g * BLOCK:(g + 1) * BLOCK], axis=0, keepdims=True)
                    dsink = dsink - jnp.where(lane1 == kh * 4 + g, tot, 0.0)
            dk2 = jnp.where(lower2, dk_fold[0], dk_fold[1])
            dv2 = jnp.where(lower2, dv_fold[0], dv_fold[1])
            cur = pl.ds(pl.multiple_of((Q_BLOCKS * n + sub) * BLOCK, BLOCK), BLOCK)
            dk_ref[cur, :] += dk2[BLOCK:]
            dv_ref[cur, :] += dv2[BLOCK:]
            if sub > 0:
                prev = pl.ds(pl.multiple_of((Q_BLOCKS * n + sub - 1) * BLOCK, BLOCK), BLOCK)
                dk_ref[prev, :] += dk2[:BLOCK]
                dv_ref[prev, :] += dv2[:BLOCK]
        ds_ref[...] += dsink

        @pl.when(n > 0)
        def _():
            prev = pl.ds(pl.multiple_of((Q_BLOCKS * n - 1) * BLOCK, BLOCK), BLOCK)
            dk_ref[prev, :] += dk2[:BLOCK]
            dv_ref[prev, :] += dv2[:BLOCK]

    blk = lambda w: pl.BlockSpec((tq, w), lambda n: (n, 0))
    return _pcall(
        body, "attn_bwd", (s // tq,),
        [pl.BlockSpec(memory_space=pltpu.SMEM)] + _attn_specs() + [blk(ATTN_W), blk(128), blk(ATTN_W)],
        [blk(ATTN_W), _full((s, KV_W)), _full((s, KV_W)), _full((1, 128))],
        [_sds((s, ATTN_W), BF16), _sds((s, KV_W), F32), _sds((s, KV_W), F32), _sds((1, 128), F32)],
        (sinks, qkv, qkv, qkv, qkv, qkv, attn, lse, dattn), comm=comm)


DPROJ_PIECES = (ATTN_W, KV_W, KV_W, CONV_W, CONV_W, CONV_W, GATE_W)
DPROJ_OFFSETS = tuple(sum(DPROJ_PIECES[:k]) for k in range(len(DPROJ_PIECES)))


def _grad_w_in(pieces, xn, comm):
    s = xn.shape[0]
    ts = min(1024, s)
    steps = s // ts
    rows0 = DPROJ_OFFSETS[6]

    def body(*refs):
        p_refs, b_ref, o_ref, acc_ref, stage_ref, sem = refs[:7], refs[7], refs[8], refs[9], refs[10], refs[11]
        i, j = pl.program_id(0), pl.program_id(1)

        @pl.when(j == 0)
        def _():
            acc_ref[...] = jnp.zeros_like(acc_ref)

        bv = b_ref[...]

        def flush(lo, n):
            stage_ref[0:n, :] = acc_ref[0:n, :].astype(BF16)
            cp = pltpu.make_async_copy(stage_ref.at[0:n, :], o_ref.at[lo:lo + n, :], sem)
            cp.start()
            cp.wait()

        @pl.when(i == 0)
        def _():
            for p_ref, off, w in zip(p_refs[:6], DPROJ_OFFSETS[:6], DPROJ_PIECES[:6]):
                acc_ref[off:off + w, :] += lax.dot_general(p_ref[...].astype(BF16), bv, TN,
                                                           preferred_element_type=F32)

            @pl.when(j == steps - 1)
            def _():
                flush(0, rows0)

        @pl.when(i == 1)
        def _():
            acc_ref[0:GATE_W, :] += lax.dot_general(p_refs[6][...], bv, TN, preferred_element_type=F32)

            @pl.when(j == steps - 1)
            def _():
                flush(rows0, GATE_W)

    def piece_spec(w, group):
        return pl.BlockSpec((ts, w), lambda i, j: (jnp.where(i == group, j, 0), 0))

    outs, couts = _pcall(
        body, "grad_w_in", (2, steps),
        [piece_spec(w, 0) for w in DPROJ_PIECES[:6]] + [piece_spec(GATE_W, 1),
                                                         pl.BlockSpec((ts, D_MODEL), lambda i, j: (j, 0))],
        [ANY], [_sds((IN_W, D_MODEL), BF16)], (*pieces, xn),
        scratch=[pltpu.VMEM((rows0, D_MODEL), F32), pltpu.VMEM((rows0, D_MODEL), BF16), pltpu.SemaphoreType.DMA],
        comm=comm)
    return outs[0], couts


def _inproj_bwd(pieces, win_t, x, g, dh1, comm):
    s = x.shape[0]
    tm = _row_tile(s, 512)

    def body(*refs):
        p_refs = refs[:7]
        w_ref, x_ref, g_ref, dh_ref, dx_ref, db_ref, dg_ref = refs[7:]

        @pl.when(pl.program_id(0) == 0)
        def _():
            db_ref[...] = jnp.zeros_like(db_ref)
            dg_ref[...] = jnp.zeros_like(dg_ref)

        dxn = jnp.zeros((tm, D_MODEL), F32)
        for p_ref, off, w in zip(p_refs, DPROJ_OFFSETS, DPROJ_PIECES):
            v = p_ref[...].astype(BF16)
            db_ref[:, off:off + w] += jnp.sum(v.astype(F32), axis=0, keepdims=True)
            dxn = dxn + jnp.dot(v, w_ref[off:off + w, :], preferred_element_type=F32)
        dx, dg = _norm_bwd_tile(x_ref[...], g_ref[...], dxn)
        dg_ref[...] += dg
        dx_ref[...] = dh_ref[...] + dx

    return _pcall(
        body, "inproj_bwd", (s // tm,),
        [_rows(tm, w) for w in DPROJ_PIECES] + [_resident((IN_W, D_MODEL)), _rows(tm, D_MODEL), _full((1, D_MODEL)),
                                                _rows(tm, D_MODEL)],
        [_rows(tm, D_MODEL), _full((8, IN_W)), _full((8, D_MODEL))],
        [_sds((s, D_MODEL), F32), _sds((8, IN_W), F32), _sds((8, D_MODEL), F32)],
        (*pieces, win_t, x, g, dh1), comm=comm)


def _adam_math(w, g, m, v):
    m2 = ADAM_B1 * m + (1.0 - ADAM_B1) * g
    v2 = ADAM_B2 * v + (1.0 - ADAM_B2) * (g * g)
    m_hat = m2 / (1.0 - ADAM_B1 ** ADAM_STEP)
    v_hat = v2 / (1.0 - ADAM_B2 ** ADAM_STEP)
    delta = -ADAM_LR * (m_hat / (jnp.sqrt(v_hat) + ADAM_EPS) + ADAM_WD * w)
    return delta, m2, v2


def _sum_slots(ref):
    tot = ref[0].astype(F32)
    for i in range(1, ref.shape[0]):
        tot = tot + ref[i].astype(F32)
    return tot


def _pair_sum(partials, name):
    r, c = partials.shape[0] // N_DEV, partials.shape[1]
    core = lax.axis_index("c").astype(jnp.int32).reshape(1)

    def body(core_ref, mine_ref, all_ref, o_ref, theirs_ref, send_sems, recv_sems):
        k = pl.program_id(0)
        x, y, cc = _my_place()

        def copy(chip):
            return pltpu.make_async_remote_copy(
                src_ref=all_ref.at[pl.ds(pl.multiple_of((2 * chip + 1 - cc) * r, 8), r), :],
                dst_ref=theirs_ref.at[chip], send_sem=send_sems.at[chip], recv_sem=recv_sems.at[chip],
                device_id=(x, y, 1 - cc), device_id_type=MESH)

        @pl.when(k == 0)
        def _():
            _barrier_signal(SIBLING_PEER)
            _barrier_wait(SIBLING_PEER)
            for chip in range(4):
                copy(chip).start()

        for chip in range(4):
            @pl.when(k == chip)
            def _(chip=chip):
                copy(chip).wait_recv()
                o_ref[...] = (mine_ref[...].astype(F32) + theirs_ref[chip].astype(F32)).astype(BF16)

        @pl.when(k == 3)
        def _():
            for chip in range(4):
                copy(chip).wait_send()

    grid_spec = pltpu.PrefetchScalarGridSpec(
        num_scalar_prefetch=1, grid=(4,),
        in_specs=[pl.BlockSpec((None, None, r, c), lambda k, core_ref: (k, core_ref[0], 0, 0)), ANY],
        out_specs=pl.BlockSpec((r, c), lambda k, core_ref: (k, 0)),
        scratch_shapes=[pltpu.VMEM((4, r, c), BF16), pltpu.SemaphoreType.DMA((4,)), pltpu.SemaphoreType.DMA((4,))])
    params = pltpu.CompilerParams(dimension_semantics=("arbitrary",), vmem_limit_bytes=VMEM_LIMIT,
                                  collective_id=BARRIER_ID[SIBLING_PEER])
    return pl.pallas_call(body, name=name, grid_spec=grid_spec, out_shape=_sds((4 * r, c), BF16),
                          compiler_params=params)(core, partials.reshape(4, 2, r, c), partials)


def _sum_adamw(parts, w, m, v, tr, name):
    r, c = w.shape

    def body(p_ref, w_ref, m_ref, v_ref, g_ref, d_ref, m2_ref, v2_ref):
        g = _sum_slots(p_ref)
        g_ref[...] = g
        d_ref[...], m2_ref[...], v2_ref[...] = _adam_math(w_ref[...], g, m_ref[...], v_ref[...])

    spec = pl.BlockSpec((tr, c), lambda i: (i, 0))
    return _pcall(body, name, (r // tr,), [pl.BlockSpec((N_DEV, tr, c), lambda i: (0, i, 0)), spec, spec, spec],
                  [spec] * 4, [_sds((r, c), F32)] * 4, (parts, w, m, v))[0]


def _sum_parts_adamw(parts, w, m, v, tr, name):
    c = w.shape[1]
    tiles = [p.shape[1] // tr for p in parts]
    starts = [sum(tiles[:k]) for k in range(len(parts))]
    n_parts = len(parts)

    def body(*refs):
        p_refs = refs[:n_parts]
        w_ref, m_ref, v_ref, g_ref, d_ref, m2_ref, v2_ref = refs[n_parts:]
        i = pl.program_id(0)
        for p_ref, st, nt in zip(p_refs, starts, tiles):
            @pl.when(jnp.logical_and(i >= st, i < st + nt))
            def _(p_ref=p_ref):
                g_ref[...] = _sum_slots(p_ref)

        d_ref[...], m2_ref[...], v2_ref[...] = _adam_math(w_ref[...], g_ref[...], m_ref[...], v_ref[...])

    def part_spec(p, st, nt):
        return pl.BlockSpec((p.shape[0], tr, c), lambda i: (0, jnp.clip(i - st, 0, nt - 1), 0))

    spec = pl.BlockSpec((tr, c), lambda i: (i, 0))
    return _pcall(
        body, name, (sum(tiles),),
        [part_spec(p, st, nt) for p, st, nt in zip(parts, starts, tiles)] + [spec, spec, spec],
        [spec] * 4, [_sds(w.shape, F32)] * 4, (*parts, w, m, v))[0]


ROW_MIX, ROW_FFN, ROW_FINAL, ROW_SINKS, ROW_LOSS, ROW_BIN, ROW_CW, ROW_FCW = 0, 1, 2, 3, 4, 5, 10, 13
FCW_ROWS = 6


def _wide_pieces(width):
    return [(k * D_MODEL, min(D_MODEL, width - k * D_MODEL)) for k in range(-(-width // D_MODEL))]


def _pack_small(dffn, dfn, dsink, loss, dcw, dfcw):
    def body(ffn_ref, fn_ref, sink_ref, loss_ref, cw_ref, fcw_ref, o_ref):
        o_ref[...] = jnp.zeros_like(o_ref)
        o_ref[ROW_FFN:ROW_FFN + 1, :] = ffn_ref[...]
        o_ref[ROW_FINAL:ROW_FINAL + 1, :] = fn_ref[...]
        o_ref[ROW_SINKS:ROW_SINKS + 1, 0:128] = sink_ref[...]
        o_ref[ROW_LOSS:ROW_LOSS + 1, 0:128] = loss_ref[...]
        o_ref[ROW_CW:ROW_CW + 3, 0:CONV_W] = cw_ref[...]
        for a in range(3):
            for k, (off, w) in enumerate(_wide_pieces(2 * D_FF)):
                row = ROW_FCW + FCW_ROWS * a + k
                o_ref[row:row + 1, 0:w] = fcw_ref[a:a + 1, off:off + w]

    return pl.pallas_call(body, name="pack_small", out_shape=_sds((SMALL_ROWS, D_MODEL), F32))(
        dffn, dfn, dsink, loss, dcw, dfcw)


def _small_sums_adamw(r_small, r_dmix, r_dbin, params):
    rows = (None, None, ROW_SINKS, ROW_FFN, ROW_FINAL)

    def sum_row0(ref):
        tot = ref[0:1, :]
        for i in range(1, N_DEV):
            tot = tot + ref[8 * i:8 * i + 1, :]
        return tot

    def body(*refs):
        r_ref, late_refs, p_refs, o_refs = refs[0], refs[1:3], refs[3:18], refs[18:]
        tot = _sum_slots(r_ref)
        for k, row in enumerate(rows):
            w_ref, m_ref, v_ref = p_refs[3 * k:3 * k + 3]
            g_ref, d_ref, m2_ref, v2_ref = o_refs[4 * k:4 * k + 4]
            if row is None:
                g_ref[...] = sum_row0(late_refs[k])
            else:
                for j, (off, w) in enumerate(_wide_pieces(w_ref.shape[1])):
                    g_ref[:, off:off + w] = tot[row + j:row + j + 1, 0:w]
            d_ref[...], m2_ref[...], v2_ref[...] = _adam_math(w_ref[...], g_ref[...], m_ref[...], v_ref[...])
        cw_ref, fcw_ref, loss_ref = o_refs[20:]
        cw_ref[...] = tot[ROW_CW:ROW_CW + 3, 0:CONV_W]
        for a in range(3):
            for j, (off, w) in enumerate(_wide_pieces(2 * D_FF)):
                row = ROW_FCW + FCW_ROWS * a + j
                fcw_ref[a:a + 1, off:off + w] = tot[row:row + 1, 0:w]
        loss_ref[...] = tot[ROW_LOSS:ROW_LOSS + 1, 0:128]

    flat = [t for p in params for t in p]
    out_shape = [_sds(p[0].shape, F32) for p in params for _ in range(4)]
    out_shape += [_sds((3, CONV_W), F32), _sds((3, 2 * D_FF), F32), _sds((1, 128), F32)]
    res = pl.pallas_call(body, name="small_sums_adamw", out_shape=out_shape)(r_small, r_dmix, r_dbin, *flat)
    return [tuple(res[4 * k:4 * k + 4]) for k in range(5)], res[20], res[21], res[22]


def _adamw_pair(a, b):
    def body(*refs):
        for k in range(2):
            w_ref, g_ref, m_ref, v_ref = refs[4 * k:4 * k + 4]
            d_ref, m2_ref, v2_ref = refs[8 + 3 * k:8 + 3 * k + 3]
            d_ref[...], m2_ref[...], v2_ref[...] = _adam_math(w_ref[...], g_ref[...], m_ref[...], v_ref[...])

    out_shape = [_sds(a[0].shape, F32)] * 3 + [_sds(b[0].shape, F32)] * 3
    res = pl.pallas_call(body, name="adamw_conv_weights", out_shape=out_shape)(*a, *b)
    return tuple(res[:3]), tuple(res[3:])


def _pad_cols(a, c):
    return jnp.pad(a, ((0, 0), (0, c - a.shape[1])))


def _to_col_slabs(g):
    r = g.shape[0]
    return jnp.transpose(g.reshape(r, N_DEV, 128), (1, 0, 2)).reshape(N_DEV * r, 128)


def _from_col_slabs(t):
    r = t.shape[0] // N_DEV
    return jnp.transpose(t.reshape(N_DEV, r, 128), (1, 0, 2)).reshape(r, N_DEV * 128)


def _slots(t):
    return t.reshape(N_DEV, t.shape[0] // N_DEV, t.shape[1])


def kernel(x, mix_norm, w_in, b_in, sinks, conv_w, w_attn_branch, w_conv_branch, w_out, ffn_norm, w_up, ffn_conv_w, w_down, final_norm, loss_target, m_mix_norm, m_w_in, m_b_in, m_sinks, m_conv_w, m_w_attn_branch, m_w_conv_branch, m_w_out, m_ffn_norm, m_w_up, m_ffn_conv_w, m_w_down, m_final_norm, v_mix_norm, v_w_in, v_b_in, v_sinks, v_conv_w, v_w_attn_branch, v_w_conv_branch, v_w_out, v_ffn_norm, v_w_up, v_ffn_conv_w, v_w_down, v_final_norm):
    xs, tgt = x[0], loss_target[0]
    me = 4 * lax.axis_index("x") + 2 * lax.axis_index("y") + lax.axis_index("c")
    in_rows, up_rows = IN_W // N_DEV, 2 * D_FF // N_DEV

    conv_sh = jnp.concatenate([_pad_cols(ffn_conv_w[0], 768), _pad_cols(conv_w[0], 768),
                               jnp.zeros((2, 768), F32)], axis=0)
    win_sh, wup_sh = w_in[0].T.astype(BF16), w_up[0].T.astype(BF16)
    wout_sh, wdown_sh = w_out[0].astype(BF16), w_down[0].astype(BF16)
    wa_sh, wc_sh = w_attn_branch[0].astype(BF16), w_conv_branch[0].astype(BF16)

    quarter, half = D_MODEL // 4, D_MODEL // 2
    phases = dict(forward_at=0.375, pass_on_at=0.875)
    (win_t,) = _exchange_only(_AllGather([win_sh]), "gather_w_in")
    (xn, qkv, cbx, gates), (wa_s, wc_s, conv_g, wup_a) = _norm_inproj(
        xs, mix_norm, win_t, b_in, _AllGather([wa_sh, wc_sh, conv_sh, (wup_sh, 0, quarter)], **phases))
    (attn, lse), (wup_b, wout) = _attn_fwd(qkv, sinks,
                                           _AllGather([(wup_sh, quarter, quarter), wout_sh], **phases))
    wa, wc = _from_col_slabs(wa_s), _from_col_slabs(wc_s)
    conv_g = conv_g.reshape(N_DEV, 8, 768)
    fcw = jnp.transpose(conv_g[:, 0:3, :up_rows], (1, 0, 2)).reshape(3, 2 * D_FF)
    cw = jnp.transpose(conv_g[:, 3:6, :CONV_W // N_DEV], (1, 0, 2)).reshape(3, CONV_W)
    (h1,), (wup_c,) = _mix_fwd(xs, cbx, gates, attn, cw, wa, wc, wout,
                               _AllGather([(wup_sh, half, half)], **phases))
    wup_parts = (wup_a, wup_b, wup_c)
    (hn, up_pre, up), (wdown,) = _ffn_up(h1, ffn_norm, wup_parts, fcw,
                                         _AllGather([wdown_sh], forward_at=0.25, pass_on_at=0.75))
    act, dh2, loss_p, dfn_p = _ffn_down_loss(up, wdown, h1, final_norm.reshape(1, D_MODEL), tgt)

    dn_rows = D_FF // N_DEV
    g_wdown = _matmul_tn(act, dh2, FF_GRAD_ROWS, "grad_w_down")
    (dup_pre, dfcw_p, dh1, dffn_p), (r_wdown,) = _ffn_bwd(dh2, wdown, up, up_pre, fcw, wup_parts, h1, ffn_norm,
                                                         _ReduceScatter([(g_wdown, 0, dn_rows)]))
    g_wup_t = _matmul_tn(dup_pre, hn, FF_GRAD_ROWS, "grad_w_up")
    q_wup = _pair_sum(g_wup_t, "pair_sum_w_up")
    (dgates, dattn, dcb, dcc, dcx, dcw_p, g_wout, g_wa_nat, g_wc_nat), (r_wup,) = _mix_bwd(
        dh1, wout, gates, attn, wa, wc, cbx, cw, _ChipExchange([q_wup]))
    g_wa, g_wc = _to_col_slabs(g_wa_nat), _to_col_slabs(g_wc_nat)
    (dq, dk, dv, dsink_p), (r_wout,) = _attn_bwd(
        qkv, sinks, attn, lse, dattn, _ReduceScatter([(g_wout, 0, D_MODEL // N_DEV)]))
    dproj = (dq, dk, dv, dcb, dcc, dcx, dgates)
    small = _pack_small(dffn_p, dfn_p, dsink_p, loss_p, dcw_p, dfcw_p)
    g_win_t, (r_wa, r_wc, r_small) = _grad_w_in(
        dproj, xn, _ReduceScatter([(g_wa, 0, ATTN_W), (g_wc, 0, CONV_W)], [small]))
    q_win = _pair_sum(g_win_t, "pair_sum_w_in")
    (dx, _, _), (r_win, r_dbin, r_dmix) = _inproj_bwd(
        dproj, win_t, xs, mix_norm, dh1,
        _ChipExchangeThenBroadcast([q_win], late_from=(1, 2), late_shapes=[(8, IN_W), (8, D_MODEL)]))

    fn2, m_fn2, v_fn2 = (t.reshape(1, D_MODEL) for t in (final_norm, m_final_norm, v_final_norm))
    small_res, g_cw_full, g_fcw_full, loss_row = _small_sums_adamw(
        _slots(r_small), r_dmix, r_dbin,
        [(mix_norm, m_mix_norm, v_mix_norm), (b_in, m_b_in, v_b_in), (sinks, m_sinks, v_sinks),
         (ffn_norm, m_ffn_norm, v_ffn_norm), (fn2, m_fn2, v_fn2)])
    loss = loss_row[0, 0]
    g_cw = lax.dynamic_slice_in_dim(g_cw_full, me * (CONV_W // N_DEV), CONV_W // N_DEV, axis=1)
    g_fcw = lax.dynamic_slice_in_dim(g_fcw_full, me * up_rows, up_rows, axis=1)
    taps = lambda t: jnp.transpose(t, (1, 0, 2))
    g_cw, g_fcw = g_cw[:, None, :], g_fcw[:, None, :]
    cw_res, fcw_res = _adamw_pair((taps(conv_w), g_cw, taps(m_conv_w), taps(v_conv_w)),
                                  (taps(ffn_conv_w), g_fcw, taps(m_ffn_conv_w), taps(v_ffn_conv_w)))

    big = {}
    big["w_in"] = tuple(t.T for t in _sum_parts_adamw(
        [r_win.reshape(4, in_rows, D_MODEL)], w_in[0].T, m_w_in[0].T, v_w_in[0].T, in_rows // 2, "adamw_w_in"))
    big["w_up"] = tuple(t.T for t in _sum_parts_adamw(
        [r_wup.reshape(4, up_rows, D_MODEL)], w_up[0].T, m_w_up[0].T, v_w_up[0].T, up_rows // 4, "adamw_w_up"))
    big["w_out"] = _sum_adamw(_slots(r_wout), w_out[0], m_w_out[0], v_w_out[0], 128, "adamw_w_out")
    big["w_down"] = _sum_adamw(_slots(r_wdown), w_down[0], m_w_down[0], v_w_down[0], dn_rows // 2, "adamw_w_down")
    big["w_attn_branch"] = _sum_adamw(_slots(r_wa), w_attn_branch[0], m_w_attn_branch[0], v_w_attn_branch[0], 256,
                                      "adamw_w_attn_branch")
    big["w_conv_branch"] = _sum_adamw(_slots(r_wc), w_conv_branch[0], m_w_conv_branch[0], v_w_conv_branch[0], 256,
                                      "adamw_w_conv_branch")

    res = dict(zip(("mix_norm", "b_in", "sinks", "ffn_norm"), small_res[:4]))
    res["final_norm"] = tuple(t.reshape(final_norm.shape) for t in small_res[4])
    res["conv_w"] = tuple(jnp.transpose(t, (1, 0, 2)) for t in (g_cw,) + cw_res)
    res["ffn_conv_w"] = tuple(jnp.transpose(t, (1, 0, 2)) for t in (g_fcw,) + fcw_res)
    for name, ref_w in (("w_in", w_in), ("w_up", w_up), ("w_out", w_out), ("w_down", w_down),
                        ("w_attn_branch", w_attn_branch), ("w_conv_branch", w_conv_branch)):
        res[name] = tuple(t.reshape(ref_w.shape) for t in big[name])

    order = ["mix_norm", "w_in", "b_in", "sinks", "conv_w", "w_attn_branch", "w_conv_branch", "w_out",
             "ffn_norm", "w_up", "ffn_conv_w", "w_down", "final_norm"]
    out = [loss, dx.reshape(x.shape)]
    for k in range(4):
        out += [res[name][k] for name in order]
    return tuple(out)
```

```python
import math

import jax
import jax.numpy as jnp
from jax import lax
from jax.experimental import pallas as pl
from jax.experimental.pallas import tpu as pltpu

F32 = jnp.float32
BF16 = jnp.bfloat16
MESH = pl.DeviceIdType.MESH
N_DEV = 8

D_MODEL = 1024
HEAD_DIM = 64
N_HEADS = 8
BLOCK = 128
ATTN_W = 512
KV_W = 128
CONV_W = 512
QKV_W = ATTN_W + 2 * KV_W
CBX_W = 3 * CONV_W
GATE_W = 2 * D_MODEL
IN_W = QKV_W + CBX_W + GATE_W
D_FF = 2816
FF_CHUNK = 256
FF_GRAD_ROWS = 1408
NORM_EPS = 1e-5
ATTN_SCALE = HEAD_DIM ** -0.5
NEG = -1e30
HALO = 16

ADAM_LR = 0.001
ADAM_B1 = 0.9
ADAM_B2 = 0.999
ADAM_EPS = 1e-08
ADAM_WD = 0.01
ADAM_STEP = 10

VMEM_LIMIT = 56 * 1024 * 1024
SMALL_ROWS = 32

NT = (((1,), (1,)), ((), ()))
TN = (((0,), (0,)), ((), ()))
ANY = pl.BlockSpec(memory_space=pl.ANY)


def _sig(v):
    return 1.0 / (1.0 + jnp.exp(-v))


def _row_tile(s, pref=256):
    return pref if s % pref == 0 else s


def _shifts_down(u, halo, ks):
    ext = jnp.concatenate([halo, u], axis=0)
    return [pltpu.roll(ext, k, axis=0)[HALO:, :] for k in ks]


def _shifts_up(u, halo, ks):
    n = u.shape[0]
    ext = jnp.concatenate([u, halo], axis=0)
    return [pltpu.roll(ext, n + HALO - k, axis=0)[:n, :] for k in ks]


def _rows_reversed(tm, c, steps):
    return pl.BlockSpec((tm, c), lambda i: (steps - 1 - i, 0))


def _prev_halo_map_reversed(tm, steps):
    return lambda i: (jnp.maximum((steps - 1 - i) * (tm // HALO) - 1, 0), 0)


def _prev_halo_map(tm):
    return lambda i: (jnp.maximum(i * (tm // HALO) - 1, 0), 0)


def _full(shape):
    return pl.BlockSpec(shape, lambda *_: (0,) * len(shape))


def _resident(shape):
    return pl.BlockSpec(shape, lambda *_: (0,) * len(shape), pipeline_mode=pl.Buffered(1))


def _rows(tm, c):
    return pl.BlockSpec((tm, c), lambda i: (i, 0))


def _sds(shape, dtype):
    return jax.ShapeDtypeStruct(shape, dtype)


def _my_place():
    x, y, c = lax.axis_index("x"), lax.axis_index("y"), lax.axis_index("c")
    return x, y, c


ALL_PEERS = tuple((j >> 2, (j >> 1) & 1, j & 1) for j in range(1, N_DEV))
SIBLING_PEER = ((0, 0, 1),)
CHIP_PEERS = ((0, 1, 0), (1, 0, 0), (1, 1, 0))
BARRIER_ID = {ALL_PEERS: 0, SIBLING_PEER: 1, CHIP_PEERS: 2}


def _barrier_signal(peers):
    x, y, c = _my_place()
    barrier = pltpu.get_barrier_semaphore()
    for dx, dy, dc in peers:
        pl.semaphore_signal(barrier, inc=1, device_id=(x ^ dx, y ^ dy, c ^ dc), device_id_type=MESH)


def _barrier_wait(peers):
    pl.semaphore_wait(pltpu.get_barrier_semaphore(), len(peers))


def _start_exchange(remote, local):
    for cp in local + remote:
        cp.start()


def _finish_exchange(remote, local):
    for cp in remote:
        cp.wait_recv()
    for cp in remote:
        cp.wait_send()
    for cp in local:
        cp.wait()


class _AllGather:
    peers = ALL_PEERS
    SLOTS = 10

    def __init__(self, shards, pass_on_at=None, forward_at=None):
        self.ins = [s[0] if isinstance(s, tuple) else s for s in shards]
        self.cols = [s[1:] if isinstance(s, tuple) else None for s in shards]
        self.middle_at, self.forward_at = pass_on_at, forward_at
        assert pass_on_at is None or forward_at is not None
        n = len(shards)
        self.out_shape = [_sds((N_DEV * s.shape[0], s.shape[1] if c is None else c[1]), s.dtype)
                          for s, c in zip(self.ins, self.cols)]
        self.sems = [pltpu.SemaphoreType.DMA((self.SLOTS * n,)), pltpu.SemaphoreType.DMA((self.SLOTS * n,)),
                     pltpu.SemaphoreType.DMA((n,))]

    def _plan(self, ins, outs, sems):
        send_sems, recv_sems, local_sems = sems
        x, y, c = _my_place()
        me, sibling = (x, y, c), (x, y, 1 - c)
        x_chip, y_chip, far_chip = (1 - x, y), (x, 1 - y), (1 - x, 1 - y)
        sends, lands, mine = [], [], []
        for k in range(len(ins)):
            r = ins[k].shape[0]
            h = (r // 2) // 16 * 16
            whole, first, second = (0, r), (0, h), (h, r - h)

            def rows(dev, rng, k=k, r=r):
                start = pl.multiple_of((4 * dev[0] + 2 * dev[1] + dev[2]) * r + rng[0], 8)
                return outs[k].at[pl.ds(start, rng[1]), :]

            def own(rng, k=k):
                cols = self.cols[k]
                if cols is None:
                    return ins[k].at[pl.ds(rng[0], rng[1]), :]
                return ins[k].at[pl.ds(rng[0], rng[1]), pl.ds(cols[0], cols[1])]

            def copy(slot, block, rng, to, mine_src=False, k=k, rows=rows, own=own):
                if rng[1] == 0:
                    return None
                return pltpu.make_async_remote_copy(
                    src_ref=own(rng) if mine_src else rows(block, rng), dst_ref=rows(block, rng),
                    send_sem=send_sems.at[self.SLOTS * k + slot], recv_sem=recv_sems.at[self.SLOTS * k + slot],
                    device_id=to, device_id_type=MESH)

            sends.append([
                copy(0, me, whole, sibling, True),
                copy(1, me, first, (*x_chip, c), True),
                copy(2, me, second, (*x_chip, c), True),
                copy(3, me, second, (*y_chip, c), True),
                copy(4, me, first, (*y_chip, c), True),
                copy(5, (*x_chip, c), first, (*y_chip, c)),
                copy(6, (*y_chip, c), second, (*x_chip, c)),
                copy(7, (*x_chip, c), whole, sibling),
                copy(8, (*y_chip, c), whole, sibling),
                copy(9, (*far_chip, c), whole, sibling)])
            lands.append([
                copy(0, sibling, whole, me),
                copy(1, (*x_chip, c), first, me), copy(2, (*x_chip, c), second, me),
                copy(3, (*y_chip, c), second, me), copy(4, (*y_chip, c), first, me),
                copy(5, (*far_chip, c), first, me), copy(6, (*far_chip, c), second, me),
                copy(7, (*x_chip, 1 - c), whole, me), copy(8, (*y_chip, 1 - c), whole, me),
                copy(9, (*far_chip, 1 - c), whole, me)])
            mine.append(pltpu.make_async_copy(own(whole), rows(me, whole), local_sems.at[k]))
        return sends, lands, mine

    @staticmethod
    def _then(lands, waits, sends, starts):
        for slot in waits:
            if lands[slot] is not None:
                lands[slot].wait_recv()
        for slot in starts:
            if sends[slot] is not None:
                sends[slot].start()

    def start(self, ins, outs, sems):
        sends, lands, mine = self._plan(ins, outs, sems)
        for cp in mine:
            cp.start()
        for slot in (1, 3, 0, 2, 4):
            for s in sends:
                self._then(None, (), s, (slot,))

    def forward(self, ins, outs, sems):
        sends, lands, _ = self._plan(ins, outs, sems)
        for s, l in zip(sends, lands):
            self._then(l, (1,), s, (5,))
            self._then(l, (3,), s, (6,))

    def middle(self, ins, outs, sems):
        sends, lands, _ = self._plan(ins, outs, sems)
        for s, l in zip(sends, lands):
            self._then(l, (2,), s, (7,))
            self._then(l, (4,), s, (8,))
        for s, l in zip(sends, lands):
            self._then(l, (5, 6), s, (9,))

    def finish(self, ins, outs, sems):
        if self.forward_at is None:
            self.forward(ins, outs, sems)
        if self.middle_at is None:
            self.middle(ins, outs, sems)
        sends, lands, mine = self._plan(ins, outs, sems)
        for s, l in zip(sends, lands):
            self._then(l, (0, 7, 8, 9), s, ())
        for s in sends:
            for cp in s:
                if cp is not None:
                    cp.wait_send()
        for cp in mine:
            cp.wait()


class _ReduceScatter:
    peers = ALL_PEERS

    def __init__(self, parts, bcast=()):
        self.parts = [(lo, cnt) for _, lo, cnt in parts]
        self.n_parts = len(parts)
        self.ins = [a for a, _, _ in parts] + list(bcast)
        self.out_shape = [_sds((N_DEV * cnt, a.shape[1]), a.dtype) for a, _, cnt in parts]
        self.out_shape += [_sds((N_DEV * b.shape[0], b.shape[1]), b.dtype) for b in bcast]
        n = len(self.ins)
        self.sems = [pltpu.SemaphoreType.DMA((7 * n,)), pltpu.SemaphoreType.DMA((7 * n,)),
                     pltpu.SemaphoreType.DMA((n,))]

    def _copies(self, ins, outs, sems):
        send_sems, recv_sems, local_sems = sems
        x, y, c = _my_place()
        me_idx = 4 * x + 2 * y + c
        remote, local = [], []
        for k in range(len(ins)):
            cnt = outs[k].shape[0] // N_DEV
            dst = outs[k].at[pl.ds(pl.multiple_of(me_idx * cnt, 8), cnt), :]
            if k < self.n_parts:
                lo, _ = self.parts[k]
                r = ins[k].shape[0] // N_DEV
                src_of = lambda idx: ins[k].at[pl.ds(pl.multiple_of(idx * r + lo, 8), cnt), :]
            else:
                src_of = lambda idx: ins[k]
            local.append(pltpu.make_async_copy(src_of(me_idx), dst, local_sems.at[k]))
            for j in range(1, N_DEV):
                peer = (x ^ (j >> 2), y ^ ((j >> 1) & 1), c ^ (j & 1))
                peer_idx = 4 * peer[0] + 2 * peer[1] + peer[2]
                remote.append(pltpu.make_async_remote_copy(
                    src_ref=src_of(peer_idx), dst_ref=dst,
                    send_sem=send_sems.at[7 * k + j - 1], recv_sem=recv_sems.at[7 * k + j - 1],
                    device_id=peer, device_id_type=MESH))
        return remote, local

    def start(self, ins, outs, sems):
        _start_exchange(*self._copies(ins, outs, sems))

    def finish(self, ins, outs, sems):
        _finish_exchange(*self._copies(ins, outs, sems))


class _ChipExchange:
    peers = CHIP_PEERS

    def __init__(self, arrays):
        self.ins = list(arrays)
        self.out_shape = [_sds(a.shape, a.dtype) for a in arrays]
        n = len(self.ins)
        self.sems = [pltpu.SemaphoreType.DMA((3 * n,)), pltpu.SemaphoreType.DMA((3 * n,)),
                     pltpu.SemaphoreType.DMA((n,))]

    def _copies(self, ins, outs, sems):
        send_sems, recv_sems, local_sems = sems
        x, y, c = _my_place()
        my_chip = 2 * x + y
        remote, local = [], []
        for k in range(len(ins)):
            r = ins[k].shape[0] // 4
            dst = outs[k].at[pl.ds(pl.multiple_of(my_chip * r, 8), r), :]
            local.append(pltpu.make_async_copy(ins[k].at[pl.ds(pl.multiple_of(my_chip * r, 8), r), :], dst,
                                               local_sems.at[k]))
            for j in range(1, 4):
                px, py = x ^ (j >> 1), y ^ (j & 1)
                src = ins[k].at[pl.ds(pl.multiple_of((2 * px + py) * r, 8), r), :]
                remote.append(pltpu.make_async_remote_copy(
                    src_ref=src, dst_ref=dst, send_sem=send_sems.at[3 * k + j - 1],
                    recv_sem=recv_sems.at[3 * k + j - 1], device_id=(px, py, c), device_id_type=MESH))
        return remote, local

    def start(self, ins, outs, sems):
        _start_exchange(*self._copies(ins, outs, sems))

    def finish(self, ins, outs, sems):
        _finish_exchange(*self._copies(ins, outs, sems))


class _ChipExchangeThenBroadcast(_ChipExchange):
    peers = ALL_PEERS
    defer_start = False

    def __init__(self, arrays, late_from, late_shapes):
        super().__init__(arrays)
        self.n_chip = len(arrays)
        self.late_from = tuple(late_from)
        self.out_shape += [_sds((N_DEV * r, c), F32) for r, c in late_shapes]
        m = len(late_shapes)
        self.sems += [pltpu.SemaphoreType.DMA((7 * m,)), pltpu.SemaphoreType.DMA((7 * m,)),
                      pltpu.SemaphoreType.DMA((m,))]

    def _late_copies(self, srcs, outs, sems):
        send_sems, recv_sems, local_sems = sems
        x, y, c = _my_place()
        me_idx = 4 * x + 2 * y + c
        remote, local = [], []
        for k, src in enumerate(srcs):
            r = src.shape[0]
            dst = outs[k].at[pl.ds(pl.multiple_of(me_idx * r, 8), r), :]
            local.append(pltpu.make_async_copy(src, dst, local_sems.at[k]))
            for j, (dx, dy, dc) in enumerate(ALL_PEERS):
                remote.append(pltpu.make_async_remote_copy(
                    src_ref=src, dst_ref=dst, send_sem=send_sems.at[7 * k + j], recv_sem=recv_sems.at[7 * k + j],
                    device_id=(x ^ dx, y ^ dy, c ^ dc), device_id_type=MESH))
        return remote, local

    def start(self, ins, outs, sems):
        _start_exchange(*self._copies(ins, outs[:self.n_chip], sems[:3]))

    def finish(self, ins, outs, sems, late_srcs):
        late = self._late_copies(late_srcs, outs[self.n_chip:], sems[3:])
        _start_exchange(*late)
        _finish_exchange(*self._copies(ins, outs[:self.n_chip], sems[:3]))
        _finish_exchange(*late)


def _pcall(body, name, grid, in_specs, out_specs, out_shape, args, scratch=(), comm=None):
    params = pltpu.CompilerParams(dimension_semantics=("arbitrary",) * len(grid), vmem_limit_bytes=VMEM_LIMIT)
    in_specs, out_specs, out_shape, scratch = list(in_specs), list(out_specs), list(out_shape), list(scratch)
    if comm is None:
        res = pl.pallas_call(body, name=name, grid=grid, in_specs=in_specs, out_specs=out_specs, out_shape=out_shape,
                             scratch_shapes=scratch, compiler_params=params)(*args)
        return list(res), []
    n_in, n_out, n_scr = len(in_specs), len(out_specs), len(scratch)
    ci, co = len(comm.ins), len(comm.out_shape)
    total = math.prod(grid)

    def carried(*refs):
        bounds = [0, n_in, n_in + ci, n_in + ci + n_out, n_in + ci + n_out + co, n_in + ci + n_out + co + n_scr]
        ins, cins, outs, couts, scr = (refs[a:b] for a, b in zip(bounds[:-1], bounds[1:]))
        sems = refs[bounds[-1]:]
        step = pl.program_id(0)
        for d in range(1, len(grid)):
            step = step * grid[d] + pl.program_id(d)

        start_step = min(1, total - 1) if getattr(comm, "defer_start", True) else 0

        @pl.when(step == 0)
        def _():
            _barrier_signal(comm.peers)

        @pl.when(step == start_step)
        def _():
            _barrier_wait(comm.peers)
            comm.start(cins, couts, sems)

        forward_at = getattr(comm, "forward_at", None)
        if forward_at is not None and int(forward_at * total) <= start_step:
            forward_at = comm.forward_at = comm.middle_at = None
        if forward_at is not None:
            @pl.when(step == int(forward_at * total))
            def _():
                comm.forward(cins, couts, sems)

        middle_at = getattr(comm, "middle_at", None)
        if middle_at is not None:
            assert forward_at is None or forward_at <= middle_at
            @pl.when(step == int(middle_at * total))
            def _():
                comm.middle(cins, couts, sems)

        body(*ins, *outs, *scr)

        @pl.when(step == total - 1)
        def _():
            late_from = getattr(comm, "late_from", None)
            if late_from is None:
                comm.finish(cins, couts, sems)
            else:
                comm.finish(cins, couts, sems, [outs[k] for k in late_from])

    params = pltpu.CompilerParams(dimension_semantics=("arbitrary",) * len(grid), vmem_limit_bytes=VMEM_LIMIT,
                                  collective_id=BARRIER_ID[comm.peers])
    res = pl.pallas_call(
        carried, name=name, grid=grid, in_specs=in_specs + [ANY] * ci, out_specs=out_specs + [ANY] * co,
        out_shape=out_shape + comm.out_shape, scratch_shapes=scratch + comm.sems, compiler_params=params,
    )(*args, *comm.ins)
    return list(res[:n_out]), list(res[n_out:])


def _exchange_only(comm, name):
    def body(*refs):
        ci, co = len(comm.ins), len(comm.out_shape)
        _barrier_signal(comm.peers)
        _barrier_wait(comm.peers)
        comm.start(refs[:ci], refs[ci:ci + co], refs[ci + co:])
        comm.finish(refs[:ci], refs[ci:ci + co], refs[ci + co:])

    params = pltpu.CompilerParams(collective_id=BARRIER_ID[comm.peers])
    return pl.pallas_call(body, name=name, out_shape=comm.out_shape, in_specs=[ANY] * len(comm.ins),
                          out_specs=[ANY] * len(comm.out_shape), scratch_shapes=comm.sems,
                          compiler_params=params)(*comm.ins)


def _norm_inproj(x, g, win_t, b_in, comm):
    s = x.shape[0]
    tm = _row_tile(s, 512)
    widths = (QKV_W, CBX_W, GATE_W)

    def body(x_ref, g_ref, w_ref, b_ref, xn_ref, qkv_ref, cbx_ref, gate_ref):
        xv = x_ref[...]
        r = lax.rsqrt(jnp.mean(xv * xv, axis=-1, keepdims=True) + NORM_EPS)
        xn = (xv * r * g_ref[...]).astype(BF16)
        xn_ref[...] = xn
        off = 0
        for o_ref, w in zip((qkv_ref, cbx_ref, gate_ref), widths):
            acc = lax.dot_general(xn, w_ref[off:off + w, :], NT, preferred_element_type=F32)
            o_ref[...] = (acc + b_ref[:, off:off + w]).astype(BF16)
            off += w

    return _pcall(
        body, "norm_inproj", (s // tm,),
        [_rows(tm, D_MODEL), _full((1, D_MODEL)), _resident((IN_W, D_MODEL)), _full((1, IN_W))],
        [_rows(tm, D_MODEL)] + [_rows(tm, w) for w in widths],
        [_sds((s, D_MODEL), BF16)] + [_sds((s, w), BF16) for w in widths],
        (x, g, win_t, b_in), comm=comm)


Q_BLOCKS = 4


def _attn_specs():
    tq = Q_BLOCKS * BLOCK
    prev = lambda n: jnp.maximum(Q_BLOCKS * n - 1, 0)
    return [pl.BlockSpec((tq, ATTN_W), lambda n: (n, 0)),
            pl.BlockSpec((BLOCK, KV_W), lambda n: (prev(n), ATTN_W // KV_W)),
            pl.BlockSpec((tq, KV_W), lambda n: (n, ATTN_W // KV_W)),
            pl.BlockSpec((BLOCK, KV_W), lambda n: (prev(n), ATTN_W // KV_W + 1)),
            pl.BlockSpec((tq, KV_W), lambda n: (n, ATTN_W // KV_W + 1))]


def _window(prev_ref, cur_ref, sub):
    if sub == 0:
        return jnp.concatenate([prev_ref[...], cur_ref[0:BLOCK, :]], axis=0)
    return cur_ref[(sub - 1) * BLOCK:(sub + 1) * BLOCK, :]


def _lower_lanes():
    return lax.broadcasted_iota(jnp.int32, (BLOCK, 128), 1) < HEAD_DIM


def _stack_heads(val, kh):
    lower = _lower_lanes()
    parts = []
    for g in range(4):
        h = kh * 4 + g
        blk = val[:, (h // 2) * 128:(h // 2 + 1) * 128]
        keep = lower if h % 2 == 0 else jnp.logical_not(lower)
        parts.append(jnp.where(keep, blk, jnp.zeros_like(blk)))
    return jnp.concatenate(parts, axis=0)


def _dup_kv(window, kh):
    t = window.astype(F32)
    rolled = pltpu.roll(t, HEAD_DIM, axis=1)
    lower = lax.broadcasted_iota(jnp.int32, t.shape, 1) < HEAD_DIM
    dup = jnp.where(lower, t, rolled) if kh == 0 else jnp.where(lower, rolled, t)
    return dup.astype(BF16)


def _attn_mask(real_prev):
    row = lax.broadcasted_iota(jnp.int32, (4 * BLOCK, 2 * BLOCK), 0)
    kj = lax.broadcasted_iota(jnp.int32, (4 * BLOCK, 2 * BLOCK), 1)
    dist = (row & (BLOCK - 1)) + BLOCK - kj
    band = jnp.logical_and(dist >= 0, dist < BLOCK)
    return jnp.logical_and(band, jnp.logical_or(kj >= BLOCK, real_prev))


def _sink_col(sinks_ref, kh):
    gi = lax.broadcasted_iota(jnp.int32, (4 * BLOCK, 1), 0) // BLOCK
    col = jnp.zeros((4 * BLOCK, 1), F32)
    for g in range(4):
        col = jnp.where(gi == g, sinks_ref[0, kh * 4 + g], col)
    return col


def _attn_fwd(qkv, sinks, comm):
    s = qkv.shape[0]
    tq = Q_BLOCKS * BLOCK

    def body(sinks_ref, q_ref, kp_ref, kc_ref, vp_ref, vc_ref, o_ref, lse_ref):
        n = pl.program_id(0)
        lower = _lower_lanes()
        lane = lax.broadcasted_iota(jnp.int32, (BLOCK, 128), 1)
        for sub in range(Q_BLOCKS):
            rows = slice(sub * BLOCK, (sub + 1) * BLOCK)
            mask = _attn_mask(n > 0 if sub == 0 else True)
            kw, vw = _window(kp_ref, kc_ref, sub), _window(vp_ref, vc_ref, sub)
            qv = q_ref[rows, :]
            lse_out = jnp.zeros((BLOCK, 128), F32)
            for kh in range(2):
                qs = _stack_heads(qv, kh)
                kd, vd = _dup_kv(kw, kh), _dup_kv(vw, kh)
                sc = lax.dot_general(qs, kd, NT, preferred_element_type=F32) * ATTN_SCALE
                sc = jnp.where(mask, sc, NEG)
                sink = _sink_col(sinks_ref, kh)
                m = jnp.maximum(jnp.max(sc, axis=1, keepdims=True), sink)
                p = jnp.exp(sc - m)
                l = jnp.sum(p, axis=1, keepdims=True) + jnp.exp(sink - m)
                o = jnp.dot(p.astype(BF16), vd, preferred_element_type=F32) / l
                lse = m + jnp.log(l)
                for pair in range(2):
                    lo = o[(2 * pair) * BLOCK:(2 * pair + 1) * BLOCK]
                    hi = o[(2 * pair + 1) * BLOCK:(2 * pair + 2) * BLOCK]
                    col = (kh * 2 + pair) * 128
                    o_ref[rows, col:col + 128] = jnp.where(lower, lo, hi).astype(BF16)
                for g in range(4):
                    lse_out = jnp.where(lane == kh * 4 + g, lse[g * BLOCK:(g + 1) * BLOCK], lse_out)
            lse_ref[rows, :] = lse_out

    return _pcall(
        body, "attn_fwd", (s // tq,),
        [pl.BlockSpec(memory_space=pltpu.SMEM)] + _attn_specs(),
        [pl.BlockSpec((tq, ATTN_W), lambda n: (n, 0)), pl.BlockSpec((tq, 128), lambda n: (n, 0))],
        [_sds((s, ATTN_W), BF16), _sds((s, 128), F32)],
        (sinks, qkv, qkv, qkv, qkv, qkv), comm=comm)


def _conv_u(cbx_ref, halo_ref, w_ref, first):
    cb = cbx_ref[:, 0:CONV_W].astype(F32)
    cc = cbx_ref[:, CONV_W:2 * CONV_W].astype(F32)
    cx = cbx_ref[:, 2 * CONV_W:3 * CONV_W].astype(F32)
    u = cc * cx
    uh = halo_ref[:, CONV_W:2 * CONV_W].astype(F32) * halo_ref[:, 2 * CONV_W:3 * CONV_W].astype(F32)
    uh = jnp.where(first, 0.0, uh)
    u1, u2 = _shifts_down(u, uh, (1, 2))
    cv = w_ref[0:1, :] * u2 + w_ref[1:2, :] * u1 + w_ref[2:3, :] * u
    return cb, cc, cx, u, cv


def _mix_fwd(x, cbx, gates, attn, conv_w, wa, wc, wout, comm):
    s = x.shape[0]
    tm = _row_tile(s)

    def body(x_ref, cbx_ref, halo_ref, gate_ref, attn_ref, cw_ref, wa_ref, wc_ref, wo_ref,
             h1_ref):
        first = pl.program_id(0) == 0
        cb, _, _, _, cv = _conv_u(cbx_ref, halo_ref, cw_ref, first)
        conv = (cb * cv).astype(BF16)
        ap = jnp.dot(attn_ref[...], wa_ref[...], preferred_element_type=F32)
        cp = jnp.dot(conv, wc_ref[...], preferred_element_type=F32)
        ga = gate_ref[:, 0:D_MODEL].astype(F32)
        gc = gate_ref[:, D_MODEL:2 * D_MODEL].astype(F32)
        merged = (_sig(ga) * ap + _sig(gc) * cp).astype(BF16)
        h1_ref[...] = x_ref[...] + jnp.dot(merged, wo_ref[...], preferred_element_type=F32)

    return _pcall(
        body, "mix_fwd", (s // tm,),
        [_rows(tm, D_MODEL), _rows(tm, CBX_W), pl.BlockSpec((HALO, CBX_W), _prev_halo_map(tm)),
         _rows(tm, GATE_W), _rows(tm, ATTN_W), _full((3, CONV_W)), _full((ATTN_W, D_MODEL)),
         _full((CONV_W, D_MODEL)), _full((D_MODEL, D_MODEL))],
        [_rows(tm, D_MODEL)], [_sds((s, D_MODEL), F32)],
        (x, cbx, cbx, gates, attn, conv_w, wa, wc, wout), comm=comm)


def _col_offsets(wup_parts):
    widths = [p.shape[1] for p in wup_parts]
    assert sum(widths) == D_MODEL
    return [(sum(widths[:k]), w) for k, w in enumerate(widths)]


def _ffn_up(h1, g, wup_parts, fcw, comm):
    s = h1.shape[0]
    tm = _row_tile(s)
    cols = _col_offsets(wup_parts)

    def body(h_ref, g_ref, *refs):
        w_refs = refs[:len(cols)]
        fcw_ref, hn_ref, pre_ref, up_ref, carry_ref = refs[len(cols):]

        @pl.when(pl.program_id(0) == 0)
        def _():
            carry_ref[...] = jnp.zeros_like(carry_ref)

        hv = h_ref[...]
        r = lax.rsqrt(jnp.mean(hv * hv, axis=-1, keepdims=True) + NORM_EPS)
        hn = (hv * r * g_ref[...]).astype(BF16)
        hn_ref[...] = hn
        for c in range(2 * D_FF // FF_CHUNK):
            sl = slice(c * FF_CHUNK, (c + 1) * FF_CHUNK)
            acc = None
            for w_ref, (off, w) in zip(w_refs, cols):
                part = lax.dot_general(hn[:, off:off + w], w_ref[sl, :], NT, preferred_element_type=F32)
                acc = part if acc is None else acc + part
            pre_ref[:, sl] = acc.astype(BF16)
            halo = carry_ref[:, sl]
            carry_ref[:, sl] = acc[tm - HALO:, :]
            u1, u2 = _shifts_down(acc, halo, (1, 2))
            w = fcw_ref[:, sl]
            up_ref[:, sl] = (w[0:1] * u2 + w[1:2] * u1 + w[2:3] * acc).astype(BF16)

    return _pcall(
        body, "ffn_up", (s // tm,),
        [_rows(tm, D_MODEL), _full((1, D_MODEL))] + [_resident((2 * D_FF, w)) for _, w in cols]
        + [_full((3, 2 * D_FF))],
        [_rows(tm, D_MODEL), _rows(tm, 2 * D_FF), _rows(tm, 2 * D_FF)],
        [_sds((s, D_MODEL), BF16), _sds((s, 2 * D_FF), BF16), _sds((s, 2 * D_FF), BF16)],
        (h1, g, *wup_parts, fcw), scratch=[pltpu.VMEM((HALO, 2 * D_FF), F32)], comm=comm)


def _ffn_down_loss(up, wdown, h1, fnorm, target):
    s = h1.shape[0]
    tm = _row_tile(s)

    steps = s // tm
    assert steps >= 2

    def body(up_hbm, wd_ref, h1_ref, fn_ref, t_ref, act_ref, dh2_ref, loss_ref, dfn_ref, ring_ref, sems):
        i = pl.program_id(0)

        def fetch(step):
            slot = step % 3
            return pltpu.make_async_copy(up_hbm.at[pl.ds(pl.multiple_of(step * tm, tm), tm), :],
                                         ring_ref.at[slot], sems.at[slot])

        @pl.when(i == 0)
        def _():
            fetch(0).start()
            fetch(1).start()
            loss_ref[...] = jnp.zeros_like(loss_ref)
            dfn_ref[...] = jnp.zeros_like(dfn_ref)

        @pl.when(i + 2 < steps)
        def _():
            fetch(i + 2).start()

        fetch(i).wait()
        up_ref = ring_ref.at[i % 3]
        h2 = h1_ref[...]
        for c in range(D_FF // FF_CHUNK):
            gsl = slice(c * FF_CHUNK, (c + 1) * FF_CHUNK)
            vsl = slice(D_FF + c * FF_CHUNK, D_FF + (c + 1) * FF_CHUNK)
            gate = up_ref[:, gsl].astype(F32)
            val = up_ref[:, vsl].astype(F32)
            act = (gate * _sig(gate) * val).astype(BF16)
            act_ref[:, gsl] = act
            h2 = h2 + jnp.dot(act, wd_ref[gsl, :], preferred_element_type=F32)
        r = lax.rsqrt(jnp.mean(h2 * h2, axis=-1, keepdims=True) + NORM_EPS)
        yhat = h2 * r
        fn = fn_ref[...]
        diff = yhat * fn - t_ref[...]
        loss_ref[...] += 0.5 * jnp.sum(jnp.sum(diff * diff, axis=1, keepdims=True), axis=0, keepdims=True) / D_MODEL
        dy = diff * (1.0 / D_MODEL)
        dfn_ref[...] += jnp.sum(dy * yhat, axis=0, keepdims=True)
        dyh = dy * fn
        dh2_ref[...] = r * (dyh - yhat * jnp.mean(dyh * yhat, axis=-1, keepdims=True))

    return _pcall(
        body, "ffn_down_loss", (s // tm,),
        [ANY, _resident((D_FF, D_MODEL)), _rows(tm, D_MODEL), _full((1, D_MODEL)), _rows(tm, D_MODEL)],
        [_rows(tm, D_FF), _rows(tm, D_MODEL), _full((1, 128)), _full((1, D_MODEL))],
        [_sds((s, D_FF), BF16), _sds((s, D_MODEL), F32), _sds((1, 128), F32), _sds((1, D_MODEL), F32)],
        (up, wdown, h1, fnorm, target),
        scratch=[pltpu.VMEM((3, tm, 2 * D_FF), BF16), pltpu.SemaphoreType.DMA((3,))])[0]


def _ffn_bwd(dh2, wdown, up, up_pre, fcw, wup_parts, h1, g, comm):
    s = dh2.shape[0]
    tm = _row_tile(s)
    cols = _col_offsets(wup_parts)

    chunk = FF_GRAD_ROWS

    def dup_cols(dh, up_ref, wd_ref, c):
        gsl = slice(c * chunk, (c + 1) * chunk)
        vsl = slice(D_FF + c * chunk, D_FF + (c + 1) * chunk)
        dact = lax.dot_general(dh, wd_ref[gsl, :], NT, preferred_element_type=F32)
        gate = up_ref[:, gsl].astype(F32)
        val = up_ref[:, vsl].astype(F32)
        sg = _sig(gate)
        return dact * val * (sg * (1.0 + gate * (1.0 - sg))), dact * gate * sg

    steps = s // tm
    assert steps >= 2

    def body(dh_ref, wd_ref, up_hbm, x_hbm, w_ref, *refs):
        wup_refs = refs[:len(cols)]
        h_ref, g_ref, dx_ref, dw_ref, dh1_ref, dg_ref, carry_ref, ring_ref, sems = refs[len(cols):]
        i = pl.program_id(0)

        def fetch(step):
            slot = step % 3
            rows = pl.ds(pl.multiple_of((steps - 1 - step) * tm, tm), tm)
            return [pltpu.make_async_copy(src.at[rows, :], ring_ref.at[k, slot], sems.at[k, slot])
                    for k, src in enumerate((up_hbm, x_hbm))]

        @pl.when(i == 0)
        def _():
            for cp in fetch(0) + fetch(1):
                cp.start()
            dw_ref[...] = jnp.zeros_like(dw_ref)
            dg_ref[...] = jnp.zeros_like(dg_ref)
            carry_ref[...] = jnp.zeros_like(carry_ref)

        @pl.when(i + 2 < steps)
        def _():
            for cp in fetch(i + 2):
                cp.start()

        for cp in fetch(i):
            cp.wait()
        up_ref, x_ref = ring_ref.at[0, i % 3], ring_ref.at[1, i % 3]
        dh2v = dh_ref[...]
        dh = dh2v.astype(BF16)
        dhn = [jnp.zeros((tm, w), F32) for _, w in cols]
        for c in range(D_FF // chunk):
            for d, off in zip(dup_cols(dh, up_ref, wd_ref, c), (c * chunk, D_FF + c * chunk)):
                sl = slice(off, off + chunk)
                dn = carry_ref[:, sl]
                carry_ref[:, sl] = d[0:HALO, :]
                xv = x_ref[:, sl].astype(F32)
                wv = w_ref[:, sl]
                d1, d2 = _shifts_up(d, dn, (1, 2))
                dx = (wv[2:3] * d + wv[1:2] * d1 + wv[0:1] * d2).astype(BF16)
                dx_ref[:, sl] = dx
                dhn = [a + jnp.dot(dx, wup_ref[sl, :], preferred_element_type=F32)
                       for a, wup_ref in zip(dhn, wup_refs)]
                dw_ref[0:1, sl] += jnp.sum(d2 * xv, axis=0, keepdims=True)
                dw_ref[1:2, sl] += jnp.sum(d1 * xv, axis=0, keepdims=True)
                dw_ref[2:3, sl] += jnp.sum(d * xv, axis=0, keepdims=True)
        dx1, dg = _norm_bwd_tile(h_ref[...], g_ref[...], jnp.concatenate(dhn, axis=1))
        dg_ref[...] += dg
        dh1_ref[...] = dh2v + dx1

    rows = lambda c: _rows_reversed(tm, c, s // tm)
    return _pcall(
        body, "ffn_bwd", (s // tm,),
        [rows(D_MODEL), _resident((D_FF, D_MODEL)), ANY, ANY, _full((3, 2 * D_FF))]
        + [_resident((2 * D_FF, w)) for _, w in cols] + [rows(D_MODEL), _full((1, D_MODEL))],
        [rows(2 * D_FF), _full((3, 2 * D_FF)), rows(D_MODEL), _full((1, D_MODEL))],
        [_sds((s, 2 * D_FF), BF16), _sds((3, 2 * D_FF), F32), _sds((s, D_MODEL), F32), _sds((1, D_MODEL), F32)],
        (dh2, wdown, up, up_pre, fcw, *wup_parts, h1, g),
        scratch=[pltpu.VMEM((HALO, 2 * D_FF), F32), pltpu.VMEM((2, 3, tm, 2 * D_FF), BF16),
                 pltpu.SemaphoreType.DMA((2, 3))], comm=comm)


def _matmul_tn(a, b, tk, name, ts=1024, comm=None):
    s, ka = a.shape
    n = b.shape[1]
    ts = min(ts, s)
    steps = s // ts

    def body(a_ref, b_ref, o_ref, acc_ref):
        j = pl.program_id(1)

        @pl.when(j == 0)
        def _():
            acc_ref[...] = jnp.zeros_like(acc_ref)

        acc_ref[...] += lax.dot_general(a_ref[...].astype(BF16), b_ref[...].astype(BF16), TN,
                                        preferred_element_type=F32)

        @pl.when(j == steps - 1)
        def _():
            o_ref[...] = acc_ref[...].astype(BF16)

    outs, couts = _pcall(
        body, name, (ka // tk, steps),
        [pl.BlockSpec((ts, tk), lambda i, j: (j, i)), pl.BlockSpec((ts, n), lambda i, j: (j, 0))],
        [pl.BlockSpec((tk, n), lambda i, j: (i, 0))], [_sds((ka, n), BF16)],
        (a, b), scratch=[pltpu.VMEM((tk, n), F32)], comm=comm)
    return outs[0] if comm is None else (outs[0], couts)


def _norm_bwd_tile(xv, g, dy):
    r = lax.rsqrt(jnp.mean(xv * xv, axis=-1, keepdims=True) + NORM_EPS)
    xhat = xv * r
    dg = jnp.sum(dy * xhat, axis=0, keepdims=True)
    dyh = dy * g
    return r * (dyh - xhat * jnp.mean(dyh * xhat, axis=-1, keepdims=True)), dg


def _mix_bwd(dh1, wout, gates, attn, wa, wc, cbx, conv_w, comm):
    s = dh1.shape[0]
    tm = _row_tile(s)
    steps = s // tm

    def body(dh_ref, wo_ref, gate_ref, attn_ref, wa_ref, wc_ref, cbx_ref, halo_ref,
             cw_ref, dg_ref, dattn_ref, dcb_ref, dcc_ref, dcx_ref, dw_ref, gwo_ref, gwa_ref, gwc_ref,
             acc_o, acc_a, acc_c, carry_ref):
        i = pl.program_id(0)

        @pl.when(i == 0)
        def _():
            dw_ref[...] = jnp.zeros_like(dw_ref)
            acc_o[...] = jnp.zeros_like(acc_o)
            acc_a[...] = jnp.zeros_like(acc_a)
            acc_c[...] = jnp.zeros_like(acc_c)
            carry_ref[...] = jnp.zeros_like(carry_ref)

        cb, cc, cx, u, cv = _conv_u(cbx_ref, halo_ref, cw_ref, i == steps - 1)
        attn = attn_ref[...]
        conv = (cb * cv).astype(BF16)
        ap = jnp.dot(attn, wa_ref[...], preferred_element_type=F32)
        cp = jnp.dot(conv, wc_ref[...], preferred_element_type=F32)
        dhb = dh_ref[...].astype(BF16)
        dm = lax.dot_general(dhb, wo_ref[...], NT, preferred_element_type=F32)
        sa = _sig(gate_ref[:, 0:D_MODEL].astype(F32))
        sc = _sig(gate_ref[:, D_MODEL:2 * D_MODEL].astype(F32))
        merged = (sa * ap + sc * cp).astype(BF16)
        da = (dm * sa).astype(BF16)
        dc = (dm * sc).astype(BF16)
        dg_ref[:, 0:D_MODEL] = (dm * ap * sa * (1.0 - sa)).astype(BF16)
        dg_ref[:, D_MODEL:2 * D_MODEL] = (dm * cp * sc * (1.0 - sc)).astype(BF16)
        dattn_ref[...] = lax.dot_general(da, wa_ref[...], NT, preferred_element_type=F32).astype(BF16)
        dconv = lax.dot_general(dc, wc_ref[...], NT, preferred_element_type=F32)
        dcb_ref[...] = (dconv * cv).astype(BF16)
        d = dconv * cb
        dn = carry_ref[...]
        carry_ref[...] = d[0:HALO, :]
        d1, d2 = _shifts_up(d, dn, (1, 2))
        du = cw_ref[2:3, :] * d + cw_ref[1:2, :] * d1 + cw_ref[0:1, :] * d2
        dcc_ref[...] = (du * cx).astype(BF16)
        dcx_ref[...] = (du * cc).astype(BF16)
        dw_ref[0:1, :] += jnp.sum(d2 * u, axis=0, keepdims=True)
        dw_ref[1:2, :] += jnp.sum(d1 * u, axis=0, keepdims=True)
        dw_ref[2:3, :] += jnp.sum(d * u, axis=0, keepdims=True)
        acc_o[...] += lax.dot_general(merged, dhb, TN, preferred_element_type=F32)
        acc_a[...] += lax.dot_general(attn, da, TN, preferred_element_type=F32)
        acc_c[...] += lax.dot_general(conv, dc, TN, preferred_element_type=F32)

        @pl.when(i == steps - 1)
        def _():
            gwo_ref[...] = acc_o[...].astype(BF16)
            gwa_ref[...] = acc_a[...].astype(BF16)
            gwc_ref[...] = acc_c[...].astype(BF16)

    rows = lambda c: _rows_reversed(tm, c, steps)
    return _pcall(
        body, "mix_bwd", (steps,),
        [rows(D_MODEL), _full((D_MODEL, D_MODEL)), rows(GATE_W), rows(ATTN_W), _full((ATTN_W, D_MODEL)),
         _full((CONV_W, D_MODEL)), rows(CBX_W), pl.BlockSpec((HALO, CBX_W), _prev_halo_map_reversed(tm, steps)),
         _full((3, CONV_W))],
        [rows(GATE_W), rows(ATTN_W), rows(CONV_W), rows(CONV_W), rows(CONV_W),
         _full((3, CONV_W)), _full((D_MODEL, D_MODEL)), _full((ATTN_W, D_MODEL)), _full((CONV_W, D_MODEL))],
        [_sds((s, GATE_W), BF16), _sds((s, ATTN_W), BF16), _sds((s, CONV_W), BF16), _sds((s, CONV_W), BF16),
         _sds((s, CONV_W), BF16), _sds((3, CONV_W), F32), _sds((D_MODEL, D_MODEL), BF16),
         _sds((ATTN_W, D_MODEL), BF16), _sds((CONV_W, D_MODEL), BF16)],
        (dh1, wout, gates, attn, wa, wc, cbx, cbx, conv_w),
        scratch=[pltpu.VMEM((D_MODEL, D_MODEL), F32), pltpu.VMEM((ATTN_W, D_MODEL), F32),
                 pltpu.VMEM((CONV_W, D_MODEL), F32), pltpu.VMEM((HALO, CONV_W), F32)], comm=comm)


def _attn_bwd(qkv, sinks, attn, lse, dattn, comm):
    s = qkv.shape[0]
    tq = Q_BLOCKS * BLOCK

    def body(sinks_ref, q_ref, kp_ref, kc_ref, vp_ref, vc_ref, o_ref, lse_ref, do_ref,
             dq_ref, dk_ref, dv_ref, ds_ref):
        n = pl.program_id(0)

        @pl.when(n == 0)
        def _():
            dk_ref[...] = jnp.zeros_like(dk_ref)
            dv_ref[...] = jnp.zeros_like(dv_ref)
            ds_ref[...] = jnp.zeros_like(ds_ref)

        lower = _lower_lanes()
        lane = lax.broadcasted_iota(jnp.int32, (BLOCK, 128), 1)
        lower2 = lax.broadcasted_iota(jnp.int32, (2 * BLOCK, 128), 1) < HEAD_DIM
        lane1 = lax.broadcasted_iota(jnp.int32, (1, 128), 1)
        dsink = jnp.zeros((1, 128), F32)
        for sub in reversed(range(Q_BLOCKS)):
            rows = slice(sub * BLOCK, (sub + 1) * BLOCK)
            mask = _attn_mask(n > 0 if sub == 0 else True)
            kw, vw = _window(kp_ref, kc_ref, sub), _window(vp_ref, vc_ref, sub)
            qv, ov, dov, lsev = q_ref[rows, :], o_ref[rows, :], do_ref[rows, :], lse_ref[rows, :]
            dk_fold, dv_fold = [], []
            for kh in range(2):
                qs = _stack_heads(qv, kh)
                dos = _stack_heads(dov, kh)
                os_ = _stack_heads(ov, kh)
                kd, vd = _dup_kv(kw, kh), _dup_kv(vw, kh)
                lse = jnp.concatenate(
                    [jnp.sum(jnp.where(lane == kh * 4 + g, lsev, 0.0), axis=1, keepdims=True) for g in range(4)],
                    axis=0)
                sc = lax.dot_general(qs, kd, NT, preferred_element_type=F32) * ATTN_SCALE
                p = jnp.exp(jnp.where(mask, sc, NEG) - lse)
                dp = lax.dot_general(dos, vd, NT, preferred_element_type=F32)
                delta = jnp.sum(dos.astype(F32) * os_.astype(F32), axis=1, keepdims=True)
                dsc = (p * (dp - delta) * ATTN_SCALE).astype(BF16)
                dqs = jnp.dot(dsc, kd, preferred_element_type=F32)
                for pair in range(2):
                    lo = dqs[(2 * pair) * BLOCK:(2 * pair + 1) * BLOCK]
                    hi = dqs[(2 * pair + 1) * BLOCK:(2 * pair + 2) * BLOCK]
                    col = (kh * 2 + pair) * 128
                    dq_ref[rows, col:col + 128] = jnp.where(lower, lo, hi).astype(BF16)
                dkd = lax.dot_general(dsc, qs, TN, preferred_element_type=F32)
                dvd = lax.dot_general(p.astype(BF16), dos, TN, preferred_element_type=F32)
                dk_fold.append(dkd + pltpu.roll(dkd, HEAD_DIM, axis=1))
                dv_fold.append(dvd + pltpu.roll(dvd, HEAD_DIM, axis=1))
                psink = jnp.exp(_sink_col(sinks_ref, kh) - lse) * delta
                for g in range(4):
                    tot = jnp.sum(psink[g * BLOCK:(g + 1) * BLOCK], axis=0, keepdims=True)
                    dsink = dsink - jnp.where(lane1 == kh * 4 + g, tot, 0.0)
            dk2 = jnp.where(lower2, dk_fold[0], dk_fold[1])
            dv2 = jnp.where(lower2, dv_fold[0], dv_fold[1])
            cur = pl.ds(pl.multiple_of((Q_BLOCKS * n + sub) * BLOCK, BLOCK), BLOCK)
            dk_ref[cur, :] += dk2[BLOCK:]
            dv_ref[cur, :] += dv2[BLOCK:]
            if sub > 0:
                prev = pl.ds(pl.multiple_of((Q_BLOCKS * n + sub - 1) * BLOCK, BLOCK), BLOCK)
                dk_ref[prev, :] += dk2[:BLOCK]
                dv_ref[prev, :] += dv2[:BLOCK]
        ds_ref[...] += dsink

        @pl.when(n > 0)
        def _():
            prev = pl.ds(pl.multiple_of((Q_BLOCKS * n - 1) * BLOCK, BLOCK), BLOCK)
            dk_ref[prev, :] += dk2[:BLOCK]
            dv_ref[prev, :] += dv2[:BLOCK]

    blk = lambda w: pl.BlockSpec((tq, w), lambda n: (n, 0))
    return _pcall(
        body, "attn_bwd", (s // tq,),
        [pl.BlockSpec(memory_space=pltpu.SMEM)] + _attn_specs() + [blk(ATTN_W), blk(128), blk(ATTN_W)],
        [blk(ATTN_W), _full((s, KV_W)), _full((s, KV_W)), _full((1, 128))],
        [_sds((s, ATTN_W), BF16), _sds((s, KV_W), F32), _sds((s, KV_W), F32), _sds((1, 128), F32)],
        (sinks, qkv, qkv, qkv, qkv, qkv, attn, lse, dattn), comm=comm)


DPROJ_PIECES = (ATTN_W, KV_W, KV_W, CONV_W, CONV_W, CONV_W, GATE_W)
DPROJ_OFFSETS = tuple(sum(DPROJ_PIECES[:k]) for k in range(len(DPROJ_PIECES)))


def _grad_w_in(pieces, xn, comm):
    s = xn.shape[0]
    ts = min(1024, s)
    steps = s // ts
    rows0 = DPROJ_OFFSETS[6]

    def body(*refs):
        p_refs, b_ref, o_ref, acc_ref, stage_ref, sem = refs[:7], refs[7], refs[8], refs[9], refs[10], refs[11]
        i, j = pl.program_id(0), pl.program_id(1)

        @pl.when(j == 0)
        def _():
            acc_ref[...] = jnp.zeros_like(acc_ref)

        bv = b_ref[...]

        def flush(lo, n):
            stage_ref[0:n, :] = acc_ref[0:n, :].astype(BF16)
            cp = pltpu.make_async_copy(stage_ref.at[0:n, :], o_ref.at[lo:lo + n, :], sem)
            cp.start()
            cp.wait()

        @pl.when(i == 0)
        def _():
            for p_ref, off, w in zip(p_refs[:6], DPROJ_OFFSETS[:6], DPROJ_PIECES[:6]):
                acc_ref[off:off + w, :] += lax.dot_general(p_ref[...].astype(BF16), bv, TN,
                                                           preferred_element_type=F32)

            @pl.when(j == steps - 1)
            def _():
                flush(0, rows0)

        @pl.when(i == 1)
        def _():
            acc_ref[0:GATE_W, :] += lax.dot_general(p_refs[6][...], bv, TN, preferred_element_type=F32)

            @pl.when(j == steps - 1)
            def _():
                flush(rows0, GATE_W)

    def piece_spec(w, group):
        return pl.BlockSpec((ts, w), lambda i, j: (jnp.where(i == group, j, 0), 0))

    outs, couts = _pcall(
        body, "grad_w_in", (2, steps),
        [piece_spec(w, 0) for w in DPROJ_PIECES[:6]] + [piece_spec(GATE_W, 1),
                                                         pl.BlockSpec((ts, D_MODEL), lambda i, j: (j, 0))],
        [ANY], [_sds((IN_W, D_MODEL), BF16)], (*pieces, xn),
        scratch=[pltpu.VMEM((rows0, D_MODEL), F32), pltpu.VMEM((rows0, D_MODEL), BF16), pltpu.SemaphoreType.DMA],
        comm=comm)
    return outs[0], couts


def _inproj_bwd(pieces, win_t, x, g, dh1, comm):
    s = x.shape[0]
    tm = _row_tile(s, 512)

    def body(*refs):
        p_refs = refs[:7]
        w_ref, x_ref, g_ref, dh_ref, dx_ref, db_ref, dg_ref = refs[7:]

        @pl.when(pl.program_id(0) == 0)
        def _():
            db_ref[...] = jnp.zeros_like(db_ref)
            dg_ref[...] = jnp.zeros_like(dg_ref)

        dxn = jnp.zeros((tm, D_MODEL), F32)
        for p_ref, off, w in zip(p_refs, DPROJ_OFFSETS, DPROJ_PIECES):
            v = p_ref[...].astype(BF16)
            db_ref[:, off:off + w] += jnp.sum(v.astype(F32), axis=0, keepdims=True)
            dxn = dxn + jnp.dot(v, w_ref[off:off + w, :], preferred_element_type=F32)
        dx, dg = _norm_bwd_tile(x_ref[...], g_ref[...], dxn)
        dg_ref[...] += dg
        dx_ref[...] = dh_ref[...] + dx

    return _pcall(
        body, "inproj_bwd", (s // tm,),
        [_rows(tm, w) for w in DPROJ_PIECES] + [_resident((IN_W, D_MODEL)), _rows(tm, D_MODEL), _full((1, D_MODEL)),
                                                _rows(tm, D_MODEL)],
        [_rows(tm, D_MODEL), _full((8, IN_W)), _full((8, D_MODEL))],
        [_sds((s, D_MODEL), F32), _sds((8, IN_W), F32), _sds((8, D_MODEL), F32)],
        (*pieces, win_t, x, g, dh1), comm=comm)


def _adam_math(w, g, m, v):
    m2 = ADAM_B1 * m + (1.0 - ADAM_B1) * g
    v2 = ADAM_B2 * v + (1.0 - ADAM_B2) * (g * g)
    m_hat = m2 / (1.0 - ADAM_B1 ** ADAM_STEP)
    v_hat = v2 / (1.0 - ADAM_B2 ** ADAM_STEP)
    delta = -ADAM_LR * (m_hat / (jnp.sqrt(v_hat) + ADAM_EPS) + ADAM_WD * w)
    return delta, m2, v2


def _sum_slots(ref):
    tot = ref[0].astype(F32)
    for i in range(1, ref.shape[0]):
        tot = tot + ref[i].astype(F32)
    return tot


def _pair_sum(partials, name):
    r, c = partials.shape[0] // N_DEV, partials.shape[1]
    core = lax.axis_index("c").astype(jnp.int32).reshape(1)

    def body(core_ref, mine_ref, all_ref, o_ref, theirs_ref, send_sems, recv_sems):
        k = pl.program_id(0)
        x, y, cc = _my_place()

        def copy(chip):
            return pltpu.make_async_remote_copy(
                src_ref=all_ref.at[pl.ds(pl.multiple_of((2 * chip + 1 - cc) * r, 8), r), :],
                dst_ref=theirs_ref.at[chip], send_sem=send_sems.at[chip], recv_sem=recv_sems.at[chip],
                device_id=(x, y, 1 - cc), device_id_type=MESH)

        @pl.when(k == 0)
        def _():
            _barrier_signal(SIBLING_PEER)
            _barrier_wait(SIBLING_PEER)
            for chip in range(4):
                copy(chip).start()

        for chip in range(4):
            @pl.when(k == chip)
            def _(chip=chip):
                copy(chip).wait_recv()
                o_ref[...] = (mine_ref[...].astype(F32) + theirs_ref[chip].astype(F32)).astype(BF16)

        @pl.when(k == 3)
        def _():
            for chip in range(4):
                copy(chip).wait_send()

    grid_spec = pltpu.PrefetchScalarGridSpec(
        num_scalar_prefetch=1, grid=(4,),
        in_specs=[pl.BlockSpec((None, None, r, c), lambda k, core_ref: (k, core_ref[0], 0, 0)), ANY],
        out_specs=pl.BlockSpec((r, c), lambda k, core_ref: (k, 0)),
        scratch_shapes=[pltpu.VMEM((4, r, c), BF16), pltpu.SemaphoreType.DMA((4,)), pltpu.SemaphoreType.DMA((4,))])
    params = pltpu.CompilerParams(dimension_semantics=("arbitrary",), vmem_limit_bytes=VMEM_LIMIT,
                                  collective_id=BARRIER_ID[SIBLING_PEER])
    return pl.pallas_call(body, name=name, grid_spec=grid_spec, out_shape=_sds((4 * r, c), BF16),
                          compiler_params=params)(core, partials.reshape(4, 2, r, c), partials)


def _sum_adamw(parts, w, m, v, tr, name):
    r, c = w.shape

    def body(p_ref, w_ref, m_ref, v_ref, g_ref, d_ref, m2_ref, v2_ref):
        g = _sum_slots(p_ref)
        g_ref[...] = g
        d_ref[...], m2_ref[...], v2_ref[...] = _adam_math(w_ref[...], g, m_ref[...], v_ref[...])

    spec = pl.BlockSpec((tr, c), lambda i: (i, 0))
    return _pcall(body, name, (r // tr,), [pl.BlockSpec((N_DEV, tr, c), lambda i: (0, i, 0)), spec, spec, spec],
                  [spec] * 4, [_sds((r, c), F32)] * 4, (parts, w, m, v))[0]


def _sum_parts_adamw(parts, w, m, v, tr, name):
    c = w.shape[1]
    tiles = [p.shape[1] // tr for p in parts]
    starts = [sum(tiles[:k]) for k in range(len(parts))]
    n_parts = len(parts)

    def body(*refs):
        p_refs = refs[:n_parts]
        w_ref, m_ref, v_ref, g_ref, d_ref, m2_ref, v2_ref = refs[n_parts:]
        i = pl.program_id(0)
        for p_ref, st, nt in zip(p_refs, starts, tiles):
            @pl.when(jnp.logical_and(i >= st, i < st + nt))
            def _(p_ref=p_ref):
                g_ref[...] = _sum_slots(p_ref)

        d_ref[...], m2_ref[...], v2_ref[...] = _adam_math(w_ref[...], g_ref[...], m_ref[...], v_ref[...])

    def part_spec(p, st, nt):
        return pl.BlockSpec((p.shape[0], tr, c), lambda i: (0, jnp.clip(i - st, 0, nt - 1), 0))

    spec = pl.BlockSpec((tr, c), lambda i: (i, 0))
    return _pcall(
        body, name, (sum(tiles),),
        [part_spec(p, st, nt) for p, st, nt in zip(parts, starts, tiles)] + [spec, spec, spec],
        [spec] * 4, [_sds(w.shape, F32)] * 4, (*parts, w, m, v))[0]


ROW_MIX, ROW_FFN, ROW_FINAL, ROW_SINKS, ROW_LOSS, ROW_BIN, ROW_CW, ROW_FCW = 0, 1, 2, 3, 4, 5, 10, 13
FCW_ROWS = 6


def _wide_pieces(width):
    return [(k * D_MODEL, min(D_MODEL, width - k * D_MODEL)) for k in range(-(-width // D_MODEL))]


def _pack_small(dffn, dfn, dsink, loss, dcw, dfcw):
    def body(ffn_ref, fn_ref, sink_ref, loss_ref, cw_ref, fcw_ref, o_ref):
        o_ref[...] = jnp.zeros_like(o_ref)
        o_ref[ROW_FFN:ROW_FFN + 1, :] = ffn_ref[...]
        o_ref[ROW_FINAL:ROW_FINAL + 1, :] = fn_ref[...]
        o_ref[ROW_SINKS:ROW_SINKS + 1, 0:128] = sink_ref[...]
        o_ref[ROW_LOSS:ROW_LOSS + 1, 0:128] = loss_ref[...]
        o_ref[ROW_CW:ROW_CW + 3, 0:CONV_W] = cw_ref[...]
        for a in range(3):
            for k, (off, w) in enumerate(_wide_pieces(2 * D_FF)):
                row = ROW_FCW + FCW_ROWS * a + k
                o_ref[row:row + 1, 0:w] = fcw_ref[a:a + 1, off:off + w]

    return pl.pallas_call(body, name="pack_small", out_shape=_sds((SMALL_ROWS, D_MODEL), F32))(
        dffn, dfn, dsink, loss, dcw, dfcw)


def _small_sums_adamw(r_small, r_dmix, r_dbin, params):
    rows = (None, None, ROW_SINKS, ROW_FFN, ROW_FINAL)

    def sum_row0(ref):
        tot = ref[0:1, :]
        for i in range(1, N_DEV):
            tot = tot + ref[8 * i:8 * i + 1, :]
        return tot

    def body(*refs):
        r_ref, late_refs, p_refs, o_refs = refs[0], refs[1:3], refs[3:18], refs[18:]
        tot = _sum_slots(r_ref)
        for k, row in enumerate(rows):
            w_ref, m_ref, v_ref = p_refs[3 * k:3 * k + 3]
            g_ref, d_ref, m2_ref, v2_ref = o_refs[4 * k:4 * k + 4]
            if row is None:
                g_ref[...] = sum_row0(late_refs[k])
            else:
                for j, (off, w) in enumerate(_wide_pieces(w_ref.shape[1])):
                    g_ref[:, off:off + w] = tot[row + j:row + j + 1, 0:w]
            d_ref[...], m2_ref[...], v2_ref[...] = _adam_math(w_ref[...], g_ref[...], m_ref[...], v_ref[...])
        cw_ref, fcw_ref, loss_ref = o_refs[20:]
        cw_ref[...] = tot[ROW_CW:ROW_CW + 3, 0:CONV_W]
        for a in range(3):
            for j, (off, w) in enumerate(_wide_pieces(2 * D_FF)):
                row = ROW_FCW + FCW_ROWS * a + j
                fcw_ref[a:a + 1, off:off + w] = tot[row:row + 1, 0:w]
        loss_ref[...] = tot[ROW_LOSS:ROW_LOSS + 1, 0:128]

    flat = [t for p in params for t in p]
    out_shape = [_sds(p[0].shape, F32) for p in params for _ in range(4)]
    out_shape += [_sds((3, CONV_W), F32), _sds((3, 2 * D_FF), F32), _sds((1, 128), F32)]
    res = pl.pallas_call(body, name="small_sums_adamw", out_shape=out_shape)(r_small, r_dmix, r_dbin, *flat)
    return [tuple(res[4 * k:4 * k + 4]) for k in range(5)], res[20], res[21], res[22]


def _adamw_pair(a, b):
    def body(*refs):
        for k in range(2):
            w_ref, g_ref, m_ref, v_ref = refs[4 * k:4 * k + 4]
            d_ref, m2_ref, v2_ref = refs[8 + 3 * k:8 + 3 * k + 3]
            d_ref[...], m2_ref[...], v2_ref[...] = _adam_math(w_ref[...], g_ref[...], m_ref[...], v_ref[...])

    out_shape = [_sds(a[0].shape, F32)] * 3 + [_sds(b[0].shape, F32)] * 3
    res = pl.pallas_call(body, name="adamw_conv_weights", out_shape=out_shape)(*a, *b)
    return tuple(res[:3]), tuple(res[3:])


def _pad_cols(a, c):
    return jnp.pad(a, ((0, 0), (0, c - a.shape[1])))


def _to_col_slabs(g):
    r = g.shape[0]
    return jnp.transpose(g.reshape(r, N_DEV, 128), (1, 0, 2)).reshape(N_DEV * r, 128)


def _from_col_slabs(t):
    r = t.shape[0] // N_DEV
    return jnp.transpose(t.reshape(N_DEV, r, 128), (1, 0, 2)).reshape(r, N_DEV * 128)


def _slots(t):
    return t.reshape(N_DEV, t.shape[0] // N_DEV, t.shape[1])


def kernel(x, mix_norm, w_in, b_in, sinks, conv_w, w_attn_branch, w_conv_branch, w_out, ffn_norm, w_up, ffn_conv_w, w_down, final_norm, loss_target, m_mix_norm, m_w_in, m_b_in, m_sinks, m_conv_w, m_w_attn_branch, m_w_conv_branch, m_w_out, m_ffn_norm, m_w_up, m_ffn_conv_w, m_w_down, m_final_norm, v_mix_norm, v_w_in, v_b_in, v_sinks, v_conv_w, v_w_attn_branch, v_w_conv_branch, v_w_out, v_ffn_norm, v_w_up, v_ffn_conv_w, v_w_down, v_final_norm):
    xs, tgt = x[0], loss_target[0]
    me = 4 * lax.axis_index("x") + 2 * lax.axis_index("y") + lax.axis_index("c")
    in_rows, up_rows = IN_W // N_DEV, 2 * D_FF // N_DEV

    conv_sh = jnp.concatenate([_pad_cols(ffn_conv_w[0], 768), _pad_cols(conv_w[0], 768),
                               jnp.zeros((2, 768), F32)], axis=0)
    win_sh, wup_sh = w_in[0].T.astype(BF16), w_up[0].T.astype(BF16)
    wout_sh, wdown_sh = w_out[0].astype(BF16), w_down[0].astype(BF16)
    wa_sh, wc_sh = w_attn_branch[0].astype(BF16), w_conv_branch[0].astype(BF16)

    quarter, half = D_MODEL // 4, D_MODEL // 2
    phases = dict(forward_at=0.375, pass_on_at=0.875)
    (win_t,) = _exchange_only(_AllGather([win_sh]), "gather_w_in")
    (xn, qkv, cbx, gates), (wa_s, wc_s, conv_g, wup_a) = _norm_inproj(
        xs, mix_norm, win_t, b_in, _AllGather([wa_sh, wc_sh, conv_sh, (wup_sh, 0, quarter)], **phases))
    (attn, lse), (wup_b, wout) = _attn_fwd(qkv, sinks,
                                           _AllGather([(wup_sh, quarter, quarter), wout_sh], **phases))
    wa, wc = _from_col_slabs(wa_s), _from_col_slabs(wc_s)
    conv_g = conv_g.reshape(N_DEV, 8, 768)
    fcw = jnp.transpose(conv_g[:, 0:3, :up_rows], (1, 0, 2)).reshape(3, 2 * D_FF)
    cw = jnp.transpose(conv_g[:, 3:6, :CONV_W // N_DEV], (1, 0, 2)).reshape(3, CONV_W)
    (h1,), (wup_c,) = _mix_fwd(xs, cbx, gates, attn, cw, wa, wc, wout,
                               _AllGather([(wup_sh, half, half)], **phases))
    wup_parts = (wup_a, wup_b, wup_c)
    (hn, up_pre, up), (wdown,) = _ffn_up(h1, ffn_norm, wup_parts, fcw,
                                         _AllGather([wdown_sh], forward_at=0.25, pass_on_at=0.75))
    act, dh2, loss_p, dfn_p = _ffn_down_loss(up, wdown, h1, final_norm.reshape(1, D_MODEL), tgt)

    dn_rows = D_FF // N_DEV
    g_wdown = _matmul_tn(act, dh2, FF_GRAD_ROWS, "grad_w_down")
    (dup_pre, dfcw_p, dh1, dffn_p), (r_wdown,) = _ffn_bwd(dh2, wdown, up, up_pre, fcw, wup_parts, h1, ffn_norm,
                                                         _ReduceScatter([(g_wdown, 0, dn_rows)]))
    g_wup_t = _matmul_tn(dup_pre, hn, FF_GRAD_ROWS, "grad_w_up")
    q_wup = _pair_sum(g_wup_t, "pair_sum_w_up")
    (dgates, dattn, dcb, dcc, dcx, dcw_p, g_wout, g_wa_nat, g_wc_nat), (r_wup,) = _mix_bwd(
        dh1, wout, gates, attn, wa, wc, cbx, cw, _ChipExchange([q_wup]))
    g_wa, g_wc = _to_col_slabs(g_wa_nat), _to_col_slabs(g_wc_nat)
    (dq, dk, dv, dsink_p), (r_wout,) = _attn_bwd(
        qkv, sinks, attn, lse, dattn, _ReduceScatter([(g_wout, 0, D_MODEL // N_DEV)]))
    dproj = (dq, dk, dv, dcb, dcc, dcx, dgates)
    small = _pack_small(dffn_p, dfn_p, dsink_p, loss_p, dcw_p, dfcw_p)
    g_win_t, (r_wa, r_wc, r_small) = _grad_w_in(
        dproj, xn, _ReduceScatter([(g_wa, 0, ATTN_W), (g_wc, 0, CONV_W)], [small]))
    q_win = _pair_sum(g_win_t, "pair_sum_w_in")
    (dx, _, _), (r_win, r_dbin, r_dmix) = _inproj_bwd(
        dproj, win_t, xs, mix_norm, dh1,
        _ChipExchangeThenBroadcast([q_win], late_from=(1, 2), late_shapes=[(8, IN_W), (8, D_MODEL)]))

    fn2, m_fn2, v_fn2 = (t.reshape(1, D_MODEL) for t in (final_norm, m_final_norm, v_final_norm))
    small_res, g_cw_full, g_fcw_full, loss_row = _small_sums_adamw(
        _slots(r_small), r_dmix, r_dbin,
        [(mix_norm, m_mix_norm, v_mix_norm), (b_in, m_b_in, v_b_in), (sinks, m_sinks, v_sinks),
         (ffn_norm, m_ffn_norm, v_ffn_norm), (fn2, m_fn2, v_fn2)])
    loss = loss_row[0, 0]
    g_cw = lax.dynamic_slice_in_dim(g_cw_full, me * (CONV_W // N_DEV), CONV_W // N_DEV, axis=1)
    g_fcw = lax.dynamic_slice_in_dim(g_fcw_full, me * up_rows, up_rows, axis=1)
    taps = lambda t: jnp.transpose(t, (1, 0, 2))
    g_cw, g_fcw = g_cw[:, None, :], g_fcw[:, None, :]
    cw_res, fcw_res = _adamw_pair((taps(conv_w), g_cw, taps(m_conv_w), taps(v_conv_w)),
                                  (taps(ffn_conv_w), g_fcw, taps(m_ffn_conv_w), taps(v_ffn_conv_w)))

    big = {}
    big["w_in"] = tuple(t.T for t in _sum_parts_adamw(
        [r_win.reshape(4, in_rows, D_MODEL)], w_in[0].T, m_w_in[0].T, v_w_in[0].T, in_rows // 2, "adamw_w_in"))
    big["w_up"] = tuple(t.T for t in _sum_parts_adamw(
        [r_wup.reshape(4, up_rows, D_MODEL)], w_up[0].T, m_w_up[0].T, v_w_up[0].T, up_rows // 4, "adamw_w_up"))
    big["w_out"] = _sum_adamw(_slots(r_wout), w_out[0], m_w_out[0], v_w_out[0], 128, "adamw_w_out")
    big["w_down"] = _sum_adamw(_slots(r_wdown), w_down[0], m_w_down[0], v_w_down[0], dn_rows // 2, "adamw_w_down")
    big["w_attn_branch"] = _sum_adamw(_slots(r_wa), w_attn_branch[0], m_w_attn_branch[0], v_w_attn_branch[0], 256,
                                      "adamw_w_attn_branch")
    big["w_conv_branch"] = _sum_adamw(_slots(r_wc), w_conv_branch[0], m_w_conv_branch[0], v_w_conv_branch[0], 256,
                                      "adamw_w_conv_branch")

    res = dict(zip(("mix_norm", "b_in", "sinks", "ffn_norm"), small_res[:4]))
    res["final_norm"] = tuple(t.reshape(final_norm.shape) for t in small_res[4])
    res["conv_w"] = tuple(jnp.transpose(t, (1, 0, 2)) for t in (g_cw,) + cw_res)
    res["ffn_conv_w"] = tuple(jnp.transpose(t, (1, 0, 2)) for t in (g_fcw,) + fcw_res)
    for name, ref_w in (("w_in", w_in), ("w_up", w_up), ("w_out", w_out), ("w_down", w_down),
                        ("w_attn_branch", w_attn_branch), ("w_conv_branch", w_conv_branch)):
        res[name] = tuple(t.reshape(ref_w.shape) for t in big[name])

    order = ["mix_norm", "w_in", "b_in", "sinks", "conv_w", "w_attn_branch", "w_conv_branch", "w_out",
             "ffn_norm", "w_up", "ffn_conv_w", "w_down", "final_norm"]
    out = [loss, dx.reshape(x.shape)]
    for k in range(4):
        out += [res[name][k] for name in order]
    return tuple(out)
```
